```python
import jax, jax.numpy as jnp
from jax import lax
import numpy as np

D_MODEL = 1024
BATCH = 8
SEQ = 4096
DEPTH = 1

ROPE_THETA = 500000.0
BLOCK = 128
NEG = -1e30
RMS_EPS = 1e-6
LN_EPS = 1e-5

MLA_HEADS = 8
MLA_NOPE_DIM = 64
MLA_ROPE_DIM = 32
MLA_V_DIM = 64
Q_LORA_RANK = 384
KV_LORA_RANK = 256
MLA_WIDTH = MLA_HEADS * MLA_V_DIM

DIL_HEADS = 8
DIL_HEAD_DIM = 64
DIL_ROT_DIM = DIL_HEAD_DIM // 4
DIL_WIDTH = DIL_HEADS * DIL_HEAD_DIM
DIL_CONFIGS = ((128, 1), (512, 4), (2048, 16))

MIX_WIDTH = MLA_WIDTH + DIL_WIDTH
IN_SPLITS = (Q_LORA_RANK, KV_LORA_RANK, MLA_ROPE_DIM, MLA_WIDTH, DIL_WIDTH, DIL_WIDTH, DIL_WIDTH, DIL_WIDTH)
IN_WIDTH = sum(IN_SPLITS)

DEEPNORM_ALPHA = (2.0 * DEPTH) ** 0.25
DEEPNORM_BETA = (8.0 * DEPTH) ** -0.25

kernel_name = "hybrid_mla_dilated_deepnorm"


def rmsnorm(t, g):
    tf = t.astype(jnp.float32)
    tf = tf * lax.rsqrt(jnp.mean(tf * tf, axis=-1, keepdims=True) + RMS_EPS)
    return (tf * g.astype(jnp.float32)).astype(t.dtype)


def layernorm(t, g, b):
    tf = t.astype(jnp.float32)
    mu = jnp.mean(tf, axis=-1, keepdims=True)
    var = jnp.mean(jnp.square(tf - mu), axis=-1, keepdims=True)
    return ((tf - mu) * lax.rsqrt(var + LN_EPS) * g.astype(jnp.float32) + b.astype(jnp.float32)).astype(t.dtype)


def rope_tables(seq_len, dim):
    inv_freq = ROPE_THETA ** (-jnp.arange(0, dim, 2, dtype=jnp.float32) / dim)
    ang = jnp.arange(seq_len, dtype=jnp.float32)[:, None] * inv_freq[None, :]
    return jnp.cos(ang), jnp.sin(ang)


def apply_rope(t, cos, sin):
    t1, t2 = jnp.split(t.astype(jnp.float32), 2, axis=-1)
    c, s = cos[:, None, :], sin[:, None, :]
    return jnp.concatenate([t1 * c - t2 * s, t1 * s + t2 * c], axis=-1).astype(t.dtype)


def mla_attention(c_q, c_kv, k_rope, q_norm_g, kv_norm_g, w_uq, w_ukv):
    B, S, _ = c_q.shape
    H, DN, DR, DV = MLA_HEADS, MLA_NOPE_DIM, MLA_ROPE_DIM, MLA_V_DIM
    cos, sin = rope_tables(S, DR)
    q = (rmsnorm(c_q, q_norm_g) @ w_uq).reshape(B, S, H, DN + DR)
    q = jnp.concatenate([q[..., :DN], apply_rope(q[..., DN:], cos, sin)], axis=-1)
    kv = (rmsnorm(c_kv, kv_norm_g) @ w_ukv).reshape(B, S, H, DN + DV)
    k_nope, v = kv[..., :DN], kv[..., DN:]
    k_pe = apply_rope(k_rope[:, :, None, :], cos, sin)
    k = jnp.concatenate([k_nope, jnp.broadcast_to(k_pe, (B, S, H, DR))], axis=-1)
    scale = (DN + DR) ** -0.5
    nblk = S // BLOCK
    qb = q.reshape(B, nblk, BLOCK, H, DN + DR).transpose(1, 0, 3, 2, 4)
    kpos = jnp.arange(S)

    def one_block(args):
        q_blk, i = args
        s = jnp.einsum('bhqd,bkhd->bhqk', q_blk, k).astype(jnp.float32) * scale
        qpos = i * BLOCK + jnp.arange(BLOCK)
        s = jnp.where(kpos[None, :] <= qpos[:, None], s, NEG)
        p = jax.nn.softmax(s, axis=-1)
        return jnp.einsum('bhqk,bkhd->bqhd', p.astype(v.dtype), v)

    out = lax.map(one_block, (qb, jnp.arange(nblk)))
    return out.transpose(1, 0, 2, 3, 4).reshape(B, S, H * DV)


def dilated_branch(q, k, v, window, dilation):
    B, S, H, D = q.shape
    n_back = window // dilation
    seg = dilation * BLOCK
    S_pad = -(-S // seg) * seg
    pad = ((0, 0), (0, S_pad - S), (0, 0), (0, 0))
    L = S_pad // dilation
    nb = L // BLOCK

    def to_sub(t):
        t = jnp.pad(t, pad).reshape(B, L, dilation, H, D).transpose(0, 2, 3, 1, 4)
        return t.reshape(B, dilation, H, nb, BLOCK, D)

    def with_prev(t):
        prev = jnp.pad(t, ((0, 0), (0, 0), (0, 0), (1, 0), (0, 0), (0, 0)))[:, :, :, :-1]
        return jnp.concatenate([prev, t], axis=4)

    qs = to_sub(q)
    ks = with_prev(to_sub(k))
    vs = with_prev(to_sub(v))
    s = jnp.einsum('bdhnqe,bdhnke->bdhnqk', qs, ks).astype(jnp.float32)
    q_loc = jnp.arange(BLOCK)
    k_loc = jnp.arange(2 * BLOCK) - BLOCK
    dist = q_loc[:, None] - k_loc[None, :]
    valid = (jnp.arange(nb)[:, None, None] * BLOCK + k_loc[None, None, :]) >= 0
    mask = (dist >= 0) & (dist <= n_back) & valid
    s = jnp.where(mask, s, NEG)
    m = jnp.max(s, axis=-1, keepdims=True)
    p = jnp.exp(s - m)
    l = jnp.sum(p, axis=-1, keepdims=True)
    num = jnp.einsum('bdhnqk,bdhnke->bdhnqe', p, vs.astype(jnp.float32))

    def to_seq(t):
        c = t.shape[-1]
        t = t.reshape(B, dilation, H, L, c).transpose(0, 3, 1, 2, 4).reshape(B, S_pad, H, c)
        return t[:, :S]

    return to_seq(num), to_seq(m), to_seq(l)


def dilated_attention(q, k, v):
    B, S, _ = q.shape
    H, D = DIL_HEADS, DIL_HEAD_DIM
    cos, sin = rope_tables(S, DIL_ROT_DIM)

    def heads_rope(t):
        t = t.reshape(B, S, H, D)
        return jnp.concatenate([apply_rope(t[..., :DIL_ROT_DIM], cos, sin), t[..., DIL_ROT_DIM:]], axis=-1)

    qh = heads_rope(q) * (D ** -0.5)
    kh = heads_rope(k)
    vh = v.reshape(B, S, H, D)
    parts = [dilated_branch(qh, kh, vh, w, d) for (w, d) in DIL_CONFIGS]
    m_all = jnp.max(jnp.stack([pm for (_, pm, _) in parts], axis=0), axis=0)
    num = jnp.zeros((B, S, H, D), jnp.float32)
    den = jnp.zeros((B, S, H, 1), jnp.float32)
    for (pn, pm, pl) in parts:
        w = jnp.exp(pm - m_all)
        num = num + w * pn
        den = den + w * pl
    return (num / den).astype(q.dtype).reshape(B, S, H * D)


def _fwd_setup_inputs(seed: int = 0) -> dict:
    key = jax.random.key(seed)
    ks = jax.random.split(key, 9)
    f32 = jnp.float32
    x = jax.random.normal(ks[0], (BATCH, SEQ, D_MODEL), f32)
    w_in = jax.random.normal(ks[1], (D_MODEL, IN_WIDTH), f32) * D_MODEL ** -0.5
    q_norm_g = 1.0 + 0.02 * jax.random.normal(ks[2], (Q_LORA_RANK,), f32)
    kv_norm_g = 1.0 + 0.02 * jax.random.normal(ks[3], (KV_LORA_RANK,), f32)
    w_uq = jax.random.normal(ks[4], (Q_LORA_RANK, MLA_HEADS * (MLA_NOPE_DIM + MLA_ROPE_DIM)), f32) * Q_LORA_RANK ** -0.5
    w_ukv = jax.random.normal(ks[5], (KV_LORA_RANK, MLA_HEADS * (MLA_NOPE_DIM + MLA_V_DIM)), f32) * KV_LORA_RANK ** -0.5
    w_out = jax.random.normal(ks[6], (MIX_WIDTH, D_MODEL), f32) * (MIX_WIDTH ** -0.5) * DEEPNORM_BETA
    ln_g = 1.0 + 0.02 * jax.random.normal(ks[7], (D_MODEL,), f32)
    ln_b = 0.02 * jax.random.normal(ks[8], (D_MODEL,), f32)
    return {"x": x, "w_in": w_in, "q_norm_g": q_norm_g, "kv_norm_g": kv_norm_g, "w_uq": w_uq,
            "w_ukv": w_ukv, "w_out": w_out, "ln_g": ln_g, "ln_b": ln_b}


def _fwd_reference(x, w_in, q_norm_g, kv_norm_g, w_uq, w_ukv, w_out, ln_g, ln_b):
    offs = np.cumsum(IN_SPLITS)[:-1].tolist()
    for _ in range(DEPTH):
        h = x @ w_in
        c_q, c_kv, k_rope, g_a, q_b, k_b, v_b, g_b = jnp.split(h, offs, axis=-1)
        y_a = mla_attention(c_q, c_kv, k_rope, q_norm_g, kv_norm_g, w_uq, w_ukv) * jax.nn.silu(g_a)
        y_b = dilated_attention(q_b, k_b, v_b) * jax.nn.silu(g_b)
        mix = jnp.concatenate([y_a, y_b], axis=-1)
        x = layernorm(DEEPNORM_ALPHA * x + mix @ w_out, ln_g, ln_b)
    return x


import jax as _jax
import jax.numpy as _jnp

TWIN_FORMAT = 'train_step'
FWD_PARAMS = ['x', 'w_in', 'q_norm_g', 'kv_norm_g', 'w_uq', 'w_ukv', 'w_out', 'ln_g', 'ln_b']
TWIN_WEIGHTS = ['w_in', 'q_norm_g', 'kv_norm_g', 'w_uq', 'w_ukv', 'w_out', 'ln_g', 'ln_b']
TWIN_DIFF_INPUT = 'x'
TWIN_INPUTS = ['x', 'w_in', 'q_norm_g', 'kv_norm_g', 'w_uq', 'w_ukv', 'w_out', 'ln_g', 'ln_b', 'loss_target', 'm_w_in', 'm_q_norm_g', 'm_kv_norm_g', 'm_w_uq', 'm_w_ukv', 'm_w_out', 'm_ln_g', 'm_ln_b', 'v_w_in', 'v_q_norm_g', 'v_kv_norm_g', 'v_w_uq', 'v_w_ukv', 'v_w_out', 'v_ln_g', 'v_ln_b']
TWIN_OUTPUTS = ['loss', 'grad_x', 'grad_w_in', 'grad_q_norm_g', 'grad_kv_norm_g', 'grad_w_uq', 'grad_w_ukv', 'grad_w_out', 'grad_ln_g', 'grad_ln_b', 'delta_w_in', 'delta_q_norm_g', 'delta_kv_norm_g', 'delta_w_uq', 'delta_w_ukv', 'delta_w_out', 'delta_ln_g', 'delta_ln_b', 'new_m_w_in', 'new_m_q_norm_g', 'new_m_kv_norm_g', 'new_m_w_uq', 'new_m_w_ukv', 'new_m_w_out', 'new_m_ln_g', 'new_m_ln_b', 'new_v_w_in', 'new_v_q_norm_g', 'new_v_kv_norm_g', 'new_v_w_uq', 'new_v_w_ukv', 'new_v_w_out', 'new_v_ln_g', 'new_v_ln_b']
TWIN_LEAF_KINDS = {'loss': 'loss', 'grad_x': 'grad_x', 'grad_w_in': 'grad_w', 'grad_q_norm_g': 'grad_w', 'grad_kv_norm_g': 'grad_w', 'grad_w_uq': 'grad_w', 'grad_w_ukv': 'grad_w', 'grad_w_out': 'grad_w', 'grad_ln_g': 'grad_w', 'grad_ln_b': 'grad_w', 'delta_w_in': 'delta_w', 'delta_q_norm_g': 'delta_w', 'delta_kv_norm_g': 'delta_w', 'delta_w_uq': 'delta_w', 'delta_w_ukv': 'delta_w', 'delta_w_out': 'delta_w', 'delta_ln_g': 'delta_w', 'delta_ln_b': 'delta_w', 'new_m_w_in': 'new_m', 'new_m_q_norm_g': 'new_m', 'new_m_kv_norm_g': 'new_m', 'new_m_w_uq': 'new_m', 'new_m_w_ukv': 'new_m', 'new_m_w_out': 'new_m', 'new_m_ln_g': 'new_m', 'new_m_ln_b': 'new_m', 'new_v_w_in': 'new_v', 'new_v_q_norm_g': 'new_v', 'new_v_kv_norm_g': 'new_v', 'new_v_w_uq': 'new_v', 'new_v_w_ukv': 'new_v', 'new_v_w_out': 'new_v', 'new_v_ln_g': 'new_v', 'new_v_ln_b': 'new_v'}


def _forward(args):
    return _fwd_reference(*[args[k] for k in FWD_PARAMS])


def _output_shape():
    def fwd():
        inp = _fwd_setup_inputs(0)
        return _fwd_reference(*[inp[k] for k in FWD_PARAMS])
    out = _jax.eval_shape(fwd)
    return out.shape, out.dtype

N_MICROBATCH = 1
ADAM_LR = 0.001
ADAM_B1 = 0.9
ADAM_B2 = 0.999
ADAM_EPS = 1e-08
ADAM_WD = 0.01
ADAM_STEP = 10
PER_EXAMPLE_BATCH_AXIS = {'x': 0, 'loss_target': 0}
SHARED_INPUTS = []
_WEIGHT_DTYPES = {'w_in': _jnp.float32, 'q_norm_g': _jnp.float32, 'kv_norm_g': _jnp.float32, 'w_uq': _jnp.float32, 'w_ukv': _jnp.float32, 'w_out': _jnp.float32, 'ln_g': _jnp.float32, 'ln_b': _jnp.float32}
MOMENT_SCALE = {'w_in': 1.223814e-02, 'q_norm_g': 1.040658e-02, 'kv_norm_g': 2.003934e-02, 'w_uq': 7.761262e-03, 'w_ukv': 9.832747e-03, 'w_out': 2.004059e-02, 'ln_g': 3.197395e+01, 'ln_b': 5.235488e-01}


def _to_microbatches(a, axis):
    t = _jnp.moveaxis(a, axis, 0)
    t = t.reshape((N_MICROBATCH, t.shape[0] // N_MICROBATCH) + t.shape[1:])
    return _jnp.moveaxis(t, 1, axis + 1)


def setup_inputs(seed: int = 0) -> dict:
    inp = _fwd_setup_inputs(seed)
    key = _jax.random.fold_in(_jax.random.key(seed), 7919)
    shape, _ = _output_shape()
    out = dict(inp)
    out["loss_target"] = _jax.random.normal(_jax.random.fold_in(key, 0), shape, _jnp.float32)
    for i, name in enumerate(TWIN_WEIGHTS):
        w = inp[name].astype(_jnp.float32)
        if MOMENT_SCALE is None:
            s = _jnp.sqrt(_jnp.mean(_jnp.square(w)) + 1e-30)
        else:
            s = MOMENT_SCALE[name]
        km, kv = _jax.random.split(_jax.random.fold_in(key, i + 1))
        out[name] = w
        out["m_" + name] = s * _jax.random.normal(km, w.shape, _jnp.float32)
        out["v_" + name] = (s * s) * _jax.random.uniform(kv, w.shape, _jnp.float32, 0.5, 1.5)
    if N_MICROBATCH > 1:
        for name, axis in PER_EXAMPLE_BATCH_AXIS.items():
            out[name] = _to_microbatches(out[name], axis)
    return {'x': out['x'], 'w_in': out['w_in'], 'q_norm_g': out['q_norm_g'], 'kv_norm_g': out['kv_norm_g'], 'w_uq': out['w_uq'], 'w_ukv': out['w_ukv'], 'w_out': out['w_out'], 'ln_g': out['ln_g'], 'ln_b': out['ln_b'], 'loss_target': out['loss_target'], 'm_w_in': out['m_w_in'], 'm_q_norm_g': out['m_q_norm_g'], 'm_kv_norm_g': out['m_kv_norm_g'], 'm_w_uq': out['m_w_uq'], 'm_w_ukv': out['m_w_ukv'], 'm_w_out': out['m_w_out'], 'm_ln_g': out['m_ln_g'], 'm_ln_b': out['m_ln_b'], 'v_w_in': out['v_w_in'], 'v_q_norm_g': out['v_q_norm_g'], 'v_kv_norm_g': out['v_kv_norm_g'], 'v_w_uq': out['v_w_uq'], 'v_w_ukv': out['v_w_ukv'], 'v_w_out': out['v_w_out'], 'v_ln_g': out['v_ln_g'], 'v_ln_b': out['v_ln_b']}


def _loss(weights, diff, rest, loss_target):
    with _jax.named_scope("forward"):
        args = {**rest, TWIN_DIFF_INPUT: diff, **{k: w.astype(_WEIGHT_DTYPES[k]) for k, w in weights.items()}}
        y = _forward(args)
    with _jax.named_scope("loss_head"):
        err = _jnp.square(y.astype(_jnp.float32) - loss_target)
        return 0.5 * _jnp.sum(_jnp.mean(err, axis=-1)) if err.ndim else 0.5 * err


def _adamw(w, g, m, v):
    m = ADAM_B1 * m + (1.0 - ADAM_B1) * g
    v = ADAM_B2 * v + (1.0 - ADAM_B2) * _jnp.square(g)
    m_hat = m / (1.0 - ADAM_B1 ** ADAM_STEP)
    v_hat = v / (1.0 - ADAM_B2 ** ADAM_STEP)
    delta = -ADAM_LR * (m_hat / (_jnp.sqrt(v_hat) + ADAM_EPS) + ADAM_WD * w)
    return delta, m, v


def reference(x, w_in, q_norm_g, kv_norm_g, w_uq, w_ukv, w_out, ln_g, ln_b, loss_target, m_w_in, m_q_norm_g, m_kv_norm_g, m_w_uq, m_w_ukv, m_w_out, m_ln_g, m_ln_b, v_w_in, v_q_norm_g, v_kv_norm_g, v_w_uq, v_w_ukv, v_w_out, v_ln_g, v_ln_b):
    given = dict(x=x, w_in=w_in, q_norm_g=q_norm_g, kv_norm_g=kv_norm_g, w_uq=w_uq, w_ukv=w_ukv, w_out=w_out, ln_g=ln_g, ln_b=ln_b, loss_target=loss_target, m_w_in=m_w_in, m_q_norm_g=m_q_norm_g, m_kv_norm_g=m_kv_norm_g, m_w_uq=m_w_uq, m_w_ukv=m_w_ukv, m_w_out=m_w_out, m_ln_g=m_ln_g, m_ln_b=m_ln_b, v_w_in=v_w_in, v_q_norm_g=v_q_norm_g, v_kv_norm_g=v_kv_norm_g, v_w_uq=v_w_uq, v_w_ukv=v_w_ukv, v_w_out=v_w_out, v_ln_g=v_ln_g, v_ln_b=v_ln_b)
    weights = {n: given[n] for n in TWIN_WEIGHTS}
    shared = {n: given[n] for n in SHARED_INPUTS}
    per_example = {n: given[n] for n in ['x']}
    grad_fn = _jax.value_and_grad(_loss, argnums=(0, 1))

    def one_microbatch(ex, loss_target):
        ex = dict(ex)
        diff = ex.pop(TWIN_DIFF_INPUT)
        return grad_fn(weights, diff, {**shared, **ex}, loss_target)

    if N_MICROBATCH == 1:
        loss, (grad_w, grad_x) = one_microbatch(per_example, given["loss_target"])
    else:
        def body(carry, xs):
            loss_sum, grad_sum = carry
            l_k, (gw_k, gx_k) = one_microbatch(xs[0], xs[1])
            with _jax.named_scope("update"):
                return (loss_sum + l_k, _jax.tree.map(_jnp.add, grad_sum, gw_k)), gx_k

        init = (_jnp.zeros((), _jnp.float32), _jax.tree.map(_jnp.zeros_like, weights))
        (loss, grad_w), grad_x = _jax.lax.scan(body, init, (per_example, given["loss_target"]))
    with _jax.named_scope("update"):
        delta_w, new_m, new_v = {}, {}, {}
        for n in TWIN_WEIGHTS:
            delta_w[n], new_m[n], new_v[n] = _adamw(weights[n], grad_w[n], given["m_" + n], given["v_" + n])
    return (loss, grad_x, *[grad_w[n] for n in TWIN_WEIGHTS], *[delta_w[n] for n in TWIN_WEIGHTS],
            *[new_m[n] for n in TWIN_WEIGHTS], *[new_v[n] for n in TWIN_WEIGHTS])
```

```python
import functools

import numpy as np
import jax
import jax.numpy as jnp
from jax import lax
from jax.experimental import pallas as pl
from jax.experimental.pallas import tpu as pltpu

F32 = jnp.float32
BF16 = jnp.bfloat16

SEQ = 4096
D_MODEL = 1024
ROPE_THETA = 500000.0
NEG = -1e30
RMS_EPS = 1e-6
LN_EPS = 1e-5
HEADS = 8
MLA_NOPE = 64
MLA_ROPE = 32
MLA_V = 64
Q_RANK = 384
KV_RANK = 256
DIL_HEAD = 64
DIL_ROT = 16
DIL_CONFIGS = ((128, 1), (512, 4), (2048, 16))
HW = HEADS * 64
IN_SPLITS = (Q_RANK, KV_RANK, MLA_ROPE, HW, HW, HW, HW, HW)
IN_WIDTH = sum(IN_SPLITS)
ALPHA = 2.0 ** 0.25
MLA_SCALE = (MLA_NOPE + MLA_ROPE) ** -0.5
DIL_SCALE = DIL_HEAD ** -0.5

ADAM_LR = 0.001
ADAM_B1 = 0.9
ADAM_B2 = 0.999
ADAM_EPS = 1e-08
ADAM_WD = 0.01
ADAM_STEP = 10

N_DEV = 8
LANES = 128
VMEM_LIMIT = 56 * 1024 * 1024

C_CQ, C_CKV, C_KR, C_GA, C_QB, C_KB, C_VB, C_GB, C_END = 0, 384, 640, 768, 1280, 1792, 2304, 2816, 3328


def _pcall(body, **kw):
    return pl.pallas_call(body, **kw)


def _cparams(**kw):
    return pltpu.CompilerParams(vmem_limit_bytes=VMEM_LIMIT, **kw)


def _rope_tables(seq):
    def tabs(dim, period):
        half = dim // 2
        inv = np.float32(ROPE_THETA) ** (-np.arange(0, dim, 2, dtype=np.float32) / np.float32(dim))
        ang = np.arange(seq, dtype=np.float32)[:, None] * inv.astype(np.float32)[None, :]
        cos, sin = np.cos(ang).astype(np.float32), np.sin(ang).astype(np.float32)
        j = np.arange(LANES) % period
        f = j % half
        c = np.where(j < dim, cos[:, f], np.float32(1.0))
        s1 = np.where(j < half, -sin[:, f], np.float32(0.0))
        s2 = np.where((j >= half) & (j < dim), sin[:, f], np.float32(0.0))
        return [c, s1, s2]
    return np.stack(tabs(MLA_ROPE, MLA_ROPE) + tabs(DIL_ROT, DIL_HEAD)).astype(np.float32)


def _rope(t, c, s1, s2, half):
    return t * c + pltpu.roll(t, LANES - half, 1) * s1 + pltpu.roll(t, half, 1) * s2


def _rope_t(d, c, s1, s2, half):
    return d * c + pltpu.roll(d * s1, half, 1) + pltpu.roll(d * s2, LANES - half, 1)


def _rope_wide(fn, t, c, s1, s2, half):
    return jnp.concatenate(
        [fn(t[:, i:i + LANES], c, s1, s2, half) for i in range(0, t.shape[1], LANES)], axis=1)


def _mla_bias(blk):
    a = np.arange(blk)
    causal = np.where(a[None, :] <= a[:, None], 0.0, NEG)
    return np.stack([np.zeros((blk, blk)), causal]).astype(np.float32)


def _dil_bias(blk):
    span = DIL_CONFIGS[-1][0] // blk
    a = np.arange(blk)
    out = []
    for off in range(span + 1):
        delta = blk * off + a[:, None] - a[None, :]
        mult = np.zeros((blk, blk))
        for window, dil in DIL_CONFIGS:
            mult += (delta >= 0) & (delta % dil == 0) & (delta <= window)
        out.append(np.where(mult > 0, np.log(np.maximum(mult, 1.0)), NEG))
    return np.stack(out).astype(np.float32)


def _steps(nq, span, by_key, diag_only_bias):
    rows = []
    if by_key:
        for ki in range(nq):
            hi = min(nq - 1, ki + span)
            for qi in range(ki, hi + 1):
                rows.append((qi, ki, int(qi == ki), int(qi == hi)))
    else:
        for qi in range(nq):
            lo = max(0, qi - span)
            for ki in range(lo, qi + 1):
                rows.append((qi, ki, int(ki == lo), int(ki == qi)))
    arr = np.array(rows, dtype=np.int32)
    off = arr[:, 0] - arr[:, 1]
    bias_idx = (off == 0).astype(np.int32) if diag_only_bias else off.astype(np.int32)
    return [jnp.asarray(v) for v in (arr[:, 0], arr[:, 1], bias_idx, arr[:, 2], arr[:, 3])]


def _fwd_proj(x, w_in_r, w_uq_r, w_ukv_r, qg, kvg, tabs, bt):
    seq = x.shape[0]

    def body(x_ref, win_ref, wuq_ref, wukv_ref, qg_ref, kvg_ref, tab_ref,
             cq_ref, ckv_ref, qn_ref, kvn_ref, qcat_ref, kn_ref, kpe_ref, v_ref,
             ga_ref, gb_ref, qb_ref, kb_ref, vb_ref):
        xb = x_ref[...].astype(BF16)

        def proj(lo, hi):
            return jnp.dot(xb, win_ref[:, lo:hi], preferred_element_type=F32)

        m_tabs = (tab_ref[0], tab_ref[1], tab_ref[2])
        d_tabs = (tab_ref[3], tab_ref[4], tab_ref[5])

        cq = proj(C_CQ, C_CKV)
        cq_ref[...] = cq
        qn = (cq * lax.rsqrt(jnp.mean(cq * cq, axis=1, keepdims=True) + RMS_EPS) * qg_ref[...]).astype(BF16)
        qn_ref[...] = qn
        q = jnp.dot(qn, wuq_ref[...], preferred_element_type=F32)
        qcat_ref[:, :HW] = (q[:, :HW] * MLA_SCALE).astype(BF16)
        qcat_ref[:, HW:] = (_rope_wide(_rope, q[:, HW:], *m_tabs, MLA_ROPE // 2) * MLA_SCALE).astype(BF16)

        ckv = proj(C_CKV, C_KR)
        ckv_ref[...] = ckv
        kvn = (ckv * lax.rsqrt(jnp.mean(ckv * ckv, axis=1, keepdims=True) + RMS_EPS) * kvg_ref[...]).astype(BF16)
        kvn_ref[...] = kvn
        kv = jnp.dot(kvn, wukv_ref[...], preferred_element_type=F32)
        kn_ref[...] = kv[:, :HW].astype(BF16)
        v_ref[...] = kv[:, HW:].astype(BF16)

        kpe_ref[...] = _rope(proj(C_KR, C_GA), *m_tabs, MLA_ROPE // 2).astype(BF16)
        ga_ref[...] = proj(C_GA, C_QB)
        qb_ref[...] = (_rope_wide(_rope, proj(C_QB, C_KB), *d_tabs, DIL_ROT // 2) * DIL_SCALE).astype(BF16)
        kb_ref[...] = _rope_wide(_rope, proj(C_KB, C_VB), *d_tabs, DIL_ROT // 2).astype(BF16)
        vb_ref[...] = proj(C_VB, C_GB).astype(BF16)
        gb_ref[...] = proj(C_GB, C_END)

    def tok(width):
        return pl.BlockSpec((bt, width), lambda i: (i, 0))

    def full(a):
        return pl.BlockSpec(a.shape, lambda i: (0,) * a.ndim)

    outs = [(Q_RANK, F32), (KV_RANK, F32), (Q_RANK, BF16), (KV_RANK, BF16), (HW + 256, BF16), (HW, BF16),
            (LANES, BF16), (HW, BF16), (HW, F32), (HW, F32), (HW, BF16), (HW, BF16), (HW, BF16)]
    return _pcall(
        body, name="fwd_proj", grid=(seq // bt,),
        in_specs=[tok(D_MODEL), full(w_in_r), full(w_uq_r), full(w_ukv_r), full(qg), full(kvg),
                  pl.BlockSpec((6, bt, LANES), lambda i: (0, i, 0))],
        out_specs=[tok(w) for w, _ in outs],
        out_shape=[jax.ShapeDtypeStruct((seq, w), dt) for w, dt in outs],
        compiler_params=_cparams(dimension_semantics=("arbitrary",)),
    )(x, w_in_r, w_uq_r, w_ukv_r, qg, kvg, tabs)


def _head_masks(lane, j, e):
    h = 2 * j + e
    me = (lane >= 64 * e) & (lane < 64 * e + 64)
    g = h % 4
    mr = (lane >= 32 * g) & (lane < 32 * g + 32)
    return me, mr


def _attn_fwd(name, q, k, kpe, v, bias, steps, blk):
    seq = q.shape[0]
    mla = kpe is not None
    n_steps = int(steps[0].shape[0])

    def body(qi_r, ki_r, bi_r, fi_r, la_r, *refs):
        if mla:
            q_ref, k_ref, kpe_ref, v_ref, b_ref, o_ref, lse_ref, m_sc, l_sc, acc_sc = refs
        else:
            q_ref, k_ref, v_ref, b_ref, o_ref, lse_ref, m_sc, l_sc, acc_sc = refs
        t = pl.program_id(0)

        @pl.when(fi_r[t] == 1)
        def _():
            m_sc[...] = jnp.full(m_sc.shape, NEG, F32)
            l_sc[...] = jnp.zeros(l_sc.shape, F32)
            acc_sc[...] = jnp.zeros(acc_sc.shape, F32)

        lane = lax.broadcasted_iota(jnp.int32, (1, LANES), 1)
        lo = lane < 64
        bias_t = b_ref[0]
        for j in range(HEADS // 2):
            cols = slice(LANES * j, LANES * (j + 1))
            kc = k_ref[:, cols]
            if mla:
                kc = jnp.concatenate([kc, kpe_ref[...]], axis=1)
                qr = q_ref[:, HW + LANES * (j // 2):HW + LANES * (j // 2 + 1)]
            qj = q_ref[:, cols]
            vj = v_ref[:, cols]
            alphas, pvs = [], []
            for e in range(2):
                h = 2 * j + e
                me, mr = _head_masks(lane, j, e)
                qe = jnp.where(me, qj, jnp.zeros_like(qj))
                if mla:
                    qe = jnp.concatenate([qe, jnp.where(mr, qr, jnp.zeros_like(qr))], axis=1)
                s = lax.dot_general(qe, kc, (((1,), (1,)), ((), ())), preferred_element_type=F32) + bias_t
                m_prev = m_sc[h]
                m_new = jnp.maximum(m_prev, jnp.max(s, axis=1, keepdims=True))
                alpha = jnp.exp(m_prev - m_new)
                p = jnp.exp(s - m_new)
                l_sc[h] = alpha * l_sc[h] + jnp.sum(p, axis=1, keepdims=True)
                m_sc[h] = m_new
                alphas.append(alpha)
                pvs.append(jnp.dot(p.astype(BF16), vj, preferred_element_type=F32))
            acc_sc[:, cols] = jnp.where(lo, alphas[0], alphas[1]) * acc_sc[:, cols] + jnp.where(lo, pvs[0], pvs[1])

        @pl.when(la_r[t] == 1)
        def _():
            for j in range(HEADS // 2):
                cols = slice(LANES * j, LANES * (j + 1))
                l = jnp.where(lo, l_sc[2 * j], l_sc[2 * j + 1])
                m = jnp.where(lo, m_sc[2 * j], m_sc[2 * j + 1])
                o_ref[:, cols] = acc_sc[:, cols] / l
                lse_ref[:, cols] = m + jnp.log(l)

    qmap = lambda t, qi, ki, bi, fi, la: (qi[t], 0)
    kmap = lambda t, qi, ki, bi, fi, la: (ki[t], 0)
    in_specs = [pl.BlockSpec((blk, q.shape[1]), qmap), pl.BlockSpec((blk, HW), kmap)]
    args = [q, k]
    if mla:
        in_specs.append(pl.BlockSpec((blk, LANES), kmap))
        args.append(kpe)
    in_specs += [pl.BlockSpec((blk, HW), kmap),
                 pl.BlockSpec((1, blk, blk), lambda t, qi, ki, bi, fi, la: (bi[t], 0, 0))]
    args += [v, bias]
    return _pcall(
        body, name=name,
        grid_spec=pltpu.PrefetchScalarGridSpec(
            num_scalar_prefetch=5, grid=(n_steps,), in_specs=in_specs,
            out_specs=[pl.BlockSpec((blk, HW), qmap), pl.BlockSpec((blk, HW), qmap)],
            scratch_shapes=[pltpu.VMEM((HEADS, blk, 1), F32), pltpu.VMEM((HEADS, blk, 1), F32),
                            pltpu.VMEM((blk, HW), F32)]),
        out_shape=[jax.ShapeDtypeStruct((seq, HW), F32), jax.ShapeDtypeStruct((seq, HW), F32)],
        compiler_params=_cparams(dimension_semantics=("arbitrary",)),
    )(*steps, *args)


def _attn_bwd(name, q, k, kpe, v, bias, do, o, lse, steps, blk):
    seq = q.shape[0]
    mla = kpe is not None
    n_steps = int(steps[0].shape[0])
    dk_dtype = BF16 if mla else F32
    kw = 2 * LANES if mla else LANES

    def body(qi_r, ki_r, bi_r, fi_r, la_r, *refs):
        if mla:
            (q_ref, k_ref, kpe_ref, v_ref, b_ref, do_ref, o_ref, lse_ref,
             dq_ref, dk_ref, dkpe_ref, dv_ref, dk_sc, dkpe_sc, dv_sc) = refs
        else:
            (q_ref, k_ref, v_ref, b_ref, do_ref, o_ref, lse_ref,
             dq_ref, dk_ref, dv_ref, dk_sc, dv_sc) = refs
        t = pl.program_id(0)

        @pl.when(t == 0)
        def _():
            dq_ref[...] = jnp.zeros(dq_ref.shape, F32)

        @pl.when(fi_r[t] == 1)
        def _():
            dk_sc[...] = jnp.zeros(dk_sc.shape, F32)
            dv_sc[...] = jnp.zeros(dv_sc.shape, F32)
            if mla:
                dkpe_sc[...] = jnp.zeros(dkpe_sc.shape, F32)

        rows = pl.ds(pl.multiple_of(qi_r[t] * blk, blk), blk)
        lane = lax.broadcasted_iota(jnp.int32, (1, LANES), 1)
        bias_t = b_ref[0]
        tn = (((0,), (0,)), ((), ()))
        nt = (((1,), (1,)), ((), ()))
        for j in range(HEADS // 2):
            cols = slice(LANES * j, LANES * (j + 1))
            kc = k_ref[:, cols]
            if mla:
                kc = jnp.concatenate([kc, kpe_ref[...]], axis=1)
                rcols = slice(HW + LANES * (j // 2), HW + LANES * (j // 2 + 1))
                qr = q_ref[:, rcols]
                dqr = jnp.zeros((blk, LANES), F32)
            qj = q_ref[:, cols]
            vj = v_ref[:, cols]
            doj = do_ref[:, cols]
            prod = doj.astype(F32) * o_ref[:, cols]
            lsej = lse_ref[:, cols]
            dqn = jnp.zeros((blk, LANES), F32)
            for e in range(2):
                me, mr = _head_masks(lane, j, e)
                qe = jnp.where(me, qj, jnp.zeros_like(qj))
                if mla:
                    qe = jnp.concatenate([qe, jnp.where(mr, qr, jnp.zeros_like(qr))], axis=1)
                s = lax.dot_general(qe, kc, nt, preferred_element_type=F32) + bias_t
                lse_e = jnp.max(jnp.where(me, lsej, -jnp.inf), axis=1, keepdims=True)
                d_e = jnp.sum(jnp.where(me, prod, 0.0), axis=1, keepdims=True)
                p = jnp.exp(s - lse_e)
                dom = jnp.where(me, doj, jnp.zeros_like(doj))
                dp = lax.dot_general(dom, vj, nt, preferred_element_type=F32)
                ds = (p * (dp - d_e)).astype(BF16)
                dv_sc[:, cols] += lax.dot_general(p.astype(BF16), dom, tn, preferred_element_type=F32)
                dkc = lax.dot_general(ds, qe, tn, preferred_element_type=F32)
                dk_sc[:, cols] += dkc[:, :LANES]
                dqc = jnp.dot(ds, kc, preferred_element_type=F32)
                dqn = dqn + jnp.where(me, dqc[:, :LANES], 0.0)
                if mla:
                    dkpe_sc[...] += dkc[:, LANES:]
                    dqr = dqr + jnp.where(mr, dqc[:, LANES:], 0.0)
            dq_ref[rows, cols] += dqn
            if mla:
                dq_ref[rows, rcols] += dqr

        @pl.when(la_r[t] == 1)
        def _():
            dk_ref[...] = dk_sc[...].astype(dk_ref.dtype)
            dv_ref[...] = dv_sc[...].astype(dv_ref.dtype)
            if mla:
                dkpe_ref[...] = dkpe_sc[...]

    qmap = lambda t, qi, ki, bi, fi, la: (qi[t], 0)
    kmap = lambda t, qi, ki, bi, fi, la: (ki[t], 0)
    in_specs = [pl.BlockSpec((blk, q.shape[1]), qmap), pl.BlockSpec((blk, HW), kmap)]
    args = [q, k]
    if mla:
        in_specs.append(pl.BlockSpec((blk, LANES), kmap))
        args.append(kpe)
    in_specs += [pl.BlockSpec((blk, HW), kmap),
                 pl.BlockSpec((1, blk, blk), lambda t, qi, ki, bi, fi, la: (bi[t], 0, 0)),
                 pl.BlockSpec((blk, HW), qmap), pl.BlockSpec((blk, HW), qmap), pl.BlockSpec((blk, HW), qmap)]
    args += [v, bias, do, o, lse]
    out_specs = [pl.BlockSpec(q.shape, lambda t, qi, ki, bi, fi, la: (0, 0)), pl.BlockSpec((blk, HW), kmap)]
    out_shape = [jax.ShapeDtypeStruct(q.shape, F32), jax.ShapeDtypeStruct((seq, HW), dk_dtype)]
    scratch = [pltpu.VMEM((blk, HW), F32)]
    if mla:
        out_specs.append(pl.BlockSpec((blk, LANES), kmap))
        out_shape.append(jax.ShapeDtypeStruct((seq, LANES), F32))
        scratch.append(pltpu.VMEM((blk, LANES), F32))
    out_specs.append(pl.BlockSpec((blk, HW), kmap))
    out_shape.append(jax.ShapeDtypeStruct((seq, HW), BF16))
    scratch.append(pltpu.VMEM((blk, HW), F32))
    return _pcall(
        body, name=name,
        grid_spec=pltpu.PrefetchScalarGridSpec(
            num_scalar_prefetch=5, grid=(n_steps,), in_specs=in_specs, out_specs=out_specs,
            scratch_shapes=scratch),
        out_shape=out_shape,
        compiler_params=_cparams(dimension_semantics=("arbitrary",)),
    )(*steps, *args)


def _out_ln(oa, ob, ga, gb, x, tgt, w_out, ln_g, ln_b, bt):
    seq = x.shape[0]

    def body(oa_ref, ob_ref, ga_ref, gb_ref, x_ref, tgt_ref, w_ref, g_ref, b_ref,
             dz_ref, doa_ref, dob_ref, dga_ref, dgb_ref, gw_ref, small_ref):
        i = pl.program_id(0)

        @pl.when(i == 0)
        def _():
            gw_ref[...] = jnp.zeros(gw_ref.shape, F32)
            small_ref[...] = jnp.zeros(small_ref.shape, F32)

        def gate(g):
            sig = 1.0 / (1.0 + jnp.exp(-g))
            return g * sig, sig * (1.0 + g * (1.0 - sig))

        o_a, o_b = oa_ref[...], ob_ref[...]
        g_a, g_b = ga_ref[...], gb_ref[...]
        sa, dsa = gate(g_a)
        sb, dsb = gate(g_b)
        mix = jnp.concatenate([o_a * sa, o_b * sb], axis=1).astype(BF16)
        z = ALPHA * x_ref[...] + jnp.dot(mix, w_ref[...], preferred_element_type=F32)
        mu = jnp.mean(z, axis=1, keepdims=True)
        zc = z - mu
        rstd = lax.rsqrt(jnp.mean(zc * zc, axis=1, keepdims=True) + LN_EPS)
        xhat = zc * rstd
        gam = g_ref[...]
        diff = xhat * gam + b_ref[...] - tgt_ref[...]
        dy = diff * (1.0 / D_MODEL)
        small_ref[0:1, :] += jnp.sum(dy * xhat, axis=0, keepdims=True)
        small_ref[1:2, :] += jnp.sum(dy, axis=0, keepdims=True)
        small_ref[2:3, :] += jnp.sum(diff * diff, axis=0, keepdims=True)
        dxh = dy * gam
        dz = rstd * (dxh - jnp.mean(dxh, axis=1, keepdims=True) - xhat * jnp.mean(dxh * xhat, axis=1, keepdims=True))
        dz_ref[...] = dz
        dzb = dz.astype(BF16)
        gw_ref[...] += lax.dot_general(mix, dzb, (((0,), (0,)), ((), ())), preferred_element_type=F32)
        dmix = lax.dot_general(dzb, w_ref[...], (((1,), (1,)), ((), ())), preferred_element_type=F32)
        dma, dmb = dmix[:, :HW], dmix[:, HW:]
        doa_ref[...] = (dma * sa).astype(BF16)
        dob_ref[...] = (dmb * sb).astype(BF16)
        dga_ref[...] = (dma * o_a * dsa).astype(BF16)
        dgb_ref[...] = (dmb * o_b * dsb).astype(BF16)

    def tok(width):
        return pl.BlockSpec((bt, width), lambda i: (i, 0))

    def full(shape):
        return pl.BlockSpec(shape, lambda i: (0,) * len(shape))

    return _pcall(
        body, name="out_ln", grid=(seq // bt,),
        in_specs=[tok(HW), tok(HW), tok(HW), tok(HW), tok(D_MODEL), tok(D_MODEL),
                  full((D_MODEL, D_MODEL)), full((1, D_MODEL)), full((1, D_MODEL))],
        out_specs=[tok(D_MODEL), tok(HW), tok(HW), tok(HW), tok(HW), full((D_MODEL, D_MODEL)), full((8, D_MODEL))],
        out_shape=[jax.ShapeDtypeStruct((seq, D_MODEL), F32)] + [jax.ShapeDtypeStruct((seq, HW), BF16)] * 4
        + [jax.ShapeDtypeStruct((D_MODEL, D_MODEL), F32), jax.ShapeDtypeStruct((8, D_MODEL), F32)],
        compiler_params=_cparams(dimension_semantics=("arbitrary",)),
    )(oa, ob, ga, gb, x, tgt, w_out, ln_g, ln_b)


def _bwd_mid(dq_m, dkn, dv, dkpe, dqb, dkb, dvb, dga, dgb, cq, ckv, qn, kvn, w_uq_r, w_ukv_r, qg, kvg, tabs, bt):
    seq = cq.shape[0]

    def body(dqm_ref, dkn_ref, dv_ref, dkpe_ref, dqb_ref, dkb_ref, dvb_ref, dga_ref, dgb_ref,
             cq_ref, ckv_ref, qn_ref, kvn_ref, wuq_ref, wukv_ref, qg_ref, kvg_ref, tab_ref,
             dh_ref, guq_ref, gukv_ref, small_ref):
        i = pl.program_id(0)

        @pl.when(i == 0)
        def _():
            guq_ref[...] = jnp.zeros(guq_ref.shape, F32)
            gukv_ref[...] = jnp.zeros(gukv_ref.shape, F32)
            small_ref[...] = jnp.zeros(small_ref.shape, F32)

        m_tabs = (tab_ref[0], tab_ref[1], tab_ref[2])
        d_tabs = (tab_ref[3], tab_ref[4], tab_ref[5])
        tn = (((0,), (0,)), ((), ()))
        nt = (((1,), (1,)), ((), ()))

        def rms_bwd(c, dn, gain):
            r = lax.rsqrt(jnp.mean(c * c, axis=1, keepdims=True) + RMS_EPS)
            u = dn * gain
            dc = r * u - c * (r * r * r) * jnp.mean(u * c, axis=1, keepdims=True)
            return dc, jnp.sum(dn * c * r, axis=0, keepdims=True)

        dqm = dqm_ref[...]
        dq = jnp.concatenate(
            [dqm[:, :HW], _rope_wide(_rope_t, dqm[:, HW:], *m_tabs, MLA_ROPE // 2)], axis=1) * MLA_SCALE
        dq = dq.astype(BF16)
        guq_ref[...] += lax.dot_general(qn_ref[...], dq, tn, preferred_element_type=F32)
        dqn = lax.dot_general(dq, wuq_ref[...], nt, preferred_element_type=F32)
        dcq, gq = rms_bwd(cq_ref[...], dqn, qg_ref[...])
        small_ref[0:1, :] += gq

        dkv = jnp.concatenate([dkn_ref[...], dv_ref[...]], axis=1)
        gukv_ref[...] += lax.dot_general(kvn_ref[...], dkv, tn, preferred_element_type=F32)
        dkvn = lax.dot_general(dkv, wukv_ref[...], nt, preferred_element_type=F32)
        dckv, gkv = rms_bwd(ckv_ref[...], dkvn, kvg_ref[...])
        small_ref[1:2, :KV_RANK] += gkv

        dh_ref[:, C_CQ:C_CKV] = dcq.astype(BF16)
        dh_ref[:, C_CKV:C_KR] = dckv.astype(BF16)
        dh_ref[:, C_KR:C_GA] = _rope_t(dkpe_ref[...], *m_tabs, MLA_ROPE // 2).astype(BF16)
        dh_ref[:, C_GA:C_QB] = dga_ref[...]
        dh_ref[:, C_QB:C_KB] = (_rope_wide(_rope_t, dqb_ref[...], *d_tabs, DIL_ROT // 2) * DIL_SCALE).astype(BF16)
        dh_ref[:, C_KB:C_VB] = _rope_wide(_rope_t, dkb_ref[...], *d_tabs, DIL_ROT // 2).astype(BF16)
        dh_ref[:, C_VB:C_GB] = dvb_ref[...]
        dh_ref[:, C_GB:C_END] = dgb_ref[...]

    def tok(width):
        return pl.BlockSpec((bt, width), lambda i: (i, 0))

    def full(shape):
        return pl.BlockSpec(shape, lambda i: (0,) * len(shape))

    return _pcall(
        body, name="bwd_mid", grid=(seq // bt,),
        in_specs=[tok(HW + 256), tok(HW), tok(HW), tok(LANES), tok(HW), tok(HW), tok(HW), tok(HW), tok(HW),
                  tok(Q_RANK), tok(KV_RANK), tok(Q_RANK), tok(KV_RANK),
                  full(w_uq_r.shape), full(w_ukv_r.shape), full((1, Q_RANK)), full((1, KV_RANK)),
                  pl.BlockSpec((6, bt, LANES), lambda i: (0, i, 0))],
        out_specs=[tok(C_END), full(w_uq_r.shape), full(w_ukv_r.shape), full((8, Q_RANK))],
        out_shape=[jax.ShapeDtypeStruct((seq, C_END), BF16), jax.ShapeDtypeStruct(w_uq_r.shape, F32),
                   jax.ShapeDtypeStruct(w_ukv_r.shape, F32), jax.ShapeDtypeStruct((8, Q_RANK), F32)],
        compiler_params=_cparams(dimension_semantics=("arbitrary",)),
    )(dq_m, dkn, dv, dkpe, dqb, dkb, dvb, dga, dgb, cq, ckv, qn, kvn, w_uq_r, w_ukv_r, qg, kvg, tabs)


def _grad_x(dz, dh, w_in_r, bt):
    seq = dz.shape[0]

    def body(dz_ref, dh_ref, w_ref, gx_ref):
        gx_ref[...] = ALPHA * dz_ref[...] + lax.dot_general(
            dh_ref[...], w_ref[...], (((1,), (1,)), ((), ())), preferred_element_type=F32)

    return _pcall(
        body, name="grad_x", grid=(seq // bt,),
        in_specs=[pl.BlockSpec((bt, D_MODEL), lambda i: (i, 0)), pl.BlockSpec((bt, C_END), lambda i: (i, 0)),
                  pl.BlockSpec(w_in_r.shape, lambda i: (0, 0))],
        out_specs=pl.BlockSpec((bt, D_MODEL), lambda i: (i, 0)),
        out_shape=jax.ShapeDtypeStruct((seq, D_MODEL), F32),
        compiler_params=_cparams(dimension_semantics=("arbitrary",)),
    )(dz, dh, w_in_r)


def _grad_w_in(x, dh, bt):
    seq = x.shape[0]

    def body(x_ref, dh_ref, gw_ref):
        @pl.when(pl.program_id(0) == 0)
        def _():
            gw_ref[...] = jnp.zeros(gw_ref.shape, F32)

        gw_ref[...] += lax.dot_general(
            x_ref[...].astype(BF16), dh_ref[...], (((0,), (0,)), ((), ())), preferred_element_type=F32)

    return _pcall(
        body, name="grad_w_in", grid=(seq // bt,),
        in_specs=[pl.BlockSpec((bt, D_MODEL), lambda i: (i, 0)), pl.BlockSpec((bt, C_END), lambda i: (i, 0))],
        out_specs=pl.BlockSpec((D_MODEL, C_END), lambda i: (0, 0)),
        out_shape=jax.ShapeDtypeStruct((D_MODEL, C_END), F32),
        compiler_params=_cparams(dimension_semantics=("arbitrary",)),
    )(x, dh)


def _reorder_weights(w_in, w_uq, w_ukv):
    w_in_r = jnp.concatenate(
        [w_in[:, :640], jnp.tile(w_in[:, 640:672], (1, 4)), w_in[:, 672:]], axis=1)
    uq = w_uq.reshape(Q_RANK, HEADS, MLA_NOPE + MLA_ROPE)
    w_uq_r = jnp.concatenate(
        [uq[:, :, :MLA_NOPE].reshape(Q_RANK, HW), uq[:, :, MLA_NOPE:].reshape(Q_RANK, HEADS * MLA_ROPE)], axis=1)
    ukv = w_ukv.reshape(KV_RANK, HEADS, MLA_NOPE + MLA_V)
    w_ukv_r = jnp.concatenate(
        [ukv[:, :, :MLA_NOPE].reshape(KV_RANK, HW), ukv[:, :, MLA_NOPE:].reshape(KV_RANK, HW)], axis=1)
    return w_in_r, w_uq_r, w_ukv_r


def _restore_grads(g_in_r, g_uq_r, g_ukv_r):
    g_in = jnp.concatenate(
        [g_in_r[:, :640], g_in_r[:, 640:768].reshape(D_MODEL, 4, MLA_ROPE).sum(axis=1), g_in_r[:, 768:]], axis=1)
    g_uq = jnp.concatenate(
        [g_uq_r[:, :HW].reshape(Q_RANK, HEADS, MLA_NOPE), g_uq_r[:, HW:].reshape(Q_RANK, HEADS, MLA_ROPE)],
        axis=2).reshape(Q_RANK, HEADS * (MLA_NOPE + MLA_ROPE))
    g_ukv = jnp.concatenate(
        [g_ukv_r[:, :HW].reshape(KV_RANK, HEADS, MLA_NOPE), g_ukv_r[:, HW:].reshape(KV_RANK, HEADS, MLA_V)],
        axis=2).reshape(KV_RANK, HEADS * (MLA_NOPE + MLA_V))
    return g_in, g_uq, g_ukv


def _local_step(x, tgt, w_in, w_uq, w_ukv, w_out, q_norm_g, kv_norm_g, ln_g, ln_b,
                bt=256, blk_mla=256, blk_dil=256):
    seq = x.shape[0]
    tabs = jnp.asarray(_rope_tables(seq))
    w_in_r, w_uq_r, w_ukv_r = _reorder_weights(w_in, w_uq, w_ukv)
    qg, kvg = q_norm_g.reshape(1, -1), kv_norm_g.reshape(1, -1)

    (cq, ckv, qn, kvn, qcat, kn, kpe, v, ga, gb, qb, kb, vb) = _fwd_proj(
        x, w_in_r, w_uq_r, w_ukv_r, qg, kvg, tabs, bt)

    nq_m, nq_d = seq // blk_mla, seq // blk_dil
    span_d = DIL_CONFIGS[-1][0] // blk_dil
    bias_m, bias_d = jnp.asarray(_mla_bias(blk_mla)), jnp.asarray(_dil_bias(blk_dil))
    oa, lse_a = _attn_fwd("mla_fwd", qcat, kn, kpe, v, bias_m, _steps(nq_m, nq_m, False, True), blk_mla)
    ob, lse_b = _attn_fwd("dil_fwd", qb, kb, None, vb, bias_d, _steps(nq_d, span_d, False, False), blk_dil)

    dz, doa, dob, dga, dgb, g_out, small1 = _out_ln(
        oa, ob, ga, gb, x, tgt, w_out, ln_g.reshape(1, -1), ln_b.reshape(1, -1), bt)

    dq_m, dkn, dkpe, dv = _attn_bwd(
        "mla_bwd", qcat, kn, kpe, v, bias_m, doa, oa, lse_a, _steps(nq_m, nq_m, True, True), blk_mla)
    dqb, dkb, dvb = _attn_bwd(
        "dil_bwd", qb, kb, None, vb, bias_d, dob, ob, lse_b, _steps(nq_d, span_d, True, False), blk_dil)

    dh, g_uq_r, g_ukv_r, small2 = _bwd_mid(
        dq_m, dkn, dv, dkpe, dqb, dkb, dvb, dga, dgb, cq, ckv, qn, kvn, w_uq_r, w_ukv_r, qg, kvg, tabs, bt)
    grad_x = _grad_x(dz, dh, w_in_r, bt)
    g_in_r = _grad_w_in(x, dh, bt)
    g_in, g_uq, g_ukv = _restore_grads(g_in_r, g_uq_r, g_ukv_r)

    loss = (0.5 / D_MODEL) * jnp.sum(small1[2])
    return loss, grad_x, g_in, g_uq, g_ukv, g_out, small2[0], small2[1, :KV_RANK], small1[0], small1[1]


MESH_ID = pl.DeviceIdType.MESH
SHARD_SHAPES = ((D_MODEL, IN_WIDTH // N_DEV), (Q_RANK, 768 // N_DEV), (KV_RANK, 1024 // N_DEV), (D_MODEL // N_DEV, D_MODEL))
ADAM_ROWS = (32, 128, 128, 16)


def _me():
    x, y, c = lax.axis_index("x"), lax.axis_index("y"), lax.axis_index("c")
    return x, y, c, 4 * x + 2 * y + c


def _peer(k):
    x, y, c, _ = _me()
    px = 1 - x if (k >> 2) & 1 else x
    py = 1 - y if (k >> 1) & 1 else y
    pc = 1 - c if k & 1 else c
    return (px, py, pc), 4 * px + 2 * py + pc


def _all_gather_weights(shards):
    def body(*refs):
        ins, outs = refs[:4], refs[4:8]
        send_sems, recv_sems = refs[8:]
        me = _me()[3]
        for t in range(4):
            outs[t][me] = ins[t][...].astype(BF16)
        sends = []
        for t in range(4):
            for k in range(1, N_DEV):
                peer, _ = _peer(k)
                cp = pltpu.make_async_remote_copy(
                    src_ref=outs[t].at[me], dst_ref=outs[t].at[me], send_sem=send_sems.at[t, k - 1],
                    recv_sem=recv_sems.at[t, k - 1], device_id=peer, device_id_type=MESH_ID)
                cp.start()
                sends.append(cp)
        for t in range(4):
            for k in range(1, N_DEV):
                peer, pidx = _peer(k)
                pltpu.make_async_remote_copy(
                    src_ref=outs[t].at[pidx], dst_ref=outs[t].at[pidx], send_sem=send_sems.at[t, k - 1],
                    recv_sem=recv_sems.at[t, k - 1], device_id=peer, device_id_type=MESH_ID).wait_recv()
        for cp in sends:
            cp.wait_send()

    vmem = pl.BlockSpec(memory_space=pltpu.VMEM)
    return _pcall(
        body, name="gather_weights",
        in_specs=[vmem] * 4, out_specs=[vmem] * 4,
        out_shape=[jax.ShapeDtypeStruct((N_DEV,) + s, BF16) for s in SHARD_SHAPES],
        scratch_shapes=[pltpu.SemaphoreType.DMA((4, N_DEV - 1)), pltpu.SemaphoreType.DMA((4, N_DEV - 1))],
        compiler_params=_cparams(),
    )(*shards)


def _adamw(w, g, m, v):
    m = ADAM_B1 * m + (1.0 - ADAM_B1) * g
    v = ADAM_B2 * v + (1.0 - ADAM_B2) * jnp.square(g)
    m_hat = m / (1.0 - ADAM_B1 ** ADAM_STEP)
    v_hat = v / (1.0 - ADAM_B2 ** ADAM_STEP)
    delta = -ADAM_LR * (m_hat / (jnp.sqrt(v_hat) + ADAM_EPS) + ADAM_WD * w)
    return delta, m, v


def _reduce_adam(grads3, small_part, wmv, small_wmv):
    def body(*refs):
        g3 = refs[0:4]
        sp_ref = refs[4]
        wmv_refs = [refs[5 + 3 * t:8 + 3 * t] for t in range(4)]
        swmv_ref = refs[17]
        out_refs = [refs[18 + 4 * t:22 + 4 * t] for t in range(4)]
        sout_ref = refs[34]
        recv = refs[35:39]
        rsmall = refs[39]
        send_sems, recv_sems, local_sems = refs[40:43]
        me = _me()[3]

        rsmall[0] = sp_ref[...]
        local = []
        for t in range(4):
            cp = pltpu.make_async_copy(g3[t].at[me], recv[t].at[0], local_sems.at[t])
            cp.start()
            local.append(cp)
        sends = []
        for k in range(1, N_DEV):
            peer, pidx = _peer(k)
            for t in range(5):
                src = rsmall.at[0] if t == 4 else g3[t].at[pidx]
                dst = rsmall.at[k] if t == 4 else recv[t].at[k]
                cp = pltpu.make_async_remote_copy(
                    src_ref=src, dst_ref=dst, send_sem=send_sems.at[t, k - 1], recv_sem=recv_sems.at[t, k - 1],
                    device_id=peer, device_id_type=MESH_ID)
                cp.start()
                sends.append(cp)
        for cp in local:
            cp.wait()
        for cp in sends:
            cp.wait_recv()

        tot = rsmall[me]
        for d in range(1, N_DEV):
            tot = tot + rsmall[jnp.bitwise_xor(me, d)]
        delta, m, v = _adamw(swmv_ref[0], tot, swmv_ref[1], swmv_ref[2])
        sout_ref[0], sout_ref[1], sout_ref[2], sout_ref[3] = tot, delta, m, v

        for t in range(4):
            rows = ADAM_ROWS[t]
            w_ref, m_ref, v_ref = wmv_refs[t]
            g_out, d_out, m_out, v_out = out_refs[t]

            def step(i, carry, t=t, rows=rows, w_ref=w_ref, m_ref=m_ref, v_ref=v_ref,
                     g_out=g_out, d_out=d_out, m_out=m_out, v_out=v_out):
                r = pl.ds(pl.multiple_of(i * rows, rows), rows)
                g = recv[t][0, r, :]
                for k in range(1, N_DEV):
                    g = g + recv[t][k, r, :]
                delta, m, v = _adamw(w_ref[r, :], g, m_ref[r, :], v_ref[r, :])
                g_out[r, :], d_out[r, :], m_out[r, :], v_out[r, :] = g, delta, m, v
                return carry

            lax.fori_loop(0, SHARD_SHAPES[t][0] // rows, step, 0)

        for cp in sends:
            cp.wait_send()

    vmem = pl.BlockSpec(memory_space=pltpu.VMEM)
    hbm = pl.BlockSpec(memory_space=pl.ANY)
    flat_wmv = [a for trio in wmv for a in trio]
    return _pcall(
        body, name="reduce_adam",
        in_specs=[hbm] * 4 + [vmem] * 14,
        out_specs=[vmem] * 17,
        out_shape=[jax.ShapeDtypeStruct(s, F32) for s in SHARD_SHAPES for _ in range(4)]
        + [jax.ShapeDtypeStruct((4, 8, D_MODEL), F32)],
        scratch_shapes=[pltpu.VMEM((N_DEV,) + s, F32) for s in SHARD_SHAPES]
        + [pltpu.VMEM((N_DEV, 8, D_MODEL), F32),
           pltpu.SemaphoreType.DMA((5, N_DEV - 1)), pltpu.SemaphoreType.DMA((5, N_DEV - 1)),
           pltpu.SemaphoreType.DMA((4,))],
        compiler_params=_cparams(),
    )(*grads3, small_part, *flat_wmv, small_wmv)


def _small_rows(ln_g, ln_b, q_norm_g, kv_norm_g):
    pad = lambda a: jnp.pad(a, (0, D_MODEL - a.shape[0]))
    return jnp.pad(jnp.stack([ln_g, ln_b, pad(q_norm_g), pad(kv_norm_g)]), ((0, 4), (0, 0)))


def kernel(x, w_in, q_norm_g, kv_norm_g, w_uq, w_ukv, w_out, ln_g, ln_b, loss_target, m_w_in, m_q_norm_g, m_kv_norm_g, m_w_uq, m_w_ukv, m_w_out, m_ln_g, m_ln_b, v_w_in, v_q_norm_g, v_kv_norm_g, v_w_uq, v_w_ukv, v_w_out, v_ln_g, v_ln_b):
    a_in, a_uq, a_ukv, a_out = _all_gather_weights([w_in, w_uq, w_ukv, w_out])
    full_in = a_in.transpose(1, 0, 2).reshape(D_MODEL, IN_WIDTH)
    full_uq = a_uq.transpose(1, 0, 2).reshape(Q_RANK, 768)
    full_ukv = a_ukv.transpose(1, 0, 2).reshape(KV_RANK, 1024)
    full_out = a_out.reshape(D_MODEL, D_MODEL)

    loss_l, grad_x, g_in, g_uq, g_ukv, g_out, g_qg, g_kvg, g_lng, g_lnb = _local_step(
        x[0], loss_target[0], full_in, full_uq, full_ukv, full_out, q_norm_g, kv_norm_g, ln_g, ln_b)
    loss = lax.psum(loss_l, ("x", "y", "c"))

    grads3 = [g_in.reshape(D_MODEL, N_DEV, -1).transpose(1, 0, 2), g_uq.reshape(Q_RANK, N_DEV, -1).transpose(1, 0, 2),
              g_ukv.reshape(KV_RANK, N_DEV, -1).transpose(1, 0, 2), g_out.reshape(N_DEV, D_MODEL // N_DEV, D_MODEL)]
    small_part = _small_rows(g_lng, g_lnb, g_qg, g_kvg)
    small_wmv = jnp.stack([_small_rows(ln_g, ln_b, q_norm_g, kv_norm_g),
                           _small_rows(m_ln_g, m_ln_b, m_q_norm_g, m_kv_norm_g),
                           _small_rows(v_ln_g, v_ln_b, v_q_norm_g, v_kv_norm_g)])
    wmv = [(w_in, m_w_in, v_w_in), (w_uq, m_w_uq, v_w_uq), (w_ukv, m_w_ukv, v_w_ukv), (w_out, m_w_out, v_w_out)]
    res = _reduce_adam(grads3, small_part, wmv, small_wmv)
    big = [res[4 * t:4 * t + 4] for t in range(4)]
    small = res[16]

    def group(kind):
        s = small[kind]
        return (big[0][kind], s[2, :Q_RANK], s[3, :KV_RANK], big[1][kind], big[2][kind], big[3][kind], s[0], s[1])

    return (loss, grad_x[None], *group(0), *group(1), *group(2), *group(3))
```

```python
import numpy as np
import jax
import jax.numpy as jnp
from jax import lax
from jax.experimental import pallas as pl
from jax.experimental.pallas import tpu as pltpu

F32 = jnp.float32
BF16 = jnp.bfloat16

D_MODEL = 1024
ROPE_THETA = 500000.0
NEG = -1e30
RMS_EPS = 1e-6
LN_EPS = 1e-5
HEADS = 8
MLA_NOPE = 64
MLA_ROPE = 32
MLA_V = 64
Q_RANK = 384
KV_RANK = 256
DIL_HEAD = 64
DIL_ROT = 16
DIL_CONFIGS = ((128, 1), (512, 4), (2048, 16))
HW = HEADS * 64
QW = HW + HEADS * MLA_ROPE
IN_SPLITS = (Q_RANK, KV_RANK, MLA_ROPE, HW, HW, HW, HW, HW)
IN_WIDTH = sum(IN_SPLITS)
ALPHA = 2.0 ** 0.25
MLA_SCALE = (MLA_NOPE + MLA_ROPE) ** -0.5
DIL_SCALE = DIL_HEAD ** -0.5

ADAM_LR = 0.001
ADAM_B1 = 0.9
ADAM_B2 = 0.999
ADAM_EPS = 1e-08
ADAM_WD = 0.01
ADAM_STEP = 10

N_DEV = 8
LANES = 128
VMEM_LIMIT = 56 * 1024 * 1024

C_CQ, C_CKV, C_KR, C_GA, C_QB, C_KB, C_VB, C_GB, C_END = 0, 384, 640, 768, 1280, 1792, 2304, 2816, 3328

NT = (((1,), (1,)), ((), ()))
TN = (((0,), (0,)), ((), ()))


def _pcall(body, **kw):
    return pl.pallas_call(body, **kw)


def _cparams(**kw):
    return pltpu.CompilerParams(vmem_limit_bytes=VMEM_LIMIT, **kw)


def _rope_tables(seq):
    def tabs(dim, period):
        half = dim // 2
        inv = np.float32(ROPE_THETA) ** (-np.arange(0, dim, 2, dtype=np.float32) / np.float32(dim))
        ang = np.arange(seq, dtype=np.float32)[:, None] * inv.astype(np.float32)[None, :]
        cos, sin = np.cos(ang).astype(np.float32), np.sin(ang).astype(np.float32)
        j = np.arange(LANES) % period
        f = j % half
        c = np.where(j < dim, cos[:, f], np.float32(1.0))
        s1 = np.where(j < half, -sin[:, f], np.float32(0.0))
        s2 = np.where((j >= half) & (j < dim), sin[:, f], np.float32(0.0))
        return [c, s1, s2]
    return np.stack(tabs(MLA_ROPE, MLA_ROPE) + tabs(DIL_ROT, DIL_HEAD)).astype(np.float32)


def _rope(t, c, s1, s2, half):
    return t * c + pltpu.roll(t, LANES - half, 1) * s1 + pltpu.roll(t, half, 1) * s2


def _rope_t(d, c, s1, s2, half):
    return d * c + pltpu.roll(d * s1, half, 1) + pltpu.roll(d * s2, LANES - half, 1)


def _rope_wide(fn, t, c, s1, s2, half):
    return jnp.concatenate(
        [fn(t[:, i:i + LANES], c, s1, s2, half) for i in range(0, t.shape[1], LANES)], axis=1)


def _mla_bias_t(blk):
    a = np.arange(blk)
    causal = np.where(a[:, None] <= a[None, :], 0.0, NEG)
    return np.stack([np.zeros((blk, blk)), causal]).astype(np.float32)


def _dil_bias_t(blk):
    span = DIL_CONFIGS[-1][0] // blk
    a = np.arange(blk)
    out = []
    for off in range(span + 1):
        delta = blk * off + a[None, :] - a[:, None]
        mult = np.zeros((blk, blk))
        for window, dil in DIL_CONFIGS:
            mult += (delta >= 0) & (delta % dil == 0) & (delta <= window)
        out.append(np.where(mult > 0, np.log(np.maximum(mult, 1.0)), NEG))
    return np.stack(out).astype(np.float32)


def _steps(nq, span, by_key, diag_only_bias):
    rows = []
    if by_key:
        for ki in range(nq):
            hi = min(nq - 1, ki + span)
            for qi in range(ki, hi + 1):
                rows.append((qi, ki, int(qi == ki), int(qi == hi)))
    else:
        for qi in range(nq):
            lo = max(0, qi - span)
            for ki in range(lo, qi + 1):
                rows.append((qi, ki, int(ki == lo), int(ki == qi)))
    arr = np.array(rows, dtype=np.int32)
    off = arr[:, 0] - arr[:, 1]
    bias_idx = (off == 0).astype(np.int32) if diag_only_bias else off.astype(np.int32)
    return [jnp.asarray(v) for v in (arr[:, 0], arr[:, 1], bias_idx, arr[:, 2], arr[:, 3])]


def _fwd_proj(x, w_in_r, w_uq_r, w_ukv_r, qg, kvg, tabs, bt):
    seq = x.shape[0]

    def body(x_ref, win_ref, wuq_ref, wukv_ref, qg_ref, kvg_ref, tab_ref,
             cq_ref, ckv_ref, qn_ref, kvn_ref, qcat_ref, kn_ref, kpe_ref, v_ref,
             ga_ref, gb_ref, qb_ref, kb_ref, vb_ref, knt_ref, kpet_ref, vt_ref, kbt_ref, vbt_ref):
        xb = x_ref[...].astype(BF16)

        def proj(lo, hi):
            return jnp.dot(xb, win_ref[:, lo:hi], preferred_element_type=F32)

        m_tabs = (tab_ref[0], tab_ref[1], tab_ref[2])
        d_tabs = (tab_ref[3], tab_ref[4], tab_ref[5])

        cq = proj(C_CQ, C_CKV)
        cq_ref[...] = cq
        qn = (cq * lax.rsqrt(jnp.mean(cq * cq, axis=1, keepdims=True) + RMS_EPS) * qg_ref[...]).astype(BF16)
        qn_ref[...] = qn
        q = jnp.dot(qn, wuq_ref[...], preferred_element_type=F32)
        qcat_ref[:, :HW] = (q[:, :HW] * MLA_SCALE).astype(BF16)
        qcat_ref[:, HW:] = (_rope_wide(_rope, q[:, HW:], *m_tabs, MLA_ROPE // 2) * MLA_SCALE).astype(BF16)

        ckv = proj(C_CKV, C_KR)
        ckv_ref[...] = ckv
        kvn = (ckv * lax.rsqrt(jnp.mean(ckv * ckv, axis=1, keepdims=True) + RMS_EPS) * kvg_ref[...]).astype(BF16)
        kvn_ref[...] = kvn
        kv = jnp.dot(kvn, wukv_ref[...], preferred_element_type=F32)
        kn_ref[...] = kv[:, :HW].astype(BF16)
        v_ref[...] = kv[:, HW:].astype(BF16)
        knt_ref[...] = kv[:, :HW].T.astype(BF16)
        vt_ref[...] = kv[:, HW:].T.astype(BF16)

        kpe = _rope(proj(C_KR, C_GA), *m_tabs, MLA_ROPE // 2)
        kpe_ref[...] = kpe.astype(BF16)
        kpet_ref[...] = kpe.T[:MLA_ROPE, :].astype(BF16)
        ga_ref[...] = proj(C_GA, C_QB)
        qb_ref[...] = (_rope_wide(_rope, proj(C_QB, C_KB), *d_tabs, DIL_ROT // 2) * DIL_SCALE).astype(BF16)
        kb = _rope_wide(_rope, proj(C_KB, C_VB), *d_tabs, DIL_ROT // 2)
        kb_ref[...] = kb.astype(BF16)
        kbt_ref[...] = kb.T.astype(BF16)
        vb = proj(C_VB, C_GB)
        vb_ref[...] = vb.astype(BF16)
        vbt_ref[...] = vb.T.astype(BF16)
        gb_ref[...] = proj(C_GB, C_END)

    def tok(width):
        return pl.BlockSpec((bt, width), lambda i: (i, 0))

    def tok_t(height):
        return pl.BlockSpec((height, bt), lambda i: (0, i))

    def full(a):
        return pl.BlockSpec(a.shape, lambda i: (0,) * a.ndim)

    outs = [(Q_RANK, F32), (KV_RANK, F32), (Q_RANK, BF16), (KV_RANK, BF16), (QW, BF16), (HW, BF16),
            (LANES, BF16), (HW, BF16), (HW, F32), (HW, F32), (HW, BF16), (HW, BF16), (HW, BF16)]
    outs_t = [HW, MLA_ROPE, HW, HW, HW]
    return _pcall(
        body, name="fwd_proj", grid=(seq // bt,),
        in_specs=[tok(D_MODEL), full(w_in_r), full(w_uq_r), full(w_ukv_r), full(qg), full(kvg),
                  pl.BlockSpec((6, bt, LANES), lambda i: (0, i, 0))],
        out_specs=[tok(w) for w, _ in outs] + [tok_t(h) for h in outs_t],
        out_shape=[jax.ShapeDtypeStruct((seq, w), dt) for w, dt in outs]
        + [jax.ShapeDtypeStruct((h, seq), BF16) for h in outs_t],
        compiler_params=_cparams(dimension_semantics=("arbitrary",)),
    )(x, w_in_r, w_uq_r, w_ukv_r, qg, kvg, tabs)


def _head_masks(lane, h):
    e, g = h % 2, h % 4
    me = (lane >= 64 * e) & (lane < 64 * e + 64)
    mr = (lane >= 32 * g) & (lane < 32 * g + 32)
    return me, mr


def _masked(mask, a):
    return jnp.where(mask, a, jnp.zeros_like(a))


def _attn_fwd(name, q, k, kpe, vt, bias_t, steps, blk):
    seq = q.shape[0]
    mla = kpe is not None
    n_steps = int(steps[0].shape[0])

    def body(qi_r, ki_r, bi_r, fi_r, la_r, *refs):
        if mla:
            q_ref, k_ref, kpe_ref, vt_ref, b_ref, o_ref, lse_ref, m_sc, l_sc, acc_sc = refs
        else:
            q_ref, k_ref, vt_ref, b_ref, o_ref, lse_ref, m_sc, l_sc, acc_sc = refs
        t = pl.program_id(0)

        @pl.when(fi_r[t] == 1)
        def _():
            m_sc[...] = jnp.full(m_sc.shape, NEG, F32)
            l_sc[...] = jnp.zeros(l_sc.shape, F32)
            acc_sc[...] = jnp.zeros(acc_sc.shape, F32)

        lane = lax.broadcasted_iota(jnp.int32, (1, LANES), 1)
        bias = b_ref[0]
        for j in range(HEADS // 2):
            cols = slice(LANES * j, LANES * (j + 1))
            qc = q_ref[:, cols]
            if mla:
                qc = jnp.concatenate([qc, q_ref[:, HW + LANES * (j // 2):HW + LANES * (j // 2 + 1)]], axis=1)
            kj = k_ref[:, cols]
            for h in (2 * j, 2 * j + 1):
                me, mr = _head_masks(lane, h)
                ke = _masked(me, kj)
                if mla:
                    ke = jnp.concatenate([ke, _masked(mr, kpe_ref[...])], axis=1)
                st = lax.dot_general(ke, qc, NT, preferred_element_type=F32) + bias
                hrow = slice(h, h + 1)
                m_prev = m_sc[hrow, :]
                m_new = jnp.maximum(m_prev, jnp.max(st, axis=0, keepdims=True))
                alpha = jnp.exp(m_prev - m_new)
                pt = jnp.exp(st - m_new)
                l_sc[hrow, :] = alpha * l_sc[hrow, :] + jnp.sum(pt, axis=0, keepdims=True)
                m_sc[hrow, :] = m_new
                rows = slice(64 * h, 64 * h + 64)
                acc_sc[rows, :] = alpha * acc_sc[rows, :] + jnp.dot(
                    vt_ref[rows, :], pt.astype(BF16), preferred_element_type=F32)

        @pl.when(la_r[t] == 1)
        def _():
            for h in range(HEADS):
                rows = slice(64 * h, 64 * h + 64)
                acc_sc[rows, :] = acc_sc[rows, :] / l_sc[h:h + 1, :]
            o_ref[...] = acc_sc[...].T
            lse_ref[...] = m_sc[...] + jnp.log(l_sc[...])

    qmap = lambda t, qi, ki, bi, fi, la: (qi[t], 0)
    kmap = lambda t, qi, ki, bi, fi, la: (ki[t], 0)
    in_specs = [pl.BlockSpec((blk, q.shape[1]), qmap), pl.BlockSpec((blk, HW), kmap)]
    args = [q, k]
    if mla:
        in_specs.append(pl.BlockSpec((blk, LANES), kmap))
        args.append(kpe)
    in_specs += [pl.BlockSpec((HW, blk), lambda t, qi, ki, bi, fi, la: (0, ki[t])),
                 pl.BlockSpec((1, blk, blk), lambda t, qi, ki, bi, fi, la: (bi[t], 0, 0))]
    args += [vt, bias_t]
    return _pcall(
        body, name=name,
        grid_spec=pltpu.PrefetchScalarGridSpec(
            num_scalar_prefetch=5, grid=(n_steps,), in_specs=in_specs,
            out_specs=[pl.BlockSpec((blk, HW), qmap),
                       pl.BlockSpec((HEADS, blk), lambda t, qi, ki, bi, fi, la: (0, qi[t]))],
            scratch_shapes=[pltpu.VMEM((HEADS, blk), F32), pltpu.VMEM((HEADS, blk), F32),
                            pltpu.VMEM((HW, blk), F32)]),
        out_shape=[jax.ShapeDtypeStruct((seq, HW), F32), jax.ShapeDtypeStruct((HEADS, seq), F32)],
        compiler_params=_cparams(dimension_semantics=("arbitrary",)),
    )(*steps, *args)


def _attn_bwd(name, q, k, kpe, v, kt, kpet, bias_t, do, lse, dstat, steps, blk):
    seq = q.shape[0]
    mla = kpe is not None
    qw = q.shape[1]
    n_steps = int(steps[0].shape[0])
    dk_dtype = BF16 if mla else F32

    def body(qi_r, ki_r, bi_r, fi_r, la_r, *refs):
        if mla:
            (q_ref, k_ref, kpe_ref, v_ref, kt_ref, kpet_ref, b_ref, do_ref, lse_ref, d_ref,
             dq_ref, dk_ref, dkpe_ref, dv_ref, dk_sc, dkpe_sc, dv_sc) = refs
        else:
            (q_ref, k_ref, v_ref, kt_ref, b_ref, do_ref, lse_ref, d_ref,
             dq_ref, dk_ref, dv_ref, dk_sc, dv_sc) = refs
        t = pl.program_id(0)

        @pl.when(t == 0)
        def _():
            dq_ref[...] = jnp.zeros(dq_ref.shape, F32)

        @pl.when(fi_r[t] == 1)
        def _():
            dk_sc[...] = jnp.zeros(dk_sc.shape, F32)
            dv_sc[...] = jnp.zeros(dv_sc.shape, F32)
            if mla:
                dkpe_sc[...] = jnp.zeros(dkpe_sc.shape, F32)

        qi = qi_r[t]
        lane = lax.broadcasted_iota(jnp.int32, (1, LANES), 1)
        bias = b_ref[0]
        for j in range(HEADS // 2):
            cols = slice(LANES * j, LANES * (j + 1))
            qc = q_ref[:, cols]
            if mla:
                qc = jnp.concatenate([qc, q_ref[:, HW + LANES * (j // 2):HW + LANES * (j // 2 + 1)]], axis=1)
            kj = k_ref[:, cols]
            vj = v_ref[:, cols]
            doj = do_ref[:, cols]
            for h in (2 * j, 2 * j + 1):
                me, mr = _head_masks(lane, h)
                ke = _masked(me, kj)
                if mla:
                    ke = jnp.concatenate([ke, _masked(mr, kpe_ref[...])], axis=1)
                st = lax.dot_general(ke, qc, NT, preferred_element_type=F32) + bias
                pt = jnp.exp(st - lse_ref[h:h + 1, :])
                dpt = lax.dot_general(_masked(me, vj), doj, NT, preferred_element_type=F32)
                dst = (pt * (dpt - d_ref[h:h + 1, :])).astype(BF16)
                dv_sc[:, cols] += jnp.dot(pt.astype(BF16), _masked(me, doj), preferred_element_type=F32)
                dkc = jnp.dot(dst, qc, preferred_element_type=F32)
                dk_sc[:, cols] += jnp.where(me, dkc[:, :LANES], 0.0)
                ktl = kt_ref[64 * h:64 * h + 64, :]
                if mla:
                    dkpe_sc[...] += jnp.where(mr, dkc[:, LANES:], 0.0)
                    ktl = jnp.concatenate([ktl, kpet_ref[...]], axis=0)
                dqc = jnp.dot(ktl, dst, preferred_element_type=F32)
                dq_ref[qi, 64 * h:64 * h + 64, :] += dqc[:64]
                if mla:
                    dq_ref[qi, HW + MLA_ROPE * h:HW + MLA_ROPE * (h + 1), :] += dqc[64:]

        @pl.when(la_r[t] == 1)
        def _():
            dk_ref[...] = dk_sc[...].astype(dk_ref.dtype)
            dv_ref[...] = dv_sc[...].astype(dv_ref.dtype)
            if mla:
                dkpe_ref[...] = dkpe_sc[...]

    qmap = lambda t, qi, ki, bi, fi, la: (qi[t], 0)
    kmap = lambda t, qi, ki, bi, fi, la: (ki[t], 0)
    qmap_t = lambda t, qi, ki, bi, fi, la: (0, qi[t])
    kmap_t = lambda t, qi, ki, bi, fi, la: (0, ki[t])
    in_specs = [pl.BlockSpec((blk, qw), qmap), pl.BlockSpec((blk, HW), kmap)]
    args = [q, k]
    if mla:
        in_specs.append(pl.BlockSpec((blk, LANES), kmap))
        args.append(kpe)
    in_specs += [pl.BlockSpec((blk, HW), kmap), pl.BlockSpec((HW, blk), kmap_t)]
    args += [v, kt]
    if mla:
        in_specs.append(pl.BlockSpec((MLA_ROPE, blk), kmap_t))
        args.append(kpet)
    in_specs += [pl.BlockSpec((1, blk, blk), lambda t, qi, ki, bi, fi, la: (bi[t], 0, 0)),
                 pl.BlockSpec((blk, HW), qmap), pl.BlockSpec((HEADS, blk), qmap_t), pl.BlockSpec((HEADS, blk), qmap_t)]
    args += [bias_t, do, lse, dstat]
    dq_shape = (seq // blk, qw, blk)
    out_specs = [pl.BlockSpec(dq_shape, lambda t, qi, ki, bi, fi, la: (0, 0, 0)), pl.BlockSpec((blk, HW), kmap)]
    out_shape = [jax.ShapeDtypeStruct(dq_shape, F32), jax.ShapeDtypeStruct((seq, HW), dk_dtype)]
    scratch = [pltpu.VMEM((blk, HW), F32)]
    if mla:
        out_specs.append(pl.BlockSpec((blk, LANES), kmap))
        out_shape.append(jax.ShapeDtypeStruct((seq, LANES), F32))
        scratch.append(pltpu.VMEM((blk, LANES), F32))
    out_specs.append(pl.BlockSpec((blk, HW), kmap))
    out_shape.append(jax.ShapeDtypeStruct((seq, HW), BF16))
    scratch.append(pltpu.VMEM((blk, HW), F32))
    return _pcall(
        body, name=name,
        grid_spec=pltpu.PrefetchScalarGridSpec(
            num_scalar_prefetch=5, grid=(n_steps,), in_specs=in_specs, out_specs=out_specs,
            scratch_shapes=scratch),
        out_shape=out_shape,
        compiler_params=_cparams(dimension_semantics=("arbitrary",)),
    )(*steps, *args)


def _out_ln(oa, ob, ga, gb, x, tgt, w_out, ln_g, ln_b, bt):
    seq = x.shape[0]

    def body(oa_ref, ob_ref, ga_ref, gb_ref, x_ref, tgt_ref, w_ref, g_ref, b_ref,
             dz_ref, doa_ref, dob_ref, dga_ref, dgb_ref, da_ref, db_ref, gw_ref, small_ref):
        i = pl.program_id(0)

        @pl.when(i == 0)
        def _():
            gw_ref[...] = jnp.zeros(gw_ref.shape, F32)
            small_ref[...] = jnp.zeros(small_ref.shape, F32)

        def gate(g):
            sig = 1.0 / (1.0 + jnp.exp(-g))
            return g * sig, sig * (1.0 + g * (1.0 - sig))

        o_a, o_b = oa_ref[...], ob_ref[...]
        g_a, g_b = ga_ref[...], gb_ref[...]
        sa, dsa = gate(g_a)
        sb, dsb = gate(g_b)
        mix = jnp.concatenate([o_a * sa, o_b * sb], axis=1).astype(BF16)
        z = ALPHA * x_ref[...] + jnp.dot(mix, w_ref[...], preferred_element_type=F32)
        mu = jnp.mean(z, axis=1, keepdims=True)
        zc = z - mu
        rstd = lax.rsqrt(jnp.mean(zc * zc, axis=1, keepdims=True) + LN_EPS)
        xhat = zc * rstd
        gam = g_ref[...]
        diff = xhat * gam + b_ref[...] - tgt_ref[...]
        dy = diff * (1.0 / D_MODEL)
        small_ref[0:1, :] += jnp.sum(dy * xhat, axis=0, keepdims=True)
        small_ref[1:2, :] += jnp.sum(dy, axis=0, keepdims=True)
        small_ref[2:3, :] += jnp.sum(diff * diff, axis=0, keepdims=True)
        dxh = dy * gam
        dz = rstd * (dxh - jnp.mean(dxh, axis=1, keepdims=True) - xhat * jnp.mean(dxh * xhat, axis=1, keepdims=True))
        dz_ref[...] = dz
        dzb = dz.astype(BF16)
        gw_ref[...] += lax.dot_general(mix, dzb, TN, preferred_element_type=F32)
        dmix = lax.dot_general(dzb, w_ref[...], NT, preferred_element_type=F32)
        doa, dob = dmix[:, :HW] * sa, dmix[:, HW:] * sb
        doa_ref[...] = doa.astype(BF16)
        dob_ref[...] = dob.astype(BF16)
        dga_ref[...] = (dmix[:, :HW] * o_a * dsa).astype(BF16)
        dgb_ref[...] = (dmix[:, HW:] * o_b * dsb).astype(BF16)
        head_of = lax.broadcasted_iota(jnp.int32, (HEADS, HW), 1) // 64
        ind = (head_of == lax.broadcasted_iota(jnp.int32, (HEADS, HW), 0)).astype(F32)
        da_ref[...] = lax.dot_general(ind, doa * o_a, NT, preferred_element_type=F32, precision=lax.Precision.HIGHEST)
        db_ref[...] = lax.dot_general(ind, dob * o_b, NT, preferred_element_type=F32, precision=lax.Precision.HIGHEST)

    def tok(width):
        return pl.BlockSpec((bt, width), lambda i: (i, 0))

    def full(shape):
        return pl.BlockSpec(shape, lambda i: (0,) * len(shape))

    stat = pl.BlockSpec((HEADS, bt), lambda i: (0, i))
    return _pcall(
        body, name="out_ln", grid=(seq // bt,),
        in_specs=[tok(HW), tok(HW), tok(HW), tok(HW), tok(D_MODEL), tok(D_MODEL),
                  full((D_MODEL, D_MODEL)), full((1, D_MODEL)), full((1, D_MODEL))],
        out_specs=[tok(D_MODEL), tok(HW), tok(HW), tok(HW), tok(HW), stat, stat,
                   full((D_MODEL, D_MODEL)), full((8, D_MODEL))],
        out_shape=[jax.ShapeDtypeStruct((seq, D_MODEL), F32)] + [jax.ShapeDtypeStruct((seq, HW), BF16)] * 4
        + [jax.ShapeDtypeStruct((HEADS, seq), F32)] * 2
        + [jax.ShapeDtypeStruct((D_MODEL, D_MODEL), F32), jax.ShapeDtypeStruct((8, D_MODEL), F32)],
        compiler_params=_cparams(dimension_semantics=("arbitrary",)),
    )(oa, ob, ga, gb, x, tgt, w_out, ln_g, ln_b)


def _bwd_mid(dq_m, dkn, dv, dkpe, dqb, dkb, dvb, dga, dgb, cq, ckv, qn, kvn, w_uq_r, w_ukv_r, qg, kvg, tabs, bt):
    seq = cq.shape[0]

    def body(dqm_ref, dkn_ref, dv_ref, dkpe_ref, dqb_ref, dkb_ref, dvb_ref, dga_ref, dgb_ref,
             cq_ref, ckv_ref, qn_ref, kvn_ref, wuq_ref, wukv_ref, qg_ref, kvg_ref, tab_ref,
             dh_ref, guq_ref, gukv_ref, small_ref):
        i = pl.program_id(0)

        @pl.when(i == 0)
        def _():
            guq_ref[...] = jnp.zeros(guq_ref.shape, F32)
            gukv_ref[...] = jnp.zeros(gukv_ref.shape, F32)
            small_ref[...] = jnp.zeros(small_ref.shape, F32)

        m_tabs = (tab_ref[0], tab_ref[1], tab_ref[2])
        d_tabs = (tab_ref[3], tab_ref[4], tab_ref[5])

        def rms_bwd(c, dn, gain):
            r = lax.rsqrt(jnp.mean(c * c, axis=1, keepdims=True) + RMS_EPS)
            u = dn * gain
            dc = r * u - c * (r * r * r) * jnp.mean(u * c, axis=1, keepdims=True)
            return dc, jnp.sum(dn * c * r, axis=0, keepdims=True)

        dqm = dqm_ref[0].T
        dq = jnp.concatenate(
            [dqm[:, :HW], _rope_wide(_rope_t, dqm[:, HW:], *m_tabs, MLA_ROPE // 2)], axis=1) * MLA_SCALE
        dq = dq.astype(BF16)
        guq_ref[...] += lax.dot_general(qn_ref[...], dq, TN, preferred_element_type=F32)
        dqn = lax.dot_general(dq, wuq_ref[...], NT, preferred_element_type=F32)
        dcq, gq = rms_bwd(cq_ref[...], dqn, qg_ref[...])
        small_ref[0:1, :] += gq

        dkv = jnp.concatenate([dkn_ref[...], dv_ref[...]], axis=1)
        gukv_ref[...] += lax.dot_general(kvn_ref[...], dkv, TN, preferred_element_type=F32)
        dkvn = lax.dot_general(dkv, wukv_ref[...], NT, preferred_element_type=F32)
        dckv, gkv = rms_bwd(ckv_ref[...], dkvn, kvg_ref[...])
        small_ref[1:2, :KV_RANK] += gkv

        dh_ref[:, C_CQ:C_CKV] = dcq.astype(BF16)
        dh_ref[:, C_CKV:C_KR] = dckv.astype(BF16)
        dh_ref[:, C_KR:C_GA] = _rope_t(dkpe_ref[...], *m_tabs, MLA_ROPE // 2).astype(BF16)
        dh_ref[:, C_GA:C_QB] = dga_ref[...]
        dh_ref[:, C_QB:C_KB] = (_rope_wide(_rope_t, dqb_ref[0].T, *d_tabs, DIL_ROT // 2) * DIL_SCALE).astype(BF16)
        dh_ref[:, C_KB:C_VB] = _rope_wide(_rope_t, dkb_ref[...], *d_tabs, DIL_ROT // 2).astype(BF16)
        dh_ref[:, C_VB:C_GB] = dvb_ref[...]
        dh_ref[:, C_GB:C_END] = dgb_ref[...]

    def tok(width):
        return pl.BlockSpec((bt, width), lambda i: (i, 0))

    def tok_t(height):
        return pl.BlockSpec((1, height, bt), lambda i: (i, 0, 0))

    def full(shape):
        return pl.BlockSpec(shape, lambda i: (0,) * len(shape))

    return _pcall(
        body, name="bwd_mid", grid=(seq // bt,),
        in_specs=[tok_t(QW), tok(HW), tok(HW), tok(LANES), tok_t(HW), tok(HW), tok(HW), tok(HW), tok(HW),
                  tok(Q_RANK), tok(KV_RANK), tok(Q_RANK), tok(KV_RANK),
                  full(w_uq_r.shape), full(w_ukv_r.shape), full((1, Q_RANK)), full((1, KV_RANK)),
                  pl.BlockSpec((6, bt, LANES), lambda i: (0, i, 0))],
        out_specs=[tok(C_END), full(w_uq_r.shape), full(w_ukv_r.shape), full((8, Q_RANK))],
        out_shape=[jax.ShapeDtypeStruct((seq, C_END), BF16), jax.ShapeDtypeStruct(w_uq_r.shape, F32),
                   jax.ShapeDtypeStruct(w_ukv_r.shape, F32), jax.ShapeDtypeStruct((8, Q_RANK), F32)],
        compiler_params=_cparams(dimension_semantics=("arbitrary",)),
    )(dq_m, dkn, dv, dkpe, dqb, dkb, dvb, dga, dgb, cq, ckv, qn, kvn, w_uq_r, w_ukv_r, qg, kvg, tabs)


def _grad_x(dz, dh, w_in_r, bt):
    seq = dz.shape[0]

    def body(dz_ref, dh_ref, w_ref, gx_ref):
        gx_ref[...] = ALPHA * dz_ref[...] + lax.dot_general(
            dh_ref[...], w_ref[...], NT, preferred_element_type=F32)

    return _pcall(
        body, name="grad_x", grid=(seq // bt,),
        in_specs=[pl.BlockSpec((bt, D_MODEL), lambda i: (i, 0)), pl.BlockSpec((bt, C_END), lambda i: (i, 0)),
                  pl.BlockSpec(w_in_r.shape, lambda i: (0, 0))],
        out_specs=pl.BlockSpec((bt, D_MODEL), lambda i: (i, 0)),
        out_shape=jax.ShapeDtypeStruct((seq, D_MODEL), F32),
        compiler_params=_cparams(dimension_semantics=("arbitrary",)),
    )(dz, dh, w_in_r)


def _grad_w_in(x, dh, bt):
    seq = x.shape[0]

    def body(x_ref, dh_ref, gw_ref):
        @pl.when(pl.program_id(0) == 0)
        def _():
            gw_ref[...] = jnp.zeros(gw_ref.shape, F32)

        gw_ref[...] += lax.dot_general(x_ref[...].astype(BF16), dh_ref[...], TN, preferred_element_type=F32)

    return _pcall(
        body, name="grad_w_in", grid=(seq // bt,),
        in_specs=[pl.BlockSpec((bt, D_MODEL), lambda i: (i, 0)), pl.BlockSpec((bt, C_END), lambda i: (i, 0))],
        out_specs=pl.BlockSpec((D_MODEL, C_END), lambda i: (0, 0)),
        out_shape=jax.ShapeDtypeStruct((D_MODEL, C_END), F32),
        compiler_params=_cparams(dimension_semantics=("arbitrary",)),
    )(x, dh)


def _reorder_weights(w_in, w_uq, w_ukv):
    w_in_r = jnp.concatenate(
        [w_in[:, :640], jnp.tile(w_in[:, 640:672], (1, 4)), w_in[:, 672:]], axis=1)
    uq = w_uq.reshape(Q_RANK, HEADS, MLA_NOPE + MLA_ROPE)
    w_uq_r = jnp.concatenate(
        [uq[:, :, :MLA_NOPE].reshape(Q_RANK, HW), uq[:, :, MLA_NOPE:].reshape(Q_RANK, HEADS * MLA_ROPE)], axis=1)
    ukv = w_ukv.reshape(KV_RANK, HEADS, MLA_NOPE + MLA_V)
    w_ukv_r = jnp.concatenate(
        [ukv[:, :, :MLA_NOPE].reshape(KV_RANK, HW), ukv[:, :, MLA_NOPE:].reshape(KV_RANK, HW)], axis=1)
    return w_in_r, w_uq_r, w_ukv_r


def _restore_grads(g_in_r, g_uq_r, g_ukv_r):
    g_in = jnp.concatenate(
        [g_in_r[:, :640], g_in_r[:, 640:768].reshape(D_MODEL, 4, MLA_ROPE).sum(axis=1), g_in_r[:, 768:]], axis=1)
    g_uq = jnp.concatenate(
        [g_uq_r[:, :HW].reshape(Q_RANK, HEADS, MLA_NOPE), g_uq_r[:, HW:].reshape(Q_RANK, HEADS, MLA_ROPE)],
        axis=2).reshape(Q_RANK, HEADS * (MLA_NOPE + MLA_ROPE))
    g_ukv = jnp.concatenate(
        [g_ukv_r[:, :HW].reshape(KV_RANK, HEADS, MLA_NOPE), g_ukv_r[:, HW:].reshape(KV_RANK, HEADS, MLA_V)],
        axis=2).reshape(KV_RANK, HEADS * (MLA_NOPE + MLA_V))
    return g_in, g_uq, g_ukv


def _local_step(x, tgt, w_in, w_uq, w_ukv, w_out, q_norm_g, kv_norm_g, ln_g, ln_b, blk=256):
    seq = x.shape[0]
    tabs = jnp.asarray(_rope_tables(seq))
    w_in_r, w_uq_r, w_ukv_r = _reorder_weights(w_in, w_uq, w_ukv)
    qg, kvg = q_norm_g.reshape(1, -1), kv_norm_g.reshape(1, -1)

    (cq, ckv, qn, kvn, qcat, kn, kpe, v, ga, gb, qb, kb, vb, knt, kpet, vt, kbt, vbt) = _fwd_proj(
        x, w_in_r, w_uq_r, w_ukv_r, qg, kvg, tabs, blk)

    nq = seq // blk
    span_d = DIL_CONFIGS[-1][0] // blk
    bias_m, bias_d = jnp.asarray(_mla_bias_t(blk)), jnp.asarray(_dil_bias_t(blk))
    oa, lse_a = _attn_fwd("mla_fwd", qcat, kn, kpe, vt, bias_m, _steps(nq, nq, False, True), blk)
    ob, lse_b = _attn_fwd("dil_fwd", qb, kb, None, vbt, bias_d, _steps(nq, span_d, False, False), blk)

    dz, doa, dob, dga, dgb, dst_a, dst_b, g_out, small1 = _out_ln(
        oa, ob, ga, gb, x, tgt, w_out, ln_g.reshape(1, -1), ln_b.reshape(1, -1), blk)

    dq_m, dkn, dkpe, dv = _attn_bwd(
        "mla_bwd", qcat, kn, kpe, v, knt, kpet, bias_m, doa, lse_a, dst_a, _steps(nq, nq, True, True), blk)
    dqb, dkb, dvb = _attn_bwd(
        "dil_bwd", qb, kb, None, vb, kbt, None, bias_d, dob, lse_b, dst_b, _steps(nq, span_d, True, False), blk)

    dh, g_uq_r, g_ukv_r, small2 = _bwd_mid(
        dq_m, dkn, dv, dkpe, dqb, dkb, dvb, dga, dgb, cq, ckv, qn, kvn, w_uq_r, w_ukv_r, qg, kvg, tabs, blk)
    grad_x = _grad_x(dz, dh, w_in_r, blk)
    g_in_r = _grad_w_in(x, dh, blk)
    g_in, g_uq, g_ukv = _restore_grads(g_in_r, g_uq_r, g_ukv_r)

    loss = (0.5 / D_MODEL) * jnp.sum(small1[2])
    return loss, grad_x, g_in, g_uq, g_ukv, g_out, small2[0], small2[1, :KV_RANK], small1[0], small1[1]


MESH_ID = pl.DeviceIdType.MESH
SHARD_SHAPES = ((D_MODEL, IN_WIDTH // N_DEV), (Q_RANK, 768 // N_DEV), (KV_RANK, 1024 // N_DEV), (D_MODEL // N_DEV, D_MODEL))
ADAM_ROWS = (32, 128, 128, 16)


def _me():
    x, y, c = lax.axis_index("x"), lax.axis_index("y"), lax.axis_index("c")
    return x, y, c, 4 * x + 2 * y + c


def _peer(k):
    x, y, c, _ = _me()
    px = 1 - x if (k >> 2) & 1 else x
    py = 1 - y if (k >> 1) & 1 else y
    pc = 1 - c if k & 1 else c
    return (px, py, pc), 4 * px + 2 * py + pc


def _all_gather_weights(shards):
    def body(*refs):
        ins, outs = refs[:4], refs[4:8]
        send_sems, recv_sems = refs[8:]
        me = _me()[3]
        for t in range(4):
            outs[t][me] = ins[t][...].astype(BF16)
        sends = []
        for t in range(4):
            for k in range(1, N_DEV):
                peer, _ = _peer(k)
                cp = pltpu.make_async_remote_copy(
                    src_ref=outs[t].at[me], dst_ref=outs[t].at[me], send_sem=send_sems.at[t, k - 1],
                    recv_sem=recv_sems.at[t, k - 1], device_id=peer, device_id_type=MESH_ID)
                cp.start()
                sends.append(cp)
        for t in range(4):
            for k in range(1, N_DEV):
                peer, pidx = _peer(k)
                pltpu.make_async_remote_copy(
                    src_ref=outs[t].at[pidx], dst_ref=outs[t].at[pidx], send_sem=send_sems.at[t, k - 1],
                    recv_sem=recv_sems.at[t, k - 1], device_id=peer, device_id_type=MESH_ID).wait_recv()
        for cp in sends:
            cp.wait_send()

    vmem = pl.BlockSpec(memory_space=pltpu.VMEM)
    return _pcall(
        body, name="gather_weights",
        in_specs=[vmem] * 4, out_specs=[vmem] * 4,
        out_shape=[jax.ShapeDtypeStruct((N_DEV,) + s, BF16) for s in SHARD_SHAPES],
        scratch_shapes=[pltpu.SemaphoreType.DMA((4, N_DEV - 1)), pltpu.SemaphoreType.DMA((4, N_DEV - 1))],
        compiler_params=_cparams(),
    )(*shards)


def _adamw(w, g, m, v):
    m = ADAM_B1 * m + (1.0 - ADAM_B1) * g
    v = ADAM_B2 * v + (1.0 - ADAM_B2) * jnp.square(g)
    m_hat = m / (1.0 - ADAM_B1 ** ADAM_STEP)
    v_hat = v / (1.0 - ADAM_B2 ** ADAM_STEP)
    delta = -ADAM_LR * (m_hat / (jnp.sqrt(v_hat) + ADAM_EPS) + ADAM_WD * w)
    return delta, m, v


def _reduce_adam(grads3, small_part, wmv, small_wmv):
    def body(*refs):
        g3 = refs[0:4]
        sp_ref = refs[4]
        wmv_refs = [refs[5 + 3 * t:8 + 3 * t] for t in range(4)]
        swmv_ref = refs[17]
        out_refs = [refs[18 + 4 * t:22 + 4 * t] for t in range(4)]
        sout_ref = refs[34]
        recv = refs[35:39]
        rsmall = refs[39]
        send_sems, recv_sems, local_sems = refs[40:43]
        me = _me()[3]

        rsmall[0] = sp_ref[...]
        local = []
        for t in range(4):
            cp = pltpu.make_async_copy(g3[t].at[me], recv[t].at[0], local_sems.at[t])
            cp.start()
            local.append(cp)
        sends = []
        for k in range(1, N_DEV):
            peer, pidx = _peer(k)
            for t in range(5):
                src = rsmall.at[0] if t == 4 else g3[t].at[pidx]
                dst = rsmall.at[k] if t == 4 else recv[t].at[k]
                cp = pltpu.make_async_remote_copy(
                    src_ref=src, dst_ref=dst, send_sem=send_sems.at[t, k - 1], recv_sem=recv_sems.at[t, k - 1],
                    device_id=peer, device_id_type=MESH_ID)
                cp.start()
                sends.append(cp)
        for cp in local:
            cp.wait()
        for cp in sends:
            cp.wait_recv()

        tot = rsmall[me]
        for d in range(1, N_DEV):
            tot = tot + rsmall[jnp.bitwise_xor(me, d)]
        delta, m, v = _adamw(swmv_ref[0], tot, swmv_ref[1], swmv_ref[2])
        sout_ref[0], sout_ref[1], sout_ref[2], sout_ref[3] = tot, delta, m, v

        for t in range(4):
            rows = ADAM_ROWS[t]
            w_ref, m_ref, v_ref = wmv_refs[t]
            g_out, d_out, m_out, v_out = out_refs[t]

            def step(i, carry, t=t, rows=rows, w_ref=w_ref, m_ref=m_ref, v_ref=v_ref,
                     g_out=g_out, d_out=d_out, m_out=m_out, v_out=v_out):
                r = pl.ds(pl.multiple_of(i * rows, rows), rows)
                g = recv[t][0, r, :]
                for k in range(1, N_DEV):
                    g = g + recv[t][k, r, :]
                delta, m, v = _adamw(w_ref[r, :], g, m_ref[r, :], v_ref[r, :])
                g_out[r, :], d_out[r, :], m_out[r, :], v_out[r, :] = g, delta, m, v
                return carry

            lax.fori_loop(0, SHARD_SHAPES[t][0] // rows, step, 0)

        for cp in sends:
            cp.wait_send()

    vmem = pl.BlockSpec(memory_space=pltpu.VMEM)
    hbm = pl.BlockSpec(memory_space=pl.ANY)
    flat_wmv = [a for trio in wmv for a in trio]
    return _pcall(
        body, name="reduce_adam",
        in_specs=[hbm] * 4 + [vmem] * 14,
        out_specs=[vmem] * 17,
        out_shape=[jax.ShapeDtypeStruct(s, F32) for s in SHARD_SHAPES for _ in range(4)]
        + [jax.ShapeDtypeStruct((4, 8, D_MODEL), F32)],
        scratch_shapes=[pltpu.VMEM((N_DEV,) + s, F32) for s in SHARD_SHAPES]
        + [pltpu.VMEM((N_DEV, 8, D_MODEL), F32),
           pltpu.SemaphoreType.DMA((5, N_DEV - 1)), pltpu.SemaphoreType.DMA((5, N_DEV - 1)),
           pltpu.SemaphoreType.DMA((4,))],
        compiler_params=_cparams(),
    )(*grads3, small_part, *flat_wmv, small_wmv)


def _small_rows(ln_g, ln_b, q_norm_g, kv_norm_g):
    pad = lambda a: jnp.pad(a, (0, D_MODEL - a.shape[0]))
    return jnp.pad(jnp.stack([ln_g, ln_b, pad(q_norm_g), pad(kv_norm_g)]), ((0, 4), (0, 0)))


def kernel(x, w_in, q_norm_g, kv_norm_g, w_uq, w_ukv, w_out, ln_g, ln_b, loss_target, m_w_in, m_q_norm_g, m_kv_norm_g, m_w_uq, m_w_ukv, m_w_out, m_ln_g, m_ln_b, v_w_in, v_q_norm_g, v_kv_norm_g, v_w_uq, v_w_ukv, v_w_out, v_ln_g, v_ln_b):
    a_in, a_uq, a_ukv, a_out = _all_gather_weights([w_in, w_uq, w_ukv, w_out])
    full_in = a_in.transpose(1, 0, 2).reshape(D_MODEL, IN_WIDTH)
    full_uq = a_uq.transpose(1, 0, 2).reshape(Q_RANK, 768)
    full_ukv = a_ukv.transpose(1, 0, 2).reshape(KV_RANK, 1024)
    full_out = a_out.reshape(D_MODEL, D_MODEL)

    loss_l, grad_x, g_in, g_uq, g_ukv, g_out, g_qg, g_kvg, g_lng, g_lnb = _local_step(
        x[0], loss_target[0], full_in, full_uq, full_ukv, full_out, q_norm_g, kv_norm_g, ln_g, ln_b)
    loss = lax.psum(loss_l, ("x", "y", "c"))

    grads3 = [g_in.reshape(D_MODEL, N_DEV, -1).transpose(1, 0, 2), g_uq.reshape(Q_RANK, N_DEV, -1).transpose(1, 0, 2),
              g_ukv.reshape(KV_RANK, N_DEV, -1).transpose(1, 0, 2), g_out.reshape(N_DEV, D_MODEL // N_DEV, D_MODEL)]
    small_part = _small_rows(g_lng, g_lnb, g_qg, g_kvg)
    small_wmv = jnp.stack([_small_rows(ln_g, ln_b, q_norm_g, kv_norm_g),
                           _small_rows(m_ln_g, m_ln_b, m_q_norm_g, m_kv_norm_g),
                           _small_rows(v_ln_g, v_ln_b, v_q_norm_g, v_kv_norm_g)])
    wmv = [(w_in, m_w_in, v_w_in), (w_uq, m_w_uq, v_w_uq), (w_ukv, m_w_ukv, v_w_ukv), (w_out, m_w_out, v_w_out)]
    res = _reduce_adam(grads3, small_part, wmv, small_wmv)
    big = [res[4 * t:4 * t + 4] for t in range(4)]
    small = res[16]

    def group(kind):
        s = small[kind]
        return (big[0][kind], s[2, :Q_RANK], s[3, :KV_RANK], big[1][kind], big[2][kind], big[3][kind], s[0], s[1])

    return (loss, grad_x[None], *group(0), *group(1), *group(2), *group(3))
```

```python
import numpy as np
import jax
import jax.numpy as jnp
from jax import lax
from jax.experimental import pallas as pl
from jax.experimental.pallas import tpu as pltpu

F32 = jnp.float32
BF16 = jnp.bfloat16

D_MODEL = 1024
ROPE_THETA = 500000.0
NEG = -1e30
RMS_EPS = 1e-6
LN_EPS = 1e-5
HEADS = 8
MLA_NOPE = 64
MLA_ROPE = 32
MLA_V = 64
Q_RANK = 384
KV_RANK = 256
DIL_HEAD = 64
DIL_ROT = 16
DIL_CONFIGS = ((128, 1), (512, 4), (2048, 16))
HW = HEADS * 64
QW = HW + HEADS * MLA_ROPE
IN_SPLITS = (Q_RANK, KV_RANK, MLA_ROPE, HW, HW, HW, HW, HW)
IN_WIDTH = sum(IN_SPLITS)
ALPHA = 2.0 ** 0.25
MLA_SCALE = (MLA_NOPE + MLA_ROPE) ** -0.5
DIL_SCALE = DIL_HEAD ** -0.5

ADAM_LR = 0.001
ADAM_B1 = 0.9
ADAM_B2 = 0.999
ADAM_EPS = 1e-08
ADAM_WD = 0.01
ADAM_STEP = 10

N_DEV = 8
LANES = 128
VMEM_LIMIT = 56 * 1024 * 1024
BLOCK_TOKENS = 256
BLOCK_MLA = 512
BLOCK_DIL = 512

C_CQ, C_CKV, C_KR, C_GA, C_QB, C_KB, C_VB, C_GB, C_END = 0, 384, 640, 768, 1280, 1792, 2304, 2816, 3328

NT = (((1,), (1,)), ((), ()))
TN = (((0,), (0,)), ((), ()))


def _pcall(body, **kw):
    return pl.pallas_call(body, **kw)


def _cparams(**kw):
    return pltpu.CompilerParams(vmem_limit_bytes=VMEM_LIMIT, **kw)


def _rope_tables(seq):
    def tabs(dim, period):
        half = dim // 2
        inv = np.float32(ROPE_THETA) ** (-np.arange(0, dim, 2, dtype=np.float32) / np.float32(dim))
        ang = np.arange(seq, dtype=np.float32)[:, None] * inv.astype(np.float32)[None, :]
        cos, sin = np.cos(ang).astype(np.float32), np.sin(ang).astype(np.float32)
        j = np.arange(LANES) % period
        f = j % half
        c = np.where(j < dim, cos[:, f], np.float32(1.0))
        s1 = np.where(j < half, -sin[:, f], np.float32(0.0))
        s2 = np.where((j >= half) & (j < dim), sin[:, f], np.float32(0.0))
        return [c, s1, s2]
    return np.stack(tabs(MLA_ROPE, MLA_ROPE) + tabs(DIL_ROT, DIL_HEAD)).astype(np.float32)


def _rope(t, c, s1, s2, half):
    return t * c + pltpu.roll(t, LANES - half, 1) * s1 + pltpu.roll(t, half, 1) * s2


def _rope_t(d, c, s1, s2, half):
    return d * c + pltpu.roll(d * s1, half, 1) + pltpu.roll(d * s2, LANES - half, 1)


def _rope_wide(fn, t, c, s1, s2, half):
    return jnp.concatenate(
        [fn(t[:, i:i + LANES], c, s1, s2, half) for i in range(0, t.shape[1], LANES)], axis=1)


def _mla_bias_t(blk):
    a = np.arange(blk)
    causal = np.where(a[:, None] <= a[None, :], 0.0, NEG)
    return np.stack([np.zeros((blk, blk)), causal]).astype(np.float32)


def _dil_bias_t(blk):
    span = DIL_CONFIGS[-1][0] // blk
    a = np.arange(blk)
    out = []
    for off in range(span + 1):
        delta = blk * off + a[None, :] - a[:, None]
        mult = np.zeros((blk, blk))
        for window, dil in DIL_CONFIGS:
            mult += (delta >= 0) & (delta % dil == 0) & (delta <= window)
        out.append(np.where(mult > 0, np.log(np.maximum(mult, 1.0)), NEG))
    return np.stack(out).astype(np.float32)


def _steps(nq, span, by_key, diag_only_bias):
    rows = []
    if by_key:
        for ki in range(nq):
            hi = min(nq - 1, ki + span)
            for qi in range(ki, hi + 1):
                rows.append((qi, ki, int(qi == ki), int(qi == hi)))
    else:
        for qi in range(nq):
            lo = max(0, qi - span)
            for ki in range(lo, qi + 1):
                rows.append((qi, ki, int(ki == lo), int(ki == qi)))
    arr = np.array(rows, dtype=np.int32)
    off = arr[:, 0] - arr[:, 1]
    bias_idx = (off == 0).astype(np.int32) if diag_only_bias else off.astype(np.int32)
    return [jnp.asarray(v) for v in (arr[:, 0], arr[:, 1], bias_idx, arr[:, 2], arr[:, 3])]


def _fwd_proj(x, w_in_r, w_uq_r, w_ukv_r, qg, kvg, tabs, bt):
    seq = x.shape[0]

    def body(x_ref, win_ref, wuq_ref, wukv_ref, qg_ref, kvg_ref, tab_ref,
             cq_ref, ckv_ref, qn_ref, kvn_ref, qcat_ref, kn_ref, kpe_ref, v_ref,
             ga_ref, gb_ref, qb_ref, kb_ref, vb_ref, knt_ref, kpet_ref, vt_ref, kbt_ref, vbt_ref):
        xb = x_ref[...].astype(BF16)

        def proj(lo, hi):
            return jnp.dot(xb, win_ref[:, lo:hi], preferred_element_type=F32)

        m_tabs = (tab_ref[0], tab_ref[1], tab_ref[2])
        d_tabs = (tab_ref[3], tab_ref[4], tab_ref[5])

        cq = proj(C_CQ, C_CKV)
        cq_ref[...] = cq
        qn = (cq * lax.rsqrt(jnp.mean(cq * cq, axis=1, keepdims=True) + RMS_EPS) * qg_ref[...]).astype(BF16)
        qn_ref[...] = qn
        q = jnp.dot(qn, wuq_ref[...], preferred_element_type=F32)
        qcat_ref[:, :HW] = (q[:, :HW] * MLA_SCALE).astype(BF16)
        qcat_ref[:, HW:] = (_rope_wide(_rope, q[:, HW:], *m_tabs, MLA_ROPE // 2) * MLA_SCALE).astype(BF16)

        ckv = proj(C_CKV, C_KR)
        ckv_ref[...] = ckv
        kvn = (ckv * lax.rsqrt(jnp.mean(ckv * ckv, axis=1, keepdims=True) + RMS_EPS) * kvg_ref[...]).astype(BF16)
        kvn_ref[...] = kvn
        kv = jnp.dot(kvn, wukv_ref[...], preferred_element_type=F32)
        kn_ref[...] = kv[:, :HW].astype(BF16)
        v_ref[...] = kv[:, HW:].astype(BF16)
        knt_ref[...] = kv[:, :HW].T.astype(BF16)
        vt_ref[...] = kv[:, HW:].T.astype(BF16)

        kpe = _rope(proj(C_KR, C_GA), *m_tabs, MLA_ROPE // 2)
        kpe_ref[...] = kpe.astype(BF16)
        kpet_ref[...] = kpe.T[:MLA_ROPE, :].astype(BF16)
        ga_ref[...] = proj(C_GA, C_QB)
        qb_ref[...] = (_rope_wide(_rope, proj(C_QB, C_KB), *d_tabs, DIL_ROT // 2) * DIL_SCALE).astype(BF16)
        kb = _rope_wide(_rope, proj(C_KB, C_VB), *d_tabs, DIL_ROT // 2)
        kb_ref[...] = kb.astype(BF16)
        kbt_ref[...] = kb.T.astype(BF16)
        vb = proj(C_VB, C_GB)
        vb_ref[...] = vb.astype(BF16)
        vbt_ref[...] = vb.T.astype(BF16)
        gb_ref[...] = proj(C_GB, C_END)

    def tok(width):
        return pl.BlockSpec((bt, width), lambda i: (i, 0))

    def tok_t(height):
        return pl.BlockSpec((height, bt), lambda i: (0, i))

    def full(a):
        return pl.BlockSpec(a.shape, lambda i: (0,) * a.ndim)

    outs = [(Q_RANK, F32), (KV_RANK, F32), (Q_RANK, BF16), (KV_RANK, BF16), (QW, BF16), (HW, BF16),
            (LANES, BF16), (HW, BF16), (HW, F32), (HW, F32), (HW, BF16), (HW, BF16), (HW, BF16)]
    outs_t = [HW, MLA_ROPE, HW, HW, HW]
    return _pcall(
        body, name="fwd_proj", grid=(seq // bt,),
        in_specs=[tok(D_MODEL), full(w_in_r), full(w_uq_r), full(w_ukv_r), full(qg), full(kvg),
                  pl.BlockSpec((6, bt, LANES), lambda i: (0, i, 0))],
        out_specs=[tok(w) for w, _ in outs] + [tok_t(h) for h in outs_t],
        out_shape=[jax.ShapeDtypeStruct((seq, w), dt) for w, dt in outs]
        + [jax.ShapeDtypeStruct((h, seq), BF16) for h in outs_t],
        compiler_params=_cparams(dimension_semantics=("arbitrary",)),
    )(x, w_in_r, w_uq_r, w_ukv_r, qg, kvg, tabs)


def _head_masks(lane, h):
    e, g = h % 2, h % 4
    me = (lane >= 64 * e) & (lane < 64 * e + 64)
    mr = (lane >= 32 * g) & (lane < 32 * g + 32)
    return me, mr


def _masked(mask, a):
    return jnp.where(mask, a, jnp.zeros_like(a))


def _attn_fwd(name, q, k, kpe, vt, bias_t, steps, blk):
    seq = q.shape[0]
    mla = kpe is not None
    n_steps = int(steps[0].shape[0])

    def body(qi_r, ki_r, bi_r, fi_r, la_r, *refs):
        if mla:
            q_ref, k_ref, kpe_ref, vt_ref, b_ref, o_ref, lse_ref, m_sc, l_sc, acc_sc = refs
        else:
            q_ref, k_ref, vt_ref, b_ref, o_ref, lse_ref, m_sc, l_sc, acc_sc = refs
        t = pl.program_id(0)

        @pl.when(fi_r[t] == 1)
        def _():
            m_sc[...] = jnp.full(m_sc.shape, NEG, F32)
            l_sc[...] = jnp.zeros(l_sc.shape, F32)
            acc_sc[...] = jnp.zeros(acc_sc.shape, F32)

        lane = lax.broadcasted_iota(jnp.int32, (1, LANES), 1)
        bias = b_ref[0]
        for j in range(HEADS // 2):
            cols = slice(LANES * j, LANES * (j + 1))
            qc = q_ref[:, cols]
            if mla:
                qc = jnp.concatenate([qc, q_ref[:, HW + LANES * (j // 2):HW + LANES * (j // 2 + 1)]], axis=1)
            kj = k_ref[:, cols]
            for h in (2 * j, 2 * j + 1):
                me, mr = _head_masks(lane, h)
                ke = _masked(me, kj)
                if mla:
                    ke = jnp.concatenate([ke, _masked(mr, kpe_ref[...])], axis=1)
                st = lax.dot_general(ke, qc, NT, preferred_element_type=F32) + bias
                hrow = slice(h, h + 1)
                m_prev = m_sc[hrow, :]
                m_new = jnp.maximum(m_prev, jnp.max(st, axis=0, keepdims=True))
                alpha = jnp.exp(m_prev - m_new)
                pt = jnp.exp(st - m_new)
                l_sc[hrow, :] = alpha * l_sc[hrow, :] + jnp.sum(pt, axis=0, keepdims=True)
                m_sc[hrow, :] = m_new
                rows = slice(64 * h, 64 * h + 64)
                acc_sc[rows, :] = alpha * acc_sc[rows, :] + jnp.dot(
                    vt_ref[rows, :], pt.astype(BF16), preferred_element_type=F32)

        @pl.when(la_r[t] == 1)
        def _():
            for h in range(HEADS):
                rows = slice(64 * h, 64 * h + 64)
                acc_sc[rows, :] = acc_sc[rows, :] / l_sc[h:h + 1, :]
            o_ref[...] = acc_sc[...].T
            lse_ref[...] = m_sc[...] + jnp.log(l_sc[...])

    qmap = lambda t, qi, ki, bi, fi, la: (qi[t], 0)
    kmap = lambda t, qi, ki, bi, fi, la: (ki[t], 0)
    in_specs = [pl.BlockSpec((blk, q.shape[1]), qmap), pl.BlockSpec((blk, HW), kmap)]
    args = [q, k]
    if mla:
        in_specs.append(pl.BlockSpec((blk, LANES), kmap))
        args.append(kpe)
    in_specs += [pl.BlockSpec((HW, blk), lambda t, qi, ki, bi, fi, la: (0, ki[t])),
                 pl.BlockSpec((1, blk, blk), lambda t, qi, ki, bi, fi, la: (bi[t], 0, 0))]
    args += [vt, bias_t]
    return _pcall(
        body, name=name,
        grid_spec=pltpu.PrefetchScalarGridSpec(
            num_scalar_prefetch=5, grid=(n_steps,), in_specs=in_specs,
            out_specs=[pl.BlockSpec((blk, HW), qmap),
                       pl.BlockSpec((HEADS, blk), lambda t, qi, ki, bi, fi, la: (0, qi[t]))],
            scratch_shapes=[pltpu.VMEM((HEADS, blk), F32), pltpu.VMEM((HEADS, blk), F32),
                            pltpu.VMEM((HW, blk), F32)]),
        out_shape=[jax.ShapeDtypeStruct((seq, HW), F32), jax.ShapeDtypeStruct((HEADS, seq), F32)],
        compiler_params=_cparams(dimension_semantics=("arbitrary",)),
    )(*steps, *args)


def _attn_bwd(name, q, k, kpe, v, kt, kpet, bias_t, do, lse, dstat, steps, blk):
    seq = q.shape[0]
    mla = kpe is not None
    qw = q.shape[1]
    n_steps = int(steps[0].shape[0])
    dk_dtype = BF16 if mla else F32

    def body(qi_r, ki_r, bi_r, fi_r, la_r, *refs):
        if mla:
            (q_ref, k_ref, kpe_ref, v_ref, kt_ref, kpet_ref, b_ref, do_ref, lse_ref, d_ref,
             dq_ref, dk_ref, dkpe_ref, dv_ref, dk_sc, dkpe_sc, dv_sc) = refs
        else:
            (q_ref, k_ref, v_ref, kt_ref, b_ref, do_ref, lse_ref, d_ref,
             dq_ref, dk_ref, dv_ref, dk_sc, dv_sc) = refs
        t = pl.program_id(0)

        @pl.when(t == 0)
        def _():
            dq_ref[...] = jnp.zeros(dq_ref.shape, F32)

        @pl.when(fi_r[t] == 1)
        def _():
            dk_sc[...] = jnp.zeros(dk_sc.shape, F32)
            dv_sc[...] = jnp.zeros(dv_sc.shape, F32)
            if mla:
                dkpe_sc[...] = jnp.zeros(dkpe_sc.shape, F32)

        qi = qi_r[t]
        lane = lax.broadcasted_iota(jnp.int32, (1, LANES), 1)
        bias = b_ref[0]
        for j in range(HEADS // 2):
            cols = slice(LANES * j, LANES * (j + 1))
            qc = q_ref[:, cols]
            if mla:
                qc = jnp.concatenate([qc, q_ref[:, HW + LANES * (j // 2):HW + LANES * (j // 2 + 1)]], axis=1)
            kj = k_ref[:, cols]
            vj = v_ref[:, cols]
            doj = do_ref[:, cols]
            for h in (2 * j, 2 * j + 1):
                me, mr = _head_masks(lane, h)
                ke = _masked(me, kj)
                if mla:
                    ke = jnp.concatenate([ke, _masked(mr, kpe_ref[...])], axis=1)
                st = lax.dot_general(ke, qc, NT, preferred_element_type=F32) + bias
                pt = jnp.exp(st - lse_ref[h:h + 1, :])
                dpt = lax.dot_general(_masked(me, vj), doj, NT, preferred_element_type=F32)
                dst = (pt * (dpt - d_ref[h:h + 1, :])).astype(BF16)
                dv_sc[:, cols] += jnp.dot(pt.astype(BF16), _masked(me, doj), preferred_element_type=F32)
                dkc = jnp.dot(dst, qc, preferred_element_type=F32)
                dk_sc[:, cols] += jnp.where(me, dkc[:, :LANES], 0.0)
                ktl = kt_ref[64 * h:64 * h + 64, :]
                if mla:
                    dkpe_sc[...] += jnp.where(mr, dkc[:, LANES:], 0.0)
                    ktl = jnp.concatenate([ktl, kpet_ref[...]], axis=0)
                dqc = jnp.dot(ktl, dst, preferred_element_type=F32)
                dq_ref[qi, 64 * h:64 * h + 64, :] += dqc[:64]
                if mla:
                    dq_ref[qi, HW + MLA_ROPE * h:HW + MLA_ROPE * (h + 1), :] += dqc[64:]

        @pl.when(la_r[t] == 1)
        def _():
            dk_ref[...] = dk_sc[...].astype(dk_ref.dtype)
            dv_ref[...] = dv_sc[...].astype(dv_ref.dtype)
            if mla:
                dkpe_ref[...] = dkpe_sc[...]

    qmap = lambda t, qi, ki, bi, fi, la: (qi[t], 0)
    kmap = lambda t, qi, ki, bi, fi, la: (ki[t], 0)
    qmap_t = lambda t, qi, ki, bi, fi, la: (0, qi[t])
    kmap_t = lambda t, qi, ki, bi, fi, la: (0, ki[t])
    in_specs = [pl.BlockSpec((blk, qw), qmap), pl.BlockSpec((blk, HW), kmap)]
    args = [q, k]
    if mla:
        in_specs.append(pl.BlockSpec((blk, LANES), kmap))
        args.append(kpe)
    in_specs += [pl.BlockSpec((blk, HW), kmap), pl.BlockSpec((HW, blk), kmap_t)]
    args += [v, kt]
    if mla:
        in_specs.append(pl.BlockSpec((MLA_ROPE, blk), kmap_t))
        args.append(kpet)
    in_specs += [pl.BlockSpec((1, blk, blk), lambda t, qi, ki, bi, fi, la: (bi[t], 0, 0)),
                 pl.BlockSpec((blk, HW), qmap), pl.BlockSpec((HEADS, blk), qmap_t), pl.BlockSpec((HEADS, blk), qmap_t)]
    args += [bias_t, do, lse, dstat]
    dq_shape = (seq // blk, qw, blk)
    out_specs = [pl.BlockSpec(dq_shape, lambda t, qi, ki, bi, fi, la: (0, 0, 0)), pl.BlockSpec((blk, HW), kmap)]
    out_shape = [jax.ShapeDtypeStruct(dq_shape, F32), jax.ShapeDtypeStruct((seq, HW), dk_dtype)]
    scratch = [pltpu.VMEM((blk, HW), F32)]
    if mla:
        out_specs.append(pl.BlockSpec((blk, LANES), kmap))
        out_shape.append(jax.ShapeDtypeStruct((seq, LANES), F32))
        scratch.append(pltpu.VMEM((blk, LANES), F32))
    out_specs.append(pl.BlockSpec((blk, HW), kmap))
    out_shape.append(jax.ShapeDtypeStruct((seq, HW), BF16))
    scratch.append(pltpu.VMEM((blk, HW), F32))
    return _pcall(
        body, name=name,
        grid_spec=pltpu.PrefetchScalarGridSpec(
            num_scalar_prefetch=5, grid=(n_steps,), in_specs=in_specs, out_specs=out_specs,
            scratch_shapes=scratch),
        out_shape=out_shape,
        compiler_params=_cparams(dimension_semantics=("arbitrary",)),
    )(*steps, *args)


def _out_ln(oa, ob, ga, gb, x, tgt, w_out, ln_g, ln_b, bt):
    seq = x.shape[0]

    def body(oa_ref, ob_ref, ga_ref, gb_ref, x_ref, tgt_ref, w_ref, g_ref, b_ref,
             dz_ref, doa_ref, dob_ref, dga_ref, dgb_ref, da_ref, db_ref, gw_ref, small_ref):
        i = pl.program_id(0)

        @pl.when(i == 0)
        def _():
            gw_ref[...] = jnp.zeros(gw_ref.shape, F32)
            small_ref[...] = jnp.zeros(small_ref.shape, F32)

        def gate(g):
            sig = 1.0 / (1.0 + jnp.exp(-g))
            return g * sig, sig * (1.0 + g * (1.0 - sig))

        o_a, o_b = oa_ref[...], ob_ref[...]
        g_a, g_b = ga_ref[...], gb_ref[...]
        sa, dsa = gate(g_a)
        sb, dsb = gate(g_b)
        mix = jnp.concatenate([o_a * sa, o_b * sb], axis=1).astype(BF16)
        z = ALPHA * x_ref[...] + jnp.dot(mix, w_ref[...], preferred_element_type=F32)
        mu = jnp.mean(z, axis=1, keepdims=True)
        zc = z - mu
        rstd = lax.rsqrt(jnp.mean(zc * zc, axis=1, keepdims=True) + LN_EPS)
        xhat = zc * rstd
        gam = g_ref[...]
        diff = xhat * gam + b_ref[...] - tgt_ref[...]
        dy = diff * (1.0 / D_MODEL)
        small_ref[0:1, :] += jnp.sum(dy * xhat, axis=0, keepdims=True)
        small_ref[1:2, :] += jnp.sum(dy, axis=0, keepdims=True)
        small_ref[2:3, :] += jnp.sum(diff * diff, axis=0, keepdims=True)
        dxh = dy * gam
        dz = rstd * (dxh - jnp.mean(dxh, axis=1, keepdims=True) - xhat * jnp.mean(dxh * xhat, axis=1, keepdims=True))
        dz_ref[...] = dz
        dzb = dz.astype(BF16)
        gw_ref[...] += lax.dot_general(mix, dzb, TN, preferred_element_type=F32)
        dmix = lax.dot_general(dzb, w_ref[...], NT, preferred_element_type=F32)
        doa, dob = dmix[:, :HW] * sa, dmix[:, HW:] * sb
        doa_ref[...] = doa.astype(BF16)
        dob_ref[...] = dob.astype(BF16)
        dga_ref[...] = (dmix[:, :HW] * o_a * dsa).astype(BF16)
        dgb_ref[...] = (dmix[:, HW:] * o_b * dsb).astype(BF16)
        head_of = lax.broadcasted_iota(jnp.int32, (HEADS, HW), 1) // 64
        ind = (head_of == lax.broadcasted_iota(jnp.int32, (HEADS, HW), 0)).astype(F32)
        da_ref[...] = lax.dot_general(ind, doa * o_a, NT, preferred_element_type=F32, precision=lax.Precision.HIGHEST)
        db_ref[...] = lax.dot_general(ind, dob * o_b, NT, preferred_element_type=F32, precision=lax.Precision.HIGHEST)

    def tok(width):
        return pl.BlockSpec((bt, width), lambda i: (i, 0))

    def full(shape):
        return pl.BlockSpec(shape, lambda i: (0,) * len(shape))

    stat = pl.BlockSpec((HEADS, bt), lambda i: (0, i))
    return _pcall(
        body, name="out_ln", grid=(seq // bt,),
        in_specs=[tok(HW), tok(HW), tok(HW), tok(HW), tok(D_MODEL), tok(D_MODEL),
                  full((D_MODEL, D_MODEL)), full((1, D_MODEL)), full((1, D_MODEL))],
        out_specs=[tok(D_MODEL), tok(HW), tok(HW), tok(HW), tok(HW), stat, stat,
                   full((D_MODEL, D_MODEL)), full((8, D_MODEL))],
        out_shape=[jax.ShapeDtypeStruct((seq, D_MODEL), F32)] + [jax.ShapeDtypeStruct((seq, HW), BF16)] * 4
        + [jax.ShapeDtypeStruct((HEADS, seq), F32)] * 2
        + [jax.ShapeDtypeStruct((D_MODEL, D_MODEL), F32), jax.ShapeDtypeStruct((8, D_MODEL), F32)],
        compiler_params=_cparams(dimension_semantics=("arbitrary",)),
    )(oa, ob, ga, gb, x, tgt, w_out, ln_g, ln_b)


def _bwd_mid(dq_m, dkn, dv, dkpe, dqb, dkb, dvb, dga, dgb, cq, ckv, qn, kvn, w_uq_r, w_ukv_r, qg, kvg, tabs, bt):
    seq = cq.shape[0]

    def body(dqm_ref, dkn_ref, dv_ref, dkpe_ref, dqb_ref, dkb_ref, dvb_ref, dga_ref, dgb_ref,
             cq_ref, ckv_ref, qn_ref, kvn_ref, wuq_ref, wukv_ref, qg_ref, kvg_ref, tab_ref,
             dh_ref, guq_ref, gukv_ref, small_ref):
        i = pl.program_id(0)

        @pl.when(i == 0)
        def _():
            guq_ref[...] = jnp.zeros(guq_ref.shape, F32)
            gukv_ref[...] = jnp.zeros(gukv_ref.shape, F32)
            small_ref[...] = jnp.zeros(small_ref.shape, F32)

        m_tabs = (tab_ref[0], tab_ref[1], tab_ref[2])
        d_tabs = (tab_ref[3], tab_ref[4], tab_ref[5])

        def rms_bwd(c, dn, gain):
            r = lax.rsqrt(jnp.mean(c * c, axis=1, keepdims=True) + RMS_EPS)
            u = dn * gain
            dc = r * u - c * (r * r * r) * jnp.mean(u * c, axis=1, keepdims=True)
            return dc, jnp.sum(dn * c * r, axis=0, keepdims=True)

        dqm = dqm_ref[0].T
        dq = jnp.concatenate(
            [dqm[:, :HW], _rope_wide(_rope_t, dqm[:, HW:], *m_tabs, MLA_ROPE // 2)], axis=1) * MLA_SCALE
        dq = dq.astype(BF16)
        guq_ref[...] += lax.dot_general(qn_ref[...], dq, TN, preferred_element_type=F32)
        dqn = lax.dot_general(dq, wuq_ref[...], NT, preferred_element_type=F32)
        dcq, gq = rms_bwd(cq_ref[...], dqn, qg_ref[...])
        small_ref[0:1, :] += gq

        dkv = jnp.concatenate([dkn_ref[...], dv_ref[...]], axis=1)
        gukv_ref[...] += lax.dot_general(kvn_ref[...], dkv, TN, preferred_element_type=F32)
        dkvn = lax.dot_general(dkv, wukv_ref[...], NT, preferred_element_type=F32)
        dckv, gkv = rms_bwd(ckv_ref[...], dkvn, kvg_ref[...])
        small_ref[1:2, :KV_RANK] += gkv

        dh_ref[:, C_CQ:C_CKV] = dcq.astype(BF16)
        dh_ref[:, C_CKV:C_KR] = dckv.astype(BF16)
        dh_ref[:, C_KR:C_GA] = _rope_t(dkpe_ref[...], *m_tabs, MLA_ROPE // 2).astype(BF16)
        dh_ref[:, C_GA:C_QB] = dga_ref[...]
        dh_ref[:, C_QB:C_KB] = (_rope_wide(_rope_t, dqb_ref[0].T, *d_tabs, DIL_ROT // 2) * DIL_SCALE).astype(BF16)
        dh_ref[:, C_KB:C_VB] = _rope_wide(_rope_t, dkb_ref[...], *d_tabs, DIL_ROT // 2).astype(BF16)
        dh_ref[:, C_VB:C_GB] = dvb_ref[...]
        dh_ref[:, C_GB:C_END] = dgb_ref[...]

    def tok(width):
        return pl.BlockSpec((bt, width), lambda i: (i, 0))

    def tok_t(a):
        per = a.shape[2] // bt
        return pl.BlockSpec((1, a.shape[1], bt), lambda i: (i // per, 0, i % per))

    def full(shape):
        return pl.BlockSpec(shape, lambda i: (0,) * len(shape))

    return _pcall(
        body, name="bwd_mid", grid=(seq // bt,),
        in_specs=[tok_t(dq_m), tok(HW), tok(HW), tok(LANES), tok_t(dqb), tok(HW), tok(HW), tok(HW), tok(HW),
                  tok(Q_RANK), tok(KV_RANK), tok(Q_RANK), tok(KV_RANK),
                  full(w_uq_r.shape), full(w_ukv_r.shape), full((1, Q_RANK)), full((1, KV_RANK)),
                  pl.BlockSpec((6, bt, LANES), lambda i: (0, i, 0))],
        out_specs=[tok(C_END), full(w_uq_r.shape), full(w_ukv_r.shape), full((8, Q_RANK))],
        out_shape=[jax.ShapeDtypeStruct((seq, C_END), BF16), jax.ShapeDtypeStruct(w_uq_r.shape, F32),
                   jax.ShapeDtypeStruct(w_ukv_r.shape, F32), jax.ShapeDtypeStruct((8, Q_RANK), F32)],
        compiler_params=_cparams(dimension_semantics=("arbitrary",)),
    )(dq_m, dkn, dv, dkpe, dqb, dkb, dvb, dga, dgb, cq, ckv, qn, kvn, w_uq_r, w_ukv_r, qg, kvg, tabs)


def _grad_x(dz, dh, w_in_r, bt):
    seq = dz.shape[0]

    def body(dz_ref, dh_ref, w_ref, gx_ref):
        gx_ref[...] = ALPHA * dz_ref[...] + lax.dot_general(
            dh_ref[...], w_ref[...], NT, preferred_element_type=F32)

    return _pcall(
        body, name="grad_x", grid=(seq // bt,),
        in_specs=[pl.BlockSpec((bt, D_MODEL), lambda i: (i, 0)), pl.BlockSpec((bt, C_END), lambda i: (i, 0)),
                  pl.BlockSpec(w_in_r.shape, lambda i: (0, 0))],
        out_specs=pl.BlockSpec((bt, D_MODEL), lambda i: (i, 0)),
        out_shape=jax.ShapeDtypeStruct((seq, D_MODEL), F32),
        compiler_params=_cparams(dimension_semantics=("arbitrary",)),
    )(dz, dh, w_in_r)


def _grad_w_in(x, dh, bt):
    seq = x.shape[0]

    def body(x_ref, dh_ref, gw_ref):
        @pl.when(pl.program_id(0) == 0)
        def _():
            gw_ref[...] = jnp.zeros(gw_ref.shape, F32)

        gw_ref[...] += lax.dot_general(x_ref[...].astype(BF16), dh_ref[...], TN, preferred_element_type=F32)

    return _pcall(
        body, name="grad_w_in", grid=(seq // bt,),
        in_specs=[pl.BlockSpec((bt, D_MODEL), lambda i: (i, 0)), pl.BlockSpec((bt, C_END), lambda i: (i, 0))],
        out_specs=pl.BlockSpec((D_MODEL, C_END), lambda i: (0, 0)),
        out_shape=jax.ShapeDtypeStruct((D_MODEL, C_END), F32),
        compiler_params=_cparams(dimension_semantics=("arbitrary",)),
    )(x, dh)


def _reorder_weights(w_in, w_uq, w_ukv):
    w_in_r = jnp.concatenate(
        [w_in[:, :640], jnp.tile(w_in[:, 640:672], (1, 4)), w_in[:, 672:]], axis=1)
    uq = w_uq.reshape(Q_RANK, HEADS, MLA_NOPE + MLA_ROPE)
    w_uq_r = jnp.concatenate(
        [uq[:, :, :MLA_NOPE].reshape(Q_RANK, HW), uq[:, :, MLA_NOPE:].reshape(Q_RANK, HEADS * MLA_ROPE)], axis=1)
    ukv = w_ukv.reshape(KV_RANK, HEADS, MLA_NOPE + MLA_V)
    w_ukv_r = jnp.concatenate(
        [ukv[:, :, :MLA_NOPE].reshape(KV_RANK, HW), ukv[:, :, MLA_NOPE:].reshape(KV_RANK, HW)], axis=1)
    return w_in_r, w_uq_r, w_ukv_r


def _restore_grads(g_in_r, g_uq_r, g_ukv_r):
    g_in = jnp.concatenate(
        [g_in_r[:, :640], g_in_r[:, 640:768].reshape(D_MODEL, 4, MLA_ROPE).sum(axis=1), g_in_r[:, 768:]], axis=1)
    g_uq = jnp.concatenate(
        [g_uq_r[:, :HW].reshape(Q_RANK, HEADS, MLA_NOPE), g_uq_r[:, HW:].reshape(Q_RANK, HEADS, MLA_ROPE)],
        axis=2).reshape(Q_RANK, HEADS * (MLA_NOPE + MLA_ROPE))
    g_ukv = jnp.concatenate(
        [g_ukv_r[:, :HW].reshape(KV_RANK, HEADS, MLA_NOPE), g_ukv_r[:, HW:].reshape(KV_RANK, HEADS, MLA_V)],
        axis=2).reshape(KV_RANK, HEADS * (MLA_NOPE + MLA_V))
    return g_in, g_uq, g_ukv


def _local_step(x, tgt, w_in, w_uq, w_ukv, w_out, q_norm_g, kv_norm_g, ln_g, ln_b,
                bt=BLOCK_TOKENS, blk_m=BLOCK_MLA, blk_d=BLOCK_DIL):
    seq = x.shape[0]
    tabs = jnp.asarray(_rope_tables(seq))
    w_in_r, w_uq_r, w_ukv_r = _reorder_weights(w_in, w_uq, w_ukv)
    qg, kvg = q_norm_g.reshape(1, -1), kv_norm_g.reshape(1, -1)

    (cq, ckv, qn, kvn, qcat, kn, kpe, v, ga, gb, qb, kb, vb, knt, kpet, vt, kbt, vbt) = _fwd_proj(
        x, w_in_r, w_uq_r, w_ukv_r, qg, kvg, tabs, bt)

    nq_m, nq_d = seq // blk_m, seq // blk_d
    span_d = DIL_CONFIGS[-1][0] // blk_d
    bias_m, bias_d = jnp.asarray(_mla_bias_t(blk_m)), jnp.asarray(_dil_bias_t(blk_d))
    oa, lse_a = _attn_fwd("mla_fwd", qcat, kn, kpe, vt, bias_m, _steps(nq_m, nq_m, False, True), blk_m)
    ob, lse_b = _attn_fwd("dil_fwd", qb, kb, None, vbt, bias_d, _steps(nq_d, span_d, False, False), blk_d)

    dz, doa, dob, dga, dgb, dst_a, dst_b, g_out, small1 = _out_ln(
        oa, ob, ga, gb, x, tgt, w_out, ln_g.reshape(1, -1), ln_b.reshape(1, -1), bt)

    dq_m, dkn, dkpe, dv = _attn_bwd(
        "mla_bwd", qcat, kn, kpe, v, knt, kpet, bias_m, doa, lse_a, dst_a, _steps(nq_m, nq_m, True, True), blk_m)
    dqb, dkb, dvb = _attn_bwd(
        "dil_bwd", qb, kb, None, vb, kbt, None, bias_d, dob, lse_b, dst_b, _steps(nq_d, span_d, True, False), blk_d)

    dh, g_uq_r, g_ukv_r, small2 = _bwd_mid(
        dq_m, dkn, dv, dkpe, dqb, dkb, dvb, dga, dgb, cq, ckv, qn, kvn, w_uq_r, w_ukv_r, qg, kvg, tabs, bt)
    grad_x = _grad_x(dz, dh, w_in_r, bt)
    g_in_r = _grad_w_in(x, dh, bt)
    g_in, g_uq, g_ukv = _restore_grads(g_in_r, g_uq_r, g_ukv_r)

    loss = (0.5 / D_MODEL) * jnp.sum(small1[2])
    return loss, grad_x, g_in, g_uq, g_ukv, g_out, small2[0], small2[1, :KV_RANK], small1[0], small1[1]


MESH_ID = pl.DeviceIdType.MESH
SHARD_SHAPES = ((D_MODEL, IN_WIDTH // N_DEV), (Q_RANK, 768 // N_DEV), (KV_RANK, 1024 // N_DEV), (D_MODEL // N_DEV, D_MODEL))
ADAM_ROWS = (32, 128, 128, 16)


def _me():
    x, y, c = lax.axis_index("x"), lax.axis_index("y"), lax.axis_index("c")
    return x, y, c, 4 * x + 2 * y + c


def _peer(k):
    x, y, c, _ = _me()
    px = 1 - x if (k >> 2) & 1 else x
    py = 1 - y if (k >> 1) & 1 else y
    pc = 1 - c if k & 1 else c
    return (px, py, pc), 4 * px + 2 * py + pc


def _all_gather_weights(shards):
    def body(*refs):
        ins, outs = refs[:4], refs[4:8]
        send_sems, recv_sems = refs[8:]
        me = _me()[3]
        for t in range(4):
            outs[t][me] = ins[t][...].astype(BF16)
        sends = []
        for t in range(4):
            for k in range(1, N_DEV):
                peer, _ = _peer(k)
                cp = pltpu.make_async_remote_copy(
                    src_ref=outs[t].at[me], dst_ref=outs[t].at[me], send_sem=send_sems.at[t, k - 1],
                    recv_sem=recv_sems.at[t, k - 1], device_id=peer, device_id_type=MESH_ID)
                cp.start()
                sends.append(cp)
        for t in range(4):
            for k in range(1, N_DEV):
                peer, pidx = _peer(k)
                pltpu.make_async_remote_copy(
                    src_ref=outs[t].at[pidx], dst_ref=outs[t].at[pidx], send_sem=send_sems.at[t, k - 1],
                    recv_sem=recv_sems.at[t, k - 1], device_id=peer, device_id_type=MESH_ID).wait_recv()
        for cp in sends:
            cp.wait_send()

    vmem = pl.BlockSpec(memory_space=pltpu.VMEM)
    return _pcall(
        body, name="gather_weights",
        in_specs=[vmem] * 4, out_specs=[vmem] * 4,
        out_shape=[jax.ShapeDtypeStruct((N_DEV,) + s, BF16) for s in SHARD_SHAPES],
        scratch_shapes=[pltpu.SemaphoreType.DMA((4, N_DEV - 1)), pltpu.SemaphoreType.DMA((4, N_DEV - 1))],
        compiler_params=_cparams(),
    )(*shards)


def _adamw(w, g, m, v):
    m = ADAM_B1 * m + (1.0 - ADAM_B1) * g
    v = ADAM_B2 * v + (1.0 - ADAM_B2) * jnp.square(g)
    m_hat = m / (1.0 - ADAM_B1 ** ADAM_STEP)
    v_hat = v / (1.0 - ADAM_B2 ** ADAM_STEP)
    delta = -ADAM_LR * (m_hat / (jnp.sqrt(v_hat) + ADAM_EPS) + ADAM_WD * w)
    return delta, m, v


def _reduce_adam(grads3, small_part, wmv, small_wmv):
    def body(*refs):
        g3 = refs[0:4]
        sp_ref = refs[4]
        wmv_refs = [refs[5 + 3 * t:8 + 3 * t] for t in range(4)]
        swmv_ref = refs[17]
        out_refs = [refs[18 + 4 * t:22 + 4 * t] for t in range(4)]
        sout_ref = refs[34]
        recv = refs[35:39]
        rsmall = refs[39]
        send_sems, recv_sems, local_sems = refs[40:43]
        me = _me()[3]

        rsmall[0] = sp_ref[...]
        local = []
        for t in range(4):
            cp = pltpu.make_async_copy(g3[t].at[me], recv[t].at[0], local_sems.at[t])
            cp.start()
            local.append(cp)
        sends = []
        for k in range(1, N_DEV):
            peer, pidx = _peer(k)
            for t in range(5):
                src = rsmall.at[0] if t == 4 else g3[t].at[pidx]
                dst = rsmall.at[k] if t == 4 else recv[t].at[k]
                cp = pltpu.make_async_remote_copy(
                    src_ref=src, dst_ref=dst, send_sem=send_sems.at[t, k - 1], recv_sem=recv_sems.at[t, k - 1],
                    device_id=peer, device_id_type=MESH_ID)
                cp.start()
                sends.append(cp)
        for cp in local:
            cp.wait()
        for cp in sends:
            cp.wait_recv()

        tot = rsmall[me]
        for d in range(1, N_DEV):
            tot = tot + rsmall[jnp.bitwise_xor(me, d)]
        delta, m, v = _adamw(swmv_ref[0], tot, swmv_ref[1], swmv_ref[2])
        sout_ref[0], sout_ref[1], sout_ref[2], sout_ref[3] = tot, delta, m, v

        for t in range(4):
            rows = ADAM_ROWS[t]
            w_ref, m_ref, v_ref = wmv_refs[t]
            g_out, d_out, m_out, v_out = out_refs[t]

            def step(i, carry, t=t, rows=rows, w_ref=w_ref, m_ref=m_ref, v_ref=v_ref,
                     g_out=g_out, d_out=d_out, m_out=m_out, v_out=v_out):
                r = pl.ds(pl.multiple_of(i * rows, rows), rows)
                g = recv[t][0, r, :]
                for k in range(1, N_DEV):
                    g = g + recv[t][k, r, :]
                delta, m, v = _adamw(w_ref[r, :], g, m_ref[r, :], v_ref[r, :])
                g_out[r, :], d_out[r, :], m_out[r, :], v_out[r, :] = g, delta, m, v
                return carry

            lax.fori_loop(0, SHARD_SHAPES[t][0] // rows, step, 0)

        for cp in sends:
            cp.wait_send()

    vmem = pl.BlockSpec(memory_space=pltpu.VMEM)
    hbm = pl.BlockSpec(memory_space=pl.ANY)
    flat_wmv = [a for trio in wmv for a in trio]
    return _pcall(
        body, name="reduce_adam",
        in_specs=[hbm] * 4 + [vmem] * 14,
        out_specs=[vmem] * 17,
        out_shape=[jax.ShapeDtypeStruct(s, F32) for s in SHARD_SHAPES for _ in range(4)]
        + [jax.ShapeDtypeStruct((4, 8, D_MODEL), F32)],
        scratch_shapes=[pltpu.VMEM((N_DEV,) + s, F32) for s in SHARD_SHAPES]
        + [pltpu.VMEM((N_DEV, 8, D_MODEL), F32),
           pltpu.SemaphoreType.DMA((5, N_DEV - 1)), pltpu.SemaphoreType.DMA((5, N_DEV - 1)),
           pltpu.SemaphoreType.DMA((4,))],
        compiler_params=_cparams(),
    )(*grads3, small_part, *flat_wmv, small_wmv)


def _small_rows(ln_g, ln_b, q_norm_g, kv_norm_g):
    pad = lambda a: jnp.pad(a, (0, D_MODEL - a.shape[0]))
    return jnp.pad(jnp.stack([ln_g, ln_b, pad(q_norm_g), pad(kv_norm_g)]), ((0, 4), (0, 0)))


def kernel(x, w_in, q_norm_g, kv_norm_g, w_uq, w_ukv, w_out, ln_g, ln_b, loss_target, m_w_in, m_q_norm_g, m_kv_norm_g, m_w_uq, m_w_ukv, m_w_out, m_ln_g, m_ln_b, v_w_in, v_q_norm_g, v_kv_norm_g, v_w_uq, v_w_ukv, v_w_out, v_ln_g, v_ln_b):
    a_in, a_uq, a_ukv, a_out = _all_gather_weights([w_in, w_uq, w_ukv, w_out])
    full_in = a_in.transpose(1, 0, 2).reshape(D_MODEL, IN_WIDTH)
    full_uq = a_uq.transpose(1, 0, 2).reshape(Q_RANK, 768)
    full_ukv = a_ukv.transpose(1, 0, 2).reshape(KV_RANK, 1024)
    full_out = a_out.reshape(D_MODEL, D_MODEL)

    loss_l, grad_x, g_in, g_uq, g_ukv, g_out, g_qg, g_kvg, g_lng, g_lnb = _local_step(
        x[0], loss_target[0], full_in, full_uq, full_ukv, full_out, q_norm_g, kv_norm_g, ln_g, ln_b)
    loss = lax.psum(loss_l, ("x", "y", "c"))

    grads3 = [g_in.reshape(D_MODEL, N_DEV, -1).transpose(1, 0, 2), g_uq.reshape(Q_RANK, N_DEV, -1).transpose(1, 0, 2),
              g_ukv.reshape(KV_RANK, N_DEV, -1).transpose(1, 0, 2), g_out.reshape(N_DEV, D_MODEL // N_DEV, D_MODEL)]
    small_part = _small_rows(g_lng, g_lnb, g_qg, g_kvg)
    small_wmv = jnp.stack([_small_rows(ln_g, ln_b, q_norm_g, kv_norm_g),
                           _small_rows(m_ln_g, m_ln_b, m_q_norm_g, m_kv_norm_g),
                           _small_rows(v_ln_g, v_ln_b, v_q_norm_g, v_kv_norm_g)])
    wmv = [(w_in, m_w_in, v_w_in), (w_uq, m_w_uq, v_w_uq), (w_ukv, m_w_ukv, v_w_ukv), (w_out, m_w_out, v_w_out)]
    res = _reduce_adam(grads3, small_part, wmv, small_wmv)
    big = [res[4 * t:4 * t + 4] for t in range(4)]
    small = res[16]

    def group(kind):
        s = small[kind]
        return (big[0][kind], s[2, :Q_RANK], s[3, :KV_RANK], big[1][kind], big[2][kind], big[3][kind], s[0], s[1])

    return (loss, grad_x[None], *group(0), *group(1), *group(2), *group(3))
```

```python
import numpy as np
import jax
import jax.numpy as jnp
from jax import lax
from jax.experimental import pallas as pl
from jax.experimental.pallas import tpu as pltpu

F32 = jnp.float32
BF16 = jnp.bfloat16

D_MODEL = 1024
ROPE_THETA = 500000.0
NEG = -1e30
RMS_EPS = 1e-6
LN_EPS = 1e-5
HEADS = 8
MLA_NOPE = 64
MLA_ROPE = 32
MLA_V = 64
Q_RANK = 384
KV_RANK = 256
DIL_HEAD = 64
DIL_ROT = 16
DIL_CONFIGS = ((128, 1), (512, 4), (2048, 16))
HW = HEADS * 64
QW = HW + HEADS * MLA_ROPE
IN_SPLITS = (Q_RANK, KV_RANK, MLA_ROPE, HW, HW, HW, HW, HW)
IN_WIDTH = sum(IN_SPLITS)
ALPHA = 2.0 ** 0.25
MLA_SCALE = (MLA_NOPE + MLA_ROPE) ** -0.5
DIL_SCALE = DIL_HEAD ** -0.5

ADAM_LR = 0.001
ADAM_B1 = 0.9
ADAM_B2 = 0.999
ADAM_EPS = 1e-08
ADAM_WD = 0.01
ADAM_STEP = 10

N_DEV = 8
LANES = 128
VMEM_LIMIT = 56 * 1024 * 1024
BLOCK_TOKENS = 256
BLOCK_MLA = 512
BLOCK_DIL = 512

C_CQ, C_CKV, C_KR, C_GA, C_QB, C_KB, C_VB, C_GB, C_END = 0, 384, 640, 768, 1280, 1792, 2304, 2816, 3328

NT = (((1,), (1,)), ((), ()))
TN = (((0,), (0,)), ((), ()))


def _pcall(body, **kw):
    return pl.pallas_call(body, **kw)


def _cparams(**kw):
    return pltpu.CompilerParams(vmem_limit_bytes=VMEM_LIMIT, **kw)


def _rope_tables(seq):
    def tabs(dim, period):
        half = dim // 2
        inv = np.float32(ROPE_THETA) ** (-np.arange(0, dim, 2, dtype=np.float32) / np.float32(dim))
        ang = np.arange(seq, dtype=np.float32)[:, None] * inv.astype(np.float32)[None, :]
        cos, sin = np.cos(ang).astype(np.float32), np.sin(ang).astype(np.float32)
        j = np.arange(LANES) % period
        f = j % half
        c = np.where(j < dim, cos[:, f], np.float32(1.0))
        s1 = np.where(j < half, -sin[:, f], np.float32(0.0))
        s2 = np.where((j >= half) & (j < dim), sin[:, f], np.float32(0.0))
        return [c, s1, s2]
    return np.stack(tabs(MLA_ROPE, MLA_ROPE) + tabs(DIL_ROT, DIL_HEAD)).astype(np.float32)


def _rope(t, c, s1, s2, half):
    return t * c + pltpu.roll(t, LANES - half, 1) * s1 + pltpu.roll(t, half, 1) * s2


def _rope_t(d, c, s1, s2, half):
    return d * c + pltpu.roll(d * s1, half, 1) + pltpu.roll(d * s2, LANES - half, 1)


def _rope_wide(fn, t, c, s1, s2, half):
    return jnp.concatenate(
        [fn(t[:, i:i + LANES], c, s1, s2, half) for i in range(0, t.shape[1], LANES)], axis=1)


def _mla_bias_t(blk):
    a = np.arange(blk)
    causal = np.where(a[:, None] <= a[None, :], 0.0, NEG)
    return np.stack([np.zeros((blk, blk)), causal]).astype(np.float32)


def _dil_bias_t(blk):
    span = DIL_CONFIGS[-1][0] // blk
    a = np.arange(blk)
    out = []
    for off in range(span + 1):
        delta = blk * off + a[None, :] - a[:, None]
        mult = np.zeros((blk, blk))
        for window, dil in DIL_CONFIGS:
            mult += (delta >= 0) & (delta % dil == 0) & (delta <= window)
        out.append(np.where(mult > 0, np.log(np.maximum(mult, 1.0)), NEG))
    return np.stack(out).astype(np.float32)


def _steps(nq, span, by_key, diag_only_bias):
    rows = []
    if by_key:
        for ki in range(nq):
            hi = min(nq - 1, ki + span)
            for qi in range(ki, hi + 1):
                rows.append((qi, ki, int(qi == ki), int(qi == hi)))
    else:
        for qi in range(nq):
            lo = max(0, qi - span)
            for ki in range(lo, qi + 1):
                rows.append((qi, ki, int(ki == lo), int(ki == qi)))
    arr = np.array(rows, dtype=np.int32)
    off = arr[:, 0] - arr[:, 1]
    bias_idx = (off == 0).astype(np.int32) if diag_only_bias else off.astype(np.int32)
    return [jnp.asarray(v) for v in (arr[:, 0], arr[:, 1], bias_idx, arr[:, 2], arr[:, 3])]


def _fwd_proj(x, w_in_r, w_uq_r, w_ukv_r, qg, kvg, tabs, bt):
    seq = x.shape[0]

    def body(x_ref, win_ref, wuq_ref, wukv_ref, qg_ref, kvg_ref, tab_ref,
             cq_ref, ckv_ref, qn_ref, kvn_ref, qcat_ref, kn_ref, kpe_ref, v_ref,
             ga_ref, gb_ref, qb_ref, kb_ref, vb_ref, knt_ref, kpet_ref, vt_ref, kbt_ref, vbt_ref):
        xb = x_ref[...].astype(BF16)

        def proj(lo, hi):
            return jnp.dot(xb, win_ref[:, lo:hi], preferred_element_type=F32)

        m_tabs = (tab_ref[0], tab_ref[1], tab_ref[2])
        d_tabs = (tab_ref[3], tab_ref[4], tab_ref[5])

        cq = proj(C_CQ, C_CKV)
        cq_ref[...] = cq
        qn = (cq * lax.rsqrt(jnp.mean(cq * cq, axis=1, keepdims=True) + RMS_EPS) * qg_ref[...]).astype(BF16)
        qn_ref[...] = qn
        q = jnp.dot(qn, wuq_ref[...], preferred_element_type=F32)
        qcat_ref[:, :HW] = (q[:, :HW] * MLA_SCALE).astype(BF16)
        qcat_ref[:, HW:] = (_rope_wide(_rope, q[:, HW:], *m_tabs, MLA_ROPE // 2) * MLA_SCALE).astype(BF16)

        ckv = proj(C_CKV, C_KR)
        ckv_ref[...] = ckv
        kvn = (ckv * lax.rsqrt(jnp.mean(ckv * ckv, axis=1, keepdims=True) + RMS_EPS) * kvg_ref[...]).astype(BF16)
        kvn_ref[...] = kvn
        kv = jnp.dot(kvn, wukv_ref[...], preferred_element_type=F32)
        kn_ref[...] = kv[:, :HW].astype(BF16)
        v_ref[...] = kv[:, HW:].astype(BF16)
        knt_ref[...] = kv[:, :HW].T.astype(BF16)
        vt_ref[...] = kv[:, HW:].T.astype(BF16)

        kpe = _rope(proj(C_KR, C_GA), *m_tabs, MLA_ROPE // 2)
        kpe_ref[...] = kpe.astype(BF16)
        kpet_ref[...] = kpe.T[:MLA_ROPE, :].astype(BF16)
        ga_ref[...] = proj(C_GA, C_QB)
        qb_ref[...] = (_rope_wide(_rope, proj(C_QB, C_KB), *d_tabs, DIL_ROT // 2) * DIL_SCALE).astype(BF16)
        kb = _rope_wide(_rope, proj(C_KB, C_VB), *d_tabs, DIL_ROT // 2)
        kb_ref[...] = kb.astype(BF16)
        kbt_ref[...] = kb.T.astype(BF16)
        vb = proj(C_VB, C_GB)
        vb_ref[...] = vb.astype(BF16)
        vbt_ref[...] = vb.T.astype(BF16)
        gb_ref[...] = proj(C_GB, C_END)

    def tok(width):
        return pl.BlockSpec((bt, width), lambda i: (i, 0))

    def tok_t(height):
        return pl.BlockSpec((height, bt), lambda i: (0, i))

    def full(a):
        return pl.BlockSpec(a.shape, lambda i: (0,) * a.ndim)

    outs = [(Q_RANK, F32), (KV_RANK, F32), (Q_RANK, BF16), (KV_RANK, BF16), (QW, BF16), (HW, BF16),
            (LANES, BF16), (HW, BF16), (HW, F32), (HW, F32), (HW, BF16), (HW, BF16), (HW, BF16)]
    outs_t = [HW, MLA_ROPE, HW, HW, HW]
    return _pcall(
        body, name="fwd_proj", grid=(seq // bt,),
        in_specs=[tok(D_MODEL), full(w_in_r), full(w_uq_r), full(w_ukv_r), full(qg), full(kvg),
                  pl.BlockSpec((6, bt, LANES), lambda i: (0, i, 0))],
        out_specs=[tok(w) for w, _ in outs] + [tok_t(h) for h in outs_t],
        out_shape=[jax.ShapeDtypeStruct((seq, w), dt) for w, dt in outs]
        + [jax.ShapeDtypeStruct((h, seq), BF16) for h in outs_t],
        compiler_params=_cparams(dimension_semantics=("arbitrary",)),
    )(x, w_in_r, w_uq_r, w_ukv_r, qg, kvg, tabs)


def _head_masks(lane, h):
    e, g = h % 2, h % 4
    me = (lane >= 64 * e) & (lane < 64 * e + 64)
    mr = (lane >= 32 * g) & (lane < 32 * g + 32)
    return me, mr


def _masked(mask, a):
    return jnp.where(mask, a, jnp.zeros_like(a))


def _attn_fwd(name, q, k, kpe, vt, bias_t, steps, blk):
    seq = q.shape[0]
    mla = kpe is not None
    n_steps = int(steps[0].shape[0])

    def body(qi_r, ki_r, bi_r, fi_r, la_r, *refs):
        if mla:
            q_ref, k_ref, kpe_ref, vt_ref, b_ref, o_ref, lse_ref, m_sc, l_sc, acc_sc = refs
        else:
            q_ref, k_ref, vt_ref, b_ref, o_ref, lse_ref, m_sc, l_sc, acc_sc = refs
        t = pl.program_id(0)

        @pl.when(fi_r[t] == 1)
        def _():
            m_sc[...] = jnp.full(m_sc.shape, NEG, F32)
            l_sc[...] = jnp.zeros(l_sc.shape, F32)
            acc_sc[...] = jnp.zeros(acc_sc.shape, F32)

        lane = lax.broadcasted_iota(jnp.int32, (1, LANES), 1)
        bias = b_ref[0]
        for j in range(HEADS // 2):
            cols = slice(LANES * j, LANES * (j + 1))
            qc = q_ref[:, cols]
            if mla:
                qc = jnp.concatenate([qc, q_ref[:, HW + LANES * (j // 2):HW + LANES * (j // 2 + 1)]], axis=1)
            kj = k_ref[:, cols]
            for h in (2 * j, 2 * j + 1):
                me, mr = _head_masks(lane, h)
                ke = _masked(me, kj)
                if mla:
                    ke = jnp.concatenate([ke, _masked(mr, kpe_ref[...])], axis=1)
                st = lax.dot_general(ke, qc, NT, preferred_element_type=F32) + bias
                hrow = slice(h, h + 1)
                m_prev = m_sc[hrow, :]
                m_new = jnp.maximum(m_prev, jnp.max(st, axis=0, keepdims=True))
                alpha = jnp.exp(m_prev - m_new)
                pt = jnp.exp(st - m_new)
                l_sc[hrow, :] = alpha * l_sc[hrow, :] + jnp.sum(pt, axis=0, keepdims=True)
                m_sc[hrow, :] = m_new
                rows = slice(64 * h, 64 * h + 64)
                acc_sc[rows, :] = alpha * acc_sc[rows, :] + jnp.dot(
                    vt_ref[rows, :], pt.astype(BF16), preferred_element_type=F32)

        @pl.when(la_r[t] == 1)
        def _():
            for h in range(HEADS):
                rows = slice(64 * h, 64 * h + 64)
                acc_sc[rows, :] = acc_sc[rows, :] / l_sc[h:h + 1, :]
            o_ref[...] = acc_sc[...].T
            lse_ref[...] = m_sc[...] + jnp.log(l_sc[...])

    qmap = lambda t, qi, ki, bi, fi, la: (qi[t], 0)
    kmap = lambda t, qi, ki, bi, fi, la: (ki[t], 0)
    in_specs = [pl.BlockSpec((blk, q.shape[1]), qmap), pl.BlockSpec((blk, HW), kmap)]
    args = [q, k]
    if mla:
        in_specs.append(pl.BlockSpec((blk, LANES), kmap))
        args.append(kpe)
    in_specs += [pl.BlockSpec((HW, blk), lambda t, qi, ki, bi, fi, la: (0, ki[t])),
                 pl.BlockSpec((1, blk, blk), lambda t, qi, ki, bi, fi, la: (bi[t], 0, 0))]
    args += [vt, bias_t]
    return _pcall(
        body, name=name,
        grid_spec=pltpu.PrefetchScalarGridSpec(
            num_scalar_prefetch=5, grid=(n_steps,), in_specs=in_specs,
            out_specs=[pl.BlockSpec((blk, HW), qmap),
                       pl.BlockSpec((HEADS, blk), lambda t, qi, ki, bi, fi, la: (0, qi[t]))],
            scratch_shapes=[pltpu.VMEM((HEADS, blk), F32), pltpu.VMEM((HEADS, blk), F32),
                            pltpu.VMEM((HW, blk), F32)]),
        out_shape=[jax.ShapeDtypeStruct((seq, HW), F32), jax.ShapeDtypeStruct((HEADS, seq), F32)],
        compiler_params=_cparams(dimension_semantics=("arbitrary",)),
    )(*steps, *args)


def _attn_bwd(name, q, k, kpe, v, kt, kpet, bias_t, do, lse, dstat, steps, blk):
    seq = q.shape[0]
    mla = kpe is not None
    qw = q.shape[1]
    n_steps = int(steps[0].shape[0])
    dk_dtype = BF16 if mla else F32

    def body(qi_r, ki_r, bi_r, fi_r, la_r, *refs):
        if mla:
            (q_ref, k_ref, kpe_ref, v_ref, kt_ref, kpet_ref, b_ref, do_ref, lse_ref, d_ref,
             dq_ref, dk_ref, dkpe_ref, dv_ref, dk_sc, dkpe_sc, dv_sc) = refs
        else:
            (q_ref, k_ref, v_ref, kt_ref, b_ref, do_ref, lse_ref, d_ref,
             dq_ref, dk_ref, dv_ref, dk_sc, dv_sc) = refs
        t = pl.program_id(0)

        @pl.when(t == 0)
        def _():
            dq_ref[...] = jnp.zeros(dq_ref.shape, F32)

        @pl.when(fi_r[t] == 1)
        def _():
            dk_sc[...] = jnp.zeros(dk_sc.shape, F32)
            dv_sc[...] = jnp.zeros(dv_sc.shape, F32)
            if mla:
                dkpe_sc[...] = jnp.zeros(dkpe_sc.shape, F32)

        qi = qi_r[t]
        lane = lax.broadcasted_iota(jnp.int32, (1, LANES), 1)
        bias = b_ref[0]
        for j in range(HEADS // 2):
            cols = slice(LANES * j, LANES * (j + 1))
            qc = q_ref[:, cols]
            if mla:
                qc = jnp.concatenate([qc, q_ref[:, HW + LANES * (j // 2):HW + LANES * (j // 2 + 1)]], axis=1)
            kj = k_ref[:, cols]
            vj = v_ref[:, cols]
            doj = do_ref[:, cols]
            for h in (2 * j, 2 * j + 1):
                me, mr = _head_masks(lane, h)
                ke = _masked(me, kj)
                if mla:
                    ke = jnp.concatenate([ke, _masked(mr, kpe_ref[...])], axis=1)
                st = lax.dot_general(ke, qc, NT, preferred_element_type=F32) + bias
                pt = jnp.exp(st - lse_ref[h:h + 1, :])
                dpt = lax.dot_general(_masked(me, vj), doj, NT, preferred_element_type=F32)
                dst = (pt * (dpt - d_ref[h:h + 1, :])).astype(BF16)
                dv_sc[:, cols] += jnp.dot(pt.astype(BF16), _masked(me, doj), preferred_element_type=F32)
                dkc = jnp.dot(dst, qc, preferred_element_type=F32)
                dk_sc[:, cols] += jnp.where(me, dkc[:, :LANES], 0.0)
                ktl = kt_ref[64 * h:64 * h + 64, :]
                if mla:
                    dkpe_sc[...] += jnp.where(mr, dkc[:, LANES:], 0.0)
                    ktl = jnp.concatenate([ktl, kpet_ref[...]], axis=0)
                dqc = jnp.dot(ktl, dst, preferred_element_type=F32)
                dq_ref[qi, 64 * h:64 * h + 64, :] += dqc[:64]
                if mla:
                    dq_ref[qi, HW + MLA_ROPE * h:HW + MLA_ROPE * (h + 1), :] += dqc[64:]

        @pl.when(la_r[t] == 1)
        def _():
            dk_ref[...] = dk_sc[...].astype(dk_ref.dtype)
            dv_ref[...] = dv_sc[...].astype(dv_ref.dtype)
            if mla:
                dkpe_ref[...] = dkpe_sc[...]

    qmap = lambda t, qi, ki, bi, fi, la: (qi[t], 0)
    kmap = lambda t, qi, ki, bi, fi, la: (ki[t], 0)
    qmap_t = lambda t, qi, ki, bi, fi, la: (0, qi[t])
    kmap_t = lambda t, qi, ki, bi, fi, la: (0, ki[t])
    in_specs = [pl.BlockSpec((blk, qw), qmap), pl.BlockSpec((blk, HW), kmap)]
    args = [q, k]
    if mla:
        in_specs.append(pl.BlockSpec((blk, LANES), kmap))
        args.append(kpe)
    in_specs += [pl.BlockSpec((blk, HW), kmap), pl.BlockSpec((HW, blk), kmap_t)]
    args += [v, kt]
    if mla:
        in_specs.append(pl.BlockSpec((MLA_ROPE, blk), kmap_t))
        args.append(kpet)
    in_specs += [pl.BlockSpec((1, blk, blk), lambda t, qi, ki, bi, fi, la: (bi[t], 0, 0)),
                 pl.BlockSpec((blk, HW), qmap), pl.BlockSpec((HEADS, blk), qmap_t), pl.BlockSpec((HEADS, blk), qmap_t)]
    args += [bias_t, do, lse, dstat]
    dq_shape = (seq // blk, qw, blk)
    out_specs = [pl.BlockSpec(dq_shape, lambda t, qi, ki, bi, fi, la: (0, 0, 0)), pl.BlockSpec((blk, HW), kmap)]
    out_shape = [jax.ShapeDtypeStruct(dq_shape, F32), jax.ShapeDtypeStruct((seq, HW), dk_dtype)]
    scratch = [pltpu.VMEM((blk, HW), F32)]
    if mla:
        out_specs.append(pl.BlockSpec((blk, LANES), kmap))
        out_shape.append(jax.ShapeDtypeStruct((seq, LANES), F32))
        scratch.append(pltpu.VMEM((blk, LANES), F32))
    out_specs.append(pl.BlockSpec((blk, HW), kmap))
    out_shape.append(jax.ShapeDtypeStruct((seq, HW), BF16))
    scratch.append(pltpu.VMEM((blk, HW), F32))
    return _pcall(
        body, name=name,
        grid_spec=pltpu.PrefetchScalarGridSpec(
            num_scalar_prefetch=5, grid=(n_steps,), in_specs=in_specs, out_specs=out_specs,
            scratch_shapes=scratch),
        out_shape=out_shape,
        compiler_params=_cparams(dimension_semantics=("arbitrary",)),
    )(*steps, *args)


def _out_ln(oa, ob, ga, gb, x, tgt, w_out, ln_g, ln_b, bt):
    seq = x.shape[0]

    def body(oa_ref, ob_ref, ga_ref, gb_ref, x_ref, tgt_ref, w_ref, g_ref, b_ref,
             dz_ref, doa_ref, dob_ref, dga_ref, dgb_ref, da_ref, db_ref, gw_ref, small_ref):
        i = pl.program_id(0)

        @pl.when(i == 0)
        def _():
            gw_ref[...] = jnp.zeros(gw_ref.shape, F32)
            small_ref[...] = jnp.zeros(small_ref.shape, F32)

        def gate(g):
            sig = 1.0 / (1.0 + jnp.exp(-g))
            return g * sig, sig * (1.0 + g * (1.0 - sig))

        o_a, o_b = oa_ref[...], ob_ref[...]
        g_a, g_b = ga_ref[...], gb_ref[...]
        sa, dsa = gate(g_a)
        sb, dsb = gate(g_b)
        mix = jnp.concatenate([o_a * sa, o_b * sb], axis=1).astype(BF16)
        z = ALPHA * x_ref[...] + jnp.dot(mix, w_ref[...], preferred_element_type=F32)
        mu = jnp.mean(z, axis=1, keepdims=True)
        zc = z - mu
        rstd = lax.rsqrt(jnp.mean(zc * zc, axis=1, keepdims=True) + LN_EPS)
        xhat = zc * rstd
        gam = g_ref[...]
        diff = xhat * gam + b_ref[...] - tgt_ref[...]
        dy = diff * (1.0 / D_MODEL)
        small_ref[0:1, :] += jnp.sum(dy * xhat, axis=0, keepdims=True)
        small_ref[1:2, :] += jnp.sum(dy, axis=0, keepdims=True)
        small_ref[2:3, :] += jnp.sum(diff * diff, axis=0, keepdims=True)
        dxh = dy * gam
        dz = rstd * (dxh - jnp.mean(dxh, axis=1, keepdims=True) - xhat * jnp.mean(dxh * xhat, axis=1, keepdims=True))
        dz_ref[...] = dz
        dzb = dz.astype(BF16)
        gw_ref[...] += lax.dot_general(mix, dzb, TN, preferred_element_type=F32)
        dmix = lax.dot_general(dzb, w_ref[...], NT, preferred_element_type=F32)
        doa, dob = dmix[:, :HW] * sa, dmix[:, HW:] * sb
        doa_ref[...] = doa.astype(BF16)
        dob_ref[...] = dob.astype(BF16)
        dga_ref[...] = (dmix[:, :HW] * o_a * dsa).astype(BF16)
        dgb_ref[...] = (dmix[:, HW:] * o_b * dsb).astype(BF16)
        head_of = lax.broadcasted_iota(jnp.int32, (HEADS, HW), 1) // 64
        ind = (head_of == lax.broadcasted_iota(jnp.int32, (HEADS, HW), 0)).astype(F32)
        da_ref[...] = lax.dot_general(ind, doa * o_a, NT, preferred_element_type=F32, precision=lax.Precision.HIGHEST)
        db_ref[...] = lax.dot_general(ind, dob * o_b, NT, preferred_element_type=F32, precision=lax.Precision.HIGHEST)

    def tok(width):
        return pl.BlockSpec((bt, width), lambda i: (i, 0))

    def full(shape):
        return pl.BlockSpec(shape, lambda i: (0,) * len(shape))

    stat = pl.BlockSpec((HEADS, bt), lambda i: (0, i))
    return _pcall(
        body, name="out_ln", grid=(seq // bt,),
        in_specs=[tok(HW), tok(HW), tok(HW), tok(HW), tok(D_MODEL), tok(D_MODEL),
                  full((D_MODEL, D_MODEL)), full((1, D_MODEL)), full((1, D_MODEL))],
        out_specs=[tok(D_MODEL), tok(HW), tok(HW), tok(HW), tok(HW), stat, stat,
                   full((D_MODEL, D_MODEL)), full((8, D_MODEL))],
        out_shape=[jax.ShapeDtypeStruct((seq, D_MODEL), F32)] + [jax.ShapeDtypeStruct((seq, HW), BF16)] * 4
        + [jax.ShapeDtypeStruct((HEADS, seq), F32)] * 2
        + [jax.ShapeDtypeStruct((D_MODEL, D_MODEL), F32), jax.ShapeDtypeStruct((8, D_MODEL), F32)],
        compiler_params=_cparams(dimension_semantics=("arbitrary",)),
    )(oa, ob, ga, gb, x, tgt, w_out, ln_g, ln_b)


def _bwd_mid(dq_m, dkn, dv, dkpe, dqb, dkb, dvb, dga, dgb, cq, ckv, qn, kvn, w_uq_r, w_ukv_r, qg, kvg, tabs, bt):
    seq = cq.shape[0]

    def body(dqm_ref, dkn_ref, dv_ref, dkpe_ref, dqb_ref, dkb_ref, dvb_ref, dga_ref, dgb_ref,
             cq_ref, ckv_ref, qn_ref, kvn_ref, wuq_ref, wukv_ref, qg_ref, kvg_ref, tab_ref,
             dh_ref, guq_ref, gukv_ref, small_ref):
        i = pl.program_id(0)

        @pl.when(i == 0)
        def _():
            guq_ref[...] = jnp.zeros(guq_ref.shape, F32)
            gukv_ref[...] = jnp.zeros(gukv_ref.shape, F32)
            small_ref[...] = jnp.zeros(small_ref.shape, F32)

        m_tabs = (tab_ref[0], tab_ref[1], tab_ref[2])
        d_tabs = (tab_ref[3], tab_ref[4], tab_ref[5])

        def rms_bwd(c, dn, gain):
            r = lax.rsqrt(jnp.mean(c * c, axis=1, keepdims=True) + RMS_EPS)
            u = dn * gain
            dc = r * u - c * (r * r * r) * jnp.mean(u * c, axis=1, keepdims=True)
            return dc, jnp.sum(dn * c * r, axis=0, keepdims=True)

        dqm = dqm_ref[0].T
        dq = jnp.concatenate(
            [dqm[:, :HW], _rope_wide(_rope_t, dqm[:, HW:], *m_tabs, MLA_ROPE // 2)], axis=1) * MLA_SCALE
        dq = dq.astype(BF16)
        guq_ref[...] += lax.dot_general(qn_ref[...], dq, TN, preferred_element_type=F32)
        dqn = lax.dot_general(dq, wuq_ref[...], NT, preferred_element_type=F32)
        dcq, gq = rms_bwd(cq_ref[...], dqn, qg_ref[...])
        small_ref[0:1, :] += gq

        dkv = jnp.concatenate([dkn_ref[...], dv_ref[...]], axis=1)
        gukv_ref[...] += lax.dot_general(kvn_ref[...], dkv, TN, preferred_element_type=F32)
        dkvn = lax.dot_general(dkv, wukv_ref[...], NT, preferred_element_type=F32)
        dckv, gkv = rms_bwd(ckv_ref[...], dkvn, kvg_ref[...])
        small_ref[1:2, :KV_RANK] += gkv

        dh_ref[:, C_CQ:C_CKV] = dcq.astype(BF16)
        dh_ref[:, C_CKV:C_KR] = dckv.astype(BF16)
        dh_ref[:, C_KR:C_GA] = _rope_t(dkpe_ref[...], *m_tabs, MLA_ROPE // 2).astype(BF16)
        dh_ref[:, C_GA:C_QB] = dga_ref[...]
        dh_ref[:, C_QB:C_KB] = (_rope_wide(_rope_t, dqb_ref[0].T, *d_tabs, DIL_ROT // 2) * DIL_SCALE).astype(BF16)
        dh_ref[:, C_KB:C_VB] = _rope_wide(_rope_t, dkb_ref[...], *d_tabs, DIL_ROT // 2).astype(BF16)
        dh_ref[:, C_VB:C_GB] = dvb_ref[...]
        dh_ref[:, C_GB:C_END] = dgb_ref[...]

    def tok(width):
        return pl.BlockSpec((bt, width), lambda i: (i, 0))

    def tok_t(a):
        per = a.shape[2] // bt
        return pl.BlockSpec((1, a.shape[1], bt), lambda i: (i // per, 0, i % per))

    def full(shape):
        return pl.BlockSpec(shape, lambda i: (0,) * len(shape))

    return _pcall(
        body, name="bwd_mid", grid=(seq // bt,),
        in_specs=[tok_t(dq_m), tok(HW), tok(HW), tok(LANES), tok_t(dqb), tok(HW), tok(HW), tok(HW), tok(HW),
                  tok(Q_RANK), tok(KV_RANK), tok(Q_RANK), tok(KV_RANK),
                  full(w_uq_r.shape), full(w_ukv_r.shape), full((1, Q_RANK)), full((1, KV_RANK)),
                  pl.BlockSpec((6, bt, LANES), lambda i: (0, i, 0))],
        out_specs=[tok(C_END), full(w_uq_r.shape), full(w_ukv_r.shape), full((8, Q_RANK))],
        out_shape=[jax.ShapeDtypeStruct((seq, C_END), BF16), jax.ShapeDtypeStruct(w_uq_r.shape, F32),
                   jax.ShapeDtypeStruct(w_ukv_r.shape, F32), jax.ShapeDtypeStruct((8, Q_RANK), F32)],
        compiler_params=_cparams(dimension_semantics=("arbitrary",)),
    )(dq_m, dkn, dv, dkpe, dqb, dkb, dvb, dga, dgb, cq, ckv, qn, kvn, w_uq_r, w_ukv_r, qg, kvg, tabs)


def _grad_x(dz, dh, w_in_r, bt):
    seq = dz.shape[0]

    def body(dz_ref, dh_ref, w_ref, gx_ref):
        gx_ref[...] = ALPHA * dz_ref[...] + lax.dot_general(
            dh_ref[...], w_ref[...], NT, preferred_element_type=F32)

    return _pcall(
        body, name="grad_x", grid=(seq // bt,),
        in_specs=[pl.BlockSpec((bt, D_MODEL), lambda i: (i, 0)), pl.BlockSpec((bt, C_END), lambda i: (i, 0)),
                  pl.BlockSpec(w_in_r.shape, lambda i: (0, 0))],
        out_specs=pl.BlockSpec((bt, D_MODEL), lambda i: (i, 0)),
        out_shape=jax.ShapeDtypeStruct((seq, D_MODEL), F32),
        compiler_params=_cparams(dimension_semantics=("arbitrary",)),
    )(dz, dh, w_in_r)


def _grad_w_in(x, dh, bt):
    seq = x.shape[0]

    def body(x_ref, dh_ref, gw_ref):
        @pl.when(pl.program_id(0) == 0)
        def _():
            gw_ref[...] = jnp.zeros(gw_ref.shape, F32)

        gw_ref[...] += lax.dot_general(x_ref[...].astype(BF16), dh_ref[...], TN, preferred_element_type=F32)

    return _pcall(
        body, name="grad_w_in", grid=(seq // bt,),
        in_specs=[pl.BlockSpec((bt, D_MODEL), lambda i: (i, 0)), pl.BlockSpec((bt, C_END), lambda i: (i, 0))],
        out_specs=pl.BlockSpec((D_MODEL, C_END), lambda i: (0, 0)),
        out_shape=jax.ShapeDtypeStruct((D_MODEL, C_END), F32),
        compiler_params=_cparams(dimension_semantics=("arbitrary",)),
    )(x, dh)


def _reorder_weights(w_in, w_uq, w_ukv):
    w_in_r = jnp.concatenate(
        [w_in[:, :640], jnp.tile(w_in[:, 640:672], (1, 4)), w_in[:, 672:]], axis=1)
    uq = w_uq.reshape(Q_RANK, HEADS, MLA_NOPE + MLA_ROPE)
    w_uq_r = jnp.concatenate(
        [uq[:, :, :MLA_NOPE].reshape(Q_RANK, HW), uq[:, :, MLA_NOPE:].reshape(Q_RANK, HEADS * MLA_ROPE)], axis=1)
    ukv = w_ukv.reshape(KV_RANK, HEADS, MLA_NOPE + MLA_V)
    w_ukv_r = jnp.concatenate(
        [ukv[:, :, :MLA_NOPE].reshape(KV_RANK, HW), ukv[:, :, MLA_NOPE:].reshape(KV_RANK, HW)], axis=1)
    return w_in_r, w_uq_r, w_ukv_r


def _restore_grads(g_in_r, g_uq_r, g_ukv_r):
    g_in = jnp.concatenate(
        [g_in_r[:, :640], g_in_r[:, 640:768].reshape(D_MODEL, 4, MLA_ROPE).sum(axis=1), g_in_r[:, 768:]], axis=1)
    g_uq = jnp.concatenate(
        [g_uq_r[:, :HW].reshape(Q_RANK, HEADS, MLA_NOPE), g_uq_r[:, HW:].reshape(Q_RANK, HEADS, MLA_ROPE)],
        axis=2).reshape(Q_RANK, HEADS * (MLA_NOPE + MLA_ROPE))
    g_ukv = jnp.concatenate(
        [g_ukv_r[:, :HW].reshape(KV_RANK, HEADS, MLA_NOPE), g_ukv_r[:, HW:].reshape(KV_RANK, HEADS, MLA_V)],
        axis=2).reshape(KV_RANK, HEADS * (MLA_NOPE + MLA_V))
    return g_in, g_uq, g_ukv


def _local_step(x, tgt, w_in, w_uq, w_ukv, w_out, q_norm_g, kv_norm_g, ln_g, ln_b,
                bt=BLOCK_TOKENS, blk_m=BLOCK_MLA, blk_d=BLOCK_DIL):
    seq = x.shape[0]
    tabs = jnp.asarray(_rope_tables(seq))
    w_in_r, w_uq_r, w_ukv_r = _reorder_weights(w_in, w_uq, w_ukv)
    qg, kvg = q_norm_g.reshape(1, -1), kv_norm_g.reshape(1, -1)

    (cq, ckv, qn, kvn, qcat, kn, kpe, v, ga, gb, qb, kb, vb, knt, kpet, vt, kbt, vbt) = _fwd_proj(
        x, w_in_r, w_uq_r, w_ukv_r, qg, kvg, tabs, bt)

    nq_m, nq_d = seq // blk_m, seq // blk_d
    span_d = DIL_CONFIGS[-1][0] // blk_d
    bias_m, bias_d = jnp.asarray(_mla_bias_t(blk_m)), jnp.asarray(_dil_bias_t(blk_d))
    oa, lse_a = _attn_fwd("mla_fwd", qcat, kn, kpe, vt, bias_m, _steps(nq_m, nq_m, False, True), blk_m)
    ob, lse_b = _attn_fwd("dil_fwd", qb, kb, None, vbt, bias_d, _steps(nq_d, span_d, False, False), blk_d)

    dz, doa, dob, dga, dgb, dst_a, dst_b, g_out, small1 = _out_ln(
        oa, ob, ga, gb, x, tgt, w_out, ln_g.reshape(1, -1), ln_b.reshape(1, -1), bt)

    dq_m, dkn, dkpe, dv = _attn_bwd(
        "mla_bwd", qcat, kn, kpe, v, knt, kpet, bias_m, doa, lse_a, dst_a, _steps(nq_m, nq_m, True, True), blk_m)
    dqb, dkb, dvb = _attn_bwd(
        "dil_bwd", qb, kb, None, vb, kbt, None, bias_d, dob, lse_b, dst_b, _steps(nq_d, span_d, True, False), blk_d)

    dh, g_uq_r, g_ukv_r, small2 = _bwd_mid(
        dq_m, dkn, dv, dkpe, dqb, dkb, dvb, dga, dgb, cq, ckv, qn, kvn, w_uq_r, w_ukv_r, qg, kvg, tabs, bt)
    grad_x = _grad_x(dz, dh, w_in_r, bt)
    g_in_r = _grad_w_in(x, dh, bt)
    g_in, g_uq, g_ukv = _restore_grads(g_in_r, g_uq_r, g_ukv_r)

    loss = (0.5 / D_MODEL) * jnp.sum(small1[2])
    return loss, grad_x, g_in, g_uq, g_ukv, g_out, small2[0], small2[1, :KV_RANK], small1[0], small1[1]


MESH_ID = pl.DeviceIdType.MESH
SHARD_SHAPES = ((D_MODEL, IN_WIDTH // N_DEV), (Q_RANK, 768 // N_DEV), (KV_RANK, 1024 // N_DEV), (D_MODEL // N_DEV, D_MODEL))
ADAM_ROWS = (32, 128, 128, 16)


def _me():
    x, y, c = lax.axis_index("x"), lax.axis_index("y"), lax.axis_index("c")
    return x, y, c, 4 * x + 2 * y + c


def _peer(k):
    x, y, c, _ = _me()
    px = 1 - x if (k >> 2) & 1 else x
    py = 1 - y if (k >> 1) & 1 else y
    pc = 1 - c if k & 1 else c
    return (px, py, pc), 4 * px + 2 * py + pc


def _all_gather_weights(shards):
    def body(*refs):
        ins, outs = refs[:4], refs[4:8]
        send_sems, recv_sems = refs[8:]
        x, y, c, me = _me()
        here, sibling = (x, y, c), (x, y, 1 - c)
        chips = [(1 - x, y), (x, 1 - y), (1 - x, 1 - y)]
        for t in range(4):
            outs[t][me] = ins[t][...].astype(BF16)

        def copy(t, k, px, py, pc, to):
            blk = outs[t].at[4 * px + 2 * py + pc]
            return pltpu.make_async_remote_copy(
                src_ref=blk, dst_ref=blk, send_sem=send_sems.at[t, k], recv_sem=recv_sems.at[t, k],
                device_id=to, device_id_type=MESH_ID)

        first = []
        for t in range(4):
            first.append(copy(t, 0, x, y, c, sibling))
            for j, (px, py) in enumerate(chips):
                first.append(copy(t, 1 + j, x, y, c, (px, py, c)))
        for cp in first:
            cp.start()
        passed = []
        for j, (px, py) in enumerate(chips):
            for t in range(4):
                copy(t, 1 + j, px, py, c, here).wait_recv()
                cp = copy(t, 4 + j, px, py, c, sibling)
                cp.start()
                passed.append(cp)
        for t in range(4):
            copy(t, 0, x, y, 1 - c, here).wait_recv()
        for j, (px, py) in enumerate(chips):
            for t in range(4):
                copy(t, 4 + j, px, py, 1 - c, here).wait_recv()
        for cp in first + passed:
            cp.wait_send()

    vmem = pl.BlockSpec(memory_space=pltpu.VMEM)
    return _pcall(
        body, name="gather_weights",
        in_specs=[vmem] * 4, out_specs=[vmem] * 4,
        out_shape=[jax.ShapeDtypeStruct((N_DEV,) + s, BF16) for s in SHARD_SHAPES],
        scratch_shapes=[pltpu.SemaphoreType.DMA((4, N_DEV - 1)), pltpu.SemaphoreType.DMA((4, N_DEV - 1))],
        compiler_params=_cparams(),
    )(*shards)


def _adamw(w, g, m, v):
    m = ADAM_B1 * m + (1.0 - ADAM_B1) * g
    v = ADAM_B2 * v + (1.0 - ADAM_B2) * jnp.square(g)
    m_hat = m / (1.0 - ADAM_B1 ** ADAM_STEP)
    v_hat = v / (1.0 - ADAM_B2 ** ADAM_STEP)
    delta = -ADAM_LR * (m_hat / (jnp.sqrt(v_hat) + ADAM_EPS) + ADAM_WD * w)
    return delta, m, v


def _reduce_adam(grads3, small_part, wmv, small_wmv):
    def body(*refs):
        g3 = refs[0:4]
        sp_ref = refs[4]
        wmv_refs = [refs[5 + 3 * t:8 + 3 * t] for t in range(4)]
        swmv_ref = refs[17]
        out_refs = [refs[18 + 4 * t:22 + 4 * t] for t in range(4)]
        sout_ref = refs[34]
        recv = refs[35:39]
        rsmall = refs[39]
        send_sems, recv_sems, local_sems = refs[40:43]
        me = _me()[3]

        rsmall[0] = sp_ref[...]
        local = []
        for t in range(4):
            cp = pltpu.make_async_copy(g3[t].at[me], recv[t].at[0], local_sems.at[t])
            cp.start()
            local.append(cp)
        sends = []
        for k in range(1, N_DEV):
            peer, pidx = _peer(k)
            for t in range(5):
                src = rsmall.at[0] if t == 4 else g3[t].at[pidx]
                dst = rsmall.at[k] if t == 4 else recv[t].at[k]
                cp = pltpu.make_async_remote_copy(
                    src_ref=src, dst_ref=dst, send_sem=send_sems.at[t, k - 1], recv_sem=recv_sems.at[t, k - 1],
                    device_id=peer, device_id_type=MESH_ID)
                cp.start()
                sends.append(cp)
        for cp in local:
            cp.wait()
        for cp in sends:
            cp.wait_recv()

        tot = rsmall[me]
        for d in range(1, N_DEV):
            tot = tot + rsmall[jnp.bitwise_xor(me, d)]
        delta, m, v = _adamw(swmv_ref[0], tot, swmv_ref[1], swmv_ref[2])
        sout_ref[0], sout_ref[1], sout_ref[2], sout_ref[3] = tot, delta, m, v

        for t in range(4):
            rows = ADAM_ROWS[t]
            w_ref, m_ref, v_ref = wmv_refs[t]
            g_out, d_out, m_out, v_out = out_refs[t]

            def step(i, carry, t=t, rows=rows, w_ref=w_ref, m_ref=m_ref, v_ref=v_ref,
                     g_out=g_out, d_out=d_out, m_out=m_out, v_out=v_out):
                r = pl.ds(pl.multiple_of(i * rows, rows), rows)
                g = recv[t][0, r, :].astype(F32)
                for k in range(1, N_DEV):
                    g = g + recv[t][k, r, :].astype(F32)
                delta, m, v = _adamw(w_ref[r, :], g, m_ref[r, :], v_ref[r, :])
                g_out[r, :], d_out[r, :], m_out[r, :], v_out[r, :] = g, delta, m, v
                return carry

            lax.fori_loop(0, SHARD_SHAPES[t][0] // rows, step, 0)

        for cp in sends:
            cp.wait_send()

    vmem = pl.BlockSpec(memory_space=pltpu.VMEM)
    hbm = pl.BlockSpec(memory_space=pl.ANY)
    flat_wmv = [a for trio in wmv for a in trio]
    return _pcall(
        body, name="reduce_adam",
        in_specs=[hbm] * 4 + [vmem] * 14,
        out_specs=[vmem] * 17,
        out_shape=[jax.ShapeDtypeStruct(s, F32) for s in SHARD_SHAPES for _ in range(4)]
        + [jax.ShapeDtypeStruct((4, 8, D_MODEL), F32)],
        scratch_shapes=[pltpu.VMEM((N_DEV,) + s, BF16) for s in SHARD_SHAPES]
        + [pltpu.VMEM((N_DEV, 8, D_MODEL), F32),
           pltpu.SemaphoreType.DMA((5, N_DEV - 1)), pltpu.SemaphoreType.DMA((5, N_DEV - 1)),
           pltpu.SemaphoreType.DMA((4,))],
        compiler_params=_cparams(),
    )(*grads3, small_part, *flat_wmv, small_wmv)


def _small_rows(ln_g, ln_b, q_norm_g, kv_norm_g):
    pad = lambda a: jnp.pad(a, (0, D_MODEL - a.shape[0]))
    return jnp.pad(jnp.stack([ln_g, ln_b, pad(q_norm_g), pad(kv_norm_g)]), ((0, 4), (0, 0)))


def kernel(x, w_in, q_norm_g, kv_norm_g, w_uq, w_ukv, w_out, ln_g, ln_b, loss_target, m_w_in, m_q_norm_g, m_kv_norm_g, m_w_uq, m_w_ukv, m_w_out, m_ln_g, m_ln_b, v_w_in, v_q_norm_g, v_kv_norm_g, v_w_uq, v_w_ukv, v_w_out, v_ln_g, v_ln_b):
    a_in, a_uq, a_ukv, a_out = _all_gather_weights([w_in, w_uq, w_ukv, w_out])
    full_in = a_in.transpose(1, 0, 2).reshape(D_MODEL, IN_WIDTH)
    full_uq = a_uq.transpose(1, 0, 2).reshape(Q_RANK, 768)
    full_ukv = a_ukv.transpose(1, 0, 2).reshape(KV_RANK, 1024)
    full_out = a_out.reshape(D_MODEL, D_MODEL)

    loss_l, grad_x, g_in, g_uq, g_ukv, g_out, g_qg, g_kvg, g_lng, g_lnb = _local_step(
        x[0], loss_target[0], full_in, full_uq, full_ukv, full_out, q_norm_g, kv_norm_g, ln_g, ln_b)
    loss = lax.psum(loss_l, ("x", "y", "c"))

    grads3 = [g_in.reshape(D_MODEL, N_DEV, -1).transpose(1, 0, 2), g_uq.reshape(Q_RANK, N_DEV, -1).transpose(1, 0, 2),
              g_ukv.reshape(KV_RANK, N_DEV, -1).transpose(1, 0, 2), g_out.reshape(N_DEV, D_MODEL // N_DEV, D_MODEL)]
    grads3 = [g.astype(BF16) for g in grads3]
    small_part = _small_rows(g_lng, g_lnb, g_qg, g_kvg)
    small_wmv = jnp.stack([_small_rows(ln_g, ln_b, q_norm_g, kv_norm_g),
                           _small_rows(m_ln_g, m_ln_b, m_q_norm_g, m_kv_norm_g),
                           _small_rows(v_ln_g, v_ln_b, v_q_norm_g, v_kv_norm_g)])
    wmv = [(w_in, m_w_in, v_w_in), (w_uq, m_w_uq, v_w_uq), (w_ukv, m_w_ukv, v_w_ukv), (w_out, m_w_out, v_w_out)]
    res = _reduce_adam(grads3, small_part, wmv, small_wmv)
    big = [res[4 * t:4 * t + 4] for t in range(4)]
    small = res[16]

    def group(kind):
        s = small[kind]
        return (big[0][kind], s[2, :Q_RANK], s[3, :KV_RANK], big[1][kind], big[2][kind], big[3][kind], s[0], s[1])

    return (loss, grad_x[None], *group(0), *group(1), *group(2), *group(3))
```

```python
import numpy as np
import jax
import jax.numpy as jnp
from jax import lax
from jax.experimental import pallas as pl
from jax.experimental.pallas import tpu as pltpu

F32 = jnp.float32
BF16 = jnp.bfloat16

D_MODEL = 1024
ROPE_THETA = 500000.0
NEG = -1e30
RMS_EPS = 1e-6
LN_EPS = 1e-5
HEADS = 8
MLA_NOPE = 64
MLA_ROPE = 32
MLA_V = 64
Q_RANK = 384
KV_RANK = 256
DIL_HEAD = 64
DIL_ROT = 16
DIL_CONFIGS = ((128, 1), (512, 4), (2048, 16))
HW = HEADS * 64
QW = HW + HEADS * MLA_ROPE
IN_SPLITS = (Q_RANK, KV_RANK, MLA_ROPE, HW, HW, HW, HW, HW)
IN_WIDTH = sum(IN_SPLITS)
ALPHA = 2.0 ** 0.25
MLA_SCALE = (MLA_NOPE + MLA_ROPE) ** -0.5
DIL_SCALE = DIL_HEAD ** -0.5
LOG2E = 1.4426950408889634
LN2 = 0.6931471805599453

ADAM_LR = 0.001
ADAM_B1 = 0.9
ADAM_B2 = 0.999
ADAM_EPS = 1e-08
ADAM_WD = 0.01
ADAM_STEP = 10

N_DEV = 8
LANES = 128
VMEM_LIMIT = 56 * 1024 * 1024
BLOCK_TOKENS = 256
BLOCK_MLA = 512
BLOCK_DIL = 512

C_CQ, C_CKV, C_KR, C_GA, C_QB, C_KB, C_VB, C_GB, C_END = 0, 384, 640, 768, 1280, 1792, 2304, 2816, 3328

NT = (((1,), (1,)), ((), ()))
TN = (((0,), (0,)), ((), ()))


def _pcall(body, **kw):
    return pl.pallas_call(body, **kw)


def _cparams(**kw):
    return pltpu.CompilerParams(vmem_limit_bytes=VMEM_LIMIT, **kw)


def _rope_tables(seq):
    def tabs(dim, period):
        half = dim // 2
        inv = np.float32(ROPE_THETA) ** (-np.arange(0, dim, 2, dtype=np.float32) / np.float32(dim))
        ang = np.arange(seq, dtype=np.float32)[:, None] * inv.astype(np.float32)[None, :]
        cos, sin = np.cos(ang).astype(np.float32), np.sin(ang).astype(np.float32)
        j = np.arange(LANES) % period
        f = j % half
        c = np.where(j < dim, cos[:, f], np.float32(1.0))
        s1 = np.where(j < half, -sin[:, f], np.float32(0.0))
        s2 = np.where((j >= half) & (j < dim), sin[:, f], np.float32(0.0))
        return [c, s1, s2]
    return np.stack(tabs(MLA_ROPE, MLA_ROPE) + tabs(DIL_ROT, DIL_HEAD)).astype(np.float32)


def _rope(t, c, s1, s2, half):
    return t * c + pltpu.roll(t, LANES - half, 1) * s1 + pltpu.roll(t, half, 1) * s2


def _rope_t(d, c, s1, s2, half):
    return d * c + pltpu.roll(d * s1, half, 1) + pltpu.roll(d * s2, LANES - half, 1)


def _rope_wide(fn, t, c, s1, s2, half):
    return jnp.concatenate(
        [fn(t[:, i:i + LANES], c, s1, s2, half) for i in range(0, t.shape[1], LANES)], axis=1)


def _mla_bias_t(blk):
    a = np.arange(blk)
    causal = np.where(a[:, None] <= a[None, :], 0.0, NEG)
    return np.stack([np.zeros((blk, blk)), causal]).astype(np.float32)


def _dil_bias_t(blk):
    span = DIL_CONFIGS[-1][0] // blk
    a = np.arange(blk)
    out = []
    for off in range(span + 1):
        delta = blk * off + a[None, :] - a[:, None]
        mult = np.zeros((blk, blk))
        for window, dil in DIL_CONFIGS:
            mult += (delta >= 0) & (delta % dil == 0) & (delta <= window)
        out.append(np.where(mult > 0, np.log2(np.maximum(mult, 1.0)), NEG))
    return np.stack(out).astype(np.float32)


def _steps(nq, span, by_key, diag_only_bias):
    rows = []
    if by_key:
        for ki in range(nq):
            hi = min(nq - 1, ki + span)
            for qi in range(ki, hi + 1):
                rows.append((qi, ki, int(qi == ki), int(qi == hi)))
    else:
        for qi in range(nq):
            lo = max(0, qi - span)
            for ki in range(lo, qi + 1):
                rows.append((qi, ki, int(ki == lo), int(ki == qi)))
    arr = np.array(rows, dtype=np.int32)
    off = arr[:, 0] - arr[:, 1]
    bias_idx = (off == 0).astype(np.int32) if diag_only_bias else off.astype(np.int32)
    return [jnp.asarray(v) for v in (arr[:, 0], arr[:, 1], bias_idx, arr[:, 2], arr[:, 3])]


def _fwd_proj(x, w_in_r, w_uq_r, w_ukv_r, qg, kvg, tabs, bt):
    seq = x.shape[0]

    def body(x_ref, win_ref, wuq_ref, wukv_ref, qg_ref, kvg_ref, tab_ref,
             cq_ref, ckv_ref, qn_ref, kvn_ref, qcat_ref, kn_ref, kpe_ref, v_ref,
             ga_ref, gb_ref, qb_ref, kb_ref, vb_ref, knt_ref, kpet_ref, vt_ref, kbt_ref, vbt_ref):
        xb = x_ref[...].astype(BF16)

        def proj(lo, hi):
            return jnp.dot(xb, win_ref[:, lo:hi], preferred_element_type=F32)

        m_tabs = (tab_ref[0], tab_ref[1], tab_ref[2])
        d_tabs = (tab_ref[3], tab_ref[4], tab_ref[5])

        cq = proj(C_CQ, C_CKV)
        cq_ref[...] = cq
        qn = (cq * lax.rsqrt(jnp.mean(cq * cq, axis=1, keepdims=True) + RMS_EPS) * qg_ref[...]).astype(BF16)
        qn_ref[...] = qn
        q = jnp.dot(qn, wuq_ref[...], preferred_element_type=F32)
        qcat_ref[:, :HW] = (q[:, :HW] * (MLA_SCALE * LOG2E)).astype(BF16)
        qcat_ref[:, HW:] = (_rope_wide(_rope, q[:, HW:], *m_tabs, MLA_ROPE // 2) * (MLA_SCALE * LOG2E)).astype(BF16)

        ckv = proj(C_CKV, C_KR)
        ckv_ref[...] = ckv
        kvn = (ckv * lax.rsqrt(jnp.mean(ckv * ckv, axis=1, keepdims=True) + RMS_EPS) * kvg_ref[...]).astype(BF16)
        kvn_ref[...] = kvn
        kv = jnp.dot(kvn, wukv_ref[...], preferred_element_type=F32)
        kn_ref[...] = kv[:, :HW].astype(BF16)
        v_ref[...] = kv[:, HW:].astype(BF16)
        knt_ref[...] = kv[:, :HW].T.astype(BF16)
        vt_ref[...] = kv[:, HW:].T.astype(BF16)

        kpe = _rope(proj(C_KR, C_GA), *m_tabs, MLA_ROPE // 2)
        kpe_ref[...] = kpe.astype(BF16)
        kpet_ref[...] = kpe.T[:MLA_ROPE, :].astype(BF16)
        ga_ref[...] = proj(C_GA, C_QB)
        qb_ref[...] = (_rope_wide(_rope, proj(C_QB, C_KB), *d_tabs, DIL_ROT // 2) * (DIL_SCALE * LOG2E)).astype(BF16)
        kb = _rope_wide(_rope, proj(C_KB, C_VB), *d_tabs, DIL_ROT // 2)
        kb_ref[...] = kb.astype(BF16)
        kbt_ref[...] = kb.T.astype(BF16)
        vb = proj(C_VB, C_GB)
        vb_ref[...] = vb.astype(BF16)
        vbt_ref[...] = vb.T.astype(BF16)
        gb_ref[...] = proj(C_GB, C_END)

    def tok(width):
        return pl.BlockSpec((bt, width), lambda i: (i, 0))

    def tok_t(height):
        return pl.BlockSpec((height, bt), lambda i: (0, i))

    def full(a):
        return pl.BlockSpec(a.shape, lambda i: (0,) * a.ndim)

    outs = [(Q_RANK, F32), (KV_RANK, F32), (Q_RANK, BF16), (KV_RANK, BF16), (QW, BF16), (HW, BF16),
            (LANES, BF16), (HW, BF16), (HW, F32), (HW, F32), (HW, BF16), (HW, BF16), (HW, BF16)]
    outs_t = [HW, MLA_ROPE, HW, HW, HW]
    return _pcall(
        body, name="fwd_proj", grid=(seq // bt,),
        in_specs=[tok(D_MODEL), full(w_in_r), full(w_uq_r), full(w_ukv_r), full(qg), full(kvg),
                  pl.BlockSpec((6, bt, LANES), lambda i: (0, i, 0))],
        out_specs=[tok(w) for w, _ in outs] + [tok_t(h) for h in outs_t],
        out_shape=[jax.ShapeDtypeStruct((seq, w), dt) for w, dt in outs]
        + [jax.ShapeDtypeStruct((h, seq), BF16) for h in outs_t],
        compiler_params=_cparams(dimension_semantics=("arbitrary",)),
    )(x, w_in_r, w_uq_r, w_ukv_r, qg, kvg, tabs)


def _head_masks(lane, h):
    e, g = h % 2, h % 4
    me = (lane >= 64 * e) & (lane < 64 * e + 64)
    mr = (lane >= 32 * g) & (lane < 32 * g + 32)
    return me, mr


def _masked(mask, a):
    return jnp.where(mask, a, jnp.zeros_like(a))


def _pair_operands(q_ref, k_ref, kpe_ref, lane, j):
    cols = slice(LANES * j, LANES * (j + 1))
    qc = q_ref[:, cols]
    kj = k_ref[:, cols]
    kes = []
    for h in (2 * j, 2 * j + 1):
        me, mr = _head_masks(lane, h)
        ke = _masked(me, kj)
        if kpe_ref is not None:
            ke = jnp.concatenate([ke, _masked(mr, kpe_ref[...])], axis=1)
        kes.append(ke)
    if kpe_ref is not None:
        qc = jnp.concatenate([qc, q_ref[:, HW + LANES * (j // 2):HW + LANES * (j // 2 + 1)]], axis=1)
    return qc, kes


def _attn_fwd(name, q, k, kpe, vt, bias_t, steps, blk):
    seq = q.shape[0]
    mla = kpe is not None
    n_steps = int(steps[0].shape[0])

    def body(qi_r, ki_r, bi_r, fi_r, la_r, *refs):
        if mla:
            q_ref, k_ref, kpe_ref, vt_ref, b_ref, o_ref, lse_ref, m_sc, l_sc, acc_sc, st_sc = refs
        else:
            q_ref, k_ref, vt_ref, b_ref, o_ref, lse_ref, m_sc, l_sc, acc_sc, st_sc = refs
        t = pl.program_id(0)

        @pl.when(fi_r[t] == 1)
        def _():
            m_sc[...] = jnp.full(m_sc.shape, NEG, F32)
            l_sc[...] = jnp.zeros(l_sc.shape, F32)
            acc_sc[...] = jnp.zeros(acc_sc.shape, F32)

        lane = lax.broadcasted_iota(jnp.int32, (1, LANES), 1)
        ones = jnp.ones((16, blk), BF16)

        def pair_scores(j, with_bias):
            qc, kes = _pair_operands(q_ref, k_ref, kpe_ref if mla else None, lane, j)
            st = lax.dot_general(jnp.concatenate(kes, axis=0), qc, NT, preferred_element_type=F32)
            maxes = []
            for e in range(2):
                se = st[e * blk:(e + 1) * blk]
                if with_bias:
                    se = se + b_ref[0]
                st_sc[j % 2, e * blk:(e + 1) * blk] = se
                maxes.append(jnp.max(se, axis=0, keepdims=True))
            return maxes

        def softmax_pv(h, col_max):
            st = st_sc[(h // 2) % 2, (h % 2) * blk:(h % 2 + 1) * blk]
            hrow = slice(h, h + 1)
            m_prev = m_sc[hrow, :]
            m_new = jnp.maximum(m_prev, col_max)
            alpha = jnp.exp2(m_prev - m_new)
            pt = jnp.exp2(st - m_new).astype(BF16)
            m_sc[hrow, :] = m_new
            rows = slice(64 * h, 64 * h + 64)
            res = jnp.dot(jnp.concatenate([vt_ref[rows, :], ones], axis=0), pt, preferred_element_type=F32)
            acc_sc[rows, :] = alpha * acc_sc[rows, :] + res[:64]
            l_sc[hrow, :] = alpha * l_sc[hrow, :] + res[64:65]

        def step(with_bias):
            maxes = pair_scores(0, with_bias)
            for j in range(HEADS // 2):
                cur = maxes
                if j + 1 < HEADS // 2:
                    maxes = pair_scores(j + 1, with_bias)
                softmax_pv(2 * j, cur[0])
                softmax_pv(2 * j + 1, cur[1])

        if mla:
            pl.when(bi_r[t] == 1)(lambda: step(True))
            pl.when(bi_r[t] == 0)(lambda: step(False))
        else:
            step(True)

        @pl.when(la_r[t] == 1)
        def _():
            for h in range(HEADS):
                rows = slice(64 * h, 64 * h + 64)
                acc_sc[rows, :] = acc_sc[rows, :] / l_sc[h:h + 1, :]
            o_ref[...] = acc_sc[...].T
            lse_ref[...] = m_sc[...] + jnp.log2(l_sc[...])

    qmap = lambda t, qi, ki, bi, fi, la: (qi[t], 0)
    kmap = lambda t, qi, ki, bi, fi, la: (ki[t], 0)
    in_specs = [pl.BlockSpec((blk, q.shape[1]), qmap), pl.BlockSpec((blk, HW), kmap)]
    args = [q, k]
    if mla:
        in_specs.append(pl.BlockSpec((blk, LANES), kmap))
        args.append(kpe)
    in_specs += [pl.BlockSpec((HW, blk), lambda t, qi, ki, bi, fi, la: (0, ki[t])),
                 pl.BlockSpec((1, blk, blk), lambda t, qi, ki, bi, fi, la: (bi[t], 0, 0))]
    args += [vt, bias_t]
    return _pcall(
        body, name=name,
        grid_spec=pltpu.PrefetchScalarGridSpec(
            num_scalar_prefetch=5, grid=(n_steps,), in_specs=in_specs,
            out_specs=[pl.BlockSpec((blk, HW), qmap),
                       pl.BlockSpec((HEADS, blk), lambda t, qi, ki, bi, fi, la: (0, qi[t]))],
            scratch_shapes=[pltpu.VMEM((HEADS, blk), F32), pltpu.VMEM((HEADS, blk), F32),
                            pltpu.VMEM((HW, blk), F32), pltpu.VMEM((2, 2 * blk, blk), F32)]),
        out_shape=[jax.ShapeDtypeStruct((seq, HW), F32), jax.ShapeDtypeStruct((HEADS, seq), F32)],
        compiler_params=_cparams(dimension_semantics=("arbitrary",)),
    )(*steps, *args)


def _attn_bwd(name, q, k, kpe, v, kt, kpet, bias_t, do, lse, dstat, steps, blk):
    seq = q.shape[0]
    mla = kpe is not None
    qw = q.shape[1]
    n_steps = int(steps[0].shape[0])
    dk_dtype = BF16 if mla else F32

    def body(qi_r, ki_r, bi_r, fi_r, la_r, *refs):
        if mla:
            (q_ref, k_ref, kpe_ref, v_ref, kt_ref, kpet_ref, b_ref, do_ref, lse_ref, d_ref,
             dq_ref, dk_ref, dkpe_ref, dv_ref, dk_sc, dkpe_sc, dv_sc, st_sc, dpt_sc) = refs
        else:
            (q_ref, k_ref, v_ref, kt_ref, b_ref, do_ref, lse_ref, d_ref,
             dq_ref, dk_ref, dv_ref, dk_sc, dv_sc, st_sc, dpt_sc) = refs
        t = pl.program_id(0)

        @pl.when(t == 0)
        def _():
            dq_ref[...] = jnp.zeros(dq_ref.shape, F32)

        @pl.when(fi_r[t] == 1)
        def _():
            dk_sc[...] = jnp.zeros(dk_sc.shape, F32)
            dv_sc[...] = jnp.zeros(dv_sc.shape, F32)
            if mla:
                dkpe_sc[...] = jnp.zeros(dkpe_sc.shape, F32)

        qi = qi_r[t]
        lane = lax.broadcasted_iota(jnp.int32, (1, LANES), 1)

        def pair_matmuls(j):
            cols = slice(LANES * j, LANES * (j + 1))
            qc, kes = _pair_operands(q_ref, k_ref, kpe_ref if mla else None, lane, j)
            st_sc[j % 2] = lax.dot_general(jnp.concatenate(kes, axis=0), qc, NT, preferred_element_type=F32)
            vj = v_ref[:, cols]
            ves = [_masked(_head_masks(lane, h)[0], vj) for h in (2 * j, 2 * j + 1)]
            dpt_sc[j % 2] = lax.dot_general(
                jnp.concatenate(ves, axis=0), do_ref[:, cols], NT, preferred_element_type=F32)

        def pair_grads(j, with_bias):
            cols = slice(LANES * j, LANES * (j + 1))
            qj, doj = q_ref[:, cols], do_ref[:, cols]
            if mla:
                qr = q_ref[:, HW + LANES * (j // 2):HW + LANES * (j // 2 + 1)]
            pts, dsts, qms, doms = [], [], [], []
            for e in range(2):
                h = 2 * j + e
                me, mr = _head_masks(lane, h)
                st = st_sc[j % 2, e * blk:(e + 1) * blk]
                if with_bias:
                    st = st + b_ref[0]
                pt = jnp.exp2(st - lse_ref[h:h + 1, :])
                dst = (pt * (dpt_sc[j % 2, e * blk:(e + 1) * blk] - d_ref[h:h + 1, :])).astype(BF16)
                pts.append(pt.astype(BF16))
                dsts.append(dst)
                doms.append(_masked(me, doj))
                qm = _masked(me, qj)
                if mla:
                    qm = jnp.concatenate([qm, _masked(mr, qr)], axis=1)
                qms.append(qm)
                ktl = kt_ref[64 * h:64 * h + 64, :]
                if mla:
                    ktl = jnp.concatenate([ktl, kpet_ref[...]], axis=0)
                dqc = jnp.dot(ktl, dst, preferred_element_type=F32)
                dq_ref[qi, 64 * h:64 * h + 64, :] += dqc[:64]
                if mla:
                    dq_ref[qi, HW + MLA_ROPE * h:HW + MLA_ROPE * (h + 1), :] += dqc[64:]
            dv_sc[:, cols] += jnp.dot(
                jnp.concatenate(pts, axis=1), jnp.concatenate(doms, axis=0), preferred_element_type=F32)
            dkc = jnp.dot(jnp.concatenate(dsts, axis=1), jnp.concatenate(qms, axis=0), preferred_element_type=F32)
            dk_sc[:, cols] += dkc[:, :LANES]
            if mla:
                dkpe_sc[...] += dkc[:, LANES:]

        def step(with_bias):
            pair_matmuls(0)
            for j in range(HEADS // 2):
                if j + 1 < HEADS // 2:
                    pair_matmuls(j + 1)
                pair_grads(j, with_bias)

        if mla:
            pl.when(bi_r[t] == 1)(lambda: step(True))
            pl.when(bi_r[t] == 0)(lambda: step(False))
        else:
            step(True)

        @pl.when(la_r[t] == 1)
        def _():
            dk_ref[...] = (dk_sc[...] * LN2).astype(dk_ref.dtype)
            dv_ref[...] = dv_sc[...].astype(dv_ref.dtype)
            if mla:
                dkpe_ref[...] = dkpe_sc[...] * LN2

    qmap = lambda t, qi, ki, bi, fi, la: (qi[t], 0)
    kmap = lambda t, qi, ki, bi, fi, la: (ki[t], 0)
    qmap_t = lambda t, qi, ki, bi, fi, la: (0, qi[t])
    kmap_t = lambda t, qi, ki, bi, fi, la: (0, ki[t])
    in_specs = [pl.BlockSpec((blk, qw), qmap), pl.BlockSpec((blk, HW), kmap)]
    args = [q, k]
    if mla:
        in_specs.append(pl.BlockSpec((blk, LANES), kmap))
        args.append(kpe)
    in_specs += [pl.BlockSpec((blk, HW), kmap), pl.BlockSpec((HW, blk), kmap_t)]
    args += [v, kt]
    if mla:
        in_specs.append(pl.BlockSpec((MLA_ROPE, blk), kmap_t))
        args.append(kpet)
    in_specs += [pl.BlockSpec((1, blk, blk), lambda t, qi, ki, bi, fi, la: (bi[t], 0, 0)),
                 pl.BlockSpec((blk, HW), qmap), pl.BlockSpec((HEADS, blk), qmap_t), pl.BlockSpec((HEADS, blk), qmap_t)]
    args += [bias_t, do, lse, dstat]
    dq_shape = (seq // blk, qw, blk)
    out_specs = [pl.BlockSpec(dq_shape, lambda t, qi, ki, bi, fi, la: (0, 0, 0)), pl.BlockSpec((blk, HW), kmap)]
    out_shape = [jax.ShapeDtypeStruct(dq_shape, F32), jax.ShapeDtypeStruct((seq, HW), dk_dtype)]
    scratch = [pltpu.VMEM((blk, HW), F32)]
    if mla:
        out_specs.append(pl.BlockSpec((blk, LANES), kmap))
        out_shape.append(jax.ShapeDtypeStruct((seq, LANES), F32))
        scratch.append(pltpu.VMEM((blk, LANES), F32))
    out_specs.append(pl.BlockSpec((blk, HW), kmap))
    out_shape.append(jax.ShapeDtypeStruct((seq, HW), BF16))
    scratch.append(pltpu.VMEM((blk, HW), F32))
    scratch += [pltpu.VMEM((2, 2 * blk, blk), F32), pltpu.VMEM((2, 2 * blk, blk), F32)]
    return _pcall(
        body, name=name,
        grid_spec=pltpu.PrefetchScalarGridSpec(
            num_scalar_prefetch=5, grid=(n_steps,), in_specs=in_specs, out_specs=out_specs,
            scratch_shapes=scratch),
        out_shape=out_shape,
        compiler_params=_cparams(dimension_semantics=("arbitrary",)),
    )(*steps, *args)


def _out_ln(oa, ob, ga, gb, x, tgt, w_out, ln_g, ln_b, bt):
    seq = x.shape[0]

    def body(oa_ref, ob_ref, ga_ref, gb_ref, x_ref, tgt_ref, w_ref, g_ref, b_ref,
             dz_ref, doa_ref, dob_ref, dga_ref, dgb_ref, da_ref, db_ref, gw_ref, small_ref):
        i = pl.program_id(0)

        @pl.when(i == 0)
        def _():
            gw_ref[...] = jnp.zeros(gw_ref.shape, F32)
            small_ref[...] = jnp.zeros(small_ref.shape, F32)

        def gate(g):
            sig = 1.0 / (1.0 + jnp.exp(-g))
            return g * sig, sig * (1.0 + g * (1.0 - sig))

        o_a, o_b = oa_ref[...], ob_ref[...]
        g_a, g_b = ga_ref[...], gb_ref[...]
        sa, dsa = gate(g_a)
        sb, dsb = gate(g_b)
        mix = jnp.concatenate([o_a * sa, o_b * sb], axis=1).astype(BF16)
        z = ALPHA * x_ref[...] + jnp.dot(mix, w_ref[...], preferred_element_type=F32)
        mu = jnp.mean(z, axis=1, keepdims=True)
        zc = z - mu
        rstd = lax.rsqrt(jnp.mean(zc * zc, axis=1, keepdims=True) + LN_EPS)
        xhat = zc * rstd
        gam = g_ref[...]
        diff = xhat * gam + b_ref[...] - tgt_ref[...]
        dy = diff * (1.0 / D_MODEL)
        small_ref[0:1, :] += jnp.sum(dy * xhat, axis=0, keepdims=True)
        small_ref[1:2, :] += jnp.sum(dy, axis=0, keepdims=True)
        small_ref[2:3, :] += jnp.sum(diff * diff, axis=0, keepdims=True)
        dxh = dy * gam
        dz = rstd * (dxh - jnp.mean(dxh, axis=1, keepdims=True) - xhat * jnp.mean(dxh * xhat, axis=1, keepdims=True))
        dz_ref[...] = dz
        dzb = dz.astype(BF16)
        gw_ref[...] += lax.dot_general(mix, dzb, TN, preferred_element_type=F32)
        dmix = lax.dot_general(dzb, w_ref[...], NT, preferred_element_type=F32)
        doa, dob = dmix[:, :HW] * sa, dmix[:, HW:] * sb
        doa_ref[...] = doa.astype(BF16)
        dob_ref[...] = dob.astype(BF16)
        dga_ref[...] = (dmix[:, :HW] * o_a * dsa).astype(BF16)
        dgb_ref[...] = (dmix[:, HW:] * o_b * dsb).astype(BF16)
        head_of = lax.broadcasted_iota(jnp.int32, (HEADS, HW), 1) // 64
        ind = (head_of == lax.broadcasted_iota(jnp.int32, (HEADS, HW), 0)).astype(F32)
        da_ref[...] = lax.dot_general(ind, doa * o_a, NT, preferred_element_type=F32, precision=lax.Precision.HIGHEST)
        db_ref[...] = lax.dot_general(ind, dob * o_b, NT, preferred_element_type=F32, precision=lax.Precision.HIGHEST)

    def tok(width):
        return pl.BlockSpec((bt, width), lambda i: (i, 0))

    def full(shape):
        return pl.BlockSpec(shape, lambda i: (0,) * len(shape))

    stat = pl.BlockSpec((HEADS, bt), lambda i: (0, i))
    return _pcall(
        body, name="out_ln", grid=(seq // bt,),
        in_specs=[tok(HW), tok(HW), tok(HW), tok(HW), tok(D_MODEL), tok(D_MODEL),
                  full((D_MODEL, D_MODEL)), full((1, D_MODEL)), full((1, D_MODEL))],
        out_specs=[tok(D_MODEL), tok(HW), tok(HW), tok(HW), tok(HW), stat, stat,
                   full((D_MODEL, D_MODEL)), full((8, D_MODEL))],
        out_shape=[jax.ShapeDtypeStruct((seq, D_MODEL), F32)] + [jax.ShapeDtypeStruct((seq, HW), BF16)] * 4
        + [jax.ShapeDtypeStruct((HEADS, seq), F32)] * 2
        + [jax.ShapeDtypeStruct((D_MODEL, D_MODEL), F32), jax.ShapeDtypeStruct((8, D_MODEL), F32)],
        compiler_params=_cparams(dimension_semantics=("arbitrary",)),
    )(oa, ob, ga, gb, x, tgt, w_out, ln_g, ln_b)


def _bwd_mid(dq_m, dkn, dv, dkpe, dqb, dkb, dvb, dga, dgb, cq, ckv, qn, kvn, w_uq_r, w_ukv_r, qg, kvg, tabs, bt):
    seq = cq.shape[0]

    def body(dqm_ref, dkn_ref, dv_ref, dkpe_ref, dqb_ref, dkb_ref, dvb_ref, dga_ref, dgb_ref,
             cq_ref, ckv_ref, qn_ref, kvn_ref, wuq_ref, wukv_ref, qg_ref, kvg_ref, tab_ref,
             dh_ref, guq_ref, gukv_ref, small_ref):
        i = pl.program_id(0)

        @pl.when(i == 0)
        def _():
            guq_ref[...] = jnp.zeros(guq_ref.shape, F32)
            gukv_ref[...] = jnp.zeros(gukv_ref.shape, F32)
            small_ref[...] = jnp.zeros(small_ref.shape, F32)

        m_tabs = (tab_ref[0], tab_ref[1], tab_ref[2])
        d_tabs = (tab_ref[3], tab_ref[4], tab_ref[5])

        def rms_bwd(c, dn, gain):
            r = lax.rsqrt(jnp.mean(c * c, axis=1, keepdims=True) + RMS_EPS)
            u = dn * gain
            dc = r * u - c * (r * r * r) * jnp.mean(u * c, axis=1, keepdims=True)
            return dc, jnp.sum(dn * c * r, axis=0, keepdims=True)

        dqm = dqm_ref[0].T
        dq = jnp.concatenate(
            [dqm[:, :HW], _rope_wide(_rope_t, dqm[:, HW:], *m_tabs, MLA_ROPE // 2)], axis=1) * MLA_SCALE
        dq = dq.astype(BF16)
        guq_ref[...] += lax.dot_general(qn_ref[...], dq, TN, preferred_element_type=F32)
        dqn = lax.dot_general(dq, wuq_ref[...], NT, preferred_element_type=F32)
        dcq, gq = rms_bwd(cq_ref[...], dqn, qg_ref[...])
        small_ref[0:1, :] += gq

        dkv = jnp.concatenate([dkn_ref[...], dv_ref[...]], axis=1)
        gukv_ref[...] += lax.dot_general(kvn_ref[...], dkv, TN, preferred_element_type=F32)
        dkvn = lax.dot_general(dkv, wukv_ref[...], NT, preferred_element_type=F32)
        dckv, gkv = rms_bwd(ckv_ref[...], dkvn, kvg_ref[...])
        small_ref[1:2, :KV_RANK] += gkv

        dh_ref[:, C_CQ:C_CKV] = dcq.astype(BF16)
        dh_ref[:, C_CKV:C_KR] = dckv.astype(BF16)
        dh_ref[:, C_KR:C_GA] = _rope_t(dkpe_ref[...], *m_tabs, MLA_ROPE // 2).astype(BF16)
        dh_ref[:, C_GA:C_QB] = dga_ref[...]
        dh_ref[:, C_QB:C_KB] = (_rope_wide(_rope_t, dqb_ref[0].T, *d_tabs, DIL_ROT // 2) * DIL_SCALE).astype(BF16)
        dh_ref[:, C_KB:C_VB] = _rope_wide(_rope_t, dkb_ref[...], *d_tabs, DIL_ROT // 2).astype(BF16)
        dh_ref[:, C_VB:C_GB] = dvb_ref[...]
        dh_ref[:, C_GB:C_END] = dgb_ref[...]

    def tok(width):
        return pl.BlockSpec((bt, width), lambda i: (i, 0))

    def tok_t(a):
        per = a.shape[2] // bt
        return pl.BlockSpec((1, a.shape[1], bt), lambda i: (i // per, 0, i % per))

    def full(shape):
        return pl.BlockSpec(shape, lambda i: (0,) * len(shape))

    return _pcall(
        body, name="bwd_mid", grid=(seq // bt,),
        in_specs=[tok_t(dq_m), tok(HW), tok(HW), tok(LANES), tok_t(dqb), tok(HW), tok(HW), tok(HW), tok(HW),
                  tok(Q_RANK), tok(KV_RANK), tok(Q_RANK), tok(KV_RANK),
                  full(w_uq_r.shape), full(w_ukv_r.shape), full((1, Q_RANK)), full((1, KV_RANK)),
                  pl.BlockSpec((6, bt, LANES), lambda i: (0, i, 0))],
        out_specs=[tok(C_END), full(w_uq_r.shape), full(w_ukv_r.shape), full((8, Q_RANK))],
        out_shape=[jax.ShapeDtypeStruct((seq, C_END), BF16), jax.ShapeDtypeStruct(w_uq_r.shape, F32),
                   jax.ShapeDtypeStruct(w_ukv_r.shape, F32), jax.ShapeDtypeStruct((8, Q_RANK), F32)],
        compiler_params=_cparams(dimension_semantics=("arbitrary",)),
    )(dq_m, dkn, dv, dkpe, dqb, dkb, dvb, dga, dgb, cq, ckv, qn, kvn, w_uq_r, w_ukv_r, qg, kvg, tabs)


def _grad_x(dz, dh, w_in_r, bt):
    seq = dz.shape[0]

    def body(dz_ref, dh_ref, w_ref, gx_ref):
        gx_ref[...] = ALPHA * dz_ref[...] + lax.dot_general(
            dh_ref[...], w_ref[...], NT, preferred_element_type=F32)

    return _pcall(
        body, name="grad_x", grid=(seq // bt,),
        in_specs=[pl.BlockSpec((bt, D_MODEL), lambda i: (i, 0)), pl.BlockSpec((bt, C_END), lambda i: (i, 0)),
                  pl.BlockSpec(w_in_r.shape, lambda i: (0, 0))],
        out_specs=pl.BlockSpec((bt, D_MODEL), lambda i: (i, 0)),
        out_shape=jax.ShapeDtypeStruct((seq, D_MODEL), F32),
        compiler_params=_cparams(dimension_semantics=("arbitrary",)),
    )(dz, dh, w_in_r)


def _grad_w_in(x, dh, bt):
    seq = x.shape[0]

    def body(x_ref, dh_ref, gw_ref):
        @pl.when(pl.program_id(0) == 0)
        def _():
            gw_ref[...] = jnp.zeros(gw_ref.shape, F32)

        gw_ref[...] += lax.dot_general(x_ref[...].astype(BF16), dh_ref[...], TN, preferred_element_type=F32)

    return _pcall(
        body, name="grad_w_in", grid=(seq // bt,),
        in_specs=[pl.BlockSpec((bt, D_MODEL), lambda i: (i, 0)), pl.BlockSpec((bt, C_END), lambda i: (i, 0))],
        out_specs=pl.BlockSpec((D_MODEL, C_END), lambda i: (0, 0)),
        out_shape=jax.ShapeDtypeStruct((D_MODEL, C_END), F32),
        compiler_params=_cparams(dimension_semantics=("arbitrary",)),
    )(x, dh)


def _reorder_weights(w_in, w_uq, w_ukv):
    w_in_r = jnp.concatenate(
        [w_in[:, :640], jnp.tile(w_in[:, 640:672], (1, 4)), w_in[:, 672:]], axis=1)
    uq = w_uq.reshape(Q_RANK, HEADS, MLA_NOPE + MLA_ROPE)
    w_uq_r = jnp.concatenate(
        [uq[:, :, :MLA_NOPE].reshape(Q_RANK, HW), uq[:, :, MLA_NOPE:].reshape(Q_RANK, HEADS * MLA_ROPE)], axis=1)
    ukv = w_ukv.reshape(KV_RANK, HEADS, MLA_NOPE + MLA_V)
    w_ukv_r = jnp.concatenate(
        [ukv[:, :, :MLA_NOPE].reshape(KV_RANK, HW), ukv[:, :, MLA_NOPE:].reshape(KV_RANK, HW)], axis=1)
    return w_in_r, w_uq_r, w_ukv_r


def _restore_grads(g_in_r, g_uq_r, g_ukv_r):
    g_in = jnp.concatenate(
        [g_in_r[:, :640], g_in_r[:, 640:768].reshape(D_MODEL, 4, MLA_ROPE).sum(axis=1), g_in_r[:, 768:]], axis=1)
    g_uq = jnp.concatenate(
        [g_uq_r[:, :HW].reshape(Q_RANK, HEADS, MLA_NOPE), g_uq_r[:, HW:].reshape(Q_RANK, HEADS, MLA_ROPE)],
        axis=2).reshape(Q_RANK, HEADS * (MLA_NOPE + MLA_ROPE))
    g_ukv = jnp.concatenate(
        [g_ukv_r[:, :HW].reshape(KV_RANK, HEADS, MLA_NOPE), g_ukv_r[:, HW:].reshape(KV_RANK, HEADS, MLA_V)],
        axis=2).reshape(KV_RANK, HEADS * (MLA_NOPE + MLA_V))
    return g_in, g_uq, g_ukv


def _local_step(x, tgt, w_in, w_uq, w_ukv, w_out, q_norm_g, kv_norm_g, ln_g, ln_b,
                bt=BLOCK_TOKENS, blk_m=BLOCK_MLA, blk_d=BLOCK_DIL):
    seq = x.shape[0]
    tabs = jnp.asarray(_rope_tables(seq))
    w_in_r, w_uq_r, w_ukv_r = _reorder_weights(w_in, w_uq, w_ukv)
    qg, kvg = q_norm_g.reshape(1, -1), kv_norm_g.reshape(1, -1)

    (cq, ckv, qn, kvn, qcat, kn, kpe, v, ga, gb, qb, kb, vb, knt, kpet, vt, kbt, vbt) = _fwd_proj(
        x, w_in_r, w_uq_r, w_ukv_r, qg, kvg, tabs, bt)

    nq_m, nq_d = seq // blk_m, seq // blk_d
    span_d = DIL_CONFIGS[-1][0] // blk_d
    bias_m, bias_d = jnp.asarray(_mla_bias_t(blk_m)), jnp.asarray(_dil_bias_t(blk_d))
    oa, lse_a = _attn_fwd("mla_fwd", qcat, kn, kpe, vt, bias_m, _steps(nq_m, nq_m, False, True), blk_m)
    ob, lse_b = _attn_fwd("dil_fwd", qb, kb, None, vbt, bias_d, _steps(nq_d, span_d, False, False), blk_d)

    dz, doa, dob, dga, dgb, dst_a, dst_b, g_out, small1 = _out_ln(
        oa, ob, ga, gb, x, tgt, w_out, ln_g.reshape(1, -1), ln_b.reshape(1, -1), bt)

    dq_m, dkn, dkpe, dv = _attn_bwd(
        "mla_bwd", qcat, kn, kpe, v, knt, kpet, bias_m, doa, lse_a, dst_a, _steps(nq_m, nq_m, True, True), blk_m)
    dqb, dkb, dvb = _attn_bwd(
        "dil_bwd", qb, kb, None, vb, kbt, None, bias_d, dob, lse_b, dst_b, _steps(nq_d, span_d, True, False), blk_d)

    dh, g_uq_r, g_ukv_r, small2 = _bwd_mid(
        dq_m, dkn, dv, dkpe, dqb, dkb, dvb, dga, dgb, cq, ckv, qn, kvn, w_uq_r, w_ukv_r, qg, kvg, tabs, bt)
    grad_x = _grad_x(dz, dh, w_in_r, bt)
    g_in_r = _grad_w_in(x, dh, bt)
    g_in, g_uq, g_ukv = _restore_grads(g_in_r, g_uq_r, g_ukv_r)

    loss = (0.5 / D_MODEL) * jnp.sum(small1[2])
    return loss, grad_x, g_in, g_uq, g_ukv, g_out, small2[0], small2[1, :KV_RANK], small1[0], small1[1]


MESH_ID = pl.DeviceIdType.MESH
SHARD_SHAPES = ((D_MODEL, IN_WIDTH // N_DEV), (Q_RANK, 768 // N_DEV), (KV_RANK, 1024 // N_DEV), (D_MODEL // N_DEV, D_MODEL))
ADAM_ROWS = (32, 128, 128, 16)


def _me():
    x, y, c = lax.axis_index("x"), lax.axis_index("y"), lax.axis_index("c")
    return x, y, c, 4 * x + 2 * y + c


def _peer(k):
    x, y, c, _ = _me()
    px = 1 - x if (k >> 2) & 1 else x
    py = 1 - y if (k >> 1) & 1 else y
    pc = 1 - c if k & 1 else c
    return (px, py, pc), 4 * px + 2 * py + pc


def _all_gather_weights(shards):
    def body(*refs):
        ins, outs = refs[:4], refs[4:8]
        send_sems, recv_sems = refs[8:]
        x, y, c, me = _me()
        here, sibling = (x, y, c), (x, y, 1 - c)
        chips = [(1 - x, y), (x, 1 - y), (1 - x, 1 - y)]
        for t in range(4):
            outs[t][me] = ins[t][...].astype(BF16)

        def copy(t, k, px, py, pc, to):
            blk = outs[t].at[4 * px + 2 * py + pc]
            return pltpu.make_async_remote_copy(
                src_ref=blk, dst_ref=blk, send_sem=send_sems.at[t, k], recv_sem=recv_sems.at[t, k],
                device_id=to, device_id_type=MESH_ID)

        first = []
        for t in range(4):
            first.append(copy(t, 0, x, y, c, sibling))
            for j, (px, py) in enumerate(chips):
                first.append(copy(t, 1 + j, x, y, c, (px, py, c)))
        for cp in first:
            cp.start()
        passed = []
        for j, (px, py) in enumerate(chips):
            for t in range(4):
                copy(t, 1 + j, px, py, c, here).wait_recv()
                cp = copy(t, 4 + j, px, py, c, sibling)
                cp.start()
                passed.append(cp)
        for t in range(4):
            copy(t, 0, x, y, 1 - c, here).wait_recv()
        for j, (px, py) in enumerate(chips):
            for t in range(4):
                copy(t, 4 + j, px, py, 1 - c, here).wait_recv()
        for cp in first + passed:
            cp.wait_send()

    vmem = pl.BlockSpec(memory_space=pltpu.VMEM)
    return _pcall(
        body, name="gather_weights",
        in_specs=[vmem] * 4, out_specs=[vmem] * 4,
        out_shape=[jax.ShapeDtypeStruct((N_DEV,) + s, BF16) for s in SHARD_SHAPES],
        scratch_shapes=[pltpu.SemaphoreType.DMA((4, N_DEV - 1)), pltpu.SemaphoreType.DMA((4, N_DEV - 1))],
        compiler_params=_cparams(),
    )(*shards)


def _adamw(w, g, m, v):
    m = ADAM_B1 * m + (1.0 - ADAM_B1) * g
    v = ADAM_B2 * v + (1.0 - ADAM_B2) * jnp.square(g)
    m_hat = m / (1.0 - ADAM_B1 ** ADAM_STEP)
    v_hat = v / (1.0 - ADAM_B2 ** ADAM_STEP)
    delta = -ADAM_LR * (m_hat / (jnp.sqrt(v_hat) + ADAM_EPS) + ADAM_WD * w)
    return delta, m, v


def _reduce_adam(grads3, small_part, wmv, small_wmv):
    def body(*refs):
        g3 = refs[0:4]
        sp_ref = refs[4]
        wmv_refs = [refs[5 + 3 * t:8 + 3 * t] for t in range(4)]
        swmv_ref = refs[17]
        out_refs = [refs[18 + 4 * t:22 + 4 * t] for t in range(4)]
        sout_ref = refs[34]
        recv = refs[35:39]
        rsmall = refs[39]
        send_sems, recv_sems, local_sems = refs[40:43]
        me = _me()[3]

        rsmall[0] = sp_ref[...]
        local = []
        for t in range(4):
            cp = pltpu.make_async_copy(g3[t].at[me], recv[t].at[0], local_sems.at[t])
            cp.start()
            local.append(cp)
        sends = []
        for k in range(1, N_DEV):
            peer, pidx = _peer(k)
            for t in range(5):
                src = rsmall.at[0] if t == 4 else g3[t].at[pidx]
                dst = rsmall.at[k] if t == 4 else recv[t].at[k]
                cp = pltpu.make_async_remote_copy(
                    src_ref=src, dst_ref=dst, send_sem=send_sems.at[t, k - 1], recv_sem=recv_sems.at[t, k - 1],
                    device_id=peer, device_id_type=MESH_ID)
                cp.start()
                sends.append(cp)
        for cp in local:
            cp.wait()
        for cp in sends:
            cp.wait_recv()

        tot = rsmall[me]
        for d in range(1, N_DEV):
            tot = tot + rsmall[jnp.bitwise_xor(me, d)]
        delta, m, v = _adamw(swmv_ref[0], tot, swmv_ref[1], swmv_ref[2])
        sout_ref[0], sout_ref[1], sout_ref[2], sout_ref[3] = tot, delta, m, v

        for t in range(4):
            rows = ADAM_ROWS[t]
            w_ref, m_ref, v_ref = wmv_refs[t]
            g_out, d_out, m_out, v_out = out_refs[t]

            def step(i, carry, t=t, rows=rows, w_ref=w_ref, m_ref=m_ref, v_ref=v_ref,
                     g_out=g_out, d_out=d_out, m_out=m_out, v_out=v_out):
                r = pl.ds(pl.multiple_of(i * rows, rows), rows)
                g = recv[t][0, r, :].astype(F32)
                for k in range(1, N_DEV):
                    g = g + recv[t][k, r, :].astype(F32)
                delta, m, v = _adamw(w_ref[r, :], g, m_ref[r, :], v_ref[r, :])
                g_out[r, :], d_out[r, :], m_out[r, :], v_out[r, :] = g, delta, m, v
                return carry

            lax.fori_loop(0, SHARD_SHAPES[t][0] // rows, step, 0)

        for cp in sends:
            cp.wait_send()

    vmem = pl.BlockSpec(memory_space=pltpu.VMEM)
    hbm = pl.BlockSpec(memory_space=pl.ANY)
    flat_wmv = [a for trio in wmv for a in trio]
    return _pcall(
        body, name="reduce_adam",
        in_specs=[hbm] * 4 + [vmem] * 14,
        out_specs=[vmem] * 17,
        out_shape=[jax.ShapeDtypeStruct(s, F32) for s in SHARD_SHAPES for _ in range(4)]
        + [jax.ShapeDtypeStruct((4, 8, D_MODEL), F32)],
        scratch_shapes=[pltpu.VMEM((N_DEV,) + s, BF16) for s in SHARD_SHAPES]
        + [pltpu.VMEM((N_DEV, 8, D_MODEL), F32),
           pltpu.SemaphoreType.DMA((5, N_DEV - 1)), pltpu.SemaphoreType.DMA((5, N_DEV - 1)),
           pltpu.SemaphoreType.DMA((4,))],
        compiler_params=_cparams(),
    )(*grads3, small_part, *flat_wmv, small_wmv)


def _small_rows(ln_g, ln_b, q_norm_g, kv_norm_g):
    pad = lambda a: jnp.pad(a, (0, D_MODEL - a.shape[0]))
    return jnp.pad(jnp.stack([ln_g, ln_b, pad(q_norm_g), pad(kv_norm_g)]), ((0, 4), (0, 0)))


def kernel(x, w_in, q_norm_g, kv_norm_g, w_uq, w_ukv, w_out, ln_g, ln_b, loss_target, m_w_in, m_q_norm_g, m_kv_norm_g, m_w_uq, m_w_ukv, m_w_out, m_ln_g, m_ln_b, v_w_in, v_q_norm_g, v_kv_norm_g, v_w_uq, v_w_ukv, v_w_out, v_ln_g, v_ln_b):
    a_in, a_uq, a_ukv, a_out = _all_gather_weights([w_in, w_uq, w_ukv, w_out])
    full_in = a_in.transpose(1, 0, 2).reshape(D_MODEL, IN_WIDTH)
    full_uq = a_uq.transpose(1, 0, 2).reshape(Q_RANK, 768)
    full_ukv = a_ukv.transpose(1, 0, 2).reshape(KV_RANK, 1024)
    full_out = a_out.reshape(D_MODEL, D_MODEL)

    loss_l, grad_x, g_in, g_uq, g_ukv, g_out, g_qg, g_kvg, g_lng, g_lnb = _local_step(
        x[0], loss_target[0], full_in, full_uq, full_ukv, full_out, q_norm_g, kv_norm_g, ln_g, ln_b)
    loss = lax.psum(loss_l, ("x", "y", "c"))

    grads3 = [g_in.reshape(D_MODEL, N_DEV, -1).transpose(1, 0, 2), g_uq.reshape(Q_RANK, N_DEV, -1).transpose(1, 0, 2),
              g_ukv.reshape(KV_RANK, N_DEV, -1).transpose(1, 0, 2), g_out.reshape(N_DEV, D_MODEL // N_DEV, D_MODEL)]
    grads3 = [g.astype(BF16) for g in grads3]
    small_part = _small_rows(g_lng, g_lnb, g_qg, g_kvg)
    small_wmv = jnp.stack([_small_rows(ln_g, ln_b, q_norm_g, kv_norm_g),
                           _small_rows(m_ln_g, m_ln_b, m_q_norm_g, m_kv_norm_g),
                           _small_rows(v_ln_g, v_ln_b, v_q_norm_g, v_kv_norm_g)])
    wmv = [(w_in, m_w_in, v_w_in), (w_uq, m_w_uq, v_w_uq), (w_ukv, m_w_ukv, v_w_ukv), (w_out, m_w_out, v_w_out)]
    res = _reduce_adam(grads3, small_part, wmv, small_wmv)
    big = [res[4 * t:4 * t + 4] for t in range(4)]
    small = res[16]

    def group(kind):
        s = small[kind]
        return (big[0][kind], s[2, :Q_RANK], s[3, :KV_RANK], big[1][kind], big[2][kind], big[3][kind], s[0], s[1])

    return (loss, grad_x[None], *group(0), *group(1), *group(2), *group(3))
```

```python
import numpy as np
import jax
import jax.numpy as jnp
from jax import lax
from jax.experimental import pallas as pl
from jax.experimental.pallas import tpu as pltpu

F32 = jnp.float32
BF16 = jnp.bfloat16

D_MODEL = 1024
ROPE_THETA = 500000.0
NEG = -1e30
RMS_EPS = 1e-6
LN_EPS = 1e-5
HEADS = 8
MLA_NOPE = 64
MLA_ROPE = 32
MLA_V = 64
Q_RANK = 384
KV_RANK = 256
DIL_HEAD = 64
DIL_ROT = 16
DIL_CONFIGS = ((128, 1), (512, 4), (2048, 16))
HW = HEADS * 64
QW = HW + HEADS * MLA_ROPE
IN_SPLITS = (Q_RANK, KV_RANK, MLA_ROPE, HW, HW, HW, HW, HW)
IN_WIDTH = sum(IN_SPLITS)
ALPHA = 2.0 ** 0.25
MLA_SCALE = (MLA_NOPE + MLA_ROPE) ** -0.5
DIL_SCALE = DIL_HEAD ** -0.5
LOG2E = 1.4426950408889634
LN2 = 0.6931471805599453

ADAM_LR = 0.001
ADAM_B1 = 0.9
ADAM_B2 = 0.999
ADAM_EPS = 1e-08
ADAM_WD = 0.01
ADAM_STEP = 10

N_DEV = 8
LANES = 128
VMEM_LIMIT = 56 * 1024 * 1024
BLOCK_TOKENS = 512
BLOCK_MLA = 512
BLOCK_DIL = 512

C_CQ, C_CKV, C_KR, C_GA, C_QB, C_KB, C_VB, C_GB, C_END = 0, 384, 640, 768, 1280, 1792, 2304, 2816, 3328

NT = (((1,), (1,)), ((), ()))
TN = (((0,), (0,)), ((), ()))


def _pcall(body, **kw):
    return pl.pallas_call(body, **kw)


def _cparams(**kw):
    return pltpu.CompilerParams(vmem_limit_bytes=VMEM_LIMIT, **kw)


def _rope_tables(seq):
    def tabs(dim, period):
        half = dim // 2
        inv = np.float32(ROPE_THETA) ** (-np.arange(0, dim, 2, dtype=np.float32) / np.float32(dim))
        ang = np.arange(seq, dtype=np.float32)[:, None] * inv.astype(np.float32)[None, :]
        cos, sin = np.cos(ang).astype(np.float32), np.sin(ang).astype(np.float32)
        j = np.arange(LANES) % period
        f = j % half
        c = np.where(j < dim, cos[:, f], np.float32(1.0))
        s1 = np.where(j < half, -sin[:, f], np.float32(0.0))
        s2 = np.where((j >= half) & (j < dim), sin[:, f], np.float32(0.0))
        return [c, s1, s2]
    return np.stack(tabs(MLA_ROPE, MLA_ROPE) + tabs(DIL_ROT, DIL_HEAD)).astype(np.float32)


def _rope(t, c, s1, s2, half):
    return t * c + pltpu.roll(t, LANES - half, 1) * s1 + pltpu.roll(t, half, 1) * s2


def _rope_t(d, c, s1, s2, half):
    return d * c + pltpu.roll(d * s1, half, 1) + pltpu.roll(d * s2, LANES - half, 1)


def _rope_wide(fn, t, c, s1, s2, half):
    return jnp.concatenate(
        [fn(t[:, i:i + LANES], c, s1, s2, half) for i in range(0, t.shape[1], LANES)], axis=1)


def _mla_bias_t(blk):
    a = np.arange(blk)
    causal = np.where(a[:, None] <= a[None, :], 0.0, NEG)
    return np.stack([np.zeros((blk, blk)), causal]).astype(np.float32)


def _dil_bias_t(blk):
    span = DIL_CONFIGS[-1][0] // blk
    a = np.arange(blk)
    out = []
    for off in range(span + 1):
        delta = blk * off + a[None, :] - a[:, None]
        mult = np.zeros((blk, blk))
        for window, dil in DIL_CONFIGS:
            mult += (delta >= 0) & (delta % dil == 0) & (delta <= window)
        out.append(np.where(mult > 0, np.log2(np.maximum(mult, 1.0)), NEG))
    return np.stack(out).astype(np.float32)


def _steps(nq, span, by_key, diag_only_bias):
    rows = []
    if by_key:
        for ki in range(nq):
            hi = min(nq - 1, ki + span)
            for qi in range(ki, hi + 1):
                rows.append((qi, ki, int(qi == ki), int(qi == hi)))
    else:
        for qi in range(nq):
            lo = max(0, qi - span)
            for ki in range(lo, qi + 1):
                rows.append((qi, ki, int(ki == lo), int(ki == qi)))
    arr = np.array(rows, dtype=np.int32)
    off = arr[:, 0] - arr[:, 1]
    bias_idx = (off == 0).astype(np.int32) if diag_only_bias else off.astype(np.int32)
    return [jnp.asarray(v) for v in (arr[:, 0], arr[:, 1], bias_idx, arr[:, 2], arr[:, 3])]


def _fwd_proj(x, w_in_r, w_uq_r, w_ukv_r, qg, kvg, tabs, bt):
    seq = x.shape[0]

    def body(x_ref, win_ref, wuq_ref, wukv_ref, qg_ref, kvg_ref, tab_ref,
             cq_ref, ckv_ref, qn_ref, kvn_ref, qcat_ref, kn_ref, kpe_ref, v_ref,
             ga_ref, gb_ref, qb_ref, kb_ref, vb_ref, knt_ref, kpet_ref, vt_ref, kbt_ref, vbt_ref):
        xb = x_ref[...].astype(BF16)

        def proj(lo, hi):
            return jnp.dot(xb, win_ref[:, lo:hi], preferred_element_type=F32)

        m_tabs = (tab_ref[0], tab_ref[1], tab_ref[2])
        d_tabs = (tab_ref[3], tab_ref[4], tab_ref[5])

        cq = proj(C_CQ, C_CKV)
        cq_ref[...] = cq
        qn = (cq * lax.rsqrt(jnp.mean(cq * cq, axis=1, keepdims=True) + RMS_EPS) * qg_ref[...]).astype(BF16)
        qn_ref[...] = qn
        q = jnp.dot(qn, wuq_ref[...], preferred_element_type=F32)
        qcat_ref[:, :HW] = (q[:, :HW] * (MLA_SCALE * LOG2E)).astype(BF16)
        qcat_ref[:, HW:] = (_rope_wide(_rope, q[:, HW:], *m_tabs, MLA_ROPE // 2) * (MLA_SCALE * LOG2E)).astype(BF16)

        ckv = proj(C_CKV, C_KR)
        ckv_ref[...] = ckv
        kvn = (ckv * lax.rsqrt(jnp.mean(ckv * ckv, axis=1, keepdims=True) + RMS_EPS) * kvg_ref[...]).astype(BF16)
        kvn_ref[...] = kvn
        kv = jnp.dot(kvn, wukv_ref[...], preferred_element_type=F32)
        kn_ref[...] = kv[:, :HW].astype(BF16)
        v_ref[...] = kv[:, HW:].astype(BF16)
        knt_ref[...] = kv[:, :HW].T.astype(BF16)
        vt_ref[...] = kv[:, HW:].T.astype(BF16)

        kpe = _rope(proj(C_KR, C_GA), *m_tabs, MLA_ROPE // 2)
        kpe_ref[...] = kpe.astype(BF16)
        kpet_ref[...] = kpe.T[:MLA_ROPE, :].astype(BF16)
        ga_ref[...] = proj(C_GA, C_QB)
        qb_ref[...] = (_rope_wide(_rope, proj(C_QB, C_KB), *d_tabs, DIL_ROT // 2) * (DIL_SCALE * LOG2E)).astype(BF16)
        kb = _rope_wide(_rope, proj(C_KB, C_VB), *d_tabs, DIL_ROT // 2)
        kb_ref[...] = kb.astype(BF16)
        kbt_ref[...] = kb.T.astype(BF16)
        vb = proj(C_VB, C_GB)
        vb_ref[...] = vb.astype(BF16)
        vbt_ref[...] = vb.T.astype(BF16)
        gb_ref[...] = proj(C_GB, C_END)

    def tok(width):
        return pl.BlockSpec((bt, width), lambda i: (i, 0))

    def tok_t(height):
        return pl.BlockSpec((height, bt), lambda i: (0, i))

    def full(a):
        return pl.BlockSpec(a.shape, lambda i: (0,) * a.ndim)

    outs = [(Q_RANK, F32), (KV_RANK, F32), (Q_RANK, BF16), (KV_RANK, BF16), (QW, BF16), (HW, BF16),
            (LANES, BF16), (HW, BF16), (HW, F32), (HW, F32), (HW, BF16), (HW, BF16), (HW, BF16)]
    outs_t = [HW, MLA_ROPE, HW, HW, HW]
    return _pcall(
        body, name="fwd_proj", grid=(seq // bt,),
        in_specs=[tok(D_MODEL), full(w_in_r), full(w_uq_r), full(w_ukv_r), full(qg), full(kvg),
                  pl.BlockSpec((6, bt, LANES), lambda i: (0, i, 0))],
        out_specs=[tok(w) for w, _ in outs] + [tok_t(h) for h in outs_t],
        out_shape=[jax.ShapeDtypeStruct((seq, w), dt) for w, dt in outs]
        + [jax.ShapeDtypeStruct((h, seq), BF16) for h in outs_t],
        compiler_params=_cparams(dimension_semantics=("arbitrary",)),
    )(x, w_in_r, w_uq_r, w_ukv_r, qg, kvg, tabs)


def _head_masks(lane, h):
    e, g = h % 2, h % 4
    me = (lane >= 64 * e) & (lane < 64 * e + 64)
    mr = (lane >= 32 * g) & (lane < 32 * g + 32)
    return me, mr


def _masked(mask, a):
    return jnp.where(mask, a, jnp.zeros_like(a))


def _pair_operands(q_ref, k_ref, kpe_ref, lane, j):
    cols = slice(LANES * j, LANES * (j + 1))
    qc = q_ref[:, cols]
    kj = k_ref[:, cols]
    kes = []
    for h in (2 * j, 2 * j + 1):
        me, mr = _head_masks(lane, h)
        ke = _masked(me, kj)
        if kpe_ref is not None:
            ke = jnp.concatenate([ke, _masked(mr, kpe_ref[...])], axis=1)
        kes.append(ke)
    if kpe_ref is not None:
        qc = jnp.concatenate([qc, q_ref[:, HW + LANES * (j // 2):HW + LANES * (j // 2 + 1)]], axis=1)
    return qc, kes


def _attn_fwd(name, q, k, kpe, vt, bias_t, steps, blk):
    seq = q.shape[0]
    mla = kpe is not None
    n_steps = int(steps[0].shape[0])

    def body(qi_r, ki_r, bi_r, fi_r, la_r, *refs):
        if mla:
            q_ref, k_ref, kpe_ref, vt_ref, b_ref, o_ref, lse_ref, m_sc, l_sc, acc_sc, st_sc = refs
        else:
            q_ref, k_ref, vt_ref, b_ref, o_ref, lse_ref, m_sc, l_sc, acc_sc, st_sc = refs
        t = pl.program_id(0)

        @pl.when(fi_r[t] == 1)
        def _():
            m_sc[...] = jnp.full(m_sc.shape, NEG, F32)
            l_sc[...] = jnp.zeros(l_sc.shape, F32)
            acc_sc[...] = jnp.zeros(acc_sc.shape, F32)

        lane = lax.broadcasted_iota(jnp.int32, (1, LANES), 1)
        ones = jnp.ones((16, blk), BF16)

        def pair_scores(j, with_bias):
            qc, kes = _pair_operands(q_ref, k_ref, kpe_ref if mla else None, lane, j)
            st = lax.dot_general(jnp.concatenate(kes, axis=0), qc, NT, preferred_element_type=F32)
            maxes = []
            for e in range(2):
                se = st[e * blk:(e + 1) * blk]
                if with_bias:
                    se = se + b_ref[0]
                st_sc[j % 2, e * blk:(e + 1) * blk] = se
                maxes.append(jnp.max(se, axis=0, keepdims=True))
            return maxes

        def softmax_pv(h, col_max):
            st = st_sc[(h // 2) % 2, (h % 2) * blk:(h % 2 + 1) * blk]
            hrow = slice(h, h + 1)
            m_prev = m_sc[hrow, :]
            m_new = jnp.maximum(m_prev, col_max)
            alpha = jnp.exp2(m_prev - m_new)
            pt = jnp.exp2(st - m_new).astype(BF16)
            m_sc[hrow, :] = m_new
            rows = slice(64 * h, 64 * h + 64)
            res = jnp.dot(jnp.concatenate([vt_ref[rows, :], ones], axis=0), pt, preferred_element_type=F32)
            acc_sc[rows, :] = alpha * acc_sc[rows, :] + res[:64]
            l_sc[hrow, :] = alpha * l_sc[hrow, :] + res[64:65]

        def step(with_bias):
            maxes = pair_scores(0, with_bias)
            for j in range(HEADS // 2):
                cur = maxes
                if j + 1 < HEADS // 2:
                    maxes = pair_scores(j + 1, with_bias)
                softmax_pv(2 * j, cur[0])
                softmax_pv(2 * j + 1, cur[1])

        if mla:
            pl.when(bi_r[t] == 1)(lambda: step(True))
            pl.when(bi_r[t] == 0)(lambda: step(False))
        else:
            step(True)

        @pl.when(la_r[t] == 1)
        def _():
            for h in range(HEADS):
                rows = slice(64 * h, 64 * h + 64)
                acc_sc[rows, :] = acc_sc[rows, :] / l_sc[h:h + 1, :]
            o_ref[...] = acc_sc[...].T
            lse_ref[...] = m_sc[...] + jnp.log2(l_sc[...])

    qmap = lambda t, qi, ki, bi, fi, la: (qi[t], 0)
    kmap = lambda t, qi, ki, bi, fi, la: (ki[t], 0)
    in_specs = [pl.BlockSpec((blk, q.shape[1]), qmap), pl.BlockSpec((blk, HW), kmap)]
    args = [q, k]
    if mla:
        in_specs.append(pl.BlockSpec((blk, LANES), kmap))
        args.append(kpe)
    in_specs += [pl.BlockSpec((HW, blk), lambda t, qi, ki, bi, fi, la: (0, ki[t])),
                 pl.BlockSpec((1, blk, blk), lambda t, qi, ki, bi, fi, la: (bi[t], 0, 0))]
    args += [vt, bias_t]
    return _pcall(
        body, name=name,
        grid_spec=pltpu.PrefetchScalarGridSpec(
            num_scalar_prefetch=5, grid=(n_steps,), in_specs=in_specs,
            out_specs=[pl.BlockSpec((blk, HW), qmap),
                       pl.BlockSpec((HEADS, blk), lambda t, qi, ki, bi, fi, la: (0, qi[t]))],
            scratch_shapes=[pltpu.VMEM((HEADS, blk), F32), pltpu.VMEM((HEADS, blk), F32),
                            pltpu.VMEM((HW, blk), F32), pltpu.VMEM((2, 2 * blk, blk), F32)]),
        out_shape=[jax.ShapeDtypeStruct((seq, HW), F32), jax.ShapeDtypeStruct((HEADS, seq), F32)],
        compiler_params=_cparams(dimension_semantics=("arbitrary",)),
    )(*steps, *args)


def _attn_bwd(name, q, k, kpe, v, kt, kpet, bias_t, do, lse, dstat, steps, blk):
    seq = q.shape[0]
    mla = kpe is not None
    qw = q.shape[1]
    n_steps = int(steps[0].shape[0])
    dk_dtype = BF16 if mla else F32

    def body(qi_r, ki_r, bi_r, fi_r, la_r, *refs):
        if mla:
            (q_ref, k_ref, kpe_ref, v_ref, kt_ref, kpet_ref, b_ref, do_ref, lse_ref, d_ref,
             dq_ref, dk_ref, dkpe_ref, dv_ref, dk_sc, dkpe_sc, dv_sc, st_sc, dpt_sc) = refs
        else:
            (q_ref, k_ref, v_ref, kt_ref, b_ref, do_ref, lse_ref, d_ref,
             dq_ref, dk_ref, dv_ref, dk_sc, dv_sc, st_sc, dpt_sc) = refs
        t = pl.program_id(0)

        @pl.when(t == 0)
        def _():
            dq_ref[...] = jnp.zeros(dq_ref.shape, F32)

        @pl.when(fi_r[t] == 1)
        def _():
            dk_sc[...] = jnp.zeros(dk_sc.shape, F32)
            dv_sc[...] = jnp.zeros(dv_sc.shape, F32)
            if mla:
                dkpe_sc[...] = jnp.zeros(dkpe_sc.shape, F32)

        qi = qi_r[t]
        lane = lax.broadcasted_iota(jnp.int32, (1, LANES), 1)

        def pair_matmuls(j):
            cols = slice(LANES * j, LANES * (j + 1))
            qc, kes = _pair_operands(q_ref, k_ref, kpe_ref if mla else None, lane, j)
            st_sc[j % 2] = lax.dot_general(jnp.concatenate(kes, axis=0), qc, NT, preferred_element_type=F32)
            vj = v_ref[:, cols]
            ves = [_masked(_head_masks(lane, h)[0], vj) for h in (2 * j, 2 * j + 1)]
            dpt_sc[j % 2] = lax.dot_general(
                jnp.concatenate(ves, axis=0), do_ref[:, cols], NT, preferred_element_type=F32)

        def pair_grads(j, with_bias):
            cols = slice(LANES * j, LANES * (j + 1))
            qj, doj = q_ref[:, cols], do_ref[:, cols]
            if mla:
                qr = q_ref[:, HW + LANES * (j // 2):HW + LANES * (j // 2 + 1)]
            pts, dsts, qms, doms = [], [], [], []
            for e in range(2):
                h = 2 * j + e
                me, mr = _head_masks(lane, h)
                st = st_sc[j % 2, e * blk:(e + 1) * blk]
                if with_bias:
                    st = st + b_ref[0]
                pt = jnp.exp2(st - lse_ref[h:h + 1, :])
                dst = (pt * (dpt_sc[j % 2, e * blk:(e + 1) * blk] - d_ref[h:h + 1, :])).astype(BF16)
                pts.append(pt.astype(BF16))
                dsts.append(dst)
                doms.append(_masked(me, doj))
                qm = _masked(me, qj)
                if mla:
                    qm = jnp.concatenate([qm, _masked(mr, qr)], axis=1)
                qms.append(qm)
                ktl = kt_ref[64 * h:64 * h + 64, :]
                if mla:
                    ktl = jnp.concatenate([ktl, kpet_ref[...]], axis=0)
                dqc = jnp.dot(ktl, dst, preferred_element_type=F32)
                dq_ref[qi, 64 * h:64 * h + 64, :] += dqc[:64]
                if mla:
                    dq_ref[qi, HW + MLA_ROPE * h:HW + MLA_ROPE * (h + 1), :] += dqc[64:]
            dv_sc[:, cols] += jnp.dot(
                jnp.concatenate(pts, axis=1), jnp.concatenate(doms, axis=0), preferred_element_type=F32)
            dkc = jnp.dot(jnp.concatenate(dsts, axis=1), jnp.concatenate(qms, axis=0), preferred_element_type=F32)
            dk_sc[:, cols] += dkc[:, :LANES]
            if mla:
                dkpe_sc[...] += dkc[:, LANES:]

        def step(with_bias):
            pair_matmuls(0)
            for j in range(HEADS // 2):
                if j + 1 < HEADS // 2:
                    pair_matmuls(j + 1)
                pair_grads(j, with_bias)

        if mla:
            pl.when(bi_r[t] == 1)(lambda: step(True))
            pl.when(bi_r[t] == 0)(lambda: step(False))
        else:
            step(True)

        @pl.when(la_r[t] == 1)
        def _():
            dk_ref[...] = (dk_sc[...] * LN2).astype(dk_ref.dtype)
            dv_ref[...] = dv_sc[...].astype(dv_ref.dtype)
            if mla:
                dkpe_ref[...] = dkpe_sc[...] * LN2

    qmap = lambda t, qi, ki, bi, fi, la: (qi[t], 0)
    kmap = lambda t, qi, ki, bi, fi, la: (ki[t], 0)
    qmap_t = lambda t, qi, ki, bi, fi, la: (0, qi[t])
    kmap_t = lambda t, qi, ki, bi, fi, la: (0, ki[t])
    in_specs = [pl.BlockSpec((blk, qw), qmap), pl.BlockSpec((blk, HW), kmap)]
    args = [q, k]
    if mla:
        in_specs.append(pl.BlockSpec((blk, LANES), kmap))
        args.append(kpe)
    in_specs += [pl.BlockSpec((blk, HW), kmap), pl.BlockSpec((HW, blk), kmap_t)]
    args += [v, kt]
    if mla:
        in_specs.append(pl.BlockSpec((MLA_ROPE, blk), kmap_t))
        args.append(kpet)
    in_specs += [pl.BlockSpec((1, blk, blk), lambda t, qi, ki, bi, fi, la: (bi[t], 0, 0)),
                 pl.BlockSpec((blk, HW), qmap), pl.BlockSpec((HEADS, blk), qmap_t), pl.BlockSpec((HEADS, blk), qmap_t)]
    args += [bias_t, do, lse, dstat]
    dq_shape = (seq // blk, qw, blk)
    out_specs = [pl.BlockSpec(dq_shape, lambda t, qi, ki, bi, fi, la: (0, 0, 0)), pl.BlockSpec((blk, HW), kmap)]
    out_shape = [jax.ShapeDtypeStruct(dq_shape, F32), jax.ShapeDtypeStruct((seq, HW), dk_dtype)]
    scratch = [pltpu.VMEM((blk, HW), F32)]
    if mla:
        out_specs.append(pl.BlockSpec((blk, LANES), kmap))
        out_shape.append(jax.ShapeDtypeStruct((seq, LANES), F32))
        scratch.append(pltpu.VMEM((blk, LANES), F32))
    out_specs.append(pl.BlockSpec((blk, HW), kmap))
    out_shape.append(jax.ShapeDtypeStruct((seq, HW), BF16))
    scratch.append(pltpu.VMEM((blk, HW), F32))
    scratch += [pltpu.VMEM((2, 2 * blk, blk), F32), pltpu.VMEM((2, 2 * blk, blk), F32)]
    return _pcall(
        body, name=name,
        grid_spec=pltpu.PrefetchScalarGridSpec(
            num_scalar_prefetch=5, grid=(n_steps,), in_specs=in_specs, out_specs=out_specs,
            scratch_shapes=scratch),
        out_shape=out_shape,
        compiler_params=_cparams(dimension_semantics=("arbitrary",)),
    )(*steps, *args)


def _out_ln(oa, ob, ga, gb, x, tgt, w_out, ln_g, ln_b, bt):
    seq = x.shape[0]

    def body(oa_ref, ob_ref, ga_ref, gb_ref, x_ref, tgt_ref, w_ref, g_ref, b_ref,
             dz_ref, doa_ref, dob_ref, dga_ref, dgb_ref, da_ref, db_ref, gw_ref, small_ref):
        i = pl.program_id(0)

        @pl.when(i == 0)
        def _():
            gw_ref[...] = jnp.zeros(gw_ref.shape, F32)
            small_ref[...] = jnp.zeros(small_ref.shape, F32)

        def gate(g):
            sig = 1.0 / (1.0 + jnp.exp(-g))
            return g * sig, sig * (1.0 + g * (1.0 - sig))

        o_a, o_b = oa_ref[...], ob_ref[...]
        g_a, g_b = ga_ref[...], gb_ref[...]
        sa, dsa = gate(g_a)
        sb, dsb = gate(g_b)
        mix = jnp.concatenate([o_a * sa, o_b * sb], axis=1).astype(BF16)
        z = ALPHA * x_ref[...] + jnp.dot(mix, w_ref[...], preferred_element_type=F32)
        mu = jnp.mean(z, axis=1, keepdims=True)
        zc = z - mu
        rstd = lax.rsqrt(jnp.mean(zc * zc, axis=1, keepdims=True) + LN_EPS)
        xhat = zc * rstd
        gam = g_ref[...]
        diff = xhat * gam + b_ref[...] - tgt_ref[...]
        dy = diff * (1.0 / D_MODEL)
        small_ref[0:1, :] += jnp.sum(dy * xhat, axis=0, keepdims=True)
        small_ref[1:2, :] += jnp.sum(dy, axis=0, keepdims=True)
        small_ref[2:3, :] += jnp.sum(diff * diff, axis=0, keepdims=True)
        dxh = dy * gam
        dz = rstd * (dxh - jnp.mean(dxh, axis=1, keepdims=True) - xhat * jnp.mean(dxh * xhat, axis=1, keepdims=True))
        dz_ref[...] = dz
        dzb = dz.astype(BF16)
        gw_ref[...] += lax.dot_general(mix, dzb, TN, preferred_element_type=F32)
        dmix = lax.dot_general(dzb, w_ref[...], NT, preferred_element_type=F32)
        doa, dob = dmix[:, :HW] * sa, dmix[:, HW:] * sb
        doa_ref[...] = doa.astype(BF16)
        dob_ref[...] = dob.astype(BF16)
        dga_ref[...] = (dmix[:, :HW] * o_a * dsa).astype(BF16)
        dgb_ref[...] = (dmix[:, HW:] * o_b * dsb).astype(BF16)
        head_of = lax.broadcasted_iota(jnp.int32, (HEADS, HW), 1) // 64
        ind = (head_of == lax.broadcasted_iota(jnp.int32, (HEADS, HW), 0)).astype(F32)
        da_ref[...] = lax.dot_general(ind, doa * o_a, NT, preferred_element_type=F32, precision=lax.Precision.HIGHEST)
        db_ref[...] = lax.dot_general(ind, dob * o_b, NT, preferred_element_type=F32, precision=lax.Precision.HIGHEST)

    def tok(width):
        return pl.BlockSpec((bt, width), lambda i: (i, 0))

    def full(shape):
        return pl.BlockSpec(shape, lambda i: (0,) * len(shape))

    stat = pl.BlockSpec((HEADS, bt), lambda i: (0, i))
    return _pcall(
        body, name="out_ln", grid=(seq // bt,),
        in_specs=[tok(HW), tok(HW), tok(HW), tok(HW), tok(D_MODEL), tok(D_MODEL),
                  full((D_MODEL, D_MODEL)), full((1, D_MODEL)), full((1, D_MODEL))],
        out_specs=[tok(D_MODEL), tok(HW), tok(HW), tok(HW), tok(HW), stat, stat,
                   full((D_MODEL, D_MODEL)), full((8, D_MODEL))],
        out_shape=[jax.ShapeDtypeStruct((seq, D_MODEL), F32)] + [jax.ShapeDtypeStruct((seq, HW), BF16)] * 4
        + [jax.ShapeDtypeStruct((HEADS, seq), F32)] * 2
        + [jax.ShapeDtypeStruct((D_MODEL, D_MODEL), F32), jax.ShapeDtypeStruct((8, D_MODEL), F32)],
        compiler_params=_cparams(dimension_semantics=("arbitrary",)),
    )(oa, ob, ga, gb, x, tgt, w_out, ln_g, ln_b)


def _bwd_mid(dq_m, dkn, dv, dkpe, dqb, dkb, dvb, dga, dgb, cq, ckv, qn, kvn, w_uq_r, w_ukv_r, qg, kvg, tabs, bt):
    seq = cq.shape[0]

    def body(dqm_ref, dkn_ref, dv_ref, dkpe_ref, dqb_ref, dkb_ref, dvb_ref, dga_ref, dgb_ref,
             cq_ref, ckv_ref, qn_ref, kvn_ref, wuq_ref, wukv_ref, qg_ref, kvg_ref, tab_ref,
             dh_ref, guq_ref, gukv_ref, small_ref):
        i = pl.program_id(0)

        @pl.when(i == 0)
        def _():
            guq_ref[...] = jnp.zeros(guq_ref.shape, F32)
            gukv_ref[...] = jnp.zeros(gukv_ref.shape, F32)
            small_ref[...] = jnp.zeros(small_ref.shape, F32)

        m_tabs = (tab_ref[0], tab_ref[1], tab_ref[2])
        d_tabs = (tab_ref[3], tab_ref[4], tab_ref[5])

        def rms_bwd(c, dn, gain):
            r = lax.rsqrt(jnp.mean(c * c, axis=1, keepdims=True) + RMS_EPS)
            u = dn * gain
            dc = r * u - c * (r * r * r) * jnp.mean(u * c, axis=1, keepdims=True)
            return dc, jnp.sum(dn * c * r, axis=0, keepdims=True)

        dqm = dqm_ref[0].T
        dq = jnp.concatenate(
            [dqm[:, :HW], _rope_wide(_rope_t, dqm[:, HW:], *m_tabs, MLA_ROPE // 2)], axis=1) * MLA_SCALE
        dq = dq.astype(BF16)
        guq_ref[...] += lax.dot_general(qn_ref[...], dq, TN, preferred_element_type=F32)
        dqn = lax.dot_general(dq, wuq_ref[...], NT, preferred_element_type=F32)
        dcq, gq = rms_bwd(cq_ref[...], dqn, qg_ref[...])
        small_ref[0:1, :] += gq

        dkv = jnp.concatenate([dkn_ref[...], dv_ref[...]], axis=1)
        gukv_ref[...] += lax.dot_general(kvn_ref[...], dkv, TN, preferred_element_type=F32)
        dkvn = lax.dot_general(dkv, wukv_ref[...], NT, preferred_element_type=F32)
        dckv, gkv = rms_bwd(ckv_ref[...], dkvn, kvg_ref[...])
        small_ref[1:2, :KV_RANK] += gkv

        dh_ref[:, C_CQ:C_CKV] = dcq.astype(BF16)
        dh_ref[:, C_CKV:C_KR] = dckv.astype(BF16)
        dh_ref[:, C_KR:C_GA] = _rope_t(dkpe_ref[...], *m_tabs, MLA_ROPE // 2).astype(BF16)
        dh_ref[:, C_GA:C_QB] = dga_ref[...]
        dh_ref[:, C_QB:C_KB] = (_rope_wide(_rope_t, dqb_ref[0].T, *d_tabs, DIL_ROT // 2) * DIL_SCALE).astype(BF16)
        dh_ref[:, C_KB:C_VB] = _rope_wide(_rope_t, dkb_ref[...], *d_tabs, DIL_ROT // 2).astype(BF16)
        dh_ref[:, C_VB:C_GB] = dvb_ref[...]
        dh_ref[:, C_GB:C_END] = dgb_ref[...]

    def tok(width):
        return pl.BlockSpec((bt, width), lambda i: (i, 0))

    def tok_t(a):
        per = a.shape[2] // bt
        return pl.BlockSpec((1, a.shape[1], bt), lambda i: (i // per, 0, i % per))

    def full(shape):
        return pl.BlockSpec(shape, lambda i: (0,) * len(shape))

    return _pcall(
        body, name="bwd_mid", grid=(seq // bt,),
        in_specs=[tok_t(dq_m), tok(HW), tok(HW), tok(LANES), tok_t(dqb), tok(HW), tok(HW), tok(HW), tok(HW),
                  tok(Q_RANK), tok(KV_RANK), tok(Q_RANK), tok(KV_RANK),
                  full(w_uq_r.shape), full(w_ukv_r.shape), full((1, Q_RANK)), full((1, KV_RANK)),
                  pl.BlockSpec((6, bt, LANES), lambda i: (0, i, 0))],
        out_specs=[tok(C_END), full(w_uq_r.shape), full(w_ukv_r.shape), full((8, Q_RANK))],
        out_shape=[jax.ShapeDtypeStruct((seq, C_END), BF16), jax.ShapeDtypeStruct(w_uq_r.shape, F32),
                   jax.ShapeDtypeStruct(w_ukv_r.shape, F32), jax.ShapeDtypeStruct((8, Q_RANK), F32)],
        compiler_params=_cparams(dimension_semantics=("arbitrary",)),
    )(dq_m, dkn, dv, dkpe, dqb, dkb, dvb, dga, dgb, cq, ckv, qn, kvn, w_uq_r, w_ukv_r, qg, kvg, tabs)


def _grad_x(dz, dh, w_in_r, bt):
    seq = dz.shape[0]

    def body(dz_ref, dh_ref, w_ref, gx_ref):
        gx_ref[...] = ALPHA * dz_ref[...] + lax.dot_general(
            dh_ref[...], w_ref[...], NT, preferred_element_type=F32)

    return _pcall(
        body, name="grad_x", grid=(seq // bt,),
        in_specs=[pl.BlockSpec((bt, D_MODEL), lambda i: (i, 0)), pl.BlockSpec((bt, C_END), lambda i: (i, 0)),
                  pl.BlockSpec(w_in_r.shape, lambda i: (0, 0))],
        out_specs=pl.BlockSpec((bt, D_MODEL), lambda i: (i, 0)),
        out_shape=jax.ShapeDtypeStruct((seq, D_MODEL), F32),
        compiler_params=_cparams(dimension_semantics=("arbitrary",)),
    )(dz, dh, w_in_r)


def _grad_w_in(x, dh, bt):
    seq = x.shape[0]

    def body(x_ref, dh_ref, gw_ref):
        @pl.when(pl.program_id(0) == 0)
        def _():
            gw_ref[...] = jnp.zeros(gw_ref.shape, F32)

        gw_ref[...] += lax.dot_general(x_ref[...].astype(BF16), dh_ref[...], TN, preferred_element_type=F32)

    return _pcall(
        body, name="grad_w_in", grid=(seq // bt,),
        in_specs=[pl.BlockSpec((bt, D_MODEL), lambda i: (i, 0)), pl.BlockSpec((bt, C_END), lambda i: (i, 0))],
        out_specs=pl.BlockSpec((D_MODEL, C_END), lambda i: (0, 0)),
        out_shape=jax.ShapeDtypeStruct((D_MODEL, C_END), F32),
        compiler_params=_cparams(dimension_semantics=("arbitrary",)),
    )(x, dh)


def _reorder_weights(w_in, w_uq, w_ukv):
    w_in_r = jnp.concatenate(
        [w_in[:, :640], jnp.tile(w_in[:, 640:672], (1, 4)), w_in[:, 672:]], axis=1)
    uq = w_uq.reshape(Q_RANK, HEADS, MLA_NOPE + MLA_ROPE)
    w_uq_r = jnp.concatenate(
        [uq[:, :, :MLA_NOPE].reshape(Q_RANK, HW), uq[:, :, MLA_NOPE:].reshape(Q_RANK, HEADS * MLA_ROPE)], axis=1)
    ukv = w_ukv.reshape(KV_RANK, HEADS, MLA_NOPE + MLA_V)
    w_ukv_r = jnp.concatenate(
        [ukv[:, :, :MLA_NOPE].reshape(KV_RANK, HW), ukv[:, :, MLA_NOPE:].reshape(KV_RANK, HW)], axis=1)
    return w_in_r, w_uq_r, w_ukv_r


def _restore_grads(g_in_r, g_uq_r, g_ukv_r):
    g_in = jnp.concatenate(
        [g_in_r[:, :640], g_in_r[:, 640:768].reshape(D_MODEL, 4, MLA_ROPE).sum(axis=1), g_in_r[:, 768:]], axis=1)
    g_uq = jnp.concatenate(
        [g_uq_r[:, :HW].reshape(Q_RANK, HEADS, MLA_NOPE), g_uq_r[:, HW:].reshape(Q_RANK, HEADS, MLA_ROPE)],
        axis=2).reshape(Q_RANK, HEADS * (MLA_NOPE + MLA_ROPE))
    g_ukv = jnp.concatenate(
        [g_ukv_r[:, :HW].reshape(KV_RANK, HEADS, MLA_NOPE), g_ukv_r[:, HW:].reshape(KV_RANK, HEADS, MLA_V)],
        axis=2).reshape(KV_RANK, HEADS * (MLA_NOPE + MLA_V))
    return g_in, g_uq, g_ukv


def _local_step(x, tgt, w_in, w_uq, w_ukv, w_out, q_norm_g, kv_norm_g, ln_g, ln_b,
                bt=BLOCK_TOKENS, blk_m=BLOCK_MLA, blk_d=BLOCK_DIL):
    seq = x.shape[0]
    tabs = jnp.asarray(_rope_tables(seq))
    w_in_r, w_uq_r, w_ukv_r = _reorder_weights(w_in, w_uq, w_ukv)
    qg, kvg = q_norm_g.reshape(1, -1), kv_norm_g.reshape(1, -1)

    (cq, ckv, qn, kvn, qcat, kn, kpe, v, ga, gb, qb, kb, vb, knt, kpet, vt, kbt, vbt) = _fwd_proj(
        x, w_in_r, w_uq_r, w_ukv_r, qg, kvg, tabs, bt)

    nq_m, nq_d = seq // blk_m, seq // blk_d
    span_d = DIL_CONFIGS[-1][0] // blk_d
    bias_m, bias_d = jnp.asarray(_mla_bias_t(blk_m)), jnp.asarray(_dil_bias_t(blk_d))
    oa, lse_a = _attn_fwd("mla_fwd", qcat, kn, kpe, vt, bias_m, _steps(nq_m, nq_m, False, True), blk_m)
    ob, lse_b = _attn_fwd("dil_fwd", qb, kb, None, vbt, bias_d, _steps(nq_d, span_d, False, False), blk_d)

    dz, doa, dob, dga, dgb, dst_a, dst_b, g_out, small1 = _out_ln(
        oa, ob, ga, gb, x, tgt, w_out, ln_g.reshape(1, -1), ln_b.reshape(1, -1), bt)

    dq_m, dkn, dkpe, dv = _attn_bwd(
        "mla_bwd", qcat, kn, kpe, v, knt, kpet, bias_m, doa, lse_a, dst_a, _steps(nq_m, nq_m, True, True), blk_m)
    dqb, dkb, dvb = _attn_bwd(
        "dil_bwd", qb, kb, None, vb, kbt, None, bias_d, dob, lse_b, dst_b, _steps(nq_d, span_d, True, False), blk_d)

    dh, g_uq_r, g_ukv_r, small2 = _bwd_mid(
        dq_m, dkn, dv, dkpe, dqb, dkb, dvb, dga, dgb, cq, ckv, qn, kvn, w_uq_r, w_ukv_r, qg, kvg, tabs, bt)
    grad_x = _grad_x(dz, dh, w_in_r, bt)
    g_in_r = _grad_w_in(x, dh, bt)
    g_in, g_uq, g_ukv = _restore_grads(g_in_r, g_uq_r, g_ukv_r)

    return small1[2], grad_x, g_in, g_uq, g_ukv, g_out, small2[0], small2[1, :KV_RANK], small1[0], small1[1]


MESH_ID = pl.DeviceIdType.MESH
SHARD_SHAPES = ((D_MODEL, IN_WIDTH // N_DEV), (Q_RANK, 768 // N_DEV), (KV_RANK, 1024 // N_DEV), (D_MODEL // N_DEV, D_MODEL))
ADAM_ROWS = (32, 128, 128, 16)


def _me():
    x, y, c = lax.axis_index("x"), lax.axis_index("y"), lax.axis_index("c")
    return x, y, c, 4 * x + 2 * y + c


def _peer(k):
    x, y, c, _ = _me()
    px = 1 - x if (k >> 2) & 1 else x
    py = 1 - y if (k >> 1) & 1 else y
    pc = 1 - c if k & 1 else c
    return (px, py, pc), 4 * px + 2 * py + pc


def _all_gather_weights(shards):
    def body(*refs):
        ins, outs = refs[:4], refs[4:8]
        send_sems, recv_sems = refs[8:]
        x, y, c, me = _me()
        here, sibling = (x, y, c), (x, y, 1 - c)
        chips = [(1 - x, y), (x, 1 - y), (1 - x, 1 - y)]
        for t in range(4):
            outs[t][me] = ins[t][...].astype(BF16)

        def copy(t, k, px, py, pc, to):
            blk = outs[t].at[4 * px + 2 * py + pc]
            return pltpu.make_async_remote_copy(
                src_ref=blk, dst_ref=blk, send_sem=send_sems.at[t, k], recv_sem=recv_sems.at[t, k],
                device_id=to, device_id_type=MESH_ID)

        first = []
        for t in range(4):
            first.append(copy(t, 0, x, y, c, sibling))
            for j, (px, py) in enumerate(chips):
                first.append(copy(t, 1 + j, x, y, c, (px, py, c)))
        for cp in first:
            cp.start()
        passed = []
        for j, (px, py) in enumerate(chips):
            for t in range(4):
                copy(t, 1 + j, px, py, c, here).wait_recv()
                cp = copy(t, 4 + j, px, py, c, sibling)
                cp.start()
                passed.append(cp)
        for t in range(4):
            copy(t, 0, x, y, 1 - c, here).wait_recv()
        for j, (px, py) in enumerate(chips):
            for t in range(4):
                copy(t, 4 + j, px, py, 1 - c, here).wait_recv()
        for cp in first + passed:
            cp.wait_send()

    vmem = pl.BlockSpec(memory_space=pltpu.VMEM)
    return _pcall(
        body, name="gather_weights",
        in_specs=[vmem] * 4, out_specs=[vmem] * 4,
        out_shape=[jax.ShapeDtypeStruct((N_DEV,) + s, BF16) for s in SHARD_SHAPES],
        scratch_shapes=[pltpu.SemaphoreType.DMA((4, N_DEV - 1)), pltpu.SemaphoreType.DMA((4, N_DEV - 1))],
        compiler_params=_cparams(),
    )(*shards)


def _adamw(w, g, m, v):
    m = ADAM_B1 * m + (1.0 - ADAM_B1) * g
    v = ADAM_B2 * v + (1.0 - ADAM_B2) * jnp.square(g)
    m_hat = m / (1.0 - ADAM_B1 ** ADAM_STEP)
    v_hat = v / (1.0 - ADAM_B2 ** ADAM_STEP)
    delta = -ADAM_LR * (m_hat / (jnp.sqrt(v_hat) + ADAM_EPS) + ADAM_WD * w)
    return delta, m, v


def _reduce_adam(grads3, small_part, wmv, small_wmv):
    def body(*refs):
        g3 = refs[0:4]
        sp_ref = refs[4]
        wmv_refs = [refs[5 + 3 * t:8 + 3 * t] for t in range(4)]
        swmv_ref = refs[17]
        out_refs = [refs[18 + 4 * t:22 + 4 * t] for t in range(4)]
        sout_ref = refs[34]
        recv = refs[35:39]
        rsmall = refs[39]
        send_sems, recv_sems, local_sems = refs[40:43]
        me = _me()[3]

        rsmall[0] = sp_ref[...]
        local = []
        for t in range(4):
            cp = pltpu.make_async_copy(g3[t].at[me], recv[t].at[0], local_sems.at[t])
            cp.start()
            local.append(cp)
        sends = []
        for k in range(1, N_DEV):
            peer, pidx = _peer(k)
            for t in range(5):
                src = rsmall.at[0] if t == 4 else g3[t].at[pidx]
                dst = rsmall.at[k] if t == 4 else recv[t].at[k]
                cp = pltpu.make_async_remote_copy(
                    src_ref=src, dst_ref=dst, send_sem=send_sems.at[t, k - 1], recv_sem=recv_sems.at[t, k - 1],
                    device_id=peer, device_id_type=MESH_ID)
                cp.start()
                sends.append(cp)
        for cp in local:
            cp.wait()
        for cp in sends:
            cp.wait_recv()

        tot = rsmall[me]
        for d in range(1, N_DEV):
            tot = tot + rsmall[jnp.bitwise_xor(me, d)]
        delta, m, v = _adamw(swmv_ref[0], tot, swmv_ref[1], swmv_ref[2])
        sout_ref[0], sout_ref[1], sout_ref[2], sout_ref[3] = tot, delta, m, v

        for t in range(4):
            rows = ADAM_ROWS[t]
            w_ref, m_ref, v_ref = wmv_refs[t]
            g_out, d_out, m_out, v_out = out_refs[t]

            def step(i, carry, t=t, rows=rows, w_ref=w_ref, m_ref=m_ref, v_ref=v_ref,
                     g_out=g_out, d_out=d_out, m_out=m_out, v_out=v_out):
                r = pl.ds(pl.multiple_of(i * rows, rows), rows)
                g = recv[t][0, r, :].astype(F32)
                for k in range(1, N_DEV):
                    g = g + recv[t][k, r, :].astype(F32)
                delta, m, v = _adamw(w_ref[r, :], g, m_ref[r, :], v_ref[r, :])
                g_out[r, :], d_out[r, :], m_out[r, :], v_out[r, :] = g, delta, m, v
                return carry

            lax.fori_loop(0, SHARD_SHAPES[t][0] // rows, step, 0)

        for cp in sends:
            cp.wait_send()

    vmem = pl.BlockSpec(memory_space=pltpu.VMEM)
    hbm = pl.BlockSpec(memory_space=pl.ANY)
    flat_wmv = [a for trio in wmv for a in trio]
    return _pcall(
        body, name="reduce_adam",
        in_specs=[hbm] * 4 + [vmem] * 14,
        out_specs=[vmem] * 17,
        out_shape=[jax.ShapeDtypeStruct(s, F32) for s in SHARD_SHAPES for _ in range(4)]
        + [jax.ShapeDtypeStruct((4, 8, D_MODEL), F32)],
        scratch_shapes=[pltpu.VMEM((N_DEV,) + s, BF16) for s in SHARD_SHAPES]
        + [pltpu.VMEM((N_DEV, 8, D_MODEL), F32),
           pltpu.SemaphoreType.DMA((5, N_DEV - 1)), pltpu.SemaphoreType.DMA((5, N_DEV - 1)),
           pltpu.SemaphoreType.DMA((4,))],
        compiler_params=_cparams(),
    )(*grads3, small_part, *flat_wmv, small_wmv)


def _small_rows(ln_g, ln_b, q_norm_g, kv_norm_g, extra=None):
    pad = lambda a: jnp.pad(a, (0, D_MODEL - a.shape[0]))
    rows = [ln_g, ln_b, pad(q_norm_g), pad(kv_norm_g)] + ([] if extra is None else [extra])
    return jnp.pad(jnp.stack(rows), ((0, 8 - len(rows)), (0, 0)))


def kernel(x, w_in, q_norm_g, kv_norm_g, w_uq, w_ukv, w_out, ln_g, ln_b, loss_target, m_w_in, m_q_norm_g, m_kv_norm_g, m_w_uq, m_w_ukv, m_w_out, m_ln_g, m_ln_b, v_w_in, v_q_norm_g, v_kv_norm_g, v_w_uq, v_w_ukv, v_w_out, v_ln_g, v_ln_b):
    a_in, a_uq, a_ukv, a_out = _all_gather_weights([w_in, w_uq, w_ukv, w_out])
    full_in = a_in.transpose(1, 0, 2).reshape(D_MODEL, IN_WIDTH)
    full_uq = a_uq.transpose(1, 0, 2).reshape(Q_RANK, 768)
    full_ukv = a_ukv.transpose(1, 0, 2).reshape(KV_RANK, 1024)
    full_out = a_out.reshape(D_MODEL, D_MODEL)

    sq_err, grad_x, g_in, g_uq, g_ukv, g_out, g_qg, g_kvg, g_lng, g_lnb = _local_step(
        x[0], loss_target[0], full_in, full_uq, full_ukv, full_out, q_norm_g, kv_norm_g, ln_g, ln_b)

    grads3 = [g_in.reshape(D_MODEL, N_DEV, -1).transpose(1, 0, 2), g_uq.reshape(Q_RANK, N_DEV, -1).transpose(1, 0, 2),
              g_ukv.reshape(KV_RANK, N_DEV, -1).transpose(1, 0, 2), g_out.reshape(N_DEV, D_MODEL // N_DEV, D_MODEL)]
    grads3 = [g.astype(BF16) for g in grads3]
    small_part = _small_rows(g_lng, g_lnb, g_qg, g_kvg, sq_err)
    small_wmv = jnp.stack([_small_rows(ln_g, ln_b, q_norm_g, kv_norm_g),
                           _small_rows(m_ln_g, m_ln_b, m_q_norm_g, m_kv_norm_g),
                           _small_rows(v_ln_g, v_ln_b, v_q_norm_g, v_kv_norm_g)])
    wmv = [(w_in, m_w_in, v_w_in), (w_uq, m_w_uq, v_w_uq), (w_ukv, m_w_ukv, v_w_ukv), (w_out, m_w_out, v_w_out)]
    res = _reduce_adam(grads3, small_part, wmv, small_wmv)
    big = [res[4 * t:4 * t + 4] for t in range(4)]
    small = res[16]
    loss = (0.5 / D_MODEL) * jnp.sum(small[0, 4])

    def group(kind):
        s = small[kind]
        return (big[0][kind], s[2, :Q_RANK], s[3, :KV_RANK], big[1][kind], big[2][kind], big[3][kind], s[0], s[1])

    return (loss, grad_x[None], *group(0), *group(1), *group(2), *group(3))
```

```python
import numpy as np
import jax
import jax.numpy as jnp
from jax import lax
from jax.experimental import pallas as pl
from jax.experimental.pallas import tpu as pltpu

F32 = jnp.float32
BF16 = jnp.bfloat16

D_MODEL = 1024
ROPE_THETA = 500000.0
NEG = -1e30
RMS_EPS = 1e-6
LN_EPS = 1e-5
HEADS = 8
MLA_NOPE = 64
MLA_ROPE = 32
MLA_V = 64
Q_RANK = 384
KV_RANK = 256
DIL_HEAD = 64
DIL_ROT = 16
DIL_CONFIGS = ((128, 1), (512, 4), (2048, 16))
HW = HEADS * 64
QW = HW + HEADS * MLA_ROPE
IN_SPLITS = (Q_RANK, KV_RANK, MLA_ROPE, HW, HW, HW, HW, HW)
IN_WIDTH = sum(IN_SPLITS)
ALPHA = 2.0 ** 0.25
MLA_SCALE = (MLA_NOPE + MLA_ROPE) ** -0.5
DIL_SCALE = DIL_HEAD ** -0.5
LOG2E = 1.4426950408889634
LN2 = 0.6931471805599453

ADAM_LR = 0.001
ADAM_B1 = 0.9
ADAM_B2 = 0.999
ADAM_EPS = 1e-08
ADAM_WD = 0.01
ADAM_STEP = 10

N_DEV = 8
LANES = 128
VMEM_LIMIT = 56 * 1024 * 1024
BLOCK_TOKENS = 512
BLOCK_MLA = 512
BLOCK_DIL = 512

C_CQ, C_CKV, C_KR, C_GA, C_QB, C_KB, C_VB, C_GB, C_END = 0, 384, 640, 768, 1280, 1792, 2304, 2816, 3328

NT = (((1,), (1,)), ((), ()))
TN = (((0,), (0,)), ((), ()))


def _pcall(body, **kw):
    return pl.pallas_call(body, **kw)


def _cparams(**kw):
    return pltpu.CompilerParams(vmem_limit_bytes=VMEM_LIMIT, **kw)


def _rope_tables(seq):
    def tabs(dim, period):
        half = dim // 2
        inv = np.float32(ROPE_THETA) ** (-np.arange(0, dim, 2, dtype=np.float32) / np.float32(dim))
        ang = np.arange(seq, dtype=np.float32)[:, None] * inv.astype(np.float32)[None, :]
        cos, sin = np.cos(ang).astype(np.float32), np.sin(ang).astype(np.float32)
        j = np.arange(LANES) % period
        f = j % half
        c = np.where(j < dim, cos[:, f], np.float32(1.0))
        s1 = np.where(j < half, -sin[:, f], np.float32(0.0))
        s2 = np.where((j >= half) & (j < dim), sin[:, f], np.float32(0.0))
        return [c, s1, s2]
    return np.stack(tabs(MLA_ROPE, MLA_ROPE) + tabs(DIL_ROT, DIL_HEAD)).astype(np.float32)


def _rope(t, c, s1, s2, half):
    return t * c + pltpu.roll(t, LANES - half, 1) * s1 + pltpu.roll(t, half, 1) * s2


def _rope_t(d, c, s1, s2, half):
    return d * c + pltpu.roll(d * s1, half, 1) + pltpu.roll(d * s2, LANES - half, 1)


def _rope_wide(fn, t, c, s1, s2, half):
    return jnp.concatenate(
        [fn(t[:, i:i + LANES], c, s1, s2, half) for i in range(0, t.shape[1], LANES)], axis=1)


def _mla_bias_t(blk):
    a = np.arange(blk)
    causal = np.where(a[:, None] <= a[None, :], 0.0, NEG)
    return np.stack([np.zeros((blk, blk)), causal]).astype(np.float32)


def _dil_bias_t(blk):
    span = DIL_CONFIGS[-1][0] // blk
    a = np.arange(blk)
    out = []
    for off in range(span + 1):
        delta = blk * off + a[None, :] - a[:, None]
        mult = np.zeros((blk, blk))
        for window, dil in DIL_CONFIGS:
            mult += (delta >= 0) & (delta % dil == 0) & (delta <= window)
        out.append(np.where(mult > 0, np.log2(np.maximum(mult, 1.0)), NEG))
    return np.stack(out).astype(np.float32)


def _steps(nq, span, by_key, diag_only_bias):
    rows = []
    if by_key:
        for ki in range(nq):
            hi = min(nq - 1, ki + span)
            for qi in range(ki, hi + 1):
                rows.append((qi, ki, int(qi == ki), int(qi == hi)))
    else:
        for qi in range(nq):
            lo = max(0, qi - span)
            for ki in range(lo, qi + 1):
                rows.append((qi, ki, int(ki == lo), int(ki == qi)))
    arr = np.array(rows, dtype=np.int32)
    off = arr[:, 0] - arr[:, 1]
    bias_idx = (off == 0).astype(np.int32) if diag_only_bias else off.astype(np.int32)
    return [jnp.asarray(v) for v in (arr[:, 0], arr[:, 1], bias_idx, arr[:, 2], arr[:, 3])]


def _fwd_proj(x, w_in_r, w_uq_r, w_ukv_r, qg, kvg, tabs, bt):
    seq = x.shape[0]

    def body(x_ref, win_ref, wuq_ref, wukv_ref, qg_ref, kvg_ref, tab_ref,
             cq_ref, ckv_ref, qn_ref, kvn_ref, qcat_ref, kn_ref, kpe_ref, v_ref,
             ga_ref, gb_ref, qb_ref, kb_ref, vb_ref, knt_ref, kpet_ref, vt_ref, kbt_ref, vbt_ref):
        xb = x_ref[...].astype(BF16)

        def proj(lo, hi):
            return jnp.dot(xb, win_ref[:, lo:hi], preferred_element_type=F32)

        m_tabs = (tab_ref[0], tab_ref[1], tab_ref[2])
        d_tabs = (tab_ref[3], tab_ref[4], tab_ref[5])

        cq = proj(C_CQ, C_CKV)
        cq_ref[...] = cq
        qn = (cq * lax.rsqrt(jnp.mean(cq * cq, axis=1, keepdims=True) + RMS_EPS) * qg_ref[...]).astype(BF16)
        qn_ref[...] = qn
        q = jnp.dot(qn, wuq_ref[...], preferred_element_type=F32)
        qcat_ref[:, :HW] = (q[:, :HW] * (MLA_SCALE * LOG2E)).astype(BF16)
        qcat_ref[:, HW:] = (_rope_wide(_rope, q[:, HW:], *m_tabs, MLA_ROPE // 2) * (MLA_SCALE * LOG2E)).astype(BF16)

        ckv = proj(C_CKV, C_KR)
        ckv_ref[...] = ckv
        kvn = (ckv * lax.rsqrt(jnp.mean(ckv * ckv, axis=1, keepdims=True) + RMS_EPS) * kvg_ref[...]).astype(BF16)
        kvn_ref[...] = kvn
        kv = jnp.dot(kvn, wukv_ref[...], preferred_element_type=F32)
        kn_ref[...] = kv[:, :HW].astype(BF16)
        v_ref[...] = kv[:, HW:].astype(BF16)
        knt_ref[...] = kv[:, :HW].T.astype(BF16)
        vt_ref[...] = kv[:, HW:].T.astype(BF16)

        kpe = _rope(proj(C_KR, C_GA), *m_tabs, MLA_ROPE // 2)
        kpe_ref[...] = kpe.astype(BF16)
        kpet_ref[...] = kpe.T[:MLA_ROPE, :].astype(BF16)
        ga_ref[...] = proj(C_GA, C_QB)
        qb_ref[...] = (_rope_wide(_rope, proj(C_QB, C_KB), *d_tabs, DIL_ROT // 2) * (DIL_SCALE * LOG2E)).astype(BF16)
        kb = _rope_wide(_rope, proj(C_KB, C_VB), *d_tabs, DIL_ROT // 2)
        kb_ref[...] = kb.astype(BF16)
        kbt_ref[...] = kb.T.astype(BF16)
        vb = proj(C_VB, C_GB)
        vb_ref[...] = vb.astype(BF16)
        vbt_ref[...] = vb.T.astype(BF16)
        gb_ref[...] = proj(C_GB, C_END)

    def tok(width):
        return pl.BlockSpec((bt, width), lambda i: (i, 0))

    def tok_t(height):
        return pl.BlockSpec((height, bt), lambda i: (0, i))

    def full(a):
        return pl.BlockSpec(a.shape, lambda i: (0,) * a.ndim)

    outs = [(Q_RANK, F32), (KV_RANK, F32), (Q_RANK, BF16), (KV_RANK, BF16), (QW, BF16), (HW, BF16),
            (LANES, BF16), (HW, BF16), (HW, F32), (HW, F32), (HW, BF16), (HW, BF16), (HW, BF16)]
    outs_t = [HW, MLA_ROPE, HW, HW, HW]
    return _pcall(
        body, name="fwd_proj", grid=(seq // bt,),
        in_specs=[tok(D_MODEL), full(w_in_r), full(w_uq_r), full(w_ukv_r), full(qg), full(kvg),
                  pl.BlockSpec((6, bt, LANES), lambda i: (0, i, 0))],
        out_specs=[tok(w) for w, _ in outs] + [tok_t(h) for h in outs_t],
        out_shape=[jax.ShapeDtypeStruct((seq, w), dt) for w, dt in outs]
        + [jax.ShapeDtypeStruct((h, seq), BF16) for h in outs_t],
        compiler_params=_cparams(dimension_semantics=("arbitrary",)),
    )(x, w_in_r, w_uq_r, w_ukv_r, qg, kvg, tabs)


def _head_masks(lane, h):
    e, g = h % 2, h % 4
    me = (lane >= 64 * e) & (lane < 64 * e + 64)
    mr = (lane >= 32 * g) & (lane < 32 * g + 32)
    return me, mr


def _masked(mask, a):
    return jnp.where(mask, a, jnp.zeros_like(a))


def _pair_operands(q_ref, k_ref, kpe_ref, lane, j):
    cols = slice(LANES * j, LANES * (j + 1))
    qc = q_ref[:, cols]
    kj = k_ref[:, cols]
    kes = []
    for h in (2 * j, 2 * j + 1):
        me, mr = _head_masks(lane, h)
        ke = _masked(me, kj)
        if kpe_ref is not None:
            ke = jnp.concatenate([ke, _masked(mr, kpe_ref[...])], axis=1)
        kes.append(ke)
    if kpe_ref is not None:
        qc = jnp.concatenate([qc, q_ref[:, HW + LANES * (j // 2):HW + LANES * (j // 2 + 1)]], axis=1)
    return qc, kes


def _attn_fwd(name, q, k, kpe, vt, bias_t, steps, blk):
    seq = q.shape[0]
    mla = kpe is not None
    n_steps = int(steps[0].shape[0])

    def body(qi_r, ki_r, bi_r, fi_r, la_r, *refs):
        if mla:
            q_ref, k_ref, kpe_ref, vt_ref, b_ref, o_ref, lse_ref, m_sc, l_sc, acc_sc, st_sc = refs
        else:
            q_ref, k_ref, vt_ref, b_ref, o_ref, lse_ref, m_sc, l_sc, acc_sc, st_sc = refs
        t = pl.program_id(0)

        @pl.when(fi_r[t] == 1)
        def _():
            m_sc[...] = jnp.full(m_sc.shape, NEG, F32)
            l_sc[...] = jnp.zeros(l_sc.shape, F32)
            acc_sc[...] = jnp.zeros(acc_sc.shape, F32)

        lane = lax.broadcasted_iota(jnp.int32, (1, LANES), 1)
        ones = jnp.ones((16, blk), BF16)

        def pair_scores(j, with_bias):
            qc, kes = _pair_operands(q_ref, k_ref, kpe_ref if mla else None, lane, j)
            st = lax.dot_general(jnp.concatenate(kes, axis=0), qc, NT, preferred_element_type=F32)
            maxes = []
            for e in range(2):
                se = st[e * blk:(e + 1) * blk]
                if with_bias:
                    se = se + b_ref[0]
                st_sc[j % 2, e * blk:(e + 1) * blk] = se
                maxes.append(jnp.max(se, axis=0, keepdims=True))
            return maxes

        def softmax_pv(h, col_max):
            st = st_sc[(h // 2) % 2, (h % 2) * blk:(h % 2 + 1) * blk]
            hrow = slice(h, h + 1)
            m_prev = m_sc[hrow, :]
            m_new = jnp.maximum(m_prev, col_max)
            alpha = jnp.exp2(m_prev - m_new)
            pt = jnp.exp2(st - m_new).astype(BF16)
            m_sc[hrow, :] = m_new
            rows = slice(64 * h, 64 * h + 64)
            res = jnp.dot(jnp.concatenate([vt_ref[rows, :], ones], axis=0), pt, preferred_element_type=F32)
            acc_sc[rows, :] = alpha * acc_sc[rows, :] + res[:64]
            l_sc[hrow, :] = alpha * l_sc[hrow, :] + res[64:65]

        def step(with_bias):
            maxes = pair_scores(0, with_bias)
            for j in range(HEADS // 2):
                cur = maxes
                if j + 1 < HEADS // 2:
                    maxes = pair_scores(j + 1, with_bias)
                softmax_pv(2 * j, cur[0])
                softmax_pv(2 * j + 1, cur[1])

        if mla:
            pl.when(bi_r[t] == 1)(lambda: step(True))
            pl.when(bi_r[t] == 0)(lambda: step(False))
        else:
            step(True)

        @pl.when(la_r[t] == 1)
        def _():
            for h in range(HEADS):
                rows = slice(64 * h, 64 * h + 64)
                acc_sc[rows, :] = acc_sc[rows, :] / l_sc[h:h + 1, :]
            o_ref[...] = acc_sc[...].T
            lse_ref[...] = m_sc[...] + jnp.log2(l_sc[...])

    qmap = lambda t, qi, ki, bi, fi, la: (qi[t], 0)
    kmap = lambda t, qi, ki, bi, fi, la: (ki[t], 0)
    in_specs = [pl.BlockSpec((blk, q.shape[1]), qmap), pl.BlockSpec((blk, HW), kmap)]
    args = [q, k]
    if mla:
        in_specs.append(pl.BlockSpec((blk, LANES), kmap))
        args.append(kpe)
    in_specs += [pl.BlockSpec((HW, blk), lambda t, qi, ki, bi, fi, la: (0, ki[t])),
                 pl.BlockSpec((1, blk, blk), lambda t, qi, ki, bi, fi, la: (bi[t], 0, 0))]
    args += [vt, bias_t]
    return _pcall(
        body, name=name,
        grid_spec=pltpu.PrefetchScalarGridSpec(
            num_scalar_prefetch=5, grid=(n_steps,), in_specs=in_specs,
            out_specs=[pl.BlockSpec((blk, HW), qmap),
                       pl.BlockSpec((HEADS, blk), lambda t, qi, ki, bi, fi, la: (0, qi[t]))],
            scratch_shapes=[pltpu.VMEM((HEADS, blk), F32), pltpu.VMEM((HEADS, blk), F32),
                            pltpu.VMEM((HW, blk), F32), pltpu.VMEM((2, 2 * blk, blk), F32)]),
        out_shape=[jax.ShapeDtypeStruct((seq, HW), F32), jax.ShapeDtypeStruct((HEADS, seq), F32)],
        compiler_params=_cparams(dimension_semantics=("arbitrary",)),
    )(*steps, *args)


def _attn_bwd(name, q, k, kpe, v, kt, kpet, bias_t, do, lse, dstat, steps, blk):
    seq = q.shape[0]
    mla = kpe is not None
    qw = q.shape[1]
    n_steps = int(steps[0].shape[0])
    dk_dtype = BF16 if mla else F32

    def body(qi_r, ki_r, bi_r, fi_r, la_r, *refs):
        if mla:
            (q_ref, k_ref, kpe_ref, v_ref, kt_ref, kpet_ref, b_ref, do_ref, lse_ref, d_ref,
             dq_ref, dk_ref, dkpe_ref, dv_ref, dk_sc, dkpe_sc, dv_sc, st_sc, dpt_sc) = refs
        else:
            (q_ref, k_ref, v_ref, kt_ref, b_ref, do_ref, lse_ref, d_ref,
             dq_ref, dk_ref, dv_ref, dk_sc, dv_sc, st_sc, dpt_sc) = refs
        t = pl.program_id(0)

        @pl.when(t == 0)
        def _():
            dq_ref[...] = jnp.zeros(dq_ref.shape, F32)

        @pl.when(fi_r[t] == 1)
        def _():
            dk_sc[...] = jnp.zeros(dk_sc.shape, F32)
            dv_sc[...] = jnp.zeros(dv_sc.shape, F32)
            if mla:
                dkpe_sc[...] = jnp.zeros(dkpe_sc.shape, F32)

        qi = qi_r[t]
        lane = lax.broadcasted_iota(jnp.int32, (1, LANES), 1)

        def pair_matmuls(j):
            cols = slice(LANES * j, LANES * (j + 1))
            qc, kes = _pair_operands(q_ref, k_ref, kpe_ref if mla else None, lane, j)
            st_sc[j % 2] = lax.dot_general(jnp.concatenate(kes, axis=0), qc, NT, preferred_element_type=F32)
            vj = v_ref[:, cols]
            ves = [_masked(_head_masks(lane, h)[0], vj) for h in (2 * j, 2 * j + 1)]
            dpt_sc[j % 2] = lax.dot_general(
                jnp.concatenate(ves, axis=0), do_ref[:, cols], NT, preferred_element_type=F32)

        def pair_grads(j, with_bias):
            cols = slice(LANES * j, LANES * (j + 1))
            qj, doj = q_ref[:, cols], do_ref[:, cols]
            if mla:
                qr = q_ref[:, HW + LANES * (j // 2):HW + LANES * (j // 2 + 1)]
            pts, dsts, qms, doms = [], [], [], []
            for e in range(2):
                h = 2 * j + e
                me, mr = _head_masks(lane, h)
                st = st_sc[j % 2, e * blk:(e + 1) * blk]
                if with_bias:
                    st = st + b_ref[0]
                pt = jnp.exp2(st - lse_ref[h:h + 1, :])
                dst = (pt * (dpt_sc[j % 2, e * blk:(e + 1) * blk] - d_ref[h:h + 1, :])).astype(BF16)
                pts.append(pt.astype(BF16))
                dsts.append(dst)
                doms.append(_masked(me, doj))
                qm = _masked(me, qj)
                if mla:
                    qm = jnp.concatenate([qm, _masked(mr, qr)], axis=1)
                qms.append(qm)
                ktl = kt_ref[64 * h:64 * h + 64, :]
                if mla:
                    ktl = jnp.concatenate([ktl, kpet_ref[...]], axis=0)
                dqc = jnp.dot(ktl, dst, preferred_element_type=F32)
                dq_ref[qi, 64 * h:64 * h + 64, :] += dqc[:64]
                if mla:
                    dq_ref[qi, HW + MLA_ROPE * h:HW + MLA_ROPE * (h + 1), :] += dqc[64:]
            dv_sc[:, cols] += jnp.dot(
                jnp.concatenate(pts, axis=1), jnp.concatenate(doms, axis=0), preferred_element_type=F32)
            dkc = jnp.dot(jnp.concatenate(dsts, axis=1), jnp.concatenate(qms, axis=0), preferred_element_type=F32)
            dk_sc[:, cols] += dkc[:, :LANES]
            if mla:
                dkpe_sc[...] += dkc[:, LANES:]

        def step(with_bias):
            pair_matmuls(0)
            for j in range(HEADS // 2):
                if j + 1 < HEADS // 2:
                    pair_matmuls(j + 1)
                pair_grads(j, with_bias)

        if mla:
            pl.when(bi_r[t] == 1)(lambda: step(True))
            pl.when(bi_r[t] == 0)(lambda: step(False))
        else:
            step(True)

        @pl.when(la_r[t] == 1)
        def _():
            dk_ref[...] = (dk_sc[...] * LN2).astype(dk_ref.dtype)
            dv_ref[...] = dv_sc[...].astype(dv_ref.dtype)
            if mla:
                dkpe_ref[...] = dkpe_sc[...] * LN2

    qmap = lambda t, qi, ki, bi, fi, la: (qi[t], 0)
    kmap = lambda t, qi, ki, bi, fi, la: (ki[t], 0)
    qmap_t = lambda t, qi, ki, bi, fi, la: (0, qi[t])
    kmap_t = lambda t, qi, ki, bi, fi, la: (0, ki[t])
    in_specs = [pl.BlockSpec((blk, qw), qmap), pl.BlockSpec((blk, HW), kmap)]
    args = [q, k]
    if mla:
        in_specs.append(pl.BlockSpec((blk, LANES), kmap))
        args.append(kpe)
    in_specs += [pl.BlockSpec((blk, HW), kmap), pl.BlockSpec((HW, blk), kmap_t)]
    args += [v, kt]
    if mla:
        in_specs.append(pl.BlockSpec((MLA_ROPE, blk), kmap_t))
        args.append(kpet)
    in_specs += [pl.BlockSpec((1, blk, blk), lambda t, qi, ki, bi, fi, la: (bi[t], 0, 0)),
                 pl.BlockSpec((blk, HW), qmap), pl.BlockSpec((HEADS, blk), qmap_t), pl.BlockSpec((HEADS, blk), qmap_t)]
    args += [bias_t, do, lse, dstat]
    dq_shape = (seq // blk, qw, blk)
    out_specs = [pl.BlockSpec(dq_shape, lambda t, qi, ki, bi, fi, la: (0, 0, 0)), pl.BlockSpec((blk, HW), kmap)]
    out_shape = [jax.ShapeDtypeStruct(dq_shape, F32), jax.ShapeDtypeStruct((seq, HW), dk_dtype)]
    scratch = [pltpu.VMEM((blk, HW), F32)]
    if mla:
        out_specs.append(pl.BlockSpec((blk, LANES), kmap))
        out_shape.append(jax.ShapeDtypeStruct((seq, LANES), F32))
        scratch.append(pltpu.VMEM((blk, LANES), F32))
    out_specs.append(pl.BlockSpec((blk, HW), kmap))
    out_shape.append(jax.ShapeDtypeStruct((seq, HW), BF16))
    scratch.append(pltpu.VMEM((blk, HW), F32))
    scratch += [pltpu.VMEM((2, 2 * blk, blk), F32), pltpu.VMEM((2, 2 * blk, blk), F32)]
    return _pcall(
        body, name=name,
        grid_spec=pltpu.PrefetchScalarGridSpec(
            num_scalar_prefetch=5, grid=(n_steps,), in_specs=in_specs, out_specs=out_specs,
            scratch_shapes=scratch),
        out_shape=out_shape,
        compiler_params=_cparams(dimension_semantics=("arbitrary",)),
    )(*steps, *args)


def _out_ln(oa, ob, ga, gb, x, tgt, w_out, ln_g, ln_b, bt):
    seq = x.shape[0]

    def body(oa_ref, ob_ref, ga_ref, gb_ref, x_ref, tgt_ref, w_ref, g_ref, b_ref,
             dz_ref, doa_ref, dob_ref, dga_ref, dgb_ref, da_ref, db_ref, gw_ref, small_ref):
        i = pl.program_id(0)

        @pl.when(i == 0)
        def _():
            gw_ref[...] = jnp.zeros(gw_ref.shape, F32)
            small_ref[...] = jnp.zeros(small_ref.shape, F32)

        def gate(g):
            sig = 1.0 / (1.0 + jnp.exp(-g))
            return g * sig, sig * (1.0 + g * (1.0 - sig))

        o_a, o_b = oa_ref[...], ob_ref[...]
        g_a, g_b = ga_ref[...], gb_ref[...]
        sa, dsa = gate(g_a)
        sb, dsb = gate(g_b)
        mix = jnp.concatenate([o_a * sa, o_b * sb], axis=1).astype(BF16)
        z = ALPHA * x_ref[...] + jnp.dot(mix, w_ref[...], preferred_element_type=F32)
        mu = jnp.mean(z, axis=1, keepdims=True)
        zc = z - mu
        rstd = lax.rsqrt(jnp.mean(zc * zc, axis=1, keepdims=True) + LN_EPS)
        xhat = zc * rstd
        gam = g_ref[...]
        diff = xhat * gam + b_ref[...] - tgt_ref[...]
        dy = diff * (1.0 / D_MODEL)
        small_ref[0:1, :] += jnp.sum(dy * xhat, axis=0, keepdims=True)
        small_ref[1:2, :] += jnp.sum(dy, axis=0, keepdims=True)
        small_ref[2:3, :] += jnp.sum(diff * diff, axis=0, keepdims=True)
        dxh = dy * gam
        dz = rstd * (dxh - jnp.mean(dxh, axis=1, keepdims=True) - xhat * jnp.mean(dxh * xhat, axis=1, keepdims=True))
        dz_ref[...] = dz
        dzb = dz.astype(BF16)
        gw_ref[...] += lax.dot_general(mix, dzb, TN, preferred_element_type=F32)
        dmix = lax.dot_general(dzb, w_ref[...], NT, preferred_element_type=F32)
        doa, dob = dmix[:, :HW] * sa, dmix[:, HW:] * sb
        doa_ref[...] = doa.astype(BF16)
        dob_ref[...] = dob.astype(BF16)
        dga_ref[...] = (dmix[:, :HW] * o_a * dsa).astype(BF16)
        dgb_ref[...] = (dmix[:, HW:] * o_b * dsb).astype(BF16)
        head_of = lax.broadcasted_iota(jnp.int32, (HEADS, HW), 1) // 64
        ind = (head_of == lax.broadcasted_iota(jnp.int32, (HEADS, HW), 0)).astype(F32)
        da_ref[...] = lax.dot_general(ind, doa * o_a, NT, preferred_element_type=F32, precision=lax.Precision.HIGHEST)
        db_ref[...] = lax.dot_general(ind, dob * o_b, NT, preferred_element_type=F32, precision=lax.Precision.HIGHEST)

    def tok(width):
        return pl.BlockSpec((bt, width), lambda i: (i, 0))

    def full(shape):
        return pl.BlockSpec(shape, lambda i: (0,) * len(shape))

    stat = pl.BlockSpec((HEADS, bt), lambda i: (0, i))
    return _pcall(
        body, name="out_ln", grid=(seq // bt,),
        in_specs=[tok(HW), tok(HW), tok(HW), tok(HW), tok(D_MODEL), tok(D_MODEL),
                  full((D_MODEL, D_MODEL)), full((1, D_MODEL)), full((1, D_MODEL))],
        out_specs=[tok(D_MODEL), tok(HW), tok(HW), tok(HW), tok(HW), stat, stat,
                   full((D_MODEL, D_MODEL)), full((8, D_MODEL))],
        out_shape=[jax.ShapeDtypeStruct((seq, D_MODEL), F32)] + [jax.ShapeDtypeStruct((seq, HW), BF16)] * 4
        + [jax.ShapeDtypeStruct((HEADS, seq), F32)] * 2
        + [jax.ShapeDtypeStruct((D_MODEL, D_MODEL), F32), jax.ShapeDtypeStruct((8, D_MODEL), F32)],
        compiler_params=_cparams(dimension_semantics=("arbitrary",)),
    )(oa, ob, ga, gb, x, tgt, w_out, ln_g, ln_b)


def _bwd_mid(dq_m, dkn, dv, dkpe, dqb, dkb, dvb, dga, dgb, cq, ckv, qn, kvn, w_uq_r, w_ukv_r, qg, kvg, tabs, bt):
    seq = cq.shape[0]

    def body(dqm_ref, dkn_ref, dv_ref, dkpe_ref, dqb_ref, dkb_ref, dvb_ref, dga_ref, dgb_ref,
             cq_ref, ckv_ref, qn_ref, kvn_ref, wuq_ref, wukv_ref, qg_ref, kvg_ref, tab_ref,
             dh_ref, guq_ref, gukv_ref, small_ref):
        i = pl.program_id(0)

        @pl.when(i == 0)
        def _():
            guq_ref[...] = jnp.zeros(guq_ref.shape, F32)
            gukv_ref[...] = jnp.zeros(gukv_ref.shape, F32)
            small_ref[...] = jnp.zeros(small_ref.shape, F32)

        m_tabs = (tab_ref[0], tab_ref[1], tab_ref[2])
        d_tabs = (tab_ref[3], tab_ref[4], tab_ref[5])

        def rms_bwd(c, dn, gain):
            r = lax.rsqrt(jnp.mean(c * c, axis=1, keepdims=True) + RMS_EPS)
            u = dn * gain
            dc = r * u - c * (r * r * r) * jnp.mean(u * c, axis=1, keepdims=True)
            return dc, jnp.sum(dn * c * r, axis=0, keepdims=True)

        dqm = dqm_ref[0].T
        dq = jnp.concatenate(
            [dqm[:, :HW], _rope_wide(_rope_t, dqm[:, HW:], *m_tabs, MLA_ROPE // 2)], axis=1) * MLA_SCALE
        dq = dq.astype(BF16)
        guq_ref[...] += lax.dot_general(qn_ref[...], dq, TN, preferred_element_type=F32)
        dqn = lax.dot_general(dq, wuq_ref[...], NT, preferred_element_type=F32)
        dcq, gq = rms_bwd(cq_ref[...], dqn, qg_ref[...])
        small_ref[0:1, :] += gq

        dkv = jnp.concatenate([dkn_ref[...], dv_ref[...]], axis=1)
        gukv_ref[...] += lax.dot_general(kvn_ref[...], dkv, TN, preferred_element_type=F32)
        dkvn = lax.dot_general(dkv, wukv_ref[...], NT, preferred_element_type=F32)
        dckv, gkv = rms_bwd(ckv_ref[...], dkvn, kvg_ref[...])
        small_ref[1:2, :KV_RANK] += gkv

        dh_ref[:, C_CQ:C_CKV] = dcq.astype(BF16)
        dh_ref[:, C_CKV:C_KR] = dckv.astype(BF16)
        dh_ref[:, C_KR:C_GA] = _rope_t(dkpe_ref[...], *m_tabs, MLA_ROPE // 2).astype(BF16)
        dh_ref[:, C_GA:C_QB] = dga_ref[...]
        dh_ref[:, C_QB:C_KB] = (_rope_wide(_rope_t, dqb_ref[0].T, *d_tabs, DIL_ROT // 2) * DIL_SCALE).astype(BF16)
        dh_ref[:, C_KB:C_VB] = _rope_wide(_rope_t, dkb_ref[...], *d_tabs, DIL_ROT // 2).astype(BF16)
        dh_ref[:, C_VB:C_GB] = dvb_ref[...]
        dh_ref[:, C_GB:C_END] = dgb_ref[...]

    def tok(width):
        return pl.BlockSpec((bt, width), lambda i: (i, 0))

    def tok_t(a):
        per = a.shape[2] // bt
        return pl.BlockSpec((1, a.shape[1], bt), lambda i: (i // per, 0, i % per))

    def full(shape):
        return pl.BlockSpec(shape, lambda i: (0,) * len(shape))

    return _pcall(
        body, name="bwd_mid", grid=(seq // bt,),
        in_specs=[tok_t(dq_m), tok(HW), tok(HW), tok(LANES), tok_t(dqb), tok(HW), tok(HW), tok(HW), tok(HW),
                  tok(Q_RANK), tok(KV_RANK), tok(Q_RANK), tok(KV_RANK),
                  full(w_uq_r.shape), full(w_ukv_r.shape), full((1, Q_RANK)), full((1, KV_RANK)),
                  pl.BlockSpec((6, bt, LANES), lambda i: (0, i, 0))],
        out_specs=[tok(C_END), full(w_uq_r.shape), full(w_ukv_r.shape), full((8, Q_RANK))],
        out_shape=[jax.ShapeDtypeStruct((seq, C_END), BF16), jax.ShapeDtypeStruct(w_uq_r.shape, F32),
                   jax.ShapeDtypeStruct(w_ukv_r.shape, F32), jax.ShapeDtypeStruct((8, Q_RANK), F32)],
        compiler_params=_cparams(dimension_semantics=("arbitrary",)),
    )(dq_m, dkn, dv, dkpe, dqb, dkb, dvb, dga, dgb, cq, ckv, qn, kvn, w_uq_r, w_ukv_r, qg, kvg, tabs)


def _grad_x(dz, dh, w_in_r, bt):
    seq = dz.shape[0]

    def body(dz_ref, dh_ref, w_ref, gx_ref):
        gx_ref[...] = ALPHA * dz_ref[...] + lax.dot_general(
            dh_ref[...], w_ref[...], NT, preferred_element_type=F32)

    return _pcall(
        body, name="grad_x", grid=(seq // bt,),
        in_specs=[pl.BlockSpec((bt, D_MODEL), lambda i: (i, 0)), pl.BlockSpec((bt, C_END), lambda i: (i, 0)),
                  pl.BlockSpec(w_in_r.shape, lambda i: (0, 0))],
        out_specs=pl.BlockSpec((bt, D_MODEL), lambda i: (i, 0)),
        out_shape=jax.ShapeDtypeStruct((seq, D_MODEL), F32),
        compiler_params=_cparams(dimension_semantics=("arbitrary",)),
    )(dz, dh, w_in_r)


def _grad_w_in(x, dh, bt):
    seq = x.shape[0]
    shard = IN_WIDTH // N_DEV
    k_lo, k_hi = IN_SPLITS[0] + IN_SPLITS[1], IN_SPLITS[0] + IN_SPLITS[1] + MLA_ROPE

    def body(x_ref, dh_ref, out_ref, acc):
        i = pl.program_id(0)

        @pl.when(i == 0)
        def _():
            acc[...] = jnp.zeros(acc.shape, F32)

        acc[...] += lax.dot_general(x_ref[...].astype(BF16), dh_ref[...], TN, preferred_element_type=F32)

        @pl.when(i == seq // bt - 1)
        def _():
            kr = acc[:, C_KR:C_GA]
            kr = kr + pltpu.roll(kr, 96, 1) + pltpu.roll(kr, 64, 1) + pltpu.roll(kr, 32, 1)
            for d in range(N_DEV):
                lo, hi = shard * d, shard * (d + 1)
                pieces = []
                if lo < k_lo:
                    pieces.append(acc[:, lo:min(hi, k_lo)])
                if lo < k_hi and hi > k_lo:
                    pieces.append(kr[:, max(lo, k_lo) - k_lo:min(hi, k_hi) - k_lo])
                if hi > k_hi:
                    shift = C_GA - k_hi
                    pieces.append(acc[:, max(lo, k_hi) + shift:hi + shift])
                blk = pieces[0] if len(pieces) == 1 else jnp.concatenate(pieces, axis=1)
                out_ref[d] = blk.astype(BF16)

    return _pcall(
        body, name="grad_w_in", grid=(seq // bt,),
        in_specs=[pl.BlockSpec((bt, D_MODEL), lambda i: (i, 0)), pl.BlockSpec((bt, C_END), lambda i: (i, 0))],
        out_specs=pl.BlockSpec((N_DEV, D_MODEL, shard), lambda i: (0, 0, 0)),
        out_shape=jax.ShapeDtypeStruct((N_DEV, D_MODEL, shard), BF16),
        scratch_shapes=[pltpu.VMEM((D_MODEL, C_END), F32)],
        compiler_params=_cparams(dimension_semantics=("arbitrary",)),
    )(x, dh)


def _restore_grads(g_uq_r, g_ukv_r):
    g_uq = jnp.concatenate(
        [g_uq_r[:, :HW].reshape(Q_RANK, HEADS, MLA_NOPE), g_uq_r[:, HW:].reshape(Q_RANK, HEADS, MLA_ROPE)],
        axis=2).reshape(Q_RANK, HEADS * (MLA_NOPE + MLA_ROPE))
    g_ukv = jnp.concatenate(
        [g_ukv_r[:, :HW].reshape(KV_RANK, HEADS, MLA_NOPE), g_ukv_r[:, HW:].reshape(KV_RANK, HEADS, MLA_V)],
        axis=2).reshape(KV_RANK, HEADS * (MLA_NOPE + MLA_V))
    return g_uq, g_ukv


def _local_step(x, tgt, w_in_r, w_uq_r, w_ukv_r, w_out, q_norm_g, kv_norm_g, ln_g, ln_b,
                bt=BLOCK_TOKENS, blk_m=BLOCK_MLA, blk_d=BLOCK_DIL):
    seq = x.shape[0]
    tabs = jnp.asarray(_rope_tables(seq))
    qg, kvg = q_norm_g.reshape(1, -1), kv_norm_g.reshape(1, -1)

    (cq, ckv, qn, kvn, qcat, kn, kpe, v, ga, gb, qb, kb, vb, knt, kpet, vt, kbt, vbt) = _fwd_proj(
        x, w_in_r, w_uq_r, w_ukv_r, qg, kvg, tabs, bt)

    nq_m, nq_d = seq // blk_m, seq // blk_d
    span_d = DIL_CONFIGS[-1][0] // blk_d
    bias_m, bias_d = jnp.asarray(_mla_bias_t(blk_m)), jnp.asarray(_dil_bias_t(blk_d))
    oa, lse_a = _attn_fwd("mla_fwd", qcat, kn, kpe, vt, bias_m, _steps(nq_m, nq_m, False, True), blk_m)
    ob, lse_b = _attn_fwd("dil_fwd", qb, kb, None, vbt, bias_d, _steps(nq_d, span_d, False, False), blk_d)

    dz, doa, dob, dga, dgb, dst_a, dst_b, g_out, small1 = _out_ln(
        oa, ob, ga, gb, x, tgt, w_out, ln_g.reshape(1, -1), ln_b.reshape(1, -1), bt)

    dq_m, dkn, dkpe, dv = _attn_bwd(
        "mla_bwd", qcat, kn, kpe, v, knt, kpet, bias_m, doa, lse_a, dst_a, _steps(nq_m, nq_m, True, True), blk_m)
    dqb, dkb, dvb = _attn_bwd(
        "dil_bwd", qb, kb, None, vb, kbt, None, bias_d, dob, lse_b, dst_b, _steps(nq_d, span_d, True, False), blk_d)

    dh, g_uq_r, g_ukv_r, small2 = _bwd_mid(
        dq_m, dkn, dv, dkpe, dqb, dkb, dvb, dga, dgb, cq, ckv, qn, kvn, w_uq_r, w_ukv_r, qg, kvg, tabs, bt)
    grad_x = _grad_x(dz, dh, w_in_r, bt)
    g_in = _grad_w_in(x, dh, bt)
    g_uq, g_ukv = _restore_grads(g_uq_r, g_ukv_r)

    return small1[2], grad_x, g_in, g_uq, g_ukv, g_out, small2[0], small2[1, :KV_RANK], small1[0], small1[1]


MESH_ID = pl.DeviceIdType.MESH
SHARD_SHAPES = ((D_MODEL, IN_WIDTH // N_DEV), (Q_RANK, 768 // N_DEV), (KV_RANK, 1024 // N_DEV), (D_MODEL // N_DEV, D_MODEL))
ADAM_ROWS = (32, 128, 128, 16)


def _me():
    x, y, c = lax.axis_index("x"), lax.axis_index("y"), lax.axis_index("c")
    return x, y, c, 4 * x + 2 * y + c


def _peer(k):
    x, y, c, _ = _me()
    px = 1 - x if (k >> 2) & 1 else x
    py = 1 - y if (k >> 1) & 1 else y
    pc = 1 - c if k & 1 else c
    return (px, py, pc), 4 * px + 2 * py + pc


def _all_gather_weights(shards):
    shard = IN_WIDTH // N_DEV
    k_lo = IN_SPLITS[0] + IN_SPLITS[1]
    k_hi = k_lo + MLA_ROPE

    def body(*refs):
        ins = refs[:4]
        win_ref, wuq_ref, wukv_ref, wout_ref = refs[4:8]
        bufs = list(refs[8:11]) + [wout_ref]
        send_sems, recv_sems = refs[11:]
        x, y, c, me = _me()
        here, sibling = (x, y, c), (x, y, 1 - c)
        chips = [(1 - x, y), (x, 1 - y), (1 - x, 1 - y)]
        for t in range(4):
            bufs[t][me] = ins[t][...].astype(BF16)

        def copy(t, k, px, py, pc, to):
            blk = bufs[t].at[4 * px + 2 * py + pc]
            return pltpu.make_async_remote_copy(
                src_ref=blk, dst_ref=blk, send_sem=send_sems.at[t, k], recv_sem=recv_sems.at[t, k],
                device_id=to, device_id_type=MESH_ID)

        first = []
        for t in range(4):
            first.append(copy(t, 0, x, y, c, sibling))
            for j, (px, py) in enumerate(chips):
                first.append(copy(t, 1 + j, x, y, c, (px, py, c)))
        for cp in first:
            cp.start()
        passed = []
        for j, (px, py) in enumerate(chips):
            for t in range(4):
                copy(t, 1 + j, px, py, c, here).wait_recv()
                cp = copy(t, 4 + j, px, py, c, sibling)
                cp.start()
                passed.append(cp)
        for t in range(4):
            copy(t, 0, x, y, 1 - c, here).wait_recv()
        for j, (px, py) in enumerate(chips):
            for t in range(4):
                copy(t, 4 + j, px, py, 1 - c, here).wait_recv()
        for cp in first + passed:
            cp.wait_send()

        a_in, a_uq, a_ukv = bufs[:3]
        for d in range(N_DEV):
            lo, hi = shard * d, shard * (d + 1)
            if lo < k_lo:
                win_ref[:, lo:min(hi, k_lo)] = a_in[d, :, 0:min(hi, k_lo) - lo]
            if lo < k_hi and hi > k_lo:
                kr = a_in[d, :, k_lo - lo:k_hi - lo]
                for rep in range(4):
                    win_ref[:, C_KR + MLA_ROPE * rep:C_KR + MLA_ROPE * (rep + 1)] = kr
            if hi > k_hi:
                src = max(lo, k_hi)
                win_ref[:, src + C_GA - k_hi:hi + C_GA - k_hi] = a_in[d, :, src - lo:hi - lo]
        for h in range(HEADS):
            wuq_ref[:, MLA_NOPE * h:MLA_NOPE * (h + 1)] = a_uq[h, :, :MLA_NOPE]
            wuq_ref[:, HW + MLA_ROPE * h:HW + MLA_ROPE * (h + 1)] = a_uq[h, :, MLA_NOPE:]
            wukv_ref[:, MLA_NOPE * h:MLA_NOPE * (h + 1)] = a_ukv[h, :, :MLA_NOPE]
            wukv_ref[:, HW + MLA_V * h:HW + MLA_V * (h + 1)] = a_ukv[h, :, MLA_NOPE:]

    vmem = pl.BlockSpec(memory_space=pltpu.VMEM)
    w_in_r, w_uq_r, w_ukv_r, a_out = _pcall(
        body, name="gather_weights",
        in_specs=[vmem] * 4, out_specs=[vmem] * 4,
        out_shape=[jax.ShapeDtypeStruct((D_MODEL, C_END), BF16), jax.ShapeDtypeStruct((Q_RANK, QW), BF16),
                   jax.ShapeDtypeStruct((KV_RANK, 2 * HW), BF16), jax.ShapeDtypeStruct((N_DEV,) + SHARD_SHAPES[3], BF16)],
        scratch_shapes=[pltpu.VMEM((N_DEV,) + s, BF16) for s in SHARD_SHAPES[:3]]
        + [pltpu.SemaphoreType.DMA((4, N_DEV - 1)), pltpu.SemaphoreType.DMA((4, N_DEV - 1))],
        compiler_params=_cparams(),
    )(*shards)
    return w_in_r, w_uq_r, w_ukv_r, a_out.reshape(D_MODEL, D_MODEL)


def _adamw(w, g, m, v):
    m = ADAM_B1 * m + (1.0 - ADAM_B1) * g
    v = ADAM_B2 * v + (1.0 - ADAM_B2) * jnp.square(g)
    m_hat = m / (1.0 - ADAM_B1 ** ADAM_STEP)
    v_hat = v / (1.0 - ADAM_B2 ** ADAM_STEP)
    delta = -ADAM_LR * (m_hat / (jnp.sqrt(v_hat) + ADAM_EPS) + ADAM_WD * w)
    return delta, m, v


def _reduce_adam(grads3, small_part, wmv, small_wmv):
    def body(*refs):
        g3 = refs[0:4]
        sp_ref = refs[4]
        wmv_refs = [refs[5 + 3 * t:8 + 3 * t] for t in range(4)]
        swmv_ref = refs[17]
        out_refs = [refs[18 + 4 * t:22 + 4 * t] for t in range(4)]
        sout_ref = refs[34]
        recv = refs[35:39]
        rsmall = refs[39]
        send_sems, recv_sems, local_sems = refs[40:43]
        me = _me()[3]

        rsmall[0] = sp_ref[...]
        local = []
        for t in range(4):
            cp = pltpu.make_async_copy(g3[t].at[me], recv[t].at[0], local_sems.at[t])
            cp.start()
            local.append(cp)
        sends = []
        for k in range(1, N_DEV):
            peer, pidx = _peer(k)
            for t in range(5):
                src = rsmall.at[0] if t == 4 else g3[t].at[pidx]
                dst = rsmall.at[k] if t == 4 else recv[t].at[k]
                cp = pltpu.make_async_remote_copy(
                    src_ref=src, dst_ref=dst, send_sem=send_sems.at[t, k - 1], recv_sem=recv_sems.at[t, k - 1],
                    device_id=peer, device_id_type=MESH_ID)
                cp.start()
                sends.append(cp)
        for cp in local:
            cp.wait()
        for cp in sends:
            cp.wait_recv()

        tot = rsmall[me]
        for d in range(1, N_DEV):
            tot = tot + rsmall[jnp.bitwise_xor(me, d)]
        delta, m, v = _adamw(swmv_ref[0], tot, swmv_ref[1], swmv_ref[2])
        sout_ref[0], sout_ref[1], sout_ref[2], sout_ref[3] = tot, delta, m, v

        for t in range(4):
            rows = ADAM_ROWS[t]
            w_ref, m_ref, v_ref = wmv_refs[t]
            g_out, d_out, m_out, v_out = out_refs[t]

            def step(i, carry, t=t, rows=rows, w_ref=w_ref, m_ref=m_ref, v_ref=v_ref,
                     g_out=g_out, d_out=d_out, m_out=m_out, v_out=v_out):
                r = pl.ds(pl.multiple_of(i * rows, rows), rows)
                g = recv[t][0, r, :].astype(F32)
                for k in range(1, N_DEV):
                    g = g + recv[t][k, r, :].astype(F32)
                delta, m, v = _adamw(w_ref[r, :], g, m_ref[r, :], v_ref[r, :])
                g_out[r, :], d_out[r, :], m_out[r, :], v_out[r, :] = g, delta, m, v
                return carry

            lax.fori_loop(0, SHARD_SHAPES[t][0] // rows, step, 0)

        for cp in sends:
            cp.wait_send()

    vmem = pl.BlockSpec(memory_space=pltpu.VMEM)
    hbm = pl.BlockSpec(memory_space=pl.ANY)
    flat_wmv = [a for trio in wmv for a in trio]
    return _pcall(
        body, name="reduce_adam",
        in_specs=[hbm] * 4 + [vmem] * 14,
        out_specs=[vmem] * 17,
        out_shape=[jax.ShapeDtypeStruct(s, F32) for s in SHARD_SHAPES for _ in range(4)]
        + [jax.ShapeDtypeStruct((4, 8, D_MODEL), F32)],
        scratch_shapes=[pltpu.VMEM((N_DEV,) + s, BF16) for s in SHARD_SHAPES]
        + [pltpu.VMEM((N_DEV, 8, D_MODEL), F32),
           pltpu.SemaphoreType.DMA((5, N_DEV - 1)), pltpu.SemaphoreType.DMA((5, N_DEV - 1)),
           pltpu.SemaphoreType.DMA((4,))],
        compiler_params=_cparams(),
    )(*grads3, small_part, *flat_wmv, small_wmv)


def _small_rows(ln_g, ln_b, q_norm_g, kv_norm_g, extra=None):
    pad = lambda a: jnp.pad(a, (0, D_MODEL - a.shape[0]))
    rows = [ln_g, ln_b, pad(q_norm_g), pad(kv_norm_g)] + ([] if extra is None else [extra])
    return jnp.pad(jnp.stack(rows), ((0, 8 - len(rows)), (0, 0)))


def kernel(x, w_in, q_norm_g, kv_norm_g, w_uq, w_ukv, w_out, ln_g, ln_b, loss_target, m_w_in, m_q_norm_g, m_kv_norm_g, m_w_uq, m_w_ukv, m_w_out, m_ln_g, m_ln_b, v_w_in, v_q_norm_g, v_kv_norm_g, v_w_uq, v_w_ukv, v_w_out, v_ln_g, v_ln_b):
    w_in_r, w_uq_r, w_ukv_r, full_out = _all_gather_weights([w_in, w_uq, w_ukv, w_out])
    sq_err, grad_x, g_in, g_uq, g_ukv, g_out, g_qg, g_kvg, g_lng, g_lnb = _local_step(
        x[0], loss_target[0], w_in_r, w_uq_r, w_ukv_r, full_out, q_norm_g, kv_norm_g, ln_g, ln_b)

    grads3 = [g_in] + [g.astype(BF16) for g in (
        g_uq.reshape(Q_RANK, N_DEV, -1).transpose(1, 0, 2), g_ukv.reshape(KV_RANK, N_DEV, -1).transpose(1, 0, 2),
        g_out.reshape(N_DEV, D_MODEL // N_DEV, D_MODEL))]
    small_part = _small_rows(g_lng, g_lnb, g_qg, g_kvg, sq_err)
    small_wmv = jnp.stack([_small_rows(ln_g, ln_b, q_norm_g, kv_norm_g),
                           _small_rows(m_ln_g, m_ln_b, m_q_norm_g, m_kv_norm_g),
                           _small_rows(v_ln_g, v_ln_b, v_q_norm_g, v_kv_norm_g)])
    wmv = [(w_in, m_w_in, v_w_in), (w_uq, m_w_uq, v_w_uq), (w_ukv, m_w_ukv, v_w_ukv), (w_out, m_w_out, v_w_out)]
    res = _reduce_adam(grads3, small_part, wmv, small_wmv)
    big = [res[4 * t:4 * t + 4] for t in range(4)]
    small = res[16]
    loss = (0.5 / D_MODEL) * jnp.sum(small[0, 4])

    def group(kind):
        s = small[kind]
        return (big[0][kind], s[2, :Q_RANK], s[3, :KV_RANK], big[1][kind], big[2][kind], big[3][kind], s[0], s[1])

    return (loss, grad_x[None], *group(0), *group(1), *group(2), *group(3))
```

```python
import numpy as np
import jax
import jax.numpy as jnp
from jax import lax
from jax.experimental import pallas as pl
from jax.experimental.pallas import tpu as pltpu

F32 = jnp.float32
BF16 = jnp.bfloat16

D_MODEL = 1024
ROPE_THETA = 500000.0
NEG = -1e30
RMS_EPS = 1e-6
LN_EPS = 1e-5
HEADS = 8
MLA_NOPE = 64
MLA_ROPE = 32
MLA_V = 64
Q_RANK = 384
KV_RANK = 256
DIL_HEAD = 64
DIL_ROT = 16
DIL_CONFIGS = ((128, 1), (512, 4), (2048, 16))
HW = HEADS * 64
QW = HW + HEADS * MLA_ROPE
IN_SPLITS = (Q_RANK, KV_RANK, MLA_ROPE, HW, HW, HW, HW, HW)
IN_WIDTH = sum(IN_SPLITS)
ALPHA = 2.0 ** 0.25
MLA_SCALE = (MLA_NOPE + MLA_ROPE) ** -0.5
DIL_SCALE = DIL_HEAD ** -0.5
LOG2E = 1.4426950408889634
LN2 = 0.6931471805599453

ADAM_LR = 0.001
ADAM_B1 = 0.9
ADAM_B2 = 0.999
ADAM_EPS = 1e-08
ADAM_WD = 0.01
ADAM_STEP = 10

N_DEV = 8
LANES = 128
VMEM_LIMIT = 56 * 1024 * 1024
BLOCK_TOKENS = 512
BLOCK_MLA = 512
BLOCK_DIL = 512

C_CQ, C_CKV, C_KR, C_GA, C_QB, C_KB, C_VB, C_GB, C_END = 0, 384, 640, 768, 1280, 1792, 2304, 2816, 3328

NT = (((1,), (1,)), ((), ()))
TN = (((0,), (0,)), ((), ()))


def _pcall(body, **kw):
    return pl.pallas_call(body, **kw)


def _cparams(**kw):
    return pltpu.CompilerParams(vmem_limit_bytes=VMEM_LIMIT, **kw)


def _rope_tables(seq):
    def tabs(dim, period):
        half = dim // 2
        inv = np.float32(ROPE_THETA) ** (-np.arange(0, dim, 2, dtype=np.float32) / np.float32(dim))
        ang = np.arange(seq, dtype=np.float32)[:, None] * inv.astype(np.float32)[None, :]
        cos, sin = np.cos(ang).astype(np.float32), np.sin(ang).astype(np.float32)
        j = np.arange(LANES) % period
        f = j % half
        c = np.where(j < dim, cos[:, f], np.float32(1.0))
        s1 = np.where(j < half, -sin[:, f], np.float32(0.0))
        s2 = np.where((j >= half) & (j < dim), sin[:, f], np.float32(0.0))
        return [c, s1, s2]
    return np.stack(tabs(MLA_ROPE, MLA_ROPE) + tabs(DIL_ROT, DIL_HEAD)).astype(np.float32)


def _rope(t, c, s1, s2, half):
    return t * c + pltpu.roll(t, LANES - half, 1) * s1 + pltpu.roll(t, half, 1) * s2


def _rope_t(d, c, s1, s2, half):
    return d * c + pltpu.roll(d * s1, half, 1) + pltpu.roll(d * s2, LANES - half, 1)


def _rope_wide(fn, t, c, s1, s2, half):
    return jnp.concatenate(
        [fn(t[:, i:i + LANES], c, s1, s2, half) for i in range(0, t.shape[1], LANES)], axis=1)


def _mla_bias_t(blk):
    a = np.arange(blk)
    causal = np.where(a[:, None] <= a[None, :], 0.0, NEG)
    return np.stack([np.zeros((blk, blk)), causal]).astype(np.float32)


def _dil_bias_t(blk):
    span = DIL_CONFIGS[-1][0] // blk
    a = np.arange(blk)
    out = []
    for off in range(span + 1):
        delta = blk * off + a[None, :] - a[:, None]
        mult = np.zeros((blk, blk))
        for window, dil in DIL_CONFIGS:
            mult += (delta >= 0) & (delta % dil == 0) & (delta <= window)
        out.append(np.where(mult > 0, np.log2(np.maximum(mult, 1.0)), NEG))
    return np.stack(out).astype(np.float32)


def _steps(nq, span, by_key, diag_only_bias):
    rows = []
    if by_key:
        for ki in range(nq):
            hi = min(nq - 1, ki + span)
            for qi in range(ki, hi + 1):
                rows.append((qi, ki, int(qi == ki), int(qi == hi)))
    else:
        for qi in range(nq):
            lo = max(0, qi - span)
            for ki in range(lo, qi + 1):
                rows.append((qi, ki, int(ki == lo), int(ki == qi)))
    arr = np.array(rows, dtype=np.int32)
    off = arr[:, 0] - arr[:, 1]
    bias_idx = (off == 0).astype(np.int32) if diag_only_bias else off.astype(np.int32)
    return [jnp.asarray(v) for v in (arr[:, 0], arr[:, 1], bias_idx, arr[:, 2], arr[:, 3])]


def _fwd_proj(x, w_in_r, w_uq_r, w_ukv_r, qg, kvg, tabs, bt):
    seq = x.shape[0]

    def body(x_ref, win_ref, wuq_ref, wukv_ref, qg_ref, kvg_ref, tab_ref,
             cq_ref, ckv_ref, qn_ref, kvn_ref, qcat_ref, kn_ref, kpe_ref, v_ref,
             ga_ref, gb_ref, qb_ref, kb_ref, vb_ref, knt_ref, kpet_ref, vt_ref, kbt_ref, vbt_ref):
        xb = x_ref[...].astype(BF16)

        def proj(lo, hi):
            return jnp.dot(xb, win_ref[:, lo:hi], preferred_element_type=F32)

        m_tabs = (tab_ref[0], tab_ref[1], tab_ref[2])
        d_tabs = (tab_ref[3], tab_ref[4], tab_ref[5])

        cq = proj(C_CQ, C_CKV)
        cq_ref[...] = cq
        qn = (cq * lax.rsqrt(jnp.mean(cq * cq, axis=1, keepdims=True) + RMS_EPS) * qg_ref[...]).astype(BF16)
        qn_ref[...] = qn
        q = jnp.dot(qn, wuq_ref[...], preferred_element_type=F32)
        qcat_ref[:, :HW] = (q[:, :HW] * (MLA_SCALE * LOG2E)).astype(BF16)
        qcat_ref[:, HW:] = (_rope_wide(_rope, q[:, HW:], *m_tabs, MLA_ROPE // 2) * (MLA_SCALE * LOG2E)).astype(BF16)

        ckv = proj(C_CKV, C_KR)
        ckv_ref[...] = ckv
        kvn = (ckv * lax.rsqrt(jnp.mean(ckv * ckv, axis=1, keepdims=True) + RMS_EPS) * kvg_ref[...]).astype(BF16)
        kvn_ref[...] = kvn
        kv = jnp.dot(kvn, wukv_ref[...], preferred_element_type=F32)
        kn_ref[...] = kv[:, :HW].astype(BF16)
        v_ref[...] = kv[:, HW:].astype(BF16)
        knt_ref[...] = kv[:, :HW].T.astype(BF16)
        vt_ref[...] = kv[:, HW:].T.astype(BF16)

        kpe = _rope(proj(C_KR, C_GA), *m_tabs, MLA_ROPE // 2)
        kpe_ref[...] = kpe.astype(BF16)
        kpet_ref[...] = kpe.T[:MLA_ROPE, :].astype(BF16)
        ga_ref[...] = proj(C_GA, C_QB)
        qb_ref[...] = (_rope_wide(_rope, proj(C_QB, C_KB), *d_tabs, DIL_ROT // 2) * (DIL_SCALE * LOG2E)).astype(BF16)
        kb = _rope_wide(_rope, proj(C_KB, C_VB), *d_tabs, DIL_ROT // 2)
        kb_ref[...] = kb.astype(BF16)
        kbt_ref[...] = kb.T.astype(BF16)
        vb = proj(C_VB, C_GB)
        vb_ref[...] = vb.astype(BF16)
        vbt_ref[...] = vb.T.astype(BF16)
        gb_ref[...] = proj(C_GB, C_END)

    def tok(width):
        return pl.BlockSpec((bt, width), lambda i: (i, 0))

    def tok_t(height):
        return pl.BlockSpec((height, bt), lambda i: (0, i))

    def full(a):
        return pl.BlockSpec(a.shape, lambda i: (0,) * a.ndim)

    outs = [(Q_RANK, F32), (KV_RANK, F32), (Q_RANK, BF16), (KV_RANK, BF16), (QW, BF16), (HW, BF16),
            (LANES, BF16), (HW, BF16), (HW, F32), (HW, F32), (HW, BF16), (HW, BF16), (HW, BF16)]
    outs_t = [HW, MLA_ROPE, HW, HW, HW]
    return _pcall(
        body, name="fwd_proj", grid=(seq // bt,),
        in_specs=[tok(D_MODEL), full(w_in_r), full(w_uq_r), full(w_ukv_r), full(qg), full(kvg),
                  pl.BlockSpec((6, bt, LANES), lambda i: (0, i, 0))],
        out_specs=[tok(w) for w, _ in outs] + [tok_t(h) for h in outs_t],
        out_shape=[jax.ShapeDtypeStruct((seq, w), dt) for w, dt in outs]
        + [jax.ShapeDtypeStruct((h, seq), BF16) for h in outs_t],
        compiler_params=_cparams(dimension_semantics=("arbitrary",)),
    )(x, w_in_r, w_uq_r, w_ukv_r, qg, kvg, tabs)


def _head_masks(lane, h):
    e, g = h % 2, h % 4
    me = (lane >= 64 * e) & (lane < 64 * e + 64)
    mr = (lane >= 32 * g) & (lane < 32 * g + 32)
    return me, mr


def _masked(mask, a):
    return jnp.where(mask, a, jnp.zeros_like(a))


def _pair_operands(q_ref, k_ref, kpe_ref, lane, j):
    cols = slice(LANES * j, LANES * (j + 1))
    qc = q_ref[:, cols]
    kj = k_ref[:, cols]
    kes = []
    for h in (2 * j, 2 * j + 1):
        me, mr = _head_masks(lane, h)
        ke = _masked(me, kj)
        if kpe_ref is not None:
            ke = jnp.concatenate([ke, _masked(mr, kpe_ref[...])], axis=1)
        kes.append(ke)
    if kpe_ref is not None:
        qc = jnp.concatenate([qc, q_ref[:, HW + LANES * (j // 2):HW + LANES * (j // 2 + 1)]], axis=1)
    return qc, kes


def _attn_fwd(name, q, k, kpe, vt, bias_t, steps, blk):
    seq = q.shape[0]
    mla = kpe is not None
    n_steps = int(steps[0].shape[0])

    def body(qi_r, ki_r, bi_r, fi_r, la_r, *refs):
        if mla:
            q_ref, k_ref, kpe_ref, vt_ref, b_ref, o_ref, lse_ref, m_sc, l_sc, acc_sc, st_sc = refs
        else:
            q_ref, k_ref, vt_ref, b_ref, o_ref, lse_ref, m_sc, l_sc, acc_sc, st_sc = refs
        t = pl.program_id(0)

        @pl.when(fi_r[t] == 1)
        def _():
            m_sc[...] = jnp.full(m_sc.shape, NEG, F32)
            l_sc[...] = jnp.zeros(l_sc.shape, F32)
            acc_sc[...] = jnp.zeros(acc_sc.shape, F32)

        lane = lax.broadcasted_iota(jnp.int32, (1, LANES), 1)
        ones = jnp.ones((16, blk), BF16)

        def pair_scores(j, with_bias):
            qc, kes = _pair_operands(q_ref, k_ref, kpe_ref if mla else None, lane, j)
            st = lax.dot_general(jnp.concatenate(kes, axis=0), qc, NT, preferred_element_type=F32)
            maxes = []
            for e in range(2):
                se = st[e * blk:(e + 1) * blk]
                if with_bias:
                    se = se + b_ref[0]
                st_sc[j % 2, e * blk:(e + 1) * blk] = se
                maxes.append(jnp.max(se, axis=0, keepdims=True))
            return maxes

        def softmax_pv(h, col_max):
            st = st_sc[(h // 2) % 2, (h % 2) * blk:(h % 2 + 1) * blk]
            hrow = slice(h, h + 1)
            m_prev = m_sc[hrow, :]
            m_new = jnp.maximum(m_prev, col_max)
            alpha = jnp.exp2(m_prev - m_new)
            pt = jnp.exp2(st - m_new).astype(BF16)
            m_sc[hrow, :] = m_new
            rows = slice(64 * h, 64 * h + 64)
            res = jnp.dot(jnp.concatenate([vt_ref[rows, :], ones], axis=0), pt, preferred_element_type=F32)
            acc_sc[rows, :] = alpha * acc_sc[rows, :] + res[:64]
            l_sc[hrow, :] = alpha * l_sc[hrow, :] + res[64:65]

        def step(with_bias):
            maxes = pair_scores(0, with_bias)
            for j in range(HEADS // 2):
                cur = maxes
                if j + 1 < HEADS // 2:
                    maxes = pair_scores(j + 1, with_bias)
                softmax_pv(2 * j, cur[0])
                softmax_pv(2 * j + 1, cur[1])

        if mla:
            pl.when(bi_r[t] == 1)(lambda: step(True))
            pl.when(bi_r[t] == 0)(lambda: step(False))
        else:
            step(True)

        @pl.when(la_r[t] == 1)
        def _():
            for h in range(HEADS):
                rows = slice(64 * h, 64 * h + 64)
                acc_sc[rows, :] = acc_sc[rows, :] / l_sc[h:h + 1, :]
            o_ref[...] = acc_sc[...].T
            lse_ref[...] = m_sc[...] + jnp.log2(l_sc[...])

    qmap = lambda t, qi, ki, bi, fi, la: (qi[t], 0)
    kmap = lambda t, qi, ki, bi, fi, la: (ki[t], 0)
    in_specs = [pl.BlockSpec((blk, q.shape[1]), qmap), pl.BlockSpec((blk, HW), kmap)]
    args = [q, k]
    if mla:
        in_specs.append(pl.BlockSpec((blk, LANES), kmap))
        args.append(kpe)
    in_specs += [pl.BlockSpec((HW, blk), lambda t, qi, ki, bi, fi, la: (0, ki[t])),
                 pl.BlockSpec((1, blk, blk), lambda t, qi, ki, bi, fi, la: (bi[t], 0, 0))]
    args += [vt, bias_t]
    return _pcall(
        body, name=name,
        grid_spec=pltpu.PrefetchScalarGridSpec(
            num_scalar_prefetch=5, grid=(n_steps,), in_specs=in_specs,
            out_specs=[pl.BlockSpec((blk, HW), qmap),
                       pl.BlockSpec((HEADS, blk), lambda t, qi, ki, bi, fi, la: (0, qi[t]))],
            scratch_shapes=[pltpu.VMEM((HEADS, blk), F32), pltpu.VMEM((HEADS, blk), F32),
                            pltpu.VMEM((HW, blk), F32), pltpu.VMEM((2, 2 * blk, blk), F32)]),
        out_shape=[jax.ShapeDtypeStruct((seq, HW), F32), jax.ShapeDtypeStruct((HEADS, seq), F32)],
        compiler_params=_cparams(dimension_semantics=("arbitrary",)),
    )(*steps, *args)


def _attn_bwd(name, q, k, kpe, v, kt, kpet, bias_t, do, lse, dstat, steps, blk):
    seq = q.shape[0]
    mla = kpe is not None
    qw = q.shape[1]
    n_steps = int(steps[0].shape[0])
    dk_dtype = BF16 if mla else F32

    def body(qi_r, ki_r, bi_r, fi_r, la_r, *refs):
        if mla:
            (q_ref, k_ref, kpe_ref, v_ref, kt_ref, kpet_ref, b_ref, do_ref, lse_ref, d_ref,
             dq_ref, dk_ref, dkpe_ref, dv_ref, dk_sc, dkpe_sc, dv_sc, st_sc, dpt_sc) = refs
        else:
            (q_ref, k_ref, v_ref, kt_ref, b_ref, do_ref, lse_ref, d_ref,
             dq_ref, dk_ref, dv_ref, dk_sc, dv_sc, st_sc, dpt_sc) = refs
        t = pl.program_id(0)

        @pl.when(t == 0)
        def _():
            dq_ref[...] = jnp.zeros(dq_ref.shape, F32)

        @pl.when(fi_r[t] == 1)
        def _():
            dk_sc[...] = jnp.zeros(dk_sc.shape, F32)
            dv_sc[...] = jnp.zeros(dv_sc.shape, F32)
            if mla:
                dkpe_sc[...] = jnp.zeros(dkpe_sc.shape, F32)

        qi = qi_r[t]
        lane = lax.broadcasted_iota(jnp.int32, (1, LANES), 1)

        def pair_matmuls(j):
            cols = slice(LANES * j, LANES * (j + 1))
            qc, kes = _pair_operands(q_ref, k_ref, kpe_ref if mla else None, lane, j)
            st_sc[j % 2] = lax.dot_general(jnp.concatenate(kes, axis=0), qc, NT, preferred_element_type=F32)
            vj = v_ref[:, cols]
            ves = [_masked(_head_masks(lane, h)[0], vj) for h in (2 * j, 2 * j + 1)]
            dpt_sc[j % 2] = lax.dot_general(
                jnp.concatenate(ves, axis=0), do_ref[:, cols], NT, preferred_element_type=F32)

        def pair_grads(j, with_bias):
            cols = slice(LANES * j, LANES * (j + 1))
            qj, doj = q_ref[:, cols], do_ref[:, cols]
            if mla:
                qr = q_ref[:, HW + LANES * (j // 2):HW + LANES * (j // 2 + 1)]
            pts, dsts, qms, doms = [], [], [], []
            for e in range(2):
                h = 2 * j + e
                me, mr = _head_masks(lane, h)
                st = st_sc[j % 2, e * blk:(e + 1) * blk]
                if with_bias:
                    st = st + b_ref[0]
                pt = jnp.exp2(st - lse_ref[h:h + 1, :])
                dst = (pt * (dpt_sc[j % 2, e * blk:(e + 1) * blk] - d_ref[h:h + 1, :])).astype(BF16)
                pts.append(pt.astype(BF16))
                dsts.append(dst)
                doms.append(_masked(me, doj))
                qm = _masked(me, qj)
                if mla:
                    qm = jnp.concatenate([qm, _masked(mr, qr)], axis=1)
                qms.append(qm)
                ktl = kt_ref[64 * h:64 * h + 64, :]
                if mla:
                    ktl = jnp.concatenate([ktl, kpet_ref[...]], axis=0)
                dqc = jnp.dot(ktl, dst, preferred_element_type=F32)
                dq_ref[qi, 64 * h:64 * h + 64, :] += dqc[:64]
                if mla:
                    dq_ref[qi, HW + MLA_ROPE * h:HW + MLA_ROPE * (h + 1), :] += dqc[64:]
            dv_sc[:, cols] += jnp.dot(
                jnp.concatenate(pts, axis=1), jnp.concatenate(doms, axis=0), preferred_element_type=F32)
            dkc = jnp.dot(jnp.concatenate(dsts, axis=1), jnp.concatenate(qms, axis=0), preferred_element_type=F32)
            dk_sc[:, cols] += dkc[:, :LANES]
            if mla:
                dkpe_sc[...] += dkc[:, LANES:]

        def step(with_bias):
            pair_matmuls(0)
            for j in range(HEADS // 2):
                if j + 1 < HEADS // 2:
                    pair_matmuls(j + 1)
                pair_grads(j, with_bias)

        if mla:
            pl.when(bi_r[t] == 1)(lambda: step(True))
            pl.when(bi_r[t] == 0)(lambda: step(False))
        else:
            step(True)

        @pl.when(la_r[t] == 1)
        def _():
            dk_ref[...] = (dk_sc[...] * LN2).astype(dk_ref.dtype)
            dv_ref[...] = dv_sc[...].astype(dv_ref.dtype)
            if mla:
                dkpe_ref[...] = dkpe_sc[...] * LN2

    qmap = lambda t, qi, ki, bi, fi, la: (qi[t], 0)
    kmap = lambda t, qi, ki, bi, fi, la: (ki[t], 0)
    qmap_t = lambda t, qi, ki, bi, fi, la: (0, qi[t])
    kmap_t = lambda t, qi, ki, bi, fi, la: (0, ki[t])
    in_specs = [pl.BlockSpec((blk, qw), qmap), pl.BlockSpec((blk, HW), kmap)]
    args = [q, k]
    if mla:
        in_specs.append(pl.BlockSpec((blk, LANES), kmap))
        args.append(kpe)
    in_specs += [pl.BlockSpec((blk, HW), kmap), pl.BlockSpec((HW, blk), kmap_t)]
    args += [v, kt]
    if mla:
        in_specs.append(pl.BlockSpec((MLA_ROPE, blk), kmap_t))
        args.append(kpet)
    in_specs += [pl.BlockSpec((1, blk, blk), lambda t, qi, ki, bi, fi, la: (bi[t], 0, 0)),
                 pl.BlockSpec((blk, HW), qmap), pl.BlockSpec((HEADS, blk), qmap_t), pl.BlockSpec((HEADS, blk), qmap_t)]
    args += [bias_t, do, lse, dstat]
    dq_shape = (seq // blk, qw, blk)
    out_specs = [pl.BlockSpec(dq_shape, lambda t, qi, ki, bi, fi, la: (0, 0, 0)), pl.BlockSpec((blk, HW), kmap)]
    out_shape = [jax.ShapeDtypeStruct(dq_shape, F32), jax.ShapeDtypeStruct((seq, HW), dk_dtype)]
    scratch = [pltpu.VMEM((blk, HW), F32)]
    if mla:
        out_specs.append(pl.BlockSpec((blk, LANES), kmap))
        out_shape.append(jax.ShapeDtypeStruct((seq, LANES), F32))
        scratch.append(pltpu.VMEM((blk, LANES), F32))
    out_specs.append(pl.BlockSpec((blk, HW), kmap))
    out_shape.append(jax.ShapeDtypeStruct((seq, HW), BF16))
    scratch.append(pltpu.VMEM((blk, HW), F32))
    scratch += [pltpu.VMEM((2, 2 * blk, blk), F32), pltpu.VMEM((2, 2 * blk, blk), F32)]
    return _pcall(
        body, name=name,
        grid_spec=pltpu.PrefetchScalarGridSpec(
            num_scalar_prefetch=5, grid=(n_steps,), in_specs=in_specs, out_specs=out_specs,
            scratch_shapes=scratch),
        out_shape=out_shape,
        compiler_params=_cparams(dimension_semantics=("arbitrary",)),
    )(*steps, *args)


def _out_ln(oa, ob, ga, gb, x, tgt, w_out, ln_g, ln_b, bt):
    seq = x.shape[0]

    def body(oa_ref, ob_ref, ga_ref, gb_ref, x_ref, tgt_ref, w_ref, g_ref, b_ref,
             dz_ref, doa_ref, dob_ref, dga_ref, dgb_ref, da_ref, db_ref, gw_ref, small_ref):
        i = pl.program_id(0)

        @pl.when(i == 0)
        def _():
            gw_ref[...] = jnp.zeros(gw_ref.shape, F32)
            small_ref[...] = jnp.zeros(small_ref.shape, F32)

        def gate(g):
            sig = 1.0 / (1.0 + jnp.exp(-g))
            return g * sig, sig * (1.0 + g * (1.0 - sig))

        o_a, o_b = oa_ref[...], ob_ref[...]
        g_a, g_b = ga_ref[...], gb_ref[...]
        sa, dsa = gate(g_a)
        sb, dsb = gate(g_b)
        mix = jnp.concatenate([o_a * sa, o_b * sb], axis=1).astype(BF16)
        z = ALPHA * x_ref[...] + jnp.dot(mix, w_ref[...], preferred_element_type=F32)
        mu = jnp.mean(z, axis=1, keepdims=True)
        zc = z - mu
        rstd = lax.rsqrt(jnp.mean(zc * zc, axis=1, keepdims=True) + LN_EPS)
        xhat = zc * rstd
        gam = g_ref[...]
        diff = xhat * gam + b_ref[...] - tgt_ref[...]
        dy = diff * (1.0 / D_MODEL)
        small_ref[0:1, :] += jnp.sum(dy * xhat, axis=0, keepdims=True)
        small_ref[1:2, :] += jnp.sum(dy, axis=0, keepdims=True)
        small_ref[2:3, :] += jnp.sum(diff * diff, axis=0, keepdims=True)
        dxh = dy * gam
        dz = rstd * (dxh - jnp.mean(dxh, axis=1, keepdims=True) - xhat * jnp.mean(dxh * xhat, axis=1, keepdims=True))
        dz_ref[...] = dz
        dzb = dz.astype(BF16)
        gw_ref[...] += lax.dot_general(mix, dzb, TN, preferred_element_type=F32)
        dmix = lax.dot_general(dzb, w_ref[...], NT, preferred_element_type=F32)
        doa, dob = dmix[:, :HW] * sa, dmix[:, HW:] * sb
        doa_ref[...] = doa.astype(BF16)
        dob_ref[...] = dob.astype(BF16)
        dga_ref[...] = (dmix[:, :HW] * o_a * dsa).astype(BF16)
        dgb_ref[...] = (dmix[:, HW:] * o_b * dsb).astype(BF16)
        head_of = lax.broadcasted_iota(jnp.int32, (HEADS, HW), 1) // 64
        ind = (head_of == lax.broadcasted_iota(jnp.int32, (HEADS, HW), 0)).astype(F32)
        da_ref[...] = lax.dot_general(ind, doa * o_a, NT, preferred_element_type=F32, precision=lax.Precision.HIGHEST)
        db_ref[...] = lax.dot_general(ind, dob * o_b, NT, preferred_element_type=F32, precision=lax.Precision.HIGHEST)

    def tok(width):
        return pl.BlockSpec((bt, width), lambda i: (i, 0))

    def full(shape):
        return pl.BlockSpec(shape, lambda i: (0,) * len(shape))

    stat = pl.BlockSpec((HEADS, bt), lambda i: (0, i))
    return _pcall(
        body, name="out_ln", grid=(seq // bt,),
        in_specs=[tok(HW), tok(HW), tok(HW), tok(HW), tok(D_MODEL), tok(D_MODEL),
                  full((D_MODEL, D_MODEL)), full((1, D_MODEL)), full((1, D_MODEL))],
        out_specs=[tok(D_MODEL), tok(HW), tok(HW), tok(HW), tok(HW), stat, stat,
                   full((D_MODEL, D_MODEL)), full((8, D_MODEL))],
        out_shape=[jax.ShapeDtypeStruct((seq, D_MODEL), F32)] + [jax.ShapeDtypeStruct((seq, HW), BF16)] * 4
        + [jax.ShapeDtypeStruct((HEADS, seq), F32)] * 2
        + [jax.ShapeDtypeStruct((D_MODEL, D_MODEL), F32), jax.ShapeDtypeStruct((8, D_MODEL), F32)],
        compiler_params=_cparams(dimension_semantics=("arbitrary",)),
    )(oa, ob, ga, gb, x, tgt, w_out, ln_g, ln_b)


def _bwd_mid(dq_m, dkn, dv, dkpe, dqb, dkb, dvb, dga, dgb, cq, ckv, qn, kvn, w_uq_r, w_ukv_r, qg, kvg, tabs, bt):
    seq = cq.shape[0]

    def body(dqm_ref, dkn_ref, dv_ref, dkpe_ref, dqb_ref, dkb_ref, dvb_ref, dga_ref, dgb_ref,
             cq_ref, ckv_ref, qn_ref, kvn_ref, wuq_ref, wukv_ref, qg_ref, kvg_ref, tab_ref,
             dh_ref, guq_ref, gukv_ref, small_ref):
        i = pl.program_id(0)

        @pl.when(i == 0)
        def _():
            guq_ref[...] = jnp.zeros(guq_ref.shape, F32)
            gukv_ref[...] = jnp.zeros(gukv_ref.shape, F32)
            small_ref[...] = jnp.zeros(small_ref.shape, F32)

        m_tabs = (tab_ref[0], tab_ref[1], tab_ref[2])
        d_tabs = (tab_ref[3], tab_ref[4], tab_ref[5])

        def rms_bwd(c, dn, gain):
            r = lax.rsqrt(jnp.mean(c * c, axis=1, keepdims=True) + RMS_EPS)
            u = dn * gain
            dc = r * u - c * (r * r * r) * jnp.mean(u * c, axis=1, keepdims=True)
            return dc, jnp.sum(dn * c * r, axis=0, keepdims=True)

        dqm = dqm_ref[0].T
        dq = jnp.concatenate(
            [dqm[:, :HW], _rope_wide(_rope_t, dqm[:, HW:], *m_tabs, MLA_ROPE // 2)], axis=1) * MLA_SCALE
        dq = dq.astype(BF16)
        guq_ref[...] += lax.dot_general(qn_ref[...], dq, TN, preferred_element_type=F32)
        dqn = lax.dot_general(dq, wuq_ref[...], NT, preferred_element_type=F32)
        dcq, gq = rms_bwd(cq_ref[...], dqn, qg_ref[...])
        small_ref[0:1, :] += gq

        dkv = jnp.concatenate([dkn_ref[...], dv_ref[...]], axis=1)
        gukv_ref[...] += lax.dot_general(kvn_ref[...], dkv, TN, preferred_element_type=F32)
        dkvn = lax.dot_general(dkv, wukv_ref[...], NT, preferred_element_type=F32)
        dckv, gkv = rms_bwd(ckv_ref[...], dkvn, kvg_ref[...])
        small_ref[1:2, :KV_RANK] += gkv

        dh_ref[:, C_CQ:C_CKV] = dcq.astype(BF16)
        dh_ref[:, C_CKV:C_KR] = dckv.astype(BF16)
        dh_ref[:, C_KR:C_GA] = _rope_t(dkpe_ref[...], *m_tabs, MLA_ROPE // 2).astype(BF16)
        dh_ref[:, C_GA:C_QB] = dga_ref[...]
        dh_ref[:, C_QB:C_KB] = (_rope_wide(_rope_t, dqb_ref[0].T, *d_tabs, DIL_ROT // 2) * DIL_SCALE).astype(BF16)
        dh_ref[:, C_KB:C_VB] = _rope_wide(_rope_t, dkb_ref[...], *d_tabs, DIL_ROT // 2).astype(BF16)
        dh_ref[:, C_VB:C_GB] = dvb_ref[...]
        dh_ref[:, C_GB:C_END] = dgb_ref[...]

    def tok(width):
        return pl.BlockSpec((bt, width), lambda i: (i, 0))

    def tok_t(a):
        per = a.shape[2] // bt
        return pl.BlockSpec((1, a.shape[1], bt), lambda i: (i // per, 0, i % per))

    def full(shape):
        return pl.BlockSpec(shape, lambda i: (0,) * len(shape))

    return _pcall(
        body, name="bwd_mid", grid=(seq // bt,),
        in_specs=[tok_t(dq_m), tok(HW), tok(HW), tok(LANES), tok_t(dqb), tok(HW), tok(HW), tok(HW), tok(HW),
                  tok(Q_RANK), tok(KV_RANK), tok(Q_RANK), tok(KV_RANK),
                  full(w_uq_r.shape), full(w_ukv_r.shape), full((1, Q_RANK)), full((1, KV_RANK)),
                  pl.BlockSpec((6, bt, LANES), lambda i: (0, i, 0))],
        out_specs=[tok(C_END), full(w_uq_r.shape), full(w_ukv_r.shape), full((8, Q_RANK))],
        out_shape=[jax.ShapeDtypeStruct((seq, C_END), BF16), jax.ShapeDtypeStruct(w_uq_r.shape, F32),
                   jax.ShapeDtypeStruct(w_ukv_r.shape, F32), jax.ShapeDtypeStruct((8, Q_RANK), F32)],
        compiler_params=_cparams(dimension_semantics=("arbitrary",)),
    )(dq_m, dkn, dv, dkpe, dqb, dkb, dvb, dga, dgb, cq, ckv, qn, kvn, w_uq_r, w_ukv_r, qg, kvg, tabs)


def _grad_x(dz, dh, w_in_r, bt):
    seq = dz.shape[0]

    def body(dz_ref, dh_ref, w_ref, gx_ref):
        gx_ref[...] = ALPHA * dz_ref[...] + lax.dot_general(
            dh_ref[...], w_ref[...], NT, preferred_element_type=F32)

    return _pcall(
        body, name="grad_x", grid=(seq // bt,),
        in_specs=[pl.BlockSpec((bt, D_MODEL), lambda i: (i, 0)), pl.BlockSpec((bt, C_END), lambda i: (i, 0)),
                  pl.BlockSpec(w_in_r.shape, lambda i: (0, 0))],
        out_specs=pl.BlockSpec((bt, D_MODEL), lambda i: (i, 0)),
        out_shape=jax.ShapeDtypeStruct((seq, D_MODEL), F32),
        compiler_params=_cparams(dimension_semantics=("arbitrary",)),
    )(dz, dh, w_in_r)


def _grad_w_in(x, dh, bt):
    seq = x.shape[0]
    shard = IN_WIDTH // N_DEV
    k_lo, k_hi = IN_SPLITS[0] + IN_SPLITS[1], IN_SPLITS[0] + IN_SPLITS[1] + MLA_ROPE

    def body(x_ref, dh_ref, out_ref, acc):
        i = pl.program_id(0)

        @pl.when(i == 0)
        def _():
            acc[...] = jnp.zeros(acc.shape, F32)

        acc[...] += lax.dot_general(x_ref[...].astype(BF16), dh_ref[...], TN, preferred_element_type=F32)

        @pl.when(i == seq // bt - 1)
        def _():
            kr = acc[:, C_KR:C_GA]
            kr = kr + pltpu.roll(kr, 96, 1) + pltpu.roll(kr, 64, 1) + pltpu.roll(kr, 32, 1)
            for d in range(N_DEV):
                lo, hi = shard * d, shard * (d + 1)
                pieces = []
                if lo < k_lo:
                    pieces.append(acc[:, lo:min(hi, k_lo)])
                if lo < k_hi and hi > k_lo:
                    pieces.append(kr[:, max(lo, k_lo) - k_lo:min(hi, k_hi) - k_lo])
                if hi > k_hi:
                    shift = C_GA - k_hi
                    pieces.append(acc[:, max(lo, k_hi) + shift:hi + shift])
                blk = pieces[0] if len(pieces) == 1 else jnp.concatenate(pieces, axis=1)
                out_ref[d] = blk.astype(BF16)

    return _pcall(
        body, name="grad_w_in", grid=(seq // bt,),
        in_specs=[pl.BlockSpec((bt, D_MODEL), lambda i: (i, 0)), pl.BlockSpec((bt, C_END), lambda i: (i, 0))],
        out_specs=pl.BlockSpec((N_DEV, D_MODEL, shard), lambda i: (0, 0, 0)),
        out_shape=jax.ShapeDtypeStruct((N_DEV, D_MODEL, shard), BF16),
        scratch_shapes=[pltpu.VMEM((D_MODEL, C_END), F32)],
        compiler_params=_cparams(dimension_semantics=("arbitrary",)),
    )(x, dh)


def _restore_grads(g_uq_r, g_ukv_r):
    g_uq = jnp.concatenate(
        [g_uq_r[:, :HW].reshape(Q_RANK, HEADS, MLA_NOPE), g_uq_r[:, HW:].reshape(Q_RANK, HEADS, MLA_ROPE)],
        axis=2).reshape(Q_RANK, HEADS * (MLA_NOPE + MLA_ROPE))
    g_ukv = jnp.concatenate(
        [g_ukv_r[:, :HW].reshape(KV_RANK, HEADS, MLA_NOPE), g_ukv_r[:, HW:].reshape(KV_RANK, HEADS, MLA_V)],
        axis=2).reshape(KV_RANK, HEADS * (MLA_NOPE + MLA_V))
    return g_uq, g_ukv


def _local_step(x, tgt, w_in_r, w_uq_r, w_ukv_r, w_out, q_norm_g, kv_norm_g, ln_g, ln_b,
                bt=BLOCK_TOKENS, blk_m=BLOCK_MLA, blk_d=BLOCK_DIL):
    seq = x.shape[0]
    tabs = jnp.asarray(_rope_tables(seq))
    qg, kvg = q_norm_g.reshape(1, -1), kv_norm_g.reshape(1, -1)

    (cq, ckv, qn, kvn, qcat, kn, kpe, v, ga, gb, qb, kb, vb, knt, kpet, vt, kbt, vbt) = _fwd_proj(
        x, w_in_r, w_uq_r, w_ukv_r, qg, kvg, tabs, bt)

    nq_m, nq_d = seq // blk_m, seq // blk_d
    span_d = DIL_CONFIGS[-1][0] // blk_d
    bias_m, bias_d = jnp.asarray(_mla_bias_t(blk_m)), jnp.asarray(_dil_bias_t(blk_d))
    oa, lse_a = _attn_fwd("mla_fwd", qcat, kn, kpe, vt, bias_m, _steps(nq_m, nq_m, False, True), blk_m)
    ob, lse_b = _attn_fwd("dil_fwd", qb, kb, None, vbt, bias_d, _steps(nq_d, span_d, False, False), blk_d)

    dz, doa, dob, dga, dgb, dst_a, dst_b, g_out, small1 = _out_ln(
        oa, ob, ga, gb, x, tgt, w_out, ln_g.reshape(1, -1), ln_b.reshape(1, -1), bt)

    dq_m, dkn, dkpe, dv = _attn_bwd(
        "mla_bwd", qcat, kn, kpe, v, knt, kpet, bias_m, doa, lse_a, dst_a, _steps(nq_m, nq_m, True, True), blk_m)
    dqb, dkb, dvb = _attn_bwd(
        "dil_bwd", qb, kb, None, vb, kbt, None, bias_d, dob, lse_b, dst_b, _steps(nq_d, span_d, True, False), blk_d)

    dh, g_uq_r, g_ukv_r, small2 = _bwd_mid(
        dq_m, dkn, dv, dkpe, dqb, dkb, dvb, dga, dgb, cq, ckv, qn, kvn, w_uq_r, w_ukv_r, qg, kvg, tabs, bt)
    grad_x = _grad_x(dz, dh, w_in_r, bt)
    g_in = _grad_w_in(x, dh, bt)
    g_uq, g_ukv = _restore_grads(g_uq_r, g_ukv_r)

    return small1[2], grad_x, g_in, g_uq, g_ukv, g_out, small2[0], small2[1, :KV_RANK], small1[0], small1[1]


MESH_ID = pl.DeviceIdType.MESH
SHARD_SHAPES = ((D_MODEL, IN_WIDTH // N_DEV), (Q_RANK, 768 // N_DEV), (KV_RANK, 1024 // N_DEV), (D_MODEL // N_DEV, D_MODEL))
ADAM_ROWS = (32, 128, 128, 16)


def _me():
    x, y, c = lax.axis_index("x"), lax.axis_index("y"), lax.axis_index("c")
    return x, y, c, 4 * x + 2 * y + c


def _peer(k):
    x, y, c, _ = _me()
    px = 1 - x if (k >> 2) & 1 else x
    py = 1 - y if (k >> 1) & 1 else y
    pc = 1 - c if k & 1 else c
    return (px, py, pc), 4 * px + 2 * py + pc


def _all_gather_weights(shards):
    shard = IN_WIDTH // N_DEV
    k_lo = IN_SPLITS[0] + IN_SPLITS[1]
    k_hi = k_lo + MLA_ROPE

    def body(*refs):
        ins = refs[:4]
        win_ref, wuq_ref, wukv_ref, wout_ref = refs[4:8]
        bufs = list(refs[8:11]) + [wout_ref]
        send_sems, recv_sems = refs[11:]
        x, y, c, me = _me()
        here, sibling = (x, y, c), (x, y, 1 - c)
        chips = [(1 - x, y), (x, 1 - y), (1 - x, 1 - y)]
        for t in range(4):
            bufs[t][me] = ins[t][...].astype(BF16)

        def copy(t, k, px, py, pc, to):
            blk = bufs[t].at[4 * px + 2 * py + pc]
            return pltpu.make_async_remote_copy(
                src_ref=blk, dst_ref=blk, send_sem=send_sems.at[t, k], recv_sem=recv_sems.at[t, k],
                device_id=to, device_id_type=MESH_ID)

        first = []
        for t in range(4):
            first.append(copy(t, 0, x, y, c, sibling))
            for j, (px, py) in enumerate(chips):
                first.append(copy(t, 1 + j, x, y, c, (px, py, c)))
        for cp in first:
            cp.start()
        passed = []
        for j, (px, py) in enumerate(chips):
            for t in range(4):
                copy(t, 1 + j, px, py, c, here).wait_recv()
                cp = copy(t, 4 + j, px, py, c, sibling)
                cp.start()
                passed.append(cp)
        for t in range(4):
            copy(t, 0, x, y, 1 - c, here).wait_recv()
        for j, (px, py) in enumerate(chips):
            for t in range(4):
                copy(t, 4 + j, px, py, 1 - c, here).wait_recv()
        for cp in first + passed:
            cp.wait_send()

        a_in, a_uq, a_ukv = bufs[:3]
        for d in range(N_DEV):
            lo, hi = shard * d, shard * (d + 1)
            if lo < k_lo:
                win_ref[:, lo:min(hi, k_lo)] = a_in[d, :, 0:min(hi, k_lo) - lo]
            if lo < k_hi and hi > k_lo:
                kr = a_in[d, :, k_lo - lo:k_hi - lo]
                for rep in range(4):
                    win_ref[:, C_KR + MLA_ROPE * rep:C_KR + MLA_ROPE * (rep + 1)] = kr
            if hi > k_hi:
                src = max(lo, k_hi)
                win_ref[:, src + C_GA - k_hi:hi + C_GA - k_hi] = a_in[d, :, src - lo:hi - lo]
        for h in range(HEADS):
            wuq_ref[:, MLA_NOPE * h:MLA_NOPE * (h + 1)] = a_uq[h, :, :MLA_NOPE]
            wuq_ref[:, HW + MLA_ROPE * h:HW + MLA_ROPE * (h + 1)] = a_uq[h, :, MLA_NOPE:]
            wukv_ref[:, MLA_NOPE * h:MLA_NOPE * (h + 1)] = a_ukv[h, :, :MLA_NOPE]
            wukv_ref[:, HW + MLA_V * h:HW + MLA_V * (h + 1)] = a_ukv[h, :, MLA_NOPE:]

    vmem = pl.BlockSpec(memory_space=pltpu.VMEM)
    w_in_r, w_uq_r, w_ukv_r, a_out = _pcall(
        body, name="gather_weights",
        in_specs=[vmem] * 4, out_specs=[vmem] * 4,
        out_shape=[jax.ShapeDtypeStruct((D_MODEL, C_END), BF16), jax.ShapeDtypeStruct((Q_RANK, QW), BF16),
                   jax.ShapeDtypeStruct((KV_RANK, 2 * HW), BF16), jax.ShapeDtypeStruct((N_DEV,) + SHARD_SHAPES[3], BF16)],
        scratch_shapes=[pltpu.VMEM((N_DEV,) + s, BF16) for s in SHARD_SHAPES[:3]]
        + [pltpu.SemaphoreType.DMA((4, N_DEV - 1)), pltpu.SemaphoreType.DMA((4, N_DEV - 1))],
        compiler_params=_cparams(),
    )(*shards)
    return w_in_r, w_uq_r, w_ukv_r, a_out.reshape(D_MODEL, D_MODEL)


def _adamw(w, g, m, v):
    m = ADAM_B1 * m + (1.0 - ADAM_B1) * g
    v = ADAM_B2 * v + (1.0 - ADAM_B2) * jnp.square(g)
    m_hat = m / (1.0 - ADAM_B1 ** ADAM_STEP)
    v_hat = v / (1.0 - ADAM_B2 ** ADAM_STEP)
    delta = -ADAM_LR * (m_hat / (jnp.sqrt(v_hat) + ADAM_EPS) + ADAM_WD * w)
    return delta, m, v


def _reduce_grads(grads3, small_part):
    def body(*refs):
        g3, sp_ref = refs[0:4], refs[4]
        gsum, ssum = refs[5:9], refs[9]
        own, sib, part, ici = refs[10:14], refs[14:18], refs[18:22], refs[22:26]
        rsmall = refs[26]
        loc_sems, d2d_send, d2d_recv, ici_send, ici_recv, sm_send, sm_recv = refs[27:]
        x, y, c, me = _me()
        sibling = (x, y, 1 - c)
        chips = [(1 - x, y), (x, 1 - y), (1 - x, 1 - y)]
        my_chip = 2 * x + y

        rsmall[0] = sp_ref[...]
        small = []
        for k in range(1, N_DEV):
            cp = pltpu.make_async_remote_copy(
                src_ref=rsmall.at[0], dst_ref=rsmall.at[k], send_sem=sm_send.at[k - 1], recv_sem=sm_recv.at[k - 1],
                device_id=_peer(k)[0], device_id_type=MESH_ID)
            cp.start()
            small.append(cp)

        local, to_sib = [], []
        for t in range(4):
            for q in range(4):
                cp = pltpu.make_async_copy(g3[t].at[2 * q + c], own[t].at[q], loc_sems.at[t, q])
                cp.start()
                local.append(cp)
                cp = pltpu.make_async_remote_copy(
                    src_ref=g3[t].at[2 * q + 1 - c], dst_ref=sib[t].at[q], send_sem=d2d_send.at[t, q],
                    recv_sem=d2d_recv.at[t, q], device_id=sibling, device_id_type=MESH_ID)
                cp.start()
                to_sib.append(cp)

        def chunks(t, fn):
            rows = ADAM_ROWS[t]

            def step(i, carry):
                fn(pl.ds(pl.multiple_of(i * rows, rows), rows))
                return carry

            lax.fori_loop(0, SHARD_SHAPES[t][0] // rows, step, 0)

        to_chips = []
        for t in range(4):
            for q in range(4):
                local[4 * t + q].wait()
                to_sib[4 * t + q].wait_recv()

            def pair_sums(r, t=t):
                for q in range(4):
                    part[t][q, r, :] = (own[t][q, r, :].astype(F32) + sib[t][q, r, :].astype(F32)).astype(BF16)
                gsum[t][r, :] = own[t][my_chip, r, :].astype(F32) + sib[t][my_chip, r, :].astype(F32)

            chunks(t, pair_sums)
            for j, (px, py) in enumerate(chips):
                cp = pltpu.make_async_remote_copy(
                    src_ref=part[t].at[2 * px + py], dst_ref=ici[t].at[j], send_sem=ici_send.at[t, j],
                    recv_sem=ici_recv.at[t, j], device_id=(px, py, c), device_id_type=MESH_ID)
                cp.start()
                to_chips.append(cp)

        for t in range(4):
            for j in range(3):
                to_chips[3 * t + j].wait_recv()

            def add_chips(r, t=t):
                g = gsum[t][r, :]
                for j in range(3):
                    g = g + ici[t][j, r, :].astype(F32)
                gsum[t][r, :] = g

            chunks(t, add_chips)

        for cp in small:
            cp.wait_recv()
        tot = rsmall[me]
        for d in range(1, N_DEV):
            tot = tot + rsmall[jnp.bitwise_xor(me, d)]
        ssum[...] = tot
        for cp in small + to_sib + to_chips:
            cp.wait_send()

    vmem = pl.BlockSpec(memory_space=pltpu.VMEM)
    hbm = pl.BlockSpec(memory_space=pl.ANY)
    dma = pltpu.SemaphoreType.DMA
    return _pcall(
        body, name="reduce_grads",
        in_specs=[hbm] * 4 + [vmem], out_specs=[vmem] * 5,
        out_shape=[jax.ShapeDtypeStruct(s, F32) for s in SHARD_SHAPES] + [jax.ShapeDtypeStruct((8, D_MODEL), F32)],
        scratch_shapes=[pltpu.VMEM((n,) + s, BF16) for n in (4, 4, 4, 3) for s in SHARD_SHAPES]
        + [pltpu.VMEM((N_DEV, 8, D_MODEL), F32), dma((4, 4)), dma((4, 4)), dma((4, 4)), dma((4, 3)), dma((4, 3)),
           dma((N_DEV - 1,)), dma((N_DEV - 1,))],
        compiler_params=_cparams(),
    )(*grads3, small_part)


def _adamw_update(grads, small_grad, wmv, small_wmv):
    def body(*refs):
        g_refs, sg_ref = refs[0:4], refs[4]
        wmv_refs = [refs[5 + 3 * t:8 + 3 * t] for t in range(4)]
        swmv_ref = refs[17]
        out_refs = [refs[18 + 4 * t:22 + 4 * t] for t in range(4)]
        sout_ref = refs[34]
        tot = sg_ref[...]
        delta, m, v = _adamw(swmv_ref[0], tot, swmv_ref[1], swmv_ref[2])
        sout_ref[0], sout_ref[1], sout_ref[2], sout_ref[3] = tot, delta, m, v
        for t in range(4):
            rows = ADAM_ROWS[t]
            w_ref, m_ref, v_ref = wmv_refs[t]
            g_out, d_out, m_out, v_out = out_refs[t]

            def step(i, carry, g_ref=g_refs[t], rows=rows, w_ref=w_ref, m_ref=m_ref, v_ref=v_ref,
                     g_out=g_out, d_out=d_out, m_out=m_out, v_out=v_out):
                r = pl.ds(pl.multiple_of(i * rows, rows), rows)
                g = g_ref[r, :]
                delta, m, v = _adamw(w_ref[r, :], g, m_ref[r, :], v_ref[r, :])
                g_out[r, :], d_out[r, :], m_out[r, :], v_out[r, :] = g, delta, m, v
                return carry

            lax.fori_loop(0, SHARD_SHAPES[t][0] // rows, step, 0)

    vmem = pl.BlockSpec(memory_space=pltpu.VMEM)
    flat_wmv = [a for trio in wmv for a in trio]
    return _pcall(
        body, name="adamw",
        in_specs=[vmem] * 18, out_specs=[vmem] * 17,
        out_shape=[jax.ShapeDtypeStruct(s, F32) for s in SHARD_SHAPES for _ in range(4)]
        + [jax.ShapeDtypeStruct((4, 8, D_MODEL), F32)],
        compiler_params=_cparams(),
    )(*grads, small_grad, *flat_wmv, small_wmv)


def _small_rows(ln_g, ln_b, q_norm_g, kv_norm_g, extra=None):
    pad = lambda a: jnp.pad(a, (0, D_MODEL - a.shape[0]))
    rows = [ln_g, ln_b, pad(q_norm_g), pad(kv_norm_g)] + ([] if extra is None else [extra])
    return jnp.pad(jnp.stack(rows), ((0, 8 - len(rows)), (0, 0)))


def kernel(x, w_in, q_norm_g, kv_norm_g, w_uq, w_ukv, w_out, ln_g, ln_b, loss_target, m_w_in, m_q_norm_g, m_kv_norm_g, m_w_uq, m_w_ukv, m_w_out, m_ln_g, m_ln_b, v_w_in, v_q_norm_g, v_kv_norm_g, v_w_uq, v_w_ukv, v_w_out, v_ln_g, v_ln_b):
    w_in_r, w_uq_r, w_ukv_r, full_out = _all_gather_weights([w_in, w_uq, w_ukv, w_out])
    sq_err, grad_x, g_in, g_uq, g_ukv, g_out, g_qg, g_kvg, g_lng, g_lnb = _local_step(
        x[0], loss_target[0], w_in_r, w_uq_r, w_ukv_r, full_out, q_norm_g, kv_norm_g, ln_g, ln_b)

    grads3 = [g_in] + [g.astype(BF16) for g in (
        g_uq.reshape(Q_RANK, N_DEV, -1).transpose(1, 0, 2), g_ukv.reshape(KV_RANK, N_DEV, -1).transpose(1, 0, 2),
        g_out.reshape(N_DEV, D_MODEL // N_DEV, D_MODEL))]
    small_part = _small_rows(g_lng, g_lnb, g_qg, g_kvg, sq_err)
    small_wmv = jnp.stack([_small_rows(ln_g, ln_b, q_norm_g, kv_norm_g),
                           _small_rows(m_ln_g, m_ln_b, m_q_norm_g, m_kv_norm_g),
                           _small_rows(v_ln_g, v_ln_b, v_q_norm_g, v_kv_norm_g)])
    wmv = [(w_in, m_w_in, v_w_in), (w_uq, m_w_uq, v_w_uq), (w_ukv, m_w_ukv, v_w_ukv), (w_out, m_w_out, v_w_out)]
    sums = _reduce_grads(grads3, small_part)
    res = _adamw_update(sums[:4], sums[4], wmv, small_wmv)
    big = [res[4 * t:4 * t + 4] for t in range(4)]
    small = res[16]
    loss = (0.5 / D_MODEL) * jnp.sum(small[0, 4])

    def group(kind):
        s = small[kind]
        return (big[0][kind], s[2, :Q_RANK], s[3, :KV_RANK], big[1][kind], big[2][kind], big[3][kind], s[0], s[1])

    return (loss, grad_x[None], *group(0), *group(1), *group(2), *group(3))
```

```python
from typing import Callable, NamedTuple

import numpy as np
import jax
import jax.numpy as jnp
from jax import lax
from jax.experimental import pallas as pl
from jax.experimental.pallas import tpu as pltpu

F32 = jnp.float32
BF16 = jnp.bfloat16

D_MODEL = 1024
ROPE_THETA = 500000.0
NEG = -1e30
RMS_EPS = 1e-6
LN_EPS = 1e-5
HEADS = 8
MLA_NOPE = 64
MLA_ROPE = 32
MLA_V = 64
Q_RANK = 384
KV_RANK = 256
DIL_HEAD = 64
DIL_ROT = 16
DIL_CONFIGS = ((128, 1), (512, 4), (2048, 16))
HW = HEADS * 64
QW = HW + HEADS * MLA_ROPE
IN_SPLITS = (Q_RANK, KV_RANK, MLA_ROPE, HW, HW, HW, HW, HW)
IN_WIDTH = sum(IN_SPLITS)
ALPHA = 2.0 ** 0.25
MLA_SCALE = (MLA_NOPE + MLA_ROPE) ** -0.5
DIL_SCALE = DIL_HEAD ** -0.5
LOG2E = 1.4426950408889634
LN2 = 0.6931471805599453

ADAM_LR = 0.001
ADAM_B1 = 0.9
ADAM_B2 = 0.999
ADAM_EPS = 1e-08
ADAM_WD = 0.01
ADAM_STEP = 10

N_DEV = 8
LANES = 128
VMEM_LIMIT = 56 * 1024 * 1024
BLOCK_TOKENS = 512
BLOCK_MLA = 512
BLOCK_DIL = 512

C_CQ, C_CKV, C_KR, C_GA, C_QB, C_KB, C_VB, C_GB, C_END = 0, 384, 640, 768, 1280, 1792, 2304, 2816, 3328

NT = (((1,), (1,)), ((), ()))
TN = (((0,), (0,)), ((), ()))


def _pcall(body, **kw):
    return pl.pallas_call(body, **kw)


def _cparams(**kw):
    return pltpu.CompilerParams(vmem_limit_bytes=VMEM_LIMIT, **kw)


def _rope_tables(seq):
    def tabs(dim, period):
        half = dim // 2
        inv = np.float32(ROPE_THETA) ** (-np.arange(0, dim, 2, dtype=np.float32) / np.float32(dim))
        ang = np.arange(seq, dtype=np.float32)[:, None] * inv.astype(np.float32)[None, :]
        cos, sin = np.cos(ang).astype(np.float32), np.sin(ang).astype(np.float32)
        j = np.arange(LANES) % period
        f = j % half
        c = np.where(j < dim, cos[:, f], np.float32(1.0))
        s1 = np.where(j < half, -sin[:, f], np.float32(0.0))
        s2 = np.where((j >= half) & (j < dim), sin[:, f], np.float32(0.0))
        return [c, s1, s2]
    return np.stack(tabs(MLA_ROPE, MLA_ROPE) + tabs(DIL_ROT, DIL_HEAD)).astype(np.float32)


def _rope(t, c, s1, s2, half):
    return t * c + pltpu.roll(t, LANES - half, 1) * s1 + pltpu.roll(t, half, 1) * s2


def _rope_t(d, c, s1, s2, half):
    return d * c + pltpu.roll(d * s1, half, 1) + pltpu.roll(d * s2, LANES - half, 1)


def _rope_wide(fn, t, c, s1, s2, half):
    return jnp.concatenate(
        [fn(t[:, i:i + LANES], c, s1, s2, half) for i in range(0, t.shape[1], LANES)], axis=1)


def _mla_bias_t(blk):
    a = np.arange(blk)
    causal = np.where(a[:, None] <= a[None, :], 0.0, NEG)
    return np.stack([np.zeros((blk, blk)), causal]).astype(np.float32)


def _dil_bias_t(blk):
    span = DIL_CONFIGS[-1][0] // blk
    a = np.arange(blk)
    out = []
    for off in range(span + 1):
        delta = blk * off + a[None, :] - a[:, None]
        mult = np.zeros((blk, blk))
        for window, dil in DIL_CONFIGS:
            mult += (delta >= 0) & (delta % dil == 0) & (delta <= window)
        out.append(np.where(mult > 0, np.log2(np.maximum(mult, 1.0)), NEG))
    return np.stack(out).astype(np.float32)


def _steps(nq, span, by_key, diag_only_bias):
    rows = []
    if by_key:
        for ki in range(nq):
            hi = min(nq - 1, ki + span)
            for qi in range(ki, hi + 1):
                rows.append((qi, ki, int(qi == ki), int(qi == hi)))
    else:
        for qi in range(nq):
            lo = max(0, qi - span)
            for ki in range(lo, qi + 1):
                rows.append((qi, ki, int(ki == lo), int(ki == qi)))
    arr = np.array(rows, dtype=np.int32)
    off = arr[:, 0] - arr[:, 1]
    bias_idx = (off == 0).astype(np.int32) if diag_only_bias else off.astype(np.int32)
    return [jnp.asarray(v) for v in (arr[:, 0], arr[:, 1], bias_idx, arr[:, 2], arr[:, 3])]


def _fwd_proj(x, w_in_r, w_uq_r, w_ukv_r, qg, kvg, tabs, bt):
    seq = x.shape[0]

    def body(x_ref, win_ref, wuq_ref, wukv_ref, qg_ref, kvg_ref, tab_ref,
             cq_ref, ckv_ref, qn_ref, kvn_ref, qcat_ref, kn_ref, kpe_ref, v_ref,
             ga_ref, gb_ref, qb_ref, kb_ref, vb_ref, knt_ref, kpet_ref, vt_ref, kbt_ref, vbt_ref):
        xb = x_ref[...].astype(BF16)

        def proj(lo, hi):
            return jnp.dot(xb, win_ref[:, lo:hi], preferred_element_type=F32)

        m_tabs = (tab_ref[0], tab_ref[1], tab_ref[2])
        d_tabs = (tab_ref[3], tab_ref[4], tab_ref[5])

        cq = proj(C_CQ, C_CKV)
        cq_ref[...] = cq
        qn = (cq * lax.rsqrt(jnp.mean(cq * cq, axis=1, keepdims=True) + RMS_EPS) * qg_ref[...]).astype(BF16)
        qn_ref[...] = qn
        q = jnp.dot(qn, wuq_ref[...], preferred_element_type=F32)
        qcat_ref[:, :HW] = (q[:, :HW] * (MLA_SCALE * LOG2E)).astype(BF16)
        qcat_ref[:, HW:] = (_rope_wide(_rope, q[:, HW:], *m_tabs, MLA_ROPE // 2) * (MLA_SCALE * LOG2E)).astype(BF16)

        ckv = proj(C_CKV, C_KR)
        ckv_ref[...] = ckv
        kvn = (ckv * lax.rsqrt(jnp.mean(ckv * ckv, axis=1, keepdims=True) + RMS_EPS) * kvg_ref[...]).astype(BF16)
        kvn_ref[...] = kvn
        kv = jnp.dot(kvn, wukv_ref[...], preferred_element_type=F32)
        kn_ref[...] = kv[:, :HW].astype(BF16)
        v_ref[...] = kv[:, HW:].astype(BF16)
        knt_ref[...] = kv[:, :HW].T.astype(BF16)
        vt_ref[...] = kv[:, HW:].T.astype(BF16)

        kpe = _rope(proj(C_KR, C_GA), *m_tabs, MLA_ROPE // 2)
        kpe_ref[...] = kpe.astype(BF16)
        kpet_ref[...] = kpe.T[:MLA_ROPE, :].astype(BF16)
        ga_ref[...] = proj(C_GA, C_QB)
        qb_ref[...] = (_rope_wide(_rope, proj(C_QB, C_KB), *d_tabs, DIL_ROT // 2) * (DIL_SCALE * LOG2E)).astype(BF16)
        kb = _rope_wide(_rope, proj(C_KB, C_VB), *d_tabs, DIL_ROT // 2)
        kb_ref[...] = kb.astype(BF16)
        kbt_ref[...] = kb.T.astype(BF16)
        vb = proj(C_VB, C_GB)
        vb_ref[...] = vb.astype(BF16)
        vbt_ref[...] = vb.T.astype(BF16)
        gb_ref[...] = proj(C_GB, C_END)

    def tok(width):
        return pl.BlockSpec((bt, width), lambda i: (i, 0))

    def tok_t(height):
        return pl.BlockSpec((height, bt), lambda i: (0, i))

    def full(a):
        return pl.BlockSpec(a.shape, lambda i: (0,) * a.ndim)

    outs = [(Q_RANK, F32), (KV_RANK, F32), (Q_RANK, BF16), (KV_RANK, BF16), (QW, BF16), (HW, BF16),
            (LANES, BF16), (HW, BF16), (HW, F32), (HW, F32), (HW, BF16), (HW, BF16), (HW, BF16)]
    outs_t = [HW, MLA_ROPE, HW, HW, HW]
    return _pcall(
        body, name="fwd_proj", grid=(seq // bt,),
        in_specs=[tok(D_MODEL), full(w_in_r), full(w_uq_r), full(w_ukv_r), full(qg), full(kvg),
                  pl.BlockSpec((6, bt, LANES), lambda i: (0, i, 0))],
        out_specs=[tok(w) for w, _ in outs] + [tok_t(h) for h in outs_t],
        out_shape=[jax.ShapeDtypeStruct((seq, w), dt) for w, dt in outs]
        + [jax.ShapeDtypeStruct((h, seq), BF16) for h in outs_t],
        compiler_params=_cparams(dimension_semantics=("arbitrary",)),
    )(x, w_in_r, w_uq_r, w_ukv_r, qg, kvg, tabs)


def _head_masks(lane, h):
    e, g = h % 2, h % 4
    me = (lane >= 64 * e) & (lane < 64 * e + 64)
    mr = (lane >= 32 * g) & (lane < 32 * g + 32)
    return me, mr


def _masked(mask, a):
    return jnp.where(mask, a, jnp.zeros_like(a))


def _pair_operands(q_ref, k_ref, kpe_ref, lane, j):
    cols = slice(LANES * j, LANES * (j + 1))
    qc = q_ref[:, cols]
    kj = k_ref[:, cols]
    kes = []
    for h in (2 * j, 2 * j + 1):
        me, mr = _head_masks(lane, h)
        ke = _masked(me, kj)
        if kpe_ref is not None:
            ke = jnp.concatenate([ke, _masked(mr, kpe_ref[...])], axis=1)
        kes.append(ke)
    if kpe_ref is not None:
        qc = jnp.concatenate([qc, q_ref[:, HW + LANES * (j // 2):HW + LANES * (j // 2 + 1)]], axis=1)
    return qc, kes


class Rider(NamedTuple):
    args: list
    in_specs: list
    out_shape: list
    out_specs: list
    scratch: list
    start: Callable
    finish: Callable


def _ride_along(body, ride, n_prefetch, n_in, n_out, n_scratch, n_steps):
    if ride is None:
        return body

    def wrapped(*refs):
        pre, rest = refs[:n_prefetch], refs[n_prefetch:]
        a = n_in
        b = a + len(ride.args)
        c = b + n_out
        d = c + len(ride.out_shape)
        e = d + n_scratch
        mine = (rest[a:b], rest[c:d], rest[e:])
        t = pl.program_id(0)
        pl.when(t == 0)(lambda: ride.start(*mine))
        body(*pre, *rest[:a], *rest[b:c], *rest[d:e])
        pl.when(t == n_steps - 1)(lambda: ride.finish(*mine))

    return wrapped


def _attn_fwd(name, q, k, kpe, vt, bias_t, steps, blk, ride=None):
    seq = q.shape[0]
    mla = kpe is not None
    n_steps = int(steps[0].shape[0])

    def body(qi_r, ki_r, bi_r, fi_r, la_r, *refs):
        if mla:
            q_ref, k_ref, kpe_ref, vt_ref, b_ref, o_ref, lse_ref, m_sc, l_sc, acc_sc, st_sc = refs
        else:
            q_ref, k_ref, vt_ref, b_ref, o_ref, lse_ref, m_sc, l_sc, acc_sc, st_sc = refs
        t = pl.program_id(0)

        @pl.when(fi_r[t] == 1)
        def _():
            m_sc[...] = jnp.full(m_sc.shape, NEG, F32)
            l_sc[...] = jnp.zeros(l_sc.shape, F32)
            acc_sc[...] = jnp.zeros(acc_sc.shape, F32)

        lane = lax.broadcasted_iota(jnp.int32, (1, LANES), 1)
        ones = jnp.ones((16, blk), BF16)

        def pair_scores(j, with_bias):
            qc, kes = _pair_operands(q_ref, k_ref, kpe_ref if mla else None, lane, j)
            st = lax.dot_general(jnp.concatenate(kes, axis=0), qc, NT, preferred_element_type=F32)
            maxes = []
            for e in range(2):
                se = st[e * blk:(e + 1) * blk]
                if with_bias:
                    se = se + b_ref[0]
                st_sc[j % 2, e * blk:(e + 1) * blk] = se
                maxes.append(jnp.max(se, axis=0, keepdims=True))
            return maxes

        def softmax_pv(h, col_max):
            st = st_sc[(h // 2) % 2, (h % 2) * blk:(h % 2 + 1) * blk]
            hrow = slice(h, h + 1)
            m_prev = m_sc[hrow, :]
            m_new = jnp.maximum(m_prev, col_max)
            alpha = jnp.exp2(m_prev - m_new)
            pt = jnp.exp2(st - m_new).astype(BF16)
            m_sc[hrow, :] = m_new
            rows = slice(64 * h, 64 * h + 64)
            res = jnp.dot(jnp.concatenate([vt_ref[rows, :], ones], axis=0), pt, preferred_element_type=F32)
            acc_sc[rows, :] = alpha * acc_sc[rows, :] + res[:64]
            l_sc[hrow, :] = alpha * l_sc[hrow, :] + res[64:65]

        def step(with_bias):
            maxes = pair_scores(0, with_bias)
            for j in range(HEADS // 2):
                cur = maxes
                if j + 1 < HEADS // 2:
                    maxes = pair_scores(j + 1, with_bias)
                softmax_pv(2 * j, cur[0])
                softmax_pv(2 * j + 1, cur[1])

        if mla:
            pl.when(bi_r[t] == 1)(lambda: step(True))
            pl.when(bi_r[t] == 0)(lambda: step(False))
        else:
            step(True)

        @pl.when(la_r[t] == 1)
        def _():
            for h in range(HEADS):
                rows = slice(64 * h, 64 * h + 64)
                acc_sc[rows, :] = acc_sc[rows, :] / l_sc[h:h + 1, :]
            o_ref[...] = acc_sc[...].T
            lse_ref[...] = m_sc[...] + jnp.log2(l_sc[...])

    qmap = lambda t, qi, ki, bi, fi, la: (qi[t], 0)
    kmap = lambda t, qi, ki, bi, fi, la: (ki[t], 0)
    in_specs = [pl.BlockSpec((blk, q.shape[1]), qmap), pl.BlockSpec((blk, HW), kmap)]
    args = [q, k]
    if mla:
        in_specs.append(pl.BlockSpec((blk, LANES), kmap))
        args.append(kpe)
    in_specs += [pl.BlockSpec((HW, blk), lambda t, qi, ki, bi, fi, la: (0, ki[t])),
                 pl.BlockSpec((1, blk, blk), lambda t, qi, ki, bi, fi, la: (bi[t], 0, 0))]
    args += [vt, bias_t]
    out_specs = [pl.BlockSpec((blk, HW), qmap), pl.BlockSpec((HEADS, blk), lambda t, qi, ki, bi, fi, la: (0, qi[t]))]
    out_shape = [jax.ShapeDtypeStruct((seq, HW), F32), jax.ShapeDtypeStruct((HEADS, seq), F32)]
    scratch = [pltpu.VMEM((HEADS, blk), F32), pltpu.VMEM((HEADS, blk), F32),
               pltpu.VMEM((HW, blk), F32), pltpu.VMEM((2, 2 * blk, blk), F32)]
    body = _ride_along(body, ride, 5, len(args), len(out_shape), len(scratch), n_steps)
    if ride is not None:
        args, in_specs = args + ride.args, in_specs + ride.in_specs
        out_specs, out_shape, scratch = out_specs + ride.out_specs, out_shape + ride.out_shape, scratch + ride.scratch
    return _pcall(
        body, name=name,
        grid_spec=pltpu.PrefetchScalarGridSpec(
            num_scalar_prefetch=5, grid=(n_steps,), in_specs=in_specs, out_specs=out_specs, scratch_shapes=scratch),
        out_shape=out_shape,
        compiler_params=_cparams(dimension_semantics=("arbitrary",)),
    )(*steps, *args)


def _attn_bwd(name, q, k, kpe, v, kt, kpet, bias_t, do, lse, dstat, steps, blk, ride=None):
    seq = q.shape[0]
    mla = kpe is not None
    qw = q.shape[1]
    n_steps = int(steps[0].shape[0])
    dk_dtype = BF16 if mla else F32

    def body(qi_r, ki_r, bi_r, fi_r, la_r, *refs):
        if mla:
            (q_ref, k_ref, kpe_ref, v_ref, kt_ref, kpet_ref, b_ref, do_ref, lse_ref, d_ref,
             dq_ref, dk_ref, dkpe_ref, dv_ref, dk_sc, dkpe_sc, dv_sc, st_sc, dpt_sc) = refs
        else:
            (q_ref, k_ref, v_ref, kt_ref, b_ref, do_ref, lse_ref, d_ref,
             dq_ref, dk_ref, dv_ref, dk_sc, dv_sc, st_sc, dpt_sc) = refs
        t = pl.program_id(0)

        @pl.when(t == 0)
        def _():
            dq_ref[...] = jnp.zeros(dq_ref.shape, F32)

        @pl.when(fi_r[t] == 1)
        def _():
            dk_sc[...] = jnp.zeros(dk_sc.shape, F32)
            dv_sc[...] = jnp.zeros(dv_sc.shape, F32)
            if mla:
                dkpe_sc[...] = jnp.zeros(dkpe_sc.shape, F32)

        qi = qi_r[t]
        lane = lax.broadcasted_iota(jnp.int32, (1, LANES), 1)

        def pair_matmuls(j):
            cols = slice(LANES * j, LANES * (j + 1))
            qc, kes = _pair_operands(q_ref, k_ref, kpe_ref if mla else None, lane, j)
            st_sc[j % 2] = lax.dot_general(jnp.concatenate(kes, axis=0), qc, NT, preferred_element_type=F32)
            vj = v_ref[:, cols]
            ves = [_masked(_head_masks(lane, h)[0], vj) for h in (2 * j, 2 * j + 1)]
            dpt_sc[j % 2] = lax.dot_general(
                jnp.concatenate(ves, axis=0), do_ref[:, cols], NT, preferred_element_type=F32)

        def pair_grads(j, with_bias):
            cols = slice(LANES * j, LANES * (j + 1))
            qj, doj = q_ref[:, cols], do_ref[:, cols]
            if mla:
                qr = q_ref[:, HW + LANES * (j // 2):HW + LANES * (j // 2 + 1)]
            pts, dsts, qms, doms = [], [], [], []
            for e in range(2):
                h = 2 * j + e
                me, mr = _head_masks(lane, h)
                st = st_sc[j % 2, e * blk:(e + 1) * blk]
                if with_bias:
                    st = st + b_ref[0]
                pt = jnp.exp2(st - lse_ref[h:h + 1, :])
                dst = (pt * (dpt_sc[j % 2, e * blk:(e + 1) * blk] - d_ref[h:h + 1, :])).astype(BF16)
                pts.append(pt.astype(BF16))
                dsts.append(dst)
                doms.append(_masked(me, doj))
                qm = _masked(me, qj)
                if mla:
                    qm = jnp.concatenate([qm, _masked(mr, qr)], axis=1)
                qms.append(qm)
                ktl = kt_ref[64 * h:64 * h + 64, :]
                if mla:
                    ktl = jnp.concatenate([ktl, kpet_ref[...]], axis=0)
                dqc = jnp.dot(ktl, dst, preferred_element_type=F32)
                dq_ref[qi, 64 * h:64 * h + 64, :] += dqc[:64]
                if mla:
                    dq_ref[qi, HW + MLA_ROPE * h:HW + MLA_ROPE * (h + 1), :] += dqc[64:]
            dv_sc[:, cols] += jnp.dot(
                jnp.concatenate(pts, axis=1), jnp.concatenate(doms, axis=0), preferred_element_type=F32)
            dkc = jnp.dot(jnp.concatenate(dsts, axis=1), jnp.concatenate(qms, axis=0), preferred_element_type=F32)
            dk_sc[:, cols] += dkc[:, :LANES]
            if mla:
                dkpe_sc[...] += dkc[:, LANES:]

        def step(with_bias):
            pair_matmuls(0)
            for j in range(HEADS // 2):
                if j + 1 < HEADS // 2:
                    pair_matmuls(j + 1)
                pair_grads(j, with_bias)

        if mla:
            pl.when(bi_r[t] == 1)(lambda: step(True))
            pl.when(bi_r[t] == 0)(lambda: step(False))
        else:
            step(True)

        @pl.when(la_r[t] == 1)
        def _():
            dk_ref[...] = (dk_sc[...] * LN2).astype(dk_ref.dtype)
            dv_ref[...] = dv_sc[...].astype(dv_ref.dtype)
            if mla:
                dkpe_ref[...] = dkpe_sc[...] * LN2

    qmap = lambda t, qi, ki, bi, fi, la: (qi[t], 0)
    kmap = lambda t, qi, ki, bi, fi, la: (ki[t], 0)
    qmap_t = lambda t, qi, ki, bi, fi, la: (0, qi[t])
    kmap_t = lambda t, qi, ki, bi, fi, la: (0, ki[t])
    in_specs = [pl.BlockSpec((blk, qw), qmap), pl.BlockSpec((blk, HW), kmap)]
    args = [q, k]
    if mla:
        in_specs.append(pl.BlockSpec((blk, LANES), kmap))
        args.append(kpe)
    in_specs += [pl.BlockSpec((blk, HW), kmap), pl.BlockSpec((HW, blk), kmap_t)]
    args += [v, kt]
    if mla:
        in_specs.append(pl.BlockSpec((MLA_ROPE, blk), kmap_t))
        args.append(kpet)
    in_specs += [pl.BlockSpec((1, blk, blk), lambda t, qi, ki, bi, fi, la: (bi[t], 0, 0)),
                 pl.BlockSpec((blk, HW), qmap), pl.BlockSpec((HEADS, blk), qmap_t), pl.BlockSpec((HEADS, blk), qmap_t)]
    args += [bias_t, do, lse, dstat]
    dq_shape = (seq // blk, qw, blk)
    out_specs = [pl.BlockSpec(dq_shape, lambda t, qi, ki, bi, fi, la: (0, 0, 0)), pl.BlockSpec((blk, HW), kmap)]
    out_shape = [jax.ShapeDtypeStruct(dq_shape, F32), jax.ShapeDtypeStruct((seq, HW), dk_dtype)]
    scratch = [pltpu.VMEM((blk, HW), F32)]
    if mla:
        out_specs.append(pl.BlockSpec((blk, LANES), kmap))
        out_shape.append(jax.ShapeDtypeStruct((seq, LANES), F32))
        scratch.append(pltpu.VMEM((blk, LANES), F32))
    out_specs.append(pl.BlockSpec((blk, HW), kmap))
    out_shape.append(jax.ShapeDtypeStruct((seq, HW), BF16))
    scratch.append(pltpu.VMEM((blk, HW), F32))
    scratch += [pltpu.VMEM((2, 2 * blk, blk), F32), pltpu.VMEM((2, 2 * blk, blk), F32)]
    body = _ride_along(body, ride, 5, len(args), len(out_shape), len(scratch), n_steps)
    if ride is not None:
        args, in_specs = args + ride.args, in_specs + ride.in_specs
        out_specs, out_shape, scratch = out_specs + ride.out_specs, out_shape + ride.out_shape, scratch + ride.scratch
    return _pcall(
        body, name=name,
        grid_spec=pltpu.PrefetchScalarGridSpec(
            num_scalar_prefetch=5, grid=(n_steps,), in_specs=in_specs, out_specs=out_specs,
            scratch_shapes=scratch),
        out_shape=out_shape,
        compiler_params=_cparams(dimension_semantics=("arbitrary",)),
    )(*steps, *args)


def _out_ln(oa, ob, ga, gb, x, tgt, w_out, ln_g, ln_b, bt):
    seq = x.shape[0]

    def body(oa_ref, ob_ref, ga_ref, gb_ref, x_ref, tgt_ref, w_ref, g_ref, b_ref,
             dz_ref, doa_ref, dob_ref, dga_ref, dgb_ref, da_ref, db_ref, gwb_ref, small_ref, gw_ref):
        i = pl.program_id(0)

        @pl.when(i == 0)
        def _():
            gw_ref[...] = jnp.zeros(gw_ref.shape, F32)
            small_ref[...] = jnp.zeros(small_ref.shape, F32)

        def gate(g):
            sig = 0.5 * jnp.tanh(0.5 * g) + 0.5
            return g * sig, sig * (1.0 + g * (1.0 - sig))

        o_a, o_b = oa_ref[...], ob_ref[...]
        g_a, g_b = ga_ref[...], gb_ref[...]
        sa, dsa = gate(g_a)
        sb, dsb = gate(g_b)
        mix = jnp.concatenate([o_a * sa, o_b * sb], axis=1).astype(BF16)
        z = ALPHA * x_ref[...] + jnp.dot(mix, w_ref[...], preferred_element_type=F32)
        mu = jnp.mean(z, axis=1, keepdims=True)
        zc = z - mu
        rstd = lax.rsqrt(jnp.mean(zc * zc, axis=1, keepdims=True) + LN_EPS)
        xhat = zc * rstd
        gam = g_ref[...]
        diff = xhat * gam + b_ref[...] - tgt_ref[...]
        dy = diff * (1.0 / D_MODEL)
        small_ref[0:1, :] += jnp.sum(dy * xhat, axis=0, keepdims=True)
        small_ref[1:2, :] += jnp.sum(dy, axis=0, keepdims=True)
        small_ref[2:3, :] += jnp.sum(diff * diff, axis=0, keepdims=True)
        dxh = dy * gam
        dz = rstd * (dxh - jnp.mean(dxh, axis=1, keepdims=True) - xhat * jnp.mean(dxh * xhat, axis=1, keepdims=True))
        dz_ref[...] = dz
        dzb = dz.astype(BF16)
        gw_ref[...] += lax.dot_general(mix, dzb, TN, preferred_element_type=F32)

        @pl.when(i == seq // bt - 1)
        def _():
            gwb_ref[...] = gw_ref[...].astype(BF16)

        dmix = lax.dot_general(dzb, w_ref[...], NT, preferred_element_type=F32)
        doa, dob = dmix[:, :HW] * sa, dmix[:, HW:] * sb
        doa_ref[...] = doa.astype(BF16)
        dob_ref[...] = dob.astype(BF16)
        dga_ref[...] = (dmix[:, :HW] * o_a * dsa).astype(BF16)
        dgb_ref[...] = (dmix[:, HW:] * o_b * dsb).astype(BF16)
        head_of = (lax.broadcasted_iota(jnp.int32, (2 * HW, LANES), 0) % HW) // 64
        ind = (head_of == lax.broadcasted_iota(jnp.int32, (2 * HW, LANES), 1)).astype(BF16)

        def head_sums(prod):
            hi = prod.astype(BF16)
            lo = (prod - hi.astype(F32)).astype(BF16)
            sums = jnp.dot(jnp.concatenate([hi, lo], axis=1), ind, preferred_element_type=F32)
            return sums.T[:HEADS, :]

        da_ref[...] = head_sums(doa * o_a)
        db_ref[...] = head_sums(dob * o_b)

    def tok(width):
        return pl.BlockSpec((bt, width), lambda i: (i, 0))

    def full(shape):
        return pl.BlockSpec(shape, lambda i: (0,) * len(shape))

    stat = pl.BlockSpec((HEADS, bt), lambda i: (0, i))
    return _pcall(
        body, name="out_ln", grid=(seq // bt,),
        in_specs=[tok(HW), tok(HW), tok(HW), tok(HW), tok(D_MODEL), tok(D_MODEL),
                  full((D_MODEL, D_MODEL)), full((1, D_MODEL)), full((1, D_MODEL))],
        out_specs=[tok(D_MODEL), tok(HW), tok(HW), tok(HW), tok(HW), stat, stat,
                   full((D_MODEL, D_MODEL)), full((8, D_MODEL))],
        out_shape=[jax.ShapeDtypeStruct((seq, D_MODEL), F32)] + [jax.ShapeDtypeStruct((seq, HW), BF16)] * 4
        + [jax.ShapeDtypeStruct((HEADS, seq), F32)] * 2
        + [jax.ShapeDtypeStruct((D_MODEL, D_MODEL), BF16), jax.ShapeDtypeStruct((8, D_MODEL), F32)],
        scratch_shapes=[pltpu.VMEM((D_MODEL, D_MODEL), F32)],
        compiler_params=_cparams(dimension_semantics=("arbitrary",)),
    )(oa, ob, ga, gb, x, tgt, w_out, ln_g, ln_b)


def _bwd_mid(dq_m, dkn, dv, dkpe, dqb, dkb, dvb, dga, dgb, cq, ckv, qn, kvn, w_uq_r, w_ukv_r, qg, kvg, tabs, bt):
    seq = cq.shape[0]

    def body(dqm_ref, dkn_ref, dv_ref, dkpe_ref, dqb_ref, dkb_ref, dvb_ref, dga_ref, dgb_ref,
             cq_ref, ckv_ref, qn_ref, kvn_ref, wuq_ref, wukv_ref, qg_ref, kvg_ref, tab_ref,
             dh_ref, guq_ref, gukv_ref, small_ref):
        i = pl.program_id(0)

        @pl.when(i == 0)
        def _():
            guq_ref[...] = jnp.zeros(guq_ref.shape, F32)
            gukv_ref[...] = jnp.zeros(gukv_ref.shape, F32)
            small_ref[...] = jnp.zeros(small_ref.shape, F32)

        m_tabs = (tab_ref[0], tab_ref[1], tab_ref[2])
        d_tabs = (tab_ref[3], tab_ref[4], tab_ref[5])

        def rms_bwd(c, dn, gain):
            r = lax.rsqrt(jnp.mean(c * c, axis=1, keepdims=True) + RMS_EPS)
            u = dn * gain
            dc = r * u - c * (r * r * r) * jnp.mean(u * c, axis=1, keepdims=True)
            return dc, jnp.sum(dn * c * r, axis=0, keepdims=True)

        dqm = dqm_ref[0].T
        dq = jnp.concatenate(
            [dqm[:, :HW], _rope_wide(_rope_t, dqm[:, HW:], *m_tabs, MLA_ROPE // 2)], axis=1) * MLA_SCALE
        dq = dq.astype(BF16)
        guq_ref[...] += lax.dot_general(qn_ref[...], dq, TN, preferred_element_type=F32)
        dqn = lax.dot_general(dq, wuq_ref[...], NT, preferred_element_type=F32)
        dcq, gq = rms_bwd(cq_ref[...], dqn, qg_ref[...])
        small_ref[0:1, :] += gq

        dkv = jnp.concatenate([dkn_ref[...], dv_ref[...]], axis=1)
        gukv_ref[...] += lax.dot_general(kvn_ref[...], dkv, TN, preferred_element_type=F32)
        dkvn = lax.dot_general(dkv, wukv_ref[...], NT, preferred_element_type=F32)
        dckv, gkv = rms_bwd(ckv_ref[...], dkvn, kvg_ref[...])
        small_ref[1:2, :KV_RANK] += gkv

        dh_ref[:, C_CQ:C_CKV] = dcq.astype(BF16)
        dh_ref[:, C_CKV:C_KR] = dckv.astype(BF16)
        dh_ref[:, C_KR:C_GA] = _rope_t(dkpe_ref[...], *m_tabs, MLA_ROPE // 2).astype(BF16)
        dh_ref[:, C_GA:C_QB] = dga_ref[...]
        dh_ref[:, C_QB:C_KB] = (_rope_wide(_rope_t, dqb_ref[0].T, *d_tabs, DIL_ROT // 2) * DIL_SCALE).astype(BF16)
        dh_ref[:, C_KB:C_VB] = _rope_wide(_rope_t, dkb_ref[...], *d_tabs, DIL_ROT // 2).astype(BF16)
        dh_ref[:, C_VB:C_GB] = dvb_ref[...]
        dh_ref[:, C_GB:C_END] = dgb_ref[...]

    def tok(width):
        return pl.BlockSpec((bt, width), lambda i: (i, 0))

    def tok_t(a):
        per = a.shape[2] // bt
        return pl.BlockSpec((1, a.shape[1], bt), lambda i: (i // per, 0, i % per))

    def full(shape):
        return pl.BlockSpec(shape, lambda i: (0,) * len(shape))

    return _pcall(
        body, name="bwd_mid", grid=(seq // bt,),
        in_specs=[tok_t(dq_m), tok(HW), tok(HW), tok(LANES), tok_t(dqb), tok(HW), tok(HW), tok(HW), tok(HW),
                  tok(Q_RANK), tok(KV_RANK), tok(Q_RANK), tok(KV_RANK),
                  full(w_uq_r.shape), full(w_ukv_r.shape), full((1, Q_RANK)), full((1, KV_RANK)),
                  pl.BlockSpec((6, bt, LANES), lambda i: (0, i, 0))],
        out_specs=[tok(C_END), full(w_uq_r.shape), full(w_ukv_r.shape), full((8, Q_RANK))],
        out_shape=[jax.ShapeDtypeStruct((seq, C_END), BF16), jax.ShapeDtypeStruct(w_uq_r.shape, F32),
                   jax.ShapeDtypeStruct(w_ukv_r.shape, F32), jax.ShapeDtypeStruct((8, Q_RANK), F32)],
        compiler_params=_cparams(dimension_semantics=("arbitrary",)),
    )(dq_m, dkn, dv, dkpe, dqb, dkb, dvb, dga, dgb, cq, ckv, qn, kvn, w_uq_r, w_ukv_r, qg, kvg, tabs)


def _grad_x(dz, dh, w_in_r, bt):
    seq = dz.shape[0]

    def body(dz_ref, dh_ref, w_ref, gx_ref):
        gx_ref[...] = ALPHA * dz_ref[...] + lax.dot_general(
            dh_ref[...], w_ref[...], NT, preferred_element_type=F32)

    return _pcall(
        body, name="grad_x", grid=(seq // bt,),
        in_specs=[pl.BlockSpec((bt, D_MODEL), lambda i: (i, 0)), pl.BlockSpec((bt, C_END), lambda i: (i, 0)),
                  pl.BlockSpec(w_in_r.shape, lambda i: (0, 0))],
        out_specs=pl.BlockSpec((bt, D_MODEL), lambda i: (i, 0)),
        out_shape=jax.ShapeDtypeStruct((seq, D_MODEL), F32),
        compiler_params=_cparams(dimension_semantics=("arbitrary",)),
    )(dz, dh, w_in_r)


def _grad_w_in(x, dh, bt):
    seq = x.shape[0]
    shard = IN_WIDTH // N_DEV
    k_lo, k_hi = IN_SPLITS[0] + IN_SPLITS[1], IN_SPLITS[0] + IN_SPLITS[1] + MLA_ROPE

    def body(x_ref, dh_ref, out_ref, acc):
        i = pl.program_id(0)

        @pl.when(i == 0)
        def _():
            acc[...] = jnp.zeros(acc.shape, F32)

        acc[...] += lax.dot_general(x_ref[...].astype(BF16), dh_ref[...], TN, preferred_element_type=F32)

        @pl.when(i == seq // bt - 1)
        def _():
            kr = acc[:, C_KR:C_GA]
            kr = kr + pltpu.roll(kr, 96, 1) + pltpu.roll(kr, 64, 1) + pltpu.roll(kr, 32, 1)
            for d in range(N_DEV):
                lo, hi = shard * d, shard * (d + 1)
                pieces = []
                if lo < k_lo:
                    pieces.append(acc[:, lo:min(hi, k_lo)])
                if lo < k_hi and hi > k_lo:
                    pieces.append(kr[:, max(lo, k_lo) - k_lo:min(hi, k_hi) - k_lo])
                if hi > k_hi:
                    shift = C_GA - k_hi
                    pieces.append(acc[:, max(lo, k_hi) + shift:hi + shift])
                blk = pieces[0] if len(pieces) == 1 else jnp.concatenate(pieces, axis=1)
                out_ref[d] = blk.astype(BF16)

    return _pcall(
        body, name="grad_w_in", grid=(seq // bt,),
        in_specs=[pl.BlockSpec((bt, D_MODEL), lambda i: (i, 0)), pl.BlockSpec((bt, C_END), lambda i: (i, 0))],
        out_specs=pl.BlockSpec((N_DEV, D_MODEL, shard), lambda i: (0, 0, 0)),
        out_shape=jax.ShapeDtypeStruct((N_DEV, D_MODEL, shard), BF16),
        scratch_shapes=[pltpu.VMEM((D_MODEL, C_END), F32)],
        compiler_params=_cparams(dimension_semantics=("arbitrary",)),
    )(x, dh)


def _restore_grads(g_uq_r, g_ukv_r):
    g_uq = jnp.concatenate(
        [g_uq_r[:, :HW].reshape(Q_RANK, HEADS, MLA_NOPE), g_uq_r[:, HW:].reshape(Q_RANK, HEADS, MLA_ROPE)],
        axis=2).reshape(Q_RANK, HEADS * (MLA_NOPE + MLA_ROPE))
    g_ukv = jnp.concatenate(
        [g_ukv_r[:, :HW].reshape(KV_RANK, HEADS, MLA_NOPE), g_ukv_r[:, HW:].reshape(KV_RANK, HEADS, MLA_V)],
        axis=2).reshape(KV_RANK, HEADS * (MLA_NOPE + MLA_V))
    return g_uq, g_ukv


def _local_step(x, tgt, w_in_r, w_uq_r, w_ukv_r, w_out_rider, g_out_rider, q_norm_g, kv_norm_g, ln_g, ln_b,
                bt=BLOCK_TOKENS, blk_m=BLOCK_MLA, blk_d=BLOCK_DIL):
    seq = x.shape[0]
    tabs = jnp.asarray(_rope_tables(seq))
    qg, kvg = q_norm_g.reshape(1, -1), kv_norm_g.reshape(1, -1)

    (cq, ckv, qn, kvn, qcat, kn, kpe, v, ga, gb, qb, kb, vb, knt, kpet, vt, kbt, vbt) = _fwd_proj(
        x, w_in_r, w_uq_r, w_ukv_r, qg, kvg, tabs, bt)

    nq_m, nq_d = seq // blk_m, seq // blk_d
    span_d = DIL_CONFIGS[-1][0] // blk_d
    bias_m, bias_d = jnp.asarray(_mla_bias_t(blk_m)), jnp.asarray(_dil_bias_t(blk_d))
    oa, lse_a, w_out = _attn_fwd(
        "mla_fwd", qcat, kn, kpe, vt, bias_m, _steps(nq_m, nq_m, False, True), blk_m, ride=w_out_rider)
    ob, lse_b = _attn_fwd("dil_fwd", qb, kb, None, vbt, bias_d, _steps(nq_d, span_d, False, False), blk_d)

    dz, doa, dob, dga, dgb, dst_a, dst_b, g_out, small1 = _out_ln(
        oa, ob, ga, gb, x, tgt, w_out.reshape(D_MODEL, D_MODEL), ln_g.reshape(1, -1), ln_b.reshape(1, -1), bt)

    dq_m, dkn, dkpe, dv, g_out_recv = _attn_bwd(
        "mla_bwd", qcat, kn, kpe, v, knt, kpet, bias_m, doa, lse_a, dst_a, _steps(nq_m, nq_m, True, True), blk_m,
        ride=g_out_rider(g_out.reshape(N_DEV, D_MODEL // N_DEV, D_MODEL)))
    dqb, dkb, dvb = _attn_bwd(
        "dil_bwd", qb, kb, None, vb, kbt, None, bias_d, dob, lse_b, dst_b, _steps(nq_d, span_d, True, False), blk_d)

    dh, g_uq_r, g_ukv_r, small2 = _bwd_mid(
        dq_m, dkn, dv, dkpe, dqb, dkb, dvb, dga, dgb, cq, ckv, qn, kvn, w_uq_r, w_ukv_r, qg, kvg, tabs, bt)
    grad_x = _grad_x(dz, dh, w_in_r, bt)
    g_in = _grad_w_in(x, dh, bt)
    g_uq, g_ukv = _restore_grads(g_uq_r, g_ukv_r)

    return small1[2], grad_x, g_in, g_uq, g_ukv, g_out_recv, small2[0], small2[1, :KV_RANK], small1[0], small1[1]


MESH_ID = pl.DeviceIdType.MESH
SHARD_SHAPES = ((D_MODEL, IN_WIDTH // N_DEV), (Q_RANK, 768 // N_DEV), (KV_RANK, 1024 // N_DEV), (D_MODEL // N_DEV, D_MODEL))
ADAM_ROWS = (32, 128, 128, 16)


def _me():
    x, y, c = lax.axis_index("x"), lax.axis_index("y"), lax.axis_index("c")
    return x, y, c, 4 * x + 2 * y + c


def _peer(k):
    x, y, c, _ = _me()
    px = 1 - x if (k >> 2) & 1 else x
    py = 1 - y if (k >> 1) & 1 else y
    pc = 1 - c if k & 1 else c
    return (px, py, pc), 4 * px + 2 * py + pc


def _all_gather_weights(shards):
    n = len(shards)
    shard = IN_WIDTH // N_DEV
    k_lo = IN_SPLITS[0] + IN_SPLITS[1]
    k_hi = k_lo + MLA_ROPE

    def body(*refs):
        ins = refs[:n]
        win_ref, wuq_ref, wukv_ref = refs[n:2 * n]
        bufs = refs[2 * n:3 * n]
        send_sems, recv_sems = refs[3 * n:]
        x, y, c, me = _me()
        here, sibling = (x, y, c), (x, y, 1 - c)
        chips = [(1 - x, y), (x, 1 - y), (1 - x, 1 - y)]
        for t in range(n):
            bufs[t][me] = ins[t][...].astype(BF16)

        def copy(t, k, px, py, pc, to):
            blk = bufs[t].at[4 * px + 2 * py + pc]
            return pltpu.make_async_remote_copy(
                src_ref=blk, dst_ref=blk, send_sem=send_sems.at[t, k], recv_sem=recv_sems.at[t, k],
                device_id=to, device_id_type=MESH_ID)

        first = []
        for t in range(n):
            first.append(copy(t, 0, x, y, c, sibling))
            for j, (px, py) in enumerate(chips):
                first.append(copy(t, 1 + j, x, y, c, (px, py, c)))
        for cp in first:
            cp.start()
        passed = []
        for j, (px, py) in enumerate(chips):
            for t in range(n):
                copy(t, 1 + j, px, py, c, here).wait_recv()
                cp = copy(t, 4 + j, px, py, c, sibling)
                cp.start()
                passed.append(cp)
        for t in range(n):
            copy(t, 0, x, y, 1 - c, here).wait_recv()
        for j, (px, py) in enumerate(chips):
            for t in range(n):
                copy(t, 4 + j, px, py, 1 - c, here).wait_recv()
        for cp in first + passed:
            cp.wait_send()

        a_in, a_uq, a_ukv = bufs
        for d in range(N_DEV):
            lo, hi = shard * d, shard * (d + 1)
            if lo < k_lo:
                win_ref[:, lo:min(hi, k_lo)] = a_in[d, :, 0:min(hi, k_lo) - lo]
            if lo < k_hi and hi > k_lo:
                kr = a_in[d, :, k_lo - lo:k_hi - lo]
                for rep in range(4):
                    win_ref[:, C_KR + MLA_ROPE * rep:C_KR + MLA_ROPE * (rep + 1)] = kr
            if hi > k_hi:
                src = max(lo, k_hi)
                win_ref[:, src + C_GA - k_hi:hi + C_GA - k_hi] = a_in[d, :, src - lo:hi - lo]
        for h in range(HEADS):
            wuq_ref[:, MLA_NOPE * h:MLA_NOPE * (h + 1)] = a_uq[h, :, :MLA_NOPE]
            wuq_ref[:, HW + MLA_ROPE * h:HW + MLA_ROPE * (h + 1)] = a_uq[h, :, MLA_NOPE:]
            wukv_ref[:, MLA_NOPE * h:MLA_NOPE * (h + 1)] = a_ukv[h, :, :MLA_NOPE]
            wukv_ref[:, HW + MLA_V * h:HW + MLA_V * (h + 1)] = a_ukv[h, :, MLA_NOPE:]

    vmem = pl.BlockSpec(memory_space=pltpu.VMEM)
    return _pcall(
        body, name="gather_weights",
        in_specs=[vmem] * n, out_specs=[vmem] * n,
        out_shape=[jax.ShapeDtypeStruct((D_MODEL, C_END), BF16), jax.ShapeDtypeStruct((Q_RANK, QW), BF16),
                   jax.ShapeDtypeStruct((KV_RANK, 2 * HW), BF16)],
        scratch_shapes=[pltpu.VMEM((N_DEV,) + s, BF16) for s in SHARD_SHAPES[:n]]
        + [pltpu.SemaphoreType.DMA((n, N_DEV - 1)), pltpu.SemaphoreType.DMA((n, N_DEV - 1))],
        compiler_params=_cparams(),
    )(*shards)


def _gather_w_out_rider(w_out):
    def copies(full_ref, stage, send_sems, recv_sems):
        me = _me()[3]
        out = []
        for k in range(1, N_DEV):
            peer, pidx = _peer(k)
            send = pltpu.make_async_remote_copy(
                src_ref=stage, dst_ref=full_ref.at[me], send_sem=send_sems.at[k - 1], recv_sem=recv_sems.at[k - 1],
                device_id=peer, device_id_type=MESH_ID)
            recv = pltpu.make_async_remote_copy(
                src_ref=stage, dst_ref=full_ref.at[pidx], send_sem=send_sems.at[k - 1], recv_sem=recv_sems.at[k - 1],
                device_id=peer, device_id_type=MESH_ID)
            out.append((send, recv))
        return out

    def start(ins, outs, scr):
        stage, send_sems, recv_sems, own_sem = scr
        stage[...] = ins[0][...].astype(BF16)
        pltpu.make_async_copy(stage, outs[0].at[_me()[3]], own_sem).start()
        for send, _ in copies(outs[0], stage, send_sems, recv_sems):
            send.start()

    def finish(ins, outs, scr):
        stage, send_sems, recv_sems, own_sem = scr
        pltpu.make_async_copy(stage, outs[0].at[_me()[3]], own_sem).wait()
        pairs = copies(outs[0], stage, send_sems, recv_sems)
        for _, recv in pairs:
            recv.wait_recv()
        for send, _ in pairs:
            send.wait_send()

    shape = SHARD_SHAPES[3]
    return Rider(
        args=[w_out], in_specs=[pl.BlockSpec(shape, lambda t, *_: (0, 0))],
        out_shape=[jax.ShapeDtypeStruct((N_DEV,) + shape, BF16)], out_specs=[pl.BlockSpec(memory_space=pl.ANY)],
        scratch=[pltpu.VMEM(shape, BF16), pltpu.SemaphoreType.DMA((N_DEV - 1,)), pltpu.SemaphoreType.DMA((N_DEV - 1,)),
                 pltpu.SemaphoreType.DMA],
        start=start, finish=finish)


def _scatter_g_out_rider(blocks):
    def copies(src_ref, dst_ref, send_sems, recv_sems):
        out = []
        for k in range(1, N_DEV):
            peer, pidx = _peer(k)
            out.append(pltpu.make_async_remote_copy(
                src_ref=src_ref.at[pidx], dst_ref=dst_ref.at[k], send_sem=send_sems.at[k - 1],
                recv_sem=recv_sems.at[k - 1], device_id=peer, device_id_type=MESH_ID))
        return out

    def start(ins, outs, scr):
        send_sems, recv_sems, own_sem = scr
        pltpu.make_async_copy(ins[0].at[_me()[3]], outs[0].at[0], own_sem).start()
        for cp in copies(ins[0], outs[0], send_sems, recv_sems):
            cp.start()

    def finish(ins, outs, scr):
        send_sems, recv_sems, own_sem = scr
        pltpu.make_async_copy(ins[0].at[_me()[3]], outs[0].at[0], own_sem).wait()
        for cp in copies(ins[0], outs[0], send_sems, recv_sems):
            cp.wait()

    hbm = pl.BlockSpec(memory_space=pl.ANY)
    return Rider(
        args=[blocks], in_specs=[hbm], out_shape=[jax.ShapeDtypeStruct(blocks.shape, blocks.dtype)], out_specs=[hbm],
        scratch=[pltpu.SemaphoreType.DMA((N_DEV - 1,)), pltpu.SemaphoreType.DMA((N_DEV - 1,)), pltpu.SemaphoreType.DMA],
        start=start, finish=finish)


def _adamw(w, g, m, v):
    m = ADAM_B1 * m + (1.0 - ADAM_B1) * g
    v = ADAM_B2 * v + (1.0 - ADAM_B2) * jnp.square(g)
    m_hat = m / (1.0 - ADAM_B1 ** ADAM_STEP)
    v_hat = v / (1.0 - ADAM_B2 ** ADAM_STEP)
    delta = -ADAM_LR * (m_hat / (jnp.sqrt(v_hat) + ADAM_EPS) + ADAM_WD * w)
    return delta, m, v


def _reduce_grads(grads3, arrived, small_part):
    n = len(grads3)

    def body(*refs):
        g3, arr_ref, sp_ref = refs[0:n], refs[n], refs[n + 1]
        gsum, gsum_out, ssum = refs[n + 2:2 * n + 2], refs[2 * n + 2], refs[2 * n + 3]
        scr = refs[2 * n + 4:]
        own, sib, part, ici = scr[0:n], scr[n:2 * n], scr[2 * n:3 * n], scr[3 * n:4 * n]
        rsmall = scr[4 * n]
        loc_sems, d2d_send, d2d_recv, ici_send, ici_recv, sm_send, sm_recv = scr[4 * n + 1:]
        x, y, c, me = _me()
        sibling = (x, y, 1 - c)
        chips = [(1 - x, y), (x, 1 - y), (1 - x, 1 - y)]
        my_chip = 2 * x + y

        rsmall[0] = sp_ref[...]
        small = []
        for k in range(1, N_DEV):
            cp = pltpu.make_async_remote_copy(
                src_ref=rsmall.at[0], dst_ref=rsmall.at[k], send_sem=sm_send.at[k - 1], recv_sem=sm_recv.at[k - 1],
                device_id=_peer(k)[0], device_id_type=MESH_ID)
            cp.start()
            small.append(cp)

        local, to_sib = [], []
        for t in range(n):
            for q in range(4):
                cp = pltpu.make_async_copy(g3[t].at[2 * q + c], own[t].at[q], loc_sems.at[t, q])
                cp.start()
                local.append(cp)
                cp = pltpu.make_async_remote_copy(
                    src_ref=g3[t].at[2 * q + 1 - c], dst_ref=sib[t].at[q], send_sem=d2d_send.at[t, q],
                    recv_sem=d2d_recv.at[t, q], device_id=sibling, device_id_type=MESH_ID)
                cp.start()
                to_sib.append(cp)

        def chunks(t, fn):
            rows = ADAM_ROWS[t]

            def step(i, carry):
                fn(pl.ds(pl.multiple_of(i * rows, rows), rows))
                return carry

            lax.fori_loop(0, SHARD_SHAPES[t][0] // rows, step, 0)

        to_chips = []
        for t in range(n):
            for q in range(4):
                local[4 * t + q].wait()
                to_sib[4 * t + q].wait_recv()

            def pair_sums(r, t=t):
                for q in range(4):
                    part[t][q, r, :] = (own[t][q, r, :].astype(F32) + sib[t][q, r, :].astype(F32)).astype(BF16)
                gsum[t][r, :] = own[t][my_chip, r, :].astype(F32) + sib[t][my_chip, r, :].astype(F32)

            chunks(t, pair_sums)
            for j, (px, py) in enumerate(chips):
                cp = pltpu.make_async_remote_copy(
                    src_ref=part[t].at[2 * px + py], dst_ref=ici[t].at[j], send_sem=ici_send.at[t, j],
                    recv_sem=ici_recv.at[t, j], device_id=(px, py, c), device_id_type=MESH_ID)
                cp.start()
                to_chips.append(cp)

        def add_arrived(r):
            g = arr_ref[0, r, :].astype(F32)
            for k in range(1, N_DEV):
                g = g + arr_ref[k, r, :].astype(F32)
            gsum_out[r, :] = g

        chunks(3, add_arrived)

        for t in range(n):
            for j in range(3):
                to_chips[3 * t + j].wait_recv()

            def add_chips(r, t=t):
                g = gsum[t][r, :]
                for j in range(3):
                    g = g + ici[t][j, r, :].astype(F32)
                gsum[t][r, :] = g

            chunks(t, add_chips)

        for cp in small:
            cp.wait_recv()
        tot = rsmall[me]
        for d in range(1, N_DEV):
            tot = tot + rsmall[jnp.bitwise_xor(me, d)]
        ssum[...] = tot
        for cp in small + to_sib + to_chips:
            cp.wait_send()

    vmem = pl.BlockSpec(memory_space=pltpu.VMEM)
    hbm = pl.BlockSpec(memory_space=pl.ANY)
    dma = pltpu.SemaphoreType.DMA
    return _pcall(
        body, name="reduce_grads",
        in_specs=[hbm] * n + [vmem, vmem], out_specs=[vmem] * (n + 2),
        out_shape=[jax.ShapeDtypeStruct(s, F32) for s in SHARD_SHAPES] + [jax.ShapeDtypeStruct((8, D_MODEL), F32)],
        scratch_shapes=[pltpu.VMEM((slots,) + s, BF16) for slots in (4, 4, 4, 3) for s in SHARD_SHAPES[:n]]
        + [pltpu.VMEM((N_DEV, 8, D_MODEL), F32), dma((n, 4)), dma((n, 4)), dma((n, 4)), dma((n, 3)), dma((n, 3)),
           dma((N_DEV - 1,)), dma((N_DEV - 1,))],
        compiler_params=_cparams(),
    )(*grads3, arrived, small_part)


def _adamw_update(grads, small_grad, wmv, small_wmv):
    def body(*refs):
        g_refs, sg_ref = refs[0:4], refs[4]
        wmv_refs = [refs[5 + 3 * t:8 + 3 * t] for t in range(4)]
        swmv_ref = refs[17]
        out_refs = [refs[18 + 4 * t:22 + 4 * t] for t in range(4)]
        sout_ref = refs[34]
        tot = sg_ref[...]
        delta, m, v = _adamw(swmv_ref[0], tot, swmv_ref[1], swmv_ref[2])
        sout_ref[0], sout_ref[1], sout_ref[2], sout_ref[3] = tot, delta, m, v
        for t in range(4):
            rows = ADAM_ROWS[t]
            w_ref, m_ref, v_ref = wmv_refs[t]
            g_out, d_out, m_out, v_out = out_refs[t]

            def step(i, carry, g_ref=g_refs[t], rows=rows, w_ref=w_ref, m_ref=m_ref, v_ref=v_ref,
                     g_out=g_out, d_out=d_out, m_out=m_out, v_out=v_out):
                r = pl.ds(pl.multiple_of(i * rows, rows), rows)
                g = g_ref[r, :]
                delta, m, v = _adamw(w_ref[r, :], g, m_ref[r, :], v_ref[r, :])
                g_out[r, :], d_out[r, :], m_out[r, :], v_out[r, :] = g, delta, m, v
                return carry

            lax.fori_loop(0, SHARD_SHAPES[t][0] // rows, step, 0)

    vmem = pl.BlockSpec(memory_space=pltpu.VMEM)
    flat_wmv = [a for trio in wmv for a in trio]
    return _pcall(
        body, name="adamw",
        in_specs=[vmem] * 18, out_specs=[vmem] * 17,
        out_shape=[jax.ShapeDtypeStruct(s, F32) for s in SHARD_SHAPES for _ in range(4)]
        + [jax.ShapeDtypeStruct((4, 8, D_MODEL), F32)],
        compiler_params=_cparams(),
    )(*grads, small_grad, *flat_wmv, small_wmv)


def _small_rows(ln_g, ln_b, q_norm_g, kv_norm_g, extra=None):
    pad = lambda a: jnp.pad(a, (0, D_MODEL - a.shape[0]))
    rows = [ln_g, ln_b, pad(q_norm_g), pad(kv_norm_g)] + ([] if extra is None else [extra])
    return jnp.pad(jnp.stack(rows), ((0, 8 - len(rows)), (0, 0)))


def kernel(x, w_in, q_norm_g, kv_norm_g, w_uq, w_ukv, w_out, ln_g, ln_b, loss_target, m_w_in, m_q_norm_g, m_kv_norm_g, m_w_uq, m_w_ukv, m_w_out, m_ln_g, m_ln_b, v_w_in, v_q_norm_g, v_kv_norm_g, v_w_uq, v_w_ukv, v_w_out, v_ln_g, v_ln_b):
    w_in_r, w_uq_r, w_ukv_r = _all_gather_weights([w_in, w_uq, w_ukv])
    sq_err, grad_x, g_in, g_uq, g_ukv, g_out_arrived, g_qg, g_kvg, g_lng, g_lnb = _local_step(
        x[0], loss_target[0], w_in_r, w_uq_r, w_ukv_r, _gather_w_out_rider(w_out), _scatter_g_out_rider,
        q_norm_g, kv_norm_g, ln_g, ln_b)

    grads3 = [g_in] + [g.astype(BF16) for g in (
        g_uq.reshape(Q_RANK, N_DEV, -1).transpose(1, 0, 2), g_ukv.reshape(KV_RANK, N_DEV, -1).transpose(1, 0, 2))]
    small_part = _small_rows(g_lng, g_lnb, g_qg, g_kvg, sq_err)
    small_wmv = jnp.stack([_small_rows(ln_g, ln_b, q_norm_g, kv_norm_g),
                           _small_rows(m_ln_g, m_ln_b, m_q_norm_g, m_kv_norm_g),
                           _small_rows(v_ln_g, v_ln_b, v_q_norm_g, v_kv_norm_g)])
    wmv = [(w_in, m_w_in, v_w_in), (w_uq, m_w_uq, v_w_uq), (w_ukv, m_w_ukv, v_w_ukv), (w_out, m_w_out, v_w_out)]
    sums = _reduce_grads(grads3, g_out_arrived, small_part)
    res = _adamw_update(sums[:4], sums[4], wmv, small_wmv)
    big = [res[4 * t:4 * t + 4] for t in range(4)]
    small = res[16]
    loss = (0.5 / D_MODEL) * jnp.sum(small[0, 4])

    def group(kind):
        s = small[kind]
        return (big[0][kind], s[2, :Q_RANK], s[3, :KV_RANK], big[1][kind], big[2][kind], big[3][kind], s[0], s[1])

    return (loss, grad_x[None], *group(0), *group(1), *group(2), *group(3))
```

```python
import functools
from typing import Callable, NamedTuple

import numpy as np
import jax
import jax.numpy as jnp
from jax import lax
from jax.experimental import pallas as pl
from jax.experimental.pallas import tpu as pltpu

F32 = jnp.float32
BF16 = jnp.bfloat16

D_MODEL = 1024
ROPE_THETA = 500000.0
NEG = -1e30
RMS_EPS = 1e-6
LN_EPS = 1e-5
HEADS = 8
MLA_NOPE = 64
MLA_ROPE = 32
MLA_V = 64
Q_RANK = 384
KV_RANK = 256
DIL_HEAD = 64
DIL_ROT = 16
DIL_CONFIGS = ((128, 1), (512, 4), (2048, 16))
HW = HEADS * 64
QW = HW + HEADS * MLA_ROPE
IN_SPLITS = (Q_RANK, KV_RANK, MLA_ROPE, HW, HW, HW, HW, HW)
IN_WIDTH = sum(IN_SPLITS)
ALPHA = 2.0 ** 0.25
MLA_SCALE = (MLA_NOPE + MLA_ROPE) ** -0.5
DIL_SCALE = DIL_HEAD ** -0.5
LOG2E = 1.4426950408889634
LN2 = 0.6931471805599453

ADAM_LR = 0.001
ADAM_B1 = 0.9
ADAM_B2 = 0.999
ADAM_EPS = 1e-08
ADAM_WD = 0.01
ADAM_STEP = 10

N_DEV = 8
LANES = 128
VMEM_LIMIT = 56 * 1024 * 1024
BLOCK_TOKENS = 512
BLOCK_MLA = 512
BLOCK_DIL = 512

C_CQ, C_CKV, C_KR, C_GA, C_QB, C_KB, C_VB, C_GB, C_END = 0, 384, 640, 768, 1280, 1792, 2304, 2816, 3328

NT = (((1,), (1,)), ((), ()))
TN = (((0,), (0,)), ((), ()))


def _pcall(body, **kw):
    return pl.pallas_call(body, **kw)


def _cparams(**kw):
    return pltpu.CompilerParams(vmem_limit_bytes=VMEM_LIMIT, **kw)


def _rope_tables(seq):
    def tabs(dim, period):
        half = dim // 2
        inv = np.float32(ROPE_THETA) ** (-np.arange(0, dim, 2, dtype=np.float32) / np.float32(dim))
        ang = np.arange(seq, dtype=np.float32)[:, None] * inv.astype(np.float32)[None, :]
        cos, sin = np.cos(ang).astype(np.float32), np.sin(ang).astype(np.float32)
        j = np.arange(LANES) % period
        f = j % half
        c = np.where(j < dim, cos[:, f], np.float32(1.0))
        s1 = np.where(j < half, -sin[:, f], np.float32(0.0))
        s2 = np.where((j >= half) & (j < dim), sin[:, f], np.float32(0.0))
        return [c, s1, s2]
    return np.stack(tabs(MLA_ROPE, MLA_ROPE) + tabs(DIL_ROT, DIL_HEAD)).astype(np.float32)


def _rope(t, c, s1, s2, half):
    return t * c + pltpu.roll(t, LANES - half, 1) * s1 + pltpu.roll(t, half, 1) * s2


def _rope_t(d, c, s1, s2, half):
    return d * c + pltpu.roll(d * s1, half, 1) + pltpu.roll(d * s2, LANES - half, 1)


def _rope_wide(fn, t, c, s1, s2, half):
    return jnp.concatenate(
        [fn(t[:, i:i + LANES], c, s1, s2, half) for i in range(0, t.shape[1], LANES)], axis=1)


def _mla_bias_t(blk):
    a = np.arange(blk)
    causal = np.where(a[:, None] <= a[None, :], 0.0, NEG)
    return np.stack([np.zeros((blk, blk)), causal]).astype(np.float32)


def _dil_bias_t(blk):
    span = DIL_CONFIGS[-1][0] // blk
    a = np.arange(blk)
    out = []
    for off in range(span + 1):
        delta = blk * off + a[None, :] - a[:, None]
        mult = np.zeros((blk, blk))
        for window, dil in DIL_CONFIGS:
            mult += (delta >= 0) & (delta % dil == 0) & (delta <= window)
        out.append(np.where(mult > 0, np.log2(np.maximum(mult, 1.0)), NEG))
    return np.stack(out).astype(np.float32)


def _steps(nq, span, by_key, diag_only_bias):
    rows = []
    if by_key:
        for ki in range(nq):
            hi = min(nq - 1, ki + span)
            for qi in range(ki, hi + 1):
                rows.append((qi, ki, int(qi == ki), int(qi == hi)))
    else:
        for qi in range(nq):
            lo = max(0, qi - span)
            for ki in range(lo, qi + 1):
                rows.append((qi, ki, int(ki == lo), int(ki == qi)))
    arr = np.array(rows, dtype=np.int32)
    off = arr[:, 0] - arr[:, 1]
    bias_idx = (off == 0).astype(np.int32) if diag_only_bias else off.astype(np.int32)
    return [jnp.asarray(v) for v in (arr[:, 0], arr[:, 1], bias_idx, arr[:, 2], arr[:, 3])]


def _fwd_proj(x, w_in_r, w_uq_r, w_ukv_r, qg, kvg, tabs, bt):
    seq = x.shape[0]

    def body(x_ref, win_ref, wuq_ref, wukv_ref, qg_ref, kvg_ref, tab_ref,
             cq_ref, ckv_ref, qn_ref, kvn_ref, qcat_ref, kn_ref, kpe_ref, v_ref,
             ga_ref, gb_ref, qb_ref, kb_ref, vb_ref, knt_ref, kpet_ref, vt_ref, kbt_ref, vbt_ref):
        xb = x_ref[...].astype(BF16)

        def proj(lo, hi):
            return jnp.dot(xb, win_ref[:, lo:hi], preferred_element_type=F32)

        m_tabs = (tab_ref[0], tab_ref[1], tab_ref[2])
        d_tabs = (tab_ref[3], tab_ref[4], tab_ref[5])

        cq = proj(C_CQ, C_CKV)
        cq_ref[...] = cq
        qn = (cq * lax.rsqrt(jnp.mean(cq * cq, axis=1, keepdims=True) + RMS_EPS) * qg_ref[...]).astype(BF16)
        qn_ref[...] = qn
        q = jnp.dot(qn, wuq_ref[...], preferred_element_type=F32)
        qcat_ref[:, :HW] = (q[:, :HW] * (MLA_SCALE * LOG2E)).astype(BF16)
        qcat_ref[:, HW:] = (_rope_wide(_rope, q[:, HW:], *m_tabs, MLA_ROPE // 2) * (MLA_SCALE * LOG2E)).astype(BF16)

        ckv = proj(C_CKV, C_KR)
        ckv_ref[...] = ckv
        kvn = (ckv * lax.rsqrt(jnp.mean(ckv * ckv, axis=1, keepdims=True) + RMS_EPS) * kvg_ref[...]).astype(BF16)
        kvn_ref[...] = kvn
        kv = jnp.dot(kvn, wukv_ref[...], preferred_element_type=F32)
        kn_ref[...] = kv[:, :HW].astype(BF16)
        v_ref[...] = kv[:, HW:].astype(BF16)
        knt_ref[...] = kv[:, :HW].T.astype(BF16)
        vt_ref[...] = kv[:, HW:].T.astype(BF16)

        kpe = _rope(proj(C_KR, C_GA), *m_tabs, MLA_ROPE // 2)
        kpe_ref[...] = kpe.astype(BF16)
        kpet_ref[...] = kpe.T[:MLA_ROPE, :].astype(BF16)
        ga_ref[...] = proj(C_GA, C_QB)
        qb_ref[...] = (_rope_wide(_rope, proj(C_QB, C_KB), *d_tabs, DIL_ROT // 2) * (DIL_SCALE * LOG2E)).astype(BF16)
        kb = _rope_wide(_rope, proj(C_KB, C_VB), *d_tabs, DIL_ROT // 2)
        kb_ref[...] = kb.astype(BF16)
        kbt_ref[...] = kb.T.astype(BF16)
        vb = proj(C_VB, C_GB)
        vb_ref[...] = vb.astype(BF16)
        vbt_ref[...] = vb.T.astype(BF16)
        gb_ref[...] = proj(C_GB, C_END)

    def tok(width):
        return pl.BlockSpec((bt, width), lambda i: (i, 0))

    def tok_t(height):
        return pl.BlockSpec((height, bt), lambda i: (0, i))

    def full(a):
        return pl.BlockSpec(a.shape, lambda i: (0,) * a.ndim)

    outs = [(Q_RANK, F32), (KV_RANK, F32), (Q_RANK, BF16), (KV_RANK, BF16), (QW, BF16), (HW, BF16),
            (LANES, BF16), (HW, BF16), (HW, F32), (HW, F32), (HW, BF16), (HW, BF16), (HW, BF16)]
    outs_t = [HW, MLA_ROPE, HW, HW, HW]
    return _pcall(
        body, name="fwd_proj", grid=(seq // bt,),
        in_specs=[tok(D_MODEL), full(w_in_r), full(w_uq_r), full(w_ukv_r), full(qg), full(kvg),
                  pl.BlockSpec((6, bt, LANES), lambda i: (0, i, 0))],
        out_specs=[tok(w) for w, _ in outs] + [tok_t(h) for h in outs_t],
        out_shape=[jax.ShapeDtypeStruct((seq, w), dt) for w, dt in outs]
        + [jax.ShapeDtypeStruct((h, seq), BF16) for h in outs_t],
        compiler_params=_cparams(dimension_semantics=("arbitrary",)),
    )(x, w_in_r, w_uq_r, w_ukv_r, qg, kvg, tabs)


def _head_masks(lane, h):
    e, g = h % 2, h % 4
    me = (lane >= 64 * e) & (lane < 64 * e + 64)
    mr = (lane >= 32 * g) & (lane < 32 * g + 32)
    return me, mr


def _masked(mask, a):
    return jnp.where(mask, a, jnp.zeros_like(a))


def _pair_operands(q_ref, k_ref, kpe_ref, lane, j):
    cols = slice(LANES * j, LANES * (j + 1))
    qc = q_ref[:, cols]
    kj = k_ref[:, cols]
    kes = []
    for h in (2 * j, 2 * j + 1):
        me, mr = _head_masks(lane, h)
        ke = _masked(me, kj)
        if kpe_ref is not None:
            ke = jnp.concatenate([ke, _masked(mr, kpe_ref[...])], axis=1)
        kes.append(ke)
    if kpe_ref is not None:
        qc = jnp.concatenate([qc, q_ref[:, HW + LANES * (j // 2):HW + LANES * (j // 2 + 1)]], axis=1)
    return qc, kes


class Rider(NamedTuple):
    args: list
    in_specs: list
    out_shape: list
    out_specs: list
    scratch: list
    start: Callable
    finish: Callable
    stages: tuple = ()


def _ride_along(body, ride, n_prefetch, n_in, n_out, n_scratch, n_steps):
    if ride is None:
        return body

    def wrapped(*refs):
        pre, rest = refs[:n_prefetch], refs[n_prefetch:]
        a = n_in
        b = a + len(ride.args)
        c = b + n_out
        d = c + len(ride.out_shape)
        e = d + n_scratch
        mine = (rest[a:b], rest[c:d], rest[e:])
        t = pl.program_id(0)
        pl.when(t == 0)(lambda: ride.start(*mine))
        for at, stage in ride.stages:
            pl.when(t == at)(functools.partial(stage, *mine))
        body(*pre, *rest[:a], *rest[b:c], *rest[d:e])
        pl.when(t == n_steps - 1)(lambda: ride.finish(*mine))

    return wrapped


def _attn_fwd(name, q, k, kpe, vt, bias_t, steps, blk, ride=None):
    seq = q.shape[0]
    mla = kpe is not None
    n_steps = int(steps[0].shape[0])

    def body(qi_r, ki_r, bi_r, fi_r, la_r, *refs):
        if mla:
            q_ref, k_ref, kpe_ref, vt_ref, b_ref, o_ref, lse_ref, m_sc, l_sc, acc_sc, st_sc = refs
        else:
            q_ref, k_ref, vt_ref, b_ref, o_ref, lse_ref, m_sc, l_sc, acc_sc, st_sc = refs
        t = pl.program_id(0)

        @pl.when(fi_r[t] == 1)
        def _():
            m_sc[...] = jnp.full(m_sc.shape, NEG, F32)
            l_sc[...] = jnp.zeros(l_sc.shape, F32)
            acc_sc[...] = jnp.zeros(acc_sc.shape, F32)

        lane = lax.broadcasted_iota(jnp.int32, (1, LANES), 1)
        ones = jnp.ones((16, blk), BF16)

        def pair_scores(j, with_bias):
            qc, kes = _pair_operands(q_ref, k_ref, kpe_ref if mla else None, lane, j)
            st = lax.dot_general(jnp.concatenate(kes, axis=0), qc, NT, preferred_element_type=F32)
            maxes = []
            for e in range(2):
                se = st[e * blk:(e + 1) * blk]
                if with_bias:
                    se = se + b_ref[0]
                st_sc[j % 2, e * blk:(e + 1) * blk] = se
                maxes.append(jnp.max(se, axis=0, keepdims=True))
            return maxes

        def softmax_pv(h, col_max):
            st = st_sc[(h // 2) % 2, (h % 2) * blk:(h % 2 + 1) * blk]
            hrow = slice(h, h + 1)
            m_prev = m_sc[hrow, :]
            m_new = jnp.maximum(m_prev, col_max)
            alpha = jnp.exp2(m_prev - m_new)
            pt = jnp.exp2(st - m_new).astype(BF16)
            m_sc[hrow, :] = m_new
            rows = slice(64 * h, 64 * h + 64)
            res = jnp.dot(jnp.concatenate([vt_ref[rows, :], ones], axis=0), pt, preferred_element_type=F32)
            acc_sc[rows, :] = alpha * acc_sc[rows, :] + res[:64]
            l_sc[hrow, :] = alpha * l_sc[hrow, :] + res[64:65]

        def step(with_bias):
            maxes = pair_scores(0, with_bias)
            for j in range(HEADS // 2):
                cur = maxes
                if j + 1 < HEADS // 2:
                    maxes = pair_scores(j + 1, with_bias)
                softmax_pv(2 * j, cur[0])
                softmax_pv(2 * j + 1, cur[1])

        if mla:
            pl.when(bi_r[t] == 1)(lambda: step(True))
            pl.when(bi_r[t] == 0)(lambda: step(False))
        else:
            step(True)

        @pl.when(la_r[t] == 1)
        def _():
            for h in range(HEADS):
                rows = slice(64 * h, 64 * h + 64)
                acc_sc[rows, :] = acc_sc[rows, :] / l_sc[h:h + 1, :]
            o_ref[...] = acc_sc[...].T
            lse_ref[...] = m_sc[...] + jnp.log2(l_sc[...])

    qmap = lambda t, qi, ki, bi, fi, la: (qi[t], 0)
    kmap = lambda t, qi, ki, bi, fi, la: (ki[t], 0)
    in_specs = [pl.BlockSpec((blk, q.shape[1]), qmap), pl.BlockSpec((blk, HW), kmap)]
    args = [q, k]
    if mla:
        in_specs.append(pl.BlockSpec((blk, LANES), kmap))
        args.append(kpe)
    in_specs += [pl.BlockSpec((HW, blk), lambda t, qi, ki, bi, fi, la: (0, ki[t])),
                 pl.BlockSpec((1, blk, blk), lambda t, qi, ki, bi, fi, la: (bi[t], 0, 0))]
    args += [vt, bias_t]
    out_specs = [pl.BlockSpec((blk, HW), qmap), pl.BlockSpec((HEADS, blk), lambda t, qi, ki, bi, fi, la: (0, qi[t]))]
    out_shape = [jax.ShapeDtypeStruct((seq, HW), F32), jax.ShapeDtypeStruct((HEADS, seq), F32)]
    scratch = [pltpu.VMEM((HEADS, blk), F32), pltpu.VMEM((HEADS, blk), F32),
               pltpu.VMEM((HW, blk), F32), pltpu.VMEM((2, 2 * blk, blk), F32)]
    body = _ride_along(body, ride, 5, len(args), len(out_shape), len(scratch), n_steps)
    if ride is not None:
        args, in_specs = args + ride.args, in_specs + ride.in_specs
        out_specs, out_shape, scratch = out_specs + ride.out_specs, out_shape + ride.out_shape, scratch + ride.scratch
    return _pcall(
        body, name=name,
        grid_spec=pltpu.PrefetchScalarGridSpec(
            num_scalar_prefetch=5, grid=(n_steps,), in_specs=in_specs, out_specs=out_specs, scratch_shapes=scratch),
        out_shape=out_shape,
        compiler_params=_cparams(dimension_semantics=("arbitrary",)),
    )(*steps, *args)


def _attn_bwd(name, q, k, kpe, v, kt, kpet, bias_t, do, lse, dstat, steps, blk, ride=None):
    seq = q.shape[0]
    mla = kpe is not None
    qw = q.shape[1]
    n_steps = int(steps[0].shape[0])
    dk_dtype = BF16 if mla else F32

    def body(qi_r, ki_r, bi_r, fi_r, la_r, *refs):
        if mla:
            (q_ref, k_ref, kpe_ref, v_ref, kt_ref, kpet_ref, b_ref, do_ref, lse_ref, d_ref,
             dq_ref, dk_ref, dkpe_ref, dv_ref, dk_sc, dkpe_sc, dv_sc, st_sc, dpt_sc) = refs
        else:
            (q_ref, k_ref, v_ref, kt_ref, b_ref, do_ref, lse_ref, d_ref,
             dq_ref, dk_ref, dv_ref, dk_sc, dv_sc, st_sc, dpt_sc) = refs
        t = pl.program_id(0)

        @pl.when(t == 0)
        def _():
            dq_ref[...] = jnp.zeros(dq_ref.shape, F32)

        @pl.when(fi_r[t] == 1)
        def _():
            dk_sc[...] = jnp.zeros(dk_sc.shape, F32)
            dv_sc[...] = jnp.zeros(dv_sc.shape, F32)
            if mla:
                dkpe_sc[...] = jnp.zeros(dkpe_sc.shape, F32)

        qi = qi_r[t]
        lane = lax.broadcasted_iota(jnp.int32, (1, LANES), 1)

        def pair_matmuls(j):
            cols = slice(LANES * j, LANES * (j + 1))
            qc, kes = _pair_operands(q_ref, k_ref, kpe_ref if mla else None, lane, j)
            st_sc[j % 2] = lax.dot_general(jnp.concatenate(kes, axis=0), qc, NT, preferred_element_type=F32)
            vj = v_ref[:, cols]
            ves = [_masked(_head_masks(lane, h)[0], vj) for h in (2 * j, 2 * j + 1)]
            dpt_sc[j % 2] = lax.dot_general(
                jnp.concatenate(ves, axis=0), do_ref[:, cols], NT, preferred_element_type=F32)

        def pair_grads(j, with_bias):
            cols = slice(LANES * j, LANES * (j + 1))
            qj, doj = q_ref[:, cols], do_ref[:, cols]
            if mla:
                qr = q_ref[:, HW + LANES * (j // 2):HW + LANES * (j // 2 + 1)]
            pts, dsts, qms, doms = [], [], [], []
            for e in range(2):
                h = 2 * j + e
                me, mr = _head_masks(lane, h)
                st = st_sc[j % 2, e * blk:(e + 1) * blk]
                if with_bias:
                    st = st + b_ref[0]
                pt = jnp.exp2(st - lse_ref[h:h + 1, :])
                dst = (pt * (dpt_sc[j % 2, e * blk:(e + 1) * blk] - d_ref[h:h + 1, :])).astype(BF16)
                pts.append(pt.astype(BF16))
                dsts.append(dst)
                doms.append(_masked(me, doj))
                qm = _masked(me, qj)
                if mla:
                    qm = jnp.concatenate([qm, _masked(mr, qr)], axis=1)
                qms.append(qm)
                ktl = kt_ref[64 * h:64 * h + 64, :]
                if mla:
                    ktl = jnp.concatenate([ktl, kpet_ref[...]], axis=0)
                dqc = jnp.dot(ktl, dst, preferred_element_type=F32)
                dq_ref[qi, 64 * h:64 * h + 64, :] += dqc[:64]
                if mla:
                    dq_ref[qi, HW + MLA_ROPE * h:HW + MLA_ROPE * (h + 1), :] += dqc[64:]
            dv_sc[:, cols] += jnp.dot(
                jnp.concatenate(pts, axis=1), jnp.concatenate(doms, axis=0), preferred_element_type=F32)
            dkc = jnp.dot(jnp.concatenate(dsts, axis=1), jnp.concatenate(qms, axis=0), preferred_element_type=F32)
            dk_sc[:, cols] += dkc[:, :LANES]
            if mla:
                dkpe_sc[...] += dkc[:, LANES:]

        def step(with_bias):
            pair_matmuls(0)
            for j in range(HEADS // 2):
                if j + 1 < HEADS // 2:
                    pair_matmuls(j + 1)
                pair_grads(j, with_bias)

        if mla:
            pl.when(bi_r[t] == 1)(lambda: step(True))
            pl.when(bi_r[t] == 0)(lambda: step(False))
        else:
            step(True)

        @pl.when(la_r[t] == 1)
        def _():
            dk_ref[...] = (dk_sc[...] * LN2).astype(dk_ref.dtype)
            dv_ref[...] = dv_sc[...].astype(dv_ref.dtype)
            if mla:
                dkpe_ref[...] = dkpe_sc[...] * LN2

    qmap = lambda t, qi, ki, bi, fi, la: (qi[t], 0)
    kmap = lambda t, qi, ki, bi, fi, la: (ki[t], 0)
    qmap_t = lambda t, qi, ki, bi, fi, la: (0, qi[t])
    kmap_t = lambda t, qi, ki, bi, fi, la: (0, ki[t])
    in_specs = [pl.BlockSpec((blk, qw), qmap), pl.BlockSpec((blk, HW), kmap)]
    args = [q, k]
    if mla:
        in_specs.append(pl.BlockSpec((blk, LANES), kmap))
        args.append(kpe)
    in_specs += [pl.BlockSpec((blk, HW), kmap), pl.BlockSpec((HW, blk), kmap_t)]
    args += [v, kt]
    if mla:
        in_specs.append(pl.BlockSpec((MLA_ROPE, blk), kmap_t))
        args.append(kpet)
    in_specs += [pl.BlockSpec((1, blk, blk), lambda t, qi, ki, bi, fi, la: (bi[t], 0, 0)),
                 pl.BlockSpec((blk, HW), qmap), pl.BlockSpec((HEADS, blk), qmap_t), pl.BlockSpec((HEADS, blk), qmap_t)]
    args += [bias_t, do, lse, dstat]
    dq_shape = (seq // blk, qw, blk)
    out_specs = [pl.BlockSpec(dq_shape, lambda t, qi, ki, bi, fi, la: (0, 0, 0)), pl.BlockSpec((blk, HW), kmap)]
    out_shape = [jax.ShapeDtypeStruct(dq_shape, F32), jax.ShapeDtypeStruct((seq, HW), dk_dtype)]
    scratch = [pltpu.VMEM((blk, HW), F32)]
    if mla:
        out_specs.append(pl.BlockSpec((blk, LANES), kmap))
        out_shape.append(jax.ShapeDtypeStruct((seq, LANES), F32))
        scratch.append(pltpu.VMEM((blk, LANES), F32))
    out_specs.append(pl.BlockSpec((blk, HW), kmap))
    out_shape.append(jax.ShapeDtypeStruct((seq, HW), BF16))
    scratch.append(pltpu.VMEM((blk, HW), F32))
    scratch += [pltpu.VMEM((2, 2 * blk, blk), F32), pltpu.VMEM((2, 2 * blk, blk), F32)]
    body = _ride_along(body, ride, 5, len(args), len(out_shape), len(scratch), n_steps)
    if ride is not None:
        args, in_specs = args + ride.args, in_specs + ride.in_specs
        out_specs, out_shape, scratch = out_specs + ride.out_specs, out_shape + ride.out_shape, scratch + ride.scratch
    return _pcall(
        body, name=name,
        grid_spec=pltpu.PrefetchScalarGridSpec(
            num_scalar_prefetch=5, grid=(n_steps,), in_specs=in_specs, out_specs=out_specs,
            scratch_shapes=scratch),
        out_shape=out_shape,
        compiler_params=_cparams(dimension_semantics=("arbitrary",)),
    )(*steps, *args)


def _out_ln(oa, ob, ga, gb, x, tgt, w_out, ln_g, ln_b, bt):
    seq = x.shape[0]

    def body(oa_ref, ob_ref, ga_ref, gb_ref, x_ref, tgt_ref, w_ref, g_ref, b_ref,
             dz_ref, doa_ref, dob_ref, dga_ref, dgb_ref, da_ref, db_ref, gwb_ref, small_ref, gw_ref):
        i = pl.program_id(0)

        @pl.when(i == 0)
        def _():
            gw_ref[...] = jnp.zeros(gw_ref.shape, F32)
            small_ref[...] = jnp.zeros(small_ref.shape, F32)

        def gate(g):
            sig = 0.5 * jnp.tanh(0.5 * g) + 0.5
            return g * sig, sig * (1.0 + g * (1.0 - sig))

        o_a, o_b = oa_ref[...], ob_ref[...]
        g_a, g_b = ga_ref[...], gb_ref[...]
        sa, dsa = gate(g_a)
        sb, dsb = gate(g_b)
        mix = jnp.concatenate([o_a * sa, o_b * sb], axis=1).astype(BF16)
        z = ALPHA * x_ref[...] + jnp.dot(mix, w_ref[...], preferred_element_type=F32)
        mu = jnp.mean(z, axis=1, keepdims=True)
        zc = z - mu
        rstd = lax.rsqrt(jnp.mean(zc * zc, axis=1, keepdims=True) + LN_EPS)
        xhat = zc * rstd
        gam = g_ref[...]
        diff = xhat * gam + b_ref[...] - tgt_ref[...]
        dy = diff * (1.0 / D_MODEL)
        small_ref[0:1, :] += jnp.sum(dy * xhat, axis=0, keepdims=True)
        small_ref[1:2, :] += jnp.sum(dy, axis=0, keepdims=True)
        small_ref[2:3, :] += jnp.sum(diff * diff, axis=0, keepdims=True)
        dxh = dy * gam
        dz = rstd * (dxh - jnp.mean(dxh, axis=1, keepdims=True) - xhat * jnp.mean(dxh * xhat, axis=1, keepdims=True))
        dz_ref[...] = dz
        dzb = dz.astype(BF16)
        gw_ref[...] += lax.dot_general(mix, dzb, TN, preferred_element_type=F32)

        @pl.when(i == seq // bt - 1)
        def _():
            gwb_ref[...] = gw_ref[...].astype(BF16)

        dmix = lax.dot_general(dzb, w_ref[...], NT, preferred_element_type=F32)
        doa, dob = dmix[:, :HW] * sa, dmix[:, HW:] * sb
        doa_ref[...] = doa.astype(BF16)
        dob_ref[...] = dob.astype(BF16)
        dga_ref[...] = (dmix[:, :HW] * o_a * dsa).astype(BF16)
        dgb_ref[...] = (dmix[:, HW:] * o_b * dsb).astype(BF16)
        head_of = (lax.broadcasted_iota(jnp.int32, (2 * HW, LANES), 0) % HW) // 64
        ind = (head_of == lax.broadcasted_iota(jnp.int32, (2 * HW, LANES), 1)).astype(BF16)

        def head_sums(prod):
            hi = prod.astype(BF16)
            lo = (prod - hi.astype(F32)).astype(BF16)
            sums = jnp.dot(jnp.concatenate([hi, lo], axis=1), ind, preferred_element_type=F32)
            return sums.T[:HEADS, :]

        da_ref[...] = head_sums(doa * o_a)
        db_ref[...] = head_sums(dob * o_b)

    def tok(width):
        return pl.BlockSpec((bt, width), lambda i: (i, 0))

    def full(shape):
        return pl.BlockSpec(shape, lambda i: (0,) * len(shape))

    stat = pl.BlockSpec((HEADS, bt), lambda i: (0, i))
    return _pcall(
        body, name="out_ln", grid=(seq // bt,),
        in_specs=[tok(HW), tok(HW), tok(HW), tok(HW), tok(D_MODEL), tok(D_MODEL),
                  full((D_MODEL, D_MODEL)), full((1, D_MODEL)), full((1, D_MODEL))],
        out_specs=[tok(D_MODEL), tok(HW), tok(HW), tok(HW), tok(HW), stat, stat,
                   full((D_MODEL, D_MODEL)), full((8, D_MODEL))],
        out_shape=[jax.ShapeDtypeStruct((seq, D_MODEL), F32)] + [jax.ShapeDtypeStruct((seq, HW), BF16)] * 4
        + [jax.ShapeDtypeStruct((HEADS, seq), F32)] * 2
        + [jax.ShapeDtypeStruct((D_MODEL, D_MODEL), BF16), jax.ShapeDtypeStruct((8, D_MODEL), F32)],
        scratch_shapes=[pltpu.VMEM((D_MODEL, D_MODEL), F32)],
        compiler_params=_cparams(dimension_semantics=("arbitrary",)),
    )(oa, ob, ga, gb, x, tgt, w_out, ln_g, ln_b)


def _bwd_mid(dq_m, dkn, dv, dkpe, dqb, dkb, dvb, dga, dgb, cq, ckv, qn, kvn, w_uq_r, w_ukv_r, qg, kvg, tabs, bt):
    seq = cq.shape[0]

    def body(dqm_ref, dkn_ref, dv_ref, dkpe_ref, dqb_ref, dkb_ref, dvb_ref, dga_ref, dgb_ref,
             cq_ref, ckv_ref, qn_ref, kvn_ref, wuq_ref, wukv_ref, qg_ref, kvg_ref, tab_ref,
             dh_ref, guq_ref, gukv_ref, small_ref):
        i = pl.program_id(0)

        @pl.when(i == 0)
        def _():
            guq_ref[...] = jnp.zeros(guq_ref.shape, F32)
            gukv_ref[...] = jnp.zeros(gukv_ref.shape, F32)
            small_ref[...] = jnp.zeros(small_ref.shape, F32)

        m_tabs = (tab_ref[0], tab_ref[1], tab_ref[2])
        d_tabs = (tab_ref[3], tab_ref[4], tab_ref[5])

        def rms_bwd(c, dn, gain):
            r = lax.rsqrt(jnp.mean(c * c, axis=1, keepdims=True) + RMS_EPS)
            u = dn * gain
            dc = r * u - c * (r * r * r) * jnp.mean(u * c, axis=1, keepdims=True)
            return dc, jnp.sum(dn * c * r, axis=0, keepdims=True)

        dqm = dqm_ref[0].T
        dq = jnp.concatenate(
            [dqm[:, :HW], _rope_wide(_rope_t, dqm[:, HW:], *m_tabs, MLA_ROPE // 2)], axis=1) * MLA_SCALE
        dq = dq.astype(BF16)
        guq_ref[...] += lax.dot_general(qn_ref[...], dq, TN, preferred_element_type=F32)
        dqn = lax.dot_general(dq, wuq_ref[...], NT, preferred_element_type=F32)
        dcq, gq = rms_bwd(cq_ref[...], dqn, qg_ref[...])
        small_ref[0:1, :] += gq

        dkv = jnp.concatenate([dkn_ref[...], dv_ref[...]], axis=1)
        gukv_ref[...] += lax.dot_general(kvn_ref[...], dkv, TN, preferred_element_type=F32)
        dkvn = lax.dot_general(dkv, wukv_ref[...], NT, preferred_element_type=F32)
        dckv, gkv = rms_bwd(ckv_ref[...], dkvn, kvg_ref[...])
        small_ref[1:2, :KV_RANK] += gkv

        dh_ref[:, C_CQ:C_CKV] = dcq.astype(BF16)
        dh_ref[:, C_CKV:C_KR] = dckv.astype(BF16)
        dh_ref[:, C_KR:C_GA] = _rope_t(dkpe_ref[...], *m_tabs, MLA_ROPE // 2).astype(BF16)
        dh_ref[:, C_GA:C_QB] = dga_ref[...]
        dh_ref[:, C_QB:C_KB] = (_rope_wide(_rope_t, dqb_ref[0].T, *d_tabs, DIL_ROT // 2) * DIL_SCALE).astype(BF16)
        dh_ref[:, C_KB:C_VB] = _rope_wide(_rope_t, dkb_ref[...], *d_tabs, DIL_ROT // 2).astype(BF16)
        dh_ref[:, C_VB:C_GB] = dvb_ref[...]
        dh_ref[:, C_GB:C_END] = dgb_ref[...]

    def tok(width):
        return pl.BlockSpec((bt, width), lambda i: (i, 0))

    def tok_t(a):
        per = a.shape[2] // bt
        return pl.BlockSpec((1, a.shape[1], bt), lambda i: (i // per, 0, i % per))

    def full(shape):
        return pl.BlockSpec(shape, lambda i: (0,) * len(shape))

    return _pcall(
        body, name="bwd_mid", grid=(seq // bt,),
        in_specs=[tok_t(dq_m), tok(HW), tok(HW), tok(LANES), tok_t(dqb), tok(HW), tok(HW), tok(HW), tok(HW),
                  tok(Q_RANK), tok(KV_RANK), tok(Q_RANK), tok(KV_RANK),
                  full(w_uq_r.shape), full(w_ukv_r.shape), full((1, Q_RANK)), full((1, KV_RANK)),
                  pl.BlockSpec((6, bt, LANES), lambda i: (0, i, 0))],
        out_specs=[tok(C_END), full(w_uq_r.shape), full(w_ukv_r.shape), full((8, Q_RANK))],
        out_shape=[jax.ShapeDtypeStruct((seq, C_END), BF16), jax.ShapeDtypeStruct(w_uq_r.shape, F32),
                   jax.ShapeDtypeStruct(w_ukv_r.shape, F32), jax.ShapeDtypeStruct((8, Q_RANK), F32)],
        compiler_params=_cparams(dimension_semantics=("arbitrary",)),
    )(dq_m, dkn, dv, dkpe, dqb, dkb, dvb, dga, dgb, cq, ckv, qn, kvn, w_uq_r, w_ukv_r, qg, kvg, tabs)


def _grad_x(dz, dh, w_in_r, bt, ride=None):
    seq = dz.shape[0]
    n_steps = seq // bt

    def body(dz_ref, dh_ref, w_ref, gx_ref):
        gx_ref[...] = ALPHA * dz_ref[...] + lax.dot_general(
            dh_ref[...], w_ref[...], NT, preferred_element_type=F32)

    args = [dz, dh, w_in_r]
    in_specs = [pl.BlockSpec((bt, D_MODEL), lambda i: (i, 0)), pl.BlockSpec((bt, C_END), lambda i: (i, 0)),
                pl.BlockSpec(w_in_r.shape, lambda i: (0, 0))]
    out_specs = [pl.BlockSpec((bt, D_MODEL), lambda i: (i, 0))]
    out_shape = [jax.ShapeDtypeStruct((seq, D_MODEL), F32)]
    scratch = []
    body = _ride_along(body, ride, 0, len(args), len(out_shape), 0, n_steps)
    if ride is not None:
        args, in_specs = args + ride.args, in_specs + ride.in_specs
        out_specs, out_shape, scratch = out_specs + ride.out_specs, out_shape + ride.out_shape, ride.scratch
    return _pcall(
        body, name="grad_x", grid=(n_steps,),
        in_specs=in_specs, out_specs=out_specs, out_shape=out_shape, scratch_shapes=scratch,
        compiler_params=_cparams(dimension_semantics=("arbitrary",)),
    )(*args)


def _grad_w_in(x, dh, bt):
    seq = x.shape[0]
    shard = IN_WIDTH // N_DEV
    k_lo, k_hi = IN_SPLITS[0] + IN_SPLITS[1], IN_SPLITS[0] + IN_SPLITS[1] + MLA_ROPE

    def body(x_ref, dh_ref, out_ref, acc):
        i = pl.program_id(0)

        @pl.when(i == 0)
        def _():
            acc[...] = jnp.zeros(acc.shape, F32)

        acc[...] += lax.dot_general(x_ref[...].astype(BF16), dh_ref[...], TN, preferred_element_type=F32)

        @pl.when(i == seq // bt - 1)
        def _():
            kr = acc[:, C_KR:C_GA]
            kr = kr + pltpu.roll(kr, 96, 1) + pltpu.roll(kr, 64, 1) + pltpu.roll(kr, 32, 1)
            for d in range(N_DEV):
                lo, hi = shard * d, shard * (d + 1)
                pieces = []
                if lo < k_lo:
                    pieces.append(acc[:, lo:min(hi, k_lo)])
                if lo < k_hi and hi > k_lo:
                    pieces.append(kr[:, max(lo, k_lo) - k_lo:min(hi, k_hi) - k_lo])
                if hi > k_hi:
                    shift = C_GA - k_hi
                    pieces.append(acc[:, max(lo, k_hi) + shift:hi + shift])
                blk = pieces[0] if len(pieces) == 1 else jnp.concatenate(pieces, axis=1)
                out_ref[d] = blk.astype(BF16)

    return _pcall(
        body, name="grad_w_in", grid=(seq // bt,),
        in_specs=[pl.BlockSpec((bt, D_MODEL), lambda i: (i, 0)), pl.BlockSpec((bt, C_END), lambda i: (i, 0))],
        out_specs=pl.BlockSpec((N_DEV, D_MODEL, shard), lambda i: (0, 0, 0)),
        out_shape=jax.ShapeDtypeStruct((N_DEV, D_MODEL, shard), BF16),
        scratch_shapes=[pltpu.VMEM((D_MODEL, C_END), F32)],
        compiler_params=_cparams(dimension_semantics=("arbitrary",)),
    )(x, dh)


def _restore_grads(g_uq_r, g_ukv_r):
    g_uq = jnp.concatenate(
        [g_uq_r[:, :HW].reshape(Q_RANK, HEADS, MLA_NOPE), g_uq_r[:, HW:].reshape(Q_RANK, HEADS, MLA_ROPE)],
        axis=2).reshape(Q_RANK, HEADS * (MLA_NOPE + MLA_ROPE))
    g_ukv = jnp.concatenate(
        [g_ukv_r[:, :HW].reshape(KV_RANK, HEADS, MLA_NOPE), g_ukv_r[:, HW:].reshape(KV_RANK, HEADS, MLA_V)],
        axis=2).reshape(KV_RANK, HEADS * (MLA_NOPE + MLA_V))
    return g_uq, g_ukv


def _local_step(x, tgt, w_in_r, w_uq_r, w_ukv_r, w_out_rider, g_out_rider, reduce_rider, q_norm_g, kv_norm_g,
                ln_g, ln_b, bt=BLOCK_TOKENS, blk_m=BLOCK_MLA, blk_d=BLOCK_DIL):
    seq = x.shape[0]
    tabs = jnp.asarray(_rope_tables(seq))
    qg, kvg = q_norm_g.reshape(1, -1), kv_norm_g.reshape(1, -1)

    (cq, ckv, qn, kvn, qcat, kn, kpe, v, ga, gb, qb, kb, vb, knt, kpet, vt, kbt, vbt) = _fwd_proj(
        x, w_in_r, w_uq_r, w_ukv_r, qg, kvg, tabs, bt)

    nq_m, nq_d = seq // blk_m, seq // blk_d
    span_d = DIL_CONFIGS[-1][0] // blk_d
    bias_m, bias_d = jnp.asarray(_mla_bias_t(blk_m)), jnp.asarray(_dil_bias_t(blk_d))
    oa, lse_a, w_out = _attn_fwd(
        "mla_fwd", qcat, kn, kpe, vt, bias_m, _steps(nq_m, nq_m, False, True), blk_m, ride=w_out_rider)
    ob, lse_b = _attn_fwd("dil_fwd", qb, kb, None, vbt, bias_d, _steps(nq_d, span_d, False, False), blk_d)

    dz, doa, dob, dga, dgb, dst_a, dst_b, g_out, small1 = _out_ln(
        oa, ob, ga, gb, x, tgt, w_out.reshape(D_MODEL, D_MODEL), ln_g.reshape(1, -1), ln_b.reshape(1, -1), bt)

    dq_m, dkn, dkpe, dv, g_out_recv = _attn_bwd(
        "mla_bwd", qcat, kn, kpe, v, knt, kpet, bias_m, doa, lse_a, dst_a, _steps(nq_m, nq_m, True, True), blk_m,
        ride=g_out_rider(g_out.reshape(N_DEV, D_MODEL // N_DEV, D_MODEL)))
    dqb, dkb, dvb = _attn_bwd(
        "dil_bwd", qb, kb, None, vb, kbt, None, bias_d, dob, lse_b, dst_b, _steps(nq_d, span_d, True, False), blk_d)

    dh, g_uq_r, g_ukv_r, small2 = _bwd_mid(
        dq_m, dkn, dv, dkpe, dqb, dkb, dvb, dga, dgb, cq, ckv, qn, kvn, w_uq_r, w_ukv_r, qg, kvg, tabs, bt)
    g_in = _grad_w_in(x, dh, bt)
    g_uq, g_ukv = _restore_grads(g_uq_r, g_ukv_r)
    grads3 = [g_in] + [g.astype(BF16) for g in (
        g_uq.reshape(Q_RANK, N_DEV, -1).transpose(1, 0, 2), g_ukv.reshape(KV_RANK, N_DEV, -1).transpose(1, 0, 2))]
    small_part = _small_rows(small1[0], small1[1], small2[0, :Q_RANK], small2[1, :KV_RANK], small1[2])
    grad_x, *reduced = _grad_x(
        dz, dh, w_in_r, bt, ride=reduce_rider(grads3, g_out_recv, small_part, min(2, seq // bt - 1)))
    return grad_x, reduced


MESH_ID = pl.DeviceIdType.MESH
SHARD_SHAPES = ((D_MODEL, IN_WIDTH // N_DEV), (Q_RANK, 768 // N_DEV), (KV_RANK, 1024 // N_DEV), (D_MODEL // N_DEV, D_MODEL))
ADAM_ROWS = (32, 128, 128, 16)


def _me():
    x, y, c = lax.axis_index("x"), lax.axis_index("y"), lax.axis_index("c")
    return x, y, c, 4 * x + 2 * y + c


def _peer(k):
    x, y, c, _ = _me()
    px = 1 - x if (k >> 2) & 1 else x
    py = 1 - y if (k >> 1) & 1 else y
    pc = 1 - c if k & 1 else c
    return (px, py, pc), 4 * px + 2 * py + pc


def _all_gather_weights(shards):
    n = len(shards)
    shard = IN_WIDTH // N_DEV
    k_lo = IN_SPLITS[0] + IN_SPLITS[1]
    k_hi = k_lo + MLA_ROPE

    def body(*refs):
        ins = refs[:n]
        win_ref, wuq_ref, wukv_ref = refs[n:2 * n]
        bufs = refs[2 * n:3 * n]
        send_sems, recv_sems = refs[3 * n:]
        x, y, c, me = _me()
        here, sibling = (x, y, c), (x, y, 1 - c)
        chips = [(1 - x, y), (x, 1 - y), (1 - x, 1 - y)]
        for t in range(n):
            bufs[t][me] = ins[t][...].astype(BF16)

        def copy(t, k, px, py, pc, to):
            blk = bufs[t].at[4 * px + 2 * py + pc]
            return pltpu.make_async_remote_copy(
                src_ref=blk, dst_ref=blk, send_sem=send_sems.at[t, k], recv_sem=recv_sems.at[t, k],
                device_id=to, device_id_type=MESH_ID)

        first = []
        for t in range(n):
            first.append(copy(t, 0, x, y, c, sibling))
            for j, (px, py) in enumerate(chips):
                first.append(copy(t, 1 + j, x, y, c, (px, py, c)))
        for cp in first:
            cp.start()
        passed = []
        for j, (px, py) in enumerate(chips):
            for t in range(n):
                copy(t, 1 + j, px, py, c, here).wait_recv()
                cp = copy(t, 4 + j, px, py, c, sibling)
                cp.start()
                passed.append(cp)
        for t in range(n):
            copy(t, 0, x, y, 1 - c, here).wait_recv()
        for j, (px, py) in enumerate(chips):
            for t in range(n):
                copy(t, 4 + j, px, py, 1 - c, here).wait_recv()
        for cp in first + passed:
            cp.wait_send()

        a_in, a_uq, a_ukv = bufs
        for d in range(N_DEV):
            lo, hi = shard * d, shard * (d + 1)
            if lo < k_lo:
                win_ref[:, lo:min(hi, k_lo)] = a_in[d, :, 0:min(hi, k_lo) - lo]
            if lo < k_hi and hi > k_lo:
                kr = a_in[d, :, k_lo - lo:k_hi - lo]
                for rep in range(4):
                    win_ref[:, C_KR + MLA_ROPE * rep:C_KR + MLA_ROPE * (rep + 1)] = kr
            if hi > k_hi:
                src = max(lo, k_hi)
                win_ref[:, src + C_GA - k_hi:hi + C_GA - k_hi] = a_in[d, :, src - lo:hi - lo]
        for h in range(HEADS):
            wuq_ref[:, MLA_NOPE * h:MLA_NOPE * (h + 1)] = a_uq[h, :, :MLA_NOPE]
            wuq_ref[:, HW + MLA_ROPE * h:HW + MLA_ROPE * (h + 1)] = a_uq[h, :, MLA_NOPE:]
            wukv_ref[:, MLA_NOPE * h:MLA_NOPE * (h + 1)] = a_ukv[h, :, :MLA_NOPE]
            wukv_ref[:, HW + MLA_V * h:HW + MLA_V * (h + 1)] = a_ukv[h, :, MLA_NOPE:]

    vmem = pl.BlockSpec(memory_space=pltpu.VMEM)
    return _pcall(
        body, name="gather_weights",
        in_specs=[vmem] * n, out_specs=[vmem] * n,
        out_shape=[jax.ShapeDtypeStruct((D_MODEL, C_END), BF16), jax.ShapeDtypeStruct((Q_RANK, QW), BF16),
                   jax.ShapeDtypeStruct((KV_RANK, 2 * HW), BF16)],
        scratch_shapes=[pltpu.VMEM((N_DEV,) + s, BF16) for s in SHARD_SHAPES[:n]]
        + [pltpu.SemaphoreType.DMA((n, N_DEV - 1)), pltpu.SemaphoreType.DMA((n, N_DEV - 1))],
        compiler_params=_cparams(),
    )(*shards)


def _gather_w_out_rider(w_out):
    def copies(full_ref, stage, send_sems, recv_sems):
        me = _me()[3]
        out = []
        for k in range(1, N_DEV):
            peer, pidx = _peer(k)
            send = pltpu.make_async_remote_copy(
                src_ref=stage, dst_ref=full_ref.at[me], send_sem=send_sems.at[k - 1], recv_sem=recv_sems.at[k - 1],
                device_id=peer, device_id_type=MESH_ID)
            recv = pltpu.make_async_remote_copy(
                src_ref=stage, dst_ref=full_ref.at[pidx], send_sem=send_sems.at[k - 1], recv_sem=recv_sems.at[k - 1],
                device_id=peer, device_id_type=MESH_ID)
            out.append((send, recv))
        return out

    def start(ins, outs, scr):
        stage, send_sems, recv_sems, own_sem = scr
        stage[...] = ins[0][...].astype(BF16)
        pltpu.make_async_copy(stage, outs[0].at[_me()[3]], own_sem).start()
        for send, _ in copies(outs[0], stage, send_sems, recv_sems):
            send.start()

    def finish(ins, outs, scr):
        stage, send_sems, recv_sems, own_sem = scr
        pltpu.make_async_copy(stage, outs[0].at[_me()[3]], own_sem).wait()
        pairs = copies(outs[0], stage, send_sems, recv_sems)
        for _, recv in pairs:
            recv.wait_recv()
        for send, _ in pairs:
            send.wait_send()

    shape = SHARD_SHAPES[3]
    return Rider(
        args=[w_out], in_specs=[pl.BlockSpec(shape, lambda t, *_: (0, 0))],
        out_shape=[jax.ShapeDtypeStruct((N_DEV,) + shape, BF16)], out_specs=[pl.BlockSpec(memory_space=pl.ANY)],
        scratch=[pltpu.VMEM(shape, BF16), pltpu.SemaphoreType.DMA((N_DEV - 1,)), pltpu.SemaphoreType.DMA((N_DEV - 1,)),
                 pltpu.SemaphoreType.DMA],
        start=start, finish=finish)


def _scatter_g_out_rider(blocks):
    def copies(src_ref, dst_ref, send_sems, recv_sems):
        out = []
        for k in range(1, N_DEV):
            peer, pidx = _peer(k)
            out.append(pltpu.make_async_remote_copy(
                src_ref=src_ref.at[pidx], dst_ref=dst_ref.at[k], send_sem=send_sems.at[k - 1],
                recv_sem=recv_sems.at[k - 1], device_id=peer, device_id_type=MESH_ID))
        return out

    def start(ins, outs, scr):
        send_sems, recv_sems, own_sem = scr
        pltpu.make_async_copy(ins[0].at[_me()[3]], outs[0].at[0], own_sem).start()
        for cp in copies(ins[0], outs[0], send_sems, recv_sems):
            cp.start()

    def finish(ins, outs, scr):
        send_sems, recv_sems, own_sem = scr
        pltpu.make_async_copy(ins[0].at[_me()[3]], outs[0].at[0], own_sem).wait()
        for cp in copies(ins[0], outs[0], send_sems, recv_sems):
            cp.wait()

    hbm = pl.BlockSpec(memory_space=pl.ANY)
    return Rider(
        args=[blocks], in_specs=[hbm], out_shape=[jax.ShapeDtypeStruct(blocks.shape, blocks.dtype)], out_specs=[hbm],
        scratch=[pltpu.SemaphoreType.DMA((N_DEV - 1,)), pltpu.SemaphoreType.DMA((N_DEV - 1,)), pltpu.SemaphoreType.DMA],
        start=start, finish=finish)


def _adamw(w, g, m, v):
    m = ADAM_B1 * m + (1.0 - ADAM_B1) * g
    v = ADAM_B2 * v + (1.0 - ADAM_B2) * jnp.square(g)
    m_hat = m / (1.0 - ADAM_B1 ** ADAM_STEP)
    v_hat = v / (1.0 - ADAM_B2 ** ADAM_STEP)
    delta = -ADAM_LR * (m_hat / (jnp.sqrt(v_hat) + ADAM_EPS) + ADAM_WD * w)
    return delta, m, v


def _reduce_grads_rider(grads3, arrived, small_part, mid_step):
    n = len(grads3)

    class Refs:
        def __init__(self, ins, outs, scr):
            self.g3, self.arr, self.sp = ins[0:n], ins[n], ins[n + 1]
            self.gsum, self.gsum_out, self.ssum = outs[0:n], outs[n], outs[n + 1]
            self.own, self.sib, self.part, self.ici = scr[0:n], scr[n:2 * n], scr[2 * n:3 * n], scr[3 * n:4 * n]
            self.rsmall = scr[4 * n]
            (self.loc_sems, self.d2d_send, self.d2d_recv, self.ici_send, self.ici_recv,
             self.sm_send, self.sm_recv) = scr[4 * n + 1:]
            self.x, self.y, self.c, self.me = _me()
            self.chips = [(1 - self.x, self.y), (self.x, 1 - self.y), (1 - self.x, 1 - self.y)]

        def small(self):
            return [pltpu.make_async_remote_copy(
                src_ref=self.rsmall.at[0], dst_ref=self.rsmall.at[k], send_sem=self.sm_send.at[k - 1],
                recv_sem=self.sm_recv.at[k - 1], device_id=_peer(k)[0], device_id_type=MESH_ID)
                for k in range(1, N_DEV)]

        def level1(self):
            local, to_sib = [], []
            for t in range(n):
                for q in range(4):
                    local.append(pltpu.make_async_copy(
                        self.g3[t].at[2 * q + self.c], self.own[t].at[q], self.loc_sems.at[t, q]))
                    to_sib.append(pltpu.make_async_remote_copy(
                        src_ref=self.g3[t].at[2 * q + 1 - self.c], dst_ref=self.sib[t].at[q],
                        send_sem=self.d2d_send.at[t, q], recv_sem=self.d2d_recv.at[t, q],
                        device_id=(self.x, self.y, 1 - self.c), device_id_type=MESH_ID))
            return local, to_sib

        def level2(self):
            return [pltpu.make_async_remote_copy(
                src_ref=self.part[t].at[2 * px + py], dst_ref=self.ici[t].at[j], send_sem=self.ici_send.at[t, j],
                recv_sem=self.ici_recv.at[t, j], device_id=(px, py, self.c), device_id_type=MESH_ID)
                for t in range(n) for j, (px, py) in enumerate(self.chips)]

    def chunks(t, fn):
        rows = ADAM_ROWS[t]

        def step(i, carry):
            fn(pl.ds(pl.multiple_of(i * rows, rows), rows))
            return carry

        lax.fori_loop(0, SHARD_SHAPES[t][0] // rows, step, 0)

    def start(*refs):
        r = Refs(*refs)
        r.rsmall[0] = r.sp[...]
        local, to_sib = r.level1()
        for cp in r.small() + local + to_sib:
            cp.start()

    def middle(*refs):
        r = Refs(*refs)
        local, to_sib = r.level1()
        for cp in local:
            cp.wait()
        for cp in to_sib:
            cp.wait_recv()
        my_chip = 2 * r.x + r.y
        for t in range(n):
            def pair_sums(rows, t=t):
                for q in range(4):
                    r.part[t][q, rows, :] = (
                        r.own[t][q, rows, :].astype(F32) + r.sib[t][q, rows, :].astype(F32)).astype(BF16)
                r.gsum[t][rows, :] = r.own[t][my_chip, rows, :].astype(F32) + r.sib[t][my_chip, rows, :].astype(F32)

            chunks(t, pair_sums)
        for cp in r.level2():
            cp.start()

    def finish(*refs):
        r = Refs(*refs)

        def add_arrived(rows):
            g = r.arr[0, rows, :].astype(F32)
            for k in range(1, N_DEV):
                g = g + r.arr[k, rows, :].astype(F32)
            r.gsum_out[rows, :] = g

        chunks(3, add_arrived)
        to_chips = r.level2()
        for cp in to_chips:
            cp.wait_recv()
        for t in range(n):
            def add_chips(rows, t=t):
                g = r.gsum[t][rows, :]
                for j in range(3):
                    g = g + r.ici[t][j, rows, :].astype(F32)
                r.gsum[t][rows, :] = g

            chunks(t, add_chips)
        small = r.small()
        for cp in small:
            cp.wait_recv()
        tot = r.rsmall[r.me]
        for d in range(1, N_DEV):
            tot = tot + r.rsmall[jnp.bitwise_xor(r.me, d)]
        r.ssum[...] = tot
        for cp in small + r.level1()[1] + to_chips:
            cp.wait_send()

    hbm = pl.BlockSpec(memory_space=pl.ANY)
    dma = pltpu.SemaphoreType.DMA

    def whole(shape):
        return pl.BlockSpec(shape, lambda i: (0,) * len(shape))

    out_shapes = list(SHARD_SHAPES) + [(8, D_MODEL)]
    return Rider(
        args=list(grads3) + [arrived, small_part],
        in_specs=[hbm] * n + [whole(arrived.shape), whole(small_part.shape)],
        out_shape=[jax.ShapeDtypeStruct(s, F32) for s in out_shapes], out_specs=[whole(s) for s in out_shapes],
        scratch=[pltpu.VMEM((slots,) + s, BF16) for slots in (4, 4, 4, 3) for s in SHARD_SHAPES[:n]]
        + [pltpu.VMEM((N_DEV, 8, D_MODEL), F32), dma((n, 4)), dma((n, 4)), dma((n, 4)), dma((n, 3)), dma((n, 3)),
           dma((N_DEV - 1,)), dma((N_DEV - 1,))],
        start=start, finish=finish, stages=((mid_step, middle),))


def _adamw_update(grads, small_grad, wmv, small_wmv):
    def body(*refs):
        g_refs, sg_ref = refs[0:4], refs[4]
        wmv_refs = [refs[5 + 3 * t:8 + 3 * t] for t in range(4)]
        swmv_ref = refs[17]
        out_refs = [refs[18 + 4 * t:22 + 4 * t] for t in range(4)]
        sout_ref = refs[34]
        tot = sg_ref[...]
        delta, m, v = _adamw(swmv_ref[0], tot, swmv_ref[1], swmv_ref[2])
        sout_ref[0], sout_ref[1], sout_ref[2], sout_ref[3] = tot, delta, m, v
        for t in range(4):
            rows = ADAM_ROWS[t]
            w_ref, m_ref, v_ref = wmv_refs[t]
            g_out, d_out, m_out, v_out = out_refs[t]

            def step(i, carry, g_ref=g_refs[t], rows=rows, w_ref=w_ref, m_ref=m_ref, v_ref=v_ref,
                     g_out=g_out, d_out=d_out, m_out=m_out, v_out=v_out):
                r = pl.ds(pl.multiple_of(i * rows, rows), rows)
                g = g_ref[r, :]
                delta, m, v = _adamw(w_ref[r, :], g, m_ref[r, :], v_ref[r, :])
                g_out[r, :], d_out[r, :], m_out[r, :], v_out[r, :] = g, delta, m, v
                return carry

            lax.fori_loop(0, SHARD_SHAPES[t][0] // rows, step, 0)

    vmem = pl.BlockSpec(memory_space=pltpu.VMEM)
    flat_wmv = [a for trio in wmv for a in trio]
    return _pcall(
        body, name="adamw",
        in_specs=[vmem] * 18, out_specs=[vmem] * 17,
        out_shape=[jax.ShapeDtypeStruct(s, F32) for s in SHARD_SHAPES for _ in range(4)]
        + [jax.ShapeDtypeStruct((4, 8, D_MODEL), F32)],
        compiler_params=_cparams(),
    )(*grads, small_grad, *flat_wmv, small_wmv)


def _small_rows(ln_g, ln_b, q_norm_g, kv_norm_g, extra=None):
    pad = lambda a: jnp.pad(a, (0, D_MODEL - a.shape[0]))
    rows = [ln_g, ln_b, pad(q_norm_g), pad(kv_norm_g)] + ([] if extra is None else [extra])
    return jnp.pad(jnp.stack(rows), ((0, 8 - len(rows)), (0, 0)))


def kernel(x, w_in, q_norm_g, kv_norm_g, w_uq, w_ukv, w_out, ln_g, ln_b, loss_target, m_w_in, m_q_norm_g, m_kv_norm_g, m_w_uq, m_w_ukv, m_w_out, m_ln_g, m_ln_b, v_w_in, v_q_norm_g, v_kv_norm_g, v_w_uq, v_w_ukv, v_w_out, v_ln_g, v_ln_b):
    w_in_r, w_uq_r, w_ukv_r = _all_gather_weights([w_in, w_uq, w_ukv])
    grad_x, sums = _local_step(
        x[0], loss_target[0], w_in_r, w_uq_r, w_ukv_r, _gather_w_out_rider(w_out), _scatter_g_out_rider,
        _reduce_grads_rider, q_norm_g, kv_norm_g, ln_g, ln_b)
    small_wmv = jnp.stack([_small_rows(ln_g, ln_b, q_norm_g, kv_norm_g),
                           _small_rows(m_ln_g, m_ln_b, m_q_norm_g, m_kv_norm_g),
                           _small_rows(v_ln_g, v_ln_b, v_q_norm_g, v_kv_norm_g)])
    wmv = [(w_in, m_w_in, v_w_in), (w_uq, m_w_uq, v_w_uq), (w_ukv, m_w_ukv, v_w_ukv), (w_out, m_w_out, v_w_out)]
    res = _adamw_update(sums[:4], sums[4], wmv, small_wmv)
    big = [res[4 * t:4 * t + 4] for t in range(4)]
    small = res[16]
    loss = (0.5 / D_MODEL) * jnp.sum(small[0, 4])

    def group(kind):
        s = small[kind]
        return (big[0][kind], s[2, :Q_RANK], s[3, :KV_RANK], big[1][kind], big[2][kind], big[3][kind], s[0], s[1])

    return (loss, grad_x[None], *group(0), *group(1), *group(2), *group(3))
```

```python
import functools
from typing import Callable, NamedTuple

import numpy as np
import jax
import jax.numpy as jnp
from jax import lax
from jax.experimental import pallas as pl
from jax.experimental.pallas import tpu as pltpu

F32 = jnp.float32
BF16 = jnp.bfloat16

D_MODEL = 1024
ROPE_THETA = 500000.0
NEG = -1e30
RMS_EPS = 1e-6
LN_EPS = 1e-5
HEADS = 8
MLA_NOPE = 64
MLA_ROPE = 32
MLA_V = 64
Q_RANK = 384
KV_RANK = 256
DIL_HEAD = 64
DIL_ROT = 16
DIL_CONFIGS = ((128, 1), (512, 4), (2048, 16))
DIL_NEAR = 512
HW = HEADS * 64
QW = HW + HEADS * MLA_ROPE
IN_SPLITS = (Q_RANK, KV_RANK, MLA_ROPE, HW, HW, HW, HW, HW)
IN_WIDTH = sum(IN_SPLITS)
ALPHA = 2.0 ** 0.25
MLA_SCALE = (MLA_NOPE + MLA_ROPE) ** -0.5
DIL_SCALE = DIL_HEAD ** -0.5
LOG2E = 1.4426950408889634
LN2 = 0.6931471805599453

ADAM_LR = 0.001
ADAM_B1 = 0.9
ADAM_B2 = 0.999
ADAM_EPS = 1e-08
ADAM_WD = 0.01
ADAM_STEP = 10

N_DEV = 8
LANES = 128
VMEM_LIMIT = 56 * 1024 * 1024
BLOCK_TOKENS = 512
BLOCK_MLA = 512
BLOCK_DIL = 512

C_CQ, C_CKV, C_KR, C_GA, C_QB, C_KB, C_VB, C_GB, C_END = 0, 384, 640, 768, 1280, 1792, 2304, 2816, 3328

NT = (((1,), (1,)), ((), ()))
TN = (((0,), (0,)), ((), ()))


def _pcall(body, **kw):
    return pl.pallas_call(body, **kw)


def _cparams(**kw):
    return pltpu.CompilerParams(vmem_limit_bytes=VMEM_LIMIT, **kw)


def _rope_tables(seq):
    def tabs(dim, period):
        half = dim // 2
        inv = np.float32(ROPE_THETA) ** (-np.arange(0, dim, 2, dtype=np.float32) / np.float32(dim))
        ang = np.arange(seq, dtype=np.float32)[:, None] * inv.astype(np.float32)[None, :]
        cos, sin = np.cos(ang).astype(np.float32), np.sin(ang).astype(np.float32)
        j = np.arange(LANES) % period
        f = j % half
        c = np.where(j < dim, cos[:, f], np.float32(1.0))
        s1 = np.where(j < half, -sin[:, f], np.float32(0.0))
        s2 = np.where((j >= half) & (j < dim), sin[:, f], np.float32(0.0))
        return [c, s1, s2]
    return np.stack(tabs(MLA_ROPE, MLA_ROPE) + tabs(DIL_ROT, DIL_HEAD)).astype(np.float32)


def _rope(t, c, s1, s2, half):
    return t * c + pltpu.roll(t, LANES - half, 1) * s1 + pltpu.roll(t, half, 1) * s2


def _rope_t(d, c, s1, s2, half):
    return d * c + pltpu.roll(d * s1, half, 1) + pltpu.roll(d * s2, LANES - half, 1)


def _rope_wide(fn, t, c, s1, s2, half):
    return jnp.concatenate(
        [fn(t[:, i:i + LANES], c, s1, s2, half) for i in range(0, t.shape[1], LANES)], axis=1)


def _mla_bias_t(blk):
    a = np.arange(blk)
    causal = np.where(a[:, None] <= a[None, :], 0.0, NEG)
    return np.stack([np.zeros((blk, blk)), causal]).astype(np.float32)


def _dil_bias_t(blk, reach):
    a = np.arange(blk)
    out = []
    for off in range(-(-reach // blk) + 1):
        delta = blk * off + a[None, :] - a[:, None]
        mult = np.zeros((blk, blk))
        for window, dil in DIL_CONFIGS:
            mult += (delta >= 0) & (delta % dil == 0) & (delta <= min(window, reach))
        out.append(np.where(mult > 0, np.log2(np.maximum(mult, 1.0)), NEG))
    return np.stack(out).astype(np.float32)


def _dil_far_bias_t(length):
    window, dil = DIL_CONFIGS[-1]
    a = np.arange(length)
    steps_back = a[None, :] - a[:, None]
    seen = (steps_back * dil > DIL_NEAR) & (steps_back * dil <= window)
    return np.where(seen, 0.0, NEG).astype(np.float32)[None]


def _to_classes(a, dil):
    s, w = a.shape
    return a.reshape(s // dil, dil, w).transpose(1, 0, 2).reshape(s, w)


def _from_classes(a, dil):
    s, w = a.shape
    return a.reshape(dil, s // dil, w).transpose(1, 0, 2).reshape(s, w)


def _lanes_to_classes(a, dil):
    h, s = a.shape
    return a.reshape(h, s // dil, dil).transpose(0, 2, 1).reshape(h, s)


def _lanes_from_classes(a, dil):
    h, s = a.shape
    return a.reshape(h, dil, s // dil).transpose(0, 2, 1).reshape(h, s)


def _steps(nq, span, by_key, diag_only_bias):
    rows = []
    if by_key:
        for ki in range(nq):
            hi = min(nq - 1, ki + span)
            for qi in range(ki, hi + 1):
                rows.append((qi, ki, int(qi == ki), int(qi == hi)))
    else:
        for qi in range(nq):
            lo = max(0, qi - span)
            for ki in range(lo, qi + 1):
                rows.append((qi, ki, int(ki == lo), int(ki == qi)))
    arr = np.array(rows, dtype=np.int32)
    off = arr[:, 0] - arr[:, 1]
    bias_idx = (off == 0).astype(np.int32) if diag_only_bias else off.astype(np.int32)
    return [jnp.asarray(v) for v in (arr[:, 0], arr[:, 1], bias_idx, arr[:, 2], arr[:, 3])]


def _fwd_proj(x, w_in_r, w_uq_r, w_ukv_r, qg, kvg, tabs, bt):
    seq = x.shape[0]

    def body(x_ref, win_ref, wuq_ref, wukv_ref, qg_ref, kvg_ref, tab_ref,
             cq_ref, ckv_ref, qn_ref, kvn_ref, qcat_ref, kn_ref, kpe_ref, v_ref,
             ga_ref, gb_ref, qb_ref, kb_ref, vb_ref, knt_ref, kpet_ref, vt_ref, kbt_ref, vbt_ref):
        xb = x_ref[...].astype(BF16)

        def proj(lo, hi):
            return jnp.dot(xb, win_ref[:, lo:hi], preferred_element_type=F32)

        m_tabs = (tab_ref[0], tab_ref[1], tab_ref[2])
        d_tabs = (tab_ref[3], tab_ref[4], tab_ref[5])

        cq = proj(C_CQ, C_CKV)
        cq_ref[...] = cq
        qn = (cq * lax.rsqrt(jnp.mean(cq * cq, axis=1, keepdims=True) + RMS_EPS) * qg_ref[...]).astype(BF16)
        qn_ref[...] = qn
        q = jnp.dot(qn, wuq_ref[...], preferred_element_type=F32)
        qcat_ref[:, :HW] = (q[:, :HW] * (MLA_SCALE * LOG2E)).astype(BF16)
        qcat_ref[:, HW:] = (_rope_wide(_rope, q[:, HW:], *m_tabs, MLA_ROPE // 2) * (MLA_SCALE * LOG2E)).astype(BF16)

        ckv = proj(C_CKV, C_KR)
        ckv_ref[...] = ckv
        kvn = (ckv * lax.rsqrt(jnp.mean(ckv * ckv, axis=1, keepdims=True) + RMS_EPS) * kvg_ref[...]).astype(BF16)
        kvn_ref[...] = kvn
        kv = jnp.dot(kvn, wukv_ref[...], preferred_element_type=F32)
        kn_ref[...] = kv[:, :HW].astype(BF16)
        v_ref[...] = kv[:, HW:].astype(BF16)
        knt_ref[...] = kv[:, :HW].T.astype(BF16)
        vt_ref[...] = kv[:, HW:].T.astype(BF16)

        kpe = _rope(proj(C_KR, C_GA), *m_tabs, MLA_ROPE // 2)
        kpe_ref[...] = kpe.astype(BF16)
        kpet_ref[...] = kpe.T[:MLA_ROPE, :].astype(BF16)
        ga_ref[...] = proj(C_GA, C_QB)
        qb_ref[...] = (_rope_wide(_rope, proj(C_QB, C_KB), *d_tabs, DIL_ROT // 2) * (DIL_SCALE * LOG2E)).astype(BF16)
        kb = _rope_wide(_rope, proj(C_KB, C_VB), *d_tabs, DIL_ROT // 2)
        kb_ref[...] = kb.astype(BF16)
        kbt_ref[...] = kb.T.astype(BF16)
        vb = proj(C_VB, C_GB)
        vb_ref[...] = vb.astype(BF16)
        vbt_ref[...] = vb.T.astype(BF16)
        gb_ref[...] = proj(C_GB, C_END)

    def tok(width):
        return pl.BlockSpec((bt, width), lambda i: (i, 0))

    def tok_t(height):
        return pl.BlockSpec((height, bt), lambda i: (0, i))

    def full(a):
        return pl.BlockSpec(a.shape, lambda i: (0,) * a.ndim)

    outs = [(Q_RANK, F32), (KV_RANK, F32), (Q_RANK, BF16), (KV_RANK, BF16), (QW, BF16), (HW, BF16),
            (LANES, BF16), (HW, BF16), (HW, F32), (HW, F32), (HW, BF16), (HW, BF16), (HW, BF16)]
    outs_t = [HW, MLA_ROPE, HW, HW, HW]
    return _pcall(
        body, name="fwd_proj", grid=(seq // bt,),
        in_specs=[tok(D_MODEL), full(w_in_r), full(w_uq_r), full(w_ukv_r), full(qg), full(kvg),
                  pl.BlockSpec((6, bt, LANES), lambda i: (0, i, 0))],
        out_specs=[tok(w) for w, _ in outs] + [tok_t(h) for h in outs_t],
        out_shape=[jax.ShapeDtypeStruct((seq, w), dt) for w, dt in outs]
        + [jax.ShapeDtypeStruct((h, seq), BF16) for h in outs_t],
        compiler_params=_cparams(dimension_semantics=("arbitrary",)),
    )(x, w_in_r, w_uq_r, w_ukv_r, qg, kvg, tabs)


def _head_masks(lane, h):
    e, g = h % 2, h % 4
    me = (lane >= 64 * e) & (lane < 64 * e + 64)
    mr = (lane >= 32 * g) & (lane < 32 * g + 32)
    return me, mr


def _masked(mask, a):
    return jnp.where(mask, a, jnp.zeros_like(a))


def _pair_operands(q_ref, k_ref, kpe_ref, lane, j):
    cols = slice(LANES * j, LANES * (j + 1))
    qc = q_ref[:, cols]
    kj = k_ref[:, cols]
    kes = []
    for h in (2 * j, 2 * j + 1):
        me, mr = _head_masks(lane, h)
        ke = _masked(me, kj)
        if kpe_ref is not None:
            ke = jnp.concatenate([ke, _masked(mr, kpe_ref[...])], axis=1)
        kes.append(ke)
    if kpe_ref is not None:
        qc = jnp.concatenate([qc, q_ref[:, HW + LANES * (j // 2):HW + LANES * (j // 2 + 1)]], axis=1)
    return qc, kes


class Rider(NamedTuple):
    args: list
    in_specs: list
    out_shape: list
    out_specs: list
    scratch: list
    start: Callable
    finish: Callable
    stages: tuple = ()


def _ride_along(body, ride, n_prefetch, n_in, n_out, n_scratch, n_steps):
    if ride is None:
        return body

    def wrapped(*refs):
        pre, rest = refs[:n_prefetch], refs[n_prefetch:]
        a = n_in
        b = a + len(ride.args)
        c = b + n_out
        d = c + len(ride.out_shape)
        e = d + n_scratch
        mine = (rest[a:b], rest[c:d], rest[e:])
        t = pl.program_id(0)
        pl.when(t == 0)(lambda: ride.start(*mine))
        for at, stage in ride.stages:
            pl.when(t == at)(functools.partial(stage, *mine))
        body(*pre, *rest[:a], *rest[b:c], *rest[d:e])
        pl.when(t == n_steps - 1)(lambda: ride.finish(*mine))

    return wrapped


def _attn_fwd(name, q, k, kpe, vt, bias_t, steps, blk, ride=None):
    seq = q.shape[0]
    mla = kpe is not None
    n_steps = int(steps[0].shape[0])

    def body(qi_r, ki_r, bi_r, fi_r, la_r, *refs):
        if mla:
            q_ref, k_ref, kpe_ref, vt_ref, b_ref, o_ref, lse_ref, m_sc, l_sc, acc_sc, st_sc = refs
        else:
            q_ref, k_ref, vt_ref, b_ref, o_ref, lse_ref, m_sc, l_sc, acc_sc, st_sc = refs
        t = pl.program_id(0)

        @pl.when(fi_r[t] == 1)
        def _():
            m_sc[...] = jnp.full(m_sc.shape, NEG, F32)
            l_sc[...] = jnp.zeros(l_sc.shape, F32)
            acc_sc[...] = jnp.zeros(acc_sc.shape, F32)

        lane = lax.broadcasted_iota(jnp.int32, (1, LANES), 1)
        ones = jnp.ones((16, blk), BF16)

        def pair_scores(j, with_bias):
            qc, kes = _pair_operands(q_ref, k_ref, kpe_ref if mla else None, lane, j)
            st = lax.dot_general(jnp.concatenate(kes, axis=0), qc, NT, preferred_element_type=F32)
            maxes = []
            for e in range(2):
                se = st[e * blk:(e + 1) * blk]
                if with_bias:
                    se = se + b_ref[0]
                st_sc[j % 2, e * blk:(e + 1) * blk] = se
                maxes.append(jnp.max(se, axis=0, keepdims=True))
            return maxes

        def softmax_pv(h, col_max):
            st = st_sc[(h // 2) % 2, (h % 2) * blk:(h % 2 + 1) * blk]
            hrow = slice(h, h + 1)
            m_prev = m_sc[hrow, :]
            m_new = jnp.maximum(m_prev, col_max)
            alpha = jnp.exp2(m_prev - m_new)
            pt = jnp.exp2(st - m_new).astype(BF16)
            m_sc[hrow, :] = m_new
            rows = slice(64 * h, 64 * h + 64)
            res = jnp.dot(jnp.concatenate([vt_ref[rows, :], ones], axis=0), pt, preferred_element_type=F32)
            acc_sc[rows, :] = alpha * acc_sc[rows, :] + res[:64]
            l_sc[hrow, :] = alpha * l_sc[hrow, :] + res[64:65]

        def step(with_bias):
            maxes = pair_scores(0, with_bias)
            for j in range(HEADS // 2):
                cur = maxes
                if j + 1 < HEADS // 2:
                    maxes = pair_scores(j + 1, with_bias)
                softmax_pv(2 * j, cur[0])
                softmax_pv(2 * j + 1, cur[1])

        if mla:
            pl.when(bi_r[t] == 1)(lambda: step(True))
            pl.when(bi_r[t] == 0)(lambda: step(False))
        else:
            step(True)

        @pl.when(la_r[t] == 1)
        def _():
            for h in range(HEADS):
                rows = slice(64 * h, 64 * h + 64)
                acc_sc[rows, :] = acc_sc[rows, :] / l_sc[h:h + 1, :]
            o_ref[...] = acc_sc[...].T
            lse_ref[...] = m_sc[...] + jnp.log2(l_sc[...])

    qmap = lambda t, qi, ki, bi, fi, la: (qi[t], 0)
    kmap = lambda t, qi, ki, bi, fi, la: (ki[t], 0)
    in_specs = [pl.BlockSpec((blk, q.shape[1]), qmap), pl.BlockSpec((blk, HW), kmap)]
    args = [q, k]
    if mla:
        in_specs.append(pl.BlockSpec((blk, LANES), kmap))
        args.append(kpe)
    in_specs += [pl.BlockSpec((HW, blk), lambda t, qi, ki, bi, fi, la: (0, ki[t])),
                 pl.BlockSpec((1, blk, blk), lambda t, qi, ki, bi, fi, la: (bi[t], 0, 0))]
    args += [vt, bias_t]
    out_specs = [pl.BlockSpec((blk, HW), qmap), pl.BlockSpec((HEADS, blk), lambda t, qi, ki, bi, fi, la: (0, qi[t]))]
    out_shape = [jax.ShapeDtypeStruct((seq, HW), F32), jax.ShapeDtypeStruct((HEADS, seq), F32)]
    scratch = [pltpu.VMEM((HEADS, blk), F32), pltpu.VMEM((HEADS, blk), F32),
               pltpu.VMEM((HW, blk), F32), pltpu.VMEM((2, 2 * blk, blk), F32)]
    body = _ride_along(body, ride, 5, len(args), len(out_shape), len(scratch), n_steps)
    if ride is not None:
        args, in_specs = args + ride.args, in_specs + ride.in_specs
        out_specs, out_shape, scratch = out_specs + ride.out_specs, out_shape + ride.out_shape, scratch + ride.scratch
    return _pcall(
        body, name=name,
        grid_spec=pltpu.PrefetchScalarGridSpec(
            num_scalar_prefetch=5, grid=(n_steps,), in_specs=in_specs, out_specs=out_specs, scratch_shapes=scratch),
        out_shape=out_shape,
        compiler_params=_cparams(dimension_semantics=("arbitrary",)),
    )(*steps, *args)


def _attn_bwd(name, q, k, kpe, v, kt, kpet, bias_t, do, lse, dstat, steps, blk, ride=None):
    seq = q.shape[0]
    mla = kpe is not None
    qw = q.shape[1]
    n_steps = int(steps[0].shape[0])
    dk_dtype = BF16 if mla else F32

    def body(qi_r, ki_r, bi_r, fi_r, la_r, *refs):
        if mla:
            (q_ref, k_ref, kpe_ref, v_ref, kt_ref, kpet_ref, b_ref, do_ref, lse_ref, d_ref,
             dq_ref, dk_ref, dkpe_ref, dv_ref, dk_sc, dkpe_sc, dv_sc, st_sc, dpt_sc) = refs
        else:
            (q_ref, k_ref, v_ref, kt_ref, b_ref, do_ref, lse_ref, d_ref,
             dq_ref, dk_ref, dv_ref, dk_sc, dv_sc, st_sc, dpt_sc) = refs
        t = pl.program_id(0)

        @pl.when(t == 0)
        def _():
            dq_ref[...] = jnp.zeros(dq_ref.shape, F32)

        @pl.when(fi_r[t] == 1)
        def _():
            dk_sc[...] = jnp.zeros(dk_sc.shape, F32)
            dv_sc[...] = jnp.zeros(dv_sc.shape, F32)
            if mla:
                dkpe_sc[...] = jnp.zeros(dkpe_sc.shape, F32)

        qi = qi_r[t]
        lane = lax.broadcasted_iota(jnp.int32, (1, LANES), 1)

        def pair_matmuls(j):
            cols = slice(LANES * j, LANES * (j + 1))
            qc, kes = _pair_operands(q_ref, k_ref, kpe_ref if mla else None, lane, j)
            st_sc[j % 2] = lax.dot_general(jnp.concatenate(kes, axis=0), qc, NT, preferred_element_type=F32)
            vj = v_ref[:, cols]
            ves = [_masked(_head_masks(lane, h)[0], vj) for h in (2 * j, 2 * j + 1)]
            dpt_sc[j % 2] = lax.dot_general(
                jnp.concatenate(ves, axis=0), do_ref[:, cols], NT, preferred_element_type=F32)

        def pair_grads(j, with_bias):
            cols = slice(LANES * j, LANES * (j + 1))
            qj, doj = q_ref[:, cols], do_ref[:, cols]
            if mla:
                qr = q_ref[:, HW + LANES * (j // 2):HW + LANES * (j // 2 + 1)]
            pts, dsts, qms, doms = [], [], [], []
            for e in range(2):
                h = 2 * j + e
                me, mr = _head_masks(lane, h)
                st = st_sc[j % 2, e * blk:(e + 1) * blk]
                if with_bias:
                    st = st + b_ref[0]
                pt = jnp.exp2(st - lse_ref[h:h + 1, :])
                dst = (pt * (dpt_sc[j % 2, e * blk:(e + 1) * blk] - d_ref[h:h + 1, :])).astype(BF16)
                pts.append(pt.astype(BF16))
                dsts.append(dst)
                doms.append(_masked(me, doj))
                qm = _masked(me, qj)
                if mla:
                    qm = jnp.concatenate([qm, _masked(mr, qr)], axis=1)
                qms.append(qm)
                ktl = kt_ref[64 * h:64 * h + 64, :]
                if mla:
                    ktl = jnp.concatenate([ktl, kpet_ref[...]], axis=0)
                dqc = jnp.dot(ktl, dst, preferred_element_type=F32)
                dq_ref[qi, 64 * h:64 * h + 64, :] += dqc[:64]
                if mla:
                    dq_ref[qi, HW + MLA_ROPE * h:HW + MLA_ROPE * (h + 1), :] += dqc[64:]
            dv_sc[:, cols] += jnp.dot(
                jnp.concatenate(pts, axis=1), jnp.concatenate(doms, axis=0), preferred_element_type=F32)
            dkc = jnp.dot(jnp.concatenate(dsts, axis=1), jnp.concatenate(qms, axis=0), preferred_element_type=F32)
            dk_sc[:, cols] += dkc[:, :LANES]
            if mla:
                dkpe_sc[...] += dkc[:, LANES:]

        def step(with_bias):
            pair_matmuls(0)
            for j in range(HEADS // 2):
                if j + 1 < HEADS // 2:
                    pair_matmuls(j + 1)
                pair_grads(j, with_bias)

        if mla:
            pl.when(bi_r[t] == 1)(lambda: step(True))
            pl.when(bi_r[t] == 0)(lambda: step(False))
        else:
            step(True)

        @pl.when(la_r[t] == 1)
        def _():
            dk_ref[...] = (dk_sc[...] * LN2).astype(dk_ref.dtype)
            dv_ref[...] = dv_sc[...].astype(dv_ref.dtype)
            if mla:
                dkpe_ref[...] = dkpe_sc[...] * LN2

    qmap = lambda t, qi, ki, bi, fi, la: (qi[t], 0)
    kmap = lambda t, qi, ki, bi, fi, la: (ki[t], 0)
    qmap_t = lambda t, qi, ki, bi, fi, la: (0, qi[t])
    kmap_t = lambda t, qi, ki, bi, fi, la: (0, ki[t])
    in_specs = [pl.BlockSpec((blk, qw), qmap), pl.BlockSpec((blk, HW), kmap)]
    args = [q, k]
    if mla:
        in_specs.append(pl.BlockSpec((blk, LANES), kmap))
        args.append(kpe)
    in_specs += [pl.BlockSpec((blk, HW), kmap), pl.BlockSpec((HW, blk), kmap_t)]
    args += [v, kt]
    if mla:
        in_specs.append(pl.BlockSpec((MLA_ROPE, blk), kmap_t))
        args.append(kpet)
    in_specs += [pl.BlockSpec((1, blk, blk), lambda t, qi, ki, bi, fi, la: (bi[t], 0, 0)),
                 pl.BlockSpec((blk, HW), qmap), pl.BlockSpec((HEADS, blk), qmap_t), pl.BlockSpec((HEADS, blk), qmap_t)]
    args += [bias_t, do, lse, dstat]
    dq_shape = (seq // blk, qw, blk)
    out_specs = [pl.BlockSpec(dq_shape, lambda t, qi, ki, bi, fi, la: (0, 0, 0)), pl.BlockSpec((blk, HW), kmap)]
    out_shape = [jax.ShapeDtypeStruct(dq_shape, F32), jax.ShapeDtypeStruct((seq, HW), dk_dtype)]
    scratch = [pltpu.VMEM((blk, HW), F32)]
    if mla:
        out_specs.append(pl.BlockSpec((blk, LANES), kmap))
        out_shape.append(jax.ShapeDtypeStruct((seq, LANES), F32))
        scratch.append(pltpu.VMEM((blk, LANES), F32))
    out_specs.append(pl.BlockSpec((blk, HW), kmap))
    out_shape.append(jax.ShapeDtypeStruct((seq, HW), BF16))
    scratch.append(pltpu.VMEM((blk, HW), F32))
    scratch += [pltpu.VMEM((2, 2 * blk, blk), F32), pltpu.VMEM((2, 2 * blk, blk), F32)]
    body = _ride_along(body, ride, 5, len(args), len(out_shape), len(scratch), n_steps)
    if ride is not None:
        args, in_specs = args + ride.args, in_specs + ride.in_specs
        out_specs, out_shape, scratch = out_specs + ride.out_specs, out_shape + ride.out_shape, scratch + ride.scratch
    return _pcall(
        body, name=name,
        grid_spec=pltpu.PrefetchScalarGridSpec(
            num_scalar_prefetch=5, grid=(n_steps,), in_specs=in_specs, out_specs=out_specs,
            scratch_shapes=scratch),
        out_shape=out_shape,
        compiler_params=_cparams(dimension_semantics=("arbitrary",)),
    )(*steps, *args)


def _out_ln(oa, ob_near, ob_far, lse_near, lse_far, ga, gb, x, tgt, w_out, ln_g, ln_b, bt):
    seq = x.shape[0]

    def body(oa_ref, obn_ref, obf_ref, lsen_ref, lsef_ref, ga_ref, gb_ref, x_ref, tgt_ref, w_ref, g_ref, b_ref,
             dz_ref, doa_ref, dob_ref, dga_ref, dgb_ref, da_ref, db_ref, lse_ref, gwb_ref, small_ref, gw_ref):
        i = pl.program_id(0)

        @pl.when(i == 0)
        def _():
            gw_ref[...] = jnp.zeros(gw_ref.shape, F32)
            small_ref[...] = jnp.zeros(small_ref.shape, F32)

        def gate(g):
            sig = 0.5 * jnp.tanh(0.5 * g) + 0.5
            return g * sig, sig * (1.0 + g * (1.0 - sig))

        lse_n, lse_f = lsen_ref[...], lsef_ref[...]
        top = jnp.maximum(lse_n, lse_f)
        e_n, e_f = jnp.exp2(lse_n - top), jnp.exp2(lse_f - top)
        lse_ref[...] = top + jnp.log2(e_n + e_f)
        inv = 1.0 / (e_n + e_f)
        head_row = lax.broadcasted_iota(jnp.int32, (2 * HEADS, HW), 0) % HEADS
        spread = (head_row == lax.broadcasted_iota(jnp.int32, (2 * HEADS, HW), 1) // 64).astype(BF16)

        def per_lane(w):
            hi = w.astype(BF16)
            lo = (w - hi.astype(F32)).astype(BF16)
            return lax.dot_general(jnp.concatenate([hi, lo], axis=0), spread, TN, preferred_element_type=F32)

        o_a = oa_ref[...]
        o_b = per_lane(e_n * inv) * obn_ref[...] + per_lane(e_f * inv) * obf_ref[...]
        g_a, g_b = ga_ref[...], gb_ref[...]
        sa, dsa = gate(g_a)
        sb, dsb = gate(g_b)
        mix = jnp.concatenate([o_a * sa, o_b * sb], axis=1).astype(BF16)
        z = ALPHA * x_ref[...] + jnp.dot(mix, w_ref[...], preferred_element_type=F32)
        mu = jnp.mean(z, axis=1, keepdims=True)
        zc = z - mu
        rstd = lax.rsqrt(jnp.mean(zc * zc, axis=1, keepdims=True) + LN_EPS)
        xhat = zc * rstd
        gam = g_ref[...]
        diff = xhat * gam + b_ref[...] - tgt_ref[...]
        dy = diff * (1.0 / D_MODEL)
        small_ref[0:1, :] += jnp.sum(dy * xhat, axis=0, keepdims=True)
        small_ref[1:2, :] += jnp.sum(dy, axis=0, keepdims=True)
        small_ref[2:3, :] += jnp.sum(diff * diff, axis=0, keepdims=True)
        dxh = dy * gam
        dz = rstd * (dxh - jnp.mean(dxh, axis=1, keepdims=True) - xhat * jnp.mean(dxh * xhat, axis=1, keepdims=True))
        dz_ref[...] = dz
        dzb = dz.astype(BF16)
        gw_ref[...] += lax.dot_general(mix, dzb, TN, preferred_element_type=F32)

        @pl.when(i == seq // bt - 1)
        def _():
            gwb_ref[...] = gw_ref[...].astype(BF16)

        dmix = lax.dot_general(dzb, w_ref[...], NT, preferred_element_type=F32)
        doa, dob = dmix[:, :HW] * sa, dmix[:, HW:] * sb
        doa_ref[...] = doa.astype(BF16)
        dob_ref[...] = dob.astype(BF16)
        dga_ref[...] = (dmix[:, :HW] * o_a * dsa).astype(BF16)
        dgb_ref[...] = (dmix[:, HW:] * o_b * dsb).astype(BF16)
        head_of = (lax.broadcasted_iota(jnp.int32, (2 * HW, LANES), 0) % HW) // 64
        ind = (head_of == lax.broadcasted_iota(jnp.int32, (2 * HW, LANES), 1)).astype(BF16)

        def head_sums(prod):
            hi = prod.astype(BF16)
            lo = (prod - hi.astype(F32)).astype(BF16)
            sums = jnp.dot(jnp.concatenate([hi, lo], axis=1), ind, preferred_element_type=F32)
            return sums.T[:HEADS, :]

        da_ref[...] = head_sums(doa * o_a)
        db_ref[...] = head_sums(dob * o_b)

    def tok(width):
        return pl.BlockSpec((bt, width), lambda i: (i, 0))

    def full(shape):
        return pl.BlockSpec(shape, lambda i: (0,) * len(shape))

    stat = pl.BlockSpec((HEADS, bt), lambda i: (0, i))
    return _pcall(
        body, name="out_ln", grid=(seq // bt,),
        in_specs=[tok(HW), tok(HW), tok(HW), stat, stat, tok(HW), tok(HW), tok(D_MODEL), tok(D_MODEL),
                  full((D_MODEL, D_MODEL)), full((1, D_MODEL)), full((1, D_MODEL))],
        out_specs=[tok(D_MODEL), tok(HW), tok(HW), tok(HW), tok(HW), stat, stat, stat,
                   full((D_MODEL, D_MODEL)), full((8, D_MODEL))],
        out_shape=[jax.ShapeDtypeStruct((seq, D_MODEL), F32)] + [jax.ShapeDtypeStruct((seq, HW), BF16)] * 4
        + [jax.ShapeDtypeStruct((HEADS, seq), F32)] * 3
        + [jax.ShapeDtypeStruct((D_MODEL, D_MODEL), BF16), jax.ShapeDtypeStruct((8, D_MODEL), F32)],
        scratch_shapes=[pltpu.VMEM((D_MODEL, D_MODEL), F32)],
        compiler_params=_cparams(dimension_semantics=("arbitrary",)),
    )(oa, ob_near, ob_far, lse_near, lse_far, ga, gb, x, tgt, w_out, ln_g, ln_b)


def _bwd_mid(dq_m, dkn, dv, dkpe, dqb, dqb_far, dkb, dvb, dga, dgb, cq, ckv, qn, kvn, w_uq_r, w_ukv_r, qg, kvg, tabs,
             bt):
    seq = cq.shape[0]

    def body(dqm_ref, dkn_ref, dv_ref, dkpe_ref, dqb_ref, dqf_ref, dkb_ref, dvb_ref, dga_ref, dgb_ref,
             cq_ref, ckv_ref, qn_ref, kvn_ref, wuq_ref, wukv_ref, qg_ref, kvg_ref, tab_ref,
             dh_ref, guq_ref, gukv_ref, small_ref):
        i = pl.program_id(0)

        @pl.when(i == 0)
        def _():
            guq_ref[...] = jnp.zeros(guq_ref.shape, F32)
            gukv_ref[...] = jnp.zeros(gukv_ref.shape, F32)
            small_ref[...] = jnp.zeros(small_ref.shape, F32)

        m_tabs = (tab_ref[0], tab_ref[1], tab_ref[2])
        d_tabs = (tab_ref[3], tab_ref[4], tab_ref[5])

        def rms_bwd(c, dn, gain):
            r = lax.rsqrt(jnp.mean(c * c, axis=1, keepdims=True) + RMS_EPS)
            u = dn * gain
            dc = r * u - c * (r * r * r) * jnp.mean(u * c, axis=1, keepdims=True)
            return dc, jnp.sum(dn * c * r, axis=0, keepdims=True)

        dqm = dqm_ref[0].T
        dq = jnp.concatenate(
            [dqm[:, :HW], _rope_wide(_rope_t, dqm[:, HW:], *m_tabs, MLA_ROPE // 2)], axis=1) * MLA_SCALE
        dq = dq.astype(BF16)
        guq_ref[...] += lax.dot_general(qn_ref[...], dq, TN, preferred_element_type=F32)
        dqn = lax.dot_general(dq, wuq_ref[...], NT, preferred_element_type=F32)
        dcq, gq = rms_bwd(cq_ref[...], dqn, qg_ref[...])
        small_ref[0:1, :] += gq

        dkv = jnp.concatenate([dkn_ref[...], dv_ref[...]], axis=1)
        gukv_ref[...] += lax.dot_general(kvn_ref[...], dkv, TN, preferred_element_type=F32)
        dkvn = lax.dot_general(dkv, wukv_ref[...], NT, preferred_element_type=F32)
        dckv, gkv = rms_bwd(ckv_ref[...], dkvn, kvg_ref[...])
        small_ref[1:2, :KV_RANK] += gkv

        dh_ref[:, C_CQ:C_CKV] = dcq.astype(BF16)
        dh_ref[:, C_CKV:C_KR] = dckv.astype(BF16)
        dh_ref[:, C_KR:C_GA] = _rope_t(dkpe_ref[...], *m_tabs, MLA_ROPE // 2).astype(BF16)
        dh_ref[:, C_GA:C_QB] = dga_ref[...]
        dqb = dqb_ref[0].T + dqf_ref[...]
        dh_ref[:, C_QB:C_KB] = (_rope_wide(_rope_t, dqb, *d_tabs, DIL_ROT // 2) * DIL_SCALE).astype(BF16)
        dh_ref[:, C_KB:C_VB] = _rope_wide(_rope_t, dkb_ref[...], *d_tabs, DIL_ROT // 2).astype(BF16)
        dh_ref[:, C_VB:C_GB] = dvb_ref[...]
        dh_ref[:, C_GB:C_END] = dgb_ref[...]

    def tok(width):
        return pl.BlockSpec((bt, width), lambda i: (i, 0))

    def tok_t(a):
        per = a.shape[2] // bt
        return pl.BlockSpec((1, a.shape[1], bt), lambda i: (i // per, 0, i % per))

    def full(shape):
        return pl.BlockSpec(shape, lambda i: (0,) * len(shape))

    return _pcall(
        body, name="bwd_mid", grid=(seq // bt,),
        in_specs=[tok_t(dq_m), tok(HW), tok(HW), tok(LANES), tok_t(dqb), tok(HW), tok(HW), tok(HW), tok(HW), tok(HW),
                  tok(Q_RANK), tok(KV_RANK), tok(Q_RANK), tok(KV_RANK),
                  full(w_uq_r.shape), full(w_ukv_r.shape), full((1, Q_RANK)), full((1, KV_RANK)),
                  pl.BlockSpec((6, bt, LANES), lambda i: (0, i, 0))],
        out_specs=[tok(C_END), full(w_uq_r.shape), full(w_ukv_r.shape), full((8, Q_RANK))],
        out_shape=[jax.ShapeDtypeStruct((seq, C_END), BF16), jax.ShapeDtypeStruct(w_uq_r.shape, F32),
                   jax.ShapeDtypeStruct(w_ukv_r.shape, F32), jax.ShapeDtypeStruct((8, Q_RANK), F32)],
        compiler_params=_cparams(dimension_semantics=("arbitrary",)),
    )(dq_m, dkn, dv, dkpe, dqb, dqb_far, dkb, dvb, dga, dgb, cq, ckv, qn, kvn, w_uq_r, w_ukv_r, qg, kvg, tabs)


def _grad_x(dz, dh, w_in_r, bt, ride=None):
    seq = dz.shape[0]
    n_steps = seq // bt

    def body(dz_ref, dh_ref, w_ref, gx_ref):
        gx_ref[...] = ALPHA * dz_ref[...] + lax.dot_general(
            dh_ref[...], w_ref[...], NT, preferred_element_type=F32)

    args = [dz, dh, w_in_r]
    in_specs = [pl.BlockSpec((bt, D_MODEL), lambda i: (i, 0)), pl.BlockSpec((bt, C_END), lambda i: (i, 0)),
                pl.BlockSpec(w_in_r.shape, lambda i: (0, 0))]
    out_specs = [pl.BlockSpec((bt, D_MODEL), lambda i: (i, 0))]
    out_shape = [jax.ShapeDtypeStruct((seq, D_MODEL), F32)]
    scratch = []
    body = _ride_along(body, ride, 0, len(args), len(out_shape), 0, n_steps)
    if ride is not None:
        args, in_specs = args + ride.args, in_specs + ride.in_specs
        out_specs, out_shape, scratch = out_specs + ride.out_specs, out_shape + ride.out_shape, ride.scratch
    return _pcall(
        body, name="grad_x", grid=(n_steps,),
        in_specs=in_specs, out_specs=out_specs, out_shape=out_shape, scratch_shapes=scratch,
        compiler_params=_cparams(dimension_semantics=("arbitrary",)),
    )(*args)


def _grad_w_in(x, dh, bt):
    seq = x.shape[0]
    shard = IN_WIDTH // N_DEV
    k_lo, k_hi = IN_SPLITS[0] + IN_SPLITS[1], IN_SPLITS[0] + IN_SPLITS[1] + MLA_ROPE

    def body(x_ref, dh_ref, out_ref, acc):
        i = pl.program_id(0)

        @pl.when(i == 0)
        def _():
            acc[...] = jnp.zeros(acc.shape, F32)

        acc[...] += lax.dot_general(x_ref[...].astype(BF16), dh_ref[...], TN, preferred_element_type=F32)

        @pl.when(i == seq // bt - 1)
        def _():
            kr = acc[:, C_KR:C_GA]
            kr = kr + pltpu.roll(kr, 96, 1) + pltpu.roll(kr, 64, 1) + pltpu.roll(kr, 32, 1)
            for d in range(N_DEV):
                lo, hi = shard * d, shard * (d + 1)
                pieces = []
                if lo < k_lo:
                    pieces.append(acc[:, lo:min(hi, k_lo)])
                if lo < k_hi and hi > k_lo:
                    pieces.append(kr[:, max(lo, k_lo) - k_lo:min(hi, k_hi) - k_lo])
                if hi > k_hi:
                    shift = C_GA - k_hi
                    pieces.append(acc[:, max(lo, k_hi) + shift:hi + shift])
                blk = pieces[0] if len(pieces) == 1 else jnp.concatenate(pieces, axis=1)
                out_ref[d] = blk.astype(BF16)

    return _pcall(
        body, name="grad_w_in", grid=(seq // bt,),
        in_specs=[pl.BlockSpec((bt, D_MODEL), lambda i: (i, 0)), pl.BlockSpec((bt, C_END), lambda i: (i, 0))],
        out_specs=pl.BlockSpec((N_DEV, D_MODEL, shard), lambda i: (0, 0, 0)),
        out_shape=jax.ShapeDtypeStruct((N_DEV, D_MODEL, shard), BF16),
        scratch_shapes=[pltpu.VMEM((D_MODEL, C_END), F32)],
        compiler_params=_cparams(dimension_semantics=("arbitrary",)),
    )(x, dh)


def _restore_grads(g_uq_r, g_ukv_r):
    g_uq = jnp.concatenate(
        [g_uq_r[:, :HW].reshape(Q_RANK, HEADS, MLA_NOPE), g_uq_r[:, HW:].reshape(Q_RANK, HEADS, MLA_ROPE)],
        axis=2).reshape(Q_RANK, HEADS * (MLA_NOPE + MLA_ROPE))
    g_ukv = jnp.concatenate(
        [g_ukv_r[:, :HW].reshape(KV_RANK, HEADS, MLA_NOPE), g_ukv_r[:, HW:].reshape(KV_RANK, HEADS, MLA_V)],
        axis=2).reshape(KV_RANK, HEADS * (MLA_NOPE + MLA_V))
    return g_uq, g_ukv


def _local_step(x, tgt, w_in_r, w_uq_r, w_ukv_r, w_out_rider, g_out_rider, reduce_rider, q_norm_g, kv_norm_g,
                ln_g, ln_b, bt=BLOCK_TOKENS, blk_m=BLOCK_MLA, blk_d=BLOCK_DIL):
    seq = x.shape[0]
    tabs = jnp.asarray(_rope_tables(seq))
    qg, kvg = q_norm_g.reshape(1, -1), kv_norm_g.reshape(1, -1)

    (cq, ckv, qn, kvn, qcat, kn, kpe, v, ga, gb, qb, kb, vb, knt, kpet, vt, kbt, vbt) = _fwd_proj(
        x, w_in_r, w_uq_r, w_ukv_r, qg, kvg, tabs, bt)

    nq_m, nq_d = seq // blk_m, seq // blk_d
    span_d = DIL_CONFIGS[-1][0] // blk_d
    bias_m, bias_d = jnp.asarray(_mla_bias_t(blk_m)), jnp.asarray(_dil_bias_t(blk_d, DIL_CONFIGS[-1][0]))
    oa, lse_a, w_out = _attn_fwd(
        "mla_fwd", qcat, kn, kpe, vt, bias_m, _steps(nq_m, nq_m, False, True), blk_m, ride=w_out_rider)

    far_dil = DIL_CONFIGS[-1][1]
    cls = seq // far_dil
    bias_near = jnp.asarray(_dil_bias_t(blk_d, DIL_NEAR))
    ob_near, lse_near = _attn_fwd(
        "dil_fwd", qb, kb, None, vbt, bias_near, _steps(nq_d, -(-DIL_NEAR // blk_d), False, False), blk_d)
    each = jnp.arange(far_dil, dtype=jnp.int32)
    steps_far = [each, each, jnp.zeros_like(each), jnp.ones_like(each), jnp.ones_like(each)]
    qb_c, kb_c, vb_c = (_to_classes(a, far_dil) for a in (qb, kb, vb))
    ob_far, lse_far = _attn_fwd(
        "dil_far_fwd", qb_c, kb_c, None, vb_c.T, jnp.asarray(_dil_far_bias_t(cls)), steps_far, cls)

    dz, doa, dob, dga, dgb, dst_a, dst_b, lse_b, g_out, small1 = _out_ln(
        oa, ob_near, _from_classes(ob_far, far_dil), lse_near, _lanes_from_classes(lse_far, far_dil), ga, gb, x, tgt,
        w_out.reshape(D_MODEL, D_MODEL), ln_g.reshape(1, -1), ln_b.reshape(1, -1), bt)

    dq_m, dkn, dkpe, dv, g_out_recv = _attn_bwd(
        "mla_bwd", qcat, kn, kpe, v, knt, kpet, bias_m, doa, lse_a, dst_a, _steps(nq_m, nq_m, True, True), blk_m,
        ride=g_out_rider(g_out.reshape(N_DEV, D_MODEL // N_DEV, D_MODEL)))
    dqb, dkb_near, dvb_near = _attn_bwd(
        "dil_bwd", qb, kb, None, vb, kbt, None, bias_near, dob, lse_b, dst_b,
        _steps(nq_d, -(-DIL_NEAR // blk_d), True, False), blk_d)
    dqb_far, dkb_far, dvb_far = _attn_bwd(
        "dil_far_bwd", qb_c, kb_c, None, vb_c, kb_c.T, None, jnp.asarray(_dil_far_bias_t(cls)),
        _to_classes(dob, far_dil), _lanes_to_classes(lse_b, far_dil), _lanes_to_classes(dst_b, far_dil), steps_far, cls)
    dqb_far = _from_classes(dqb_far.transpose(0, 2, 1).reshape(seq, HW), far_dil)
    dkb = dkb_near + _from_classes(dkb_far, far_dil)
    dvb = (dvb_near.astype(F32) + _from_classes(dvb_far, far_dil).astype(F32)).astype(BF16)

    dh, g_uq_r, g_ukv_r, small2 = _bwd_mid(
        dq_m, dkn, dv, dkpe, dqb, dqb_far, dkb, dvb, dga, dgb, cq, ckv, qn, kvn, w_uq_r, w_ukv_r, qg, kvg, tabs, bt)
    g_in = _grad_w_in(x, dh, bt)
    g_uq, g_ukv = _restore_grads(g_uq_r, g_ukv_r)
    grads3 = [g_in] + [g.astype(BF16) for g in (
        g_uq.reshape(Q_RANK, N_DEV, -1).transpose(1, 0, 2), g_ukv.reshape(KV_RANK, N_DEV, -1).transpose(1, 0, 2))]
    small_part = _small_rows(small1[0], small1[1], small2[0, :Q_RANK], small2[1, :KV_RANK], small1[2])
    grad_x, *reduced = _grad_x(
        dz, dh, w_in_r, bt, ride=reduce_rider(grads3, g_out_recv, small_part, min(2, seq // bt - 1)))
    return grad_x, reduced


MESH_ID = pl.DeviceIdType.MESH
SHARD_SHAPES = ((D_MODEL, IN_WIDTH // N_DEV), (Q_RANK, 768 // N_DEV), (KV_RANK, 1024 // N_DEV), (D_MODEL // N_DEV, D_MODEL))
ADAM_ROWS = (32, 128, 128, 16)


def _me():
    x, y, c = lax.axis_index("x"), lax.axis_index("y"), lax.axis_index("c")
    return x, y, c, 4 * x + 2 * y + c


def _peer(k):
    x, y, c, _ = _me()
    px = 1 - x if (k >> 2) & 1 else x
    py = 1 - y if (k >> 1) & 1 else y
    pc = 1 - c if k & 1 else c
    return (px, py, pc), 4 * px + 2 * py + pc


def _all_gather_weights(shards):
    n = len(shards)
    shard = IN_WIDTH // N_DEV
    k_lo = IN_SPLITS[0] + IN_SPLITS[1]
    k_hi = k_lo + MLA_ROPE

    def body(*refs):
        ins = refs[:n]
        win_ref, wuq_ref, wukv_ref = refs[n:2 * n]
        bufs = refs[2 * n:3 * n]
        send_sems, recv_sems = refs[3 * n:]
        x, y, c, me = _me()
        here, sibling = (x, y, c), (x, y, 1 - c)
        chips = [(1 - x, y), (x, 1 - y), (1 - x, 1 - y)]
        for t in range(n):
            bufs[t][me] = ins[t][...].astype(BF16)

        def copy(t, k, px, py, pc, to):
            blk = bufs[t].at[4 * px + 2 * py + pc]
            return pltpu.make_async_remote_copy(
                src_ref=blk, dst_ref=blk, send_sem=send_sems.at[t, k], recv_sem=recv_sems.at[t, k],
                device_id=to, device_id_type=MESH_ID)

        first = []
        for t in range(n):
            first.append(copy(t, 0, x, y, c, sibling))
            for j, (px, py) in enumerate(chips):
                first.append(copy(t, 1 + j, x, y, c, (px, py, c)))
        for cp in first:
            cp.start()
        passed = []
        for j, (px, py) in enumerate(chips):
            for t in range(n):
                copy(t, 1 + j, px, py, c, here).wait_recv()
                cp = copy(t, 4 + j, px, py, c, sibling)
                cp.start()
                passed.append(cp)
        for t in range(n):
            copy(t, 0, x, y, 1 - c, here).wait_recv()
        for j, (px, py) in enumerate(chips):
            for t in range(n):
                copy(t, 4 + j, px, py, 1 - c, here).wait_recv()
        for cp in first + passed:
            cp.wait_send()

        a_in, a_uq, a_ukv = bufs
        for d in range(N_DEV):
            lo, hi = shard * d, shard * (d + 1)
            if lo < k_lo:
                win_ref[:, lo:min(hi, k_lo)] = a_in[d, :, 0:min(hi, k_lo) - lo]
            if lo < k_hi and hi > k_lo:
                kr = a_in[d, :, k_lo - lo:k_hi - lo]
                for rep in range(4):
                    win_ref[:, C_KR + MLA_ROPE * rep:C_KR + MLA_ROPE * (rep + 1)] = kr
            if hi > k_hi:
                src = max(lo, k_hi)
                win_ref[:, src + C_GA - k_hi:hi + C_GA - k_hi] = a_in[d, :, src - lo:hi - lo]
        for h in range(HEADS):
            wuq_ref[:, MLA_NOPE * h:MLA_NOPE * (h + 1)] = a_uq[h, :, :MLA_NOPE]
            wuq_ref[:, HW + MLA_ROPE * h:HW + MLA_ROPE * (h + 1)] = a_uq[h, :, MLA_NOPE:]
            wukv_ref[:, MLA_NOPE * h:MLA_NOPE * (h + 1)] = a_ukv[h, :, :MLA_NOPE]
            wukv_ref[:, HW + MLA_V * h:HW + MLA_V * (h + 1)] = a_ukv[h, :, MLA_NOPE:]

    vmem = pl.BlockSpec(memory_space=pltpu.VMEM)
    return _pcall(
        body, name="gather_weights",
        in_specs=[vmem] * n, out_specs=[vmem] * n,
        out_shape=[jax.ShapeDtypeStruct((D_MODEL, C_END), BF16), jax.ShapeDtypeStruct((Q_RANK, QW), BF16),
                   jax.ShapeDtypeStruct((KV_RANK, 2 * HW), BF16)],
        scratch_shapes=[pltpu.VMEM((N_DEV,) + s, BF16) for s in SHARD_SHAPES[:n]]
        + [pltpu.SemaphoreType.DMA((n, N_DEV - 1)), pltpu.SemaphoreType.DMA((n, N_DEV - 1))],
        compiler_params=_cparams(),
    )(*shards)


def _gather_w_out_rider(w_out):
    def copies(full_ref, stage, send_sems, recv_sems):
        me = _me()[3]
        out = []
        for k in range(1, N_DEV):
            peer, pidx = _peer(k)
            send = pltpu.make_async_remote_copy(
                src_ref=stage, dst_ref=full_ref.at[me], send_sem=send_sems.at[k - 1], recv_sem=recv_sems.at[k - 1],
                device_id=peer, device_id_type=MESH_ID)
            recv = pltpu.make_async_remote_copy(
                src_ref=stage, dst_ref=full_ref.at[pidx], send_sem=send_sems.at[k - 1], recv_sem=recv_sems.at[k - 1],
                device_id=peer, device_id_type=MESH_ID)
            out.append((send, recv))
        return out

    def start(ins, outs, scr):
        stage, send_sems, recv_sems, own_sem = scr
        stage[...] = ins[0][...].astype(BF16)
        pltpu.make_async_copy(stage, outs[0].at[_me()[3]], own_sem).start()
        for send, _ in copies(outs[0], stage, send_sems, recv_sems):
            send.start()

    def finish(ins, outs, scr):
        stage, send_sems, recv_sems, own_sem = scr
        pltpu.make_async_copy(stage, outs[0].at[_me()[3]], own_sem).wait()
        pairs = copies(outs[0], stage, send_sems, recv_sems)
        for _, recv in pairs:
            recv.wait_recv()
        for send, _ in pairs:
            send.wait_send()

    shape = SHARD_SHAPES[3]
    return Rider(
        args=[w_out], in_specs=[pl.BlockSpec(shape, lambda t, *_: (0, 0))],
        out_shape=[jax.ShapeDtypeStruct((N_DEV,) + shape, BF16)], out_specs=[pl.BlockSpec(memory_space=pl.ANY)],
        scratch=[pltpu.VMEM(shape, BF16), pltpu.SemaphoreType.DMA((N_DEV - 1,)), pltpu.SemaphoreType.DMA((N_DEV - 1,)),
                 pltpu.SemaphoreType.DMA],
        start=start, finish=finish)


def _scatter_g_out_rider(blocks):
    def copies(src_ref, dst_ref, send_sems, recv_sems):
        out = []
        for k in range(1, N_DEV):
            peer, pidx = _peer(k)
            out.append(pltpu.make_async_remote_copy(
                src_ref=src_ref.at[pidx], dst_ref=dst_ref.at[k], send_sem=send_sems.at[k - 1],
                recv_sem=recv_sems.at[k - 1], device_id=peer, device_id_type=MESH_ID))
        return out

    def start(ins, outs, scr):
        send_sems, recv_sems, own_sem = scr
        pltpu.make_async_copy(ins[0].at[_me()[3]], outs[0].at[0], own_sem).start()
        for cp in copies(ins[0], outs[0], send_sems, recv_sems):
            cp.start()

    def finish(ins, outs, scr):
        send_sems, recv_sems, own_sem = scr
        pltpu.make_async_copy(ins[0].at[_me()[3]], outs[0].at[0], own_sem).wait()
        for cp in copies(ins[0], outs[0], send_sems, recv_sems):
            cp.wait()

    hbm = pl.BlockSpec(memory_space=pl.ANY)
    return Rider(
        args=[blocks], in_specs=[hbm], out_shape=[jax.ShapeDtypeStruct(blocks.shape, blocks.dtype)], out_specs=[hbm],
        scratch=[pltpu.SemaphoreType.DMA((N_DEV - 1,)), pltpu.SemaphoreType.DMA((N_DEV - 1,)), pltpu.SemaphoreType.DMA],
        start=start, finish=finish)


def _adamw(w, g, m, v):
    m = ADAM_B1 * m + (1.0 - ADAM_B1) * g
    v = ADAM_B2 * v + (1.0 - ADAM_B2) * jnp.square(g)
    m_hat = m / (1.0 - ADAM_B1 ** ADAM_STEP)
    v_hat = v / (1.0 - ADAM_B2 ** ADAM_STEP)
    delta = -ADAM_LR * (m_hat / (jnp.sqrt(v_hat) + ADAM_EPS) + ADAM_WD * w)
    return delta, m, v


def _reduce_grads_rider(grads3, arrived, small_part, mid_step):
    n = len(grads3)

    class Refs:
        def __init__(self, ins, outs, scr):
            self.g3, self.arr, self.sp = ins[0:n], ins[n], ins[n + 1]
            self.gsum, self.gsum_out, self.ssum = outs[0:n], outs[n], outs[n + 1]
            self.own, self.sib, self.part, self.ici = scr[0:n], scr[n:2 * n], scr[2 * n:3 * n], scr[3 * n:4 * n]
            self.rsmall = scr[4 * n]
            (self.loc_sems, self.d2d_send, self.d2d_recv, self.ici_send, self.ici_recv,
             self.sm_send, self.sm_recv) = scr[4 * n + 1:]
            self.x, self.y, self.c, self.me = _me()
            self.chips = [(1 - self.x, self.y), (self.x, 1 - self.y), (1 - self.x, 1 - self.y)]

        def small(self):
            return [pltpu.make_async_remote_copy(
                src_ref=self.rsmall.at[0], dst_ref=self.rsmall.at[k], send_sem=self.sm_send.at[k - 1],
                recv_sem=self.sm_recv.at[k - 1], device_id=_peer(k)[0], device_id_type=MESH_ID)
                for k in range(1, N_DEV)]

        def level1(self):
            local, to_sib = [], []
            for t in range(n):
                for q in range(4):
                    local.append(pltpu.make_async_copy(
                        self.g3[t].at[2 * q + self.c], self.own[t].at[q], self.loc_sems.at[t, q]))
                    to_sib.append(pltpu.make_async_remote_copy(
                        src_ref=self.g3[t].at[2 * q + 1 - self.c], dst_ref=self.sib[t].at[q],
                        send_sem=self.d2d_send.at[t, q], recv_sem=self.d2d_recv.at[t, q],
                        device_id=(self.x, self.y, 1 - self.c), device_id_type=MESH_ID))
            return local, to_sib

        def level2(self):
            return [pltpu.make_async_remote_copy(
                src_ref=self.part[t].at[2 * px + py], dst_ref=self.ici[t].at[j], send_sem=self.ici_send.at[t, j],
                recv_sem=self.ici_recv.at[t, j], device_id=(px, py, self.c), device_id_type=MESH_ID)
                for t in range(n) for j, (px, py) in enumerate(self.chips)]

    def chunks(t, fn):
        rows = ADAM_ROWS[t]

        def step(i, carry):
            fn(pl.ds(pl.multiple_of(i * rows, rows), rows))
            return carry

        lax.fori_loop(0, SHARD_SHAPES[t][0] // rows, step, 0)

    def start(*refs):
        r = Refs(*refs)
        r.rsmall[0] = r.sp[...]
        local, to_sib = r.level1()
        for cp in r.small() + local + to_sib:
            cp.start()

    def middle(*refs):
        r = Refs(*refs)
        local, to_sib = r.level1()
        for cp in local:
            cp.wait()
        for cp in to_sib:
            cp.wait_recv()
        my_chip = 2 * r.x + r.y
        for t in range(n):
            def pair_sums(rows, t=t):
                for q in range(4):
                    r.part[t][q, rows, :] = (
                        r.own[t][q, rows, :].astype(F32) + r.sib[t][q, rows, :].astype(F32)).astype(BF16)
                r.gsum[t][rows, :] = r.own[t][my_chip, rows, :].astype(F32) + r.sib[t][my_chip, rows, :].astype(F32)

            chunks(t, pair_sums)
        for cp in r.level2():
            cp.start()

    def finish(*refs):
        r = Refs(*refs)

        def add_arrived(rows):
            g = r.arr[0, rows, :].astype(F32)
            for k in range(1, N_DEV):
                g = g + r.arr[k, rows, :].astype(F32)
            r.gsum_out[rows, :] = g

        chunks(3, add_arrived)
        to_chips = r.level2()
        for cp in to_chips:
            cp.wait_recv()
        for t in range(n):
            def add_chips(rows, t=t):
                g = r.gsum[t][rows, :]
                for j in range(3):
                    g = g + r.ici[t][j, rows, :].astype(F32)
                r.gsum[t][rows, :] = g

            chunks(t, add_chips)
        small = r.small()
        for cp in small:
            cp.wait_recv()
        tot = r.rsmall[r.me]
        for d in range(1, N_DEV):
            tot = tot + r.rsmall[jnp.bitwise_xor(r.me, d)]
        r.ssum[...] = tot
        for cp in small + r.level1()[1] + to_chips:
            cp.wait_send()

    hbm = pl.BlockSpec(memory_space=pl.ANY)
    dma = pltpu.SemaphoreType.DMA

    def whole(shape):
        return pl.BlockSpec(shape, lambda i: (0,) * len(shape))

    out_shapes = list(SHARD_SHAPES) + [(8, D_MODEL)]
    return Rider(
        args=list(grads3) + [arrived, small_part],
        in_specs=[hbm] * n + [whole(arrived.shape), whole(small_part.shape)],
        out_shape=[jax.ShapeDtypeStruct(s, F32) for s in out_shapes], out_specs=[whole(s) for s in out_shapes],
        scratch=[pltpu.VMEM((slots,) + s, BF16) for slots in (4, 4, 4, 3) for s in SHARD_SHAPES[:n]]
        + [pltpu.VMEM((N_DEV, 8, D_MODEL), F32), dma((n, 4)), dma((n, 4)), dma((n, 4)), dma((n, 3)), dma((n, 3)),
           dma((N_DEV - 1,)), dma((N_DEV - 1,))],
        start=start, finish=finish, stages=((mid_step, middle),))


def _adamw_update(grads, small_grad, wmv, small_wmv):
    def body(*refs):
        g_refs, sg_ref = refs[0:4], refs[4]
        wmv_refs = [refs[5 + 3 * t:8 + 3 * t] for t in range(4)]
        swmv_ref = refs[17]
        out_refs = [refs[18 + 4 * t:22 + 4 * t] for t in range(4)]
        sout_ref = refs[34]
        tot = sg_ref[...]
        delta, m, v = _adamw(swmv_ref[0], tot, swmv_ref[1], swmv_ref[2])
        sout_ref[0], sout_ref[1], sout_ref[2], sout_ref[3] = tot, delta, m, v
        for t in range(4):
            rows = ADAM_ROWS[t]
            w_ref, m_ref, v_ref = wmv_refs[t]
            g_out, d_out, m_out, v_out = out_refs[t]

            def step(i, carry, g_ref=g_refs[t], rows=rows, w_ref=w_ref, m_ref=m_ref, v_ref=v_ref,
                     g_out=g_out, d_out=d_out, m_out=m_out, v_out=v_out):
                r = pl.ds(pl.multiple_of(i * rows, rows), rows)
                g = g_ref[r, :]
                delta, m, v = _adamw(w_ref[r, :], g, m_ref[r, :], v_ref[r, :])
                g_out[r, :], d_out[r, :], m_out[r, :], v_out[r, :] = g, delta, m, v
                return carry

            lax.fori_loop(0, SHARD_SHAPES[t][0] // rows, step, 0)

    vmem = pl.BlockSpec(memory_space=pltpu.VMEM)
    flat_wmv = [a for trio in wmv for a in trio]
    return _pcall(
        body, name="adamw",
        in_specs=[vmem] * 18, out_specs=[vmem] * 17,
        out_shape=[jax.ShapeDtypeStruct(s, F32) for s in SHARD_SHAPES for _ in range(4)]
        + [jax.ShapeDtypeStruct((4, 8, D_MODEL), F32)],
        compiler_params=_cparams(),
    )(*grads, small_grad, *flat_wmv, small_wmv)


def _small_rows(ln_g, ln_b, q_norm_g, kv_norm_g, extra=None):
    pad = lambda a: jnp.pad(a, (0, D_MODEL - a.shape[0]))
    rows = [ln_g, ln_b, pad(q_norm_g), pad(kv_norm_g)] + ([] if extra is None else [extra])
    return jnp.pad(jnp.stack(rows), ((0, 8 - len(rows)), (0, 0)))


def kernel(x, w_in, q_norm_g, kv_norm_g, w_uq, w_ukv, w_out, ln_g, ln_b, loss_target, m_w_in, m_q_norm_g, m_kv_norm_g, m_w_uq, m_w_ukv, m_w_out, m_ln_g, m_ln_b, v_w_in, v_q_norm_g, v_kv_norm_g, v_w_uq, v_w_ukv, v_w_out, v_ln_g, v_ln_b):
    w_in_r, w_uq_r, w_ukv_r = _all_gather_weights([w_in, w_uq, w_ukv])
    grad_x, sums = _local_step(
        x[0], loss_target[0], w_in_r, w_uq_r, w_ukv_r, _gather_w_out_rider(w_out), _scatter_g_out_rider,
        _reduce_grads_rider, q_norm_g, kv_norm_g, ln_g, ln_b)
    small_wmv = jnp.stack([_small_rows(ln_g, ln_b, q_norm_g, kv_norm_g),
                           _small_rows(m_ln_g, m_ln_b, m_q_norm_g, m_kv_norm_g),
                           _small_rows(v_ln_g, v_ln_b, v_q_norm_g, v_kv_norm_g)])
    wmv = [(w_in, m_w_in, v_w_in), (w_uq, m_w_uq, v_w_uq), (w_ukv, m_w_ukv, v_w_ukv), (w_out, m_w_out, v_w_out)]
    res = _adamw_update(sums[:4], sums[4], wmv, small_wmv)
    big = [res[4 * t:4 * t + 4] for t in range(4)]
    small = res[16]
    loss = (0.5 / D_MODEL) * jnp.sum(small[0, 4])

    def group(kind):
        s = small[kind]
        return (big[0][kind], s[2, :Q_RANK], s[3, :KV_RANK], big[1][kind], big[2][kind], big[3][kind], s[0], s[1])

    return (loss, grad_x[None], *group(0), *group(1), *group(2), *group(3))
```

```python
import functools
from typing import Callable, NamedTuple

import numpy as np
import jax
import jax.numpy as jnp
from jax import lax
from jax.experimental import pallas as pl
from jax.experimental.pallas import tpu as pltpu

F32 = jnp.float32
BF16 = jnp.bfloat16

D_MODEL = 1024
ROPE_THETA = 500000.0
NEG = -1e30
RMS_EPS = 1e-6
LN_EPS = 1e-5
HEADS = 8
MLA_NOPE = 64
MLA_ROPE = 32
MLA_V = 64
Q_RANK = 384
KV_RANK = 256
DIL_HEAD = 64
DIL_ROT = 16
DIL_CONFIGS = ((128, 1), (512, 4), (2048, 16))
DIL_NEAR = 512
HW = HEADS * 64
QW = HW + HEADS * MLA_ROPE
IN_SPLITS = (Q_RANK, KV_RANK, MLA_ROPE, HW, HW, HW, HW, HW)
IN_WIDTH = sum(IN_SPLITS)
ALPHA = 2.0 ** 0.25
MLA_SCALE = (MLA_NOPE + MLA_ROPE) ** -0.5
DIL_SCALE = DIL_HEAD ** -0.5
LOG2E = 1.4426950408889634
LN2 = 0.6931471805599453

ADAM_LR = 0.001
ADAM_B1 = 0.9
ADAM_B2 = 0.999
ADAM_EPS = 1e-08
ADAM_WD = 0.01
ADAM_STEP = 10

N_DEV = 8
LANES = 128
VMEM_LIMIT = 56 * 1024 * 1024
BLOCK_TOKENS = 512
BLOCK_MLA = 512
BLOCK_DIL = 512

C_CQ, C_CKV, C_KR, C_GA, C_QB, C_KB, C_VB, C_GB, C_END = 0, 384, 640, 768, 1280, 1792, 2304, 2816, 3328

NT = (((1,), (1,)), ((), ()))
TN = (((0,), (0,)), ((), ()))


def _pcall(body, **kw):
    return pl.pallas_call(body, **kw)


def _cparams(**kw):
    return pltpu.CompilerParams(vmem_limit_bytes=VMEM_LIMIT, **kw)


def _rope_tables(seq):
    def tabs(dim, period):
        half = dim // 2
        inv = np.float32(ROPE_THETA) ** (-np.arange(0, dim, 2, dtype=np.float32) / np.float32(dim))
        ang = np.arange(seq, dtype=np.float32)[:, None] * inv.astype(np.float32)[None, :]
        cos, sin = np.cos(ang).astype(np.float32), np.sin(ang).astype(np.float32)
        j = np.arange(LANES) % period
        f = j % half
        c = np.where(j < dim, cos[:, f], np.float32(1.0))
        s1 = np.where(j < half, -sin[:, f], np.float32(0.0))
        s2 = np.where((j >= half) & (j < dim), sin[:, f], np.float32(0.0))
        return [c, s1, s2]
    return np.stack(tabs(MLA_ROPE, MLA_ROPE) + tabs(DIL_ROT, DIL_HEAD)).astype(np.float32)


def _rope(t, c, s1, s2, half):
    return t * c + pltpu.roll(t, LANES - half, 1) * s1 + pltpu.roll(t, half, 1) * s2


def _rope_t(d, c, s1, s2, half):
    return d * c + pltpu.roll(d * s1, half, 1) + pltpu.roll(d * s2, LANES - half, 1)


def _rope_wide(fn, t, c, s1, s2, half):
    return jnp.concatenate(
        [fn(t[:, i:i + LANES], c, s1, s2, half) for i in range(0, t.shape[1], LANES)], axis=1)


def _mla_bias_t(blk):
    a = np.arange(blk)
    causal = np.where(a[:, None] <= a[None, :], 0.0, NEG)
    return np.stack([np.zeros((blk, blk)), causal]).astype(np.float32)


def _dil_bias_t(blk, reach):
    a = np.arange(blk)
    out = []
    for off in range(-(-reach // blk) + 1):
        delta = blk * off + a[None, :] - a[:, None]
        mult = np.zeros((blk, blk))
        for window, dil in DIL_CONFIGS:
            mult += (delta >= 0) & (delta % dil == 0) & (delta <= min(window, reach))
        out.append(np.where(mult > 0, np.log2(np.maximum(mult, 1.0)), NEG))
    return np.stack(out).astype(np.float32)


def _dil_far_bias_t(length):
    window, dil = DIL_CONFIGS[-1]
    a = np.arange(length)
    steps_back = a[None, :] - a[:, None]
    seen = (steps_back * dil > DIL_NEAR) & (steps_back * dil <= window)
    return np.where(seen, 0.0, NEG).astype(np.float32)[None]


def _to_classes(a, dil):
    s, w = a.shape
    return a.reshape(s // dil, dil, w).transpose(1, 0, 2).reshape(s, w)


def _from_classes(a, dil):
    s, w = a.shape
    return a.reshape(dil, s // dil, w).transpose(1, 0, 2).reshape(s, w)


def _lanes_to_classes(a, dil):
    h, s = a.shape
    return a.reshape(h, s // dil, dil).transpose(0, 2, 1).reshape(h, s)


def _lanes_from_classes(a, dil):
    h, s = a.shape
    return a.reshape(h, dil, s // dil).transpose(0, 2, 1).reshape(h, s)


def _steps(nq, span, by_key, diag_only_bias):
    rows = []
    if by_key:
        for ki in range(nq):
            hi = min(nq - 1, ki + span)
            for qi in range(ki, hi + 1):
                rows.append((qi, ki, int(qi == ki), int(qi == hi)))
    else:
        for qi in range(nq):
            lo = max(0, qi - span)
            for ki in range(lo, qi + 1):
                rows.append((qi, ki, int(ki == lo), int(ki == qi)))
    arr = np.array(rows, dtype=np.int32)
    off = arr[:, 0] - arr[:, 1]
    bias_idx = (off == 0).astype(np.int32) if diag_only_bias else off.astype(np.int32)
    return [jnp.asarray(v) for v in (arr[:, 0], arr[:, 1], bias_idx, arr[:, 2], arr[:, 3])]


def _by_class(val, out_ref, lanes_sc):
    n_cls, per = out_ref.shape[0], out_ref.shape[1]
    for c in range(val.shape[1] // LANES):
        lanes_sc[c] = val[:, LANES * c:LANES * (c + 1)]
        for r in range(n_cls):
            rows = lanes_sc.at[c][pl.ds(r, per, stride=n_cls), :]
            out_ref[r, :, LANES * c:LANES * (c + 1)] = rows.astype(out_ref.dtype)


def _in_sequence(ref, lanes_sc):
    n_cls, per, width = ref.shape
    for c in range(width // LANES):
        for r in range(n_cls):
            lanes_sc.at[c][pl.ds(r, per, stride=n_cls), :] = ref[r, :, LANES * c:LANES * (c + 1)].astype(F32)
    return jnp.concatenate([lanes_sc[c] for c in range(width // LANES)], axis=1)


def _fwd_proj(x, w_in_r, w_uq_r, w_ukv_r, qg, kvg, tabs, bt, n_cls):
    seq = x.shape[0]

    def body(x_ref, win_ref, wuq_ref, wukv_ref, qg_ref, kvg_ref, tab_ref,
             cq_ref, ckv_ref, qn_ref, kvn_ref, qcat_ref, kn_ref, kpe_ref, v_ref,
             ga_ref, gb_ref, qb_ref, kb_ref, vb_ref, knt_ref, kpet_ref, vt_ref, kbt_ref, vbt_ref,
             qbc_ref, kbc_ref, vbc_ref, lanes_sc):
        xb = x_ref[...].astype(BF16)

        def proj(lo, hi):
            return jnp.dot(xb, win_ref[:, lo:hi], preferred_element_type=F32)

        m_tabs = (tab_ref[0], tab_ref[1], tab_ref[2])
        d_tabs = (tab_ref[3], tab_ref[4], tab_ref[5])

        cq = proj(C_CQ, C_CKV)
        cq_ref[...] = cq
        qn = (cq * lax.rsqrt(jnp.mean(cq * cq, axis=1, keepdims=True) + RMS_EPS) * qg_ref[...]).astype(BF16)
        qn_ref[...] = qn
        q = jnp.dot(qn, wuq_ref[...], preferred_element_type=F32)
        qcat_ref[:, :HW] = (q[:, :HW] * (MLA_SCALE * LOG2E)).astype(BF16)
        qcat_ref[:, HW:] = (_rope_wide(_rope, q[:, HW:], *m_tabs, MLA_ROPE // 2) * (MLA_SCALE * LOG2E)).astype(BF16)

        ckv = proj(C_CKV, C_KR)
        ckv_ref[...] = ckv
        kvn = (ckv * lax.rsqrt(jnp.mean(ckv * ckv, axis=1, keepdims=True) + RMS_EPS) * kvg_ref[...]).astype(BF16)
        kvn_ref[...] = kvn
        kv = jnp.dot(kvn, wukv_ref[...], preferred_element_type=F32)
        kn_ref[...] = kv[:, :HW].astype(BF16)
        v_ref[...] = kv[:, HW:].astype(BF16)
        knt_ref[...] = kv[:, :HW].T.astype(BF16)
        vt_ref[...] = kv[:, HW:].T.astype(BF16)

        kpe = _rope(proj(C_KR, C_GA), *m_tabs, MLA_ROPE // 2)
        kpe_ref[...] = kpe.astype(BF16)
        kpet_ref[...] = kpe.T[:MLA_ROPE, :].astype(BF16)
        ga_ref[...] = proj(C_GA, C_QB)
        qb = _rope_wide(_rope, proj(C_QB, C_KB), *d_tabs, DIL_ROT // 2) * (DIL_SCALE * LOG2E)
        qb_ref[...] = qb.astype(BF16)
        _by_class(qb, qbc_ref, lanes_sc)
        kb = _rope_wide(_rope, proj(C_KB, C_VB), *d_tabs, DIL_ROT // 2)
        kb_ref[...] = kb.astype(BF16)
        kbt_ref[...] = kb.T.astype(BF16)
        _by_class(kb, kbc_ref, lanes_sc)
        vb = proj(C_VB, C_GB)
        vb_ref[...] = vb.astype(BF16)
        vbt_ref[...] = vb.T.astype(BF16)
        _by_class(vb, vbc_ref, lanes_sc)
        gb_ref[...] = proj(C_GB, C_END)

    def tok(width):
        return pl.BlockSpec((bt, width), lambda i: (i, 0))

    def tok_t(height):
        return pl.BlockSpec((height, bt), lambda i: (0, i))

    def full(a):
        return pl.BlockSpec(a.shape, lambda i: (0,) * a.ndim)

    outs = [(Q_RANK, F32), (KV_RANK, F32), (Q_RANK, BF16), (KV_RANK, BF16), (QW, BF16), (HW, BF16),
            (LANES, BF16), (HW, BF16), (HW, F32), (HW, F32), (HW, BF16), (HW, BF16), (HW, BF16)]
    outs_t = [HW, MLA_ROPE, HW, HW, HW]
    by_class = pl.BlockSpec((n_cls, bt // n_cls, HW), lambda i: (0, i, 0))
    return _pcall(
        body, name="fwd_proj", grid=(seq // bt,),
        in_specs=[tok(D_MODEL), full(w_in_r), full(w_uq_r), full(w_ukv_r), full(qg), full(kvg),
                  pl.BlockSpec((6, bt, LANES), lambda i: (0, i, 0))],
        out_specs=[tok(w) for w, _ in outs] + [tok_t(h) for h in outs_t] + [by_class] * 3,
        out_shape=[jax.ShapeDtypeStruct((seq, w), dt) for w, dt in outs]
        + [jax.ShapeDtypeStruct((h, seq), BF16) for h in outs_t]
        + [jax.ShapeDtypeStruct((n_cls, seq // n_cls, HW), BF16)] * 3,
        scratch_shapes=[pltpu.VMEM((HW // LANES, bt, LANES), F32)],
        compiler_params=_cparams(dimension_semantics=("arbitrary",)),
    )(x, w_in_r, w_uq_r, w_ukv_r, qg, kvg, tabs)


def _head_masks(lane, h):
    e, g = h % 2, h % 4
    me = (lane >= 64 * e) & (lane < 64 * e + 64)
    mr = (lane >= 32 * g) & (lane < 32 * g + 32)
    return me, mr


def _masked(mask, a):
    return jnp.where(mask, a, jnp.zeros_like(a))


def _pair_operands(q_ref, k_ref, kpe_ref, lane, j):
    cols = slice(LANES * j, LANES * (j + 1))
    qc = q_ref[:, cols]
    kj = k_ref[:, cols]
    kes = []
    for h in (2 * j, 2 * j + 1):
        me, mr = _head_masks(lane, h)
        ke = _masked(me, kj)
        if kpe_ref is not None:
            ke = jnp.concatenate([ke, _masked(mr, kpe_ref[...])], axis=1)
        kes.append(ke)
    if kpe_ref is not None:
        qc = jnp.concatenate([qc, q_ref[:, HW + LANES * (j // 2):HW + LANES * (j // 2 + 1)]], axis=1)
    return qc, kes


class Rider(NamedTuple):
    args: list
    in_specs: list
    out_shape: list
    out_specs: list
    scratch: list
    start: Callable
    finish: Callable
    stages: tuple = ()


def _ride_along(body, ride, n_prefetch, n_in, n_out, n_scratch, n_steps):
    if ride is None:
        return body

    def wrapped(*refs):
        pre, rest = refs[:n_prefetch], refs[n_prefetch:]
        a = n_in
        b = a + len(ride.args)
        c = b + n_out
        d = c + len(ride.out_shape)
        e = d + n_scratch
        mine = (rest[a:b], rest[c:d], rest[e:])
        t = pl.program_id(0)
        pl.when(t == 0)(lambda: ride.start(*mine))
        for at, stage in ride.stages:
            pl.when(t == at)(functools.partial(stage, *mine))
        body(*pre, *rest[:a], *rest[b:c], *rest[d:e])
        pl.when(t == n_steps - 1)(lambda: ride.finish(*mine))

    return wrapped


def _attn_fwd(name, q, k, kpe, vt, bias_t, steps, blk, ride=None, v_token_major=False):
    seq = q.shape[0]
    mla = kpe is not None
    n_steps = int(steps[0].shape[0])

    def body(qi_r, ki_r, bi_r, fi_r, la_r, *refs):
        if mla:
            q_ref, k_ref, kpe_ref, vt_ref, b_ref, o_ref, lse_ref, m_sc, l_sc, acc_sc, st_sc = refs
        else:
            q_ref, k_ref, vt_ref, b_ref, o_ref, lse_ref, m_sc, l_sc, acc_sc, st_sc = refs
        t = pl.program_id(0)

        @pl.when(fi_r[t] == 1)
        def _():
            m_sc[...] = jnp.full(m_sc.shape, NEG, F32)
            l_sc[...] = jnp.zeros(l_sc.shape, F32)
            acc_sc[...] = jnp.zeros(acc_sc.shape, F32)

        lane = lax.broadcasted_iota(jnp.int32, (1, LANES), 1)
        ones = jnp.ones((16, blk), BF16)
        if v_token_major:
            vt_all = vt_ref[...].astype(F32).T.astype(BF16)
            vt_rows = lambda rows: vt_all[rows, :]
        else:
            vt_rows = lambda rows: vt_ref[rows, :]

        def pair_scores(j, with_bias):
            qc, kes = _pair_operands(q_ref, k_ref, kpe_ref if mla else None, lane, j)
            st = lax.dot_general(jnp.concatenate(kes, axis=0), qc, NT, preferred_element_type=F32)
            maxes = []
            for e in range(2):
                se = st[e * blk:(e + 1) * blk]
                if with_bias:
                    se = se + b_ref[0]
                st_sc[j % 2, e * blk:(e + 1) * blk] = se
                maxes.append(jnp.max(se, axis=0, keepdims=True))
            return maxes

        def softmax_pv(h, col_max):
            st = st_sc[(h // 2) % 2, (h % 2) * blk:(h % 2 + 1) * blk]
            hrow = slice(h, h + 1)
            m_prev = m_sc[hrow, :]
            m_new = jnp.maximum(m_prev, col_max)
            alpha = jnp.exp2(m_prev - m_new)
            pt = jnp.exp2(st - m_new).astype(BF16)
            m_sc[hrow, :] = m_new
            rows = slice(64 * h, 64 * h + 64)
            res = jnp.dot(jnp.concatenate([vt_rows(rows), ones], axis=0), pt, preferred_element_type=F32)
            acc_sc[rows, :] = alpha * acc_sc[rows, :] + res[:64]
            l_sc[hrow, :] = alpha * l_sc[hrow, :] + res[64:65]

        def step(with_bias):
            maxes = pair_scores(0, with_bias)
            for j in range(HEADS // 2):
                cur = maxes
                if j + 1 < HEADS // 2:
                    maxes = pair_scores(j + 1, with_bias)
                softmax_pv(2 * j, cur[0])
                softmax_pv(2 * j + 1, cur[1])

        if mla:
            pl.when(bi_r[t] == 1)(lambda: step(True))
            pl.when(bi_r[t] == 0)(lambda: step(False))
        else:
            step(True)

        @pl.when(la_r[t] == 1)
        def _():
            for h in range(HEADS):
                rows = slice(64 * h, 64 * h + 64)
                acc_sc[rows, :] = acc_sc[rows, :] / l_sc[h:h + 1, :]
            o_ref[...] = acc_sc[...].T
            lse_ref[...] = m_sc[...] + jnp.log2(l_sc[...])

    qmap = lambda t, qi, ki, bi, fi, la: (qi[t], 0)
    kmap = lambda t, qi, ki, bi, fi, la: (ki[t], 0)
    in_specs = [pl.BlockSpec((blk, q.shape[1]), qmap), pl.BlockSpec((blk, HW), kmap)]
    args = [q, k]
    if mla:
        in_specs.append(pl.BlockSpec((blk, LANES), kmap))
        args.append(kpe)
    in_specs += [pl.BlockSpec((blk, HW), kmap) if v_token_major else
                 pl.BlockSpec((HW, blk), lambda t, qi, ki, bi, fi, la: (0, ki[t])),
                 pl.BlockSpec((1, blk, blk), lambda t, qi, ki, bi, fi, la: (bi[t], 0, 0))]
    args += [vt, bias_t]
    out_specs = [pl.BlockSpec((blk, HW), qmap), pl.BlockSpec((HEADS, blk), lambda t, qi, ki, bi, fi, la: (0, qi[t]))]
    out_shape = [jax.ShapeDtypeStruct((seq, HW), F32), jax.ShapeDtypeStruct((HEADS, seq), F32)]
    scratch = [pltpu.VMEM((HEADS, blk), F32), pltpu.VMEM((HEADS, blk), F32),
               pltpu.VMEM((HW, blk), F32), pltpu.VMEM((2, 2 * blk, blk), F32)]
    body = _ride_along(body, ride, 5, len(args), len(out_shape), len(scratch), n_steps)
    if ride is not None:
        args, in_specs = args + ride.args, in_specs + ride.in_specs
        out_specs, out_shape, scratch = out_specs + ride.out_specs, out_shape + ride.out_shape, scratch + ride.scratch
    return _pcall(
        body, name=name,
        grid_spec=pltpu.PrefetchScalarGridSpec(
            num_scalar_prefetch=5, grid=(n_steps,), in_specs=in_specs, out_specs=out_specs, scratch_shapes=scratch),
        out_shape=out_shape,
        compiler_params=_cparams(dimension_semantics=("arbitrary",)),
    )(*steps, *args)


def _attn_bwd(name, q, k, kpe, v, kt, kpet, bias_t, do, lse, dstat, steps, blk, ride=None, single_visit=False):
    assert not (single_visit and kpe is not None) and (kt is not None or single_visit)
    seq = q.shape[0]
    mla = kpe is not None
    qw = q.shape[1]
    n_steps = int(steps[0].shape[0])
    dk_dtype = BF16 if mla else F32

    def body(qi_r, ki_r, bi_r, fi_r, la_r, *refs):
        if mla:
            (q_ref, k_ref, kpe_ref, v_ref, kt_ref, kpet_ref, b_ref, do_ref, lse_ref, d_ref,
             dq_ref, dk_ref, dkpe_ref, dv_ref, dk_sc, dkpe_sc, dv_sc, st_sc, dpt_sc) = refs
        else:
            q_ref, k_ref, v_ref, *rest = refs
            kt_ref = rest.pop(0) if kt is not None else None
            b_ref, do_ref, lse_ref, d_ref, dq_ref, dk_ref, dv_ref, *rest = rest
            dq_tok_ref = rest.pop(0) if single_visit else None
            dk_sc, dv_sc, st_sc, dpt_sc = rest
        t = pl.program_id(0)

        @pl.when(t == 0)
        def _():
            dq_ref[...] = jnp.zeros(dq_ref.shape, F32)

        @pl.when(fi_r[t] == 1)
        def _():
            dk_sc[...] = jnp.zeros(dk_sc.shape, F32)
            dv_sc[...] = jnp.zeros(dv_sc.shape, F32)
            if mla:
                dkpe_sc[...] = jnp.zeros(dkpe_sc.shape, F32)

        qi = qi_r[t]
        lane = lax.broadcasted_iota(jnp.int32, (1, LANES), 1)
        if kt is None:
            kt_all = k_ref[...].astype(F32).T.astype(BF16)
            kt_rows = lambda rows: kt_all[rows, :]
        else:
            kt_rows = lambda rows: kt_ref[rows, :]

        def pair_matmuls(j):
            cols = slice(LANES * j, LANES * (j + 1))
            qc, kes = _pair_operands(q_ref, k_ref, kpe_ref if mla else None, lane, j)
            st_sc[j % 2] = lax.dot_general(jnp.concatenate(kes, axis=0), qc, NT, preferred_element_type=F32)
            vj = v_ref[:, cols]
            ves = [_masked(_head_masks(lane, h)[0], vj) for h in (2 * j, 2 * j + 1)]
            dpt_sc[j % 2] = lax.dot_general(
                jnp.concatenate(ves, axis=0), do_ref[:, cols], NT, preferred_element_type=F32)

        def pair_grads(j, with_bias):
            cols = slice(LANES * j, LANES * (j + 1))
            qj, doj = q_ref[:, cols], do_ref[:, cols]
            if mla:
                qr = q_ref[:, HW + LANES * (j // 2):HW + LANES * (j // 2 + 1)]
            pts, dsts, qms, doms = [], [], [], []
            for e in range(2):
                h = 2 * j + e
                me, mr = _head_masks(lane, h)
                st = st_sc[j % 2, e * blk:(e + 1) * blk]
                if with_bias:
                    st = st + b_ref[0]
                pt = jnp.exp2(st - lse_ref[h:h + 1, :])
                dst = (pt * (dpt_sc[j % 2, e * blk:(e + 1) * blk] - d_ref[h:h + 1, :])).astype(BF16)
                pts.append(pt.astype(BF16))
                dsts.append(dst)
                doms.append(_masked(me, doj))
                qm = _masked(me, qj)
                if mla:
                    qm = jnp.concatenate([qm, _masked(mr, qr)], axis=1)
                qms.append(qm)
                ktl = kt_rows(slice(64 * h, 64 * h + 64))
                if mla:
                    ktl = jnp.concatenate([ktl, kpet_ref[...]], axis=0)
                dqc = jnp.dot(ktl, dst, preferred_element_type=F32)
                dq_ref[qi, 64 * h:64 * h + 64, :] += dqc[:64]
                if mla:
                    dq_ref[qi, HW + MLA_ROPE * h:HW + MLA_ROPE * (h + 1), :] += dqc[64:]
            dv_sc[:, cols] += jnp.dot(
                jnp.concatenate(pts, axis=1), jnp.concatenate(doms, axis=0), preferred_element_type=F32)
            dkc = jnp.dot(jnp.concatenate(dsts, axis=1), jnp.concatenate(qms, axis=0), preferred_element_type=F32)
            dk_sc[:, cols] += dkc[:, :LANES]
            if mla:
                dkpe_sc[...] += dkc[:, LANES:]

        def step(with_bias):
            pair_matmuls(0)
            for j in range(HEADS // 2):
                if j + 1 < HEADS // 2:
                    pair_matmuls(j + 1)
                pair_grads(j, with_bias)

        if mla:
            pl.when(bi_r[t] == 1)(lambda: step(True))
            pl.when(bi_r[t] == 0)(lambda: step(False))
        else:
            step(True)
        if single_visit:
            dq_tok_ref[...] = dq_ref[qi].T

        @pl.when(la_r[t] == 1)
        def _():
            dk_ref[...] = (dk_sc[...] * LN2).astype(dk_ref.dtype)
            dv_ref[...] = dv_sc[...].astype(dv_ref.dtype)
            if mla:
                dkpe_ref[...] = dkpe_sc[...] * LN2

    qmap = lambda t, qi, ki, bi, fi, la: (qi[t], 0)
    kmap = lambda t, qi, ki, bi, fi, la: (ki[t], 0)
    qmap_t = lambda t, qi, ki, bi, fi, la: (0, qi[t])
    kmap_t = lambda t, qi, ki, bi, fi, la: (0, ki[t])
    in_specs = [pl.BlockSpec((blk, qw), qmap), pl.BlockSpec((blk, HW), kmap)]
    args = [q, k]
    if mla:
        in_specs.append(pl.BlockSpec((blk, LANES), kmap))
        args.append(kpe)
    in_specs.append(pl.BlockSpec((blk, HW), kmap))
    args.append(v)
    if kt is not None:
        in_specs.append(pl.BlockSpec((HW, blk), kmap_t))
        args.append(kt)
    if mla:
        in_specs.append(pl.BlockSpec((MLA_ROPE, blk), kmap_t))
        args.append(kpet)
    in_specs += [pl.BlockSpec((1, blk, blk), lambda t, qi, ki, bi, fi, la: (bi[t], 0, 0)),
                 pl.BlockSpec((blk, HW), qmap), pl.BlockSpec((HEADS, blk), qmap_t), pl.BlockSpec((HEADS, blk), qmap_t)]
    args += [bias_t, do, lse, dstat]
    dq_shape = (seq // blk, qw, blk)
    out_specs = [pl.BlockSpec(dq_shape, lambda t, qi, ki, bi, fi, la: (0, 0, 0)), pl.BlockSpec((blk, HW), kmap)]
    out_shape = [jax.ShapeDtypeStruct(dq_shape, F32), jax.ShapeDtypeStruct((seq, HW), dk_dtype)]
    scratch = [pltpu.VMEM((blk, HW), F32)]
    if mla:
        out_specs.append(pl.BlockSpec((blk, LANES), kmap))
        out_shape.append(jax.ShapeDtypeStruct((seq, LANES), F32))
        scratch.append(pltpu.VMEM((blk, LANES), F32))
    out_specs.append(pl.BlockSpec((blk, HW), kmap))
    out_shape.append(jax.ShapeDtypeStruct((seq, HW), BF16))
    if single_visit:
        out_specs.append(pl.BlockSpec((blk, qw), qmap))
        out_shape.append(jax.ShapeDtypeStruct((seq, qw), F32))
    scratch.append(pltpu.VMEM((blk, HW), F32))
    scratch += [pltpu.VMEM((2, 2 * blk, blk), F32), pltpu.VMEM((2, 2 * blk, blk), F32)]
    body = _ride_along(body, ride, 5, len(args), len(out_shape), len(scratch), n_steps)
    if ride is not None:
        args, in_specs = args + ride.args, in_specs + ride.in_specs
        out_specs, out_shape, scratch = out_specs + ride.out_specs, out_shape + ride.out_shape, scratch + ride.scratch
    return _pcall(
        body, name=name,
        grid_spec=pltpu.PrefetchScalarGridSpec(
            num_scalar_prefetch=5, grid=(n_steps,), in_specs=in_specs, out_specs=out_specs,
            scratch_shapes=scratch),
        out_shape=out_shape,
        compiler_params=_cparams(dimension_semantics=("arbitrary",)),
    )(*steps, *args)


def _out_ln(oa, ob_near, ob_far, lse_near, lse_far, ga, gb, x, tgt, w_out, ln_g, ln_b, bt):
    seq = x.shape[0]

    def body(oa_ref, obn_ref, obf_ref, lsen_ref, lsef_ref, ga_ref, gb_ref, x_ref, tgt_ref, w_ref, g_ref, b_ref,
             dz_ref, doa_ref, dob_ref, dga_ref, dgb_ref, da_ref, db_ref, lse_ref, gwb_ref, small_ref, dobc_ref,
             gw_ref, lanes_sc):
        i = pl.program_id(0)

        @pl.when(i == 0)
        def _():
            gw_ref[...] = jnp.zeros(gw_ref.shape, F32)
            small_ref[...] = jnp.zeros(small_ref.shape, F32)

        def gate(g):
            sig = 0.5 * jnp.tanh(0.5 * g) + 0.5
            return g * sig, sig * (1.0 + g * (1.0 - sig))

        lse_n, lse_f = lsen_ref[...], lsef_ref[...]
        top = jnp.maximum(lse_n, lse_f)
        e_n, e_f = jnp.exp2(lse_n - top), jnp.exp2(lse_f - top)
        lse_ref[...] = top + jnp.log2(e_n + e_f)
        inv = 1.0 / (e_n + e_f)
        head_row = lax.broadcasted_iota(jnp.int32, (2 * HEADS, HW), 0) % HEADS
        spread = (head_row == lax.broadcasted_iota(jnp.int32, (2 * HEADS, HW), 1) // 64).astype(BF16)

        def per_lane(w):
            hi = w.astype(BF16)
            lo = (w - hi.astype(F32)).astype(BF16)
            return lax.dot_general(jnp.concatenate([hi, lo], axis=0), spread, TN, preferred_element_type=F32)

        o_a = oa_ref[...]
        o_b = per_lane(e_n * inv) * obn_ref[...] + per_lane(e_f * inv) * _in_sequence(obf_ref, lanes_sc)
        g_a, g_b = ga_ref[...], gb_ref[...]
        sa, dsa = gate(g_a)
        sb, dsb = gate(g_b)
        mix = jnp.concatenate([o_a * sa, o_b * sb], axis=1).astype(BF16)
        z = ALPHA * x_ref[...] + jnp.dot(mix, w_ref[...], preferred_element_type=F32)
        mu = jnp.mean(z, axis=1, keepdims=True)
        zc = z - mu
        rstd = lax.rsqrt(jnp.mean(zc * zc, axis=1, keepdims=True) + LN_EPS)
        xhat = zc * rstd
        gam = g_ref[...]
        diff = xhat * gam + b_ref[...] - tgt_ref[...]
        dy = diff * (1.0 / D_MODEL)
        small_ref[0:1, :] += jnp.sum(dy * xhat, axis=0, keepdims=True)
        small_ref[1:2, :] += jnp.sum(dy, axis=0, keepdims=True)
        small_ref[2:3, :] += jnp.sum(diff * diff, axis=0, keepdims=True)
        dxh = dy * gam
        dz = rstd * (dxh - jnp.mean(dxh, axis=1, keepdims=True) - xhat * jnp.mean(dxh * xhat, axis=1, keepdims=True))
        dz_ref[...] = dz
        dzb = dz.astype(BF16)
        gw_ref[...] += lax.dot_general(mix, dzb, TN, preferred_element_type=F32)

        @pl.when(i == seq // bt - 1)
        def _():
            gwb_ref[...] = gw_ref[...].astype(BF16)

        dmix = lax.dot_general(dzb, w_ref[...], NT, preferred_element_type=F32)
        doa, dob = dmix[:, :HW] * sa, dmix[:, HW:] * sb
        doa_ref[...] = doa.astype(BF16)
        dob_ref[...] = dob.astype(BF16)
        _by_class(dob, dobc_ref, lanes_sc)
        dga_ref[...] = (dmix[:, :HW] * o_a * dsa).astype(BF16)
        dgb_ref[...] = (dmix[:, HW:] * o_b * dsb).astype(BF16)
        head_of = (lax.broadcasted_iota(jnp.int32, (2 * HW, LANES), 0) % HW) // 64
        ind = (head_of == lax.broadcasted_iota(jnp.int32, (2 * HW, LANES), 1)).astype(BF16)

        def head_sums(prod):
            hi = prod.astype(BF16)
            lo = (prod - hi.astype(F32)).astype(BF16)
            sums = jnp.dot(jnp.concatenate([hi, lo], axis=1), ind, preferred_element_type=F32)
            return sums.T[:HEADS, :]

        da_ref[...] = head_sums(doa * o_a)
        db_ref[...] = head_sums(dob * o_b)

    def tok(width):
        return pl.BlockSpec((bt, width), lambda i: (i, 0))

    def full(shape):
        return pl.BlockSpec(shape, lambda i: (0,) * len(shape))

    stat = pl.BlockSpec((HEADS, bt), lambda i: (0, i))
    n_cls = ob_far.shape[0]
    by_class = pl.BlockSpec((n_cls, bt // n_cls, HW), lambda i: (0, i, 0))
    return _pcall(
        body, name="out_ln", grid=(seq // bt,),
        in_specs=[tok(HW), tok(HW), by_class, stat, stat, tok(HW), tok(HW), tok(D_MODEL), tok(D_MODEL),
                  full((D_MODEL, D_MODEL)), full((1, D_MODEL)), full((1, D_MODEL))],
        out_specs=[tok(D_MODEL), tok(HW), tok(HW), tok(HW), tok(HW), stat, stat, stat,
                   full((D_MODEL, D_MODEL)), full((8, D_MODEL)), by_class],
        out_shape=[jax.ShapeDtypeStruct((seq, D_MODEL), F32)] + [jax.ShapeDtypeStruct((seq, HW), BF16)] * 4
        + [jax.ShapeDtypeStruct((HEADS, seq), F32)] * 3
        + [jax.ShapeDtypeStruct((D_MODEL, D_MODEL), BF16), jax.ShapeDtypeStruct((8, D_MODEL), F32),
           jax.ShapeDtypeStruct(ob_far.shape, BF16)],
        scratch_shapes=[pltpu.VMEM((D_MODEL, D_MODEL), F32), pltpu.VMEM((HW // LANES, bt, LANES), F32)],
        compiler_params=_cparams(dimension_semantics=("arbitrary",)),
    )(oa, ob_near, ob_far, lse_near, lse_far, ga, gb, x, tgt, w_out, ln_g, ln_b)


def _bwd_mid(dq_m, dkn, dv, dkpe, dqb, dkb, dvb, far, dga, dgb, cq, ckv, qn, kvn, w_uq_r, w_ukv_r, qg, kvg, tabs, bt):
    n_cls = far[0].shape[0]
    seq = cq.shape[0]

    def body(dqm_ref, dkn_ref, dv_ref, dkpe_ref, dqb_ref, dkb_ref, dvb_ref, dqf_ref, dkf_ref, dvf_ref, dga_ref, dgb_ref,
             cq_ref, ckv_ref, qn_ref, kvn_ref, wuq_ref, wukv_ref, qg_ref, kvg_ref, tab_ref,
             dh_ref, guq_ref, gukv_ref, small_ref, seq_sc):
        i = pl.program_id(0)

        @pl.when(i == 0)
        def _():
            guq_ref[...] = jnp.zeros(guq_ref.shape, F32)
            gukv_ref[...] = jnp.zeros(gukv_ref.shape, F32)
            small_ref[...] = jnp.zeros(small_ref.shape, F32)

        m_tabs = (tab_ref[0], tab_ref[1], tab_ref[2])
        d_tabs = (tab_ref[3], tab_ref[4], tab_ref[5])

        def rms_bwd(c, dn, gain):
            r = lax.rsqrt(jnp.mean(c * c, axis=1, keepdims=True) + RMS_EPS)
            u = dn * gain
            dc = r * u - c * (r * r * r) * jnp.mean(u * c, axis=1, keepdims=True)
            return dc, jnp.sum(dn * c * r, axis=0, keepdims=True)

        dqm = dqm_ref[0].T
        dq = jnp.concatenate(
            [dqm[:, :HW], _rope_wide(_rope_t, dqm[:, HW:], *m_tabs, MLA_ROPE // 2)], axis=1) * MLA_SCALE
        dq = dq.astype(BF16)
        guq_ref[...] += lax.dot_general(qn_ref[...], dq, TN, preferred_element_type=F32)
        dqn = lax.dot_general(dq, wuq_ref[...], NT, preferred_element_type=F32)
        dcq, gq = rms_bwd(cq_ref[...], dqn, qg_ref[...])
        small_ref[0:1, :] += gq

        dkv = jnp.concatenate([dkn_ref[...], dv_ref[...]], axis=1)
        gukv_ref[...] += lax.dot_general(kvn_ref[...], dkv, TN, preferred_element_type=F32)
        dkvn = lax.dot_general(dkv, wukv_ref[...], NT, preferred_element_type=F32)
        dckv, gkv = rms_bwd(ckv_ref[...], dkvn, kvg_ref[...])
        small_ref[1:2, :KV_RANK] += gkv

        dh_ref[:, C_CQ:C_CKV] = dcq.astype(BF16)
        dh_ref[:, C_CKV:C_KR] = dckv.astype(BF16)
        dh_ref[:, C_KR:C_GA] = _rope_t(dkpe_ref[...], *m_tabs, MLA_ROPE // 2).astype(BF16)
        dh_ref[:, C_GA:C_QB] = dga_ref[...]
        in_sequence = functools.partial(_in_sequence, lanes_sc=seq_sc)
        dqb = dqb_ref[0].T + in_sequence(dqf_ref)
        dh_ref[:, C_QB:C_KB] = (_rope_wide(_rope_t, dqb, *d_tabs, DIL_ROT // 2) * DIL_SCALE).astype(BF16)
        dkb = dkb_ref[...] + in_sequence(dkf_ref)
        dh_ref[:, C_KB:C_VB] = _rope_wide(_rope_t, dkb, *d_tabs, DIL_ROT // 2).astype(BF16)
        dh_ref[:, C_VB:C_GB] = (dvb_ref[...].astype(F32) + in_sequence(dvf_ref)).astype(BF16)
        dh_ref[:, C_GB:C_END] = dgb_ref[...]

    def tok(width):
        return pl.BlockSpec((bt, width), lambda i: (i, 0))

    def tok_t(a):
        per = a.shape[2] // bt
        return pl.BlockSpec((1, a.shape[1], bt), lambda i: (i // per, 0, i % per))

    def full(shape):
        return pl.BlockSpec(shape, lambda i: (0,) * len(shape))

    by_class = pl.BlockSpec((n_cls, bt // n_cls, HW), lambda i: (0, i, 0))
    return _pcall(
        body, name="bwd_mid", grid=(seq // bt,),
        in_specs=[tok_t(dq_m), tok(HW), tok(HW), tok(LANES), tok_t(dqb), tok(HW), tok(HW), by_class, by_class, by_class,
                  tok(HW), tok(HW),
                  tok(Q_RANK), tok(KV_RANK), tok(Q_RANK), tok(KV_RANK),
                  full(w_uq_r.shape), full(w_ukv_r.shape), full((1, Q_RANK)), full((1, KV_RANK)),
                  pl.BlockSpec((6, bt, LANES), lambda i: (0, i, 0))],
        out_specs=[tok(C_END), full(w_uq_r.shape), full(w_ukv_r.shape), full((8, Q_RANK))],
        out_shape=[jax.ShapeDtypeStruct((seq, C_END), BF16), jax.ShapeDtypeStruct(w_uq_r.shape, F32),
                   jax.ShapeDtypeStruct(w_ukv_r.shape, F32), jax.ShapeDtypeStruct((8, Q_RANK), F32)],
        scratch_shapes=[pltpu.VMEM((HW // LANES, bt, LANES), F32)],
        compiler_params=_cparams(dimension_semantics=("arbitrary",)),
    )(dq_m, dkn, dv, dkpe, dqb, dkb, dvb, *far, dga, dgb, cq, ckv, qn, kvn, w_uq_r, w_ukv_r, qg, kvg, tabs)


def _grad_x(dz, dh, w_in_r, bt, ride=None):
    seq = dz.shape[0]
    n_steps = seq // bt

    def body(dz_ref, dh_ref, w_ref, gx_ref):
        gx_ref[...] = ALPHA * dz_ref[...] + lax.dot_general(
            dh_ref[...], w_ref[...], NT, preferred_element_type=F32)

    args = [dz, dh, w_in_r]
    in_specs = [pl.BlockSpec((bt, D_MODEL), lambda i: (i, 0)), pl.BlockSpec((bt, C_END), lambda i: (i, 0)),
                pl.BlockSpec(w_in_r.shape, lambda i: (0, 0))]
    out_specs = [pl.BlockSpec((bt, D_MODEL), lambda i: (i, 0))]
    out_shape = [jax.ShapeDtypeStruct((seq, D_MODEL), F32)]
    scratch = []
    body = _ride_along(body, ride, 0, len(args), len(out_shape), 0, n_steps)
    if ride is not None:
        args, in_specs = args + ride.args, in_specs + ride.in_specs
        out_specs, out_shape, scratch = out_specs + ride.out_specs, out_shape + ride.out_shape, ride.scratch
    return _pcall(
        body, name="grad_x", grid=(n_steps,),
        in_specs=in_specs, out_specs=out_specs, out_shape=out_shape, scratch_shapes=scratch,
        compiler_params=_cparams(dimension_semantics=("arbitrary",)),
    )(*args)


def _grad_w_in(x, dh, bt):
    seq = x.shape[0]
    shard = IN_WIDTH // N_DEV
    k_lo, k_hi = IN_SPLITS[0] + IN_SPLITS[1], IN_SPLITS[0] + IN_SPLITS[1] + MLA_ROPE

    def body(x_ref, dh_ref, out_ref, acc):
        i = pl.program_id(0)

        @pl.when(i == 0)
        def _():
            acc[...] = jnp.zeros(acc.shape, F32)

        acc[...] += lax.dot_general(x_ref[...].astype(BF16), dh_ref[...], TN, preferred_element_type=F32)

        @pl.when(i == seq // bt - 1)
        def _():
            kr = acc[:, C_KR:C_GA]
            kr = kr + pltpu.roll(kr, 96, 1) + pltpu.roll(kr, 64, 1) + pltpu.roll(kr, 32, 1)
            for d in range(N_DEV):
                lo, hi = shard * d, shard * (d + 1)
                pieces = []
                if lo < k_lo:
                    pieces.append(acc[:, lo:min(hi, k_lo)])
                if lo < k_hi and hi > k_lo:
                    pieces.append(kr[:, max(lo, k_lo) - k_lo:min(hi, k_hi) - k_lo])
                if hi > k_hi:
                    shift = C_GA - k_hi
                    pieces.append(acc[:, max(lo, k_hi) + shift:hi + shift])
                blk = pieces[0] if len(pieces) == 1 else jnp.concatenate(pieces, axis=1)
                out_ref[d] = blk.astype(BF16)

    return _pcall(
        body, name="grad_w_in", grid=(seq // bt,),
        in_specs=[pl.BlockSpec((bt, D_MODEL), lambda i: (i, 0)), pl.BlockSpec((bt, C_END), lambda i: (i, 0))],
        out_specs=pl.BlockSpec((N_DEV, D_MODEL, shard), lambda i: (0, 0, 0)),
        out_shape=jax.ShapeDtypeStruct((N_DEV, D_MODEL, shard), BF16),
        scratch_shapes=[pltpu.VMEM((D_MODEL, C_END), F32)],
        compiler_params=_cparams(dimension_semantics=("arbitrary",)),
    )(x, dh)


def _restore_grads(g_uq_r, g_ukv_r):
    g_uq = jnp.concatenate(
        [g_uq_r[:, :HW].reshape(Q_RANK, HEADS, MLA_NOPE), g_uq_r[:, HW:].reshape(Q_RANK, HEADS, MLA_ROPE)],
        axis=2).reshape(Q_RANK, HEADS * (MLA_NOPE + MLA_ROPE))
    g_ukv = jnp.concatenate(
        [g_ukv_r[:, :HW].reshape(KV_RANK, HEADS, MLA_NOPE), g_ukv_r[:, HW:].reshape(KV_RANK, HEADS, MLA_V)],
        axis=2).reshape(KV_RANK, HEADS * (MLA_NOPE + MLA_V))
    return g_uq, g_ukv


def _local_step(x, tgt, w_in_r, w_uq_r, w_ukv_r, w_out_rider, g_out_rider, reduce_rider, q_norm_g, kv_norm_g,
                ln_g, ln_b, bt=BLOCK_TOKENS, blk_m=BLOCK_MLA, blk_d=BLOCK_DIL):
    seq = x.shape[0]
    tabs = jnp.asarray(_rope_tables(seq))
    qg, kvg = q_norm_g.reshape(1, -1), kv_norm_g.reshape(1, -1)

    far_dil = DIL_CONFIGS[-1][1]
    cls = seq // far_dil
    (cq, ckv, qn, kvn, qcat, kn, kpe, v, ga, gb, qb, kb, vb, knt, kpet, vt, kbt, vbt, qb_c, kb_c, vb_c) = _fwd_proj(
        x, w_in_r, w_uq_r, w_ukv_r, qg, kvg, tabs, bt, far_dil)
    qb_c, kb_c, vb_c = (a.reshape(seq, HW) for a in (qb_c, kb_c, vb_c))

    nq_m, nq_d = seq // blk_m, seq // blk_d
    span_d = DIL_CONFIGS[-1][0] // blk_d
    bias_m, bias_d = jnp.asarray(_mla_bias_t(blk_m)), jnp.asarray(_dil_bias_t(blk_d, DIL_CONFIGS[-1][0]))
    oa, lse_a, w_out = _attn_fwd(
        "mla_fwd", qcat, kn, kpe, vt, bias_m, _steps(nq_m, nq_m, False, True), blk_m, ride=w_out_rider)

    bias_near = jnp.asarray(_dil_bias_t(blk_d, DIL_NEAR))
    ob_near, lse_near = _attn_fwd(
        "dil_fwd", qb, kb, None, vbt, bias_near, _steps(nq_d, -(-DIL_NEAR // blk_d), False, False), blk_d)
    each = jnp.arange(far_dil, dtype=jnp.int32)
    steps_far = [each, each, jnp.zeros_like(each), jnp.ones_like(each), jnp.ones_like(each)]
    bias_far = jnp.asarray(_dil_far_bias_t(cls))
    ob_far, lse_far = _attn_fwd(
        "dil_far_fwd", qb_c, kb_c, None, vb_c, bias_far, steps_far, cls, v_token_major=True)

    dz, doa, dob, dga, dgb, dst_a, dst_b, lse_b, g_out, small1, dob_c = _out_ln(
        oa, ob_near, ob_far.reshape(far_dil, cls, HW), lse_near, _lanes_from_classes(lse_far, far_dil), ga, gb, x, tgt,
        w_out.reshape(D_MODEL, D_MODEL), ln_g.reshape(1, -1), ln_b.reshape(1, -1), bt)

    dq_m, dkn, dkpe, dv, g_out_recv = _attn_bwd(
        "mla_bwd", qcat, kn, kpe, v, knt, kpet, bias_m, doa, lse_a, dst_a, _steps(nq_m, nq_m, True, True), blk_m,
        ride=g_out_rider(g_out.reshape(N_DEV, D_MODEL // N_DEV, D_MODEL)))
    dqb, dkb_near, dvb_near = _attn_bwd(
        "dil_bwd", qb, kb, None, vb, kbt, None, bias_near, dob, lse_b, dst_b,
        _steps(nq_d, -(-DIL_NEAR // blk_d), True, False), blk_d)
    _, dkb_far, dvb_far, dqb_far = _attn_bwd(
        "dil_far_bwd", qb_c, kb_c, None, vb_c, None, None, bias_far, dob_c.reshape(seq, HW),
        _lanes_to_classes(lse_b, far_dil), _lanes_to_classes(dst_b, far_dil), steps_far, cls, single_visit=True)
    far = [a.reshape(far_dil, cls, HW) for a in (dqb_far, dkb_far, dvb_far)]

    dh, g_uq_r, g_ukv_r, small2 = _bwd_mid(
        dq_m, dkn, dv, dkpe, dqb, dkb_near, dvb_near, far, dga, dgb, cq, ckv, qn, kvn, w_uq_r, w_ukv_r, qg, kvg, tabs, bt)
    g_in = _grad_w_in(x, dh, bt)
    g_uq, g_ukv = _restore_grads(g_uq_r, g_ukv_r)
    grads3 = [g_in] + [g.astype(BF16) for g in (
        g_uq.reshape(Q_RANK, N_DEV, -1).transpose(1, 0, 2), g_ukv.reshape(KV_RANK, N_DEV, -1).transpose(1, 0, 2))]
    small_part = _small_rows(small1[0], small1[1], small2[0, :Q_RANK], small2[1, :KV_RANK], small1[2])
    grad_x, *reduced = _grad_x(
        dz, dh, w_in_r, bt, ride=reduce_rider(grads3, g_out_recv, small_part, min(2, seq // bt - 1)))
    return grad_x, reduced


MESH_ID = pl.DeviceIdType.MESH
SHARD_SHAPES = ((D_MODEL, IN_WIDTH // N_DEV), (Q_RANK, 768 // N_DEV), (KV_RANK, 1024 // N_DEV), (D_MODEL // N_DEV, D_MODEL))
ADAM_ROWS = (32, 128, 128, 16)


def _me():
    x, y, c = lax.axis_index("x"), lax.axis_index("y"), lax.axis_index("c")
    return x, y, c, 4 * x + 2 * y + c


def _peer(k):
    x, y, c, _ = _me()
    px = 1 - x if (k >> 2) & 1 else x
    py = 1 - y if (k >> 1) & 1 else y
    pc = 1 - c if k & 1 else c
    return (px, py, pc), 4 * px + 2 * py + pc


def _all_gather_weights(shards):
    n = len(shards)
    shard = IN_WIDTH // N_DEV
    k_lo = IN_SPLITS[0] + IN_SPLITS[1]
    k_hi = k_lo + MLA_ROPE

    def body(*refs):
        ins = refs[:n]
        win_ref, wuq_ref, wukv_ref = refs[n:2 * n]
        bufs = refs[2 * n:3 * n]
        send_sems, recv_sems = refs[3 * n:]
        x, y, c, me = _me()
        here, sibling = (x, y, c), (x, y, 1 - c)
        chips = [(1 - x, y), (x, 1 - y), (1 - x, 1 - y)]
        for t in range(n):
            bufs[t][me] = ins[t][...].astype(BF16)

        def copy(t, k, px, py, pc, to):
            blk = bufs[t].at[4 * px + 2 * py + pc]
            return pltpu.make_async_remote_copy(
                src_ref=blk, dst_ref=blk, send_sem=send_sems.at[t, k], recv_sem=recv_sems.at[t, k],
                device_id=to, device_id_type=MESH_ID)

        first = []
        for t in range(n):
            first.append(copy(t, 0, x, y, c, sibling))
            for j, (px, py) in enumerate(chips):
                first.append(copy(t, 1 + j, x, y, c, (px, py, c)))
        for cp in first:
            cp.start()
        passed = []
        for j, (px, py) in enumerate(chips):
            for t in range(n):
                copy(t, 1 + j, px, py, c, here).wait_recv()
                cp = copy(t, 4 + j, px, py, c, sibling)
                cp.start()
                passed.append(cp)
        for t in range(n):
            copy(t, 0, x, y, 1 - c, here).wait_recv()
        for j, (px, py) in enumerate(chips):
            for t in range(n):
                copy(t, 4 + j, px, py, 1 - c, here).wait_recv()
        for cp in first + passed:
            cp.wait_send()

        a_in, a_uq, a_ukv = bufs
        for d in range(N_DEV):
            lo, hi = shard * d, shard * (d + 1)
            if lo < k_lo:
                win_ref[:, lo:min(hi, k_lo)] = a_in[d, :, 0:min(hi, k_lo) - lo]
            if lo < k_hi and hi > k_lo:
                kr = a_in[d, :, k_lo - lo:k_hi - lo]
                for rep in range(4):
                    win_ref[:, C_KR + MLA_ROPE * rep:C_KR + MLA_ROPE * (rep + 1)] = kr
            if hi > k_hi:
                src = max(lo, k_hi)
                win_ref[:, src + C_GA - k_hi:hi + C_GA - k_hi] = a_in[d, :, src - lo:hi - lo]
        for h in range(HEADS):
            wuq_ref[:, MLA_NOPE * h:MLA_NOPE * (h + 1)] = a_uq[h, :, :MLA_NOPE]
            wuq_ref[:, HW + MLA_ROPE * h:HW + MLA_ROPE * (h + 1)] = a_uq[h, :, MLA_NOPE:]
            wukv_ref[:, MLA_NOPE * h:MLA_NOPE * (h + 1)] = a_ukv[h, :, :MLA_NOPE]
            wukv_ref[:, HW + MLA_V * h:HW + MLA_V * (h + 1)] = a_ukv[h, :, MLA_NOPE:]

    vmem = pl.BlockSpec(memory_space=pltpu.VMEM)
    return _pcall(
        body, name="gather_weights",
        in_specs=[vmem] * n, out_specs=[vmem] * n,
        out_shape=[jax.ShapeDtypeStruct((D_MODEL, C_END), BF16), jax.ShapeDtypeStruct((Q_RANK, QW), BF16),
                   jax.ShapeDtypeStruct((KV_RANK, 2 * HW), BF16)],
        scratch_shapes=[pltpu.VMEM((N_DEV,) + s, BF16) for s in SHARD_SHAPES[:n]]
        + [pltpu.SemaphoreType.DMA((n, N_DEV - 1)), pltpu.SemaphoreType.DMA((n, N_DEV - 1))],
        compiler_params=_cparams(),
    )(*shards)


def _gather_w_out_rider(w_out):
    def copies(full_ref, stage, send_sems, recv_sems):
        me = _me()[3]
        out = []
        for k in range(1, N_DEV):
            peer, pidx = _peer(k)
            send = pltpu.make_async_remote_copy(
                src_ref=stage, dst_ref=full_ref.at[me], send_sem=send_sems.at[k - 1], recv_sem=recv_sems.at[k - 1],
                device_id=peer, device_id_type=MESH_ID)
            recv = pltpu.make_async_remote_copy(
                src_ref=stage, dst_ref=full_ref.at[pidx], send_sem=send_sems.at[k - 1], recv_sem=recv_sems.at[k - 1],
                device_id=peer, device_id_type=MESH_ID)
            out.append((send, recv))
        return out

    def start(ins, outs, scr):
        stage, send_sems, recv_sems, own_sem = scr
        stage[...] = ins[0][...].astype(BF16)
        pltpu.make_async_copy(stage, outs[0].at[_me()[3]], own_sem).start()
        for send, _ in copies(outs[0], stage, send_sems, recv_sems):
            send.start()

    def finish(ins, outs, scr):
        stage, send_sems, recv_sems, own_sem = scr
        pltpu.make_async_copy(stage, outs[0].at[_me()[3]], own_sem).wait()
        pairs = copies(outs[0], stage, send_sems, recv_sems)
        for _, recv in pairs:
            recv.wait_recv()
        for send, _ in pairs:
            send.wait_send()

    shape = SHARD_SHAPES[3]
    return Rider(
        args=[w_out], in_specs=[pl.BlockSpec(shape, lambda t, *_: (0, 0))],
        out_shape=[jax.ShapeDtypeStruct((N_DEV,) + shape, BF16)], out_specs=[pl.BlockSpec(memory_space=pl.ANY)],
        scratch=[pltpu.VMEM(shape, BF16), pltpu.SemaphoreType.DMA((N_DEV - 1,)), pltpu.SemaphoreType.DMA((N_DEV - 1,)),
                 pltpu.SemaphoreType.DMA],
        start=start, finish=finish)


def _scatter_g_out_rider(blocks):
    def copies(src_ref, dst_ref, send_sems, recv_sems):
        out = []
        for k in range(1, N_DEV):
            peer, pidx = _peer(k)
            out.append(pltpu.make_async_remote_copy(
                src_ref=src_ref.at[pidx], dst_ref=dst_ref.at[k], send_sem=send_sems.at[k - 1],
                recv_sem=recv_sems.at[k - 1], device_id=peer, device_id_type=MESH_ID))
        return out

    def start(ins, outs, scr):
        send_sems, recv_sems, own_sem = scr
        pltpu.make_async_copy(ins[0].at[_me()[3]], outs[0].at[0], own_sem).start()
        for cp in copies(ins[0], outs[0], send_sems, recv_sems):
            cp.start()

    def finish(ins, outs, scr):
        send_sems, recv_sems, own_sem = scr
        pltpu.make_async_copy(ins[0].at[_me()[3]], outs[0].at[0], own_sem).wait()
        for cp in copies(ins[0], outs[0], send_sems, recv_sems):
            cp.wait()

    hbm = pl.BlockSpec(memory_space=pl.ANY)
    return Rider(
        args=[blocks], in_specs=[hbm], out_shape=[jax.ShapeDtypeStruct(blocks.shape, blocks.dtype)], out_specs=[hbm],
        scratch=[pltpu.SemaphoreType.DMA((N_DEV - 1,)), pltpu.SemaphoreType.DMA((N_DEV - 1,)), pltpu.SemaphoreType.DMA],
        start=start, finish=finish)


def _adamw(w, g, m, v):
    m = ADAM_B1 * m + (1.0 - ADAM_B1) * g
    v = ADAM_B2 * v + (1.0 - ADAM_B2) * jnp.square(g)
    m_hat = m / (1.0 - ADAM_B1 ** ADAM_STEP)
    v_hat = v / (1.0 - ADAM_B2 ** ADAM_STEP)
    delta = -ADAM_LR * (m_hat / (jnp.sqrt(v_hat) + ADAM_EPS) + ADAM_WD * w)
    return delta, m, v


def _reduce_grads_rider(grads3, arrived, small_part, mid_step):
    n = len(grads3)

    class Refs:
        def __init__(self, ins, outs, scr):
            self.g3, self.arr, self.sp = ins[0:n], ins[n], ins[n + 1]
            self.gsum, self.gsum_out, self.ssum = outs[0:n], outs[n], outs[n + 1]
            self.own, self.sib, self.part, self.ici = scr[0:n], scr[n:2 * n], scr[2 * n:3 * n], scr[3 * n:4 * n]
            self.rsmall = scr[4 * n]
            (self.loc_sems, self.d2d_send, self.d2d_recv, self.ici_send, self.ici_recv,
             self.sm_send, self.sm_recv) = scr[4 * n + 1:]
            self.x, self.y, self.c, self.me = _me()
            self.chips = [(1 - self.x, self.y), (self.x, 1 - self.y), (1 - self.x, 1 - self.y)]

        def small(self):
            return [pltpu.make_async_remote_copy(
                src_ref=self.rsmall.at[0], dst_ref=self.rsmall.at[k], send_sem=self.sm_send.at[k - 1],
                recv_sem=self.sm_recv.at[k - 1], device_id=_peer(k)[0], device_id_type=MESH_ID)
                for k in range(1, N_DEV)]

        def level1(self):
            local, to_sib = [], []
            for t in range(n):
                for q in range(4):
                    local.append(pltpu.make_async_copy(
                        self.g3[t].at[2 * q + self.c], self.own[t].at[q], self.loc_sems.at[t, q]))
                    to_sib.append(pltpu.make_async_remote_copy(
                        src_ref=self.g3[t].at[2 * q + 1 - self.c], dst_ref=self.sib[t].at[q],
                        send_sem=self.d2d_send.at[t, q], recv_sem=self.d2d_recv.at[t, q],
                        device_id=(self.x, self.y, 1 - self.c), device_id_type=MESH_ID))
            return local, to_sib

        def level2(self):
            return [pltpu.make_async_remote_copy(
                src_ref=self.part[t].at[2 * px + py], dst_ref=self.ici[t].at[j], send_sem=self.ici_send.at[t, j],
                recv_sem=self.ici_recv.at[t, j], device_id=(px, py, self.c), device_id_type=MESH_ID)
                for t in range(n) for j, (px, py) in enumerate(self.chips)]

    def chunks(t, fn):
        rows = ADAM_ROWS[t]

        def step(i, carry):
            fn(pl.ds(pl.multiple_of(i * rows, rows), rows))
            return carry

        lax.fori_loop(0, SHARD_SHAPES[t][0] // rows, step, 0)

    def start(*refs):
        r = Refs(*refs)
        r.rsmall[0] = r.sp[...]
        local, to_sib = r.level1()
        for cp in r.small() + local + to_sib:
            cp.start()

    def middle(*refs):
        r = Refs(*refs)
        local, to_sib = r.level1()
        for cp in local:
            cp.wait()
        for cp in to_sib:
            cp.wait_recv()
        my_chip = 2 * r.x + r.y
        for t in range(n):
            def pair_sums(rows, t=t):
                for q in range(4):
                    r.part[t][q, rows, :] = (
                        r.own[t][q, rows, :].astype(F32) + r.sib[t][q, rows, :].astype(F32)).astype(BF16)
                r.gsum[t][rows, :] = r.own[t][my_chip, rows, :].astype(F32) + r.sib[t][my_chip, rows, :].astype(F32)

            chunks(t, pair_sums)
        for cp in r.level2():
            cp.start()

    def finish(*refs):
        r = Refs(*refs)

        def add_arrived(rows):
            g = r.arr[0, rows, :].astype(F32)
            for k in range(1, N_DEV):
                g = g + r.arr[k, rows, :].astype(F32)
            r.gsum_out[rows, :] = g

        chunks(3, add_arrived)
        to_chips = r.level2()
        for cp in to_chips:
            cp.wait_recv()
        for t in range(n):
            def add_chips(rows, t=t):
                g = r.gsum[t][rows, :]
                for j in range(3):
                    g = g + r.ici[t][j, rows, :].astype(F32)
                r.gsum[t][rows, :] = g

            chunks(t, add_chips)
        small = r.small()
        for cp in small:
            cp.wait_recv()
        tot = r.rsmall[r.me]
        for d in range(1, N_DEV):
            tot = tot + r.rsmall[jnp.bitwise_xor(r.me, d)]
        r.ssum[...] = tot
        for cp in small + r.level1()[1] + to_chips:
            cp.wait_send()

    hbm = pl.BlockSpec(memory_space=pl.ANY)
    dma = pltpu.SemaphoreType.DMA

    def whole(shape):
        return pl.BlockSpec(shape, lambda i: (0,) * len(shape))

    out_shapes = list(SHARD_SHAPES) + [(8, D_MODEL)]
    return Rider(
        args=list(grads3) + [arrived, small_part],
        in_specs=[hbm] * n + [whole(arrived.shape), whole(small_part.shape)],
        out_shape=[jax.ShapeDtypeStruct(s, F32) for s in out_shapes], out_specs=[whole(s) for s in out_shapes],
        scratch=[pltpu.VMEM((slots,) + s, BF16) for slots in (4, 4, 4, 3) for s in SHARD_SHAPES[:n]]
        + [pltpu.VMEM((N_DEV, 8, D_MODEL), F32), dma((n, 4)), dma((n, 4)), dma((n, 4)), dma((n, 3)), dma((n, 3)),
           dma((N_DEV - 1,)), dma((N_DEV - 1,))],
        start=start, finish=finish, stages=((mid_step, middle),))


def _adamw_update(grads, small_grad, wmv, small_wmv):
    def body(*refs):
        g_refs, sg_ref = refs[0:4], refs[4]
        wmv_refs = [refs[5 + 3 * t:8 + 3 * t] for t in range(4)]
        swmv_ref = refs[17]
        out_refs = [refs[18 + 4 * t:22 + 4 * t] for t in range(4)]
        sout_ref = refs[34]
        tot = sg_ref[...]
        delta, m, v = _adamw(swmv_ref[0], tot, swmv_ref[1], swmv_ref[2])
        sout_ref[0], sout_ref[1], sout_ref[2], sout_ref[3] = tot, delta, m, v
        for t in range(4):
            rows = ADAM_ROWS[t]
            w_ref, m_ref, v_ref = wmv_refs[t]
            g_out, d_out, m_out, v_out = out_refs[t]

            def step(i, carry, g_ref=g_refs[t], rows=rows, w_ref=w_ref, m_ref=m_ref, v_ref=v_ref,
                     g_out=g_out, d_out=d_out, m_out=m_out, v_out=v_out):
                r = pl.ds(pl.multiple_of(i * rows, rows), rows)
                g = g_ref[r, :]
                delta, m, v = _adamw(w_ref[r, :], g, m_ref[r, :], v_ref[r, :])
                g_out[r, :], d_out[r, :], m_out[r, :], v_out[r, :] = g, delta, m, v
                return carry

            lax.fori_loop(0, SHARD_SHAPES[t][0] // rows, step, 0)

    vmem = pl.BlockSpec(memory_space=pltpu.VMEM)
    flat_wmv = [a for trio in wmv for a in trio]
    return _pcall(
        body, name="adamw",
        in_specs=[vmem] * 18, out_specs=[vmem] * 17,
        out_shape=[jax.ShapeDtypeStruct(s, F32) for s in SHARD_SHAPES for _ in range(4)]
        + [jax.ShapeDtypeStruct((4, 8, D_MODEL), F32)],
        compiler_params=_cparams(),
    )(*grads, small_grad, *flat_wmv, small_wmv)


def _small_rows(ln_g, ln_b, q_norm_g, kv_norm_g, extra=None):
    pad = lambda a: jnp.pad(a, (0, D_MODEL - a.shape[0]))
    rows = [ln_g, ln_b, pad(q_norm_g), pad(kv_norm_g)] + ([] if extra is None else [extra])
    return jnp.pad(jnp.stack(rows), ((0, 8 - len(rows)), (0, 0)))


def kernel(x, w_in, q_norm_g, kv_norm_g, w_uq, w_ukv, w_out, ln_g, ln_b, loss_target, m_w_in, m_q_norm_g, m_kv_norm_g, m_w_uq, m_w_ukv, m_w_out, m_ln_g, m_ln_b, v_w_in, v_q_norm_g, v_kv_norm_g, v_w_uq, v_w_ukv, v_w_out, v_ln_g, v_ln_b):
    w_in_r, w_uq_r, w_ukv_r = _all_gather_weights([w_in, w_uq, w_ukv])
    grad_x, sums = _local_step(
        x[0], loss_target[0], w_in_r, w_uq_r, w_ukv_r, _gather_w_out_rider(w_out), _scatter_g_out_rider,
        _reduce_grads_rider, q_norm_g, kv_norm_g, ln_g, ln_b)
    small_wmv = jnp.stack([_small_rows(ln_g, ln_b, q_norm_g, kv_norm_g),
                           _small_rows(m_ln_g, m_ln_b, m_q_norm_g, m_kv_norm_g),
                           _small_rows(v_ln_g, v_ln_b, v_q_norm_g, v_kv_norm_g)])
    wmv = [(w_in, m_w_in, v_w_in), (w_uq, m_w_uq, v_w_uq), (w_ukv, m_w_ukv, v_w_ukv), (w_out, m_w_out, v_w_out)]
    res = _adamw_update(sums[:4], sums[4], wmv, small_wmv)
    big = [res[4 * t:4 * t + 4] for t in range(4)]
    small = res[16]
    loss = (0.5 / D_MODEL) * jnp.sum(small[0, 4])

    def group(kind):
        s = small[kind]
        return (big[0][kind], s[2, :Q_RANK], s[3, :KV_RANK], big[1][kind], big[2][kind], big[3][kind], s[0], s[1])

    return (loss, grad_x[None], *group(0), *group(1), *group(2), *group(3))
```

```python
import functools
from typing import Callable, NamedTuple

import numpy as np
import jax
import jax.numpy as jnp
from jax import lax
from jax.experimental import pallas as pl
from jax.experimental.pallas import tpu as pltpu

F32 = jnp.float32
BF16 = jnp.bfloat16

D_MODEL = 1024
ROPE_THETA = 500000.0
NEG = -1e30
RMS_EPS = 1e-6
LN_EPS = 1e-5
HEADS = 8
MLA_NOPE = 64
MLA_ROPE = 32
MLA_V = 64
Q_RANK = 384
KV_RANK = 256
DIL_HEAD = 64
DIL_ROT = 16
DIL_CONFIGS = ((128, 1), (512, 4), (2048, 16))
DIL_NEAR = 512
HW = HEADS * 64
QW = HW + HEADS * MLA_ROPE
IN_SPLITS = (Q_RANK, KV_RANK, MLA_ROPE, HW, HW, HW, HW, HW)
IN_WIDTH = sum(IN_SPLITS)
ALPHA = 2.0 ** 0.25
MLA_SCALE = (MLA_NOPE + MLA_ROPE) ** -0.5
DIL_SCALE = DIL_HEAD ** -0.5
LOG2E = 1.4426950408889634
LN2 = 0.6931471805599453

ADAM_LR = 0.001
ADAM_B1 = 0.9
ADAM_B2 = 0.999
ADAM_EPS = 1e-08
ADAM_WD = 0.01
ADAM_STEP = 10

N_DEV = 8
LANES = 128
VMEM_LIMIT = 56 * 1024 * 1024
BLOCK_TOKENS = 512
BLOCK_MLA = 512
BLOCK_DIL = 512

C_CQ, C_CKV, C_KR, C_GA, C_QB, C_KB, C_VB, C_GB, C_END = 0, 384, 640, 768, 1280, 1792, 2304, 2816, 3328

NT = (((1,), (1,)), ((), ()))
TN = (((0,), (0,)), ((), ()))


def _pcall(body, **kw):
    return pl.pallas_call(body, **kw)


def _cparams(**kw):
    return pltpu.CompilerParams(vmem_limit_bytes=VMEM_LIMIT, **kw)


def _rope_tables(seq):
    def tabs(dim, period):
        half = dim // 2
        inv = np.float32(ROPE_THETA) ** (-np.arange(0, dim, 2, dtype=np.float32) / np.float32(dim))
        ang = np.arange(seq, dtype=np.float32)[:, None] * inv.astype(np.float32)[None, :]
        cos, sin = np.cos(ang).astype(np.float32), np.sin(ang).astype(np.float32)
        j = np.arange(LANES) % period
        f = j % half
        c = np.where(j < dim, cos[:, f], np.float32(1.0))
        s1 = np.where(j < half, -sin[:, f], np.float32(0.0))
        s2 = np.where((j >= half) & (j < dim), sin[:, f], np.float32(0.0))
        return [c, s1, s2]
    return np.stack(tabs(MLA_ROPE, MLA_ROPE) + tabs(DIL_ROT, DIL_HEAD)).astype(np.float32)


def _rope(t, c, s1, s2, half):
    return t * c + pltpu.roll(t, LANES - half, 1) * s1 + pltpu.roll(t, half, 1) * s2


def _rope_t(d, c, s1, s2, half):
    return d * c + pltpu.roll(d * s1, half, 1) + pltpu.roll(d * s2, LANES - half, 1)


def _rope_wide(fn, t, c, s1, s2, half):
    return jnp.concatenate(
        [fn(t[:, i:i + LANES], c, s1, s2, half) for i in range(0, t.shape[1], LANES)], axis=1)


def _mla_bias_t(blk):
    a = np.arange(blk)
    causal = np.where(a[:, None] <= a[None, :], 0.0, NEG)
    return np.stack([np.zeros((blk, blk)), causal]).astype(np.float32)


def _dil_bias_t(blk, reach):
    a = np.arange(blk)
    out = []
    for off in range(-(-reach // blk) + 1):
        delta = blk * off + a[None, :] - a[:, None]
        mult = np.zeros((blk, blk))
        for window, dil in DIL_CONFIGS:
            mult += (delta >= 0) & (delta % dil == 0) & (delta <= min(window, reach))
        out.append(np.where(mult > 0, np.log2(np.maximum(mult, 1.0)), NEG))
    return np.stack(out).astype(np.float32)


def _dil_far_bias_t(length):
    window, dil = DIL_CONFIGS[-1]
    a = np.arange(length)
    steps_back = a[None, :] - a[:, None]
    seen = (steps_back * dil > DIL_NEAR) & (steps_back * dil <= window)
    return np.where(seen, 0.0, NEG).astype(np.float32)[None]


def _to_classes(a, dil):
    s, w = a.shape
    return a.reshape(s // dil, dil, w).transpose(1, 0, 2).reshape(s, w)


def _from_classes(a, dil):
    s, w = a.shape
    return a.reshape(dil, s // dil, w).transpose(1, 0, 2).reshape(s, w)


def _lanes_to_classes(a, dil):
    h, s = a.shape
    return a.reshape(h, s // dil, dil).transpose(0, 2, 1).reshape(h, s)


def _lanes_from_classes(a, dil):
    h, s = a.shape
    return a.reshape(h, dil, s // dil).transpose(0, 2, 1).reshape(h, s)


def _steps(nq, span, by_key, diag_only_bias):
    rows = []
    if by_key:
        for ki in range(nq):
            hi = min(nq - 1, ki + span)
            for qi in range(ki, hi + 1):
                rows.append((qi, ki, int(qi == ki), int(qi == hi)))
    else:
        for qi in range(nq):
            lo = max(0, qi - span)
            for ki in range(lo, qi + 1):
                rows.append((qi, ki, int(ki == lo), int(ki == qi)))
    arr = np.array(rows, dtype=np.int32)
    off = arr[:, 0] - arr[:, 1]
    bias_idx = (off == 0).astype(np.int32) if diag_only_bias else off.astype(np.int32)
    return [jnp.asarray(v) for v in (arr[:, 0], arr[:, 1], bias_idx, arr[:, 2], arr[:, 3])]


def _by_class(val, out_ref, lanes_sc):
    n_cls, per = out_ref.shape[0], out_ref.shape[1]
    for c in range(val.shape[1] // LANES):
        lanes_sc[c] = val[:, LANES * c:LANES * (c + 1)]
        for r in range(n_cls):
            rows = lanes_sc.at[c][pl.ds(r, per, stride=n_cls), :]
            out_ref[r, :, LANES * c:LANES * (c + 1)] = rows.astype(out_ref.dtype)


def _in_sequence(ref, lanes_sc):
    n_cls, per, width = ref.shape
    for c in range(width // LANES):
        for r in range(n_cls):
            lanes_sc.at[c][pl.ds(r, per, stride=n_cls), :] = ref[r, :, LANES * c:LANES * (c + 1)].astype(F32)
    return jnp.concatenate([lanes_sc[c] for c in range(width // LANES)], axis=1)


def _fwd_proj(x, w_in_r, w_uq_r, w_ukv_r, qg, kvg, tabs, bt, n_cls):
    seq = x.shape[0]

    def body(x_ref, win_ref, wuq_ref, wukv_ref, qg_ref, kvg_ref, tab_ref,
             cq_ref, ckv_ref, qn_ref, kvn_ref, qcat_ref, kn_ref, kpe_ref, v_ref,
             ga_ref, gb_ref, qb_ref, kb_ref, vb_ref, knt_ref, kpet_ref, vt_ref, kbt_ref, vbt_ref,
             qbc_ref, kbc_ref, vbc_ref, lanes_sc):
        xb = x_ref[...].astype(BF16)

        def proj(lo, hi):
            return jnp.dot(xb, win_ref[:, lo:hi], preferred_element_type=F32)

        m_tabs = (tab_ref[0], tab_ref[1], tab_ref[2])
        d_tabs = (tab_ref[3], tab_ref[4], tab_ref[5])

        cq = proj(C_CQ, C_CKV)
        cq_ref[...] = cq
        qn = (cq * lax.rsqrt(jnp.mean(cq * cq, axis=1, keepdims=True) + RMS_EPS) * qg_ref[...]).astype(BF16)
        qn_ref[...] = qn
        q = jnp.dot(qn, wuq_ref[...], preferred_element_type=F32)
        qcat_ref[:, :HW] = (q[:, :HW] * (MLA_SCALE * LOG2E)).astype(BF16)
        qcat_ref[:, HW:] = (_rope_wide(_rope, q[:, HW:], *m_tabs, MLA_ROPE // 2) * (MLA_SCALE * LOG2E)).astype(BF16)

        ckv = proj(C_CKV, C_KR)
        ckv_ref[...] = ckv
        kvn = (ckv * lax.rsqrt(jnp.mean(ckv * ckv, axis=1, keepdims=True) + RMS_EPS) * kvg_ref[...]).astype(BF16)
        kvn_ref[...] = kvn
        kv = jnp.dot(kvn, wukv_ref[...], preferred_element_type=F32)
        kn_ref[...] = kv[:, :HW].astype(BF16)
        v_ref[...] = kv[:, HW:].astype(BF16)
        knt_ref[...] = kv[:, :HW].T.astype(BF16)
        vt_ref[...] = kv[:, HW:].T.astype(BF16)

        kpe = _rope(proj(C_KR, C_GA), *m_tabs, MLA_ROPE // 2)
        kpe_ref[...] = kpe.astype(BF16)
        kpet_ref[...] = kpe.T[:MLA_ROPE, :].astype(BF16)
        ga_ref[...] = proj(C_GA, C_QB)
        qb = _rope_wide(_rope, proj(C_QB, C_KB), *d_tabs, DIL_ROT // 2) * (DIL_SCALE * LOG2E)
        qb_ref[...] = qb.astype(BF16)
        _by_class(qb, qbc_ref, lanes_sc)
        kb = _rope_wide(_rope, proj(C_KB, C_VB), *d_tabs, DIL_ROT // 2)
        kb_ref[...] = kb.astype(BF16)
        kbt_ref[...] = kb.T.astype(BF16)
        _by_class(kb, kbc_ref, lanes_sc)
        vb = proj(C_VB, C_GB)
        vb_ref[...] = vb.astype(BF16)
        vbt_ref[...] = vb.T.astype(BF16)
        _by_class(vb, vbc_ref, lanes_sc)
        gb_ref[...] = proj(C_GB, C_END)

    def tok(width):
        return pl.BlockSpec((bt, width), lambda i: (i, 0))

    def tok_t(height):
        return pl.BlockSpec((height, bt), lambda i: (0, i))

    def full(a):
        return pl.BlockSpec(a.shape, lambda i: (0,) * a.ndim)

    outs = [(Q_RANK, F32), (KV_RANK, F32), (Q_RANK, BF16), (KV_RANK, BF16), (QW, BF16), (HW, BF16),
            (LANES, BF16), (HW, BF16), (HW, F32), (HW, F32), (HW, BF16), (HW, BF16), (HW, BF16)]
    outs_t = [HW, MLA_ROPE, HW, HW, HW]
    by_class = pl.BlockSpec((n_cls, bt // n_cls, HW), lambda i: (0, i, 0))
    return _pcall(
        body, name="fwd_proj", grid=(seq // bt,),
        in_specs=[tok(D_MODEL), full(w_in_r), full(w_uq_r), full(w_ukv_r), full(qg), full(kvg),
                  pl.BlockSpec((6, bt, LANES), lambda i: (0, i, 0))],
        out_specs=[tok(w) for w, _ in outs] + [tok_t(h) for h in outs_t] + [by_class] * 3,
        out_shape=[jax.ShapeDtypeStruct((seq, w), dt) for w, dt in outs]
        + [jax.ShapeDtypeStruct((h, seq), BF16) for h in outs_t]
        + [jax.ShapeDtypeStruct((n_cls, seq // n_cls, HW), BF16)] * 3,
        scratch_shapes=[pltpu.VMEM((HW // LANES, bt, LANES), F32)],
        compiler_params=_cparams(dimension_semantics=("arbitrary",)),
    )(x, w_in_r, w_uq_r, w_ukv_r, qg, kvg, tabs)


def _head_masks(lane, h):
    e, g = h % 2, h % 4
    me = (lane >= 64 * e) & (lane < 64 * e + 64)
    mr = (lane >= 32 * g) & (lane < 32 * g + 32)
    return me, mr


def _masked(mask, a):
    return jnp.where(mask, a, jnp.zeros_like(a))


def _pair_operands(q_ref, k_ref, kpe_ref, lane, j, ks=slice(None), qs=slice(None)):
    cols = slice(LANES * j, LANES * (j + 1))
    qc = q_ref[qs, cols]
    kj = k_ref[ks, cols]
    kes = []
    for h in (2 * j, 2 * j + 1):
        me, mr = _head_masks(lane, h)
        ke = _masked(me, kj)
        if kpe_ref is not None:
            ke = jnp.concatenate([ke, _masked(mr, kpe_ref[ks, :])], axis=1)
        kes.append(ke)
    if kpe_ref is not None:
        qc = jnp.concatenate([qc, q_ref[qs, HW + LANES * (j // 2):HW + LANES * (j // 2 + 1)]], axis=1)
    return qc, kes


def _tile_variants(bias_t):
    out = {}
    for i, tile in enumerate(np.asarray(bias_t)):
        h = tile.shape[0] // 2
        skip = 1 if (tile[h:, :h] == NEG).all() else 2 if (tile[:h, h:] == NEG).all() else 0
        out[i] = (bool((tile != 0).any()), skip)
    return out


def _tile_parts(blk, skip):
    lo, hi, full = slice(0, blk // 2), slice(blk // 2, blk), slice(0, blk)
    return {0: [(full, full)], 1: [(lo, full), (hi, hi)], 2: [(hi, full), (lo, lo)]}[skip]


class Rider(NamedTuple):
    args: list
    in_specs: list
    out_shape: list
    out_specs: list
    scratch: list
    start: Callable
    finish: Callable
    stages: tuple = ()


def _ride_along(body, ride, n_prefetch, n_in, n_out, n_scratch, n_steps):
    if ride is None:
        return body

    def wrapped(*refs):
        pre, rest = refs[:n_prefetch], refs[n_prefetch:]
        a = n_in
        b = a + len(ride.args)
        c = b + n_out
        d = c + len(ride.out_shape)
        e = d + n_scratch
        mine = (rest[a:b], rest[c:d], rest[e:])
        t = pl.program_id(0)
        pl.when(t == 0)(lambda: ride.start(*mine))
        for at, stage in ride.stages:
            pl.when(t == at)(functools.partial(stage, *mine))
        body(*pre, *rest[:a], *rest[b:c], *rest[d:e])
        pl.when(t == n_steps - 1)(lambda: ride.finish(*mine))

    return wrapped


def _attn_fwd(name, q, k, kpe, vt, bias_t, steps, blk, ride=None, v_token_major=False):
    seq = q.shape[0]
    mla = kpe is not None
    n_steps = int(steps[0].shape[0])
    variants = _tile_variants(bias_t)

    def body(qi_r, ki_r, bi_r, fi_r, la_r, *refs):
        if mla:
            q_ref, k_ref, kpe_ref, vt_ref, b_ref, o_ref, lse_ref, m_sc, l_sc, acc_sc, st_sc = refs
        else:
            q_ref, k_ref, vt_ref, b_ref, o_ref, lse_ref, m_sc, l_sc, acc_sc, st_sc = refs
        t = pl.program_id(0)

        @pl.when(fi_r[t] == 1)
        def _():
            m_sc[...] = jnp.full(m_sc.shape, NEG, F32)
            l_sc[...] = jnp.zeros(l_sc.shape, F32)
            acc_sc[...] = jnp.zeros(acc_sc.shape, F32)

        lane = lax.broadcasted_iota(jnp.int32, (1, LANES), 1)
        if v_token_major:
            vt_all = vt_ref[...].astype(F32).T.astype(BF16)
            vt_rows = lambda rows, ks: vt_all[rows, ks]
        else:
            vt_rows = lambda rows, ks: vt_ref[rows, ks]

        def tile_pass(ks, qs, with_bias):
            nk, nq = ks.stop - ks.start, qs.stop - qs.start
            ones = jnp.ones((16, nk), BF16)

            def pair_scores(j):
                qc, kes = _pair_operands(q_ref, k_ref, kpe_ref if mla else None, lane, j, ks, qs)
                st = lax.dot_general(jnp.concatenate(kes, axis=0), qc, NT, preferred_element_type=F32)
                maxes = []
                for e in range(2):
                    se = st[e * nk:(e + 1) * nk]
                    if with_bias:
                        se = se + b_ref[0, ks, qs]
                    st_sc[j % 2, e * nk:(e + 1) * nk, 0:nq] = se
                    maxes.append(jnp.max(se, axis=0, keepdims=True))
                return maxes

            def softmax_pv(h, col_max):
                st = st_sc[(h // 2) % 2, (h % 2) * nk:(h % 2 + 1) * nk, 0:nq]
                hrow = slice(h, h + 1)
                m_prev = m_sc[hrow, qs]
                m_new = jnp.maximum(m_prev, col_max)
                alpha = jnp.exp2(m_prev - m_new)
                pt = jnp.exp2(st - m_new).astype(BF16)
                m_sc[hrow, qs] = m_new
                rows = slice(64 * h, 64 * h + 64)
                res = jnp.dot(jnp.concatenate([vt_rows(rows, ks), ones], axis=0), pt, preferred_element_type=F32)
                acc_sc[rows, qs] = alpha * acc_sc[rows, qs] + res[:64]
                l_sc[hrow, qs] = alpha * l_sc[hrow, qs] + res[64:65]

            maxes = pair_scores(0)
            for j in range(HEADS // 2):
                cur = maxes
                if j + 1 < HEADS // 2:
                    maxes = pair_scores(j + 1)
                softmax_pv(2 * j, cur[0])
                softmax_pv(2 * j + 1, cur[1])

        def step(with_bias, skip):
            for ks, qs in _tile_parts(blk, skip):
                tile_pass(ks, qs, with_bias)

        for idx, (with_bias, skip) in variants.items():
            if len(variants) == 1:
                step(with_bias, skip)
            else:
                pl.when(bi_r[t] == idx)(functools.partial(step, with_bias, skip))

        @pl.when(la_r[t] == 1)
        def _():
            for h in range(HEADS):
                rows = slice(64 * h, 64 * h + 64)
                acc_sc[rows, :] = acc_sc[rows, :] / l_sc[h:h + 1, :]
            o_ref[...] = acc_sc[...].T
            lse_ref[...] = m_sc[...] + jnp.log2(l_sc[...])

    qmap = lambda t, qi, ki, bi, fi, la: (qi[t], 0)
    kmap = lambda t, qi, ki, bi, fi, la: (ki[t], 0)
    in_specs = [pl.BlockSpec((blk, q.shape[1]), qmap), pl.BlockSpec((blk, HW), kmap)]
    args = [q, k]
    if mla:
        in_specs.append(pl.BlockSpec((blk, LANES), kmap))
        args.append(kpe)
    in_specs += [pl.BlockSpec((blk, HW), kmap) if v_token_major else
                 pl.BlockSpec((HW, blk), lambda t, qi, ki, bi, fi, la: (0, ki[t])),
                 pl.BlockSpec((1, blk, blk), lambda t, qi, ki, bi, fi, la: (bi[t], 0, 0))]
    args += [vt, jnp.asarray(bias_t)]
    out_specs = [pl.BlockSpec((blk, HW), qmap), pl.BlockSpec((HEADS, blk), lambda t, qi, ki, bi, fi, la: (0, qi[t]))]
    out_shape = [jax.ShapeDtypeStruct((seq, HW), F32), jax.ShapeDtypeStruct((HEADS, seq), F32)]
    scratch = [pltpu.VMEM((HEADS, blk), F32), pltpu.VMEM((HEADS, blk), F32),
               pltpu.VMEM((HW, blk), F32), pltpu.VMEM((2, 2 * blk, blk), F32)]
    body = _ride_along(body, ride, 5, len(args), len(out_shape), len(scratch), n_steps)
    if ride is not None:
        args, in_specs = args + ride.args, in_specs + ride.in_specs
        out_specs, out_shape, scratch = out_specs + ride.out_specs, out_shape + ride.out_shape, scratch + ride.scratch
    return _pcall(
        body, name=name,
        grid_spec=pltpu.PrefetchScalarGridSpec(
            num_scalar_prefetch=5, grid=(n_steps,), in_specs=in_specs, out_specs=out_specs, scratch_shapes=scratch),
        out_shape=out_shape,
        compiler_params=_cparams(dimension_semantics=("arbitrary",)),
    )(*steps, *args)


def _attn_bwd(name, q, k, kpe, v, kt, kpet, bias_t, do, lse, dstat, steps, blk, ride=None, single_visit=False):
    assert not (single_visit and kpe is not None) and (kt is not None or single_visit)
    seq = q.shape[0]
    mla = kpe is not None
    qw = q.shape[1]
    n_steps = int(steps[0].shape[0])
    dk_dtype = BF16 if mla else F32
    variants = _tile_variants(bias_t)

    def body(qi_r, ki_r, bi_r, fi_r, la_r, *refs):
        if mla:
            (q_ref, k_ref, kpe_ref, v_ref, kt_ref, kpet_ref, b_ref, do_ref, lse_ref, d_ref,
             dq_ref, dk_ref, dkpe_ref, dv_ref, dk_sc, dkpe_sc, dv_sc, st_sc, dpt_sc) = refs
        else:
            q_ref, k_ref, v_ref, *rest = refs
            kt_ref = rest.pop(0) if kt is not None else None
            b_ref, do_ref, lse_ref, d_ref, dq_ref, dk_ref, dv_ref, *rest = rest
            dq_tok_ref = rest.pop(0) if single_visit else None
            dk_sc, dv_sc, st_sc, dpt_sc = rest
        t = pl.program_id(0)

        @pl.when(t == 0)
        def _():
            dq_ref[...] = jnp.zeros(dq_ref.shape, F32)

        @pl.when(fi_r[t] == 1)
        def _():
            dk_sc[...] = jnp.zeros(dk_sc.shape, F32)
            dv_sc[...] = jnp.zeros(dv_sc.shape, F32)
            if mla:
                dkpe_sc[...] = jnp.zeros(dkpe_sc.shape, F32)

        qi = qi_r[t]
        lane = lax.broadcasted_iota(jnp.int32, (1, LANES), 1)
        if kt is None:
            kt_all = k_ref[...].astype(F32).T.astype(BF16)
            kt_rows = lambda rows, ks: kt_all[rows, ks]
        else:
            kt_rows = lambda rows, ks: kt_ref[rows, ks]

        def tile_pass(ks, qs, with_bias):
            nk, nq = ks.stop - ks.start, qs.stop - qs.start

            def pair_matmuls(j):
                cols = slice(LANES * j, LANES * (j + 1))
                qc, kes = _pair_operands(q_ref, k_ref, kpe_ref if mla else None, lane, j, ks, qs)
                st_sc[j % 2, 0:2 * nk, 0:nq] = lax.dot_general(
                    jnp.concatenate(kes, axis=0), qc, NT, preferred_element_type=F32)
                vj = v_ref[ks, cols]
                ves = [_masked(_head_masks(lane, h)[0], vj) for h in (2 * j, 2 * j + 1)]
                dpt_sc[j % 2, 0:2 * nk, 0:nq] = lax.dot_general(
                    jnp.concatenate(ves, axis=0), do_ref[qs, cols], NT, preferred_element_type=F32)

            def pair_grads(j):
                cols = slice(LANES * j, LANES * (j + 1))
                qj, doj = q_ref[qs, cols], do_ref[qs, cols]
                if mla:
                    qr = q_ref[qs, HW + LANES * (j // 2):HW + LANES * (j // 2 + 1)]
                pts, dsts, qms, doms = [], [], [], []
                for e in range(2):
                    h = 2 * j + e
                    me, mr = _head_masks(lane, h)
                    st = st_sc[j % 2, e * nk:(e + 1) * nk, 0:nq]
                    if with_bias:
                        st = st + b_ref[0, ks, qs]
                    pt = jnp.exp2(st - lse_ref[h:h + 1, qs])
                    dst = (pt * (dpt_sc[j % 2, e * nk:(e + 1) * nk, 0:nq] - d_ref[h:h + 1, qs])).astype(BF16)
                    pts.append(pt.astype(BF16))
                    dsts.append(dst)
                    doms.append(_masked(me, doj))
                    qm = _masked(me, qj)
                    if mla:
                        qm = jnp.concatenate([qm, _masked(mr, qr)], axis=1)
                    qms.append(qm)
                    ktl = kt_rows(slice(64 * h, 64 * h + 64), ks)
                    if mla:
                        ktl = jnp.concatenate([ktl, kpet_ref[:, ks]], axis=0)
                    dqc = jnp.dot(ktl, dst, preferred_element_type=F32)
                    dq_ref[qi, 64 * h:64 * h + 64, qs] += dqc[:64]
                    if mla:
                        dq_ref[qi, HW + MLA_ROPE * h:HW + MLA_ROPE * (h + 1), qs] += dqc[64:]
                dv_sc[ks, cols] += jnp.dot(
                    jnp.concatenate(pts, axis=1), jnp.concatenate(doms, axis=0), preferred_element_type=F32)
                dkc = jnp.dot(jnp.concatenate(dsts, axis=1), jnp.concatenate(qms, axis=0), preferred_element_type=F32)
                dk_sc[ks, cols] += dkc[:, :LANES]
                if mla:
                    dkpe_sc[ks, :] += dkc[:, LANES:]

            pair_matmuls(0)
            for j in range(HEADS // 2):
                if j + 1 < HEADS // 2:
                    pair_matmuls(j + 1)
                pair_grads(j)

        def step(with_bias, skip):
            for ks, qs in _tile_parts(blk, skip):
                tile_pass(ks, qs, with_bias)

        for idx, (with_bias, skip) in variants.items():
            if len(variants) == 1:
                step(with_bias, skip)
            else:
                pl.when(bi_r[t] == idx)(functools.partial(step, with_bias, skip))
        if single_visit:
            dq_tok_ref[...] = dq_ref[qi].T

        @pl.when(la_r[t] == 1)
        def _():
            dk_ref[...] = (dk_sc[...] * LN2).astype(dk_ref.dtype)
            dv_ref[...] = dv_sc[...].astype(dv_ref.dtype)
            if mla:
                dkpe_ref[...] = dkpe_sc[...] * LN2

    qmap = lambda t, qi, ki, bi, fi, la: (qi[t], 0)
    kmap = lambda t, qi, ki, bi, fi, la: (ki[t], 0)
    qmap_t = lambda t, qi, ki, bi, fi, la: (0, qi[t])
    kmap_t = lambda t, qi, ki, bi, fi, la: (0, ki[t])
    in_specs = [pl.BlockSpec((blk, qw), qmap), pl.BlockSpec((blk, HW), kmap)]
    args = [q, k]
    if mla:
        in_specs.append(pl.BlockSpec((blk, LANES), kmap))
        args.append(kpe)
    in_specs.append(pl.BlockSpec((blk, HW), kmap))
    args.append(v)
    if kt is not None:
        in_specs.append(pl.BlockSpec((HW, blk), kmap_t))
        args.append(kt)
    if mla:
        in_specs.append(pl.BlockSpec((MLA_ROPE, blk), kmap_t))
        args.append(kpet)
    in_specs += [pl.BlockSpec((1, blk, blk), lambda t, qi, ki, bi, fi, la: (bi[t], 0, 0)),
                 pl.BlockSpec((blk, HW), qmap), pl.BlockSpec((HEADS, blk), qmap_t), pl.BlockSpec((HEADS, blk), qmap_t)]
    args += [jnp.asarray(bias_t), do, lse, dstat]
    dq_shape = (seq // blk, qw, blk)
    out_specs = [pl.BlockSpec(dq_shape, lambda t, qi, ki, bi, fi, la: (0, 0, 0)), pl.BlockSpec((blk, HW), kmap)]
    out_shape = [jax.ShapeDtypeStruct(dq_shape, F32), jax.ShapeDtypeStruct((seq, HW), dk_dtype)]
    scratch = [pltpu.VMEM((blk, HW), F32)]
    if mla:
        out_specs.append(pl.BlockSpec((blk, LANES), kmap))
        out_shape.append(jax.ShapeDtypeStruct((seq, LANES), F32))
        scratch.append(pltpu.VMEM((blk, LANES), F32))
    out_specs.append(pl.BlockSpec((blk, HW), kmap))
    out_shape.append(jax.ShapeDtypeStruct((seq, HW), BF16))
    if single_visit:
        out_specs.append(pl.BlockSpec((blk, qw), qmap))
        out_shape.append(jax.ShapeDtypeStruct((seq, qw), F32))
    scratch.append(pltpu.VMEM((blk, HW), F32))
    scratch += [pltpu.VMEM((2, 2 * blk, blk), F32), pltpu.VMEM((2, 2 * blk, blk), F32)]
    body = _ride_along(body, ride, 5, len(args), len(out_shape), len(scratch), n_steps)
    if ride is not None:
        args, in_specs = args + ride.args, in_specs + ride.in_specs
        out_specs, out_shape, scratch = out_specs + ride.out_specs, out_shape + ride.out_shape, scratch + ride.scratch
    return _pcall(
        body, name=name,
        grid_spec=pltpu.PrefetchScalarGridSpec(
            num_scalar_prefetch=5, grid=(n_steps,), in_specs=in_specs, out_specs=out_specs,
            scratch_shapes=scratch),
        out_shape=out_shape,
        compiler_params=_cparams(dimension_semantics=("arbitrary",)),
    )(*steps, *args)


def _out_ln(oa, ob_near, ob_far, lse_near, lse_far, ga, gb, x, tgt, w_out, ln_g, ln_b, bt):
    seq = x.shape[0]

    def body(oa_ref, obn_ref, obf_ref, lsen_ref, lsef_ref, ga_ref, gb_ref, x_ref, tgt_ref, w_ref, g_ref, b_ref,
             dz_ref, doa_ref, dob_ref, dga_ref, dgb_ref, da_ref, db_ref, lse_ref, gwb_ref, small_ref, dobc_ref,
             gw_ref, lanes_sc):
        i = pl.program_id(0)

        @pl.when(i == 0)
        def _():
            gw_ref[...] = jnp.zeros(gw_ref.shape, F32)
            small_ref[...] = jnp.zeros(small_ref.shape, F32)

        def gate(g):
            sig = 0.5 * jnp.tanh(0.5 * g) + 0.5
            return g * sig, sig * (1.0 + g * (1.0 - sig))

        lse_n, lse_f = lsen_ref[...], lsef_ref[...]
        top = jnp.maximum(lse_n, lse_f)
        e_n, e_f = jnp.exp2(lse_n - top), jnp.exp2(lse_f - top)
        lse_ref[...] = top + jnp.log2(e_n + e_f)
        inv = 1.0 / (e_n + e_f)
        head_row = lax.broadcasted_iota(jnp.int32, (2 * HEADS, HW), 0) % HEADS
        spread = (head_row == lax.broadcasted_iota(jnp.int32, (2 * HEADS, HW), 1) // 64).astype(BF16)

        def per_lane(w):
            hi = w.astype(BF16)
            lo = (w - hi.astype(F32)).astype(BF16)
            return lax.dot_general(jnp.concatenate([hi, lo], axis=0), spread, TN, preferred_element_type=F32)

        o_a = oa_ref[...]
        o_b = per_lane(e_n * inv) * obn_ref[...] + per_lane(e_f * inv) * _in_sequence(obf_ref, lanes_sc)
        g_a, g_b = ga_ref[...], gb_ref[...]
        sa, dsa = gate(g_a)
        sb, dsb = gate(g_b)
        mix = jnp.concatenate([o_a * sa, o_b * sb], axis=1).astype(BF16)
        z = ALPHA * x_ref[...] + jnp.dot(mix, w_ref[...], preferred_element_type=F32)
        mu = jnp.mean(z, axis=1, keepdims=True)
        zc = z - mu
        rstd = lax.rsqrt(jnp.mean(zc * zc, axis=1, keepdims=True) + LN_EPS)
        xhat = zc * rstd
        gam = g_ref[...]
        diff = xhat * gam + b_ref[...] - tgt_ref[...]
        dy = diff * (1.0 / D_MODEL)
        small_ref[0:1, :] += jnp.sum(dy * xhat, axis=0, keepdims=True)
        small_ref[1:2, :] += jnp.sum(dy, axis=0, keepdims=True)
        small_ref[2:3, :] += jnp.sum(diff * diff, axis=0, keepdims=True)
        dxh = dy * gam
        dz = rstd * (dxh - jnp.mean(dxh, axis=1, keepdims=True) - xhat * jnp.mean(dxh * xhat, axis=1, keepdims=True))
        dz_ref[...] = dz
        dzb = dz.astype(BF16)
        gw_ref[...] += lax.dot_general(mix, dzb, TN, preferred_element_type=F32)

        @pl.when(i == seq // bt - 1)
        def _():
            gwb_ref[...] = gw_ref[...].astype(BF16)

        dmix = lax.dot_general(dzb, w_ref[...], NT, preferred_element_type=F32)
        doa, dob = dmix[:, :HW] * sa, dmix[:, HW:] * sb
        doa_ref[...] = doa.astype(BF16)
        dob_ref[...] = dob.astype(BF16)
        _by_class(dob, dobc_ref, lanes_sc)
        dga_ref[...] = (dmix[:, :HW] * o_a * dsa).astype(BF16)
        dgb_ref[...] = (dmix[:, HW:] * o_b * dsb).astype(BF16)
        head_of = (lax.broadcasted_iota(jnp.int32, (2 * HW, LANES), 0) % HW) // 64
        ind = (head_of == lax.broadcasted_iota(jnp.int32, (2 * HW, LANES), 1)).astype(BF16)

        def head_sums(prod):
            hi = prod.astype(BF16)
            lo = (prod - hi.astype(F32)).astype(BF16)
            sums = jnp.dot(jnp.concatenate([hi, lo], axis=1), ind, preferred_element_type=F32)
            return sums.T[:HEADS, :]

        da_ref[...] = head_sums(doa * o_a)
        db_ref[...] = head_sums(dob * o_b)

    def tok(width):
        return pl.BlockSpec((bt, width), lambda i: (i, 0))

    def full(shape):
        return pl.BlockSpec(shape, lambda i: (0,) * len(shape))

    stat = pl.BlockSpec((HEADS, bt), lambda i: (0, i))
    n_cls = ob_far.shape[0]
    by_class = pl.BlockSpec((n_cls, bt // n_cls, HW), lambda i: (0, i, 0))
    return _pcall(
        body, name="out_ln", grid=(seq // bt,),
        in_specs=[tok(HW), tok(HW), by_class, stat, stat, tok(HW), tok(HW), tok(D_MODEL), tok(D_MODEL),
                  full((D_MODEL, D_MODEL)), full((1, D_MODEL)), full((1, D_MODEL))],
        out_specs=[tok(D_MODEL), tok(HW), tok(HW), tok(HW), tok(HW), stat, stat, stat,
                   full((D_MODEL, D_MODEL)), full((8, D_MODEL)), by_class],
        out_shape=[jax.ShapeDtypeStruct((seq, D_MODEL), F32)] + [jax.ShapeDtypeStruct((seq, HW), BF16)] * 4
        + [jax.ShapeDtypeStruct((HEADS, seq), F32)] * 3
        + [jax.ShapeDtypeStruct((D_MODEL, D_MODEL), BF16), jax.ShapeDtypeStruct((8, D_MODEL), F32),
           jax.ShapeDtypeStruct(ob_far.shape, BF16)],
        scratch_shapes=[pltpu.VMEM((D_MODEL, D_MODEL), F32), pltpu.VMEM((HW // LANES, bt, LANES), F32)],
        compiler_params=_cparams(dimension_semantics=("arbitrary",)),
    )(oa, ob_near, ob_far, lse_near, lse_far, ga, gb, x, tgt, w_out, ln_g, ln_b)


def _bwd_mid(dq_m, dkn, dv, dkpe, dqb, dkb, dvb, far, dga, dgb, cq, ckv, qn, kvn, w_uq_r, w_ukv_r, qg, kvg, tabs, bt):
    n_cls = far[0].shape[0]
    seq = cq.shape[0]

    def body(dqm_ref, dkn_ref, dv_ref, dkpe_ref, dqb_ref, dkb_ref, dvb_ref, dqf_ref, dkf_ref, dvf_ref, dga_ref, dgb_ref,
             cq_ref, ckv_ref, qn_ref, kvn_ref, wuq_ref, wukv_ref, qg_ref, kvg_ref, tab_ref,
             dh_ref, guq_ref, gukv_ref, small_ref, seq_sc):
        i = pl.program_id(0)

        @pl.when(i == 0)
        def _():
            guq_ref[...] = jnp.zeros(guq_ref.shape, F32)
            gukv_ref[...] = jnp.zeros(gukv_ref.shape, F32)
            small_ref[...] = jnp.zeros(small_ref.shape, F32)

        m_tabs = (tab_ref[0], tab_ref[1], tab_ref[2])
        d_tabs = (tab_ref[3], tab_ref[4], tab_ref[5])

        def rms_bwd(c, dn, gain):
            r = lax.rsqrt(jnp.mean(c * c, axis=1, keepdims=True) + RMS_EPS)
            u = dn * gain
            dc = r * u - c * (r * r * r) * jnp.mean(u * c, axis=1, keepdims=True)
            return dc, jnp.sum(dn * c * r, axis=0, keepdims=True)

        dqm = dqm_ref[0].T
        dq = jnp.concatenate(
            [dqm[:, :HW], _rope_wide(_rope_t, dqm[:, HW:], *m_tabs, MLA_ROPE // 2)], axis=1) * MLA_SCALE
        dq = dq.astype(BF16)
        guq_ref[...] += lax.dot_general(qn_ref[...], dq, TN, preferred_element_type=F32)
        dqn = lax.dot_general(dq, wuq_ref[...], NT, preferred_element_type=F32)
        dcq, gq = rms_bwd(cq_ref[...], dqn, qg_ref[...])
        small_ref[0:1, :] += gq

        dkv = jnp.concatenate([dkn_ref[...], dv_ref[...]], axis=1)
        gukv_ref[...] += lax.dot_general(kvn_ref[...], dkv, TN, preferred_element_type=F32)
        dkvn = lax.dot_general(dkv, wukv_ref[...], NT, preferred_element_type=F32)
        dckv, gkv = rms_bwd(ckv_ref[...], dkvn, kvg_ref[...])
        small_ref[1:2, :KV_RANK] += gkv

        dh_ref[:, C_CQ:C_CKV] = dcq.astype(BF16)
        dh_ref[:, C_CKV:C_KR] = dckv.astype(BF16)
        dh_ref[:, C_KR:C_GA] = _rope_t(dkpe_ref[...], *m_tabs, MLA_ROPE // 2).astype(BF16)
        dh_ref[:, C_GA:C_QB] = dga_ref[...]
        in_sequence = functools.partial(_in_sequence, lanes_sc=seq_sc)
        dqb = dqb_ref[0].T + in_sequence(dqf_ref)
        dh_ref[:, C_QB:C_KB] = (_rope_wide(_rope_t, dqb, *d_tabs, DIL_ROT // 2) * DIL_SCALE).astype(BF16)
        dkb = dkb_ref[...] + in_sequence(dkf_ref)
        dh_ref[:, C_KB:C_VB] = _rope_wide(_rope_t, dkb, *d_tabs, DIL_ROT // 2).astype(BF16)
        dh_ref[:, C_VB:C_GB] = (dvb_ref[...].astype(F32) + in_sequence(dvf_ref)).astype(BF16)
        dh_ref[:, C_GB:C_END] = dgb_ref[...]

    def tok(width):
        return pl.BlockSpec((bt, width), lambda i: (i, 0))

    def tok_t(a):
        per = a.shape[2] // bt
        return pl.BlockSpec((1, a.shape[1], bt), lambda i: (i // per, 0, i % per))

    def full(shape):
        return pl.BlockSpec(shape, lambda i: (0,) * len(shape))

    by_class = pl.BlockSpec((n_cls, bt // n_cls, HW), lambda i: (0, i, 0))
    return _pcall(
        body, name="bwd_mid", grid=(seq // bt,),
        in_specs=[tok_t(dq_m), tok(HW), tok(HW), tok(LANES), tok_t(dqb), tok(HW), tok(HW), by_class, by_class, by_class,
                  tok(HW), tok(HW),
                  tok(Q_RANK), tok(KV_RANK), tok(Q_RANK), tok(KV_RANK),
                  full(w_uq_r.shape), full(w_ukv_r.shape), full((1, Q_RANK)), full((1, KV_RANK)),
                  pl.BlockSpec((6, bt, LANES), lambda i: (0, i, 0))],
        out_specs=[tok(C_END), full(w_uq_r.shape), full(w_ukv_r.shape), full((8, Q_RANK))],
        out_shape=[jax.ShapeDtypeStruct((seq, C_END), BF16), jax.ShapeDtypeStruct(w_uq_r.shape, F32),
                   jax.ShapeDtypeStruct(w_ukv_r.shape, F32), jax.ShapeDtypeStruct((8, Q_RANK), F32)],
        scratch_shapes=[pltpu.VMEM((HW // LANES, bt, LANES), F32)],
        compiler_params=_cparams(dimension_semantics=("arbitrary",)),
    )(dq_m, dkn, dv, dkpe, dqb, dkb, dvb, *far, dga, dgb, cq, ckv, qn, kvn, w_uq_r, w_ukv_r, qg, kvg, tabs)


def _grad_x(dz, dh, w_in_r, bt, ride=None):
    seq = dz.shape[0]
    n_steps = seq // bt

    def body(dz_ref, dh_ref, w_ref, gx_ref):
        gx_ref[...] = ALPHA * dz_ref[...] + lax.dot_general(
            dh_ref[...], w_ref[...], NT, preferred_element_type=F32)

    args = [dz, dh, w_in_r]
    in_specs = [pl.BlockSpec((bt, D_MODEL), lambda i: (i, 0)), pl.BlockSpec((bt, C_END), lambda i: (i, 0)),
                pl.BlockSpec(w_in_r.shape, lambda i: (0, 0))]
    out_specs = [pl.BlockSpec((bt, D_MODEL), lambda i: (i, 0))]
    out_shape = [jax.ShapeDtypeStruct((seq, D_MODEL), F32)]
    scratch = []
    body = _ride_along(body, ride, 0, len(args), len(out_shape), 0, n_steps)
    if ride is not None:
        args, in_specs = args + ride.args, in_specs + ride.in_specs
        out_specs, out_shape, scratch = out_specs + ride.out_specs, out_shape + ride.out_shape, ride.scratch
    return _pcall(
        body, name="grad_x", grid=(n_steps,),
        in_specs=in_specs, out_specs=out_specs, out_shape=out_shape, scratch_shapes=scratch,
        compiler_params=_cparams(dimension_semantics=("arbitrary",)),
    )(*args)


def _grad_w_in(x, dh, bt):
    seq = x.shape[0]
    shard = IN_WIDTH // N_DEV
    k_lo, k_hi = IN_SPLITS[0] + IN_SPLITS[1], IN_SPLITS[0] + IN_SPLITS[1] + MLA_ROPE

    def body(x_ref, dh_ref, out_ref, acc):
        i = pl.program_id(0)

        @pl.when(i == 0)
        def _():
            acc[...] = jnp.zeros(acc.shape, F32)

        acc[...] += lax.dot_general(x_ref[...].astype(BF16), dh_ref[...], TN, preferred_element_type=F32)

        @pl.when(i == seq // bt - 1)
        def _():
            kr = acc[:, C_KR:C_GA]
            kr = kr + pltpu.roll(kr, 96, 1) + pltpu.roll(kr, 64, 1) + pltpu.roll(kr, 32, 1)
            for d in range(N_DEV):
                lo, hi = shard * d, shard * (d + 1)
                pieces = []
                if lo < k_lo:
                    pieces.append(acc[:, lo:min(hi, k_lo)])
                if lo < k_hi and hi > k_lo:
                    pieces.append(kr[:, max(lo, k_lo) - k_lo:min(hi, k_hi) - k_lo])
                if hi > k_hi:
                    shift = C_GA - k_hi
                    pieces.append(acc[:, max(lo, k_hi) + shift:hi + shift])
                blk = pieces[0] if len(pieces) == 1 else jnp.concatenate(pieces, axis=1)
                out_ref[d] = blk.astype(BF16)

    return _pcall(
        body, name="grad_w_in", grid=(seq // bt,),
        in_specs=[pl.BlockSpec((bt, D_MODEL), lambda i: (i, 0)), pl.BlockSpec((bt, C_END), lambda i: (i, 0))],
        out_specs=pl.BlockSpec((N_DEV, D_MODEL, shard), lambda i: (0, 0, 0)),
        out_shape=jax.ShapeDtypeStruct((N_DEV, D_MODEL, shard), BF16),
        scratch_shapes=[pltpu.VMEM((D_MODEL, C_END), F32)],
        compiler_params=_cparams(dimension_semantics=("arbitrary",)),
    )(x, dh)


def _restore_grads(g_uq_r, g_ukv_r):
    g_uq = jnp.concatenate(
        [g_uq_r[:, :HW].reshape(Q_RANK, HEADS, MLA_NOPE), g_uq_r[:, HW:].reshape(Q_RANK, HEADS, MLA_ROPE)],
        axis=2).reshape(Q_RANK, HEADS * (MLA_NOPE + MLA_ROPE))
    g_ukv = jnp.concatenate(
        [g_ukv_r[:, :HW].reshape(KV_RANK, HEADS, MLA_NOPE), g_ukv_r[:, HW:].reshape(KV_RANK, HEADS, MLA_V)],
        axis=2).reshape(KV_RANK, HEADS * (MLA_NOPE + MLA_V))
    return g_uq, g_ukv


def _local_step(x, tgt, w_in_r, w_uq_r, w_ukv_r, w_out_rider, g_out_rider, reduce_rider, q_norm_g, kv_norm_g,
                ln_g, ln_b, bt=BLOCK_TOKENS, blk_m=BLOCK_MLA, blk_d=BLOCK_DIL):
    seq = x.shape[0]
    tabs = jnp.asarray(_rope_tables(seq))
    qg, kvg = q_norm_g.reshape(1, -1), kv_norm_g.reshape(1, -1)

    far_dil = DIL_CONFIGS[-1][1]
    cls = seq // far_dil
    (cq, ckv, qn, kvn, qcat, kn, kpe, v, ga, gb, qb, kb, vb, knt, kpet, vt, kbt, vbt, qb_c, kb_c, vb_c) = _fwd_proj(
        x, w_in_r, w_uq_r, w_ukv_r, qg, kvg, tabs, bt, far_dil)
    qb_c, kb_c, vb_c = (a.reshape(seq, HW) for a in (qb_c, kb_c, vb_c))

    nq_m, nq_d = seq // blk_m, seq // blk_d
    bias_m = _mla_bias_t(blk_m)
    oa, lse_a, w_out = _attn_fwd(
        "mla_fwd", qcat, kn, kpe, vt, bias_m, _steps(nq_m, nq_m, False, True), blk_m, ride=w_out_rider)

    bias_near = _dil_bias_t(blk_d, DIL_NEAR)
    ob_near, lse_near = _attn_fwd(
        "dil_fwd", qb, kb, None, vbt, bias_near, _steps(nq_d, -(-DIL_NEAR // blk_d), False, False), blk_d)
    each = jnp.arange(far_dil, dtype=jnp.int32)
    steps_far = [each, each, jnp.zeros_like(each), jnp.ones_like(each), jnp.ones_like(each)]
    bias_far = _dil_far_bias_t(cls)
    ob_far, lse_far = _attn_fwd(
        "dil_far_fwd", qb_c, kb_c, None, vb_c, bias_far, steps_far, cls, v_token_major=True)

    dz, doa, dob, dga, dgb, dst_a, dst_b, lse_b, g_out, small1, dob_c = _out_ln(
        oa, ob_near, ob_far.reshape(far_dil, cls, HW), lse_near, _lanes_from_classes(lse_far, far_dil), ga, gb, x, tgt,
        w_out.reshape(D_MODEL, D_MODEL), ln_g.reshape(1, -1), ln_b.reshape(1, -1), bt)

    dq_m, dkn, dkpe, dv, g_out_recv = _attn_bwd(
        "mla_bwd", qcat, kn, kpe, v, knt, kpet, bias_m, doa, lse_a, dst_a, _steps(nq_m, nq_m, True, True), blk_m,
        ride=g_out_rider(g_out.reshape(N_DEV, D_MODEL // N_DEV, D_MODEL)))
    dqb, dkb_near, dvb_near = _attn_bwd(
        "dil_bwd", qb, kb, None, vb, kbt, None, bias_near, dob, lse_b, dst_b,
        _steps(nq_d, -(-DIL_NEAR // blk_d), True, False), blk_d)
    _, dkb_far, dvb_far, dqb_far = _attn_bwd(
        "dil_far_bwd", qb_c, kb_c, None, vb_c, None, None, bias_far, dob_c.reshape(seq, HW),
        _lanes_to_classes(lse_b, far_dil), _lanes_to_classes(dst_b, far_dil), steps_far, cls, single_visit=True)
    far = [a.reshape(far_dil, cls, HW) for a in (dqb_far, dkb_far, dvb_far)]

    dh, g_uq_r, g_ukv_r, small2 = _bwd_mid(
        dq_m, dkn, dv, dkpe, dqb, dkb_near, dvb_near, far, dga, dgb, cq, ckv, qn, kvn, w_uq_r, w_ukv_r, qg, kvg, tabs, bt)
    g_in = _grad_w_in(x, dh, bt)
    g_uq, g_ukv = _restore_grads(g_uq_r, g_ukv_r)
    grads3 = [g_in] + [g.astype(BF16) for g in (
        g_uq.reshape(Q_RANK, N_DEV, -1).transpose(1, 0, 2), g_ukv.reshape(KV_RANK, N_DEV, -1).transpose(1, 0, 2))]
    small_part = _small_rows(small1[0], small1[1], small2[0, :Q_RANK], small2[1, :KV_RANK], small1[2])
    grad_x, *reduced = _grad_x(
        dz, dh, w_in_r, bt, ride=reduce_rider(grads3, g_out_recv, small_part, min(2, seq // bt - 1)))
    return grad_x, reduced


MESH_ID = pl.DeviceIdType.MESH
SHARD_SHAPES = ((D_MODEL, IN_WIDTH // N_DEV), (Q_RANK, 768 // N_DEV), (KV_RANK, 1024 // N_DEV), (D_MODEL // N_DEV, D_MODEL))
ADAM_ROWS = (32, 128, 128, 16)


def _me():
    x, y, c = lax.axis_index("x"), lax.axis_index("y"), lax.axis_index("c")
    return x, y, c, 4 * x + 2 * y + c


def _peer(k):
    x, y, c, _ = _me()
    px = 1 - x if (k >> 2) & 1 else x
    py = 1 - y if (k >> 1) & 1 else y
    pc = 1 - c if k & 1 else c
    return (px, py, pc), 4 * px + 2 * py + pc


def _all_gather_weights(shards):
    n = len(shards)
    shard = IN_WIDTH // N_DEV
    k_lo = IN_SPLITS[0] + IN_SPLITS[1]
    k_hi = k_lo + MLA_ROPE

    def body(*refs):
        ins = refs[:n]
        win_ref, wuq_ref, wukv_ref = refs[n:2 * n]
        bufs = refs[2 * n:3 * n]
        send_sems, recv_sems = refs[3 * n:]
        x, y, c, me = _me()
        here, sibling = (x, y, c), (x, y, 1 - c)
        chips = [(1 - x, y), (x, 1 - y), (1 - x, 1 - y)]
        for t in range(n):
            bufs[t][me] = ins[t][...].astype(BF16)

        def copy(t, k, px, py, pc, to):
            blk = bufs[t].at[4 * px + 2 * py + pc]
            return pltpu.make_async_remote_copy(
                src_ref=blk, dst_ref=blk, send_sem=send_sems.at[t, k], recv_sem=recv_sems.at[t, k],
                device_id=to, device_id_type=MESH_ID)

        first = []
        for t in range(n):
            first.append(copy(t, 0, x, y, c, sibling))
            for j, (px, py) in enumerate(chips):
                first.append(copy(t, 1 + j, x, y, c, (px, py, c)))
        for cp in first:
            cp.start()
        passed = []
        for j, (px, py) in enumerate(chips):
            for t in range(n):
                copy(t, 1 + j, px, py, c, here).wait_recv()
                cp = copy(t, 4 + j, px, py, c, sibling)
                cp.start()
                passed.append(cp)
        for t in range(n):
            copy(t, 0, x, y, 1 - c, here).wait_recv()
        for j, (px, py) in enumerate(chips):
            for t in range(n):
                copy(t, 4 + j, px, py, 1 - c, here).wait_recv()
        for cp in first + passed:
            cp.wait_send()

        a_in, a_uq, a_ukv = bufs
        for d in range(N_DEV):
            lo, hi = shard * d, shard * (d + 1)
            if lo < k_lo:
                win_ref[:, lo:min(hi, k_lo)] = a_in[d, :, 0:min(hi, k_lo) - lo]
            if lo < k_hi and hi > k_lo:
                kr = a_in[d, :, k_lo - lo:k_hi - lo]
                for rep in range(4):
                    win_ref[:, C_KR + MLA_ROPE * rep:C_KR + MLA_ROPE * (rep + 1)] = kr
            if hi > k_hi:
                src = max(lo, k_hi)
                win_ref[:, src + C_GA - k_hi:hi + C_GA - k_hi] = a_in[d, :, src - lo:hi - lo]
        for h in range(HEADS):
            wuq_ref[:, MLA_NOPE * h:MLA_NOPE * (h + 1)] = a_uq[h, :, :MLA_NOPE]
            wuq_ref[:, HW + MLA_ROPE * h:HW + MLA_ROPE * (h + 1)] = a_uq[h, :, MLA_NOPE:]
            wukv_ref[:, MLA_NOPE * h:MLA_NOPE * (h + 1)] = a_ukv[h, :, :MLA_NOPE]
            wukv_ref[:, HW + MLA_V * h:HW + MLA_V * (h + 1)] = a_ukv[h, :, MLA_NOPE:]

    vmem = pl.BlockSpec(memory_space=pltpu.VMEM)
    return _pcall(
        body, name="gather_weights",
        in_specs=[vmem] * n, out_specs=[vmem] * n,
        out_shape=[jax.ShapeDtypeStruct((D_MODEL, C_END), BF16), jax.ShapeDtypeStruct((Q_RANK, QW), BF16),
                   jax.ShapeDtypeStruct((KV_RANK, 2 * HW), BF16)],
        scratch_shapes=[pltpu.VMEM((N_DEV,) + s, BF16) for s in SHARD_SHAPES[:n]]
        + [pltpu.SemaphoreType.DMA((n, N_DEV - 1)), pltpu.SemaphoreType.DMA((n, N_DEV - 1))],
        compiler_params=_cparams(),
    )(*shards)


def _gather_w_out_rider(w_out):
    def copies(full_ref, stage, send_sems, recv_sems):
        me = _me()[3]
        out = []
        for k in range(1, N_DEV):
            peer, pidx = _peer(k)
            send = pltpu.make_async_remote_copy(
                src_ref=stage, dst_ref=full_ref.at[me], send_sem=send_sems.at[k - 1], recv_sem=recv_sems.at[k - 1],
                device_id=peer, device_id_type=MESH_ID)
            recv = pltpu.make_async_remote_copy(
                src_ref=stage, dst_ref=full_ref.at[pidx], send_sem=send_sems.at[k - 1], recv_sem=recv_sems.at[k - 1],
                device_id=peer, device_id_type=MESH_ID)
            out.append((send, recv))
        return out

    def start(ins, outs, scr):
        stage, send_sems, recv_sems, own_sem = scr
        stage[...] = ins[0][...].astype(BF16)
        pltpu.make_async_copy(stage, outs[0].at[_me()[3]], own_sem).start()
        for send, _ in copies(outs[0], stage, send_sems, recv_sems):
            send.start()

    def finish(ins, outs, scr):
        stage, send_sems, recv_sems, own_sem = scr
        pltpu.make_async_copy(stage, outs[0].at[_me()[3]], own_sem).wait()
        pairs = copies(outs[0], stage, send_sems, recv_sems)
        for _, recv in pairs:
            recv.wait_recv()
        for send, _ in pairs:
            send.wait_send()

    shape = SHARD_SHAPES[3]
    return Rider(
        args=[w_out], in_specs=[pl.BlockSpec(shape, lambda t, *_: (0, 0))],
        out_shape=[jax.ShapeDtypeStruct((N_DEV,) + shape, BF16)], out_specs=[pl.BlockSpec(memory_space=pl.ANY)],
        scratch=[pltpu.VMEM(shape, BF16), pltpu.SemaphoreType.DMA((N_DEV - 1,)), pltpu.SemaphoreType.DMA((N_DEV - 1,)),
                 pltpu.SemaphoreType.DMA],
        start=start, finish=finish)


def _scatter_g_out_rider(blocks):
    def copies(src_ref, dst_ref, send_sems, recv_sems):
        out = []
        for k in range(1, N_DEV):
            peer, pidx = _peer(k)
            out.append(pltpu.make_async_remote_copy(
                src_ref=src_ref.at[pidx], dst_ref=dst_ref.at[k], send_sem=send_sems.at[k - 1],
                recv_sem=recv_sems.at[k - 1], device_id=peer, device_id_type=MESH_ID))
        return out

    def start(ins, outs, scr):
        send_sems, recv_sems, own_sem = scr
        pltpu.make_async_copy(ins[0].at[_me()[3]], outs[0].at[0], own_sem).start()
        for cp in copies(ins[0], outs[0], send_sems, recv_sems):
            cp.start()

    def finish(ins, outs, scr):
        send_sems, recv_sems, own_sem = scr
        pltpu.make_async_copy(ins[0].at[_me()[3]], outs[0].at[0], own_sem).wait()
        for cp in copies(ins[0], outs[0], send_sems, recv_sems):
            cp.wait()

    hbm = pl.BlockSpec(memory_space=pl.ANY)
    return Rider(
        args=[blocks], in_specs=[hbm], out_shape=[jax.ShapeDtypeStruct(blocks.shape, blocks.dtype)], out_specs=[hbm],
        scratch=[pltpu.SemaphoreType.DMA((N_DEV - 1,)), pltpu.SemaphoreType.DMA((N_DEV - 1,)), pltpu.SemaphoreType.DMA],
        start=start, finish=finish)


def _adamw(w, g, m, v):
    m = ADAM_B1 * m + (1.0 - ADAM_B1) * g
    v = ADAM_B2 * v + (1.0 - ADAM_B2) * jnp.square(g)
    m_hat = m / (1.0 - ADAM_B1 ** ADAM_STEP)
    v_hat = v / (1.0 - ADAM_B2 ** ADAM_STEP)
    delta = -ADAM_LR * (m_hat / (jnp.sqrt(v_hat) + ADAM_EPS) + ADAM_WD * w)
    return delta, m, v


def _reduce_grads_rider(grads3, arrived, small_part, mid_step):
    n = len(grads3)

    class Refs:
        def __init__(self, ins, outs, scr):
            self.g3, self.arr, self.sp = ins[0:n], ins[n], ins[n + 1]
            self.gsum, self.gsum_out, self.ssum = outs[0:n], outs[n], outs[n + 1]
            self.own, self.sib, self.part, self.ici = scr[0:n], scr[n:2 * n], scr[2 * n:3 * n], scr[3 * n:4 * n]
            self.rsmall = scr[4 * n]
            (self.loc_sems, self.d2d_send, self.d2d_recv, self.ici_send, self.ici_recv,
             self.sm_send, self.sm_recv) = scr[4 * n + 1:]
            self.x, self.y, self.c, self.me = _me()
            self.chips = [(1 - self.x, self.y), (self.x, 1 - self.y), (1 - self.x, 1 - self.y)]

        def small(self):
            return [pltpu.make_async_remote_copy(
                src_ref=self.rsmall.at[0], dst_ref=self.rsmall.at[k], send_sem=self.sm_send.at[k - 1],
                recv_sem=self.sm_recv.at[k - 1], device_id=_peer(k)[0], device_id_type=MESH_ID)
                for k in range(1, N_DEV)]

        def level1(self):
            local, to_sib = [], []
            for t in range(n):
                for q in range(4):
                    local.append(pltpu.make_async_copy(
                        self.g3[t].at[2 * q + self.c], self.own[t].at[q], self.loc_sems.at[t, q]))
                    to_sib.append(pltpu.make_async_remote_copy(
                        src_ref=self.g3[t].at[2 * q + 1 - self.c], dst_ref=self.sib[t].at[q],
                        send_sem=self.d2d_send.at[t, q], recv_sem=self.d2d_recv.at[t, q],
                        device_id=(self.x, self.y, 1 - self.c), device_id_type=MESH_ID))
            return local, to_sib

        def level2(self):
            return [pltpu.make_async_remote_copy(
                src_ref=self.part[t].at[2 * px + py], dst_ref=self.ici[t].at[j], send_sem=self.ici_send.at[t, j],
                recv_sem=self.ici_recv.at[t, j], device_id=(px, py, self.c), device_id_type=MESH_ID)
                for t in range(n) for j, (px, py) in enumerate(self.chips)]

    def chunks(t, fn):
        rows = ADAM_ROWS[t]

        def step(i, carry):
            fn(pl.ds(pl.multiple_of(i * rows, rows), rows))
            return carry

        lax.fori_loop(0, SHARD_SHAPES[t][0] // rows, step, 0)

    def start(*refs):
        r = Refs(*refs)
        r.rsmall[0] = r.sp[...]
        local, to_sib = r.level1()
        for cp in r.small() + local + to_sib:
            cp.start()

    def middle(*refs):
        r = Refs(*refs)
        local, to_sib = r.level1()
        for cp in local:
            cp.wait()
        for cp in to_sib:
            cp.wait_recv()
        my_chip = 2 * r.x + r.y
        for t in range(n):
            def pair_sums(rows, t=t):
                for q in range(4):
                    r.part[t][q, rows, :] = (
                        r.own[t][q, rows, :].astype(F32) + r.sib[t][q, rows, :].astype(F32)).astype(BF16)
                r.gsum[t][rows, :] = r.own[t][my_chip, rows, :].astype(F32) + r.sib[t][my_chip, rows, :].astype(F32)

            chunks(t, pair_sums)
        for cp in r.level2():
            cp.start()

    def finish(*refs):
        r = Refs(*refs)

        def add_arrived(rows):
            g = r.arr[0, rows, :].astype(F32)
            for k in range(1, N_DEV):
                g = g + r.arr[k, rows, :].astype(F32)
            r.gsum_out[rows, :] = g

        chunks(3, add_arrived)
        to_chips = r.level2()
        for cp in to_chips:
            cp.wait_recv()
        for t in range(n):
            def add_chips(rows, t=t):
                g = r.gsum[t][rows, :]
                for j in range(3):
                    g = g + r.ici[t][j, rows, :].astype(F32)
                r.gsum[t][rows, :] = g

            chunks(t, add_chips)
        small = r.small()
        for cp in small:
            cp.wait_recv()
        tot = r.rsmall[r.me]
        for d in range(1, N_DEV):
            tot = tot + r.rsmall[jnp.bitwise_xor(r.me, d)]
        r.ssum[...] = tot
        for cp in small + r.level1()[1] + to_chips:
            cp.wait_send()

    hbm = pl.BlockSpec(memory_space=pl.ANY)
    dma = pltpu.SemaphoreType.DMA

    def whole(shape):
        return pl.BlockSpec(shape, lambda i: (0,) * len(shape))

    out_shapes = list(SHARD_SHAPES) + [(8, D_MODEL)]
    return Rider(
        args=list(grads3) + [arrived, small_part],
        in_specs=[hbm] * n + [whole(arrived.shape), whole(small_part.shape)],
        out_shape=[jax.ShapeDtypeStruct(s, F32) for s in out_shapes], out_specs=[whole(s) for s in out_shapes],
        scratch=[pltpu.VMEM((slots,) + s, BF16) for slots in (4, 4, 4, 3) for s in SHARD_SHAPES[:n]]
        + [pltpu.VMEM((N_DEV, 8, D_MODEL), F32), dma((n, 4)), dma((n, 4)), dma((n, 4)), dma((n, 3)), dma((n, 3)),
           dma((N_DEV - 1,)), dma((N_DEV - 1,))],
        start=start, finish=finish, stages=((mid_step, middle),))


def _adamw_update(grads, small_grad, wmv, small_wmv):
    def body(*refs):
        g_refs, sg_ref = refs[0:4], refs[4]
        wmv_refs = [refs[5 + 3 * t:8 + 3 * t] for t in range(4)]
        swmv_ref = refs[17]
        out_refs = [refs[18 + 4 * t:22 + 4 * t] for t in range(4)]
        sout_ref = refs[34]
        tot = sg_ref[...]
        delta, m, v = _adamw(swmv_ref[0], tot, swmv_ref[1], swmv_ref[2])
        sout_ref[0], sout_ref[1], sout_ref[2], sout_ref[3] = tot, delta, m, v
        for t in range(4):
            rows = ADAM_ROWS[t]
            w_ref, m_ref, v_ref = wmv_refs[t]
            g_out, d_out, m_out, v_out = out_refs[t]

            def step(i, carry, g_ref=g_refs[t], rows=rows, w_ref=w_ref, m_ref=m_ref, v_ref=v_ref,
                     g_out=g_out, d_out=d_out, m_out=m_out, v_out=v_out):
                r = pl.ds(pl.multiple_of(i * rows, rows), rows)
                g = g_ref[r, :]
                delta, m, v = _adamw(w_ref[r, :], g, m_ref[r, :], v_ref[r, :])
                g_out[r, :], d_out[r, :], m_out[r, :], v_out[r, :] = g, delta, m, v
                return carry

            lax.fori_loop(0, SHARD_SHAPES[t][0] // rows, step, 0)

    vmem = pl.BlockSpec(memory_space=pltpu.VMEM)
    flat_wmv = [a for trio in wmv for a in trio]
    return _pcall(
        body, name="adamw",
        in_specs=[vmem] * 18, out_specs=[vmem] * 17,
        out_shape=[jax.ShapeDtypeStruct(s, F32) for s in SHARD_SHAPES for _ in range(4)]
        + [jax.ShapeDtypeStruct((4, 8, D_MODEL), F32)],
        compiler_params=_cparams(),
    )(*grads, small_grad, *flat_wmv, small_wmv)


def _small_rows(ln_g, ln_b, q_norm_g, kv_norm_g, extra=None):
    pad = lambda a: jnp.pad(a, (0, D_MODEL - a.shape[0]))
    rows = [ln_g, ln_b, pad(q_norm_g), pad(kv_norm_g)] + ([] if extra is None else [extra])
    return jnp.pad(jnp.stack(rows), ((0, 8 - len(rows)), (0, 0)))


def kernel(x, w_in, q_norm_g, kv_norm_g, w_uq, w_ukv, w_out, ln_g, ln_b, loss_target, m_w_in, m_q_norm_g, m_kv_norm_g, m_w_uq, m_w_ukv, m_w_out, m_ln_g, m_ln_b, v_w_in, v_q_norm_g, v_kv_norm_g, v_w_uq, v_w_ukv, v_w_out, v_ln_g, v_ln_b):
    w_in_r, w_uq_r, w_ukv_r = _all_gather_weights([w_in, w_uq, w_ukv])
    grad_x, sums = _local_step(
        x[0], loss_target[0], w_in_r, w_uq_r, w_ukv_r, _gather_w_out_rider(w_out), _scatter_g_out_rider,
        _reduce_grads_rider, q_norm_g, kv_norm_g, ln_g, ln_b)
    small_wmv = jnp.stack([_small_rows(ln_g, ln_b, q_norm_g, kv_norm_g),
                           _small_rows(m_ln_g, m_ln_b, m_q_norm_g, m_kv_norm_g),
                           _small_rows(v_ln_g, v_ln_b, v_q_norm_g, v_kv_norm_g)])
    wmv = [(w_in, m_w_in, v_w_in), (w_uq, m_w_uq, v_w_uq), (w_ukv, m_w_ukv, v_w_ukv), (w_out, m_w_out, v_w_out)]
    res = _adamw_update(sums[:4], sums[4], wmv, small_wmv)
    big = [res[4 * t:4 * t + 4] for t in range(4)]
    small = res[16]
    loss = (0.5 / D_MODEL) * jnp.sum(small[0, 4])

    def group(kind):
        s = small[kind]
        return (big[0][kind], s[2, :Q_RANK], s[3, :KV_RANK], big[1][kind], big[2][kind], big[3][kind], s[0], s[1])

    return (loss, grad_x[None], *group(0), *group(1), *group(2), *group(3))
```

```python
import functools
from typing import Callable, NamedTuple

import numpy as np
import jax
import jax.numpy as jnp
from jax import lax
from jax.experimental import pallas as pl
from jax.experimental.pallas import tpu as pltpu

F32 = jnp.float32
BF16 = jnp.bfloat16

D_MODEL = 1024
ROPE_THETA = 500000.0
NEG = -1e30
RMS_EPS = 1e-6
LN_EPS = 1e-5
HEADS = 8
MLA_NOPE = 64
MLA_ROPE = 32
MLA_V = 64
Q_RANK = 384
KV_RANK = 256
DIL_HEAD = 64
DIL_ROT = 16
DIL_CONFIGS = ((128, 1), (512, 4), (2048, 16))
DIL_NEAR = 512
HW = HEADS * 64
QW = HW + HEADS * MLA_ROPE
IN_SPLITS = (Q_RANK, KV_RANK, MLA_ROPE, HW, HW, HW, HW, HW)
IN_WIDTH = sum(IN_SPLITS)
ALPHA = 2.0 ** 0.25
MLA_SCALE = (MLA_NOPE + MLA_ROPE) ** -0.5
DIL_SCALE = DIL_HEAD ** -0.5
LOG2E = 1.4426950408889634
LN2 = 0.6931471805599453

ADAM_LR = 0.001
ADAM_B1 = 0.9
ADAM_B2 = 0.999
ADAM_EPS = 1e-08
ADAM_WD = 0.01
ADAM_STEP = 10

N_DEV = 8
LANES = 128
VMEM_LIMIT = 56 * 1024 * 1024
BLOCK_TOKENS = 512
BLOCK_MLA = 512
BLOCK_DIL = 512

C_CQ, C_CKV, C_KR, C_GA, C_QB, C_KB, C_VB, C_GB, C_END = 0, 384, 640, 768, 1280, 1792, 2304, 2816, 3328

NT = (((1,), (1,)), ((), ()))
TN = (((0,), (0,)), ((), ()))


def _pcall(body, **kw):
    return pl.pallas_call(body, **kw)


def _cparams(**kw):
    return pltpu.CompilerParams(vmem_limit_bytes=VMEM_LIMIT, **kw)


def _rope_tables(seq):
    def tabs(dim, period):
        half = dim // 2
        inv = np.float32(ROPE_THETA) ** (-np.arange(0, dim, 2, dtype=np.float32) / np.float32(dim))
        ang = np.arange(seq, dtype=np.float32)[:, None] * inv.astype(np.float32)[None, :]
        cos, sin = np.cos(ang).astype(np.float32), np.sin(ang).astype(np.float32)
        j = np.arange(LANES) % period
        f = j % half
        c = np.where(j < dim, cos[:, f], np.float32(1.0))
        s1 = np.where(j < half, -sin[:, f], np.float32(0.0))
        s2 = np.where((j >= half) & (j < dim), sin[:, f], np.float32(0.0))
        return [c, s1, s2]
    return np.stack(tabs(MLA_ROPE, MLA_ROPE) + tabs(DIL_ROT, DIL_HEAD)).astype(np.float32)


def _rope(t, c, s1, s2, half):
    return t * c + pltpu.roll(t, LANES - half, 1) * s1 + pltpu.roll(t, half, 1) * s2


def _rope_t(d, c, s1, s2, half):
    return d * c + pltpu.roll(d * s1, half, 1) + pltpu.roll(d * s2, LANES - half, 1)


def _rope_wide(fn, t, c, s1, s2, half):
    return jnp.concatenate(
        [fn(t[:, i:i + LANES], c, s1, s2, half) for i in range(0, t.shape[1], LANES)], axis=1)


def _mla_bias_t(blk):
    a = np.arange(blk)
    causal = np.where(a[:, None] <= a[None, :], 0.0, NEG)
    return np.stack([np.zeros((blk, blk)), causal]).astype(np.float32)


def _dil_bias_t(blk, reach):
    a = np.arange(blk)
    out = []
    for off in range(-(-reach // blk) + 1):
        delta = blk * off + a[None, :] - a[:, None]
        mult = np.zeros((blk, blk))
        for window, dil in DIL_CONFIGS:
            mult += (delta >= 0) & (delta % dil == 0) & (delta <= min(window, reach))
        out.append(np.where(mult > 0, np.log2(np.maximum(mult, 1.0)), NEG))
    return np.stack(out).astype(np.float32)


def _dil_far_bias_t(length):
    window, dil = DIL_CONFIGS[-1]
    a = np.arange(length)
    steps_back = a[None, :] - a[:, None]
    seen = (steps_back * dil > DIL_NEAR) & (steps_back * dil <= window)
    return np.where(seen, 0.0, NEG).astype(np.float32)[None]


def _lanes_to_classes(a, dil):
    h, s = a.shape
    return a.reshape(h, s // dil, dil).transpose(0, 2, 1).reshape(h, s)


def _lanes_from_classes(a, dil):
    h, s = a.shape
    return a.reshape(h, dil, s // dil).transpose(0, 2, 1).reshape(h, s)


def _steps(nq, span, by_key, diag_only_bias):
    rows = []
    if by_key:
        for ki in range(nq):
            hi = min(nq - 1, ki + span)
            for qi in range(ki, hi + 1):
                rows.append((qi, ki, int(qi == ki), int(qi == hi)))
    else:
        for qi in range(nq):
            lo = max(0, qi - span)
            for ki in range(lo, qi + 1):
                rows.append((qi, ki, int(ki == lo), int(ki == qi)))
    arr = np.array(rows, dtype=np.int32)
    off = arr[:, 0] - arr[:, 1]
    bias_idx = (off == 0).astype(np.int32) if diag_only_bias else off.astype(np.int32)
    return [jnp.asarray(v) for v in (arr[:, 0], arr[:, 1], bias_idx, arr[:, 2], arr[:, 3])]


def _by_class(val, out_ref, lanes_sc):
    n_cls, per = out_ref.shape[0], out_ref.shape[1]
    for c in range(val.shape[1] // LANES):
        lanes_sc[c] = val[:, LANES * c:LANES * (c + 1)]
        for r in range(n_cls):
            rows = lanes_sc.at[c][pl.ds(r, per, stride=n_cls), :]
            out_ref[r, :, LANES * c:LANES * (c + 1)] = rows.astype(out_ref.dtype)


def _in_sequence(ref, lanes_sc):
    n_cls, per, width = ref.shape
    for c in range(width // LANES):
        for r in range(n_cls):
            lanes_sc.at[c][pl.ds(r, per, stride=n_cls), :] = ref[r, :, LANES * c:LANES * (c + 1)].astype(F32)
    return jnp.concatenate([lanes_sc[c] for c in range(width // LANES)], axis=1)


def _fwd_proj(x, w_in_r, w_uq_r, w_ukv_r, qg, kvg, tabs, bt, n_cls):
    seq = x.shape[0]

    def body(x_ref, win_ref, wuq_ref, wukv_ref, qg_ref, kvg_ref, tab_ref,
             cq_ref, ckv_ref, qn_ref, kvn_ref, qcat_ref, kn_ref, kpe_ref, v_ref,
             ga_ref, gb_ref, qb_ref, kb_ref, vb_ref, knt_ref, kpet_ref, vt_ref, kbt_ref, vbt_ref,
             qbc_ref, kbc_ref, vbc_ref, lanes_sc):
        xb = x_ref[...].astype(BF16)

        def proj(lo, hi):
            return jnp.dot(xb, win_ref[:, lo:hi], preferred_element_type=F32)

        m_tabs = (tab_ref[0], tab_ref[1], tab_ref[2])
        d_tabs = (tab_ref[3], tab_ref[4], tab_ref[5])

        cq = proj(C_CQ, C_CKV)
        cq_ref[...] = cq
        qn = (cq * lax.rsqrt(jnp.mean(cq * cq, axis=1, keepdims=True) + RMS_EPS) * qg_ref[...]).astype(BF16)
        qn_ref[...] = qn
        q = jnp.dot(qn, wuq_ref[...], preferred_element_type=F32)
        qcat_ref[:, :HW] = (q[:, :HW] * (MLA_SCALE * LOG2E)).astype(BF16)
        qcat_ref[:, HW:] = (_rope_wide(_rope, q[:, HW:], *m_tabs, MLA_ROPE // 2) * (MLA_SCALE * LOG2E)).astype(BF16)

        ckv = proj(C_CKV, C_KR)
        ckv_ref[...] = ckv
        kvn = (ckv * lax.rsqrt(jnp.mean(ckv * ckv, axis=1, keepdims=True) + RMS_EPS) * kvg_ref[...]).astype(BF16)
        kvn_ref[...] = kvn
        kv = jnp.dot(kvn, wukv_ref[...], preferred_element_type=F32)
        kn_ref[...] = kv[:, :HW].astype(BF16)
        v_ref[...] = kv[:, HW:].astype(BF16)
        knt_ref[...] = kv[:, :HW].T.astype(BF16)
        vt_ref[...] = kv[:, HW:].T.astype(BF16)

        kpe = _rope(proj(C_KR, C_GA), *m_tabs, MLA_ROPE // 2)
        kpe_ref[...] = kpe.astype(BF16)
        kpet_ref[...] = kpe.T[:MLA_ROPE, :].astype(BF16)
        ga_ref[...] = proj(C_GA, C_QB)
        qb = _rope_wide(_rope, proj(C_QB, C_KB), *d_tabs, DIL_ROT // 2) * (DIL_SCALE * LOG2E)
        qb_ref[...] = qb.astype(BF16)
        _by_class(qb, qbc_ref, lanes_sc)
        kb = _rope_wide(_rope, proj(C_KB, C_VB), *d_tabs, DIL_ROT // 2)
        kb_ref[...] = kb.astype(BF16)
        kbt_ref[...] = kb.T.astype(BF16)
        _by_class(kb, kbc_ref, lanes_sc)
        vb = proj(C_VB, C_GB)
        vb_ref[...] = vb.astype(BF16)
        vbt_ref[...] = vb.T.astype(BF16)
        _by_class(vb, vbc_ref, lanes_sc)
        gb_ref[...] = proj(C_GB, C_END)

    def tok(width):
        return pl.BlockSpec((bt, width), lambda i: (i, 0))

    def tok_t(height):
        return pl.BlockSpec((height, bt), lambda i: (0, i))

    def full(a):
        return pl.BlockSpec(a.shape, lambda i: (0,) * a.ndim)

    outs = [(Q_RANK, F32), (KV_RANK, F32), (Q_RANK, BF16), (KV_RANK, BF16), (QW, BF16), (HW, BF16),
            (LANES, BF16), (HW, BF16), (HW, F32), (HW, F32), (HW, BF16), (HW, BF16), (HW, BF16)]
    outs_t = [HW, MLA_ROPE, HW, HW, HW]
    by_class = pl.BlockSpec((n_cls, bt // n_cls, HW), lambda i: (0, i, 0))
    return _pcall(
        body, name="fwd_proj", grid=(seq // bt,),
        in_specs=[tok(D_MODEL), full(w_in_r), full(w_uq_r), full(w_ukv_r), full(qg), full(kvg),
                  pl.BlockSpec((6, bt, LANES), lambda i: (0, i, 0))],
        out_specs=[tok(w) for w, _ in outs] + [tok_t(h) for h in outs_t] + [by_class] * 3,
        out_shape=[jax.ShapeDtypeStruct((seq, w), dt) for w, dt in outs]
        + [jax.ShapeDtypeStruct((h, seq), BF16) for h in outs_t]
        + [jax.ShapeDtypeStruct((n_cls, seq // n_cls, HW), BF16)] * 3,
        scratch_shapes=[pltpu.VMEM((HW // LANES, bt, LANES), F32)],
        compiler_params=_cparams(dimension_semantics=("arbitrary",)),
    )(x, w_in_r, w_uq_r, w_ukv_r, qg, kvg, tabs)


def _head_masks(lane, h):
    e, g = h % 2, h % 4
    me = (lane >= 64 * e) & (lane < 64 * e + 64)
    mr = (lane >= 32 * g) & (lane < 32 * g + 32)
    return me, mr


def _masked(mask, a):
    return jnp.where(mask, a, jnp.zeros_like(a))


def _pair_operands(q_ref, k_ref, kpe_ref, lane, j, ks=slice(None), qs=slice(None)):
    cols = slice(LANES * j, LANES * (j + 1))
    qc = q_ref[qs, cols]
    kj = k_ref[ks, cols]
    kes = []
    for h in (2 * j, 2 * j + 1):
        me, mr = _head_masks(lane, h)
        ke = _masked(me, kj)
        if kpe_ref is not None:
            ke = jnp.concatenate([ke, _masked(mr, kpe_ref[ks, :])], axis=1)
        kes.append(ke)
    if kpe_ref is not None:
        qc = jnp.concatenate([qc, q_ref[qs, HW + LANES * (j // 2):HW + LANES * (j // 2 + 1)]], axis=1)
    return qc, kes


def _tile_variants(bias_t):
    out = {}
    for i, tile in enumerate(np.asarray(bias_t)):
        h = tile.shape[0] // 2
        skip = 1 if (tile[h:, :h] == NEG).all() else 2 if (tile[:h, h:] == NEG).all() else 0
        out[i] = (bool((tile != 0).any()), skip)
    return out


def _tile_parts(blk, skip):
    lo, hi, full = slice(0, blk // 2), slice(blk // 2, blk), slice(0, blk)
    return {0: [(full, full)], 1: [(lo, full), (hi, hi)], 2: [(hi, full), (lo, lo)]}[skip]


class Rider(NamedTuple):
    args: list
    in_specs: list
    out_shape: list
    out_specs: list
    scratch: list
    start: Callable
    finish: Callable
    stages: tuple = ()


def _ride_along(body, ride, n_prefetch, n_in, n_out, n_scratch, n_steps):
    if ride is None:
        return body

    def wrapped(*refs):
        pre, rest = refs[:n_prefetch], refs[n_prefetch:]
        a = n_in
        b = a + len(ride.args)
        c = b + n_out
        d = c + len(ride.out_shape)
        e = d + n_scratch
        mine = (rest[a:b], rest[c:d], rest[e:])
        t = pl.program_id(0)
        pl.when(t == 0)(lambda: ride.start(*mine))
        for at, stage in ride.stages:
            pl.when(t == at)(functools.partial(stage, *mine))
        body(*pre, *rest[:a], *rest[b:c], *rest[d:e])
        pl.when(t == n_steps - 1)(lambda: ride.finish(*mine))

    return wrapped


def _attn_fwd(name, q, k, kpe, vt, bias_t, steps, blk, ride=None, v_token_major=False):
    seq = q.shape[0]
    mla = kpe is not None
    n_steps = int(steps[0].shape[0])
    variants = _tile_variants(bias_t)

    def body(qi_r, ki_r, bi_r, fi_r, la_r, *refs):
        if mla:
            q_ref, k_ref, kpe_ref, vt_ref, b_ref, o_ref, lse_ref, m_sc, l_sc, acc_sc, st_sc = refs
        else:
            q_ref, k_ref, vt_ref, b_ref, o_ref, lse_ref, m_sc, l_sc, acc_sc, st_sc = refs
        t = pl.program_id(0)

        @pl.when(fi_r[t] == 1)
        def _():
            m_sc[...] = jnp.full(m_sc.shape, NEG, F32)
            l_sc[...] = jnp.zeros(l_sc.shape, F32)
            acc_sc[...] = jnp.zeros(acc_sc.shape, F32)

        lane = lax.broadcasted_iota(jnp.int32, (1, LANES), 1)
        if v_token_major:
            vt_all = vt_ref[...].astype(F32).T.astype(BF16)
            vt_rows = lambda rows, ks: vt_all[rows, ks]
        else:
            vt_rows = lambda rows, ks: vt_ref[rows, ks]

        def tile_pass(ks, qs, with_bias):
            nk, nq = ks.stop - ks.start, qs.stop - qs.start
            ones = jnp.ones((16, nk), BF16)

            def pair_scores(j):
                qc, kes = _pair_operands(q_ref, k_ref, kpe_ref if mla else None, lane, j, ks, qs)
                st = lax.dot_general(jnp.concatenate(kes, axis=0), qc, NT, preferred_element_type=F32)
                maxes = []
                for e in range(2):
                    se = st[e * nk:(e + 1) * nk]
                    if with_bias:
                        se = se + b_ref[0, ks, qs]
                    st_sc[j % 2, e * nk:(e + 1) * nk, 0:nq] = se
                    maxes.append(jnp.max(se, axis=0, keepdims=True))
                return maxes

            def softmax_pv(h, col_max):
                st = st_sc[(h // 2) % 2, (h % 2) * nk:(h % 2 + 1) * nk, 0:nq]
                hrow = slice(h, h + 1)
                m_prev = m_sc[hrow, qs]
                m_new = jnp.maximum(m_prev, col_max)
                alpha = jnp.exp2(m_prev - m_new)
                pt = jnp.exp2(st - m_new).astype(BF16)
                m_sc[hrow, qs] = m_new
                rows = slice(64 * h, 64 * h + 64)
                res = jnp.dot(jnp.concatenate([vt_rows(rows, ks), ones], axis=0), pt, preferred_element_type=F32)
                acc_sc[rows, qs] = alpha * acc_sc[rows, qs] + res[:64]
                l_sc[hrow, qs] = alpha * l_sc[hrow, qs] + res[64:65]

            maxes = pair_scores(0)
            for j in range(HEADS // 2):
                cur = maxes
                if j + 1 < HEADS // 2:
                    maxes = pair_scores(j + 1)
                softmax_pv(2 * j, cur[0])
                softmax_pv(2 * j + 1, cur[1])

        def step(with_bias, skip):
            for ks, qs in _tile_parts(blk, skip):
                tile_pass(ks, qs, with_bias)

        for idx, (with_bias, skip) in variants.items():
            if len(variants) == 1:
                step(with_bias, skip)
            else:
                pl.when(bi_r[t] == idx)(functools.partial(step, with_bias, skip))

        @pl.when(la_r[t] == 1)
        def _():
            for h in range(HEADS):
                rows = slice(64 * h, 64 * h + 64)
                acc_sc[rows, :] = acc_sc[rows, :] / l_sc[h:h + 1, :]
            o_ref[...] = acc_sc[...].T
            lse_ref[...] = m_sc[...] + jnp.log2(l_sc[...])

    qmap = lambda t, qi, ki, bi, fi, la: (qi[t], 0)
    kmap = lambda t, qi, ki, bi, fi, la: (ki[t], 0)
    in_specs = [pl.BlockSpec((blk, q.shape[1]), qmap), pl.BlockSpec((blk, HW), kmap)]
    args = [q, k]
    if mla:
        in_specs.append(pl.BlockSpec((blk, LANES), kmap))
        args.append(kpe)
    in_specs += [pl.BlockSpec((blk, HW), kmap) if v_token_major else
                 pl.BlockSpec((HW, blk), lambda t, qi, ki, bi, fi, la: (0, ki[t])),
                 pl.BlockSpec((1, blk, blk), lambda t, qi, ki, bi, fi, la: (bi[t], 0, 0))]
    args += [vt, jnp.asarray(bias_t)]
    out_specs = [pl.BlockSpec((blk, HW), qmap), pl.BlockSpec((HEADS, blk), lambda t, qi, ki, bi, fi, la: (0, qi[t]))]
    out_shape = [jax.ShapeDtypeStruct((seq, HW), F32), jax.ShapeDtypeStruct((HEADS, seq), F32)]
    scratch = [pltpu.VMEM((HEADS, blk), F32), pltpu.VMEM((HEADS, blk), F32),
               pltpu.VMEM((HW, blk), F32), pltpu.VMEM((2, 2 * blk, blk), F32)]
    body = _ride_along(body, ride, 5, len(args), len(out_shape), len(scratch), n_steps)
    if ride is not None:
        args, in_specs = args + ride.args, in_specs + ride.in_specs
        out_specs, out_shape, scratch = out_specs + ride.out_specs, out_shape + ride.out_shape, scratch + ride.scratch
    return _pcall(
        body, name=name,
        grid_spec=pltpu.PrefetchScalarGridSpec(
            num_scalar_prefetch=5, grid=(n_steps,), in_specs=in_specs, out_specs=out_specs, scratch_shapes=scratch),
        out_shape=out_shape,
        compiler_params=_cparams(dimension_semantics=("arbitrary",)),
    )(*steps, *args)


def _attn_bwd(name, q, k, kpe, v, kt, kpet, bias_t, do, lse, dstat, steps, blk, ride=None, single_visit=False):
    assert not (single_visit and kpe is not None) and (kt is not None or single_visit)
    seq = q.shape[0]
    mla = kpe is not None
    qw = q.shape[1]
    n_steps = int(steps[0].shape[0])
    dk_dtype = BF16 if mla else F32
    variants = _tile_variants(bias_t)

    def body(qi_r, ki_r, bi_r, fi_r, la_r, *refs):
        if mla:
            (q_ref, k_ref, kpe_ref, v_ref, kt_ref, kpet_ref, b_ref, do_ref, lse_ref, d_ref,
             dq_ref, dk_ref, dkpe_ref, dv_ref, dk_sc, dkpe_sc, dv_sc, st_sc, dpt_sc) = refs
        else:
            q_ref, k_ref, v_ref, *rest = refs
            kt_ref = rest.pop(0) if kt is not None else None
            b_ref, do_ref, lse_ref, d_ref, dq_ref, dk_ref, dv_ref, *rest = rest
            dq_tok_ref = rest.pop(0) if single_visit else None
            dk_sc, dv_sc, st_sc, dpt_sc = rest
        t = pl.program_id(0)

        @pl.when(t == 0)
        def _():
            dq_ref[...] = jnp.zeros(dq_ref.shape, F32)

        @pl.when(fi_r[t] == 1)
        def _():
            dk_sc[...] = jnp.zeros(dk_sc.shape, F32)
            dv_sc[...] = jnp.zeros(dv_sc.shape, F32)
            if mla:
                dkpe_sc[...] = jnp.zeros(dkpe_sc.shape, F32)

        qi = qi_r[t]
        lane = lax.broadcasted_iota(jnp.int32, (1, LANES), 1)
        if kt is None:
            kt_all = k_ref[...].astype(F32).T.astype(BF16)
            kt_rows = lambda rows, ks: kt_all[rows, ks]
        else:
            kt_rows = lambda rows, ks: kt_ref[rows, ks]

        def tile_pass(ks, qs, with_bias):
            nk, nq = ks.stop - ks.start, qs.stop - qs.start

            def pair_matmuls(j):
                cols = slice(LANES * j, LANES * (j + 1))
                qc, kes = _pair_operands(q_ref, k_ref, kpe_ref if mla else None, lane, j, ks, qs)
                st_sc[j % 2, 0:2 * nk, 0:nq] = lax.dot_general(
                    jnp.concatenate(kes, axis=0), qc, NT, preferred_element_type=F32)
                vj = v_ref[ks, cols]
                ves = [_masked(_head_masks(lane, h)[0], vj) for h in (2 * j, 2 * j + 1)]
                dpt_sc[j % 2, 0:2 * nk, 0:nq] = lax.dot_general(
                    jnp.concatenate(ves, axis=0), do_ref[qs, cols], NT, preferred_element_type=F32)

            def pair_grads(j):
                cols = slice(LANES * j, LANES * (j + 1))
                qj, doj = q_ref[qs, cols], do_ref[qs, cols]
                if mla:
                    qr = q_ref[qs, HW + LANES * (j // 2):HW + LANES * (j // 2 + 1)]
                pts, dsts, qms, doms = [], [], [], []
                for e in range(2):
                    h = 2 * j + e
                    me, mr = _head_masks(lane, h)
                    st = st_sc[j % 2, e * nk:(e + 1) * nk, 0:nq]
                    if with_bias:
                        st = st + b_ref[0, ks, qs]
                    pt = jnp.exp2(st - lse_ref[h:h + 1, qs])
                    dst = (pt * (dpt_sc[j % 2, e * nk:(e + 1) * nk, 0:nq] - d_ref[h:h + 1, qs])).astype(BF16)
                    pts.append(pt.astype(BF16))
                    dsts.append(dst)
                    doms.append(_masked(me, doj))
                    qm = _masked(me, qj)
                    if mla:
                        qm = jnp.concatenate([qm, _masked(mr, qr)], axis=1)
                    qms.append(qm)
                    ktl = kt_rows(slice(64 * h, 64 * h + 64), ks)
                    if mla:
                        ktl = jnp.concatenate([ktl, kpet_ref[:, ks]], axis=0)
                    dqc = jnp.dot(ktl, dst, preferred_element_type=F32)
                    dq_ref[qi, 64 * h:64 * h + 64, qs] += dqc[:64]
                    if mla:
                        dq_ref[qi, HW + MLA_ROPE * h:HW + MLA_ROPE * (h + 1), qs] += dqc[64:]
                dv_sc[ks, cols] += jnp.dot(
                    jnp.concatenate(pts, axis=1), jnp.concatenate(doms, axis=0), preferred_element_type=F32)
                dkc = jnp.dot(jnp.concatenate(dsts, axis=1), jnp.concatenate(qms, axis=0), preferred_element_type=F32)
                dk_sc[ks, cols] += dkc[:, :LANES]
                if mla:
                    dkpe_sc[ks, :] += dkc[:, LANES:]

            pair_matmuls(0)
            for j in range(HEADS // 2):
                if j + 1 < HEADS // 2:
                    pair_matmuls(j + 1)
                pair_grads(j)

        def step(with_bias, skip):
            for ks, qs in _tile_parts(blk, skip):
                tile_pass(ks, qs, with_bias)

        for idx, (with_bias, skip) in variants.items():
            if len(variants) == 1:
                step(with_bias, skip)
            else:
                pl.when(bi_r[t] == idx)(functools.partial(step, with_bias, skip))
        if single_visit:
            dq_tok_ref[...] = dq_ref[qi].T

        @pl.when(la_r[t] == 1)
        def _():
            dk_ref[...] = (dk_sc[...] * LN2).astype(dk_ref.dtype)
            dv_ref[...] = dv_sc[...].astype(dv_ref.dtype)
            if mla:
                dkpe_ref[...] = dkpe_sc[...] * LN2

    qmap = lambda t, qi, ki, bi, fi, la: (qi[t], 0)
    kmap = lambda t, qi, ki, bi, fi, la: (ki[t], 0)
    qmap_t = lambda t, qi, ki, bi, fi, la: (0, qi[t])
    kmap_t = lambda t, qi, ki, bi, fi, la: (0, ki[t])
    in_specs = [pl.BlockSpec((blk, qw), qmap), pl.BlockSpec((blk, HW), kmap)]
    args = [q, k]
    if mla:
        in_specs.append(pl.BlockSpec((blk, LANES), kmap))
        args.append(kpe)
    in_specs.append(pl.BlockSpec((blk, HW), kmap))
    args.append(v)
    if kt is not None:
        in_specs.append(pl.BlockSpec((HW, blk), kmap_t))
        args.append(kt)
    if mla:
        in_specs.append(pl.BlockSpec((MLA_ROPE, blk), kmap_t))
        args.append(kpet)
    in_specs += [pl.BlockSpec((1, blk, blk), lambda t, qi, ki, bi, fi, la: (bi[t], 0, 0)),
                 pl.BlockSpec((blk, HW), qmap), pl.BlockSpec((HEADS, blk), qmap_t), pl.BlockSpec((HEADS, blk), qmap_t)]
    args += [jnp.asarray(bias_t), do, lse, dstat]
    dq_shape = (seq // blk, qw, blk)
    out_specs = [pl.BlockSpec(dq_shape, lambda t, qi, ki, bi, fi, la: (0, 0, 0)), pl.BlockSpec((blk, HW), kmap)]
    out_shape = [jax.ShapeDtypeStruct(dq_shape, F32), jax.ShapeDtypeStruct((seq, HW), dk_dtype)]
    scratch = [pltpu.VMEM((blk, HW), F32)]
    if mla:
        out_specs.append(pl.BlockSpec((blk, LANES), kmap))
        out_shape.append(jax.ShapeDtypeStruct((seq, LANES), F32))
        scratch.append(pltpu.VMEM((blk, LANES), F32))
    out_specs.append(pl.BlockSpec((blk, HW), kmap))
    out_shape.append(jax.ShapeDtypeStruct((seq, HW), BF16))
    if single_visit:
        out_specs.append(pl.BlockSpec((blk, qw), qmap))
        out_shape.append(jax.ShapeDtypeStruct((seq, qw), F32))
    scratch.append(pltpu.VMEM((blk, HW), F32))
    scratch += [pltpu.VMEM((2, 2 * blk, blk), F32), pltpu.VMEM((2, 2 * blk, blk), F32)]
    body = _ride_along(body, ride, 5, len(args), len(out_shape), len(scratch), n_steps)
    if ride is not None:
        args, in_specs = args + ride.args, in_specs + ride.in_specs
        out_specs, out_shape, scratch = out_specs + ride.out_specs, out_shape + ride.out_shape, scratch + ride.scratch
    return _pcall(
        body, name=name,
        grid_spec=pltpu.PrefetchScalarGridSpec(
            num_scalar_prefetch=5, grid=(n_steps,), in_specs=in_specs, out_specs=out_specs,
            scratch_shapes=scratch),
        out_shape=out_shape,
        compiler_params=_cparams(dimension_semantics=("arbitrary",)),
    )(*steps, *args)


def _out_ln(oa, ob_near, ob_far, lse_near, lse_far, ga, gb, x, tgt, w_out, ln_g, ln_b, bt):
    seq = x.shape[0]

    def body(oa_ref, obn_ref, obf_ref, lsen_ref, lsef_ref, ga_ref, gb_ref, x_ref, tgt_ref, w_ref, g_ref, b_ref,
             dz_ref, doa_ref, dob_ref, dga_ref, dgb_ref, da_ref, db_ref, lse_ref, gwb_ref, small_ref, dobc_ref,
             gw_ref, lanes_sc):
        i = pl.program_id(0)

        @pl.when(i == 0)
        def _():
            gw_ref[...] = jnp.zeros(gw_ref.shape, F32)
            small_ref[...] = jnp.zeros(small_ref.shape, F32)

        def gate(g):
            sig = 0.5 * jnp.tanh(0.5 * g) + 0.5
            return g * sig, sig * (1.0 + g * (1.0 - sig))

        lse_n, lse_f = lsen_ref[...], lsef_ref[...]
        top = jnp.maximum(lse_n, lse_f)
        e_n, e_f = jnp.exp2(lse_n - top), jnp.exp2(lse_f - top)
        lse_ref[...] = top + jnp.log2(e_n + e_f)
        inv = 1.0 / (e_n + e_f)
        head_row = lax.broadcasted_iota(jnp.int32, (2 * HEADS, HW), 0) % HEADS
        spread = (head_row == lax.broadcasted_iota(jnp.int32, (2 * HEADS, HW), 1) // 64).astype(BF16)

        def per_lane(w):
            hi = w.astype(BF16)
            lo = (w - hi.astype(F32)).astype(BF16)
            return lax.dot_general(jnp.concatenate([hi, lo], axis=0), spread, TN, preferred_element_type=F32)

        o_a = oa_ref[...]
        o_b = per_lane(e_n * inv) * obn_ref[...] + per_lane(e_f * inv) * _in_sequence(obf_ref, lanes_sc)
        g_a, g_b = ga_ref[...], gb_ref[...]
        sa, dsa = gate(g_a)
        sb, dsb = gate(g_b)
        mix = jnp.concatenate([o_a * sa, o_b * sb], axis=1).astype(BF16)
        z = ALPHA * x_ref[...] + jnp.dot(mix, w_ref[...], preferred_element_type=F32)
        mu = jnp.mean(z, axis=1, keepdims=True)
        zc = z - mu
        rstd = lax.rsqrt(jnp.mean(zc * zc, axis=1, keepdims=True) + LN_EPS)
        xhat = zc * rstd
        gam = g_ref[...]
        diff = xhat * gam + b_ref[...] - tgt_ref[...]
        dy = diff * (1.0 / D_MODEL)
        small_ref[0:1, :] += jnp.sum(dy * xhat, axis=0, keepdims=True)
        small_ref[1:2, :] += jnp.sum(dy, axis=0, keepdims=True)
        small_ref[2:3, :] += jnp.sum(diff * diff, axis=0, keepdims=True)
        dxh = dy * gam
        dz = rstd * (dxh - jnp.mean(dxh, axis=1, keepdims=True) - xhat * jnp.mean(dxh * xhat, axis=1, keepdims=True))
        dz_ref[...] = dz
        dzb = dz.astype(BF16)
        gw_ref[...] += lax.dot_general(mix, dzb, TN, preferred_element_type=F32)

        @pl.when(i == seq // bt - 1)
        def _():
            gwb_ref[...] = gw_ref[...].astype(BF16)

        dmix = lax.dot_general(dzb, w_ref[...], NT, preferred_element_type=F32)
        doa, dob = dmix[:, :HW] * sa, dmix[:, HW:] * sb
        doa_ref[...] = doa.astype(BF16)
        dob_ref[...] = dob.astype(BF16)
        _by_class(dob, dobc_ref, lanes_sc)
        dga_ref[...] = (dmix[:, :HW] * o_a * dsa).astype(BF16)
        dgb_ref[...] = (dmix[:, HW:] * o_b * dsb).astype(BF16)
        head_of = (lax.broadcasted_iota(jnp.int32, (2 * HW, LANES), 0) % HW) // 64
        ind = (head_of == lax.broadcasted_iota(jnp.int32, (2 * HW, LANES), 1)).astype(BF16)

        def head_sums(prod):
            hi = prod.astype(BF16)
            lo = (prod - hi.astype(F32)).astype(BF16)
            sums = jnp.dot(jnp.concatenate([hi, lo], axis=1), ind, preferred_element_type=F32)
            return sums.T[:HEADS, :]

        da_ref[...] = head_sums(doa * o_a)
        db_ref[...] = head_sums(dob * o_b)

    def tok(width):
        return pl.BlockSpec((bt, width), lambda i: (i, 0))

    def full(shape):
        return pl.BlockSpec(shape, lambda i: (0,) * len(shape))

    stat = pl.BlockSpec((HEADS, bt), lambda i: (0, i))
    n_cls = ob_far.shape[0]
    by_class = pl.BlockSpec((n_cls, bt // n_cls, HW), lambda i: (0, i, 0))
    return _pcall(
        body, name="out_ln", grid=(seq // bt,),
        in_specs=[tok(HW), tok(HW), by_class, stat, stat, tok(HW), tok(HW), tok(D_MODEL), tok(D_MODEL),
                  full((D_MODEL, D_MODEL)), full((1, D_MODEL)), full((1, D_MODEL))],
        out_specs=[tok(D_MODEL), tok(HW), tok(HW), tok(HW), tok(HW), stat, stat, stat,
                   full((D_MODEL, D_MODEL)), full((8, D_MODEL)), by_class],
        out_shape=[jax.ShapeDtypeStruct((seq, D_MODEL), F32)] + [jax.ShapeDtypeStruct((seq, HW), BF16)] * 4
        + [jax.ShapeDtypeStruct((HEADS, seq), F32)] * 3
        + [jax.ShapeDtypeStruct((D_MODEL, D_MODEL), BF16), jax.ShapeDtypeStruct((8, D_MODEL), F32),
           jax.ShapeDtypeStruct(ob_far.shape, BF16)],
        scratch_shapes=[pltpu.VMEM((D_MODEL, D_MODEL), F32), pltpu.VMEM((HW // LANES, bt, LANES), F32)],
        compiler_params=_cparams(dimension_semantics=("arbitrary",)),
    )(oa, ob_near, ob_far, lse_near, lse_far, ga, gb, x, tgt, w_out, ln_g, ln_b)


def _bwd_mid(dq_m, dkn, dv, dkpe, dqb, dkb, dvb, far, dga, dgb, cq, ckv, qn, kvn, w_uq_r, w_ukv_r, qg, kvg, tabs, bt):
    n_cls = far[0].shape[0]
    seq = cq.shape[0]

    def body(dqm_ref, dkn_ref, dv_ref, dkpe_ref, dqb_ref, dkb_ref, dvb_ref, dqf_ref, dkf_ref, dvf_ref, dga_ref, dgb_ref,
             cq_ref, ckv_ref, qn_ref, kvn_ref, wuq_ref, wukv_ref, qg_ref, kvg_ref, tab_ref,
             dh_ref, guq3_ref, gukv3_ref, small_ref, seq_sc, guq_ref, gukv_ref):
        i = pl.program_id(0)

        @pl.when(i == 0)
        def _():
            guq_ref[...] = jnp.zeros(guq_ref.shape, F32)
            gukv_ref[...] = jnp.zeros(gukv_ref.shape, F32)
            small_ref[...] = jnp.zeros(small_ref.shape, F32)

        m_tabs = (tab_ref[0], tab_ref[1], tab_ref[2])
        d_tabs = (tab_ref[3], tab_ref[4], tab_ref[5])

        def rms_bwd(c, dn, gain):
            r = lax.rsqrt(jnp.mean(c * c, axis=1, keepdims=True) + RMS_EPS)
            u = dn * gain
            dc = r * u - c * (r * r * r) * jnp.mean(u * c, axis=1, keepdims=True)
            return dc, jnp.sum(dn * c * r, axis=0, keepdims=True)

        dqm = dqm_ref[0].T
        dq = jnp.concatenate(
            [dqm[:, :HW], _rope_wide(_rope_t, dqm[:, HW:], *m_tabs, MLA_ROPE // 2)], axis=1) * MLA_SCALE
        dq = dq.astype(BF16)
        guq_ref[...] += lax.dot_general(qn_ref[...], dq, TN, preferred_element_type=F32)
        dqn = lax.dot_general(dq, wuq_ref[...], NT, preferred_element_type=F32)
        dcq, gq = rms_bwd(cq_ref[...], dqn, qg_ref[...])
        small_ref[0:1, :] += gq

        dkv = jnp.concatenate([dkn_ref[...], dv_ref[...]], axis=1)
        gukv_ref[...] += lax.dot_general(kvn_ref[...], dkv, TN, preferred_element_type=F32)
        dkvn = lax.dot_general(dkv, wukv_ref[...], NT, preferred_element_type=F32)
        dckv, gkv = rms_bwd(ckv_ref[...], dkvn, kvg_ref[...])
        small_ref[1:2, :KV_RANK] += gkv

        dh_ref[:, C_CQ:C_CKV] = dcq.astype(BF16)
        dh_ref[:, C_CKV:C_KR] = dckv.astype(BF16)
        dh_ref[:, C_KR:C_GA] = _rope_t(dkpe_ref[...], *m_tabs, MLA_ROPE // 2).astype(BF16)
        dh_ref[:, C_GA:C_QB] = dga_ref[...]
        in_sequence = functools.partial(_in_sequence, lanes_sc=seq_sc)
        dqb = dqb_ref[0].T + in_sequence(dqf_ref)
        dh_ref[:, C_QB:C_KB] = (_rope_wide(_rope_t, dqb, *d_tabs, DIL_ROT // 2) * DIL_SCALE).astype(BF16)
        dkb = dkb_ref[...] + in_sequence(dkf_ref)
        dh_ref[:, C_KB:C_VB] = _rope_wide(_rope_t, dkb, *d_tabs, DIL_ROT // 2).astype(BF16)
        dh_ref[:, C_VB:C_GB] = (dvb_ref[...].astype(F32) + in_sequence(dvf_ref)).astype(BF16)
        dh_ref[:, C_GB:C_END] = dgb_ref[...]

        @pl.when(i == seq // bt - 1)
        def _():
            for h in range(HEADS):
                guq3_ref[h] = jnp.concatenate(
                    [guq_ref[:, MLA_NOPE * h:MLA_NOPE * (h + 1)],
                     guq_ref[:, HW + MLA_ROPE * h:HW + MLA_ROPE * (h + 1)]], axis=1).astype(BF16)
                gukv3_ref[h] = jnp.concatenate(
                    [gukv_ref[:, MLA_NOPE * h:MLA_NOPE * (h + 1)],
                     gukv_ref[:, HW + MLA_V * h:HW + MLA_V * (h + 1)]], axis=1).astype(BF16)

    def tok(width):
        return pl.BlockSpec((bt, width), lambda i: (i, 0))

    def tok_t(a):
        per = a.shape[2] // bt
        return pl.BlockSpec((1, a.shape[1], bt), lambda i: (i // per, 0, i % per))

    def full(shape):
        return pl.BlockSpec(shape, lambda i: (0,) * len(shape))

    by_class = pl.BlockSpec((n_cls, bt // n_cls, HW), lambda i: (0, i, 0))
    uq3 = (HEADS, Q_RANK, MLA_NOPE + MLA_ROPE)
    ukv3 = (HEADS, KV_RANK, MLA_NOPE + MLA_V)
    return _pcall(
        body, name="bwd_mid", grid=(seq // bt,),
        in_specs=[tok_t(dq_m), tok(HW), tok(HW), tok(LANES), tok_t(dqb), tok(HW), tok(HW), by_class, by_class, by_class,
                  tok(HW), tok(HW),
                  tok(Q_RANK), tok(KV_RANK), tok(Q_RANK), tok(KV_RANK),
                  full(w_uq_r.shape), full(w_ukv_r.shape), full((1, Q_RANK)), full((1, KV_RANK)),
                  pl.BlockSpec((6, bt, LANES), lambda i: (0, i, 0))],
        out_specs=[tok(C_END), full(uq3), full(ukv3), full((8, Q_RANK))],
        out_shape=[jax.ShapeDtypeStruct((seq, C_END), BF16), jax.ShapeDtypeStruct(uq3, BF16),
                   jax.ShapeDtypeStruct(ukv3, BF16), jax.ShapeDtypeStruct((8, Q_RANK), F32)],
        scratch_shapes=[pltpu.VMEM((HW // LANES, bt, LANES), F32), pltpu.VMEM(w_uq_r.shape, F32),
                        pltpu.VMEM(w_ukv_r.shape, F32)],
        compiler_params=_cparams(dimension_semantics=("arbitrary",)),
    )(dq_m, dkn, dv, dkpe, dqb, dkb, dvb, *far, dga, dgb, cq, ckv, qn, kvn, w_uq_r, w_ukv_r, qg, kvg, tabs)


def _grad_x(dz, dh, w_in_r, bt, ride=None):
    seq = dz.shape[0]
    n_steps = seq // bt

    def body(dz_ref, dh_ref, w_ref, gx_ref):
        gx_ref[...] = ALPHA * dz_ref[...] + lax.dot_general(
            dh_ref[...], w_ref[...], NT, preferred_element_type=F32)

    args = [dz, dh, w_in_r]
    in_specs = [pl.BlockSpec((bt, D_MODEL), lambda i: (i, 0)), pl.BlockSpec((bt, C_END), lambda i: (i, 0)),
                pl.BlockSpec(w_in_r.shape, lambda i: (0, 0))]
    out_specs = [pl.BlockSpec((bt, D_MODEL), lambda i: (i, 0))]
    out_shape = [jax.ShapeDtypeStruct((seq, D_MODEL), F32)]
    scratch = []
    body = _ride_along(body, ride, 0, len(args), len(out_shape), 0, n_steps)
    if ride is not None:
        args, in_specs = args + ride.args, in_specs + ride.in_specs
        out_specs, out_shape, scratch = out_specs + ride.out_specs, out_shape + ride.out_shape, ride.scratch
    return _pcall(
        body, name="grad_x", grid=(n_steps,),
        in_specs=in_specs, out_specs=out_specs, out_shape=out_shape, scratch_shapes=scratch,
        compiler_params=_cparams(dimension_semantics=("arbitrary",)),
    )(*args)


def _grad_w_in(x, dh, bt):
    seq = x.shape[0]
    shard = IN_WIDTH // N_DEV
    k_lo, k_hi = IN_SPLITS[0] + IN_SPLITS[1], IN_SPLITS[0] + IN_SPLITS[1] + MLA_ROPE

    def body(x_ref, dh_ref, out_ref, acc):
        i = pl.program_id(0)

        @pl.when(i == 0)
        def _():
            acc[...] = jnp.zeros(acc.shape, F32)

        acc[...] += lax.dot_general(x_ref[...].astype(BF16), dh_ref[...], TN, preferred_element_type=F32)

        @pl.when(i == seq // bt - 1)
        def _():
            kr = acc[:, C_KR:C_GA]
            kr = kr + pltpu.roll(kr, 96, 1) + pltpu.roll(kr, 64, 1) + pltpu.roll(kr, 32, 1)
            for d in range(N_DEV):
                lo, hi = shard * d, shard * (d + 1)
                pieces = []
                if lo < k_lo:
                    pieces.append(acc[:, lo:min(hi, k_lo)])
                if lo < k_hi and hi > k_lo:
                    pieces.append(kr[:, max(lo, k_lo) - k_lo:min(hi, k_hi) - k_lo])
                if hi > k_hi:
                    shift = C_GA - k_hi
                    pieces.append(acc[:, max(lo, k_hi) + shift:hi + shift])
                blk = pieces[0] if len(pieces) == 1 else jnp.concatenate(pieces, axis=1)
                out_ref[d] = blk.astype(BF16)

    return _pcall(
        body, name="grad_w_in", grid=(seq // bt,),
        in_specs=[pl.BlockSpec((bt, D_MODEL), lambda i: (i, 0)), pl.BlockSpec((bt, C_END), lambda i: (i, 0))],
        out_specs=pl.BlockSpec((N_DEV, D_MODEL, shard), lambda i: (0, 0, 0)),
        out_shape=jax.ShapeDtypeStruct((N_DEV, D_MODEL, shard), BF16),
        scratch_shapes=[pltpu.VMEM((D_MODEL, C_END), F32)],
        compiler_params=_cparams(dimension_semantics=("arbitrary",)),
    )(x, dh)


def _local_step(x, tgt, w_in_r, w_uq_r, w_ukv_r, w_out_rider, g_out_rider, reduce_rider, q_norm_g, kv_norm_g,
                ln_g, ln_b, bt=BLOCK_TOKENS, blk_m=BLOCK_MLA, blk_d=BLOCK_DIL):
    seq = x.shape[0]
    tabs = jnp.asarray(_rope_tables(seq))
    qg, kvg = q_norm_g.reshape(1, -1), kv_norm_g.reshape(1, -1)

    far_dil = DIL_CONFIGS[-1][1]
    cls = seq // far_dil
    (cq, ckv, qn, kvn, qcat, kn, kpe, v, ga, gb, qb, kb, vb, knt, kpet, vt, kbt, vbt, qb_c, kb_c, vb_c) = _fwd_proj(
        x, w_in_r, w_uq_r, w_ukv_r, qg, kvg, tabs, bt, far_dil)
    qb_c, kb_c, vb_c = (a.reshape(seq, HW) for a in (qb_c, kb_c, vb_c))

    nq_m, nq_d = seq // blk_m, seq // blk_d
    bias_m = _mla_bias_t(blk_m)
    oa, lse_a, w_out = _attn_fwd(
        "mla_fwd", qcat, kn, kpe, vt, bias_m, _steps(nq_m, nq_m, False, True), blk_m, ride=w_out_rider)

    bias_near = _dil_bias_t(blk_d, DIL_NEAR)
    ob_near, lse_near = _attn_fwd(
        "dil_fwd", qb, kb, None, vbt, bias_near, _steps(nq_d, -(-DIL_NEAR // blk_d), False, False), blk_d)
    each = jnp.arange(far_dil, dtype=jnp.int32)
    steps_far = [each, each, jnp.zeros_like(each), jnp.ones_like(each), jnp.ones_like(each)]
    bias_far = _dil_far_bias_t(cls)
    ob_far, lse_far = _attn_fwd(
        "dil_far_fwd", qb_c, kb_c, None, vb_c, bias_far, steps_far, cls, v_token_major=True)

    dz, doa, dob, dga, dgb, dst_a, dst_b, lse_b, g_out, small1, dob_c = _out_ln(
        oa, ob_near, ob_far.reshape(far_dil, cls, HW), lse_near, _lanes_from_classes(lse_far, far_dil), ga, gb, x, tgt,
        w_out.reshape(D_MODEL, D_MODEL), ln_g.reshape(1, -1), ln_b.reshape(1, -1), bt)

    dq_m, dkn, dkpe, dv, g_out_recv = _attn_bwd(
        "mla_bwd", qcat, kn, kpe, v, knt, kpet, bias_m, doa, lse_a, dst_a, _steps(nq_m, nq_m, True, True), blk_m,
        ride=g_out_rider(g_out.reshape(N_DEV, D_MODEL // N_DEV, D_MODEL)))
    dqb, dkb_near, dvb_near = _attn_bwd(
        "dil_bwd", qb, kb, None, vb, kbt, None, bias_near, dob, lse_b, dst_b,
        _steps(nq_d, -(-DIL_NEAR // blk_d), True, False), blk_d)
    _, dkb_far, dvb_far, dqb_far = _attn_bwd(
        "dil_far_bwd", qb_c, kb_c, None, vb_c, None, None, bias_far, dob_c.reshape(seq, HW),
        _lanes_to_classes(lse_b, far_dil), _lanes_to_classes(dst_b, far_dil), steps_far, cls, single_visit=True)
    far = [a.reshape(far_dil, cls, HW) for a in (dqb_far, dkb_far, dvb_far)]

    dh, g_uq, g_ukv, small2 = _bwd_mid(
        dq_m, dkn, dv, dkpe, dqb, dkb_near, dvb_near, far, dga, dgb, cq, ckv, qn, kvn, w_uq_r, w_ukv_r, qg, kvg, tabs, bt)
    g_in = _grad_w_in(x, dh, bt)
    grads3 = [g_in, g_uq, g_ukv]
    small_part = _small_rows(small1[0], small1[1], small2[0, :Q_RANK], small2[1, :KV_RANK], small1[2])
    grad_x, *reduced = _grad_x(
        dz, dh, w_in_r, bt, ride=reduce_rider(grads3, g_out_recv, small_part, min(2, seq // bt - 1)))
    return grad_x, reduced


MESH_ID = pl.DeviceIdType.MESH
SHARD_SHAPES = ((D_MODEL, IN_WIDTH // N_DEV), (Q_RANK, 768 // N_DEV), (KV_RANK, 1024 // N_DEV), (D_MODEL // N_DEV, D_MODEL))
ADAM_ROWS = (32, 128, 128, 16)


def _me():
    x, y, c = lax.axis_index("x"), lax.axis_index("y"), lax.axis_index("c")
    return x, y, c, 4 * x + 2 * y + c


def _peer(k):
    x, y, c, _ = _me()
    px = 1 - x if (k >> 2) & 1 else x
    py = 1 - y if (k >> 1) & 1 else y
    pc = 1 - c if k & 1 else c
    return (px, py, pc), 4 * px + 2 * py + pc


def _all_gather_weights(shards):
    n = len(shards)
    shard = IN_WIDTH // N_DEV
    k_lo = IN_SPLITS[0] + IN_SPLITS[1]
    k_hi = k_lo + MLA_ROPE

    def body(*refs):
        ins = refs[:n]
        win_ref, wuq_ref, wukv_ref = refs[n:2 * n]
        bufs = refs[2 * n:3 * n]
        send_sems, recv_sems = refs[3 * n:]
        x, y, c, me = _me()
        here, sibling = (x, y, c), (x, y, 1 - c)
        chips = [(1 - x, y), (x, 1 - y), (1 - x, 1 - y)]
        for t in range(n):
            bufs[t][me] = ins[t][...].astype(BF16)

        def copy(t, k, px, py, pc, to):
            blk = bufs[t].at[4 * px + 2 * py + pc]
            return pltpu.make_async_remote_copy(
                src_ref=blk, dst_ref=blk, send_sem=send_sems.at[t, k], recv_sem=recv_sems.at[t, k],
                device_id=to, device_id_type=MESH_ID)

        first = []
        for t in range(n):
            first.append(copy(t, 0, x, y, c, sibling))
            for j, (px, py) in enumerate(chips):
                first.append(copy(t, 1 + j, x, y, c, (px, py, c)))
        for cp in first:
            cp.start()
        passed = []
        for j, (px, py) in enumerate(chips):
            for t in range(n):
                copy(t, 1 + j, px, py, c, here).wait_recv()
                cp = copy(t, 4 + j, px, py, c, sibling)
                cp.start()
                passed.append(cp)
        for t in range(n):
            copy(t, 0, x, y, 1 - c, here).wait_recv()
        for j, (px, py) in enumerate(chips):
            for t in range(n):
                copy(t, 4 + j, px, py, 1 - c, here).wait_recv()
        for cp in first + passed:
            cp.wait_send()

        a_in, a_uq, a_ukv = bufs
        for d in range(N_DEV):
            lo, hi = shard * d, shard * (d + 1)
            if lo < k_lo:
                win_ref[:, lo:min(hi, k_lo)] = a_in[d, :, 0:min(hi, k_lo) - lo]
            if lo < k_hi and hi > k_lo:
                kr = a_in[d, :, k_lo - lo:k_hi - lo]
                for rep in range(4):
                    win_ref[:, C_KR + MLA_ROPE * rep:C_KR + MLA_ROPE * (rep + 1)] = kr
            if hi > k_hi:
                src = max(lo, k_hi)
                win_ref[:, src + C_GA - k_hi:hi + C_GA - k_hi] = a_in[d, :, src - lo:hi - lo]
        for h in range(HEADS):
            wuq_ref[:, MLA_NOPE * h:MLA_NOPE * (h + 1)] = a_uq[h, :, :MLA_NOPE]
            wuq_ref[:, HW + MLA_ROPE * h:HW + MLA_ROPE * (h + 1)] = a_uq[h, :, MLA_NOPE:]
            wukv_ref[:, MLA_NOPE * h:MLA_NOPE * (h + 1)] = a_ukv[h, :, :MLA_NOPE]
            wukv_ref[:, HW + MLA_V * h:HW + MLA_V * (h + 1)] = a_ukv[h, :, MLA_NOPE:]

    vmem = pl.BlockSpec(memory_space=pltpu.VMEM)
    return _pcall(
        body, name="gather_weights",
        in_specs=[vmem] * n, out_specs=[vmem] * n,
        out_shape=[jax.ShapeDtypeStruct((D_MODEL, C_END), BF16), jax.ShapeDtypeStruct((Q_RANK, QW), BF16),
                   jax.ShapeDtypeStruct((KV_RANK, 2 * HW), BF16)],
        scratch_shapes=[pltpu.VMEM((N_DEV,) + s, BF16) for s in SHARD_SHAPES[:n]]
        + [pltpu.SemaphoreType.DMA((n, N_DEV - 1)), pltpu.SemaphoreType.DMA((n, N_DEV - 1))],
        compiler_params=_cparams(),
    )(*shards)


def _gather_w_out_rider(w_out):
    def copies(full_ref, stage, send_sems, recv_sems):
        me = _me()[3]
        out = []
        for k in range(1, N_DEV):
            peer, pidx = _peer(k)
            send = pltpu.make_async_remote_copy(
                src_ref=stage, dst_ref=full_ref.at[me], send_sem=send_sems.at[k - 1], recv_sem=recv_sems.at[k - 1],
                device_id=peer, device_id_type=MESH_ID)
            recv = pltpu.make_async_remote_copy(
                src_ref=stage, dst_ref=full_ref.at[pidx], send_sem=send_sems.at[k - 1], recv_sem=recv_sems.at[k - 1],
                device_id=peer, device_id_type=MESH_ID)
            out.append((send, recv))
        return out

    def start(ins, outs, scr):
        stage, send_sems, recv_sems, own_sem = scr
        stage[...] = ins[0][...].astype(BF16)
        pltpu.make_async_copy(stage, outs[0].at[_me()[3]], own_sem).start()
        for send, _ in copies(outs[0], stage, send_sems, recv_sems):
            send.start()

    def finish(ins, outs, scr):
        stage, send_sems, recv_sems, own_sem = scr
        pltpu.make_async_copy(stage, outs[0].at[_me()[3]], own_sem).wait()
        pairs = copies(outs[0], stage, send_sems, recv_sems)
        for _, recv in pairs:
            recv.wait_recv()
        for send, _ in pairs:
            send.wait_send()

    shape = SHARD_SHAPES[3]
    return Rider(
        args=[w_out], in_specs=[pl.BlockSpec(shape, lambda t, *_: (0, 0))],
        out_shape=[jax.ShapeDtypeStruct((N_DEV,) + shape, BF16)], out_specs=[pl.BlockSpec(memory_space=pl.ANY)],
        scratch=[pltpu.VMEM(shape, BF16), pltpu.SemaphoreType.DMA((N_DEV - 1,)), pltpu.SemaphoreType.DMA((N_DEV - 1,)),
                 pltpu.SemaphoreType.DMA],
        start=start, finish=finish)


def _scatter_g_out_rider(blocks):
    def copies(src_ref, dst_ref, send_sems, recv_sems):
        out = []
        for k in range(1, N_DEV):
            peer, pidx = _peer(k)
            out.append(pltpu.make_async_remote_copy(
                src_ref=src_ref.at[pidx], dst_ref=dst_ref.at[k], send_sem=send_sems.at[k - 1],
                recv_sem=recv_sems.at[k - 1], device_id=peer, device_id_type=MESH_ID))
        return out

    def start(ins, outs, scr):
        send_sems, recv_sems, own_sem = scr
        pltpu.make_async_copy(ins[0].at[_me()[3]], outs[0].at[0], own_sem).start()
        for cp in copies(ins[0], outs[0], send_sems, recv_sems):
            cp.start()

    def finish(ins, outs, scr):
        send_sems, recv_sems, own_sem = scr
        pltpu.make_async_copy(ins[0].at[_me()[3]], outs[0].at[0], own_sem).wait()
        for cp in copies(ins[0], outs[0], send_sems, recv_sems):
            cp.wait()

    hbm = pl.BlockSpec(memory_space=pl.ANY)
    return Rider(
        args=[blocks], in_specs=[hbm], out_shape=[jax.ShapeDtypeStruct(blocks.shape, blocks.dtype)], out_specs=[hbm],
        scratch=[pltpu.SemaphoreType.DMA((N_DEV - 1,)), pltpu.SemaphoreType.DMA((N_DEV - 1,)), pltpu.SemaphoreType.DMA],
        start=start, finish=finish)


def _adamw(w, g, m, v):
    m = ADAM_B1 * m + (1.0 - ADAM_B1) * g
    v = ADAM_B2 * v + (1.0 - ADAM_B2) * jnp.square(g)
    m_hat = m / (1.0 - ADAM_B1 ** ADAM_STEP)
    v_hat = v / (1.0 - ADAM_B2 ** ADAM_STEP)
    delta = -ADAM_LR * (m_hat / (jnp.sqrt(v_hat) + ADAM_EPS) + ADAM_WD * w)
    return delta, m, v


def _reduce_grads_rider(grads3, arrived, small_part, mid_step):
    n = len(grads3)

    class Refs:
        def __init__(self, ins, outs, scr):
            self.g3, self.arr, self.sp = ins[0:n], ins[n], ins[n + 1]
            self.gsum, self.gsum_out, self.ssum = outs[0:n], outs[n], outs[n + 1]
            self.own, self.sib, self.part, self.ici = scr[0:n], scr[n:2 * n], scr[2 * n:3 * n], scr[3 * n:4 * n]
            self.rsmall = scr[4 * n]
            (self.loc_sems, self.d2d_send, self.d2d_recv, self.ici_send, self.ici_recv,
             self.sm_send, self.sm_recv) = scr[4 * n + 1:]
            self.x, self.y, self.c, self.me = _me()
            self.chips = [(1 - self.x, self.y), (self.x, 1 - self.y), (1 - self.x, 1 - self.y)]

        def small(self):
            return [pltpu.make_async_remote_copy(
                src_ref=self.rsmall.at[0], dst_ref=self.rsmall.at[k], send_sem=self.sm_send.at[k - 1],
                recv_sem=self.sm_recv.at[k - 1], device_id=_peer(k)[0], device_id_type=MESH_ID)
                for k in range(1, N_DEV)]

        def level1(self):
            local, to_sib = [], []
            for t in range(n):
                for q in range(4):
                    local.append(pltpu.make_async_copy(
                        self.g3[t].at[2 * q + self.c], self.own[t].at[q], self.loc_sems.at[t, q]))
                    to_sib.append(pltpu.make_async_remote_copy(
                        src_ref=self.g3[t].at[2 * q + 1 - self.c], dst_ref=self.sib[t].at[q],
                        send_sem=self.d2d_send.at[t, q], recv_sem=self.d2d_recv.at[t, q],
                        device_id=(self.x, self.y, 1 - self.c), device_id_type=MESH_ID))
            return local, to_sib

        def level2(self):
            return [pltpu.make_async_remote_copy(
                src_ref=self.part[t].at[2 * px + py], dst_ref=self.ici[t].at[j], send_sem=self.ici_send.at[t, j],
                recv_sem=self.ici_recv.at[t, j], device_id=(px, py, self.c), device_id_type=MESH_ID)
                for t in range(n) for j, (px, py) in enumerate(self.chips)]

    def chunks(t, fn):
        rows = ADAM_ROWS[t]

        def step(i, carry):
            fn(pl.ds(pl.multiple_of(i * rows, rows), rows))
            return carry

        lax.fori_loop(0, SHARD_SHAPES[t][0] // rows, step, 0)

    def start(*refs):
        r = Refs(*refs)
        r.rsmall[0] = r.sp[...]
        local, to_sib = r.level1()
        for cp in r.small() + local + to_sib:
            cp.start()

    def middle(*refs):
        r = Refs(*refs)
        local, to_sib = r.level1()
        for cp in local:
            cp.wait()
        for cp in to_sib:
            cp.wait_recv()
        my_chip = 2 * r.x + r.y
        for t in range(n):
            def pair_sums(rows, t=t):
                for q in range(4):
                    r.part[t][q, rows, :] = (
                        r.own[t][q, rows, :].astype(F32) + r.sib[t][q, rows, :].astype(F32)).astype(BF16)
                r.gsum[t][rows, :] = r.own[t][my_chip, rows, :].astype(F32) + r.sib[t][my_chip, rows, :].astype(F32)

            chunks(t, pair_sums)
        for cp in r.level2():
            cp.start()

    def finish(*refs):
        r = Refs(*refs)

        def add_arrived(rows):
            g = r.arr[0, rows, :].astype(F32)
            for k in range(1, N_DEV):
                g = g + r.arr[k, rows, :].astype(F32)
            r.gsum_out[rows, :] = g

        chunks(3, add_arrived)
        to_chips = r.level2()
        for cp in to_chips:
            cp.wait_recv()
        for t in range(n):
            def add_chips(rows, t=t):
                g = r.gsum[t][rows, :]
                for j in range(3):
                    g = g + r.ici[t][j, rows, :].astype(F32)
                r.gsum[t][rows, :] = g

            chunks(t, add_chips)
        small = r.small()
        for cp in small:
            cp.wait_recv()
        tot = r.rsmall[r.me]
        for d in range(1, N_DEV):
            tot = tot + r.rsmall[jnp.bitwise_xor(r.me, d)]
        r.ssum[...] = tot
        for cp in small + r.level1()[1] + to_chips:
            cp.wait_send()

    hbm = pl.BlockSpec(memory_space=pl.ANY)
    dma = pltpu.SemaphoreType.DMA

    def whole(shape):
        return pl.BlockSpec(shape, lambda i: (0,) * len(shape))

    out_shapes = list(SHARD_SHAPES) + [(8, D_MODEL)]
    return Rider(
        args=list(grads3) + [arrived, small_part],
        in_specs=[hbm] * n + [whole(arrived.shape), whole(small_part.shape)],
        out_shape=[jax.ShapeDtypeStruct(s, F32) for s in out_shapes], out_specs=[whole(s) for s in out_shapes],
        scratch=[pltpu.VMEM((slots,) + s, BF16) for slots in (4, 4, 4, 3) for s in SHARD_SHAPES[:n]]
        + [pltpu.VMEM((N_DEV, 8, D_MODEL), F32), dma((n, 4)), dma((n, 4)), dma((n, 4)), dma((n, 3)), dma((n, 3)),
           dma((N_DEV - 1,)), dma((N_DEV - 1,))],
        start=start, finish=finish, stages=((mid_step, middle),))


def _adamw_update(grads, small_grad, wmv, small_wmv):
    n_small = len(small_wmv)

    def body(*refs):
        g_refs, sg_ref = refs[0:4], refs[4]
        wmv_refs = [refs[5 + 3 * t:8 + 3 * t] for t in range(4)]
        swmv_refs = [refs[17 + 3 * t:20 + 3 * t] for t in range(n_small)]
        outs = refs[17 + 3 * n_small:]
        out_refs = [outs[4 * t:4 * t + 4] for t in range(4)]
        sout_refs = [outs[16 + 4 * t:20 + 4 * t] for t in range(n_small)]
        loss_ref = outs[16 + 4 * n_small]
        for t, (w_ref, m_ref, v_ref) in enumerate(swmv_refs):
            g = sg_ref[t:t + 1, :w_ref.shape[1]]
            delta, m, v = _adamw(w_ref[...], g, m_ref[...], v_ref[...])
            sout_refs[t][0][...], sout_refs[t][1][...], sout_refs[t][2][...], sout_refs[t][3][...] = g, delta, m, v
        loss_ref[...] = (0.5 / D_MODEL) * jnp.sum(sg_ref[n_small:n_small + 1, :], axis=1, keepdims=True)
        for t in range(4):
            rows = ADAM_ROWS[t]
            w_ref, m_ref, v_ref = wmv_refs[t]
            g_out, d_out, m_out, v_out = out_refs[t]

            def step(i, carry, g_ref=g_refs[t], rows=rows, w_ref=w_ref, m_ref=m_ref, v_ref=v_ref,
                     g_out=g_out, d_out=d_out, m_out=m_out, v_out=v_out):
                r = pl.ds(pl.multiple_of(i * rows, rows), rows)
                g = g_ref[r, :]
                delta, m, v = _adamw(w_ref[r, :], g, m_ref[r, :], v_ref[r, :])
                g_out[r, :], d_out[r, :], m_out[r, :], v_out[r, :] = g, delta, m, v
                return carry

            lax.fori_loop(0, SHARD_SHAPES[t][0] // rows, step, 0)

    vmem = pl.BlockSpec(memory_space=pltpu.VMEM)
    flat_wmv = [a for trio in wmv for a in trio]
    flat_small = [a for trio in small_wmv for a in trio]
    out_shape = ([jax.ShapeDtypeStruct(s, F32) for s in SHARD_SHAPES for _ in range(4)]
                 + [jax.ShapeDtypeStruct(trio[0].shape, F32) for trio in small_wmv for _ in range(4)]
                 + [jax.ShapeDtypeStruct((1, 1), F32)])
    return _pcall(
        body, name="adamw",
        in_specs=[vmem] * (5 + len(flat_wmv) + len(flat_small)), out_specs=[vmem] * len(out_shape),
        out_shape=out_shape,
        compiler_params=_cparams(),
    )(*grads, small_grad, *flat_wmv, *flat_small)


def _small_rows(ln_g, ln_b, q_norm_g, kv_norm_g, extra=None):
    pad = lambda a: jnp.pad(a, (0, D_MODEL - a.shape[0]))
    rows = [ln_g, ln_b, pad(q_norm_g), pad(kv_norm_g)] + ([] if extra is None else [extra])
    return jnp.pad(jnp.stack(rows), ((0, 8 - len(rows)), (0, 0)))


def kernel(x, w_in, q_norm_g, kv_norm_g, w_uq, w_ukv, w_out, ln_g, ln_b, loss_target, m_w_in, m_q_norm_g, m_kv_norm_g, m_w_uq, m_w_ukv, m_w_out, m_ln_g, m_ln_b, v_w_in, v_q_norm_g, v_kv_norm_g, v_w_uq, v_w_ukv, v_w_out, v_ln_g, v_ln_b):
    w_in_r, w_uq_r, w_ukv_r = _all_gather_weights([w_in, w_uq, w_ukv])
    grad_x, sums = _local_step(
        x[0], loss_target[0], w_in_r, w_uq_r, w_ukv_r, _gather_w_out_rider(w_out), _scatter_g_out_rider,
        _reduce_grads_rider, q_norm_g, kv_norm_g, ln_g, ln_b)
    row = lambda a: a.reshape(1, -1)
    small_wmv = [(row(ln_g), row(m_ln_g), row(v_ln_g)), (row(ln_b), row(m_ln_b), row(v_ln_b)),
                 (row(q_norm_g), row(m_q_norm_g), row(v_q_norm_g)), (row(kv_norm_g), row(m_kv_norm_g), row(v_kv_norm_g))]
    wmv = [(w_in, m_w_in, v_w_in), (w_uq, m_w_uq, v_w_uq), (w_ukv, m_w_ukv, v_w_ukv), (w_out, m_w_out, v_w_out)]
    res = _adamw_update(sums[:4], sums[4], wmv, small_wmv)
    big = [res[4 * t:4 * t + 4] for t in range(4)]
    small = [[a.reshape(-1) for a in res[16 + 4 * t:20 + 4 * t]] for t in range(4)]
    loss = res[32].reshape(())

    def group(kind):
        return (big[0][kind], small[2][kind], small[3][kind], big[1][kind], big[2][kind], big[3][kind],
                small[0][kind], small[1][kind])

    return (loss, grad_x[None], *group(0), *group(1), *group(2), *group(3))
```

```python
import functools
from typing import Callable, NamedTuple

import numpy as np
import jax
import jax.numpy as jnp
from jax import lax
from jax.experimental import pallas as pl
from jax.experimental.pallas import tpu as pltpu

F32 = jnp.float32
BF16 = jnp.bfloat16

D_MODEL = 1024
ROPE_THETA = 500000.0
NEG = -1e30
RMS_EPS = 1e-6
LN_EPS = 1e-5
HEADS = 8
MLA_NOPE = 64
MLA_ROPE = 32
MLA_V = 64
Q_RANK = 384
KV_RANK = 256
DIL_HEAD = 64
DIL_ROT = 16
DIL_CONFIGS = ((128, 1), (512, 4), (2048, 16))
DIL_NEAR = 512
HW = HEADS * 64
QW = HW + HEADS * MLA_ROPE
IN_SPLITS = (Q_RANK, KV_RANK, MLA_ROPE, HW, HW, HW, HW, HW)
IN_WIDTH = sum(IN_SPLITS)
ALPHA = 2.0 ** 0.25
MLA_SCALE = (MLA_NOPE + MLA_ROPE) ** -0.5
DIL_SCALE = DIL_HEAD ** -0.5
LOG2E = 1.4426950408889634
LN2 = 0.6931471805599453

ADAM_LR = 0.001
ADAM_B1 = 0.9
ADAM_B2 = 0.999
ADAM_EPS = 1e-08
ADAM_WD = 0.01
ADAM_STEP = 10

N_DEV = 8
LANES = 128
VMEM_LIMIT = 56 * 1024 * 1024
BLOCK_TOKENS = 512
BLOCK_MLA = 512
BLOCK_DIL = 512

C_CQ, C_CKV, C_KR, C_GA, C_QB, C_KB, C_VB, C_GB, C_END = 0, 384, 640, 768, 1280, 1792, 2304, 2816, 3328

NT = (((1,), (1,)), ((), ()))
TN = (((0,), (0,)), ((), ()))


def _pcall(body, **kw):
    return pl.pallas_call(body, **kw)


def _cparams(**kw):
    return pltpu.CompilerParams(vmem_limit_bytes=VMEM_LIMIT, **kw)


def _rope_tables(seq):
    def tabs(dim, period):
        half = dim // 2
        inv = np.float32(ROPE_THETA) ** (-np.arange(0, dim, 2, dtype=np.float32) / np.float32(dim))
        ang = np.arange(seq, dtype=np.float32)[:, None] * inv.astype(np.float32)[None, :]
        cos, sin = np.cos(ang).astype(np.float32), np.sin(ang).astype(np.float32)
        j = np.arange(LANES) % period
        f = j % half
        c = np.where(j < dim, cos[:, f], np.float32(1.0))
        s1 = np.where(j < half, -sin[:, f], np.float32(0.0))
        s2 = np.where((j >= half) & (j < dim), sin[:, f], np.float32(0.0))
        return [c, s1, s2]
    return np.stack(tabs(MLA_ROPE, MLA_ROPE) + tabs(DIL_ROT, DIL_HEAD)).astype(np.float32)


def _rope(t, c, s1, s2, half):
    return t * c + pltpu.roll(t, LANES - half, 1) * s1 + pltpu.roll(t, half, 1) * s2


def _rope_t(d, c, s1, s2, half):
    return d * c + pltpu.roll(d * s1, half, 1) + pltpu.roll(d * s2, LANES - half, 1)


def _rope_wide(fn, t, c, s1, s2, half):
    return jnp.concatenate(
        [fn(t[:, i:i + LANES], c, s1, s2, half) for i in range(0, t.shape[1], LANES)], axis=1)


def _mla_bias_t(blk):
    a = np.arange(blk)
    causal = np.where(a[:, None] <= a[None, :], 0.0, NEG)
    return np.stack([np.zeros((blk, blk)), causal]).astype(np.float32)


def _dil_bias_t(blk, reach):
    a = np.arange(blk)
    out = []
    for off in range(-(-reach // blk) + 1):
        delta = blk * off + a[None, :] - a[:, None]
        mult = np.zeros((blk, blk))
        for window, dil in DIL_CONFIGS:
            mult += (delta >= 0) & (delta % dil == 0) & (delta <= min(window, reach))
        out.append(np.where(mult > 0, np.log2(np.maximum(mult, 1.0)), NEG))
    return np.stack(out).astype(np.float32)


def _dil_far_bias_t(length):
    window, dil = DIL_CONFIGS[-1]
    a = np.arange(length)
    steps_back = a[None, :] - a[:, None]
    seen = (steps_back * dil > DIL_NEAR) & (steps_back * dil <= window)
    return np.where(seen, 0.0, NEG).astype(np.float32)[None]


def _lanes_to_classes(a, dil):
    h, s = a.shape
    return a.reshape(h, s // dil, dil).transpose(0, 2, 1).reshape(h, s)


def _lanes_from_classes(a, dil):
    h, s = a.shape
    return a.reshape(h, dil, s // dil).transpose(0, 2, 1).reshape(h, s)


def _steps(nq, span, by_key, diag_only_bias):
    rows = []
    if by_key:
        for ki in range(nq):
            hi = min(nq - 1, ki + span)
            for qi in range(ki, hi + 1):
                rows.append((qi, ki, int(qi == ki), int(qi == hi)))
    else:
        for qi in range(nq):
            lo = max(0, qi - span)
            for ki in range(lo, qi + 1):
                rows.append((qi, ki, int(ki == lo), int(ki == qi)))
    arr = np.array(rows, dtype=np.int32)
    off = arr[:, 0] - arr[:, 1]
    bias_idx = (off == 0).astype(np.int32) if diag_only_bias else off.astype(np.int32)
    return [jnp.asarray(v) for v in (arr[:, 0], arr[:, 1], bias_idx, arr[:, 2], arr[:, 3])]


def _by_class(val, out_ref, lanes_sc):
    n_cls, per = out_ref.shape[0], out_ref.shape[1]
    for c in range(val.shape[1] // LANES):
        lanes_sc[c] = val[:, LANES * c:LANES * (c + 1)]
        for r in range(n_cls):
            rows = lanes_sc.at[c][pl.ds(r, per, stride=n_cls), :]
            out_ref[r, :, LANES * c:LANES * (c + 1)] = rows.astype(out_ref.dtype)


def _in_sequence(ref, lanes_sc):
    n_cls, per, width = ref.shape
    for c in range(width // LANES):
        for r in range(n_cls):
            lanes_sc.at[c][pl.ds(r, per, stride=n_cls), :] = ref[r, :, LANES * c:LANES * (c + 1)].astype(F32)
    return jnp.concatenate([lanes_sc[c] for c in range(width // LANES)], axis=1)


def _fwd_proj(x, w_in_r, w_uq_r, w_ukv_r, qg, kvg, tabs, bt, n_cls):
    seq = x.shape[0]

    def body(x_ref, win_ref, wuq_ref, wukv_ref, qg_ref, kvg_ref, tab_ref,
             cq_ref, ckv_ref, qn_ref, kvn_ref, qcat_ref, kn_ref, kpe_ref, v_ref,
             ga_ref, gb_ref, qb_ref, kb_ref, vb_ref, knt_ref, kpet_ref, vt_ref, kbt_ref, vbt_ref,
             qbc_ref, kbc_ref, vbc_ref, lanes_sc):
        xb = x_ref[...].astype(BF16)

        def proj(lo, hi):
            return jnp.dot(xb, win_ref[:, lo:hi], preferred_element_type=F32)

        m_tabs = (tab_ref[0], tab_ref[1], tab_ref[2])
        d_tabs = (tab_ref[3], tab_ref[4], tab_ref[5])

        cq = proj(C_CQ, C_CKV)
        cq_ref[...] = cq
        qn = (cq * lax.rsqrt(jnp.mean(cq * cq, axis=1, keepdims=True) + RMS_EPS) * qg_ref[...]).astype(BF16)
        qn_ref[...] = qn
        q = jnp.dot(qn, wuq_ref[...], preferred_element_type=F32)
        qcat_ref[:, :HW] = (q[:, :HW] * (MLA_SCALE * LOG2E)).astype(BF16)
        qcat_ref[:, HW:] = (_rope_wide(_rope, q[:, HW:], *m_tabs, MLA_ROPE // 2) * (MLA_SCALE * LOG2E)).astype(BF16)

        ckv = proj(C_CKV, C_KR)
        ckv_ref[...] = ckv
        kvn = (ckv * lax.rsqrt(jnp.mean(ckv * ckv, axis=1, keepdims=True) + RMS_EPS) * kvg_ref[...]).astype(BF16)
        kvn_ref[...] = kvn
        kv = jnp.dot(kvn, wukv_ref[...], preferred_element_type=F32)
        kn_ref[...] = kv[:, :HW].astype(BF16)
        v_ref[...] = kv[:, HW:].astype(BF16)
        knt_ref[...] = kv[:, :HW].T.astype(BF16)
        vt_ref[...] = kv[:, HW:].T.astype(BF16)

        kpe = _rope(proj(C_KR, C_GA), *m_tabs, MLA_ROPE // 2)
        kpe_ref[...] = kpe.astype(BF16)
        kpet_ref[...] = kpe.T[:MLA_ROPE, :].astype(BF16)
        ga_ref[...] = proj(C_GA, C_QB)
        qb = _rope_wide(_rope, proj(C_QB, C_KB), *d_tabs, DIL_ROT // 2) * (DIL_SCALE * LOG2E)
        qb_ref[...] = qb.astype(BF16)
        _by_class(qb, qbc_ref, lanes_sc)
        kb = _rope_wide(_rope, proj(C_KB, C_VB), *d_tabs, DIL_ROT // 2)
        kb_ref[...] = kb.astype(BF16)
        kbt_ref[...] = kb.T.astype(BF16)
        _by_class(kb, kbc_ref, lanes_sc)
        vb = proj(C_VB, C_GB)
        vb_ref[...] = vb.astype(BF16)
        vbt_ref[...] = vb.T.astype(BF16)
        _by_class(vb, vbc_ref, lanes_sc)
        gb_ref[...] = proj(C_GB, C_END)

    def tok(width):
        return pl.BlockSpec((bt, width), lambda i: (i, 0))

    def tok_t(height):
        return pl.BlockSpec((height, bt), lambda i: (0, i))

    def full(a):
        return pl.BlockSpec(a.shape, lambda i: (0,) * a.ndim)

    outs = [(Q_RANK, F32), (KV_RANK, F32), (Q_RANK, BF16), (KV_RANK, BF16), (QW, BF16), (HW, BF16),
            (LANES, BF16), (HW, BF16), (HW, F32), (HW, F32), (HW, BF16), (HW, BF16), (HW, BF16)]
    outs_t = [HW, MLA_ROPE, HW, HW, HW]
    by_class = pl.BlockSpec((n_cls, bt // n_cls, HW), lambda i: (0, i, 0))
    return _pcall(
        body, name="fwd_proj", grid=(seq // bt,),
        in_specs=[tok(D_MODEL), full(w_in_r), full(w_uq_r), full(w_ukv_r), full(qg), full(kvg),
                  pl.BlockSpec((6, bt, LANES), lambda i: (0, i, 0))],
        out_specs=[tok(w) for w, _ in outs] + [tok_t(h) for h in outs_t] + [by_class] * 3,
        out_shape=[jax.ShapeDtypeStruct((seq, w), dt) for w, dt in outs]
        + [jax.ShapeDtypeStruct((h, seq), BF16) for h in outs_t]
        + [jax.ShapeDtypeStruct((n_cls, seq // n_cls, HW), BF16)] * 3,
        scratch_shapes=[pltpu.VMEM((HW // LANES, bt, LANES), F32)],
        compiler_params=_cparams(dimension_semantics=("arbitrary",)),
    )(x, w_in_r, w_uq_r, w_ukv_r, qg, kvg, tabs)


def _head_masks(lane, h):
    e, g = h % 2, h % 4
    me = (lane >= 64 * e) & (lane < 64 * e + 64)
    mr = (lane >= 32 * g) & (lane < 32 * g + 32)
    return me, mr


def _masked(mask, a):
    return jnp.where(mask, a, jnp.zeros_like(a))


def _pair_operands(q_ref, k_ref, kpe_ref, lane, j, ks=slice(None), qs=slice(None)):
    cols = slice(LANES * j, LANES * (j + 1))
    qc = q_ref[qs, cols]
    kj = k_ref[ks, cols]
    kes = []
    for h in (2 * j, 2 * j + 1):
        me, mr = _head_masks(lane, h)
        ke = _masked(me, kj)
        if kpe_ref is not None:
            ke = jnp.concatenate([ke, _masked(mr, kpe_ref[ks, :])], axis=1)
        kes.append(ke)
    if kpe_ref is not None:
        qc = jnp.concatenate([qc, q_ref[qs, HW + LANES * (j // 2):HW + LANES * (j // 2 + 1)]], axis=1)
    return qc, kes


def _tile_variants(bias_t):
    out = {}
    for i, tile in enumerate(np.asarray(bias_t)):
        h = tile.shape[0] // 2
        skip = 1 if (tile[h:, :h] == NEG).all() else 2 if (tile[:h, h:] == NEG).all() else 0
        out[i] = (bool((tile != 0).any()), skip)
    return out


def _tile_parts(blk, skip):
    lo, hi, full = slice(0, blk // 2), slice(blk // 2, blk), slice(0, blk)
    return {0: [(full, full)], 1: [(lo, full), (hi, hi)], 2: [(hi, full), (lo, lo)]}[skip]


class Rider(NamedTuple):
    args: list
    in_specs: list
    out_shape: list
    out_specs: list
    scratch: list
    start: Callable
    finish: Callable
    stages: tuple = ()


def _ride_along(body, ride, n_prefetch, n_in, n_out, n_scratch, n_steps):
    if ride is None:
        return body

    def wrapped(*refs):
        pre, rest = refs[:n_prefetch], refs[n_prefetch:]
        a = n_in
        b = a + len(ride.args)
        c = b + n_out
        d = c + len(ride.out_shape)
        e = d + n_scratch
        mine = (rest[a:b], rest[c:d], rest[e:])
        t = pl.program_id(0)
        pl.when(t == 0)(lambda: ride.start(*mine))
        for at, stage in ride.stages:
            pl.when(t == at)(functools.partial(stage, *mine))
        body(*pre, *rest[:a], *rest[b:c], *rest[d:e])
        pl.when(t == n_steps - 1)(lambda: ride.finish(*mine))

    return wrapped


def _attn_fwd(name, q, k, kpe, vt, bias_t, steps, blk, ride=None, v_token_major=False):
    seq = q.shape[0]
    mla = kpe is not None
    n_steps = int(steps[0].shape[0])
    variants = _tile_variants(bias_t)

    def body(qi_r, ki_r, bi_r, fi_r, la_r, *refs):
        if mla:
            q_ref, k_ref, kpe_ref, vt_ref, b_ref, o_ref, lse_ref, m_sc, l_sc, acc_sc, st_sc = refs
        else:
            q_ref, k_ref, vt_ref, b_ref, o_ref, lse_ref, m_sc, l_sc, acc_sc, st_sc = refs
        t = pl.program_id(0)

        @pl.when(fi_r[t] == 1)
        def _():
            m_sc[...] = jnp.full(m_sc.shape, NEG, F32)
            l_sc[...] = jnp.zeros(l_sc.shape, F32)
            acc_sc[...] = jnp.zeros(acc_sc.shape, F32)

        lane = lax.broadcasted_iota(jnp.int32, (1, LANES), 1)
        if v_token_major:
            vt_all = vt_ref[...].astype(F32).T.astype(BF16)
            vt_rows = lambda rows, ks: vt_all[rows, ks]
        else:
            vt_rows = lambda rows, ks: vt_ref[rows, ks]

        def tile_pass(ks, qs, with_bias):
            nk, nq = ks.stop - ks.start, qs.stop - qs.start
            ones = jnp.ones((16, nk), BF16)

            def pair_scores(j):
                qc, kes = _pair_operands(q_ref, k_ref, kpe_ref if mla else None, lane, j, ks, qs)
                st = lax.dot_general(jnp.concatenate(kes, axis=0), qc, NT, preferred_element_type=F32)
                maxes = []
                for e in range(2):
                    se = st[e * nk:(e + 1) * nk]
                    if with_bias:
                        se = se + b_ref[0, ks, qs]
                    st_sc[j % 2, e * nk:(e + 1) * nk, 0:nq] = se
                    maxes.append(jnp.max(se, axis=0, keepdims=True))
                return maxes

            def softmax_pv(h, col_max):
                st = st_sc[(h // 2) % 2, (h % 2) * nk:(h % 2 + 1) * nk, 0:nq]
                hrow = slice(h, h + 1)
                m_prev = m_sc[hrow, qs]
                m_new = jnp.maximum(m_prev, col_max)
                alpha = jnp.exp2(m_prev - m_new)
                pt = jnp.exp2(st - m_new).astype(BF16)
                m_sc[hrow, qs] = m_new
                rows = slice(64 * h, 64 * h + 64)
                res = jnp.dot(jnp.concatenate([vt_rows(rows, ks), ones], axis=0), pt, preferred_element_type=F32)
                acc_sc[rows, qs] = alpha * acc_sc[rows, qs] + res[:64]
                l_sc[hrow, qs] = alpha * l_sc[hrow, qs] + res[64:65]

            maxes = pair_scores(0)
            for j in range(HEADS // 2):
                cur = maxes
                if j + 1 < HEADS // 2:
                    maxes = pair_scores(j + 1)
                softmax_pv(2 * j, cur[0])
                softmax_pv(2 * j + 1, cur[1])

        def step(with_bias, skip):
            for ks, qs in _tile_parts(blk, skip):
                tile_pass(ks, qs, with_bias)

        for idx, (with_bias, skip) in variants.items():
            if len(variants) == 1:
                step(with_bias, skip)
            else:
                pl.when(bi_r[t] == idx)(functools.partial(step, with_bias, skip))

        @pl.when(la_r[t] == 1)
        def _():
            for h in range(HEADS):
                rows = slice(64 * h, 64 * h + 64)
                acc_sc[rows, :] = acc_sc[rows, :] / l_sc[h:h + 1, :]
            o_ref[...] = acc_sc[...].T
            lse_ref[...] = m_sc[...] + jnp.log2(l_sc[...])

    qmap = lambda t, qi, ki, bi, fi, la: (qi[t], 0)
    kmap = lambda t, qi, ki, bi, fi, la: (ki[t], 0)
    in_specs = [pl.BlockSpec((blk, q.shape[1]), qmap), pl.BlockSpec((blk, HW), kmap)]
    args = [q, k]
    if mla:
        in_specs.append(pl.BlockSpec((blk, LANES), kmap))
        args.append(kpe)
    in_specs += [pl.BlockSpec((blk, HW), kmap) if v_token_major else
                 pl.BlockSpec((HW, blk), lambda t, qi, ki, bi, fi, la: (0, ki[t])),
                 pl.BlockSpec((1, blk, blk), lambda t, qi, ki, bi, fi, la: (bi[t], 0, 0))]
    args += [vt, jnp.asarray(bias_t)]
    out_specs = [pl.BlockSpec((blk, HW), qmap), pl.BlockSpec((HEADS, blk), lambda t, qi, ki, bi, fi, la: (0, qi[t]))]
    out_shape = [jax.ShapeDtypeStruct((seq, HW), F32), jax.ShapeDtypeStruct((HEADS, seq), F32)]
    scratch = [pltpu.VMEM((HEADS, blk), F32), pltpu.VMEM((HEADS, blk), F32),
               pltpu.VMEM((HW, blk), F32), pltpu.VMEM((2, 2 * blk, blk), F32)]
    body = _ride_along(body, ride, 5, len(args), len(out_shape), len(scratch), n_steps)
    if ride is not None:
        args, in_specs = args + ride.args, in_specs + ride.in_specs
        out_specs, out_shape, scratch = out_specs + ride.out_specs, out_shape + ride.out_shape, scratch + ride.scratch
    return _pcall(
        body, name=name,
        grid_spec=pltpu.PrefetchScalarGridSpec(
            num_scalar_prefetch=5, grid=(n_steps,), in_specs=in_specs, out_specs=out_specs, scratch_shapes=scratch),
        out_shape=out_shape,
        compiler_params=_cparams(dimension_semantics=("arbitrary",)),
    )(*steps, *args)


def _attn_bwd(name, q, k, kpe, v, kt, kpet, bias_t, do, lse, dstat, steps, blk, ride=None, single_visit=False):
    assert not (single_visit and kpe is not None) and (kt is not None or single_visit)
    seq = q.shape[0]
    mla = kpe is not None
    qw = q.shape[1]
    n_steps = int(steps[0].shape[0])
    dk_dtype = BF16 if mla else F32
    variants = _tile_variants(bias_t)

    def body(qi_r, ki_r, bi_r, fi_r, la_r, *refs):
        if mla:
            (q_ref, k_ref, kpe_ref, v_ref, kt_ref, kpet_ref, b_ref, do_ref, lse_ref, d_ref,
             dq_ref, dk_ref, dkpe_ref, dv_ref, dk_sc, dkpe_sc, dv_sc, st_sc, dpt_sc) = refs
        else:
            q_ref, k_ref, v_ref, *rest = refs
            kt_ref = rest.pop(0) if kt is not None else None
            b_ref, do_ref, lse_ref, d_ref, dq_ref, dk_ref, dv_ref, *rest = rest
            dq_tok_ref = rest.pop(0) if single_visit else None
            dk_sc, dv_sc, st_sc, dpt_sc = rest
        t = pl.program_id(0)

        @pl.when(t == 0)
        def _():
            dq_ref[...] = jnp.zeros(dq_ref.shape, F32)

        @pl.when(fi_r[t] == 1)
        def _():
            dk_sc[...] = jnp.zeros(dk_sc.shape, F32)
            dv_sc[...] = jnp.zeros(dv_sc.shape, F32)
            if mla:
                dkpe_sc[...] = jnp.zeros(dkpe_sc.shape, F32)

        qi = qi_r[t]
        lane = lax.broadcasted_iota(jnp.int32, (1, LANES), 1)
        if kt is None:
            kt_all = k_ref[...].astype(F32).T.astype(BF16)
            kt_rows = lambda rows, ks: kt_all[rows, ks]
        else:
            kt_rows = lambda rows, ks: kt_ref[rows, ks]

        def tile_pass(ks, qs, with_bias):
            nk, nq = ks.stop - ks.start, qs.stop - qs.start

            def pair_matmuls(j):
                cols = slice(LANES * j, LANES * (j + 1))
                qc, kes = _pair_operands(q_ref, k_ref, kpe_ref if mla else None, lane, j, ks, qs)
                st_sc[j % 2, 0:2 * nk, 0:nq] = lax.dot_general(
                    jnp.concatenate(kes, axis=0), qc, NT, preferred_element_type=F32)
                vj = v_ref[ks, cols]
                ves = [_masked(_head_masks(lane, h)[0], vj) for h in (2 * j, 2 * j + 1)]
                dpt_sc[j % 2, 0:2 * nk, 0:nq] = lax.dot_general(
                    jnp.concatenate(ves, axis=0), do_ref[qs, cols], NT, preferred_element_type=F32)

            def pair_grads(j):
                cols = slice(LANES * j, LANES * (j + 1))
                qj, doj = q_ref[qs, cols], do_ref[qs, cols]
                if mla:
                    qr = q_ref[qs, HW + LANES * (j // 2):HW + LANES * (j // 2 + 1)]
                pts, dsts, qms, doms = [], [], [], []
                for e in range(2):
                    h = 2 * j + e
                    me, mr = _head_masks(lane, h)
                    st = st_sc[j % 2, e * nk:(e + 1) * nk, 0:nq]
                    if with_bias:
                        st = st + b_ref[0, ks, qs]
                    pt = jnp.exp2(st - lse_ref[h:h + 1, qs])
                    dst = (pt * (dpt_sc[j % 2, e * nk:(e + 1) * nk, 0:nq] - d_ref[h:h + 1, qs])).astype(BF16)
                    pts.append(pt.astype(BF16))
                    dsts.append(dst)
                    doms.append(_masked(me, doj))
                    qm = _masked(me, qj)
                    if mla:
                        qm = jnp.concatenate([qm, _masked(mr, qr)], axis=1)
                    qms.append(qm)
                    ktl = kt_rows(slice(64 * h, 64 * h + 64), ks)
                    if mla:
                        ktl = jnp.concatenate([ktl, kpet_ref[:, ks]], axis=0)
                    dqc = jnp.dot(ktl, dst, preferred_element_type=F32)
                    dq_ref[qi, 64 * h:64 * h + 64, qs] += dqc[:64]
                    if mla:
                        dq_ref[qi, HW + MLA_ROPE * h:HW + MLA_ROPE * (h + 1), qs] += dqc[64:]
                dv_sc[ks, cols] += jnp.dot(
                    jnp.concatenate(pts, axis=1), jnp.concatenate(doms, axis=0), preferred_element_type=F32)
                dkc = jnp.dot(jnp.concatenate(dsts, axis=1), jnp.concatenate(qms, axis=0), preferred_element_type=F32)
                dk_sc[ks, cols] += dkc[:, :LANES]
                if mla:
                    dkpe_sc[ks, :] += dkc[:, LANES:]

            pair_matmuls(0)
            for j in range(HEADS // 2):
                if j + 1 < HEADS // 2:
                    pair_matmuls(j + 1)
                pair_grads(j)

        def step(with_bias, skip):
            for ks, qs in _tile_parts(blk, skip):
                tile_pass(ks, qs, with_bias)

        for idx, (with_bias, skip) in variants.items():
            if len(variants) == 1:
                step(with_bias, skip)
            else:
                pl.when(bi_r[t] == idx)(functools.partial(step, with_bias, skip))
        if single_visit:
            dq_tok_ref[...] = dq_ref[qi].T

        @pl.when(la_r[t] == 1)
        def _():
            dk_ref[...] = (dk_sc[...] * LN2).astype(dk_ref.dtype)
            dv_ref[...] = dv_sc[...].astype(dv_ref.dtype)
            if mla:
                dkpe_ref[...] = dkpe_sc[...] * LN2

    qmap = lambda t, qi, ki, bi, fi, la: (qi[t], 0)
    kmap = lambda t, qi, ki, bi, fi, la: (ki[t], 0)
    qmap_t = lambda t, qi, ki, bi, fi, la: (0, qi[t])
    kmap_t = lambda t, qi, ki, bi, fi, la: (0, ki[t])
    in_specs = [pl.BlockSpec((blk, qw), qmap), pl.BlockSpec((blk, HW), kmap)]
    args = [q, k]
    if mla:
        in_specs.append(pl.BlockSpec((blk, LANES), kmap))
        args.append(kpe)
    in_specs.append(pl.BlockSpec((blk, HW), kmap))
    args.append(v)
    if kt is not None:
        in_specs.append(pl.BlockSpec((HW, blk), kmap_t))
        args.append(kt)
    if mla:
        in_specs.append(pl.BlockSpec((MLA_ROPE, blk), kmap_t))
        args.append(kpet)
    in_specs += [pl.BlockSpec((1, blk, blk), lambda t, qi, ki, bi, fi, la: (bi[t], 0, 0)),
                 pl.BlockSpec((blk, HW), qmap), pl.BlockSpec((HEADS, blk), qmap_t), pl.BlockSpec((HEADS, blk), qmap_t)]
    args += [jnp.asarray(bias_t), do, lse, dstat]
    dq_shape = (seq // blk, qw, blk)
    out_specs = [pl.BlockSpec(dq_shape, lambda t, qi, ki, bi, fi, la: (0, 0, 0)), pl.BlockSpec((blk, HW), kmap)]
    out_shape = [jax.ShapeDtypeStruct(dq_shape, F32), jax.ShapeDtypeStruct((seq, HW), dk_dtype)]
    scratch = [pltpu.VMEM((blk, HW), F32)]
    if mla:
        out_specs.append(pl.BlockSpec((blk, LANES), kmap))
        out_shape.append(jax.ShapeDtypeStruct((seq, LANES), F32))
        scratch.append(pltpu.VMEM((blk, LANES), F32))
    out_specs.append(pl.BlockSpec((blk, HW), kmap))
    out_shape.append(jax.ShapeDtypeStruct((seq, HW), BF16))
    if single_visit:
        out_specs.append(pl.BlockSpec((blk, qw), qmap))
        out_shape.append(jax.ShapeDtypeStruct((seq, qw), F32))
    scratch.append(pltpu.VMEM((blk, HW), F32))
    scratch += [pltpu.VMEM((2, 2 * blk, blk), F32), pltpu.VMEM((2, 2 * blk, blk), F32)]
    body = _ride_along(body, ride, 5, len(args), len(out_shape), len(scratch), n_steps)
    if ride is not None:
        args, in_specs = args + ride.args, in_specs + ride.in_specs
        out_specs, out_shape, scratch = out_specs + ride.out_specs, out_shape + ride.out_shape, scratch + ride.scratch
    return _pcall(
        body, name=name,
        grid_spec=pltpu.PrefetchScalarGridSpec(
            num_scalar_prefetch=5, grid=(n_steps,), in_specs=in_specs, out_specs=out_specs,
            scratch_shapes=scratch),
        out_shape=out_shape,
        compiler_params=_cparams(dimension_semantics=("arbitrary",)),
    )(*steps, *args)


def _out_ln(oa, ob_near, ob_far, lse_near, lse_far, ga, gb, x, tgt, w_out, ln_g, ln_b, bt):
    seq = x.shape[0]

    def body(oa_ref, obn_ref, obf_ref, lsen_ref, lsef_ref, ga_ref, gb_ref, x_ref, tgt_ref, w_ref, g_ref, b_ref,
             dz_ref, doa_ref, dob_ref, dga_ref, dgb_ref, da_ref, db_ref, lse_ref, gwb_ref, small_ref, dobc_ref,
             gw_ref, lanes_sc):
        i = pl.program_id(0)

        @pl.when(i == 0)
        def _():
            gw_ref[...] = jnp.zeros(gw_ref.shape, F32)
            small_ref[...] = jnp.zeros(small_ref.shape, F32)

        def gate(g):
            sig = 0.5 * jnp.tanh(0.5 * g) + 0.5
            return g * sig, sig * (1.0 + g * (1.0 - sig))

        lse_n, lse_f = lsen_ref[...], lsef_ref[...]
        top = jnp.maximum(lse_n, lse_f)
        e_n, e_f = jnp.exp2(lse_n - top), jnp.exp2(lse_f - top)
        lse_ref[...] = top + jnp.log2(e_n + e_f)
        inv = 1.0 / (e_n + e_f)
        head_row = lax.broadcasted_iota(jnp.int32, (2 * HEADS, HW), 0) % HEADS
        spread = (head_row == lax.broadcasted_iota(jnp.int32, (2 * HEADS, HW), 1) // 64).astype(BF16)

        def per_lane(w):
            hi = w.astype(BF16)
            lo = (w - hi.astype(F32)).astype(BF16)
            return lax.dot_general(jnp.concatenate([hi, lo], axis=0), spread, TN, preferred_element_type=F32)

        o_a = oa_ref[...]
        o_b = per_lane(e_n * inv) * obn_ref[...] + per_lane(e_f * inv) * _in_sequence(obf_ref, lanes_sc)
        g_a, g_b = ga_ref[...], gb_ref[...]
        sa, dsa = gate(g_a)
        sb, dsb = gate(g_b)
        mix = jnp.concatenate([o_a * sa, o_b * sb], axis=1).astype(BF16)
        z = ALPHA * x_ref[...] + jnp.dot(mix, w_ref[...], preferred_element_type=F32)
        mu = jnp.mean(z, axis=1, keepdims=True)
        zc = z - mu
        rstd = lax.rsqrt(jnp.mean(zc * zc, axis=1, keepdims=True) + LN_EPS)
        xhat = zc * rstd
        gam = g_ref[...]
        diff = xhat * gam + b_ref[...] - tgt_ref[...]
        dy = diff * (1.0 / D_MODEL)
        small_ref[0:1, :] += jnp.sum(dy * xhat, axis=0, keepdims=True)
        small_ref[1:2, :] += jnp.sum(dy, axis=0, keepdims=True)
        small_ref[2:3, :] += jnp.sum(diff * diff, axis=0, keepdims=True)
        dxh = dy * gam
        dz = rstd * (dxh - jnp.mean(dxh, axis=1, keepdims=True) - xhat * jnp.mean(dxh * xhat, axis=1, keepdims=True))
        dz_ref[...] = dz
        dzb = dz.astype(BF16)
        gw_ref[...] += lax.dot_general(mix, dzb, TN, preferred_element_type=F32)

        @pl.when(i == seq // bt - 1)
        def _():
            gwb_ref[...] = gw_ref[...].astype(BF16)

        dmix = lax.dot_general(dzb, w_ref[...], NT, preferred_element_type=F32)
        doa, dob = dmix[:, :HW] * sa, dmix[:, HW:] * sb
        doa_ref[...] = doa.astype(BF16)
        dob_ref[...] = dob.astype(BF16)
        _by_class(dob, dobc_ref, lanes_sc)
        dga_ref[...] = (dmix[:, :HW] * o_a * dsa).astype(BF16)
        dgb_ref[...] = (dmix[:, HW:] * o_b * dsb).astype(BF16)
        head_of = (lax.broadcasted_iota(jnp.int32, (2 * HW, LANES), 0) % HW) // 64
        ind = (head_of == lax.broadcasted_iota(jnp.int32, (2 * HW, LANES), 1)).astype(BF16)

        def head_sums(prod):
            hi = prod.astype(BF16)
            lo = (prod - hi.astype(F32)).astype(BF16)
            sums = jnp.dot(jnp.concatenate([hi, lo], axis=1), ind, preferred_element_type=F32)
            return sums.T[:HEADS, :]

        da_ref[...] = head_sums(doa * o_a)
        db_ref[...] = head_sums(dob * o_b)

    def tok(width):
        return pl.BlockSpec((bt, width), lambda i: (i, 0))

    def full(shape):
        return pl.BlockSpec(shape, lambda i: (0,) * len(shape))

    stat = pl.BlockSpec((HEADS, bt), lambda i: (0, i))
    n_cls = ob_far.shape[0]
    by_class = pl.BlockSpec((n_cls, bt // n_cls, HW), lambda i: (0, i, 0))
    return _pcall(
        body, name="out_ln", grid=(seq // bt,),
        in_specs=[tok(HW), tok(HW), by_class, stat, stat, tok(HW), tok(HW), tok(D_MODEL), tok(D_MODEL),
                  full((D_MODEL, D_MODEL)), full((1, D_MODEL)), full((1, D_MODEL))],
        out_specs=[tok(D_MODEL), tok(HW), tok(HW), tok(HW), tok(HW), stat, stat, stat,
                   full((D_MODEL, D_MODEL)), full((8, D_MODEL)), by_class],
        out_shape=[jax.ShapeDtypeStruct((seq, D_MODEL), F32)] + [jax.ShapeDtypeStruct((seq, HW), BF16)] * 4
        + [jax.ShapeDtypeStruct((HEADS, seq), F32)] * 3
        + [jax.ShapeDtypeStruct((D_MODEL, D_MODEL), BF16), jax.ShapeDtypeStruct((8, D_MODEL), F32),
           jax.ShapeDtypeStruct(ob_far.shape, BF16)],
        scratch_shapes=[pltpu.VMEM((D_MODEL, D_MODEL), F32), pltpu.VMEM((HW // LANES, bt, LANES), F32)],
        compiler_params=_cparams(dimension_semantics=("arbitrary",)),
    )(oa, ob_near, ob_far, lse_near, lse_far, ga, gb, x, tgt, w_out, ln_g, ln_b)


def _bwd_mid(dq_m, dkn, dv, dkpe, dqb, dkb, dvb, far, dga, dgb, cq, ckv, qn, kvn, w_uq_r, w_ukv_r, qg, kvg, tabs, bt):
    n_cls = far[0].shape[0]
    seq = cq.shape[0]

    def body(dqm_ref, dkn_ref, dv_ref, dkpe_ref, dqb_ref, dkb_ref, dvb_ref, dqf_ref, dkf_ref, dvf_ref, dga_ref, dgb_ref,
             cq_ref, ckv_ref, qn_ref, kvn_ref, wuq_ref, wukv_ref, qg_ref, kvg_ref, tab_ref,
             dh_ref, guq3_ref, gukv3_ref, small_ref, seq_sc, guq_ref, gukv_ref):
        i = pl.program_id(0)

        @pl.when(i == 0)
        def _():
            guq_ref[...] = jnp.zeros(guq_ref.shape, F32)
            gukv_ref[...] = jnp.zeros(gukv_ref.shape, F32)
            small_ref[...] = jnp.zeros(small_ref.shape, F32)

        m_tabs = (tab_ref[0], tab_ref[1], tab_ref[2])
        d_tabs = (tab_ref[3], tab_ref[4], tab_ref[5])

        def rms_bwd(c, dn, gain):
            r = lax.rsqrt(jnp.mean(c * c, axis=1, keepdims=True) + RMS_EPS)
            u = dn * gain
            dc = r * u - c * (r * r * r) * jnp.mean(u * c, axis=1, keepdims=True)
            return dc, jnp.sum(dn * c * r, axis=0, keepdims=True)

        dqm = dqm_ref[0].T
        dq = jnp.concatenate(
            [dqm[:, :HW], _rope_wide(_rope_t, dqm[:, HW:], *m_tabs, MLA_ROPE // 2)], axis=1) * MLA_SCALE
        dq = dq.astype(BF16)
        guq_ref[...] += lax.dot_general(qn_ref[...], dq, TN, preferred_element_type=F32)
        dqn = lax.dot_general(dq, wuq_ref[...], NT, preferred_element_type=F32)
        dcq, gq = rms_bwd(cq_ref[...], dqn, qg_ref[...])
        small_ref[0:1, :] += gq

        dkv = jnp.concatenate([dkn_ref[...], dv_ref[...]], axis=1)
        gukv_ref[...] += lax.dot_general(kvn_ref[...], dkv, TN, preferred_element_type=F32)
        dkvn = lax.dot_general(dkv, wukv_ref[...], NT, preferred_element_type=F32)
        dckv, gkv = rms_bwd(ckv_ref[...], dkvn, kvg_ref[...])
        small_ref[1:2, :KV_RANK] += gkv

        dh_ref[:, C_CQ:C_CKV] = dcq.astype(BF16)
        dh_ref[:, C_CKV:C_KR] = dckv.astype(BF16)
        dh_ref[:, C_KR:C_GA] = _rope_t(dkpe_ref[...], *m_tabs, MLA_ROPE // 2).astype(BF16)
        dh_ref[:, C_GA:C_QB] = dga_ref[...]
        in_sequence = functools.partial(_in_sequence, lanes_sc=seq_sc)
        dqb = dqb_ref[0].T + in_sequence(dqf_ref)
        dh_ref[:, C_QB:C_KB] = (_rope_wide(_rope_t, dqb, *d_tabs, DIL_ROT // 2) * DIL_SCALE).astype(BF16)
        dkb = dkb_ref[...] + in_sequence(dkf_ref)
        dh_ref[:, C_KB:C_VB] = _rope_wide(_rope_t, dkb, *d_tabs, DIL_ROT // 2).astype(BF16)
        dh_ref[:, C_VB:C_GB] = (dvb_ref[...].astype(F32) + in_sequence(dvf_ref)).astype(BF16)
        dh_ref[:, C_GB:C_END] = dgb_ref[...]

        @pl.when(i == seq // bt - 1)
        def _():
            for h in range(HEADS):
                guq3_ref[h] = jnp.concatenate(
                    [guq_ref[:, MLA_NOPE * h:MLA_NOPE * (h + 1)],
                     guq_ref[:, HW + MLA_ROPE * h:HW + MLA_ROPE * (h + 1)]], axis=1).astype(BF16)
                gukv3_ref[h] = jnp.concatenate(
                    [gukv_ref[:, MLA_NOPE * h:MLA_NOPE * (h + 1)],
                     gukv_ref[:, HW + MLA_V * h:HW + MLA_V * (h + 1)]], axis=1).astype(BF16)

    def tok(width):
        return pl.BlockSpec((bt, width), lambda i: (i, 0))

    def tok_t(a):
        per = a.shape[2] // bt
        return pl.BlockSpec((1, a.shape[1], bt), lambda i: (i // per, 0, i % per))

    def full(shape):
        return pl.BlockSpec(shape, lambda i: (0,) * len(shape))

    by_class = pl.BlockSpec((n_cls, bt // n_cls, HW), lambda i: (0, i, 0))
    uq3 = (HEADS, Q_RANK, MLA_NOPE + MLA_ROPE)
    ukv3 = (HEADS, KV_RANK, MLA_NOPE + MLA_V)
    return _pcall(
        body, name="bwd_mid", grid=(seq // bt,),
        in_specs=[tok_t(dq_m), tok(HW), tok(HW), tok(LANES), tok_t(dqb), tok(HW), tok(HW), by_class, by_class, by_class,
                  tok(HW), tok(HW),
                  tok(Q_RANK), tok(KV_RANK), tok(Q_RANK), tok(KV_RANK),
                  full(w_uq_r.shape), full(w_ukv_r.shape), full((1, Q_RANK)), full((1, KV_RANK)),
                  pl.BlockSpec((6, bt, LANES), lambda i: (0, i, 0))],
        out_specs=[tok(C_END), full(uq3), full(ukv3), full((8, Q_RANK))],
        out_shape=[jax.ShapeDtypeStruct((seq, C_END), BF16), jax.ShapeDtypeStruct(uq3, BF16),
                   jax.ShapeDtypeStruct(ukv3, BF16), jax.ShapeDtypeStruct((8, Q_RANK), F32)],
        scratch_shapes=[pltpu.VMEM((HW // LANES, bt, LANES), F32), pltpu.VMEM(w_uq_r.shape, F32),
                        pltpu.VMEM(w_ukv_r.shape, F32)],
        compiler_params=_cparams(dimension_semantics=("arbitrary",)),
    )(dq_m, dkn, dv, dkpe, dqb, dkb, dvb, *far, dga, dgb, cq, ckv, qn, kvn, w_uq_r, w_ukv_r, qg, kvg, tabs)


def _grad_x(dz, dh, w_in_r, bt, ride=None):
    seq = dz.shape[0]
    n_steps = seq // bt

    def body(dz_ref, dh_ref, w_ref, gx_ref):
        gx_ref[...] = ALPHA * dz_ref[...] + lax.dot_general(
            dh_ref[...], w_ref[...], NT, preferred_element_type=F32)

    args = [dz, dh, w_in_r]
    in_specs = [pl.BlockSpec((bt, D_MODEL), lambda i: (i, 0)), pl.BlockSpec((bt, C_END), lambda i: (i, 0)),
                pl.BlockSpec(w_in_r.shape, lambda i: (0, 0))]
    out_specs = [pl.BlockSpec((bt, D_MODEL), lambda i: (i, 0))]
    out_shape = [jax.ShapeDtypeStruct((seq, D_MODEL), F32)]
    scratch = []
    body = _ride_along(body, ride, 0, len(args), len(out_shape), 0, n_steps)
    if ride is not None:
        args, in_specs = args + ride.args, in_specs + ride.in_specs
        out_specs, out_shape, scratch = out_specs + ride.out_specs, out_shape + ride.out_shape, ride.scratch
    return _pcall(
        body, name="grad_x", grid=(n_steps,),
        in_specs=in_specs, out_specs=out_specs, out_shape=out_shape, scratch_shapes=scratch,
        compiler_params=_cparams(dimension_semantics=("arbitrary",)),
    )(*args)


def _grad_w_in(x, dh, bt):
    seq = x.shape[0]
    shard = IN_WIDTH // N_DEV
    k_lo, k_hi = IN_SPLITS[0] + IN_SPLITS[1], IN_SPLITS[0] + IN_SPLITS[1] + MLA_ROPE

    def body(x_ref, dh_ref, out_ref, acc):
        i = pl.program_id(0)

        @pl.when(i == 0)
        def _():
            acc[...] = jnp.zeros(acc.shape, F32)

        acc[...] += lax.dot_general(x_ref[...].astype(BF16), dh_ref[...], TN, preferred_element_type=F32)

        @pl.when(i == seq // bt - 1)
        def _():
            kr = acc[:, C_KR:C_GA]
            kr = kr + pltpu.roll(kr, 96, 1) + pltpu.roll(kr, 64, 1) + pltpu.roll(kr, 32, 1)
            for d in range(N_DEV):
                lo, hi = shard * d, shard * (d + 1)
                pieces = []
                if lo < k_lo:
                    pieces.append(acc[:, lo:min(hi, k_lo)])
                if lo < k_hi and hi > k_lo:
                    pieces.append(kr[:, max(lo, k_lo) - k_lo:min(hi, k_hi) - k_lo])
                if hi > k_hi:
                    shift = C_GA - k_hi
                    pieces.append(acc[:, max(lo, k_hi) + shift:hi + shift])
                blk = pieces[0] if len(pieces) == 1 else jnp.concatenate(pieces, axis=1)
                out_ref[d] = blk.astype(BF16)

    return _pcall(
        body, name="grad_w_in", grid=(seq // bt,),
        in_specs=[pl.BlockSpec((bt, D_MODEL), lambda i: (i, 0)), pl.BlockSpec((bt, C_END), lambda i: (i, 0))],
        out_specs=pl.BlockSpec((N_DEV, D_MODEL, shard), lambda i: (0, 0, 0)),
        out_shape=jax.ShapeDtypeStruct((N_DEV, D_MODEL, shard), BF16),
        scratch_shapes=[pltpu.VMEM((D_MODEL, C_END), F32)],
        compiler_params=_cparams(dimension_semantics=("arbitrary",)),
    )(x, dh)


def _local_step(x, tgt, w_in_r, w_uq_r, w_ukv_r, w_out_rider, g_out_rider, reduce_rider, q_norm_g, kv_norm_g,
                ln_g, ln_b, bt=BLOCK_TOKENS, blk_m=BLOCK_MLA, blk_d=BLOCK_DIL):
    seq = x.shape[0]
    tabs = jnp.asarray(_rope_tables(seq))
    qg, kvg = q_norm_g.reshape(1, -1), kv_norm_g.reshape(1, -1)

    far_dil = DIL_CONFIGS[-1][1]
    cls = seq // far_dil
    (cq, ckv, qn, kvn, qcat, kn, kpe, v, ga, gb, qb, kb, vb, knt, kpet, vt, kbt, vbt, qb_c, kb_c, vb_c) = _fwd_proj(
        x, w_in_r, w_uq_r, w_ukv_r, qg, kvg, tabs, bt, far_dil)
    qb_c, kb_c, vb_c = (a.reshape(seq, HW) for a in (qb_c, kb_c, vb_c))

    nq_m, nq_d = seq // blk_m, seq // blk_d
    bias_m = _mla_bias_t(blk_m)
    oa, lse_a, w_out = _attn_fwd(
        "mla_fwd", qcat, kn, kpe, vt, bias_m, _steps(nq_m, nq_m, False, True), blk_m, ride=w_out_rider)

    bias_near = _dil_bias_t(blk_d, DIL_NEAR)
    ob_near, lse_near = _attn_fwd(
        "dil_fwd", qb, kb, None, vbt, bias_near, _steps(nq_d, -(-DIL_NEAR // blk_d), False, False), blk_d)
    each = jnp.arange(far_dil, dtype=jnp.int32)
    steps_far = [each, each, jnp.zeros_like(each), jnp.ones_like(each), jnp.ones_like(each)]
    bias_far = _dil_far_bias_t(cls)
    ob_far, lse_far = _attn_fwd(
        "dil_far_fwd", qb_c, kb_c, None, vb_c, bias_far, steps_far, cls, v_token_major=True)

    dz, doa, dob, dga, dgb, dst_a, dst_b, lse_b, g_out, small1, dob_c = _out_ln(
        oa, ob_near, ob_far.reshape(far_dil, cls, HW), lse_near, _lanes_from_classes(lse_far, far_dil), ga, gb, x, tgt,
        w_out.reshape(D_MODEL, D_MODEL), ln_g.reshape(1, -1), ln_b.reshape(1, -1), bt)

    dq_m, dkn, dkpe, dv, g_out_recv = _attn_bwd(
        "mla_bwd", qcat, kn, kpe, v, knt, kpet, bias_m, doa, lse_a, dst_a, _steps(nq_m, nq_m, True, True), blk_m,
        ride=g_out_rider(g_out.reshape(N_DEV, D_MODEL // N_DEV, D_MODEL)))
    dqb, dkb_near, dvb_near = _attn_bwd(
        "dil_bwd", qb, kb, None, vb, kbt, None, bias_near, dob, lse_b, dst_b,
        _steps(nq_d, -(-DIL_NEAR // blk_d), True, False), blk_d)
    _, dkb_far, dvb_far, dqb_far = _attn_bwd(
        "dil_far_bwd", qb_c, kb_c, None, vb_c, None, None, bias_far, dob_c.reshape(seq, HW),
        _lanes_to_classes(lse_b, far_dil), _lanes_to_classes(dst_b, far_dil), steps_far, cls, single_visit=True)
    far = [a.reshape(far_dil, cls, HW) for a in (dqb_far, dkb_far, dvb_far)]

    dh, g_uq, g_ukv, small2 = _bwd_mid(
        dq_m, dkn, dv, dkpe, dqb, dkb_near, dvb_near, far, dga, dgb, cq, ckv, qn, kvn, w_uq_r, w_ukv_r, qg, kvg, tabs, bt)
    g_in = _grad_w_in(x, dh, bt)
    grads3 = [g_in, g_uq, g_ukv]
    small_part = _small_rows(small1[0], small1[1], small2[0, :Q_RANK], small2[1, :KV_RANK], small1[2])
    grad_x, *reduced = _grad_x(
        dz, dh, w_in_r, bt, ride=reduce_rider(grads3, g_out_recv, small_part, min(1, seq // bt - 1)))
    return grad_x, reduced


MESH_ID = pl.DeviceIdType.MESH
SHARD_SHAPES = ((D_MODEL, IN_WIDTH // N_DEV), (Q_RANK, 768 // N_DEV), (KV_RANK, 1024 // N_DEV), (D_MODEL // N_DEV, D_MODEL))
ADAM_ROWS = (32, 128, 128, 16)


def _me():
    x, y, c = lax.axis_index("x"), lax.axis_index("y"), lax.axis_index("c")
    return x, y, c, 4 * x + 2 * y + c


def _peer(k):
    x, y, c, _ = _me()
    px = 1 - x if (k >> 2) & 1 else x
    py = 1 - y if (k >> 1) & 1 else y
    pc = 1 - c if k & 1 else c
    return (px, py, pc), 4 * px + 2 * py + pc


def _all_gather_weights(shards):
    n = len(shards)
    shard = IN_WIDTH // N_DEV
    k_lo = IN_SPLITS[0] + IN_SPLITS[1]
    k_hi = k_lo + MLA_ROPE

    def body(*refs):
        ins = refs[:n]
        win_ref, wuq_ref, wukv_ref = refs[n:2 * n]
        bufs = refs[2 * n:3 * n]
        send_sems, recv_sems = refs[3 * n:]
        x, y, c, me = _me()
        here, sibling = (x, y, c), (x, y, 1 - c)
        chips = [(1 - x, y), (x, 1 - y), (1 - x, 1 - y)]
        for t in range(n):
            bufs[t][me] = ins[t][...].astype(BF16)

        def copy(t, k, px, py, pc, to):
            blk = bufs[t].at[4 * px + 2 * py + pc]
            return pltpu.make_async_remote_copy(
                src_ref=blk, dst_ref=blk, send_sem=send_sems.at[t, k], recv_sem=recv_sems.at[t, k],
                device_id=to, device_id_type=MESH_ID)

        first = []
        for t in range(n):
            first.append(copy(t, 0, x, y, c, sibling))
            for j, (px, py) in enumerate(chips):
                first.append(copy(t, 1 + j, x, y, c, (px, py, c)))
        for cp in first:
            cp.start()
        passed = []
        for j, (px, py) in enumerate(chips):
            for t in range(n):
                copy(t, 1 + j, px, py, c, here).wait_recv()
                cp = copy(t, 4 + j, px, py, c, sibling)
                cp.start()
                passed.append(cp)
        for t in range(n):
            copy(t, 0, x, y, 1 - c, here).wait_recv()
        for j, (px, py) in enumerate(chips):
            for t in range(n):
                copy(t, 4 + j, px, py, 1 - c, here).wait_recv()
        for cp in first + passed:
            cp.wait_send()

        a_in, a_uq, a_ukv = bufs
        for d in range(N_DEV):
            lo, hi = shard * d, shard * (d + 1)
            if lo < k_lo:
                win_ref[:, lo:min(hi, k_lo)] = a_in[d, :, 0:min(hi, k_lo) - lo]
            if lo < k_hi and hi > k_lo:
                kr = a_in[d, :, k_lo - lo:k_hi - lo]
                for rep in range(4):
                    win_ref[:, C_KR + MLA_ROPE * rep:C_KR + MLA_ROPE * (rep + 1)] = kr
            if hi > k_hi:
                src = max(lo, k_hi)
                win_ref[:, src + C_GA - k_hi:hi + C_GA - k_hi] = a_in[d, :, src - lo:hi - lo]
        for h in range(HEADS):
            wuq_ref[:, MLA_NOPE * h:MLA_NOPE * (h + 1)] = a_uq[h, :, :MLA_NOPE]
            wuq_ref[:, HW + MLA_ROPE * h:HW + MLA_ROPE * (h + 1)] = a_uq[h, :, MLA_NOPE:]
            wukv_ref[:, MLA_NOPE * h:MLA_NOPE * (h + 1)] = a_ukv[h, :, :MLA_NOPE]
            wukv_ref[:, HW + MLA_V * h:HW + MLA_V * (h + 1)] = a_ukv[h, :, MLA_NOPE:]

    vmem = pl.BlockSpec(memory_space=pltpu.VMEM)
    return _pcall(
        body, name="gather_weights",
        in_specs=[vmem] * n, out_specs=[vmem] * n,
        out_shape=[jax.ShapeDtypeStruct((D_MODEL, C_END), BF16), jax.ShapeDtypeStruct((Q_RANK, QW), BF16),
                   jax.ShapeDtypeStruct((KV_RANK, 2 * HW), BF16)],
        scratch_shapes=[pltpu.VMEM((N_DEV,) + s, BF16) for s in SHARD_SHAPES[:n]]
        + [pltpu.SemaphoreType.DMA((n, N_DEV - 1)), pltpu.SemaphoreType.DMA((n, N_DEV - 1))],
        compiler_params=_cparams(),
    )(*shards)


def _gather_w_out_rider(w_out):
    def copies(full_ref, stage, send_sems, recv_sems):
        me = _me()[3]
        out = []
        for k in range(1, N_DEV):
            peer, pidx = _peer(k)
            send = pltpu.make_async_remote_copy(
                src_ref=stage, dst_ref=full_ref.at[me], send_sem=send_sems.at[k - 1], recv_sem=recv_sems.at[k - 1],
                device_id=peer, device_id_type=MESH_ID)
            recv = pltpu.make_async_remote_copy(
                src_ref=stage, dst_ref=full_ref.at[pidx], send_sem=send_sems.at[k - 1], recv_sem=recv_sems.at[k - 1],
                device_id=peer, device_id_type=MESH_ID)
            out.append((send, recv))
        return out

    def start(ins, outs, scr):
        stage, send_sems, recv_sems, own_sem = scr
        stage[...] = ins[0][...].astype(BF16)
        pltpu.make_async_copy(stage, outs[0].at[_me()[3]], own_sem).start()
        for send, _ in copies(outs[0], stage, send_sems, recv_sems):
            send.start()

    def finish(ins, outs, scr):
        stage, send_sems, recv_sems, own_sem = scr
        pltpu.make_async_copy(stage, outs[0].at[_me()[3]], own_sem).wait()
        pairs = copies(outs[0], stage, send_sems, recv_sems)
        for _, recv in pairs:
            recv.wait_recv()
        for send, _ in pairs:
            send.wait_send()

    shape = SHARD_SHAPES[3]
    return Rider(
        args=[w_out], in_specs=[pl.BlockSpec(shape, lambda t, *_: (0, 0))],
        out_shape=[jax.ShapeDtypeStruct((N_DEV,) + shape, BF16)], out_specs=[pl.BlockSpec(memory_space=pl.ANY)],
        scratch=[pltpu.VMEM(shape, BF16), pltpu.SemaphoreType.DMA((N_DEV - 1,)), pltpu.SemaphoreType.DMA((N_DEV - 1,)),
                 pltpu.SemaphoreType.DMA],
        start=start, finish=finish)


def _scatter_g_out_rider(blocks):
    def copies(src_ref, dst_ref, send_sems, recv_sems):
        out = []
        for k in range(1, N_DEV):
            peer, pidx = _peer(k)
            out.append(pltpu.make_async_remote_copy(
                src_ref=src_ref.at[pidx], dst_ref=dst_ref.at[k], send_sem=send_sems.at[k - 1],
                recv_sem=recv_sems.at[k - 1], device_id=peer, device_id_type=MESH_ID))
        return out

    def start(ins, outs, scr):
        send_sems, recv_sems, own_sem = scr
        pltpu.make_async_copy(ins[0].at[_me()[3]], outs[0].at[0], own_sem).start()
        for cp in copies(ins[0], outs[0], send_sems, recv_sems):
            cp.start()

    def finish(ins, outs, scr):
        send_sems, recv_sems, own_sem = scr
        pltpu.make_async_copy(ins[0].at[_me()[3]], outs[0].at[0], own_sem).wait()
        for cp in copies(ins[0], outs[0], send_sems, recv_sems):
            cp.wait()

    hbm = pl.BlockSpec(memory_space=pl.ANY)
    return Rider(
        args=[blocks], in_specs=[hbm], out_shape=[jax.ShapeDtypeStruct(blocks.shape, blocks.dtype)], out_specs=[hbm],
        scratch=[pltpu.SemaphoreType.DMA((N_DEV - 1,)), pltpu.SemaphoreType.DMA((N_DEV - 1,)), pltpu.SemaphoreType.DMA],
        start=start, finish=finish)


def _adamw(w, g, m, v):
    m = ADAM_B1 * m + (1.0 - ADAM_B1) * g
    v = ADAM_B2 * v + (1.0 - ADAM_B2) * jnp.square(g)
    m_hat = m / (1.0 - ADAM_B1 ** ADAM_STEP)
    v_hat = v / (1.0 - ADAM_B2 ** ADAM_STEP)
    delta = -ADAM_LR * (m_hat / (jnp.sqrt(v_hat) + ADAM_EPS) + ADAM_WD * w)
    return delta, m, v


def _reduce_grads_rider(grads3, arrived, small_part, mid_step):
    n = len(grads3)

    class Refs:
        def __init__(self, ins, outs, scr):
            self.g3, self.arr, self.sp = ins[0:n], ins[n], ins[n + 1]
            self.gsum, self.gsum_out, self.ssum = outs[0:n], outs[n], outs[n + 1]
            self.own, self.sib, self.part, self.ici = scr[0:n], scr[n:2 * n], scr[2 * n:3 * n], scr[3 * n:4 * n]
            self.rsmall = scr[4 * n]
            (self.loc_sems, self.d2d_send, self.d2d_recv, self.ici_send, self.ici_recv,
             self.sm_send, self.sm_recv) = scr[4 * n + 1:]
            self.x, self.y, self.c, self.me = _me()
            self.chips = [(1 - self.x, self.y), (self.x, 1 - self.y), (1 - self.x, 1 - self.y)]

        def small(self):
            return [pltpu.make_async_remote_copy(
                src_ref=self.rsmall.at[0], dst_ref=self.rsmall.at[k], send_sem=self.sm_send.at[k - 1],
                recv_sem=self.sm_recv.at[k - 1], device_id=_peer(k)[0], device_id_type=MESH_ID)
                for k in range(1, N_DEV)]

        def level1(self):
            local, to_sib = [], []
            for t in range(n):
                for q in range(4):
                    local.append(pltpu.make_async_copy(
                        self.g3[t].at[2 * q + self.c], self.own[t].at[q], self.loc_sems.at[t, q]))
                    to_sib.append(pltpu.make_async_remote_copy(
                        src_ref=self.g3[t].at[2 * q + 1 - self.c], dst_ref=self.sib[t].at[q],
                        send_sem=self.d2d_send.at[t, q], recv_sem=self.d2d_recv.at[t, q],
                        device_id=(self.x, self.y, 1 - self.c), device_id_type=MESH_ID))
            return local, to_sib

        def level2(self):
            return [pltpu.make_async_remote_copy(
                src_ref=self.part[t].at[2 * px + py], dst_ref=self.ici[t].at[j], send_sem=self.ici_send.at[t, j],
                recv_sem=self.ici_recv.at[t, j], device_id=(px, py, self.c), device_id_type=MESH_ID)
                for t in range(n) for j, (px, py) in enumerate(self.chips)]

    def chunks(t, fn):
        rows = ADAM_ROWS[t]

        def step(i, carry):
            fn(pl.ds(pl.multiple_of(i * rows, rows), rows))
            return carry

        lax.fori_loop(0, SHARD_SHAPES[t][0] // rows, step, 0)

    def start(*refs):
        r = Refs(*refs)
        r.rsmall[0] = r.sp[...]
        local, to_sib = r.level1()
        for cp in r.small() + local + to_sib:
            cp.start()

    def middle(*refs):
        r = Refs(*refs)
        local, to_sib = r.level1()
        for cp in local:
            cp.wait()
        for cp in to_sib:
            cp.wait_recv()
        my_chip = 2 * r.x + r.y
        for t in range(n):
            def pair_sums(rows, t=t):
                for q in range(4):
                    r.part[t][q, rows, :] = (
                        r.own[t][q, rows, :].astype(F32) + r.sib[t][q, rows, :].astype(F32)).astype(BF16)
                r.gsum[t][rows, :] = r.own[t][my_chip, rows, :].astype(F32) + r.sib[t][my_chip, rows, :].astype(F32)

            chunks(t, pair_sums)
        for cp in r.level2():
            cp.start()

        def add_arrived(rows):
            g = r.arr[0, rows, :].astype(F32)
            for k in range(1, N_DEV):
                g = g + r.arr[k, rows, :].astype(F32)
            r.gsum_out[rows, :] = g

        chunks(3, add_arrived)

    def finish(*refs):
        r = Refs(*refs)
        to_chips = r.level2()
        for cp in to_chips:
            cp.wait_recv()
        for t in range(n):
            def add_chips(rows, t=t):
                g = r.gsum[t][rows, :]
                for j in range(3):
                    g = g + r.ici[t][j, rows, :].astype(F32)
                r.gsum[t][rows, :] = g

            chunks(t, add_chips)
        small = r.small()
        for cp in small:
            cp.wait_recv()
        tot = r.rsmall[r.me]
        for d in range(1, N_DEV):
            tot = tot + r.rsmall[jnp.bitwise_xor(r.me, d)]
        r.ssum[...] = tot
        for cp in small + r.level1()[1] + to_chips:
            cp.wait_send()

    hbm = pl.BlockSpec(memory_space=pl.ANY)
    dma = pltpu.SemaphoreType.DMA

    def whole(shape):
        return pl.BlockSpec(shape, lambda i: (0,) * len(shape))

    out_shapes = list(SHARD_SHAPES) + [(8, D_MODEL)]
    return Rider(
        args=list(grads3) + [arrived, small_part],
        in_specs=[hbm] * n + [whole(arrived.shape), whole(small_part.shape)],
        out_shape=[jax.ShapeDtypeStruct(s, F32) for s in out_shapes], out_specs=[whole(s) for s in out_shapes],
        scratch=[pltpu.VMEM((slots,) + s, BF16) for slots in (4, 4, 4, 3) for s in SHARD_SHAPES[:n]]
        + [pltpu.VMEM((N_DEV, 8, D_MODEL), F32), dma((n, 4)), dma((n, 4)), dma((n, 4)), dma((n, 3)), dma((n, 3)),
           dma((N_DEV - 1,)), dma((N_DEV - 1,))],
        start=start, finish=finish, stages=((mid_step, middle),))


def _adamw_update(grads, small_grad, wmv, small_wmv):
    n_small = len(small_wmv)

    def body(*refs):
        g_refs, sg_ref = refs[0:4], refs[4]
        wmv_refs = [refs[5 + 3 * t:8 + 3 * t] for t in range(4)]
        swmv_refs = [refs[17 + 3 * t:20 + 3 * t] for t in range(n_small)]
        outs = refs[17 + 3 * n_small:]
        out_refs = [outs[4 * t:4 * t + 4] for t in range(4)]
        sout_refs = [outs[16 + 4 * t:20 + 4 * t] for t in range(n_small)]
        loss_ref = outs[16 + 4 * n_small]
        for t, (w_ref, m_ref, v_ref) in enumerate(swmv_refs):
            g = sg_ref[t:t + 1, :w_ref.shape[1]]
            delta, m, v = _adamw(w_ref[...], g, m_ref[...], v_ref[...])
            sout_refs[t][0][...], sout_refs[t][1][...], sout_refs[t][2][...], sout_refs[t][3][...] = g, delta, m, v
        loss_ref[...] = (0.5 / D_MODEL) * jnp.sum(sg_ref[n_small:n_small + 1, :], axis=1, keepdims=True)
        for t in range(4):
            rows = ADAM_ROWS[t]
            w_ref, m_ref, v_ref = wmv_refs[t]
            g_out, d_out, m_out, v_out = out_refs[t]

            def step(i, carry, g_ref=g_refs[t], rows=rows, w_ref=w_ref, m_ref=m_ref, v_ref=v_ref,
                     g_out=g_out, d_out=d_out, m_out=m_out, v_out=v_out):
                r = pl.ds(pl.multiple_of(i * rows, rows), rows)
                g = g_ref[r, :]
                delta, m, v = _adamw(w_ref[r, :], g, m_ref[r, :], v_ref[r, :])
                g_out[r, :], d_out[r, :], m_out[r, :], v_out[r, :] = g, delta, m, v
                return carry

            lax.fori_loop(0, SHARD_SHAPES[t][0] // rows, step, 0)

    vmem = pl.BlockSpec(memory_space=pltpu.VMEM)
    flat_wmv = [a for trio in wmv for a in trio]
    flat_small = [a for trio in small_wmv for a in trio]
    out_shape = ([jax.ShapeDtypeStruct(s, F32) for s in SHARD_SHAPES for _ in range(4)]
                 + [jax.ShapeDtypeStruct(trio[0].shape, F32) for trio in small_wmv for _ in range(4)]
                 + [jax.ShapeDtypeStruct((1, 1), F32)])
    return _pcall(
        body, name="adamw",
        in_specs=[vmem] * (5 + len(flat_wmv) + len(flat_small)), out_specs=[vmem] * len(out_shape),
        out_shape=out_shape,
        compiler_params=_cparams(),
    )(*grads, small_grad, *flat_wmv, *flat_small)


def _small_rows(ln_g, ln_b, q_norm_g, kv_norm_g, extra=None):
    pad = lambda a: jnp.pad(a, (0, D_MODEL - a.shape[0]))
    rows = [ln_g, ln_b, pad(q_norm_g), pad(kv_norm_g)] + ([] if extra is None else [extra])
    return jnp.pad(jnp.stack(rows), ((0, 8 - len(rows)), (0, 0)))


def kernel(x, w_in, q_norm_g, kv_norm_g, w_uq, w_ukv, w_out, ln_g, ln_b, loss_target, m_w_in, m_q_norm_g, m_kv_norm_g, m_w_uq, m_w_ukv, m_w_out, m_ln_g, m_ln_b, v_w_in, v_q_norm_g, v_kv_norm_g, v_w_uq, v_w_ukv, v_w_out, v_ln_g, v_ln_b):
    w_in_r, w_uq_r, w_ukv_r = _all_gather_weights([w_in, w_uq, w_ukv])
    grad_x, sums = _local_step(
        x[0], loss_target[0], w_in_r, w_uq_r, w_ukv_r, _gather_w_out_rider(w_out), _scatter_g_out_rider,
        _reduce_grads_rider, q_norm_g, kv_norm_g, ln_g, ln_b)
    row = lambda a: a.reshape(1, -1)
    small_wmv = [(row(ln_g), row(m_ln_g), row(v_ln_g)), (row(ln_b), row(m_ln_b), row(v_ln_b)),
                 (row(q_norm_g), row(m_q_norm_g), row(v_q_norm_g)), (row(kv_norm_g), row(m_kv_norm_g), row(v_kv_norm_g))]
    wmv = [(w_in, m_w_in, v_w_in), (w_uq, m_w_uq, v_w_uq), (w_ukv, m_w_ukv, v_w_ukv), (w_out, m_w_out, v_w_out)]
    res = _adamw_update(sums[:4], sums[4], wmv, small_wmv)
    big = [res[4 * t:4 * t + 4] for t in range(4)]
    small = [[a.reshape(-1) for a in res[16 + 4 * t:20 + 4 * t]] for t in range(4)]
    loss = res[32].reshape(())

    def group(kind):
        return (big[0][kind], small[2][kind], small[3][kind], big[1][kind], big[2][kind], big[3][kind],
                small[0][kind], small[1][kind])

    return (loss, grad_x[None], *group(0), *group(1), *group(2), *group(3))
```

```python
import functools
from typing import Callable, NamedTuple

import numpy as np
import jax
import jax.numpy as jnp
from jax import lax
from jax.experimental import pallas as pl
from jax.experimental.pallas import tpu as pltpu

F32 = jnp.float32
BF16 = jnp.bfloat16

D_MODEL = 1024
ROPE_THETA = 500000.0
NEG = -1e30
RMS_EPS = 1e-6
LN_EPS = 1e-5
HEADS = 8
MLA_NOPE = 64
MLA_ROPE = 32
MLA_V = 64
Q_RANK = 384
KV_RANK = 256
DIL_HEAD = 64
DIL_ROT = 16
DIL_CONFIGS = ((128, 1), (512, 4), (2048, 16))
DIL_NEAR = 512
HW = HEADS * 64
QW = HW + HEADS * MLA_ROPE
IN_SPLITS = (Q_RANK, KV_RANK, MLA_ROPE, HW, HW, HW, HW, HW)
IN_WIDTH = sum(IN_SPLITS)
ALPHA = 2.0 ** 0.25
MLA_SCALE = (MLA_NOPE + MLA_ROPE) ** -0.5
DIL_SCALE = DIL_HEAD ** -0.5
LOG2E = 1.4426950408889634
LN2 = 0.6931471805599453

ADAM_LR = 0.001
ADAM_B1 = 0.9
ADAM_B2 = 0.999
ADAM_EPS = 1e-08
ADAM_WD = 0.01
ADAM_STEP = 10

N_DEV = 8
LANES = 128
VMEM_LIMIT = 56 * 1024 * 1024
BLOCK_TOKENS = 512
BLOCK_MLA = 512
BLOCK_DIL = 512

C_CQ, C_CKV, C_KR, C_GA, C_QB, C_KB, C_VB, C_GB, C_END = 0, 384, 640, 768, 1280, 1792, 2304, 2816, 3328

NT = (((1,), (1,)), ((), ()))
TN = (((0,), (0,)), ((), ()))


def _pcall(body, **kw):
    return pl.pallas_call(body, **kw)


def _cparams(**kw):
    return pltpu.CompilerParams(vmem_limit_bytes=VMEM_LIMIT, **kw)


def _rope_tables(seq):
    def tabs(dim, period):
        half = dim // 2
        inv = np.float32(ROPE_THETA) ** (-np.arange(0, dim, 2, dtype=np.float32) / np.float32(dim))
        ang = np.arange(seq, dtype=np.float32)[:, None] * inv.astype(np.float32)[None, :]
        cos, sin = np.cos(ang).astype(np.float32), np.sin(ang).astype(np.float32)
        j = np.arange(LANES) % period
        f = j % half
        c = np.where(j < dim, cos[:, f], np.float32(1.0))
        s1 = np.where(j < half, -sin[:, f], np.float32(0.0))
        s2 = np.where((j >= half) & (j < dim), sin[:, f], np.float32(0.0))
        return [c, s1, s2]
    return np.stack(tabs(MLA_ROPE, MLA_ROPE) + tabs(DIL_ROT, DIL_HEAD)).astype(np.float32)


def _rope(t, c, s1, s2, half):
    return t * c + pltpu.roll(t, LANES - half, 1) * s1 + pltpu.roll(t, half, 1) * s2


def _rope_t(d, c, s1, s2, half):
    return d * c + pltpu.roll(d * s1, half, 1) + pltpu.roll(d * s2, LANES - half, 1)


def _rope_wide(fn, t, c, s1, s2, half):
    return jnp.concatenate(
        [fn(t[:, i:i + LANES], c, s1, s2, half) for i in range(0, t.shape[1], LANES)], axis=1)


def _mla_bias_t(blk):
    a = np.arange(blk)
    causal = np.where(a[:, None] <= a[None, :], 0.0, NEG)
    return np.stack([np.zeros((blk, blk)), causal]).astype(np.float32)


def _dil_bias_t(blk, reach):
    a = np.arange(blk)
    out = []
    for off in range(-(-reach // blk) + 1):
        delta = blk * off + a[None, :] - a[:, None]
        mult = np.zeros((blk, blk))
        for window, dil in DIL_CONFIGS:
            mult += (delta >= 0) & (delta % dil == 0) & (delta <= min(window, reach))
        out.append(np.where(mult > 0, np.log2(np.maximum(mult, 1.0)), NEG))
    return np.stack(out).astype(np.float32)


def _dil_far_bias_t(length):
    window, dil = DIL_CONFIGS[-1]
    a = np.arange(length)
    steps_back = a[None, :] - a[:, None]
    seen = (steps_back * dil > DIL_NEAR) & (steps_back * dil <= window)
    return np.where(seen, 0.0, NEG).astype(np.float32)[None]


def _lanes_to_classes(a, dil):
    h, s = a.shape
    return a.reshape(h, s // dil, dil).transpose(0, 2, 1).reshape(h, s)


def _lanes_from_classes(a, dil):
    h, s = a.shape
    return a.reshape(h, dil, s // dil).transpose(0, 2, 1).reshape(h, s)


def _steps(nq, span, by_key, diag_only_bias):
    rows = []
    if by_key:
        for ki in range(nq):
            hi = min(nq - 1, ki + span)
            for qi in range(ki, hi + 1):
                rows.append((qi, ki, int(qi == ki), int(qi == hi)))
    else:
        for qi in range(nq):
            lo = max(0, qi - span)
            for ki in range(lo, qi + 1):
                rows.append((qi, ki, int(ki == lo), int(ki == qi)))
    arr = np.array(rows, dtype=np.int32)
    off = arr[:, 0] - arr[:, 1]
    bias_idx = (off == 0).astype(np.int32) if diag_only_bias else off.astype(np.int32)
    return [jnp.asarray(v) for v in (arr[:, 0], arr[:, 1], bias_idx, arr[:, 2], arr[:, 3])]


def _by_class(val, out_ref, lanes_sc):
    n_cls, per = out_ref.shape[0], out_ref.shape[1]
    for c in range(val.shape[1] // LANES):
        lanes_sc[c] = val[:, LANES * c:LANES * (c + 1)]
        for r in range(n_cls):
            rows = lanes_sc.at[c][pl.ds(r, per, stride=n_cls), :]
            out_ref[r, :, LANES * c:LANES * (c + 1)] = rows.astype(out_ref.dtype)


def _in_sequence(ref, lanes_sc):
    n_cls, per, width = ref.shape
    for c in range(width // LANES):
        for r in range(n_cls):
            lanes_sc.at[c][pl.ds(r, per, stride=n_cls), :] = ref[r, :, LANES * c:LANES * (c + 1)].astype(F32)
    return jnp.concatenate([lanes_sc[c] for c in range(width // LANES)], axis=1)


def _fwd_proj(x, w_in_r, w_uq_r, w_ukv_r, qg, kvg, tabs, bt, n_cls):
    seq = x.shape[0]

    def body(x_ref, win_ref, wuq_ref, wukv_ref, qg_ref, kvg_ref, tab_ref,
             cq_ref, ckv_ref, qn_ref, kvn_ref, qcat_ref, kn_ref, kpe_ref, v_ref,
             ga_ref, gb_ref, qb_ref, kb_ref, vb_ref, knt_ref, kpet_ref, vt_ref, kbt_ref, vbt_ref,
             qbc_ref, kbc_ref, vbc_ref, lanes_sc):
        xb = x_ref[...].astype(BF16)

        def proj(lo, hi):
            return jnp.dot(xb, win_ref[:, lo:hi], preferred_element_type=F32)

        m_tabs = (tab_ref[0], tab_ref[1], tab_ref[2])
        d_tabs = (tab_ref[3], tab_ref[4], tab_ref[5])

        cq = proj(C_CQ, C_CKV)
        cq_ref[...] = cq
        qn = (cq * lax.rsqrt(jnp.mean(cq * cq, axis=1, keepdims=True) + RMS_EPS) * qg_ref[...]).astype(BF16)
        qn_ref[...] = qn
        q = jnp.dot(qn, wuq_ref[...], preferred_element_type=F32)
        qcat_ref[:, :HW] = (q[:, :HW] * (MLA_SCALE * LOG2E)).astype(BF16)
        qcat_ref[:, HW:] = (_rope_wide(_rope, q[:, HW:], *m_tabs, MLA_ROPE // 2) * (MLA_SCALE * LOG2E)).astype(BF16)

        ckv = proj(C_CKV, C_KR)
        ckv_ref[...] = ckv
        kvn = (ckv * lax.rsqrt(jnp.mean(ckv * ckv, axis=1, keepdims=True) + RMS_EPS) * kvg_ref[...]).astype(BF16)
        kvn_ref[...] = kvn
        kv = jnp.dot(kvn, wukv_ref[...], preferred_element_type=F32)
        kn_ref[...] = kv[:, :HW].astype(BF16)
        v_ref[...] = kv[:, HW:].astype(BF16)
        knt_ref[...] = kv[:, :HW].T.astype(BF16)
        vt_ref[...] = kv[:, HW:].T.astype(BF16)

        kpe = _rope(proj(C_KR, C_GA), *m_tabs, MLA_ROPE // 2)
        kpe_ref[...] = kpe.astype(BF16)
        kpet_ref[...] = kpe.T[:MLA_ROPE, :].astype(BF16)
        ga_ref[...] = proj(C_GA, C_QB)
        qb = _rope_wide(_rope, proj(C_QB, C_KB), *d_tabs, DIL_ROT // 2) * (DIL_SCALE * LOG2E)
        qb_ref[...] = qb.astype(BF16)
        _by_class(qb, qbc_ref, lanes_sc)
        kb = _rope_wide(_rope, proj(C_KB, C_VB), *d_tabs, DIL_ROT // 2)
        kb_ref[...] = kb.astype(BF16)
        kbt_ref[...] = kb.T.astype(BF16)
        _by_class(kb, kbc_ref, lanes_sc)
        vb = proj(C_VB, C_GB)
        vb_ref[...] = vb.astype(BF16)
        vbt_ref[...] = vb.T.astype(BF16)
        _by_class(vb, vbc_ref, lanes_sc)
        gb_ref[...] = proj(C_GB, C_END)

    def tok(width):
        return pl.BlockSpec((bt, width), lambda i: (i, 0))

    def tok_t(height):
        return pl.BlockSpec((height, bt), lambda i: (0, i))

    def full(a):
        return pl.BlockSpec(a.shape, lambda i: (0,) * a.ndim)

    outs = [(Q_RANK, F32), (KV_RANK, F32), (Q_RANK, BF16), (KV_RANK, BF16), (QW, BF16), (HW, BF16),
            (LANES, BF16), (HW, BF16), (HW, F32), (HW, F32), (HW, BF16), (HW, BF16), (HW, BF16)]
    outs_t = [HW, MLA_ROPE, HW, HW, HW]
    by_class = pl.BlockSpec((n_cls, bt // n_cls, HW), lambda i: (0, i, 0))
    return _pcall(
        body, name="fwd_proj", grid=(seq // bt,),
        in_specs=[tok(D_MODEL), full(w_in_r), full(w_uq_r), full(w_ukv_r), full(qg), full(kvg),
                  pl.BlockSpec((6, bt, LANES), lambda i: (0, i, 0))],
        out_specs=[tok(w) for w, _ in outs] + [tok_t(h) for h in outs_t] + [by_class] * 3,
        out_shape=[jax.ShapeDtypeStruct((seq, w), dt) for w, dt in outs]
        + [jax.ShapeDtypeStruct((h, seq), BF16) for h in outs_t]
        + [jax.ShapeDtypeStruct((n_cls, seq // n_cls, HW), BF16)] * 3,
        scratch_shapes=[pltpu.VMEM((HW // LANES, bt, LANES), F32)],
        compiler_params=_cparams(dimension_semantics=("arbitrary",)),
    )(x, w_in_r, w_uq_r, w_ukv_r, qg, kvg, tabs)


def _head_masks(lane, h):
    e, g = h % 2, h % 4
    me = (lane >= 64 * e) & (lane < 64 * e + 64)
    mr = (lane >= 32 * g) & (lane < 32 * g + 32)
    return me, mr


def _masked(mask, a):
    return jnp.where(mask, a, jnp.zeros_like(a))


def _pair_operands(q_ref, k_ref, kpe_ref, lane, j, ks=slice(None), qs=slice(None)):
    cols = slice(LANES * j, LANES * (j + 1))
    qc = q_ref[qs, cols]
    kj = k_ref[ks, cols]
    kes = []
    for h in (2 * j, 2 * j + 1):
        me, mr = _head_masks(lane, h)
        ke = _masked(me, kj)
        if kpe_ref is not None:
            ke = jnp.concatenate([ke, _masked(mr, kpe_ref[ks, :])], axis=1)
        kes.append(ke)
    if kpe_ref is not None:
        qc = jnp.concatenate([qc, q_ref[qs, HW + LANES * (j // 2):HW + LANES * (j // 2 + 1)]], axis=1)
    return qc, kes


def _tile_variants(bias_t):
    out = {}
    for i, tile in enumerate(np.asarray(bias_t)):
        h = tile.shape[0] // 2
        skip = 1 if (tile[h:, :h] == NEG).all() else 2 if (tile[:h, h:] == NEG).all() else 0
        out[i] = (bool((tile != 0).any()), skip)
    return out


def _tile_parts(blk, skip):
    lo, hi, full = slice(0, blk // 2), slice(blk // 2, blk), slice(0, blk)
    return {0: [(full, full)], 1: [(lo, full), (hi, hi)], 2: [(hi, full), (lo, lo)]}[skip]


class Rider(NamedTuple):
    args: list
    in_specs: list
    out_shape: list
    out_specs: list
    scratch: list
    start: Callable
    finish: Callable
    stages: tuple = ()


def _ride_along(body, ride, n_prefetch, n_in, n_out, n_scratch, n_steps):
    if ride is None:
        return body

    def wrapped(*refs):
        pre, rest = refs[:n_prefetch], refs[n_prefetch:]
        a = n_in
        b = a + len(ride.args)
        c = b + n_out
        d = c + len(ride.out_shape)
        e = d + n_scratch
        mine = (rest[a:b], rest[c:d], rest[e:])
        t = pl.program_id(0)
        pl.when(t == 0)(lambda: ride.start(*mine))
        for at, stage in ride.stages:
            pl.when(t == at)(functools.partial(stage, *mine))
        body(*pre, *rest[:a], *rest[b:c], *rest[d:e])
        pl.when(t == n_steps - 1)(lambda: ride.finish(*mine))

    return wrapped


def _attn_fwd(name, q, k, kpe, vt, bias_t, steps, blk, ride=None, v_token_major=False):
    seq = q.shape[0]
    mla = kpe is not None
    n_steps = int(steps[0].shape[0])
    variants = _tile_variants(bias_t)

    def body(qi_r, ki_r, bi_r, fi_r, la_r, *refs):
        if mla:
            q_ref, k_ref, kpe_ref, vt_ref, b_ref, o_ref, lse_ref, m_sc, l_sc, acc_sc, st_sc = refs
        else:
            q_ref, k_ref, vt_ref, b_ref, o_ref, lse_ref, m_sc, l_sc, acc_sc, st_sc = refs
        t = pl.program_id(0)

        @pl.when(fi_r[t] == 1)
        def _():
            m_sc[...] = jnp.full(m_sc.shape, NEG, F32)
            l_sc[...] = jnp.zeros(l_sc.shape, F32)
            acc_sc[...] = jnp.zeros(acc_sc.shape, F32)

        lane = lax.broadcasted_iota(jnp.int32, (1, LANES), 1)
        if v_token_major:
            vt_all = vt_ref[...].astype(F32).T.astype(BF16)
            vt_rows = lambda rows, ks: vt_all[rows, ks]
        else:
            vt_rows = lambda rows, ks: vt_ref[rows, ks]

        def tile_pass(ks, qs, with_bias):
            nk, nq = ks.stop - ks.start, qs.stop - qs.start
            ones = jnp.ones((16, nk), BF16)

            def pair_scores(j):
                qc, kes = _pair_operands(q_ref, k_ref, kpe_ref if mla else None, lane, j, ks, qs)
                st = lax.dot_general(jnp.concatenate(kes, axis=0), qc, NT, preferred_element_type=F32)
                maxes = []
                for e in range(2):
                    se = st[e * nk:(e + 1) * nk]
                    if with_bias:
                        se = se + b_ref[0, ks, qs]
                    st_sc[j % 2, e * nk:(e + 1) * nk, 0:nq] = se
                    maxes.append(jnp.max(se, axis=0, keepdims=True))
                return maxes

            def softmax_pv(h, col_max):
                st = st_sc[(h // 2) % 2, (h % 2) * nk:(h % 2 + 1) * nk, 0:nq]
                hrow = slice(h, h + 1)
                m_prev = m_sc[hrow, qs]
                m_new = jnp.maximum(m_prev, col_max)
                alpha = jnp.exp2(m_prev - m_new)
                pt = jnp.exp2(st - m_new).astype(BF16)
                m_sc[hrow, qs] = m_new
                rows = slice(64 * h, 64 * h + 64)
                res = jnp.dot(jnp.concatenate([vt_rows(rows, ks), ones], axis=0), pt, preferred_element_type=F32)
                acc_sc[rows, qs] = alpha * acc_sc[rows, qs] + res[:64]
                l_sc[hrow, qs] = alpha * l_sc[hrow, qs] + res[64:65]

            maxes = pair_scores(0)
            for j in range(HEADS // 2):
                cur = maxes
                if j + 1 < HEADS // 2:
                    maxes = pair_scores(j + 1)
                softmax_pv(2 * j, cur[0])
                softmax_pv(2 * j + 1, cur[1])

        def step(with_bias, skip):
            for ks, qs in _tile_parts(blk, skip):
                tile_pass(ks, qs, with_bias)

        for idx, (with_bias, skip) in variants.items():
            if len(variants) == 1:
                step(with_bias, skip)
            else:
                pl.when(bi_r[t] == idx)(functools.partial(step, with_bias, skip))

        @pl.when(la_r[t] == 1)
        def _():
            for h in range(HEADS):
                rows = slice(64 * h, 64 * h + 64)
                acc_sc[rows, :] = acc_sc[rows, :] / l_sc[h:h + 1, :]
            o_ref[...] = acc_sc[...].T
            lse_ref[...] = m_sc[...] + jnp.log2(l_sc[...])

    qmap = lambda t, qi, ki, bi, fi, la: (qi[t], 0)
    kmap = lambda t, qi, ki, bi, fi, la: (ki[t], 0)
    in_specs = [pl.BlockSpec((blk, q.shape[1]), qmap), pl.BlockSpec((blk, HW), kmap)]
    args = [q, k]
    if mla:
        in_specs.append(pl.BlockSpec((blk, LANES), kmap))
        args.append(kpe)
    in_specs += [pl.BlockSpec((blk, HW), kmap) if v_token_major else
                 pl.BlockSpec((HW, blk), lambda t, qi, ki, bi, fi, la: (0, ki[t])),
                 pl.BlockSpec((1, blk, blk), lambda t, qi, ki, bi, fi, la: (bi[t], 0, 0))]
    args += [vt, jnp.asarray(bias_t)]
    out_specs = [pl.BlockSpec((blk, HW), qmap), pl.BlockSpec((HEADS, blk), lambda t, qi, ki, bi, fi, la: (0, qi[t]))]
    out_shape = [jax.ShapeDtypeStruct((seq, HW), F32), jax.ShapeDtypeStruct((HEADS, seq), F32)]
    scratch = [pltpu.VMEM((HEADS, blk), F32), pltpu.VMEM((HEADS, blk), F32),
               pltpu.VMEM((HW, blk), F32), pltpu.VMEM((2, 2 * blk, blk), F32)]
    body = _ride_along(body, ride, 5, len(args), len(out_shape), len(scratch), n_steps)
    if ride is not None:
        args, in_specs = args + ride.args, in_specs + ride.in_specs
        out_specs, out_shape, scratch = out_specs + ride.out_specs, out_shape + ride.out_shape, scratch + ride.scratch
    return _pcall(
        body, name=name,
        grid_spec=pltpu.PrefetchScalarGridSpec(
            num_scalar_prefetch=5, grid=(n_steps,), in_specs=in_specs, out_specs=out_specs, scratch_shapes=scratch),
        out_shape=out_shape,
        compiler_params=_cparams(dimension_semantics=("arbitrary",)),
    )(*steps, *args)


def _attn_bwd(name, q, k, kpe, v, kt, kpet, bias_t, do, lse, dstat, steps, blk, ride=None, single_visit=False):
    assert not (single_visit and kpe is not None) and (kt is not None or single_visit)
    seq = q.shape[0]
    mla = kpe is not None
    qw = q.shape[1]
    n_steps = int(steps[0].shape[0])
    dk_dtype = BF16 if mla else F32
    variants = _tile_variants(bias_t)

    def body(qi_r, ki_r, bi_r, fi_r, la_r, *refs):
        if mla:
            (q_ref, k_ref, kpe_ref, v_ref, kt_ref, kpet_ref, b_ref, do_ref, lse_ref, d_ref,
             dq_ref, dk_ref, dkpe_ref, dv_ref, dk_sc, dkpe_sc, dv_sc, st_sc, dpt_sc) = refs
        else:
            q_ref, k_ref, v_ref, *rest = refs
            kt_ref = rest.pop(0) if kt is not None else None
            b_ref, do_ref, lse_ref, d_ref, dq_out_ref, dk_ref, dv_ref, dk_sc, dv_sc, st_sc, dpt_sc, *rest = rest
            dq_ref = rest[0] if single_visit else dq_out_ref
        t = pl.program_id(0)

        @pl.when(jnp.logical_or(t == 0, single_visit))
        def _():
            dq_ref[...] = jnp.zeros(dq_ref.shape, F32)

        @pl.when(fi_r[t] == 1)
        def _():
            dk_sc[...] = jnp.zeros(dk_sc.shape, F32)
            dv_sc[...] = jnp.zeros(dv_sc.shape, F32)
            if mla:
                dkpe_sc[...] = jnp.zeros(dkpe_sc.shape, F32)

        qi = 0 if single_visit else qi_r[t]
        lane = lax.broadcasted_iota(jnp.int32, (1, LANES), 1)
        if kt is None:
            kt_all = k_ref[...].astype(F32).T.astype(BF16)
            kt_rows = lambda rows, ks: kt_all[rows, ks]
        else:
            kt_rows = lambda rows, ks: kt_ref[rows, ks]

        def tile_pass(ks, qs, with_bias):
            nk, nq = ks.stop - ks.start, qs.stop - qs.start

            def pair_matmuls(j):
                cols = slice(LANES * j, LANES * (j + 1))
                qc, kes = _pair_operands(q_ref, k_ref, kpe_ref if mla else None, lane, j, ks, qs)
                st_sc[j % 2, 0:2 * nk, 0:nq] = lax.dot_general(
                    jnp.concatenate(kes, axis=0), qc, NT, preferred_element_type=F32)
                vj = v_ref[ks, cols]
                ves = [_masked(_head_masks(lane, h)[0], vj) for h in (2 * j, 2 * j + 1)]
                dpt_sc[j % 2, 0:2 * nk, 0:nq] = lax.dot_general(
                    jnp.concatenate(ves, axis=0), do_ref[qs, cols], NT, preferred_element_type=F32)

            def pair_grads(j):
                cols = slice(LANES * j, LANES * (j + 1))
                qj, doj = q_ref[qs, cols], do_ref[qs, cols]
                if mla:
                    qr = q_ref[qs, HW + LANES * (j // 2):HW + LANES * (j // 2 + 1)]
                pts, dsts, qms, doms = [], [], [], []
                for e in range(2):
                    h = 2 * j + e
                    me, mr = _head_masks(lane, h)
                    st = st_sc[j % 2, e * nk:(e + 1) * nk, 0:nq]
                    if with_bias:
                        st = st + b_ref[0, ks, qs]
                    pt = jnp.exp2(st - lse_ref[h:h + 1, qs])
                    dst = (pt * (dpt_sc[j % 2, e * nk:(e + 1) * nk, 0:nq] - d_ref[h:h + 1, qs])).astype(BF16)
                    pts.append(pt.astype(BF16))
                    dsts.append(dst)
                    doms.append(_masked(me, doj))
                    qm = _masked(me, qj)
                    if mla:
                        qm = jnp.concatenate([qm, _masked(mr, qr)], axis=1)
                    qms.append(qm)
                    ktl = kt_rows(slice(64 * h, 64 * h + 64), ks)
                    if mla:
                        ktl = jnp.concatenate([ktl, kpet_ref[:, ks]], axis=0)
                    dqc = jnp.dot(ktl, dst, preferred_element_type=F32)
                    dq_ref[qi, 64 * h:64 * h + 64, qs] += dqc[:64]
                    if mla:
                        dq_ref[qi, HW + MLA_ROPE * h:HW + MLA_ROPE * (h + 1), qs] += dqc[64:]
                dv_sc[ks, cols] += jnp.dot(
                    jnp.concatenate(pts, axis=1), jnp.concatenate(doms, axis=0), preferred_element_type=F32)
                dkc = jnp.dot(jnp.concatenate(dsts, axis=1), jnp.concatenate(qms, axis=0), preferred_element_type=F32)
                dk_sc[ks, cols] += dkc[:, :LANES]
                if mla:
                    dkpe_sc[ks, :] += dkc[:, LANES:]

            pair_matmuls(0)
            for j in range(HEADS // 2):
                if j + 1 < HEADS // 2:
                    pair_matmuls(j + 1)
                pair_grads(j)

        def step(with_bias, skip):
            for ks, qs in _tile_parts(blk, skip):
                tile_pass(ks, qs, with_bias)

        for idx, (with_bias, skip) in variants.items():
            if len(variants) == 1:
                step(with_bias, skip)
            else:
                pl.when(bi_r[t] == idx)(functools.partial(step, with_bias, skip))
        if single_visit:
            dq_out_ref[...] = dq_ref[0].T

        @pl.when(la_r[t] == 1)
        def _():
            dk_ref[...] = (dk_sc[...] * LN2).astype(dk_ref.dtype)
            dv_ref[...] = dv_sc[...].astype(dv_ref.dtype)
            if mla:
                dkpe_ref[...] = dkpe_sc[...] * LN2

    qmap = lambda t, qi, ki, bi, fi, la: (qi[t], 0)
    kmap = lambda t, qi, ki, bi, fi, la: (ki[t], 0)
    qmap_t = lambda t, qi, ki, bi, fi, la: (0, qi[t])
    kmap_t = lambda t, qi, ki, bi, fi, la: (0, ki[t])
    in_specs = [pl.BlockSpec((blk, qw), qmap), pl.BlockSpec((blk, HW), kmap)]
    args = [q, k]
    if mla:
        in_specs.append(pl.BlockSpec((blk, LANES), kmap))
        args.append(kpe)
    in_specs.append(pl.BlockSpec((blk, HW), kmap))
    args.append(v)
    if kt is not None:
        in_specs.append(pl.BlockSpec((HW, blk), kmap_t))
        args.append(kt)
    if mla:
        in_specs.append(pl.BlockSpec((MLA_ROPE, blk), kmap_t))
        args.append(kpet)
    in_specs += [pl.BlockSpec((1, blk, blk), lambda t, qi, ki, bi, fi, la: (bi[t], 0, 0)),
                 pl.BlockSpec((blk, HW), qmap), pl.BlockSpec((HEADS, blk), qmap_t), pl.BlockSpec((HEADS, blk), qmap_t)]
    args += [jnp.asarray(bias_t), do, lse, dstat]
    dq_shape = (seq // blk, qw, blk)
    if single_visit:
        out_specs, out_shape = [pl.BlockSpec((blk, qw), qmap)], [jax.ShapeDtypeStruct((seq, qw), F32)]
    else:
        out_specs = [pl.BlockSpec(dq_shape, lambda t, qi, ki, bi, fi, la: (0, 0, 0))]
        out_shape = [jax.ShapeDtypeStruct(dq_shape, F32)]
    out_specs.append(pl.BlockSpec((blk, HW), kmap))
    out_shape.append(jax.ShapeDtypeStruct((seq, HW), dk_dtype))
    scratch = [pltpu.VMEM((blk, HW), F32)]
    if mla:
        out_specs.append(pl.BlockSpec((blk, LANES), kmap))
        out_shape.append(jax.ShapeDtypeStruct((seq, LANES), F32))
        scratch.append(pltpu.VMEM((blk, LANES), F32))
    out_specs.append(pl.BlockSpec((blk, HW), kmap))
    out_shape.append(jax.ShapeDtypeStruct((seq, HW), BF16))
    scratch.append(pltpu.VMEM((blk, HW), F32))
    scratch += [pltpu.VMEM((2, 2 * blk, blk), F32), pltpu.VMEM((2, 2 * blk, blk), F32)]
    if single_visit:
        scratch.append(pltpu.VMEM((1, qw, blk), F32))
    body = _ride_along(body, ride, 5, len(args), len(out_shape), len(scratch), n_steps)
    if ride is not None:
        args, in_specs = args + ride.args, in_specs + ride.in_specs
        out_specs, out_shape, scratch = out_specs + ride.out_specs, out_shape + ride.out_shape, scratch + ride.scratch
    return _pcall(
        body, name=name,
        grid_spec=pltpu.PrefetchScalarGridSpec(
            num_scalar_prefetch=5, grid=(n_steps,), in_specs=in_specs, out_specs=out_specs,
            scratch_shapes=scratch),
        out_shape=out_shape,
        compiler_params=_cparams(dimension_semantics=("arbitrary",)),
    )(*steps, *args)


def _out_ln(oa, ob_near, ob_far, lse_near, lse_far, ga, gb, x, tgt, w_out, ln_g, ln_b, bt):
    seq = x.shape[0]

    def body(oa_ref, obn_ref, obf_ref, lsen_ref, lsef_ref, ga_ref, gb_ref, x_ref, tgt_ref, w_ref, g_ref, b_ref,
             dz_ref, doa_ref, dob_ref, dga_ref, dgb_ref, da_ref, db_ref, lse_ref, gwb_ref, small_ref, dobc_ref,
             gw_ref, lanes_sc):
        i = pl.program_id(0)

        @pl.when(i == 0)
        def _():
            gw_ref[...] = jnp.zeros(gw_ref.shape, F32)
            small_ref[...] = jnp.zeros(small_ref.shape, F32)

        def gate(g):
            sig = 0.5 * jnp.tanh(0.5 * g) + 0.5
            return g * sig, sig * (1.0 + g * (1.0 - sig))

        lse_n, lse_f = lsen_ref[...], lsef_ref[...]
        top = jnp.maximum(lse_n, lse_f)
        e_n, e_f = jnp.exp2(lse_n - top), jnp.exp2(lse_f - top)
        lse_ref[...] = top + jnp.log2(e_n + e_f)
        inv = 1.0 / (e_n + e_f)
        head_row = lax.broadcasted_iota(jnp.int32, (2 * HEADS, HW), 0) % HEADS
        spread = (head_row == lax.broadcasted_iota(jnp.int32, (2 * HEADS, HW), 1) // 64).astype(BF16)

        def per_lane(w):
            hi = w.astype(BF16)
            lo = (w - hi.astype(F32)).astype(BF16)
            return lax.dot_general(jnp.concatenate([hi, lo], axis=0), spread, TN, preferred_element_type=F32)

        o_a = oa_ref[...]
        o_b = per_lane(e_n * inv) * obn_ref[...] + per_lane(e_f * inv) * _in_sequence(obf_ref, lanes_sc)
        g_a, g_b = ga_ref[...], gb_ref[...]
        sa, dsa = gate(g_a)
        sb, dsb = gate(g_b)
        mix = jnp.concatenate([o_a * sa, o_b * sb], axis=1).astype(BF16)
        z = ALPHA * x_ref[...] + jnp.dot(mix, w_ref[...], preferred_element_type=F32)
        mu = jnp.mean(z, axis=1, keepdims=True)
        zc = z - mu
        rstd = lax.rsqrt(jnp.mean(zc * zc, axis=1, keepdims=True) + LN_EPS)
        xhat = zc * rstd
        gam = g_ref[...]
        diff = xhat * gam + b_ref[...] - tgt_ref[...]
        dy = diff * (1.0 / D_MODEL)
        small_ref[0:1, :] += jnp.sum(dy * xhat, axis=0, keepdims=True)
        small_ref[1:2, :] += jnp.sum(dy, axis=0, keepdims=True)
        small_ref[2:3, :] += jnp.sum(diff * diff, axis=0, keepdims=True)
        dxh = dy * gam
        dz = rstd * (dxh - jnp.mean(dxh, axis=1, keepdims=True) - xhat * jnp.mean(dxh * xhat, axis=1, keepdims=True))
        dz_ref[...] = dz
        dzb = dz.astype(BF16)
        gw_ref[...] += lax.dot_general(mix, dzb, TN, preferred_element_type=F32)

        @pl.when(i == seq // bt - 1)
        def _():
            gwb_ref[...] = gw_ref[...].astype(BF16)

        dmix = lax.dot_general(dzb, w_ref[...], NT, preferred_element_type=F32)
        doa, dob = dmix[:, :HW] * sa, dmix[:, HW:] * sb
        doa_ref[...] = doa.astype(BF16)
        dob_ref[...] = dob.astype(BF16)
        _by_class(dob, dobc_ref, lanes_sc)
        dga_ref[...] = (dmix[:, :HW] * o_a * dsa).astype(BF16)
        dgb_ref[...] = (dmix[:, HW:] * o_b * dsb).astype(BF16)
        head_of = (lax.broadcasted_iota(jnp.int32, (2 * HW, LANES), 0) % HW) // 64
        ind = (head_of == lax.broadcasted_iota(jnp.int32, (2 * HW, LANES), 1)).astype(BF16)

        def head_sums(prod):
            hi = prod.astype(BF16)
            lo = (prod - hi.astype(F32)).astype(BF16)
            sums = jnp.dot(jnp.concatenate([hi, lo], axis=1), ind, preferred_element_type=F32)
            return sums.T[:HEADS, :]

        da_ref[...] = head_sums(doa * o_a)
        db_ref[...] = head_sums(dob * o_b)

    def tok(width):
        return pl.BlockSpec((bt, width), lambda i: (i, 0))

    def full(shape):
        return pl.BlockSpec(shape, lambda i: (0,) * len(shape))

    stat = pl.BlockSpec((HEADS, bt), lambda i: (0, i))
    n_cls = ob_far.shape[0]
    by_class = pl.BlockSpec((n_cls, bt // n_cls, HW), lambda i: (0, i, 0))
    return _pcall(
        body, name="out_ln", grid=(seq // bt,),
        in_specs=[tok(HW), tok(HW), by_class, stat, stat, tok(HW), tok(HW), tok(D_MODEL), tok(D_MODEL),
                  full((D_MODEL, D_MODEL)), full((1, D_MODEL)), full((1, D_MODEL))],
        out_specs=[tok(D_MODEL), tok(HW), tok(HW), tok(HW), tok(HW), stat, stat, stat,
                   full((D_MODEL, D_MODEL)), full((8, D_MODEL)), by_class],
        out_shape=[jax.ShapeDtypeStruct((seq, D_MODEL), F32)] + [jax.ShapeDtypeStruct((seq, HW), BF16)] * 4
        + [jax.ShapeDtypeStruct((HEADS, seq), F32)] * 3
        + [jax.ShapeDtypeStruct((D_MODEL, D_MODEL), BF16), jax.ShapeDtypeStruct((8, D_MODEL), F32),
           jax.ShapeDtypeStruct(ob_far.shape, BF16)],
        scratch_shapes=[pltpu.VMEM((D_MODEL, D_MODEL), F32), pltpu.VMEM((HW // LANES, bt, LANES), F32)],
        compiler_params=_cparams(dimension_semantics=("arbitrary",)),
    )(oa, ob_near, ob_far, lse_near, lse_far, ga, gb, x, tgt, w_out, ln_g, ln_b)


def _bwd_mid(dq_m, dkn, dv, dkpe, dqb, dkb, dvb, far, dga, dgb, cq, ckv, qn, kvn, w_uq_r, w_ukv_r, qg, kvg, tabs, bt):
    n_cls = far[0].shape[0]
    seq = cq.shape[0]

    def body(dqm_ref, dkn_ref, dv_ref, dkpe_ref, dqb_ref, dkb_ref, dvb_ref, dqf_ref, dkf_ref, dvf_ref, dga_ref, dgb_ref,
             cq_ref, ckv_ref, qn_ref, kvn_ref, wuq_ref, wukv_ref, qg_ref, kvg_ref, tab_ref,
             dh_ref, guq3_ref, gukv3_ref, small_ref, seq_sc, guq_ref, gukv_ref):
        i = pl.program_id(0)

        @pl.when(i == 0)
        def _():
            guq_ref[...] = jnp.zeros(guq_ref.shape, F32)
            gukv_ref[...] = jnp.zeros(gukv_ref.shape, F32)
            small_ref[...] = jnp.zeros(small_ref.shape, F32)

        m_tabs = (tab_ref[0], tab_ref[1], tab_ref[2])
        d_tabs = (tab_ref[3], tab_ref[4], tab_ref[5])

        def rms_bwd(c, dn, gain):
            r = lax.rsqrt(jnp.mean(c * c, axis=1, keepdims=True) + RMS_EPS)
            u = dn * gain
            dc = r * u - c * (r * r * r) * jnp.mean(u * c, axis=1, keepdims=True)
            return dc, jnp.sum(dn * c * r, axis=0, keepdims=True)

        dqm = dqm_ref[0].T
        dq = jnp.concatenate(
            [dqm[:, :HW], _rope_wide(_rope_t, dqm[:, HW:], *m_tabs, MLA_ROPE // 2)], axis=1) * MLA_SCALE
        dq = dq.astype(BF16)
        guq_ref[...] += lax.dot_general(qn_ref[...], dq, TN, preferred_element_type=F32)
        dqn = lax.dot_general(dq, wuq_ref[...], NT, preferred_element_type=F32)
        dcq, gq = rms_bwd(cq_ref[...], dqn, qg_ref[...])
        small_ref[0:1, :] += gq

        dkv = jnp.concatenate([dkn_ref[...], dv_ref[...]], axis=1)
        gukv_ref[...] += lax.dot_general(kvn_ref[...], dkv, TN, preferred_element_type=F32)
        dkvn = lax.dot_general(dkv, wukv_ref[...], NT, preferred_element_type=F32)
        dckv, gkv = rms_bwd(ckv_ref[...], dkvn, kvg_ref[...])
        small_ref[1:2, :KV_RANK] += gkv

        dh_ref[:, C_CQ:C_CKV] = dcq.astype(BF16)
        dh_ref[:, C_CKV:C_KR] = dckv.astype(BF16)
        dh_ref[:, C_KR:C_GA] = _rope_t(dkpe_ref[...], *m_tabs, MLA_ROPE // 2).astype(BF16)
        dh_ref[:, C_GA:C_QB] = dga_ref[...]
        in_sequence = functools.partial(_in_sequence, lanes_sc=seq_sc)
        dqb = dqb_ref[0].T + in_sequence(dqf_ref)
        dh_ref[:, C_QB:C_KB] = (_rope_wide(_rope_t, dqb, *d_tabs, DIL_ROT // 2) * DIL_SCALE).astype(BF16)
        dkb = dkb_ref[...] + in_sequence(dkf_ref)
        dh_ref[:, C_KB:C_VB] = _rope_wide(_rope_t, dkb, *d_tabs, DIL_ROT // 2).astype(BF16)
        dh_ref[:, C_VB:C_GB] = (dvb_ref[...].astype(F32) + in_sequence(dvf_ref)).astype(BF16)
        dh_ref[:, C_GB:C_END] = dgb_ref[...]

        @pl.when(i == seq // bt - 1)
        def _():
            for h in range(HEADS):
                guq3_ref[h] = jnp.concatenate(
                    [guq_ref[:, MLA_NOPE * h:MLA_NOPE * (h + 1)],
                     guq_ref[:, HW + MLA_ROPE * h:HW + MLA_ROPE * (h + 1)]], axis=1).astype(BF16)
                gukv3_ref[h] = jnp.concatenate(
                    [gukv_ref[:, MLA_NOPE * h:MLA_NOPE * (h + 1)],
                     gukv_ref[:, HW + MLA_V * h:HW + MLA_V * (h + 1)]], axis=1).astype(BF16)

    def tok(width):
        return pl.BlockSpec((bt, width), lambda i: (i, 0))

    def tok_t(a):
        per = a.shape[2] // bt
        return pl.BlockSpec((1, a.shape[1], bt), lambda i: (i // per, 0, i % per))

    def full(shape):
        return pl.BlockSpec(shape, lambda i: (0,) * len(shape))

    by_class = pl.BlockSpec((n_cls, bt // n_cls, HW), lambda i: (0, i, 0))
    uq3 = (HEADS, Q_RANK, MLA_NOPE + MLA_ROPE)
    ukv3 = (HEADS, KV_RANK, MLA_NOPE + MLA_V)
    return _pcall(
        body, name="bwd_mid", grid=(seq // bt,),
        in_specs=[tok_t(dq_m), tok(HW), tok(HW), tok(LANES), tok_t(dqb), tok(HW), tok(HW), by_class, by_class, by_class,
                  tok(HW), tok(HW),
                  tok(Q_RANK), tok(KV_RANK), tok(Q_RANK), tok(KV_RANK),
                  full(w_uq_r.shape), full(w_ukv_r.shape), full((1, Q_RANK)), full((1, KV_RANK)),
                  pl.BlockSpec((6, bt, LANES), lambda i: (0, i, 0))],
        out_specs=[tok(C_END), full(uq3), full(ukv3), full((8, Q_RANK))],
        out_shape=[jax.ShapeDtypeStruct((seq, C_END), BF16), jax.ShapeDtypeStruct(uq3, BF16),
                   jax.ShapeDtypeStruct(ukv3, BF16), jax.ShapeDtypeStruct((8, Q_RANK), F32)],
        scratch_shapes=[pltpu.VMEM((HW // LANES, bt, LANES), F32), pltpu.VMEM(w_uq_r.shape, F32),
                        pltpu.VMEM(w_ukv_r.shape, F32)],
        compiler_params=_cparams(dimension_semantics=("arbitrary",)),
    )(dq_m, dkn, dv, dkpe, dqb, dkb, dvb, *far, dga, dgb, cq, ckv, qn, kvn, w_uq_r, w_ukv_r, qg, kvg, tabs)


def _grad_x(dz, dh, w_in_r, bt, ride=None):
    seq = dz.shape[0]
    n_steps = seq // bt

    def body(dz_ref, dh_ref, w_ref, gx_ref):
        gx_ref[...] = ALPHA * dz_ref[...] + lax.dot_general(
            dh_ref[...], w_ref[...], NT, preferred_element_type=F32)

    args = [dz, dh, w_in_r]
    in_specs = [pl.BlockSpec((bt, D_MODEL), lambda i: (i, 0)), pl.BlockSpec((bt, C_END), lambda i: (i, 0)),
                pl.BlockSpec(w_in_r.shape, lambda i: (0, 0))]
    out_specs = [pl.BlockSpec((bt, D_MODEL), lambda i: (i, 0))]
    out_shape = [jax.ShapeDtypeStruct((seq, D_MODEL), F32)]
    scratch = []
    body = _ride_along(body, ride, 0, len(args), len(out_shape), 0, n_steps)
    if ride is not None:
        args, in_specs = args + ride.args, in_specs + ride.in_specs
        out_specs, out_shape, scratch = out_specs + ride.out_specs, out_shape + ride.out_shape, ride.scratch
    return _pcall(
        body, name="grad_x", grid=(n_steps,),
        in_specs=in_specs, out_specs=out_specs, out_shape=out_shape, scratch_shapes=scratch,
        compiler_params=_cparams(dimension_semantics=("arbitrary",)),
    )(*args)


def _grad_w_in(x, dh, bt):
    seq = x.shape[0]
    shard = IN_WIDTH // N_DEV
    k_lo, k_hi = IN_SPLITS[0] + IN_SPLITS[1], IN_SPLITS[0] + IN_SPLITS[1] + MLA_ROPE

    def body(x_ref, dh_ref, out_ref, acc):
        i = pl.program_id(0)

        @pl.when(i == 0)
        def _():
            acc[...] = jnp.zeros(acc.shape, F32)

        acc[...] += lax.dot_general(x_ref[...].astype(BF16), dh_ref[...], TN, preferred_element_type=F32)

        @pl.when(i == seq // bt - 1)
        def _():
            kr = acc[:, C_KR:C_GA]
            kr = kr + pltpu.roll(kr, 96, 1) + pltpu.roll(kr, 64, 1) + pltpu.roll(kr, 32, 1)
            for d in range(N_DEV):
                lo, hi = shard * d, shard * (d + 1)
                pieces = []
                if lo < k_lo:
                    pieces.append(acc[:, lo:min(hi, k_lo)])
                if lo < k_hi and hi > k_lo:
                    pieces.append(kr[:, max(lo, k_lo) - k_lo:min(hi, k_hi) - k_lo])
                if hi > k_hi:
                    shift = C_GA - k_hi
                    pieces.append(acc[:, max(lo, k_hi) + shift:hi + shift])
                blk = pieces[0] if len(pieces) == 1 else jnp.concatenate(pieces, axis=1)
                out_ref[d] = blk.astype(BF16)

    return _pcall(
        body, name="grad_w_in", grid=(seq // bt,),
        in_specs=[pl.BlockSpec((bt, D_MODEL), lambda i: (i, 0)), pl.BlockSpec((bt, C_END), lambda i: (i, 0))],
        out_specs=pl.BlockSpec((N_DEV, D_MODEL, shard), lambda i: (0, 0, 0)),
        out_shape=jax.ShapeDtypeStruct((N_DEV, D_MODEL, shard), BF16),
        scratch_shapes=[pltpu.VMEM((D_MODEL, C_END), F32)],
        compiler_params=_cparams(dimension_semantics=("arbitrary",)),
    )(x, dh)


def _local_step(x, tgt, w_in_r, w_uq_r, w_ukv_r, w_out_rider, g_out_rider, reduce_rider, q_norm_g, kv_norm_g,
                ln_g, ln_b, bt=BLOCK_TOKENS, blk_m=BLOCK_MLA, blk_d=BLOCK_DIL):
    seq = x.shape[0]
    tabs = jnp.asarray(_rope_tables(seq))
    qg, kvg = q_norm_g.reshape(1, -1), kv_norm_g.reshape(1, -1)

    far_dil = DIL_CONFIGS[-1][1]
    cls = seq // far_dil
    (cq, ckv, qn, kvn, qcat, kn, kpe, v, ga, gb, qb, kb, vb, knt, kpet, vt, kbt, vbt, qb_c, kb_c, vb_c) = _fwd_proj(
        x, w_in_r, w_uq_r, w_ukv_r, qg, kvg, tabs, bt, far_dil)
    qb_c, kb_c, vb_c = (a.reshape(seq, HW) for a in (qb_c, kb_c, vb_c))

    nq_m, nq_d = seq // blk_m, seq // blk_d
    bias_m = _mla_bias_t(blk_m)
    oa, lse_a, w_out = _attn_fwd(
        "mla_fwd", qcat, kn, kpe, vt, bias_m, _steps(nq_m, nq_m, False, True), blk_m, ride=w_out_rider)

    bias_near = _dil_bias_t(blk_d, DIL_NEAR)
    ob_near, lse_near = _attn_fwd(
        "dil_fwd", qb, kb, None, vbt, bias_near, _steps(nq_d, -(-DIL_NEAR // blk_d), False, False), blk_d)
    each = jnp.arange(far_dil, dtype=jnp.int32)
    steps_far = [each, each, jnp.zeros_like(each), jnp.ones_like(each), jnp.ones_like(each)]
    bias_far = _dil_far_bias_t(cls)
    ob_far, lse_far = _attn_fwd(
        "dil_far_fwd", qb_c, kb_c, None, vb_c, bias_far, steps_far, cls, v_token_major=True)

    dz, doa, dob, dga, dgb, dst_a, dst_b, lse_b, g_out, small1, dob_c = _out_ln(
        oa, ob_near, ob_far.reshape(far_dil, cls, HW), lse_near, _lanes_from_classes(lse_far, far_dil), ga, gb, x, tgt,
        w_out.reshape(D_MODEL, D_MODEL), ln_g.reshape(1, -1), ln_b.reshape(1, -1), bt)

    dq_m, dkn, dkpe, dv, g_out_recv = _attn_bwd(
        "mla_bwd", qcat, kn, kpe, v, knt, kpet, bias_m, doa, lse_a, dst_a, _steps(nq_m, nq_m, True, True), blk_m,
        ride=g_out_rider(g_out.reshape(N_DEV, D_MODEL // N_DEV, D_MODEL)))
    dqb, dkb_near, dvb_near = _attn_bwd(
        "dil_bwd", qb, kb, None, vb, kbt, None, bias_near, dob, lse_b, dst_b,
        _steps(nq_d, -(-DIL_NEAR // blk_d), True, False), blk_d)
    dqb_far, dkb_far, dvb_far = _attn_bwd(
        "dil_far_bwd", qb_c, kb_c, None, vb_c, None, None, bias_far, dob_c.reshape(seq, HW),
        _lanes_to_classes(lse_b, far_dil), _lanes_to_classes(dst_b, far_dil), steps_far, cls, single_visit=True)
    far = [a.reshape(far_dil, cls, HW) for a in (dqb_far, dkb_far, dvb_far)]

    dh, g_uq, g_ukv, small2 = _bwd_mid(
        dq_m, dkn, dv, dkpe, dqb, dkb_near, dvb_near, far, dga, dgb, cq, ckv, qn, kvn, w_uq_r, w_ukv_r, qg, kvg, tabs, bt)
    g_in = _grad_w_in(x, dh, min(seq, 2 * bt))
    grads3 = [g_in, g_uq, g_ukv]
    small_part = _small_rows(small1[0], small1[1], small2[0, :Q_RANK], small2[1, :KV_RANK], small1[2])
    grad_x, *reduced = _grad_x(
        dz, dh, w_in_r, bt, ride=reduce_rider(grads3, g_out_recv, small_part, min(1, seq // bt - 1)))
    return grad_x, reduced


MESH_ID = pl.DeviceIdType.MESH
SHARD_SHAPES = ((D_MODEL, IN_WIDTH // N_DEV), (Q_RANK, 768 // N_DEV), (KV_RANK, 1024 // N_DEV), (D_MODEL // N_DEV, D_MODEL))
ADAM_ROWS = (32, 128, 128, 16)


def _me():
    x, y, c = lax.axis_index("x"), lax.axis_index("y"), lax.axis_index("c")
    return x, y, c, 4 * x + 2 * y + c


def _peer(k):
    x, y, c, _ = _me()
    px = 1 - x if (k >> 2) & 1 else x
    py = 1 - y if (k >> 1) & 1 else y
    pc = 1 - c if k & 1 else c
    return (px, py, pc), 4 * px + 2 * py + pc


def _all_gather_weights(shards):
    n = len(shards)
    shard = IN_WIDTH // N_DEV
    k_lo = IN_SPLITS[0] + IN_SPLITS[1]
    k_hi = k_lo + MLA_ROPE

    def body(*refs):
        ins = refs[:n]
        win_ref, wuq_ref, wukv_ref = refs[n:2 * n]
        bufs = refs[2 * n:3 * n]
        send_sems, recv_sems = refs[3 * n:]
        x, y, c, me = _me()
        here, sibling = (x, y, c), (x, y, 1 - c)
        chips = [(1 - x, y), (x, 1 - y), (1 - x, 1 - y)]
        for t in range(n):
            bufs[t][me] = ins[t][...].astype(BF16)

        def copy(t, k, px, py, pc, to):
            blk = bufs[t].at[4 * px + 2 * py + pc]
            return pltpu.make_async_remote_copy(
                src_ref=blk, dst_ref=blk, send_sem=send_sems.at[t, k], recv_sem=recv_sems.at[t, k],
                device_id=to, device_id_type=MESH_ID)

        first = []
        for t in range(n):
            first.append(copy(t, 0, x, y, c, sibling))
            for j, (px, py) in enumerate(chips):
                first.append(copy(t, 1 + j, x, y, c, (px, py, c)))
        for cp in first:
            cp.start()
        passed = []
        for j, (px, py) in enumerate(chips):
            for t in range(n):
                copy(t, 1 + j, px, py, c, here).wait_recv()
                cp = copy(t, 4 + j, px, py, c, sibling)
                cp.start()
                passed.append(cp)
        for t in range(n):
            copy(t, 0, x, y, 1 - c, here).wait_recv()
        for j, (px, py) in enumerate(chips):
            for t in range(n):
                copy(t, 4 + j, px, py, 1 - c, here).wait_recv()
        for cp in first + passed:
            cp.wait_send()

        a_in, a_uq, a_ukv = bufs
        for d in range(N_DEV):
            lo, hi = shard * d, shard * (d + 1)
            if lo < k_lo:
                win_ref[:, lo:min(hi, k_lo)] = a_in[d, :, 0:min(hi, k_lo) - lo]
            if lo < k_hi and hi > k_lo:
                kr = a_in[d, :, k_lo - lo:k_hi - lo]
                for rep in range(4):
                    win_ref[:, C_KR + MLA_ROPE * rep:C_KR + MLA_ROPE * (rep + 1)] = kr
            if hi > k_hi:
                src = max(lo, k_hi)
                win_ref[:, src + C_GA - k_hi:hi + C_GA - k_hi] = a_in[d, :, src - lo:hi - lo]
        for h in range(HEADS):
            wuq_ref[:, MLA_NOPE * h:MLA_NOPE * (h + 1)] = a_uq[h, :, :MLA_NOPE]
            wuq_ref[:, HW + MLA_ROPE * h:HW + MLA_ROPE * (h + 1)] = a_uq[h, :, MLA_NOPE:]
            wukv_ref[:, MLA_NOPE * h:MLA_NOPE * (h + 1)] = a_ukv[h, :, :MLA_NOPE]
            wukv_ref[:, HW + MLA_V * h:HW + MLA_V * (h + 1)] = a_ukv[h, :, MLA_NOPE:]

    vmem = pl.BlockSpec(memory_space=pltpu.VMEM)
    return _pcall(
        body, name="gather_weights",
        in_specs=[vmem] * n, out_specs=[vmem] * n,
        out_shape=[jax.ShapeDtypeStruct((D_MODEL, C_END), BF16), jax.ShapeDtypeStruct((Q_RANK, QW), BF16),
                   jax.ShapeDtypeStruct((KV_RANK, 2 * HW), BF16)],
        scratch_shapes=[pltpu.VMEM((N_DEV,) + s, BF16) for s in SHARD_SHAPES[:n]]
        + [pltpu.SemaphoreType.DMA((n, N_DEV - 1)), pltpu.SemaphoreType.DMA((n, N_DEV - 1))],
        compiler_params=_cparams(),
    )(*shards)


def _gather_w_out_rider(w_out):
    def copies(full_ref, stage, send_sems, recv_sems):
        me = _me()[3]
        out = []
        for k in range(1, N_DEV):
            peer, pidx = _peer(k)
            send = pltpu.make_async_remote_copy(
                src_ref=stage, dst_ref=full_ref.at[me], send_sem=send_sems.at[k - 1], recv_sem=recv_sems.at[k - 1],
                device_id=peer, device_id_type=MESH_ID)
            recv = pltpu.make_async_remote_copy(
                src_ref=stage, dst_ref=full_ref.at[pidx], send_sem=send_sems.at[k - 1], recv_sem=recv_sems.at[k - 1],
                device_id=peer, device_id_type=MESH_ID)
            out.append((send, recv))
        return out

    def start(ins, outs, scr):
        stage, send_sems, recv_sems, own_sem = scr
        stage[...] = ins[0][...].astype(BF16)
        pltpu.make_async_copy(stage, outs[0].at[_me()[3]], own_sem).start()
        for send, _ in copies(outs[0], stage, send_sems, recv_sems):
            send.start()

    def finish(ins, outs, scr):
        stage, send_sems, recv_sems, own_sem = scr
        pltpu.make_async_copy(stage, outs[0].at[_me()[3]], own_sem).wait()
        pairs = copies(outs[0], stage, send_sems, recv_sems)
        for _, recv in pairs:
            recv.wait_recv()
        for send, _ in pairs:
            send.wait_send()

    shape = SHARD_SHAPES[3]
    return Rider(
        args=[w_out], in_specs=[pl.BlockSpec(shape, lambda t, *_: (0, 0))],
        out_shape=[jax.ShapeDtypeStruct((N_DEV,) + shape, BF16)], out_specs=[pl.BlockSpec(memory_space=pl.ANY)],
        scratch=[pltpu.VMEM(shape, BF16), pltpu.SemaphoreType.DMA((N_DEV - 1,)), pltpu.SemaphoreType.DMA((N_DEV - 1,)),
                 pltpu.SemaphoreType.DMA],
        start=start, finish=finish)


def _scatter_g_out_rider(blocks):
    def copies(src_ref, dst_ref, send_sems, recv_sems):
        out = []
        for k in range(1, N_DEV):
            peer, pidx = _peer(k)
            out.append(pltpu.make_async_remote_copy(
                src_ref=src_ref.at[pidx], dst_ref=dst_ref.at[k], send_sem=send_sems.at[k - 1],
                recv_sem=recv_sems.at[k - 1], device_id=peer, device_id_type=MESH_ID))
        return out

    def start(ins, outs, scr):
        send_sems, recv_sems, own_sem = scr
        pltpu.make_async_copy(ins[0].at[_me()[3]], outs[0].at[0], own_sem).start()
        for cp in copies(ins[0], outs[0], send_sems, recv_sems):
            cp.start()

    def finish(ins, outs, scr):
        send_sems, recv_sems, own_sem = scr
        pltpu.make_async_copy(ins[0].at[_me()[3]], outs[0].at[0], own_sem).wait()
        for cp in copies(ins[0], outs[0], send_sems, recv_sems):
            cp.wait()

    hbm = pl.BlockSpec(memory_space=pl.ANY)
    return Rider(
        args=[blocks], in_specs=[hbm], out_shape=[jax.ShapeDtypeStruct(blocks.shape, blocks.dtype)], out_specs=[hbm],
        scratch=[pltpu.SemaphoreType.DMA((N_DEV - 1,)), pltpu.SemaphoreType.DMA((N_DEV - 1,)), pltpu.SemaphoreType.DMA],
        start=start, finish=finish)


def _adamw(w, g, m, v):
    m = ADAM_B1 * m + (1.0 - ADAM_B1) * g
    v = ADAM_B2 * v + (1.0 - ADAM_B2) * jnp.square(g)
    m_hat = m / (1.0 - ADAM_B1 ** ADAM_STEP)
    v_hat = v / (1.0 - ADAM_B2 ** ADAM_STEP)
    delta = -ADAM_LR * (m_hat / (jnp.sqrt(v_hat) + ADAM_EPS) + ADAM_WD * w)
    return delta, m, v


def _reduce_grads_rider(grads3, arrived, small_part, mid_step):
    n = len(grads3)

    class Refs:
        def __init__(self, ins, outs, scr):
            self.g3, self.arr, self.sp = ins[0:n], ins[n], ins[n + 1]
            self.gsum, self.gsum_out, self.ssum = outs[0:n], outs[n], outs[n + 1]
            self.own, self.sib, self.part, self.ici = scr[0:n], scr[n:2 * n], scr[2 * n:3 * n], scr[3 * n:4 * n]
            self.rsmall = scr[4 * n]
            (self.loc_sems, self.d2d_send, self.d2d_recv, self.ici_send, self.ici_recv,
             self.sm_send, self.sm_recv) = scr[4 * n + 1:]
            self.x, self.y, self.c, self.me = _me()
            self.chips = [(1 - self.x, self.y), (self.x, 1 - self.y), (1 - self.x, 1 - self.y)]

        def small(self):
            return [pltpu.make_async_remote_copy(
                src_ref=self.rsmall.at[0], dst_ref=self.rsmall.at[k], send_sem=self.sm_send.at[k - 1],
                recv_sem=self.sm_recv.at[k - 1], device_id=_peer(k)[0], device_id_type=MESH_ID)
                for k in range(1, N_DEV)]

        def level1(self):
            local, to_sib = [], []
            for t in range(n):
                for q in range(4):
                    local.append(pltpu.make_async_copy(
                        self.g3[t].at[2 * q + self.c], self.own[t].at[q], self.loc_sems.at[t, q]))
                    to_sib.append(pltpu.make_async_remote_copy(
                        src_ref=self.g3[t].at[2 * q + 1 - self.c], dst_ref=self.sib[t].at[q],
                        send_sem=self.d2d_send.at[t, q], recv_sem=self.d2d_recv.at[t, q],
                        device_id=(self.x, self.y, 1 - self.c), device_id_type=MESH_ID))
            return local, to_sib

        def level2(self):
            return [pltpu.make_async_remote_copy(
                src_ref=self.part[t].at[2 * px + py], dst_ref=self.ici[t].at[j], send_sem=self.ici_send.at[t, j],
                recv_sem=self.ici_recv.at[t, j], device_id=(px, py, self.c), device_id_type=MESH_ID)
                for t in range(n) for j, (px, py) in enumerate(self.chips)]

    def chunks(t, fn):
        rows = ADAM_ROWS[t]

        def step(i, carry):
            fn(pl.ds(pl.multiple_of(i * rows, rows), rows))
            return carry

        lax.fori_loop(0, SHARD_SHAPES[t][0] // rows, step, 0)

    def start(*refs):
        r = Refs(*refs)
        r.rsmall[0] = r.sp[...]
        local, to_sib = r.level1()
        for cp in r.small() + local + to_sib:
            cp.start()

    def middle(*refs):
        r = Refs(*refs)
        local, to_sib = r.level1()
        for cp in local:
            cp.wait()
        for cp in to_sib:
            cp.wait_recv()
        my_chip = 2 * r.x + r.y
        for t in range(n):
            def pair_sums(rows, t=t):
                for q in range(4):
                    r.part[t][q, rows, :] = (
                        r.own[t][q, rows, :].astype(F32) + r.sib[t][q, rows, :].astype(F32)).astype(BF16)
                r.gsum[t][rows, :] = r.own[t][my_chip, rows, :].astype(F32) + r.sib[t][my_chip, rows, :].astype(F32)

            chunks(t, pair_sums)
        for cp in r.level2():
            cp.start()

        def add_arrived(rows):
            g = r.arr[0, rows, :].astype(F32)
            for k in range(1, N_DEV):
                g = g + r.arr[k, rows, :].astype(F32)
            r.gsum_out[rows, :] = g

        chunks(3, add_arrived)

    def finish(*refs):
        r = Refs(*refs)
        to_chips = r.level2()
        for cp in to_chips:
            cp.wait_recv()
        for t in range(n):
            def add_chips(rows, t=t):
                g = r.gsum[t][rows, :]
                for j in range(3):
                    g = g + r.ici[t][j, rows, :].astype(F32)
                r.gsum[t][rows, :] = g

            chunks(t, add_chips)
        small = r.small()
        for cp in small:
            cp.wait_recv()
        tot = r.rsmall[r.me]
        for d in range(1, N_DEV):
            tot = tot + r.rsmall[jnp.bitwise_xor(r.me, d)]
        r.ssum[...] = tot
        for cp in small + r.level1()[1] + to_chips:
            cp.wait_send()

    hbm = pl.BlockSpec(memory_space=pl.ANY)
    dma = pltpu.SemaphoreType.DMA

    def whole(shape):
        return pl.BlockSpec(shape, lambda i: (0,) * len(shape))

    out_shapes = list(SHARD_SHAPES) + [(8, D_MODEL)]
    return Rider(
        args=list(grads3) + [arrived, small_part],
        in_specs=[hbm] * n + [whole(arrived.shape), whole(small_part.shape)],
        out_shape=[jax.ShapeDtypeStruct(s, F32) for s in out_shapes], out_specs=[whole(s) for s in out_shapes],
        scratch=[pltpu.VMEM((slots,) + s, BF16) for slots in (4, 4, 4, 3) for s in SHARD_SHAPES[:n]]
        + [pltpu.VMEM((N_DEV, 8, D_MODEL), F32), dma((n, 4)), dma((n, 4)), dma((n, 4)), dma((n, 3)), dma((n, 3)),
           dma((N_DEV - 1,)), dma((N_DEV - 1,))],
        start=start, finish=finish, stages=((mid_step, middle),))


def _adamw_update(grads, small_grad, wmv, small_wmv):
    n_small = len(small_wmv)

    def body(*refs):
        g_refs, sg_ref = refs[0:4], refs[4]
        wmv_refs = [refs[5 + 3 * t:8 + 3 * t] for t in range(4)]
        swmv_refs = [refs[17 + 3 * t:20 + 3 * t] for t in range(n_small)]
        outs = refs[17 + 3 * n_small:]
        out_refs = [outs[4 * t:4 * t + 4] for t in range(4)]
        sout_refs = [outs[16 + 4 * t:20 + 4 * t] for t in range(n_small)]
        loss_ref = outs[16 + 4 * n_small]
        for t, (w_ref, m_ref, v_ref) in enumerate(swmv_refs):
            g = sg_ref[t:t + 1, :w_ref.shape[1]]
            delta, m, v = _adamw(w_ref[...], g, m_ref[...], v_ref[...])
            sout_refs[t][0][...], sout_refs[t][1][...], sout_refs[t][2][...], sout_refs[t][3][...] = g, delta, m, v
        loss_ref[...] = (0.5 / D_MODEL) * jnp.sum(sg_ref[n_small:n_small + 1, :], axis=1, keepdims=True)
        for t in range(4):
            rows = ADAM_ROWS[t]
            w_ref, m_ref, v_ref = wmv_refs[t]
            g_out, d_out, m_out, v_out = out_refs[t]

            def step(i, carry, g_ref=g_refs[t], rows=rows, w_ref=w_ref, m_ref=m_ref, v_ref=v_ref,
                     g_out=g_out, d_out=d_out, m_out=m_out, v_out=v_out):
                r = pl.ds(pl.multiple_of(i * rows, rows), rows)
                g = g_ref[r, :]
                delta, m, v = _adamw(w_ref[r, :], g, m_ref[r, :], v_ref[r, :])
                g_out[r, :], d_out[r, :], m_out[r, :], v_out[r, :] = g, delta, m, v
                return carry

            lax.fori_loop(0, SHARD_SHAPES[t][0] // rows, step, 0)

    vmem = pl.BlockSpec(memory_space=pltpu.VMEM)
    flat_wmv = [a for trio in wmv for a in trio]
    flat_small = [a for trio in small_wmv for a in trio]
    out_shape = ([jax.ShapeDtypeStruct(s, F32) for s in SHARD_SHAPES for _ in range(4)]
                 + [jax.ShapeDtypeStruct(trio[0].shape, F32) for trio in small_wmv for _ in range(4)]
                 + [jax.ShapeDtypeStruct((1, 1), F32)])
    return _pcall(
        body, name="adamw",
        in_specs=[vmem] * (5 + len(flat_wmv) + len(flat_small)), out_specs=[vmem] * len(out_shape),
        out_shape=out_shape,
        compiler_params=_cparams(),
    )(*grads, small_grad, *flat_wmv, *flat_small)


def _small_rows(ln_g, ln_b, q_norm_g, kv_norm_g, extra=None):
    pad = lambda a: jnp.pad(a, (0, D_MODEL - a.shape[0]))
    rows = [ln_g, ln_b, pad(q_norm_g), pad(kv_norm_g)] + ([] if extra is None else [extra])
    return jnp.pad(jnp.stack(rows), ((0, 8 - len(rows)), (0, 0)))


def kernel(x, w_in, q_norm_g, kv_norm_g, w_uq, w_ukv, w_out, ln_g, ln_b, loss_target, m_w_in, m_q_norm_g, m_kv_norm_g, m_w_uq, m_w_ukv, m_w_out, m_ln_g, m_ln_b, v_w_in, v_q_norm_g, v_kv_norm_g, v_w_uq, v_w_ukv, v_w_out, v_ln_g, v_ln_b):
    w_in_r, w_uq_r, w_ukv_r = _all_gather_weights([w_in, w_uq, w_ukv])
    grad_x, sums = _local_step(
        x[0], loss_target[0], w_in_r, w_uq_r, w_ukv_r, _gather_w_out_rider(w_out), _scatter_g_out_rider,
        _reduce_grads_rider, q_norm_g, kv_norm_g, ln_g, ln_b)
    row = lambda a: a.reshape(1, -1)
    small_wmv = [(row(ln_g), row(m_ln_g), row(v_ln_g)), (row(ln_b), row(m_ln_b), row(v_ln_b)),
                 (row(q_norm_g), row(m_q_norm_g), row(v_q_norm_g)), (row(kv_norm_g), row(m_kv_norm_g), row(v_kv_norm_g))]
    wmv = [(w_in, m_w_in, v_w_in), (w_uq, m_w_uq, v_w_uq), (w_ukv, m_w_ukv, v_w_ukv), (w_out, m_w_out, v_w_out)]
    res = _adamw_update(sums[:4], sums[4], wmv, small_wmv)
    big = [res[4 * t:4 * t + 4] for t in range(4)]
    small = [[a.reshape(-1) for a in res[16 + 4 * t:20 + 4 * t]] for t in range(4)]
    loss = res[32].reshape(())

    def group(kind):
        return (big[0][kind], small[2][kind], small[3][kind], big[1][kind], big[2][kind], big[3][kind],
                small[0][kind], small[1][kind])

    return (loss, grad_x[None], *group(0), *group(1), *group(2), *group(3))
```

```python
import functools
from typing import Callable, NamedTuple

import numpy as np
import jax
import jax.numpy as jnp
from jax import lax
from jax.experimental import pallas as pl
from jax.experimental.pallas import tpu as pltpu

F32 = jnp.float32
BF16 = jnp.bfloat16

D_MODEL = 1024
ROPE_THETA = 500000.0
NEG = -1e30
RMS_EPS = 1e-6
LN_EPS = 1e-5
HEADS = 8
MLA_NOPE = 64
MLA_ROPE = 32
MLA_V = 64
Q_RANK = 384
KV_RANK = 256
DIL_HEAD = 64
DIL_ROT = 16
DIL_CONFIGS = ((128, 1), (512, 4), (2048, 16))
DIL_NEAR = 512
HW = HEADS * 64
QW = HW + HEADS * MLA_ROPE
IN_SPLITS = (Q_RANK, KV_RANK, MLA_ROPE, HW, HW, HW, HW, HW)
IN_WIDTH = sum(IN_SPLITS)
ALPHA = 2.0 ** 0.25
MLA_SCALE = (MLA_NOPE + MLA_ROPE) ** -0.5
DIL_SCALE = DIL_HEAD ** -0.5
LOG2E = 1.4426950408889634
LN2 = 0.6931471805599453

ADAM_LR = 0.001
ADAM_B1 = 0.9
ADAM_B2 = 0.999
ADAM_EPS = 1e-08
ADAM_WD = 0.01
ADAM_STEP = 10

N_DEV = 8
LANES = 128
VMEM_LIMIT = 56 * 1024 * 1024
BLOCK_TOKENS = 512
BLOCK_MLA = 512
BLOCK_DIL = 512

C_CQ, C_CKV, C_KR, C_GA, C_QB, C_KB, C_VB, C_GB, C_END = 0, 384, 640, 768, 1280, 1792, 2304, 2816, 3328

NT = (((1,), (1,)), ((), ()))
TN = (((0,), (0,)), ((), ()))


def _pcall(body, **kw):
    return pl.pallas_call(body, **kw)


def _cparams(**kw):
    return pltpu.CompilerParams(vmem_limit_bytes=VMEM_LIMIT, **kw)


def _rope_tables(seq):
    def tabs(dim, period):
        half = dim // 2
        inv = np.float32(ROPE_THETA) ** (-np.arange(0, dim, 2, dtype=np.float32) / np.float32(dim))
        ang = np.arange(seq, dtype=np.float32)[:, None] * inv.astype(np.float32)[None, :]
        cos, sin = np.cos(ang).astype(np.float32), np.sin(ang).astype(np.float32)
        j = np.arange(LANES) % period
        f = j % half
        c = np.where(j < dim, cos[:, f], np.float32(1.0))
        s1 = np.where(j < half, -sin[:, f], np.float32(0.0))
        s2 = np.where((j >= half) & (j < dim), sin[:, f], np.float32(0.0))
        return [c, s1, s2]
    return np.stack(tabs(MLA_ROPE, MLA_ROPE) + tabs(DIL_ROT, DIL_HEAD)).astype(np.float32)


def _rope(t, c, s1, s2, half):
    return t * c + pltpu.roll(t, LANES - half, 1) * s1 + pltpu.roll(t, half, 1) * s2


def _rope_t(d, c, s1, s2, half):
    return d * c + pltpu.roll(d * s1, half, 1) + pltpu.roll(d * s2, LANES - half, 1)


def _rope_wide(fn, t, c, s1, s2, half):
    return jnp.concatenate(
        [fn(t[:, i:i + LANES], c, s1, s2, half) for i in range(0, t.shape[1], LANES)], axis=1)


def _mla_bias_t(blk):
    a = np.arange(blk)
    causal = np.where(a[:, None] <= a[None, :], 0.0, NEG)
    return np.stack([np.zeros((blk, blk)), causal]).astype(np.float32)


def _dil_bias_t(blk, reach):
    a = np.arange(blk)
    out = []
    for off in range(-(-reach // blk) + 1):
        delta = blk * off + a[None, :] - a[:, None]
        mult = np.zeros((blk, blk))
        for window, dil in DIL_CONFIGS:
            mult += (delta >= 0) & (delta % dil == 0) & (delta <= min(window, reach))
        out.append(np.where(mult > 0, np.log2(np.maximum(mult, 1.0)), NEG))
    return np.stack(out).astype(np.float32)


def _dil_far_bias_t(length):
    window, dil = DIL_CONFIGS[-1]
    a = np.arange(length)
    steps_back = a[None, :] - a[:, None]
    seen = (steps_back * dil > DIL_NEAR) & (steps_back * dil <= window)
    return np.where(seen, 0.0, NEG).astype(np.float32)[None]


def _lanes_to_classes(a, dil):
    h, s = a.shape
    return a.reshape(h, s // dil, dil).transpose(0, 2, 1).reshape(h, s)


def _lanes_from_classes(a, dil):
    h, s = a.shape
    return a.reshape(h, dil, s // dil).transpose(0, 2, 1).reshape(h, s)


def _steps(nq, span, by_key, diag_only_bias):
    rows = []
    if by_key:
        for ki in range(nq):
            hi = min(nq - 1, ki + span)
            for qi in range(ki, hi + 1):
                rows.append((qi, ki, int(qi == ki), int(qi == hi)))
    else:
        for qi in range(nq):
            lo = max(0, qi - span)
            for ki in range(lo, qi + 1):
                rows.append((qi, ki, int(ki == lo), int(ki == qi)))
    arr = np.array(rows, dtype=np.int32)
    off = arr[:, 0] - arr[:, 1]
    bias_idx = (off == 0).astype(np.int32) if diag_only_bias else off.astype(np.int32)
    return [jnp.asarray(v) for v in (arr[:, 0], arr[:, 1], bias_idx, arr[:, 2], arr[:, 3])]


def _by_class(val, out_ref, lanes_sc):
    n_cls, per = out_ref.shape[0], out_ref.shape[1]
    for c in range(val.shape[1] // LANES):
        lanes_sc[c] = val[:, LANES * c:LANES * (c + 1)]
        for r in range(n_cls):
            rows = lanes_sc.at[c][pl.ds(r, per, stride=n_cls), :]
            out_ref[r, :, LANES * c:LANES * (c + 1)] = rows.astype(out_ref.dtype)


def _in_sequence(ref, lanes_sc):
    n_cls, per, width = ref.shape
    for c in range(width // LANES):
        for r in range(n_cls):
            lanes_sc.at[c][pl.ds(r, per, stride=n_cls), :] = ref[r, :, LANES * c:LANES * (c + 1)].astype(F32)
    return jnp.concatenate([lanes_sc[c] for c in range(width // LANES)], axis=1)


def _fwd_proj(x, w_in_r, w_uq_r, w_ukv_r, qg, kvg, tabs, bt, n_cls):
    seq = x.shape[0]

    def body(x_ref, win_ref, wuq_ref, wukv_ref, qg_ref, kvg_ref, tab_ref,
             cq_ref, ckv_ref, qn_ref, kvn_ref, qcat_ref, kn_ref, kpe_ref, v_ref,
             ga_ref, gb_ref, qb_ref, kb_ref, vb_ref, knt_ref, kpet_ref, vt_ref, kbt_ref, vbt_ref,
             qbc_ref, kbc_ref, vbc_ref, lanes_sc):
        xb = x_ref[...].astype(BF16)

        def proj(lo, hi):
            return jnp.dot(xb, win_ref[:, lo:hi], preferred_element_type=F32)

        m_tabs = (tab_ref[0], tab_ref[1], tab_ref[2])
        d_tabs = (tab_ref[3], tab_ref[4], tab_ref[5])

        def use_cq(cq):
            cq_ref[...] = cq
            qn = (cq * lax.rsqrt(jnp.mean(cq * cq, axis=1, keepdims=True) + RMS_EPS) * qg_ref[...]).astype(BF16)
            qn_ref[...] = qn
            q = jnp.dot(qn, wuq_ref[...], preferred_element_type=F32)
            qcat_ref[:, :HW] = (q[:, :HW] * (MLA_SCALE * LOG2E)).astype(BF16)
            qcat_ref[:, HW:] = (
                _rope_wide(_rope, q[:, HW:], *m_tabs, MLA_ROPE // 2) * (MLA_SCALE * LOG2E)).astype(BF16)

        def use_ckv(ckv):
            ckv_ref[...] = ckv
            kvn = (ckv * lax.rsqrt(jnp.mean(ckv * ckv, axis=1, keepdims=True) + RMS_EPS) * kvg_ref[...]).astype(BF16)
            kvn_ref[...] = kvn
            kv = jnp.dot(kvn, wukv_ref[...], preferred_element_type=F32)
            kn_ref[...] = kv[:, :HW].astype(BF16)
            v_ref[...] = kv[:, HW:].astype(BF16)
            knt_ref[...] = kv[:, :HW].T.astype(BF16)
            vt_ref[...] = kv[:, HW:].T.astype(BF16)

        def use_kr(kr):
            kpe = _rope(kr, *m_tabs, MLA_ROPE // 2)
            kpe_ref[...] = kpe.astype(BF16)
            kpet_ref[...] = kpe.T[:MLA_ROPE, :].astype(BF16)

        def use_ga(ga):
            ga_ref[...] = ga

        def use_qb(qb):
            qb = _rope_wide(_rope, qb, *d_tabs, DIL_ROT // 2) * (DIL_SCALE * LOG2E)
            qb_ref[...] = qb.astype(BF16)
            _by_class(qb, qbc_ref, lanes_sc)

        def use_kb(kb):
            kb = _rope_wide(_rope, kb, *d_tabs, DIL_ROT // 2)
            kb_ref[...] = kb.astype(BF16)
            kbt_ref[...] = kb.T.astype(BF16)
            _by_class(kb, kbc_ref, lanes_sc)

        def use_vb(vb):
            vb_ref[...] = vb.astype(BF16)
            vbt_ref[...] = vb.T.astype(BF16)
            _by_class(vb, vbc_ref, lanes_sc)

        def use_gb(gb):
            gb_ref[...] = gb

        pieces = [(C_CQ, C_CKV, use_cq), (C_CKV, C_KR, use_ckv), (C_KR, C_GA, use_kr), (C_GA, C_QB, use_ga),
                  (C_QB, C_KB, use_qb), (C_KB, C_VB, use_kb), (C_VB, C_GB, use_vb), (C_GB, C_END, use_gb)]
        ahead = proj(*pieces[0][:2])
        for n, (_, _, use) in enumerate(pieces):
            cur = ahead
            if n + 1 < len(pieces):
                ahead = proj(*pieces[n + 1][:2])
            use(cur)

    def tok(width):
        return pl.BlockSpec((bt, width), lambda i: (i, 0))

    def tok_t(height):
        return pl.BlockSpec((height, bt), lambda i: (0, i))

    def full(a):
        return pl.BlockSpec(a.shape, lambda i: (0,) * a.ndim)

    outs = [(Q_RANK, F32), (KV_RANK, F32), (Q_RANK, BF16), (KV_RANK, BF16), (QW, BF16), (HW, BF16),
            (LANES, BF16), (HW, BF16), (HW, F32), (HW, F32), (HW, BF16), (HW, BF16), (HW, BF16)]
    outs_t = [HW, MLA_ROPE, HW, HW, HW]
    by_class = pl.BlockSpec((n_cls, bt // n_cls, HW), lambda i: (0, i, 0))
    return _pcall(
        body, name="fwd_proj", grid=(seq // bt,),
        in_specs=[tok(D_MODEL), full(w_in_r), full(w_uq_r), full(w_ukv_r), full(qg), full(kvg),
                  pl.BlockSpec((6, bt, LANES), lambda i: (0, i, 0))],
        out_specs=[tok(w) for w, _ in outs] + [tok_t(h) for h in outs_t] + [by_class] * 3,
        out_shape=[jax.ShapeDtypeStruct((seq, w), dt) for w, dt in outs]
        + [jax.ShapeDtypeStruct((h, seq), BF16) for h in outs_t]
        + [jax.ShapeDtypeStruct((n_cls, seq // n_cls, HW), BF16)] * 3,
        scratch_shapes=[pltpu.VMEM((HW // LANES, bt, LANES), F32)],
        compiler_params=_cparams(dimension_semantics=("arbitrary",)),
    )(x, w_in_r, w_uq_r, w_ukv_r, qg, kvg, tabs)


def _head_masks(lane, h):
    e, g = h % 2, h % 4
    me = (lane >= 64 * e) & (lane < 64 * e + 64)
    mr = (lane >= 32 * g) & (lane < 32 * g + 32)
    return me, mr


def _masked(mask, a):
    return jnp.where(mask, a, jnp.zeros_like(a))


def _pair_operands(q_ref, k_ref, kpe_ref, lane, j, ks=slice(None), qs=slice(None)):
    cols = slice(LANES * j, LANES * (j + 1))
    qc = q_ref[qs, cols]
    kj = k_ref[ks, cols]
    kes = []
    for h in (2 * j, 2 * j + 1):
        me, mr = _head_masks(lane, h)
        ke = _masked(me, kj)
        if kpe_ref is not None:
            ke = jnp.concatenate([ke, _masked(mr, kpe_ref[ks, :])], axis=1)
        kes.append(ke)
    if kpe_ref is not None:
        qc = jnp.concatenate([qc, q_ref[qs, HW + LANES * (j // 2):HW + LANES * (j // 2 + 1)]], axis=1)
    return qc, kes


def _tile_variants(bias_t):
    out = {}
    for i, tile in enumerate(np.asarray(bias_t)):
        h = tile.shape[0] // 2
        skip = 1 if (tile[h:, :h] == NEG).all() else 2 if (tile[:h, h:] == NEG).all() else 0
        out[i] = (bool((tile != 0).any()), skip)
    return out


def _tile_parts(blk, skip):
    lo, hi, full = slice(0, blk // 2), slice(blk // 2, blk), slice(0, blk)
    return {0: [(full, full)], 1: [(lo, full), (hi, hi)], 2: [(hi, full), (lo, lo)]}[skip]


class Rider(NamedTuple):
    args: list
    in_specs: list
    out_shape: list
    out_specs: list
    scratch: list
    start: Callable
    finish: Callable
    stages: tuple = ()


def _ride_along(body, ride, n_prefetch, n_in, n_out, n_scratch, n_steps):
    if ride is None:
        return body

    def wrapped(*refs):
        pre, rest = refs[:n_prefetch], refs[n_prefetch:]
        a = n_in
        b = a + len(ride.args)
        c = b + n_out
        d = c + len(ride.out_shape)
        e = d + n_scratch
        mine = (rest[a:b], rest[c:d], rest[e:])
        t = pl.program_id(0)
        pl.when(t == 0)(lambda: ride.start(*mine))
        for at, stage in ride.stages:
            pl.when(t == at)(functools.partial(stage, *mine))
        body(*pre, *rest[:a], *rest[b:c], *rest[d:e])
        pl.when(t == n_steps - 1)(lambda: ride.finish(*mine))

    return wrapped


def _attn_fwd(name, q, k, kpe, vt, bias_t, steps, blk, ride=None, v_token_major=False):
    seq = q.shape[0]
    mla = kpe is not None
    n_steps = int(steps[0].shape[0])
    variants = _tile_variants(bias_t)

    def body(qi_r, ki_r, bi_r, fi_r, la_r, *refs):
        if mla:
            q_ref, k_ref, kpe_ref, vt_ref, b_ref, o_ref, lse_ref, m_sc, l_sc, acc_sc, st_sc = refs
        else:
            q_ref, k_ref, vt_ref, b_ref, o_ref, lse_ref, m_sc, l_sc, acc_sc, st_sc = refs
        t = pl.program_id(0)

        @pl.when(fi_r[t] == 1)
        def _():
            m_sc[...] = jnp.full(m_sc.shape, NEG, F32)
            l_sc[...] = jnp.zeros(l_sc.shape, F32)
            acc_sc[...] = jnp.zeros(acc_sc.shape, F32)

        lane = lax.broadcasted_iota(jnp.int32, (1, LANES), 1)
        if v_token_major:
            vt_all = vt_ref[...].astype(F32).T.astype(BF16)
            vt_rows = lambda rows, ks: vt_all[rows, ks]
        else:
            vt_rows = lambda rows, ks: vt_ref[rows, ks]

        def tile_pass(ks, qs, with_bias):
            nk, nq = ks.stop - ks.start, qs.stop - qs.start
            ones = jnp.ones((16, nk), BF16)

            def pair_scores(j):
                qc, kes = _pair_operands(q_ref, k_ref, kpe_ref if mla else None, lane, j, ks, qs)
                st = lax.dot_general(jnp.concatenate(kes, axis=0), qc, NT, preferred_element_type=F32)
                maxes = []
                for e in range(2):
                    se = st[e * nk:(e + 1) * nk]
                    if with_bias:
                        se = se + b_ref[0, ks, qs]
                    st_sc[j % 2, e * nk:(e + 1) * nk, 0:nq] = se
                    maxes.append(jnp.max(se, axis=0, keepdims=True))
                return maxes

            def softmax_pv(h, col_max):
                st = st_sc[(h // 2) % 2, (h % 2) * nk:(h % 2 + 1) * nk, 0:nq]
                hrow = slice(h, h + 1)
                m_prev = m_sc[hrow, qs]
                m_new = jnp.maximum(m_prev, col_max)
                alpha = jnp.exp2(m_prev - m_new)
                pt = jnp.exp2(st - m_new).astype(BF16)
                m_sc[hrow, qs] = m_new
                rows = slice(64 * h, 64 * h + 64)
                res = jnp.dot(jnp.concatenate([vt_rows(rows, ks), ones], axis=0), pt, preferred_element_type=F32)
                acc_sc[rows, qs] = alpha * acc_sc[rows, qs] + res[:64]
                l_sc[hrow, qs] = alpha * l_sc[hrow, qs] + res[64:65]

            maxes = pair_scores(0)
            for j in range(HEADS // 2):
                cur = maxes
                if j + 1 < HEADS // 2:
                    maxes = pair_scores(j + 1)
                softmax_pv(2 * j, cur[0])
                softmax_pv(2 * j + 1, cur[1])

        def step(with_bias, skip):
            for ks, qs in _tile_parts(blk, skip):
                tile_pass(ks, qs, with_bias)

        for idx, (with_bias, skip) in variants.items():
            if len(variants) == 1:
                step(with_bias, skip)
            else:
                pl.when(bi_r[t] == idx)(functools.partial(step, with_bias, skip))

        @pl.when(la_r[t] == 1)
        def _():
            for h in range(HEADS):
                rows = slice(64 * h, 64 * h + 64)
                acc_sc[rows, :] = acc_sc[rows, :] / l_sc[h:h + 1, :]
            o_ref[...] = acc_sc[...].T
            lse_ref[...] = m_sc[...] + jnp.log2(l_sc[...])

    qmap = lambda t, qi, ki, bi, fi, la: (qi[t], 0)
    kmap = lambda t, qi, ki, bi, fi, la: (ki[t], 0)
    in_specs = [pl.BlockSpec((blk, q.shape[1]), qmap), pl.BlockSpec((blk, HW), kmap)]
    args = [q, k]
    if mla:
        in_specs.append(pl.BlockSpec((blk, LANES), kmap))
        args.append(kpe)
    in_specs += [pl.BlockSpec((blk, HW), kmap) if v_token_major else
                 pl.BlockSpec((HW, blk), lambda t, qi, ki, bi, fi, la: (0, ki[t])),
                 pl.BlockSpec((1, blk, blk), lambda t, qi, ki, bi, fi, la: (bi[t], 0, 0))]
    args += [vt, jnp.asarray(bias_t)]
    out_specs = [pl.BlockSpec((blk, HW), qmap), pl.BlockSpec((HEADS, blk), lambda t, qi, ki, bi, fi, la: (0, qi[t]))]
    out_shape = [jax.ShapeDtypeStruct((seq, HW), F32), jax.ShapeDtypeStruct((HEADS, seq), F32)]
    scratch = [pltpu.VMEM((HEADS, blk), F32), pltpu.VMEM((HEADS, blk), F32),
               pltpu.VMEM((HW, blk), F32), pltpu.VMEM((2, 2 * blk, blk), F32)]
    body = _ride_along(body, ride, 5, len(args), len(out_shape), len(scratch), n_steps)
    if ride is not None:
        args, in_specs = args + ride.args, in_specs + ride.in_specs
        out_specs, out_shape, scratch = out_specs + ride.out_specs, out_shape + ride.out_shape, scratch + ride.scratch
    return _pcall(
        body, name=name,
        grid_spec=pltpu.PrefetchScalarGridSpec(
            num_scalar_prefetch=5, grid=(n_steps,), in_specs=in_specs, out_specs=out_specs, scratch_shapes=scratch),
        out_shape=out_shape,
        compiler_params=_cparams(dimension_semantics=("arbitrary",)),
    )(*steps, *args)


def _attn_bwd(name, q, k, kpe, v, kt, kpet, bias_t, do, lse, dstat, steps, blk, ride=None, single_visit=False):
    assert not (single_visit and kpe is not None) and (kt is not None or single_visit)
    seq = q.shape[0]
    mla = kpe is not None
    qw = q.shape[1]
    n_steps = int(steps[0].shape[0])
    dk_dtype = BF16 if mla else F32
    variants = _tile_variants(bias_t)

    def body(qi_r, ki_r, bi_r, fi_r, la_r, *refs):
        if mla:
            (q_ref, k_ref, kpe_ref, v_ref, kt_ref, kpet_ref, b_ref, do_ref, lse_ref, d_ref,
             dq_ref, dk_ref, dkpe_ref, dv_ref, dk_sc, dkpe_sc, dv_sc, st_sc, dpt_sc) = refs
        else:
            q_ref, k_ref, v_ref, *rest = refs
            kt_ref = rest.pop(0) if kt is not None else None
            b_ref, do_ref, lse_ref, d_ref, dq_out_ref, dk_ref, dv_ref, dk_sc, dv_sc, st_sc, dpt_sc, *rest = rest
            dq_ref = rest[0] if single_visit else dq_out_ref
        t = pl.program_id(0)

        @pl.when(jnp.logical_or(t == 0, single_visit))
        def _():
            dq_ref[...] = jnp.zeros(dq_ref.shape, F32)

        @pl.when(fi_r[t] == 1)
        def _():
            dk_sc[...] = jnp.zeros(dk_sc.shape, F32)
            dv_sc[...] = jnp.zeros(dv_sc.shape, F32)
            if mla:
                dkpe_sc[...] = jnp.zeros(dkpe_sc.shape, F32)

        qi = 0 if single_visit else qi_r[t]
        lane = lax.broadcasted_iota(jnp.int32, (1, LANES), 1)
        if kt is None:
            kt_all = k_ref[...].astype(F32).T.astype(BF16)
            kt_rows = lambda rows, ks: kt_all[rows, ks]
        else:
            kt_rows = lambda rows, ks: kt_ref[rows, ks]

        def tile_pass(ks, qs, with_bias):
            nk, nq = ks.stop - ks.start, qs.stop - qs.start

            def pair_matmuls(j):
                cols = slice(LANES * j, LANES * (j + 1))
                qc, kes = _pair_operands(q_ref, k_ref, kpe_ref if mla else None, lane, j, ks, qs)
                st_sc[j % 2, 0:2 * nk, 0:nq] = lax.dot_general(
                    jnp.concatenate(kes, axis=0), qc, NT, preferred_element_type=F32)
                vj = v_ref[ks, cols]
                ves = [_masked(_head_masks(lane, h)[0], vj) for h in (2 * j, 2 * j + 1)]
                dpt_sc[j % 2, 0:2 * nk, 0:nq] = lax.dot_general(
                    jnp.concatenate(ves, axis=0), do_ref[qs, cols], NT, preferred_element_type=F32)

            def pair_grads(j):
                cols = slice(LANES * j, LANES * (j + 1))
                qj, doj = q_ref[qs, cols], do_ref[qs, cols]
                if mla:
                    qr = q_ref[qs, HW + LANES * (j // 2):HW + LANES * (j // 2 + 1)]
                pts, dsts, qms, doms = [], [], [], []
                for e in range(2):
                    h = 2 * j + e
                    me, mr = _head_masks(lane, h)
                    st = st_sc[j % 2, e * nk:(e + 1) * nk, 0:nq]
                    if with_bias:
                        st = st + b_ref[0, ks, qs]
                    pt = jnp.exp2(st - lse_ref[h:h + 1, qs])
                    dst = (pt * (dpt_sc[j % 2, e * nk:(e + 1) * nk, 0:nq] - d_ref[h:h + 1, qs])).astype(BF16)
                    pts.append(pt.astype(BF16))
                    dsts.append(dst)
                    doms.append(_masked(me, doj))
                    qm = _masked(me, qj)
                    if mla:
                        qm = jnp.concatenate([qm, _masked(mr, qr)], axis=1)
                    qms.append(qm)
                    ktl = kt_rows(slice(64 * h, 64 * h + 64), ks)
                    if mla:
                        ktl = jnp.concatenate([ktl, kpet_ref[:, ks]], axis=0)
                    dqc = jnp.dot(ktl, dst, preferred_element_type=F32)
                    dq_ref[qi, 64 * h:64 * h + 64, qs] += dqc[:64]
                    if mla:
                        dq_ref[qi, HW + MLA_ROPE * h:HW + MLA_ROPE * (h + 1), qs] += dqc[64:]
                dv_sc[ks, cols] += jnp.dot(
                    jnp.concatenate(pts, axis=1), jnp.concatenate(doms, axis=0), preferred_element_type=F32)
                dkc = jnp.dot(jnp.concatenate(dsts, axis=1), jnp.concatenate(qms, axis=0), preferred_element_type=F32)
                dk_sc[ks, cols] += dkc[:, :LANES]
                if mla:
                    dkpe_sc[ks, :] += dkc[:, LANES:]

            pair_matmuls(0)
            for j in range(HEADS // 2):
                if j + 1 < HEADS // 2:
                    pair_matmuls(j + 1)
                pair_grads(j)

        def step(with_bias, skip):
            for ks, qs in _tile_parts(blk, skip):
                tile_pass(ks, qs, with_bias)

        for idx, (with_bias, skip) in variants.items():
            if len(variants) == 1:
                step(with_bias, skip)
            else:
                pl.when(bi_r[t] == idx)(functools.partial(step, with_bias, skip))
        if single_visit:
            dq_out_ref[...] = dq_ref[0].T

        @pl.when(la_r[t] == 1)
        def _():
            dk_ref[...] = (dk_sc[...] * LN2).astype(dk_ref.dtype)
            dv_ref[...] = dv_sc[...].astype(dv_ref.dtype)
            if mla:
                dkpe_ref[...] = dkpe_sc[...] * LN2

    qmap = lambda t, qi, ki, bi, fi, la: (qi[t], 0)
    kmap = lambda t, qi, ki, bi, fi, la: (ki[t], 0)
    qmap_t = lambda t, qi, ki, bi, fi, la: (0, qi[t])
    kmap_t = lambda t, qi, ki, bi, fi, la: (0, ki[t])
    in_specs = [pl.BlockSpec((blk, qw), qmap), pl.BlockSpec((blk, HW), kmap)]
    args = [q, k]
    if mla:
        in_specs.append(pl.BlockSpec((blk, LANES), kmap))
        args.append(kpe)
    in_specs.append(pl.BlockSpec((blk, HW), kmap))
    args.append(v)
    if kt is not None:
        in_specs.append(pl.BlockSpec((HW, blk), kmap_t))
        args.append(kt)
    if mla:
        in_specs.append(pl.BlockSpec((MLA_ROPE, blk), kmap_t))
        args.append(kpet)
    in_specs += [pl.BlockSpec((1, blk, blk), lambda t, qi, ki, bi, fi, la: (bi[t], 0, 0)),
                 pl.BlockSpec((blk, HW), qmap), pl.BlockSpec((HEADS, blk), qmap_t), pl.BlockSpec((HEADS, blk), qmap_t)]
    args += [jnp.asarray(bias_t), do, lse, dstat]
    dq_shape = (seq // blk, qw, blk)
    if single_visit:
        out_specs, out_shape = [pl.BlockSpec((blk, qw), qmap)], [jax.ShapeDtypeStruct((seq, qw), F32)]
    else:
        out_specs = [pl.BlockSpec(dq_shape, lambda t, qi, ki, bi, fi, la: (0, 0, 0))]
        out_shape = [jax.ShapeDtypeStruct(dq_shape, F32)]
    out_specs.append(pl.BlockSpec((blk, HW), kmap))
    out_shape.append(jax.ShapeDtypeStruct((seq, HW), dk_dtype))
    scratch = [pltpu.VMEM((blk, HW), F32)]
    if mla:
        out_specs.append(pl.BlockSpec((blk, LANES), kmap))
        out_shape.append(jax.ShapeDtypeStruct((seq, LANES), F32))
        scratch.append(pltpu.VMEM((blk, LANES), F32))
    out_specs.append(pl.BlockSpec((blk, HW), kmap))
    out_shape.append(jax.ShapeDtypeStruct((seq, HW), BF16))
    scratch.append(pltpu.VMEM((blk, HW), F32))
    scratch += [pltpu.VMEM((2, 2 * blk, blk), F32), pltpu.VMEM((2, 2 * blk, blk), F32)]
    if single_visit:
        scratch.append(pltpu.VMEM((1, qw, blk), F32))
    body = _ride_along(body, ride, 5, len(args), len(out_shape), len(scratch), n_steps)
    if ride is not None:
        args, in_specs = args + ride.args, in_specs + ride.in_specs
        out_specs, out_shape, scratch = out_specs + ride.out_specs, out_shape + ride.out_shape, scratch + ride.scratch
    return _pcall(
        body, name=name,
        grid_spec=pltpu.PrefetchScalarGridSpec(
            num_scalar_prefetch=5, grid=(n_steps,), in_specs=in_specs, out_specs=out_specs,
            scratch_shapes=scratch),
        out_shape=out_shape,
        compiler_params=_cparams(dimension_semantics=("arbitrary",)),
    )(*steps, *args)


def _out_ln(oa, ob_near, ob_far, lse_near, lse_far, ga, gb, x, tgt, w_out, ln_g, ln_b, bt):
    seq = x.shape[0]

    def body(oa_ref, obn_ref, obf_ref, lsen_ref, lsef_ref, ga_ref, gb_ref, x_ref, tgt_ref, w_ref, g_ref, b_ref,
             dz_ref, doa_ref, dob_ref, dga_ref, dgb_ref, da_ref, db_ref, lse_ref, gwb_ref, small_ref, dobc_ref,
             gw_ref, lanes_sc):
        i = pl.program_id(0)

        @pl.when(i == 0)
        def _():
            gw_ref[...] = jnp.zeros(gw_ref.shape, F32)
            small_ref[...] = jnp.zeros(small_ref.shape, F32)

        def gate(g):
            sig = 0.5 * jnp.tanh(0.5 * g) + 0.5
            return g * sig, sig * (1.0 + g * (1.0 - sig))

        lse_n, lse_f = lsen_ref[...], lsef_ref[...]
        top = jnp.maximum(lse_n, lse_f)
        e_n, e_f = jnp.exp2(lse_n - top), jnp.exp2(lse_f - top)
        lse_ref[...] = top + jnp.log2(e_n + e_f)
        inv = 1.0 / (e_n + e_f)
        head_row = lax.broadcasted_iota(jnp.int32, (2 * HEADS, HW), 0) % HEADS
        spread = (head_row == lax.broadcasted_iota(jnp.int32, (2 * HEADS, HW), 1) // 64).astype(BF16)

        def per_lane(w):
            hi = w.astype(BF16)
            lo = (w - hi.astype(F32)).astype(BF16)
            return lax.dot_general(jnp.concatenate([hi, lo], axis=0), spread, TN, preferred_element_type=F32)

        o_a = oa_ref[...]
        o_b = per_lane(e_n * inv) * obn_ref[...] + per_lane(e_f * inv) * _in_sequence(obf_ref, lanes_sc)
        g_a, g_b = ga_ref[...], gb_ref[...]
        sa, dsa = gate(g_a)
        sb, dsb = gate(g_b)
        mix = jnp.concatenate([o_a * sa, o_b * sb], axis=1).astype(BF16)
        z = ALPHA * x_ref[...] + jnp.dot(mix, w_ref[...], preferred_element_type=F32)
        mu = jnp.mean(z, axis=1, keepdims=True)
        zc = z - mu
        rstd = lax.rsqrt(jnp.mean(zc * zc, axis=1, keepdims=True) + LN_EPS)
        xhat = zc * rstd
        gam = g_ref[...]
        diff = xhat * gam + b_ref[...] - tgt_ref[...]
        dy = diff * (1.0 / D_MODEL)
        small_ref[0:1, :] += jnp.sum(dy * xhat, axis=0, keepdims=True)
        small_ref[1:2, :] += jnp.sum(dy, axis=0, keepdims=True)
        small_ref[2:3, :] += jnp.sum(diff * diff, axis=0, keepdims=True)
        dxh = dy * gam
        dz = rstd * (dxh - jnp.mean(dxh, axis=1, keepdims=True) - xhat * jnp.mean(dxh * xhat, axis=1, keepdims=True))
        dz_ref[...] = dz
        dzb = dz.astype(BF16)
        gw_ref[...] += lax.dot_general(mix, dzb, TN, preferred_element_type=F32)

        @pl.when(i == seq // bt - 1)
        def _():
            gwb_ref[...] = gw_ref[...].astype(BF16)

        dmix = lax.dot_general(dzb, w_ref[...], NT, preferred_element_type=F32)
        doa, dob = dmix[:, :HW] * sa, dmix[:, HW:] * sb
        doa_ref[...] = doa.astype(BF16)
        dob_ref[...] = dob.astype(BF16)
        _by_class(dob, dobc_ref, lanes_sc)
        dga_ref[...] = (dmix[:, :HW] * o_a * dsa).astype(BF16)
        dgb_ref[...] = (dmix[:, HW:] * o_b * dsb).astype(BF16)
        head_of = (lax.broadcasted_iota(jnp.int32, (2 * HW, LANES), 0) % HW) // 64
        ind = (head_of == lax.broadcasted_iota(jnp.int32, (2 * HW, LANES), 1)).astype(BF16)

        def head_sums(prod):
            hi = prod.astype(BF16)
            lo = (prod - hi.astype(F32)).astype(BF16)
            sums = jnp.dot(jnp.concatenate([hi, lo], axis=1), ind, preferred_element_type=F32)
            return sums.T[:HEADS, :]

        da_ref[...] = head_sums(doa * o_a)
        db_ref[...] = head_sums(dob * o_b)

    def tok(width):
        return pl.BlockSpec((bt, width), lambda i: (i, 0))

    def full(shape):
        return pl.BlockSpec(shape, lambda i: (0,) * len(shape))

    stat = pl.BlockSpec((HEADS, bt), lambda i: (0, i))
    n_cls = ob_far.shape[0]
    by_class = pl.BlockSpec((n_cls, bt // n_cls, HW), lambda i: (0, i, 0))
    return _pcall(
        body, name="out_ln", grid=(seq // bt,),
        in_specs=[tok(HW), tok(HW), by_class, stat, stat, tok(HW), tok(HW), tok(D_MODEL), tok(D_MODEL),
                  full((D_MODEL, D_MODEL)), full((1, D_MODEL)), full((1, D_MODEL))],
        out_specs=[tok(D_MODEL), tok(HW), tok(HW), tok(HW), tok(HW), stat, stat, stat,
                   full((D_MODEL, D_MODEL)), full((8, D_MODEL)), by_class],
        out_shape=[jax.ShapeDtypeStruct((seq, D_MODEL), F32)] + [jax.ShapeDtypeStruct((seq, HW), BF16)] * 4
        + [jax.ShapeDtypeStruct((HEADS, seq), F32)] * 3
        + [jax.ShapeDtypeStruct((D_MODEL, D_MODEL), BF16), jax.ShapeDtypeStruct((8, D_MODEL), F32),
           jax.ShapeDtypeStruct(ob_far.shape, BF16)],
        scratch_shapes=[pltpu.VMEM((D_MODEL, D_MODEL), F32), pltpu.VMEM((HW // LANES, bt, LANES), F32)],
        compiler_params=_cparams(dimension_semantics=("arbitrary",)),
    )(oa, ob_near, ob_far, lse_near, lse_far, ga, gb, x, tgt, w_out, ln_g, ln_b)


def _bwd_mid(dq_m, dkn, dv, dkpe, dqb, dkb, dvb, far, dga, dgb, cq, ckv, qn, kvn, w_uq_r, w_ukv_r, qg, kvg, tabs, bt):
    n_cls = far[0].shape[0]
    seq = cq.shape[0]

    def body(dqm_ref, dkn_ref, dv_ref, dkpe_ref, dqb_ref, dkb_ref, dvb_ref, dqf_ref, dkf_ref, dvf_ref, dga_ref, dgb_ref,
             cq_ref, ckv_ref, qn_ref, kvn_ref, wuq_ref, wukv_ref, qg_ref, kvg_ref, tab_ref,
             dh_ref, guq3_ref, gukv3_ref, small_ref, seq_sc, guq_ref, gukv_ref):
        i = pl.program_id(0)

        @pl.when(i == 0)
        def _():
            guq_ref[...] = jnp.zeros(guq_ref.shape, F32)
            gukv_ref[...] = jnp.zeros(gukv_ref.shape, F32)
            small_ref[...] = jnp.zeros(small_ref.shape, F32)

        m_tabs = (tab_ref[0], tab_ref[1], tab_ref[2])
        d_tabs = (tab_ref[3], tab_ref[4], tab_ref[5])

        def rms_bwd(c, dn, gain):
            r = lax.rsqrt(jnp.mean(c * c, axis=1, keepdims=True) + RMS_EPS)
            u = dn * gain
            dc = r * u - c * (r * r * r) * jnp.mean(u * c, axis=1, keepdims=True)
            return dc, jnp.sum(dn * c * r, axis=0, keepdims=True)

        dqm = dqm_ref[0].T
        dq = jnp.concatenate(
            [dqm[:, :HW], _rope_wide(_rope_t, dqm[:, HW:], *m_tabs, MLA_ROPE // 2)], axis=1) * MLA_SCALE
        dq = dq.astype(BF16)
        guq_ref[...] += lax.dot_general(qn_ref[...], dq, TN, preferred_element_type=F32)
        dqn = lax.dot_general(dq, wuq_ref[...], NT, preferred_element_type=F32)
        dcq, gq = rms_bwd(cq_ref[...], dqn, qg_ref[...])
        small_ref[0:1, :] += gq

        dkv = jnp.concatenate([dkn_ref[...], dv_ref[...]], axis=1)
        gukv_ref[...] += lax.dot_general(kvn_ref[...], dkv, TN, preferred_element_type=F32)
        dkvn = lax.dot_general(dkv, wukv_ref[...], NT, preferred_element_type=F32)
        dckv, gkv = rms_bwd(ckv_ref[...], dkvn, kvg_ref[...])
        small_ref[1:2, :KV_RANK] += gkv

        dh_ref[:, C_CQ:C_CKV] = dcq.astype(BF16)
        dh_ref[:, C_CKV:C_KR] = dckv.astype(BF16)
        dh_ref[:, C_KR:C_GA] = _rope_t(dkpe_ref[...], *m_tabs, MLA_ROPE // 2).astype(BF16)
        dh_ref[:, C_GA:C_QB] = dga_ref[...]
        in_sequence = functools.partial(_in_sequence, lanes_sc=seq_sc)
        dqb = dqb_ref[0].T + in_sequence(dqf_ref)
        dh_ref[:, C_QB:C_KB] = (_rope_wide(_rope_t, dqb, *d_tabs, DIL_ROT // 2) * DIL_SCALE).astype(BF16)
        dkb = dkb_ref[...] + in_sequence(dkf_ref)
        dh_ref[:, C_KB:C_VB] = _rope_wide(_rope_t, dkb, *d_tabs, DIL_ROT // 2).astype(BF16)
        dh_ref[:, C_VB:C_GB] = (dvb_ref[...].astype(F32) + in_sequence(dvf_ref)).astype(BF16)
        dh_ref[:, C_GB:C_END] = dgb_ref[...]

        @pl.when(i == seq // bt - 1)
        def _():
            for h in range(HEADS):
                guq3_ref[h] = jnp.concatenate(
                    [guq_ref[:, MLA_NOPE * h:MLA_NOPE * (h + 1)],
                     guq_ref[:, HW + MLA_ROPE * h:HW + MLA_ROPE * (h + 1)]], axis=1).astype(BF16)
                gukv3_ref[h] = jnp.concatenate(
                    [gukv_ref[:, MLA_NOPE * h:MLA_NOPE * (h + 1)],
                     gukv_ref[:, HW + MLA_V * h:HW + MLA_V * (h + 1)]], axis=1).astype(BF16)

    def tok(width):
        return pl.BlockSpec((bt, width), lambda i: (i, 0))

    def tok_t(a):
        per = a.shape[2] // bt
        return pl.BlockSpec((1, a.shape[1], bt), lambda i: (i // per, 0, i % per))

    def full(shape):
        return pl.BlockSpec(shape, lambda i: (0,) * len(shape))

    by_class = pl.BlockSpec((n_cls, bt // n_cls, HW), lambda i: (0, i, 0))
    uq3 = (HEADS, Q_RANK, MLA_NOPE + MLA_ROPE)
    ukv3 = (HEADS, KV_RANK, MLA_NOPE + MLA_V)
    return _pcall(
        body, name="bwd_mid", grid=(seq // bt,),
        in_specs=[tok_t(dq_m), tok(HW), tok(HW), tok(LANES), tok_t(dqb), tok(HW), tok(HW), by_class, by_class, by_class,
                  tok(HW), tok(HW),
                  tok(Q_RANK), tok(KV_RANK), tok(Q_RANK), tok(KV_RANK),
                  full(w_uq_r.shape), full(w_ukv_r.shape), full((1, Q_RANK)), full((1, KV_RANK)),
                  pl.BlockSpec((6, bt, LANES), lambda i: (0, i, 0))],
        out_specs=[tok(C_END), full(uq3), full(ukv3), full((8, Q_RANK))],
        out_shape=[jax.ShapeDtypeStruct((seq, C_END), BF16), jax.ShapeDtypeStruct(uq3, BF16),
                   jax.ShapeDtypeStruct(ukv3, BF16), jax.ShapeDtypeStruct((8, Q_RANK), F32)],
        scratch_shapes=[pltpu.VMEM((HW // LANES, bt, LANES), F32), pltpu.VMEM(w_uq_r.shape, F32),
                        pltpu.VMEM(w_ukv_r.shape, F32)],
        compiler_params=_cparams(dimension_semantics=("arbitrary",)),
    )(dq_m, dkn, dv, dkpe, dqb, dkb, dvb, *far, dga, dgb, cq, ckv, qn, kvn, w_uq_r, w_ukv_r, qg, kvg, tabs)


def _grad_x(dz, dh, w_in_r, bt, ride=None):
    seq = dz.shape[0]
    n_steps = seq // bt

    def body(dz_ref, dh_ref, w_ref, gx_ref):
        gx_ref[...] = ALPHA * dz_ref[...] + lax.dot_general(
            dh_ref[...], w_ref[...], NT, preferred_element_type=F32)

    args = [dz, dh, w_in_r]
    in_specs = [pl.BlockSpec((bt, D_MODEL), lambda i: (i, 0)), pl.BlockSpec((bt, C_END), lambda i: (i, 0)),
                pl.BlockSpec(w_in_r.shape, lambda i: (0, 0))]
    out_specs = [pl.BlockSpec((bt, D_MODEL), lambda i: (i, 0))]
    out_shape = [jax.ShapeDtypeStruct((seq, D_MODEL), F32)]
    scratch = []
    body = _ride_along(body, ride, 0, len(args), len(out_shape), 0, n_steps)
    if ride is not None:
        args, in_specs = args + ride.args, in_specs + ride.in_specs
        out_specs, out_shape, scratch = out_specs + ride.out_specs, out_shape + ride.out_shape, ride.scratch
    return _pcall(
        body, name="grad_x", grid=(n_steps,),
        in_specs=in_specs, out_specs=out_specs, out_shape=out_shape, scratch_shapes=scratch,
        compiler_params=_cparams(dimension_semantics=("arbitrary",)),
    )(*args)


def _grad_w_in(x, dh, bt):
    seq = x.shape[0]
    shard = IN_WIDTH // N_DEV
    k_lo, k_hi = IN_SPLITS[0] + IN_SPLITS[1], IN_SPLITS[0] + IN_SPLITS[1] + MLA_ROPE

    def body(x_ref, dh_ref, out_ref, acc):
        i = pl.program_id(0)

        @pl.when(i == 0)
        def _():
            acc[...] = jnp.zeros(acc.shape, F32)

        acc[...] += lax.dot_general(x_ref[...].astype(BF16), dh_ref[...], TN, preferred_element_type=F32)

        @pl.when(i == seq // bt - 1)
        def _():
            kr = acc[:, C_KR:C_GA]
            kr = kr + pltpu.roll(kr, 96, 1) + pltpu.roll(kr, 64, 1) + pltpu.roll(kr, 32, 1)
            for d in range(N_DEV):
                lo, hi = shard * d, shard * (d + 1)
                pieces = []
                if lo < k_lo:
                    pieces.append(acc[:, lo:min(hi, k_lo)])
                if lo < k_hi and hi > k_lo:
                    pieces.append(kr[:, max(lo, k_lo) - k_lo:min(hi, k_hi) - k_lo])
                if hi > k_hi:
                    shift = C_GA - k_hi
                    pieces.append(acc[:, max(lo, k_hi) + shift:hi + shift])
                blk = pieces[0] if len(pieces) == 1 else jnp.concatenate(pieces, axis=1)
                out_ref[d] = blk.astype(BF16)

    return _pcall(
        body, name="grad_w_in", grid=(seq // bt,),
        in_specs=[pl.BlockSpec((bt, D_MODEL), lambda i: (i, 0)), pl.BlockSpec((bt, C_END), lambda i: (i, 0))],
        out_specs=pl.BlockSpec((N_DEV, D_MODEL, shard), lambda i: (0, 0, 0)),
        out_shape=jax.ShapeDtypeStruct((N_DEV, D_MODEL, shard), BF16),
        scratch_shapes=[pltpu.VMEM((D_MODEL, C_END), F32)],
        compiler_params=_cparams(dimension_semantics=("arbitrary",)),
    )(x, dh)


def _local_step(x, tgt, w_in_r, w_uq_r, w_ukv_r, w_out_rider, g_out_rider, reduce_rider, q_norm_g, kv_norm_g,
                ln_g, ln_b, bt=BLOCK_TOKENS, blk_m=BLOCK_MLA, blk_d=BLOCK_DIL):
    seq = x.shape[0]
    tabs = jnp.asarray(_rope_tables(seq))
    qg, kvg = q_norm_g.reshape(1, -1), kv_norm_g.reshape(1, -1)

    far_dil = DIL_CONFIGS[-1][1]
    cls = seq // far_dil
    (cq, ckv, qn, kvn, qcat, kn, kpe, v, ga, gb, qb, kb, vb, knt, kpet, vt, kbt, vbt, qb_c, kb_c, vb_c) = _fwd_proj(
        x, w_in_r, w_uq_r, w_ukv_r, qg, kvg, tabs, bt, far_dil)
    qb_c, kb_c, vb_c = (a.reshape(seq, HW) for a in (qb_c, kb_c, vb_c))

    nq_m, nq_d = seq // blk_m, seq // blk_d
    bias_m = _mla_bias_t(blk_m)
    oa, lse_a, w_out = _attn_fwd(
        "mla_fwd", qcat, kn, kpe, vt, bias_m, _steps(nq_m, nq_m, False, True), blk_m, ride=w_out_rider)

    bias_near = _dil_bias_t(blk_d, DIL_NEAR)
    ob_near, lse_near = _attn_fwd(
        "dil_fwd", qb, kb, None, vbt, bias_near, _steps(nq_d, -(-DIL_NEAR // blk_d), False, False), blk_d)
    each = jnp.arange(far_dil, dtype=jnp.int32)
    steps_far = [each, each, jnp.zeros_like(each), jnp.ones_like(each), jnp.ones_like(each)]
    bias_far = _dil_far_bias_t(cls)
    ob_far, lse_far = _attn_fwd(
        "dil_far_fwd", qb_c, kb_c, None, vb_c, bias_far, steps_far, cls, v_token_major=True)

    dz, doa, dob, dga, dgb, dst_a, dst_b, lse_b, g_out, small1, dob_c = _out_ln(
        oa, ob_near, ob_far.reshape(far_dil, cls, HW), lse_near, _lanes_from_classes(lse_far, far_dil), ga, gb, x, tgt,
        w_out.reshape(D_MODEL, D_MODEL), ln_g.reshape(1, -1), ln_b.reshape(1, -1), bt)

    dq_m, dkn, dkpe, dv, g_out_recv = _attn_bwd(
        "mla_bwd", qcat, kn, kpe, v, knt, kpet, bias_m, doa, lse_a, dst_a, _steps(nq_m, nq_m, True, True), blk_m,
        ride=g_out_rider(g_out.reshape(N_DEV, D_MODEL // N_DEV, D_MODEL)))
    dqb, dkb_near, dvb_near = _attn_bwd(
        "dil_bwd", qb, kb, None, vb, kbt, None, bias_near, dob, lse_b, dst_b,
        _steps(nq_d, -(-DIL_NEAR // blk_d), True, False), blk_d)
    dqb_far, dkb_far, dvb_far = _attn_bwd(
        "dil_far_bwd", qb_c, kb_c, None, vb_c, None, None, bias_far, dob_c.reshape(seq, HW),
        _lanes_to_classes(lse_b, far_dil), _lanes_to_classes(dst_b, far_dil), steps_far, cls, single_visit=True)
    far = [a.reshape(far_dil, cls, HW) for a in (dqb_far, dkb_far, dvb_far)]

    dh, g_uq, g_ukv, small2 = _bwd_mid(
        dq_m, dkn, dv, dkpe, dqb, dkb_near, dvb_near, far, dga, dgb, cq, ckv, qn, kvn, w_uq_r, w_ukv_r, qg, kvg, tabs, bt)
    g_in = _grad_w_in(x, dh, min(seq, 2 * bt))
    grads3 = [g_in, g_uq, g_ukv]
    small_part = _small_rows(small1[0], small1[1], small2[0, :Q_RANK], small2[1, :KV_RANK], small1[2])
    grad_x, *reduced = _grad_x(
        dz, dh, w_in_r, bt, ride=reduce_rider(grads3, g_out_recv, small_part, min(1, seq // bt - 1)))
    return grad_x, reduced


MESH_ID = pl.DeviceIdType.MESH
SHARD_SHAPES = ((D_MODEL, IN_WIDTH // N_DEV), (Q_RANK, 768 // N_DEV), (KV_RANK, 1024 // N_DEV), (D_MODEL // N_DEV, D_MODEL))
ADAM_ROWS = (32, 128, 128, 16)


def _me():
    x, y, c = lax.axis_index("x"), lax.axis_index("y"), lax.axis_index("c")
    return x, y, c, 4 * x + 2 * y + c


def _peer(k):
    x, y, c, _ = _me()
    px = 1 - x if (k >> 2) & 1 else x
    py = 1 - y if (k >> 1) & 1 else y
    pc = 1 - c if k & 1 else c
    return (px, py, pc), 4 * px + 2 * py + pc


def _all_gather_weights(shards):
    n = len(shards)
    shard = IN_WIDTH // N_DEV
    k_lo = IN_SPLITS[0] + IN_SPLITS[1]
    k_hi = k_lo + MLA_ROPE

    def body(*refs):
        ins = refs[:n]
        win_ref, wuq_ref, wukv_ref = refs[n:2 * n]
        bufs = refs[2 * n:3 * n]
        send_sems, recv_sems = refs[3 * n:]
        x, y, c, me = _me()
        here, sibling = (x, y, c), (x, y, 1 - c)
        chips = [(1 - x, y), (x, 1 - y), (1 - x, 1 - y)]
        for t in range(n):
            bufs[t][me] = ins[t][...].astype(BF16)

        def copy(t, k, px, py, pc, to):
            blk = bufs[t].at[4 * px + 2 * py + pc]
            return pltpu.make_async_remote_copy(
                src_ref=blk, dst_ref=blk, send_sem=send_sems.at[t, k], recv_sem=recv_sems.at[t, k],
                device_id=to, device_id_type=MESH_ID)

        first = []
        for t in range(n):
            first.append(copy(t, 0, x, y, c, sibling))
            for j, (px, py) in enumerate(chips):
                first.append(copy(t, 1 + j, x, y, c, (px, py, c)))
        for cp in first:
            cp.start()
        passed = []
        for j, (px, py) in enumerate(chips):
            for t in range(n):
                copy(t, 1 + j, px, py, c, here).wait_recv()
                cp = copy(t, 4 + j, px, py, c, sibling)
                cp.start()
                passed.append(cp)
        for t in range(n):
            copy(t, 0, x, y, 1 - c, here).wait_recv()
        for j, (px, py) in enumerate(chips):
            for t in range(n):
                copy(t, 4 + j, px, py, 1 - c, here).wait_recv()
        for cp in first + passed:
            cp.wait_send()

        a_in, a_uq, a_ukv = bufs
        for d in range(N_DEV):
            lo, hi = shard * d, shard * (d + 1)
            if lo < k_lo:
                win_ref[:, lo:min(hi, k_lo)] = a_in[d, :, 0:min(hi, k_lo) - lo]
            if lo < k_hi and hi > k_lo:
                kr = a_in[d, :, k_lo - lo:k_hi - lo]
                for rep in range(4):
                    win_ref[:, C_KR + MLA_ROPE * rep:C_KR + MLA_ROPE * (rep + 1)] = kr
            if hi > k_hi:
                src = max(lo, k_hi)
                win_ref[:, src + C_GA - k_hi:hi + C_GA - k_hi] = a_in[d, :, src - lo:hi - lo]
        for h in range(HEADS):
            wuq_ref[:, MLA_NOPE * h:MLA_NOPE * (h + 1)] = a_uq[h, :, :MLA_NOPE]
            wuq_ref[:, HW + MLA_ROPE * h:HW + MLA_ROPE * (h + 1)] = a_uq[h, :, MLA_NOPE:]
            wukv_ref[:, MLA_NOPE * h:MLA_NOPE * (h + 1)] = a_ukv[h, :, :MLA_NOPE]
            wukv_ref[:, HW + MLA_V * h:HW + MLA_V * (h + 1)] = a_ukv[h, :, MLA_NOPE:]

    vmem = pl.BlockSpec(memory_space=pltpu.VMEM)
    return _pcall(
        body, name="gather_weights",
        in_specs=[vmem] * n, out_specs=[vmem] * n,
        out_shape=[jax.ShapeDtypeStruct((D_MODEL, C_END), BF16), jax.ShapeDtypeStruct((Q_RANK, QW), BF16),
                   jax.ShapeDtypeStruct((KV_RANK, 2 * HW), BF16)],
        scratch_shapes=[pltpu.VMEM((N_DEV,) + s, BF16) for s in SHARD_SHAPES[:n]]
        + [pltpu.SemaphoreType.DMA((n, N_DEV - 1)), pltpu.SemaphoreType.DMA((n, N_DEV - 1))],
        compiler_params=_cparams(),
    )(*shards)


def _gather_w_out_rider(w_out):
    def copies(full_ref, stage, send_sems, recv_sems):
        me = _me()[3]
        out = []
        for k in range(1, N_DEV):
            peer, pidx = _peer(k)
            send = pltpu.make_async_remote_copy(
                src_ref=stage, dst_ref=full_ref.at[me], send_sem=send_sems.at[k - 1], recv_sem=recv_sems.at[k - 1],
                device_id=peer, device_id_type=MESH_ID)
            recv = pltpu.make_async_remote_copy(
                src_ref=stage, dst_ref=full_ref.at[pidx], send_sem=send_sems.at[k - 1], recv_sem=recv_sems.at[k - 1],
                device_id=peer, device_id_type=MESH_ID)
            out.append((send, recv))
        return out

    def start(ins, outs, scr):
        stage, send_sems, recv_sems, own_sem = scr
        stage[...] = ins[0][...].astype(BF16)
        pltpu.make_async_copy(stage, outs[0].at[_me()[3]], own_sem).start()
        for send, _ in copies(outs[0], stage, send_sems, recv_sems):
            send.start()

    def finish(ins, outs, scr):
        stage, send_sems, recv_sems, own_sem = scr
        pltpu.make_async_copy(stage, outs[0].at[_me()[3]], own_sem).wait()
        pairs = copies(outs[0], stage, send_sems, recv_sems)
        for _, recv in pairs:
            recv.wait_recv()
        for send, _ in pairs:
            send.wait_send()

    shape = SHARD_SHAPES[3]
    return Rider(
        args=[w_out], in_specs=[pl.BlockSpec(shape, lambda t, *_: (0, 0))],
        out_shape=[jax.ShapeDtypeStruct((N_DEV,) + shape, BF16)], out_specs=[pl.BlockSpec(memory_space=pl.ANY)],
        scratch=[pltpu.VMEM(shape, BF16), pltpu.SemaphoreType.DMA((N_DEV - 1,)), pltpu.SemaphoreType.DMA((N_DEV - 1,)),
                 pltpu.SemaphoreType.DMA],
        start=start, finish=finish)


def _scatter_g_out_rider(blocks):
    def copies(src_ref, dst_ref, send_sems, recv_sems):
        out = []
        for k in range(1, N_DEV):
            peer, pidx = _peer(k)
            out.append(pltpu.make_async_remote_copy(
                src_ref=src_ref.at[pidx], dst_ref=dst_ref.at[k], send_sem=send_sems.at[k - 1],
                recv_sem=recv_sems.at[k - 1], device_id=peer, device_id_type=MESH_ID))
        return out

    def start(ins, outs, scr):
        send_sems, recv_sems, own_sem = scr
        pltpu.make_async_copy(ins[0].at[_me()[3]], outs[0].at[0], own_sem).start()
        for cp in copies(ins[0], outs[0], send_sems, recv_sems):
            cp.start()

    def finish(ins, outs, scr):
        send_sems, recv_sems, own_sem = scr
        pltpu.make_async_copy(ins[0].at[_me()[3]], outs[0].at[0], own_sem).wait()
        for cp in copies(ins[0], outs[0], send_sems, recv_sems):
            cp.wait()

    hbm = pl.BlockSpec(memory_space=pl.ANY)
    return Rider(
        args=[blocks], in_specs=[hbm], out_shape=[jax.ShapeDtypeStruct(blocks.shape, blocks.dtype)], out_specs=[hbm],
        scratch=[pltpu.SemaphoreType.DMA((N_DEV - 1,)), pltpu.SemaphoreType.DMA((N_DEV - 1,)), pltpu.SemaphoreType.DMA],
        start=start, finish=finish)


def _adamw(w, g, m, v):
    m = ADAM_B1 * m + (1.0 - ADAM_B1) * g
    v = ADAM_B2 * v + (1.0 - ADAM_B2) * jnp.square(g)
    m_hat = m / (1.0 - ADAM_B1 ** ADAM_STEP)
    v_hat = v / (1.0 - ADAM_B2 ** ADAM_STEP)
    delta = -ADAM_LR * (m_hat / (jnp.sqrt(v_hat) + ADAM_EPS) + ADAM_WD * w)
    return delta, m, v


def _reduce_grads_rider(grads3, arrived, small_part, mid_step):
    n = len(grads3)

    class Refs:
        def __init__(self, ins, outs, scr):
            self.g3, self.arr, self.sp = ins[0:n], ins[n], ins[n + 1]
            self.gsum, self.gsum_out, self.ssum = outs[0:n], outs[n], outs[n + 1]
            self.own, self.sib, self.part, self.ici = scr[0:n], scr[n:2 * n], scr[2 * n:3 * n], scr[3 * n:4 * n]
            self.rsmall = scr[4 * n]
            (self.loc_sems, self.d2d_send, self.d2d_recv, self.ici_send, self.ici_recv,
             self.sm_send, self.sm_recv) = scr[4 * n + 1:]
            self.x, self.y, self.c, self.me = _me()
            self.chips = [(1 - self.x, self.y), (self.x, 1 - self.y), (1 - self.x, 1 - self.y)]

        def small(self):
            return [pltpu.make_async_remote_copy(
                src_ref=self.rsmall.at[0], dst_ref=self.rsmall.at[k], send_sem=self.sm_send.at[k - 1],
                recv_sem=self.sm_recv.at[k - 1], device_id=_peer(k)[0], device_id_type=MESH_ID)
                for k in range(1, N_DEV)]

        def level1(self):
            local, to_sib = [], []
            for t in range(n):
                for q in range(4):
                    local.append(pltpu.make_async_copy(
                        self.g3[t].at[2 * q + self.c], self.own[t].at[q], self.loc_sems.at[t, q]))
                    to_sib.append(pltpu.make_async_remote_copy(
                        src_ref=self.g3[t].at[2 * q + 1 - self.c], dst_ref=self.sib[t].at[q],
                        send_sem=self.d2d_send.at[t, q], recv_sem=self.d2d_recv.at[t, q],
                        device_id=(self.x, self.y, 1 - self.c), device_id_type=MESH_ID))
            return local, to_sib

        def level2(self):
            return [pltpu.make_async_remote_copy(
                src_ref=self.part[t].at[2 * px + py], dst_ref=self.ici[t].at[j], send_sem=self.ici_send.at[t, j],
                recv_sem=self.ici_recv.at[t, j], device_id=(px, py, self.c), device_id_type=MESH_ID)
                for t in range(n) for j, (px, py) in enumerate(self.chips)]

    def chunks(t, fn):
        rows = ADAM_ROWS[t]

        def step(i, carry):
            fn(pl.ds(pl.multiple_of(i * rows, rows), rows))
            return carry

        lax.fori_loop(0, SHARD_SHAPES[t][0] // rows, step, 0)

    def start(*refs):
        r = Refs(*refs)
        r.rsmall[0] = r.sp[...]
        local, to_sib = r.level1()
        for cp in r.small() + local + to_sib:
            cp.start()

    def middle(*refs):
        r = Refs(*refs)
        local, to_sib = r.level1()
        for cp in local:
            cp.wait()
        for cp in to_sib:
            cp.wait_recv()
        my_chip = 2 * r.x + r.y
        for t in range(n):
            def pair_sums(rows, t=t):
                for q in range(4):
                    r.part[t][q, rows, :] = (
                        r.own[t][q, rows, :].astype(F32) + r.sib[t][q, rows, :].astype(F32)).astype(BF16)
                r.gsum[t][rows, :] = r.own[t][my_chip, rows, :].astype(F32) + r.sib[t][my_chip, rows, :].astype(F32)

            chunks(t, pair_sums)
        for cp in r.level2():
            cp.start()

        def add_arrived(rows):
            g = r.arr[0, rows, :].astype(F32)
            for k in range(1, N_DEV):
                g = g + r.arr[k, rows, :].astype(F32)
            r.gsum_out[rows, :] = g

        chunks(3, add_arrived)

    def finish(*refs):
        r = Refs(*refs)
        to_chips = r.level2()
        for cp in to_chips:
            cp.wait_recv()
        for t in range(n):
            def add_chips(rows, t=t):
                g = r.gsum[t][rows, :]
                for j in range(3):
                    g = g + r.ici[t][j, rows, :].astype(F32)
                r.gsum[t][rows, :] = g

            chunks(t, add_chips)
        small = r.small()
        for cp in small:
            cp.wait_recv()
        tot = r.rsmall[r.me]
        for d in range(1, N_DEV):
            tot = tot + r.rsmall[jnp.bitwise_xor(r.me, d)]
        r.ssum[...] = tot
        for cp in small + r.level1()[1] + to_chips:
            cp.wait_send()

    hbm = pl.BlockSpec(memory_space=pl.ANY)
    dma = pltpu.SemaphoreType.DMA

    def whole(shape):
        return pl.BlockSpec(shape, lambda i: (0,) * len(shape))

    out_shapes = list(SHARD_SHAPES) + [(8, D_MODEL)]
    return Rider(
        args=list(grads3) + [arrived, small_part],
        in_specs=[hbm] * n + [whole(arrived.shape), whole(small_part.shape)],
        out_shape=[jax.ShapeDtypeStruct(s, F32) for s in out_shapes], out_specs=[whole(s) for s in out_shapes],
        scratch=[pltpu.VMEM((slots,) + s, BF16) for slots in (4, 4, 4, 3) for s in SHARD_SHAPES[:n]]
        + [pltpu.VMEM((N_DEV, 8, D_MODEL), F32), dma((n, 4)), dma((n, 4)), dma((n, 4)), dma((n, 3)), dma((n, 3)),
           dma((N_DEV - 1,)), dma((N_DEV - 1,))],
        start=start, finish=finish, stages=((mid_step, middle),))


def _adamw_update(grads, small_grad, wmv, small_wmv):
    n_small = len(small_wmv)

    def body(*refs):
        g_refs, sg_ref = refs[0:4], refs[4]
        wmv_refs = [refs[5 + 3 * t:8 + 3 * t] for t in range(4)]
        swmv_refs = [refs[17 + 3 * t:20 + 3 * t] for t in range(n_small)]
        outs = refs[17 + 3 * n_small:]
        out_refs = [outs[4 * t:4 * t + 4] for t in range(4)]
        sout_refs = [outs[16 + 4 * t:20 + 4 * t] for t in range(n_small)]
        loss_ref = outs[16 + 4 * n_small]
        for t, (w_ref, m_ref, v_ref) in enumerate(swmv_refs):
            g = sg_ref[t:t + 1, :w_ref.shape[1]]
            delta, m, v = _adamw(w_ref[...], g, m_ref[...], v_ref[...])
            sout_refs[t][0][...], sout_refs[t][1][...], sout_refs[t][2][...], sout_refs[t][3][...] = g, delta, m, v
        loss_ref[...] = (0.5 / D_MODEL) * jnp.sum(sg_ref[n_small:n_small + 1, :], axis=1, keepdims=True)
        for t in range(4):
            rows = ADAM_ROWS[t]
            w_ref, m_ref, v_ref = wmv_refs[t]
            g_out, d_out, m_out, v_out = out_refs[t]

            def step(i, carry, g_ref=g_refs[t], rows=rows, w_ref=w_ref, m_ref=m_ref, v_ref=v_ref,
                     g_out=g_out, d_out=d_out, m_out=m_out, v_out=v_out):
                r = pl.ds(pl.multiple_of(i * rows, rows), rows)
                g = g_ref[r, :]
                delta, m, v = _adamw(w_ref[r, :], g, m_ref[r, :], v_ref[r, :])
                g_out[r, :], d_out[r, :], m_out[r, :], v_out[r, :] = g, delta, m, v
                return carry

            lax.fori_loop(0, SHARD_SHAPES[t][0] // rows, step, 0)

    vmem = pl.BlockSpec(memory_space=pltpu.VMEM)
    flat_wmv = [a for trio in wmv for a in trio]
    flat_small = [a for trio in small_wmv for a in trio]
    out_shape = ([jax.ShapeDtypeStruct(s, F32) for s in SHARD_SHAPES for _ in range(4)]
                 + [jax.ShapeDtypeStruct(trio[0].shape, F32) for trio in small_wmv for _ in range(4)]
                 + [jax.ShapeDtypeStruct((1, 1), F32)])
    return _pcall(
        body, name="adamw",
        in_specs=[vmem] * (5 + len(flat_wmv) + len(flat_small)), out_specs=[vmem] * len(out_shape),
        out_shape=out_shape,
        compiler_params=_cparams(),
    )(*grads, small_grad, *flat_wmv, *flat_small)


def _small_rows(ln_g, ln_b, q_norm_g, kv_norm_g, extra=None):
    pad = lambda a: jnp.pad(a, (0, D_MODEL - a.shape[0]))
    rows = [ln_g, ln_b, pad(q_norm_g), pad(kv_norm_g)] + ([] if extra is None else [extra])
    return jnp.pad(jnp.stack(rows), ((0, 8 - len(rows)), (0, 0)))


def kernel(x, w_in, q_norm_g, kv_norm_g, w_uq, w_ukv, w_out, ln_g, ln_b, loss_target, m_w_in, m_q_norm_g, m_kv_norm_g, m_w_uq, m_w_ukv, m_w_out, m_ln_g, m_ln_b, v_w_in, v_q_norm_g, v_kv_norm_g, v_w_uq, v_w_ukv, v_w_out, v_ln_g, v_ln_b):
    w_in_r, w_uq_r, w_ukv_r = _all_gather_weights([w_in, w_uq, w_ukv])
    grad_x, sums = _local_step(
        x[0], loss_target[0], w_in_r, w_uq_r, w_ukv_r, _gather_w_out_rider(w_out), _scatter_g_out_rider,
        _reduce_grads_rider, q_norm_g, kv_norm_g, ln_g, ln_b)
    row = lambda a: a.reshape(1, -1)
    small_wmv = [(row(ln_g), row(m_ln_g), row(v_ln_g)), (row(ln_b), row(m_ln_b), row(v_ln_b)),
                 (row(q_norm_g), row(m_q_norm_g), row(v_q_norm_g)), (row(kv_norm_g), row(m_kv_norm_g), row(v_kv_norm_g))]
    wmv = [(w_in, m_w_in, v_w_in), (w_uq, m_w_uq, v_w_uq), (w_ukv, m_w_ukv, v_w_ukv), (w_out, m_w_out, v_w_out)]
    res = _adamw_update(sums[:4], sums[4], wmv, small_wmv)
    big = [res[4 * t:4 * t + 4] for t in range(4)]
    small = [[a.reshape(-1) for a in res[16 + 4 * t:20 + 4 * t]] for t in range(4)]
    loss = res[32].reshape(())

    def group(kind):
        return (big[0][kind], small[2][kind], small[3][kind], big[1][kind], big[2][kind], big[3][kind],
                small[0][kind], small[1][kind])

    return (loss, grad_x[None], *group(0), *group(1), *group(2), *group(3))
```

```python
import functools
from typing import Callable, NamedTuple

import numpy as np
import jax
import jax.numpy as jnp
from jax import lax
from jax.experimental import pallas as pl
from jax.experimental.pallas import tpu as pltpu

F32 = jnp.float32
BF16 = jnp.bfloat16

D_MODEL = 1024
ROPE_THETA = 500000.0
NEG = -1e30
RMS_EPS = 1e-6
LN_EPS = 1e-5
HEADS = 8
MLA_NOPE = 64
MLA_ROPE = 32
MLA_V = 64
Q_RANK = 384
KV_RANK = 256
DIL_HEAD = 64
DIL_ROT = 16
DIL_CONFIGS = ((128, 1), (512, 4), (2048, 16))
DIL_NEAR = 512
HW = HEADS * 64
QW = HW + HEADS * MLA_ROPE
IN_SPLITS = (Q_RANK, KV_RANK, MLA_ROPE, HW, HW, HW, HW, HW)
IN_WIDTH = sum(IN_SPLITS)
ALPHA = 2.0 ** 0.25
MLA_SCALE = (MLA_NOPE + MLA_ROPE) ** -0.5
DIL_SCALE = DIL_HEAD ** -0.5
LOG2E = 1.4426950408889634
LN2 = 0.6931471805599453

ADAM_LR = 0.001
ADAM_B1 = 0.9
ADAM_B2 = 0.999
ADAM_EPS = 1e-08
ADAM_WD = 0.01
ADAM_STEP = 10

N_DEV = 8
LANES = 128
VMEM_LIMIT = 56 * 1024 * 1024
BLOCK_TOKENS = 512
BLOCK_MLA = 512
BLOCK_DIL = 512

C_CQ, C_CKV, C_KR, C_GA, C_QB, C_KB, C_VB, C_GB, C_END = 0, 384, 640, 768, 1280, 1792, 2304, 2816, 3328

NT = (((1,), (1,)), ((), ()))
TN = (((0,), (0,)), ((), ()))


def _pcall(body, **kw):
    return pl.pallas_call(body, **kw)


def _cparams(**kw):
    return pltpu.CompilerParams(vmem_limit_bytes=VMEM_LIMIT, **kw)


def _rope_tables(seq):
    def tabs(dim, period):
        half = dim // 2
        inv = np.float32(ROPE_THETA) ** (-np.arange(0, dim, 2, dtype=np.float32) / np.float32(dim))
        ang = np.arange(seq, dtype=np.float32)[:, None] * inv.astype(np.float32)[None, :]
        cos, sin = np.cos(ang).astype(np.float32), np.sin(ang).astype(np.float32)
        j = np.arange(LANES) % period
        f = j % half
        c = np.where(j < dim, cos[:, f], np.float32(1.0))
        s1 = np.where(j < half, -sin[:, f], np.float32(0.0))
        s2 = np.where((j >= half) & (j < dim), sin[:, f], np.float32(0.0))
        return [c, s1, s2]
    return np.stack(tabs(MLA_ROPE, MLA_ROPE) + tabs(DIL_ROT, DIL_HEAD)).astype(np.float32)


def _rope(t, c, s1, s2, half):
    return t * c + pltpu.roll(t, LANES - half, 1) * s1 + pltpu.roll(t, half, 1) * s2


def _rope_t(d, c, s1, s2, half):
    return d * c + pltpu.roll(d * s1, half, 1) + pltpu.roll(d * s2, LANES - half, 1)


def _rope_wide(fn, t, c, s1, s2, half):
    return jnp.concatenate(
        [fn(t[:, i:i + LANES], c, s1, s2, half) for i in range(0, t.shape[1], LANES)], axis=1)


def _mla_bias_t(blk):
    a = np.arange(blk)
    causal = np.where(a[:, None] <= a[None, :], 0.0, NEG)
    return np.stack([np.zeros((blk, blk)), causal]).astype(np.float32)


def _dil_bias_t(blk, reach):
    a = np.arange(blk)
    out = []
    for off in range(-(-reach // blk) + 1):
        delta = blk * off + a[None, :] - a[:, None]
        mult = np.zeros((blk, blk))
        for window, dil in DIL_CONFIGS:
            mult += (delta >= 0) & (delta % dil == 0) & (delta <= min(window, reach))
        out.append(np.where(mult > 0, np.log2(np.maximum(mult, 1.0)), NEG))
    return np.stack(out).astype(np.float32)


def _dil_far_bias_t(length):
    window, dil = DIL_CONFIGS[-1]
    a = np.arange(length)
    steps_back = a[None, :] - a[:, None]
    seen = (steps_back * dil > DIL_NEAR) & (steps_back * dil <= window)
    return np.where(seen, 0.0, NEG).astype(np.float32)[None]


def _lanes_to_classes(a, dil):
    h, s = a.shape
    return a.reshape(h, s // dil, dil).transpose(0, 2, 1).reshape(h, s)


def _lanes_from_classes(a, dil):
    h, s = a.shape
    return a.reshape(h, dil, s // dil).transpose(0, 2, 1).reshape(h, s)


def _steps(nq, span, by_key, diag_only_bias):
    rows = []
    if by_key:
        for ki in range(nq):
            hi = min(nq - 1, ki + span)
            for qi in range(ki, hi + 1):
                rows.append((qi, ki, int(qi == ki), int(qi == hi)))
    else:
        for qi in range(nq):
            lo = max(0, qi - span)
            for ki in range(lo, qi + 1):
                rows.append((qi, ki, int(ki == lo), int(ki == qi)))
    arr = np.array(rows, dtype=np.int32)
    off = arr[:, 0] - arr[:, 1]
    bias_idx = (off == 0).astype(np.int32) if diag_only_bias else off.astype(np.int32)
    return [jnp.asarray(v) for v in (arr[:, 0], arr[:, 1], bias_idx, arr[:, 2], arr[:, 3])]


def _by_class(val, out_ref, lanes_sc):
    n_cls, per = out_ref.shape[0], out_ref.shape[1]
    for c in range(val.shape[1] // LANES):
        lanes_sc[c] = val[:, LANES * c:LANES * (c + 1)]
        for r in range(n_cls):
            rows = lanes_sc.at[c][pl.ds(r, per, stride=n_cls), :]
            out_ref[r, :, LANES * c:LANES * (c + 1)] = rows.astype(out_ref.dtype)


def _in_sequence(ref, lanes_sc):
    n_cls, per, width = ref.shape
    for c in range(width // LANES):
        for r in range(n_cls):
            lanes_sc.at[c][pl.ds(r, per, stride=n_cls), :] = ref[r, :, LANES * c:LANES * (c + 1)].astype(F32)
    return jnp.concatenate([lanes_sc[c] for c in range(width // LANES)], axis=1)


def _fwd_proj(x, w_in_r, w_uq_r, w_ukv_r, qg, kvg, tabs, bt, n_cls):
    seq = x.shape[0]

    def body(x_ref, win_ref, wuq_ref, wukv_ref, qg_ref, kvg_ref, tab_ref,
             cq_ref, ckv_ref, qn_ref, kvn_ref, qcat_ref, kn_ref, kpe_ref, v_ref,
             ga_ref, gb_ref, qb_ref, kb_ref, vb_ref, knt_ref, kpet_ref, vt_ref, kbt_ref, vbt_ref,
             qbc_ref, kbc_ref, vbc_ref, lanes_sc):
        xb = x_ref[...].astype(BF16)

        def proj(lo, hi):
            return jnp.dot(xb, win_ref[:, lo:hi], preferred_element_type=F32)

        m_tabs = (tab_ref[0], tab_ref[1], tab_ref[2])
        d_tabs = (tab_ref[3], tab_ref[4], tab_ref[5])

        def use_cq(cq):
            cq_ref[...] = cq
            qn = (cq * lax.rsqrt(jnp.mean(cq * cq, axis=1, keepdims=True) + RMS_EPS) * qg_ref[...]).astype(BF16)
            qn_ref[...] = qn
            q = jnp.dot(qn, wuq_ref[...], preferred_element_type=F32)
            qcat_ref[:, :HW] = (q[:, :HW] * (MLA_SCALE * LOG2E)).astype(BF16)
            qcat_ref[:, HW:] = (
                _rope_wide(_rope, q[:, HW:], *m_tabs, MLA_ROPE // 2) * (MLA_SCALE * LOG2E)).astype(BF16)

        def use_ckv(ckv):
            ckv_ref[...] = ckv
            kvn = (ckv * lax.rsqrt(jnp.mean(ckv * ckv, axis=1, keepdims=True) + RMS_EPS) * kvg_ref[...]).astype(BF16)
            kvn_ref[...] = kvn
            kv = jnp.dot(kvn, wukv_ref[...], preferred_element_type=F32)
            kn_ref[...] = kv[:, :HW].astype(BF16)
            v_ref[...] = kv[:, HW:].astype(BF16)
            knt_ref[...] = kv[:, :HW].T.astype(BF16)
            vt_ref[...] = kv[:, HW:].T.astype(BF16)

        def use_kr(kr):
            kpe = _rope(kr, *m_tabs, MLA_ROPE // 2)
            kpe_ref[...] = kpe.astype(BF16)
            kpet_ref[...] = kpe.T[:MLA_ROPE, :].astype(BF16)

        def use_ga(ga):
            ga_ref[...] = ga

        def use_qb(qb):
            qb = _rope_wide(_rope, qb, *d_tabs, DIL_ROT // 2) * (DIL_SCALE * LOG2E)
            qb_ref[...] = qb.astype(BF16)
            _by_class(qb, qbc_ref, lanes_sc)

        def use_kb(kb):
            kb = _rope_wide(_rope, kb, *d_tabs, DIL_ROT // 2)
            kb_ref[...] = kb.astype(BF16)
            kbt_ref[...] = kb.T.astype(BF16)
            _by_class(kb, kbc_ref, lanes_sc)

        def use_vb(vb):
            vb_ref[...] = vb.astype(BF16)
            vbt_ref[...] = vb.T.astype(BF16)
            _by_class(vb, vbc_ref, lanes_sc)

        def use_gb(gb):
            gb_ref[...] = gb

        pieces = [(C_CQ, C_CKV, use_cq), (C_CKV, C_KR, use_ckv), (C_KR, C_GA, use_kr), (C_GA, C_QB, use_ga),
                  (C_QB, C_KB, use_qb), (C_KB, C_VB, use_kb), (C_VB, C_GB, use_vb), (C_GB, C_END, use_gb)]
        ahead = proj(*pieces[0][:2])
        for n, (_, _, use) in enumerate(pieces):
            cur = ahead
            if n + 1 < len(pieces):
                ahead = proj(*pieces[n + 1][:2])
            use(cur)

    def tok(width):
        return pl.BlockSpec((bt, width), lambda i: (i, 0))

    def tok_t(height):
        return pl.BlockSpec((height, bt), lambda i: (0, i))

    def full(a):
        return pl.BlockSpec(a.shape, lambda i: (0,) * a.ndim)

    outs = [(Q_RANK, F32), (KV_RANK, F32), (Q_RANK, BF16), (KV_RANK, BF16), (QW, BF16), (HW, BF16),
            (LANES, BF16), (HW, BF16), (HW, F32), (HW, F32), (HW, BF16), (HW, BF16), (HW, BF16)]
    outs_t = [HW, MLA_ROPE, HW, HW, HW]
    by_class = pl.BlockSpec((n_cls, bt // n_cls, HW), lambda i: (0, i, 0))
    return _pcall(
        body, name="fwd_proj", grid=(seq // bt,),
        in_specs=[tok(D_MODEL), full(w_in_r), full(w_uq_r), full(w_ukv_r), full(qg), full(kvg),
                  pl.BlockSpec((6, bt, LANES), lambda i: (0, i, 0))],
        out_specs=[tok(w) for w, _ in outs] + [tok_t(h) for h in outs_t] + [by_class] * 3,
        out_shape=[jax.ShapeDtypeStruct((seq, w), dt) for w, dt in outs]
        + [jax.ShapeDtypeStruct((h, seq), BF16) for h in outs_t]
        + [jax.ShapeDtypeStruct((n_cls, seq // n_cls, HW), BF16)] * 3,
        scratch_shapes=[pltpu.VMEM((HW // LANES, bt, LANES), F32)],
        compiler_params=_cparams(dimension_semantics=("arbitrary",)),
    )(x, w_in_r, w_uq_r, w_ukv_r, qg, kvg, tabs)


def _head_masks(lane, h):
    e, g = h % 2, h % 4
    me = (lane >= 64 * e) & (lane < 64 * e + 64)
    mr = (lane >= 32 * g) & (lane < 32 * g + 32)
    return me, mr


def _masked(mask, a):
    return jnp.where(mask, a, jnp.zeros_like(a))


def _pair_operands(q_ref, k_ref, kpe_ref, lane, j, ks=slice(None), qs=slice(None)):
    cols = slice(LANES * j, LANES * (j + 1))
    qc = q_ref[qs, cols]
    kj = k_ref[ks, cols]
    kes = []
    for h in (2 * j, 2 * j + 1):
        me, mr = _head_masks(lane, h)
        ke = _masked(me, kj)
        if kpe_ref is not None:
            ke = jnp.concatenate([ke, _masked(mr, kpe_ref[ks, :])], axis=1)
        kes.append(ke)
    if kpe_ref is not None:
        qc = jnp.concatenate([qc, q_ref[qs, HW + LANES * (j // 2):HW + LANES * (j // 2 + 1)]], axis=1)
    return qc, kes


def _tile_variants(bias_t):
    out = {}
    for i, tile in enumerate(np.asarray(bias_t)):
        h = tile.shape[0] // 2
        skip = 1 if (tile[h:, :h] == NEG).all() else 2 if (tile[:h, h:] == NEG).all() else 0
        out[i] = (bool((tile != 0).any()), skip)
    return out


def _tile_parts(blk, skip):
    lo, hi, full = slice(0, blk // 2), slice(blk // 2, blk), slice(0, blk)
    return {0: [(full, full)], 1: [(lo, full), (hi, hi)], 2: [(hi, full), (lo, lo)]}[skip]


class Rider(NamedTuple):
    args: list
    in_specs: list
    out_shape: list
    out_specs: list
    scratch: list
    start: Callable
    finish: Callable
    stages: tuple = ()


def _ride_along(body, ride, n_prefetch, n_in, n_out, n_scratch, n_steps):
    if ride is None:
        return body

    def wrapped(*refs):
        pre, rest = refs[:n_prefetch], refs[n_prefetch:]
        a = n_in
        b = a + len(ride.args)
        c = b + n_out
        d = c + len(ride.out_shape)
        e = d + n_scratch
        mine = (rest[a:b], rest[c:d], rest[e:])
        t = pl.program_id(0)
        pl.when(t == 0)(lambda: ride.start(*mine))
        for at, stage in ride.stages:
            pl.when(t == at)(functools.partial(stage, *mine))
        body(*pre, *rest[:a], *rest[b:c], *rest[d:e])
        pl.when(t == n_steps - 1)(lambda: ride.finish(*mine))

    return wrapped


def _attn_fwd(name, q, k, kpe, vt, bias_t, steps, blk, ride=None, v_token_major=False):
    seq = q.shape[0]
    mla = kpe is not None
    n_steps = int(steps[0].shape[0])
    variants = _tile_variants(bias_t)

    def body(qi_r, ki_r, bi_r, fi_r, la_r, *refs):
        if mla:
            q_ref, k_ref, kpe_ref, vt_ref, b_ref, o_ref, lse_ref, m_sc, l_sc, acc_sc, st_sc = refs
        else:
            q_ref, k_ref, vt_ref, b_ref, o_ref, lse_ref, m_sc, l_sc, acc_sc, st_sc = refs
        t = pl.program_id(0)

        @pl.when(fi_r[t] == 1)
        def _():
            m_sc[...] = jnp.full(m_sc.shape, NEG, F32)
            l_sc[...] = jnp.zeros(l_sc.shape, F32)
            acc_sc[...] = jnp.zeros(acc_sc.shape, F32)

        lane = lax.broadcasted_iota(jnp.int32, (1, LANES), 1)
        if v_token_major:
            vt_all = vt_ref[...].astype(F32).T.astype(BF16)
            vt_rows = lambda rows, ks: vt_all[rows, ks]
        else:
            vt_rows = lambda rows, ks: vt_ref[rows, ks]

        def tile_pass(ks, qs, with_bias):
            nk, nq = ks.stop - ks.start, qs.stop - qs.start
            ones = jnp.ones((16, nk), BF16)

            def pair_scores(j):
                qc, kes = _pair_operands(q_ref, k_ref, kpe_ref if mla else None, lane, j, ks, qs)
                st = lax.dot_general(jnp.concatenate(kes, axis=0), qc, NT, preferred_element_type=F32)
                maxes = []
                for e in range(2):
                    se = st[e * nk:(e + 1) * nk]
                    if with_bias:
                        se = se + b_ref[0, ks, qs]
                    st_sc[j % 2, e * nk:(e + 1) * nk, 0:nq] = se
                    maxes.append(jnp.max(se, axis=0, keepdims=True))
                return maxes

            def softmax_pv(h, col_max):
                st = st_sc[(h // 2) % 2, (h % 2) * nk:(h % 2 + 1) * nk, 0:nq]
                hrow = slice(h, h + 1)
                m_prev = m_sc[hrow, qs]
                m_new = jnp.maximum(m_prev, col_max)
                alpha = jnp.exp2(m_prev - m_new)
                pt = jnp.exp2(st - m_new).astype(BF16)
                m_sc[hrow, qs] = m_new
                rows = slice(64 * h, 64 * h + 64)
                res = jnp.dot(jnp.concatenate([vt_rows(rows, ks), ones], axis=0), pt, preferred_element_type=F32)
                acc_sc[rows, qs] = alpha * acc_sc[rows, qs] + res[:64]
                l_sc[hrow, qs] = alpha * l_sc[hrow, qs] + res[64:65]

            maxes = pair_scores(0)
            for j in range(HEADS // 2):
                cur = maxes
                if j + 1 < HEADS // 2:
                    maxes = pair_scores(j + 1)
                softmax_pv(2 * j, cur[0])
                softmax_pv(2 * j + 1, cur[1])

        def step(with_bias, skip):
            for ks, qs in _tile_parts(blk, skip):
                tile_pass(ks, qs, with_bias)

        for idx, (with_bias, skip) in variants.items():
            if len(variants) == 1:
                step(with_bias, skip)
            else:
                pl.when(bi_r[t] == idx)(functools.partial(step, with_bias, skip))

        @pl.when(la_r[t] == 1)
        def _():
            for h in range(HEADS):
                rows = slice(64 * h, 64 * h + 64)
                acc_sc[rows, :] = acc_sc[rows, :] / l_sc[h:h + 1, :]
            o_ref[...] = acc_sc[...].T
            lse_ref[...] = m_sc[...] + jnp.log2(l_sc[...])

    qmap = lambda t, qi, ki, bi, fi, la: (qi[t], 0)
    kmap = lambda t, qi, ki, bi, fi, la: (ki[t], 0)
    in_specs = [pl.BlockSpec((blk, q.shape[1]), qmap), pl.BlockSpec((blk, HW), kmap)]
    args = [q, k]
    if mla:
        in_specs.append(pl.BlockSpec((blk, LANES), kmap))
        args.append(kpe)
    in_specs += [pl.BlockSpec((blk, HW), kmap) if v_token_major else
                 pl.BlockSpec((HW, blk), lambda t, qi, ki, bi, fi, la: (0, ki[t])),
                 pl.BlockSpec((1, blk, blk), lambda t, qi, ki, bi, fi, la: (bi[t], 0, 0))]
    args += [vt, jnp.asarray(bias_t)]
    out_specs = [pl.BlockSpec((blk, HW), qmap), pl.BlockSpec((HEADS, blk), lambda t, qi, ki, bi, fi, la: (0, qi[t]))]
    out_shape = [jax.ShapeDtypeStruct((seq, HW), F32), jax.ShapeDtypeStruct((HEADS, seq), F32)]
    scratch = [pltpu.VMEM((HEADS, blk), F32), pltpu.VMEM((HEADS, blk), F32),
               pltpu.VMEM((HW, blk), F32), pltpu.VMEM((2, 2 * blk, blk), F32)]
    body = _ride_along(body, ride, 5, len(args), len(out_shape), len(scratch), n_steps)
    if ride is not None:
        args, in_specs = args + ride.args, in_specs + ride.in_specs
        out_specs, out_shape, scratch = out_specs + ride.out_specs, out_shape + ride.out_shape, scratch + ride.scratch
    return _pcall(
        body, name=name,
        grid_spec=pltpu.PrefetchScalarGridSpec(
            num_scalar_prefetch=5, grid=(n_steps,), in_specs=in_specs, out_specs=out_specs, scratch_shapes=scratch),
        out_shape=out_shape,
        compiler_params=_cparams(dimension_semantics=("arbitrary",)),
    )(*steps, *args)


def _attn_bwd(name, q, k, kpe, v, kt, kpet, bias_t, do, lse, dstat, steps, blk, ride=None, single_visit=False):
    assert not (single_visit and kpe is not None) and (kt is not None or single_visit)
    seq = q.shape[0]
    mla = kpe is not None
    qw = q.shape[1]
    n_steps = int(steps[0].shape[0])
    dk_dtype = BF16 if mla else F32
    variants = _tile_variants(bias_t)

    def body(qi_r, ki_r, bi_r, fi_r, la_r, *refs):
        if mla:
            (q_ref, k_ref, kpe_ref, v_ref, kt_ref, kpet_ref, b_ref, do_ref, lse_ref, d_ref,
             dq_ref, dk_ref, dkpe_ref, dv_ref, dk_sc, dkpe_sc, dv_sc, st_sc, dpt_sc) = refs
        else:
            q_ref, k_ref, v_ref, *rest = refs
            kt_ref = rest.pop(0) if kt is not None else None
            b_ref, do_ref, lse_ref, d_ref, dq_out_ref, dk_ref, dv_ref, dk_sc, dv_sc, st_sc, dpt_sc, *rest = rest
            dq_ref = rest[0] if single_visit else dq_out_ref
        t = pl.program_id(0)

        @pl.when(jnp.logical_or(t == 0, single_visit))
        def _():
            dq_ref[...] = jnp.zeros(dq_ref.shape, F32)

        @pl.when(fi_r[t] == 1)
        def _():
            dk_sc[...] = jnp.zeros(dk_sc.shape, F32)
            dv_sc[...] = jnp.zeros(dv_sc.shape, F32)
            if mla:
                dkpe_sc[...] = jnp.zeros(dkpe_sc.shape, F32)

        qi = 0 if single_visit else qi_r[t]
        lane = lax.broadcasted_iota(jnp.int32, (1, LANES), 1)
        if kt is None:
            kt_all = k_ref[...].astype(F32).T.astype(BF16)
            kt_rows = lambda rows, ks: kt_all[rows, ks]
        else:
            kt_rows = lambda rows, ks: kt_ref[rows, ks]

        def tile_pass(ks, qs, with_bias):
            nk, nq = ks.stop - ks.start, qs.stop - qs.start

            def pair_matmuls(j):
                cols = slice(LANES * j, LANES * (j + 1))
                qc, kes = _pair_operands(q_ref, k_ref, kpe_ref if mla else None, lane, j, ks, qs)
                st_sc[j % 2, 0:2 * nk, 0:nq] = lax.dot_general(
                    jnp.concatenate(kes, axis=0), qc, NT, preferred_element_type=F32)
                vj = v_ref[ks, cols]
                ves = [_masked(_head_masks(lane, h)[0], vj) for h in (2 * j, 2 * j + 1)]
                dpt_sc[j % 2, 0:2 * nk, 0:nq] = lax.dot_general(
                    jnp.concatenate(ves, axis=0), do_ref[qs, cols], NT, preferred_element_type=F32)

            def pair_grads(j):
                cols = slice(LANES * j, LANES * (j + 1))
                qj, doj = q_ref[qs, cols], do_ref[qs, cols]
                if mla:
                    qr = q_ref[qs, HW + LANES * (j // 2):HW + LANES * (j // 2 + 1)]
                pts, dsts, qms, doms = [], [], [], []
                for e in range(2):
                    h = 2 * j + e
                    me, mr = _head_masks(lane, h)
                    st = st_sc[j % 2, e * nk:(e + 1) * nk, 0:nq]
                    if with_bias:
                        st = st + b_ref[0, ks, qs]
                    pt = jnp.exp2(st - lse_ref[h:h + 1, qs])
                    dst = (pt * (dpt_sc[j % 2, e * nk:(e + 1) * nk, 0:nq] - d_ref[h:h + 1, qs])).astype(BF16)
                    pts.append(pt.astype(BF16))
                    dsts.append(dst)
                    doms.append(_masked(me, doj))
                    qm = _masked(me, qj)
                    if mla:
                        qm = jnp.concatenate([qm, _masked(mr, qr)], axis=1)
                    qms.append(qm)
                    ktl = kt_rows(slice(64 * h, 64 * h + 64), ks)
                    if mla:
                        ktl = jnp.concatenate([ktl, kpet_ref[:, ks]], axis=0)
                    dqc = jnp.dot(ktl, dst, preferred_element_type=F32)
                    dq_ref[qi, 64 * h:64 * h + 64, qs] += dqc[:64]
                    if mla:
                        dq_ref[qi, HW + MLA_ROPE * h:HW + MLA_ROPE * (h + 1), qs] += dqc[64:]
                dv_sc[ks, cols] += jnp.dot(
                    jnp.concatenate(pts, axis=1), jnp.concatenate(doms, axis=0), preferred_element_type=F32)
                dkc = jnp.dot(jnp.concatenate(dsts, axis=1), jnp.concatenate(qms, axis=0), preferred_element_type=F32)
                dk_sc[ks, cols] += dkc[:, :LANES]
                if mla:
                    dkpe_sc[ks, :] += dkc[:, LANES:]

            pair_matmuls(0)
            for j in range(HEADS // 2):
                if j + 1 < HEADS // 2:
                    pair_matmuls(j + 1)
                pair_grads(j)

        def step(with_bias, skip):
            for ks, qs in _tile_parts(blk, skip):
                tile_pass(ks, qs, with_bias)

        for idx, (with_bias, skip) in variants.items():
            if len(variants) == 1:
                step(with_bias, skip)
            else:
                pl.when(bi_r[t] == idx)(functools.partial(step, with_bias, skip))
        if single_visit:
            dq_out_ref[...] = dq_ref[0].T

        @pl.when(la_r[t] == 1)
        def _():
            dk_ref[...] = (dk_sc[...] * LN2).astype(dk_ref.dtype)
            dv_ref[...] = dv_sc[...].astype(dv_ref.dtype)
            if mla:
                dkpe_ref[...] = dkpe_sc[...] * LN2

    qmap = lambda t, qi, ki, bi, fi, la: (qi[t], 0)
    kmap = lambda t, qi, ki, bi, fi, la: (ki[t], 0)
    qmap_t = lambda t, qi, ki, bi, fi, la: (0, qi[t])
    kmap_t = lambda t, qi, ki, bi, fi, la: (0, ki[t])
    in_specs = [pl.BlockSpec((blk, qw), qmap), pl.BlockSpec((blk, HW), kmap)]
    args = [q, k]
    if mla:
        in_specs.append(pl.BlockSpec((blk, LANES), kmap))
        args.append(kpe)
    in_specs.append(pl.BlockSpec((blk, HW), kmap))
    args.append(v)
    if kt is not None:
        in_specs.append(pl.BlockSpec((HW, blk), kmap_t))
        args.append(kt)
    if mla:
        in_specs.append(pl.BlockSpec((MLA_ROPE, blk), kmap_t))
        args.append(kpet)
    in_specs += [pl.BlockSpec((1, blk, blk), lambda t, qi, ki, bi, fi, la: (bi[t], 0, 0)),
                 pl.BlockSpec((blk, HW), qmap), pl.BlockSpec((HEADS, blk), qmap_t), pl.BlockSpec((HEADS, blk), qmap_t)]
    args += [jnp.asarray(bias_t), do, lse, dstat]
    dq_shape = (seq // blk, qw, blk)
    if single_visit:
        out_specs, out_shape = [pl.BlockSpec((blk, qw), qmap)], [jax.ShapeDtypeStruct((seq, qw), F32)]
    else:
        out_specs = [pl.BlockSpec(dq_shape, lambda t, qi, ki, bi, fi, la: (0, 0, 0))]
        out_shape = [jax.ShapeDtypeStruct(dq_shape, F32)]
    out_specs.append(pl.BlockSpec((blk, HW), kmap))
    out_shape.append(jax.ShapeDtypeStruct((seq, HW), dk_dtype))
    scratch = [pltpu.VMEM((blk, HW), F32)]
    if mla:
        out_specs.append(pl.BlockSpec((blk, LANES), kmap))
        out_shape.append(jax.ShapeDtypeStruct((seq, LANES), F32))
        scratch.append(pltpu.VMEM((blk, LANES), F32))
    out_specs.append(pl.BlockSpec((blk, HW), kmap))
    out_shape.append(jax.ShapeDtypeStruct((seq, HW), BF16))
    scratch.append(pltpu.VMEM((blk, HW), F32))
    scratch += [pltpu.VMEM((2, 2 * blk, blk), F32), pltpu.VMEM((2, 2 * blk, blk), F32)]
    if single_visit:
        scratch.append(pltpu.VMEM((1, qw, blk), F32))
    body = _ride_along(body, ride, 5, len(args), len(out_shape), len(scratch), n_steps)
    if ride is not None:
        args, in_specs = args + ride.args, in_specs + ride.in_specs
        out_specs, out_shape, scratch = out_specs + ride.out_specs, out_shape + ride.out_shape, scratch + ride.scratch
    return _pcall(
        body, name=name,
        grid_spec=pltpu.PrefetchScalarGridSpec(
            num_scalar_prefetch=5, grid=(n_steps,), in_specs=in_specs, out_specs=out_specs,
            scratch_shapes=scratch),
        out_shape=out_shape,
        compiler_params=_cparams(dimension_semantics=("arbitrary",)),
    )(*steps, *args)


def _out_ln(oa, ob_near, ob_far, lse_near, lse_far, ga, gb, x, tgt, w_out, ln_g, ln_b, bt):
    seq = x.shape[0]

    def body(oa_ref, obn_ref, obf_ref, lsen_ref, lsef_ref, ga_ref, gb_ref, x_ref, tgt_ref, w_ref, g_ref, b_ref,
             dz_ref, doa_ref, dob_ref, dga_ref, dgb_ref, da_ref, db_ref, lse_ref, gwb_ref, small_ref, dobc_ref,
             gw_ref, lanes_sc):
        i = pl.program_id(0)

        @pl.when(i == 0)
        def _():
            gw_ref[...] = jnp.zeros(gw_ref.shape, F32)
            small_ref[...] = jnp.zeros(small_ref.shape, F32)

        def gate(g):
            sig = 0.5 * jnp.tanh(0.5 * g) + 0.5
            return g * sig, sig * (1.0 + g * (1.0 - sig))

        lse_n, lse_f = lsen_ref[...], lsef_ref[...]
        top = jnp.maximum(lse_n, lse_f)
        e_n, e_f = jnp.exp2(lse_n - top), jnp.exp2(lse_f - top)
        lse_ref[...] = top + jnp.log2(e_n + e_f)
        inv = 1.0 / (e_n + e_f)
        head_row = lax.broadcasted_iota(jnp.int32, (2 * HEADS, HW), 0) % HEADS
        spread = (head_row == lax.broadcasted_iota(jnp.int32, (2 * HEADS, HW), 1) // 64).astype(BF16)

        def per_lane(w):
            hi = w.astype(BF16)
            lo = (w - hi.astype(F32)).astype(BF16)
            return lax.dot_general(jnp.concatenate([hi, lo], axis=0), spread, TN, preferred_element_type=F32)

        o_a = oa_ref[...]
        o_b = per_lane(e_n * inv) * obn_ref[...] + per_lane(e_f * inv) * _in_sequence(obf_ref, lanes_sc)
        g_a, g_b = ga_ref[...], gb_ref[...]
        sa, dsa = gate(g_a)
        sb, dsb = gate(g_b)
        mix = jnp.concatenate([o_a * sa, o_b * sb], axis=1).astype(BF16)
        z = ALPHA * x_ref[...] + jnp.dot(mix, w_ref[...], preferred_element_type=F32)
        mu = jnp.mean(z, axis=1, keepdims=True)
        zc = z - mu
        rstd = lax.rsqrt(jnp.mean(zc * zc, axis=1, keepdims=True) + LN_EPS)
        xhat = zc * rstd
        gam = g_ref[...]
        diff = xhat * gam + b_ref[...] - tgt_ref[...]
        dy = diff * (1.0 / D_MODEL)
        small_ref[0:1, :] += jnp.sum(dy * xhat, axis=0, keepdims=True)
        small_ref[1:2, :] += jnp.sum(dy, axis=0, keepdims=True)
        small_ref[2:3, :] += jnp.sum(diff * diff, axis=0, keepdims=True)
        dxh = dy * gam
        dz = rstd * (dxh - jnp.mean(dxh, axis=1, keepdims=True) - xhat * jnp.mean(dxh * xhat, axis=1, keepdims=True))
        dz_ref[...] = dz
        dzb = dz.astype(BF16)
        gw_ref[...] += lax.dot_general(mix, dzb, TN, preferred_element_type=F32)

        @pl.when(i == seq // bt - 1)
        def _():
            gwb_ref[...] = gw_ref[...].astype(BF16)

        dmix = lax.dot_general(dzb, w_ref[...], NT, preferred_element_type=F32)
        doa, dob = dmix[:, :HW] * sa, dmix[:, HW:] * sb
        doa_ref[...] = doa.astype(BF16)
        dob_ref[...] = dob.astype(BF16)
        _by_class(dob, dobc_ref, lanes_sc)
        dga_ref[...] = (dmix[:, :HW] * o_a * dsa).astype(BF16)
        dgb_ref[...] = (dmix[:, HW:] * o_b * dsb).astype(BF16)
        head_of = (lax.broadcasted_iota(jnp.int32, (2 * HW, LANES), 0) % HW) // 64
        ind = (head_of == lax.broadcasted_iota(jnp.int32, (2 * HW, LANES), 1)).astype(BF16)

        def head_sums(prod):
            hi = prod.astype(BF16)
            lo = (prod - hi.astype(F32)).astype(BF16)
            sums = jnp.dot(jnp.concatenate([hi, lo], axis=1), ind, preferred_element_type=F32)
            return sums.T[:HEADS, :]

        da_ref[...] = head_sums(doa * o_a)
        db_ref[...] = head_sums(dob * o_b)

    def tok(width):
        return pl.BlockSpec((bt, width), lambda i: (i, 0))

    def full(shape):
        return pl.BlockSpec(shape, lambda i: (0,) * len(shape))

    stat = pl.BlockSpec((HEADS, bt), lambda i: (0, i))
    n_cls = ob_far.shape[0]
    by_class = pl.BlockSpec((n_cls, bt // n_cls, HW), lambda i: (0, i, 0))
    return _pcall(
        body, name="out_ln", grid=(seq // bt,),
        in_specs=[tok(HW), tok(HW), by_class, stat, stat, tok(HW), tok(HW), tok(D_MODEL), tok(D_MODEL),
                  full((D_MODEL, D_MODEL)), full((1, D_MODEL)), full((1, D_MODEL))],
        out_specs=[tok(D_MODEL), tok(HW), tok(HW), tok(HW), tok(HW), stat, stat, stat,
                   full((D_MODEL, D_MODEL)), full((8, D_MODEL)), by_class],
        out_shape=[jax.ShapeDtypeStruct((seq, D_MODEL), F32)] + [jax.ShapeDtypeStruct((seq, HW), BF16)] * 4
        + [jax.ShapeDtypeStruct((HEADS, seq), F32)] * 3
        + [jax.ShapeDtypeStruct((D_MODEL, D_MODEL), BF16), jax.ShapeDtypeStruct((8, D_MODEL), F32),
           jax.ShapeDtypeStruct(ob_far.shape, BF16)],
        scratch_shapes=[pltpu.VMEM((D_MODEL, D_MODEL), F32), pltpu.VMEM((HW // LANES, bt, LANES), F32)],
        compiler_params=_cparams(dimension_semantics=("arbitrary",)),
    )(oa, ob_near, ob_far, lse_near, lse_far, ga, gb, x, tgt, w_out, ln_g, ln_b)


def _bwd_mid(dq_m, dkn, dv, dkpe, dqb, dkb, dvb, far, dga, dgb, cq, ckv, qn, kvn, w_uq_r, w_ukv_r, qg, kvg, tabs, bt):
    n_cls = far[0].shape[0]
    seq = cq.shape[0]

    def body(dqm_ref, dkn_ref, dv_ref, dkpe_ref, dqb_ref, dkb_ref, dvb_ref, dqf_ref, dkf_ref, dvf_ref, dga_ref, dgb_ref,
             cq_ref, ckv_ref, qn_ref, kvn_ref, wuq_ref, wukv_ref, qg_ref, kvg_ref, tab_ref,
             dh_ref, guq3_ref, gukv3_ref, small_ref, seq_sc, guq_ref, gukv_ref):
        i = pl.program_id(0)

        @pl.when(i == 0)
        def _():
            guq_ref[...] = jnp.zeros(guq_ref.shape, F32)
            gukv_ref[...] = jnp.zeros(gukv_ref.shape, F32)
            small_ref[...] = jnp.zeros(small_ref.shape, F32)

        m_tabs = (tab_ref[0], tab_ref[1], tab_ref[2])
        d_tabs = (tab_ref[3], tab_ref[4], tab_ref[5])

        def rms_bwd(c, dn, gain):
            r = lax.rsqrt(jnp.mean(c * c, axis=1, keepdims=True) + RMS_EPS)
            u = dn * gain
            dc = r * u - c * (r * r * r) * jnp.mean(u * c, axis=1, keepdims=True)
            return dc, jnp.sum(dn * c * r, axis=0, keepdims=True)

        dqm = dqm_ref[0].T
        dq = jnp.concatenate(
            [dqm[:, :HW], _rope_wide(_rope_t, dqm[:, HW:], *m_tabs, MLA_ROPE // 2)], axis=1) * MLA_SCALE
        dq = dq.astype(BF16)
        dkv = jnp.concatenate([dkn_ref[...], dv_ref[...]], axis=1)
        guq_ref[...] += lax.dot_general(qn_ref[...], dq, TN, preferred_element_type=F32)
        dqn = lax.dot_general(dq, wuq_ref[...], NT, preferred_element_type=F32)
        gukv_ref[...] += lax.dot_general(kvn_ref[...], dkv, TN, preferred_element_type=F32)
        dkvn = lax.dot_general(dkv, wukv_ref[...], NT, preferred_element_type=F32)

        dh_ref[:, C_KR:C_GA] = _rope_t(dkpe_ref[...], *m_tabs, MLA_ROPE // 2).astype(BF16)
        dh_ref[:, C_GA:C_QB] = dga_ref[...]
        in_sequence = functools.partial(_in_sequence, lanes_sc=seq_sc)
        dqb = dqb_ref[0].T + in_sequence(dqf_ref)
        dh_ref[:, C_QB:C_KB] = (_rope_wide(_rope_t, dqb, *d_tabs, DIL_ROT // 2) * DIL_SCALE).astype(BF16)
        dkb = dkb_ref[...] + in_sequence(dkf_ref)
        dh_ref[:, C_KB:C_VB] = _rope_wide(_rope_t, dkb, *d_tabs, DIL_ROT // 2).astype(BF16)
        dh_ref[:, C_VB:C_GB] = (dvb_ref[...].astype(F32) + in_sequence(dvf_ref)).astype(BF16)
        dh_ref[:, C_GB:C_END] = dgb_ref[...]

        dcq, gq = rms_bwd(cq_ref[...], dqn, qg_ref[...])
        small_ref[0:1, :] += gq
        dckv, gkv = rms_bwd(ckv_ref[...], dkvn, kvg_ref[...])
        small_ref[1:2, :KV_RANK] += gkv
        dh_ref[:, C_CQ:C_CKV] = dcq.astype(BF16)
        dh_ref[:, C_CKV:C_KR] = dckv.astype(BF16)

        @pl.when(i == seq // bt - 1)
        def _():
            for h in range(HEADS):
                guq3_ref[h] = jnp.concatenate(
                    [guq_ref[:, MLA_NOPE * h:MLA_NOPE * (h + 1)],
                     guq_ref[:, HW + MLA_ROPE * h:HW + MLA_ROPE * (h + 1)]], axis=1).astype(BF16)
                gukv3_ref[h] = jnp.concatenate(
                    [gukv_ref[:, MLA_NOPE * h:MLA_NOPE * (h + 1)],
                     gukv_ref[:, HW + MLA_V * h:HW + MLA_V * (h + 1)]], axis=1).astype(BF16)

    def tok(width):
        return pl.BlockSpec((bt, width), lambda i: (i, 0))

    def tok_t(a):
        per = a.shape[2] // bt
        return pl.BlockSpec((1, a.shape[1], bt), lambda i: (i // per, 0, i % per))

    def full(shape):
        return pl.BlockSpec(shape, lambda i: (0,) * len(shape))

    by_class = pl.BlockSpec((n_cls, bt // n_cls, HW), lambda i: (0, i, 0))
    uq3 = (HEADS, Q_RANK, MLA_NOPE + MLA_ROPE)
    ukv3 = (HEADS, KV_RANK, MLA_NOPE + MLA_V)
    return _pcall(
        body, name="bwd_mid", grid=(seq // bt,),
        in_specs=[tok_t(dq_m), tok(HW), tok(HW), tok(LANES), tok_t(dqb), tok(HW), tok(HW), by_class, by_class, by_class,
                  tok(HW), tok(HW),
                  tok(Q_RANK), tok(KV_RANK), tok(Q_RANK), tok(KV_RANK),
                  full(w_uq_r.shape), full(w_ukv_r.shape), full((1, Q_RANK)), full((1, KV_RANK)),
                  pl.BlockSpec((6, bt, LANES), lambda i: (0, i, 0))],
        out_specs=[tok(C_END), full(uq3), full(ukv3), full((8, Q_RANK))],
        out_shape=[jax.ShapeDtypeStruct((seq, C_END), BF16), jax.ShapeDtypeStruct(uq3, BF16),
                   jax.ShapeDtypeStruct(ukv3, BF16), jax.ShapeDtypeStruct((8, Q_RANK), F32)],
        scratch_shapes=[pltpu.VMEM((HW // LANES, bt, LANES), F32), pltpu.VMEM(w_uq_r.shape, F32),
                        pltpu.VMEM(w_ukv_r.shape, F32)],
        compiler_params=_cparams(dimension_semantics=("arbitrary",)),
    )(dq_m, dkn, dv, dkpe, dqb, dkb, dvb, *far, dga, dgb, cq, ckv, qn, kvn, w_uq_r, w_ukv_r, qg, kvg, tabs)


def _grad_x(dz, dh, w_in_r, bt, ride=None):
    seq = dz.shape[0]
    n_steps = seq // bt

    def body(dz_ref, dh_ref, w_ref, gx_ref):
        gx_ref[...] = ALPHA * dz_ref[...] + lax.dot_general(
            dh_ref[...], w_ref[...], NT, preferred_element_type=F32)

    args = [dz, dh, w_in_r]
    in_specs = [pl.BlockSpec((bt, D_MODEL), lambda i: (i, 0)), pl.BlockSpec((bt, C_END), lambda i: (i, 0)),
                pl.BlockSpec(w_in_r.shape, lambda i: (0, 0))]
    out_specs = [pl.BlockSpec((bt, D_MODEL), lambda i: (i, 0))]
    out_shape = [jax.ShapeDtypeStruct((seq, D_MODEL), F32)]
    scratch = []
    body = _ride_along(body, ride, 0, len(args), len(out_shape), 0, n_steps)
    if ride is not None:
        args, in_specs = args + ride.args, in_specs + ride.in_specs
        out_specs, out_shape, scratch = out_specs + ride.out_specs, out_shape + ride.out_shape, ride.scratch
    return _pcall(
        body, name="grad_x", grid=(n_steps,),
        in_specs=in_specs, out_specs=out_specs, out_shape=out_shape, scratch_shapes=scratch,
        compiler_params=_cparams(dimension_semantics=("arbitrary",)),
    )(*args)


def _grad_w_in(x, dh, bt):
    seq = x.shape[0]
    shard = IN_WIDTH // N_DEV
    k_lo, k_hi = IN_SPLITS[0] + IN_SPLITS[1], IN_SPLITS[0] + IN_SPLITS[1] + MLA_ROPE

    def body(x_ref, dh_ref, out_ref, acc):
        i = pl.program_id(0)

        @pl.when(i == 0)
        def _():
            acc[...] = jnp.zeros(acc.shape, F32)

        acc[...] += lax.dot_general(x_ref[...].astype(BF16), dh_ref[...], TN, preferred_element_type=F32)

        @pl.when(i == seq // bt - 1)
        def _():
            kr = acc[:, C_KR:C_GA]
            kr = kr + pltpu.roll(kr, 96, 1) + pltpu.roll(kr, 64, 1) + pltpu.roll(kr, 32, 1)
            for d in range(N_DEV):
                lo, hi = shard * d, shard * (d + 1)
                pieces = []
                if lo < k_lo:
                    pieces.append(acc[:, lo:min(hi, k_lo)])
                if lo < k_hi and hi > k_lo:
                    pieces.append(kr[:, max(lo, k_lo) - k_lo:min(hi, k_hi) - k_lo])
                if hi > k_hi:
                    shift = C_GA - k_hi
                    pieces.append(acc[:, max(lo, k_hi) + shift:hi + shift])
                blk = pieces[0] if len(pieces) == 1 else jnp.concatenate(pieces, axis=1)
                out_ref[d] = blk.astype(BF16)

    return _pcall(
        body, name="grad_w_in", grid=(seq // bt,),
        in_specs=[pl.BlockSpec((bt, D_MODEL), lambda i: (i, 0)), pl.BlockSpec((bt, C_END), lambda i: (i, 0))],
        out_specs=pl.BlockSpec((N_DEV, D_MODEL, shard), lambda i: (0, 0, 0)),
        out_shape=jax.ShapeDtypeStruct((N_DEV, D_MODEL, shard), BF16),
        scratch_shapes=[pltpu.VMEM((D_MODEL, C_END), F32)],
        compiler_params=_cparams(dimension_semantics=("arbitrary",)),
    )(x, dh)


def _local_step(x, tgt, w_in_r, w_uq_r, w_ukv_r, w_out_rider, g_out_rider, reduce_rider, q_norm_g, kv_norm_g,
                ln_g, ln_b, bt=BLOCK_TOKENS, blk_m=BLOCK_MLA, blk_d=BLOCK_DIL):
    seq = x.shape[0]
    tabs = jnp.asarray(_rope_tables(seq))
    qg, kvg = q_norm_g.reshape(1, -1), kv_norm_g.reshape(1, -1)

    far_dil = DIL_CONFIGS[-1][1]
    cls = seq // far_dil
    (cq, ckv, qn, kvn, qcat, kn, kpe, v, ga, gb, qb, kb, vb, knt, kpet, vt, kbt, vbt, qb_c, kb_c, vb_c) = _fwd_proj(
        x, w_in_r, w_uq_r, w_ukv_r, qg, kvg, tabs, bt, far_dil)
    qb_c, kb_c, vb_c = (a.reshape(seq, HW) for a in (qb_c, kb_c, vb_c))

    nq_m, nq_d = seq // blk_m, seq // blk_d
    bias_m = _mla_bias_t(blk_m)
    oa, lse_a, w_out = _attn_fwd(
        "mla_fwd", qcat, kn, kpe, vt, bias_m, _steps(nq_m, nq_m, False, True), blk_m, ride=w_out_rider)

    bias_near = _dil_bias_t(blk_d, DIL_NEAR)
    ob_near, lse_near = _attn_fwd(
        "dil_fwd", qb, kb, None, vbt, bias_near, _steps(nq_d, -(-DIL_NEAR // blk_d), False, False), blk_d)
    each = jnp.arange(far_dil, dtype=jnp.int32)
    steps_far = [each, each, jnp.zeros_like(each), jnp.ones_like(each), jnp.ones_like(each)]
    bias_far = _dil_far_bias_t(cls)
    ob_far, lse_far = _attn_fwd(
        "dil_far_fwd", qb_c, kb_c, None, vb_c, bias_far, steps_far, cls, v_token_major=True)

    dz, doa, dob, dga, dgb, dst_a, dst_b, lse_b, g_out, small1, dob_c = _out_ln(
        oa, ob_near, ob_far.reshape(far_dil, cls, HW), lse_near, _lanes_from_classes(lse_far, far_dil), ga, gb, x, tgt,
        w_out.reshape(D_MODEL, D_MODEL), ln_g.reshape(1, -1), ln_b.reshape(1, -1), bt)

    dq_m, dkn, dkpe, dv, g_out_recv = _attn_bwd(
        "mla_bwd", qcat, kn, kpe, v, knt, kpet, bias_m, doa, lse_a, dst_a, _steps(nq_m, nq_m, True, True), blk_m,
        ride=g_out_rider(g_out.reshape(N_DEV, D_MODEL // N_DEV, D_MODEL)))
    dqb, dkb_near, dvb_near = _attn_bwd(
        "dil_bwd", qb, kb, None, vb, kbt, None, bias_near, dob, lse_b, dst_b,
        _steps(nq_d, -(-DIL_NEAR // blk_d), True, False), blk_d)
    dqb_far, dkb_far, dvb_far = _attn_bwd(
        "dil_far_bwd", qb_c, kb_c, None, vb_c, None, None, bias_far, dob_c.reshape(seq, HW),
        _lanes_to_classes(lse_b, far_dil), _lanes_to_classes(dst_b, far_dil), steps_far, cls, single_visit=True)
    far = [a.reshape(far_dil, cls, HW) for a in (dqb_far, dkb_far, dvb_far)]

    dh, g_uq, g_ukv, small2 = _bwd_mid(
        dq_m, dkn, dv, dkpe, dqb, dkb_near, dvb_near, far, dga, dgb, cq, ckv, qn, kvn, w_uq_r, w_ukv_r, qg, kvg, tabs, bt)
    g_in = _grad_w_in(x, dh, min(seq, 2 * bt))
    grads3 = [g_in, g_uq, g_ukv]
    small_part = _small_rows(small1[0], small1[1], small2[0, :Q_RANK], small2[1, :KV_RANK], small1[2])
    grad_x, *reduced = _grad_x(
        dz, dh, w_in_r, bt, ride=reduce_rider(grads3, g_out_recv, small_part, min(1, seq // bt - 1)))
    return grad_x, reduced


MESH_ID = pl.DeviceIdType.MESH
SHARD_SHAPES = ((D_MODEL, IN_WIDTH // N_DEV), (Q_RANK, 768 // N_DEV), (KV_RANK, 1024 // N_DEV), (D_MODEL // N_DEV, D_MODEL))
ADAM_ROWS = (32, 128, 128, 16)


def _me():
    x, y, c = lax.axis_index("x"), lax.axis_index("y"), lax.axis_index("c")
    return x, y, c, 4 * x + 2 * y + c


def _peer(k):
    x, y, c, _ = _me()
    px = 1 - x if (k >> 2) & 1 else x
    py = 1 - y if (k >> 1) & 1 else y
    pc = 1 - c if k & 1 else c
    return (px, py, pc), 4 * px + 2 * py + pc


def _all_gather_weights(shards):
    n = len(shards)
    shard = IN_WIDTH // N_DEV
    k_lo = IN_SPLITS[0] + IN_SPLITS[1]
    k_hi = k_lo + MLA_ROPE

    def body(*refs):
        ins = refs[:n]
        win_ref, wuq_ref, wukv_ref = refs[n:2 * n]
        bufs = refs[2 * n:3 * n]
        send_sems, recv_sems = refs[3 * n:]
        x, y, c, me = _me()
        here, sibling = (x, y, c), (x, y, 1 - c)
        chips = [(1 - x, y), (x, 1 - y), (1 - x, 1 - y)]
        for t in range(n):
            bufs[t][me] = ins[t][...].astype(BF16)

        def copy(t, k, px, py, pc, to):
            blk = bufs[t].at[4 * px + 2 * py + pc]
            return pltpu.make_async_remote_copy(
                src_ref=blk, dst_ref=blk, send_sem=send_sems.at[t, k], recv_sem=recv_sems.at[t, k],
                device_id=to, device_id_type=MESH_ID)

        first = []
        for t in range(n):
            first.append(copy(t, 0, x, y, c, sibling))
            for j, (px, py) in enumerate(chips):
                first.append(copy(t, 1 + j, x, y, c, (px, py, c)))
        for cp in first:
            cp.start()
        passed = []
        for j, (px, py) in enumerate(chips):
            for t in range(n):
                copy(t, 1 + j, px, py, c, here).wait_recv()
                cp = copy(t, 4 + j, px, py, c, sibling)
                cp.start()
                passed.append(cp)
        for t in range(n):
            copy(t, 0, x, y, 1 - c, here).wait_recv()
        for j, (px, py) in enumerate(chips):
            for t in range(n):
                copy(t, 4 + j, px, py, 1 - c, here).wait_recv()
        for cp in first + passed:
            cp.wait_send()

        a_in, a_uq, a_ukv = bufs
        for d in range(N_DEV):
            lo, hi = shard * d, shard * (d + 1)
            if lo < k_lo:
                win_ref[:, lo:min(hi, k_lo)] = a_in[d, :, 0:min(hi, k_lo) - lo]
            if lo < k_hi and hi > k_lo:
                kr = a_in[d, :, k_lo - lo:k_hi - lo]
                for rep in range(4):
                    win_ref[:, C_KR + MLA_ROPE * rep:C_KR + MLA_ROPE * (rep + 1)] = kr
            if hi > k_hi:
                src = max(lo, k_hi)
                win_ref[:, src + C_GA - k_hi:hi + C_GA - k_hi] = a_in[d, :, src - lo:hi - lo]
        for h in range(HEADS):
            wuq_ref[:, MLA_NOPE * h:MLA_NOPE * (h + 1)] = a_uq[h, :, :MLA_NOPE]
            wuq_ref[:, HW + MLA_ROPE * h:HW + MLA_ROPE * (h + 1)] = a_uq[h, :, MLA_NOPE:]
            wukv_ref[:, MLA_NOPE * h:MLA_NOPE * (h + 1)] = a_ukv[h, :, :MLA_NOPE]
            wukv_ref[:, HW + MLA_V * h:HW + MLA_V * (h + 1)] = a_ukv[h, :, MLA_NOPE:]

    vmem = pl.BlockSpec(memory_space=pltpu.VMEM)
    return _pcall(
        body, name="gather_weights",
        in_specs=[vmem] * n, out_specs=[vmem] * n,
        out_shape=[jax.ShapeDtypeStruct((D_MODEL, C_END), BF16), jax.ShapeDtypeStruct((Q_RANK, QW), BF16),
                   jax.ShapeDtypeStruct((KV_RANK, 2 * HW), BF16)],
        scratch_shapes=[pltpu.VMEM((N_DEV,) + s, BF16) for s in SHARD_SHAPES[:n]]
        + [pltpu.SemaphoreType.DMA((n, N_DEV - 1)), pltpu.SemaphoreType.DMA((n, N_DEV - 1))],
        compiler_params=_cparams(),
    )(*shards)


def _gather_w_out_rider(w_out):
    def copies(full_ref, stage, send_sems, recv_sems):
        me = _me()[3]
        out = []
        for k in range(1, N_DEV):
            peer, pidx = _peer(k)
            send = pltpu.make_async_remote_copy(
                src_ref=stage, dst_ref=full_ref.at[me], send_sem=send_sems.at[k - 1], recv_sem=recv_sems.at[k - 1],
                device_id=peer, device_id_type=MESH_ID)
            recv = pltpu.make_async_remote_copy(
                src_ref=stage, dst_ref=full_ref.at[pidx], send_sem=send_sems.at[k - 1], recv_sem=recv_sems.at[k - 1],
                device_id=peer, device_id_type=MESH_ID)
            out.append((send, recv))
        return out

    def start(ins, outs, scr):
        stage, send_sems, recv_sems, own_sem = scr
        stage[...] = ins[0][...].astype(BF16)
        pltpu.make_async_copy(stage, outs[0].at[_me()[3]], own_sem).start()
        for send, _ in copies(outs[0], stage, send_sems, recv_sems):
            send.start()

    def finish(ins, outs, scr):
        stage, send_sems, recv_sems, own_sem = scr
        pltpu.make_async_copy(stage, outs[0].at[_me()[3]], own_sem).wait()
        pairs = copies(outs[0], stage, send_sems, recv_sems)
        for _, recv in pairs:
            recv.wait_recv()
        for send, _ in pairs:
            send.wait_send()

    shape = SHARD_SHAPES[3]
    return Rider(
        args=[w_out], in_specs=[pl.BlockSpec(shape, lambda t, *_: (0, 0))],
        out_shape=[jax.ShapeDtypeStruct((N_DEV,) + shape, BF16)], out_specs=[pl.BlockSpec(memory_space=pl.ANY)],
        scratch=[pltpu.VMEM(shape, BF16), pltpu.SemaphoreType.DMA((N_DEV - 1,)), pltpu.SemaphoreType.DMA((N_DEV - 1,)),
                 pltpu.SemaphoreType.DMA],
        start=start, finish=finish)


def _scatter_g_out_rider(blocks):
    def copies(src_ref, dst_ref, send_sems, recv_sems):
        out = []
        for k in range(1, N_DEV):
            peer, pidx = _peer(k)
            out.append(pltpu.make_async_remote_copy(
                src_ref=src_ref.at[pidx], dst_ref=dst_ref.at[k], send_sem=send_sems.at[k - 1],
                recv_sem=recv_sems.at[k - 1], device_id=peer, device_id_type=MESH_ID))
        return out

    def start(ins, outs, scr):
        send_sems, recv_sems, own_sem = scr
        pltpu.make_async_copy(ins[0].at[_me()[3]], outs[0].at[0], own_sem).start()
        for cp in copies(ins[0], outs[0], send_sems, recv_sems):
            cp.start()

    def finish(ins, outs, scr):
        send_sems, recv_sems, own_sem = scr
        pltpu.make_async_copy(ins[0].at[_me()[3]], outs[0].at[0], own_sem).wait()
        for cp in copies(ins[0], outs[0], send_sems, recv_sems):
            cp.wait()

    hbm = pl.BlockSpec(memory_space=pl.ANY)
    return Rider(
        args=[blocks], in_specs=[hbm], out_shape=[jax.ShapeDtypeStruct(blocks.shape, blocks.dtype)], out_specs=[hbm],
        scratch=[pltpu.SemaphoreType.DMA((N_DEV - 1,)), pltpu.SemaphoreType.DMA((N_DEV - 1,)), pltpu.SemaphoreType.DMA],
        start=start, finish=finish)


def _adamw(w, g, m, v):
    m = ADAM_B1 * m + (1.0 - ADAM_B1) * g
    v = ADAM_B2 * v + (1.0 - ADAM_B2) * jnp.square(g)
    m_hat = m / (1.0 - ADAM_B1 ** ADAM_STEP)
    v_hat = v / (1.0 - ADAM_B2 ** ADAM_STEP)
    delta = -ADAM_LR * (m_hat / (jnp.sqrt(v_hat) + ADAM_EPS) + ADAM_WD * w)
    return delta, m, v


def _reduce_grads_rider(grads3, arrived, small_part, mid_step):
    n = len(grads3)

    class Refs:
        def __init__(self, ins, outs, scr):
            self.g3, self.arr, self.sp = ins[0:n], ins[n], ins[n + 1]
            self.gsum, self.gsum_out, self.ssum = outs[0:n], outs[n], outs[n + 1]
            self.own, self.sib, self.part, self.ici = scr[0:n], scr[n:2 * n], scr[2 * n:3 * n], scr[3 * n:4 * n]
            self.rsmall = scr[4 * n]
            (self.loc_sems, self.d2d_send, self.d2d_recv, self.ici_send, self.ici_recv,
             self.sm_send, self.sm_recv) = scr[4 * n + 1:]
            self.x, self.y, self.c, self.me = _me()
            self.chips = [(1 - self.x, self.y), (self.x, 1 - self.y), (1 - self.x, 1 - self.y)]

        def small(self):
            return [pltpu.make_async_remote_copy(
                src_ref=self.rsmall.at[0], dst_ref=self.rsmall.at[k], send_sem=self.sm_send.at[k - 1],
                recv_sem=self.sm_recv.at[k - 1], device_id=_peer(k)[0], device_id_type=MESH_ID)
                for k in range(1, N_DEV)]

        def level1(self):
            local, to_sib = [], []
            for t in range(n):
                for q in range(4):
                    local.append(pltpu.make_async_copy(
                        self.g3[t].at[2 * q + self.c], self.own[t].at[q], self.loc_sems.at[t, q]))
                    to_sib.append(pltpu.make_async_remote_copy(
                        src_ref=self.g3[t].at[2 * q + 1 - self.c], dst_ref=self.sib[t].at[q],
                        send_sem=self.d2d_send.at[t, q], recv_sem=self.d2d_recv.at[t, q],
                        device_id=(self.x, self.y, 1 - self.c), device_id_type=MESH_ID))
            return local, to_sib

        def level2(self):
            return [pltpu.make_async_remote_copy(
                src_ref=self.part[t].at[2 * px + py], dst_ref=self.ici[t].at[j], send_sem=self.ici_send.at[t, j],
                recv_sem=self.ici_recv.at[t, j], device_id=(px, py, self.c), device_id_type=MESH_ID)
                for t in range(n) for j, (px, py) in enumerate(self.chips)]

    def chunks(t, fn):
        rows = ADAM_ROWS[t]

        def step(i, carry):
            fn(pl.ds(pl.multiple_of(i * rows, rows), rows))
            return carry

        lax.fori_loop(0, SHARD_SHAPES[t][0] // rows, step, 0)

    def start(*refs):
        r = Refs(*refs)
        r.rsmall[0] = r.sp[...]
        local, to_sib = r.level1()
        for cp in r.small() + local + to_sib:
            cp.start()

    def middle(*refs):
        r = Refs(*refs)
        local, to_sib = r.level1()
        for cp in local:
            cp.wait()
        for cp in to_sib:
            cp.wait_recv()
        my_chip = 2 * r.x + r.y
        for t in range(n):
            def pair_sums(rows, t=t):
                for q in range(4):
                    r.part[t][q, rows, :] = (
                        r.own[t][q, rows, :].astype(F32) + r.sib[t][q, rows, :].astype(F32)).astype(BF16)
                r.gsum[t][rows, :] = r.own[t][my_chip, rows, :].astype(F32) + r.sib[t][my_chip, rows, :].astype(F32)

            chunks(t, pair_sums)
        for cp in r.level2():
            cp.start()

        def add_arrived(rows):
            g = r.arr[0, rows, :].astype(F32)
            for k in range(1, N_DEV):
                g = g + r.arr[k, rows, :].astype(F32)
            r.gsum_out[rows, :] = g

        chunks(3, add_arrived)

    def finish(*refs):
        r = Refs(*refs)
        to_chips = r.level2()
        for cp in to_chips:
            cp.wait_recv()
        for t in range(n):
            def add_chips(rows, t=t):
                g = r.gsum[t][rows, :]
                for j in range(3):
                    g = g + r.ici[t][j, rows, :].astype(F32)
                r.gsum[t][rows, :] = g

            chunks(t, add_chips)
        small = r.small()
        for cp in small:
            cp.wait_recv()
        tot = r.rsmall[r.me]
        for d in range(1, N_DEV):
            tot = tot + r.rsmall[jnp.bitwise_xor(r.me, d)]
        r.ssum[...] = tot
        for cp in small + r.level1()[1] + to_chips:
            cp.wait_send()

    hbm = pl.BlockSpec(memory_space=pl.ANY)
    dma = pltpu.SemaphoreType.DMA

    def whole(shape):
        return pl.BlockSpec(shape, lambda i: (0,) * len(shape))

    out_shapes = list(SHARD_SHAPES) + [(8, D_MODEL)]
    return Rider(
        args=list(grads3) + [arrived, small_part],
        in_specs=[hbm] * n + [whole(arrived.shape), whole(small_part.shape)],
        out_shape=[jax.ShapeDtypeStruct(s, F32) for s in out_shapes], out_specs=[whole(s) for s in out_shapes],
        scratch=[pltpu.VMEM((slots,) + s, BF16) for slots in (4, 4, 4, 3) for s in SHARD_SHAPES[:n]]
        + [pltpu.VMEM((N_DEV, 8, D_MODEL), F32), dma((n, 4)), dma((n, 4)), dma((n, 4)), dma((n, 3)), dma((n, 3)),
           dma((N_DEV - 1,)), dma((N_DEV - 1,))],
        start=start, finish=finish, stages=((mid_step, middle),))


def _adamw_update(grads, small_grad, wmv, small_wmv):
    n_small = len(small_wmv)

    def body(*refs):
        g_refs, sg_ref = refs[0:4], refs[4]
        wmv_refs = [refs[5 + 3 * t:8 + 3 * t] for t in range(4)]
        swmv_refs = [refs[17 + 3 * t:20 + 3 * t] for t in range(n_small)]
        outs = refs[17 + 3 * n_small:]
        out_refs = [outs[4 * t:4 * t + 4] for t in range(4)]
        sout_refs = [outs[16 + 4 * t:20 + 4 * t] for t in range(n_small)]
        loss_ref = outs[16 + 4 * n_small]
        for t, (w_ref, m_ref, v_ref) in enumerate(swmv_refs):
            g = sg_ref[t:t + 1, :w_ref.shape[1]]
            delta, m, v = _adamw(w_ref[...], g, m_ref[...], v_ref[...])
            sout_refs[t][0][...], sout_refs[t][1][...], sout_refs[t][2][...], sout_refs[t][3][...] = g, delta, m, v
        loss_ref[...] = (0.5 / D_MODEL) * jnp.sum(sg_ref[n_small:n_small + 1, :], axis=1, keepdims=True)
        for t in range(4):
            rows = ADAM_ROWS[t]
            w_ref, m_ref, v_ref = wmv_refs[t]
            g_out, d_out, m_out, v_out = out_refs[t]

            def step(i, carry, g_ref=g_refs[t], rows=rows, w_ref=w_ref, m_ref=m_ref, v_ref=v_ref,
                     g_out=g_out, d_out=d_out, m_out=m_out, v_out=v_out):
                r = pl.ds(pl.multiple_of(i * rows, rows), rows)
                g = g_ref[r, :]
                delta, m, v = _adamw(w_ref[r, :], g, m_ref[r, :], v_ref[r, :])
                g_out[r, :], d_out[r, :], m_out[r, :], v_out[r, :] = g, delta, m, v
                return carry

            lax.fori_loop(0, SHARD_SHAPES[t][0] // rows, step, 0)

    vmem = pl.BlockSpec(memory_space=pltpu.VMEM)
    flat_wmv = [a for trio in wmv for a in trio]
    flat_small = [a for trio in small_wmv for a in trio]
    out_shape = ([jax.ShapeDtypeStruct(s, F32) for s in SHARD_SHAPES for _ in range(4)]
                 + [jax.ShapeDtypeStruct(trio[0].shape, F32) for trio in small_wmv for _ in range(4)]
                 + [jax.ShapeDtypeStruct((1, 1), F32)])
    return _pcall(
        body, name="adamw",
        in_specs=[vmem] * (5 + len(flat_wmv) + len(flat_small)), out_specs=[vmem] * len(out_shape),
        out_shape=out_shape,
        compiler_params=_cparams(),
    )(*grads, small_grad, *flat_wmv, *flat_small)


def _small_rows(ln_g, ln_b, q_norm_g, kv_norm_g, extra=None):
    pad = lambda a: jnp.pad(a, (0, D_MODEL - a.shape[0]))
    rows = [ln_g, ln_b, pad(q_norm_g), pad(kv_norm_g)] + ([] if extra is None else [extra])
    return jnp.pad(jnp.stack(rows), ((0, 8 - len(rows)), (0, 0)))


def kernel(x, w_in, q_norm_g, kv_norm_g, w_uq, w_ukv, w_out, ln_g, ln_b, loss_target, m_w_in, m_q_norm_g, m_kv_norm_g, m_w_uq, m_w_ukv, m_w_out, m_ln_g, m_ln_b, v_w_in, v_q_norm_g, v_kv_norm_g, v_w_uq, v_w_ukv, v_w_out, v_ln_g, v_ln_b):
    w_in_r, w_uq_r, w_ukv_r = _all_gather_weights([w_in, w_uq, w_ukv])
    grad_x, sums = _local_step(
        x[0], loss_target[0], w_in_r, w_uq_r, w_ukv_r, _gather_w_out_rider(w_out), _scatter_g_out_rider,
        _reduce_grads_rider, q_norm_g, kv_norm_g, ln_g, ln_b)
    row = lambda a: a.reshape(1, -1)
    small_wmv = [(row(ln_g), row(m_ln_g), row(v_ln_g)), (row(ln_b), row(m_ln_b), row(v_ln_b)),
                 (row(q_norm_g), row(m_q_norm_g), row(v_q_norm_g)), (row(kv_norm_g), row(m_kv_norm_g), row(v_kv_norm_g))]
    wmv = [(w_in, m_w_in, v_w_in), (w_uq, m_w_uq, v_w_uq), (w_ukv, m_w_ukv, v_w_ukv), (w_out, m_w_out, v_w_out)]
    res = _adamw_update(sums[:4], sums[4], wmv, small_wmv)
    big = [res[4 * t:4 * t + 4] for t in range(4)]
    small = [[a.reshape(-1) for a in res[16 + 4 * t:20 + 4 * t]] for t in range(4)]
    loss = res[32].reshape(())

    def group(kind):
        return (big[0][kind], small[2][kind], small[3][kind], big[1][kind], big[2][kind], big[3][kind],
                small[0][kind], small[1][kind])

    return (loss, grad_x[None], *group(0), *group(1), *group(2), *group(3))
```

```python
import functools
from typing import Callable, NamedTuple

import numpy as np
import jax
import jax.numpy as jnp
from jax import lax
from jax.experimental import pallas as pl
from jax.experimental.pallas import tpu as pltpu

F32 = jnp.float32
BF16 = jnp.bfloat16

D_MODEL = 1024
ROPE_THETA = 500000.0
NEG = -1e30
RMS_EPS = 1e-6
LN_EPS = 1e-5
HEADS = 8
MLA_NOPE = 64
MLA_ROPE = 32
MLA_V = 64
Q_RANK = 384
KV_RANK = 256
DIL_HEAD = 64
DIL_ROT = 16
DIL_CONFIGS = ((128, 1), (512, 4), (2048, 16))
DIL_NEAR = 512
HW = HEADS * 64
QW = HW + HEADS * MLA_ROPE
IN_SPLITS = (Q_RANK, KV_RANK, MLA_ROPE, HW, HW, HW, HW, HW)
IN_WIDTH = sum(IN_SPLITS)
ALPHA = 2.0 ** 0.25
MLA_SCALE = (MLA_NOPE + MLA_ROPE) ** -0.5
DIL_SCALE = DIL_HEAD ** -0.5
LOG2E = 1.4426950408889634
LN2 = 0.6931471805599453

ADAM_LR = 0.001
ADAM_B1 = 0.9
ADAM_B2 = 0.999
ADAM_EPS = 1e-08
ADAM_WD = 0.01
ADAM_STEP = 10

N_DEV = 8
LANES = 128
VMEM_LIMIT = 56 * 1024 * 1024
BLOCK_TOKENS = 512
BLOCK_MLA = 512
BLOCK_DIL = 512

C_CQ, C_CKV, C_KR, C_GA, C_QB, C_KB, C_VB, C_GB, C_END = 0, 384, 640, 768, 1280, 1792, 2304, 2816, 3328

NT = (((1,), (1,)), ((), ()))
TN = (((0,), (0,)), ((), ()))


def _pcall(body, **kw):
    return pl.pallas_call(body, **kw)


def _cparams(**kw):
    return pltpu.CompilerParams(vmem_limit_bytes=VMEM_LIMIT, **kw)


def _rope_tables(seq):
    def tabs(dim, period):
        half = dim // 2
        inv = np.float32(ROPE_THETA) ** (-np.arange(0, dim, 2, dtype=np.float32) / np.float32(dim))
        ang = np.arange(seq, dtype=np.float32)[:, None] * inv.astype(np.float32)[None, :]
        cos, sin = np.cos(ang).astype(np.float32), np.sin(ang).astype(np.float32)
        j = np.arange(LANES) % period
        f = j % half
        c = np.where(j < dim, cos[:, f], np.float32(1.0))
        s1 = np.where(j < half, -sin[:, f], np.float32(0.0))
        s2 = np.where((j >= half) & (j < dim), sin[:, f], np.float32(0.0))
        return [c, s1, s2]
    return np.stack(tabs(MLA_ROPE, MLA_ROPE) + tabs(DIL_ROT, DIL_HEAD)).astype(np.float32)


def _rope(t, c, s1, s2, half):
    return t * c + pltpu.roll(t, LANES - half, 1) * s1 + pltpu.roll(t, half, 1) * s2


def _rope_t(d, c, s1, s2, half):
    return d * c + pltpu.roll(d * s1, half, 1) + pltpu.roll(d * s2, LANES - half, 1)


def _rope_wide(fn, t, c, s1, s2, half):
    return jnp.concatenate(
        [fn(t[:, i:i + LANES], c, s1, s2, half) for i in range(0, t.shape[1], LANES)], axis=1)


def _mla_bias_t(blk):
    a = np.arange(blk)
    causal = np.where(a[:, None] <= a[None, :], 0.0, NEG)
    return np.stack([np.zeros((blk, blk)), causal]).astype(np.float32)


def _dil_bias_t(blk, reach):
    a = np.arange(blk)
    out = []
    for off in range(-(-reach // blk) + 1):
        delta = blk * off + a[None, :] - a[:, None]
        mult = np.zeros((blk, blk))
        for window, dil in DIL_CONFIGS:
            mult += (delta >= 0) & (delta % dil == 0) & (delta <= min(window, reach))
        out.append(np.where(mult > 0, np.log2(np.maximum(mult, 1.0)), NEG))
    return np.stack(out).astype(np.float32)


def _dil_far_bias_t(length):
    window, dil = DIL_CONFIGS[-1]
    a = np.arange(length)
    steps_back = a[None, :] - a[:, None]
    seen = (steps_back * dil > DIL_NEAR) & (steps_back * dil <= window)
    return np.where(seen, 0.0, NEG).astype(np.float32)[None]


def _lanes_to_classes(a, dil):
    h, s = a.shape
    return a.reshape(h, s // dil, dil).transpose(0, 2, 1).reshape(h, s)


def _lanes_from_classes(a, dil):
    h, s = a.shape
    return a.reshape(h, dil, s // dil).transpose(0, 2, 1).reshape(h, s)


def _steps(nq, span, by_key, diag_only_bias):
    rows = []
    if by_key:
        for ki in range(nq):
            hi = min(nq - 1, ki + span)
            for qi in range(ki, hi + 1):
                rows.append((qi, ki, int(qi == ki), int(qi == hi)))
    else:
        for qi in range(nq):
            lo = max(0, qi - span)
            for ki in range(lo, qi + 1):
                rows.append((qi, ki, int(ki == lo), int(ki == qi)))
    arr = np.array(rows, dtype=np.int32)
    off = arr[:, 0] - arr[:, 1]
    bias_idx = (off == 0).astype(np.int32) if diag_only_bias else off.astype(np.int32)
    return [jnp.asarray(v) for v in (arr[:, 0], arr[:, 1], bias_idx, arr[:, 2], arr[:, 3])]


def _by_class(val, out_ref, lanes_sc):
    n_cls, per = out_ref.shape[0], out_ref.shape[1]
    for c in range(val.shape[1] // LANES):
        lanes_sc[c] = val[:, LANES * c:LANES * (c + 1)]
        for r in range(n_cls):
            rows = lanes_sc.at[c][pl.ds(r, per, stride=n_cls), :]
            out_ref[r, :, LANES * c:LANES * (c + 1)] = rows.astype(out_ref.dtype)


def _in_sequence(ref, lanes_sc):
    n_cls, per, width = ref.shape
    for c in range(width // LANES):
        for r in range(n_cls):
            lanes_sc.at[c][pl.ds(r, per, stride=n_cls), :] = ref[r, :, LANES * c:LANES * (c + 1)].astype(F32)
    return jnp.concatenate([lanes_sc[c] for c in range(width // LANES)], axis=1)


def _fwd_proj(x, w_in_r, w_uq_r, w_ukv_r, qg, kvg, tabs, bt, n_cls):
    seq = x.shape[0]

    def body(x_ref, win_ref, wuq_ref, wukv_ref, qg_ref, kvg_ref, tab_ref,
             cq_ref, ckv_ref, qn_ref, kvn_ref, qcat_ref, kn_ref, kpe_ref, v_ref,
             ga_ref, gb_ref, qb_ref, kb_ref, vb_ref, knt_ref, kpet_ref, vt_ref, kbt_ref, vbt_ref,
             qbc_ref, kbc_ref, vbc_ref, lanes_sc):
        xb = x_ref[...].astype(BF16)

        def proj(lo, hi):
            return jnp.dot(xb, win_ref[:, lo:hi], preferred_element_type=F32)

        m_tabs = (tab_ref[0], tab_ref[1], tab_ref[2])
        d_tabs = (tab_ref[3], tab_ref[4], tab_ref[5])

        def use_cq(cq):
            cq_ref[...] = cq
            qn = (cq * lax.rsqrt(jnp.mean(cq * cq, axis=1, keepdims=True) + RMS_EPS) * qg_ref[...]).astype(BF16)
            qn_ref[...] = qn
            q = jnp.dot(qn, wuq_ref[...], preferred_element_type=F32)
            qcat_ref[:, :HW] = (q[:, :HW] * (MLA_SCALE * LOG2E)).astype(BF16)
            qcat_ref[:, HW:] = (
                _rope_wide(_rope, q[:, HW:], *m_tabs, MLA_ROPE // 2) * (MLA_SCALE * LOG2E)).astype(BF16)

        def use_ckv(ckv):
            ckv_ref[...] = ckv
            kvn = (ckv * lax.rsqrt(jnp.mean(ckv * ckv, axis=1, keepdims=True) + RMS_EPS) * kvg_ref[...]).astype(BF16)
            kvn_ref[...] = kvn
            kv = jnp.dot(kvn, wukv_ref[...], preferred_element_type=F32)
            kn_ref[...] = kv[:, :HW].astype(BF16)
            v_ref[...] = kv[:, HW:].astype(BF16)
            knt_ref[...] = kv[:, :HW].T.astype(BF16)
            vt_ref[...] = kv[:, HW:].T.astype(BF16)

        def use_kr(kr):
            kpe = _rope(kr, *m_tabs, MLA_ROPE // 2)
            kpe_ref[...] = kpe.astype(BF16)
            kpet_ref[...] = kpe.T[:MLA_ROPE, :].astype(BF16)

        def use_ga(ga):
            ga_ref[...] = ga

        def use_qb(qb):
            qb = _rope_wide(_rope, qb, *d_tabs, DIL_ROT // 2) * (DIL_SCALE * LOG2E)
            qb_ref[...] = qb.astype(BF16)
            _by_class(qb, qbc_ref, lanes_sc)

        def use_kb(kb):
            kb = _rope_wide(_rope, kb, *d_tabs, DIL_ROT // 2)
            kb_ref[...] = kb.astype(BF16)
            kbt_ref[...] = kb.T.astype(BF16)
            _by_class(kb, kbc_ref, lanes_sc)

        def use_vb(vb):
            vb_ref[...] = vb.astype(BF16)
            vbt_ref[...] = vb.T.astype(BF16)
            _by_class(vb, vbc_ref, lanes_sc)

        def use_gb(gb):
            gb_ref[...] = gb

        pieces = [(C_CQ, C_CKV, use_cq), (C_CKV, C_KR, use_ckv), (C_KR, C_GA, use_kr), (C_GA, C_QB, use_ga),
                  (C_QB, C_KB, use_qb), (C_KB, C_VB, use_kb), (C_VB, C_GB, use_vb), (C_GB, C_END, use_gb)]
        ahead = proj(*pieces[0][:2])
        for n, (_, _, use) in enumerate(pieces):
            cur = ahead
            if n + 1 < len(pieces):
                ahead = proj(*pieces[n + 1][:2])
            use(cur)

    def tok(width):
        return pl.BlockSpec((bt, width), lambda i: (i, 0))

    def tok_t(height):
        return pl.BlockSpec((height, bt), lambda i: (0, i))

    def full(a):
        return pl.BlockSpec(a.shape, lambda i: (0,) * a.ndim)

    outs = [(Q_RANK, F32), (KV_RANK, F32), (Q_RANK, BF16), (KV_RANK, BF16), (QW, BF16), (HW, BF16),
            (LANES, BF16), (HW, BF16), (HW, F32), (HW, F32), (HW, BF16), (HW, BF16), (HW, BF16)]
    outs_t = [HW, MLA_ROPE, HW, HW, HW]
    by_class = pl.BlockSpec((n_cls, bt // n_cls, HW), lambda i: (0, i, 0))
    return _pcall(
        body, name="fwd_proj", grid=(seq // bt,),
        in_specs=[tok(D_MODEL), full(w_in_r), full(w_uq_r), full(w_ukv_r), full(qg), full(kvg),
                  pl.BlockSpec((6, bt, LANES), lambda i: (0, i, 0))],
        out_specs=[tok(w) for w, _ in outs] + [tok_t(h) for h in outs_t] + [by_class] * 3,
        out_shape=[jax.ShapeDtypeStruct((seq, w), dt) for w, dt in outs]
        + [jax.ShapeDtypeStruct((h, seq), BF16) for h in outs_t]
        + [jax.ShapeDtypeStruct((n_cls, seq // n_cls, HW), BF16)] * 3,
        scratch_shapes=[pltpu.VMEM((HW // LANES, bt, LANES), F32)],
        compiler_params=_cparams(dimension_semantics=("arbitrary",)),
    )(x, w_in_r, w_uq_r, w_ukv_r, qg, kvg, tabs)


def _head_masks(lane, h):
    e, g = h % 2, h % 4
    me = (lane >= 64 * e) & (lane < 64 * e + 64)
    mr = (lane >= 32 * g) & (lane < 32 * g + 32)
    return me, mr


def _masked(mask, a):
    return jnp.where(mask, a, jnp.zeros_like(a))


def _pair_operands(q_ref, k_ref, kpe_ref, lane, j, ks=slice(None), qs=slice(None)):
    cols = slice(LANES * j, LANES * (j + 1))
    qc = q_ref[qs, cols]
    kj = k_ref[ks, cols]
    kes = []
    for h in (2 * j, 2 * j + 1):
        me, mr = _head_masks(lane, h)
        ke = _masked(me, kj)
        if kpe_ref is not None:
            ke = jnp.concatenate([ke, _masked(mr, kpe_ref[ks, :])], axis=1)
        kes.append(ke)
    if kpe_ref is not None:
        qc = jnp.concatenate([qc, q_ref[qs, HW + LANES * (j // 2):HW + LANES * (j // 2 + 1)]], axis=1)
    return qc, kes


def _tile_variants(bias_t):
    out = {}
    for i, tile in enumerate(np.asarray(bias_t)):
        h = tile.shape[0] // 2
        skip = 1 if (tile[h:, :h] == NEG).all() else 2 if (tile[:h, h:] == NEG).all() else 0
        out[i] = (bool((tile != 0).any()), skip)
    return out


def _tile_parts(blk, skip):
    lo, hi, full = slice(0, blk // 2), slice(blk // 2, blk), slice(0, blk)
    return {0: [(full, full)], 1: [(lo, full), (hi, hi)], 2: [(hi, full), (lo, lo)]}[skip]


class Rider(NamedTuple):
    args: list
    in_specs: list
    out_shape: list
    out_specs: list
    scratch: list
    start: Callable
    finish: Callable
    stages: tuple = ()


def _ride_along(body, ride, n_prefetch, n_in, n_out, n_scratch, n_steps):
    if ride is None:
        return body

    def wrapped(*refs):
        pre, rest = refs[:n_prefetch], refs[n_prefetch:]
        a = n_in
        b = a + len(ride.args)
        c = b + n_out
        d = c + len(ride.out_shape)
        e = d + n_scratch
        mine = (rest[a:b], rest[c:d], rest[e:])
        t = pl.program_id(0)
        pl.when(t == 0)(lambda: ride.start(*mine))
        for at, stage in ride.stages:
            pl.when(t == at)(functools.partial(stage, *mine))
        body(*pre, *rest[:a], *rest[b:c], *rest[d:e])
        pl.when(t == n_steps - 1)(lambda: ride.finish(*mine))

    return wrapped


def _attn_fwd(name, q, k, kpe, vt, bias_t, steps, blk, ride=None, v_token_major=False):
    seq = q.shape[0]
    mla = kpe is not None
    n_steps = int(steps[0].shape[0])
    variants = _tile_variants(bias_t)

    def body(qi_r, ki_r, bi_r, fi_r, la_r, *refs):
        if mla:
            q_ref, k_ref, kpe_ref, vt_ref, b_ref, o_ref, lse_ref, m_sc, l_sc, acc_sc, st_sc = refs
        else:
            q_ref, k_ref, vt_ref, b_ref, o_ref, lse_ref, m_sc, l_sc, acc_sc, st_sc = refs
        t = pl.program_id(0)

        @pl.when(fi_r[t] == 1)
        def _():
            m_sc[...] = jnp.full(m_sc.shape, NEG, F32)
            l_sc[...] = jnp.zeros(l_sc.shape, F32)
            acc_sc[...] = jnp.zeros(acc_sc.shape, F32)

        lane = lax.broadcasted_iota(jnp.int32, (1, LANES), 1)
        if v_token_major:
            vt_all = vt_ref[...].astype(F32).T.astype(BF16)
            vt_rows = lambda rows, ks: vt_all[rows, ks]
        else:
            vt_rows = lambda rows, ks: vt_ref[rows, ks]

        def tile_pass(ks, qs, with_bias):
            nk, nq = ks.stop - ks.start, qs.stop - qs.start
            ones = jnp.ones((16, nk), BF16)

            def pair_scores(j):
                qc, kes = _pair_operands(q_ref, k_ref, kpe_ref if mla else None, lane, j, ks, qs)
                st = lax.dot_general(jnp.concatenate(kes, axis=0), qc, NT, preferred_element_type=F32)
                maxes = []
                for e in range(2):
                    se = st[e * nk:(e + 1) * nk]
                    if with_bias:
                        se = se + b_ref[0, ks, qs]
                    st_sc[j % 2, e * nk:(e + 1) * nk, 0:nq] = se
                    maxes.append(jnp.max(se, axis=0, keepdims=True))
                return maxes

            def softmax_pv(h, col_max):
                st = st_sc[(h // 2) % 2, (h % 2) * nk:(h % 2 + 1) * nk, 0:nq]
                hrow = slice(h, h + 1)
                m_prev = m_sc[hrow, qs]
                m_new = jnp.maximum(m_prev, col_max)
                alpha = jnp.exp2(m_prev - m_new)
                pt = jnp.exp2(st - m_new).astype(BF16)
                m_sc[hrow, qs] = m_new
                rows = slice(64 * h, 64 * h + 64)
                res = jnp.dot(jnp.concatenate([vt_rows(rows, ks), ones], axis=0), pt, preferred_element_type=F32)
                acc_sc[rows, qs] = alpha * acc_sc[rows, qs] + res[:64]
                l_sc[hrow, qs] = alpha * l_sc[hrow, qs] + res[64:65]

            maxes = pair_scores(0)
            for j in range(HEADS // 2):
                cur = maxes
                if j + 1 < HEADS // 2:
                    maxes = pair_scores(j + 1)
                softmax_pv(2 * j, cur[0])
                softmax_pv(2 * j + 1, cur[1])

        def step(with_bias, skip):
            for ks, qs in _tile_parts(blk, skip):
                tile_pass(ks, qs, with_bias)

        for idx, (with_bias, skip) in variants.items():
            if len(variants) == 1:
                step(with_bias, skip)
            else:
                pl.when(bi_r[t] == idx)(functools.partial(step, with_bias, skip))

        @pl.when(la_r[t] == 1)
        def _():
            for h in range(HEADS):
                rows = slice(64 * h, 64 * h + 64)
                acc_sc[rows, :] = acc_sc[rows, :] / l_sc[h:h + 1, :]
            o_ref[...] = acc_sc[...].T
            lse_ref[...] = m_sc[...] + jnp.log2(l_sc[...])

    qmap = lambda t, qi, ki, bi, fi, la: (qi[t], 0)
    kmap = lambda t, qi, ki, bi, fi, la: (ki[t], 0)
    in_specs = [pl.BlockSpec((blk, q.shape[1]), qmap), pl.BlockSpec((blk, HW), kmap)]
    args = [q, k]
    if mla:
        in_specs.append(pl.BlockSpec((blk, LANES), kmap))
        args.append(kpe)
    in_specs += [pl.BlockSpec((blk, HW), kmap) if v_token_major else
                 pl.BlockSpec((HW, blk), lambda t, qi, ki, bi, fi, la: (0, ki[t])),
                 pl.BlockSpec((1, blk, blk), lambda t, qi, ki, bi, fi, la: (bi[t], 0, 0))]
    args += [vt, jnp.asarray(bias_t)]
    out_specs = [pl.BlockSpec((blk, HW), qmap), pl.BlockSpec((HEADS, blk), lambda t, qi, ki, bi, fi, la: (0, qi[t]))]
    out_shape = [jax.ShapeDtypeStruct((seq, HW), F32), jax.ShapeDtypeStruct((HEADS, seq), F32)]
    scratch = [pltpu.VMEM((HEADS, blk), F32), pltpu.VMEM((HEADS, blk), F32),
               pltpu.VMEM((HW, blk), F32), pltpu.VMEM((2, 2 * blk, blk), F32)]
    body = _ride_along(body, ride, 5, len(args), len(out_shape), len(scratch), n_steps)
    if ride is not None:
        args, in_specs = args + ride.args, in_specs + ride.in_specs
        out_specs, out_shape, scratch = out_specs + ride.out_specs, out_shape + ride.out_shape, scratch + ride.scratch
    return _pcall(
        body, name=name,
        grid_spec=pltpu.PrefetchScalarGridSpec(
            num_scalar_prefetch=5, grid=(n_steps,), in_specs=in_specs, out_specs=out_specs, scratch_shapes=scratch),
        out_shape=out_shape,
        compiler_params=_cparams(dimension_semantics=("arbitrary",)),
    )(*steps, *args)


def _attn_bwd(name, q, k, kpe, v, kt, kpet, bias_t, do, lse, dstat, steps, blk, ride=None, single_visit=False):
    assert not (single_visit and kpe is not None) and (kt is not None or single_visit)
    seq = q.shape[0]
    mla = kpe is not None
    qw = q.shape[1]
    n_steps = int(steps[0].shape[0])
    dk_dtype = BF16 if mla else F32
    variants = _tile_variants(bias_t)

    def body(qi_r, ki_r, bi_r, fi_r, la_r, *refs):
        if mla:
            (q_ref, k_ref, kpe_ref, v_ref, kt_ref, kpet_ref, b_ref, do_ref, lse_ref, d_ref,
             dq_ref, dk_ref, dkpe_ref, dv_ref, dk_sc, dkpe_sc, dv_sc, st_sc, dpt_sc) = refs
        else:
            q_ref, k_ref, v_ref, *rest = refs
            kt_ref = rest.pop(0) if kt is not None else None
            b_ref, do_ref, lse_ref, d_ref, dq_out_ref, dk_ref, dv_ref, dk_sc, dv_sc, st_sc, dpt_sc, *rest = rest
            dq_ref = rest[0] if single_visit else dq_out_ref
        t = pl.program_id(0)

        @pl.when(jnp.logical_or(t == 0, single_visit))
        def _():
            dq_ref[...] = jnp.zeros(dq_ref.shape, F32)

        @pl.when(fi_r[t] == 1)
        def _():
            dk_sc[...] = jnp.zeros(dk_sc.shape, F32)
            dv_sc[...] = jnp.zeros(dv_sc.shape, F32)
            if mla:
                dkpe_sc[...] = jnp.zeros(dkpe_sc.shape, F32)

        qi = 0 if single_visit else qi_r[t]
        lane = lax.broadcasted_iota(jnp.int32, (1, LANES), 1)
        if kt is None:
            kt_all = k_ref[...].astype(F32).T.astype(BF16)
            kt_rows = lambda rows, ks: kt_all[rows, ks]
        else:
            kt_rows = lambda rows, ks: kt_ref[rows, ks]

        def tile_pass(ks, qs, with_bias):
            nk, nq = ks.stop - ks.start, qs.stop - qs.start

            def pair_matmuls(j):
                cols = slice(LANES * j, LANES * (j + 1))
                qc, kes = _pair_operands(q_ref, k_ref, kpe_ref if mla else None, lane, j, ks, qs)
                st_sc[j % 2, 0:2 * nk, 0:nq] = lax.dot_general(
                    jnp.concatenate(kes, axis=0), qc, NT, preferred_element_type=F32)
                vj = v_ref[ks, cols]
                ves = [_masked(_head_masks(lane, h)[0], vj) for h in (2 * j, 2 * j + 1)]
                dpt_sc[j % 2, 0:2 * nk, 0:nq] = lax.dot_general(
                    jnp.concatenate(ves, axis=0), do_ref[qs, cols], NT, preferred_element_type=F32)

            def pair_grads(j):
                cols = slice(LANES * j, LANES * (j + 1))
                qj, doj = q_ref[qs, cols], do_ref[qs, cols]
                if mla:
                    qr = q_ref[qs, HW + LANES * (j // 2):HW + LANES * (j // 2 + 1)]
                pts, dsts, qms, doms = [], [], [], []
                for e in range(2):
                    h = 2 * j + e
                    me, mr = _head_masks(lane, h)
                    st = st_sc[j % 2, e * nk:(e + 1) * nk, 0:nq]
                    if with_bias:
                        st = st + b_ref[0, ks, qs]
                    pt = jnp.exp2(st - lse_ref[h:h + 1, qs])
                    dst = (pt * (dpt_sc[j % 2, e * nk:(e + 1) * nk, 0:nq] - d_ref[h:h + 1, qs])).astype(BF16)
                    pts.append(pt.astype(BF16))
                    dsts.append(dst)
                    doms.append(_masked(me, doj))
                    qm = _masked(me, qj)
                    if mla:
                        qm = jnp.concatenate([qm, _masked(mr, qr)], axis=1)
                    qms.append(qm)
                    ktl = kt_rows(slice(64 * h, 64 * h + 64), ks)
                    if mla:
                        ktl = jnp.concatenate([ktl, kpet_ref[:, ks]], axis=0)
                    dqc = jnp.dot(ktl, dst, preferred_element_type=F32)
                    dq_ref[qi, 64 * h:64 * h + 64, qs] += dqc[:64]
                    if mla:
                        dq_ref[qi, HW + MLA_ROPE * h:HW + MLA_ROPE * (h + 1), qs] += dqc[64:]
                dv_sc[ks, cols] += jnp.dot(
                    jnp.concatenate(pts, axis=1), jnp.concatenate(doms, axis=0), preferred_element_type=F32)
                dkc = jnp.dot(jnp.concatenate(dsts, axis=1), jnp.concatenate(qms, axis=0), preferred_element_type=F32)
                dk_sc[ks, cols] += dkc[:, :LANES]
                if mla:
                    dkpe_sc[ks, :] += dkc[:, LANES:]

            pair_matmuls(0)
            for j in range(HEADS // 2):
                if j + 1 < HEADS // 2:
                    pair_matmuls(j + 1)
                pair_grads(j)

        def step(with_bias, skip):
            for ks, qs in _tile_parts(blk, skip):
                tile_pass(ks, qs, with_bias)

        for idx, (with_bias, skip) in variants.items():
            if len(variants) == 1:
                step(with_bias, skip)
            else:
                pl.when(bi_r[t] == idx)(functools.partial(step, with_bias, skip))
        if single_visit:
            dq_out_ref[...] = dq_ref[0].T

        @pl.when(la_r[t] == 1)
        def _():
            dk_ref[...] = (dk_sc[...] * LN2).astype(dk_ref.dtype)
            dv_ref[...] = dv_sc[...].astype(dv_ref.dtype)
            if mla:
                dkpe_ref[...] = dkpe_sc[...] * LN2

    qmap = lambda t, qi, ki, bi, fi, la: (qi[t], 0)
    kmap = lambda t, qi, ki, bi, fi, la: (ki[t], 0)
    qmap_t = lambda t, qi, ki, bi, fi, la: (0, qi[t])
    kmap_t = lambda t, qi, ki, bi, fi, la: (0, ki[t])
    in_specs = [pl.BlockSpec((blk, qw), qmap), pl.BlockSpec((blk, HW), kmap)]
    args = [q, k]
    if mla:
        in_specs.append(pl.BlockSpec((blk, LANES), kmap))
        args.append(kpe)
    in_specs.append(pl.BlockSpec((blk, HW), kmap))
    args.append(v)
    if kt is not None:
        in_specs.append(pl.BlockSpec((HW, blk), kmap_t))
        args.append(kt)
    if mla:
        in_specs.append(pl.BlockSpec((MLA_ROPE, blk), kmap_t))
        args.append(kpet)
    in_specs += [pl.BlockSpec((1, blk, blk), lambda t, qi, ki, bi, fi, la: (bi[t], 0, 0)),
                 pl.BlockSpec((blk, HW), qmap), pl.BlockSpec((HEADS, blk), qmap_t), pl.BlockSpec((HEADS, blk), qmap_t)]
    args += [jnp.asarray(bias_t), do, lse, dstat]
    dq_shape = (seq // blk, qw, blk)
    if single_visit:
        out_specs, out_shape = [pl.BlockSpec((blk, qw), qmap)], [jax.ShapeDtypeStruct((seq, qw), F32)]
    else:
        out_specs = [pl.BlockSpec(dq_shape, lambda t, qi, ki, bi, fi, la: (0, 0, 0))]
        out_shape = [jax.ShapeDtypeStruct(dq_shape, F32)]
    out_specs.append(pl.BlockSpec((blk, HW), kmap))
    out_shape.append(jax.ShapeDtypeStruct((seq, HW), dk_dtype))
    scratch = [pltpu.VMEM((blk, HW), F32)]
    if mla:
        out_specs.append(pl.BlockSpec((blk, LANES), kmap))
        out_shape.append(jax.ShapeDtypeStruct((seq, LANES), F32))
        scratch.append(pltpu.VMEM((blk, LANES), F32))
    out_specs.append(pl.BlockSpec((blk, HW), kmap))
    out_shape.append(jax.ShapeDtypeStruct((seq, HW), BF16))
    scratch.append(pltpu.VMEM((blk, HW), F32))
    scratch += [pltpu.VMEM((2, 2 * blk, blk), F32), pltpu.VMEM((2, 2 * blk, blk), F32)]
    if single_visit:
        scratch.append(pltpu.VMEM((1, qw, blk), F32))
    body = _ride_along(body, ride, 5, len(args), len(out_shape), len(scratch), n_steps)
    if ride is not None:
        args, in_specs = args + ride.args, in_specs + ride.in_specs
        out_specs, out_shape, scratch = out_specs + ride.out_specs, out_shape + ride.out_shape, scratch + ride.scratch
    return _pcall(
        body, name=name,
        grid_spec=pltpu.PrefetchScalarGridSpec(
            num_scalar_prefetch=5, grid=(n_steps,), in_specs=in_specs, out_specs=out_specs,
            scratch_shapes=scratch),
        out_shape=out_shape,
        compiler_params=_cparams(dimension_semantics=("arbitrary",)),
    )(*steps, *args)


def _out_ln(oa, ob_near, ob_far, lse_near, lse_far, ga, gb, x, tgt, w_out, ln_g, ln_b, bt):
    seq = x.shape[0]

    def body(oa_ref, obn_ref, obf_ref, lsen_ref, lsef_ref, ga_ref, gb_ref, x_ref, tgt_ref, w_ref, g_ref, b_ref,
             dz_ref, doa_ref, dob_ref, dga_ref, dgb_ref, da_ref, db_ref, lse_ref, gwb_ref, small_ref, dobc_ref,
             gw_ref, lanes_sc):
        i = pl.program_id(0)

        @pl.when(i == 0)
        def _():
            gw_ref[...] = jnp.zeros(gw_ref.shape, F32)
            small_ref[...] = jnp.zeros(small_ref.shape, F32)

        def gate(g):
            sig = 0.5 * jnp.tanh(0.5 * g) + 0.5
            return g * sig, sig * (1.0 + g * (1.0 - sig))

        lse_n, lse_f = lsen_ref[...], lsef_ref[...]
        top = jnp.maximum(lse_n, lse_f)
        e_n, e_f = jnp.exp2(lse_n - top), jnp.exp2(lse_f - top)
        lse_ref[...] = top + jnp.log2(e_n + e_f)
        inv = 1.0 / (e_n + e_f)
        head_row = lax.broadcasted_iota(jnp.int32, (2 * HEADS, HW), 0) % HEADS
        spread = (head_row == lax.broadcasted_iota(jnp.int32, (2 * HEADS, HW), 1) // 64).astype(BF16)

        def per_lane(w):
            hi = w.astype(BF16)
            lo = (w - hi.astype(F32)).astype(BF16)
            return lax.dot_general(jnp.concatenate([hi, lo], axis=0), spread, TN, preferred_element_type=F32)

        o_b_all = per_lane(e_n * inv) * obn_ref[...] + per_lane(e_f * inv) * _in_sequence(obf_ref, lanes_sc)
        gam = g_ref[...]
        halves = [slice(0, bt // 2), slice(bt // 2, bt)]

        def gates_and_projection(rows):
            o_a, o_b = oa_ref[rows, :], o_b_all[rows]
            sa, dsa = gate(ga_ref[rows, :])
            sb, dsb = gate(gb_ref[rows, :])
            mix = jnp.concatenate([o_a * sa, o_b * sb], axis=1).astype(BF16)
            z = ALPHA * x_ref[rows, :] + jnp.dot(mix, w_ref[...], preferred_element_type=F32)
            return o_a, o_b, sa, dsa, sb, dsb, mix, z

        def norm_and_back(rows, mix, z):
            mu = jnp.mean(z, axis=1, keepdims=True)
            zc = z - mu
            rstd = lax.rsqrt(jnp.mean(zc * zc, axis=1, keepdims=True) + LN_EPS)
            xhat = zc * rstd
            diff = xhat * gam + b_ref[...] - tgt_ref[rows, :]
            dy = diff * (1.0 / D_MODEL)
            small_ref[0:1, :] += jnp.sum(dy * xhat, axis=0, keepdims=True)
            small_ref[1:2, :] += jnp.sum(dy, axis=0, keepdims=True)
            small_ref[2:3, :] += jnp.sum(diff * diff, axis=0, keepdims=True)
            dxh = dy * gam
            dz = rstd * (dxh - jnp.mean(dxh, axis=1, keepdims=True)
                         - xhat * jnp.mean(dxh * xhat, axis=1, keepdims=True))
            dz_ref[rows, :] = dz
            dzb = dz.astype(BF16)
            gw_ref[...] += lax.dot_general(mix, dzb, TN, preferred_element_type=F32)
            return lax.dot_general(dzb, w_ref[...], NT, preferred_element_type=F32)

        def gate_back(rows, o_a, o_b, sa, dsa, sb, dsb, dmix):
            doa, dob = dmix[:, :HW] * sa, dmix[:, HW:] * sb
            doa_ref[rows, :] = doa.astype(BF16)
            dob_ref[rows, :] = dob.astype(BF16)
            dga_ref[rows, :] = (dmix[:, :HW] * o_a * dsa).astype(BF16)
            dgb_ref[rows, :] = (dmix[:, HW:] * o_b * dsb).astype(BF16)
            return dob, doa * o_a, dob * o_b

        fronts = [gates_and_projection(rows) for rows in halves]
        dmixes = [norm_and_back(rows, f[6], f[7]) for rows, f in zip(halves, fronts)]
        backs = [gate_back(rows, *f[:6], dmix) for rows, f, dmix in zip(halves, fronts, dmixes)]
        dob, prod_a, prod_b = (jnp.concatenate(parts, axis=0) for parts in zip(*backs))
        _by_class(dob, dobc_ref, lanes_sc)

        @pl.when(i == seq // bt - 1)
        def _():
            gwb_ref[...] = gw_ref[...].astype(BF16)

        head_of = (lax.broadcasted_iota(jnp.int32, (2 * HW, LANES), 0) % HW) // 64
        ind = (head_of == lax.broadcasted_iota(jnp.int32, (2 * HW, LANES), 1)).astype(BF16)

        def head_sums(prod):
            hi = prod.astype(BF16)
            lo = (prod - hi.astype(F32)).astype(BF16)
            sums = jnp.dot(jnp.concatenate([hi, lo], axis=1), ind, preferred_element_type=F32)
            return sums.T[:HEADS, :]

        da_ref[...] = head_sums(prod_a)
        db_ref[...] = head_sums(prod_b)

    def tok(width):
        return pl.BlockSpec((bt, width), lambda i: (i, 0))

    def full(shape):
        return pl.BlockSpec(shape, lambda i: (0,) * len(shape))

    stat = pl.BlockSpec((HEADS, bt), lambda i: (0, i))
    n_cls = ob_far.shape[0]
    by_class = pl.BlockSpec((n_cls, bt // n_cls, HW), lambda i: (0, i, 0))
    return _pcall(
        body, name="out_ln", grid=(seq // bt,),
        in_specs=[tok(HW), tok(HW), by_class, stat, stat, tok(HW), tok(HW), tok(D_MODEL), tok(D_MODEL),
                  full((D_MODEL, D_MODEL)), full((1, D_MODEL)), full((1, D_MODEL))],
        out_specs=[tok(D_MODEL), tok(HW), tok(HW), tok(HW), tok(HW), stat, stat, stat,
                   full((D_MODEL, D_MODEL)), full((8, D_MODEL)), by_class],
        out_shape=[jax.ShapeDtypeStruct((seq, D_MODEL), F32)] + [jax.ShapeDtypeStruct((seq, HW), BF16)] * 4
        + [jax.ShapeDtypeStruct((HEADS, seq), F32)] * 3
        + [jax.ShapeDtypeStruct((D_MODEL, D_MODEL), BF16), jax.ShapeDtypeStruct((8, D_MODEL), F32),
           jax.ShapeDtypeStruct(ob_far.shape, BF16)],
        scratch_shapes=[pltpu.VMEM((D_MODEL, D_MODEL), F32), pltpu.VMEM((HW // LANES, bt, LANES), F32)],
        compiler_params=_cparams(dimension_semantics=("arbitrary",)),
    )(oa, ob_near, ob_far, lse_near, lse_far, ga, gb, x, tgt, w_out, ln_g, ln_b)


def _bwd_mid(dq_m, dkn, dv, dkpe, dqb, dkb, dvb, far, dga, dgb, cq, ckv, qn, kvn, w_uq_r, w_ukv_r, qg, kvg, tabs, bt):
    n_cls = far[0].shape[0]
    seq = cq.shape[0]

    def body(dqm_ref, dkn_ref, dv_ref, dkpe_ref, dqb_ref, dkb_ref, dvb_ref, dqf_ref, dkf_ref, dvf_ref, dga_ref, dgb_ref,
             cq_ref, ckv_ref, qn_ref, kvn_ref, wuq_ref, wukv_ref, qg_ref, kvg_ref, tab_ref,
             dh_ref, guq3_ref, gukv3_ref, small_ref, seq_sc, guq_ref, gukv_ref):
        i = pl.program_id(0)

        @pl.when(i == 0)
        def _():
            guq_ref[...] = jnp.zeros(guq_ref.shape, F32)
            gukv_ref[...] = jnp.zeros(gukv_ref.shape, F32)
            small_ref[...] = jnp.zeros(small_ref.shape, F32)

        m_tabs = (tab_ref[0], tab_ref[1], tab_ref[2])
        d_tabs = (tab_ref[3], tab_ref[4], tab_ref[5])

        def rms_bwd(c, dn, gain):
            r = lax.rsqrt(jnp.mean(c * c, axis=1, keepdims=True) + RMS_EPS)
            u = dn * gain
            dc = r * u - c * (r * r * r) * jnp.mean(u * c, axis=1, keepdims=True)
            return dc, jnp.sum(dn * c * r, axis=0, keepdims=True)

        dqm = dqm_ref[0].T
        dq = jnp.concatenate(
            [dqm[:, :HW], _rope_wide(_rope_t, dqm[:, HW:], *m_tabs, MLA_ROPE // 2)], axis=1) * MLA_SCALE
        dq = dq.astype(BF16)
        dkv = jnp.concatenate([dkn_ref[...], dv_ref[...]], axis=1)
        guq_ref[...] += lax.dot_general(qn_ref[...], dq, TN, preferred_element_type=F32)
        dqn = lax.dot_general(dq, wuq_ref[...], NT, preferred_element_type=F32)
        gukv_ref[...] += lax.dot_general(kvn_ref[...], dkv, TN, preferred_element_type=F32)
        dkvn = lax.dot_general(dkv, wukv_ref[...], NT, preferred_element_type=F32)

        dh_ref[:, C_KR:C_GA] = _rope_t(dkpe_ref[...], *m_tabs, MLA_ROPE // 2).astype(BF16)
        dh_ref[:, C_GA:C_QB] = dga_ref[...]
        in_sequence = functools.partial(_in_sequence, lanes_sc=seq_sc)
        dqb = dqb_ref[0].T + in_sequence(dqf_ref)
        dh_ref[:, C_QB:C_KB] = (_rope_wide(_rope_t, dqb, *d_tabs, DIL_ROT // 2) * DIL_SCALE).astype(BF16)
        dkb = dkb_ref[...] + in_sequence(dkf_ref)
        dh_ref[:, C_KB:C_VB] = _rope_wide(_rope_t, dkb, *d_tabs, DIL_ROT // 2).astype(BF16)
        dh_ref[:, C_VB:C_GB] = (dvb_ref[...].astype(F32) + in_sequence(dvf_ref)).astype(BF16)
        dh_ref[:, C_GB:C_END] = dgb_ref[...]

        dcq, gq = rms_bwd(cq_ref[...], dqn, qg_ref[...])
        small_ref[0:1, :] += gq
        dckv, gkv = rms_bwd(ckv_ref[...], dkvn, kvg_ref[...])
        small_ref[1:2, :KV_RANK] += gkv
        dh_ref[:, C_CQ:C_CKV] = dcq.astype(BF16)
        dh_ref[:, C_CKV:C_KR] = dckv.astype(BF16)

        @pl.when(i == seq // bt - 1)
        def _():
            for h in range(HEADS):
                guq3_ref[h] = jnp.concatenate(
                    [guq_ref[:, MLA_NOPE * h:MLA_NOPE * (h + 1)],
                     guq_ref[:, HW + MLA_ROPE * h:HW + MLA_ROPE * (h + 1)]], axis=1).astype(BF16)
                gukv3_ref[h] = jnp.concatenate(
                    [gukv_ref[:, MLA_NOPE * h:MLA_NOPE * (h + 1)],
                     gukv_ref[:, HW + MLA_V * h:HW + MLA_V * (h + 1)]], axis=1).astype(BF16)

    def tok(width):
        return pl.BlockSpec((bt, width), lambda i: (i, 0))

    def tok_t(a):
        per = a.shape[2] // bt
        return pl.BlockSpec((1, a.shape[1], bt), lambda i: (i // per, 0, i % per))

    def full(shape):
        return pl.BlockSpec(shape, lambda i: (0,) * len(shape))

    by_class = pl.BlockSpec((n_cls, bt // n_cls, HW), lambda i: (0, i, 0))
    uq3 = (HEADS, Q_RANK, MLA_NOPE + MLA_ROPE)
    ukv3 = (HEADS, KV_RANK, MLA_NOPE + MLA_V)
    return _pcall(
        body, name="bwd_mid", grid=(seq // bt,),
        in_specs=[tok_t(dq_m), tok(HW), tok(HW), tok(LANES), tok_t(dqb), tok(HW), tok(HW), by_class, by_class, by_class,
                  tok(HW), tok(HW),
                  tok(Q_RANK), tok(KV_RANK), tok(Q_RANK), tok(KV_RANK),
                  full(w_uq_r.shape), full(w_ukv_r.shape), full((1, Q_RANK)), full((1, KV_RANK)),
                  pl.BlockSpec((6, bt, LANES), lambda i: (0, i, 0))],
        out_specs=[tok(C_END), full(uq3), full(ukv3), full((8, Q_RANK))],
        out_shape=[jax.ShapeDtypeStruct((seq, C_END), BF16), jax.ShapeDtypeStruct(uq3, BF16),
                   jax.ShapeDtypeStruct(ukv3, BF16), jax.ShapeDtypeStruct((8, Q_RANK), F32)],
        scratch_shapes=[pltpu.VMEM((HW // LANES, bt, LANES), F32), pltpu.VMEM(w_uq_r.shape, F32),
                        pltpu.VMEM(w_ukv_r.shape, F32)],
        compiler_params=_cparams(dimension_semantics=("arbitrary",)),
    )(dq_m, dkn, dv, dkpe, dqb, dkb, dvb, *far, dga, dgb, cq, ckv, qn, kvn, w_uq_r, w_ukv_r, qg, kvg, tabs)


def _grad_x(dz, dh, w_in_r, bt, ride=None):
    seq = dz.shape[0]
    n_steps = seq // bt

    def body(dz_ref, dh_ref, w_ref, gx_ref):
        gx_ref[...] = ALPHA * dz_ref[...] + lax.dot_general(
            dh_ref[...], w_ref[...], NT, preferred_element_type=F32)

    args = [dz, dh, w_in_r]
    in_specs = [pl.BlockSpec((bt, D_MODEL), lambda i: (i, 0)), pl.BlockSpec((bt, C_END), lambda i: (i, 0)),
                pl.BlockSpec(w_in_r.shape, lambda i: (0, 0))]
    out_specs = [pl.BlockSpec((bt, D_MODEL), lambda i: (i, 0))]
    out_shape = [jax.ShapeDtypeStruct((seq, D_MODEL), F32)]
    scratch = []
    body = _ride_along(body, ride, 0, len(args), len(out_shape), 0, n_steps)
    if ride is not None:
        args, in_specs = args + ride.args, in_specs + ride.in_specs
        out_specs, out_shape, scratch = out_specs + ride.out_specs, out_shape + ride.out_shape, ride.scratch
    return _pcall(
        body, name="grad_x", grid=(n_steps,),
        in_specs=in_specs, out_specs=out_specs, out_shape=out_shape, scratch_shapes=scratch,
        compiler_params=_cparams(dimension_semantics=("arbitrary",)),
    )(*args)


def _grad_w_in(x, dh, bt):
    seq = x.shape[0]
    shard = IN_WIDTH // N_DEV
    k_lo, k_hi = IN_SPLITS[0] + IN_SPLITS[1], IN_SPLITS[0] + IN_SPLITS[1] + MLA_ROPE

    def body(x_ref, dh_ref, out_ref, acc):
        i = pl.program_id(0)

        @pl.when(i == 0)
        def _():
            acc[...] = jnp.zeros(acc.shape, F32)

        acc[...] += lax.dot_general(x_ref[...].astype(BF16), dh_ref[...], TN, preferred_element_type=F32)

        @pl.when(i == seq // bt - 1)
        def _():
            kr = acc[:, C_KR:C_GA]
            kr = kr + pltpu.roll(kr, 96, 1) + pltpu.roll(kr, 64, 1) + pltpu.roll(kr, 32, 1)
            for d in range(N_DEV):
                lo, hi = shard * d, shard * (d + 1)
                pieces = []
                if lo < k_lo:
                    pieces.append(acc[:, lo:min(hi, k_lo)])
                if lo < k_hi and hi > k_lo:
                    pieces.append(kr[:, max(lo, k_lo) - k_lo:min(hi, k_hi) - k_lo])
                if hi > k_hi:
                    shift = C_GA - k_hi
                    pieces.append(acc[:, max(lo, k_hi) + shift:hi + shift])
                blk = pieces[0] if len(pieces) == 1 else jnp.concatenate(pieces, axis=1)
                out_ref[d] = blk.astype(BF16)

    return _pcall(
        body, name="grad_w_in", grid=(seq // bt,),
        in_specs=[pl.BlockSpec((bt, D_MODEL), lambda i: (i, 0)), pl.BlockSpec((bt, C_END), lambda i: (i, 0))],
        out_specs=pl.BlockSpec((N_DEV, D_MODEL, shard), lambda i: (0, 0, 0)),
        out_shape=jax.ShapeDtypeStruct((N_DEV, D_MODEL, shard), BF16),
        scratch_shapes=[pltpu.VMEM((D_MODEL, C_END), F32)],
        compiler_params=_cparams(dimension_semantics=("arbitrary",)),
    )(x, dh)


def _local_step(x, tgt, w_in_r, w_uq_r, w_ukv_r, w_out_rider, g_out_rider, reduce_rider, q_norm_g, kv_norm_g,
                ln_g, ln_b, bt=BLOCK_TOKENS, blk_m=BLOCK_MLA, blk_d=BLOCK_DIL):
    seq = x.shape[0]
    tabs = jnp.asarray(_rope_tables(seq))
    qg, kvg = q_norm_g.reshape(1, -1), kv_norm_g.reshape(1, -1)

    far_dil = DIL_CONFIGS[-1][1]
    cls = seq // far_dil
    (cq, ckv, qn, kvn, qcat, kn, kpe, v, ga, gb, qb, kb, vb, knt, kpet, vt, kbt, vbt, qb_c, kb_c, vb_c) = _fwd_proj(
        x, w_in_r, w_uq_r, w_ukv_r, qg, kvg, tabs, bt, far_dil)
    qb_c, kb_c, vb_c = (a.reshape(seq, HW) for a in (qb_c, kb_c, vb_c))

    nq_m, nq_d = seq // blk_m, seq // blk_d
    bias_m = _mla_bias_t(blk_m)
    oa, lse_a, w_out = _attn_fwd(
        "mla_fwd", qcat, kn, kpe, vt, bias_m, _steps(nq_m, nq_m, False, True), blk_m, ride=w_out_rider)

    bias_near = _dil_bias_t(blk_d, DIL_NEAR)
    ob_near, lse_near = _attn_fwd(
        "dil_fwd", qb, kb, None, vbt, bias_near, _steps(nq_d, -(-DIL_NEAR // blk_d), False, False), blk_d)
    each = jnp.arange(far_dil, dtype=jnp.int32)
    steps_far = [each, each, jnp.zeros_like(each), jnp.ones_like(each), jnp.ones_like(each)]
    bias_far = _dil_far_bias_t(cls)
    ob_far, lse_far = _attn_fwd(
        "dil_far_fwd", qb_c, kb_c, None, vb_c, bias_far, steps_far, cls, v_token_major=True)

    dz, doa, dob, dga, dgb, dst_a, dst_b, lse_b, g_out, small1, dob_c = _out_ln(
        oa, ob_near, ob_far.reshape(far_dil, cls, HW), lse_near, _lanes_from_classes(lse_far, far_dil), ga, gb, x, tgt,
        w_out.reshape(D_MODEL, D_MODEL), ln_g.reshape(1, -1), ln_b.reshape(1, -1), bt)

    dq_m, dkn, dkpe, dv, g_out_recv = _attn_bwd(
        "mla_bwd", qcat, kn, kpe, v, knt, kpet, bias_m, doa, lse_a, dst_a, _steps(nq_m, nq_m, True, True), blk_m,
        ride=g_out_rider(g_out.reshape(N_DEV, D_MODEL // N_DEV, D_MODEL)))
    dqb, dkb_near, dvb_near = _attn_bwd(
        "dil_bwd", qb, kb, None, vb, kbt, None, bias_near, dob, lse_b, dst_b,
        _steps(nq_d, -(-DIL_NEAR // blk_d), True, False), blk_d)
    dqb_far, dkb_far, dvb_far = _attn_bwd(
        "dil_far_bwd", qb_c, kb_c, None, vb_c, None, None, bias_far, dob_c.reshape(seq, HW),
        _lanes_to_classes(lse_b, far_dil), _lanes_to_classes(dst_b, far_dil), steps_far, cls, single_visit=True)
    far = [a.reshape(far_dil, cls, HW) for a in (dqb_far, dkb_far, dvb_far)]

    dh, g_uq, g_ukv, small2 = _bwd_mid(
        dq_m, dkn, dv, dkpe, dqb, dkb_near, dvb_near, far, dga, dgb, cq, ckv, qn, kvn, w_uq_r, w_ukv_r, qg, kvg, tabs, bt)
    g_in = _grad_w_in(x, dh, min(seq, 2 * bt))
    grads3 = [g_in, g_uq, g_ukv]
    small_part = _small_rows(small1[0], small1[1], small2[0, :Q_RANK], small2[1, :KV_RANK], small1[2])
    grad_x, *reduced = _grad_x(
        dz, dh, w_in_r, bt, ride=reduce_rider(grads3, g_out_recv, small_part, min(1, seq // bt - 1)))
    return grad_x, reduced


MESH_ID = pl.DeviceIdType.MESH
SHARD_SHAPES = ((D_MODEL, IN_WIDTH // N_DEV), (Q_RANK, 768 // N_DEV), (KV_RANK, 1024 // N_DEV), (D_MODEL // N_DEV, D_MODEL))
ADAM_ROWS = (32, 128, 128, 16)


def _me():
    x, y, c = lax.axis_index("x"), lax.axis_index("y"), lax.axis_index("c")
    return x, y, c, 4 * x + 2 * y + c


def _peer(k):
    x, y, c, _ = _me()
    px = 1 - x if (k >> 2) & 1 else x
    py = 1 - y if (k >> 1) & 1 else y
    pc = 1 - c if k & 1 else c
    return (px, py, pc), 4 * px + 2 * py + pc


def _all_gather_weights(shards):
    n = len(shards)
    shard = IN_WIDTH // N_DEV
    k_lo = IN_SPLITS[0] + IN_SPLITS[1]
    k_hi = k_lo + MLA_ROPE

    def body(*refs):
        ins = refs[:n]
        win_ref, wuq_ref, wukv_ref = refs[n:2 * n]
        bufs = refs[2 * n:3 * n]
        send_sems, recv_sems = refs[3 * n:]
        x, y, c, me = _me()
        here, sibling = (x, y, c), (x, y, 1 - c)
        chips = [(1 - x, y), (x, 1 - y), (1 - x, 1 - y)]
        for t in range(n):
            bufs[t][me] = ins[t][...].astype(BF16)

        def copy(t, k, px, py, pc, to):
            blk = bufs[t].at[4 * px + 2 * py + pc]
            return pltpu.make_async_remote_copy(
                src_ref=blk, dst_ref=blk, send_sem=send_sems.at[t, k], recv_sem=recv_sems.at[t, k],
                device_id=to, device_id_type=MESH_ID)

        first = []
        for t in range(n):
            first.append(copy(t, 0, x, y, c, sibling))
            for j, (px, py) in enumerate(chips):
                first.append(copy(t, 1 + j, x, y, c, (px, py, c)))
        for cp in first:
            cp.start()
        passed = []
        for j, (px, py) in enumerate(chips):
            for t in range(n):
                copy(t, 1 + j, px, py, c, here).wait_recv()
                cp = copy(t, 4 + j, px, py, c, sibling)
                cp.start()
                passed.append(cp)
        for t in range(n):
            copy(t, 0, x, y, 1 - c, here).wait_recv()
        for j, (px, py) in enumerate(chips):
            for t in range(n):
                copy(t, 4 + j, px, py, 1 - c, here).wait_recv()
        for cp in first + passed:
            cp.wait_send()

        a_in, a_uq, a_ukv = bufs
        for d in range(N_DEV):
            lo, hi = shard * d, shard * (d + 1)
            if lo < k_lo:
                win_ref[:, lo:min(hi, k_lo)] = a_in[d, :, 0:min(hi, k_lo) - lo]
            if lo < k_hi and hi > k_lo:
                kr = a_in[d, :, k_lo - lo:k_hi - lo]
                for rep in range(4):
                    win_ref[:, C_KR + MLA_ROPE * rep:C_KR + MLA_ROPE * (rep + 1)] = kr
            if hi > k_hi:
                src = max(lo, k_hi)
                win_ref[:, src + C_GA - k_hi:hi + C_GA - k_hi] = a_in[d, :, src - lo:hi - lo]
        for h in range(HEADS):
            wuq_ref[:, MLA_NOPE * h:MLA_NOPE * (h + 1)] = a_uq[h, :, :MLA_NOPE]
            wuq_ref[:, HW + MLA_ROPE * h:HW + MLA_ROPE * (h + 1)] = a_uq[h, :, MLA_NOPE:]
            wukv_ref[:, MLA_NOPE * h:MLA_NOPE * (h + 1)] = a_ukv[h, :, :MLA_NOPE]
            wukv_ref[:, HW + MLA_V * h:HW + MLA_V * (h + 1)] = a_ukv[h, :, MLA_NOPE:]

    vmem = pl.BlockSpec(memory_space=pltpu.VMEM)
    return _pcall(
        body, name="gather_weights",
        in_specs=[vmem] * n, out_specs=[vmem] * n,
        out_shape=[jax.ShapeDtypeStruct((D_MODEL, C_END), BF16), jax.ShapeDtypeStruct((Q_RANK, QW), BF16),
                   jax.ShapeDtypeStruct((KV_RANK, 2 * HW), BF16)],
        scratch_shapes=[pltpu.VMEM((N_DEV,) + s, BF16) for s in SHARD_SHAPES[:n]]
        + [pltpu.SemaphoreType.DMA((n, N_DEV - 1)), pltpu.SemaphoreType.DMA((n, N_DEV - 1))],
        compiler_params=_cparams(),
    )(*shards)


def _gather_w_out_rider(w_out):
    def copies(full_ref, stage, send_sems, recv_sems):
        me = _me()[3]
        out = []
        for k in range(1, N_DEV):
            peer, pidx = _peer(k)
            send = pltpu.make_async_remote_copy(
                src_ref=stage, dst_ref=full_ref.at[me], send_sem=send_sems.at[k - 1], recv_sem=recv_sems.at[k - 1],
                device_id=peer, device_id_type=MESH_ID)
            recv = pltpu.make_async_remote_copy(
                src_ref=stage, dst_ref=full_ref.at[pidx], send_sem=send_sems.at[k - 1], recv_sem=recv_sems.at[k - 1],
                device_id=peer, device_id_type=MESH_ID)
            out.append((send, recv))
        return out

    def start(ins, outs, scr):
        stage, send_sems, recv_sems, own_sem = scr
        stage[...] = ins[0][...].astype(BF16)
        pltpu.make_async_copy(stage, outs[0].at[_me()[3]], own_sem).start()
        for send, _ in copies(outs[0], stage, send_sems, recv_sems):
            send.start()

    def finish(ins, outs, scr):
        stage, send_sems, recv_sems, own_sem = scr
        pltpu.make_async_copy(stage, outs[0].at[_me()[3]], own_sem).wait()
        pairs = copies(outs[0], stage, send_sems, recv_sems)
        for _, recv in pairs:
            recv.wait_recv()
        for send, _ in pairs:
            send.wait_send()

    shape = SHARD_SHAPES[3]
    return Rider(
        args=[w_out], in_specs=[pl.BlockSpec(shape, lambda t, *_: (0, 0))],
        out_shape=[jax.ShapeDtypeStruct((N_DEV,) + shape, BF16)], out_specs=[pl.BlockSpec(memory_space=pl.ANY)],
        scratch=[pltpu.VMEM(shape, BF16), pltpu.SemaphoreType.DMA((N_DEV - 1,)), pltpu.SemaphoreType.DMA((N_DEV - 1,)),
                 pltpu.SemaphoreType.DMA],
        start=start, finish=finish)


def _scatter_g_out_rider(blocks):
    def copies(src_ref, dst_ref, send_sems, recv_sems):
        out = []
        for k in range(1, N_DEV):
            peer, pidx = _peer(k)
            out.append(pltpu.make_async_remote_copy(
                src_ref=src_ref.at[pidx], dst_ref=dst_ref.at[k], send_sem=send_sems.at[k - 1],
                recv_sem=recv_sems.at[k - 1], device_id=peer, device_id_type=MESH_ID))
        return out

    def start(ins, outs, scr):
        send_sems, recv_sems, own_sem = scr
        pltpu.make_async_copy(ins[0].at[_me()[3]], outs[0].at[0], own_sem).start()
        for cp in copies(ins[0], outs[0], send_sems, recv_sems):
            cp.start()

    def finish(ins, outs, scr):
        send_sems, recv_sems, own_sem = scr
        pltpu.make_async_copy(ins[0].at[_me()[3]], outs[0].at[0], own_sem).wait()
        for cp in copies(ins[0], outs[0], send_sems, recv_sems):
            cp.wait()

    hbm = pl.BlockSpec(memory_space=pl.ANY)
    return Rider(
        args=[blocks], in_specs=[hbm], out_shape=[jax.ShapeDtypeStruct(blocks.shape, blocks.dtype)], out_specs=[hbm],
        scratch=[pltpu.SemaphoreType.DMA((N_DEV - 1,)), pltpu.SemaphoreType.DMA((N_DEV - 1,)), pltpu.SemaphoreType.DMA],
        start=start, finish=finish)


def _adamw(w, g, m, v):
    m = ADAM_B1 * m + (1.0 - ADAM_B1) * g
    v = ADAM_B2 * v + (1.0 - ADAM_B2) * jnp.square(g)
    m_hat = m / (1.0 - ADAM_B1 ** ADAM_STEP)
    v_hat = v / (1.0 - ADAM_B2 ** ADAM_STEP)
    delta = -ADAM_LR * (m_hat / (jnp.sqrt(v_hat) + ADAM_EPS) + ADAM_WD * w)
    return delta, m, v


def _reduce_grads_rider(grads3, arrived, small_part, mid_step):
    n = len(grads3)

    class Refs:
        def __init__(self, ins, outs, scr):
            self.g3, self.arr, self.sp = ins[0:n], ins[n], ins[n + 1]
            self.gsum, self.gsum_out, self.ssum = outs[0:n], outs[n], outs[n + 1]
            self.own, self.sib, self.part, self.ici = scr[0:n], scr[n:2 * n], scr[2 * n:3 * n], scr[3 * n:4 * n]
            self.rsmall = scr[4 * n]
            (self.loc_sems, self.d2d_send, self.d2d_recv, self.ici_send, self.ici_recv,
             self.sm_send, self.sm_recv) = scr[4 * n + 1:]
            self.x, self.y, self.c, self.me = _me()
            self.chips = [(1 - self.x, self.y), (self.x, 1 - self.y), (1 - self.x, 1 - self.y)]

        def small(self):
            return [pltpu.make_async_remote_copy(
                src_ref=self.rsmall.at[0], dst_ref=self.rsmall.at[k], send_sem=self.sm_send.at[k - 1],
                recv_sem=self.sm_recv.at[k - 1], device_id=_peer(k)[0], device_id_type=MESH_ID)
                for k in range(1, N_DEV)]

        def level1(self):
            local, to_sib = [], []
            for t in range(n):
                for q in range(4):
                    local.append(pltpu.make_async_copy(
                        self.g3[t].at[2 * q + self.c], self.own[t].at[q], self.loc_sems.at[t, q]))
                    to_sib.append(pltpu.make_async_remote_copy(
                        src_ref=self.g3[t].at[2 * q + 1 - self.c], dst_ref=self.sib[t].at[q],
                        send_sem=self.d2d_send.at[t, q], recv_sem=self.d2d_recv.at[t, q],
                        device_id=(self.x, self.y, 1 - self.c), device_id_type=MESH_ID))
            return local, to_sib

        def level2(self):
            return [pltpu.make_async_remote_copy(
                src_ref=self.part[t].at[2 * px + py], dst_ref=self.ici[t].at[j], send_sem=self.ici_send.at[t, j],
                recv_sem=self.ici_recv.at[t, j], device_id=(px, py, self.c), device_id_type=MESH_ID)
                for t in range(n) for j, (px, py) in enumerate(self.chips)]

    def chunks(t, fn):
        rows = ADAM_ROWS[t]

        def step(i, carry):
            fn(pl.ds(pl.multiple_of(i * rows, rows), rows))
            return carry

        lax.fori_loop(0, SHARD_SHAPES[t][0] // rows, step, 0)

    def start(*refs):
        r = Refs(*refs)
        r.rsmall[0] = r.sp[...]
        local, to_sib = r.level1()
        for cp in r.small() + local + to_sib:
            cp.start()

    def middle(*refs):
        r = Refs(*refs)
        local, to_sib = r.level1()
        for cp in local:
            cp.wait()
        for cp in to_sib:
            cp.wait_recv()
        my_chip = 2 * r.x + r.y
        for t in range(n):
            def pair_sums(rows, t=t):
                for q in range(4):
                    r.part[t][q, rows, :] = (
                        r.own[t][q, rows, :].astype(F32) + r.sib[t][q, rows, :].astype(F32)).astype(BF16)
                r.gsum[t][rows, :] = r.own[t][my_chip, rows, :].astype(F32) + r.sib[t][my_chip, rows, :].astype(F32)

            chunks(t, pair_sums)
        for cp in r.level2():
            cp.start()

        def add_arrived(rows):
            g = r.arr[0, rows, :].astype(F32)
            for k in range(1, N_DEV):
                g = g + r.arr[k, rows, :].astype(F32)
            r.gsum_out[rows, :] = g

        chunks(3, add_arrived)

    def finish(*refs):
        r = Refs(*refs)
        to_chips = r.level2()
        for cp in to_chips:
            cp.wait_recv()
        for t in range(n):
            def add_chips(rows, t=t):
                g = r.gsum[t][rows, :]
                for j in range(3):
                    g = g + r.ici[t][j, rows, :].astype(F32)
                r.gsum[t][rows, :] = g

            chunks(t, add_chips)
        small = r.small()
        for cp in small:
            cp.wait_recv()
        tot = r.rsmall[r.me]
        for d in range(1, N_DEV):
            tot = tot + r.rsmall[jnp.bitwise_xor(r.me, d)]
        r.ssum[...] = tot
        for cp in small + r.level1()[1] + to_chips:
            cp.wait_send()

    hbm = pl.BlockSpec(memory_space=pl.ANY)
    dma = pltpu.SemaphoreType.DMA

    def whole(shape):
        return pl.BlockSpec(shape, lambda i: (0,) * len(shape))

    out_shapes = list(SHARD_SHAPES) + [(8, D_MODEL)]
    return Rider(
        args=list(grads3) + [arrived, small_part],
        in_specs=[hbm] * n + [whole(arrived.shape), whole(small_part.shape)],
        out_shape=[jax.ShapeDtypeStruct(s, F32) for s in out_shapes], out_specs=[whole(s) for s in out_shapes],
        scratch=[pltpu.VMEM((slots,) + s, BF16) for slots in (4, 4, 4, 3) for s in SHARD_SHAPES[:n]]
        + [pltpu.VMEM((N_DEV, 8, D_MODEL), F32), dma((n, 4)), dma((n, 4)), dma((n, 4)), dma((n, 3)), dma((n, 3)),
           dma((N_DEV - 1,)), dma((N_DEV - 1,))],
        start=start, finish=finish, stages=((mid_step, middle),))


def _adamw_update(grads, small_grad, wmv, small_wmv):
    n_small = len(small_wmv)

    def body(*refs):
        g_refs, sg_ref = refs[0:4], refs[4]
        wmv_refs = [refs[5 + 3 * t:8 + 3 * t] for t in range(4)]
        swmv_refs = [refs[17 + 3 * t:20 + 3 * t] for t in range(n_small)]
        outs = refs[17 + 3 * n_small:]
        out_refs = [outs[4 * t:4 * t + 4] for t in range(4)]
        sout_refs = [outs[16 + 4 * t:20 + 4 * t] for t in range(n_small)]
        loss_ref = outs[16 + 4 * n_small]
        for t, (w_ref, m_ref, v_ref) in enumerate(swmv_refs):
            g = sg_ref[t:t + 1, :w_ref.shape[1]]
            delta, m, v = _adamw(w_ref[...], g, m_ref[...], v_ref[...])
            sout_refs[t][0][...], sout_refs[t][1][...], sout_refs[t][2][...], sout_refs[t][3][...] = g, delta, m, v
        loss_ref[...] = (0.5 / D_MODEL) * jnp.sum(sg_ref[n_small:n_small + 1, :], axis=1, keepdims=True)
        for t in range(4):
            rows = ADAM_ROWS[t]
            w_ref, m_ref, v_ref = wmv_refs[t]
            g_out, d_out, m_out, v_out = out_refs[t]

            def step(i, carry, g_ref=g_refs[t], rows=rows, w_ref=w_ref, m_ref=m_ref, v_ref=v_ref,
                     g_out=g_out, d_out=d_out, m_out=m_out, v_out=v_out):
                r = pl.ds(pl.multiple_of(i * rows, rows), rows)
                g = g_ref[r, :]
                delta, m, v = _adamw(w_ref[r, :], g, m_ref[r, :], v_ref[r, :])
                g_out[r, :], d_out[r, :], m_out[r, :], v_out[r, :] = g, delta, m, v
                return carry

            lax.fori_loop(0, SHARD_SHAPES[t][0] // rows, step, 0)

    vmem = pl.BlockSpec(memory_space=pltpu.VMEM)
    flat_wmv = [a for trio in wmv for a in trio]
    flat_small = [a for trio in small_wmv for a in trio]
    out_shape = ([jax.ShapeDtypeStruct(s, F32) for s in SHARD_SHAPES for _ in range(4)]
                 + [jax.ShapeDtypeStruct(trio[0].shape, F32) for trio in small_wmv for _ in range(4)]
                 + [jax.ShapeDtypeStruct((1, 1), F32)])
    return _pcall(
        body, name="adamw",
        in_specs=[vmem] * (5 + len(flat_wmv) + len(flat_small)), out_specs=[vmem] * len(out_shape),
        out_shape=out_shape,
        compiler_params=_cparams(),
    )(*grads, small_grad, *flat_wmv, *flat_small)


def _small_rows(ln_g, ln_b, q_norm_g, kv_norm_g, extra=None):
    pad = lambda a: jnp.pad(a, (0, D_MODEL - a.shape[0]))
    rows = [ln_g, ln_b, pad(q_norm_g), pad(kv_norm_g)] + ([] if extra is None else [extra])
    return jnp.pad(jnp.stack(rows), ((0, 8 - len(rows)), (0, 0)))


def kernel(x, w_in, q_norm_g, kv_norm_g, w_uq, w_ukv, w_out, ln_g, ln_b, loss_target, m_w_in, m_q_norm_g, m_kv_norm_g, m_w_uq, m_w_ukv, m_w_out, m_ln_g, m_ln_b, v_w_in, v_q_norm_g, v_kv_norm_g, v_w_uq, v_w_ukv, v_w_out, v_ln_g, v_ln_b):
    w_in_r, w_uq_r, w_ukv_r = _all_gather_weights([w_in, w_uq, w_ukv])
    grad_x, sums = _local_step(
        x[0], loss_target[0], w_in_r, w_uq_r, w_ukv_r, _gather_w_out_rider(w_out), _scatter_g_out_rider,
        _reduce_grads_rider, q_norm_g, kv_norm_g, ln_g, ln_b)
    row = lambda a: a.reshape(1, -1)
    small_wmv = [(row(ln_g), row(m_ln_g), row(v_ln_g)), (row(ln_b), row(m_ln_b), row(v_ln_b)),
                 (row(q_norm_g), row(m_q_norm_g), row(v_q_norm_g)), (row(kv_norm_g), row(m_kv_norm_g), row(v_kv_norm_g))]
    wmv = [(w_in, m_w_in, v_w_in), (w_uq, m_w_uq, v_w_uq), (w_ukv, m_w_ukv, v_w_ukv), (w_out, m_w_out, v_w_out)]
    res = _adamw_update(sums[:4], sums[4], wmv, small_wmv)
    big = [res[4 * t:4 * t + 4] for t in range(4)]
    small = [[a.reshape(-1) for a in res[16 + 4 * t:20 + 4 * t]] for t in range(4)]
    loss = res[32].reshape(())

    def group(kind):
        return (big[0][kind], small[2][kind], small[3][kind], big[1][kind], big[2][kind], big[3][kind],
                small[0][kind], small[1][kind])

    return (loss, grad_x[None], *group(0), *group(1), *group(2), *group(3))
```

```python
import functools
from typing import Callable, NamedTuple

import numpy as np
import jax
import jax.numpy as jnp
from jax import lax
from jax.experimental import pallas as pl
from jax.experimental.pallas import tpu as pltpu

F32 = jnp.float32
BF16 = jnp.bfloat16

D_MODEL = 1024
ROPE_THETA = 500000.0
NEG = -1e30
RMS_EPS = 1e-6
LN_EPS = 1e-5
HEADS = 8
MLA_NOPE = 64
MLA_ROPE = 32
MLA_V = 64
Q_RANK = 384
KV_RANK = 256
DIL_HEAD = 64
DIL_ROT = 16
DIL_CONFIGS = ((128, 1), (512, 4), (2048, 16))
DIL_NEAR = 512
HW = HEADS * 64
QW = HW + HEADS * MLA_ROPE
IN_SPLITS = (Q_RANK, KV_RANK, MLA_ROPE, HW, HW, HW, HW, HW)
IN_WIDTH = sum(IN_SPLITS)
ALPHA = 2.0 ** 0.25
MLA_SCALE = (MLA_NOPE + MLA_ROPE) ** -0.5
DIL_SCALE = DIL_HEAD ** -0.5
LOG2E = 1.4426950408889634
LN2 = 0.6931471805599453

ADAM_LR = 0.001
ADAM_B1 = 0.9
ADAM_B2 = 0.999
ADAM_EPS = 1e-08
ADAM_WD = 0.01
ADAM_STEP = 10

N_DEV = 8
LANES = 128
VMEM_LIMIT = 56 * 1024 * 1024
BLOCK_TOKENS = 512
BLOCK_MLA = 512
BLOCK_DIL = 512

C_CQ, C_CKV, C_KR, C_GA, C_QB, C_KB, C_VB, C_GB, C_END = 0, 384, 640, 768, 1280, 1792, 2304, 2816, 3328

NT = (((1,), (1,)), ((), ()))
TN = (((0,), (0,)), ((), ()))


def _pcall(body, **kw):
    return pl.pallas_call(body, **kw)


def _cparams(**kw):
    return pltpu.CompilerParams(vmem_limit_bytes=VMEM_LIMIT, **kw)


def _rope_tables(seq):
    def tabs(dim, period):
        half = dim // 2
        inv = np.float32(ROPE_THETA) ** (-np.arange(0, dim, 2, dtype=np.float32) / np.float32(dim))
        ang = np.arange(seq, dtype=np.float32)[:, None] * inv.astype(np.float32)[None, :]
        cos, sin = np.cos(ang).astype(np.float32), np.sin(ang).astype(np.float32)
        j = np.arange(LANES) % period
        f = j % half
        c = np.where(j < dim, cos[:, f], np.float32(1.0))
        s1 = np.where(j < half, -sin[:, f], np.float32(0.0))
        s2 = np.where((j >= half) & (j < dim), sin[:, f], np.float32(0.0))
        return [c, s1, s2]
    return np.stack(tabs(MLA_ROPE, MLA_ROPE) + tabs(DIL_ROT, DIL_HEAD)).astype(np.float32)


def _rope(t, c, s1, s2, half):
    return t * c + pltpu.roll(t, LANES - half, 1) * s1 + pltpu.roll(t, half, 1) * s2


def _rope_t(d, c, s1, s2, half):
    return d * c + pltpu.roll(d * s1, half, 1) + pltpu.roll(d * s2, LANES - half, 1)


def _rope_wide(fn, t, c, s1, s2, half):
    return jnp.concatenate(
        [fn(t[:, i:i + LANES], c, s1, s2, half) for i in range(0, t.shape[1], LANES)], axis=1)


def _mla_bias_t(blk):
    a = np.arange(blk)
    causal = np.where(a[:, None] <= a[None, :], 0.0, NEG)
    return np.stack([np.zeros((blk, blk)), causal]).astype(np.float32)


def _dil_bias_t(blk, reach):
    a = np.arange(blk)
    out = []
    for off in range(-(-reach // blk) + 1):
        delta = blk * off + a[None, :] - a[:, None]
        mult = np.zeros((blk, blk))
        for window, dil in DIL_CONFIGS:
            mult += (delta >= 0) & (delta % dil == 0) & (delta <= min(window, reach))
        out.append(np.where(mult > 0, np.log2(np.maximum(mult, 1.0)), NEG))
    return np.stack(out).astype(np.float32)


def _dil_far_bias_t(length):
    window, dil = DIL_CONFIGS[-1]
    a = np.arange(length)
    steps_back = a[None, :] - a[:, None]
    seen = (steps_back * dil > DIL_NEAR) & (steps_back * dil <= window)
    return np.where(seen, 0.0, NEG).astype(np.float32)[None]


def _lanes_to_classes(a, dil):
    h, s = a.shape
    return a.reshape(h, s // dil, dil).transpose(0, 2, 1).reshape(h, s)


def _lanes_from_classes(a, dil):
    h, s = a.shape
    return a.reshape(h, dil, s // dil).transpose(0, 2, 1).reshape(h, s)


def _steps(nq, span, by_key, diag_only_bias):
    rows = []
    if by_key:
        for ki in range(nq):
            hi = min(nq - 1, ki + span)
            for qi in range(ki, hi + 1):
                rows.append((qi, ki, int(qi == ki), int(qi == hi)))
    else:
        for qi in range(nq):
            lo = max(0, qi - span)
            for ki in range(lo, qi + 1):
                rows.append((qi, ki, int(ki == lo), int(ki == qi)))
    arr = np.array(rows, dtype=np.int32)
    off = arr[:, 0] - arr[:, 1]
    bias_idx = (off == 0).astype(np.int32) if diag_only_bias else off.astype(np.int32)
    return [jnp.asarray(v) for v in (arr[:, 0], arr[:, 1], bias_idx, arr[:, 2], arr[:, 3])]


def _by_class(val, out_ref, lanes_sc):
    n_cls, per = out_ref.shape[0], out_ref.shape[1]
    for c in range(val.shape[1] // LANES):
        lanes_sc[c] = val[:, LANES * c:LANES * (c + 1)]
        for r in range(n_cls):
            rows = lanes_sc.at[c][pl.ds(r, per, stride=n_cls), :]
            out_ref[r, :, LANES * c:LANES * (c + 1)] = rows.astype(out_ref.dtype)


def _in_sequence(ref, lanes_sc):
    n_cls, per, width = ref.shape
    for c in range(width // LANES):
        for r in range(n_cls):
            lanes_sc.at[c][pl.ds(r, per, stride=n_cls), :] = ref[r, :, LANES * c:LANES * (c + 1)].astype(F32)
    return jnp.concatenate([lanes_sc[c] for c in range(width // LANES)], axis=1)


def _fwd_proj(x, w_in_r, w_uq_r, w_ukv_r, qg, kvg, tabs, bt, n_cls):
    seq = x.shape[0]

    def body(x_ref, win_ref, wuq_ref, wukv_ref, qg_ref, kvg_ref, tab_ref,
             cq_ref, ckv_ref, qn_ref, kvn_ref, qcat_ref, kn_ref, kpe_ref, v_ref,
             ga_ref, gb_ref, qb_ref, kb_ref, vb_ref, knt_ref, kpet_ref, vt_ref, kbt_ref, vbt_ref,
             qbc_ref, kbc_ref, vbc_ref, lanes_sc):
        xb = x_ref[...].astype(BF16)

        def proj(lo, hi):
            return jnp.dot(xb, win_ref[:, lo:hi], preferred_element_type=F32)

        m_tabs = (tab_ref[0], tab_ref[1], tab_ref[2])
        d_tabs = (tab_ref[3], tab_ref[4], tab_ref[5])

        def use_cq(cq):
            cq_ref[...] = cq
            qn = (cq * lax.rsqrt(jnp.mean(cq * cq, axis=1, keepdims=True) + RMS_EPS) * qg_ref[...]).astype(BF16)
            qn_ref[...] = qn
            q = jnp.dot(qn, wuq_ref[...], preferred_element_type=F32)
            qcat_ref[:, :HW] = (q[:, :HW] * (MLA_SCALE * LOG2E)).astype(BF16)
            qcat_ref[:, HW:] = (
                _rope_wide(_rope, q[:, HW:], *m_tabs, MLA_ROPE // 2) * (MLA_SCALE * LOG2E)).astype(BF16)

        def use_ckv(ckv):
            ckv_ref[...] = ckv
            kvn = (ckv * lax.rsqrt(jnp.mean(ckv * ckv, axis=1, keepdims=True) + RMS_EPS) * kvg_ref[...]).astype(BF16)
            kvn_ref[...] = kvn
            kv = jnp.dot(kvn, wukv_ref[...], preferred_element_type=F32)
            kn_ref[...] = kv[:, :HW].astype(BF16)
            v_ref[...] = kv[:, HW:].astype(BF16)
            knt_ref[...] = kv[:, :HW].T.astype(BF16)
            vt_ref[...] = kv[:, HW:].T.astype(BF16)

        def use_kr(kr):
            kpe = _rope(kr, *m_tabs, MLA_ROPE // 2)
            kpe_ref[...] = kpe.astype(BF16)
            kpet_ref[...] = kpe.T[:MLA_ROPE, :].astype(BF16)

        def use_ga(ga):
            ga_ref[...] = ga

        def use_qb(qb):
            qb = _rope_wide(_rope, qb, *d_tabs, DIL_ROT // 2) * (DIL_SCALE * LOG2E)
            qb_ref[...] = qb.astype(BF16)
            _by_class(qb, qbc_ref, lanes_sc)

        def use_kb(kb):
            kb = _rope_wide(_rope, kb, *d_tabs, DIL_ROT // 2)
            kb_ref[...] = kb.astype(BF16)
            kbt_ref[...] = kb.T.astype(BF16)
            _by_class(kb, kbc_ref, lanes_sc)

        def use_vb(vb):
            vb_ref[...] = vb.astype(BF16)
            vbt_ref[...] = vb.T.astype(BF16)
            _by_class(vb, vbc_ref, lanes_sc)

        def use_gb(gb):
            gb_ref[...] = gb

        pieces = [(C_CQ, C_CKV, use_cq), (C_CKV, C_KR, use_ckv), (C_KR, C_GA, use_kr), (C_GA, C_QB, use_ga),
                  (C_QB, C_KB, use_qb), (C_KB, C_VB, use_kb), (C_VB, C_GB, use_vb), (C_GB, C_END, use_gb)]
        ahead = proj(*pieces[0][:2])
        for n, (_, _, use) in enumerate(pieces):
            cur = ahead
            if n + 1 < len(pieces):
                ahead = proj(*pieces[n + 1][:2])
            use(cur)

    def tok(width):
        return pl.BlockSpec((bt, width), lambda i: (i, 0))

    def tok_t(height):
        return pl.BlockSpec((height, bt), lambda i: (0, i))

    def full(a):
        return pl.BlockSpec(a.shape, lambda i: (0,) * a.ndim)

    outs = [(Q_RANK, F32), (KV_RANK, F32), (Q_RANK, BF16), (KV_RANK, BF16), (QW, BF16), (HW, BF16),
            (LANES, BF16), (HW, BF16), (HW, F32), (HW, F32), (HW, BF16), (HW, BF16), (HW, BF16)]
    outs_t = [HW, MLA_ROPE, HW, HW, HW]
    by_class = pl.BlockSpec((n_cls, bt // n_cls, HW), lambda i: (0, i, 0))
    return _pcall(
        body, name="fwd_proj", grid=(seq // bt,),
        in_specs=[tok(D_MODEL), full(w_in_r), full(w_uq_r), full(w_ukv_r), full(qg), full(kvg),
                  pl.BlockSpec((6, bt, LANES), lambda i: (0, i, 0))],
        out_specs=[tok(w) for w, _ in outs] + [tok_t(h) for h in outs_t] + [by_class] * 3,
        out_shape=[jax.ShapeDtypeStruct((seq, w), dt) for w, dt in outs]
        + [jax.ShapeDtypeStruct((h, seq), BF16) for h in outs_t]
        + [jax.ShapeDtypeStruct((n_cls, seq // n_cls, HW), BF16)] * 3,
        scratch_shapes=[pltpu.VMEM((HW // LANES, bt, LANES), F32)],
        compiler_params=_cparams(dimension_semantics=("arbitrary",)),
    )(x, w_in_r, w_uq_r, w_ukv_r, qg, kvg, tabs)


def _head_masks(lane, h):
    e, g = h % 2, h % 4
    me = (lane >= 64 * e) & (lane < 64 * e + 64)
    mr = (lane >= 32 * g) & (lane < 32 * g + 32)
    return me, mr


def _masked(mask, a):
    return jnp.where(mask, a, jnp.zeros_like(a))


def _pair_operands(q_ref, k_ref, kpe_ref, lane, j, ks=slice(None), qs=slice(None)):
    cols = slice(LANES * j, LANES * (j + 1))
    qc = q_ref[qs, cols]
    kj = k_ref[ks, cols]
    kes = []
    for h in (2 * j, 2 * j + 1):
        me, mr = _head_masks(lane, h)
        ke = _masked(me, kj)
        if kpe_ref is not None:
            ke = jnp.concatenate([ke, _masked(mr, kpe_ref[ks, :])], axis=1)
        kes.append(ke)
    if kpe_ref is not None:
        qc = jnp.concatenate([qc, q_ref[qs, HW + LANES * (j // 2):HW + LANES * (j // 2 + 1)]], axis=1)
    return qc, kes


def _tile_variants(bias_t):
    out = {}
    for i, tile in enumerate(np.asarray(bias_t)):
        h = tile.shape[0] // 2
        skip = 1 if (tile[h:, :h] == NEG).all() else 2 if (tile[:h, h:] == NEG).all() else 0
        out[i] = (bool((tile != 0).any()), skip)
    return out


def _tile_parts(blk, skip):
    lo, hi, full = slice(0, blk // 2), slice(blk // 2, blk), slice(0, blk)
    return {0: [(full, full)], 1: [(lo, full), (hi, hi)], 2: [(hi, full), (lo, lo)]}[skip]


class Rider(NamedTuple):
    args: list
    in_specs: list
    out_shape: list
    out_specs: list
    scratch: list
    start: Callable
    finish: Callable
    stages: tuple = ()


def _ride_along(body, ride, n_prefetch, n_in, n_out, n_scratch, n_steps):
    if ride is None:
        return body

    def wrapped(*refs):
        pre, rest = refs[:n_prefetch], refs[n_prefetch:]
        a = n_in
        b = a + len(ride.args)
        c = b + n_out
        d = c + len(ride.out_shape)
        e = d + n_scratch
        mine = (rest[a:b], rest[c:d], rest[e:])
        t = pl.program_id(0)
        pl.when(t == 0)(lambda: ride.start(*mine))
        for at, stage in ride.stages:
            pl.when(t == at)(functools.partial(stage, *mine))
        body(*pre, *rest[:a], *rest[b:c], *rest[d:e])
        pl.when(t == n_steps - 1)(lambda: ride.finish(*mine))

    return wrapped


def _attn_fwd(name, q, k, kpe, vt, bias_t, steps, blk, ride=None, v_token_major=False):
    seq = q.shape[0]
    mla = kpe is not None
    n_steps = int(steps[0].shape[0])
    variants = _tile_variants(bias_t)

    def body(qi_r, ki_r, bi_r, fi_r, la_r, *refs):
        if mla:
            q_ref, k_ref, kpe_ref, vt_ref, b_ref, o_ref, lse_ref, m_sc, l_sc, acc_sc, st_sc = refs
        else:
            q_ref, k_ref, vt_ref, b_ref, o_ref, lse_ref, m_sc, l_sc, acc_sc, st_sc = refs
        t = pl.program_id(0)

        @pl.when(fi_r[t] == 1)
        def _():
            m_sc[...] = jnp.full(m_sc.shape, NEG, F32)
            l_sc[...] = jnp.zeros(l_sc.shape, F32)
            acc_sc[...] = jnp.zeros(acc_sc.shape, F32)

        lane = lax.broadcasted_iota(jnp.int32, (1, LANES), 1)
        if v_token_major:
            vt_all = vt_ref[...].astype(F32).T.astype(BF16)
            vt_rows = lambda rows, ks: vt_all[rows, ks]
        else:
            vt_rows = lambda rows, ks: vt_ref[rows, ks]

        def tile_pass(ks, qs, with_bias):
            nk, nq = ks.stop - ks.start, qs.stop - qs.start
            ones = jnp.ones((16, nk), BF16)

            def pair_scores(j):
                qc, kes = _pair_operands(q_ref, k_ref, kpe_ref if mla else None, lane, j, ks, qs)
                st = lax.dot_general(jnp.concatenate(kes, axis=0), qc, NT, preferred_element_type=F32)
                maxes = []
                for e in range(2):
                    se = st[e * nk:(e + 1) * nk]
                    if with_bias:
                        se = se + b_ref[0, ks, qs]
                    st_sc[j % 2, e * nk:(e + 1) * nk, 0:nq] = se
                    maxes.append(jnp.max(se, axis=0, keepdims=True))
                return maxes

            def softmax_pv(h, col_max):
                st = st_sc[(h // 2) % 2, (h % 2) * nk:(h % 2 + 1) * nk, 0:nq]
                hrow = slice(h, h + 1)
                m_prev = m_sc[hrow, qs]
                m_new = jnp.maximum(m_prev, col_max)
                alpha = jnp.exp2(m_prev - m_new)
                pt = jnp.exp2(st - m_new).astype(BF16)
                m_sc[hrow, qs] = m_new
                rows = slice(64 * h, 64 * h + 64)
                res = jnp.dot(jnp.concatenate([vt_rows(rows, ks), ones], axis=0), pt, preferred_element_type=F32)
                acc_sc[rows, qs] = alpha * acc_sc[rows, qs] + res[:64]
                l_sc[hrow, qs] = alpha * l_sc[hrow, qs] + res[64:65]

            maxes = pair_scores(0)
            for j in range(HEADS // 2):
                cur = maxes
                if j + 1 < HEADS // 2:
                    maxes = pair_scores(j + 1)
                softmax_pv(2 * j, cur[0])
                softmax_pv(2 * j + 1, cur[1])

        def step(with_bias, skip):
            for ks, qs in _tile_parts(blk, skip):
                tile_pass(ks, qs, with_bias)

        for idx, (with_bias, skip) in variants.items():
            if len(variants) == 1:
                step(with_bias, skip)
            else:
                pl.when(bi_r[t] == idx)(functools.partial(step, with_bias, skip))

        @pl.when(la_r[t] == 1)
        def _():
            for h in range(HEADS):
                rows = slice(64 * h, 64 * h + 64)
                acc_sc[rows, :] = acc_sc[rows, :] / l_sc[h:h + 1, :]
            o_ref[...] = acc_sc[...].T
            lse_ref[...] = m_sc[...] + jnp.log2(l_sc[...])

    qmap = lambda t, qi, ki, bi, fi, la: (qi[t], 0)
    kmap = lambda t, qi, ki, bi, fi, la: (ki[t], 0)
    in_specs = [pl.BlockSpec((blk, q.shape[1]), qmap), pl.BlockSpec((blk, HW), kmap)]
    args = [q, k]
    if mla:
        in_specs.append(pl.BlockSpec((blk, LANES), kmap))
        args.append(kpe)
    in_specs += [pl.BlockSpec((blk, HW), kmap) if v_token_major else
                 pl.BlockSpec((HW, blk), lambda t, qi, ki, bi, fi, la: (0, ki[t])),
                 pl.BlockSpec((1, blk, blk), lambda t, qi, ki, bi, fi, la: (bi[t], 0, 0))]
    args += [vt, jnp.asarray(bias_t)]
    out_specs = [pl.BlockSpec((blk, HW), qmap), pl.BlockSpec((HEADS, blk), lambda t, qi, ki, bi, fi, la: (0, qi[t]))]
    out_shape = [jax.ShapeDtypeStruct((seq, HW), F32), jax.ShapeDtypeStruct((HEADS, seq), F32)]
    scratch = [pltpu.VMEM((HEADS, blk), F32), pltpu.VMEM((HEADS, blk), F32),
               pltpu.VMEM((HW, blk), F32), pltpu.VMEM((2, 2 * blk, blk), F32)]
    body = _ride_along(body, ride, 5, len(args), len(out_shape), len(scratch), n_steps)
    if ride is not None:
        args, in_specs = args + ride.args, in_specs + ride.in_specs
        out_specs, out_shape, scratch = out_specs + ride.out_specs, out_shape + ride.out_shape, scratch + ride.scratch
    return _pcall(
        body, name=name,
        grid_spec=pltpu.PrefetchScalarGridSpec(
            num_scalar_prefetch=5, grid=(n_steps,), in_specs=in_specs, out_specs=out_specs, scratch_shapes=scratch),
        out_shape=out_shape,
        compiler_params=_cparams(dimension_semantics=("arbitrary",)),
    )(*steps, *args)


def _attn_bwd(name, q, k, kpe, v, kt, kpet, bias_t, do, lse, dstat, steps, blk, ride=None, single_visit=False):
    assert not (single_visit and kpe is not None) and (kt is not None or single_visit)
    seq = q.shape[0]
    mla = kpe is not None
    qw = q.shape[1]
    n_steps = int(steps[0].shape[0])
    dk_dtype = BF16 if mla else F32
    variants = _tile_variants(bias_t)

    def body(qi_r, ki_r, bi_r, fi_r, la_r, *refs):
        if mla:
            (q_ref, k_ref, kpe_ref, v_ref, kt_ref, kpet_ref, b_ref, do_ref, lse_ref, d_ref,
             dq_ref, dk_ref, dkpe_ref, dv_ref, dk_sc, dkpe_sc, dv_sc, st_sc, dpt_sc) = refs
        else:
            q_ref, k_ref, v_ref, *rest = refs
            kt_ref = rest.pop(0) if kt is not None else None
            b_ref, do_ref, lse_ref, d_ref, dq_out_ref, dk_ref, dv_ref, dk_sc, dv_sc, st_sc, dpt_sc, *rest = rest
            dq_ref = rest[0] if single_visit else dq_out_ref
        t = pl.program_id(0)

        @pl.when(jnp.logical_or(t == 0, single_visit))
        def _():
            dq_ref[...] = jnp.zeros(dq_ref.shape, F32)

        @pl.when(fi_r[t] == 1)
        def _():
            dk_sc[...] = jnp.zeros(dk_sc.shape, F32)
            dv_sc[...] = jnp.zeros(dv_sc.shape, F32)
            if mla:
                dkpe_sc[...] = jnp.zeros(dkpe_sc.shape, F32)

        qi = 0 if single_visit else qi_r[t]
        lane = lax.broadcasted_iota(jnp.int32, (1, LANES), 1)
        if kt is None:
            kt_all = k_ref[...].astype(F32).T.astype(BF16)
            kt_rows = lambda rows, ks: kt_all[rows, ks]
        else:
            kt_rows = lambda rows, ks: kt_ref[rows, ks]

        def tile_pass(ks, qs, with_bias):
            nk, nq = ks.stop - ks.start, qs.stop - qs.start

            def pair_matmuls(j):
                cols = slice(LANES * j, LANES * (j + 1))
                qc, kes = _pair_operands(q_ref, k_ref, kpe_ref if mla else None, lane, j, ks, qs)
                st_sc[j % 2, 0:2 * nk, 0:nq] = lax.dot_general(
                    jnp.concatenate(kes, axis=0), qc, NT, preferred_element_type=F32)
                vj = v_ref[ks, cols]
                ves = [_masked(_head_masks(lane, h)[0], vj) for h in (2 * j, 2 * j + 1)]
                dpt_sc[j % 2, 0:2 * nk, 0:nq] = lax.dot_general(
                    jnp.concatenate(ves, axis=0), do_ref[qs, cols], NT, preferred_element_type=F32)

            def pair_grads(j):
                cols = slice(LANES * j, LANES * (j + 1))
                qj, doj = q_ref[qs, cols], do_ref[qs, cols]
                if mla:
                    qr = q_ref[qs, HW + LANES * (j // 2):HW + LANES * (j // 2 + 1)]
                pts, dsts, qms, doms = [], [], [], []
                for e in range(2):
                    h = 2 * j + e
                    me, mr = _head_masks(lane, h)
                    st = st_sc[j % 2, e * nk:(e + 1) * nk, 0:nq]
                    if with_bias:
                        st = st + b_ref[0, ks, qs]
                    pt = jnp.exp2(st - lse_ref[h:h + 1, qs])
                    dst = (pt * (dpt_sc[j % 2, e * nk:(e + 1) * nk, 0:nq] - d_ref[h:h + 1, qs])).astype(BF16)
                    pts.append(pt.astype(BF16))
                    dsts.append(dst)
                    doms.append(_masked(me, doj))
                    qm = _masked(me, qj)
                    if mla:
                        qm = jnp.concatenate([qm, _masked(mr, qr)], axis=1)
                    qms.append(qm)
                    ktl = kt_rows(slice(64 * h, 64 * h + 64), ks)
                    if mla:
                        ktl = jnp.concatenate([ktl, kpet_ref[:, ks]], axis=0)
                    dqc = jnp.dot(ktl, dst, preferred_element_type=F32)
                    dq_ref[qi, 64 * h:64 * h + 64, qs] += dqc[:64]
                    if mla:
                        dq_ref[qi, HW + MLA_ROPE * h:HW + MLA_ROPE * (h + 1), qs] += dqc[64:]
                dv_sc[ks, cols] += jnp.dot(
                    jnp.concatenate(pts, axis=1), jnp.concatenate(doms, axis=0), preferred_element_type=F32)
                dkc = jnp.dot(jnp.concatenate(dsts, axis=1), jnp.concatenate(qms, axis=0), preferred_element_type=F32)
                dk_sc[ks, cols] += dkc[:, :LANES]
                if mla:
                    dkpe_sc[ks, :] += dkc[:, LANES:]

            pair_matmuls(0)
            for j in range(HEADS // 2):
                if j + 1 < HEADS // 2:
                    pair_matmuls(j + 1)
                pair_grads(j)

        def step(with_bias, skip):
            for ks, qs in _tile_parts(blk, skip):
                tile_pass(ks, qs, with_bias)

        for idx, (with_bias, skip) in variants.items():
            if len(variants) == 1:
                step(with_bias, skip)
            else:
                pl.when(bi_r[t] == idx)(functools.partial(step, with_bias, skip))
        if single_visit:
            dq_out_ref[...] = dq_ref[0].T

        @pl.when(la_r[t] == 1)
        def _():
            dk_ref[...] = (dk_sc[...] * LN2).astype(dk_ref.dtype)
            dv_ref[...] = dv_sc[...].astype(dv_ref.dtype)
            if mla:
                dkpe_ref[...] = dkpe_sc[...] * LN2

    qmap = lambda t, qi, ki, bi, fi, la: (qi[t], 0)
    kmap = lambda t, qi, ki, bi, fi, la: (ki[t], 0)
    qmap_t = lambda t, qi, ki, bi, fi, la: (0, qi[t])
    kmap_t = lambda t, qi, ki, bi, fi, la: (0, ki[t])
    in_specs = [pl.BlockSpec((blk, qw), qmap), pl.BlockSpec((blk, HW), kmap)]
    args = [q, k]
    if mla:
        in_specs.append(pl.BlockSpec((blk, LANES), kmap))
        args.append(kpe)
    in_specs.append(pl.BlockSpec((blk, HW), kmap))
    args.append(v)
    if kt is not None:
        in_specs.append(pl.BlockSpec((HW, blk), kmap_t))
        args.append(kt)
    if mla:
        in_specs.append(pl.BlockSpec((MLA_ROPE, blk), kmap_t))
        args.append(kpet)
    in_specs += [pl.BlockSpec((1, blk, blk), lambda t, qi, ki, bi, fi, la: (bi[t], 0, 0)),
                 pl.BlockSpec((blk, HW), qmap), pl.BlockSpec((HEADS, blk), qmap_t), pl.BlockSpec((HEADS, blk), qmap_t)]
    args += [jnp.asarray(bias_t), do, lse, dstat]
    dq_shape = (seq // blk, qw, blk)
    if single_visit:
        out_specs, out_shape = [pl.BlockSpec((blk, qw), qmap)], [jax.ShapeDtypeStruct((seq, qw), F32)]
    else:
        out_specs = [pl.BlockSpec(dq_shape, lambda t, qi, ki, bi, fi, la: (0, 0, 0))]
        out_shape = [jax.ShapeDtypeStruct(dq_shape, F32)]
    out_specs.append(pl.BlockSpec((blk, HW), kmap))
    out_shape.append(jax.ShapeDtypeStruct((seq, HW), dk_dtype))
    scratch = [pltpu.VMEM((blk, HW), F32)]
    if mla:
        out_specs.append(pl.BlockSpec((blk, LANES), kmap))
        out_shape.append(jax.ShapeDtypeStruct((seq, LANES), F32))
        scratch.append(pltpu.VMEM((blk, LANES), F32))
    out_specs.append(pl.BlockSpec((blk, HW), kmap))
    out_shape.append(jax.ShapeDtypeStruct((seq, HW), BF16))
    scratch.append(pltpu.VMEM((blk, HW), F32))
    scratch += [pltpu.VMEM((2, 2 * blk, blk), F32), pltpu.VMEM((2, 2 * blk, blk), F32)]
    if single_visit:
        scratch.append(pltpu.VMEM((1, qw, blk), F32))
    body = _ride_along(body, ride, 5, len(args), len(out_shape), len(scratch), n_steps)
    if ride is not None:
        args, in_specs = args + ride.args, in_specs + ride.in_specs
        out_specs, out_shape, scratch = out_specs + ride.out_specs, out_shape + ride.out_shape, scratch + ride.scratch
    return _pcall(
        body, name=name,
        grid_spec=pltpu.PrefetchScalarGridSpec(
            num_scalar_prefetch=5, grid=(n_steps,), in_specs=in_specs, out_specs=out_specs,
            scratch_shapes=scratch),
        out_shape=out_shape,
        compiler_params=_cparams(dimension_semantics=("arbitrary",)),
    )(*steps, *args)


def _out_ln(oa, ob_near, ob_far, lse_near, lse_far, ga, gb, x, tgt, w_out, ln_g, ln_b, bt):
    seq = x.shape[0]

    def body(oa_ref, obn_ref, obf_ref, lsen_ref, lsef_ref, ga_ref, gb_ref, x_ref, tgt_ref, w_ref, g_ref, b_ref,
             dz_ref, doa_ref, dob_ref, dga_ref, dgb_ref, da_ref, db_ref, lse_ref, gwb_ref, small_ref, dobc_ref,
             gw_ref, lanes_sc):
        i = pl.program_id(0)

        @pl.when(i == 0)
        def _():
            gw_ref[...] = jnp.zeros(gw_ref.shape, F32)
            small_ref[...] = jnp.zeros(small_ref.shape, F32)

        def gate(g):
            sig = 0.5 * jnp.tanh(0.5 * g) + 0.5
            return g * sig, sig * (1.0 + g * (1.0 - sig))

        lse_n, lse_f = lsen_ref[...], lsef_ref[...]
        top = jnp.maximum(lse_n, lse_f)
        e_n, e_f = jnp.exp2(lse_n - top), jnp.exp2(lse_f - top)
        lse_ref[...] = top + jnp.log2(e_n + e_f)
        inv = 1.0 / (e_n + e_f)
        head_row = lax.broadcasted_iota(jnp.int32, (2 * HEADS, HW), 0) % HEADS
        spread = (head_row == lax.broadcasted_iota(jnp.int32, (2 * HEADS, HW), 1) // 64).astype(BF16)

        def per_lane(w):
            hi = w.astype(BF16)
            lo = (w - hi.astype(F32)).astype(BF16)
            return lax.dot_general(jnp.concatenate([hi, lo], axis=0), spread, TN, preferred_element_type=F32)

        o_b_all = per_lane(e_n * inv) * obn_ref[...] + per_lane(e_f * inv) * _in_sequence(obf_ref, lanes_sc)
        gam = g_ref[...]
        halves = [slice(0, bt // 2), slice(bt // 2, bt)]

        def gates_and_projection(rows):
            o_a, o_b = oa_ref[rows, :], o_b_all[rows]
            sa, dsa = gate(ga_ref[rows, :])
            sb, dsb = gate(gb_ref[rows, :])
            mix = jnp.concatenate([o_a * sa, o_b * sb], axis=1).astype(BF16)
            z = ALPHA * x_ref[rows, :] + jnp.dot(mix, w_ref[...], preferred_element_type=F32)
            return o_a, o_b, sa, dsa, sb, dsb, mix, z

        def norm_and_back(rows, mix, z):
            mu = jnp.mean(z, axis=1, keepdims=True)
            zc = z - mu
            rstd = lax.rsqrt(jnp.mean(zc * zc, axis=1, keepdims=True) + LN_EPS)
            xhat = zc * rstd
            diff = xhat * gam + b_ref[...] - tgt_ref[rows, :]
            dy = diff * (1.0 / D_MODEL)
            small_ref[0:1, :] += jnp.sum(dy * xhat, axis=0, keepdims=True)
            small_ref[1:2, :] += jnp.sum(dy, axis=0, keepdims=True)
            small_ref[2:3, :] += jnp.sum(diff * diff, axis=0, keepdims=True)
            dxh = dy * gam
            dz = rstd * (dxh - jnp.mean(dxh, axis=1, keepdims=True)
                         - xhat * jnp.mean(dxh * xhat, axis=1, keepdims=True))
            dz_ref[rows, :] = dz
            dzb = dz.astype(BF16)
            gw_ref[...] += lax.dot_general(mix, dzb, TN, preferred_element_type=F32)
            return lax.dot_general(dzb, w_ref[...], NT, preferred_element_type=F32)

        def gate_back(rows, o_a, o_b, sa, dsa, sb, dsb, dmix):
            doa, dob = dmix[:, :HW] * sa, dmix[:, HW:] * sb
            doa_ref[rows, :] = doa.astype(BF16)
            dob_ref[rows, :] = dob.astype(BF16)
            dga_ref[rows, :] = (dmix[:, :HW] * o_a * dsa).astype(BF16)
            dgb_ref[rows, :] = (dmix[:, HW:] * o_b * dsb).astype(BF16)
            return dob, doa * o_a, dob * o_b

        fronts = [gates_and_projection(rows) for rows in halves]
        dmixes = [norm_and_back(rows, f[6], f[7]) for rows, f in zip(halves, fronts)]
        backs = [gate_back(rows, *f[:6], dmix) for rows, f, dmix in zip(halves, fronts, dmixes)]
        dob, prod_a, prod_b = (jnp.concatenate(parts, axis=0) for parts in zip(*backs))
        _by_class(dob, dobc_ref, lanes_sc)

        @pl.when(i == seq // bt - 1)
        def _():
            gwb_ref[...] = gw_ref[...].astype(BF16)

        head_of = (lax.broadcasted_iota(jnp.int32, (2 * HW, LANES), 0) % HW) // 64
        ind = (head_of == lax.broadcasted_iota(jnp.int32, (2 * HW, LANES), 1)).astype(BF16)

        def head_sums(prod):
            hi = prod.astype(BF16)
            lo = (prod - hi.astype(F32)).astype(BF16)
            sums = jnp.dot(jnp.concatenate([hi, lo], axis=1), ind, preferred_element_type=F32)
            return sums.T[:HEADS, :]

        da_ref[...] = head_sums(prod_a)
        db_ref[...] = head_sums(prod_b)

    def tok(width):
        return pl.BlockSpec((bt, width), lambda i: (i, 0))

    def full(shape):
        return pl.BlockSpec(shape, lambda i: (0,) * len(shape))

    stat = pl.BlockSpec((HEADS, bt), lambda i: (0, i))
    n_cls = ob_far.shape[0]
    by_class = pl.BlockSpec((n_cls, bt // n_cls, HW), lambda i: (0, i, 0))
    return _pcall(
        body, name="out_ln", grid=(seq // bt,),
        in_specs=[tok(HW), tok(HW), by_class, stat, stat, tok(HW), tok(HW), tok(D_MODEL), tok(D_MODEL),
                  full((D_MODEL, D_MODEL)), full((1, D_MODEL)), full((1, D_MODEL))],
        out_specs=[tok(D_MODEL), tok(HW), tok(HW), tok(HW), tok(HW), stat, stat, stat,
                   full((D_MODEL, D_MODEL)), full((8, D_MODEL)), by_class],
        out_shape=[jax.ShapeDtypeStruct((seq, D_MODEL), F32)] + [jax.ShapeDtypeStruct((seq, HW), BF16)] * 4
        + [jax.ShapeDtypeStruct((HEADS, seq), F32)] * 3
        + [jax.ShapeDtypeStruct((D_MODEL, D_MODEL), BF16), jax.ShapeDtypeStruct((8, D_MODEL), F32),
           jax.ShapeDtypeStruct(ob_far.shape, BF16)],
        scratch_shapes=[pltpu.VMEM((D_MODEL, D_MODEL), F32), pltpu.VMEM((HW // LANES, bt, LANES), F32)],
        compiler_params=_cparams(dimension_semantics=("arbitrary",)),
    )(oa, ob_near, ob_far, lse_near, lse_far, ga, gb, x, tgt, w_out, ln_g, ln_b)


def _bwd_mid(dq_m, dkn, dv, dkpe, dqb, dkb, dvb, far, dga, dgb, cq, ckv, qn, kvn, w_uq_r, w_ukv_r, qg, kvg, tabs, bt):
    n_cls = far[0].shape[0]
    seq = cq.shape[0]

    def body(dqm_ref, dkn_ref, dv_ref, dkpe_ref, dqb_ref, dkb_ref, dvb_ref, dqf_ref, dkf_ref, dvf_ref, dga_ref, dgb_ref,
             cq_ref, ckv_ref, qn_ref, kvn_ref, wuq_ref, wukv_ref, qg_ref, kvg_ref, tab_ref,
             dh_ref, guq3_ref, gukv3_ref, small_ref, seq_sc, guq_ref, gukv_ref):
        i = pl.program_id(0)

        @pl.when(i == 0)
        def _():
            guq_ref[...] = jnp.zeros(guq_ref.shape, F32)
            gukv_ref[...] = jnp.zeros(gukv_ref.shape, F32)
            small_ref[...] = jnp.zeros(small_ref.shape, F32)

        m_tabs = (tab_ref[0], tab_ref[1], tab_ref[2])
        d_tabs = (tab_ref[3], tab_ref[4], tab_ref[5])

        def rms_bwd(c, dn, gain):
            r = lax.rsqrt(jnp.mean(c * c, axis=1, keepdims=True) + RMS_EPS)
            u = dn * gain
            dc = r * u - c * (r * r * r) * jnp.mean(u * c, axis=1, keepdims=True)
            return dc, jnp.sum(dn * c * r, axis=0, keepdims=True)

        dqm = dqm_ref[0].T
        dq = jnp.concatenate(
            [dqm[:, :HW], _rope_wide(_rope_t, dqm[:, HW:], *m_tabs, MLA_ROPE // 2)], axis=1) * MLA_SCALE
        dq = dq.astype(BF16)
        dkv = jnp.concatenate([dkn_ref[...], dv_ref[...]], axis=1)
        guq_ref[...] += lax.dot_general(qn_ref[...], dq, TN, preferred_element_type=F32)
        dqn = lax.dot_general(dq, wuq_ref[...], NT, preferred_element_type=F32)
        gukv_ref[...] += lax.dot_general(kvn_ref[...], dkv, TN, preferred_element_type=F32)
        dkvn = lax.dot_general(dkv, wukv_ref[...], NT, preferred_element_type=F32)

        dh_ref[:, C_KR:C_GA] = _rope_t(dkpe_ref[...], *m_tabs, MLA_ROPE // 2).astype(BF16)
        dh_ref[:, C_GA:C_QB] = dga_ref[...]
        in_sequence = functools.partial(_in_sequence, lanes_sc=seq_sc)
        dqb = dqb_ref[0].T + in_sequence(dqf_ref)
        dh_ref[:, C_QB:C_KB] = (_rope_wide(_rope_t, dqb, *d_tabs, DIL_ROT // 2) * DIL_SCALE).astype(BF16)
        dkb = dkb_ref[...] + in_sequence(dkf_ref)
        dh_ref[:, C_KB:C_VB] = _rope_wide(_rope_t, dkb, *d_tabs, DIL_ROT // 2).astype(BF16)
        dh_ref[:, C_VB:C_GB] = (dvb_ref[...].astype(F32) + in_sequence(dvf_ref)).astype(BF16)
        dh_ref[:, C_GB:C_END] = dgb_ref[...]

        dcq, gq = rms_bwd(cq_ref[...], dqn, qg_ref[...])
        small_ref[0:1, :] += gq
        dckv, gkv = rms_bwd(ckv_ref[...], dkvn, kvg_ref[...])
        small_ref[1:2, :KV_RANK] += gkv
        dh_ref[:, C_CQ:C_CKV] = dcq.astype(BF16)
        dh_ref[:, C_CKV:C_KR] = dckv.astype(BF16)

        @pl.when(i == seq // bt - 1)
        def _():
            for h in range(HEADS):
                guq3_ref[h] = jnp.concatenate(
                    [guq_ref[:, MLA_NOPE * h:MLA_NOPE * (h + 1)],
                     guq_ref[:, HW + MLA_ROPE * h:HW + MLA_ROPE * (h + 1)]], axis=1).astype(BF16)
                gukv3_ref[h] = jnp.concatenate(
                    [gukv_ref[:, MLA_NOPE * h:MLA_NOPE * (h + 1)],
                     gukv_ref[:, HW + MLA_V * h:HW + MLA_V * (h + 1)]], axis=1).astype(BF16)

    def tok(width):
        return pl.BlockSpec((bt, width), lambda i: (i, 0))

    def tok_t(a):
        per = a.shape[2] // bt
        return pl.BlockSpec((1, a.shape[1], bt), lambda i: (i // per, 0, i % per))

    def full(shape):
        return pl.BlockSpec(shape, lambda i: (0,) * len(shape))

    by_class = pl.BlockSpec((n_cls, bt // n_cls, HW), lambda i: (0, i, 0))
    uq3 = (HEADS, Q_RANK, MLA_NOPE + MLA_ROPE)
    ukv3 = (HEADS, KV_RANK, MLA_NOPE + MLA_V)
    return _pcall(
        body, name="bwd_mid", grid=(seq // bt,),
        in_specs=[tok_t(dq_m), tok(HW), tok(HW), tok(LANES), tok_t(dqb), tok(HW), tok(HW), by_class, by_class, by_class,
                  tok(HW), tok(HW),
                  tok(Q_RANK), tok(KV_RANK), tok(Q_RANK), tok(KV_RANK),
                  full(w_uq_r.shape), full(w_ukv_r.shape), full((1, Q_RANK)), full((1, KV_RANK)),
                  pl.BlockSpec((6, bt, LANES), lambda i: (0, i, 0))],
        out_specs=[tok(C_END), full(uq3), full(ukv3), full((8, Q_RANK))],
        out_shape=[jax.ShapeDtypeStruct((seq, C_END), BF16), jax.ShapeDtypeStruct(uq3, BF16),
                   jax.ShapeDtypeStruct(ukv3, BF16), jax.ShapeDtypeStruct((8, Q_RANK), F32)],
        scratch_shapes=[pltpu.VMEM((HW // LANES, bt, LANES), F32), pltpu.VMEM(w_uq_r.shape, F32),
                        pltpu.VMEM(w_ukv_r.shape, F32)],
        compiler_params=_cparams(dimension_semantics=("arbitrary",)),
    )(dq_m, dkn, dv, dkpe, dqb, dkb, dvb, *far, dga, dgb, cq, ckv, qn, kvn, w_uq_r, w_ukv_r, qg, kvg, tabs)


def _grad_x(dz, dh, w_in_r, bt, ride=None):
    seq = dz.shape[0]
    n_steps = seq // bt

    def body(dz_ref, dh_ref, w_ref, gx_ref):
        gx_ref[...] = ALPHA * dz_ref[...] + lax.dot_general(
            dh_ref[...], w_ref[...], NT, preferred_element_type=F32)

    args = [dz, dh, w_in_r]
    in_specs = [pl.BlockSpec((bt, D_MODEL), lambda i: (i, 0)), pl.BlockSpec((bt, C_END), lambda i: (i, 0)),
                pl.BlockSpec(w_in_r.shape, lambda i: (0, 0))]
    out_specs = [pl.BlockSpec((bt, D_MODEL), lambda i: (i, 0))]
    out_shape = [jax.ShapeDtypeStruct((seq, D_MODEL), F32)]
    scratch = []
    body = _ride_along(body, ride, 0, len(args), len(out_shape), 0, n_steps)
    if ride is not None:
        args, in_specs = args + ride.args, in_specs + ride.in_specs
        out_specs, out_shape, scratch = out_specs + ride.out_specs, out_shape + ride.out_shape, ride.scratch
    return _pcall(
        body, name="grad_x", grid=(n_steps,),
        in_specs=in_specs, out_specs=out_specs, out_shape=out_shape, scratch_shapes=scratch,
        compiler_params=_cparams(dimension_semantics=("arbitrary",)),
    )(*args)


def _grad_w_in(x, dh, bt):
    seq = x.shape[0]
    shard = IN_WIDTH // N_DEV
    k_lo, k_hi = IN_SPLITS[0] + IN_SPLITS[1], IN_SPLITS[0] + IN_SPLITS[1] + MLA_ROPE

    def body(x_ref, dh_ref, out_ref, acc):
        i = pl.program_id(0)

        @pl.when(i == 0)
        def _():
            acc[...] = jnp.zeros(acc.shape, F32)

        acc[...] += lax.dot_general(x_ref[...].astype(BF16), dh_ref[...], TN, preferred_element_type=F32)

        @pl.when(i == seq // bt - 1)
        def _():
            kr = acc[:, C_KR:C_GA]
            kr = kr + pltpu.roll(kr, 96, 1) + pltpu.roll(kr, 64, 1) + pltpu.roll(kr, 32, 1)
            for d in range(N_DEV):
                lo, hi = shard * d, shard * (d + 1)
                pieces = []
                if lo < k_lo:
                    pieces.append(acc[:, lo:min(hi, k_lo)])
                if lo < k_hi and hi > k_lo:
                    pieces.append(kr[:, max(lo, k_lo) - k_lo:min(hi, k_hi) - k_lo])
                if hi > k_hi:
                    shift = C_GA - k_hi
                    pieces.append(acc[:, max(lo, k_hi) + shift:hi + shift])
                blk = pieces[0] if len(pieces) == 1 else jnp.concatenate(pieces, axis=1)
                out_ref[d] = blk.astype(BF16)

    return _pcall(
        body, name="grad_w_in", grid=(seq // bt,),
        in_specs=[pl.BlockSpec((bt, D_MODEL), lambda i: (i, 0)), pl.BlockSpec((bt, C_END), lambda i: (i, 0))],
        out_specs=pl.BlockSpec((N_DEV, D_MODEL, shard), lambda i: (0, 0, 0)),
        out_shape=jax.ShapeDtypeStruct((N_DEV, D_MODEL, shard), BF16),
        scratch_shapes=[pltpu.VMEM((D_MODEL, C_END), F32)],
        compiler_params=_cparams(dimension_semantics=("arbitrary",)),
    )(x, dh)


def _local_step(x, tgt, w_in_r, w_uq_r, w_ukv_r, w_out_rider, g_out_rider, reduce_rider, q_norm_g, kv_norm_g,
                ln_g, ln_b, bt=BLOCK_TOKENS, blk_m=BLOCK_MLA, blk_d=BLOCK_DIL):
    seq = x.shape[0]
    tabs = jnp.asarray(_rope_tables(seq))
    qg, kvg = q_norm_g.reshape(1, -1), kv_norm_g.reshape(1, -1)

    far_dil = DIL_CONFIGS[-1][1]
    cls = seq // far_dil
    (cq, ckv, qn, kvn, qcat, kn, kpe, v, ga, gb, qb, kb, vb, knt, kpet, vt, kbt, vbt, qb_c, kb_c, vb_c) = _fwd_proj(
        x, w_in_r, w_uq_r, w_ukv_r, qg, kvg, tabs, bt, far_dil)
    qb_c, kb_c, vb_c = (a.reshape(seq, HW) for a in (qb_c, kb_c, vb_c))

    nq_m, nq_d = seq // blk_m, seq // blk_d
    bias_m = _mla_bias_t(blk_m)
    oa, lse_a, w_out = _attn_fwd(
        "mla_fwd", qcat, kn, kpe, vt, bias_m, _steps(nq_m, nq_m, False, True), blk_m, ride=w_out_rider)

    bias_near = _dil_bias_t(blk_d, DIL_NEAR)
    ob_near, lse_near = _attn_fwd(
        "dil_fwd", qb, kb, None, vbt, bias_near, _steps(nq_d, -(-DIL_NEAR // blk_d), False, False), blk_d)
    each = jnp.arange(far_dil, dtype=jnp.int32)
    steps_far = [each, each, jnp.zeros_like(each), jnp.ones_like(each), jnp.ones_like(each)]
    bias_far = _dil_far_bias_t(cls)
    ob_far, lse_far = _attn_fwd(
        "dil_far_fwd", qb_c, kb_c, None, vb_c, bias_far, steps_far, cls, v_token_major=True)

    dz, doa, dob, dga, dgb, dst_a, dst_b, lse_b, g_out, small1, dob_c = _out_ln(
        oa, ob_near, ob_far.reshape(far_dil, cls, HW), lse_near, _lanes_from_classes(lse_far, far_dil), ga, gb, x, tgt,
        w_out.reshape(D_MODEL, D_MODEL), ln_g.reshape(1, -1), ln_b.reshape(1, -1), bt)

    dq_m, dkn, dkpe, dv, g_out_recv = _attn_bwd(
        "mla_bwd", qcat, kn, kpe, v, knt, kpet, bias_m, doa, lse_a, dst_a, _steps(nq_m, nq_m, True, True), blk_m,
        ride=g_out_rider(g_out.reshape(N_DEV, D_MODEL // N_DEV, D_MODEL)))
    dqb, dkb_near, dvb_near = _attn_bwd(
        "dil_bwd", qb, kb, None, vb, kbt, None, bias_near, dob, lse_b, dst_b,
        _steps(nq_d, -(-DIL_NEAR // blk_d), True, False), blk_d)
    dqb_far, dkb_far, dvb_far = _attn_bwd(
        "dil_far_bwd", qb_c, kb_c, None, vb_c, None, None, bias_far, dob_c.reshape(seq, HW),
        _lanes_to_classes(lse_b, far_dil), _lanes_to_classes(dst_b, far_dil), steps_far, cls, single_visit=True)
    far = [a.reshape(far_dil, cls, HW) for a in (dqb_far, dkb_far, dvb_far)]

    dh, g_uq, g_ukv, small2 = _bwd_mid(
        dq_m, dkn, dv, dkpe, dqb, dkb_near, dvb_near, far, dga, dgb, cq, ckv, qn, kvn, w_uq_r, w_ukv_r, qg, kvg, tabs, bt)
    g_in = _grad_w_in(x, dh, min(seq, 2 * bt))
    grads3 = [g_in, g_uq, g_ukv]
    small_part = _small_rows(small1[0], small1[1], small2[0, :Q_RANK], small2[1, :KV_RANK], small1[2])
    grad_x, *reduced = _grad_x(
        dz, dh, w_in_r, bt, ride=reduce_rider(grads3, g_out_recv, small_part, min(1, seq // bt - 1)))
    return grad_x, reduced


MESH_ID = pl.DeviceIdType.MESH
SHARD_SHAPES = ((D_MODEL, IN_WIDTH // N_DEV), (Q_RANK, 768 // N_DEV), (KV_RANK, 1024 // N_DEV), (D_MODEL // N_DEV, D_MODEL))
ADAM_ROWS = (32, 128, 128, 16)


def _me():
    x, y, c = lax.axis_index("x"), lax.axis_index("y"), lax.axis_index("c")
    return x, y, c, 4 * x + 2 * y + c


def _peer(k):
    x, y, c, _ = _me()
    px = 1 - x if (k >> 2) & 1 else x
    py = 1 - y if (k >> 1) & 1 else y
    pc = 1 - c if k & 1 else c
    return (px, py, pc), 4 * px + 2 * py + pc


def _all_gather_weights(shards):
    n = len(shards)
    shard = IN_WIDTH // N_DEV
    k_lo = IN_SPLITS[0] + IN_SPLITS[1]
    k_hi = k_lo + MLA_ROPE

    def body(*refs):
        ins = refs[:n]
        win_ref, wuq_ref, wukv_ref = refs[n:2 * n]
        bufs = refs[2 * n:3 * n]
        send_sems, recv_sems = refs[3 * n:]
        x, y, c, me = _me()
        here, sibling = (x, y, c), (x, y, 1 - c)
        along_x, along_y, across = (1 - x, y), (x, 1 - y), (1 - x, 1 - y)
        for t in range(n):
            bufs[t][me] = ins[t][...].astype(BF16)

        def copy(t, k, chip, pc, to, half=None):
            blk = bufs[t].at[4 * chip[0] + 2 * chip[1] + pc]
            if half is not None:
                rows = SHARD_SHAPES[t][0] // 2
                blk = blk.at[pl.ds(half * rows, rows), :]
            return pltpu.make_async_remote_copy(
                src_ref=blk, dst_ref=blk, send_sem=send_sems.at[t, k], recv_sem=recv_sems.at[t, k],
                device_id=to, device_id_type=MESH_ID)

        sends = []
        for t in range(n):
            sends += [copy(t, 0, (x, y), c, sibling), copy(t, 1, (x, y), c, (*along_x, c)),
                      copy(t, 2, (x, y), c, (*along_y, c))]
        for cp in sends:
            cp.start()
        for t in range(n):
            copy(t, 1, along_x, c, here).wait_recv()
            sends += [copy(t, 3, along_x, c, (*along_y, c), half=0), copy(t, 5, along_x, c, sibling)]
            sends[-2].start()
            sends[-1].start()
        for t in range(n):
            copy(t, 2, along_y, c, here).wait_recv()
            sends += [copy(t, 4, along_y, c, (*along_x, c), half=1), copy(t, 6, along_y, c, sibling)]
            sends[-2].start()
            sends[-1].start()
        for t in range(n):
            copy(t, 3, across, c, here, half=0).wait_recv()
            copy(t, 4, across, c, here, half=1).wait_recv()
            sends.append(copy(t, 7, across, c, sibling))
            sends[-1].start()
        for t in range(n):
            copy(t, 0, (x, y), 1 - c, here).wait_recv()
            for k, chip in ((5, along_x), (6, along_y), (7, across)):
                copy(t, k, chip, 1 - c, here).wait_recv()
        for cp in sends:
            cp.wait_send()

        a_in, a_uq, a_ukv = bufs
        for d in range(N_DEV):
            lo, hi = shard * d, shard * (d + 1)
            if lo < k_lo:
                win_ref[:, lo:min(hi, k_lo)] = a_in[d, :, 0:min(hi, k_lo) - lo]
            if lo < k_hi and hi > k_lo:
                kr = a_in[d, :, k_lo - lo:k_hi - lo]
                for rep in range(4):
                    win_ref[:, C_KR + MLA_ROPE * rep:C_KR + MLA_ROPE * (rep + 1)] = kr
            if hi > k_hi:
                src = max(lo, k_hi)
                win_ref[:, src + C_GA - k_hi:hi + C_GA - k_hi] = a_in[d, :, src - lo:hi - lo]
        for h in range(HEADS):
            wuq_ref[:, MLA_NOPE * h:MLA_NOPE * (h + 1)] = a_uq[h, :, :MLA_NOPE]
            wuq_ref[:, HW + MLA_ROPE * h:HW + MLA_ROPE * (h + 1)] = a_uq[h, :, MLA_NOPE:]
            wukv_ref[:, MLA_NOPE * h:MLA_NOPE * (h + 1)] = a_ukv[h, :, :MLA_NOPE]
            wukv_ref[:, HW + MLA_V * h:HW + MLA_V * (h + 1)] = a_ukv[h, :, MLA_NOPE:]

    vmem = pl.BlockSpec(memory_space=pltpu.VMEM)
    return _pcall(
        body, name="gather_weights",
        in_specs=[vmem] * n, out_specs=[vmem] * n,
        out_shape=[jax.ShapeDtypeStruct((D_MODEL, C_END), BF16), jax.ShapeDtypeStruct((Q_RANK, QW), BF16),
                   jax.ShapeDtypeStruct((KV_RANK, 2 * HW), BF16)],
        scratch_shapes=[pltpu.VMEM((N_DEV,) + s, BF16) for s in SHARD_SHAPES[:n]]
        + [pltpu.SemaphoreType.DMA((n, 8)), pltpu.SemaphoreType.DMA((n, 8))],
        compiler_params=_cparams(),
    )(*shards)


def _gather_w_out_rider(w_out):
    def copies(full_ref, stage, send_sems, recv_sems):
        me = _me()[3]
        out = []
        for k in range(1, N_DEV):
            peer, pidx = _peer(k)
            send = pltpu.make_async_remote_copy(
                src_ref=stage, dst_ref=full_ref.at[me], send_sem=send_sems.at[k - 1], recv_sem=recv_sems.at[k - 1],
                device_id=peer, device_id_type=MESH_ID)
            recv = pltpu.make_async_remote_copy(
                src_ref=stage, dst_ref=full_ref.at[pidx], send_sem=send_sems.at[k - 1], recv_sem=recv_sems.at[k - 1],
                device_id=peer, device_id_type=MESH_ID)
            out.append((send, recv))
        return out

    def start(ins, outs, scr):
        stage, send_sems, recv_sems, own_sem = scr
        stage[...] = ins[0][...].astype(BF16)
        pltpu.make_async_copy(stage, outs[0].at[_me()[3]], own_sem).start()
        for send, _ in copies(outs[0], stage, send_sems, recv_sems):
            send.start()

    def finish(ins, outs, scr):
        stage, send_sems, recv_sems, own_sem = scr
        pltpu.make_async_copy(stage, outs[0].at[_me()[3]], own_sem).wait()
        pairs = copies(outs[0], stage, send_sems, recv_sems)
        for _, recv in pairs:
            recv.wait_recv()
        for send, _ in pairs:
            send.wait_send()

    shape = SHARD_SHAPES[3]
    return Rider(
        args=[w_out], in_specs=[pl.BlockSpec(shape, lambda t, *_: (0, 0))],
        out_shape=[jax.ShapeDtypeStruct((N_DEV,) + shape, BF16)], out_specs=[pl.BlockSpec(memory_space=pl.ANY)],
        scratch=[pltpu.VMEM(shape, BF16), pltpu.SemaphoreType.DMA((N_DEV - 1,)), pltpu.SemaphoreType.DMA((N_DEV - 1,)),
                 pltpu.SemaphoreType.DMA],
        start=start, finish=finish)


def _scatter_g_out_rider(blocks):
    def copies(src_ref, dst_ref, send_sems, recv_sems):
        out = []
        for k in range(1, N_DEV):
            peer, pidx = _peer(k)
            out.append(pltpu.make_async_remote_copy(
                src_ref=src_ref.at[pidx], dst_ref=dst_ref.at[k], send_sem=send_sems.at[k - 1],
                recv_sem=recv_sems.at[k - 1], device_id=peer, device_id_type=MESH_ID))
        return out

    def start(ins, outs, scr):
        send_sems, recv_sems, own_sem = scr
        pltpu.make_async_copy(ins[0].at[_me()[3]], outs[0].at[0], own_sem).start()
        for cp in copies(ins[0], outs[0], send_sems, recv_sems):
            cp.start()

    def finish(ins, outs, scr):
        send_sems, recv_sems, own_sem = scr
        pltpu.make_async_copy(ins[0].at[_me()[3]], outs[0].at[0], own_sem).wait()
        for cp in copies(ins[0], outs[0], send_sems, recv_sems):
            cp.wait()

    hbm = pl.BlockSpec(memory_space=pl.ANY)
    return Rider(
        args=[blocks], in_specs=[hbm], out_shape=[jax.ShapeDtypeStruct(blocks.shape, blocks.dtype)], out_specs=[hbm],
        scratch=[pltpu.SemaphoreType.DMA((N_DEV - 1,)), pltpu.SemaphoreType.DMA((N_DEV - 1,)), pltpu.SemaphoreType.DMA],
        start=start, finish=finish)


def _adamw(w, g, m, v):
    m = ADAM_B1 * m + (1.0 - ADAM_B1) * g
    v = ADAM_B2 * v + (1.0 - ADAM_B2) * jnp.square(g)
    m_hat = m / (1.0 - ADAM_B1 ** ADAM_STEP)
    v_hat = v / (1.0 - ADAM_B2 ** ADAM_STEP)
    delta = -ADAM_LR * (m_hat / (jnp.sqrt(v_hat) + ADAM_EPS) + ADAM_WD * w)
    return delta, m, v


def _reduce_grads_rider(grads3, arrived, small_part, mid_step):
    n = len(grads3)

    class Refs:
        def __init__(self, ins, outs, scr):
            self.g3, self.arr, self.sp = ins[0:n], ins[n], ins[n + 1]
            self.gsum, self.gsum_out, self.ssum = outs[0:n], outs[n], outs[n + 1]
            self.own, self.sib, self.part, self.ici = scr[0:n], scr[n:2 * n], scr[2 * n:3 * n], scr[3 * n:4 * n]
            self.rsmall = scr[4 * n]
            (self.loc_sems, self.d2d_send, self.d2d_recv, self.ici_send, self.ici_recv,
             self.sm_send, self.sm_recv) = scr[4 * n + 1:]
            self.x, self.y, self.c, self.me = _me()
            self.chips = [(1 - self.x, self.y), (self.x, 1 - self.y), (1 - self.x, 1 - self.y)]

        def small(self):
            return [pltpu.make_async_remote_copy(
                src_ref=self.rsmall.at[0], dst_ref=self.rsmall.at[k], send_sem=self.sm_send.at[k - 1],
                recv_sem=self.sm_recv.at[k - 1], device_id=_peer(k)[0], device_id_type=MESH_ID)
                for k in range(1, N_DEV)]

        def level1(self):
            local, to_sib = [], []
            for t in range(n):
                for q in range(4):
                    local.append(pltpu.make_async_copy(
                        self.g3[t].at[2 * q + self.c], self.own[t].at[q], self.loc_sems.at[t, q]))
                    to_sib.append(pltpu.make_async_remote_copy(
                        src_ref=self.g3[t].at[2 * q + 1 - self.c], dst_ref=self.sib[t].at[q],
                        send_sem=self.d2d_send.at[t, q], recv_sem=self.d2d_recv.at[t, q],
                        device_id=(self.x, self.y, 1 - self.c), device_id_type=MESH_ID))
            return local, to_sib

        def level2(self):
            return [pltpu.make_async_remote_copy(
                src_ref=self.part[t].at[2 * px + py], dst_ref=self.ici[t].at[j], send_sem=self.ici_send.at[t, j],
                recv_sem=self.ici_recv.at[t, j], device_id=(px, py, self.c), device_id_type=MESH_ID)
                for t in range(n) for j, (px, py) in enumerate(self.chips)]

    def chunks(t, fn):
        rows = ADAM_ROWS[t]

        def step(i, carry):
            fn(pl.ds(pl.multiple_of(i * rows, rows), rows))
            return carry

        lax.fori_loop(0, SHARD_SHAPES[t][0] // rows, step, 0)

    def start(*refs):
        r = Refs(*refs)
        r.rsmall[0] = r.sp[...]
        local, to_sib = r.level1()
        for cp in r.small() + local + to_sib:
            cp.start()

    def middle(*refs):
        r = Refs(*refs)
        local, to_sib = r.level1()
        for cp in local:
            cp.wait()
        for cp in to_sib:
            cp.wait_recv()
        my_chip = 2 * r.x + r.y
        for t in range(n):
            def pair_sums(rows, t=t):
                for q in range(4):
                    r.part[t][q, rows, :] = (
                        r.own[t][q, rows, :].astype(F32) + r.sib[t][q, rows, :].astype(F32)).astype(BF16)
                r.gsum[t][rows, :] = r.own[t][my_chip, rows, :].astype(F32) + r.sib[t][my_chip, rows, :].astype(F32)

            chunks(t, pair_sums)
        for cp in r.level2():
            cp.start()

        def add_arrived(rows):
            g = r.arr[0, rows, :].astype(F32)
            for k in range(1, N_DEV):
                g = g + r.arr[k, rows, :].astype(F32)
            r.gsum_out[rows, :] = g

        chunks(3, add_arrived)

    def finish(*refs):
        r = Refs(*refs)
        to_chips = r.level2()
        for cp in to_chips:
            cp.wait_recv()
        for t in range(n):
            def add_chips(rows, t=t):
                g = r.gsum[t][rows, :]
                for j in range(3):
                    g = g + r.ici[t][j, rows, :].astype(F32)
                r.gsum[t][rows, :] = g

            chunks(t, add_chips)
        small = r.small()
        for cp in small:
            cp.wait_recv()
        tot = r.rsmall[r.me]
        for d in range(1, N_DEV):
            tot = tot + r.rsmall[jnp.bitwise_xor(r.me, d)]
        r.ssum[...] = tot
        for cp in small + r.level1()[1] + to_chips:
            cp.wait_send()

    hbm = pl.BlockSpec(memory_space=pl.ANY)
    dma = pltpu.SemaphoreType.DMA

    def whole(shape):
        return pl.BlockSpec(shape, lambda i: (0,) * len(shape))

    out_shapes = list(SHARD_SHAPES) + [(8, D_MODEL)]
    return Rider(
        args=list(grads3) + [arrived, small_part],
        in_specs=[hbm] * n + [whole(arrived.shape), whole(small_part.shape)],
        out_shape=[jax.ShapeDtypeStruct(s, F32) for s in out_shapes], out_specs=[whole(s) for s in out_shapes],
        scratch=[pltpu.VMEM((slots,) + s, BF16) for slots in (4, 4, 4, 3) for s in SHARD_SHAPES[:n]]
        + [pltpu.VMEM((N_DEV, 8, D_MODEL), F32), dma((n, 4)), dma((n, 4)), dma((n, 4)), dma((n, 3)), dma((n, 3)),
           dma((N_DEV - 1,)), dma((N_DEV - 1,))],
        start=start, finish=finish, stages=((mid_step, middle),))


def _adamw_update(grads, small_grad, wmv, small_wmv):
    n_small = len(small_wmv)

    def body(*refs):
        g_refs, sg_ref = refs[0:4], refs[4]
        wmv_refs = [refs[5 + 3 * t:8 + 3 * t] for t in range(4)]
        swmv_refs = [refs[17 + 3 * t:20 + 3 * t] for t in range(n_small)]
        outs = refs[17 + 3 * n_small:]
        out_refs = [outs[4 * t:4 * t + 4] for t in range(4)]
        sout_refs = [outs[16 + 4 * t:20 + 4 * t] for t in range(n_small)]
        loss_ref = outs[16 + 4 * n_small]
        for t, (w_ref, m_ref, v_ref) in enumerate(swmv_refs):
            g = sg_ref[t:t + 1, :w_ref.shape[1]]
            delta, m, v = _adamw(w_ref[...], g, m_ref[...], v_ref[...])
            sout_refs[t][0][...], sout_refs[t][1][...], sout_refs[t][2][...], sout_refs[t][3][...] = g, delta, m, v
        loss_ref[...] = (0.5 / D_MODEL) * jnp.sum(sg_ref[n_small:n_small + 1, :], axis=1, keepdims=True)
        for t in range(4):
            rows = ADAM_ROWS[t]
            w_ref, m_ref, v_ref = wmv_refs[t]
            g_out, d_out, m_out, v_out = out_refs[t]

            def step(i, carry, g_ref=g_refs[t], rows=rows, w_ref=w_ref, m_ref=m_ref, v_ref=v_ref,
                     g_out=g_out, d_out=d_out, m_out=m_out, v_out=v_out):
                r = pl.ds(pl.multiple_of(i * rows, rows), rows)
                g = g_ref[r, :]
                delta, m, v = _adamw(w_ref[r, :], g, m_ref[r, :], v_ref[r, :])
                g_out[r, :], d_out[r, :], m_out[r, :], v_out[r, :] = g, delta, m, v
                return carry

            lax.fori_loop(0, SHARD_SHAPES[t][0] // rows, step, 0)

    vmem = pl.BlockSpec(memory_space=pltpu.VMEM)
    flat_wmv = [a for trio in wmv for a in trio]
    flat_small = [a for trio in small_wmv for a in trio]
    out_shape = ([jax.ShapeDtypeStruct(s, F32) for s in SHARD_SHAPES for _ in range(4)]
                 + [jax.ShapeDtypeStruct(trio[0].shape, F32) for trio in small_wmv for _ in range(4)]
                 + [jax.ShapeDtypeStruct((1, 1), F32)])
    return _pcall(
        body, name="adamw",
        in_specs=[vmem] * (5 + len(flat_wmv) + len(flat_small)), out_specs=[vmem] * len(out_shape),
        out_shape=out_shape,
        compiler_params=_cparams(),
    )(*grads, small_grad, *flat_wmv, *flat_small)


def _small_rows(ln_g, ln_b, q_norm_g, kv_norm_g, extra=None):
    pad = lambda a: jnp.pad(a, (0, D_MODEL - a.shape[0]))
    rows = [ln_g, ln_b, pad(q_norm_g), pad(kv_norm_g)] + ([] if extra is None else [extra])
    return jnp.pad(jnp.stack(rows), ((0, 8 - len(rows)), (0, 0)))


def kernel(x, w_in, q_norm_g, kv_norm_g, w_uq, w_ukv, w_out, ln_g, ln_b, loss_target, m_w_in, m_q_norm_g, m_kv_norm_g, m_w_uq, m_w_ukv, m_w_out, m_ln_g, m_ln_b, v_w_in, v_q_norm_g, v_kv_norm_g, v_w_uq, v_w_ukv, v_w_out, v_ln_g, v_ln_b):
    w_in_r, w_uq_r, w_ukv_r = _all_gather_weights([w_in, w_uq, w_ukv])
    grad_x, sums = _local_step(
        x[0], loss_target[0], w_in_r, w_uq_r, w_ukv_r, _gather_w_out_rider(w_out), _scatter_g_out_rider,
        _reduce_grads_rider, q_norm_g, kv_norm_g, ln_g, ln_b)
    row = lambda a: a.reshape(1, -1)
    small_wmv = [(row(ln_g), row(m_ln_g), row(v_ln_g)), (row(ln_b), row(m_ln_b), row(v_ln_b)),
                 (row(q_norm_g), row(m_q_norm_g), row(v_q_norm_g)), (row(kv_norm_g), row(m_kv_norm_g), row(v_kv_norm_g))]
    wmv = [(w_in, m_w_in, v_w_in), (w_uq, m_w_uq, v_w_uq), (w_ukv, m_w_ukv, v_w_ukv), (w_out, m_w_out, v_w_out)]
    res = _adamw_update(sums[:4], sums[4], wmv, small_wmv)
    big = [res[4 * t:4 * t + 4] for t in range(4)]
    small = [[a.reshape(-1) for a in res[16 + 4 * t:20 + 4 * t]] for t in range(4)]
    loss = res[32].reshape(())

    def group(kind):
        return (big[0][kind], small[2][kind], small[3][kind], big[1][kind], big[2][kind], big[3][kind],
                small[0][kind], small[1][kind])

    return (loss, grad_x[None], *group(0), *group(1), *group(2), *group(3))
```

```python
import functools
from typing import Callable, NamedTuple

import numpy as np
import jax
import jax.numpy as jnp
from jax import lax
from jax.experimental import pallas as pl
from jax.experimental.pallas import tpu as pltpu

F32 = jnp.float32
BF16 = jnp.bfloat16

D_MODEL = 1024
ROPE_THETA = 500000.0
NEG = -1e30
RMS_EPS = 1e-6
LN_EPS = 1e-5
HEADS = 8
MLA_NOPE = 64
MLA_ROPE = 32
MLA_V = 64
Q_RANK = 384
KV_RANK = 256
DIL_HEAD = 64
DIL_ROT = 16
DIL_CONFIGS = ((128, 1), (512, 4), (2048, 16))
DIL_NEAR = 512
HW = HEADS * 64
QW = HW + HEADS * MLA_ROPE
IN_SPLITS = (Q_RANK, KV_RANK, MLA_ROPE, HW, HW, HW, HW, HW)
IN_WIDTH = sum(IN_SPLITS)
ALPHA = 2.0 ** 0.25
MLA_SCALE = (MLA_NOPE + MLA_ROPE) ** -0.5
DIL_SCALE = DIL_HEAD ** -0.5
LOG2E = 1.4426950408889634
LN2 = 0.6931471805599453

ADAM_LR = 0.001
ADAM_B1 = 0.9
ADAM_B2 = 0.999
ADAM_EPS = 1e-08
ADAM_WD = 0.01
ADAM_STEP = 10

N_DEV = 8
LANES = 128
VMEM_LIMIT = 56 * 1024 * 1024
BLOCK_TOKENS = 512
BLOCK_MLA = 512
BLOCK_DIL = 512

C_CQ, C_CKV, C_KR, C_GA, C_QB, C_KB, C_VB, C_GB, C_END = 0, 384, 640, 768, 1280, 1792, 2304, 2816, 3328

NT = (((1,), (1,)), ((), ()))
TN = (((0,), (0,)), ((), ()))


def _pcall(body, **kw):
    return pl.pallas_call(body, **kw)


def _cparams(**kw):
    return pltpu.CompilerParams(vmem_limit_bytes=VMEM_LIMIT, **kw)


def _rope_tables(seq):
    def tabs(dim, period):
        half = dim // 2
        inv = np.float32(ROPE_THETA) ** (-np.arange(0, dim, 2, dtype=np.float32) / np.float32(dim))
        ang = np.arange(seq, dtype=np.float32)[:, None] * inv.astype(np.float32)[None, :]
        cos, sin = np.cos(ang).astype(np.float32), np.sin(ang).astype(np.float32)
        j = np.arange(LANES) % period
        f = j % half
        c = np.where(j < dim, cos[:, f], np.float32(1.0))
        s1 = np.where(j < half, -sin[:, f], np.float32(0.0))
        s2 = np.where((j >= half) & (j < dim), sin[:, f], np.float32(0.0))
        return [c, s1, s2]
    return np.stack(tabs(MLA_ROPE, MLA_ROPE) + tabs(DIL_ROT, DIL_HEAD)).astype(np.float32)


def _rope(t, c, s1, s2, half):
    return t * c + pltpu.roll(t, LANES - half, 1) * s1 + pltpu.roll(t, half, 1) * s2


def _rope_t(d, c, s1, s2, half):
    return d * c + pltpu.roll(d * s1, half, 1) + pltpu.roll(d * s2, LANES - half, 1)


def _rope_wide(fn, t, c, s1, s2, half):
    return jnp.concatenate(
        [fn(t[:, i:i + LANES], c, s1, s2, half) for i in range(0, t.shape[1], LANES)], axis=1)


def _mla_bias_t(blk):
    a = np.arange(blk)
    causal = np.where(a[:, None] <= a[None, :], 0.0, NEG)
    return np.stack([np.zeros((blk, blk)), causal]).astype(np.float32)


def _dil_bias_t(blk, reach):
    a = np.arange(blk)
    out = []
    for off in range(-(-reach // blk) + 1):
        delta = blk * off + a[None, :] - a[:, None]
        mult = np.zeros((blk, blk))
        for window, dil in DIL_CONFIGS:
            mult += (delta >= 0) & (delta % dil == 0) & (delta <= min(window, reach))
        out.append(np.where(mult > 0, np.log2(np.maximum(mult, 1.0)), NEG))
    return np.stack(out).astype(np.float32)


def _dil_far_bias_t(length):
    window, dil = DIL_CONFIGS[-1]
    a = np.arange(length)
    steps_back = a[None, :] - a[:, None]
    seen = (steps_back * dil > DIL_NEAR) & (steps_back * dil <= window)
    return np.where(seen, 0.0, NEG).astype(np.float32)[None]


def _lanes_to_classes(a, dil):
    h, s = a.shape
    return a.reshape(h, s // dil, dil).transpose(0, 2, 1).reshape(h, s)


def _lanes_from_classes(a, dil):
    h, s = a.shape
    return a.reshape(h, dil, s // dil).transpose(0, 2, 1).reshape(h, s)


def _steps(nq, span, by_key, diag_only_bias):
    rows = []
    if by_key:
        for ki in range(nq):
            hi = min(nq - 1, ki + span)
            for qi in range(ki, hi + 1):
                rows.append((qi, ki, int(qi == ki), int(qi == hi)))
    else:
        for qi in range(nq):
            lo = max(0, qi - span)
            for ki in range(lo, qi + 1):
                rows.append((qi, ki, int(ki == lo), int(ki == qi)))
    arr = np.array(rows, dtype=np.int32)
    off = arr[:, 0] - arr[:, 1]
    bias_idx = (off == 0).astype(np.int32) if diag_only_bias else off.astype(np.int32)
    return [jnp.asarray(v) for v in (arr[:, 0], arr[:, 1], bias_idx, arr[:, 2], arr[:, 3])]


def _by_class(val, out_ref, lanes_sc):
    n_cls, per = out_ref.shape[0], out_ref.shape[1]
    for c in range(val.shape[1] // LANES):
        lanes_sc[c] = val[:, LANES * c:LANES * (c + 1)]
        for r in range(n_cls):
            rows = lanes_sc.at[c][pl.ds(r, per, stride=n_cls), :]
            out_ref[r, :, LANES * c:LANES * (c + 1)] = rows.astype(out_ref.dtype)


def _in_sequence(ref, lanes_sc):
    n_cls, per, width = ref.shape
    for c in range(width // LANES):
        for r in range(n_cls):
            lanes_sc.at[c][pl.ds(r, per, stride=n_cls), :] = ref[r, :, LANES * c:LANES * (c + 1)].astype(F32)
    return jnp.concatenate([lanes_sc[c] for c in range(width // LANES)], axis=1)


def _fwd_proj(x, w_in_r, w_uq_r, w_ukv_r, qg, kvg, tabs, bt, n_cls):
    seq = x.shape[0]

    def body(x_ref, win_ref, wuq_ref, wukv_ref, qg_ref, kvg_ref, tab_ref,
             cq_ref, ckv_ref, qn_ref, kvn_ref, qcat_ref, kn_ref, kpe_ref, v_ref,
             ga_ref, gb_ref, qb_ref, kb_ref, vb_ref, knt_ref, kpet_ref, vt_ref, kbt_ref, vbt_ref,
             qbc_ref, kbc_ref, vbc_ref, lanes_sc):
        xb = x_ref[...].astype(BF16)

        def proj(lo, hi):
            return jnp.dot(xb, win_ref[:, lo:hi], preferred_element_type=F32)

        m_tabs = (tab_ref[0], tab_ref[1], tab_ref[2])
        d_tabs = (tab_ref[3], tab_ref[4], tab_ref[5])

        def use_cq(cq):
            cq_ref[...] = cq
            qn = (cq * lax.rsqrt(jnp.mean(cq * cq, axis=1, keepdims=True) + RMS_EPS) * qg_ref[...]).astype(BF16)
            qn_ref[...] = qn
            q = jnp.dot(qn, wuq_ref[...], preferred_element_type=F32)
            qcat_ref[:, :HW] = (q[:, :HW] * (MLA_SCALE * LOG2E)).astype(BF16)
            qcat_ref[:, HW:] = (
                _rope_wide(_rope, q[:, HW:], *m_tabs, MLA_ROPE // 2) * (MLA_SCALE * LOG2E)).astype(BF16)

        def use_ckv(ckv):
            ckv_ref[...] = ckv
            kvn = (ckv * lax.rsqrt(jnp.mean(ckv * ckv, axis=1, keepdims=True) + RMS_EPS) * kvg_ref[...]).astype(BF16)
            kvn_ref[...] = kvn
            kv = jnp.dot(kvn, wukv_ref[...], preferred_element_type=F32)
            kn_ref[...] = kv[:, :HW].astype(BF16)
            v_ref[...] = kv[:, HW:].astype(BF16)
            knt_ref[...] = kv[:, :HW].T.astype(BF16)
            vt_ref[...] = kv[:, HW:].T.astype(BF16)

        def use_kr(kr):
            kpe = _rope(kr, *m_tabs, MLA_ROPE // 2)
            kpe_ref[...] = kpe.astype(BF16)
            kpet_ref[...] = kpe.T[:MLA_ROPE, :].astype(BF16)

        def use_ga(ga):
            ga_ref[...] = ga

        def use_qb(qb):
            qb = _rope_wide(_rope, qb, *d_tabs, DIL_ROT // 2) * (DIL_SCALE * LOG2E)
            qb_ref[...] = qb.astype(BF16)
            _by_class(qb, qbc_ref, lanes_sc)

        def use_kb(kb):
            kb = _rope_wide(_rope, kb, *d_tabs, DIL_ROT // 2)
            kb_ref[...] = kb.astype(BF16)
            kbt_ref[...] = kb.T.astype(BF16)
            _by_class(kb, kbc_ref, lanes_sc)

        def use_vb(vb):
            vb_ref[...] = vb.astype(BF16)
            vbt_ref[...] = vb.T.astype(BF16)
            _by_class(vb, vbc_ref, lanes_sc)

        def use_gb(gb):
            gb_ref[...] = gb

        pieces = [(C_CQ, C_CKV, use_cq), (C_CKV, C_KR, use_ckv), (C_KR, C_GA, use_kr), (C_GA, C_QB, use_ga),
                  (C_QB, C_KB, use_qb), (C_KB, C_VB, use_kb), (C_VB, C_GB, use_vb), (C_GB, C_END, use_gb)]
        ahead = proj(*pieces[0][:2])
        for n, (_, _, use) in enumerate(pieces):
            cur = ahead
            if n + 1 < len(pieces):
                ahead = proj(*pieces[n + 1][:2])
            use(cur)

    def tok(width):
        return pl.BlockSpec((bt, width), lambda i: (i, 0))

    def tok_t(height):
        return pl.BlockSpec((height, bt), lambda i: (0, i))

    def full(a):
        return pl.BlockSpec(a.shape, lambda i: (0,) * a.ndim)

    outs = [(Q_RANK, F32), (KV_RANK, F32), (Q_RANK, BF16), (KV_RANK, BF16), (QW, BF16), (HW, BF16),
            (LANES, BF16), (HW, BF16), (HW, F32), (HW, F32), (HW, BF16), (HW, BF16), (HW, BF16)]
    outs_t = [HW, MLA_ROPE, HW, HW, HW]
    by_class = pl.BlockSpec((n_cls, bt // n_cls, HW), lambda i: (0, i, 0))
    return _pcall(
        body, name="fwd_proj", grid=(seq // bt,),
        in_specs=[tok(D_MODEL), full(w_in_r), full(w_uq_r), full(w_ukv_r), full(qg), full(kvg),
                  pl.BlockSpec((6, bt, LANES), lambda i: (0, i, 0))],
        out_specs=[tok(w) for w, _ in outs] + [tok_t(h) for h in outs_t] + [by_class] * 3,
        out_shape=[jax.ShapeDtypeStruct((seq, w), dt) for w, dt in outs]
        + [jax.ShapeDtypeStruct((h, seq), BF16) for h in outs_t]
        + [jax.ShapeDtypeStruct((n_cls, seq // n_cls, HW), BF16)] * 3,
        scratch_shapes=[pltpu.VMEM((HW // LANES, bt, LANES), F32)],
        compiler_params=_cparams(dimension_semantics=("arbitrary",)),
    )(x, w_in_r, w_uq_r, w_ukv_r, qg, kvg, tabs)


def _head_masks(lane, h):
    e, g = h % 2, h % 4
    me = (lane >= 64 * e) & (lane < 64 * e + 64)
    mr = (lane >= 32 * g) & (lane < 32 * g + 32)
    return me, mr


def _masked(mask, a):
    return jnp.where(mask, a, jnp.zeros_like(a))


def _pair_operands(q_ref, k_ref, kpe_ref, lane, j, ks=slice(None), qs=slice(None)):
    cols = slice(LANES * j, LANES * (j + 1))
    qc = q_ref[qs, cols]
    kj = k_ref[ks, cols]
    kes = []
    for h in (2 * j, 2 * j + 1):
        me, mr = _head_masks(lane, h)
        ke = _masked(me, kj)
        if kpe_ref is not None:
            ke = jnp.concatenate([ke, _masked(mr, kpe_ref[ks, :])], axis=1)
        kes.append(ke)
    if kpe_ref is not None:
        qc = jnp.concatenate([qc, q_ref[qs, HW + LANES * (j // 2):HW + LANES * (j // 2 + 1)]], axis=1)
    return qc, kes


def _tile_variants(bias_t):
    out = {}
    for i, tile in enumerate(np.asarray(bias_t)):
        h = tile.shape[0] // 2
        skip = 1 if (tile[h:, :h] == NEG).all() else 2 if (tile[:h, h:] == NEG).all() else 0
        out[i] = (bool((tile != 0).any()), skip)
    return out


def _tile_parts(blk, skip):
    lo, hi, full = slice(0, blk // 2), slice(blk // 2, blk), slice(0, blk)
    return {0: [(full, full)], 1: [(lo, full), (hi, hi)], 2: [(hi, full), (lo, lo)]}[skip]


class Rider(NamedTuple):
    args: list
    in_specs: list
    out_shape: list
    out_specs: list
    scratch: list
    start: Callable
    finish: Callable
    stages: tuple = ()


def _ride_along(body, ride, n_prefetch, n_in, n_out, n_scratch, n_steps):
    if ride is None:
        return body

    def wrapped(*refs):
        pre, rest = refs[:n_prefetch], refs[n_prefetch:]
        a = n_in
        b = a + len(ride.args)
        c = b + n_out
        d = c + len(ride.out_shape)
        e = d + n_scratch
        mine = (rest[a:b], rest[c:d], rest[e:])
        t = pl.program_id(0)
        pl.when(t == 0)(lambda: ride.start(*mine))
        for at, stage in ride.stages:
            pl.when(t == at)(functools.partial(stage, *mine))
        body(*pre, *rest[:a], *rest[b:c], *rest[d:e])
        pl.when(t == n_steps - 1)(lambda: ride.finish(*mine))

    return wrapped


def _attn_fwd(name, q, k, kpe, vt, bias_t, steps, blk, ride=None, v_token_major=False):
    seq = q.shape[0]
    mla = kpe is not None
    n_steps = int(steps[0].shape[0])
    variants = _tile_variants(bias_t)

    def body(qi_r, ki_r, bi_r, fi_r, la_r, *refs):
        if mla:
            q_ref, k_ref, kpe_ref, vt_ref, b_ref, o_ref, lse_ref, m_sc, l_sc, acc_sc, st_sc = refs
        else:
            q_ref, k_ref, vt_ref, b_ref, o_ref, lse_ref, m_sc, l_sc, acc_sc, st_sc = refs
        t = pl.program_id(0)

        @pl.when(fi_r[t] == 1)
        def _():
            m_sc[...] = jnp.full(m_sc.shape, NEG, F32)
            l_sc[...] = jnp.zeros(l_sc.shape, F32)
            acc_sc[...] = jnp.zeros(acc_sc.shape, F32)

        lane = lax.broadcasted_iota(jnp.int32, (1, LANES), 1)
        if v_token_major:
            vt_all = vt_ref[...].astype(F32).T.astype(BF16)
            vt_rows = lambda rows, ks: vt_all[rows, ks]
        else:
            vt_rows = lambda rows, ks: vt_ref[rows, ks]

        def tile_pass(ks, qs, with_bias):
            nk, nq = ks.stop - ks.start, qs.stop - qs.start
            ones = jnp.ones((16, nk), BF16)

            def pair_scores(j):
                qc, kes = _pair_operands(q_ref, k_ref, kpe_ref if mla else None, lane, j, ks, qs)
                st = lax.dot_general(jnp.concatenate(kes, axis=0), qc, NT, preferred_element_type=F32)
                maxes = []
                for e in range(2):
                    se = st[e * nk:(e + 1) * nk]
                    if with_bias:
                        se = se + b_ref[0, ks, qs]
                    st_sc[j % 2, e * nk:(e + 1) * nk, 0:nq] = se
                    maxes.append(jnp.max(se, axis=0, keepdims=True))
                return maxes

            def softmax_pv(h, col_max):
                st = st_sc[(h // 2) % 2, (h % 2) * nk:(h % 2 + 1) * nk, 0:nq]
                hrow = slice(h, h + 1)
                m_prev = m_sc[hrow, qs]
                m_new = jnp.maximum(m_prev, col_max)
                alpha = jnp.exp2(m_prev - m_new)
                pt = jnp.exp2(st - m_new).astype(BF16)
                m_sc[hrow, qs] = m_new
                rows = slice(64 * h, 64 * h + 64)
                res = jnp.dot(jnp.concatenate([vt_rows(rows, ks), ones], axis=0), pt, preferred_element_type=F32)
                acc_sc[rows, qs] = alpha * acc_sc[rows, qs] + res[:64]
                l_sc[hrow, qs] = alpha * l_sc[hrow, qs] + res[64:65]

            maxes = pair_scores(0)
            for j in range(HEADS // 2):
                cur = maxes
                if j + 1 < HEADS // 2:
                    maxes = pair_scores(j + 1)
                softmax_pv(2 * j, cur[0])
                softmax_pv(2 * j + 1, cur[1])

        def step(with_bias, skip):
            for ks, qs in _tile_parts(blk, skip):
                tile_pass(ks, qs, with_bias)

        for idx, (with_bias, skip) in variants.items():
            if len(variants) == 1:
                step(with_bias, skip)
            else:
                pl.when(bi_r[t] == idx)(functools.partial(step, with_bias, skip))

        @pl.when(la_r[t] == 1)
        def _():
            for h in range(HEADS):
                rows = slice(64 * h, 64 * h + 64)
                acc_sc[rows, :] = acc_sc[rows, :] / l_sc[h:h + 1, :]
            o_ref[...] = acc_sc[...].T
            lse_ref[...] = m_sc[...] + jnp.log2(l_sc[...])

    qmap = lambda t, qi, ki, bi, fi, la: (qi[t], 0)
    kmap = lambda t, qi, ki, bi, fi, la: (ki[t], 0)
    in_specs = [pl.BlockSpec((blk, q.shape[1]), qmap), pl.BlockSpec((blk, HW), kmap)]
    args = [q, k]
    if mla:
        in_specs.append(pl.BlockSpec((blk, LANES), kmap))
        args.append(kpe)
    in_specs += [pl.BlockSpec((blk, HW), kmap) if v_token_major else
                 pl.BlockSpec((HW, blk), lambda t, qi, ki, bi, fi, la: (0, ki[t])),
                 pl.BlockSpec((1, blk, blk), lambda t, qi, ki, bi, fi, la: (bi[t], 0, 0))]
    args += [vt, jnp.asarray(bias_t)]
    out_specs = [pl.BlockSpec((blk, HW), qmap), pl.BlockSpec((HEADS, blk), lambda t, qi, ki, bi, fi, la: (0, qi[t]))]
    out_shape = [jax.ShapeDtypeStruct((seq, HW), F32), jax.ShapeDtypeStruct((HEADS, seq), F32)]
    scratch = [pltpu.VMEM((HEADS, blk), F32), pltpu.VMEM((HEADS, blk), F32),
               pltpu.VMEM((HW, blk), F32), pltpu.VMEM((2, 2 * blk, blk), F32)]
    body = _ride_along(body, ride, 5, len(args), len(out_shape), len(scratch), n_steps)
    if ride is not None:
        args, in_specs = args + ride.args, in_specs + ride.in_specs
        out_specs, out_shape, scratch = out_specs + ride.out_specs, out_shape + ride.out_shape, scratch + ride.scratch
    return _pcall(
        body, name=name,
        grid_spec=pltpu.PrefetchScalarGridSpec(
            num_scalar_prefetch=5, grid=(n_steps,), in_specs=in_specs, out_specs=out_specs, scratch_shapes=scratch),
        out_shape=out_shape,
        compiler_params=_cparams(dimension_semantics=("arbitrary",)),
    )(*steps, *args)


def _attn_bwd(name, q, k, kpe, v, kt, kpet, bias_t, do, lse, dstat, steps, blk, ride=None, single_visit=False):
    assert not (single_visit and kpe is not None) and (kt is not None or single_visit)
    seq = q.shape[0]
    mla = kpe is not None
    qw = q.shape[1]
    n_steps = int(steps[0].shape[0])
    dk_dtype = BF16 if mla else F32
    variants = _tile_variants(bias_t)

    def body(qi_r, ki_r, bi_r, fi_r, la_r, *refs):
        if mla:
            (q_ref, k_ref, kpe_ref, v_ref, kt_ref, kpet_ref, b_ref, do_ref, lse_ref, d_ref,
             dq_ref, dk_ref, dkpe_ref, dv_ref, dk_sc, dkpe_sc, dv_sc, st_sc, dpt_sc) = refs
        else:
            q_ref, k_ref, v_ref, *rest = refs
            kt_ref = rest.pop(0) if kt is not None else None
            b_ref, do_ref, lse_ref, d_ref, dq_out_ref, dk_ref, dv_ref, dk_sc, dv_sc, st_sc, dpt_sc, *rest = rest
            dq_ref = rest[0] if single_visit else dq_out_ref
        t = pl.program_id(0)

        @pl.when(jnp.logical_or(t == 0, single_visit))
        def _():
            dq_ref[...] = jnp.zeros(dq_ref.shape, F32)

        @pl.when(fi_r[t] == 1)
        def _():
            dk_sc[...] = jnp.zeros(dk_sc.shape, F32)
            dv_sc[...] = jnp.zeros(dv_sc.shape, F32)
            if mla:
                dkpe_sc[...] = jnp.zeros(dkpe_sc.shape, F32)

        qi = 0 if single_visit else qi_r[t]
        lane = lax.broadcasted_iota(jnp.int32, (1, LANES), 1)
        if kt is None:
            kt_all = k_ref[...].astype(F32).T.astype(BF16)
            kt_rows = lambda rows, ks: kt_all[rows, ks]
        else:
            kt_rows = lambda rows, ks: kt_ref[rows, ks]

        def tile_pass(ks, qs, with_bias):
            nk, nq = ks.stop - ks.start, qs.stop - qs.start

            def pair_matmuls(j):
                cols = slice(LANES * j, LANES * (j + 1))
                qc, kes = _pair_operands(q_ref, k_ref, kpe_ref if mla else None, lane, j, ks, qs)
                st_sc[j % 2, 0:2 * nk, 0:nq] = lax.dot_general(
                    jnp.concatenate(kes, axis=0), qc, NT, preferred_element_type=F32)
                vj = v_ref[ks, cols]
                ves = [_masked(_head_masks(lane, h)[0], vj) for h in (2 * j, 2 * j + 1)]
                dpt_sc[j % 2, 0:2 * nk, 0:nq] = lax.dot_general(
                    jnp.concatenate(ves, axis=0), do_ref[qs, cols], NT, preferred_element_type=F32)

            def pair_grads(j):
                cols = slice(LANES * j, LANES * (j + 1))
                qj, doj = q_ref[qs, cols], do_ref[qs, cols]
                if mla:
                    qr = q_ref[qs, HW + LANES * (j // 2):HW + LANES * (j // 2 + 1)]
                pts, dsts, qms, doms = [], [], [], []
                for e in range(2):
                    h = 2 * j + e
                    me, mr = _head_masks(lane, h)
                    st = st_sc[j % 2, e * nk:(e + 1) * nk, 0:nq]
                    if with_bias:
                        st = st + b_ref[0, ks, qs]
                    pt = jnp.exp2(st - lse_ref[h:h + 1, qs])
                    dst = (pt * (dpt_sc[j % 2, e * nk:(e + 1) * nk, 0:nq] - d_ref[h:h + 1, qs])).astype(BF16)
                    pts.append(pt.astype(BF16))
                    dsts.append(dst)
                    doms.append(_masked(me, doj))
                    qm = _masked(me, qj)
                    if mla:
                        qm = jnp.concatenate([qm, _masked(mr, qr)], axis=1)
                    qms.append(qm)
                    ktl = kt_rows(slice(64 * h, 64 * h + 64), ks)
                    if mla:
                        ktl = jnp.concatenate([ktl, kpet_ref[:, ks]], axis=0)
                    dqc = jnp.dot(ktl, dst, preferred_element_type=F32)
                    dq_ref[qi, 64 * h:64 * h + 64, qs] += dqc[:64]
                    if mla:
                        dq_ref[qi, HW + MLA_ROPE * h:HW + MLA_ROPE * (h + 1), qs] += dqc[64:]
                dv_sc[ks, cols] += jnp.dot(
                    jnp.concatenate(pts, axis=1), jnp.concatenate(doms, axis=0), preferred_element_type=F32)
                dkc = jnp.dot(jnp.concatenate(dsts, axis=1), jnp.concatenate(qms, axis=0), preferred_element_type=F32)
                dk_sc[ks, cols] += dkc[:, :LANES]
                if mla:
                    dkpe_sc[ks, :] += dkc[:, LANES:]

            pair_matmuls(0)
            for j in range(HEADS // 2):
                if j + 1 < HEADS // 2:
                    pair_matmuls(j + 1)
                pair_grads(j)

        def step(with_bias, skip):
            for ks, qs in _tile_parts(blk, skip):
                tile_pass(ks, qs, with_bias)

        for idx, (with_bias, skip) in variants.items():
            if len(variants) == 1:
                step(with_bias, skip)
            else:
                pl.when(bi_r[t] == idx)(functools.partial(step, with_bias, skip))
        if single_visit:
            dq_out_ref[...] = dq_ref[0].T

        @pl.when(la_r[t] == 1)
        def _():
            dk_ref[...] = (dk_sc[...] * LN2).astype(dk_ref.dtype)
            dv_ref[...] = dv_sc[...].astype(dv_ref.dtype)
            if mla:
                dkpe_ref[...] = dkpe_sc[...] * LN2

    qmap = lambda t, qi, ki, bi, fi, la: (qi[t], 0)
    kmap = lambda t, qi, ki, bi, fi, la: (ki[t], 0)
    qmap_t = lambda t, qi, ki, bi, fi, la: (0, qi[t])
    kmap_t = lambda t, qi, ki, bi, fi, la: (0, ki[t])
    in_specs = [pl.BlockSpec((blk, qw), qmap), pl.BlockSpec((blk, HW), kmap)]
    args = [q, k]
    if mla:
        in_specs.append(pl.BlockSpec((blk, LANES), kmap))
        args.append(kpe)
    in_specs.append(pl.BlockSpec((blk, HW), kmap))
    args.append(v)
    if kt is not None:
        in_specs.append(pl.BlockSpec((HW, blk), kmap_t))
        args.append(kt)
    if mla:
        in_specs.append(pl.BlockSpec((MLA_ROPE, blk), kmap_t))
        args.append(kpet)
    in_specs += [pl.BlockSpec((1, blk, blk), lambda t, qi, ki, bi, fi, la: (bi[t], 0, 0)),
                 pl.BlockSpec((blk, HW), qmap), pl.BlockSpec((HEADS, blk), qmap_t), pl.BlockSpec((HEADS, blk), qmap_t)]
    args += [jnp.asarray(bias_t), do, lse, dstat]
    dq_shape = (seq // blk, qw, blk)
    if single_visit:
        out_specs, out_shape = [pl.BlockSpec((blk, qw), qmap)], [jax.ShapeDtypeStruct((seq, qw), F32)]
    else:
        out_specs = [pl.BlockSpec(dq_shape, lambda t, qi, ki, bi, fi, la: (0, 0, 0))]
        out_shape = [jax.ShapeDtypeStruct(dq_shape, F32)]
    out_specs.append(pl.BlockSpec((blk, HW), kmap))
    out_shape.append(jax.ShapeDtypeStruct((seq, HW), dk_dtype))
    scratch = [pltpu.VMEM((blk, HW), F32)]
    if mla:
        out_specs.append(pl.BlockSpec((blk, LANES), kmap))
        out_shape.append(jax.ShapeDtypeStruct((seq, LANES), F32))
        scratch.append(pltpu.VMEM((blk, LANES), F32))
    out_specs.append(pl.BlockSpec((blk, HW), kmap))
    out_shape.append(jax.ShapeDtypeStruct((seq, HW), BF16))
    scratch.append(pltpu.VMEM((blk, HW), F32))
    scratch += [pltpu.VMEM((2, 2 * blk, blk), F32), pltpu.VMEM((2, 2 * blk, blk), F32)]
    if single_visit:
        scratch.append(pltpu.VMEM((1, qw, blk), F32))
    body = _ride_along(body, ride, 5, len(args), len(out_shape), len(scratch), n_steps)
    if ride is not None:
        args, in_specs = args + ride.args, in_specs + ride.in_specs
        out_specs, out_shape, scratch = out_specs + ride.out_specs, out_shape + ride.out_shape, scratch + ride.scratch
    return _pcall(
        body, name=name,
        grid_spec=pltpu.PrefetchScalarGridSpec(
            num_scalar_prefetch=5, grid=(n_steps,), in_specs=in_specs, out_specs=out_specs,
            scratch_shapes=scratch),
        out_shape=out_shape,
        compiler_params=_cparams(dimension_semantics=("arbitrary",)),
    )(*steps, *args)


def _out_ln(oa, ob_near, ob_far, lse_near, lse_far, ga, gb, x, tgt, w_out, ln_g, ln_b, bt):
    seq = x.shape[0]

    def body(oa_ref, obn_ref, obf_ref, lsen_ref, lsef_ref, ga_ref, gb_ref, x_ref, tgt_ref, w_ref, g_ref, b_ref,
             dz_ref, doa_ref, dob_ref, dga_ref, dgb_ref, da_ref, db_ref, lse_ref, gwb_ref, small_ref, dobc_ref,
             gw_ref, lanes_sc):
        i = pl.program_id(0)

        @pl.when(i == 0)
        def _():
            gw_ref[...] = jnp.zeros(gw_ref.shape, F32)
            small_ref[...] = jnp.zeros(small_ref.shape, F32)

        def gate(g):
            sig = 0.5 * jnp.tanh(0.5 * g) + 0.5
            return g * sig, sig * (1.0 + g * (1.0 - sig))

        lse_n, lse_f = lsen_ref[...], lsef_ref[...]
        top = jnp.maximum(lse_n, lse_f)
        e_n, e_f = jnp.exp2(lse_n - top), jnp.exp2(lse_f - top)
        lse_ref[...] = top + jnp.log2(e_n + e_f)
        inv = 1.0 / (e_n + e_f)
        head_row = lax.broadcasted_iota(jnp.int32, (2 * HEADS, HW), 0) % HEADS
        spread = (head_row == lax.broadcasted_iota(jnp.int32, (2 * HEADS, HW), 1) // 64).astype(BF16)

        def per_lane(w):
            hi = w.astype(BF16)
            lo = (w - hi.astype(F32)).astype(BF16)
            return lax.dot_general(jnp.concatenate([hi, lo], axis=0), spread, TN, preferred_element_type=F32)

        o_b_all = per_lane(e_n * inv) * obn_ref[...] + per_lane(e_f * inv) * _in_sequence(obf_ref, lanes_sc)
        gam = g_ref[...]
        halves = [slice(0, bt // 2), slice(bt // 2, bt)]

        def gates_and_projection(rows):
            o_a, o_b = oa_ref[rows, :], o_b_all[rows]
            sa, dsa = gate(ga_ref[rows, :])
            sb, dsb = gate(gb_ref[rows, :])
            mix = jnp.concatenate([o_a * sa, o_b * sb], axis=1).astype(BF16)
            z = ALPHA * x_ref[rows, :] + jnp.dot(mix, w_ref[...], preferred_element_type=F32)
            return o_a, o_b, sa, dsa, sb, dsb, mix, z

        def norm_and_back(rows, mix, z):
            mu = jnp.mean(z, axis=1, keepdims=True)
            zc = z - mu
            rstd = lax.rsqrt(jnp.mean(zc * zc, axis=1, keepdims=True) + LN_EPS)
            xhat = zc * rstd
            diff = xhat * gam + b_ref[...] - tgt_ref[rows, :]
            dy = diff * (1.0 / D_MODEL)
            small_ref[0:1, :] += jnp.sum(dy * xhat, axis=0, keepdims=True)
            small_ref[1:2, :] += jnp.sum(dy, axis=0, keepdims=True)
            small_ref[2:3, :] += jnp.sum(diff * diff, axis=0, keepdims=True)
            dxh = dy * gam
            dz = rstd * (dxh - jnp.mean(dxh, axis=1, keepdims=True)
                         - xhat * jnp.mean(dxh * xhat, axis=1, keepdims=True))
            dz_ref[rows, :] = dz
            dzb = dz.astype(BF16)
            gw_ref[...] += lax.dot_general(mix, dzb, TN, preferred_element_type=F32)
            return lax.dot_general(dzb, w_ref[...], NT, preferred_element_type=F32)

        def gate_back(rows, o_a, o_b, sa, dsa, sb, dsb, dmix):
            doa, dob = dmix[:, :HW] * sa, dmix[:, HW:] * sb
            doa_ref[rows, :] = doa.astype(BF16)
            dob_ref[rows, :] = dob.astype(BF16)
            dga_ref[rows, :] = (dmix[:, :HW] * o_a * dsa).astype(BF16)
            dgb_ref[rows, :] = (dmix[:, HW:] * o_b * dsb).astype(BF16)
            return dob, doa * o_a, dob * o_b

        fronts = [gates_and_projection(rows) for rows in halves]
        dmixes = [norm_and_back(rows, f[6], f[7]) for rows, f in zip(halves, fronts)]
        backs = [gate_back(rows, *f[:6], dmix) for rows, f, dmix in zip(halves, fronts, dmixes)]
        dob, prod_a, prod_b = (jnp.concatenate(parts, axis=0) for parts in zip(*backs))
        _by_class(dob, dobc_ref, lanes_sc)

        @pl.when(i == seq // bt - 1)
        def _():
            gwb_ref[...] = gw_ref[...].astype(BF16)

        head_of = (lax.broadcasted_iota(jnp.int32, (2 * HW, LANES), 0) % HW) // 64
        ind = (head_of == lax.broadcasted_iota(jnp.int32, (2 * HW, LANES), 1)).astype(BF16)

        def head_sums(prod):
            hi = prod.astype(BF16)
            lo = (prod - hi.astype(F32)).astype(BF16)
            sums = jnp.dot(jnp.concatenate([hi, lo], axis=1), ind, preferred_element_type=F32)
            return sums.T[:HEADS, :]

        da_ref[...] = head_sums(prod_a)
        db_ref[...] = head_sums(prod_b)

    def tok(width):
        return pl.BlockSpec((bt, width), lambda i: (i, 0))

    def full(shape):
        return pl.BlockSpec(shape, lambda i: (0,) * len(shape))

    stat = pl.BlockSpec((HEADS, bt), lambda i: (0, i))
    n_cls = ob_far.shape[0]
    by_class = pl.BlockSpec((n_cls, bt // n_cls, HW), lambda i: (0, i, 0))
    return _pcall(
        body, name="out_ln", grid=(seq // bt,),
        in_specs=[tok(HW), tok(HW), by_class, stat, stat, tok(HW), tok(HW), tok(D_MODEL), tok(D_MODEL),
                  full((D_MODEL, D_MODEL)), full((1, D_MODEL)), full((1, D_MODEL))],
        out_specs=[tok(D_MODEL), tok(HW), tok(HW), tok(HW), tok(HW), stat, stat, stat,
                   full((D_MODEL, D_MODEL)), full((8, D_MODEL)), by_class],
        out_shape=[jax.ShapeDtypeStruct((seq, D_MODEL), F32)] + [jax.ShapeDtypeStruct((seq, HW), BF16)] * 4
        + [jax.ShapeDtypeStruct((HEADS, seq), F32)] * 3
        + [jax.ShapeDtypeStruct((D_MODEL, D_MODEL), BF16), jax.ShapeDtypeStruct((8, D_MODEL), F32),
           jax.ShapeDtypeStruct(ob_far.shape, BF16)],
        scratch_shapes=[pltpu.VMEM((D_MODEL, D_MODEL), F32), pltpu.VMEM((HW // LANES, bt, LANES), F32)],
        compiler_params=_cparams(dimension_semantics=("arbitrary",)),
    )(oa, ob_near, ob_far, lse_near, lse_far, ga, gb, x, tgt, w_out, ln_g, ln_b)


def _bwd_mid(dq_m, dkn, dv, dkpe, dqb, dkb, dvb, far, dga, dgb, cq, ckv, qn, kvn, w_uq_r, w_ukv_r, qg, kvg, tabs, bt):
    n_cls = far[0].shape[0]
    seq = cq.shape[0]

    def body(dqm_ref, dkn_ref, dv_ref, dkpe_ref, dqb_ref, dkb_ref, dvb_ref, dqf_ref, dkf_ref, dvf_ref, dga_ref, dgb_ref,
             cq_ref, ckv_ref, qn_ref, kvn_ref, wuq_ref, wukv_ref, qg_ref, kvg_ref, tab_ref,
             dh_ref, guq3_ref, gukv3_ref, small_ref, seq_sc, guq_ref, gukv_ref):
        i = pl.program_id(0)

        @pl.when(i == 0)
        def _():
            guq_ref[...] = jnp.zeros(guq_ref.shape, F32)
            gukv_ref[...] = jnp.zeros(gukv_ref.shape, F32)
            small_ref[...] = jnp.zeros(small_ref.shape, F32)

        m_tabs = (tab_ref[0], tab_ref[1], tab_ref[2])
        d_tabs = (tab_ref[3], tab_ref[4], tab_ref[5])

        def rms_bwd(c, dn, gain):
            r = lax.rsqrt(jnp.mean(c * c, axis=1, keepdims=True) + RMS_EPS)
            u = dn * gain
            dc = r * u - c * (r * r * r) * jnp.mean(u * c, axis=1, keepdims=True)
            return dc, jnp.sum(dn * c * r, axis=0, keepdims=True)

        dqm = dqm_ref[0].T
        dq = jnp.concatenate(
            [dqm[:, :HW], _rope_wide(_rope_t, dqm[:, HW:], *m_tabs, MLA_ROPE // 2)], axis=1) * MLA_SCALE
        dq = dq.astype(BF16)
        dkv = jnp.concatenate([dkn_ref[...], dv_ref[...]], axis=1)
        guq_ref[...] += lax.dot_general(qn_ref[...], dq, TN, preferred_element_type=F32)
        dqn = lax.dot_general(dq, wuq_ref[...], NT, preferred_element_type=F32)
        gukv_ref[...] += lax.dot_general(kvn_ref[...], dkv, TN, preferred_element_type=F32)
        dkvn = lax.dot_general(dkv, wukv_ref[...], NT, preferred_element_type=F32)

        dh_ref[:, C_KR:C_GA] = _rope_t(dkpe_ref[...], *m_tabs, MLA_ROPE // 2).astype(BF16)
        dh_ref[:, C_GA:C_QB] = dga_ref[...]
        in_sequence = functools.partial(_in_sequence, lanes_sc=seq_sc)
        dqb = dqb_ref[0].T + in_sequence(dqf_ref)
        dh_ref[:, C_QB:C_KB] = (_rope_wide(_rope_t, dqb, *d_tabs, DIL_ROT // 2) * DIL_SCALE).astype(BF16)
        dkb = dkb_ref[...] + in_sequence(dkf_ref)
        dh_ref[:, C_KB:C_VB] = _rope_wide(_rope_t, dkb, *d_tabs, DIL_ROT // 2).astype(BF16)
        dh_ref[:, C_VB:C_GB] = (dvb_ref[...].astype(F32) + in_sequence(dvf_ref)).astype(BF16)
        dh_ref[:, C_GB:C_END] = dgb_ref[...]

        dcq, gq = rms_bwd(cq_ref[...], dqn, qg_ref[...])
        small_ref[0:1, :] += gq
        dckv, gkv = rms_bwd(ckv_ref[...], dkvn, kvg_ref[...])
        small_ref[1:2, :KV_RANK] += gkv
        dh_ref[:, C_CQ:C_CKV] = dcq.astype(BF16)
        dh_ref[:, C_CKV:C_KR] = dckv.astype(BF16)

        @pl.when(i == seq // bt - 1)
        def _():
            for h in range(HEADS):
                guq3_ref[h] = jnp.concatenate(
                    [guq_ref[:, MLA_NOPE * h:MLA_NOPE * (h + 1)],
                     guq_ref[:, HW + MLA_ROPE * h:HW + MLA_ROPE * (h + 1)]], axis=1).astype(BF16)
                gukv3_ref[h] = jnp.concatenate(
                    [gukv_ref[:, MLA_NOPE * h:MLA_NOPE * (h + 1)],
                     gukv_ref[:, HW + MLA_V * h:HW + MLA_V * (h + 1)]], axis=1).astype(BF16)

    def tok(width):
        return pl.BlockSpec((bt, width), lambda i: (i, 0))

    def tok_t(a):
        per = a.shape[2] // bt
        return pl.BlockSpec((1, a.shape[1], bt), lambda i: (i // per, 0, i % per))

    def full(shape):
        return pl.BlockSpec(shape, lambda i: (0,) * len(shape))

    by_class = pl.BlockSpec((n_cls, bt // n_cls, HW), lambda i: (0, i, 0))
    uq3 = (HEADS, Q_RANK, MLA_NOPE + MLA_ROPE)
    ukv3 = (HEADS, KV_RANK, MLA_NOPE + MLA_V)
    return _pcall(
        body, name="bwd_mid", grid=(seq // bt,),
        in_specs=[tok_t(dq_m), tok(HW), tok(HW), tok(LANES), tok_t(dqb), tok(HW), tok(HW), by_class, by_class, by_class,
                  tok(HW), tok(HW),
                  tok(Q_RANK), tok(KV_RANK), tok(Q_RANK), tok(KV_RANK),
                  full(w_uq_r.shape), full(w_ukv_r.shape), full((1, Q_RANK)), full((1, KV_RANK)),
                  pl.BlockSpec((6, bt, LANES), lambda i: (0, i, 0))],
        out_specs=[tok(C_END), full(uq3), full(ukv3), full((8, Q_RANK))],
        out_shape=[jax.ShapeDtypeStruct((seq, C_END), BF16), jax.ShapeDtypeStruct(uq3, BF16),
                   jax.ShapeDtypeStruct(ukv3, BF16), jax.ShapeDtypeStruct((8, Q_RANK), F32)],
        scratch_shapes=[pltpu.VMEM((HW // LANES, bt, LANES), F32), pltpu.VMEM(w_uq_r.shape, F32),
                        pltpu.VMEM(w_ukv_r.shape, F32)],
        compiler_params=_cparams(dimension_semantics=("arbitrary",)),
    )(dq_m, dkn, dv, dkpe, dqb, dkb, dvb, *far, dga, dgb, cq, ckv, qn, kvn, w_uq_r, w_ukv_r, qg, kvg, tabs)


def _grad_x(dz, dh, w_in_r, bt, ride=None):
    seq = dz.shape[0]
    n_steps = seq // bt

    def body(dz_ref, dh_ref, w_ref, gx_ref):
        gx_ref[...] = ALPHA * dz_ref[...] + lax.dot_general(
            dh_ref[...], w_ref[...], NT, preferred_element_type=F32)

    args = [dz, dh, w_in_r]
    in_specs = [pl.BlockSpec((bt, D_MODEL), lambda i: (i, 0)), pl.BlockSpec((bt, C_END), lambda i: (i, 0)),
                pl.BlockSpec(w_in_r.shape, lambda i: (0, 0))]
    out_specs = [pl.BlockSpec((bt, D_MODEL), lambda i: (i, 0))]
    out_shape = [jax.ShapeDtypeStruct((seq, D_MODEL), F32)]
    scratch = []
    body = _ride_along(body, ride, 0, len(args), len(out_shape), 0, n_steps)
    if ride is not None:
        args, in_specs = args + ride.args, in_specs + ride.in_specs
        out_specs, out_shape, scratch = out_specs + ride.out_specs, out_shape + ride.out_shape, ride.scratch
    return _pcall(
        body, name="grad_x", grid=(n_steps,),
        in_specs=in_specs, out_specs=out_specs, out_shape=out_shape, scratch_shapes=scratch,
        compiler_params=_cparams(dimension_semantics=("arbitrary",)),
    )(*args)


def _grad_w_in(x, dh, bt):
    seq = x.shape[0]
    shard = IN_WIDTH // N_DEV
    k_lo, k_hi = IN_SPLITS[0] + IN_SPLITS[1], IN_SPLITS[0] + IN_SPLITS[1] + MLA_ROPE

    def body(x_ref, dh_ref, out_ref, acc):
        i = pl.program_id(0)

        @pl.when(i == 0)
        def _():
            acc[...] = jnp.zeros(acc.shape, F32)

        acc[...] += lax.dot_general(x_ref[...].astype(BF16), dh_ref[...], TN, preferred_element_type=F32)

        @pl.when(i == seq // bt - 1)
        def _():
            kr = acc[:, C_KR:C_GA]
            kr = kr + pltpu.roll(kr, 96, 1) + pltpu.roll(kr, 64, 1) + pltpu.roll(kr, 32, 1)
            for d in range(N_DEV):
                lo, hi = shard * d, shard * (d + 1)
                pieces = []
                if lo < k_lo:
                    pieces.append(acc[:, lo:min(hi, k_lo)])
                if lo < k_hi and hi > k_lo:
                    pieces.append(kr[:, max(lo, k_lo) - k_lo:min(hi, k_hi) - k_lo])
                if hi > k_hi:
                    shift = C_GA - k_hi
                    pieces.append(acc[:, max(lo, k_hi) + shift:hi + shift])
                blk = pieces[0] if len(pieces) == 1 else jnp.concatenate(pieces, axis=1)
                out_ref[d] = blk.astype(BF16)

    return _pcall(
        body, name="grad_w_in", grid=(seq // bt,),
        in_specs=[pl.BlockSpec((bt, D_MODEL), lambda i: (i, 0)), pl.BlockSpec((bt, C_END), lambda i: (i, 0))],
        out_specs=pl.BlockSpec((N_DEV, D_MODEL, shard), lambda i: (0, 0, 0)),
        out_shape=jax.ShapeDtypeStruct((N_DEV, D_MODEL, shard), BF16),
        scratch_shapes=[pltpu.VMEM((D_MODEL, C_END), F32)],
        compiler_params=_cparams(dimension_semantics=("arbitrary",)),
    )(x, dh)


def _local_step(x, tgt, w_in_r, w_uq_r, w_ukv_r, w_out_rider, g_out_rider, reduce_rider, q_norm_g, kv_norm_g,
                ln_g, ln_b, bt=BLOCK_TOKENS, blk_m=BLOCK_MLA, blk_d=BLOCK_DIL):
    seq = x.shape[0]
    tabs = jnp.asarray(_rope_tables(seq))
    qg, kvg = q_norm_g.reshape(1, -1), kv_norm_g.reshape(1, -1)

    far_dil = DIL_CONFIGS[-1][1]
    cls = seq // far_dil
    (cq, ckv, qn, kvn, qcat, kn, kpe, v, ga, gb, qb, kb, vb, knt, kpet, vt, kbt, vbt, qb_c, kb_c, vb_c) = _fwd_proj(
        x, w_in_r, w_uq_r, w_ukv_r, qg, kvg, tabs, bt, far_dil)
    qb_c, kb_c, vb_c = (a.reshape(seq, HW) for a in (qb_c, kb_c, vb_c))

    nq_m, nq_d = seq // blk_m, seq // blk_d
    bias_m = _mla_bias_t(blk_m)
    oa, lse_a, w_out = _attn_fwd(
        "mla_fwd", qcat, kn, kpe, vt, bias_m, _steps(nq_m, nq_m, False, True), blk_m, ride=w_out_rider)

    bias_near = _dil_bias_t(blk_d, DIL_NEAR)
    ob_near, lse_near = _attn_fwd(
        "dil_fwd", qb, kb, None, vbt, bias_near, _steps(nq_d, -(-DIL_NEAR // blk_d), False, False), blk_d)
    each = jnp.arange(far_dil, dtype=jnp.int32)
    steps_far = [each, each, jnp.zeros_like(each), jnp.ones_like(each), jnp.ones_like(each)]
    bias_far = _dil_far_bias_t(cls)
    ob_far, lse_far = _attn_fwd(
        "dil_far_fwd", qb_c, kb_c, None, vb_c, bias_far, steps_far, cls, v_token_major=True)

    dz, doa, dob, dga, dgb, dst_a, dst_b, lse_b, g_out, small1, dob_c = _out_ln(
        oa, ob_near, ob_far.reshape(far_dil, cls, HW), lse_near, _lanes_from_classes(lse_far, far_dil), ga, gb, x, tgt,
        w_out.reshape(D_MODEL, D_MODEL), ln_g.reshape(1, -1), ln_b.reshape(1, -1), bt)

    dq_m, dkn, dkpe, dv, g_out_recv = _attn_bwd(
        "mla_bwd", qcat, kn, kpe, v, knt, kpet, bias_m, doa, lse_a, dst_a, _steps(nq_m, nq_m, True, True), blk_m,
        ride=g_out_rider(g_out.reshape(N_DEV, D_MODEL // N_DEV, D_MODEL)))
    dqb, dkb_near, dvb_near = _attn_bwd(
        "dil_bwd", qb, kb, None, vb, kbt, None, bias_near, dob, lse_b, dst_b,
        _steps(nq_d, -(-DIL_NEAR // blk_d), True, False), blk_d)
    dqb_far, dkb_far, dvb_far = _attn_bwd(
        "dil_far_bwd", qb_c, kb_c, None, vb_c, None, None, bias_far, dob_c.reshape(seq, HW),
        _lanes_to_classes(lse_b, far_dil), _lanes_to_classes(dst_b, far_dil), steps_far, cls, single_visit=True)
    far = [a.reshape(far_dil, cls, HW) for a in (dqb_far, dkb_far, dvb_far)]

    dh, g_uq, g_ukv, small2 = _bwd_mid(
        dq_m, dkn, dv, dkpe, dqb, dkb_near, dvb_near, far, dga, dgb, cq, ckv, qn, kvn, w_uq_r, w_ukv_r, qg, kvg, tabs, bt)
    g_in = _grad_w_in(x, dh, min(seq, 2 * bt))
    grads3 = [g_in, g_uq, g_ukv]
    small_part = _small_rows(small1[0], small1[1], small2[0, :Q_RANK], small2[1, :KV_RANK], small1[2])
    grad_x, *reduced = _grad_x(
        dz, dh, w_in_r, bt, ride=reduce_rider(grads3, g_out_recv, small_part, (1, seq // bt - 2)))
    return grad_x, reduced


MESH_ID = pl.DeviceIdType.MESH
SHARD_SHAPES = ((D_MODEL, IN_WIDTH // N_DEV), (Q_RANK, 768 // N_DEV), (KV_RANK, 1024 // N_DEV), (D_MODEL // N_DEV, D_MODEL))
ADAM_ROWS = (32, 128, 128, 16)


def _me():
    x, y, c = lax.axis_index("x"), lax.axis_index("y"), lax.axis_index("c")
    return x, y, c, 4 * x + 2 * y + c


def _peer(k):
    x, y, c, _ = _me()
    px = 1 - x if (k >> 2) & 1 else x
    py = 1 - y if (k >> 1) & 1 else y
    pc = 1 - c if k & 1 else c
    return (px, py, pc), 4 * px + 2 * py + pc


def _all_gather_weights(shards):
    n = len(shards)
    shard = IN_WIDTH // N_DEV
    k_lo = IN_SPLITS[0] + IN_SPLITS[1]
    k_hi = k_lo + MLA_ROPE

    def body(*refs):
        ins = refs[:n]
        win_ref, wuq_ref, wukv_ref = refs[n:2 * n]
        bufs = refs[2 * n:3 * n]
        send_sems, recv_sems = refs[3 * n:]
        x, y, c, me = _me()
        here, sibling = (x, y, c), (x, y, 1 - c)
        along_x, along_y, across = (1 - x, y), (x, 1 - y), (1 - x, 1 - y)
        for t in range(n):
            bufs[t][me] = ins[t][...].astype(BF16)

        def copy(t, k, chip, pc, to, half=None):
            blk = bufs[t].at[4 * chip[0] + 2 * chip[1] + pc]
            if half is not None:
                rows = SHARD_SHAPES[t][0] // 2
                blk = blk.at[pl.ds(half * rows, rows), :]
            return pltpu.make_async_remote_copy(
                src_ref=blk, dst_ref=blk, send_sem=send_sems.at[t, k], recv_sem=recv_sems.at[t, k],
                device_id=to, device_id_type=MESH_ID)

        sends = []
        for t in range(n):
            sends += [copy(t, 0, (x, y), c, sibling), copy(t, 1, (x, y), c, (*along_x, c)),
                      copy(t, 2, (x, y), c, (*along_y, c))]
        for cp in sends:
            cp.start()
        for t in range(n):
            copy(t, 1, along_x, c, here).wait_recv()
            sends += [copy(t, 3, along_x, c, (*along_y, c), half=0), copy(t, 5, along_x, c, sibling)]
            sends[-2].start()
            sends[-1].start()
        for t in range(n):
            copy(t, 2, along_y, c, here).wait_recv()
            sends += [copy(t, 4, along_y, c, (*along_x, c), half=1), copy(t, 6, along_y, c, sibling)]
            sends[-2].start()
            sends[-1].start()
        for t in range(n):
            copy(t, 3, across, c, here, half=0).wait_recv()
            copy(t, 4, across, c, here, half=1).wait_recv()
            sends.append(copy(t, 7, across, c, sibling))
            sends[-1].start()
        for t in range(n):
            copy(t, 0, (x, y), 1 - c, here).wait_recv()
            for k, chip in ((5, along_x), (6, along_y), (7, across)):
                copy(t, k, chip, 1 - c, here).wait_recv()
        for cp in sends:
            cp.wait_send()

        a_in, a_uq, a_ukv = bufs
        for d in range(N_DEV):
            lo, hi = shard * d, shard * (d + 1)
            if lo < k_lo:
                win_ref[:, lo:min(hi, k_lo)] = a_in[d, :, 0:min(hi, k_lo) - lo]
            if lo < k_hi and hi > k_lo:
                kr = a_in[d, :, k_lo - lo:k_hi - lo]
                for rep in range(4):
                    win_ref[:, C_KR + MLA_ROPE * rep:C_KR + MLA_ROPE * (rep + 1)] = kr
            if hi > k_hi:
                src = max(lo, k_hi)
                win_ref[:, src + C_GA - k_hi:hi + C_GA - k_hi] = a_in[d, :, src - lo:hi - lo]
        for h in range(HEADS):
            wuq_ref[:, MLA_NOPE * h:MLA_NOPE * (h + 1)] = a_uq[h, :, :MLA_NOPE]
            wuq_ref[:, HW + MLA_ROPE * h:HW + MLA_ROPE * (h + 1)] = a_uq[h, :, MLA_NOPE:]
            wukv_ref[:, MLA_NOPE * h:MLA_NOPE * (h + 1)] = a_ukv[h, :, :MLA_NOPE]
            wukv_ref[:, HW + MLA_V * h:HW + MLA_V * (h + 1)] = a_ukv[h, :, MLA_NOPE:]

    vmem = pl.BlockSpec(memory_space=pltpu.VMEM)
    return _pcall(
        body, name="gather_weights",
        in_specs=[vmem] * n, out_specs=[vmem] * n,
        out_shape=[jax.ShapeDtypeStruct((D_MODEL, C_END), BF16), jax.ShapeDtypeStruct((Q_RANK, QW), BF16),
                   jax.ShapeDtypeStruct((KV_RANK, 2 * HW), BF16)],
        scratch_shapes=[pltpu.VMEM((N_DEV,) + s, BF16) for s in SHARD_SHAPES[:n]]
        + [pltpu.SemaphoreType.DMA((n, 8)), pltpu.SemaphoreType.DMA((n, 8))],
        compiler_params=_cparams(),
    )(*shards)


def _gather_w_out_rider(w_out):
    def copies(full_ref, stage, send_sems, recv_sems):
        me = _me()[3]
        out = []
        for k in range(1, N_DEV):
            peer, pidx = _peer(k)
            send = pltpu.make_async_remote_copy(
                src_ref=stage, dst_ref=full_ref.at[me], send_sem=send_sems.at[k - 1], recv_sem=recv_sems.at[k - 1],
                device_id=peer, device_id_type=MESH_ID)
            recv = pltpu.make_async_remote_copy(
                src_ref=stage, dst_ref=full_ref.at[pidx], send_sem=send_sems.at[k - 1], recv_sem=recv_sems.at[k - 1],
                device_id=peer, device_id_type=MESH_ID)
            out.append((send, recv))
        return out

    def start(ins, outs, scr):
        stage, send_sems, recv_sems, own_sem = scr
        stage[...] = ins[0][...].astype(BF16)
        pltpu.make_async_copy(stage, outs[0].at[_me()[3]], own_sem).start()
        for send, _ in copies(outs[0], stage, send_sems, recv_sems):
            send.start()

    def finish(ins, outs, scr):
        stage, send_sems, recv_sems, own_sem = scr
        pltpu.make_async_copy(stage, outs[0].at[_me()[3]], own_sem).wait()
        pairs = copies(outs[0], stage, send_sems, recv_sems)
        for _, recv in pairs:
            recv.wait_recv()
        for send, _ in pairs:
            send.wait_send()

    shape = SHARD_SHAPES[3]
    return Rider(
        args=[w_out], in_specs=[pl.BlockSpec(shape, lambda t, *_: (0, 0))],
        out_shape=[jax.ShapeDtypeStruct((N_DEV,) + shape, BF16)], out_specs=[pl.BlockSpec(memory_space=pl.ANY)],
        scratch=[pltpu.VMEM(shape, BF16), pltpu.SemaphoreType.DMA((N_DEV - 1,)), pltpu.SemaphoreType.DMA((N_DEV - 1,)),
                 pltpu.SemaphoreType.DMA],
        start=start, finish=finish)


def _scatter_g_out_rider(blocks):
    def copies(src_ref, dst_ref, send_sems, recv_sems):
        out = []
        for k in range(1, N_DEV):
            peer, pidx = _peer(k)
            out.append(pltpu.make_async_remote_copy(
                src_ref=src_ref.at[pidx], dst_ref=dst_ref.at[k], send_sem=send_sems.at[k - 1],
                recv_sem=recv_sems.at[k - 1], device_id=peer, device_id_type=MESH_ID))
        return out

    def start(ins, outs, scr):
        send_sems, recv_sems, own_sem = scr
        pltpu.make_async_copy(ins[0].at[_me()[3]], outs[0].at[0], own_sem).start()
        for cp in copies(ins[0], outs[0], send_sems, recv_sems):
            cp.start()

    def finish(ins, outs, scr):
        send_sems, recv_sems, own_sem = scr
        pltpu.make_async_copy(ins[0].at[_me()[3]], outs[0].at[0], own_sem).wait()
        for cp in copies(ins[0], outs[0], send_sems, recv_sems):
            cp.wait()

    hbm = pl.BlockSpec(memory_space=pl.ANY)
    return Rider(
        args=[blocks], in_specs=[hbm], out_shape=[jax.ShapeDtypeStruct(blocks.shape, blocks.dtype)], out_specs=[hbm],
        scratch=[pltpu.SemaphoreType.DMA((N_DEV - 1,)), pltpu.SemaphoreType.DMA((N_DEV - 1,)), pltpu.SemaphoreType.DMA],
        start=start, finish=finish)


def _adamw(w, g, m, v):
    m = ADAM_B1 * m + (1.0 - ADAM_B1) * g
    v = ADAM_B2 * v + (1.0 - ADAM_B2) * jnp.square(g)
    m_hat = m / (1.0 - ADAM_B1 ** ADAM_STEP)
    v_hat = v / (1.0 - ADAM_B2 ** ADAM_STEP)
    delta = -ADAM_LR * (m_hat / (jnp.sqrt(v_hat) + ADAM_EPS) + ADAM_WD * w)
    return delta, m, v


HALF_ROWS = (32, 64, 128)


def _reduce_grads_rider(grads3, arrived, small_part, steps):
    n = len(grads3)
    first_round, second_round = steps

    class Refs:
        def __init__(self, ins, outs, scr):
            self.g3, self.arr, self.sp = ins[0:n], ins[n], ins[n + 1]
            self.gsum, self.gsum_out, self.ssum = outs[0:n], outs[n], outs[n + 1]
            self.own, self.sib, self.part = scr[0:n], scr[n:2 * n], scr[2 * n:3 * n]
            self.in_a, self.out_b, self.in_b = scr[3 * n:4 * n], scr[4 * n:5 * n], scr[5 * n:6 * n]
            self.rsmall = scr[6 * n]
            (self.loc_sems, self.d2d_send, self.d2d_recv, self.a_send, self.a_recv, self.b_send, self.b_recv,
             self.sm_send, self.sm_recv) = scr[6 * n + 1:]
            self.x, self.y, self.c, self.me = _me()
            self.along_x, self.along_y = (1 - self.x, self.y), (self.x, 1 - self.y)
            self.across = (1 - self.x, 1 - self.y)

        def small(self):
            return [pltpu.make_async_remote_copy(
                src_ref=self.rsmall.at[0], dst_ref=self.rsmall.at[k], send_sem=self.sm_send.at[k - 1],
                recv_sem=self.sm_recv.at[k - 1], device_id=_peer(k)[0], device_id_type=MESH_ID)
                for k in range(1, N_DEV)]

        def level1(self):
            local, to_sib = [], []
            for t in range(n):
                for q in range(4):
                    local.append(pltpu.make_async_copy(
                        self.g3[t].at[2 * q + self.c], self.own[t].at[q], self.loc_sems.at[t, q]))
                    to_sib.append(pltpu.make_async_remote_copy(
                        src_ref=self.g3[t].at[2 * q + 1 - self.c], dst_ref=self.sib[t].at[q],
                        send_sem=self.d2d_send.at[t, q], recv_sem=self.d2d_recv.at[t, q],
                        device_id=(self.x, self.y, 1 - self.c), device_id_type=MESH_ID))
            return local, to_sib

        def round_a(self):
            out = []
            for t in range(n):
                half = SHARD_SHAPES[t][0] // 2
                for k, (to, chip, h) in enumerate([(self.along_x, self.along_x, 0), (self.along_x, self.across, 0),
                                                   (self.along_y, self.along_y, 1), (self.along_y, self.across, 1)]):
                    out.append(pltpu.make_async_remote_copy(
                        src_ref=self.part[t].at[2 * chip[0] + chip[1], pl.ds(h * half, half), :],
                        dst_ref=self.in_a[t].at[k], send_sem=self.a_send.at[t, k], recv_sem=self.a_recv.at[t, k],
                        device_id=(*to, self.c), device_id_type=MESH_ID))
            return out

        def round_b(self):
            return [pltpu.make_async_remote_copy(
                src_ref=self.out_b[t].at[k], dst_ref=self.in_b[t].at[k], send_sem=self.b_send.at[t, k],
                recv_sem=self.b_recv.at[t, k], device_id=(*to, self.c), device_id_type=MESH_ID)
                for t in range(n) for k, to in enumerate([self.along_y, self.along_x])]

    def chunks(t, rows, count, fn):
        def step(i, carry):
            fn(pl.multiple_of(i * rows, rows))
            return carry

        lax.fori_loop(0, count // rows, step, 0)

    def start(*refs):
        r = Refs(*refs)
        r.rsmall[0] = r.sp[...]
        local, to_sib = r.level1()
        for cp in r.small() + local + to_sib:
            cp.start()

    def begin_rounds(*refs):
        r = Refs(*refs)
        local, to_sib = r.level1()
        for cp in local:
            cp.wait()
        for cp in to_sib:
            cp.wait_recv()
        my_chip = 2 * r.x + r.y
        for t in range(n):
            rows = ADAM_ROWS[t]

            def pair_sums(at, t=t, rows=rows):
                sl = pl.ds(at, rows)
                for q in range(4):
                    r.part[t][q, sl, :] = (r.own[t][q, sl, :].astype(F32) + r.sib[t][q, sl, :].astype(F32)).astype(BF16)
                r.gsum[t][sl, :] = r.own[t][my_chip, sl, :].astype(F32) + r.sib[t][my_chip, sl, :].astype(F32)

            chunks(t, rows, SHARD_SHAPES[t][0], pair_sums)
        for cp in r.round_a():
            cp.start()

        def add_arrived(at):
            sl = pl.ds(at, ADAM_ROWS[3])
            g = r.arr[0, sl, :].astype(F32)
            for k in range(1, N_DEV):
                g = g + r.arr[k, sl, :].astype(F32)
            r.gsum_out[sl, :] = g

        chunks(3, ADAM_ROWS[3], SHARD_SHAPES[3][0], add_arrived)

    def pass_on(*refs):
        r = Refs(*refs)
        for cp in r.round_a():
            cp.wait_recv()
        q_x, q_y = 2 * r.along_x[0] + r.along_x[1], 2 * r.along_y[0] + r.along_y[1]
        for t in range(n):
            rows, half = HALF_ROWS[t], SHARD_SHAPES[t][0] // 2

            def add(at, t=t, rows=rows, half=half):
                lo, hi = pl.ds(at, rows), pl.ds(half + at, rows)
                r.gsum[t][lo, :] = r.gsum[t][lo, :] + r.in_a[t][0, lo, :].astype(F32)
                r.out_b[t][0, lo, :] = (r.part[t][q_y, lo, :].astype(F32) + r.in_a[t][1, lo, :].astype(F32)).astype(BF16)
                r.gsum[t][hi, :] = r.gsum[t][hi, :] + r.in_a[t][2, lo, :].astype(F32)
                r.out_b[t][1, lo, :] = (r.part[t][q_x, hi, :].astype(F32) + r.in_a[t][3, lo, :].astype(F32)).astype(BF16)

            chunks(t, rows, half, add)
        for cp in r.round_b():
            cp.start()

    def finish(*refs):
        r = Refs(*refs)
        passed = r.round_b()
        for cp in passed:
            cp.wait_recv()
        for t in range(n):
            rows, half = HALF_ROWS[t], SHARD_SHAPES[t][0] // 2

            def add(at, t=t, rows=rows, half=half):
                lo, hi = pl.ds(at, rows), pl.ds(half + at, rows)
                r.gsum[t][lo, :] = r.gsum[t][lo, :] + r.in_b[t][0, lo, :].astype(F32)
                r.gsum[t][hi, :] = r.gsum[t][hi, :] + r.in_b[t][1, lo, :].astype(F32)

            chunks(t, rows, half, add)
        small = r.small()
        for cp in small:
            cp.wait_recv()
        tot = r.rsmall[r.me]
        for d in range(1, N_DEV):
            tot = tot + r.rsmall[jnp.bitwise_xor(r.me, d)]
        r.ssum[...] = tot
        for cp in small + r.level1()[1] + r.round_a() + passed:
            cp.wait_send()

    hbm = pl.BlockSpec(memory_space=pl.ANY)
    dma = pltpu.SemaphoreType.DMA

    def whole(shape):
        return pl.BlockSpec(shape, lambda i: (0,) * len(shape))

    def halves(slots):
        return [pltpu.VMEM((slots, s[0] // 2, s[1]), BF16) for s in SHARD_SHAPES[:n]]

    out_shapes = list(SHARD_SHAPES) + [(8, D_MODEL)]
    return Rider(
        args=list(grads3) + [arrived, small_part],
        in_specs=[hbm] * n + [whole(arrived.shape), whole(small_part.shape)],
        out_shape=[jax.ShapeDtypeStruct(s, F32) for s in out_shapes], out_specs=[whole(s) for s in out_shapes],
        scratch=[pltpu.VMEM((4,) + s, BF16) for _ in range(3) for s in SHARD_SHAPES[:n]]
        + halves(4) + halves(2) + halves(2)
        + [pltpu.VMEM((N_DEV, 8, D_MODEL), F32), dma((n, 4)), dma((n, 4)), dma((n, 4)), dma((n, 4)), dma((n, 4)),
           dma((n, 2)), dma((n, 2)), dma((N_DEV - 1,)), dma((N_DEV - 1,))],
        start=start, finish=finish, stages=((first_round, begin_rounds), (second_round, pass_on)))


def _adamw_update(grads, small_grad, wmv, small_wmv):
    n_small = len(small_wmv)

    def body(*refs):
        g_refs, sg_ref = refs[0:4], refs[4]
        wmv_refs = [refs[5 + 3 * t:8 + 3 * t] for t in range(4)]
        swmv_refs = [refs[17 + 3 * t:20 + 3 * t] for t in range(n_small)]
        outs = refs[17 + 3 * n_small:]
        out_refs = [outs[4 * t:4 * t + 4] for t in range(4)]
        sout_refs = [outs[16 + 4 * t:20 + 4 * t] for t in range(n_small)]
        loss_ref = outs[16 + 4 * n_small]
        for t, (w_ref, m_ref, v_ref) in enumerate(swmv_refs):
            g = sg_ref[t:t + 1, :w_ref.shape[1]]
            delta, m, v = _adamw(w_ref[...], g, m_ref[...], v_ref[...])
            sout_refs[t][0][...], sout_refs[t][1][...], sout_refs[t][2][...], sout_refs[t][3][...] = g, delta, m, v
        loss_ref[...] = (0.5 / D_MODEL) * jnp.sum(sg_ref[n_small:n_small + 1, :], axis=1, keepdims=True)
        for t in range(4):
            rows = ADAM_ROWS[t]
            w_ref, m_ref, v_ref = wmv_refs[t]
            g_out, d_out, m_out, v_out = out_refs[t]

            def step(i, carry, g_ref=g_refs[t], rows=rows, w_ref=w_ref, m_ref=m_ref, v_ref=v_ref,
                     g_out=g_out, d_out=d_out, m_out=m_out, v_out=v_out):
                r = pl.ds(pl.multiple_of(i * rows, rows), rows)
                g = g_ref[r, :]
                delta, m, v = _adamw(w_ref[r, :], g, m_ref[r, :], v_ref[r, :])
                g_out[r, :], d_out[r, :], m_out[r, :], v_out[r, :] = g, delta, m, v
                return carry

            lax.fori_loop(0, SHARD_SHAPES[t][0] // rows, step, 0)

    vmem = pl.BlockSpec(memory_space=pltpu.VMEM)
    flat_wmv = [a for trio in wmv for a in trio]
    flat_small = [a for trio in small_wmv for a in trio]
    out_shape = ([jax.ShapeDtypeStruct(s, F32) for s in SHARD_SHAPES for _ in range(4)]
                 + [jax.ShapeDtypeStruct(trio[0].shape, F32) for trio in small_wmv for _ in range(4)]
                 + [jax.ShapeDtypeStruct((1, 1), F32)])
    return _pcall(
        body, name="adamw",
        in_specs=[vmem] * (5 + len(flat_wmv) + len(flat_small)), out_specs=[vmem] * len(out_shape),
        out_shape=out_shape,
        compiler_params=_cparams(),
    )(*grads, small_grad, *flat_wmv, *flat_small)


def _small_rows(ln_g, ln_b, q_norm_g, kv_norm_g, extra=None):
    pad = lambda a: jnp.pad(a, (0, D_MODEL - a.shape[0]))
    rows = [ln_g, ln_b, pad(q_norm_g), pad(kv_norm_g)] + ([] if extra is None else [extra])
    return jnp.pad(jnp.stack(rows), ((0, 8 - len(rows)), (0, 0)))


def kernel(x, w_in, q_norm_g, kv_norm_g, w_uq, w_ukv, w_out, ln_g, ln_b, loss_target, m_w_in, m_q_norm_g, m_kv_norm_g, m_w_uq, m_w_ukv, m_w_out, m_ln_g, m_ln_b, v_w_in, v_q_norm_g, v_kv_norm_g, v_w_uq, v_w_ukv, v_w_out, v_ln_g, v_ln_b):
    w_in_r, w_uq_r, w_ukv_r = _all_gather_weights([w_in, w_uq, w_ukv])
    grad_x, sums = _local_step(
        x[0], loss_target[0], w_in_r, w_uq_r, w_ukv_r, _gather_w_out_rider(w_out), _scatter_g_out_rider,
        _reduce_grads_rider, q_norm_g, kv_norm_g, ln_g, ln_b)
    row = lambda a: a.reshape(1, -1)
    small_wmv = [(row(ln_g), row(m_ln_g), row(v_ln_g)), (row(ln_b), row(m_ln_b), row(v_ln_b)),
                 (row(q_norm_g), row(m_q_norm_g), row(v_q_norm_g)), (row(kv_norm_g), row(m_kv_norm_g), row(v_kv_norm_g))]
    wmv = [(w_in, m_w_in, v_w_in), (w_uq, m_w_uq, v_w_uq), (w_ukv, m_w_ukv, v_w_ukv), (w_out, m_w_out, v_w_out)]
    res = _adamw_update(sums[:4], sums[4], wmv, small_wmv)
    big = [res[4 * t:4 * t + 4] for t in range(4)]
    small = [[a.reshape(-1) for a in res[16 + 4 * t:20 + 4 * t]] for t in range(4)]
    loss = res[32].reshape(())

    def group(kind):
        return (big[0][kind], small[2][kind], small[3][kind], big[1][kind], big[2][kind], big[3][kind],
                small[0][kind], small[1][kind])

    return (loss, grad_x[None], *group(0), *group(1), *group(2), *group(3))
```

```python
import functools
from typing import Callable, NamedTuple

import numpy as np
import jax
import jax.numpy as jnp
from jax import lax
from jax.experimental import pallas as pl
from jax.experimental.pallas import tpu as pltpu

F32 = jnp.float32
BF16 = jnp.bfloat16

D_MODEL = 1024
ROPE_THETA = 500000.0
NEG = -1e30
RMS_EPS = 1e-6
LN_EPS = 1e-5
HEADS = 8
MLA_NOPE = 64
MLA_ROPE = 32
MLA_V = 64
Q_RANK = 384
KV_RANK = 256
DIL_HEAD = 64
DIL_ROT = 16
DIL_CONFIGS = ((128, 1), (512, 4), (2048, 16))
DIL_NEAR = 512
HW = HEADS * 64
QW = HW + HEADS * MLA_ROPE
IN_SPLITS = (Q_RANK, KV_RANK, MLA_ROPE, HW, HW, HW, HW, HW)
IN_WIDTH = sum(IN_SPLITS)
ALPHA = 2.0 ** 0.25
MLA_SCALE = (MLA_NOPE + MLA_ROPE) ** -0.5
DIL_SCALE = DIL_HEAD ** -0.5
LOG2E = 1.4426950408889634
LN2 = 0.6931471805599453

ADAM_LR = 0.001
ADAM_B1 = 0.9
ADAM_B2 = 0.999
ADAM_EPS = 1e-08
ADAM_WD = 0.01
ADAM_STEP = 10

N_DEV = 8
LANES = 128
VMEM_LIMIT = 56 * 1024 * 1024
BLOCK_TOKENS = 512
BLOCK_MLA = 512
BLOCK_DIL = 512
GRAD_W_IN_CUT = 384

C_CQ, C_CKV, C_KR, C_GA, C_QB, C_KB, C_VB, C_GB, C_END = 0, 384, 640, 768, 1280, 1792, 2304, 2816, 3328

NT = (((1,), (1,)), ((), ()))
TN = (((0,), (0,)), ((), ()))


def _pcall(body, **kw):
    return pl.pallas_call(body, **kw)


def _cparams(**kw):
    return pltpu.CompilerParams(vmem_limit_bytes=VMEM_LIMIT, **kw)


def _rope_tables(seq):
    def tabs(dim, period):
        half = dim // 2
        inv = np.float32(ROPE_THETA) ** (-np.arange(0, dim, 2, dtype=np.float32) / np.float32(dim))
        ang = np.arange(seq, dtype=np.float32)[:, None] * inv.astype(np.float32)[None, :]
        cos, sin = np.cos(ang).astype(np.float32), np.sin(ang).astype(np.float32)
        j = np.arange(LANES) % period
        f = j % half
        c = np.where(j < dim, cos[:, f], np.float32(1.0))
        s1 = np.where(j < half, -sin[:, f], np.float32(0.0))
        s2 = np.where((j >= half) & (j < dim), sin[:, f], np.float32(0.0))
        return [c, s1, s2]
    return np.stack(tabs(MLA_ROPE, MLA_ROPE) + tabs(DIL_ROT, DIL_HEAD)).astype(np.float32)


def _rope(t, c, s1, s2, half):
    return t * c + pltpu.roll(t, LANES - half, 1) * s1 + pltpu.roll(t, half, 1) * s2


def _rope_t(d, c, s1, s2, half):
    return d * c + pltpu.roll(d * s1, half, 1) + pltpu.roll(d * s2, LANES - half, 1)


def _rope_wide(fn, t, c, s1, s2, half):
    return jnp.concatenate(
        [fn(t[:, i:i + LANES], c, s1, s2, half) for i in range(0, t.shape[1], LANES)], axis=1)


def _mla_bias_t(blk):
    a = np.arange(blk)
    causal = np.where(a[:, None] <= a[None, :], 0.0, NEG)
    return np.stack([np.zeros((blk, blk)), causal]).astype(np.float32)


def _dil_bias_t(blk, reach):
    a = np.arange(blk)
    out = []
    for off in range(-(-reach // blk) + 1):
        delta = blk * off + a[None, :] - a[:, None]
        mult = np.zeros((blk, blk))
        for window, dil in DIL_CONFIGS:
            mult += (delta >= 0) & (delta % dil == 0) & (delta <= min(window, reach))
        out.append(np.where(mult > 0, np.log2(np.maximum(mult, 1.0)), NEG))
    return np.stack(out).astype(np.float32)


def _dil_far_bias_t(length):
    window, dil = DIL_CONFIGS[-1]
    a = np.arange(length)
    steps_back = a[None, :] - a[:, None]
    seen = (steps_back * dil > DIL_NEAR) & (steps_back * dil <= window)
    return np.where(seen, 0.0, NEG).astype(np.float32)[None]


def _lanes_to_classes(a, dil):
    h, s = a.shape
    return a.reshape(h, s // dil, dil).transpose(0, 2, 1).reshape(h, s)


def _lanes_from_classes(a, dil):
    h, s = a.shape
    return a.reshape(h, dil, s // dil).transpose(0, 2, 1).reshape(h, s)


def _steps(nq, span, by_key, diag_only_bias):
    rows = []
    if by_key:
        for ki in range(nq):
            hi = min(nq - 1, ki + span)
            for qi in range(ki, hi + 1):
                rows.append((qi, ki, int(qi == ki), int(qi == hi)))
    else:
        for qi in range(nq):
            lo = max(0, qi - span)
            for ki in range(lo, qi + 1):
                rows.append((qi, ki, int(ki == lo), int(ki == qi)))
    arr = np.array(rows, dtype=np.int32)
    off = arr[:, 0] - arr[:, 1]
    bias_idx = (off == 0).astype(np.int32) if diag_only_bias else off.astype(np.int32)
    return [jnp.asarray(v) for v in (arr[:, 0], arr[:, 1], bias_idx, arr[:, 2], arr[:, 3])]


def _by_class(val, out_ref, lanes_sc):
    n_cls, per = out_ref.shape[0], out_ref.shape[1]
    for c in range(val.shape[1] // LANES):
        lanes_sc[c] = val[:, LANES * c:LANES * (c + 1)]
        for r in range(n_cls):
            rows = lanes_sc.at[c][pl.ds(r, per, stride=n_cls), :]
            out_ref[r, :, LANES * c:LANES * (c + 1)] = rows.astype(out_ref.dtype)


def _in_sequence(ref, lanes_sc):
    n_cls, per, width = ref.shape
    for c in range(width // LANES):
        for r in range(n_cls):
            lanes_sc.at[c][pl.ds(r, per, stride=n_cls), :] = ref[r, :, LANES * c:LANES * (c + 1)].astype(F32)
    return jnp.concatenate([lanes_sc[c] for c in range(width // LANES)], axis=1)


def _fwd_proj(x, w_in_r, w_uq_r, w_ukv_r, qg, kvg, tabs, bt, n_cls):
    seq = x.shape[0]

    def body(x_ref, win_ref, wuq_ref, wukv_ref, qg_ref, kvg_ref, tab_ref,
             cq_ref, ckv_ref, qn_ref, kvn_ref, qcat_ref, kn_ref, kpe_ref, v_ref,
             ga_ref, gb_ref, qb_ref, kb_ref, vb_ref, knt_ref, kpet_ref, vt_ref, kbt_ref, vbt_ref,
             qbc_ref, kbc_ref, vbc_ref, lanes_sc):
        xb = x_ref[...].astype(BF16)

        def proj(lo, hi):
            return jnp.dot(xb, win_ref[:, lo:hi], preferred_element_type=F32)

        m_tabs = (tab_ref[0], tab_ref[1], tab_ref[2])
        d_tabs = (tab_ref[3], tab_ref[4], tab_ref[5])

        def use_cq(cq):
            cq_ref[...] = cq
            qn = (cq * lax.rsqrt(jnp.mean(cq * cq, axis=1, keepdims=True) + RMS_EPS) * qg_ref[...]).astype(BF16)
            qn_ref[...] = qn
            q = jnp.dot(qn, wuq_ref[...], preferred_element_type=F32)
            qcat_ref[:, :HW] = (q[:, :HW] * (MLA_SCALE * LOG2E)).astype(BF16)
            qcat_ref[:, HW:] = (
                _rope_wide(_rope, q[:, HW:], *m_tabs, MLA_ROPE // 2) * (MLA_SCALE * LOG2E)).astype(BF16)

        def use_ckv(ckv):
            ckv_ref[...] = ckv
            kvn = (ckv * lax.rsqrt(jnp.mean(ckv * ckv, axis=1, keepdims=True) + RMS_EPS) * kvg_ref[...]).astype(BF16)
            kvn_ref[...] = kvn
            kv = jnp.dot(kvn, wukv_ref[...], preferred_element_type=F32)
            kn_ref[...] = kv[:, :HW].astype(BF16)
            v_ref[...] = kv[:, HW:].astype(BF16)
            knt_ref[...] = kv[:, :HW].T.astype(BF16)
            vt_ref[...] = kv[:, HW:].T.astype(BF16)

        def use_kr(kr):
            kpe = _rope(kr, *m_tabs, MLA_ROPE // 2)
            kpe_ref[...] = kpe.astype(BF16)
            kpet_ref[...] = kpe.T[:MLA_ROPE, :].astype(BF16)

        def use_ga(ga):
            ga_ref[...] = ga

        def use_qb(qb):
            qb = _rope_wide(_rope, qb, *d_tabs, DIL_ROT // 2) * (DIL_SCALE * LOG2E)
            qb_ref[...] = qb.astype(BF16)
            _by_class(qb, qbc_ref, lanes_sc)

        def use_kb(kb):
            kb = _rope_wide(_rope, kb, *d_tabs, DIL_ROT // 2)
            kb_ref[...] = kb.astype(BF16)
            kbt_ref[...] = kb.T.astype(BF16)
            _by_class(kb, kbc_ref, lanes_sc)

        def use_vb(vb):
            vb_ref[...] = vb.astype(BF16)
            vbt_ref[...] = vb.T.astype(BF16)
            _by_class(vb, vbc_ref, lanes_sc)

        def use_gb(gb):
            gb_ref[...] = gb

        pieces = [(C_CQ, C_CKV, use_cq), (C_CKV, C_KR, use_ckv), (C_KR, C_GA, use_kr), (C_GA, C_QB, use_ga),
                  (C_QB, C_KB, use_qb), (C_KB, C_VB, use_kb), (C_VB, C_GB, use_vb), (C_GB, C_END, use_gb)]
        ahead = proj(*pieces[0][:2])
        for n, (_, _, use) in enumerate(pieces):
            cur = ahead
            if n + 1 < len(pieces):
                ahead = proj(*pieces[n + 1][:2])
            use(cur)

    def tok(width):
        return pl.BlockSpec((bt, width), lambda i: (i, 0))

    def tok_t(height):
        return pl.BlockSpec((height, bt), lambda i: (0, i))

    def full(a):
        return pl.BlockSpec(a.shape, lambda i: (0,) * a.ndim)

    outs = [(Q_RANK, F32), (KV_RANK, F32), (Q_RANK, BF16), (KV_RANK, BF16), (QW, BF16), (HW, BF16),
            (LANES, BF16), (HW, BF16), (HW, F32), (HW, F32), (HW, BF16), (HW, BF16), (HW, BF16)]
    outs_t = [HW, MLA_ROPE, HW, HW, HW]
    by_class = pl.BlockSpec((n_cls, bt // n_cls, HW), lambda i: (0, i, 0))
    return _pcall(
        body, name="fwd_proj", grid=(seq // bt,),
        in_specs=[tok(D_MODEL), full(w_in_r), full(w_uq_r), full(w_ukv_r), full(qg), full(kvg),
                  pl.BlockSpec((6, bt, LANES), lambda i: (0, i, 0))],
        out_specs=[tok(w) for w, _ in outs] + [tok_t(h) for h in outs_t] + [by_class] * 3,
        out_shape=[jax.ShapeDtypeStruct((seq, w), dt) for w, dt in outs]
        + [jax.ShapeDtypeStruct((h, seq), BF16) for h in outs_t]
        + [jax.ShapeDtypeStruct((n_cls, seq // n_cls, HW), BF16)] * 3,
        scratch_shapes=[pltpu.VMEM((HW // LANES, bt, LANES), F32)],
        compiler_params=_cparams(dimension_semantics=("arbitrary",)),
    )(x, w_in_r, w_uq_r, w_ukv_r, qg, kvg, tabs)


def _head_masks(lane, h):
    e, g = h % 2, h % 4
    me = (lane >= 64 * e) & (lane < 64 * e + 64)
    mr = (lane >= 32 * g) & (lane < 32 * g + 32)
    return me, mr


def _masked(mask, a):
    return jnp.where(mask, a, jnp.zeros_like(a))


def _pair_operands(q_ref, k_ref, kpe_ref, lane, j, ks=slice(None), qs=slice(None)):
    cols = slice(LANES * j, LANES * (j + 1))
    qc = q_ref[qs, cols]
    kj = k_ref[ks, cols]
    kes = []
    for h in (2 * j, 2 * j + 1):
        me, mr = _head_masks(lane, h)
        ke = _masked(me, kj)
        if kpe_ref is not None:
            ke = jnp.concatenate([ke, _masked(mr, kpe_ref[ks, :])], axis=1)
        kes.append(ke)
    if kpe_ref is not None:
        qc = jnp.concatenate([qc, q_ref[qs, HW + LANES * (j // 2):HW + LANES * (j // 2 + 1)]], axis=1)
    return qc, kes


def _tile_variants(bias_t):
    out = {}
    for i, tile in enumerate(np.asarray(bias_t)):
        h = tile.shape[0] // 2
        skip = 1 if (tile[h:, :h] == NEG).all() else 2 if (tile[:h, h:] == NEG).all() else 0
        out[i] = (bool((tile != 0).any()), skip)
    return out


def _tile_parts(blk, skip):
    lo, hi, full = slice(0, blk // 2), slice(blk // 2, blk), slice(0, blk)
    return {0: [(full, full)], 1: [(lo, full), (hi, hi)], 2: [(hi, full), (lo, lo)]}[skip]


class Rider(NamedTuple):
    args: list
    in_specs: list
    out_shape: list
    out_specs: list
    scratch: list
    start: Callable
    finish: Callable
    stages: tuple = ()


def _ride_along(body, ride, n_prefetch, n_in, n_out, n_scratch, n_steps):
    if ride is None:
        return body

    def wrapped(*refs):
        pre, rest = refs[:n_prefetch], refs[n_prefetch:]
        a = n_in
        b = a + len(ride.args)
        c = b + n_out
        d = c + len(ride.out_shape)
        e = d + n_scratch
        mine = (rest[a:b], rest[c:d], rest[e:])
        t = pl.program_id(0)
        pl.when(t == 0)(lambda: ride.start(*mine))
        for at, stage in ride.stages:
            pl.when(t == at)(functools.partial(stage, *mine))
        body(*pre, *rest[:a], *rest[b:c], *rest[d:e])
        pl.when(t == n_steps - 1)(lambda: ride.finish(*mine))

    return wrapped


def _attn_fwd(name, q, k, kpe, vt, bias_t, steps, blk, ride=None, v_token_major=False):
    seq = q.shape[0]
    mla = kpe is not None
    n_steps = int(steps[0].shape[0])
    variants = _tile_variants(bias_t)

    def body(qi_r, ki_r, bi_r, fi_r, la_r, *refs):
        if mla:
            q_ref, k_ref, kpe_ref, vt_ref, b_ref, o_ref, lse_ref, m_sc, l_sc, acc_sc, st_sc = refs
        else:
            q_ref, k_ref, vt_ref, b_ref, o_ref, lse_ref, m_sc, l_sc, acc_sc, st_sc = refs
        t = pl.program_id(0)

        @pl.when(fi_r[t] == 1)
        def _():
            m_sc[...] = jnp.full(m_sc.shape, NEG, F32)
            l_sc[...] = jnp.zeros(l_sc.shape, F32)
            acc_sc[...] = jnp.zeros(acc_sc.shape, F32)

        lane = lax.broadcasted_iota(jnp.int32, (1, LANES), 1)
        if v_token_major:
            vt_all = vt_ref[...].astype(F32).T.astype(BF16)
            vt_rows = lambda rows, ks: vt_all[rows, ks]
        else:
            vt_rows = lambda rows, ks: vt_ref[rows, ks]

        def tile_pass(ks, qs, with_bias):
            nk, nq = ks.stop - ks.start, qs.stop - qs.start
            ones = jnp.ones((16, nk), BF16)

            def pair_scores(j):
                qc, kes = _pair_operands(q_ref, k_ref, kpe_ref if mla else None, lane, j, ks, qs)
                st = lax.dot_general(jnp.concatenate(kes, axis=0), qc, NT, preferred_element_type=F32)
                maxes = []
                for e in range(2):
                    se = st[e * nk:(e + 1) * nk]
                    if with_bias:
                        se = se + b_ref[0, ks, qs]
                    st_sc[j % 2, e * nk:(e + 1) * nk, 0:nq] = se
                    maxes.append(jnp.max(se, axis=0, keepdims=True))
                return maxes

            def softmax_pv(h, col_max):
                st = st_sc[(h // 2) % 2, (h % 2) * nk:(h % 2 + 1) * nk, 0:nq]
                hrow = slice(h, h + 1)
                m_prev = m_sc[hrow, qs]
                m_new = jnp.maximum(m_prev, col_max)
                alpha = jnp.exp2(m_prev - m_new)
                pt = jnp.exp2(st - m_new).astype(BF16)
                m_sc[hrow, qs] = m_new
                rows = slice(64 * h, 64 * h + 64)
                res = jnp.dot(jnp.concatenate([vt_rows(rows, ks), ones], axis=0), pt, preferred_element_type=F32)
                acc_sc[rows, qs] = alpha * acc_sc[rows, qs] + res[:64]
                l_sc[hrow, qs] = alpha * l_sc[hrow, qs] + res[64:65]

            maxes = pair_scores(0)
            for j in range(HEADS // 2):
                cur = maxes
                if j + 1 < HEADS // 2:
                    maxes = pair_scores(j + 1)
                softmax_pv(2 * j, cur[0])
                softmax_pv(2 * j + 1, cur[1])

        def step(with_bias, skip):
            for ks, qs in _tile_parts(blk, skip):
                tile_pass(ks, qs, with_bias)

        for idx, (with_bias, skip) in variants.items():
            if len(variants) == 1:
                step(with_bias, skip)
            else:
                pl.when(bi_r[t] == idx)(functools.partial(step, with_bias, skip))

        @pl.when(la_r[t] == 1)
        def _():
            for h in range(HEADS):
                rows = slice(64 * h, 64 * h + 64)
                acc_sc[rows, :] = acc_sc[rows, :] / l_sc[h:h + 1, :]
            o_ref[...] = acc_sc[...].T
            lse_ref[...] = m_sc[...] + jnp.log2(l_sc[...])

    qmap = lambda t, qi, ki, bi, fi, la: (qi[t], 0)
    kmap = lambda t, qi, ki, bi, fi, la: (ki[t], 0)
    in_specs = [pl.BlockSpec((blk, q.shape[1]), qmap), pl.BlockSpec((blk, HW), kmap)]
    args = [q, k]
    if mla:
        in_specs.append(pl.BlockSpec((blk, LANES), kmap))
        args.append(kpe)
    in_specs += [pl.BlockSpec((blk, HW), kmap) if v_token_major else
                 pl.BlockSpec((HW, blk), lambda t, qi, ki, bi, fi, la: (0, ki[t])),
                 pl.BlockSpec((1, blk, blk), lambda t, qi, ki, bi, fi, la: (bi[t], 0, 0))]
    args += [vt, jnp.asarray(bias_t)]
    out_specs = [pl.BlockSpec((blk, HW), qmap), pl.BlockSpec((HEADS, blk), lambda t, qi, ki, bi, fi, la: (0, qi[t]))]
    out_shape = [jax.ShapeDtypeStruct((seq, HW), F32), jax.ShapeDtypeStruct((HEADS, seq), F32)]
    scratch = [pltpu.VMEM((HEADS, blk), F32), pltpu.VMEM((HEADS, blk), F32),
               pltpu.VMEM((HW, blk), F32), pltpu.VMEM((2, 2 * blk, blk), F32)]
    body = _ride_along(body, ride, 5, len(args), len(out_shape), len(scratch), n_steps)
    if ride is not None:
        args, in_specs = args + ride.args, in_specs + ride.in_specs
        out_specs, out_shape, scratch = out_specs + ride.out_specs, out_shape + ride.out_shape, scratch + ride.scratch
    return _pcall(
        body, name=name,
        grid_spec=pltpu.PrefetchScalarGridSpec(
            num_scalar_prefetch=5, grid=(n_steps,), in_specs=in_specs, out_specs=out_specs, scratch_shapes=scratch),
        out_shape=out_shape,
        compiler_params=_cparams(dimension_semantics=("arbitrary",)),
    )(*steps, *args)


def _attn_bwd(name, q, k, kpe, v, kt, kpet, bias_t, do, lse, dstat, steps, blk, ride=None, single_visit=False):
    assert not (single_visit and kpe is not None) and (kt is not None or single_visit)
    seq = q.shape[0]
    mla = kpe is not None
    qw = q.shape[1]
    n_steps = int(steps[0].shape[0])
    dk_dtype = BF16 if mla else F32
    variants = _tile_variants(bias_t)

    def body(qi_r, ki_r, bi_r, fi_r, la_r, *refs):
        if mla:
            (q_ref, k_ref, kpe_ref, v_ref, kt_ref, kpet_ref, b_ref, do_ref, lse_ref, d_ref,
             dq_ref, dk_ref, dkpe_ref, dv_ref, dk_sc, dkpe_sc, dv_sc, st_sc, dpt_sc) = refs
        else:
            q_ref, k_ref, v_ref, *rest = refs
            kt_ref = rest.pop(0) if kt is not None else None
            b_ref, do_ref, lse_ref, d_ref, dq_out_ref, dk_ref, dv_ref, dk_sc, dv_sc, st_sc, dpt_sc, *rest = rest
            dq_ref = rest[0] if single_visit else dq_out_ref
        t = pl.program_id(0)

        @pl.when(jnp.logical_or(t == 0, single_visit))
        def _():
            dq_ref[...] = jnp.zeros(dq_ref.shape, F32)

        @pl.when(fi_r[t] == 1)
        def _():
            dk_sc[...] = jnp.zeros(dk_sc.shape, F32)
            dv_sc[...] = jnp.zeros(dv_sc.shape, F32)
            if mla:
                dkpe_sc[...] = jnp.zeros(dkpe_sc.shape, F32)

        qi = 0 if single_visit else qi_r[t]
        lane = lax.broadcasted_iota(jnp.int32, (1, LANES), 1)
        if kt is None:
            kt_all = k_ref[...].astype(F32).T.astype(BF16)
            kt_rows = lambda rows, ks: kt_all[rows, ks]
        else:
            kt_rows = lambda rows, ks: kt_ref[rows, ks]

        def tile_pass(ks, qs, with_bias):
            nk, nq = ks.stop - ks.start, qs.stop - qs.start

            def pair_matmuls(j):
                cols = slice(LANES * j, LANES * (j + 1))
                qc, kes = _pair_operands(q_ref, k_ref, kpe_ref if mla else None, lane, j, ks, qs)
                st_sc[j % 2, 0:2 * nk, 0:nq] = lax.dot_general(
                    jnp.concatenate(kes, axis=0), qc, NT, preferred_element_type=F32)
                vj = v_ref[ks, cols]
                ves = [_masked(_head_masks(lane, h)[0], vj) for h in (2 * j, 2 * j + 1)]
                dpt_sc[j % 2, 0:2 * nk, 0:nq] = lax.dot_general(
                    jnp.concatenate(ves, axis=0), do_ref[qs, cols], NT, preferred_element_type=F32)

            def pair_grads(j):
                cols = slice(LANES * j, LANES * (j + 1))
                qj, doj = q_ref[qs, cols], do_ref[qs, cols]
                if mla:
                    qr = q_ref[qs, HW + LANES * (j // 2):HW + LANES * (j // 2 + 1)]
                pts, dsts, qms, doms = [], [], [], []
                for e in range(2):
                    h = 2 * j + e
                    me, mr = _head_masks(lane, h)
                    st = st_sc[j % 2, e * nk:(e + 1) * nk, 0:nq]
                    if with_bias:
                        st = st + b_ref[0, ks, qs]
                    pt = jnp.exp2(st - lse_ref[h:h + 1, qs])
                    dst = (pt * (dpt_sc[j % 2, e * nk:(e + 1) * nk, 0:nq] - d_ref[h:h + 1, qs])).astype(BF16)
                    pts.append(pt.astype(BF16))
                    dsts.append(dst)
                    doms.append(_masked(me, doj))
                    qm = _masked(me, qj)
                    if mla:
                        qm = jnp.concatenate([qm, _masked(mr, qr)], axis=1)
                    qms.append(qm)
                    ktl = kt_rows(slice(64 * h, 64 * h + 64), ks)
                    if mla:
                        ktl = jnp.concatenate([ktl, kpet_ref[:, ks]], axis=0)
                    dqc = jnp.dot(ktl, dst, preferred_element_type=F32)
                    dq_ref[qi, 64 * h:64 * h + 64, qs] += dqc[:64]
                    if mla:
                        dq_ref[qi, HW + MLA_ROPE * h:HW + MLA_ROPE * (h + 1), qs] += dqc[64:]
                dv_sc[ks, cols] += jnp.dot(
                    jnp.concatenate(pts, axis=1), jnp.concatenate(doms, axis=0), preferred_element_type=F32)
                dkc = jnp.dot(jnp.concatenate(dsts, axis=1), jnp.concatenate(qms, axis=0), preferred_element_type=F32)
                dk_sc[ks, cols] += dkc[:, :LANES]
                if mla:
                    dkpe_sc[ks, :] += dkc[:, LANES:]

            pair_matmuls(0)
            for j in range(HEADS // 2):
                if j + 1 < HEADS // 2:
                    pair_matmuls(j + 1)
                pair_grads(j)

        def step(with_bias, skip):
            for ks, qs in _tile_parts(blk, skip):
                tile_pass(ks, qs, with_bias)

        for idx, (with_bias, skip) in variants.items():
            if len(variants) == 1:
                step(with_bias, skip)
            else:
                pl.when(bi_r[t] == idx)(functools.partial(step, with_bias, skip))
        if single_visit:
            dq_out_ref[...] = dq_ref[0].T

        @pl.when(la_r[t] == 1)
        def _():
            dk_ref[...] = (dk_sc[...] * LN2).astype(dk_ref.dtype)
            dv_ref[...] = dv_sc[...].astype(dv_ref.dtype)
            if mla:
                dkpe_ref[...] = dkpe_sc[...] * LN2

    qmap = lambda t, qi, ki, bi, fi, la: (qi[t], 0)
    kmap = lambda t, qi, ki, bi, fi, la: (ki[t], 0)
    qmap_t = lambda t, qi, ki, bi, fi, la: (0, qi[t])
    kmap_t = lambda t, qi, ki, bi, fi, la: (0, ki[t])
    in_specs = [pl.BlockSpec((blk, qw), qmap), pl.BlockSpec((blk, HW), kmap)]
    args = [q, k]
    if mla:
        in_specs.append(pl.BlockSpec((blk, LANES), kmap))
        args.append(kpe)
    in_specs.append(pl.BlockSpec((blk, HW), kmap))
    args.append(v)
    if kt is not None:
        in_specs.append(pl.BlockSpec((HW, blk), kmap_t))
        args.append(kt)
    if mla:
        in_specs.append(pl.BlockSpec((MLA_ROPE, blk), kmap_t))
        args.append(kpet)
    in_specs += [pl.BlockSpec((1, blk, blk), lambda t, qi, ki, bi, fi, la: (bi[t], 0, 0)),
                 pl.BlockSpec((blk, HW), qmap), pl.BlockSpec((HEADS, blk), qmap_t), pl.BlockSpec((HEADS, blk), qmap_t)]
    args += [jnp.asarray(bias_t), do, lse, dstat]
    dq_shape = (seq // blk, qw, blk)
    if single_visit:
        out_specs, out_shape = [pl.BlockSpec((blk, qw), qmap)], [jax.ShapeDtypeStruct((seq, qw), F32)]
    else:
        out_specs = [pl.BlockSpec(dq_shape, lambda t, qi, ki, bi, fi, la: (0, 0, 0))]
        out_shape = [jax.ShapeDtypeStruct(dq_shape, F32)]
    out_specs.append(pl.BlockSpec((blk, HW), kmap))
    out_shape.append(jax.ShapeDtypeStruct((seq, HW), dk_dtype))
    scratch = [pltpu.VMEM((blk, HW), F32)]
    if mla:
        out_specs.append(pl.BlockSpec((blk, LANES), kmap))
        out_shape.append(jax.ShapeDtypeStruct((seq, LANES), F32))
        scratch.append(pltpu.VMEM((blk, LANES), F32))
    out_specs.append(pl.BlockSpec((blk, HW), kmap))
    out_shape.append(jax.ShapeDtypeStruct((seq, HW), BF16))
    scratch.append(pltpu.VMEM((blk, HW), F32))
    scratch += [pltpu.VMEM((2, 2 * blk, blk), F32), pltpu.VMEM((2, 2 * blk, blk), F32)]
    if single_visit:
        scratch.append(pltpu.VMEM((1, qw, blk), F32))
    body = _ride_along(body, ride, 5, len(args), len(out_shape), len(scratch), n_steps)
    if ride is not None:
        args, in_specs = args + ride.args, in_specs + ride.in_specs
        out_specs, out_shape, scratch = out_specs + ride.out_specs, out_shape + ride.out_shape, scratch + ride.scratch
    return _pcall(
        body, name=name,
        grid_spec=pltpu.PrefetchScalarGridSpec(
            num_scalar_prefetch=5, grid=(n_steps,), in_specs=in_specs, out_specs=out_specs,
            scratch_shapes=scratch),
        out_shape=out_shape,
        compiler_params=_cparams(dimension_semantics=("arbitrary",)),
    )(*steps, *args)


def _out_ln(oa, ob_near, ob_far, lse_near, lse_far, ga, gb, x, tgt, w_out, ln_g, ln_b, bt):
    seq = x.shape[0]

    def body(oa_ref, obn_ref, obf_ref, lsen_ref, lsef_ref, ga_ref, gb_ref, x_ref, tgt_ref, w_ref, g_ref, b_ref,
             dz_ref, doa_ref, dob_ref, dga_ref, dgb_ref, da_ref, db_ref, lse_ref, gwb_ref, small_ref, dobc_ref,
             gw_ref, lanes_sc):
        i = pl.program_id(0)

        @pl.when(i == 0)
        def _():
            gw_ref[...] = jnp.zeros(gw_ref.shape, F32)
            small_ref[...] = jnp.zeros(small_ref.shape, F32)

        def gate(g):
            sig = 0.5 * jnp.tanh(0.5 * g) + 0.5
            return g * sig, sig * (1.0 + g * (1.0 - sig))

        lse_n, lse_f = lsen_ref[...], lsef_ref[...]
        top = jnp.maximum(lse_n, lse_f)
        e_n, e_f = jnp.exp2(lse_n - top), jnp.exp2(lse_f - top)
        lse_ref[...] = top + jnp.log2(e_n + e_f)
        inv = 1.0 / (e_n + e_f)
        head_row = lax.broadcasted_iota(jnp.int32, (2 * HEADS, HW), 0) % HEADS
        spread = (head_row == lax.broadcasted_iota(jnp.int32, (2 * HEADS, HW), 1) // 64).astype(BF16)

        def per_lane(w):
            hi = w.astype(BF16)
            lo = (w - hi.astype(F32)).astype(BF16)
            return lax.dot_general(jnp.concatenate([hi, lo], axis=0), spread, TN, preferred_element_type=F32)

        o_b_all = per_lane(e_n * inv) * obn_ref[...] + per_lane(e_f * inv) * _in_sequence(obf_ref, lanes_sc)
        gam = g_ref[...]
        halves = [slice(0, bt // 2), slice(bt // 2, bt)]

        def gates_and_projection(rows):
            o_a, o_b = oa_ref[rows, :], o_b_all[rows]
            sa, dsa = gate(ga_ref[rows, :])
            sb, dsb = gate(gb_ref[rows, :])
            mix = jnp.concatenate([o_a * sa, o_b * sb], axis=1).astype(BF16)
            z = ALPHA * x_ref[rows, :] + jnp.dot(mix, w_ref[...], preferred_element_type=F32)
            return o_a, o_b, sa, dsa, sb, dsb, mix, z

        def norm_and_back(rows, mix, z):
            mu = jnp.mean(z, axis=1, keepdims=True)
            zc = z - mu
            rstd = lax.rsqrt(jnp.mean(zc * zc, axis=1, keepdims=True) + LN_EPS)
            xhat = zc * rstd
            diff = xhat * gam + b_ref[...] - tgt_ref[rows, :]
            dy = diff * (1.0 / D_MODEL)
            small_ref[0:1, :] += jnp.sum(dy * xhat, axis=0, keepdims=True)
            small_ref[1:2, :] += jnp.sum(dy, axis=0, keepdims=True)
            small_ref[2:3, :] += jnp.sum(diff * diff, axis=0, keepdims=True)
            dxh = dy * gam
            dz = rstd * (dxh - jnp.mean(dxh, axis=1, keepdims=True)
                         - xhat * jnp.mean(dxh * xhat, axis=1, keepdims=True))
            dz_ref[rows, :] = dz
            dzb = dz.astype(BF16)
            gw_ref[...] += lax.dot_general(mix, dzb, TN, preferred_element_type=F32)
            return lax.dot_general(dzb, w_ref[...], NT, preferred_element_type=F32)

        def gate_back(rows, o_a, o_b, sa, dsa, sb, dsb, dmix):
            doa, dob = dmix[:, :HW] * sa, dmix[:, HW:] * sb
            doa_ref[rows, :] = doa.astype(BF16)
            dob_ref[rows, :] = dob.astype(BF16)
            dga_ref[rows, :] = (dmix[:, :HW] * o_a * dsa).astype(BF16)
            dgb_ref[rows, :] = (dmix[:, HW:] * o_b * dsb).astype(BF16)
            return dob, doa * o_a, dob * o_b

        fronts = [gates_and_projection(rows) for rows in halves]
        dmixes = [norm_and_back(rows, f[6], f[7]) for rows, f in zip(halves, fronts)]
        backs = [gate_back(rows, *f[:6], dmix) for rows, f, dmix in zip(halves, fronts, dmixes)]
        dob, prod_a, prod_b = (jnp.concatenate(parts, axis=0) for parts in zip(*backs))
        _by_class(dob, dobc_ref, lanes_sc)

        @pl.when(i == seq // bt - 1)
        def _():
            gwb_ref[...] = gw_ref[...].astype(BF16)

        head_of = (lax.broadcasted_iota(jnp.int32, (2 * HW, LANES), 0) % HW) // 64
        ind = (head_of == lax.broadcasted_iota(jnp.int32, (2 * HW, LANES), 1)).astype(BF16)

        def head_sums(prod):
            hi = prod.astype(BF16)
            lo = (prod - hi.astype(F32)).astype(BF16)
            sums = jnp.dot(jnp.concatenate([hi, lo], axis=1), ind, preferred_element_type=F32)
            return sums.T[:HEADS, :]

        da_ref[...] = head_sums(prod_a)
        db_ref[...] = head_sums(prod_b)

    def tok(width):
        return pl.BlockSpec((bt, width), lambda i: (i, 0))

    def full(shape):
        return pl.BlockSpec(shape, lambda i: (0,) * len(shape))

    stat = pl.BlockSpec((HEADS, bt), lambda i: (0, i))
    n_cls = ob_far.shape[0]
    by_class = pl.BlockSpec((n_cls, bt // n_cls, HW), lambda i: (0, i, 0))
    return _pcall(
        body, name="out_ln", grid=(seq // bt,),
        in_specs=[tok(HW), tok(HW), by_class, stat, stat, tok(HW), tok(HW), tok(D_MODEL), tok(D_MODEL),
                  full((D_MODEL, D_MODEL)), full((1, D_MODEL)), full((1, D_MODEL))],
        out_specs=[tok(D_MODEL), tok(HW), tok(HW), tok(HW), tok(HW), stat, stat, stat,
                   full((D_MODEL, D_MODEL)), full((8, D_MODEL)), by_class],
        out_shape=[jax.ShapeDtypeStruct((seq, D_MODEL), F32)] + [jax.ShapeDtypeStruct((seq, HW), BF16)] * 4
        + [jax.ShapeDtypeStruct((HEADS, seq), F32)] * 3
        + [jax.ShapeDtypeStruct((D_MODEL, D_MODEL), BF16), jax.ShapeDtypeStruct((8, D_MODEL), F32),
           jax.ShapeDtypeStruct(ob_far.shape, BF16)],
        scratch_shapes=[pltpu.VMEM((D_MODEL, D_MODEL), F32), pltpu.VMEM((HW // LANES, bt, LANES), F32)],
        compiler_params=_cparams(dimension_semantics=("arbitrary",)),
    )(oa, ob_near, ob_far, lse_near, lse_far, ga, gb, x, tgt, w_out, ln_g, ln_b)


def _bwd_mid(dq_m, dkn, dv, dkpe, dqb, dkb, dvb, far, dga, dgb, cq, ckv, qn, kvn, w_uq_r, w_ukv_r, qg, kvg, tabs, bt):
    n_cls = far[0].shape[0]
    seq = cq.shape[0]

    def body(dqm_ref, dkn_ref, dv_ref, dkpe_ref, dqb_ref, dkb_ref, dvb_ref, dqf_ref, dkf_ref, dvf_ref, dga_ref, dgb_ref,
             cq_ref, ckv_ref, qn_ref, kvn_ref, wuq_ref, wukv_ref, qg_ref, kvg_ref, tab_ref,
             dh_ref, guq3_ref, gukv3_ref, small_ref, seq_sc, guq_ref, gukv_ref):
        i = pl.program_id(0)

        @pl.when(i == 0)
        def _():
            guq_ref[...] = jnp.zeros(guq_ref.shape, F32)
            gukv_ref[...] = jnp.zeros(gukv_ref.shape, F32)
            small_ref[...] = jnp.zeros(small_ref.shape, F32)

        m_tabs = (tab_ref[0], tab_ref[1], tab_ref[2])
        d_tabs = (tab_ref[3], tab_ref[4], tab_ref[5])

        def rms_bwd(c, dn, gain):
            r = lax.rsqrt(jnp.mean(c * c, axis=1, keepdims=True) + RMS_EPS)
            u = dn * gain
            dc = r * u - c * (r * r * r) * jnp.mean(u * c, axis=1, keepdims=True)
            return dc, jnp.sum(dn * c * r, axis=0, keepdims=True)

        dqm = dqm_ref[0].T
        dq = jnp.concatenate(
            [dqm[:, :HW], _rope_wide(_rope_t, dqm[:, HW:], *m_tabs, MLA_ROPE // 2)], axis=1) * MLA_SCALE
        dq = dq.astype(BF16)
        dkv = jnp.concatenate([dkn_ref[...], dv_ref[...]], axis=1)
        guq_ref[...] += lax.dot_general(qn_ref[...], dq, TN, preferred_element_type=F32)
        dqn = lax.dot_general(dq, wuq_ref[...], NT, preferred_element_type=F32)
        gukv_ref[...] += lax.dot_general(kvn_ref[...], dkv, TN, preferred_element_type=F32)
        dkvn = lax.dot_general(dkv, wukv_ref[...], NT, preferred_element_type=F32)

        dh_ref[:, C_KR:C_GA] = _rope_t(dkpe_ref[...], *m_tabs, MLA_ROPE // 2).astype(BF16)
        dh_ref[:, C_GA:C_QB] = dga_ref[...]
        in_sequence = functools.partial(_in_sequence, lanes_sc=seq_sc)
        dqb = dqb_ref[0].T + in_sequence(dqf_ref)
        dh_ref[:, C_QB:C_KB] = (_rope_wide(_rope_t, dqb, *d_tabs, DIL_ROT // 2) * DIL_SCALE).astype(BF16)
        dkb = dkb_ref[...] + in_sequence(dkf_ref)
        dh_ref[:, C_KB:C_VB] = _rope_wide(_rope_t, dkb, *d_tabs, DIL_ROT // 2).astype(BF16)
        dh_ref[:, C_VB:C_GB] = (dvb_ref[...].astype(F32) + in_sequence(dvf_ref)).astype(BF16)
        dh_ref[:, C_GB:C_END] = dgb_ref[...]

        dcq, gq = rms_bwd(cq_ref[...], dqn, qg_ref[...])
        small_ref[0:1, :] += gq
        dckv, gkv = rms_bwd(ckv_ref[...], dkvn, kvg_ref[...])
        small_ref[1:2, :KV_RANK] += gkv
        dh_ref[:, C_CQ:C_CKV] = dcq.astype(BF16)
        dh_ref[:, C_CKV:C_KR] = dckv.astype(BF16)

        @pl.when(i == seq // bt - 1)
        def _():
            for h in range(HEADS):
                guq3_ref[h] = jnp.concatenate(
                    [guq_ref[:, MLA_NOPE * h:MLA_NOPE * (h + 1)],
                     guq_ref[:, HW + MLA_ROPE * h:HW + MLA_ROPE * (h + 1)]], axis=1).astype(BF16)
                gukv3_ref[h] = jnp.concatenate(
                    [gukv_ref[:, MLA_NOPE * h:MLA_NOPE * (h + 1)],
                     gukv_ref[:, HW + MLA_V * h:HW + MLA_V * (h + 1)]], axis=1).astype(BF16)

    def tok(width):
        return pl.BlockSpec((bt, width), lambda i: (i, 0))

    def tok_t(a):
        per = a.shape[2] // bt
        return pl.BlockSpec((1, a.shape[1], bt), lambda i: (i // per, 0, i % per))

    def full(shape):
        return pl.BlockSpec(shape, lambda i: (0,) * len(shape))

    by_class = pl.BlockSpec((n_cls, bt // n_cls, HW), lambda i: (0, i, 0))
    uq3 = (HEADS, Q_RANK, MLA_NOPE + MLA_ROPE)
    ukv3 = (HEADS, KV_RANK, MLA_NOPE + MLA_V)
    return _pcall(
        body, name="bwd_mid", grid=(seq // bt,),
        in_specs=[tok_t(dq_m), tok(HW), tok(HW), tok(LANES), tok_t(dqb), tok(HW), tok(HW), by_class, by_class, by_class,
                  tok(HW), tok(HW),
                  tok(Q_RANK), tok(KV_RANK), tok(Q_RANK), tok(KV_RANK),
                  full(w_uq_r.shape), full(w_ukv_r.shape), full((1, Q_RANK)), full((1, KV_RANK)),
                  pl.BlockSpec((6, bt, LANES), lambda i: (0, i, 0))],
        out_specs=[tok(C_END), full(uq3), full(ukv3), full((8, Q_RANK))],
        out_shape=[jax.ShapeDtypeStruct((seq, C_END), BF16), jax.ShapeDtypeStruct(uq3, BF16),
                   jax.ShapeDtypeStruct(ukv3, BF16), jax.ShapeDtypeStruct((8, Q_RANK), F32)],
        scratch_shapes=[pltpu.VMEM((HW // LANES, bt, LANES), F32), pltpu.VMEM(w_uq_r.shape, F32),
                        pltpu.VMEM(w_ukv_r.shape, F32)],
        compiler_params=_cparams(dimension_semantics=("arbitrary",)),
    )(dq_m, dkn, dv, dkpe, dqb, dkb, dvb, *far, dga, dgb, cq, ckv, qn, kvn, w_uq_r, w_ukv_r, qg, kvg, tabs)


def _grad_x(dz, dh, w_in_r, bt, ride=None):
    seq = dz.shape[0]
    n_steps = seq // bt

    def body(dz_ref, dh_ref, w_ref, gx_ref):
        gx_ref[...] = ALPHA * dz_ref[...] + lax.dot_general(
            dh_ref[...], w_ref[...], NT, preferred_element_type=F32)

    args = [dz, dh, w_in_r]
    in_specs = [pl.BlockSpec((bt, D_MODEL), lambda i: (i, 0)), pl.BlockSpec((bt, C_END), lambda i: (i, 0)),
                pl.BlockSpec(w_in_r.shape, lambda i: (0, 0))]
    out_specs = [pl.BlockSpec((bt, D_MODEL), lambda i: (i, 0))]
    out_shape = [jax.ShapeDtypeStruct((seq, D_MODEL), F32)]
    scratch = []
    body = _ride_along(body, ride, 0, len(args), len(out_shape), 0, n_steps)
    if ride is not None:
        args, in_specs = args + ride.args, in_specs + ride.in_specs
        out_specs, out_shape, scratch = out_specs + ride.out_specs, out_shape + ride.out_shape, ride.scratch
    return _pcall(
        body, name="grad_x", grid=(n_steps,),
        in_specs=in_specs, out_specs=out_specs, out_shape=out_shape, scratch_shapes=scratch,
        compiler_params=_cparams(dimension_semantics=("arbitrary",)),
    )(*args)


def _grad_w_in(name, x, dh, bt, rows, ride=None):
    seq = x.shape[0]
    n_steps = seq // bt
    n_rows = rows[1] - rows[0]
    shard = IN_WIDTH // N_DEV
    k_lo, k_hi = IN_SPLITS[0] + IN_SPLITS[1], IN_SPLITS[0] + IN_SPLITS[1] + MLA_ROPE

    def body(x_ref, dh_ref, out_ref, acc):
        i = pl.program_id(0)

        @pl.when(i == 0)
        def _():
            acc[...] = jnp.zeros(acc.shape, F32)

        acc[...] += lax.dot_general(
            x_ref[:, rows[0]:rows[1]].astype(BF16), dh_ref[...], TN, preferred_element_type=F32)

        @pl.when(i == n_steps - 1)
        def _():
            kr = acc[:, C_KR:C_GA]
            kr = kr + pltpu.roll(kr, 96, 1) + pltpu.roll(kr, 64, 1) + pltpu.roll(kr, 32, 1)
            for d in range(N_DEV):
                lo, hi = shard * d, shard * (d + 1)
                pieces = []
                if lo < k_lo:
                    pieces.append(acc[:, lo:min(hi, k_lo)])
                if lo < k_hi and hi > k_lo:
                    pieces.append(kr[:, max(lo, k_lo) - k_lo:min(hi, k_hi) - k_lo])
                if hi > k_hi:
                    shift = C_GA - k_hi
                    pieces.append(acc[:, max(lo, k_hi) + shift:hi + shift])
                blk = pieces[0] if len(pieces) == 1 else jnp.concatenate(pieces, axis=1)
                out_ref[d] = blk.astype(BF16)

    args = [x, dh]
    in_specs = [pl.BlockSpec((bt, D_MODEL), lambda i: (i, 0)), pl.BlockSpec((bt, C_END), lambda i: (i, 0))]
    out_specs = [pl.BlockSpec((N_DEV, n_rows, shard), lambda i: (0, 0, 0))]
    out_shape = [jax.ShapeDtypeStruct((N_DEV, n_rows, shard), BF16)]
    scratch = [pltpu.VMEM((n_rows, C_END), F32)]
    body = _ride_along(body, ride, 0, len(args), len(out_shape), len(scratch), n_steps)
    if ride is not None:
        args, in_specs = args + ride.args, in_specs + ride.in_specs
        out_specs, out_shape, scratch = out_specs + ride.out_specs, out_shape + ride.out_shape, scratch + ride.scratch
    return _pcall(
        body, name=name, grid=(n_steps,),
        in_specs=in_specs, out_specs=out_specs, out_shape=out_shape, scratch_shapes=scratch,
        compiler_params=_cparams(dimension_semantics=("arbitrary",)),
    )(*args)


def _local_step(x, tgt, w_in_r, w_uq_r, w_ukv_r, w_out_rider, g_out_rider, reduce_rider, q_norm_g, kv_norm_g,
                ln_g, ln_b, bt=BLOCK_TOKENS, blk_m=BLOCK_MLA, blk_d=BLOCK_DIL):
    seq = x.shape[0]
    tabs = jnp.asarray(_rope_tables(seq))
    qg, kvg = q_norm_g.reshape(1, -1), kv_norm_g.reshape(1, -1)

    far_dil = DIL_CONFIGS[-1][1]
    cls = seq // far_dil
    (cq, ckv, qn, kvn, qcat, kn, kpe, v, ga, gb, qb, kb, vb, knt, kpet, vt, kbt, vbt, qb_c, kb_c, vb_c) = _fwd_proj(
        x, w_in_r, w_uq_r, w_ukv_r, qg, kvg, tabs, bt, far_dil)
    qb_c, kb_c, vb_c = (a.reshape(seq, HW) for a in (qb_c, kb_c, vb_c))

    nq_m, nq_d = seq // blk_m, seq // blk_d
    bias_m = _mla_bias_t(blk_m)
    oa, lse_a, w_out = _attn_fwd(
        "mla_fwd", qcat, kn, kpe, vt, bias_m, _steps(nq_m, nq_m, False, True), blk_m, ride=w_out_rider)

    bias_near = _dil_bias_t(blk_d, DIL_NEAR)
    ob_near, lse_near = _attn_fwd(
        "dil_fwd", qb, kb, None, vbt, bias_near, _steps(nq_d, -(-DIL_NEAR // blk_d), False, False), blk_d)
    each = jnp.arange(far_dil, dtype=jnp.int32)
    steps_far = [each, each, jnp.zeros_like(each), jnp.ones_like(each), jnp.ones_like(each)]
    bias_far = _dil_far_bias_t(cls)
    ob_far, lse_far = _attn_fwd(
        "dil_far_fwd", qb_c, kb_c, None, vb_c, bias_far, steps_far, cls, v_token_major=True)

    dz, doa, dob, dga, dgb, dst_a, dst_b, lse_b, g_out, small1, dob_c = _out_ln(
        oa, ob_near, ob_far.reshape(far_dil, cls, HW), lse_near, _lanes_from_classes(lse_far, far_dil), ga, gb, x, tgt,
        w_out.reshape(D_MODEL, D_MODEL), ln_g.reshape(1, -1), ln_b.reshape(1, -1), bt)

    dq_m, dkn, dkpe, dv, g_out_recv = _attn_bwd(
        "mla_bwd", qcat, kn, kpe, v, knt, kpet, bias_m, doa, lse_a, dst_a, _steps(nq_m, nq_m, True, True), blk_m,
        ride=g_out_rider(g_out.reshape(N_DEV, D_MODEL // N_DEV, D_MODEL)))
    dqb, dkb_near, dvb_near = _attn_bwd(
        "dil_bwd", qb, kb, None, vb, kbt, None, bias_near, dob, lse_b, dst_b,
        _steps(nq_d, -(-DIL_NEAR // blk_d), True, False), blk_d)
    dqb_far, dkb_far, dvb_far = _attn_bwd(
        "dil_far_bwd", qb_c, kb_c, None, vb_c, None, None, bias_far, dob_c.reshape(seq, HW),
        _lanes_to_classes(lse_b, far_dil), _lanes_to_classes(dst_b, far_dil), steps_far, cls, single_visit=True)
    far = [a.reshape(far_dil, cls, HW) for a in (dqb_far, dkb_far, dvb_far)]

    dh, g_uq, g_ukv, small2 = _bwd_mid(
        dq_m, dkn, dv, dkpe, dqb, dkb_near, dvb_near, far, dga, dgb, cq, ckv, qn, kvn, w_uq_r, w_ukv_r, qg, kvg, tabs, bt)
    small_part = _small_rows(small1[0], small1[1], small2[0, :Q_RANK], small2[1, :KV_RANK], small1[2])
    steps = (1, seq // bt - 3)
    (g_in_first,) = _grad_w_in("grad_w_in_first", x, dh, min(seq, 2 * bt), (0, GRAD_W_IN_CUT))
    g_in_rest, *reduced_first = _grad_w_in(
        "grad_w_in_rest", x, dh, bt, (GRAD_W_IN_CUT, D_MODEL),
        ride=reduce_rider([g_in_first, g_uq, g_ukv], g_out_recv, small_part, steps))
    grad_x, *reduced_rest = _grad_x(dz, dh, w_in_r, bt, ride=reduce_rider([g_in_rest], None, None, steps))
    return grad_x, (reduced_first, reduced_rest)


MESH_ID = pl.DeviceIdType.MESH
SHARD_SHAPES = ((D_MODEL, IN_WIDTH // N_DEV), (Q_RANK, 768 // N_DEV), (KV_RANK, 1024 // N_DEV), (D_MODEL // N_DEV, D_MODEL))
ADAM_ROWS = (32, 128, 128, 16)


def _me():
    x, y, c = lax.axis_index("x"), lax.axis_index("y"), lax.axis_index("c")
    return x, y, c, 4 * x + 2 * y + c


def _peer(k):
    x, y, c, _ = _me()
    px = 1 - x if (k >> 2) & 1 else x
    py = 1 - y if (k >> 1) & 1 else y
    pc = 1 - c if k & 1 else c
    return (px, py, pc), 4 * px + 2 * py + pc


def _all_gather_weights(shards):
    n = len(shards)
    shard = IN_WIDTH // N_DEV
    k_lo = IN_SPLITS[0] + IN_SPLITS[1]
    k_hi = k_lo + MLA_ROPE

    def body(*refs):
        ins = refs[:n]
        win_ref, wuq_ref, wukv_ref = refs[n:2 * n]
        bufs = refs[2 * n:3 * n]
        send_sems, recv_sems = refs[3 * n:]
        x, y, c, me = _me()
        here, sibling = (x, y, c), (x, y, 1 - c)
        along_x, along_y, across = (1 - x, y), (x, 1 - y), (1 - x, 1 - y)
        for t in range(n):
            bufs[t][me] = ins[t][...].astype(BF16)

        def copy(t, k, chip, pc, to, half=None):
            blk = bufs[t].at[4 * chip[0] + 2 * chip[1] + pc]
            if half is not None:
                rows = SHARD_SHAPES[t][0] // 2
                blk = blk.at[pl.ds(half * rows, rows), :]
            return pltpu.make_async_remote_copy(
                src_ref=blk, dst_ref=blk, send_sem=send_sems.at[t, k], recv_sem=recv_sems.at[t, k],
                device_id=to, device_id_type=MESH_ID)

        sends = []
        for t in range(n):
            sends += [copy(t, 0, (x, y), c, sibling), copy(t, 1, (x, y), c, (*along_x, c)),
                      copy(t, 2, (x, y), c, (*along_y, c))]
        for cp in sends:
            cp.start()
        for t in range(n):
            copy(t, 1, along_x, c, here).wait_recv()
            sends += [copy(t, 3, along_x, c, (*along_y, c), half=0), copy(t, 5, along_x, c, sibling)]
            sends[-2].start()
            sends[-1].start()
        for t in range(n):
            copy(t, 2, along_y, c, here).wait_recv()
            sends += [copy(t, 4, along_y, c, (*along_x, c), half=1), copy(t, 6, along_y, c, sibling)]
            sends[-2].start()
            sends[-1].start()
        for t in range(n):
            copy(t, 3, across, c, here, half=0).wait_recv()
            copy(t, 4, across, c, here, half=1).wait_recv()
            sends.append(copy(t, 7, across, c, sibling))
            sends[-1].start()
        for t in range(n):
            copy(t, 0, (x, y), 1 - c, here).wait_recv()
            for k, chip in ((5, along_x), (6, along_y), (7, across)):
                copy(t, k, chip, 1 - c, here).wait_recv()
        for cp in sends:
            cp.wait_send()

        a_in, a_uq, a_ukv = bufs
        for d in range(N_DEV):
            lo, hi = shard * d, shard * (d + 1)
            if lo < k_lo:
                win_ref[:, lo:min(hi, k_lo)] = a_in[d, :, 0:min(hi, k_lo) - lo]
            if lo < k_hi and hi > k_lo:
                kr = a_in[d, :, k_lo - lo:k_hi - lo]
                for rep in range(4):
                    win_ref[:, C_KR + MLA_ROPE * rep:C_KR + MLA_ROPE * (rep + 1)] = kr
            if hi > k_hi:
                src = max(lo, k_hi)
                win_ref[:, src + C_GA - k_hi:hi + C_GA - k_hi] = a_in[d, :, src - lo:hi - lo]
        for h in range(HEADS):
            wuq_ref[:, MLA_NOPE * h:MLA_NOPE * (h + 1)] = a_uq[h, :, :MLA_NOPE]
            wuq_ref[:, HW + MLA_ROPE * h:HW + MLA_ROPE * (h + 1)] = a_uq[h, :, MLA_NOPE:]
            wukv_ref[:, MLA_NOPE * h:MLA_NOPE * (h + 1)] = a_ukv[h, :, :MLA_NOPE]
            wukv_ref[:, HW + MLA_V * h:HW + MLA_V * (h + 1)] = a_ukv[h, :, MLA_NOPE:]

    vmem = pl.BlockSpec(memory_space=pltpu.VMEM)
    return _pcall(
        body, name="gather_weights",
        in_specs=[vmem] * n, out_specs=[vmem] * n,
        out_shape=[jax.ShapeDtypeStruct((D_MODEL, C_END), BF16), jax.ShapeDtypeStruct((Q_RANK, QW), BF16),
                   jax.ShapeDtypeStruct((KV_RANK, 2 * HW), BF16)],
        scratch_shapes=[pltpu.VMEM((N_DEV,) + s, BF16) for s in SHARD_SHAPES[:n]]
        + [pltpu.SemaphoreType.DMA((n, 8)), pltpu.SemaphoreType.DMA((n, 8))],
        compiler_params=_cparams(),
    )(*shards)


def _gather_w_out_rider(w_out):
    def copies(full_ref, stage, send_sems, recv_sems):
        me = _me()[3]
        out = []
        for k in range(1, N_DEV):
            peer, pidx = _peer(k)
            send = pltpu.make_async_remote_copy(
                src_ref=stage, dst_ref=full_ref.at[me], send_sem=send_sems.at[k - 1], recv_sem=recv_sems.at[k - 1],
                device_id=peer, device_id_type=MESH_ID)
            recv = pltpu.make_async_remote_copy(
                src_ref=stage, dst_ref=full_ref.at[pidx], send_sem=send_sems.at[k - 1], recv_sem=recv_sems.at[k - 1],
                device_id=peer, device_id_type=MESH_ID)
            out.append((send, recv))
        return out

    def start(ins, outs, scr):
        stage, send_sems, recv_sems, own_sem = scr
        stage[...] = ins[0][...].astype(BF16)
        pltpu.make_async_copy(stage, outs[0].at[_me()[3]], own_sem).start()
        for send, _ in copies(outs[0], stage, send_sems, recv_sems):
            send.start()

    def finish(ins, outs, scr):
        stage, send_sems, recv_sems, own_sem = scr
        pltpu.make_async_copy(stage, outs[0].at[_me()[3]], own_sem).wait()
        pairs = copies(outs[0], stage, send_sems, recv_sems)
        for _, recv in pairs:
            recv.wait_recv()
        for send, _ in pairs:
            send.wait_send()

    shape = SHARD_SHAPES[3]
    return Rider(
        args=[w_out], in_specs=[pl.BlockSpec(shape, lambda t, *_: (0, 0))],
        out_shape=[jax.ShapeDtypeStruct((N_DEV,) + shape, BF16)], out_specs=[pl.BlockSpec(memory_space=pl.ANY)],
        scratch=[pltpu.VMEM(shape, BF16), pltpu.SemaphoreType.DMA((N_DEV - 1,)), pltpu.SemaphoreType.DMA((N_DEV - 1,)),
                 pltpu.SemaphoreType.DMA],
        start=start, finish=finish)


def _scatter_g_out_rider(blocks):
    def copies(src_ref, dst_ref, send_sems, recv_sems):
        out = []
        for k in range(1, N_DEV):
            peer, pidx = _peer(k)
            out.append(pltpu.make_async_remote_copy(
                src_ref=src_ref.at[pidx], dst_ref=dst_ref.at[k], send_sem=send_sems.at[k - 1],
                recv_sem=recv_sems.at[k - 1], device_id=peer, device_id_type=MESH_ID))
        return out

    def start(ins, outs, scr):
        send_sems, recv_sems, own_sem = scr
        pltpu.make_async_copy(ins[0].at[_me()[3]], outs[0].at[0], own_sem).start()
        for cp in copies(ins[0], outs[0], send_sems, recv_sems):
            cp.start()

    def finish(ins, outs, scr):
        send_sems, recv_sems, own_sem = scr
        pltpu.make_async_copy(ins[0].at[_me()[3]], outs[0].at[0], own_sem).wait()
        for cp in copies(ins[0], outs[0], send_sems, recv_sems):
            cp.wait()

    hbm = pl.BlockSpec(memory_space=pl.ANY)
    return Rider(
        args=[blocks], in_specs=[hbm], out_shape=[jax.ShapeDtypeStruct(blocks.shape, blocks.dtype)], out_specs=[hbm],
        scratch=[pltpu.SemaphoreType.DMA((N_DEV - 1,)), pltpu.SemaphoreType.DMA((N_DEV - 1,)), pltpu.SemaphoreType.DMA],
        start=start, finish=finish)


def _adamw(w, g, m, v):
    m = ADAM_B1 * m + (1.0 - ADAM_B1) * g
    v = ADAM_B2 * v + (1.0 - ADAM_B2) * jnp.square(g)
    m_hat = m / (1.0 - ADAM_B1 ** ADAM_STEP)
    v_hat = v / (1.0 - ADAM_B2 ** ADAM_STEP)
    delta = -ADAM_LR * (m_hat / (jnp.sqrt(v_hat) + ADAM_EPS) + ADAM_WD * w)
    return delta, m, v


def _chunk_rows(count, cols):
    rows = (16 * 8 * LANES) // (-(-cols // LANES) * LANES)
    while count % rows:
        rows //= 2
    return rows


def _reduce_grads_rider(grads3, arrived, small_part, steps):
    n = len(grads3)
    shapes = [tuple(g.shape[1:]) for g in grads3]
    with_small = arrived is not None
    first_round, second_round = steps

    class Refs:
        def __init__(self, ins, outs, scr):
            self.g3, self.gsum = ins[0:n], outs[0:n]
            if with_small:
                self.arr, self.sp, self.gsum_out, self.ssum = ins[n], ins[n + 1], outs[n], outs[n + 1]
            self.own, self.sib, self.part = scr[0:n], scr[n:2 * n], scr[2 * n:3 * n]
            self.in_a, self.out_b, self.in_b = scr[3 * n:4 * n], scr[4 * n:5 * n], scr[5 * n:6 * n]
            self.rsmall = scr[6 * n]
            (self.loc_sems, self.d2d_send, self.d2d_recv, self.a_send, self.a_recv, self.b_send, self.b_recv,
             self.sm_send, self.sm_recv) = scr[6 * n + 1:]
            self.x, self.y, self.c, self.me = _me()
            self.along_x, self.along_y = (1 - self.x, self.y), (self.x, 1 - self.y)
            self.across = (1 - self.x, 1 - self.y)

        def small(self):
            if not with_small:
                return []
            return [pltpu.make_async_remote_copy(
                src_ref=self.rsmall.at[0], dst_ref=self.rsmall.at[k], send_sem=self.sm_send.at[k - 1],
                recv_sem=self.sm_recv.at[k - 1], device_id=_peer(k)[0], device_id_type=MESH_ID)
                for k in range(1, N_DEV)]

        def level1(self):
            local, to_sib = [], []
            for t in range(n):
                for q in range(4):
                    local.append(pltpu.make_async_copy(
                        self.g3[t].at[2 * q + self.c], self.own[t].at[q], self.loc_sems.at[t, q]))
                    to_sib.append(pltpu.make_async_remote_copy(
                        src_ref=self.g3[t].at[2 * q + 1 - self.c], dst_ref=self.sib[t].at[q],
                        send_sem=self.d2d_send.at[t, q], recv_sem=self.d2d_recv.at[t, q],
                        device_id=(self.x, self.y, 1 - self.c), device_id_type=MESH_ID))
            return local, to_sib

        def round_a(self):
            out = []
            for t in range(n):
                half = shapes[t][0] // 2
                for k, (to, chip, h) in enumerate([(self.along_x, self.along_x, 0), (self.along_x, self.across, 0),
                                                   (self.along_y, self.along_y, 1), (self.along_y, self.across, 1)]):
                    out.append(pltpu.make_async_remote_copy(
                        src_ref=self.part[t].at[2 * chip[0] + chip[1], pl.ds(h * half, half), :],
                        dst_ref=self.in_a[t].at[k], send_sem=self.a_send.at[t, k], recv_sem=self.a_recv.at[t, k],
                        device_id=(*to, self.c), device_id_type=MESH_ID))
            return out

        def round_b(self):
            return [pltpu.make_async_remote_copy(
                src_ref=self.out_b[t].at[k], dst_ref=self.in_b[t].at[k], send_sem=self.b_send.at[t, k],
                recv_sem=self.b_recv.at[t, k], device_id=(*to, self.c), device_id_type=MESH_ID)
                for t in range(n) for k, to in enumerate([self.along_y, self.along_x])]

    def chunks(rows, count, fn):
        def step(i, carry):
            fn(pl.multiple_of(i * rows, rows))
            return carry

        lax.fori_loop(0, count // rows, step, 0)

    def start(*refs):
        r = Refs(*refs)
        if with_small:
            r.rsmall[0] = r.sp[...]
        local, to_sib = r.level1()
        for cp in r.small() + local + to_sib:
            cp.start()

    def begin_rounds(*refs):
        r = Refs(*refs)
        local, to_sib = r.level1()
        for cp in local:
            cp.wait()
        for cp in to_sib:
            cp.wait_recv()
        my_chip = 2 * r.x + r.y
        for t in range(n):
            rows = _chunk_rows(*shapes[t])

            def pair_sums(at, t=t, rows=rows):
                sl = pl.ds(at, rows)
                for q in range(4):
                    r.part[t][q, sl, :] = (r.own[t][q, sl, :].astype(F32) + r.sib[t][q, sl, :].astype(F32)).astype(BF16)
                r.gsum[t][sl, :] = r.own[t][my_chip, sl, :].astype(F32) + r.sib[t][my_chip, sl, :].astype(F32)

            chunks(rows, shapes[t][0], pair_sums)
        for cp in r.round_a():
            cp.start()
        if not with_small:
            return
        rows_out = _chunk_rows(*SHARD_SHAPES[3])

        def add_arrived(at):
            sl = pl.ds(at, rows_out)
            g = r.arr[0, sl, :].astype(F32)
            for k in range(1, N_DEV):
                g = g + r.arr[k, sl, :].astype(F32)
            r.gsum_out[sl, :] = g

        chunks(rows_out, SHARD_SHAPES[3][0], add_arrived)

    def pass_on(*refs):
        r = Refs(*refs)
        for cp in r.round_a():
            cp.wait_recv()
        q_x, q_y = 2 * r.along_x[0] + r.along_x[1], 2 * r.along_y[0] + r.along_y[1]
        for t in range(n):
            half = shapes[t][0] // 2
            rows = _chunk_rows(half, shapes[t][1])

            def add(at, t=t, rows=rows, half=half):
                lo, hi = pl.ds(at, rows), pl.ds(half + at, rows)
                r.gsum[t][lo, :] = r.gsum[t][lo, :] + r.in_a[t][0, lo, :].astype(F32)
                r.out_b[t][0, lo, :] = (r.part[t][q_y, lo, :].astype(F32) + r.in_a[t][1, lo, :].astype(F32)).astype(BF16)
                r.gsum[t][hi, :] = r.gsum[t][hi, :] + r.in_a[t][2, lo, :].astype(F32)
                r.out_b[t][1, lo, :] = (r.part[t][q_x, hi, :].astype(F32) + r.in_a[t][3, lo, :].astype(F32)).astype(BF16)

            chunks(rows, half, add)
        for cp in r.round_b():
            cp.start()

    def finish(*refs):
        r = Refs(*refs)
        passed = r.round_b()
        for cp in passed:
            cp.wait_recv()
        for t in range(n):
            half = shapes[t][0] // 2
            rows = _chunk_rows(half, shapes[t][1])

            def add(at, t=t, rows=rows, half=half):
                lo, hi = pl.ds(at, rows), pl.ds(half + at, rows)
                r.gsum[t][lo, :] = r.gsum[t][lo, :] + r.in_b[t][0, lo, :].astype(F32)
                r.gsum[t][hi, :] = r.gsum[t][hi, :] + r.in_b[t][1, lo, :].astype(F32)

            chunks(rows, half, add)
        small = r.small()
        for cp in small:
            cp.wait_recv()
        if with_small:
            tot = r.rsmall[r.me]
            for d in range(1, N_DEV):
                tot = tot + r.rsmall[jnp.bitwise_xor(r.me, d)]
            r.ssum[...] = tot
        for cp in small + r.level1()[1] + r.round_a() + passed:
            cp.wait_send()

    hbm = pl.BlockSpec(memory_space=pl.ANY)
    dma = pltpu.SemaphoreType.DMA

    def whole(shape):
        return pl.BlockSpec(shape, lambda i: (0,) * len(shape))

    def halves(slots):
        return [pltpu.VMEM((slots, s[0] // 2, s[1]), BF16) for s in shapes]

    small_args = [arrived, small_part] if with_small else []
    out_shapes = shapes + ([SHARD_SHAPES[3], (8, D_MODEL)] if with_small else [])
    return Rider(
        args=list(grads3) + small_args,
        in_specs=[hbm] * n + [whole(a.shape) for a in small_args],
        out_shape=[jax.ShapeDtypeStruct(s, F32) for s in out_shapes], out_specs=[whole(s) for s in out_shapes],
        scratch=[pltpu.VMEM((4,) + s, BF16) for _ in range(3) for s in shapes]
        + halves(4) + halves(2) + halves(2)
        + [pltpu.VMEM((N_DEV, 8, D_MODEL), F32), dma((n, 4)), dma((n, 4)), dma((n, 4)), dma((n, 4)), dma((n, 4)),
           dma((n, 2)), dma((n, 2)), dma((N_DEV - 1,)), dma((N_DEV - 1,))],
        start=start, finish=finish, stages=((first_round, begin_rounds), (second_round, pass_on)))


def _adamw_update(grads, small_grad, wmv, small_wmv):
    n_small = len(small_wmv)
    flat_grads = [g for pieces in grads for g in pieces]
    n_g = len(flat_grads)

    def body(*refs):
        g_refs, sg_ref, refs = iter(refs[0:n_g]), refs[n_g], refs[n_g + 1:]
        wmv_refs = [refs[3 * t:3 * t + 3] for t in range(4)]
        swmv_refs = [refs[12 + 3 * t:15 + 3 * t] for t in range(n_small)]
        outs = refs[12 + 3 * n_small:]
        out_refs = [outs[4 * t:4 * t + 4] for t in range(4)]
        sout_refs = [outs[16 + 4 * t:20 + 4 * t] for t in range(n_small)]
        loss_ref = outs[16 + 4 * n_small]
        for t, (w_ref, m_ref, v_ref) in enumerate(swmv_refs):
            g = sg_ref[t:t + 1, :w_ref.shape[1]]
            delta, m, v = _adamw(w_ref[...], g, m_ref[...], v_ref[...])
            sout_refs[t][0][...], sout_refs[t][1][...], sout_refs[t][2][...], sout_refs[t][3][...] = g, delta, m, v
        loss_ref[...] = (0.5 / D_MODEL) * jnp.sum(sg_ref[n_small:n_small + 1, :], axis=1, keepdims=True)
        for t in range(4):
            rows = ADAM_ROWS[t]
            w_ref, m_ref, v_ref = wmv_refs[t]
            g_out, d_out, m_out, v_out = out_refs[t]

            def step(i, carry, g_ref, first, rows=rows, w_ref=w_ref, m_ref=m_ref, v_ref=v_ref,
                     g_out=g_out, d_out=d_out, m_out=m_out, v_out=v_out):
                at = pl.multiple_of(i * rows, rows)
                r = pl.ds(pl.multiple_of(first + at, rows), rows)
                g = g_ref[pl.ds(at, rows), :]
                delta, m, v = _adamw(w_ref[r, :], g, m_ref[r, :], v_ref[r, :])
                g_out[r, :], d_out[r, :], m_out[r, :], v_out[r, :] = g, delta, m, v
                return carry

            first = 0
            for piece in grads[t]:
                lax.fori_loop(0, piece.shape[0] // rows, functools.partial(step, g_ref=next(g_refs), first=first), 0)
                first += piece.shape[0]

    vmem = pl.BlockSpec(memory_space=pltpu.VMEM)
    flat_wmv = [a for trio in wmv for a in trio]
    flat_small = [a for trio in small_wmv for a in trio]
    out_shape = ([jax.ShapeDtypeStruct(s, F32) for s in SHARD_SHAPES for _ in range(4)]
                 + [jax.ShapeDtypeStruct(trio[0].shape, F32) for trio in small_wmv for _ in range(4)]
                 + [jax.ShapeDtypeStruct((1, 1), F32)])
    return _pcall(
        body, name="adamw",
        in_specs=[vmem] * (n_g + 1 + len(flat_wmv) + len(flat_small)), out_specs=[vmem] * len(out_shape),
        out_shape=out_shape,
        compiler_params=_cparams(),
    )(*flat_grads, small_grad, *flat_wmv, *flat_small)


def _small_rows(ln_g, ln_b, q_norm_g, kv_norm_g, extra=None):
    pad = lambda a: jnp.pad(a, (0, D_MODEL - a.shape[0]))
    rows = [ln_g, ln_b, pad(q_norm_g), pad(kv_norm_g)] + ([] if extra is None else [extra])
    return jnp.pad(jnp.stack(rows), ((0, 8 - len(rows)), (0, 0)))


def kernel(x, w_in, q_norm_g, kv_norm_g, w_uq, w_ukv, w_out, ln_g, ln_b, loss_target, m_w_in, m_q_norm_g, m_kv_norm_g, m_w_uq, m_w_ukv, m_w_out, m_ln_g, m_ln_b, v_w_in, v_q_norm_g, v_kv_norm_g, v_w_uq, v_w_ukv, v_w_out, v_ln_g, v_ln_b):
    w_in_r, w_uq_r, w_ukv_r = _all_gather_weights([w_in, w_uq, w_ukv])
    grad_x, ((g_in_first, g_uq, g_ukv, g_out, g_small), (g_in_rest,)) = _local_step(
        x[0], loss_target[0], w_in_r, w_uq_r, w_ukv_r, _gather_w_out_rider(w_out), _scatter_g_out_rider,
        _reduce_grads_rider, q_norm_g, kv_norm_g, ln_g, ln_b)
    row = lambda a: a.reshape(1, -1)
    small_wmv = [(row(ln_g), row(m_ln_g), row(v_ln_g)), (row(ln_b), row(m_ln_b), row(v_ln_b)),
                 (row(q_norm_g), row(m_q_norm_g), row(v_q_norm_g)), (row(kv_norm_g), row(m_kv_norm_g), row(v_kv_norm_g))]
    wmv = [(w_in, m_w_in, v_w_in), (w_uq, m_w_uq, v_w_uq), (w_ukv, m_w_ukv, v_w_ukv), (w_out, m_w_out, v_w_out)]
    res = _adamw_update([[g_in_first, g_in_rest], [g_uq], [g_ukv], [g_out]], g_small, wmv, small_wmv)
    big = [res[4 * t:4 * t + 4] for t in range(4)]
    small = [[a.reshape(-1) for a in res[16 + 4 * t:20 + 4 * t]] for t in range(4)]
    loss = res[32].reshape(())

    def group(kind):
        return (big[0][kind], small[2][kind], small[3][kind], big[1][kind], big[2][kind], big[3][kind],
                small[0][kind], small[1][kind])

    return (loss, grad_x[None], *group(0), *group(1), *group(2), *group(3))
```

```python
import functools
from typing import Callable, NamedTuple

import numpy as np
import jax
import jax.numpy as jnp
from jax import lax
from jax.experimental import pallas as pl
from jax.experimental.pallas import tpu as pltpu

F32 = jnp.float32
BF16 = jnp.bfloat16

D_MODEL = 1024
ROPE_THETA = 500000.0
NEG = -1e30
RMS_EPS = 1e-6
LN_EPS = 1e-5
HEADS = 8
MLA_NOPE = 64
MLA_ROPE = 32
MLA_V = 64
Q_RANK = 384
KV_RANK = 256
DIL_HEAD = 64
DIL_ROT = 16
DIL_CONFIGS = ((128, 1), (512, 4), (2048, 16))
DIL_NEAR = 512
HW = HEADS * 64
QW = HW + HEADS * MLA_ROPE
IN_SPLITS = (Q_RANK, KV_RANK, MLA_ROPE, HW, HW, HW, HW, HW)
IN_WIDTH = sum(IN_SPLITS)
ALPHA = 2.0 ** 0.25
MLA_SCALE = (MLA_NOPE + MLA_ROPE) ** -0.5
DIL_SCALE = DIL_HEAD ** -0.5
LOG2E = 1.4426950408889634
LN2 = 0.6931471805599453

ADAM_LR = 0.001
ADAM_B1 = 0.9
ADAM_B2 = 0.999
ADAM_EPS = 1e-08
ADAM_WD = 0.01
ADAM_STEP = 10

N_DEV = 8
LANES = 128
VMEM_LIMIT = 56 * 1024 * 1024
BLOCK_TOKENS = 512
BLOCK_MLA = 512
BLOCK_DIL = 512

C_CQ, C_CKV, C_KR, C_GA, C_QB, C_KB, C_VB, C_GB, C_END = 0, 384, 640, 768, 1280, 1792, 2304, 2816, 3328

NT = (((1,), (1,)), ((), ()))
TN = (((0,), (0,)), ((), ()))


def _pcall(body, **kw):
    return pl.pallas_call(body, **kw)


def _cparams(**kw):
    return pltpu.CompilerParams(vmem_limit_bytes=VMEM_LIMIT, **kw)


def _rope_tables(seq):
    def tabs(dim, period):
        half = dim // 2
        inv = np.float32(ROPE_THETA) ** (-np.arange(0, dim, 2, dtype=np.float32) / np.float32(dim))
        ang = np.arange(seq, dtype=np.float32)[:, None] * inv.astype(np.float32)[None, :]
        cos, sin = np.cos(ang).astype(np.float32), np.sin(ang).astype(np.float32)
        j = np.arange(LANES) % period
        f = j % half
        c = np.where(j < dim, cos[:, f], np.float32(1.0))
        s1 = np.where(j < half, -sin[:, f], np.float32(0.0))
        s2 = np.where((j >= half) & (j < dim), sin[:, f], np.float32(0.0))
        return [c, s1, s2]
    return np.stack(tabs(MLA_ROPE, MLA_ROPE) + tabs(DIL_ROT, DIL_HEAD)).astype(np.float32)


def _rope(t, c, s1, s2, half):
    return t * c + pltpu.roll(t, LANES - half, 1) * s1 + pltpu.roll(t, half, 1) * s2


def _rope_t(d, c, s1, s2, half):
    return d * c + pltpu.roll(d * s1, half, 1) + pltpu.roll(d * s2, LANES - half, 1)


def _rope_wide(fn, t, c, s1, s2, half):
    return jnp.concatenate(
        [fn(t[:, i:i + LANES], c, s1, s2, half) for i in range(0, t.shape[1], LANES)], axis=1)


def _mla_bias_t(blk):
    a = np.arange(blk)
    causal = np.where(a[:, None] <= a[None, :], 0.0, NEG)
    return np.stack([np.zeros((blk, blk)), causal]).astype(np.float32)


def _dil_bias_t(blk, reach):
    a = np.arange(blk)
    out = []
    for off in range(-(-reach // blk) + 1):
        delta = blk * off + a[None, :] - a[:, None]
        mult = np.zeros((blk, blk))
        for window, dil in DIL_CONFIGS:
            mult += (delta >= 0) & (delta % dil == 0) & (delta <= min(window, reach))
        out.append(np.where(mult > 0, np.log2(np.maximum(mult, 1.0)), NEG))
    return np.stack(out).astype(np.float32)


def _dil_far_bias_t(length):
    window, dil = DIL_CONFIGS[-1]
    a = np.arange(length)
    steps_back = a[None, :] - a[:, None]
    seen = (steps_back * dil > DIL_NEAR) & (steps_back * dil <= window)
    return np.where(seen, 0.0, NEG).astype(np.float32)[None]


def _lanes_to_classes(a, dil):
    h, s = a.shape
    return a.reshape(h, s // dil, dil).transpose(0, 2, 1).reshape(h, s)


def _lanes_from_classes(a, dil):
    h, s = a.shape
    return a.reshape(h, dil, s // dil).transpose(0, 2, 1).reshape(h, s)


def _steps(nq, span, by_key, diag_only_bias):
    rows = []
    if by_key:
        for ki in range(nq):
            hi = min(nq - 1, ki + span)
            for qi in range(ki, hi + 1):
                rows.append((qi, ki, int(qi == ki), int(qi == hi)))
    else:
        for qi in range(nq):
            lo = max(0, qi - span)
            for ki in range(lo, qi + 1):
                rows.append((qi, ki, int(ki == lo), int(ki == qi)))
    arr = np.array(rows, dtype=np.int32)
    off = arr[:, 0] - arr[:, 1]
    bias_idx = (off == 0).astype(np.int32) if diag_only_bias else off.astype(np.int32)
    return [jnp.asarray(v) for v in (arr[:, 0], arr[:, 1], bias_idx, arr[:, 2], arr[:, 3])]


def _by_class(val, out_ref, lanes_sc):
    n_cls, per = out_ref.shape[0], out_ref.shape[1]
    for c in range(val.shape[1] // LANES):
        lanes_sc[c] = val[:, LANES * c:LANES * (c + 1)]
        for r in range(n_cls):
            rows = lanes_sc.at[c][pl.ds(r, per, stride=n_cls), :]
            out_ref[r, :, LANES * c:LANES * (c + 1)] = rows.astype(out_ref.dtype)


def _in_sequence(ref, lanes_sc):
    n_cls, per, width = ref.shape
    for c in range(width // LANES):
        for r in range(n_cls):
            lanes_sc.at[c][pl.ds(r, per, stride=n_cls), :] = ref[r, :, LANES * c:LANES * (c + 1)].astype(F32)
    return jnp.concatenate([lanes_sc[c] for c in range(width // LANES)], axis=1)


def _fwd_proj(x, w_in_r, w_uq_r, w_ukv_r, qg, kvg, tabs, bt, n_cls):
    seq = x.shape[0]

    def body(x_ref, win_ref, wuq_ref, wukv_ref, qg_ref, kvg_ref, tab_ref,
             cq_ref, ckv_ref, qn_ref, kvn_ref, qcat_ref, kn_ref, kpe_ref, v_ref,
             ga_ref, gb_ref, qb_ref, kb_ref, vb_ref, knt_ref, kpet_ref, vt_ref, kbt_ref, vbt_ref,
             qbc_ref, kbc_ref, vbc_ref, lanes_sc):
        xb = x_ref[...].astype(BF16)

        def proj(lo, hi):
            return jnp.dot(xb, win_ref[:, lo:hi], preferred_element_type=F32)

        m_tabs = (tab_ref[0], tab_ref[1], tab_ref[2])
        d_tabs = (tab_ref[3], tab_ref[4], tab_ref[5])

        def use_cq(cq):
            cq_ref[...] = cq
            qn = (cq * lax.rsqrt(jnp.mean(cq * cq, axis=1, keepdims=True) + RMS_EPS) * qg_ref[...]).astype(BF16)
            qn_ref[...] = qn
            q = jnp.dot(qn, wuq_ref[...], preferred_element_type=F32)
            qcat_ref[:, :HW] = (q[:, :HW] * (MLA_SCALE * LOG2E)).astype(BF16)
            qcat_ref[:, HW:] = (
                _rope_wide(_rope, q[:, HW:], *m_tabs, MLA_ROPE // 2) * (MLA_SCALE * LOG2E)).astype(BF16)

        def use_ckv(ckv):
            ckv_ref[...] = ckv
            kvn = (ckv * lax.rsqrt(jnp.mean(ckv * ckv, axis=1, keepdims=True) + RMS_EPS) * kvg_ref[...]).astype(BF16)
            kvn_ref[...] = kvn
            kv = jnp.dot(kvn, wukv_ref[...], preferred_element_type=F32)
            kn_ref[...] = kv[:, :HW].astype(BF16)
            v_ref[...] = kv[:, HW:].astype(BF16)
            knt_ref[...] = kv[:, :HW].T.astype(BF16)
            vt_ref[...] = kv[:, HW:].T.astype(BF16)

        def use_kr(kr):
            kpe = _rope(kr, *m_tabs, MLA_ROPE // 2)
            kpe_ref[...] = kpe.astype(BF16)
            kpet_ref[...] = kpe.T[:MLA_ROPE, :].astype(BF16)

        def use_ga(ga):
            ga_ref[...] = ga

        def use_qb(qb):
            qb = _rope_wide(_rope, qb, *d_tabs, DIL_ROT // 2) * (DIL_SCALE * LOG2E)
            qb_ref[...] = qb.astype(BF16)
            _by_class(qb, qbc_ref, lanes_sc)

        def use_kb(kb):
            kb = _rope_wide(_rope, kb, *d_tabs, DIL_ROT // 2)
            kb_ref[...] = kb.astype(BF16)
            kbt_ref[...] = kb.T.astype(BF16)
            _by_class(kb, kbc_ref, lanes_sc)

        def use_vb(vb):
            vb_ref[...] = vb.astype(BF16)
            vbt_ref[...] = vb.T.astype(BF16)
            _by_class(vb, vbc_ref, lanes_sc)

        def use_gb(gb):
            gb_ref[...] = gb

        pieces = [(C_CQ, C_CKV, use_cq), (C_CKV, C_KR, use_ckv), (C_KR, C_GA, use_kr), (C_GA, C_QB, use_ga),
                  (C_QB, C_KB, use_qb), (C_KB, C_VB, use_kb), (C_VB, C_GB, use_vb), (C_GB, C_END, use_gb)]
        ahead = proj(*pieces[0][:2])
        for n, (_, _, use) in enumerate(pieces):
            cur = ahead
            if n + 1 < len(pieces):
                ahead = proj(*pieces[n + 1][:2])
            use(cur)

    def tok(width):
        return pl.BlockSpec((bt, width), lambda i: (i, 0))

    def tok_t(height):
        return pl.BlockSpec((height, bt), lambda i: (0, i))

    def full(a):
        return pl.BlockSpec(a.shape, lambda i: (0,) * a.ndim)

    outs = [(Q_RANK, F32), (KV_RANK, F32), (Q_RANK, BF16), (KV_RANK, BF16), (QW, BF16), (HW, BF16),
            (LANES, BF16), (HW, BF16), (HW, F32), (HW, F32), (HW, BF16), (HW, BF16), (HW, BF16)]
    outs_t = [HW, MLA_ROPE, HW, HW, HW]
    by_class = pl.BlockSpec((n_cls, bt // n_cls, HW), lambda i: (0, i, 0))
    return _pcall(
        body, name="fwd_proj", grid=(seq // bt,),
        in_specs=[tok(D_MODEL), full(w_in_r), full(w_uq_r), full(w_ukv_r), full(qg), full(kvg),
                  pl.BlockSpec((6, bt, LANES), lambda i: (0, i, 0))],
        out_specs=[tok(w) for w, _ in outs] + [tok_t(h) for h in outs_t] + [by_class] * 3,
        out_shape=[jax.ShapeDtypeStruct((seq, w), dt) for w, dt in outs]
        + [jax.ShapeDtypeStruct((h, seq), BF16) for h in outs_t]
        + [jax.ShapeDtypeStruct((n_cls, seq // n_cls, HW), BF16)] * 3,
        scratch_shapes=[pltpu.VMEM((HW // LANES, bt, LANES), F32)],
        compiler_params=_cparams(dimension_semantics=("arbitrary",)),
    )(x, w_in_r, w_uq_r, w_ukv_r, qg, kvg, tabs)


def _head_masks(lane, h):
    e, g = h % 2, h % 4
    me = (lane >= 64 * e) & (lane < 64 * e + 64)
    mr = (lane >= 32 * g) & (lane < 32 * g + 32)
    return me, mr


def _masked(mask, a):
    return jnp.where(mask, a, jnp.zeros_like(a))


def _pair_operands(q_ref, k_ref, kpe_ref, lane, j, ks=slice(None), qs=slice(None)):
    cols = slice(LANES * j, LANES * (j + 1))
    qc = q_ref[qs, cols]
    kj = k_ref[ks, cols]
    kes = []
    for h in (2 * j, 2 * j + 1):
        me, mr = _head_masks(lane, h)
        ke = _masked(me, kj)
        if kpe_ref is not None:
            ke = jnp.concatenate([ke, _masked(mr, kpe_ref[ks, :])], axis=1)
        kes.append(ke)
    if kpe_ref is not None:
        qc = jnp.concatenate([qc, q_ref[qs, HW + LANES * (j // 2):HW + LANES * (j // 2 + 1)]], axis=1)
    return qc, kes


def _tile_variants(bias_t):
    out = {}
    for i, tile in enumerate(np.asarray(bias_t)):
        h = tile.shape[0] // 2
        skip = 1 if (tile[h:, :h] == NEG).all() else 2 if (tile[:h, h:] == NEG).all() else 0
        out[i] = (bool((tile != 0).any()), skip)
    return out


def _tile_parts(blk, skip):
    lo, hi, full = slice(0, blk // 2), slice(blk // 2, blk), slice(0, blk)
    return {0: [(full, full)], 1: [(lo, full), (hi, hi)], 2: [(hi, full), (lo, lo)]}[skip]


class Rider(NamedTuple):
    args: list
    in_specs: list
    out_shape: list
    out_specs: list
    scratch: list
    start: Callable
    finish: Callable
    stages: tuple = ()


def _ride_along(body, ride, n_prefetch, n_in, n_out, n_scratch, n_steps):
    if ride is None:
        return body

    def wrapped(*refs):
        pre, rest = refs[:n_prefetch], refs[n_prefetch:]
        a = n_in
        b = a + len(ride.args)
        c = b + n_out
        d = c + len(ride.out_shape)
        e = d + n_scratch
        mine = (rest[a:b], rest[c:d], rest[e:])
        t = pl.program_id(0)
        pl.when(t == 0)(lambda: ride.start(*mine))
        for at, stage in ride.stages:
            pl.when(t == at)(functools.partial(stage, *mine))
        body(*pre, *rest[:a], *rest[b:c], *rest[d:e])
        pl.when(t == n_steps - 1)(lambda: ride.finish(*mine))

    return wrapped


def _attn_fwd(name, q, k, kpe, vt, bias_t, steps, blk, ride=None, v_token_major=False):
    seq = q.shape[0]
    mla = kpe is not None
    n_steps = int(steps[0].shape[0])
    variants = _tile_variants(bias_t)

    def body(qi_r, ki_r, bi_r, fi_r, la_r, *refs):
        if mla:
            q_ref, k_ref, kpe_ref, vt_ref, b_ref, o_ref, lse_ref, m_sc, l_sc, acc_sc, st_sc = refs
        else:
            q_ref, k_ref, vt_ref, b_ref, o_ref, lse_ref, m_sc, l_sc, acc_sc, st_sc = refs
        t = pl.program_id(0)

        @pl.when(fi_r[t] == 1)
        def _():
            m_sc[...] = jnp.full(m_sc.shape, NEG, F32)
            l_sc[...] = jnp.zeros(l_sc.shape, F32)
            acc_sc[...] = jnp.zeros(acc_sc.shape, F32)

        lane = lax.broadcasted_iota(jnp.int32, (1, LANES), 1)
        if v_token_major:
            vt_all = vt_ref[...].astype(F32).T.astype(BF16)
            vt_rows = lambda rows, ks: vt_all[rows, ks]
        else:
            vt_rows = lambda rows, ks: vt_ref[rows, ks]

        def tile_pass(ks, qs, with_bias):
            nk, nq = ks.stop - ks.start, qs.stop - qs.start
            ones = jnp.ones((16, nk), BF16)

            def pair_scores(j):
                qc, kes = _pair_operands(q_ref, k_ref, kpe_ref if mla else None, lane, j, ks, qs)
                st = lax.dot_general(jnp.concatenate(kes, axis=0), qc, NT, preferred_element_type=F32)
                maxes = []
                for e in range(2):
                    se = st[e * nk:(e + 1) * nk]
                    if with_bias:
                        se = se + b_ref[0, ks, qs]
                    st_sc[j % 2, e * nk:(e + 1) * nk, 0:nq] = se
                    maxes.append(jnp.max(se, axis=0, keepdims=True))
                return maxes

            def softmax_pv(h, col_max):
                st = st_sc[(h // 2) % 2, (h % 2) * nk:(h % 2 + 1) * nk, 0:nq]
                hrow = slice(h, h + 1)
                m_prev = m_sc[hrow, qs]
                m_new = jnp.maximum(m_prev, col_max)
                alpha = jnp.exp2(m_prev - m_new)
                pt = jnp.exp2(st - m_new).astype(BF16)
                m_sc[hrow, qs] = m_new
                rows = slice(64 * h, 64 * h + 64)
                res = jnp.dot(jnp.concatenate([vt_rows(rows, ks), ones], axis=0), pt, preferred_element_type=F32)
                acc_sc[rows, qs] = alpha * acc_sc[rows, qs] + res[:64]
                l_sc[hrow, qs] = alpha * l_sc[hrow, qs] + res[64:65]

            maxes = pair_scores(0)
            for j in range(HEADS // 2):
                cur = maxes
                if j + 1 < HEADS // 2:
                    maxes = pair_scores(j + 1)
                softmax_pv(2 * j, cur[0])
                softmax_pv(2 * j + 1, cur[1])

        def step(with_bias, skip):
            for ks, qs in _tile_parts(blk, skip):
                tile_pass(ks, qs, with_bias)

        for idx, (with_bias, skip) in variants.items():
            if len(variants) == 1:
                step(with_bias, skip)
            else:
                pl.when(bi_r[t] == idx)(functools.partial(step, with_bias, skip))

        @pl.when(la_r[t] == 1)
        def _():
            for h in range(HEADS):
                rows = slice(64 * h, 64 * h + 64)
                acc_sc[rows, :] = acc_sc[rows, :] / l_sc[h:h + 1, :]
            o_ref[...] = acc_sc[...].T
            lse_ref[...] = m_sc[...] + jnp.log2(l_sc[...])

    qmap = lambda t, qi, ki, bi, fi, la: (qi[t], 0)
    kmap = lambda t, qi, ki, bi, fi, la: (ki[t], 0)
    in_specs = [pl.BlockSpec((blk, q.shape[1]), qmap), pl.BlockSpec((blk, HW), kmap)]
    args = [q, k]
    if mla:
        in_specs.append(pl.BlockSpec((blk, LANES), kmap))
        args.append(kpe)
    in_specs += [pl.BlockSpec((blk, HW), kmap) if v_token_major else
                 pl.BlockSpec((HW, blk), lambda t, qi, ki, bi, fi, la: (0, ki[t])),
                 pl.BlockSpec((1, blk, blk), lambda t, qi, ki, bi, fi, la: (bi[t], 0, 0))]
    args += [vt, jnp.asarray(bias_t)]
    out_specs = [pl.BlockSpec((blk, HW), qmap), pl.BlockSpec((HEADS, blk), lambda t, qi, ki, bi, fi, la: (0, qi[t]))]
    out_shape = [jax.ShapeDtypeStruct((seq, HW), F32), jax.ShapeDtypeStruct((HEADS, seq), F32)]
    scratch = [pltpu.VMEM((HEADS, blk), F32), pltpu.VMEM((HEADS, blk), F32),
               pltpu.VMEM((HW, blk), F32), pltpu.VMEM((2, 2 * blk, blk), F32)]
    body = _ride_along(body, ride, 5, len(args), len(out_shape), len(scratch), n_steps)
    if ride is not None:
        args, in_specs = args + ride.args, in_specs + ride.in_specs
        out_specs, out_shape, scratch = out_specs + ride.out_specs, out_shape + ride.out_shape, scratch + ride.scratch
    return _pcall(
        body, name=name,
        grid_spec=pltpu.PrefetchScalarGridSpec(
            num_scalar_prefetch=5, grid=(n_steps,), in_specs=in_specs, out_specs=out_specs, scratch_shapes=scratch),
        out_shape=out_shape,
        compiler_params=_cparams(dimension_semantics=("arbitrary",)),
    )(*steps, *args)


def _attn_bwd(name, q, k, kpe, v, kt, kpet, bias_t, do, lse, dstat, steps, blk, ride=None, single_visit=False):
    assert not (single_visit and kpe is not None) and (kt is not None or single_visit)
    seq = q.shape[0]
    mla = kpe is not None
    qw = q.shape[1]
    n_steps = int(steps[0].shape[0])
    dk_dtype = BF16 if mla else F32
    variants = _tile_variants(bias_t)

    def body(qi_r, ki_r, bi_r, fi_r, la_r, *refs):
        if mla:
            (q_ref, k_ref, kpe_ref, v_ref, kt_ref, kpet_ref, b_ref, do_ref, lse_ref, d_ref,
             dq_ref, dk_ref, dkpe_ref, dv_ref, dk_sc, dkpe_sc, dv_sc, st_sc, dpt_sc) = refs
        else:
            q_ref, k_ref, v_ref, *rest = refs
            kt_ref = rest.pop(0) if kt is not None else None
            b_ref, do_ref, lse_ref, d_ref, dq_out_ref, dk_ref, dv_ref, dk_sc, dv_sc, st_sc, dpt_sc, *rest = rest
            dq_ref = rest[0] if single_visit else dq_out_ref
        t = pl.program_id(0)

        @pl.when(jnp.logical_or(t == 0, single_visit))
        def _():
            dq_ref[...] = jnp.zeros(dq_ref.shape, F32)

        @pl.when(fi_r[t] == 1)
        def _():
            dk_sc[...] = jnp.zeros(dk_sc.shape, F32)
            dv_sc[...] = jnp.zeros(dv_sc.shape, F32)
            if mla:
                dkpe_sc[...] = jnp.zeros(dkpe_sc.shape, F32)

        qi = 0 if single_visit else qi_r[t]
        lane = lax.broadcasted_iota(jnp.int32, (1, LANES), 1)
        if kt is None:
            kt_all = k_ref[...].astype(F32).T.astype(BF16)
            kt_rows = lambda rows, ks: kt_all[rows, ks]
        else:
            kt_rows = lambda rows, ks: kt_ref[rows, ks]

        def tile_pass(ks, qs, with_bias):
            nk, nq = ks.stop - ks.start, qs.stop - qs.start

            def pair_matmuls(j):
                cols = slice(LANES * j, LANES * (j + 1))
                qc, kes = _pair_operands(q_ref, k_ref, kpe_ref if mla else None, lane, j, ks, qs)
                st_sc[j % 2, 0:2 * nk, 0:nq] = lax.dot_general(
                    jnp.concatenate(kes, axis=0), qc, NT, preferred_element_type=F32)
                vj = v_ref[ks, cols]
                ves = [_masked(_head_masks(lane, h)[0], vj) for h in (2 * j, 2 * j + 1)]
                dpt_sc[j % 2, 0:2 * nk, 0:nq] = lax.dot_general(
                    jnp.concatenate(ves, axis=0), do_ref[qs, cols], NT, preferred_element_type=F32)

            def pair_grads(j):
                cols = slice(LANES * j, LANES * (j + 1))
                qj, doj = q_ref[qs, cols], do_ref[qs, cols]
                if mla:
                    qr = q_ref[qs, HW + LANES * (j // 2):HW + LANES * (j // 2 + 1)]
                pts, dsts, qms, doms = [], [], [], []
                for e in range(2):
                    h = 2 * j + e
                    me, mr = _head_masks(lane, h)
                    st = st_sc[j % 2, e * nk:(e + 1) * nk, 0:nq]
                    if with_bias:
                        st = st + b_ref[0, ks, qs]
                    pt = jnp.exp2(st - lse_ref[h:h + 1, qs])
                    dst = (pt * (dpt_sc[j % 2, e * nk:(e + 1) * nk, 0:nq] - d_ref[h:h + 1, qs])).astype(BF16)
                    pts.append(pt.astype(BF16))
                    dsts.append(dst)
                    doms.append(_masked(me, doj))
                    qm = _masked(me, qj)
                    if mla:
                        qm = jnp.concatenate([qm, _masked(mr, qr)], axis=1)
                    qms.append(qm)
                    ktl = kt_rows(slice(64 * h, 64 * h + 64), ks)
                    if mla:
                        ktl = jnp.concatenate([ktl, kpet_ref[:, ks]], axis=0)
                    dqc = jnp.dot(ktl, dst, preferred_element_type=F32)
                    dq_ref[qi, 64 * h:64 * h + 64, qs] += dqc[:64]
                    if mla:
                        dq_ref[qi, HW + MLA_ROPE * h:HW + MLA_ROPE * (h + 1), qs] += dqc[64:]
                dv_sc[ks, cols] += jnp.dot(
                    jnp.concatenate(pts, axis=1), jnp.concatenate(doms, axis=0), preferred_element_type=F32)
                dkc = jnp.dot(jnp.concatenate(dsts, axis=1), jnp.concatenate(qms, axis=0), preferred_element_type=F32)
                dk_sc[ks, cols] += dkc[:, :LANES]
                if mla:
                    dkpe_sc[ks, :] += dkc[:, LANES:]

            pair_matmuls(0)
            for j in range(HEADS // 2):
                if j + 1 < HEADS // 2:
                    pair_matmuls(j + 1)
                pair_grads(j)

        def step(with_bias, skip):
            for ks, qs in _tile_parts(blk, skip):
                tile_pass(ks, qs, with_bias)

        for idx, (with_bias, skip) in variants.items():
            if len(variants) == 1:
                step(with_bias, skip)
            else:
                pl.when(bi_r[t] == idx)(functools.partial(step, with_bias, skip))
        if single_visit:
            dq_out_ref[...] = dq_ref[0].T

        @pl.when(la_r[t] == 1)
        def _():
            dk_ref[...] = (dk_sc[...] * LN2).astype(dk_ref.dtype)
            dv_ref[...] = dv_sc[...].astype(dv_ref.dtype)
            if mla:
                dkpe_ref[...] = dkpe_sc[...] * LN2

    qmap = lambda t, qi, ki, bi, fi, la: (qi[t], 0)
    kmap = lambda t, qi, ki, bi, fi, la: (ki[t], 0)
    qmap_t = lambda t, qi, ki, bi, fi, la: (0, qi[t])
    kmap_t = lambda t, qi, ki, bi, fi, la: (0, ki[t])
    in_specs = [pl.BlockSpec((blk, qw), qmap), pl.BlockSpec((blk, HW), kmap)]
    args = [q, k]
    if mla:
        in_specs.append(pl.BlockSpec((blk, LANES), kmap))
        args.append(kpe)
    in_specs.append(pl.BlockSpec((blk, HW), kmap))
    args.append(v)
    if kt is not None:
        in_specs.append(pl.BlockSpec((HW, blk), kmap_t))
        args.append(kt)
    if mla:
        in_specs.append(pl.BlockSpec((MLA_ROPE, blk), kmap_t))
        args.append(kpet)
    in_specs += [pl.BlockSpec((1, blk, blk), lambda t, qi, ki, bi, fi, la: (bi[t], 0, 0)),
                 pl.BlockSpec((blk, HW), qmap), pl.BlockSpec((HEADS, blk), qmap_t), pl.BlockSpec((HEADS, blk), qmap_t)]
    args += [jnp.asarray(bias_t), do, lse, dstat]
    dq_shape = (seq // blk, qw, blk)
    if single_visit:
        out_specs, out_shape = [pl.BlockSpec((blk, qw), qmap)], [jax.ShapeDtypeStruct((seq, qw), F32)]
    else:
        out_specs = [pl.BlockSpec(dq_shape, lambda t, qi, ki, bi, fi, la: (0, 0, 0))]
        out_shape = [jax.ShapeDtypeStruct(dq_shape, F32)]
    out_specs.append(pl.BlockSpec((blk, HW), kmap))
    out_shape.append(jax.ShapeDtypeStruct((seq, HW), dk_dtype))
    scratch = [pltpu.VMEM((blk, HW), F32)]
    if mla:
        out_specs.append(pl.BlockSpec((blk, LANES), kmap))
        out_shape.append(jax.ShapeDtypeStruct((seq, LANES), F32))
        scratch.append(pltpu.VMEM((blk, LANES), F32))
    out_specs.append(pl.BlockSpec((blk, HW), kmap))
    out_shape.append(jax.ShapeDtypeStruct((seq, HW), BF16))
    scratch.append(pltpu.VMEM((blk, HW), F32))
    scratch += [pltpu.VMEM((2, 2 * blk, blk), F32), pltpu.VMEM((2, 2 * blk, blk), F32)]
    if single_visit:
        scratch.append(pltpu.VMEM((1, qw, blk), F32))
    body = _ride_along(body, ride, 5, len(args), len(out_shape), len(scratch), n_steps)
    if ride is not None:
        args, in_specs = args + ride.args, in_specs + ride.in_specs
        out_specs, out_shape, scratch = out_specs + ride.out_specs, out_shape + ride.out_shape, scratch + ride.scratch
    return _pcall(
        body, name=name,
        grid_spec=pltpu.PrefetchScalarGridSpec(
            num_scalar_prefetch=5, grid=(n_steps,), in_specs=in_specs, out_specs=out_specs,
            scratch_shapes=scratch),
        out_shape=out_shape,
        compiler_params=_cparams(dimension_semantics=("arbitrary",)),
    )(*steps, *args)


def _out_ln(oa, ob_near, ob_far, lse_near, lse_far, ga, gb, x, tgt, w_out, ln_g, ln_b, bt):
    seq = x.shape[0]

    def body(oa_ref, obn_ref, obf_ref, lsen_ref, lsef_ref, ga_ref, gb_ref, x_ref, tgt_ref, w_ref, g_ref, b_ref,
             dz_ref, doa_ref, dob_ref, dga_ref, dgb_ref, da_ref, db_ref, lse_ref, gwb_ref, small_ref, dobc_ref,
             gw_ref, lanes_sc):
        i = pl.program_id(0)

        @pl.when(i == 0)
        def _():
            gw_ref[...] = jnp.zeros(gw_ref.shape, F32)
            small_ref[...] = jnp.zeros(small_ref.shape, F32)

        def gate(g):
            sig = 0.5 * jnp.tanh(0.5 * g) + 0.5
            return g * sig, sig * (1.0 + g * (1.0 - sig))

        lse_n, lse_f = lsen_ref[...], lsef_ref[...]
        top = jnp.maximum(lse_n, lse_f)
        e_n, e_f = jnp.exp2(lse_n - top), jnp.exp2(lse_f - top)
        lse_ref[...] = top + jnp.log2(e_n + e_f)
        inv = 1.0 / (e_n + e_f)
        head_row = lax.broadcasted_iota(jnp.int32, (2 * HEADS, HW), 0) % HEADS
        spread = (head_row == lax.broadcasted_iota(jnp.int32, (2 * HEADS, HW), 1) // 64).astype(BF16)

        def per_lane(w):
            hi = w.astype(BF16)
            lo = (w - hi.astype(F32)).astype(BF16)
            return lax.dot_general(jnp.concatenate([hi, lo], axis=0), spread, TN, preferred_element_type=F32)

        o_b_all = per_lane(e_n * inv) * obn_ref[...] + per_lane(e_f * inv) * _in_sequence(obf_ref, lanes_sc)
        gam = g_ref[...]
        halves = [slice(0, bt // 2), slice(bt // 2, bt)]

        def gates_and_projection(rows):
            o_a, o_b = oa_ref[rows, :], o_b_all[rows]
            sa, dsa = gate(ga_ref[rows, :])
            sb, dsb = gate(gb_ref[rows, :])
            mix = jnp.concatenate([o_a * sa, o_b * sb], axis=1).astype(BF16)
            z = ALPHA * x_ref[rows, :] + jnp.dot(mix, w_ref[...], preferred_element_type=F32)
            return o_a, o_b, sa, dsa, sb, dsb, mix, z

        def norm_and_back(rows, mix, z):
            mu = jnp.mean(z, axis=1, keepdims=True)
            zc = z - mu
            rstd = lax.rsqrt(jnp.mean(zc * zc, axis=1, keepdims=True) + LN_EPS)
            xhat = zc * rstd
            diff = xhat * gam + b_ref[...] - tgt_ref[rows, :]
            dy = diff * (1.0 / D_MODEL)
            small_ref[0:1, :] += jnp.sum(dy * xhat, axis=0, keepdims=True)
            small_ref[1:2, :] += jnp.sum(dy, axis=0, keepdims=True)
            small_ref[2:3, :] += jnp.sum(diff * diff, axis=0, keepdims=True)
            dxh = dy * gam
            dz = rstd * (dxh - jnp.mean(dxh, axis=1, keepdims=True)
                         - xhat * jnp.mean(dxh * xhat, axis=1, keepdims=True))
            dz_ref[rows, :] = dz
            dzb = dz.astype(BF16)
            gw_ref[...] += lax.dot_general(mix, dzb, TN, preferred_element_type=F32)
            return lax.dot_general(dzb, w_ref[...], NT, preferred_element_type=F32)

        def gate_back(rows, o_a, o_b, sa, dsa, sb, dsb, dmix):
            doa, dob = dmix[:, :HW] * sa, dmix[:, HW:] * sb
            doa_ref[rows, :] = doa.astype(BF16)
            dob_ref[rows, :] = dob.astype(BF16)
            dga_ref[rows, :] = (dmix[:, :HW] * o_a * dsa).astype(BF16)
            dgb_ref[rows, :] = (dmix[:, HW:] * o_b * dsb).astype(BF16)
            return dob, doa * o_a, dob * o_b

        fronts = [gates_and_projection(rows) for rows in halves]
        dmixes = [norm_and_back(rows, f[6], f[7]) for rows, f in zip(halves, fronts)]
        backs = [gate_back(rows, *f[:6], dmix) for rows, f, dmix in zip(halves, fronts, dmixes)]
        dob, prod_a, prod_b = (jnp.concatenate(parts, axis=0) for parts in zip(*backs))
        _by_class(dob, dobc_ref, lanes_sc)

        @pl.when(i == seq // bt - 1)
        def _():
            gwb_ref[...] = gw_ref[...].astype(BF16)

        head_of = (lax.broadcasted_iota(jnp.int32, (2 * HW, LANES), 0) % HW) // 64
        ind = (head_of == lax.broadcasted_iota(jnp.int32, (2 * HW, LANES), 1)).astype(BF16)

        def head_sums(prod):
            hi = prod.astype(BF16)
            lo = (prod - hi.astype(F32)).astype(BF16)
            sums = jnp.dot(jnp.concatenate([hi, lo], axis=1), ind, preferred_element_type=F32)
            return sums.T[:HEADS, :]

        da_ref[...] = head_sums(prod_a)
        db_ref[...] = head_sums(prod_b)

    def tok(width):
        return pl.BlockSpec((bt, width), lambda i: (i, 0))

    def full(shape):
        return pl.BlockSpec(shape, lambda i: (0,) * len(shape))

    stat = pl.BlockSpec((HEADS, bt), lambda i: (0, i))
    n_cls = ob_far.shape[0]
    by_class = pl.BlockSpec((n_cls, bt // n_cls, HW), lambda i: (0, i, 0))
    return _pcall(
        body, name="out_ln", grid=(seq // bt,),
        in_specs=[tok(HW), tok(HW), by_class, stat, stat, tok(HW), tok(HW), tok(D_MODEL), tok(D_MODEL),
                  full((D_MODEL, D_MODEL)), full((1, D_MODEL)), full((1, D_MODEL))],
        out_specs=[tok(D_MODEL), tok(HW), tok(HW), tok(HW), tok(HW), stat, stat, stat,
                   full((D_MODEL, D_MODEL)), full((8, D_MODEL)), by_class],
        out_shape=[jax.ShapeDtypeStruct((seq, D_MODEL), F32)] + [jax.ShapeDtypeStruct((seq, HW), BF16)] * 4
        + [jax.ShapeDtypeStruct((HEADS, seq), F32)] * 3
        + [jax.ShapeDtypeStruct((D_MODEL, D_MODEL), BF16), jax.ShapeDtypeStruct((8, D_MODEL), F32),
           jax.ShapeDtypeStruct(ob_far.shape, BF16)],
        scratch_shapes=[pltpu.VMEM((D_MODEL, D_MODEL), F32), pltpu.VMEM((HW // LANES, bt, LANES), F32)],
        compiler_params=_cparams(dimension_semantics=("arbitrary",)),
    )(oa, ob_near, ob_far, lse_near, lse_far, ga, gb, x, tgt, w_out, ln_g, ln_b)


def _bwd_mid(dq_m, dkn, dv, dkpe, dqb, dkb, dvb, far, dga, dgb, cq, ckv, qn, kvn, w_uq_r, w_ukv_r, qg, kvg, tabs, bt):
    n_cls = far[0].shape[0]
    seq = cq.shape[0]

    def body(dqm_ref, dkn_ref, dv_ref, dkpe_ref, dqb_ref, dkb_ref, dvb_ref, dqf_ref, dkf_ref, dvf_ref, dga_ref, dgb_ref,
             cq_ref, ckv_ref, qn_ref, kvn_ref, wuq_ref, wukv_ref, qg_ref, kvg_ref, tab_ref,
             dh_ref, guq3_ref, gukv3_ref, small_ref, seq_sc, guq_ref, gukv_ref):
        i = pl.program_id(0)

        @pl.when(i == 0)
        def _():
            guq_ref[...] = jnp.zeros(guq_ref.shape, F32)
            gukv_ref[...] = jnp.zeros(gukv_ref.shape, F32)
            small_ref[...] = jnp.zeros(small_ref.shape, F32)

        m_tabs = (tab_ref[0], tab_ref[1], tab_ref[2])
        d_tabs = (tab_ref[3], tab_ref[4], tab_ref[5])

        def rms_bwd(c, dn, gain):
            r = lax.rsqrt(jnp.mean(c * c, axis=1, keepdims=True) + RMS_EPS)
            u = dn * gain
            dc = r * u - c * (r * r * r) * jnp.mean(u * c, axis=1, keepdims=True)
            return dc, jnp.sum(dn * c * r, axis=0, keepdims=True)

        dqm = dqm_ref[0].T
        dq = jnp.concatenate(
            [dqm[:, :HW], _rope_wide(_rope_t, dqm[:, HW:], *m_tabs, MLA_ROPE // 2)], axis=1) * MLA_SCALE
        dq = dq.astype(BF16)
        dkv = jnp.concatenate([dkn_ref[...], dv_ref[...]], axis=1)
        guq_ref[...] += lax.dot_general(qn_ref[...], dq, TN, preferred_element_type=F32)
        dqn = lax.dot_general(dq, wuq_ref[...], NT, preferred_element_type=F32)
        gukv_ref[...] += lax.dot_general(kvn_ref[...], dkv, TN, preferred_element_type=F32)
        dkvn = lax.dot_general(dkv, wukv_ref[...], NT, preferred_element_type=F32)

        dh_ref[:, C_KR:C_GA] = _rope_t(dkpe_ref[...], *m_tabs, MLA_ROPE // 2).astype(BF16)
        dh_ref[:, C_GA:C_QB] = dga_ref[...]
        in_sequence = functools.partial(_in_sequence, lanes_sc=seq_sc)
        dqb = dqb_ref[0].T + in_sequence(dqf_ref)
        dh_ref[:, C_QB:C_KB] = (_rope_wide(_rope_t, dqb, *d_tabs, DIL_ROT // 2) * DIL_SCALE).astype(BF16)
        dkb = dkb_ref[...] + in_sequence(dkf_ref)
        dh_ref[:, C_KB:C_VB] = _rope_wide(_rope_t, dkb, *d_tabs, DIL_ROT // 2).astype(BF16)
        dh_ref[:, C_VB:C_GB] = (dvb_ref[...].astype(F32) + in_sequence(dvf_ref)).astype(BF16)
        dh_ref[:, C_GB:C_END] = dgb_ref[...]

        dcq, gq = rms_bwd(cq_ref[...], dqn, qg_ref[...])
        small_ref[0:1, :] += gq
        dckv, gkv = rms_bwd(ckv_ref[...], dkvn, kvg_ref[...])
        small_ref[1:2, :KV_RANK] += gkv
        dh_ref[:, C_CQ:C_CKV] = dcq.astype(BF16)
        dh_ref[:, C_CKV:C_KR] = dckv.astype(BF16)

        @pl.when(i == seq // bt - 1)
        def _():
            for h in range(HEADS):
                guq3_ref[h] = jnp.concatenate(
                    [guq_ref[:, MLA_NOPE * h:MLA_NOPE * (h + 1)],
                     guq_ref[:, HW + MLA_ROPE * h:HW + MLA_ROPE * (h + 1)]], axis=1).astype(BF16)
                gukv3_ref[h] = jnp.concatenate(
                    [gukv_ref[:, MLA_NOPE * h:MLA_NOPE * (h + 1)],
                     gukv_ref[:, HW + MLA_V * h:HW + MLA_V * (h + 1)]], axis=1).astype(BF16)

    def tok(width):
        return pl.BlockSpec((bt, width), lambda i: (i, 0))

    def tok_t(a):
        per = a.shape[2] // bt
        return pl.BlockSpec((1, a.shape[1], bt), lambda i: (i // per, 0, i % per))

    def full(shape):
        return pl.BlockSpec(shape, lambda i: (0,) * len(shape))

    by_class = pl.BlockSpec((n_cls, bt // n_cls, HW), lambda i: (0, i, 0))
    uq3 = (HEADS, Q_RANK, MLA_NOPE + MLA_ROPE)
    ukv3 = (HEADS, KV_RANK, MLA_NOPE + MLA_V)
    return _pcall(
        body, name="bwd_mid", grid=(seq // bt,),
        in_specs=[tok_t(dq_m), tok(HW), tok(HW), tok(LANES), tok_t(dqb), tok(HW), tok(HW), by_class, by_class, by_class,
                  tok(HW), tok(HW),
                  tok(Q_RANK), tok(KV_RANK), tok(Q_RANK), tok(KV_RANK),
                  full(w_uq_r.shape), full(w_ukv_r.shape), full((1, Q_RANK)), full((1, KV_RANK)),
                  pl.BlockSpec((6, bt, LANES), lambda i: (0, i, 0))],
        out_specs=[tok(C_END), full(uq3), full(ukv3), full((8, Q_RANK))],
        out_shape=[jax.ShapeDtypeStruct((seq, C_END), BF16), jax.ShapeDtypeStruct(uq3, BF16),
                   jax.ShapeDtypeStruct(ukv3, BF16), jax.ShapeDtypeStruct((8, Q_RANK), F32)],
        scratch_shapes=[pltpu.VMEM((HW // LANES, bt, LANES), F32), pltpu.VMEM(w_uq_r.shape, F32),
                        pltpu.VMEM(w_ukv_r.shape, F32)],
        compiler_params=_cparams(dimension_semantics=("arbitrary",)),
    )(dq_m, dkn, dv, dkpe, dqb, dkb, dvb, *far, dga, dgb, cq, ckv, qn, kvn, w_uq_r, w_ukv_r, qg, kvg, tabs)


def _grad_x(dz, dh, w_in_r, bt, ride=None):
    seq = dz.shape[0]
    n_steps = seq // bt

    def body(dz_ref, dh_ref, w_ref, gx_ref):
        gx_ref[...] = ALPHA * dz_ref[...] + lax.dot_general(
            dh_ref[...], w_ref[...], NT, preferred_element_type=F32)

    args = [dz, dh, w_in_r]
    in_specs = [pl.BlockSpec((bt, D_MODEL), lambda i: (i, 0)), pl.BlockSpec((bt, C_END), lambda i: (i, 0)),
                pl.BlockSpec(w_in_r.shape, lambda i: (0, 0))]
    out_specs = [pl.BlockSpec((bt, D_MODEL), lambda i: (i, 0))]
    out_shape = [jax.ShapeDtypeStruct((seq, D_MODEL), F32)]
    scratch = []
    body = _ride_along(body, ride, 0, len(args), len(out_shape), 0, n_steps)
    if ride is not None:
        args, in_specs = args + ride.args, in_specs + ride.in_specs
        out_specs, out_shape, scratch = out_specs + ride.out_specs, out_shape + ride.out_shape, ride.scratch
    return _pcall(
        body, name="grad_x", grid=(n_steps,),
        in_specs=in_specs, out_specs=out_specs, out_shape=out_shape, scratch_shapes=scratch,
        compiler_params=_cparams(dimension_semantics=("arbitrary",)),
    )(*args)


def _grad_w_in(x, dh, bt, ride=None):
    seq = x.shape[0]
    n_steps = seq // bt
    shard = IN_WIDTH // N_DEV
    k_lo, k_hi = IN_SPLITS[0] + IN_SPLITS[1], IN_SPLITS[0] + IN_SPLITS[1] + MLA_ROPE

    def body(x_ref, dh_ref, out_ref, acc):
        i = pl.program_id(0)

        @pl.when(i == 0)
        def _():
            acc[...] = jnp.zeros(acc.shape, F32)

        acc[...] += lax.dot_general(x_ref[...].astype(BF16), dh_ref[...], TN, preferred_element_type=F32)

        @pl.when(i == n_steps - 1)
        def _():
            kr = acc[:, C_KR:C_GA]
            kr = kr + pltpu.roll(kr, 96, 1) + pltpu.roll(kr, 64, 1) + pltpu.roll(kr, 32, 1)
            for d in range(N_DEV):
                lo, hi = shard * d, shard * (d + 1)
                pieces = []
                if lo < k_lo:
                    pieces.append(acc[:, lo:min(hi, k_lo)])
                if lo < k_hi and hi > k_lo:
                    pieces.append(kr[:, max(lo, k_lo) - k_lo:min(hi, k_hi) - k_lo])
                if hi > k_hi:
                    shift = C_GA - k_hi
                    pieces.append(acc[:, max(lo, k_hi) + shift:hi + shift])
                blk = pieces[0] if len(pieces) == 1 else jnp.concatenate(pieces, axis=1)
                out_ref[d] = blk.astype(BF16)

    args = [x, dh]
    in_specs = [pl.BlockSpec((bt, D_MODEL), lambda i: (i, 0)), pl.BlockSpec((bt, C_END), lambda i: (i, 0))]
    out_specs = [pl.BlockSpec((N_DEV, D_MODEL, shard), lambda i: (0, 0, 0))]
    out_shape = [jax.ShapeDtypeStruct((N_DEV, D_MODEL, shard), BF16)]
    scratch = [pltpu.VMEM((D_MODEL, C_END), F32)]
    body = _ride_along(body, ride, 0, len(args), len(out_shape), len(scratch), n_steps)
    if ride is not None:
        args, in_specs = args + ride.args, in_specs + ride.in_specs
        out_specs, out_shape, scratch = out_specs + ride.out_specs, out_shape + ride.out_shape, scratch + ride.scratch
    return _pcall(
        body, name="grad_w_in", grid=(n_steps,),
        in_specs=in_specs, out_specs=out_specs, out_shape=out_shape, scratch_shapes=scratch,
        compiler_params=_cparams(dimension_semantics=("arbitrary",)),
    )(*args)


def _local_step(x, tgt, w_in_r, w_uq_r, w_ukv_r, w_out_rider, g_out_rider, reduce_rider, q_norm_g, kv_norm_g,
                ln_g, ln_b, bt=BLOCK_TOKENS, blk_m=BLOCK_MLA, blk_d=BLOCK_DIL):
    seq = x.shape[0]
    tabs = jnp.asarray(_rope_tables(seq))
    qg, kvg = q_norm_g.reshape(1, -1), kv_norm_g.reshape(1, -1)

    far_dil = DIL_CONFIGS[-1][1]
    cls = seq // far_dil
    (cq, ckv, qn, kvn, qcat, kn, kpe, v, ga, gb, qb, kb, vb, knt, kpet, vt, kbt, vbt, qb_c, kb_c, vb_c) = _fwd_proj(
        x, w_in_r, w_uq_r, w_ukv_r, qg, kvg, tabs, bt, far_dil)
    qb_c, kb_c, vb_c = (a.reshape(seq, HW) for a in (qb_c, kb_c, vb_c))

    nq_m, nq_d = seq // blk_m, seq // blk_d
    bias_m = _mla_bias_t(blk_m)
    oa, lse_a, w_out = _attn_fwd(
        "mla_fwd", qcat, kn, kpe, vt, bias_m, _steps(nq_m, nq_m, False, True), blk_m, ride=w_out_rider)

    bias_near = _dil_bias_t(blk_d, DIL_NEAR)
    ob_near, lse_near = _attn_fwd(
        "dil_fwd", qb, kb, None, vbt, bias_near, _steps(nq_d, -(-DIL_NEAR // blk_d), False, False), blk_d)
    each = np.arange(far_dil, dtype=np.int32)
    steps_far = [jnp.asarray(v) for v in (each, each, np.zeros_like(each), np.ones_like(each), np.ones_like(each))]
    bias_far = _dil_far_bias_t(cls)
    ob_far, lse_far = _attn_fwd(
        "dil_far_fwd", qb_c, kb_c, None, vb_c, bias_far, steps_far, cls, v_token_major=True)

    dz, doa, dob, dga, dgb, dst_a, dst_b, lse_b, g_out, small1, dob_c = _out_ln(
        oa, ob_near, ob_far.reshape(far_dil, cls, HW), lse_near, _lanes_from_classes(lse_far, far_dil), ga, gb, x, tgt,
        w_out.reshape(D_MODEL, D_MODEL), ln_g.reshape(1, -1), ln_b.reshape(1, -1), bt)

    dq_m, dkn, dkpe, dv, g_out_recv = _attn_bwd(
        "mla_bwd", qcat, kn, kpe, v, knt, kpet, bias_m, doa, lse_a, dst_a, _steps(nq_m, nq_m, True, True), blk_m,
        ride=g_out_rider(g_out.reshape(N_DEV, D_MODEL // N_DEV, D_MODEL)))
    dqb, dkb_near, dvb_near = _attn_bwd(
        "dil_bwd", qb, kb, None, vb, kbt, None, bias_near, dob, lse_b, dst_b,
        _steps(nq_d, -(-DIL_NEAR // blk_d), True, False), blk_d)
    dqb_far, dkb_far, dvb_far = _attn_bwd(
        "dil_far_bwd", qb_c, kb_c, None, vb_c, None, None, bias_far, dob_c.reshape(seq, HW),
        _lanes_to_classes(lse_b, far_dil), _lanes_to_classes(dst_b, far_dil), steps_far, cls, single_visit=True)
    far = [a.reshape(far_dil, cls, HW) for a in (dqb_far, dkb_far, dvb_far)]

    dh, g_uq, g_ukv, small2 = _bwd_mid(
        dq_m, dkn, dv, dkpe, dqb, dkb_near, dvb_near, far, dga, dgb, cq, ckv, qn, kvn, w_uq_r, w_ukv_r, qg, kvg, tabs, bt)
    bt_w = min(seq, 2 * bt)
    g_in, *reduced_others = _grad_w_in(
        x, dh, bt_w, ride=reduce_rider([g_uq, g_ukv], g_out_recv, (small1, small2), (1, seq // bt_w - 2)))
    grad_x, g_in = _grad_x(dz, dh, w_in_r, bt, ride=reduce_rider([g_in], None, None, (1, seq // bt - 2)))
    return grad_x, [g_in] + reduced_others


MESH_ID = pl.DeviceIdType.MESH
SHARD_SHAPES = ((D_MODEL, IN_WIDTH // N_DEV), (Q_RANK, 768 // N_DEV), (KV_RANK, 1024 // N_DEV), (D_MODEL // N_DEV, D_MODEL))
ADAM_ROWS = (32, 128, 128, 16)


def _me():
    x, y, c = lax.axis_index("x"), lax.axis_index("y"), lax.axis_index("c")
    return x, y, c, 4 * x + 2 * y + c


def _peer(k):
    x, y, c, _ = _me()
    px = 1 - x if (k >> 2) & 1 else x
    py = 1 - y if (k >> 1) & 1 else y
    pc = 1 - c if k & 1 else c
    return (px, py, pc), 4 * px + 2 * py + pc


def _all_gather_weights(shards):
    n = len(shards)
    shard = IN_WIDTH // N_DEV
    k_lo = IN_SPLITS[0] + IN_SPLITS[1]
    k_hi = k_lo + MLA_ROPE

    def body(*refs):
        ins = refs[:n]
        win_ref, wuq_ref, wukv_ref = refs[n:2 * n]
        bufs = refs[2 * n:3 * n]
        send_sems, recv_sems = refs[3 * n:]
        x, y, c, me = _me()
        here, sibling = (x, y, c), (x, y, 1 - c)
        along_x, along_y, across = (1 - x, y), (x, 1 - y), (1 - x, 1 - y)
        for t in range(n):
            bufs[t][me] = ins[t][...].astype(BF16)

        def copy(t, k, chip, pc, to, half=None):
            blk = bufs[t].at[4 * chip[0] + 2 * chip[1] + pc]
            if half is not None:
                rows = SHARD_SHAPES[t][0] // 2
                blk = blk.at[pl.ds(half * rows, rows), :]
            return pltpu.make_async_remote_copy(
                src_ref=blk, dst_ref=blk, send_sem=send_sems.at[t, k], recv_sem=recv_sems.at[t, k],
                device_id=to, device_id_type=MESH_ID)

        sends = []
        for t in range(n):
            sends += [copy(t, 0, (x, y), c, sibling), copy(t, 1, (x, y), c, (*along_x, c)),
                      copy(t, 2, (x, y), c, (*along_y, c))]
        for cp in sends:
            cp.start()
        for t in range(n):
            copy(t, 1, along_x, c, here).wait_recv()
            sends += [copy(t, 3, along_x, c, (*along_y, c), half=0), copy(t, 5, along_x, c, sibling)]
            sends[-2].start()
            sends[-1].start()
        for t in range(n):
            copy(t, 2, along_y, c, here).wait_recv()
            sends += [copy(t, 4, along_y, c, (*along_x, c), half=1), copy(t, 6, along_y, c, sibling)]
            sends[-2].start()
            sends[-1].start()
        for t in range(n):
            copy(t, 3, across, c, here, half=0).wait_recv()
            copy(t, 4, across, c, here, half=1).wait_recv()
            sends.append(copy(t, 7, across, c, sibling))
            sends[-1].start()
        for t in range(n):
            copy(t, 0, (x, y), 1 - c, here).wait_recv()
            for k, chip in ((5, along_x), (6, along_y), (7, across)):
                copy(t, k, chip, 1 - c, here).wait_recv()
        for cp in sends:
            cp.wait_send()

        a_in, a_uq, a_ukv = bufs
        for d in range(N_DEV):
            lo, hi = shard * d, shard * (d + 1)
            if lo < k_lo:
                win_ref[:, lo:min(hi, k_lo)] = a_in[d, :, 0:min(hi, k_lo) - lo]
            if lo < k_hi and hi > k_lo:
                kr = a_in[d, :, k_lo - lo:k_hi - lo]
                for rep in range(4):
                    win_ref[:, C_KR + MLA_ROPE * rep:C_KR + MLA_ROPE * (rep + 1)] = kr
            if hi > k_hi:
                src = max(lo, k_hi)
                win_ref[:, src + C_GA - k_hi:hi + C_GA - k_hi] = a_in[d, :, src - lo:hi - lo]
        for h in range(HEADS):
            wuq_ref[:, MLA_NOPE * h:MLA_NOPE * (h + 1)] = a_uq[h, :, :MLA_NOPE]
            wuq_ref[:, HW + MLA_ROPE * h:HW + MLA_ROPE * (h + 1)] = a_uq[h, :, MLA_NOPE:]
            wukv_ref[:, MLA_NOPE * h:MLA_NOPE * (h + 1)] = a_ukv[h, :, :MLA_NOPE]
            wukv_ref[:, HW + MLA_V * h:HW + MLA_V * (h + 1)] = a_ukv[h, :, MLA_NOPE:]

    vmem = pl.BlockSpec(memory_space=pltpu.VMEM)
    return _pcall(
        body, name="gather_weights",
        in_specs=[vmem] * n, out_specs=[vmem] * n,
        out_shape=[jax.ShapeDtypeStruct((D_MODEL, C_END), BF16), jax.ShapeDtypeStruct((Q_RANK, QW), BF16),
                   jax.ShapeDtypeStruct((KV_RANK, 2 * HW), BF16)],
        scratch_shapes=[pltpu.VMEM((N_DEV,) + s, BF16) for s in SHARD_SHAPES[:n]]
        + [pltpu.SemaphoreType.DMA((n, 8)), pltpu.SemaphoreType.DMA((n, 8))],
        compiler_params=_cparams(),
    )(*shards)


def _gather_w_out_rider(w_out):
    def copies(full_ref, stage, send_sems, recv_sems):
        me = _me()[3]
        out = []
        for k in range(1, N_DEV):
            peer, pidx = _peer(k)
            send = pltpu.make_async_remote_copy(
                src_ref=stage, dst_ref=full_ref.at[me], send_sem=send_sems.at[k - 1], recv_sem=recv_sems.at[k - 1],
                device_id=peer, device_id_type=MESH_ID)
            recv = pltpu.make_async_remote_copy(
                src_ref=stage, dst_ref=full_ref.at[pidx], send_sem=send_sems.at[k - 1], recv_sem=recv_sems.at[k - 1],
                device_id=peer, device_id_type=MESH_ID)
            out.append((send, recv))
        return out

    def start(ins, outs, scr):
        stage, send_sems, recv_sems, own_sem = scr
        stage[...] = ins[0][...].astype(BF16)
        pltpu.make_async_copy(stage, outs[0].at[_me()[3]], own_sem).start()
        for send, _ in copies(outs[0], stage, send_sems, recv_sems):
            send.start()

    def finish(ins, outs, scr):
        stage, send_sems, recv_sems, own_sem = scr
        pltpu.make_async_copy(stage, outs[0].at[_me()[3]], own_sem).wait()
        pairs = copies(outs[0], stage, send_sems, recv_sems)
        for _, recv in pairs:
            recv.wait_recv()
        for send, _ in pairs:
            send.wait_send()

    shape = SHARD_SHAPES[3]
    return Rider(
        args=[w_out], in_specs=[pl.BlockSpec(shape, lambda t, *_: (0, 0))],
        out_shape=[jax.ShapeDtypeStruct((N_DEV,) + shape, BF16)], out_specs=[pl.BlockSpec(memory_space=pl.ANY)],
        scratch=[pltpu.VMEM(shape, BF16), pltpu.SemaphoreType.DMA((N_DEV - 1,)), pltpu.SemaphoreType.DMA((N_DEV - 1,)),
                 pltpu.SemaphoreType.DMA],
        start=start, finish=finish)


def _scatter_g_out_rider(blocks):
    def copies(src_ref, dst_ref, send_sems, recv_sems):
        out = []
        for k in range(1, N_DEV):
            peer, pidx = _peer(k)
            out.append(pltpu.make_async_remote_copy(
                src_ref=src_ref.at[pidx], dst_ref=dst_ref.at[k], send_sem=send_sems.at[k - 1],
                recv_sem=recv_sems.at[k - 1], device_id=peer, device_id_type=MESH_ID))
        return out

    def start(ins, outs, scr):
        send_sems, recv_sems, own_sem = scr
        pltpu.make_async_copy(ins[0].at[_me()[3]], outs[0].at[0], own_sem).start()
        for cp in copies(ins[0], outs[0], send_sems, recv_sems):
            cp.start()

    def finish(ins, outs, scr):
        send_sems, recv_sems, own_sem = scr
        pltpu.make_async_copy(ins[0].at[_me()[3]], outs[0].at[0], own_sem).wait()
        for cp in copies(ins[0], outs[0], send_sems, recv_sems):
            cp.wait()

    hbm = pl.BlockSpec(memory_space=pl.ANY)
    return Rider(
        args=[blocks], in_specs=[hbm], out_shape=[jax.ShapeDtypeStruct(blocks.shape, blocks.dtype)], out_specs=[hbm],
        scratch=[pltpu.SemaphoreType.DMA((N_DEV - 1,)), pltpu.SemaphoreType.DMA((N_DEV - 1,)), pltpu.SemaphoreType.DMA],
        start=start, finish=finish)


def _adamw(w, g, m, v):
    m = ADAM_B1 * m + (1.0 - ADAM_B1) * g
    v = ADAM_B2 * v + (1.0 - ADAM_B2) * jnp.square(g)
    m_hat = m / (1.0 - ADAM_B1 ** ADAM_STEP)
    v_hat = v / (1.0 - ADAM_B2 ** ADAM_STEP)
    delta = -ADAM_LR * (m_hat / (jnp.sqrt(v_hat) + ADAM_EPS) + ADAM_WD * w)
    return delta, m, v


def _chunk_rows(count, cols):
    rows = (16 * 8 * LANES) // (-(-cols // LANES) * LANES)
    while count % rows:
        rows //= 2
    return rows


SMALL_ROWS = (0, 1, 3, 4)
LOSS_ROW = 2


def _reduce_grads_rider(grads3, arrived, small_parts, steps):
    n = len(grads3)
    shapes = [tuple(g.shape[1:]) for g in grads3]
    with_small = arrived is not None
    first_round, second_round = steps

    class Refs:
        def __init__(self, ins, outs, scr):
            self.g3, self.gsum = ins[0:n], outs[0:n]
            if with_small:
                self.arr, self.sp_wide, self.sp_q = ins[n:n + 3]
                self.gsum_out, self.ssum = outs[n:n + 2]
            self.own, self.sib, self.part = scr[0:n], scr[n:2 * n], scr[2 * n:3 * n]
            self.in_a, self.out_b, self.in_b = scr[3 * n:4 * n], scr[4 * n:5 * n], scr[5 * n:6 * n]
            self.rsmall = scr[6 * n]
            (self.loc_sems, self.d2d_send, self.d2d_recv, self.a_send, self.a_recv, self.b_send, self.b_recv,
             self.sm_send, self.sm_recv) = scr[6 * n + 1:]
            self.x, self.y, self.c, self.me = _me()
            self.along_x, self.along_y = (1 - self.x, self.y), (self.x, 1 - self.y)
            self.across = (1 - self.x, 1 - self.y)

        def small(self):
            if not with_small:
                return []
            return [pltpu.make_async_remote_copy(
                src_ref=self.rsmall.at[0], dst_ref=self.rsmall.at[k], send_sem=self.sm_send.at[k - 1],
                recv_sem=self.sm_recv.at[k - 1], device_id=_peer(k)[0], device_id_type=MESH_ID)
                for k in range(1, N_DEV)]

        def level1(self):
            local, to_sib = [], []
            for t in range(n):
                for q in range(4):
                    local.append(pltpu.make_async_copy(
                        self.g3[t].at[2 * q + self.c], self.own[t].at[q], self.loc_sems.at[t, q]))
                    to_sib.append(pltpu.make_async_remote_copy(
                        src_ref=self.g3[t].at[2 * q + 1 - self.c], dst_ref=self.sib[t].at[q],
                        send_sem=self.d2d_send.at[t, q], recv_sem=self.d2d_recv.at[t, q],
                        device_id=(self.x, self.y, 1 - self.c), device_id_type=MESH_ID))
            return local, to_sib

        def round_a(self):
            out = []
            for t in range(n):
                half = shapes[t][0] // 2
                for k, (to, chip, h) in enumerate([(self.along_x, self.along_x, 0), (self.along_x, self.across, 0),
                                                   (self.along_y, self.along_y, 1), (self.along_y, self.across, 1)]):
                    out.append(pltpu.make_async_remote_copy(
                        src_ref=self.part[t].at[2 * chip[0] + chip[1], pl.ds(h * half, half), :],
                        dst_ref=self.in_a[t].at[k], send_sem=self.a_send.at[t, k], recv_sem=self.a_recv.at[t, k],
                        device_id=(*to, self.c), device_id_type=MESH_ID))
            return out

        def round_b(self):
            return [pltpu.make_async_remote_copy(
                src_ref=self.out_b[t].at[k], dst_ref=self.in_b[t].at[k], send_sem=self.b_send.at[t, k],
                recv_sem=self.b_recv.at[t, k], device_id=(*to, self.c), device_id_type=MESH_ID)
                for t in range(n) for k, to in enumerate([self.along_y, self.along_x])]

    def chunks(rows, count, fn):
        def step(i, carry):
            fn(pl.multiple_of(i * rows, rows))
            return carry

        lax.fori_loop(0, count // rows, step, 0)

    def start(*refs):
        r = Refs(*refs)
        if with_small:
            r.rsmall[0] = r.sp_wide[...]
            for k, row in enumerate(SMALL_ROWS[2:]):
                r.rsmall[0, row:row + 1, 0:Q_RANK] = r.sp_q[k:k + 1, :]
        local, to_sib = r.level1()
        for cp in r.small() + local + to_sib:
            cp.start()

    def begin_rounds(*refs):
        r = Refs(*refs)
        local, to_sib = r.level1()
        for cp in local:
            cp.wait()
        for cp in to_sib:
            cp.wait_recv()
        my_chip = 2 * r.x + r.y
        for t in range(n):
            rows = _chunk_rows(*shapes[t])

            def pair_sums(at, t=t, rows=rows):
                sl = pl.ds(at, rows)
                for q in range(4):
                    r.part[t][q, sl, :] = (r.own[t][q, sl, :].astype(F32) + r.sib[t][q, sl, :].astype(F32)).astype(BF16)
                r.gsum[t][sl, :] = r.own[t][my_chip, sl, :].astype(F32) + r.sib[t][my_chip, sl, :].astype(F32)

            chunks(rows, shapes[t][0], pair_sums)
        for cp in r.round_a():
            cp.start()
        if not with_small:
            return
        rows_out = _chunk_rows(*SHARD_SHAPES[3])

        def add_arrived(at):
            sl = pl.ds(at, rows_out)
            g = r.arr[0, sl, :].astype(F32)
            for k in range(1, N_DEV):
                g = g + r.arr[k, sl, :].astype(F32)
            r.gsum_out[sl, :] = g

        chunks(rows_out, SHARD_SHAPES[3][0], add_arrived)

    def pass_on(*refs):
        r = Refs(*refs)
        for cp in r.round_a():
            cp.wait_recv()
        q_x, q_y = 2 * r.along_x[0] + r.along_x[1], 2 * r.along_y[0] + r.along_y[1]
        for t in range(n):
            half = shapes[t][0] // 2
            rows = _chunk_rows(half, shapes[t][1])

            def add(at, t=t, rows=rows, half=half):
                lo, hi = pl.ds(at, rows), pl.ds(half + at, rows)
                r.gsum[t][lo, :] = r.gsum[t][lo, :] + r.in_a[t][0, lo, :].astype(F32)
                r.out_b[t][0, lo, :] = (r.part[t][q_y, lo, :].astype(F32) + r.in_a[t][1, lo, :].astype(F32)).astype(BF16)
                r.gsum[t][hi, :] = r.gsum[t][hi, :] + r.in_a[t][2, lo, :].astype(F32)
                r.out_b[t][1, lo, :] = (r.part[t][q_x, hi, :].astype(F32) + r.in_a[t][3, lo, :].astype(F32)).astype(BF16)

            chunks(rows, half, add)
        for cp in r.round_b():
            cp.start()

    def finish(*refs):
        r = Refs(*refs)
        passed = r.round_b()
        for cp in passed:
            cp.wait_recv()
        for t in range(n):
            half = shapes[t][0] // 2
            rows = _chunk_rows(half, shapes[t][1])

            def add(at, t=t, rows=rows, half=half):
                lo, hi = pl.ds(at, rows), pl.ds(half + at, rows)
                r.gsum[t][lo, :] = r.gsum[t][lo, :] + r.in_b[t][0, lo, :].astype(F32)
                r.gsum[t][hi, :] = r.gsum[t][hi, :] + r.in_b[t][1, lo, :].astype(F32)

            chunks(rows, half, add)
        small = r.small()
        for cp in small:
            cp.wait_recv()
        if with_small:
            tot = r.rsmall[r.me]
            for d in range(1, N_DEV):
                tot = tot + r.rsmall[jnp.bitwise_xor(r.me, d)]
            r.ssum[...] = tot
        for cp in small + r.level1()[1] + r.round_a() + passed:
            cp.wait_send()

    hbm = pl.BlockSpec(memory_space=pl.ANY)
    dma = pltpu.SemaphoreType.DMA

    def whole(shape):
        return pl.BlockSpec(shape, lambda i: (0,) * len(shape))

    def halves(slots):
        return [pltpu.VMEM((slots, s[0] // 2, s[1]), BF16) for s in shapes]

    small_args = [arrived, *small_parts] if with_small else []
    out_shapes = shapes + ([SHARD_SHAPES[3], (8, D_MODEL)] if with_small else [])
    return Rider(
        args=list(grads3) + small_args,
        in_specs=[hbm] * n + [whole(a.shape) for a in small_args],
        out_shape=[jax.ShapeDtypeStruct(s, F32) for s in out_shapes], out_specs=[whole(s) for s in out_shapes],
        scratch=[pltpu.VMEM((4,) + s, BF16) for _ in range(3) for s in shapes]
        + halves(4) + halves(2) + halves(2)
        + [pltpu.VMEM((N_DEV, 8, D_MODEL), F32), dma((n, 4)), dma((n, 4)), dma((n, 4)), dma((n, 4)), dma((n, 4)),
           dma((n, 2)), dma((n, 2)), dma((N_DEV - 1,)), dma((N_DEV - 1,))],
        start=start, finish=finish, stages=((first_round, begin_rounds), (second_round, pass_on)))


def _adamw_update(grads, small_grad, wmv, small_wmv):
    n_small = len(small_wmv)

    def body(*refs):
        g_refs, sg_ref = refs[0:4], refs[4]
        wmv_refs = [refs[5 + 3 * t:8 + 3 * t] for t in range(4)]
        swmv_refs = [refs[17 + 3 * t:20 + 3 * t] for t in range(n_small)]
        outs = refs[17 + 3 * n_small:]
        out_refs = [outs[4 * t:4 * t + 4] for t in range(4)]
        sout_refs = [outs[16 + 4 * t:20 + 4 * t] for t in range(n_small)]
        loss_ref = outs[16 + 4 * n_small]
        for t, (w_ref, m_ref, v_ref) in enumerate(swmv_refs):
            g = sg_ref[SMALL_ROWS[t]:SMALL_ROWS[t] + 1, :w_ref.shape[1]]
            delta, m, v = _adamw(w_ref[...], g, m_ref[...], v_ref[...])
            sout_refs[t][0][...], sout_refs[t][1][...], sout_refs[t][2][...], sout_refs[t][3][...] = g, delta, m, v
        loss_ref[...] = (0.5 / D_MODEL) * jnp.sum(sg_ref[LOSS_ROW:LOSS_ROW + 1, :], axis=1, keepdims=True)
        for t in range(4):
            rows = ADAM_ROWS[t]
            w_ref, m_ref, v_ref = wmv_refs[t]
            g_out, d_out, m_out, v_out = out_refs[t]

            def step(i, carry, g_ref=g_refs[t], rows=rows, w_ref=w_ref, m_ref=m_ref, v_ref=v_ref,
                     g_out=g_out, d_out=d_out, m_out=m_out, v_out=v_out):
                r = pl.ds(pl.multiple_of(i * rows, rows), rows)
                g = g_ref[r, :]
                delta, m, v = _adamw(w_ref[r, :], g, m_ref[r, :], v_ref[r, :])
                g_out[r, :], d_out[r, :], m_out[r, :], v_out[r, :] = g, delta, m, v
                return carry

            lax.fori_loop(0, SHARD_SHAPES[t][0] // rows, step, 0)

    vmem = pl.BlockSpec(memory_space=pltpu.VMEM)
    flat_wmv = [a for trio in wmv for a in trio]
    flat_small = [a for trio in small_wmv for a in trio]
    out_shape = ([jax.ShapeDtypeStruct(s, F32) for s in SHARD_SHAPES for _ in range(4)]
                 + [jax.ShapeDtypeStruct(trio[0].shape, F32) for trio in small_wmv for _ in range(4)]
                 + [jax.ShapeDtypeStruct((1, 1), F32)])
    return _pcall(
        body, name="adamw",
        in_specs=[vmem] * (5 + len(flat_wmv) + len(flat_small)), out_specs=[vmem] * len(out_shape),
        out_shape=out_shape,
        compiler_params=_cparams(),
    )(*grads, small_grad, *flat_wmv, *flat_small)


def kernel(x, w_in, q_norm_g, kv_norm_g, w_uq, w_ukv, w_out, ln_g, ln_b, loss_target, m_w_in, m_q_norm_g, m_kv_norm_g, m_w_uq, m_w_ukv, m_w_out, m_ln_g, m_ln_b, v_w_in, v_q_norm_g, v_kv_norm_g, v_w_uq, v_w_ukv, v_w_out, v_ln_g, v_ln_b):
    w_in_r, w_uq_r, w_ukv_r = _all_gather_weights([w_in, w_uq, w_ukv])
    grad_x, sums = _local_step(
        x[0], loss_target[0], w_in_r, w_uq_r, w_ukv_r, _gather_w_out_rider(w_out), _scatter_g_out_rider,
        _reduce_grads_rider, q_norm_g, kv_norm_g, ln_g, ln_b)
    row = lambda a: a.reshape(1, -1)
    small_wmv = [(row(ln_g), row(m_ln_g), row(v_ln_g)), (row(ln_b), row(m_ln_b), row(v_ln_b)),
                 (row(q_norm_g), row(m_q_norm_g), row(v_q_norm_g)), (row(kv_norm_g), row(m_kv_norm_g), row(v_kv_norm_g))]
    wmv = [(w_in, m_w_in, v_w_in), (w_uq, m_w_uq, v_w_uq), (w_ukv, m_w_ukv, v_w_ukv), (w_out, m_w_out, v_w_out)]
    res = _adamw_update(sums[:4], sums[4], wmv, small_wmv)
    big = [res[4 * t:4 * t + 4] for t in range(4)]
    small = [[a.reshape(-1) for a in res[16 + 4 * t:20 + 4 * t]] for t in range(4)]
    loss = res[32].reshape(())

    def group(kind):
        return (big[0][kind], small[2][kind], small[3][kind], big[1][kind], big[2][kind], big[3][kind],
                small[0][kind], small[1][kind])

    return (loss, grad_x[None], *group(0), *group(1), *group(2), *group(3))
```

```python
import functools
from typing import Callable, NamedTuple

import numpy as np
import jax
import jax.numpy as jnp
from jax import lax
from jax.experimental import pallas as pl
from jax.experimental.pallas import tpu as pltpu

F32 = jnp.float32
BF16 = jnp.bfloat16

D_MODEL = 1024
ROPE_THETA = 500000.0
NEG = -1e30
RMS_EPS = 1e-6
LN_EPS = 1e-5
HEADS = 8
MLA_NOPE = 64
MLA_ROPE = 32
MLA_V = 64
Q_RANK = 384
KV_RANK = 256
DIL_HEAD = 64
DIL_ROT = 16
DIL_CONFIGS = ((128, 1), (512, 4), (2048, 16))
DIL_NEAR = 512
HW = HEADS * 64
QW = HW + HEADS * MLA_ROPE
IN_SPLITS = (Q_RANK, KV_RANK, MLA_ROPE, HW, HW, HW, HW, HW)
IN_WIDTH = sum(IN_SPLITS)
ALPHA = 2.0 ** 0.25
MLA_SCALE = (MLA_NOPE + MLA_ROPE) ** -0.5
DIL_SCALE = DIL_HEAD ** -0.5
LOG2E = 1.4426950408889634
LN2 = 0.6931471805599453

ADAM_LR = 0.001
ADAM_B1 = 0.9
ADAM_B2 = 0.999
ADAM_EPS = 1e-08
ADAM_WD = 0.01
ADAM_STEP = 10

N_DEV = 8
LANES = 128
VMEM_LIMIT = 56 * 1024 * 1024
BLOCK_TOKENS = 512
BLOCK_MLA = 512
BLOCK_DIL = 512
GRAD_W_IN_CUT = 384

C_CQ, C_CKV, C_KR, C_GA, C_QB, C_KB, C_VB, C_GB, C_END = 0, 384, 640, 768, 1280, 1792, 2304, 2816, 3328

NT = (((1,), (1,)), ((), ()))
TN = (((0,), (0,)), ((), ()))


def _pcall(body, **kw):
    return pl.pallas_call(body, **kw)


def _cparams(**kw):
    return pltpu.CompilerParams(vmem_limit_bytes=VMEM_LIMIT, **kw)


def _rope_tables(seq):
    def tabs(dim, period):
        half = dim // 2
        inv = np.float32(ROPE_THETA) ** (-np.arange(0, dim, 2, dtype=np.float32) / np.float32(dim))
        ang = np.arange(seq, dtype=np.float32)[:, None] * inv.astype(np.float32)[None, :]
        cos, sin = np.cos(ang).astype(np.float32), np.sin(ang).astype(np.float32)
        j = np.arange(LANES) % period
        f = j % half
        c = np.where(j < dim, cos[:, f], np.float32(1.0))
        s1 = np.where(j < half, -sin[:, f], np.float32(0.0))
        s2 = np.where((j >= half) & (j < dim), sin[:, f], np.float32(0.0))
        return [c, s1, s2]
    return np.stack(tabs(MLA_ROPE, MLA_ROPE) + tabs(DIL_ROT, DIL_HEAD)).astype(np.float32)


def _rope(t, c, s1, s2, half):
    return t * c + pltpu.roll(t, LANES - half, 1) * s1 + pltpu.roll(t, half, 1) * s2


def _rope_t(d, c, s1, s2, half):
    return d * c + pltpu.roll(d * s1, half, 1) + pltpu.roll(d * s2, LANES - half, 1)


def _rope_wide(fn, t, c, s1, s2, half):
    return jnp.concatenate(
        [fn(t[:, i:i + LANES], c, s1, s2, half) for i in range(0, t.shape[1], LANES)], axis=1)


def _mla_bias_t(blk):
    a = np.arange(blk)
    causal = np.where(a[:, None] <= a[None, :], 0.0, NEG)
    return np.stack([np.zeros((blk, blk)), causal]).astype(np.float32)


def _dil_bias_t(blk, reach):
    a = np.arange(blk)
    out = []
    for off in range(-(-reach // blk) + 1):
        delta = blk * off + a[None, :] - a[:, None]
        mult = np.zeros((blk, blk))
        for window, dil in DIL_CONFIGS:
            mult += (delta >= 0) & (delta % dil == 0) & (delta <= min(window, reach))
        out.append(np.where(mult > 0, np.log2(np.maximum(mult, 1.0)), NEG))
    return np.stack(out).astype(np.float32)


def _dil_far_bias_t(length):
    window, dil = DIL_CONFIGS[-1]
    a = np.arange(length)
    steps_back = a[None, :] - a[:, None]
    seen = (steps_back * dil > DIL_NEAR) & (steps_back * dil <= window)
    return np.where(seen, 0.0, NEG).astype(np.float32)[None]


def _lanes_to_classes(a, dil):
    h, s = a.shape
    return a.reshape(h, s // dil, dil).transpose(0, 2, 1).reshape(h, s)


def _lanes_from_classes(a, dil):
    h, s = a.shape
    return a.reshape(h, dil, s // dil).transpose(0, 2, 1).reshape(h, s)


def _steps(nq, span, by_key, diag_only_bias):
    rows = []
    if by_key:
        for ki in range(nq):
            hi = min(nq - 1, ki + span)
            for qi in range(ki, hi + 1):
                rows.append((qi, ki, int(qi == ki), int(qi == hi)))
    else:
        for qi in range(nq):
            lo = max(0, qi - span)
            for ki in range(lo, qi + 1):
                rows.append((qi, ki, int(ki == lo), int(ki == qi)))
    arr = np.array(rows, dtype=np.int32)
    off = arr[:, 0] - arr[:, 1]
    bias_idx = (off == 0).astype(np.int32) if diag_only_bias else off.astype(np.int32)
    return [jnp.asarray(v) for v in (arr[:, 0], arr[:, 1], bias_idx, arr[:, 2], arr[:, 3])]


def _by_class(val, out_ref, lanes_sc):
    n_cls, per = out_ref.shape[0], out_ref.shape[1]
    for c in range(val.shape[1] // LANES):
        lanes_sc[c] = val[:, LANES * c:LANES * (c + 1)]
        for r in range(n_cls):
            rows = lanes_sc.at[c][pl.ds(r, per, stride=n_cls), :]
            out_ref[r, :, LANES * c:LANES * (c + 1)] = rows.astype(out_ref.dtype)


def _in_sequence(ref, lanes_sc):
    n_cls, per, width = ref.shape
    for c in range(width // LANES):
        for r in range(n_cls):
            lanes_sc.at[c][pl.ds(r, per, stride=n_cls), :] = ref[r, :, LANES * c:LANES * (c + 1)].astype(F32)
    return jnp.concatenate([lanes_sc[c] for c in range(width // LANES)], axis=1)


def _fwd_proj(x, w_in_r, w_uq_r, w_ukv_r, qg, kvg, tabs, bt, n_cls):
    seq = x.shape[0]

    def body(x_ref, win_ref, wuq_ref, wukv_ref, qg_ref, kvg_ref, tab_ref,
             cq_ref, ckv_ref, qn_ref, kvn_ref, qcat_ref, kn_ref, kpe_ref, v_ref,
             ga_ref, gb_ref, qb_ref, kb_ref, vb_ref, knt_ref, kpet_ref, vt_ref, kbt_ref, vbt_ref,
             qbc_ref, kbc_ref, vbc_ref, lanes_sc):
        xb = x_ref[...].astype(BF16)

        def proj(lo, hi):
            return jnp.dot(xb, win_ref[:, lo:hi], preferred_element_type=F32)

        m_tabs = (tab_ref[0], tab_ref[1], tab_ref[2])
        d_tabs = (tab_ref[3], tab_ref[4], tab_ref[5])

        def use_cq(cq):
            cq_ref[...] = cq
            qn = (cq * lax.rsqrt(jnp.mean(cq * cq, axis=1, keepdims=True) + RMS_EPS) * qg_ref[...]).astype(BF16)
            qn_ref[...] = qn
            q = jnp.dot(qn, wuq_ref[...], preferred_element_type=F32)
            qcat_ref[:, :HW] = (q[:, :HW] * (MLA_SCALE * LOG2E)).astype(BF16)
            qcat_ref[:, HW:] = (
                _rope_wide(_rope, q[:, HW:], *m_tabs, MLA_ROPE // 2) * (MLA_SCALE * LOG2E)).astype(BF16)

        def use_ckv(ckv):
            ckv_ref[...] = ckv
            kvn = (ckv * lax.rsqrt(jnp.mean(ckv * ckv, axis=1, keepdims=True) + RMS_EPS) * kvg_ref[...]).astype(BF16)
            kvn_ref[...] = kvn
            kv = jnp.dot(kvn, wukv_ref[...], preferred_element_type=F32)
            kn_ref[...] = kv[:, :HW].astype(BF16)
            v_ref[...] = kv[:, HW:].astype(BF16)
            knt_ref[...] = kv[:, :HW].T.astype(BF16)
            vt_ref[...] = kv[:, HW:].T.astype(BF16)

        def use_kr(kr):
            kpe = _rope(kr, *m_tabs, MLA_ROPE // 2)
            kpe_ref[...] = kpe.astype(BF16)
            kpet_ref[...] = kpe.T[:MLA_ROPE, :].astype(BF16)

        def use_ga(ga):
            ga_ref[...] = ga

        def use_qb(qb):
            qb = _rope_wide(_rope, qb, *d_tabs, DIL_ROT // 2) * (DIL_SCALE * LOG2E)
            qb_ref[...] = qb.astype(BF16)
            _by_class(qb, qbc_ref, lanes_sc)

        def use_kb(kb):
            kb = _rope_wide(_rope, kb, *d_tabs, DIL_ROT // 2)
            kb_ref[...] = kb.astype(BF16)
            kbt_ref[...] = kb.T.astype(BF16)
            _by_class(kb, kbc_ref, lanes_sc)

        def use_vb(vb):
            vb_ref[...] = vb.astype(BF16)
            vbt_ref[...] = vb.T.astype(BF16)
            _by_class(vb, vbc_ref, lanes_sc)

        def use_gb(gb):
            gb_ref[...] = gb

        pieces = [(C_CQ, C_CKV, use_cq), (C_CKV, C_KR, use_ckv), (C_KR, C_GA, use_kr), (C_GA, C_QB, use_ga),
                  (C_QB, C_KB, use_qb), (C_KB, C_VB, use_kb), (C_VB, C_GB, use_vb), (C_GB, C_END, use_gb)]
        ahead = proj(*pieces[0][:2])
        for n, (_, _, use) in enumerate(pieces):
            cur = ahead
            if n + 1 < len(pieces):
                ahead = proj(*pieces[n + 1][:2])
            use(cur)

    def tok(width):
        return pl.BlockSpec((bt, width), lambda i: (i, 0))

    def tok_t(height):
        return pl.BlockSpec((height, bt), lambda i: (0, i))

    def full(a):
        return pl.BlockSpec(a.shape, lambda i: (0,) * a.ndim)

    outs = [(Q_RANK, F32), (KV_RANK, F32), (Q_RANK, BF16), (KV_RANK, BF16), (QW, BF16), (HW, BF16),
            (LANES, BF16), (HW, BF16), (HW, F32), (HW, F32), (HW, BF16), (HW, BF16), (HW, BF16)]
    outs_t = [HW, MLA_ROPE, HW, HW, HW]
    by_class = pl.BlockSpec((n_cls, bt // n_cls, HW), lambda i: (0, i, 0))
    return _pcall(
        body, name="fwd_proj", grid=(seq // bt,),
        in_specs=[tok(D_MODEL), full(w_in_r), full(w_uq_r), full(w_ukv_r), full(qg), full(kvg),
                  pl.BlockSpec((6, bt, LANES), lambda i: (0, i, 0))],
        out_specs=[tok(w) for w, _ in outs] + [tok_t(h) for h in outs_t] + [by_class] * 3,
        out_shape=[jax.ShapeDtypeStruct((seq, w), dt) for w, dt in outs]
        + [jax.ShapeDtypeStruct((h, seq), BF16) for h in outs_t]
        + [jax.ShapeDtypeStruct((n_cls, seq // n_cls, HW), BF16)] * 3,
        scratch_shapes=[pltpu.VMEM((HW // LANES, bt, LANES), F32)],
        compiler_params=_cparams(dimension_semantics=("arbitrary",)),
    )(x, w_in_r, w_uq_r, w_ukv_r, qg, kvg, tabs)


def _head_masks(lane, h):
    e, g = h % 2, h % 4
    me = (lane >= 64 * e) & (lane < 64 * e + 64)
    mr = (lane >= 32 * g) & (lane < 32 * g + 32)
    return me, mr


def _masked(mask, a):
    return jnp.where(mask, a, jnp.zeros_like(a))


def _pair_operands(q_ref, k_ref, kpe_ref, lane, j, ks=slice(None), qs=slice(None)):
    cols = slice(LANES * j, LANES * (j + 1))
    qc = q_ref[qs, cols]
    kj = k_ref[ks, cols]
    kes = []
    for h in (2 * j, 2 * j + 1):
        me, mr = _head_masks(lane, h)
        ke = _masked(me, kj)
        if kpe_ref is not None:
            ke = jnp.concatenate([ke, _masked(mr, kpe_ref[ks, :])], axis=1)
        kes.append(ke)
    if kpe_ref is not None:
        qc = jnp.concatenate([qc, q_ref[qs, HW + LANES * (j // 2):HW + LANES * (j // 2 + 1)]], axis=1)
    return qc, kes


def _tile_variants(bias_t):
    out = {}
    for i, tile in enumerate(np.asarray(bias_t)):
        h = tile.shape[0] // 2
        skip = 1 if (tile[h:, :h] == NEG).all() else 2 if (tile[:h, h:] == NEG).all() else 0
        out[i] = (bool((tile != 0).any()), skip)
    return out


def _tile_parts(blk, skip):
    lo, hi, full = slice(0, blk // 2), slice(blk // 2, blk), slice(0, blk)
    return {0: [(full, full)], 1: [(lo, full), (hi, hi)], 2: [(hi, full), (lo, lo)]}[skip]


class Rider(NamedTuple):
    args: list
    in_specs: list
    out_shape: list
    out_specs: list
    scratch: list
    start: Callable
    finish: Callable
    stages: tuple = ()
    start_step: int = 0
    shared: int = 0


def _ride_along(body, ride, n_prefetch, n_in, n_out, n_scratch, n_steps):
    if ride is None:
        return body

    def wrapped(*refs):
        pre, rest = refs[:n_prefetch], refs[n_prefetch:]
        a = n_in
        b = a + len(ride.args)
        c = b + n_out
        d = c + len(ride.out_shape)
        e = d + n_scratch
        mine = (rest[a:b], rest[c:d], rest[e:])
        t = pl.program_id(0)
        pl.when(t == ride.start_step)(lambda: ride.start(*mine))
        for at, stage in ride.stages:
            pl.when(t == at)(functools.partial(stage, *mine))
        body(*pre, *rest[:a], *rest[b:c], *rest[d:e + ride.shared])
        pl.when(t == n_steps - 1)(lambda: ride.finish(*mine))

    return wrapped


def _attn_fwd(name, q, k, kpe, vt, bias_t, steps, blk, ride=None, v_token_major=False):
    seq = q.shape[0]
    mla = kpe is not None
    n_steps = int(steps[0].shape[0])
    variants = _tile_variants(bias_t)

    def body(qi_r, ki_r, bi_r, fi_r, la_r, *refs):
        if mla:
            q_ref, k_ref, kpe_ref, vt_ref, b_ref, o_ref, lse_ref, m_sc, l_sc, acc_sc, st_sc = refs
        else:
            q_ref, k_ref, vt_ref, b_ref, o_ref, lse_ref, m_sc, l_sc, acc_sc, st_sc = refs
        t = pl.program_id(0)

        @pl.when(fi_r[t] == 1)
        def _():
            m_sc[...] = jnp.full(m_sc.shape, NEG, F32)
            l_sc[...] = jnp.zeros(l_sc.shape, F32)
            acc_sc[...] = jnp.zeros(acc_sc.shape, F32)

        lane = lax.broadcasted_iota(jnp.int32, (1, LANES), 1)
        if v_token_major:
            vt_all = vt_ref[...].astype(F32).T.astype(BF16)
            vt_rows = lambda rows, ks: vt_all[rows, ks]
        else:
            vt_rows = lambda rows, ks: vt_ref[rows, ks]

        def tile_pass(ks, qs, with_bias):
            nk, nq = ks.stop - ks.start, qs.stop - qs.start
            ones = jnp.ones((16, nk), BF16)

            def pair_scores(j):
                qc, kes = _pair_operands(q_ref, k_ref, kpe_ref if mla else None, lane, j, ks, qs)
                st = lax.dot_general(jnp.concatenate(kes, axis=0), qc, NT, preferred_element_type=F32)
                maxes = []
                for e in range(2):
                    se = st[e * nk:(e + 1) * nk]
                    if with_bias:
                        se = se + b_ref[0, ks, qs]
                    st_sc[j % 2, e * nk:(e + 1) * nk, 0:nq] = se
                    maxes.append(jnp.max(se, axis=0, keepdims=True))
                return maxes

            def softmax_pv(h, col_max):
                st = st_sc[(h // 2) % 2, (h % 2) * nk:(h % 2 + 1) * nk, 0:nq]
                hrow = slice(h, h + 1)
                m_prev = m_sc[hrow, qs]
                m_new = jnp.maximum(m_prev, col_max)
                alpha = jnp.exp2(m_prev - m_new)
                pt = jnp.exp2(st - m_new).astype(BF16)
                m_sc[hrow, qs] = m_new
                rows = slice(64 * h, 64 * h + 64)
                res = jnp.dot(jnp.concatenate([vt_rows(rows, ks), ones], axis=0), pt, preferred_element_type=F32)
                acc_sc[rows, qs] = alpha * acc_sc[rows, qs] + res[:64]
                l_sc[hrow, qs] = alpha * l_sc[hrow, qs] + res[64:65]

            maxes = pair_scores(0)
            for j in range(HEADS // 2):
                cur = maxes
                if j + 1 < HEADS // 2:
                    maxes = pair_scores(j + 1)
                softmax_pv(2 * j, cur[0])
                softmax_pv(2 * j + 1, cur[1])

        def step(with_bias, skip):
            for ks, qs in _tile_parts(blk, skip):
                tile_pass(ks, qs, with_bias)

        for idx, (with_bias, skip) in variants.items():
            if len(variants) == 1:
                step(with_bias, skip)
            else:
                pl.when(bi_r[t] == idx)(functools.partial(step, with_bias, skip))

        @pl.when(la_r[t] == 1)
        def _():
            for h in range(HEADS):
                rows = slice(64 * h, 64 * h + 64)
                acc_sc[rows, :] = acc_sc[rows, :] / l_sc[h:h + 1, :]
            o_ref[...] = acc_sc[...].T
            lse_ref[...] = m_sc[...] + jnp.log2(l_sc[...])

    qmap = lambda t, qi, ki, bi, fi, la: (qi[t], 0)
    kmap = lambda t, qi, ki, bi, fi, la: (ki[t], 0)
    in_specs = [pl.BlockSpec((blk, q.shape[1]), qmap), pl.BlockSpec((blk, HW), kmap)]
    args = [q, k]
    if mla:
        in_specs.append(pl.BlockSpec((blk, LANES), kmap))
        args.append(kpe)
    in_specs += [pl.BlockSpec((blk, HW), kmap) if v_token_major else
                 pl.BlockSpec((HW, blk), lambda t, qi, ki, bi, fi, la: (0, ki[t])),
                 pl.BlockSpec((1, blk, blk), lambda t, qi, ki, bi, fi, la: (bi[t], 0, 0))]
    args += [vt, jnp.asarray(bias_t)]
    out_specs = [pl.BlockSpec((blk, HW), qmap), pl.BlockSpec((HEADS, blk), lambda t, qi, ki, bi, fi, la: (0, qi[t]))]
    out_shape = [jax.ShapeDtypeStruct((seq, HW), F32), jax.ShapeDtypeStruct((HEADS, seq), F32)]
    scratch = [pltpu.VMEM((HEADS, blk), F32), pltpu.VMEM((HEADS, blk), F32),
               pltpu.VMEM((HW, blk), F32), pltpu.VMEM((2, 2 * blk, blk), F32)]
    body = _ride_along(body, ride, 5, len(args), len(out_shape), len(scratch), n_steps)
    if ride is not None:
        args, in_specs = args + ride.args, in_specs + ride.in_specs
        out_specs, out_shape, scratch = out_specs + ride.out_specs, out_shape + ride.out_shape, scratch + ride.scratch
    return _pcall(
        body, name=name,
        grid_spec=pltpu.PrefetchScalarGridSpec(
            num_scalar_prefetch=5, grid=(n_steps,), in_specs=in_specs, out_specs=out_specs, scratch_shapes=scratch),
        out_shape=out_shape,
        compiler_params=_cparams(dimension_semantics=("arbitrary",)),
    )(*steps, *args)


def _attn_bwd(name, q, k, kpe, v, kt, kpet, bias_t, do, lse, dstat, steps, blk, ride=None, single_visit=False):
    assert not (single_visit and kpe is not None) and (kt is not None or single_visit)
    seq = q.shape[0]
    mla = kpe is not None
    qw = q.shape[1]
    n_steps = int(steps[0].shape[0])
    dk_dtype = BF16 if mla else F32
    variants = _tile_variants(bias_t)

    def body(qi_r, ki_r, bi_r, fi_r, la_r, *refs):
        if mla:
            (q_ref, k_ref, kpe_ref, v_ref, kt_ref, kpet_ref, b_ref, do_ref, lse_ref, d_ref,
             dq_ref, dk_ref, dkpe_ref, dv_ref, dk_sc, dkpe_sc, dv_sc, st_sc, dpt_sc) = refs
        else:
            q_ref, k_ref, v_ref, *rest = refs
            kt_ref = rest.pop(0) if kt is not None else None
            b_ref, do_ref, lse_ref, d_ref, dq_out_ref, dk_ref, dv_ref, dk_sc, dv_sc, st_sc, dpt_sc, *rest = rest
            dq_ref = rest[0] if single_visit else dq_out_ref
        t = pl.program_id(0)

        @pl.when(jnp.logical_or(t == 0, single_visit))
        def _():
            dq_ref[...] = jnp.zeros(dq_ref.shape, F32)

        @pl.when(fi_r[t] == 1)
        def _():
            dk_sc[...] = jnp.zeros(dk_sc.shape, F32)
            dv_sc[...] = jnp.zeros(dv_sc.shape, F32)
            if mla:
                dkpe_sc[...] = jnp.zeros(dkpe_sc.shape, F32)

        qi = 0 if single_visit else qi_r[t]
        lane = lax.broadcasted_iota(jnp.int32, (1, LANES), 1)
        if kt is None:
            kt_all = k_ref[...].astype(F32).T.astype(BF16)
            kt_rows = lambda rows, ks: kt_all[rows, ks]
        else:
            kt_rows = lambda rows, ks: kt_ref[rows, ks]

        def tile_pass(ks, qs, with_bias):
            nk, nq = ks.stop - ks.start, qs.stop - qs.start

            def pair_matmuls(j):
                cols = slice(LANES * j, LANES * (j + 1))
                qc, kes = _pair_operands(q_ref, k_ref, kpe_ref if mla else None, lane, j, ks, qs)
                st_sc[j % 2, 0:2 * nk, 0:nq] = lax.dot_general(
                    jnp.concatenate(kes, axis=0), qc, NT, preferred_element_type=F32)
                vj = v_ref[ks, cols]
                ves = [_masked(_head_masks(lane, h)[0], vj) for h in (2 * j, 2 * j + 1)]
                dpt_sc[j % 2, 0:2 * nk, 0:nq] = lax.dot_general(
                    jnp.concatenate(ves, axis=0), do_ref[qs, cols], NT, preferred_element_type=F32)

            def pair_grads(j):
                cols = slice(LANES * j, LANES * (j + 1))
                qj, doj = q_ref[qs, cols], do_ref[qs, cols]
                if mla:
                    qr = q_ref[qs, HW + LANES * (j // 2):HW + LANES * (j // 2 + 1)]
                pts, dsts, qms, doms = [], [], [], []
                for e in range(2):
                    h = 2 * j + e
                    me, mr = _head_masks(lane, h)
                    st = st_sc[j % 2, e * nk:(e + 1) * nk, 0:nq]
                    if with_bias:
                        st = st + b_ref[0, ks, qs]
                    pt = jnp.exp2(st - lse_ref[h:h + 1, qs])
                    dst = (pt * (dpt_sc[j % 2, e * nk:(e + 1) * nk, 0:nq] - d_ref[h:h + 1, qs])).astype(BF16)
                    pts.append(pt.astype(BF16))
                    dsts.append(dst)
                    doms.append(_masked(me, doj))
                    qm = _masked(me, qj)
                    if mla:
                        qm = jnp.concatenate([qm, _masked(mr, qr)], axis=1)
                    qms.append(qm)
                    ktl = kt_rows(slice(64 * h, 64 * h + 64), ks)
                    if mla:
                        ktl = jnp.concatenate([ktl, kpet_ref[:, ks]], axis=0)
                    dqc = jnp.dot(ktl, dst, preferred_element_type=F32)
                    dq_ref[qi, 64 * h:64 * h + 64, qs] += dqc[:64]
                    if mla:
                        dq_ref[qi, HW + MLA_ROPE * h:HW + MLA_ROPE * (h + 1), qs] += dqc[64:]
                dv_sc[ks, cols] += jnp.dot(
                    jnp.concatenate(pts, axis=1), jnp.concatenate(doms, axis=0), preferred_element_type=F32)
                dkc = jnp.dot(jnp.concatenate(dsts, axis=1), jnp.concatenate(qms, axis=0), preferred_element_type=F32)
                dk_sc[ks, cols] += dkc[:, :LANES]
                if mla:
                    dkpe_sc[ks, :] += dkc[:, LANES:]

            pair_matmuls(0)
            for j in range(HEADS // 2):
                if j + 1 < HEADS // 2:
                    pair_matmuls(j + 1)
                pair_grads(j)

        def step(with_bias, skip):
            for ks, qs in _tile_parts(blk, skip):
                tile_pass(ks, qs, with_bias)

        for idx, (with_bias, skip) in variants.items():
            if len(variants) == 1:
                step(with_bias, skip)
            else:
                pl.when(bi_r[t] == idx)(functools.partial(step, with_bias, skip))
        if single_visit:
            dq_out_ref[...] = dq_ref[0].T

        @pl.when(la_r[t] == 1)
        def _():
            dk_ref[...] = (dk_sc[...] * LN2).astype(dk_ref.dtype)
            dv_ref[...] = dv_sc[...].astype(dv_ref.dtype)
            if mla:
                dkpe_ref[...] = dkpe_sc[...] * LN2

    qmap = lambda t, qi, ki, bi, fi, la: (qi[t], 0)
    kmap = lambda t, qi, ki, bi, fi, la: (ki[t], 0)
    qmap_t = lambda t, qi, ki, bi, fi, la: (0, qi[t])
    kmap_t = lambda t, qi, ki, bi, fi, la: (0, ki[t])
    in_specs = [pl.BlockSpec((blk, qw), qmap), pl.BlockSpec((blk, HW), kmap)]
    args = [q, k]
    if mla:
        in_specs.append(pl.BlockSpec((blk, LANES), kmap))
        args.append(kpe)
    in_specs.append(pl.BlockSpec((blk, HW), kmap))
    args.append(v)
    if kt is not None:
        in_specs.append(pl.BlockSpec((HW, blk), kmap_t))
        args.append(kt)
    if mla:
        in_specs.append(pl.BlockSpec((MLA_ROPE, blk), kmap_t))
        args.append(kpet)
    in_specs += [pl.BlockSpec((1, blk, blk), lambda t, qi, ki, bi, fi, la: (bi[t], 0, 0)),
                 pl.BlockSpec((blk, HW), qmap), pl.BlockSpec((HEADS, blk), qmap_t), pl.BlockSpec((HEADS, blk), qmap_t)]
    args += [jnp.asarray(bias_t), do, lse, dstat]
    dq_shape = (seq // blk, qw, blk)
    if single_visit:
        out_specs, out_shape = [pl.BlockSpec((blk, qw), qmap)], [jax.ShapeDtypeStruct((seq, qw), F32)]
    else:
        out_specs = [pl.BlockSpec(dq_shape, lambda t, qi, ki, bi, fi, la: (0, 0, 0))]
        out_shape = [jax.ShapeDtypeStruct(dq_shape, F32)]
    out_specs.append(pl.BlockSpec((blk, HW), kmap))
    out_shape.append(jax.ShapeDtypeStruct((seq, HW), dk_dtype))
    scratch = [pltpu.VMEM((blk, HW), F32)]
    if mla:
        out_specs.append(pl.BlockSpec((blk, LANES), kmap))
        out_shape.append(jax.ShapeDtypeStruct((seq, LANES), F32))
        scratch.append(pltpu.VMEM((blk, LANES), F32))
    out_specs.append(pl.BlockSpec((blk, HW), kmap))
    out_shape.append(jax.ShapeDtypeStruct((seq, HW), BF16))
    scratch.append(pltpu.VMEM((blk, HW), F32))
    scratch += [pltpu.VMEM((2, 2 * blk, blk), F32), pltpu.VMEM((2, 2 * blk, blk), F32)]
    if single_visit:
        scratch.append(pltpu.VMEM((1, qw, blk), F32))
    body = _ride_along(body, ride, 5, len(args), len(out_shape), len(scratch), n_steps)
    if ride is not None:
        args, in_specs = args + ride.args, in_specs + ride.in_specs
        out_specs, out_shape, scratch = out_specs + ride.out_specs, out_shape + ride.out_shape, scratch + ride.scratch
    return _pcall(
        body, name=name,
        grid_spec=pltpu.PrefetchScalarGridSpec(
            num_scalar_prefetch=5, grid=(n_steps,), in_specs=in_specs, out_specs=out_specs,
            scratch_shapes=scratch),
        out_shape=out_shape,
        compiler_params=_cparams(dimension_semantics=("arbitrary",)),
    )(*steps, *args)


def _out_ln(oa, ob_near, ob_far, lse_near, lse_far, ga, gb, x, tgt, w_out, ln_g, ln_b, bt):
    seq = x.shape[0]

    def body(oa_ref, obn_ref, obf_ref, lsen_ref, lsef_ref, ga_ref, gb_ref, x_ref, tgt_ref, w_ref, g_ref, b_ref,
             dz_ref, doa_ref, dob_ref, dga_ref, dgb_ref, da_ref, db_ref, lse_ref, gwb_ref, small_ref, dobc_ref,
             gw_ref, lanes_sc):
        i = pl.program_id(0)

        @pl.when(i == 0)
        def _():
            gw_ref[...] = jnp.zeros(gw_ref.shape, F32)
            small_ref[...] = jnp.zeros(small_ref.shape, F32)

        def gate(g):
            sig = 0.5 * jnp.tanh(0.5 * g) + 0.5
            return g * sig, sig * (1.0 + g * (1.0 - sig))

        lse_n, lse_f = lsen_ref[...], lsef_ref[...]
        top = jnp.maximum(lse_n, lse_f)
        e_n, e_f = jnp.exp2(lse_n - top), jnp.exp2(lse_f - top)
        lse_ref[...] = top + jnp.log2(e_n + e_f)
        inv = 1.0 / (e_n + e_f)
        head_row = lax.broadcasted_iota(jnp.int32, (2 * HEADS, HW), 0) % HEADS
        spread = (head_row == lax.broadcasted_iota(jnp.int32, (2 * HEADS, HW), 1) // 64).astype(BF16)

        def per_lane(w):
            hi = w.astype(BF16)
            lo = (w - hi.astype(F32)).astype(BF16)
            return lax.dot_general(jnp.concatenate([hi, lo], axis=0), spread, TN, preferred_element_type=F32)

        o_b_all = per_lane(e_n * inv) * obn_ref[...] + per_lane(e_f * inv) * _in_sequence(obf_ref, lanes_sc)
        gam = g_ref[...]
        halves = [slice(0, bt // 2), slice(bt // 2, bt)]

        def gates_and_projection(rows):
            o_a, o_b = oa_ref[rows, :], o_b_all[rows]
            sa, dsa = gate(ga_ref[rows, :])
            sb, dsb = gate(gb_ref[rows, :])
            mix = jnp.concatenate([o_a * sa, o_b * sb], axis=1).astype(BF16)
            z = ALPHA * x_ref[rows, :] + jnp.dot(mix, w_ref[...], preferred_element_type=F32)
            return o_a, o_b, sa, dsa, sb, dsb, mix, z

        def norm_and_back(rows, mix, z):
            mu = jnp.mean(z, axis=1, keepdims=True)
            zc = z - mu
            rstd = lax.rsqrt(jnp.mean(zc * zc, axis=1, keepdims=True) + LN_EPS)
            xhat = zc * rstd
            diff = xhat * gam + b_ref[...] - tgt_ref[rows, :]
            dy = diff * (1.0 / D_MODEL)
            small_ref[0:1, :] += jnp.sum(dy * xhat, axis=0, keepdims=True)
            small_ref[1:2, :] += jnp.sum(dy, axis=0, keepdims=True)
            small_ref[2:3, :] += jnp.sum(diff * diff, axis=0, keepdims=True)
            dxh = dy * gam
            dz = rstd * (dxh - jnp.mean(dxh, axis=1, keepdims=True)
                         - xhat * jnp.mean(dxh * xhat, axis=1, keepdims=True))
            dz_ref[rows, :] = dz
            dzb = dz.astype(BF16)
            gw_ref[...] += lax.dot_general(mix, dzb, TN, preferred_element_type=F32)
            return lax.dot_general(dzb, w_ref[...], NT, preferred_element_type=F32)

        def gate_back(rows, o_a, o_b, sa, dsa, sb, dsb, dmix):
            doa, dob = dmix[:, :HW] * sa, dmix[:, HW:] * sb
            doa_ref[rows, :] = doa.astype(BF16)
            dob_ref[rows, :] = dob.astype(BF16)
            dga_ref[rows, :] = (dmix[:, :HW] * o_a * dsa).astype(BF16)
            dgb_ref[rows, :] = (dmix[:, HW:] * o_b * dsb).astype(BF16)
            return dob, doa * o_a, dob * o_b

        fronts = [gates_and_projection(rows) for rows in halves]
        dmixes = [norm_and_back(rows, f[6], f[7]) for rows, f in zip(halves, fronts)]
        backs = [gate_back(rows, *f[:6], dmix) for rows, f, dmix in zip(halves, fronts, dmixes)]
        dob, prod_a, prod_b = (jnp.concatenate(parts, axis=0) for parts in zip(*backs))
        _by_class(dob, dobc_ref, lanes_sc)

        @pl.when(i == seq // bt - 1)
        def _():
            gwb_ref[...] = gw_ref[...].astype(BF16)

        head_of = (lax.broadcasted_iota(jnp.int32, (2 * HW, LANES), 0) % HW) // 64
        ind = (head_of == lax.broadcasted_iota(jnp.int32, (2 * HW, LANES), 1)).astype(BF16)

        def head_sums(prod):
            hi = prod.astype(BF16)
            lo = (prod - hi.astype(F32)).astype(BF16)
            sums = jnp.dot(jnp.concatenate([hi, lo], axis=1), ind, preferred_element_type=F32)
            return sums.T[:HEADS, :]

        da_ref[...] = head_sums(prod_a)
        db_ref[...] = head_sums(prod_b)

    def tok(width):
        return pl.BlockSpec((bt, width), lambda i: (i, 0))

    def full(shape):
        return pl.BlockSpec(shape, lambda i: (0,) * len(shape))

    stat = pl.BlockSpec((HEADS, bt), lambda i: (0, i))
    n_cls = ob_far.shape[0]
    by_class = pl.BlockSpec((n_cls, bt // n_cls, HW), lambda i: (0, i, 0))
    return _pcall(
        body, name="out_ln", grid=(seq // bt,),
        in_specs=[tok(HW), tok(HW), by_class, stat, stat, tok(HW), tok(HW), tok(D_MODEL), tok(D_MODEL),
                  full((D_MODEL, D_MODEL)), full((1, D_MODEL)), full((1, D_MODEL))],
        out_specs=[tok(D_MODEL), tok(HW), tok(HW), tok(HW), tok(HW), stat, stat, stat,
                   full((D_MODEL, D_MODEL)), full((8, D_MODEL)), by_class],
        out_shape=[jax.ShapeDtypeStruct((seq, D_MODEL), F32)] + [jax.ShapeDtypeStruct((seq, HW), BF16)] * 4
        + [jax.ShapeDtypeStruct((HEADS, seq), F32)] * 3
        + [jax.ShapeDtypeStruct((D_MODEL, D_MODEL), BF16), jax.ShapeDtypeStruct((8, D_MODEL), F32),
           jax.ShapeDtypeStruct(ob_far.shape, BF16)],
        scratch_shapes=[pltpu.VMEM((D_MODEL, D_MODEL), F32), pltpu.VMEM((HW // LANES, bt, LANES), F32)],
        compiler_params=_cparams(dimension_semantics=("arbitrary",)),
    )(oa, ob_near, ob_far, lse_near, lse_far, ga, gb, x, tgt, w_out, ln_g, ln_b)


def _bwd_mid(dq_m, dkn, dv, dkpe, dqb, dkb, dvb, far, dga, dgb, cq, ckv, qn, kvn, w_uq_r, w_ukv_r, qg, kvg, tabs, bt):
    n_cls = far[0].shape[0]
    seq = cq.shape[0]

    def body(dqm_ref, dkn_ref, dv_ref, dkpe_ref, dqb_ref, dkb_ref, dvb_ref, dqf_ref, dkf_ref, dvf_ref, dga_ref, dgb_ref,
             cq_ref, ckv_ref, qn_ref, kvn_ref, wuq_ref, wukv_ref, qg_ref, kvg_ref, tab_ref,
             dh_ref, guq3_ref, gukv3_ref, small_ref, seq_sc, guq_ref, gukv_ref):
        i = pl.program_id(0)

        @pl.when(i == 0)
        def _():
            guq_ref[...] = jnp.zeros(guq_ref.shape, F32)
            gukv_ref[...] = jnp.zeros(gukv_ref.shape, F32)
            small_ref[...] = jnp.zeros(small_ref.shape, F32)

        m_tabs = (tab_ref[0], tab_ref[1], tab_ref[2])
        d_tabs = (tab_ref[3], tab_ref[4], tab_ref[5])

        def rms_bwd(c, dn, gain):
            r = lax.rsqrt(jnp.mean(c * c, axis=1, keepdims=True) + RMS_EPS)
            u = dn * gain
            dc = r * u - c * (r * r * r) * jnp.mean(u * c, axis=1, keepdims=True)
            return dc, jnp.sum(dn * c * r, axis=0, keepdims=True)

        dqm = dqm_ref[0].T
        dq = jnp.concatenate(
            [dqm[:, :HW], _rope_wide(_rope_t, dqm[:, HW:], *m_tabs, MLA_ROPE // 2)], axis=1) * MLA_SCALE
        dq = dq.astype(BF16)
        dkv = jnp.concatenate([dkn_ref[...], dv_ref[...]], axis=1)
        guq_ref[...] += lax.dot_general(qn_ref[...], dq, TN, preferred_element_type=F32)
        dqn = lax.dot_general(dq, wuq_ref[...], NT, preferred_element_type=F32)
        gukv_ref[...] += lax.dot_general(kvn_ref[...], dkv, TN, preferred_element_type=F32)
        dkvn = lax.dot_general(dkv, wukv_ref[...], NT, preferred_element_type=F32)

        dh_ref[:, C_KR:C_GA] = _rope_t(dkpe_ref[...], *m_tabs, MLA_ROPE // 2).astype(BF16)
        dh_ref[:, C_GA:C_QB] = dga_ref[...]
        in_sequence = functools.partial(_in_sequence, lanes_sc=seq_sc)
        dqb = dqb_ref[0].T + in_sequence(dqf_ref)
        dh_ref[:, C_QB:C_KB] = (_rope_wide(_rope_t, dqb, *d_tabs, DIL_ROT // 2) * DIL_SCALE).astype(BF16)
        dkb = dkb_ref[...] + in_sequence(dkf_ref)
        dh_ref[:, C_KB:C_VB] = _rope_wide(_rope_t, dkb, *d_tabs, DIL_ROT // 2).astype(BF16)
        dh_ref[:, C_VB:C_GB] = (dvb_ref[...].astype(F32) + in_sequence(dvf_ref)).astype(BF16)
        dh_ref[:, C_GB:C_END] = dgb_ref[...]

        dcq, gq = rms_bwd(cq_ref[...], dqn, qg_ref[...])
        small_ref[0:1, :] += gq
        dckv, gkv = rms_bwd(ckv_ref[...], dkvn, kvg_ref[...])
        small_ref[1:2, :KV_RANK] += gkv
        dh_ref[:, C_CQ:C_CKV] = dcq.astype(BF16)
        dh_ref[:, C_CKV:C_KR] = dckv.astype(BF16)

        @pl.when(i == seq // bt - 1)
        def _():
            for h in range(HEADS):
                guq3_ref[h] = jnp.concatenate(
                    [guq_ref[:, MLA_NOPE * h:MLA_NOPE * (h + 1)],
                     guq_ref[:, HW + MLA_ROPE * h:HW + MLA_ROPE * (h + 1)]], axis=1).astype(BF16)
                gukv3_ref[h] = jnp.concatenate(
                    [gukv_ref[:, MLA_NOPE * h:MLA_NOPE * (h + 1)],
                     gukv_ref[:, HW + MLA_V * h:HW + MLA_V * (h + 1)]], axis=1).astype(BF16)

    def tok(width):
        return pl.BlockSpec((bt, width), lambda i: (i, 0))

    def tok_t(a):
        per = a.shape[2] // bt
        return pl.BlockSpec((1, a.shape[1], bt), lambda i: (i // per, 0, i % per))

    def full(shape):
        return pl.BlockSpec(shape, lambda i: (0,) * len(shape))

    by_class = pl.BlockSpec((n_cls, bt // n_cls, HW), lambda i: (0, i, 0))
    uq3 = (HEADS, Q_RANK, MLA_NOPE + MLA_ROPE)
    ukv3 = (HEADS, KV_RANK, MLA_NOPE + MLA_V)
    return _pcall(
        body, name="bwd_mid", grid=(seq // bt,),
        in_specs=[tok_t(dq_m), tok(HW), tok(HW), tok(LANES), tok_t(dqb), tok(HW), tok(HW), by_class, by_class, by_class,
                  tok(HW), tok(HW),
                  tok(Q_RANK), tok(KV_RANK), tok(Q_RANK), tok(KV_RANK),
                  full(w_uq_r.shape), full(w_ukv_r.shape), full((1, Q_RANK)), full((1, KV_RANK)),
                  pl.BlockSpec((6, bt, LANES), lambda i: (0, i, 0))],
        out_specs=[tok(C_END), full(uq3), full(ukv3), full((8, Q_RANK))],
        out_shape=[jax.ShapeDtypeStruct((seq, C_END), BF16), jax.ShapeDtypeStruct(uq3, BF16),
                   jax.ShapeDtypeStruct(ukv3, BF16), jax.ShapeDtypeStruct((8, Q_RANK), F32)],
        scratch_shapes=[pltpu.VMEM((HW // LANES, bt, LANES), F32), pltpu.VMEM(w_uq_r.shape, F32),
                        pltpu.VMEM(w_ukv_r.shape, F32)],
        compiler_params=_cparams(dimension_semantics=("arbitrary",)),
    )(dq_m, dkn, dv, dkpe, dqb, dkb, dvb, *far, dga, dgb, cq, ckv, qn, kvn, w_uq_r, w_ukv_r, qg, kvg, tabs)


def _grad_x(dz, dh, w_in_r, bt, ride=None):
    seq = dz.shape[0]
    n_steps = seq // bt

    def body(dz_ref, dh_ref, w_ref, gx_ref):
        gx_ref[...] = ALPHA * dz_ref[...] + lax.dot_general(
            dh_ref[...], w_ref[...], NT, preferred_element_type=F32)

    args = [dz, dh, w_in_r]
    in_specs = [pl.BlockSpec((bt, D_MODEL), lambda i: (i, 0)), pl.BlockSpec((bt, C_END), lambda i: (i, 0)),
                pl.BlockSpec(w_in_r.shape, lambda i: (0, 0))]
    out_specs = [pl.BlockSpec((bt, D_MODEL), lambda i: (i, 0))]
    out_shape = [jax.ShapeDtypeStruct((seq, D_MODEL), F32)]
    scratch = []
    body = _ride_along(body, ride, 0, len(args), len(out_shape), 0, n_steps)
    if ride is not None:
        args, in_specs = args + ride.args, in_specs + ride.in_specs
        out_specs, out_shape, scratch = out_specs + ride.out_specs, out_shape + ride.out_shape, ride.scratch
    return _pcall(
        body, name="grad_x", grid=(n_steps,),
        in_specs=in_specs, out_specs=out_specs, out_shape=out_shape, scratch_shapes=scratch,
        compiler_params=_cparams(dimension_semantics=("arbitrary",)),
    )(*args)


def _grad_w_in(x, dh, bt, cut, ride):
    seq = x.shape[0]
    n_tok = seq // bt
    shard = IN_WIDTH // N_DEV
    k_lo, k_hi = IN_SPLITS[0] + IN_SPLITS[1], IN_SPLITS[0] + IN_SPLITS[1] + MLA_ROPE

    def body(x_ref, dh_ref, rest_ref, acc, first_ref):
        i = pl.program_id(0)
        for part, (r_lo, r_hi, dst_ref) in enumerate([(0, cut, first_ref), (cut, D_MODEL, rest_ref)]):
            rows = slice(0, r_hi - r_lo)

            @pl.when(i == part * n_tok)
            def _():
                acc[rows, :] = jnp.zeros((r_hi - r_lo, C_END), F32)

            @pl.when(i // n_tok == part)
            def _():
                acc[rows, :] += lax.dot_general(
                    x_ref[:, r_lo:r_hi].astype(BF16), dh_ref[...], TN, preferred_element_type=F32)

            @pl.when(i == (part + 1) * n_tok - 1)
            def _():
                kr = acc[rows, C_KR:C_GA]
                kr = kr + pltpu.roll(kr, 96, 1) + pltpu.roll(kr, 64, 1) + pltpu.roll(kr, 32, 1)
                for d in range(N_DEV):
                    lo, hi = shard * d, shard * (d + 1)
                    pieces = []
                    if lo < k_lo:
                        pieces.append(acc[rows, lo:min(hi, k_lo)])
                    if lo < k_hi and hi > k_lo:
                        pieces.append(kr[:, max(lo, k_lo) - k_lo:min(hi, k_hi) - k_lo])
                    if hi > k_hi:
                        shift = C_GA - k_hi
                        pieces.append(acc[rows, max(lo, k_hi) + shift:hi + shift])
                    blk = pieces[0] if len(pieces) == 1 else jnp.concatenate(pieces, axis=1)
                    dst_ref[d] = blk.astype(BF16)

    args = [x, dh]
    in_specs = [pl.BlockSpec((bt, D_MODEL), lambda i: (i % n_tok, 0)),
                pl.BlockSpec((bt, C_END), lambda i: (i % n_tok, 0))]
    out_specs = [pl.BlockSpec((N_DEV, D_MODEL - cut, shard), lambda i: (0, 0, 0))]
    out_shape = [jax.ShapeDtypeStruct((N_DEV, D_MODEL - cut, shard), BF16)]
    scratch = [pltpu.VMEM((max(cut, D_MODEL - cut), C_END), F32)]
    assert ride.shared == 1
    body = _ride_along(body, ride, 0, len(args), len(out_shape), len(scratch), 2 * n_tok)
    return _pcall(
        body, name="grad_w_in", grid=(2 * n_tok,),
        in_specs=in_specs + ride.in_specs, out_specs=out_specs + ride.out_specs,
        out_shape=out_shape + ride.out_shape, scratch_shapes=scratch + ride.scratch,
        compiler_params=_cparams(dimension_semantics=("arbitrary",)),
    )(*args, *ride.args)


def _local_step(x, tgt, w_in_r, w_uq_r, w_ukv_r, w_out_rider, g_out_rider, reduce_rider, q_norm_g, kv_norm_g,
                ln_g, ln_b, bt=BLOCK_TOKENS, blk_m=BLOCK_MLA, blk_d=BLOCK_DIL):
    seq = x.shape[0]
    tabs = jnp.asarray(_rope_tables(seq))
    qg, kvg = q_norm_g.reshape(1, -1), kv_norm_g.reshape(1, -1)

    far_dil = DIL_CONFIGS[-1][1]
    cls = seq // far_dil
    (cq, ckv, qn, kvn, qcat, kn, kpe, v, ga, gb, qb, kb, vb, knt, kpet, vt, kbt, vbt, qb_c, kb_c, vb_c) = _fwd_proj(
        x, w_in_r, w_uq_r, w_ukv_r, qg, kvg, tabs, bt, far_dil)
    qb_c, kb_c, vb_c = (a.reshape(seq, HW) for a in (qb_c, kb_c, vb_c))

    nq_m, nq_d = seq // blk_m, seq // blk_d
    bias_m = _mla_bias_t(blk_m)
    oa, lse_a, w_out = _attn_fwd(
        "mla_fwd", qcat, kn, kpe, vt, bias_m, _steps(nq_m, nq_m, False, True), blk_m, ride=w_out_rider)

    bias_near = _dil_bias_t(blk_d, DIL_NEAR)
    ob_near, lse_near = _attn_fwd(
        "dil_fwd", qb, kb, None, vbt, bias_near, _steps(nq_d, -(-DIL_NEAR // blk_d), False, False), blk_d)
    each = np.arange(far_dil, dtype=np.int32)
    steps_far = [jnp.asarray(v) for v in (each, each, np.zeros_like(each), np.ones_like(each), np.ones_like(each))]
    bias_far = _dil_far_bias_t(cls)
    ob_far, lse_far = _attn_fwd(
        "dil_far_fwd", qb_c, kb_c, None, vb_c, bias_far, steps_far, cls, v_token_major=True)

    dz, doa, dob, dga, dgb, dst_a, dst_b, lse_b, g_out, small1, dob_c = _out_ln(
        oa, ob_near, ob_far.reshape(far_dil, cls, HW), lse_near, _lanes_from_classes(lse_far, far_dil), ga, gb, x, tgt,
        w_out.reshape(D_MODEL, D_MODEL), ln_g.reshape(1, -1), ln_b.reshape(1, -1), bt)

    dq_m, dkn, dkpe, dv, g_out_recv = _attn_bwd(
        "mla_bwd", qcat, kn, kpe, v, knt, kpet, bias_m, doa, lse_a, dst_a, _steps(nq_m, nq_m, True, True), blk_m,
        ride=g_out_rider(g_out.reshape(N_DEV, D_MODEL // N_DEV, D_MODEL)))
    dqb, dkb_near, dvb_near = _attn_bwd(
        "dil_bwd", qb, kb, None, vb, kbt, None, bias_near, dob, lse_b, dst_b,
        _steps(nq_d, -(-DIL_NEAR // blk_d), True, False), blk_d)
    dqb_far, dkb_far, dvb_far = _attn_bwd(
        "dil_far_bwd", qb_c, kb_c, None, vb_c, None, None, bias_far, dob_c.reshape(seq, HW),
        _lanes_to_classes(lse_b, far_dil), _lanes_to_classes(dst_b, far_dil), steps_far, cls, single_visit=True)
    far = [a.reshape(far_dil, cls, HW) for a in (dqb_far, dkb_far, dvb_far)]

    dh, g_uq, g_ukv, small2 = _bwd_mid(
        dq_m, dkn, dv, dkpe, dqb, dkb_near, dvb_near, far, dga, dgb, cq, ckv, qn, kvn, w_uq_r, w_ukv_r, qg, kvg, tabs, bt)
    bt_w = min(seq, 2 * bt)
    n_tok = seq // bt_w
    first = jax.ShapeDtypeStruct((N_DEV, GRAD_W_IN_CUT, IN_WIDTH // N_DEV), BF16)
    g_in_rest, *reduced_first = _grad_w_in(
        x, dh, bt_w, GRAD_W_IN_CUT,
        reduce_rider([first, g_uq, g_ukv], g_out_recv, (small1, small2), (n_tok, n_tok, 2 * n_tok - 1)))
    grad_x, g_in_rest = _grad_x(dz, dh, w_in_r, bt, ride=reduce_rider([g_in_rest], None, None, (0, 1, seq // bt - 2)))
    return grad_x, [g_in_rest] + reduced_first


MESH_ID = pl.DeviceIdType.MESH
SHARD_SHAPES = ((D_MODEL, IN_WIDTH // N_DEV), (Q_RANK, 768 // N_DEV), (KV_RANK, 1024 // N_DEV), (D_MODEL // N_DEV, D_MODEL))
ADAM_ROWS = (32, 128, 128, 16)


def _me():
    x, y, c = lax.axis_index("x"), lax.axis_index("y"), lax.axis_index("c")
    return x, y, c, 4 * x + 2 * y + c


def _peer(k):
    x, y, c, _ = _me()
    px = 1 - x if (k >> 2) & 1 else x
    py = 1 - y if (k >> 1) & 1 else y
    pc = 1 - c if k & 1 else c
    return (px, py, pc), 4 * px + 2 * py + pc


def _all_gather_weights(shards):
    n = len(shards)
    shard = IN_WIDTH // N_DEV
    k_lo = IN_SPLITS[0] + IN_SPLITS[1]
    k_hi = k_lo + MLA_ROPE

    def body(*refs):
        ins = refs[:n]
        win_ref, wuq_ref, wukv_ref = refs[n:2 * n]
        bufs = refs[2 * n:3 * n]
        send_sems, recv_sems = refs[3 * n:]
        x, y, c, me = _me()
        here, sibling = (x, y, c), (x, y, 1 - c)
        along_x, along_y, across = (1 - x, y), (x, 1 - y), (1 - x, 1 - y)
        for t in range(n):
            bufs[t][me] = ins[t][...].astype(BF16)

        def copy(t, k, chip, pc, to, half=None):
            blk = bufs[t].at[4 * chip[0] + 2 * chip[1] + pc]
            if half is not None:
                rows = SHARD_SHAPES[t][0] // 2
                blk = blk.at[pl.ds(half * rows, rows), :]
            return pltpu.make_async_remote_copy(
                src_ref=blk, dst_ref=blk, send_sem=send_sems.at[t, k], recv_sem=recv_sems.at[t, k],
                device_id=to, device_id_type=MESH_ID)

        sends = []
        for t in range(n):
            sends += [copy(t, 0, (x, y), c, sibling), copy(t, 1, (x, y), c, (*along_x, c)),
                      copy(t, 2, (x, y), c, (*along_y, c))]
        for cp in sends:
            cp.start()
        for t in range(n):
            copy(t, 1, along_x, c, here).wait_recv()
            sends += [copy(t, 3, along_x, c, (*along_y, c), half=0), copy(t, 5, along_x, c, sibling)]
            sends[-2].start()
            sends[-1].start()
        for t in range(n):
            copy(t, 2, along_y, c, here).wait_recv()
            sends += [copy(t, 4, along_y, c, (*along_x, c), half=1), copy(t, 6, along_y, c, sibling)]
            sends[-2].start()
            sends[-1].start()
        for t in range(n):
            copy(t, 3, across, c, here, half=0).wait_recv()
            copy(t, 4, across, c, here, half=1).wait_recv()
            sends.append(copy(t, 7, across, c, sibling))
            sends[-1].start()
        for t in range(n):
            copy(t, 0, (x, y), 1 - c, here).wait_recv()
            for k, chip in ((5, along_x), (6, along_y), (7, across)):
                copy(t, k, chip, 1 - c, here).wait_recv()
        for cp in sends:
            cp.wait_send()

        a_in, a_uq, a_ukv = bufs
        for d in range(N_DEV):
            lo, hi = shard * d, shard * (d + 1)
            if lo < k_lo:
                win_ref[:, lo:min(hi, k_lo)] = a_in[d, :, 0:min(hi, k_lo) - lo]
            if lo < k_hi and hi > k_lo:
                kr = a_in[d, :, k_lo - lo:k_hi - lo]
                for rep in range(4):
                    win_ref[:, C_KR + MLA_ROPE * rep:C_KR + MLA_ROPE * (rep + 1)] = kr
            if hi > k_hi:
                src = max(lo, k_hi)
                win_ref[:, src + C_GA - k_hi:hi + C_GA - k_hi] = a_in[d, :, src - lo:hi - lo]
        for h in range(HEADS):
            wuq_ref[:, MLA_NOPE * h:MLA_NOPE * (h + 1)] = a_uq[h, :, :MLA_NOPE]
            wuq_ref[:, HW + MLA_ROPE * h:HW + MLA_ROPE * (h + 1)] = a_uq[h, :, MLA_NOPE:]
            wukv_ref[:, MLA_NOPE * h:MLA_NOPE * (h + 1)] = a_ukv[h, :, :MLA_NOPE]
            wukv_ref[:, HW + MLA_V * h:HW + MLA_V * (h + 1)] = a_ukv[h, :, MLA_NOPE:]

    vmem = pl.BlockSpec(memory_space=pltpu.VMEM)
    return _pcall(
        body, name="gather_weights",
        in_specs=[vmem] * n, out_specs=[vmem] * n,
        out_shape=[jax.ShapeDtypeStruct((D_MODEL, C_END), BF16), jax.ShapeDtypeStruct((Q_RANK, QW), BF16),
                   jax.ShapeDtypeStruct((KV_RANK, 2 * HW), BF16)],
        scratch_shapes=[pltpu.VMEM((N_DEV,) + s, BF16) for s in SHARD_SHAPES[:n]]
        + [pltpu.SemaphoreType.DMA((n, 8)), pltpu.SemaphoreType.DMA((n, 8))],
        compiler_params=_cparams(),
    )(*shards)


def _gather_w_out_rider(w_out):
    def copies(full_ref, stage, send_sems, recv_sems):
        me = _me()[3]
        out = []
        for k in range(1, N_DEV):
            peer, pidx = _peer(k)
            send = pltpu.make_async_remote_copy(
                src_ref=stage, dst_ref=full_ref.at[me], send_sem=send_sems.at[k - 1], recv_sem=recv_sems.at[k - 1],
                device_id=peer, device_id_type=MESH_ID)
            recv = pltpu.make_async_remote_copy(
                src_ref=stage, dst_ref=full_ref.at[pidx], send_sem=send_sems.at[k - 1], recv_sem=recv_sems.at[k - 1],
                device_id=peer, device_id_type=MESH_ID)
            out.append((send, recv))
        return out

    def start(ins, outs, scr):
        stage, send_sems, recv_sems, own_sem = scr
        stage[...] = ins[0][...].astype(BF16)
        pltpu.make_async_copy(stage, outs[0].at[_me()[3]], own_sem).start()
        for send, _ in copies(outs[0], stage, send_sems, recv_sems):
            send.start()

    def finish(ins, outs, scr):
        stage, send_sems, recv_sems, own_sem = scr
        pltpu.make_async_copy(stage, outs[0].at[_me()[3]], own_sem).wait()
        pairs = copies(outs[0], stage, send_sems, recv_sems)
        for _, recv in pairs:
            recv.wait_recv()
        for send, _ in pairs:
            send.wait_send()

    shape = SHARD_SHAPES[3]
    return Rider(
        args=[w_out], in_specs=[pl.BlockSpec(shape, lambda t, *_: (0, 0))],
        out_shape=[jax.ShapeDtypeStruct((N_DEV,) + shape, BF16)], out_specs=[pl.BlockSpec(memory_space=pl.ANY)],
        scratch=[pltpu.VMEM(shape, BF16), pltpu.SemaphoreType.DMA((N_DEV - 1,)), pltpu.SemaphoreType.DMA((N_DEV - 1,)),
                 pltpu.SemaphoreType.DMA],
        start=start, finish=finish)


def _scatter_g_out_rider(blocks):
    def copies(src_ref, dst_ref, send_sems, recv_sems):
        out = []
        for k in range(1, N_DEV):
            peer, pidx = _peer(k)
            out.append(pltpu.make_async_remote_copy(
                src_ref=src_ref.at[pidx], dst_ref=dst_ref.at[k], send_sem=send_sems.at[k - 1],
                recv_sem=recv_sems.at[k - 1], device_id=peer, device_id_type=MESH_ID))
        return out

    def start(ins, outs, scr):
        send_sems, recv_sems, own_sem = scr
        pltpu.make_async_copy(ins[0].at[_me()[3]], outs[0].at[0], own_sem).start()
        for cp in copies(ins[0], outs[0], send_sems, recv_sems):
            cp.start()

    def finish(ins, outs, scr):
        send_sems, recv_sems, own_sem = scr
        pltpu.make_async_copy(ins[0].at[_me()[3]], outs[0].at[0], own_sem).wait()
        for cp in copies(ins[0], outs[0], send_sems, recv_sems):
            cp.wait()

    hbm = pl.BlockSpec(memory_space=pl.ANY)
    return Rider(
        args=[blocks], in_specs=[hbm], out_shape=[jax.ShapeDtypeStruct(blocks.shape, blocks.dtype)], out_specs=[hbm],
        scratch=[pltpu.SemaphoreType.DMA((N_DEV - 1,)), pltpu.SemaphoreType.DMA((N_DEV - 1,)), pltpu.SemaphoreType.DMA],
        start=start, finish=finish)


def _adamw(w, g, m, v):
    m = ADAM_B1 * m + (1.0 - ADAM_B1) * g
    v = ADAM_B2 * v + (1.0 - ADAM_B2) * jnp.square(g)
    m_hat = m / (1.0 - ADAM_B1 ** ADAM_STEP)
    v_hat = v / (1.0 - ADAM_B2 ** ADAM_STEP)
    delta = -ADAM_LR * (m_hat / (jnp.sqrt(v_hat) + ADAM_EPS) + ADAM_WD * w)
    return delta, m, v


def _chunk_rows(count, cols):
    rows = (16 * 8 * LANES) // (-(-cols // LANES) * LANES)
    while count % rows:
        rows //= 2
    return rows


SMALL_ROWS = (0, 1, 3, 4)
LOSS_ROW = 2


def _reduce_grads_rider(grads3, arrived, small_parts, steps):
    n = len(grads3)
    shapes = [tuple(g.shape[1:]) for g in grads3]
    with_small = arrived is not None
    start_step, first_round, second_round = steps
    n_held = sum(isinstance(g, jax.ShapeDtypeStruct) for g in grads3)
    held, handed = grads3[:n_held], grads3[n_held:]
    assert not any(isinstance(g, jax.ShapeDtypeStruct) for g in handed)

    class Refs:
        def __init__(self, ins, outs, scr):
            self.g3, self.gsum = list(scr[0:n_held]) + list(ins[0:n - n_held]), outs[0:n]
            ins, scr = ins[n - n_held:], scr[n_held:]
            if with_small:
                self.arr, self.sp_wide, self.sp_q = ins
                self.gsum_out, self.ssum = outs[n:n + 2]
            self.own, self.sib, self.part = scr[0:n], scr[n:2 * n], scr[2 * n:3 * n]
            self.in_a, self.out_b, self.in_b = scr[3 * n:4 * n], scr[4 * n:5 * n], scr[5 * n:6 * n]
            self.rsmall = scr[6 * n]
            (self.loc_sems, self.d2d_send, self.d2d_recv, self.a_send, self.a_recv, self.b_send, self.b_recv,
             self.sm_send, self.sm_recv) = scr[6 * n + 1:]
            self.x, self.y, self.c, self.me = _me()
            self.along_x, self.along_y = (1 - self.x, self.y), (self.x, 1 - self.y)
            self.across = (1 - self.x, 1 - self.y)

        def small(self):
            if not with_small:
                return []
            return [pltpu.make_async_remote_copy(
                src_ref=self.rsmall.at[0], dst_ref=self.rsmall.at[k], send_sem=self.sm_send.at[k - 1],
                recv_sem=self.sm_recv.at[k - 1], device_id=_peer(k)[0], device_id_type=MESH_ID)
                for k in range(1, N_DEV)]

        def level1(self):
            local, to_sib = [], []
            for t in range(n):
                for q in range(4):
                    local.append(pltpu.make_async_copy(
                        self.g3[t].at[2 * q + self.c], self.own[t].at[q], self.loc_sems.at[t, q]))
                    to_sib.append(pltpu.make_async_remote_copy(
                        src_ref=self.g3[t].at[2 * q + 1 - self.c], dst_ref=self.sib[t].at[q],
                        send_sem=self.d2d_send.at[t, q], recv_sem=self.d2d_recv.at[t, q],
                        device_id=(self.x, self.y, 1 - self.c), device_id_type=MESH_ID))
            return local, to_sib

        def round_a(self):
            out = []
            for t in range(n):
                half = shapes[t][0] // 2
                for k, (to, chip, h) in enumerate([(self.along_x, self.along_x, 0), (self.along_x, self.across, 0),
                                                   (self.along_y, self.along_y, 1), (self.along_y, self.across, 1)]):
                    out.append(pltpu.make_async_remote_copy(
                        src_ref=self.part[t].at[2 * chip[0] + chip[1], pl.ds(h * half, half), :],
                        dst_ref=self.in_a[t].at[k], send_sem=self.a_send.at[t, k], recv_sem=self.a_recv.at[t, k],
                        device_id=(*to, self.c), device_id_type=MESH_ID))
            return out

        def round_b(self):
            return [pltpu.make_async_remote_copy(
                src_ref=self.out_b[t].at[k], dst_ref=self.in_b[t].at[k], send_sem=self.b_send.at[t, k],
                recv_sem=self.b_recv.at[t, k], device_id=(*to, self.c), device_id_type=MESH_ID)
                for t in range(n) for k, to in enumerate([self.along_y, self.along_x])]

    def chunks(rows, count, fn):
        def step(i, carry):
            fn(pl.multiple_of(i * rows, rows))
            return carry

        lax.fori_loop(0, count // rows, step, 0)

    def start(*refs):
        r = Refs(*refs)
        if with_small:
            r.rsmall[0] = r.sp_wide[...]
            for k, row in enumerate(SMALL_ROWS[2:]):
                r.rsmall[0, row:row + 1, 0:Q_RANK] = r.sp_q[k:k + 1, :]
        local, to_sib = r.level1()
        for cp in r.small() + local + to_sib:
            cp.start()

    def begin_rounds(*refs):
        r = Refs(*refs)
        local, to_sib = r.level1()
        for cp in local:
            cp.wait()
        for cp in to_sib:
            cp.wait_recv()
        my_chip = 2 * r.x + r.y
        for t in range(n):
            rows = _chunk_rows(*shapes[t])

            def pair_sums(at, t=t, rows=rows):
                sl = pl.ds(at, rows)
                for q in range(4):
                    r.part[t][q, sl, :] = (r.own[t][q, sl, :].astype(F32) + r.sib[t][q, sl, :].astype(F32)).astype(BF16)
                r.gsum[t][sl, :] = r.own[t][my_chip, sl, :].astype(F32) + r.sib[t][my_chip, sl, :].astype(F32)

            chunks(rows, shapes[t][0], pair_sums)
        for cp in r.round_a():
            cp.start()
        if not with_small:
            return
        rows_out = _chunk_rows(*SHARD_SHAPES[3])

        def add_arrived(at):
            sl = pl.ds(at, rows_out)
            g = r.arr[0, sl, :].astype(F32)
            for k in range(1, N_DEV):
                g = g + r.arr[k, sl, :].astype(F32)
            r.gsum_out[sl, :] = g

        chunks(rows_out, SHARD_SHAPES[3][0], add_arrived)

    def pass_on(*refs):
        r = Refs(*refs)
        for cp in r.round_a():
            cp.wait_recv()
        q_x, q_y = 2 * r.along_x[0] + r.along_x[1], 2 * r.along_y[0] + r.along_y[1]
        for t in range(n):
            half = shapes[t][0] // 2
            rows = _chunk_rows(half, shapes[t][1])

            def add(at, t=t, rows=rows, half=half):
                lo, hi = pl.ds(at, rows), pl.ds(half + at, rows)
                r.gsum[t][lo, :] = r.gsum[t][lo, :] + r.in_a[t][0, lo, :].astype(F32)
                r.out_b[t][0, lo, :] = (r.part[t][q_y, lo, :].astype(F32) + r.in_a[t][1, lo, :].astype(F32)).astype(BF16)
                r.gsum[t][hi, :] = r.gsum[t][hi, :] + r.in_a[t][2, lo, :].astype(F32)
                r.out_b[t][1, lo, :] = (r.part[t][q_x, hi, :].astype(F32) + r.in_a[t][3, lo, :].astype(F32)).astype(BF16)

            chunks(rows, half, add)
        for cp in r.round_b():
            cp.start()

    def finish(*refs):
        r = Refs(*refs)
        passed = r.round_b()
        for cp in passed:
            cp.wait_recv()
        for t in range(n):
            half = shapes[t][0] // 2
            rows = _chunk_rows(half, shapes[t][1])

            def add(at, t=t, rows=rows, half=half):
                lo, hi = pl.ds(at, rows), pl.ds(half + at, rows)
                r.gsum[t][lo, :] = r.gsum[t][lo, :] + r.in_b[t][0, lo, :].astype(F32)
                r.gsum[t][hi, :] = r.gsum[t][hi, :] + r.in_b[t][1, lo, :].astype(F32)

            chunks(rows, half, add)
        small = r.small()
        for cp in small:
            cp.wait_recv()
        if with_small:
            tot = r.rsmall[r.me]
            for d in range(1, N_DEV):
                tot = tot + r.rsmall[jnp.bitwise_xor(r.me, d)]
            r.ssum[...] = tot
        for cp in small + r.level1()[1] + r.round_a() + passed:
            cp.wait_send()

    hbm = pl.BlockSpec(memory_space=pl.ANY)
    dma = pltpu.SemaphoreType.DMA

    def whole(shape):
        return pl.BlockSpec(shape, lambda i: (0,) * len(shape))

    def halves(slots):
        return [pltpu.VMEM((slots, s[0] // 2, s[1]), BF16) for s in shapes]

    small_args = [arrived, *small_parts] if with_small else []
    out_shapes = shapes + ([SHARD_SHAPES[3], (8, D_MODEL)] if with_small else [])
    return Rider(
        args=list(handed) + small_args,
        in_specs=[hbm] * len(handed) + [whole(a.shape) for a in small_args],
        out_shape=[jax.ShapeDtypeStruct(s, F32) for s in out_shapes], out_specs=[whole(s) for s in out_shapes],
        scratch=[pltpu.VMEM(g.shape, g.dtype) for g in held]
        + [pltpu.VMEM((4,) + s, BF16) for _ in range(3) for s in shapes]
        + halves(4) + halves(2) + halves(2)
        + [pltpu.VMEM((N_DEV, 8, D_MODEL), F32), dma((n, 4)), dma((n, 4)), dma((n, 4)), dma((n, 4)), dma((n, 4)),
           dma((n, 2)), dma((n, 2)), dma((N_DEV - 1,)), dma((N_DEV - 1,))],
        start=start, finish=finish, stages=((first_round, begin_rounds), (second_round, pass_on)),
        start_step=start_step, shared=n_held)


def _adamw_update(grads, small_grad, wmv, small_wmv):
    n_small = len(small_wmv)
    flat_grads = [g for pieces in grads for g in pieces]
    n_g = len(flat_grads)

    def body(*refs):
        g_refs, sg_ref, refs = iter(refs[0:n_g]), refs[n_g], refs[n_g + 1:]
        wmv_refs = [refs[3 * t:3 * t + 3] for t in range(4)]
        swmv_refs = [refs[12 + 3 * t:15 + 3 * t] for t in range(n_small)]
        outs = refs[12 + 3 * n_small:]
        out_refs = [outs[4 * t:4 * t + 4] for t in range(4)]
        sout_refs = [outs[16 + 4 * t:20 + 4 * t] for t in range(n_small)]
        loss_ref = outs[16 + 4 * n_small]
        for t, (w_ref, m_ref, v_ref) in enumerate(swmv_refs):
            g = sg_ref[SMALL_ROWS[t]:SMALL_ROWS[t] + 1, :w_ref.shape[1]]
            delta, m, v = _adamw(w_ref[...], g, m_ref[...], v_ref[...])
            sout_refs[t][0][...], sout_refs[t][1][...], sout_refs[t][2][...], sout_refs[t][3][...] = g, delta, m, v
        loss_ref[...] = (0.5 / D_MODEL) * jnp.sum(sg_ref[LOSS_ROW:LOSS_ROW + 1, :], axis=1, keepdims=True)
        for t in range(4):
            rows = ADAM_ROWS[t]
            w_ref, m_ref, v_ref = wmv_refs[t]
            g_out, d_out, m_out, v_out = out_refs[t]

            def step(i, carry, g_ref, first, rows=rows, w_ref=w_ref, m_ref=m_ref, v_ref=v_ref,
                     g_out=g_out, d_out=d_out, m_out=m_out, v_out=v_out):
                at = pl.multiple_of(i * rows, rows)
                r = pl.ds(pl.multiple_of(first + at, rows), rows)
                g = g_ref[pl.ds(at, rows), :]
                delta, m, v = _adamw(w_ref[r, :], g, m_ref[r, :], v_ref[r, :])
                g_out[r, :], d_out[r, :], m_out[r, :], v_out[r, :] = g, delta, m, v
                return carry

            first = 0
            for piece in grads[t]:
                lax.fori_loop(0, piece.shape[0] // rows, functools.partial(step, g_ref=next(g_refs), first=first), 0)
                first += piece.shape[0]

    vmem = pl.BlockSpec(memory_space=pltpu.VMEM)
    flat_wmv = [a for trio in wmv for a in trio]
    flat_small = [a for trio in small_wmv for a in trio]
    out_shape = ([jax.ShapeDtypeStruct(s, F32) for s in SHARD_SHAPES for _ in range(4)]
                 + [jax.ShapeDtypeStruct(trio[0].shape, F32) for trio in small_wmv for _ in range(4)]
                 + [jax.ShapeDtypeStruct((1, 1), F32)])
    return _pcall(
        body, name="adamw",
        in_specs=[vmem] * (n_g + 1 + len(flat_wmv) + len(flat_small)), out_specs=[vmem] * len(out_shape),
        out_shape=out_shape,
        compiler_params=_cparams(),
    )(*flat_grads, small_grad, *flat_wmv, *flat_small)


def kernel(x, w_in, q_norm_g, kv_norm_g, w_uq, w_ukv, w_out, ln_g, ln_b, loss_target, m_w_in, m_q_norm_g, m_kv_norm_g, m_w_uq, m_w_ukv, m_w_out, m_ln_g, m_ln_b, v_w_in, v_q_norm_g, v_kv_norm_g, v_w_uq, v_w_ukv, v_w_out, v_ln_g, v_ln_b):
    w_in_r, w_uq_r, w_ukv_r = _all_gather_weights([w_in, w_uq, w_ukv])
    grad_x, (g_in_rest, g_in_first, g_uq, g_ukv, g_out, g_small) = _local_step(
        x[0], loss_target[0], w_in_r, w_uq_r, w_ukv_r, _gather_w_out_rider(w_out), _scatter_g_out_rider,
        _reduce_grads_rider, q_norm_g, kv_norm_g, ln_g, ln_b)
    row = lambda a: a.reshape(1, -1)
    small_wmv = [(row(ln_g), row(m_ln_g), row(v_ln_g)), (row(ln_b), row(m_ln_b), row(v_ln_b)),
                 (row(q_norm_g), row(m_q_norm_g), row(v_q_norm_g)), (row(kv_norm_g), row(m_kv_norm_g), row(v_kv_norm_g))]
    wmv = [(w_in, m_w_in, v_w_in), (w_uq, m_w_uq, v_w_uq), (w_ukv, m_w_ukv, v_w_ukv), (w_out, m_w_out, v_w_out)]
    res = _adamw_update([[g_in_first, g_in_rest], [g_uq], [g_ukv], [g_out]], g_small, wmv, small_wmv)
    big = [res[4 * t:4 * t + 4] for t in range(4)]
    small = [[a.reshape(-1) for a in res[16 + 4 * t:20 + 4 * t]] for t in range(4)]
    loss = res[32].reshape(())

    def group(kind):
        return (big[0][kind], small[2][kind], small[3][kind], big[1][kind], big[2][kind], big[3][kind],
                small[0][kind], small[1][kind])

    return (loss, grad_x[None], *group(0), *group(1), *group(2), *group(3))
```

```python
import functools
from typing import Callable, NamedTuple

import numpy as np
import jax
import jax.numpy as jnp
from jax import lax
from jax.experimental import pallas as pl
from jax.experimental.pallas import tpu as pltpu

F32 = jnp.float32
BF16 = jnp.bfloat16

D_MODEL = 1024
ROPE_THETA = 500000.0
NEG = -1e30
RMS_EPS = 1e-6
LN_EPS = 1e-5
HEADS = 8
MLA_NOPE = 64
MLA_ROPE = 32
MLA_V = 64
Q_RANK = 384
KV_RANK = 256
DIL_HEAD = 64
DIL_ROT = 16
DIL_CONFIGS = ((128, 1), (512, 4), (2048, 16))
DIL_NEAR = 512
HW = HEADS * 64
QW = HW + HEADS * MLA_ROPE
IN_SPLITS = (Q_RANK, KV_RANK, MLA_ROPE, HW, HW, HW, HW, HW)
IN_WIDTH = sum(IN_SPLITS)
ALPHA = 2.0 ** 0.25
MLA_SCALE = (MLA_NOPE + MLA_ROPE) ** -0.5
DIL_SCALE = DIL_HEAD ** -0.5
LOG2E = 1.4426950408889634
LN2 = 0.6931471805599453

ADAM_LR = 0.001
ADAM_B1 = 0.9
ADAM_B2 = 0.999
ADAM_EPS = 1e-08
ADAM_WD = 0.01
ADAM_STEP = 10

N_DEV = 8
LANES = 128
VMEM_LIMIT = 56 * 1024 * 1024
BLOCK_TOKENS = 512
BLOCK_MLA = 512
BLOCK_DIL = 512
GRAD_W_IN_CUT = 384

C_CQ, C_CKV, C_KR, C_GA, C_QB, C_KB, C_VB, C_GB, C_END = 0, 384, 640, 768, 1280, 1792, 2304, 2816, 3328

NT = (((1,), (1,)), ((), ()))
TN = (((0,), (0,)), ((), ()))


def _pcall(body, **kw):
    return pl.pallas_call(body, **kw)


def _cparams(**kw):
    return pltpu.CompilerParams(vmem_limit_bytes=VMEM_LIMIT, **kw)


def _rope_tables(seq):
    def tabs(dim, period):
        half = dim // 2
        inv = np.float32(ROPE_THETA) ** (-np.arange(0, dim, 2, dtype=np.float32) / np.float32(dim))
        ang = np.arange(seq, dtype=np.float32)[:, None] * inv.astype(np.float32)[None, :]
        cos, sin = np.cos(ang).astype(np.float32), np.sin(ang).astype(np.float32)
        j = np.arange(LANES) % period
        f = j % half
        c = np.where(j < dim, cos[:, f], np.float32(1.0))
        s1 = np.where(j < half, -sin[:, f], np.float32(0.0))
        s2 = np.where((j >= half) & (j < dim), sin[:, f], np.float32(0.0))
        return [c, s1, s2]
    return np.stack(tabs(MLA_ROPE, MLA_ROPE) + tabs(DIL_ROT, DIL_HEAD)).astype(np.float32)


def _rope(t, c, s1, s2, half):
    return t * c + pltpu.roll(t, LANES - half, 1) * s1 + pltpu.roll(t, half, 1) * s2


def _rope_t(d, c, s1, s2, half):
    return d * c + pltpu.roll(d * s1, half, 1) + pltpu.roll(d * s2, LANES - half, 1)


def _rope_wide(fn, t, c, s1, s2, half):
    return jnp.concatenate(
        [fn(t[:, i:i + LANES], c, s1, s2, half) for i in range(0, t.shape[1], LANES)], axis=1)


def _mla_bias_t(blk):
    a = np.arange(blk)
    causal = np.where(a[:, None] <= a[None, :], 0.0, NEG)
    return np.stack([np.zeros((blk, blk)), causal]).astype(np.float32)


def _dil_bias_t(blk, reach):
    a = np.arange(blk)
    out = []
    for off in range(-(-reach // blk) + 1):
        delta = blk * off + a[None, :] - a[:, None]
        mult = np.zeros((blk, blk))
        for window, dil in DIL_CONFIGS:
            mult += (delta >= 0) & (delta % dil == 0) & (delta <= min(window, reach))
        out.append(np.where(mult > 0, np.log2(np.maximum(mult, 1.0)), NEG))
    return np.stack(out).astype(np.float32)


def _dil_far_bias_t(length):
    window, dil = DIL_CONFIGS[-1]
    a = np.arange(length)
    steps_back = a[None, :] - a[:, None]
    seen = (steps_back * dil > DIL_NEAR) & (steps_back * dil <= window)
    return np.where(seen, 0.0, NEG).astype(np.float32)[None]


def _lanes_to_classes(a, dil):
    h, s = a.shape
    return a.reshape(h, s // dil, dil).transpose(0, 2, 1).reshape(h, s)


def _lanes_from_classes(a, dil):
    h, s = a.shape
    return a.reshape(h, dil, s // dil).transpose(0, 2, 1).reshape(h, s)


def _steps(nq, span, by_key, diag_only_bias):
    rows = []
    if by_key:
        for ki in range(nq):
            hi = min(nq - 1, ki + span)
            for qi in range(ki, hi + 1):
                rows.append((qi, ki, int(qi == ki), int(qi == hi)))
    else:
        for qi in range(nq):
            lo = max(0, qi - span)
            for ki in range(lo, qi + 1):
                rows.append((qi, ki, int(ki == lo), int(ki == qi)))
    arr = np.array(rows, dtype=np.int32)
    off = arr[:, 0] - arr[:, 1]
    bias_idx = (off == 0).astype(np.int32) if diag_only_bias else off.astype(np.int32)
    return [jnp.asarray(v) for v in (arr[:, 0], arr[:, 1], bias_idx, arr[:, 2], arr[:, 3])]


def _by_class(val, out_ref, lanes_sc):
    n_cls, per = out_ref.shape[0], out_ref.shape[1]
    for c in range(val.shape[1] // LANES):
        lanes_sc[c] = val[:, LANES * c:LANES * (c + 1)]
        for r in range(n_cls):
            rows = lanes_sc.at[c][pl.ds(r, per, stride=n_cls), :]
            out_ref[r, :, LANES * c:LANES * (c + 1)] = rows.astype(out_ref.dtype)


def _in_sequence(ref, lanes_sc):
    n_cls, per, width = ref.shape
    for c in range(width // LANES):
        for r in range(n_cls):
            lanes_sc.at[c][pl.ds(r, per, stride=n_cls), :] = ref[r, :, LANES * c:LANES * (c + 1)].astype(F32)
    return jnp.concatenate([lanes_sc[c] for c in range(width // LANES)], axis=1)


def _fwd_proj(x, w_in_r, w_uq_r, w_ukv_r, qg, kvg, tabs, bt, n_cls):
    seq = x.shape[0]

    def body(x_ref, win_ref, wuq_ref, wukv_ref, qg_ref, kvg_ref, tab_ref,
             cq_ref, ckv_ref, qn_ref, kvn_ref, qcat_ref, kn_ref, kpe_ref, v_ref,
             ga_ref, gb_ref, qb_ref, kb_ref, vb_ref, knt_ref, kpet_ref, vt_ref, kbt_ref, vbt_ref,
             qbc_ref, kbc_ref, vbc_ref, lanes_sc):
        xb = x_ref[...].astype(BF16)

        def proj(lo, hi):
            return jnp.dot(xb, win_ref[:, lo:hi], preferred_element_type=F32)

        m_tabs = (tab_ref[0], tab_ref[1], tab_ref[2])
        d_tabs = (tab_ref[3], tab_ref[4], tab_ref[5])

        def use_cq(cq):
            cq_ref[...] = cq
            qn = (cq * lax.rsqrt(jnp.mean(cq * cq, axis=1, keepdims=True) + RMS_EPS) * qg_ref[...]).astype(BF16)
            qn_ref[...] = qn
            q = jnp.dot(qn, wuq_ref[...], preferred_element_type=F32)
            qcat_ref[:, :HW] = (q[:, :HW] * (MLA_SCALE * LOG2E)).astype(BF16)
            qcat_ref[:, HW:] = (
                _rope_wide(_rope, q[:, HW:], *m_tabs, MLA_ROPE // 2) * (MLA_SCALE * LOG2E)).astype(BF16)

        def use_ckv(ckv):
            ckv_ref[...] = ckv
            kvn = (ckv * lax.rsqrt(jnp.mean(ckv * ckv, axis=1, keepdims=True) + RMS_EPS) * kvg_ref[...]).astype(BF16)
            kvn_ref[...] = kvn
            kv = jnp.dot(kvn, wukv_ref[...], preferred_element_type=F32)
            kn_ref[...] = kv[:, :HW].astype(BF16)
            v_ref[...] = kv[:, HW:].astype(BF16)
            knt_ref[...] = kv[:, :HW].T.astype(BF16)
            vt_ref[...] = kv[:, HW:].T.astype(BF16)

        def use_kr(kr):
            kpe = _rope(kr, *m_tabs, MLA_ROPE // 2)
            kpe_ref[...] = kpe.astype(BF16)
            kpet_ref[...] = kpe.T[:MLA_ROPE, :].astype(BF16)

        def use_ga(ga):
            ga_ref[...] = ga

        def use_qb(qb):
            qb = _rope_wide(_rope, qb, *d_tabs, DIL_ROT // 2) * (DIL_SCALE * LOG2E)
            qb_ref[...] = qb.astype(BF16)
            _by_class(qb, qbc_ref, lanes_sc)

        def use_kb(kb):
            kb = _rope_wide(_rope, kb, *d_tabs, DIL_ROT // 2)
            kb_ref[...] = kb.astype(BF16)
            kbt_ref[...] = kb.T.astype(BF16)
            _by_class(kb, kbc_ref, lanes_sc)

        def use_vb(vb):
            vb_ref[...] = vb.astype(BF16)
            vbt_ref[...] = vb.T.astype(BF16)
            _by_class(vb, vbc_ref, lanes_sc)

        def use_gb(gb):
            gb_ref[...] = gb

        pieces = [(C_CQ, C_CKV, use_cq), (C_CKV, C_KR, use_ckv), (C_KR, C_GA, use_kr), (C_GA, C_QB, use_ga),
                  (C_QB, C_KB, use_qb), (C_KB, C_VB, use_kb), (C_VB, C_GB, use_vb), (C_GB, C_END, use_gb)]
        ahead = proj(*pieces[0][:2])
        for n, (_, _, use) in enumerate(pieces):
            cur = ahead
            if n + 1 < len(pieces):
                ahead = proj(*pieces[n + 1][:2])
            use(cur)

    def tok(width):
        return pl.BlockSpec((bt, width), lambda i: (i, 0))

    def tok_t(height):
        return pl.BlockSpec((height, bt), lambda i: (0, i))

    def full(a):
        return pl.BlockSpec(a.shape, lambda i: (0,) * a.ndim)

    outs = [(Q_RANK, F32), (KV_RANK, F32), (Q_RANK, BF16), (KV_RANK, BF16), (QW, BF16), (HW, BF16),
            (LANES, BF16), (HW, BF16), (HW, F32), (HW, F32), (HW, BF16), (HW, BF16), (HW, BF16)]
    outs_t = [HW, MLA_ROPE, HW, HW, HW]
    by_class = pl.BlockSpec((n_cls, bt // n_cls, HW), lambda i: (0, i, 0))
    return _pcall(
        body, name="fwd_proj", grid=(seq // bt,),
        in_specs=[tok(D_MODEL), full(w_in_r), full(w_uq_r), full(w_ukv_r), full(qg), full(kvg),
                  pl.BlockSpec((6, bt, LANES), lambda i: (0, i, 0))],
        out_specs=[tok(w) for w, _ in outs] + [tok_t(h) for h in outs_t] + [by_class] * 3,
        out_shape=[jax.ShapeDtypeStruct((seq, w), dt) for w, dt in outs]
        + [jax.ShapeDtypeStruct((h, seq), BF16) for h in outs_t]
        + [jax.ShapeDtypeStruct((n_cls, seq // n_cls, HW), BF16)] * 3,
        scratch_shapes=[pltpu.VMEM((HW // LANES, bt, LANES), F32)],
        compiler_params=_cparams(dimension_semantics=("arbitrary",)),
    )(x, w_in_r, w_uq_r, w_ukv_r, qg, kvg, tabs)


def _head_masks(lane, h):
    e, g = h % 2, h % 4
    me = (lane >= 64 * e) & (lane < 64 * e + 64)
    mr = (lane >= 32 * g) & (lane < 32 * g + 32)
    return me, mr


def _masked(mask, a):
    return jnp.where(mask, a, jnp.zeros_like(a))


def _pair_operands(q_ref, k_ref, kpe_ref, lane, j, ks=slice(None), qs=slice(None)):
    cols = slice(LANES * j, LANES * (j + 1))
    qc = q_ref[qs, cols]
    kj = k_ref[ks, cols]
    kes = []
    for h in (2 * j, 2 * j + 1):
        me, mr = _head_masks(lane, h)
        ke = _masked(me, kj)
        if kpe_ref is not None:
            ke = jnp.concatenate([ke, _masked(mr, kpe_ref[ks, :])], axis=1)
        kes.append(ke)
    if kpe_ref is not None:
        qc = jnp.concatenate([qc, q_ref[qs, HW + LANES * (j // 2):HW + LANES * (j // 2 + 1)]], axis=1)
    return qc, kes


def _tile_variants(bias_t):
    out = {}
    for i, tile in enumerate(np.asarray(bias_t)):
        h = tile.shape[0] // 2
        skip = 1 if (tile[h:, :h] == NEG).all() else 2 if (tile[:h, h:] == NEG).all() else 0
        out[i] = (bool((tile != 0).any()), skip)
    return out


def _tile_parts(blk, skip):
    lo, hi, full = slice(0, blk // 2), slice(blk // 2, blk), slice(0, blk)
    return {0: [(full, full)], 1: [(lo, full), (hi, hi)], 2: [(hi, full), (lo, lo)]}[skip]


class Rider(NamedTuple):
    args: list
    in_specs: list
    out_shape: list
    out_specs: list
    scratch: list
    start: Callable
    finish: Callable
    stages: tuple = ()
    start_step: int = 0
    shared: int = 0


def _ride_along(body, ride, n_prefetch, n_in, n_out, n_scratch, n_steps):
    if ride is None:
        return body

    def wrapped(*refs):
        pre, rest = refs[:n_prefetch], refs[n_prefetch:]
        a = n_in
        b = a + len(ride.args)
        c = b + n_out
        d = c + len(ride.out_shape)
        e = d + n_scratch
        mine = (rest[a:b], rest[c:d], rest[e:])
        t = pl.program_id(0)
        pl.when(t == ride.start_step)(lambda: ride.start(*mine))
        for at, stage in ride.stages:
            pl.when(t == at)(functools.partial(stage, *mine))
        body(*pre, *rest[:a], *rest[b:c], *rest[d:e + ride.shared])
        pl.when(t == n_steps - 1)(lambda: ride.finish(*mine))

    return wrapped


def _attn_fwd(name, q, k, kpe, vt, bias_t, steps, blk, ride=None, v_token_major=False):
    seq = q.shape[0]
    mla = kpe is not None
    n_steps = int(steps[0].shape[0])
    variants = _tile_variants(bias_t)

    def body(qi_r, ki_r, bi_r, fi_r, la_r, *refs):
        if mla:
            q_ref, k_ref, kpe_ref, vt_ref, b_ref, o_ref, lse_ref, m_sc, l_sc, acc_sc, st_sc = refs
        else:
            q_ref, k_ref, vt_ref, b_ref, o_ref, lse_ref, m_sc, l_sc, acc_sc, st_sc = refs
        t = pl.program_id(0)

        @pl.when(fi_r[t] == 1)
        def _():
            m_sc[...] = jnp.full(m_sc.shape, NEG, F32)
            l_sc[...] = jnp.zeros(l_sc.shape, F32)
            acc_sc[...] = jnp.zeros(acc_sc.shape, F32)

        lane = lax.broadcasted_iota(jnp.int32, (1, LANES), 1)
        if v_token_major:
            vt_all = vt_ref[...].astype(F32).T.astype(BF16)
            vt_rows = lambda rows, ks: vt_all[rows, ks]
        else:
            vt_rows = lambda rows, ks: vt_ref[rows, ks]

        def tile_pass(ks, qs, with_bias):
            nk, nq = ks.stop - ks.start, qs.stop - qs.start
            ones = jnp.ones((16, nk), BF16)

            def pair_scores(j):
                qc, kes = _pair_operands(q_ref, k_ref, kpe_ref if mla else None, lane, j, ks, qs)
                st = lax.dot_general(jnp.concatenate(kes, axis=0), qc, NT, preferred_element_type=F32)
                maxes = []
                for e in range(2):
                    se = st[e * nk:(e + 1) * nk]
                    if with_bias:
                        se = se + b_ref[0, ks, qs]
                    st_sc[j % 2, e * nk:(e + 1) * nk, 0:nq] = se
                    maxes.append(jnp.max(se, axis=0, keepdims=True))
                return maxes

            def softmax_pv(h, col_max):
                st = st_sc[(h // 2) % 2, (h % 2) * nk:(h % 2 + 1) * nk, 0:nq]
                hrow = slice(h, h + 1)
                m_prev = m_sc[hrow, qs]
                m_new = jnp.maximum(m_prev, col_max)
                alpha = jnp.exp2(m_prev - m_new)
                pt = jnp.exp2(st - m_new).astype(BF16)
                m_sc[hrow, qs] = m_new
                rows = slice(64 * h, 64 * h + 64)
                res = jnp.dot(jnp.concatenate([vt_rows(rows, ks), ones], axis=0), pt, preferred_element_type=F32)
                acc_sc[rows, qs] = alpha * acc_sc[rows, qs] + res[:64]
                l_sc[hrow, qs] = alpha * l_sc[hrow, qs] + res[64:65]

            maxes = pair_scores(0)
            for j in range(HEADS // 2):
                cur = maxes
                if j + 1 < HEADS // 2:
                    maxes = pair_scores(j + 1)
                softmax_pv(2 * j, cur[0])
                softmax_pv(2 * j + 1, cur[1])

        def step(with_bias, skip):
            for ks, qs in _tile_parts(blk, skip):
                tile_pass(ks, qs, with_bias)

        for idx, (with_bias, skip) in variants.items():
            if len(variants) == 1:
                step(with_bias, skip)
            else:
                pl.when(bi_r[t] == idx)(functools.partial(step, with_bias, skip))

        @pl.when(la_r[t] == 1)
        def _():
            for h in range(HEADS):
                rows = slice(64 * h, 64 * h + 64)
                acc_sc[rows, :] = acc_sc[rows, :] / l_sc[h:h + 1, :]
            o_ref[...] = acc_sc[...].T
            lse_ref[...] = m_sc[...] + jnp.log2(l_sc[...])

    qmap = lambda t, qi, ki, bi, fi, la: (qi[t], 0)
    kmap = lambda t, qi, ki, bi, fi, la: (ki[t], 0)
    in_specs = [pl.BlockSpec((blk, q.shape[1]), qmap), pl.BlockSpec((blk, HW), kmap)]
    args = [q, k]
    if mla:
        in_specs.append(pl.BlockSpec((blk, LANES), kmap))
        args.append(kpe)
    in_specs += [pl.BlockSpec((blk, HW), kmap) if v_token_major else
                 pl.BlockSpec((HW, blk), lambda t, qi, ki, bi, fi, la: (0, ki[t])),
                 pl.BlockSpec((1, blk, blk), lambda t, qi, ki, bi, fi, la: (bi[t], 0, 0))]
    args += [vt, jnp.asarray(bias_t)]
    out_specs = [pl.BlockSpec((blk, HW), qmap), pl.BlockSpec((HEADS, blk), lambda t, qi, ki, bi, fi, la: (0, qi[t]))]
    out_shape = [jax.ShapeDtypeStruct((seq, HW), F32), jax.ShapeDtypeStruct((HEADS, seq), F32)]
    scratch = [pltpu.VMEM((HEADS, blk), F32), pltpu.VMEM((HEADS, blk), F32),
               pltpu.VMEM((HW, blk), F32), pltpu.VMEM((2, 2 * blk, blk), F32)]
    body = _ride_along(body, ride, 5, len(args), len(out_shape), len(scratch), n_steps)
    if ride is not None:
        args, in_specs = args + ride.args, in_specs + ride.in_specs
        out_specs, out_shape, scratch = out_specs + ride.out_specs, out_shape + ride.out_shape, scratch + ride.scratch
    return _pcall(
        body, name=name,
        grid_spec=pltpu.PrefetchScalarGridSpec(
            num_scalar_prefetch=5, grid=(n_steps,), in_specs=in_specs, out_specs=out_specs, scratch_shapes=scratch),
        out_shape=out_shape,
        compiler_params=_cparams(dimension_semantics=("arbitrary",)),
    )(*steps, *args)


def _attn_bwd(name, q, k, kpe, v, kt, kpet, bias_t, do, lse, dstat, steps, blk, ride=None, single_visit=False):
    assert not (single_visit and kpe is not None) and (kt is not None or single_visit)
    seq = q.shape[0]
    mla = kpe is not None
    qw = q.shape[1]
    n_steps = int(steps[0].shape[0])
    dk_dtype = BF16 if mla else F32
    variants = _tile_variants(bias_t)

    def body(qi_r, ki_r, bi_r, fi_r, la_r, *refs):
        if mla:
            (q_ref, k_ref, kpe_ref, v_ref, kt_ref, kpet_ref, b_ref, do_ref, lse_ref, d_ref,
             dq_ref, dk_ref, dkpe_ref, dv_ref, dk_sc, dkpe_sc, dv_sc, st_sc, dpt_sc) = refs
        else:
            q_ref, k_ref, v_ref, *rest = refs
            kt_ref = rest.pop(0) if kt is not None else None
            b_ref, do_ref, lse_ref, d_ref, dq_out_ref, dk_ref, dv_ref, dk_sc, dv_sc, st_sc, dpt_sc, *rest = rest
            dq_ref = rest[0] if single_visit else dq_out_ref
        t = pl.program_id(0)

        @pl.when(jnp.logical_or(t == 0, single_visit))
        def _():
            dq_ref[...] = jnp.zeros(dq_ref.shape, F32)

        @pl.when(fi_r[t] == 1)
        def _():
            dk_sc[...] = jnp.zeros(dk_sc.shape, F32)
            dv_sc[...] = jnp.zeros(dv_sc.shape, F32)
            if mla:
                dkpe_sc[...] = jnp.zeros(dkpe_sc.shape, F32)

        qi = 0 if single_visit else qi_r[t]
        lane = lax.broadcasted_iota(jnp.int32, (1, LANES), 1)
        if kt is None:
            kt_all = k_ref[...].astype(F32).T.astype(BF16)
            kt_rows = lambda rows, ks: kt_all[rows, ks]
        else:
            kt_rows = lambda rows, ks: kt_ref[rows, ks]

        def tile_pass(ks, qs, with_bias):
            nk, nq = ks.stop - ks.start, qs.stop - qs.start

            def pair_matmuls(j):
                cols = slice(LANES * j, LANES * (j + 1))
                qc, kes = _pair_operands(q_ref, k_ref, kpe_ref if mla else None, lane, j, ks, qs)
                st_sc[j % 2, 0:2 * nk, 0:nq] = lax.dot_general(
                    jnp.concatenate(kes, axis=0), qc, NT, preferred_element_type=F32)
                vj = v_ref[ks, cols]
                ves = [_masked(_head_masks(lane, h)[0], vj) for h in (2 * j, 2 * j + 1)]
                dpt_sc[j % 2, 0:2 * nk, 0:nq] = lax.dot_general(
                    jnp.concatenate(ves, axis=0), do_ref[qs, cols], NT, preferred_element_type=F32)

            def pair_grads(j):
                cols = slice(LANES * j, LANES * (j + 1))
                qj, doj = q_ref[qs, cols], do_ref[qs, cols]
                if mla:
                    qr = q_ref[qs, HW + LANES * (j // 2):HW + LANES * (j // 2 + 1)]
                pts, dsts, qms, doms = [], [], [], []
                for e in range(2):
                    h = 2 * j + e
                    me, mr = _head_masks(lane, h)
                    st = st_sc[j % 2, e * nk:(e + 1) * nk, 0:nq]
                    if with_bias:
                        st = st + b_ref[0, ks, qs]
                    pt = jnp.exp2(st - lse_ref[h:h + 1, qs])
                    dst = (pt * (dpt_sc[j % 2, e * nk:(e + 1) * nk, 0:nq] - d_ref[h:h + 1, qs])).astype(BF16)
                    pts.append(pt.astype(BF16))
                    dsts.append(dst)
                    doms.append(_masked(me, doj))
                    qm = _masked(me, qj)
                    if mla:
                        qm = jnp.concatenate([qm, _masked(mr, qr)], axis=1)
                    qms.append(qm)
                    ktl = kt_rows(slice(64 * h, 64 * h + 64), ks)
                    if mla:
                        ktl = jnp.concatenate([ktl, kpet_ref[:, ks]], axis=0)
                    dqc = jnp.dot(ktl, dst, preferred_element_type=F32)
                    dq_ref[qi, 64 * h:64 * h + 64, qs] += dqc[:64]
                    if mla:
                        dq_ref[qi, HW + MLA_ROPE * h:HW + MLA_ROPE * (h + 1), qs] += dqc[64:]
                dv_sc[ks, cols] += jnp.dot(
                    jnp.concatenate(pts, axis=1), jnp.concatenate(doms, axis=0), preferred_element_type=F32)
                dkc = jnp.dot(jnp.concatenate(dsts, axis=1), jnp.concatenate(qms, axis=0), preferred_element_type=F32)
                dk_sc[ks, cols] += dkc[:, :LANES]
                if mla:
                    dkpe_sc[ks, :] += dkc[:, LANES:]

            pair_matmuls(0)
            for j in range(HEADS // 2):
                if j + 1 < HEADS // 2:
                    pair_matmuls(j + 1)
                pair_grads(j)

        def step(with_bias, skip):
            for ks, qs in _tile_parts(blk, skip):
                tile_pass(ks, qs, with_bias)

        for idx, (with_bias, skip) in variants.items():
            if len(variants) == 1:
                step(with_bias, skip)
            else:
                pl.when(bi_r[t] == idx)(functools.partial(step, with_bias, skip))
        if single_visit:
            dq_out_ref[...] = dq_ref[0].T

        @pl.when(la_r[t] == 1)
        def _():
            dk_ref[...] = (dk_sc[...] * LN2).astype(dk_ref.dtype)
            dv_ref[...] = dv_sc[...].astype(dv_ref.dtype)
            if mla:
                dkpe_ref[...] = dkpe_sc[...] * LN2

    qmap = lambda t, qi, ki, bi, fi, la: (qi[t], 0)
    kmap = lambda t, qi, ki, bi, fi, la: (ki[t], 0)
    qmap_t = lambda t, qi, ki, bi, fi, la: (0, qi[t])
    kmap_t = lambda t, qi, ki, bi, fi, la: (0, ki[t])
    in_specs = [pl.BlockSpec((blk, qw), qmap), pl.BlockSpec((blk, HW), kmap)]
    args = [q, k]
    if mla:
        in_specs.append(pl.BlockSpec((blk, LANES), kmap))
        args.append(kpe)
    in_specs.append(pl.BlockSpec((blk, HW), kmap))
    args.append(v)
    if kt is not None:
        in_specs.append(pl.BlockSpec((HW, blk), kmap_t))
        args.append(kt)
    if mla:
        in_specs.append(pl.BlockSpec((MLA_ROPE, blk), kmap_t))
        args.append(kpet)
    in_specs += [pl.BlockSpec((1, blk, blk), lambda t, qi, ki, bi, fi, la: (bi[t], 0, 0)),
                 pl.BlockSpec((blk, HW), qmap), pl.BlockSpec((HEADS, blk), qmap_t), pl.BlockSpec((HEADS, blk), qmap_t)]
    args += [jnp.asarray(bias_t), do, lse, dstat]
    dq_shape = (seq // blk, qw, blk)
    if single_visit:
        out_specs, out_shape = [pl.BlockSpec((blk, qw), qmap)], [jax.ShapeDtypeStruct((seq, qw), F32)]
    else:
        out_specs = [pl.BlockSpec(dq_shape, lambda t, qi, ki, bi, fi, la: (0, 0, 0))]
        out_shape = [jax.ShapeDtypeStruct(dq_shape, F32)]
    out_specs.append(pl.BlockSpec((blk, HW), kmap))
    out_shape.append(jax.ShapeDtypeStruct((seq, HW), dk_dtype))
    scratch = [pltpu.VMEM((blk, HW), F32)]
    if mla:
        out_specs.append(pl.BlockSpec((blk, LANES), kmap))
        out_shape.append(jax.ShapeDtypeStruct((seq, LANES), F32))
        scratch.append(pltpu.VMEM((blk, LANES), F32))
    out_specs.append(pl.BlockSpec((blk, HW), kmap))
    out_shape.append(jax.ShapeDtypeStruct((seq, HW), BF16))
    scratch.append(pltpu.VMEM((blk, HW), F32))
    scratch += [pltpu.VMEM((2, 2 * blk, blk), F32), pltpu.VMEM((2, 2 * blk, blk), F32)]
    if single_visit:
        scratch.append(pltpu.VMEM((1, qw, blk), F32))
    body = _ride_along(body, ride, 5, len(args), len(out_shape), len(scratch), n_steps)
    if ride is not None:
        args, in_specs = args + ride.args, in_specs + ride.in_specs
        out_specs, out_shape, scratch = out_specs + ride.out_specs, out_shape + ride.out_shape, scratch + ride.scratch
    return _pcall(
        body, name=name,
        grid_spec=pltpu.PrefetchScalarGridSpec(
            num_scalar_prefetch=5, grid=(n_steps,), in_specs=in_specs, out_specs=out_specs,
            scratch_shapes=scratch),
        out_shape=out_shape,
        compiler_params=_cparams(dimension_semantics=("arbitrary",)),
    )(*steps, *args)


def _out_ln(oa, ob_near, ob_far, lse_near, lse_far, ga, gb, x, tgt, w_out, ln_g, ln_b, bt):
    seq = x.shape[0]

    def body(oa_ref, obn_ref, obf_ref, lsen_ref, lsef_ref, ga_ref, gb_ref, x_ref, tgt_ref, w_ref, g_ref, b_ref,
             dz_ref, doa_ref, dob_ref, dga_ref, dgb_ref, da_ref, db_ref, lse_ref, gwb_ref, small_ref, dobc_ref,
             gw_ref, lanes_sc):
        i = pl.program_id(0)

        @pl.when(i == 0)
        def _():
            gw_ref[...] = jnp.zeros(gw_ref.shape, F32)
            small_ref[...] = jnp.zeros(small_ref.shape, F32)

        def gate(g):
            sig = 0.5 * jnp.tanh(0.5 * g) + 0.5
            return g * sig, sig * (1.0 + g * (1.0 - sig))

        lse_n, lse_f = lsen_ref[...], lsef_ref[...]
        top = jnp.maximum(lse_n, lse_f)
        e_n, e_f = jnp.exp2(lse_n - top), jnp.exp2(lse_f - top)
        lse_ref[...] = top + jnp.log2(e_n + e_f)
        inv = 1.0 / (e_n + e_f)
        head_row = lax.broadcasted_iota(jnp.int32, (2 * HEADS, HW), 0) % HEADS
        spread = (head_row == lax.broadcasted_iota(jnp.int32, (2 * HEADS, HW), 1) // 64).astype(BF16)

        def per_lane(w):
            hi = w.astype(BF16)
            lo = (w - hi.astype(F32)).astype(BF16)
            return lax.dot_general(jnp.concatenate([hi, lo], axis=0), spread, TN, preferred_element_type=F32)

        o_b_all = per_lane(e_n * inv) * obn_ref[...] + per_lane(e_f * inv) * _in_sequence(obf_ref, lanes_sc)
        gam = g_ref[...]
        halves = [slice(0, bt // 2), slice(bt // 2, bt)]

        def gates_and_projection(rows):
            o_a, o_b = oa_ref[rows, :], o_b_all[rows]
            sa, dsa = gate(ga_ref[rows, :])
            sb, dsb = gate(gb_ref[rows, :])
            mix = jnp.concatenate([o_a * sa, o_b * sb], axis=1).astype(BF16)
            z = ALPHA * x_ref[rows, :] + jnp.dot(mix, w_ref[...], preferred_element_type=F32)
            return o_a, o_b, sa, dsa, sb, dsb, mix, z

        def norm_and_back(rows, mix, z):
            mu = jnp.mean(z, axis=1, keepdims=True)
            zc = z - mu
            rstd = lax.rsqrt(jnp.mean(zc * zc, axis=1, keepdims=True) + LN_EPS)
            xhat = zc * rstd
            diff = xhat * gam + b_ref[...] - tgt_ref[rows, :]
            dy = diff * (1.0 / D_MODEL)
            small_ref[0:1, :] += jnp.sum(dy * xhat, axis=0, keepdims=True)
            small_ref[1:2, :] += jnp.sum(dy, axis=0, keepdims=True)
            small_ref[2:3, :] += jnp.sum(diff * diff, axis=0, keepdims=True)
            dxh = dy * gam
            dz = rstd * (dxh - jnp.mean(dxh, axis=1, keepdims=True)
                         - xhat * jnp.mean(dxh * xhat, axis=1, keepdims=True))
            dz_ref[rows, :] = dz
            dzb = dz.astype(BF16)
            gw_ref[...] += lax.dot_general(mix, dzb, TN, preferred_element_type=F32)
            return lax.dot_general(dzb, w_ref[...], NT, preferred_element_type=F32)

        def gate_back(rows, o_a, o_b, sa, dsa, sb, dsb, dmix):
            doa, dob = dmix[:, :HW] * sa, dmix[:, HW:] * sb
            doa_ref[rows, :] = doa.astype(BF16)
            dob_ref[rows, :] = dob.astype(BF16)
            dga_ref[rows, :] = (dmix[:, :HW] * o_a * dsa).astype(BF16)
            dgb_ref[rows, :] = (dmix[:, HW:] * o_b * dsb).astype(BF16)
            return dob, doa * o_a, dob * o_b

        fronts = [gates_and_projection(rows) for rows in halves]
        dmixes = [norm_and_back(rows, f[6], f[7]) for rows, f in zip(halves, fronts)]
        backs = [gate_back(rows, *f[:6], dmix) for rows, f, dmix in zip(halves, fronts, dmixes)]
        dob, prod_a, prod_b = (jnp.concatenate(parts, axis=0) for parts in zip(*backs))
        _by_class(dob, dobc_ref, lanes_sc)

        @pl.when(i == seq // bt - 1)
        def _():
            gwb_ref[...] = gw_ref[...].astype(BF16)

        head_of = (lax.broadcasted_iota(jnp.int32, (2 * HW, LANES), 0) % HW) // 64
        ind = (head_of == lax.broadcasted_iota(jnp.int32, (2 * HW, LANES), 1)).astype(BF16)

        def head_sums(prod):
            hi = prod.astype(BF16)
            lo = (prod - hi.astype(F32)).astype(BF16)
            sums = jnp.dot(jnp.concatenate([hi, lo], axis=1), ind, preferred_element_type=F32)
            return sums.T[:HEADS, :]

        da_ref[...] = head_sums(prod_a)
        db_ref[...] = head_sums(prod_b)

    def tok(width):
        return pl.BlockSpec((bt, width), lambda i: (i, 0))

    def full(shape):
        return pl.BlockSpec(shape, lambda i: (0,) * len(shape))

    stat = pl.BlockSpec((HEADS, bt), lambda i: (0, i))
    n_cls = ob_far.shape[0]
    by_class = pl.BlockSpec((n_cls, bt // n_cls, HW), lambda i: (0, i, 0))
    return _pcall(
        body, name="out_ln", grid=(seq // bt,),
        in_specs=[tok(HW), tok(HW), by_class, stat, stat, tok(HW), tok(HW), tok(D_MODEL), tok(D_MODEL),
                  full((D_MODEL, D_MODEL)), full((1, D_MODEL)), full((1, D_MODEL))],
        out_specs=[tok(D_MODEL), tok(HW), tok(HW), tok(HW), tok(HW), stat, stat, stat,
                   full((D_MODEL, D_MODEL)), full((8, D_MODEL)), by_class],
        out_shape=[jax.ShapeDtypeStruct((seq, D_MODEL), F32)] + [jax.ShapeDtypeStruct((seq, HW), BF16)] * 4
        + [jax.ShapeDtypeStruct((HEADS, seq), F32)] * 3
        + [jax.ShapeDtypeStruct((D_MODEL, D_MODEL), BF16), jax.ShapeDtypeStruct((8, D_MODEL), F32),
           jax.ShapeDtypeStruct(ob_far.shape, BF16)],
        scratch_shapes=[pltpu.VMEM((D_MODEL, D_MODEL), F32), pltpu.VMEM((HW // LANES, bt, LANES), F32)],
        compiler_params=_cparams(dimension_semantics=("arbitrary",)),
    )(oa, ob_near, ob_far, lse_near, lse_far, ga, gb, x, tgt, w_out, ln_g, ln_b)


def _bwd_mid(dq_m, dkn, dv, dkpe, dqb, dkb, dvb, far, dga, dgb, cq, ckv, qn, kvn, w_uq_r, w_ukv_r, qg, kvg, tabs, bt):
    n_cls = far[0].shape[0]
    seq = cq.shape[0]

    def body(dqm_ref, dkn_ref, dv_ref, dkpe_ref, dqb_ref, dkb_ref, dvb_ref, dqf_ref, dkf_ref, dvf_ref, dga_ref, dgb_ref,
             cq_ref, ckv_ref, qn_ref, kvn_ref, wuq_ref, wukv_ref, qg_ref, kvg_ref, tab_ref,
             dh_ref, guq3_ref, gukv3_ref, small_ref, seq_sc, guq_ref, gukv_ref):
        i = pl.program_id(0)

        @pl.when(i == 0)
        def _():
            guq_ref[...] = jnp.zeros(guq_ref.shape, F32)
            gukv_ref[...] = jnp.zeros(gukv_ref.shape, F32)
            small_ref[...] = jnp.zeros(small_ref.shape, F32)

        m_tabs = (tab_ref[0], tab_ref[1], tab_ref[2])
        d_tabs = (tab_ref[3], tab_ref[4], tab_ref[5])

        def rms_bwd(c, dn, gain):
            r = lax.rsqrt(jnp.mean(c * c, axis=1, keepdims=True) + RMS_EPS)
            u = dn * gain
            dc = r * u - c * (r * r * r) * jnp.mean(u * c, axis=1, keepdims=True)
            return dc, jnp.sum(dn * c * r, axis=0, keepdims=True)

        dqm = dqm_ref[0].T
        dq = jnp.concatenate(
            [dqm[:, :HW], _rope_wide(_rope_t, dqm[:, HW:], *m_tabs, MLA_ROPE // 2)], axis=1) * MLA_SCALE
        dq = dq.astype(BF16)
        dkv = jnp.concatenate([dkn_ref[...], dv_ref[...]], axis=1)
        guq_ref[...] += lax.dot_general(qn_ref[...], dq, TN, preferred_element_type=F32)
        dqn = lax.dot_general(dq, wuq_ref[...], NT, preferred_element_type=F32)
        gukv_ref[...] += lax.dot_general(kvn_ref[...], dkv, TN, preferred_element_type=F32)
        dkvn = lax.dot_general(dkv, wukv_ref[...], NT, preferred_element_type=F32)

        dh_ref[:, C_KR:C_GA] = _rope_t(dkpe_ref[...], *m_tabs, MLA_ROPE // 2).astype(BF16)
        dh_ref[:, C_GA:C_QB] = dga_ref[...]
        in_sequence = functools.partial(_in_sequence, lanes_sc=seq_sc)
        dqb = dqb_ref[0].T + in_sequence(dqf_ref)
        dh_ref[:, C_QB:C_KB] = (_rope_wide(_rope_t, dqb, *d_tabs, DIL_ROT // 2) * DIL_SCALE).astype(BF16)
        dkb = dkb_ref[...] + in_sequence(dkf_ref)
        dh_ref[:, C_KB:C_VB] = _rope_wide(_rope_t, dkb, *d_tabs, DIL_ROT // 2).astype(BF16)
        dh_ref[:, C_VB:C_GB] = (dvb_ref[...].astype(F32) + in_sequence(dvf_ref)).astype(BF16)
        dh_ref[:, C_GB:C_END] = dgb_ref[...]

        dcq, gq = rms_bwd(cq_ref[...], dqn, qg_ref[...])
        small_ref[0:1, :] += gq
        dckv, gkv = rms_bwd(ckv_ref[...], dkvn, kvg_ref[...])
        small_ref[1:2, :KV_RANK] += gkv
        dh_ref[:, C_CQ:C_CKV] = dcq.astype(BF16)
        dh_ref[:, C_CKV:C_KR] = dckv.astype(BF16)

        @pl.when(i == seq // bt - 1)
        def _():
            for h in range(HEADS):
                guq3_ref[h] = jnp.concatenate(
                    [guq_ref[:, MLA_NOPE * h:MLA_NOPE * (h + 1)],
                     guq_ref[:, HW + MLA_ROPE * h:HW + MLA_ROPE * (h + 1)]], axis=1).astype(BF16)
                gukv3_ref[h] = jnp.concatenate(
                    [gukv_ref[:, MLA_NOPE * h:MLA_NOPE * (h + 1)],
                     gukv_ref[:, HW + MLA_V * h:HW + MLA_V * (h + 1)]], axis=1).astype(BF16)

    def tok(width):
        return pl.BlockSpec((bt, width), lambda i: (i, 0))

    def tok_t(a):
        per = a.shape[2] // bt
        return pl.BlockSpec((1, a.shape[1], bt), lambda i: (i // per, 0, i % per))

    def full(shape):
        return pl.BlockSpec(shape, lambda i: (0,) * len(shape))

    by_class = pl.BlockSpec((n_cls, bt // n_cls, HW), lambda i: (0, i, 0))
    uq3 = (HEADS, Q_RANK, MLA_NOPE + MLA_ROPE)
    ukv3 = (HEADS, KV_RANK, MLA_NOPE + MLA_V)
    return _pcall(
        body, name="bwd_mid", grid=(seq // bt,),
        in_specs=[tok_t(dq_m), tok(HW), tok(HW), tok(LANES), tok_t(dqb), tok(HW), tok(HW), by_class, by_class, by_class,
                  tok(HW), tok(HW),
                  tok(Q_RANK), tok(KV_RANK), tok(Q_RANK), tok(KV_RANK),
                  full(w_uq_r.shape), full(w_ukv_r.shape), full((1, Q_RANK)), full((1, KV_RANK)),
                  pl.BlockSpec((6, bt, LANES), lambda i: (0, i, 0))],
        out_specs=[tok(C_END), full(uq3), full(ukv3), full((8, Q_RANK))],
        out_shape=[jax.ShapeDtypeStruct((seq, C_END), BF16), jax.ShapeDtypeStruct(uq3, BF16),
                   jax.ShapeDtypeStruct(ukv3, BF16), jax.ShapeDtypeStruct((8, Q_RANK), F32)],
        scratch_shapes=[pltpu.VMEM((HW // LANES, bt, LANES), F32), pltpu.VMEM(w_uq_r.shape, F32),
                        pltpu.VMEM(w_ukv_r.shape, F32)],
        compiler_params=_cparams(dimension_semantics=("arbitrary",)),
    )(dq_m, dkn, dv, dkpe, dqb, dkb, dvb, *far, dga, dgb, cq, ckv, qn, kvn, w_uq_r, w_ukv_r, qg, kvg, tabs)


def _grad_x(dz, dh, w_in_r, bt, ride=None):
    seq = dz.shape[0]
    n_steps = seq // bt

    def body(dz_ref, dh_ref, w_ref, gx_ref):
        gx_ref[...] = ALPHA * dz_ref[...] + lax.dot_general(
            dh_ref[...], w_ref[...], NT, preferred_element_type=F32)

    args = [dz, dh, w_in_r]
    in_specs = [pl.BlockSpec((bt, D_MODEL), lambda i: (i, 0)), pl.BlockSpec((bt, C_END), lambda i: (i, 0)),
                pl.BlockSpec(w_in_r.shape, lambda i: (0, 0))]
    out_specs = [pl.BlockSpec((bt, D_MODEL), lambda i: (i, 0))]
    out_shape = [jax.ShapeDtypeStruct((seq, D_MODEL), F32)]
    scratch = []
    body = _ride_along(body, ride, 0, len(args), len(out_shape), 0, n_steps)
    if ride is not None:
        args, in_specs = args + ride.args, in_specs + ride.in_specs
        out_specs, out_shape, scratch = out_specs + ride.out_specs, out_shape + ride.out_shape, ride.scratch
    return _pcall(
        body, name="grad_x", grid=(n_steps,),
        in_specs=in_specs, out_specs=out_specs, out_shape=out_shape, scratch_shapes=scratch,
        compiler_params=_cparams(dimension_semantics=("arbitrary",)),
    )(*args)


def _grad_w_in(x, dh, bt, cut, ride):
    seq = x.shape[0]
    n_tok = seq // bt
    shard = IN_WIDTH // N_DEV
    k_lo, k_hi = IN_SPLITS[0] + IN_SPLITS[1], IN_SPLITS[0] + IN_SPLITS[1] + MLA_ROPE

    def body(x_ref, dh_ref, rest_ref, acc, first_ref):
        i = pl.program_id(0)
        for part, (r_lo, r_hi, dst_ref) in enumerate([(0, cut, first_ref), (cut, D_MODEL, rest_ref)]):
            rows = slice(0, r_hi - r_lo)

            @pl.when(i == part * n_tok)
            def _():
                acc[rows, :] = jnp.zeros((r_hi - r_lo, C_END), F32)

            @pl.when(i // n_tok == part)
            def _():
                acc[rows, :] += lax.dot_general(
                    x_ref[:, r_lo:r_hi].astype(BF16), dh_ref[...], TN, preferred_element_type=F32)

            @pl.when(i == (part + 1) * n_tok - 1)
            def _():
                kr = acc[rows, C_KR:C_GA]
                kr = kr + pltpu.roll(kr, 96, 1) + pltpu.roll(kr, 64, 1) + pltpu.roll(kr, 32, 1)
                for d in range(N_DEV):
                    lo, hi = shard * d, shard * (d + 1)
                    pieces = []
                    if lo < k_lo:
                        pieces.append(acc[rows, lo:min(hi, k_lo)])
                    if lo < k_hi and hi > k_lo:
                        pieces.append(kr[:, max(lo, k_lo) - k_lo:min(hi, k_hi) - k_lo])
                    if hi > k_hi:
                        shift = C_GA - k_hi
                        pieces.append(acc[rows, max(lo, k_hi) + shift:hi + shift])
                    blk = pieces[0] if len(pieces) == 1 else jnp.concatenate(pieces, axis=1)
                    dst_ref[d] = blk.astype(BF16)

    args = [x, dh]
    in_specs = [pl.BlockSpec((bt, D_MODEL), lambda i: (i % n_tok, 0)),
                pl.BlockSpec((bt, C_END), lambda i: (i % n_tok, 0))]
    out_specs = [pl.BlockSpec((N_DEV, D_MODEL - cut, shard), lambda i: (0, 0, 0))]
    out_shape = [jax.ShapeDtypeStruct((N_DEV, D_MODEL - cut, shard), BF16)]
    scratch = [pltpu.VMEM((max(cut, D_MODEL - cut), C_END), F32)]
    assert ride.shared == 1
    body = _ride_along(body, ride, 0, len(args), len(out_shape), len(scratch), 2 * n_tok)
    return _pcall(
        body, name="grad_w_in", grid=(2 * n_tok,),
        in_specs=in_specs + ride.in_specs, out_specs=out_specs + ride.out_specs,
        out_shape=out_shape + ride.out_shape, scratch_shapes=scratch + ride.scratch,
        compiler_params=_cparams(dimension_semantics=("arbitrary",)),
    )(*args, *ride.args)


def _local_step(x, tgt, w_in_r, w_uq_r, w_ukv_r, w_out_rider, g_out_rider, reduce_rider, q_norm_g, kv_norm_g,
                ln_g, ln_b, bt=BLOCK_TOKENS, blk_m=BLOCK_MLA, blk_d=BLOCK_DIL):
    seq = x.shape[0]
    tabs = jnp.asarray(_rope_tables(seq))
    qg, kvg = q_norm_g.reshape(1, -1), kv_norm_g.reshape(1, -1)

    far_dil = DIL_CONFIGS[-1][1]
    cls = seq // far_dil
    (cq, ckv, qn, kvn, qcat, kn, kpe, v, ga, gb, qb, kb, vb, knt, kpet, vt, kbt, vbt, qb_c, kb_c, vb_c) = _fwd_proj(
        x, w_in_r, w_uq_r, w_ukv_r, qg, kvg, tabs, bt, far_dil)
    qb_c, kb_c, vb_c = (a.reshape(seq, HW) for a in (qb_c, kb_c, vb_c))

    nq_m, nq_d = seq // blk_m, seq // blk_d
    bias_m = _mla_bias_t(blk_m)
    oa, lse_a, w_out = _attn_fwd(
        "mla_fwd", qcat, kn, kpe, vt, bias_m, _steps(nq_m, nq_m, False, True), blk_m, ride=w_out_rider)

    bias_near = _dil_bias_t(blk_d, DIL_NEAR)
    ob_near, lse_near = _attn_fwd(
        "dil_fwd", qb, kb, None, vbt, bias_near, _steps(nq_d, -(-DIL_NEAR // blk_d), False, False), blk_d)
    each = np.arange(far_dil, dtype=np.int32)
    steps_far = [jnp.asarray(v) for v in (each, each, np.zeros_like(each), np.ones_like(each), np.ones_like(each))]
    bias_far = _dil_far_bias_t(cls)
    ob_far, lse_far = _attn_fwd(
        "dil_far_fwd", qb_c, kb_c, None, vb_c, bias_far, steps_far, cls, v_token_major=True)

    dz, doa, dob, dga, dgb, dst_a, dst_b, lse_b, g_out, small1, dob_c = _out_ln(
        oa, ob_near, ob_far.reshape(far_dil, cls, HW), lse_near, _lanes_from_classes(lse_far, far_dil), ga, gb, x, tgt,
        w_out.reshape(D_MODEL, D_MODEL), ln_g.reshape(1, -1), ln_b.reshape(1, -1), bt)

    dq_m, dkn, dkpe, dv, g_out_recv = _attn_bwd(
        "mla_bwd", qcat, kn, kpe, v, knt, kpet, bias_m, doa, lse_a, dst_a, _steps(nq_m, nq_m, True, True), blk_m,
        ride=g_out_rider(g_out.reshape(N_DEV, D_MODEL // N_DEV, D_MODEL)))
    dqb, dkb_near, dvb_near = _attn_bwd(
        "dil_bwd", qb, kb, None, vb, kbt, None, bias_near, dob, lse_b, dst_b,
        _steps(nq_d, -(-DIL_NEAR // blk_d), True, False), blk_d)
    dqb_far, dkb_far, dvb_far = _attn_bwd(
        "dil_far_bwd", qb_c, kb_c, None, vb_c, None, None, bias_far, dob_c.reshape(seq, HW),
        _lanes_to_classes(lse_b, far_dil), _lanes_to_classes(dst_b, far_dil), steps_far, cls, single_visit=True)
    far = [a.reshape(far_dil, cls, HW) for a in (dqb_far, dkb_far, dvb_far)]

    dh, g_uq, g_ukv, small2 = _bwd_mid(
        dq_m, dkn, dv, dkpe, dqb, dkb_near, dvb_near, far, dga, dgb, cq, ckv, qn, kvn, w_uq_r, w_ukv_r, qg, kvg, tabs, bt)
    bt_w = min(seq, 2 * bt)
    n_tok = seq // bt_w
    first = jax.ShapeDtypeStruct((N_DEV, GRAD_W_IN_CUT, IN_WIDTH // N_DEV), BF16)
    g_in_rest, *reduced_first = _grad_w_in(
        x, dh, bt_w, GRAD_W_IN_CUT,
        reduce_rider([first, g_uq, g_ukv], g_out_recv, (small1, small2), (n_tok, n_tok, 2 * n_tok - 2)))
    grad_x, g_in_rest = _grad_x(dz, dh, w_in_r, bt, ride=reduce_rider([g_in_rest], None, None, (0, 1, seq // bt - 2)))
    return grad_x, [g_in_rest] + reduced_first


MESH_ID = pl.DeviceIdType.MESH
SHARD_SHAPES = ((D_MODEL, IN_WIDTH // N_DEV), (Q_RANK, 768 // N_DEV), (KV_RANK, 1024 // N_DEV), (D_MODEL // N_DEV, D_MODEL))
ADAM_ROWS = (32, 128, 128, 16)


def _me():
    x, y, c = lax.axis_index("x"), lax.axis_index("y"), lax.axis_index("c")
    return x, y, c, 4 * x + 2 * y + c


def _peer(k):
    x, y, c, _ = _me()
    px = 1 - x if (k >> 2) & 1 else x
    py = 1 - y if (k >> 1) & 1 else y
    pc = 1 - c if k & 1 else c
    return (px, py, pc), 4 * px + 2 * py + pc


def _all_gather_weights(shards):
    n = len(shards)
    shard = IN_WIDTH // N_DEV
    k_lo = IN_SPLITS[0] + IN_SPLITS[1]
    k_hi = k_lo + MLA_ROPE

    def body(*refs):
        ins = refs[:n]
        win_ref, wuq_ref, wukv_ref = refs[n:2 * n]
        bufs = refs[2 * n:3 * n]
        send_sems, recv_sems = refs[3 * n:]
        x, y, c, me = _me()
        here, sibling = (x, y, c), (x, y, 1 - c)
        along_x, along_y, across = (1 - x, y), (x, 1 - y), (1 - x, 1 - y)
        for t in range(n):
            bufs[t][me] = ins[t][...].astype(BF16)

        def copy(t, k, chip, pc, to, half=None):
            blk = bufs[t].at[4 * chip[0] + 2 * chip[1] + pc]
            if half is not None:
                rows = SHARD_SHAPES[t][0] // 2
                blk = blk.at[pl.ds(half * rows, rows), :]
            return pltpu.make_async_remote_copy(
                src_ref=blk, dst_ref=blk, send_sem=send_sems.at[t, k], recv_sem=recv_sems.at[t, k],
                device_id=to, device_id_type=MESH_ID)

        sends = []
        for t in range(n):
            sends += [copy(t, 0, (x, y), c, sibling), copy(t, 1, (x, y), c, (*along_x, c)),
                      copy(t, 2, (x, y), c, (*along_y, c))]
        for cp in sends:
            cp.start()
        for t in range(n):
            copy(t, 1, along_x, c, here).wait_recv()
            sends += [copy(t, 3, along_x, c, (*along_y, c), half=0), copy(t, 5, along_x, c, sibling)]
            sends[-2].start()
            sends[-1].start()
        for t in range(n):
            copy(t, 2, along_y, c, here).wait_recv()
            sends += [copy(t, 4, along_y, c, (*along_x, c), half=1), copy(t, 6, along_y, c, sibling)]
            sends[-2].start()
            sends[-1].start()
        for t in range(n):
            copy(t, 3, across, c, here, half=0).wait_recv()
            copy(t, 4, across, c, here, half=1).wait_recv()
            sends.append(copy(t, 7, across, c, sibling))
            sends[-1].start()
        for t in range(n):
            copy(t, 0, (x, y), 1 - c, here).wait_recv()
            for k, chip in ((5, along_x), (6, along_y), (7, across)):
                copy(t, k, chip, 1 - c, here).wait_recv()
        for cp in sends:
            cp.wait_send()

        a_in, a_uq, a_ukv = bufs
        for d in range(N_DEV):
            lo, hi = shard * d, shard * (d + 1)
            if lo < k_lo:
                win_ref[:, lo:min(hi, k_lo)] = a_in[d, :, 0:min(hi, k_lo) - lo]
            if lo < k_hi and hi > k_lo:
                kr = a_in[d, :, k_lo - lo:k_hi - lo]
                for rep in range(4):
                    win_ref[:, C_KR + MLA_ROPE * rep:C_KR + MLA_ROPE * (rep + 1)] = kr
            if hi > k_hi:
                src = max(lo, k_hi)
                win_ref[:, src + C_GA - k_hi:hi + C_GA - k_hi] = a_in[d, :, src - lo:hi - lo]
        for h in range(HEADS):
            wuq_ref[:, MLA_NOPE * h:MLA_NOPE * (h + 1)] = a_uq[h, :, :MLA_NOPE]
            wuq_ref[:, HW + MLA_ROPE * h:HW + MLA_ROPE * (h + 1)] = a_uq[h, :, MLA_NOPE:]
            wukv_ref[:, MLA_NOPE * h:MLA_NOPE * (h + 1)] = a_ukv[h, :, :MLA_NOPE]
            wukv_ref[:, HW + MLA_V * h:HW + MLA_V * (h + 1)] = a_ukv[h, :, MLA_NOPE:]

    vmem = pl.BlockSpec(memory_space=pltpu.VMEM)
    return _pcall(
        body, name="gather_weights",
        in_specs=[vmem] * n, out_specs=[vmem] * n,
        out_shape=[jax.ShapeDtypeStruct((D_MODEL, C_END), BF16), jax.ShapeDtypeStruct((Q_RANK, QW), BF16),
                   jax.ShapeDtypeStruct((KV_RANK, 2 * HW), BF16)],
        scratch_shapes=[pltpu.VMEM((N_DEV,) + s, BF16) for s in SHARD_SHAPES[:n]]
        + [pltpu.SemaphoreType.DMA((n, 8)), pltpu.SemaphoreType.DMA((n, 8))],
        compiler_params=_cparams(),
    )(*shards)


def _gather_w_out_rider(w_out):
    def copies(full_ref, stage, send_sems, recv_sems):
        me = _me()[3]
        out = []
        for k in range(1, N_DEV):
            peer, pidx = _peer(k)
            send = pltpu.make_async_remote_copy(
                src_ref=stage, dst_ref=full_ref.at[me], send_sem=send_sems.at[k - 1], recv_sem=recv_sems.at[k - 1],
                device_id=peer, device_id_type=MESH_ID)
            recv = pltpu.make_async_remote_copy(
                src_ref=stage, dst_ref=full_ref.at[pidx], send_sem=send_sems.at[k - 1], recv_sem=recv_sems.at[k - 1],
                device_id=peer, device_id_type=MESH_ID)
            out.append((send, recv))
        return out

    def start(ins, outs, scr):
        stage, send_sems, recv_sems, own_sem = scr
        stage[...] = ins[0][...].astype(BF16)
        pltpu.make_async_copy(stage, outs[0].at[_me()[3]], own_sem).start()
        for send, _ in copies(outs[0], stage, send_sems, recv_sems):
            send.start()

    def finish(ins, outs, scr):
        stage, send_sems, recv_sems, own_sem = scr
        pltpu.make_async_copy(stage, outs[0].at[_me()[3]], own_sem).wait()
        pairs = copies(outs[0], stage, send_sems, recv_sems)
        for _, recv in pairs:
            recv.wait_recv()
        for send, _ in pairs:
            send.wait_send()

    shape = SHARD_SHAPES[3]
    return Rider(
        args=[w_out], in_specs=[pl.BlockSpec(shape, lambda t, *_: (0, 0))],
        out_shape=[jax.ShapeDtypeStruct((N_DEV,) + shape, BF16)], out_specs=[pl.BlockSpec(memory_space=pl.ANY)],
        scratch=[pltpu.VMEM(shape, BF16), pltpu.SemaphoreType.DMA((N_DEV - 1,)), pltpu.SemaphoreType.DMA((N_DEV - 1,)),
                 pltpu.SemaphoreType.DMA],
        start=start, finish=finish)


def _scatter_g_out_rider(blocks):
    def copies(src_ref, dst_ref, send_sems, recv_sems):
        out = []
        for k in range(1, N_DEV):
            peer, pidx = _peer(k)
            out.append(pltpu.make_async_remote_copy(
                src_ref=src_ref.at[pidx], dst_ref=dst_ref.at[k], send_sem=send_sems.at[k - 1],
                recv_sem=recv_sems.at[k - 1], device_id=peer, device_id_type=MESH_ID))
        return out

    def start(ins, outs, scr):
        send_sems, recv_sems, own_sem = scr
        pltpu.make_async_copy(ins[0].at[_me()[3]], outs[0].at[0], own_sem).start()
        for cp in copies(ins[0], outs[0], send_sems, recv_sems):
            cp.start()

    def finish(ins, outs, scr):
        send_sems, recv_sems, own_sem = scr
        pltpu.make_async_copy(ins[0].at[_me()[3]], outs[0].at[0], own_sem).wait()
        for cp in copies(ins[0], outs[0], send_sems, recv_sems):
            cp.wait()

    hbm = pl.BlockSpec(memory_space=pl.ANY)
    return Rider(
        args=[blocks], in_specs=[hbm], out_shape=[jax.ShapeDtypeStruct(blocks.shape, blocks.dtype)], out_specs=[hbm],
        scratch=[pltpu.SemaphoreType.DMA((N_DEV - 1,)), pltpu.SemaphoreType.DMA((N_DEV - 1,)), pltpu.SemaphoreType.DMA],
        start=start, finish=finish)


def _adamw(w, g, m, v):
    m = ADAM_B1 * m + (1.0 - ADAM_B1) * g
    v = ADAM_B2 * v + (1.0 - ADAM_B2) * jnp.square(g)
    m_hat = m / (1.0 - ADAM_B1 ** ADAM_STEP)
    v_hat = v / (1.0 - ADAM_B2 ** ADAM_STEP)
    delta = -ADAM_LR * (m_hat / (jnp.sqrt(v_hat) + ADAM_EPS) + ADAM_WD * w)
    return delta, m, v


def _chunk_rows(count, cols):
    rows = (16 * 8 * LANES) // (-(-cols // LANES) * LANES)
    while count % rows:
        rows //= 2
    return rows


SMALL_ROWS = (0, 1, 3, 4)
LOSS_ROW = 2


def _reduce_grads_rider(grads3, arrived, small_parts, steps):
    n = len(grads3)
    shapes = [tuple(g.shape[1:]) for g in grads3]
    with_small = arrived is not None
    start_step, first_round, second_round = steps
    n_held = sum(isinstance(g, jax.ShapeDtypeStruct) for g in grads3)
    held, handed = grads3[:n_held], grads3[n_held:]
    assert not any(isinstance(g, jax.ShapeDtypeStruct) for g in handed)

    class Refs:
        def __init__(self, ins, outs, scr):
            self.g3, self.gsum = list(scr[0:n_held]) + list(ins[0:n - n_held]), outs[0:n]
            ins, scr = ins[n - n_held:], scr[n_held:]
            if with_small:
                self.arr, self.sp_wide, self.sp_q = ins
                self.gsum_out, self.ssum = outs[n:n + 2]
            self.own, self.sib, self.part = scr[0:n], scr[n:2 * n], scr[2 * n:3 * n]
            self.in_a, self.out_b, self.in_b = scr[3 * n:4 * n], scr[4 * n:5 * n], scr[5 * n:6 * n]
            self.rsmall = scr[6 * n]
            (self.loc_sems, self.d2d_send, self.d2d_recv, self.a_send, self.a_recv, self.b_send, self.b_recv,
             self.sm_send, self.sm_recv) = scr[6 * n + 1:]
            self.x, self.y, self.c, self.me = _me()
            self.along_x, self.along_y = (1 - self.x, self.y), (self.x, 1 - self.y)
            self.across = (1 - self.x, 1 - self.y)

        def small(self):
            if not with_small:
                return []
            return [pltpu.make_async_remote_copy(
                src_ref=self.rsmall.at[0], dst_ref=self.rsmall.at[k], send_sem=self.sm_send.at[k - 1],
                recv_sem=self.sm_recv.at[k - 1], device_id=_peer(k)[0], device_id_type=MESH_ID)
                for k in range(1, N_DEV)]

        def level1(self):
            local, to_sib = [], []
            for t in range(n):
                for q in range(4):
                    local.append(pltpu.make_async_copy(
                        self.g3[t].at[2 * q + self.c], self.own[t].at[q], self.loc_sems.at[t, q]))
                    to_sib.append(pltpu.make_async_remote_copy(
                        src_ref=self.g3[t].at[2 * q + 1 - self.c], dst_ref=self.sib[t].at[q],
                        send_sem=self.d2d_send.at[t, q], recv_sem=self.d2d_recv.at[t, q],
                        device_id=(self.x, self.y, 1 - self.c), device_id_type=MESH_ID))
            return local, to_sib

        def round_a(self):
            out = []
            for t in range(n):
                half = shapes[t][0] // 2
                for k, (to, chip, h) in enumerate([(self.along_x, self.along_x, 0), (self.along_x, self.across, 0),
                                                   (self.along_y, self.along_y, 1), (self.along_y, self.across, 1)]):
                    out.append(pltpu.make_async_remote_copy(
                        src_ref=self.part[t].at[2 * chip[0] + chip[1], pl.ds(h * half, half), :],
                        dst_ref=self.in_a[t].at[k], send_sem=self.a_send.at[t, k], recv_sem=self.a_recv.at[t, k],
                        device_id=(*to, self.c), device_id_type=MESH_ID))
            return out

        def round_b(self):
            return [pltpu.make_async_remote_copy(
                src_ref=self.out_b[t].at[k], dst_ref=self.in_b[t].at[k], send_sem=self.b_send.at[t, k],
                recv_sem=self.b_recv.at[t, k], device_id=(*to, self.c), device_id_type=MESH_ID)
                for t in range(n) for k, to in enumerate([self.along_y, self.along_x])]

    def chunks(rows, count, fn):
        def step(i, carry):
            fn(pl.multiple_of(i * rows, rows))
            return carry

        lax.fori_loop(0, count // rows, step, 0)

    def start(*refs):
        r = Refs(*refs)
        if with_small:
            r.rsmall[0] = r.sp_wide[...]
            for k, row in enumerate(SMALL_ROWS[2:]):
                r.rsmall[0, row:row + 1, 0:Q_RANK] = r.sp_q[k:k + 1, :]
        local, to_sib = r.level1()
        for cp in r.small() + local + to_sib:
            cp.start()

    def begin_rounds(*refs):
        r = Refs(*refs)
        local, to_sib = r.level1()
        for cp in local:
            cp.wait()
        for cp in to_sib:
            cp.wait_recv()
        my_chip = 2 * r.x + r.y
        for t in range(n):
            rows = _chunk_rows(*shapes[t])

            def pair_sums(at, t=t, rows=rows):
                sl = pl.ds(at, rows)
                for q in range(4):
                    r.part[t][q, sl, :] = (r.own[t][q, sl, :].astype(F32) + r.sib[t][q, sl, :].astype(F32)).astype(BF16)
                r.gsum[t][sl, :] = r.own[t][my_chip, sl, :].astype(F32) + r.sib[t][my_chip, sl, :].astype(F32)

            chunks(rows, shapes[t][0], pair_sums)
        for cp in r.round_a():
            cp.start()
        if not with_small:
            return
        rows_out = _chunk_rows(*SHARD_SHAPES[3])

        def add_arrived(at):
            sl = pl.ds(at, rows_out)
            g = r.arr[0, sl, :].astype(F32)
            for k in range(1, N_DEV):
                g = g + r.arr[k, sl, :].astype(F32)
            r.gsum_out[sl, :] = g

        chunks(rows_out, SHARD_SHAPES[3][0], add_arrived)

    def pass_on(*refs):
        r = Refs(*refs)
        for cp in r.round_a():
            cp.wait_recv()
        q_x, q_y = 2 * r.along_x[0] + r.along_x[1], 2 * r.along_y[0] + r.along_y[1]
        for t in range(n):
            half = shapes[t][0] // 2
            rows = _chunk_rows(half, shapes[t][1])

            def add(at, t=t, rows=rows, half=half):
                lo, hi = pl.ds(at, rows), pl.ds(half + at, rows)
                r.gsum[t][lo, :] = r.gsum[t][lo, :] + r.in_a[t][0, lo, :].astype(F32)
                r.out_b[t][0, lo, :] = (r.part[t][q_y, lo, :].astype(F32) + r.in_a[t][1, lo, :].astype(F32)).astype(BF16)
                r.gsum[t][hi, :] = r.gsum[t][hi, :] + r.in_a[t][2, lo, :].astype(F32)
                r.out_b[t][1, lo, :] = (r.part[t][q_x, hi, :].astype(F32) + r.in_a[t][3, lo, :].astype(F32)).astype(BF16)

            chunks(rows, half, add)
        for cp in r.round_b():
            cp.start()

    def finish(*refs):
        r = Refs(*refs)
        passed = r.round_b()
        for cp in passed:
            cp.wait_recv()
        for t in range(n):
            half = shapes[t][0] // 2
            rows = _chunk_rows(half, shapes[t][1])

            def add(at, t=t, rows=rows, half=half):
                lo, hi = pl.ds(at, rows), pl.ds(half + at, rows)
                r.gsum[t][lo, :] = r.gsum[t][lo, :] + r.in_b[t][0, lo, :].astype(F32)
                r.gsum[t][hi, :] = r.gsum[t][hi, :] + r.in_b[t][1, lo, :].astype(F32)

            chunks(rows, half, add)
        small = r.small()
        for cp in small:
            cp.wait_recv()
        if with_small:
            tot = r.rsmall[r.me]
            for d in range(1, N_DEV):
                tot = tot + r.rsmall[jnp.bitwise_xor(r.me, d)]
            r.ssum[...] = tot
        for cp in small + r.level1()[1] + r.round_a() + passed:
            cp.wait_send()

    hbm = pl.BlockSpec(memory_space=pl.ANY)
    dma = pltpu.SemaphoreType.DMA

    def whole(shape):
        return pl.BlockSpec(shape, lambda i: (0,) * len(shape))

    def halves(slots):
        return [pltpu.VMEM((slots, s[0] // 2, s[1]), BF16) for s in shapes]

    small_args = [arrived, *small_parts] if with_small else []
    out_shapes = shapes + ([SHARD_SHAPES[3], (8, D_MODEL)] if with_small else [])
    return Rider(
        args=list(handed) + small_args,
        in_specs=[hbm] * len(handed) + [whole(a.shape) for a in small_args],
        out_shape=[jax.ShapeDtypeStruct(s, F32) for s in out_shapes], out_specs=[whole(s) for s in out_shapes],
        scratch=[pltpu.VMEM(g.shape, g.dtype) for g in held]
        + [pltpu.VMEM((4,) + s, BF16) for _ in range(3) for s in shapes]
        + halves(4) + halves(2) + halves(2)
        + [pltpu.VMEM((N_DEV, 8, D_MODEL), F32), dma((n, 4)), dma((n, 4)), dma((n, 4)), dma((n, 4)), dma((n, 4)),
           dma((n, 2)), dma((n, 2)), dma((N_DEV - 1,)), dma((N_DEV - 1,))],
        start=start, finish=finish, stages=((first_round, begin_rounds), (second_round, pass_on)),
        start_step=start_step, shared=n_held)


def _adamw_update(grads, small_grad, wmv, small_wmv):
    n_small = len(small_wmv)
    flat_grads = [g for pieces in grads for g in pieces]
    n_g = len(flat_grads)

    def body(*refs):
        g_refs, sg_ref, refs = iter(refs[0:n_g]), refs[n_g], refs[n_g + 1:]
        wmv_refs = [refs[3 * t:3 * t + 3] for t in range(4)]
        swmv_refs = [refs[12 + 3 * t:15 + 3 * t] for t in range(n_small)]
        outs = refs[12 + 3 * n_small:]
        out_refs = [outs[4 * t:4 * t + 4] for t in range(4)]
        sout_refs = [outs[16 + 4 * t:20 + 4 * t] for t in range(n_small)]
        loss_ref = outs[16 + 4 * n_small]
        for t, (w_ref, m_ref, v_ref) in enumerate(swmv_refs):
            g = sg_ref[SMALL_ROWS[t]:SMALL_ROWS[t] + 1, :w_ref.shape[1]]
            delta, m, v = _adamw(w_ref[...], g, m_ref[...], v_ref[...])
            sout_refs[t][0][...], sout_refs[t][1][...], sout_refs[t][2][...], sout_refs[t][3][...] = g, delta, m, v
        loss_ref[...] = (0.5 / D_MODEL) * jnp.sum(sg_ref[LOSS_ROW:LOSS_ROW + 1, :], axis=1, keepdims=True)
        for t in range(4):
            rows = ADAM_ROWS[t]
            w_ref, m_ref, v_ref = wmv_refs[t]
            g_out, d_out, m_out, v_out = out_refs[t]

            def step(i, carry, g_ref, first, rows=rows, w_ref=w_ref, m_ref=m_ref, v_ref=v_ref,
                     g_out=g_out, d_out=d_out, m_out=m_out, v_out=v_out):
                at = pl.multiple_of(i * rows, rows)
                r = pl.ds(pl.multiple_of(first + at, rows), rows)
                g = g_ref[pl.ds(at, rows), :]
                delta, m, v = _adamw(w_ref[r, :], g, m_ref[r, :], v_ref[r, :])
                g_out[r, :], d_out[r, :], m_out[r, :], v_out[r, :] = g, delta, m, v
                return carry

            first = 0
            for piece in grads[t]:
                lax.fori_loop(0, piece.shape[0] // rows, functools.partial(step, g_ref=next(g_refs), first=first), 0)
                first += piece.shape[0]

    vmem = pl.BlockSpec(memory_space=pltpu.VMEM)
    flat_wmv = [a for trio in wmv for a in trio]
    flat_small = [a for trio in small_wmv for a in trio]
    out_shape = ([jax.ShapeDtypeStruct(s, F32) for s in SHARD_SHAPES for _ in range(4)]
                 + [jax.ShapeDtypeStruct(trio[0].shape, F32) for trio in small_wmv for _ in range(4)]
                 + [jax.ShapeDtypeStruct((1, 1), F32)])
    return _pcall(
        body, name="adamw",
        in_specs=[vmem] * (n_g + 1 + len(flat_wmv) + len(flat_small)), out_specs=[vmem] * len(out_shape),
        out_shape=out_shape,
        compiler_params=_cparams(),
    )(*flat_grads, small_grad, *flat_wmv, *flat_small)


def kernel(x, w_in, q_norm_g, kv_norm_g, w_uq, w_ukv, w_out, ln_g, ln_b, loss_target, m_w_in, m_q_norm_g, m_kv_norm_g, m_w_uq, m_w_ukv, m_w_out, m_ln_g, m_ln_b, v_w_in, v_q_norm_g, v_kv_norm_g, v_w_uq, v_w_ukv, v_w_out, v_ln_g, v_ln_b):
    w_in_r, w_uq_r, w_ukv_r = _all_gather_weights([w_in, w_uq, w_ukv])
    grad_x, (g_in_rest, g_in_first, g_uq, g_ukv, g_out, g_small) = _local_step(
        x[0], loss_target[0], w_in_r, w_uq_r, w_ukv_r, _gather_w_out_rider(w_out), _scatter_g_out_rider,
        _reduce_grads_rider, q_norm_g, kv_norm_g, ln_g, ln_b)
    row = lambda a: a.reshape(1, -1)
    small_wmv = [(row(ln_g), row(m_ln_g), row(v_ln_g)), (row(ln_b), row(m_ln_b), row(v_ln_b)),
                 (row(q_norm_g), row(m_q_norm_g), row(v_q_norm_g)), (row(kv_norm_g), row(m_kv_norm_g), row(v_kv_norm_g))]
    wmv = [(w_in, m_w_in, v_w_in), (w_uq, m_w_uq, v_w_uq), (w_ukv, m_w_ukv, v_w_ukv), (w_out, m_w_out, v_w_out)]
    res = _adamw_update([[g_in_first, g_in_rest], [g_uq], [g_ukv], [g_out]], g_small, wmv, small_wmv)
    big = [res[4 * t:4 * t + 4] for t in range(4)]
    small = [[a.reshape(-1) for a in res[16 + 4 * t:20 + 4 * t]] for t in range(4)]
    loss = res[32].reshape(())

    def group(kind):
        return (big[0][kind], small[2][kind], small[3][kind], big[1][kind], big[2][kind], big[3][kind],
                small[0][kind], small[1][kind])

    return (loss, grad_x[None], *group(0), *group(1), *group(2), *group(3))
```

```python
import functools
from typing import Callable, NamedTuple

import numpy as np
import jax
import jax.numpy as jnp
from jax import lax
from jax.experimental import pallas as pl
from jax.experimental.pallas import tpu as pltpu

F32 = jnp.float32
BF16 = jnp.bfloat16

D_MODEL = 1024
ROPE_THETA = 500000.0
NEG = -1e30
RMS_EPS = 1e-6
LN_EPS = 1e-5
HEADS = 8
MLA_NOPE = 64
MLA_ROPE = 32
MLA_V = 64
Q_RANK = 384
KV_RANK = 256
DIL_HEAD = 64
DIL_ROT = 16
DIL_CONFIGS = ((128, 1), (512, 4), (2048, 16))
DIL_NEAR = 512
HW = HEADS * 64
QW = HW + HEADS * MLA_ROPE
IN_SPLITS = (Q_RANK, KV_RANK, MLA_ROPE, HW, HW, HW, HW, HW)
IN_WIDTH = sum(IN_SPLITS)
ALPHA = 2.0 ** 0.25
MLA_SCALE = (MLA_NOPE + MLA_ROPE) ** -0.5
DIL_SCALE = DIL_HEAD ** -0.5
LOG2E = 1.4426950408889634
LN2 = 0.6931471805599453

ADAM_LR = 0.001
ADAM_B1 = 0.9
ADAM_B2 = 0.999
ADAM_EPS = 1e-08
ADAM_WD = 0.01
ADAM_STEP = 10

N_DEV = 8
LANES = 128
VMEM_LIMIT = 56 * 1024 * 1024
BLOCK_TOKENS = 512
BLOCK_MLA = 512
BLOCK_DIL = 512
GRAD_W_IN_CUT = 256

C_CQ, C_CKV, C_KR, C_GA, C_QB, C_KB, C_VB, C_GB, C_END = 0, 384, 640, 768, 1280, 1792, 2304, 2816, 3328

NT = (((1,), (1,)), ((), ()))
TN = (((0,), (0,)), ((), ()))


def _pcall(body, **kw):
    return pl.pallas_call(body, **kw)


def _cparams(**kw):
    return pltpu.CompilerParams(vmem_limit_bytes=VMEM_LIMIT, **kw)


def _rope_tables(seq):
    def tabs(dim, period):
        half = dim // 2
        inv = np.float32(ROPE_THETA) ** (-np.arange(0, dim, 2, dtype=np.float32) / np.float32(dim))
        ang = np.arange(seq, dtype=np.float32)[:, None] * inv.astype(np.float32)[None, :]
        cos, sin = np.cos(ang).astype(np.float32), np.sin(ang).astype(np.float32)
        j = np.arange(LANES) % period
        f = j % half
        c = np.where(j < dim, cos[:, f], np.float32(1.0))
        s1 = np.where(j < half, -sin[:, f], np.float32(0.0))
        s2 = np.where((j >= half) & (j < dim), sin[:, f], np.float32(0.0))
        return [c, s1, s2]
    return np.stack(tabs(MLA_ROPE, MLA_ROPE) + tabs(DIL_ROT, DIL_HEAD)).astype(np.float32)


def _rope(t, c, s1, s2, half):
    return t * c + pltpu.roll(t, LANES - half, 1) * s1 + pltpu.roll(t, half, 1) * s2


def _rope_t(d, c, s1, s2, half):
    return d * c + pltpu.roll(d * s1, half, 1) + pltpu.roll(d * s2, LANES - half, 1)


def _rope_wide(fn, t, c, s1, s2, half):
    return jnp.concatenate(
        [fn(t[:, i:i + LANES], c, s1, s2, half) for i in range(0, t.shape[1], LANES)], axis=1)


def _mla_bias_t(blk):
    a = np.arange(blk)
    causal = np.where(a[:, None] <= a[None, :], 0.0, NEG)
    return np.stack([np.zeros((blk, blk)), causal]).astype(np.float32)


def _dil_bias_t(blk, reach):
    a = np.arange(blk)
    out = []
    for off in range(-(-reach // blk) + 1):
        delta = blk * off + a[None, :] - a[:, None]
        mult = np.zeros((blk, blk))
        for window, dil in DIL_CONFIGS:
            mult += (delta >= 0) & (delta % dil == 0) & (delta <= min(window, reach))
        out.append(np.where(mult > 0, np.log2(np.maximum(mult, 1.0)), NEG))
    return np.stack(out).astype(np.float32)


def _dil_far_bias_t(length):
    window, dil = DIL_CONFIGS[-1]
    a = np.arange(length)
    steps_back = a[None, :] - a[:, None]
    seen = (steps_back * dil > DIL_NEAR) & (steps_back * dil <= window)
    return np.where(seen, 0.0, NEG).astype(np.float32)[None]


def _lanes_to_classes(a, dil):
    h, s = a.shape
    return a.reshape(h, s // dil, dil).transpose(0, 2, 1).reshape(h, s)


def _lanes_from_classes(a, dil):
    h, s = a.shape
    return a.reshape(h, dil, s // dil).transpose(0, 2, 1).reshape(h, s)


def _steps(nq, span, by_key, diag_only_bias):
    rows = []
    if by_key:
        for ki in range(nq):
            hi = min(nq - 1, ki + span)
            for qi in range(ki, hi + 1):
                rows.append((qi, ki, int(qi == ki), int(qi == hi)))
    else:
        for qi in range(nq):
            lo = max(0, qi - span)
            for ki in range(lo, qi + 1):
                rows.append((qi, ki, int(ki == lo), int(ki == qi)))
    arr = np.array(rows, dtype=np.int32)
    off = arr[:, 0] - arr[:, 1]
    bias_idx = (off == 0).astype(np.int32) if diag_only_bias else off.astype(np.int32)
    return [jnp.asarray(v) for v in (arr[:, 0], arr[:, 1], bias_idx, arr[:, 2], arr[:, 3])]


def _by_class(val, out_ref, lanes_sc):
    n_cls, per = out_ref.shape[0], out_ref.shape[1]
    for c in range(val.shape[1] // LANES):
        lanes_sc[c] = val[:, LANES * c:LANES * (c + 1)]
        for r in range(n_cls):
            rows = lanes_sc.at[c][pl.ds(r, per, stride=n_cls), :]
            out_ref[r, :, LANES * c:LANES * (c + 1)] = rows.astype(out_ref.dtype)


def _in_sequence(ref, lanes_sc):
    n_cls, per, width = ref.shape
    for c in range(width // LANES):
        for r in range(n_cls):
            lanes_sc.at[c][pl.ds(r, per, stride=n_cls), :] = ref[r, :, LANES * c:LANES * (c + 1)].astype(F32)
    return jnp.concatenate([lanes_sc[c] for c in range(width // LANES)], axis=1)


def _fwd_proj(x, w_in_r, w_uq_r, w_ukv_r, qg, kvg, tabs, bt, n_cls):
    seq = x.shape[0]

    def body(x_ref, win_ref, wuq_ref, wukv_ref, qg_ref, kvg_ref, tab_ref,
             cq_ref, ckv_ref, qn_ref, kvn_ref, qcat_ref, kn_ref, kpe_ref, v_ref,
             ga_ref, gb_ref, qb_ref, kb_ref, vb_ref, knt_ref, kpet_ref, vt_ref, kbt_ref, vbt_ref,
             qbc_ref, kbc_ref, vbc_ref, lanes_sc):
        xb = x_ref[...].astype(BF16)

        def proj(lo, hi):
            return jnp.dot(xb, win_ref[:, lo:hi], preferred_element_type=F32)

        m_tabs = (tab_ref[0], tab_ref[1], tab_ref[2])
        d_tabs = (tab_ref[3], tab_ref[4], tab_ref[5])

        def use_cq(cq):
            cq_ref[...] = cq
            qn = (cq * lax.rsqrt(jnp.mean(cq * cq, axis=1, keepdims=True) + RMS_EPS) * qg_ref[...]).astype(BF16)
            qn_ref[...] = qn
            q = jnp.dot(qn, wuq_ref[...], preferred_element_type=F32)
            qcat_ref[:, :HW] = (q[:, :HW] * (MLA_SCALE * LOG2E)).astype(BF16)
            qcat_ref[:, HW:] = (
                _rope_wide(_rope, q[:, HW:], *m_tabs, MLA_ROPE // 2) * (MLA_SCALE * LOG2E)).astype(BF16)

        def use_ckv(ckv):
            ckv_ref[...] = ckv
            kvn = (ckv * lax.rsqrt(jnp.mean(ckv * ckv, axis=1, keepdims=True) + RMS_EPS) * kvg_ref[...]).astype(BF16)
            kvn_ref[...] = kvn
            kv = jnp.dot(kvn, wukv_ref[...], preferred_element_type=F32)
            kn_ref[...] = kv[:, :HW].astype(BF16)
            v_ref[...] = kv[:, HW:].astype(BF16)
            knt_ref[...] = kv[:, :HW].T.astype(BF16)
            vt_ref[...] = kv[:, HW:].T.astype(BF16)

        def use_kr(kr):
            kpe = _rope(kr, *m_tabs, MLA_ROPE // 2)
            kpe_ref[...] = kpe.astype(BF16)
            kpet_ref[...] = kpe.T[:MLA_ROPE, :].astype(BF16)

        def use_ga(ga):
            ga_ref[...] = ga

        def use_qb(qb):
            qb = _rope_wide(_rope, qb, *d_tabs, DIL_ROT // 2) * (DIL_SCALE * LOG2E)
            qb_ref[...] = qb.astype(BF16)
            _by_class(qb, qbc_ref, lanes_sc)

        def use_kb(kb):
            kb = _rope_wide(_rope, kb, *d_tabs, DIL_ROT // 2)
            kb_ref[...] = kb.astype(BF16)
            kbt_ref[...] = kb.T.astype(BF16)
            _by_class(kb, kbc_ref, lanes_sc)

        def use_vb(vb):
            vb_ref[...] = vb.astype(BF16)
            vbt_ref[...] = vb.T.astype(BF16)
            _by_class(vb, vbc_ref, lanes_sc)

        def use_gb(gb):
            gb_ref[...] = gb

        pieces = [(C_CQ, C_CKV, use_cq), (C_CKV, C_KR, use_ckv), (C_KR, C_GA, use_kr), (C_GA, C_QB, use_ga),
                  (C_QB, C_KB, use_qb), (C_KB, C_VB, use_kb), (C_VB, C_GB, use_vb), (C_GB, C_END, use_gb)]
        ahead = proj(*pieces[0][:2])
        for n, (_, _, use) in enumerate(pieces):
            cur = ahead
            if n + 1 < len(pieces):
                ahead = proj(*pieces[n + 1][:2])
            use(cur)

    def tok(width):
        return pl.BlockSpec((bt, width), lambda i: (i, 0))

    def tok_t(height):
        return pl.BlockSpec((height, bt), lambda i: (0, i))

    def full(a):
        return pl.BlockSpec(a.shape, lambda i: (0,) * a.ndim)

    outs = [(Q_RANK, F32), (KV_RANK, F32), (Q_RANK, BF16), (KV_RANK, BF16), (QW, BF16), (HW, BF16),
            (LANES, BF16), (HW, BF16), (HW, F32), (HW, F32), (HW, BF16), (HW, BF16), (HW, BF16)]
    outs_t = [HW, MLA_ROPE, HW, HW, HW]
    by_class = pl.BlockSpec((n_cls, bt // n_cls, HW), lambda i: (0, i, 0))
    return _pcall(
        body, name="fwd_proj", grid=(seq // bt,),
        in_specs=[tok(D_MODEL), full(w_in_r), full(w_uq_r), full(w_ukv_r), full(qg), full(kvg),
                  pl.BlockSpec((6, bt, LANES), lambda i: (0, i, 0))],
        out_specs=[tok(w) for w, _ in outs] + [tok_t(h) for h in outs_t] + [by_class] * 3,
        out_shape=[jax.ShapeDtypeStruct((seq, w), dt) for w, dt in outs]
        + [jax.ShapeDtypeStruct((h, seq), BF16) for h in outs_t]
        + [jax.ShapeDtypeStruct((n_cls, seq // n_cls, HW), BF16)] * 3,
        scratch_shapes=[pltpu.VMEM((HW // LANES, bt, LANES), F32)],
        compiler_params=_cparams(dimension_semantics=("arbitrary",)),
    )(x, w_in_r, w_uq_r, w_ukv_r, qg, kvg, tabs)


def _head_masks(lane, h):
    e, g = h % 2, h % 4
    me = (lane >= 64 * e) & (lane < 64 * e + 64)
    mr = (lane >= 32 * g) & (lane < 32 * g + 32)
    return me, mr


def _masked(mask, a):
    return jnp.where(mask, a, jnp.zeros_like(a))


def _pair_operands(q_ref, k_ref, kpe_ref, lane, j, ks=slice(None), qs=slice(None)):
    cols = slice(LANES * j, LANES * (j + 1))
    qc = q_ref[qs, cols]
    kj = k_ref[ks, cols]
    kes = []
    for h in (2 * j, 2 * j + 1):
        me, mr = _head_masks(lane, h)
        ke = _masked(me, kj)
        if kpe_ref is not None:
            ke = jnp.concatenate([ke, _masked(mr, kpe_ref[ks, :])], axis=1)
        kes.append(ke)
    if kpe_ref is not None:
        qc = jnp.concatenate([qc, q_ref[qs, HW + LANES * (j // 2):HW + LANES * (j // 2 + 1)]], axis=1)
    return qc, kes


def _tile_variants(bias_t):
    out = {}
    for i, tile in enumerate(np.asarray(bias_t)):
        h = tile.shape[0] // 2
        skip = 1 if (tile[h:, :h] == NEG).all() else 2 if (tile[:h, h:] == NEG).all() else 0
        out[i] = (bool((tile != 0).any()), skip)
    return out


def _tile_parts(blk, skip):
    lo, hi, full = slice(0, blk // 2), slice(blk // 2, blk), slice(0, blk)
    return {0: [(full, full)], 1: [(lo, full), (hi, hi)], 2: [(hi, full), (lo, lo)]}[skip]


class Rider(NamedTuple):
    args: list
    in_specs: list
    out_shape: list
    out_specs: list
    scratch: list
    start: Callable
    finish: Callable
    stages: tuple = ()
    start_step: int = 0
    shared: int = 0


def _ride_along(body, ride, n_prefetch, n_in, n_out, n_scratch, n_steps):
    if ride is None:
        return body

    def wrapped(*refs):
        pre, rest = refs[:n_prefetch], refs[n_prefetch:]
        a = n_in
        b = a + len(ride.args)
        c = b + n_out
        d = c + len(ride.out_shape)
        e = d + n_scratch
        mine = (rest[a:b], rest[c:d], rest[e:])
        t = pl.program_id(0)
        pl.when(t == ride.start_step)(lambda: ride.start(*mine))
        for at, stage in ride.stages:
            pl.when(t == at)(functools.partial(stage, *mine))
        body(*pre, *rest[:a], *rest[b:c], *rest[d:e + ride.shared])
        pl.when(t == n_steps - 1)(lambda: ride.finish(*mine))

    return wrapped


def _attn_fwd(name, q, k, kpe, vt, bias_t, steps, blk, ride=None, v_token_major=False):
    seq = q.shape[0]
    mla = kpe is not None
    n_steps = int(steps[0].shape[0])
    variants = _tile_variants(bias_t)

    def body(qi_r, ki_r, bi_r, fi_r, la_r, *refs):
        if mla:
            q_ref, k_ref, kpe_ref, vt_ref, b_ref, o_ref, lse_ref, m_sc, l_sc, acc_sc, st_sc = refs
        else:
            q_ref, k_ref, vt_ref, b_ref, o_ref, lse_ref, m_sc, l_sc, acc_sc, st_sc = refs
        t = pl.program_id(0)

        @pl.when(fi_r[t] == 1)
        def _():
            m_sc[...] = jnp.full(m_sc.shape, NEG, F32)
            l_sc[...] = jnp.zeros(l_sc.shape, F32)
            acc_sc[...] = jnp.zeros(acc_sc.shape, F32)

        lane = lax.broadcasted_iota(jnp.int32, (1, LANES), 1)
        if v_token_major:
            vt_all = vt_ref[...].astype(F32).T.astype(BF16)
            vt_rows = lambda rows, ks: vt_all[rows, ks]
        else:
            vt_rows = lambda rows, ks: vt_ref[rows, ks]

        def tile_pass(ks, qs, with_bias):
            nk, nq = ks.stop - ks.start, qs.stop - qs.start
            ones = jnp.ones((16, nk), BF16)

            def pair_scores(j):
                qc, kes = _pair_operands(q_ref, k_ref, kpe_ref if mla else None, lane, j, ks, qs)
                st = lax.dot_general(jnp.concatenate(kes, axis=0), qc, NT, preferred_element_type=F32)
                maxes = []
                for e in range(2):
                    se = st[e * nk:(e + 1) * nk]
                    if with_bias:
                        se = se + b_ref[0, ks, qs]
                    st_sc[j % 2, e * nk:(e + 1) * nk, 0:nq] = se
                    maxes.append(jnp.max(se, axis=0, keepdims=True))
                return maxes

            def softmax_pv(h, col_max):
                st = st_sc[(h // 2) % 2, (h % 2) * nk:(h % 2 + 1) * nk, 0:nq]
                hrow = slice(h, h + 1)
                m_prev = m_sc[hrow, qs]
                m_new = jnp.maximum(m_prev, col_max)
                alpha = jnp.exp2(m_prev - m_new)
                pt = jnp.exp2(st - m_new).astype(BF16)
                m_sc[hrow, qs] = m_new
                rows = slice(64 * h, 64 * h + 64)
                res = jnp.dot(jnp.concatenate([vt_rows(rows, ks), ones], axis=0), pt, preferred_element_type=F32)
                acc_sc[rows, qs] = alpha * acc_sc[rows, qs] + res[:64]
                l_sc[hrow, qs] = alpha * l_sc[hrow, qs] + res[64:65]

            maxes = pair_scores(0)
            for j in range(HEADS // 2):
                cur = maxes
                if j + 1 < HEADS // 2:
                    maxes = pair_scores(j + 1)
                softmax_pv(2 * j, cur[0])
                softmax_pv(2 * j + 1, cur[1])

        def step(with_bias, skip):
            for ks, qs in _tile_parts(blk, skip):
                tile_pass(ks, qs, with_bias)

        for idx, (with_bias, skip) in variants.items():
            if len(variants) == 1:
                step(with_bias, skip)
            else:
                pl.when(bi_r[t] == idx)(functools.partial(step, with_bias, skip))

        @pl.when(la_r[t] == 1)
        def _():
            for h in range(HEADS):
                rows = slice(64 * h, 64 * h + 64)
                acc_sc[rows, :] = acc_sc[rows, :] / l_sc[h:h + 1, :]
            o_ref[...] = acc_sc[...].T
            lse_ref[...] = m_sc[...] + jnp.log2(l_sc[...])

    qmap = lambda t, qi, ki, bi, fi, la: (qi[t], 0)
    kmap = lambda t, qi, ki, bi, fi, la: (ki[t], 0)
    in_specs = [pl.BlockSpec((blk, q.shape[1]), qmap), pl.BlockSpec((blk, HW), kmap)]
    args = [q, k]
    if mla:
        in_specs.append(pl.BlockSpec((blk, LANES), kmap))
        args.append(kpe)
    in_specs += [pl.BlockSpec((blk, HW), kmap) if v_token_major else
                 pl.BlockSpec((HW, blk), lambda t, qi, ki, bi, fi, la: (0, ki[t])),
                 pl.BlockSpec((1, blk, blk), lambda t, qi, ki, bi, fi, la: (bi[t], 0, 0))]
    args += [vt, jnp.asarray(bias_t)]
    out_specs = [pl.BlockSpec((blk, HW), qmap), pl.BlockSpec((HEADS, blk), lambda t, qi, ki, bi, fi, la: (0, qi[t]))]
    out_shape = [jax.ShapeDtypeStruct((seq, HW), F32), jax.ShapeDtypeStruct((HEADS, seq), F32)]
    scratch = [pltpu.VMEM((HEADS, blk), F32), pltpu.VMEM((HEADS, blk), F32),
               pltpu.VMEM((HW, blk), F32), pltpu.VMEM((2, 2 * blk, blk), F32)]
    body = _ride_along(body, ride, 5, len(args), len(out_shape), len(scratch), n_steps)
    if ride is not None:
        args, in_specs = args + ride.args, in_specs + ride.in_specs
        out_specs, out_shape, scratch = out_specs + ride.out_specs, out_shape + ride.out_shape, scratch + ride.scratch
    return _pcall(
        body, name=name,
        grid_spec=pltpu.PrefetchScalarGridSpec(
            num_scalar_prefetch=5, grid=(n_steps,), in_specs=in_specs, out_specs=out_specs, scratch_shapes=scratch),
        out_shape=out_shape,
        compiler_params=_cparams(dimension_semantics=("arbitrary",)),
    )(*steps, *args)


def _attn_bwd(name, q, k, kpe, v, kt, kpet, bias_t, do, lse, dstat, steps, blk, ride=None, single_visit=False):
    assert not (single_visit and kpe is not None) and (kt is not None or single_visit)
    seq = q.shape[0]
    mla = kpe is not None
    qw = q.shape[1]
    n_steps = int(steps[0].shape[0])
    dk_dtype = BF16 if mla else F32
    variants = _tile_variants(bias_t)

    def body(qi_r, ki_r, bi_r, fi_r, la_r, *refs):
        if mla:
            (q_ref, k_ref, kpe_ref, v_ref, kt_ref, kpet_ref, b_ref, do_ref, lse_ref, d_ref,
             dq_ref, dk_ref, dkpe_ref, dv_ref, dk_sc, dkpe_sc, dv_sc, st_sc, dpt_sc) = refs
        else:
            q_ref, k_ref, v_ref, *rest = refs
            kt_ref = rest.pop(0) if kt is not None else None
            b_ref, do_ref, lse_ref, d_ref, dq_out_ref, dk_ref, dv_ref, dk_sc, dv_sc, st_sc, dpt_sc, *rest = rest
            dq_ref = rest[0] if single_visit else dq_out_ref
        t = pl.program_id(0)

        @pl.when(jnp.logical_or(t == 0, single_visit))
        def _():
            dq_ref[...] = jnp.zeros(dq_ref.shape, F32)

        @pl.when(fi_r[t] == 1)
        def _():
            dk_sc[...] = jnp.zeros(dk_sc.shape, F32)
            dv_sc[...] = jnp.zeros(dv_sc.shape, F32)
            if mla:
                dkpe_sc[...] = jnp.zeros(dkpe_sc.shape, F32)

        qi = 0 if single_visit else qi_r[t]
        lane = lax.broadcasted_iota(jnp.int32, (1, LANES), 1)
        if kt is None:
            kt_all = k_ref[...].astype(F32).T.astype(BF16)
            kt_rows = lambda rows, ks: kt_all[rows, ks]
        else:
            kt_rows = lambda rows, ks: kt_ref[rows, ks]

        def tile_pass(ks, qs, with_bias):
            nk, nq = ks.stop - ks.start, qs.stop - qs.start

            def pair_matmuls(j):
                cols = slice(LANES * j, LANES * (j + 1))
                qc, kes = _pair_operands(q_ref, k_ref, kpe_ref if mla else None, lane, j, ks, qs)
                st_sc[j % 2, 0:2 * nk, 0:nq] = lax.dot_general(
                    jnp.concatenate(kes, axis=0), qc, NT, preferred_element_type=F32)
                vj = v_ref[ks, cols]
                ves = [_masked(_head_masks(lane, h)[0], vj) for h in (2 * j, 2 * j + 1)]
                dpt_sc[j % 2, 0:2 * nk, 0:nq] = lax.dot_general(
                    jnp.concatenate(ves, axis=0), do_ref[qs, cols], NT, preferred_element_type=F32)

            def pair_grads(j):
                cols = slice(LANES * j, LANES * (j + 1))
                qj, doj = q_ref[qs, cols], do_ref[qs, cols]
                if mla:
                    qr = q_ref[qs, HW + LANES * (j // 2):HW + LANES * (j // 2 + 1)]
                pts, dsts, qms, doms = [], [], [], []
                for e in range(2):
                    h = 2 * j + e
                    me, mr = _head_masks(lane, h)
                    st = st_sc[j % 2, e * nk:(e + 1) * nk, 0:nq]
                    if with_bias:
                        st = st + b_ref[0, ks, qs]
                    pt = jnp.exp2(st - lse_ref[h:h + 1, qs])
                    dst = (pt * (dpt_sc[j % 2, e * nk:(e + 1) * nk, 0:nq] - d_ref[h:h + 1, qs])).astype(BF16)
                    pts.append(pt.astype(BF16))
                    dsts.append(dst)
                    doms.append(_masked(me, doj))
                    qm = _masked(me, qj)
                    if mla:
                        qm = jnp.concatenate([qm, _masked(mr, qr)], axis=1)
                    qms.append(qm)
                    ktl = kt_rows(slice(64 * h, 64 * h + 64), ks)
                    if mla:
                        ktl = jnp.concatenate([ktl, kpet_ref[:, ks]], axis=0)
                    dqc = jnp.dot(ktl, dst, preferred_element_type=F32)
                    dq_ref[qi, 64 * h:64 * h + 64, qs] += dqc[:64]
                    if mla:
                        dq_ref[qi, HW + MLA_ROPE * h:HW + MLA_ROPE * (h + 1), qs] += dqc[64:]
                dv_sc[ks, cols] += jnp.dot(
                    jnp.concatenate(pts, axis=1), jnp.concatenate(doms, axis=0), preferred_element_type=F32)
                dkc = jnp.dot(jnp.concatenate(dsts, axis=1), jnp.concatenate(qms, axis=0), preferred_element_type=F32)
                dk_sc[ks, cols] += dkc[:, :LANES]
                if mla:
                    dkpe_sc[ks, :] += dkc[:, LANES:]

            pair_matmuls(0)
            for j in range(HEADS // 2):
                if j + 1 < HEADS // 2:
                    pair_matmuls(j + 1)
                pair_grads(j)

        def step(with_bias, skip):
            for ks, qs in _tile_parts(blk, skip):
                tile_pass(ks, qs, with_bias)

        for idx, (with_bias, skip) in variants.items():
            if len(variants) == 1:
                step(with_bias, skip)
            else:
                pl.when(bi_r[t] == idx)(functools.partial(step, with_bias, skip))
        if single_visit:
            dq_out_ref[...] = dq_ref[0].T

        @pl.when(la_r[t] == 1)
        def _():
            dk_ref[...] = (dk_sc[...] * LN2).astype(dk_ref.dtype)
            dv_ref[...] = dv_sc[...].astype(dv_ref.dtype)
            if mla:
                dkpe_ref[...] = dkpe_sc[...] * LN2

    qmap = lambda t, qi, ki, bi, fi, la: (qi[t], 0)
    kmap = lambda t, qi, ki, bi, fi, la: (ki[t], 0)
    qmap_t = lambda t, qi, ki, bi, fi, la: (0, qi[t])
    kmap_t = lambda t, qi, ki, bi, fi, la: (0, ki[t])
    in_specs = [pl.BlockSpec((blk, qw), qmap), pl.BlockSpec((blk, HW), kmap)]
    args = [q, k]
    if mla:
        in_specs.append(pl.BlockSpec((blk, LANES), kmap))
        args.append(kpe)
    in_specs.append(pl.BlockSpec((blk, HW), kmap))
    args.append(v)
    if kt is not None:
        in_specs.append(pl.BlockSpec((HW, blk), kmap_t))
        args.append(kt)
    if mla:
        in_specs.append(pl.BlockSpec((MLA_ROPE, blk), kmap_t))
        args.append(kpet)
    in_specs += [pl.BlockSpec((1, blk, blk), lambda t, qi, ki, bi, fi, la: (bi[t], 0, 0)),
                 pl.BlockSpec((blk, HW), qmap), pl.BlockSpec((HEADS, blk), qmap_t), pl.BlockSpec((HEADS, blk), qmap_t)]
    args += [jnp.asarray(bias_t), do, lse, dstat]
    dq_shape = (seq // blk, qw, blk)
    if single_visit:
        out_specs, out_shape = [pl.BlockSpec((blk, qw), qmap)], [jax.ShapeDtypeStruct((seq, qw), F32)]
    else:
        out_specs = [pl.BlockSpec(dq_shape, lambda t, qi, ki, bi, fi, la: (0, 0, 0))]
        out_shape = [jax.ShapeDtypeStruct(dq_shape, F32)]
    out_specs.append(pl.BlockSpec((blk, HW), kmap))
    out_shape.append(jax.ShapeDtypeStruct((seq, HW), dk_dtype))
    scratch = [pltpu.VMEM((blk, HW), F32)]
    if mla:
        out_specs.append(pl.BlockSpec((blk, LANES), kmap))
        out_shape.append(jax.ShapeDtypeStruct((seq, LANES), F32))
        scratch.append(pltpu.VMEM((blk, LANES), F32))
    out_specs.append(pl.BlockSpec((blk, HW), kmap))
    out_shape.append(jax.ShapeDtypeStruct((seq, HW), BF16))
    scratch.append(pltpu.VMEM((blk, HW), F32))
    scratch += [pltpu.VMEM((2, 2 * blk, blk), F32), pltpu.VMEM((2, 2 * blk, blk), F32)]
    if single_visit:
        scratch.append(pltpu.VMEM((1, qw, blk), F32))
    body = _ride_along(body, ride, 5, len(args), len(out_shape), len(scratch), n_steps)
    if ride is not None:
        args, in_specs = args + ride.args, in_specs + ride.in_specs
        out_specs, out_shape, scratch = out_specs + ride.out_specs, out_shape + ride.out_shape, scratch + ride.scratch
    return _pcall(
        body, name=name,
        grid_spec=pltpu.PrefetchScalarGridSpec(
            num_scalar_prefetch=5, grid=(n_steps,), in_specs=in_specs, out_specs=out_specs,
            scratch_shapes=scratch),
        out_shape=out_shape,
        compiler_params=_cparams(dimension_semantics=("arbitrary",)),
    )(*steps, *args)


def _out_ln(oa, ob_near, ob_far, lse_near, lse_far, ga, gb, x, tgt, w_out, ln_g, ln_b, bt):
    seq = x.shape[0]

    def body(oa_ref, obn_ref, obf_ref, lsen_ref, lsef_ref, ga_ref, gb_ref, x_ref, tgt_ref, w_ref, g_ref, b_ref,
             dz_ref, doa_ref, dob_ref, dga_ref, dgb_ref, da_ref, db_ref, lse_ref, gwb_ref, small_ref, dobc_ref,
             gw_ref, lanes_sc):
        i = pl.program_id(0)

        @pl.when(i == 0)
        def _():
            gw_ref[...] = jnp.zeros(gw_ref.shape, F32)
            small_ref[...] = jnp.zeros(small_ref.shape, F32)

        def gate(g):
            sig = 0.5 * jnp.tanh(0.5 * g) + 0.5
            return g * sig, sig * (1.0 + g * (1.0 - sig))

        lse_n, lse_f = lsen_ref[...], lsef_ref[...]
        top = jnp.maximum(lse_n, lse_f)
        e_n, e_f = jnp.exp2(lse_n - top), jnp.exp2(lse_f - top)
        lse_ref[...] = top + jnp.log2(e_n + e_f)
        inv = 1.0 / (e_n + e_f)
        head_row = lax.broadcasted_iota(jnp.int32, (2 * HEADS, HW), 0) % HEADS
        spread = (head_row == lax.broadcasted_iota(jnp.int32, (2 * HEADS, HW), 1) // 64).astype(BF16)

        def per_lane(w):
            hi = w.astype(BF16)
            lo = (w - hi.astype(F32)).astype(BF16)
            return lax.dot_general(jnp.concatenate([hi, lo], axis=0), spread, TN, preferred_element_type=F32)

        o_b_all = per_lane(e_n * inv) * obn_ref[...] + per_lane(e_f * inv) * _in_sequence(obf_ref, lanes_sc)
        gam = g_ref[...]
        halves = [slice(0, bt // 2), slice(bt // 2, bt)]

        def gates_and_projection(rows):
            o_a, o_b = oa_ref[rows, :], o_b_all[rows]
            sa, dsa = gate(ga_ref[rows, :])
            sb, dsb = gate(gb_ref[rows, :])
            mix = jnp.concatenate([o_a * sa, o_b * sb], axis=1).astype(BF16)
            z = ALPHA * x_ref[rows, :] + jnp.dot(mix, w_ref[...], preferred_element_type=F32)
            return o_a, o_b, sa, dsa, sb, dsb, mix, z

        def norm_and_back(rows, mix, z):
            mu = jnp.mean(z, axis=1, keepdims=True)
            zc = z - mu
            rstd = lax.rsqrt(jnp.mean(zc * zc, axis=1, keepdims=True) + LN_EPS)
            xhat = zc * rstd
            diff = xhat * gam + b_ref[...] - tgt_ref[rows, :]
            dy = diff * (1.0 / D_MODEL)
            small_ref[0:1, :] += jnp.sum(dy * xhat, axis=0, keepdims=True)
            small_ref[1:2, :] += jnp.sum(dy, axis=0, keepdims=True)
            small_ref[2:3, :] += jnp.sum(diff * diff, axis=0, keepdims=True)
            dxh = dy * gam
            dz = rstd * (dxh - jnp.mean(dxh, axis=1, keepdims=True)
                         - xhat * jnp.mean(dxh * xhat, axis=1, keepdims=True))
            dz_ref[rows, :] = dz
            dzb = dz.astype(BF16)
            gw_ref[...] += lax.dot_general(mix, dzb, TN, preferred_element_type=F32)
            return lax.dot_general(dzb, w_ref[...], NT, preferred_element_type=F32)

        def gate_back(rows, o_a, o_b, sa, dsa, sb, dsb, dmix):
            doa, dob = dmix[:, :HW] * sa, dmix[:, HW:] * sb
            doa_ref[rows, :] = doa.astype(BF16)
            dob_ref[rows, :] = dob.astype(BF16)
            dga_ref[rows, :] = (dmix[:, :HW] * o_a * dsa).astype(BF16)
            dgb_ref[rows, :] = (dmix[:, HW:] * o_b * dsb).astype(BF16)
            return dob, doa * o_a, dob * o_b

        fronts = [gates_and_projection(rows) for rows in halves]
        dmixes = [norm_and_back(rows, f[6], f[7]) for rows, f in zip(halves, fronts)]
        backs = [gate_back(rows, *f[:6], dmix) for rows, f, dmix in zip(halves, fronts, dmixes)]
        dob, prod_a, prod_b = (jnp.concatenate(parts, axis=0) for parts in zip(*backs))
        _by_class(dob, dobc_ref, lanes_sc)

        @pl.when(i == seq // bt - 1)
        def _():
            gwb_ref[...] = gw_ref[...].astype(BF16)

        head_of = (lax.broadcasted_iota(jnp.int32, (2 * HW, LANES), 0) % HW) // 64
        ind = (head_of == lax.broadcasted_iota(jnp.int32, (2 * HW, LANES), 1)).astype(BF16)

        def head_sums(prod):
            hi = prod.astype(BF16)
            lo = (prod - hi.astype(F32)).astype(BF16)
            sums = jnp.dot(jnp.concatenate([hi, lo], axis=1), ind, preferred_element_type=F32)
            return sums.T[:HEADS, :]

        da_ref[...] = head_sums(prod_a)
        db_ref[...] = head_sums(prod_b)

    def tok(width):
        return pl.BlockSpec((bt, width), lambda i: (i, 0))

    def full(shape):
        return pl.BlockSpec(shape, lambda i: (0,) * len(shape))

    stat = pl.BlockSpec((HEADS, bt), lambda i: (0, i))
    n_cls = ob_far.shape[0]
    by_class = pl.BlockSpec((n_cls, bt // n_cls, HW), lambda i: (0, i, 0))
    return _pcall(
        body, name="out_ln", grid=(seq // bt,),
        in_specs=[tok(HW), tok(HW), by_class, stat, stat, tok(HW), tok(HW), tok(D_MODEL), tok(D_MODEL),
                  full((D_MODEL, D_MODEL)), full((1, D_MODEL)), full((1, D_MODEL))],
        out_specs=[tok(D_MODEL), tok(HW), tok(HW), tok(HW), tok(HW), stat, stat, stat,
                   full((D_MODEL, D_MODEL)), full((8, D_MODEL)), by_class],
        out_shape=[jax.ShapeDtypeStruct((seq, D_MODEL), F32)] + [jax.ShapeDtypeStruct((seq, HW), BF16)] * 4
        + [jax.ShapeDtypeStruct((HEADS, seq), F32)] * 3
        + [jax.ShapeDtypeStruct((D_MODEL, D_MODEL), BF16), jax.ShapeDtypeStruct((8, D_MODEL), F32),
           jax.ShapeDtypeStruct(ob_far.shape, BF16)],
        scratch_shapes=[pltpu.VMEM((D_MODEL, D_MODEL), F32), pltpu.VMEM((HW // LANES, bt, LANES), F32)],
        compiler_params=_cparams(dimension_semantics=("arbitrary",)),
    )(oa, ob_near, ob_far, lse_near, lse_far, ga, gb, x, tgt, w_out, ln_g, ln_b)


def _bwd_mid(dq_m, dkn, dv, dkpe, dqb, dkb, dvb, far, dga, dgb, cq, ckv, qn, kvn, w_uq_r, w_ukv_r, qg, kvg, tabs, bt):
    n_cls = far[0].shape[0]
    seq = cq.shape[0]

    def body(dqm_ref, dkn_ref, dv_ref, dkpe_ref, dqb_ref, dkb_ref, dvb_ref, dqf_ref, dkf_ref, dvf_ref, dga_ref, dgb_ref,
             cq_ref, ckv_ref, qn_ref, kvn_ref, wuq_ref, wukv_ref, qg_ref, kvg_ref, tab_ref,
             dh_ref, guq3_ref, gukv3_ref, small_ref, seq_sc, guq_ref, gukv_ref):
        i = pl.program_id(0)

        @pl.when(i == 0)
        def _():
            guq_ref[...] = jnp.zeros(guq_ref.shape, F32)
            gukv_ref[...] = jnp.zeros(gukv_ref.shape, F32)
            small_ref[...] = jnp.zeros(small_ref.shape, F32)

        m_tabs = (tab_ref[0], tab_ref[1], tab_ref[2])
        d_tabs = (tab_ref[3], tab_ref[4], tab_ref[5])

        def rms_bwd(c, dn, gain):
            r = lax.rsqrt(jnp.mean(c * c, axis=1, keepdims=True) + RMS_EPS)
            u = dn * gain
            dc = r * u - c * (r * r * r) * jnp.mean(u * c, axis=1, keepdims=True)
            return dc, jnp.sum(dn * c * r, axis=0, keepdims=True)

        dqm = dqm_ref[0].T
        dq = jnp.concatenate(
            [dqm[:, :HW], _rope_wide(_rope_t, dqm[:, HW:], *m_tabs, MLA_ROPE // 2)], axis=1) * MLA_SCALE
        dq = dq.astype(BF16)
        dkv = jnp.concatenate([dkn_ref[...], dv_ref[...]], axis=1)
        guq_ref[...] += lax.dot_general(qn_ref[...], dq, TN, preferred_element_type=F32)
        dqn = lax.dot_general(dq, wuq_ref[...], NT, preferred_element_type=F32)
        gukv_ref[...] += lax.dot_general(kvn_ref[...], dkv, TN, preferred_element_type=F32)
        dkvn = lax.dot_general(dkv, wukv_ref[...], NT, preferred_element_type=F32)

        dh_ref[:, C_KR:C_GA] = _rope_t(dkpe_ref[...], *m_tabs, MLA_ROPE // 2).astype(BF16)
        dh_ref[:, C_GA:C_QB] = dga_ref[...]
        in_sequence = functools.partial(_in_sequence, lanes_sc=seq_sc)
        dqb = dqb_ref[0].T + in_sequence(dqf_ref)
        dh_ref[:, C_QB:C_KB] = (_rope_wide(_rope_t, dqb, *d_tabs, DIL_ROT // 2) * DIL_SCALE).astype(BF16)
        dkb = dkb_ref[...] + in_sequence(dkf_ref)
        dh_ref[:, C_KB:C_VB] = _rope_wide(_rope_t, dkb, *d_tabs, DIL_ROT // 2).astype(BF16)
        dh_ref[:, C_VB:C_GB] = (dvb_ref[...].astype(F32) + in_sequence(dvf_ref)).astype(BF16)
        dh_ref[:, C_GB:C_END] = dgb_ref[...]

        dcq, gq = rms_bwd(cq_ref[...], dqn, qg_ref[...])
        small_ref[0:1, :] += gq
        dckv, gkv = rms_bwd(ckv_ref[...], dkvn, kvg_ref[...])
        small_ref[1:2, :KV_RANK] += gkv
        dh_ref[:, C_CQ:C_CKV] = dcq.astype(BF16)
        dh_ref[:, C_CKV:C_KR] = dckv.astype(BF16)

        @pl.when(i == seq // bt - 1)
        def _():
            for h in range(HEADS):
                guq3_ref[h] = jnp.concatenate(
                    [guq_ref[:, MLA_NOPE * h:MLA_NOPE * (h + 1)],
                     guq_ref[:, HW + MLA_ROPE * h:HW + MLA_ROPE * (h + 1)]], axis=1).astype(BF16)
                gukv3_ref[h] = jnp.concatenate(
                    [gukv_ref[:, MLA_NOPE * h:MLA_NOPE * (h + 1)],
                     gukv_ref[:, HW + MLA_V * h:HW + MLA_V * (h + 1)]], axis=1).astype(BF16)

    def tok(width):
        return pl.BlockSpec((bt, width), lambda i: (i, 0))

    def tok_t(a):
        per = a.shape[2] // bt
        return pl.BlockSpec((1, a.shape[1], bt), lambda i: (i // per, 0, i % per))

    def full(shape):
        return pl.BlockSpec(shape, lambda i: (0,) * len(shape))

    by_class = pl.BlockSpec((n_cls, bt // n_cls, HW), lambda i: (0, i, 0))
    uq3 = (HEADS, Q_RANK, MLA_NOPE + MLA_ROPE)
    ukv3 = (HEADS, KV_RANK, MLA_NOPE + MLA_V)
    return _pcall(
        body, name="bwd_mid", grid=(seq // bt,),
        in_specs=[tok_t(dq_m), tok(HW), tok(HW), tok(LANES), tok_t(dqb), tok(HW), tok(HW), by_class, by_class, by_class,
                  tok(HW), tok(HW),
                  tok(Q_RANK), tok(KV_RANK), tok(Q_RANK), tok(KV_RANK),
                  full(w_uq_r.shape), full(w_ukv_r.shape), full((1, Q_RANK)), full((1, KV_RANK)),
                  pl.BlockSpec((6, bt, LANES), lambda i: (0, i, 0))],
        out_specs=[tok(C_END), full(uq3), full(ukv3), full((8, Q_RANK))],
        out_shape=[jax.ShapeDtypeStruct((seq, C_END), BF16), jax.ShapeDtypeStruct(uq3, BF16),
                   jax.ShapeDtypeStruct(ukv3, BF16), jax.ShapeDtypeStruct((8, Q_RANK), F32)],
        scratch_shapes=[pltpu.VMEM((HW // LANES, bt, LANES), F32), pltpu.VMEM(w_uq_r.shape, F32),
                        pltpu.VMEM(w_ukv_r.shape, F32)],
        compiler_params=_cparams(dimension_semantics=("arbitrary",)),
    )(dq_m, dkn, dv, dkpe, dqb, dkb, dvb, *far, dga, dgb, cq, ckv, qn, kvn, w_uq_r, w_ukv_r, qg, kvg, tabs)


def _grad_x(dz, dh, w_in_r, bt, ride=None):
    seq = dz.shape[0]
    n_steps = seq // bt

    def body(dz_ref, dh_ref, w_ref, gx_ref):
        gx_ref[...] = ALPHA * dz_ref[...] + lax.dot_general(
            dh_ref[...], w_ref[...], NT, preferred_element_type=F32)

    args = [dz, dh, w_in_r]
    in_specs = [pl.BlockSpec((bt, D_MODEL), lambda i: (i, 0)), pl.BlockSpec((bt, C_END), lambda i: (i, 0)),
                pl.BlockSpec(w_in_r.shape, lambda i: (0, 0))]
    out_specs = [pl.BlockSpec((bt, D_MODEL), lambda i: (i, 0))]
    out_shape = [jax.ShapeDtypeStruct((seq, D_MODEL), F32)]
    scratch = []
    body = _ride_along(body, ride, 0, len(args), len(out_shape), 0, n_steps)
    if ride is not None:
        args, in_specs = args + ride.args, in_specs + ride.in_specs
        out_specs, out_shape, scratch = out_specs + ride.out_specs, out_shape + ride.out_shape, ride.scratch
    return _pcall(
        body, name="grad_x", grid=(n_steps,),
        in_specs=in_specs, out_specs=out_specs, out_shape=out_shape, scratch_shapes=scratch,
        compiler_params=_cparams(dimension_semantics=("arbitrary",)),
    )(*args)


def _grad_w_in(x, dh, bt, cut, ride):
    seq = x.shape[0]
    n_tok = seq // bt
    shard = IN_WIDTH // N_DEV
    k_lo, k_hi = IN_SPLITS[0] + IN_SPLITS[1], IN_SPLITS[0] + IN_SPLITS[1] + MLA_ROPE

    def body(x_ref, dh_ref, rest_ref, acc, first_ref):
        i = pl.program_id(0)
        for part, (r_lo, r_hi, dst_ref) in enumerate([(0, cut, first_ref), (cut, D_MODEL, rest_ref)]):
            rows = slice(0, r_hi - r_lo)

            @pl.when(i == part * n_tok)
            def _():
                acc[rows, :] = jnp.zeros((r_hi - r_lo, C_END), F32)

            @pl.when(i // n_tok == part)
            def _():
                acc[rows, :] += lax.dot_general(
                    x_ref[:, r_lo:r_hi].astype(BF16), dh_ref[...], TN, preferred_element_type=F32)

            @pl.when(i == (part + 1) * n_tok - 1)
            def _():
                kr = acc[rows, C_KR:C_GA]
                kr = kr + pltpu.roll(kr, 96, 1) + pltpu.roll(kr, 64, 1) + pltpu.roll(kr, 32, 1)
                for d in range(N_DEV):
                    lo, hi = shard * d, shard * (d + 1)
                    pieces = []
                    if lo < k_lo:
                        pieces.append(acc[rows, lo:min(hi, k_lo)])
                    if lo < k_hi and hi > k_lo:
                        pieces.append(kr[:, max(lo, k_lo) - k_lo:min(hi, k_hi) - k_lo])
                    if hi > k_hi:
                        shift = C_GA - k_hi
                        pieces.append(acc[rows, max(lo, k_hi) + shift:hi + shift])
                    blk = pieces[0] if len(pieces) == 1 else jnp.concatenate(pieces, axis=1)
                    dst_ref[d] = blk.astype(BF16)

    args = [x, dh]
    in_specs = [pl.BlockSpec((bt, D_MODEL), lambda i: (i % n_tok, 0)),
                pl.BlockSpec((bt, C_END), lambda i: (i % n_tok, 0))]
    out_specs = [pl.BlockSpec((N_DEV, D_MODEL - cut, shard), lambda i: (0, 0, 0))]
    out_shape = [jax.ShapeDtypeStruct((N_DEV, D_MODEL - cut, shard), BF16)]
    scratch = [pltpu.VMEM((max(cut, D_MODEL - cut), C_END), F32)]
    assert ride.shared == 1
    body = _ride_along(body, ride, 0, len(args), len(out_shape), len(scratch), 2 * n_tok)
    return _pcall(
        body, name="grad_w_in", grid=(2 * n_tok,),
        in_specs=in_specs + ride.in_specs, out_specs=out_specs + ride.out_specs,
        out_shape=out_shape + ride.out_shape, scratch_shapes=scratch + ride.scratch,
        compiler_params=_cparams(dimension_semantics=("arbitrary",)),
    )(*args, *ride.args)


def _local_step(x, tgt, w_in_r, w_uq_r, w_ukv_r, w_out_rider, g_out_rider, reduce_rider, q_norm_g, kv_norm_g,
                ln_g, ln_b, bt=BLOCK_TOKENS, blk_m=BLOCK_MLA, blk_d=BLOCK_DIL):
    seq = x.shape[0]
    tabs = jnp.asarray(_rope_tables(seq))
    qg, kvg = q_norm_g.reshape(1, -1), kv_norm_g.reshape(1, -1)

    far_dil = DIL_CONFIGS[-1][1]
    cls = seq // far_dil
    (cq, ckv, qn, kvn, qcat, kn, kpe, v, ga, gb, qb, kb, vb, knt, kpet, vt, kbt, vbt, qb_c, kb_c, vb_c) = _fwd_proj(
        x, w_in_r, w_uq_r, w_ukv_r, qg, kvg, tabs, bt, far_dil)
    qb_c, kb_c, vb_c = (a.reshape(seq, HW) for a in (qb_c, kb_c, vb_c))

    nq_m, nq_d = seq // blk_m, seq // blk_d
    bias_m = _mla_bias_t(blk_m)
    oa, lse_a, w_out = _attn_fwd(
        "mla_fwd", qcat, kn, kpe, vt, bias_m, _steps(nq_m, nq_m, False, True), blk_m, ride=w_out_rider)

    bias_near = _dil_bias_t(blk_d, DIL_NEAR)
    ob_near, lse_near = _attn_fwd(
        "dil_fwd", qb, kb, None, vbt, bias_near, _steps(nq_d, -(-DIL_NEAR // blk_d), False, False), blk_d)
    each = np.arange(far_dil, dtype=np.int32)
    steps_far = [jnp.asarray(v) for v in (each, each, np.zeros_like(each), np.ones_like(each), np.ones_like(each))]
    bias_far = _dil_far_bias_t(cls)
    ob_far, lse_far = _attn_fwd(
        "dil_far_fwd", qb_c, kb_c, None, vb_c, bias_far, steps_far, cls, v_token_major=True)

    dz, doa, dob, dga, dgb, dst_a, dst_b, lse_b, g_out, small1, dob_c = _out_ln(
        oa, ob_near, ob_far.reshape(far_dil, cls, HW), lse_near, _lanes_from_classes(lse_far, far_dil), ga, gb, x, tgt,
        w_out.reshape(D_MODEL, D_MODEL), ln_g.reshape(1, -1), ln_b.reshape(1, -1), bt)

    dq_m, dkn, dkpe, dv, g_out_recv = _attn_bwd(
        "mla_bwd", qcat, kn, kpe, v, knt, kpet, bias_m, doa, lse_a, dst_a, _steps(nq_m, nq_m, True, True), blk_m,
        ride=g_out_rider(g_out.reshape(N_DEV, D_MODEL // N_DEV, D_MODEL)))
    dqb, dkb_near, dvb_near = _attn_bwd(
        "dil_bwd", qb, kb, None, vb, kbt, None, bias_near, dob, lse_b, dst_b,
        _steps(nq_d, -(-DIL_NEAR // blk_d), True, False), blk_d)
    dqb_far, dkb_far, dvb_far = _attn_bwd(
        "dil_far_bwd", qb_c, kb_c, None, vb_c, None, None, bias_far, dob_c.reshape(seq, HW),
        _lanes_to_classes(lse_b, far_dil), _lanes_to_classes(dst_b, far_dil), steps_far, cls, single_visit=True)
    far = [a.reshape(far_dil, cls, HW) for a in (dqb_far, dkb_far, dvb_far)]

    dh, g_uq, g_ukv, small2 = _bwd_mid(
        dq_m, dkn, dv, dkpe, dqb, dkb_near, dvb_near, far, dga, dgb, cq, ckv, qn, kvn, w_uq_r, w_ukv_r, qg, kvg, tabs, bt)
    bt_w = min(seq, 2 * bt)
    n_tok = seq // bt_w
    first = jax.ShapeDtypeStruct((N_DEV, GRAD_W_IN_CUT, IN_WIDTH // N_DEV), BF16)
    g_in_rest, *reduced_first = _grad_w_in(
        x, dh, bt_w, GRAD_W_IN_CUT,
        reduce_rider([first, g_uq, g_ukv], g_out_recv, (small1, small2), (n_tok, n_tok, 2 * n_tok - 1)))
    grad_x, g_in_rest = _grad_x(dz, dh, w_in_r, bt, ride=reduce_rider([g_in_rest], None, None, (0, 1, seq // bt - 2)))
    return grad_x, [g_in_rest] + reduced_first


MESH_ID = pl.DeviceIdType.MESH
SHARD_SHAPES = ((D_MODEL, IN_WIDTH // N_DEV), (Q_RANK, 768 // N_DEV), (KV_RANK, 1024 // N_DEV), (D_MODEL // N_DEV, D_MODEL))
ADAM_ROWS = (32, 128, 128, 16)


def _me():
    x, y, c = lax.axis_index("x"), lax.axis_index("y"), lax.axis_index("c")
    return x, y, c, 4 * x + 2 * y + c


def _peer(k):
    x, y, c, _ = _me()
    px = 1 - x if (k >> 2) & 1 else x
    py = 1 - y if (k >> 1) & 1 else y
    pc = 1 - c if k & 1 else c
    return (px, py, pc), 4 * px + 2 * py + pc


def _all_gather_weights(shards):
    n = len(shards)
    shard = IN_WIDTH // N_DEV
    k_lo = IN_SPLITS[0] + IN_SPLITS[1]
    k_hi = k_lo + MLA_ROPE

    def body(*refs):
        ins = refs[:n]
        win_ref, wuq_ref, wukv_ref = refs[n:2 * n]
        bufs = refs[2 * n:3 * n]
        send_sems, recv_sems = refs[3 * n:]
        x, y, c, me = _me()
        here, sibling = (x, y, c), (x, y, 1 - c)
        along_x, along_y, across = (1 - x, y), (x, 1 - y), (1 - x, 1 - y)
        for t in range(n):
            bufs[t][me] = ins[t][...].astype(BF16)

        def copy(t, k, chip, pc, to, half=None):
            blk = bufs[t].at[4 * chip[0] + 2 * chip[1] + pc]
            if half is not None:
                rows = SHARD_SHAPES[t][0] // 2
                blk = blk.at[pl.ds(half * rows, rows), :]
            return pltpu.make_async_remote_copy(
                src_ref=blk, dst_ref=blk, send_sem=send_sems.at[t, k], recv_sem=recv_sems.at[t, k],
                device_id=to, device_id_type=MESH_ID)

        sends = []
        for t in range(n):
            sends += [copy(t, 0, (x, y), c, sibling), copy(t, 1, (x, y), c, (*along_x, c)),
                      copy(t, 2, (x, y), c, (*along_y, c))]
        for cp in sends:
            cp.start()
        for t in range(n):
            copy(t, 1, along_x, c, here).wait_recv()
            sends += [copy(t, 3, along_x, c, (*along_y, c), half=0), copy(t, 5, along_x, c, sibling)]
            sends[-2].start()
            sends[-1].start()
        for t in range(n):
            copy(t, 2, along_y, c, here).wait_recv()
            sends += [copy(t, 4, along_y, c, (*along_x, c), half=1), copy(t, 6, along_y, c, sibling)]
            sends[-2].start()
            sends[-1].start()
        for t in range(n):
            copy(t, 3, across, c, here, half=0).wait_recv()
            copy(t, 4, across, c, here, half=1).wait_recv()
            sends.append(copy(t, 7, across, c, sibling))
            sends[-1].start()
        for t in range(n):
            copy(t, 0, (x, y), 1 - c, here).wait_recv()
            for k, chip in ((5, along_x), (6, along_y), (7, across)):
                copy(t, k, chip, 1 - c, here).wait_recv()
        for cp in sends:
            cp.wait_send()

        a_in, a_uq, a_ukv = bufs
        for d in range(N_DEV):
            lo, hi = shard * d, shard * (d + 1)
            if lo < k_lo:
                win_ref[:, lo:min(hi, k_lo)] = a_in[d, :, 0:min(hi, k_lo) - lo]
            if lo < k_hi and hi > k_lo:
                kr = a_in[d, :, k_lo - lo:k_hi - lo]
                for rep in range(4):
                    win_ref[:, C_KR + MLA_ROPE * rep:C_KR + MLA_ROPE * (rep + 1)] = kr
            if hi > k_hi:
                src = max(lo, k_hi)
                win_ref[:, src + C_GA - k_hi:hi + C_GA - k_hi] = a_in[d, :, src - lo:hi - lo]
        for h in range(HEADS):
            wuq_ref[:, MLA_NOPE * h:MLA_NOPE * (h + 1)] = a_uq[h, :, :MLA_NOPE]
            wuq_ref[:, HW + MLA_ROPE * h:HW + MLA_ROPE * (h + 1)] = a_uq[h, :, MLA_NOPE:]
            wukv_ref[:, MLA_NOPE * h:MLA_NOPE * (h + 1)] = a_ukv[h, :, :MLA_NOPE]
            wukv_ref[:, HW + MLA_V * h:HW + MLA_V * (h + 1)] = a_ukv[h, :, MLA_NOPE:]

    vmem = pl.BlockSpec(memory_space=pltpu.VMEM)
    return _pcall(
        body, name="gather_weights",
        in_specs=[vmem] * n, out_specs=[vmem] * n,
        out_shape=[jax.ShapeDtypeStruct((D_MODEL, C_END), BF16), jax.ShapeDtypeStruct((Q_RANK, QW), BF16),
                   jax.ShapeDtypeStruct((KV_RANK, 2 * HW), BF16)],
        scratch_shapes=[pltpu.VMEM((N_DEV,) + s, BF16) for s in SHARD_SHAPES[:n]]
        + [pltpu.SemaphoreType.DMA((n, 8)), pltpu.SemaphoreType.DMA((n, 8))],
        compiler_params=_cparams(),
    )(*shards)


def _gather_w_out_rider(w_out):
    def copies(full_ref, stage, send_sems, recv_sems):
        me = _me()[3]
        out = []
        for k in range(1, N_DEV):
            peer, pidx = _peer(k)
            send = pltpu.make_async_remote_copy(
                src_ref=stage, dst_ref=full_ref.at[me], send_sem=send_sems.at[k - 1], recv_sem=recv_sems.at[k - 1],
                device_id=peer, device_id_type=MESH_ID)
            recv = pltpu.make_async_remote_copy(
                src_ref=stage, dst_ref=full_ref.at[pidx], send_sem=send_sems.at[k - 1], recv_sem=recv_sems.at[k - 1],
                device_id=peer, device_id_type=MESH_ID)
            out.append((send, recv))
        return out

    def start(ins, outs, scr):
        stage, send_sems, recv_sems, own_sem = scr
        stage[...] = ins[0][...].astype(BF16)
        pltpu.make_async_copy(stage, outs[0].at[_me()[3]], own_sem).start()
        for send, _ in copies(outs[0], stage, send_sems, recv_sems):
            send.start()

    def finish(ins, outs, scr):
        stage, send_sems, recv_sems, own_sem = scr
        pltpu.make_async_copy(stage, outs[0].at[_me()[3]], own_sem).wait()
        pairs = copies(outs[0], stage, send_sems, recv_sems)
        for _, recv in pairs:
            recv.wait_recv()
        for send, _ in pairs:
            send.wait_send()

    shape = SHARD_SHAPES[3]
    return Rider(
        args=[w_out], in_specs=[pl.BlockSpec(shape, lambda t, *_: (0, 0))],
        out_shape=[jax.ShapeDtypeStruct((N_DEV,) + shape, BF16)], out_specs=[pl.BlockSpec(memory_space=pl.ANY)],
        scratch=[pltpu.VMEM(shape, BF16), pltpu.SemaphoreType.DMA((N_DEV - 1,)), pltpu.SemaphoreType.DMA((N_DEV - 1,)),
                 pltpu.SemaphoreType.DMA],
        start=start, finish=finish)


def _scatter_g_out_rider(blocks):
    def copies(src_ref, dst_ref, send_sems, recv_sems):
        out = []
        for k in range(1, N_DEV):
            peer, pidx = _peer(k)
            out.append(pltpu.make_async_remote_copy(
                src_ref=src_ref.at[pidx], dst_ref=dst_ref.at[k], send_sem=send_sems.at[k - 1],
                recv_sem=recv_sems.at[k - 1], device_id=peer, device_id_type=MESH_ID))
        return out

    def start(ins, outs, scr):
        send_sems, recv_sems, own_sem = scr
        pltpu.make_async_copy(ins[0].at[_me()[3]], outs[0].at[0], own_sem).start()
        for cp in copies(ins[0], outs[0], send_sems, recv_sems):
            cp.start()

    def finish(ins, outs, scr):
        send_sems, recv_sems, own_sem = scr
        pltpu.make_async_copy(ins[0].at[_me()[3]], outs[0].at[0], own_sem).wait()
        for cp in copies(ins[0], outs[0], send_sems, recv_sems):
            cp.wait()

    hbm = pl.BlockSpec(memory_space=pl.ANY)
    return Rider(
        args=[blocks], in_specs=[hbm], out_shape=[jax.ShapeDtypeStruct(blocks.shape, blocks.dtype)], out_specs=[hbm],
        scratch=[pltpu.SemaphoreType.DMA((N_DEV - 1,)), pltpu.SemaphoreType.DMA((N_DEV - 1,)), pltpu.SemaphoreType.DMA],
        start=start, finish=finish)


def _adamw(w, g, m, v):
    m = ADAM_B1 * m + (1.0 - ADAM_B1) * g
    v = ADAM_B2 * v + (1.0 - ADAM_B2) * jnp.square(g)
    m_hat = m / (1.0 - ADAM_B1 ** ADAM_STEP)
    v_hat = v / (1.0 - ADAM_B2 ** ADAM_STEP)
    delta = -ADAM_LR * (m_hat / (jnp.sqrt(v_hat) + ADAM_EPS) + ADAM_WD * w)
    return delta, m, v


def _chunk_rows(count, cols):
    rows = (16 * 8 * LANES) // (-(-cols // LANES) * LANES)
    while count % rows:
        rows //= 2
    return rows


SMALL_ROWS = (0, 1, 3, 4)
LOSS_ROW = 2


def _reduce_grads_rider(grads3, arrived, small_parts, steps):
    n = len(grads3)
    shapes = [tuple(g.shape[1:]) for g in grads3]
    with_small = arrived is not None
    start_step, first_round, second_round = steps
    n_held = sum(isinstance(g, jax.ShapeDtypeStruct) for g in grads3)
    held, handed = grads3[:n_held], grads3[n_held:]
    assert not any(isinstance(g, jax.ShapeDtypeStruct) for g in handed)

    class Refs:
        def __init__(self, ins, outs, scr):
            self.g3, self.gsum = list(scr[0:n_held]) + list(ins[0:n - n_held]), outs[0:n]
            ins, scr = ins[n - n_held:], scr[n_held:]
            if with_small:
                self.arr, self.sp_wide, self.sp_q = ins
                self.gsum_out, self.ssum = outs[n:n + 2]
            self.own, self.sib, self.part = scr[0:n], scr[n:2 * n], scr[2 * n:3 * n]
            self.in_a, self.out_b, self.in_b = scr[3 * n:4 * n], scr[4 * n:5 * n], scr[5 * n:6 * n]
            self.rsmall = scr[6 * n]
            (self.loc_sems, self.d2d_send, self.d2d_recv, self.a_send, self.a_recv, self.b_send, self.b_recv,
             self.sm_send, self.sm_recv) = scr[6 * n + 1:]
            self.x, self.y, self.c, self.me = _me()
            self.along_x, self.along_y = (1 - self.x, self.y), (self.x, 1 - self.y)
            self.across = (1 - self.x, 1 - self.y)

        def small(self):
            if not with_small:
                return []
            return [pltpu.make_async_remote_copy(
                src_ref=self.rsmall.at[0], dst_ref=self.rsmall.at[k], send_sem=self.sm_send.at[k - 1],
                recv_sem=self.sm_recv.at[k - 1], device_id=_peer(k)[0], device_id_type=MESH_ID)
                for k in range(1, N_DEV)]

        def level1(self):
            local, to_sib = [], []
            for t in range(n):
                for q in range(4):
                    local.append(pltpu.make_async_copy(
                        self.g3[t].at[2 * q + self.c], self.own[t].at[q], self.loc_sems.at[t, q]))
                    to_sib.append(pltpu.make_async_remote_copy(
                        src_ref=self.g3[t].at[2 * q + 1 - self.c], dst_ref=self.sib[t].at[q],
                        send_sem=self.d2d_send.at[t, q], recv_sem=self.d2d_recv.at[t, q],
                        device_id=(self.x, self.y, 1 - self.c), device_id_type=MESH_ID))
            return local, to_sib

        def round_a(self):
            out = []
            for t in range(n):
                half = shapes[t][0] // 2
                for k, (to, chip, h) in enumerate([(self.along_x, self.along_x, 0), (self.along_x, self.across, 0),
                                                   (self.along_y, self.along_y, 1), (self.along_y, self.across, 1)]):
                    out.append(pltpu.make_async_remote_copy(
                        src_ref=self.part[t].at[2 * chip[0] + chip[1], pl.ds(h * half, half), :],
                        dst_ref=self.in_a[t].at[k], send_sem=self.a_send.at[t, k], recv_sem=self.a_recv.at[t, k],
                        device_id=(*to, self.c), device_id_type=MESH_ID))
            return out

        def round_b(self):
            return [pltpu.make_async_remote_copy(
                src_ref=self.out_b[t].at[k], dst_ref=self.in_b[t].at[k], send_sem=self.b_send.at[t, k],
                recv_sem=self.b_recv.at[t, k], device_id=(*to, self.c), device_id_type=MESH_ID)
                for t in range(n) for k, to in enumerate([self.along_y, self.along_x])]

    def chunks(rows, count, fn):
        def step(i, carry):
            fn(pl.multiple_of(i * rows, rows))
            return carry

        lax.fori_loop(0, count // rows, step, 0)

    def start(*refs):
        r = Refs(*refs)
        if with_small:
            r.rsmall[0] = r.sp_wide[...]
            for k, row in enumerate(SMALL_ROWS[2:]):
                r.rsmall[0, row:row + 1, 0:Q_RANK] = r.sp_q[k:k + 1, :]
        local, to_sib = r.level1()
        for cp in r.small() + local + to_sib:
            cp.start()

    def begin_rounds(*refs):
        r = Refs(*refs)
        local, to_sib = r.level1()
        for cp in local:
            cp.wait()
        for cp in to_sib:
            cp.wait_recv()
        my_chip = 2 * r.x + r.y
        for t in range(n):
            rows = _chunk_rows(*shapes[t])

            def pair_sums(at, t=t, rows=rows):
                sl = pl.ds(at, rows)
                for q in range(4):
                    r.part[t][q, sl, :] = (r.own[t][q, sl, :].astype(F32) + r.sib[t][q, sl, :].astype(F32)).astype(BF16)
                r.gsum[t][sl, :] = r.own[t][my_chip, sl, :].astype(F32) + r.sib[t][my_chip, sl, :].astype(F32)

            chunks(rows, shapes[t][0], pair_sums)
        for cp in r.round_a():
            cp.start()
        if not with_small:
            return
        rows_out = _chunk_rows(*SHARD_SHAPES[3])

        def add_arrived(at):
            sl = pl.ds(at, rows_out)
            g = r.arr[0, sl, :].astype(F32)
            for k in range(1, N_DEV):
                g = g + r.arr[k, sl, :].astype(F32)
            r.gsum_out[sl, :] = g

        chunks(rows_out, SHARD_SHAPES[3][0], add_arrived)

    def pass_on(*refs):
        r = Refs(*refs)
        for cp in r.round_a():
            cp.wait_recv()
        q_x, q_y = 2 * r.along_x[0] + r.along_x[1], 2 * r.along_y[0] + r.along_y[1]
        for t in range(n):
            half = shapes[t][0] // 2
            rows = _chunk_rows(half, shapes[t][1])

            def add(at, t=t, rows=rows, half=half):
                lo, hi = pl.ds(at, rows), pl.ds(half + at, rows)
                r.gsum[t][lo, :] = r.gsum[t][lo, :] + r.in_a[t][0, lo, :].astype(F32)
                r.out_b[t][0, lo, :] = (r.part[t][q_y, lo, :].astype(F32) + r.in_a[t][1, lo, :].astype(F32)).astype(BF16)
                r.gsum[t][hi, :] = r.gsum[t][hi, :] + r.in_a[t][2, lo, :].astype(F32)
                r.out_b[t][1, lo, :] = (r.part[t][q_x, hi, :].astype(F32) + r.in_a[t][3, lo, :].astype(F32)).astype(BF16)

            chunks(rows, half, add)
        for cp in r.round_b():
            cp.start()

    def finish(*refs):
        r = Refs(*refs)
        passed = r.round_b()
        for cp in passed:
            cp.wait_recv()
        for t in range(n):
            half = shapes[t][0] // 2
            rows = _chunk_rows(half, shapes[t][1])

            def add(at, t=t, rows=rows, half=half):
                lo, hi = pl.ds(at, rows), pl.ds(half + at, rows)
                r.gsum[t][lo, :] = r.gsum[t][lo, :] + r.in_b[t][0, lo, :].astype(F32)
                r.gsum[t][hi, :] = r.gsum[t][hi, :] + r.in_b[t][1, lo, :].astype(F32)

            chunks(rows, half, add)
        small = r.small()
        for cp in small:
            cp.wait_recv()
        if with_small:
            tot = r.rsmall[r.me]
            for d in range(1, N_DEV):
                tot = tot + r.rsmall[jnp.bitwise_xor(r.me, d)]
            r.ssum[...] = tot
        for cp in small + r.level1()[1] + r.round_a() + passed:
            cp.wait_send()

    hbm = pl.BlockSpec(memory_space=pl.ANY)
    dma = pltpu.SemaphoreType.DMA

    def whole(shape):
        return pl.BlockSpec(shape, lambda i: (0,) * len(shape))

    def halves(slots):
        return [pltpu.VMEM((slots, s[0] // 2, s[1]), BF16) for s in shapes]

    small_args = [arrived, *small_parts] if with_small else []
    out_shapes = shapes + ([SHARD_SHAPES[3], (8, D_MODEL)] if with_small else [])
    return Rider(
        args=list(handed) + small_args,
        in_specs=[hbm] * len(handed) + [whole(a.shape) for a in small_args],
        out_shape=[jax.ShapeDtypeStruct(s, F32) for s in out_shapes], out_specs=[whole(s) for s in out_shapes],
        scratch=[pltpu.VMEM(g.shape, g.dtype) for g in held]
        + [pltpu.VMEM((4,) + s, BF16) for _ in range(3) for s in shapes]
        + halves(4) + halves(2) + halves(2)
        + [pltpu.VMEM((N_DEV, 8, D_MODEL), F32), dma((n, 4)), dma((n, 4)), dma((n, 4)), dma((n, 4)), dma((n, 4)),
           dma((n, 2)), dma((n, 2)), dma((N_DEV - 1,)), dma((N_DEV - 1,))],
        start=start, finish=finish, stages=((first_round, begin_rounds), (second_round, pass_on)),
        start_step=start_step, shared=n_held)


def _adamw_update(grads, small_grad, wmv, small_wmv):
    n_small = len(small_wmv)
    flat_grads = [g for pieces in grads for g in pieces]
    n_g = len(flat_grads)

    def body(*refs):
        g_refs, sg_ref, refs = iter(refs[0:n_g]), refs[n_g], refs[n_g + 1:]
        wmv_refs = [refs[3 * t:3 * t + 3] for t in range(4)]
        swmv_refs = [refs[12 + 3 * t:15 + 3 * t] for t in range(n_small)]
        outs = refs[12 + 3 * n_small:]
        out_refs = [outs[4 * t:4 * t + 4] for t in range(4)]
        sout_refs = [outs[16 + 4 * t:20 + 4 * t] for t in range(n_small)]
        loss_ref = outs[16 + 4 * n_small]
        for t, (w_ref, m_ref, v_ref) in enumerate(swmv_refs):
            g = sg_ref[SMALL_ROWS[t]:SMALL_ROWS[t] + 1, :w_ref.shape[1]]
            delta, m, v = _adamw(w_ref[...], g, m_ref[...], v_ref[...])
            sout_refs[t][0][...], sout_refs[t][1][...], sout_refs[t][2][...], sout_refs[t][3][...] = g, delta, m, v
        loss_ref[...] = (0.5 / D_MODEL) * jnp.sum(sg_ref[LOSS_ROW:LOSS_ROW + 1, :], axis=1, keepdims=True)
        for t in range(4):
            rows = ADAM_ROWS[t]
            w_ref, m_ref, v_ref = wmv_refs[t]
            g_out, d_out, m_out, v_out = out_refs[t]

            def step(i, carry, g_ref, first, rows=rows, w_ref=w_ref, m_ref=m_ref, v_ref=v_ref,
                     g_out=g_out, d_out=d_out, m_out=m_out, v_out=v_out):
                at = pl.multiple_of(i * rows, rows)
                r = pl.ds(pl.multiple_of(first + at, rows), rows)
                g = g_ref[pl.ds(at, rows), :]
                delta, m, v = _adamw(w_ref[r, :], g, m_ref[r, :], v_ref[r, :])
                g_out[r, :], d_out[r, :], m_out[r, :], v_out[r, :] = g, delta, m, v
                return carry

            first = 0
            for piece in grads[t]:
                lax.fori_loop(0, piece.shape[0] // rows, functools.partial(step, g_ref=next(g_refs), first=first), 0)
                first += piece.shape[0]

    vmem = pl.BlockSpec(memory_space=pltpu.VMEM)
    flat_wmv = [a for trio in wmv for a in trio]
    flat_small = [a for trio in small_wmv for a in trio]
    out_shape = ([jax.ShapeDtypeStruct(s, F32) for s in SHARD_SHAPES for _ in range(4)]
                 + [jax.ShapeDtypeStruct(trio[0].shape, F32) for trio in small_wmv for _ in range(4)]
                 + [jax.ShapeDtypeStruct((1, 1), F32)])
    return _pcall(
        body, name="adamw",
        in_specs=[vmem] * (n_g + 1 + len(flat_wmv) + len(flat_small)), out_specs=[vmem] * len(out_shape),
        out_shape=out_shape,
        compiler_params=_cparams(),
    )(*flat_grads, small_grad, *flat_wmv, *flat_small)


def kernel(x, w_in, q_norm_g, kv_norm_g, w_uq, w_ukv, w_out, ln_g, ln_b, loss_target, m_w_in, m_q_norm_g, m_kv_norm_g, m_w_uq, m_w_ukv, m_w_out, m_ln_g, m_ln_b, v_w_in, v_q_norm_g, v_kv_norm_g, v_w_uq, v_w_ukv, v_w_out, v_ln_g, v_ln_b):
    w_in_r, w_uq_r, w_ukv_r = _all_gather_weights([w_in, w_uq, w_ukv])
    grad_x, (g_in_rest, g_in_first, g_uq, g_ukv, g_out, g_small) = _local_step(
        x[0], loss_target[0], w_in_r, w_uq_r, w_ukv_r, _gather_w_out_rider(w_out), _scatter_g_out_rider,
        _reduce_grads_rider, q_norm_g, kv_norm_g, ln_g, ln_b)
    row = lambda a: a.reshape(1, -1)
    small_wmv = [(row(ln_g), row(m_ln_g), row(v_ln_g)), (row(ln_b), row(m_ln_b), row(v_ln_b)),
                 (row(q_norm_g), row(m_q_norm_g), row(v_q_norm_g)), (row(kv_norm_g), row(m_kv_norm_g), row(v_kv_norm_g))]
    wmv = [(w_in, m_w_in, v_w_in), (w_uq, m_w_uq, v_w_uq), (w_ukv, m_w_ukv, v_w_ukv), (w_out, m_w_out, v_w_out)]
    res = _adamw_update([[g_in_first, g_in_rest], [g_uq], [g_ukv], [g_out]], g_small, wmv, small_wmv)
    big = [res[4 * t:4 * t + 4] for t in range(4)]
    small = [[a.reshape(-1) for a in res[16 + 4 * t:20 + 4 * t]] for t in range(4)]
    loss = res[32].reshape(())

    def group(kind):
        return (big[0][kind], small[2][kind], small[3][kind], big[1][kind], big[2][kind], big[3][kind],
                small[0][kind], small[1][kind])

    return (loss, grad_x[None], *group(0), *group(1), *group(2), *group(3))
```

```python
import functools
from typing import Callable, NamedTuple

import numpy as np
import jax
import jax.numpy as jnp
from jax import lax
from jax.experimental import pallas as pl
from jax.experimental.pallas import tpu as pltpu

F32 = jnp.float32
BF16 = jnp.bfloat16

D_MODEL = 1024
ROPE_THETA = 500000.0
NEG = -1e30
RMS_EPS = 1e-6
LN_EPS = 1e-5
HEADS = 8
MLA_NOPE = 64
MLA_ROPE = 32
MLA_V = 64
Q_RANK = 384
KV_RANK = 256
DIL_HEAD = 64
DIL_ROT = 16
DIL_CONFIGS = ((128, 1), (512, 4), (2048, 16))
DIL_NEAR = 512
HW = HEADS * 64
QW = HW + HEADS * MLA_ROPE
IN_SPLITS = (Q_RANK, KV_RANK, MLA_ROPE, HW, HW, HW, HW, HW)
IN_WIDTH = sum(IN_SPLITS)
ALPHA = 2.0 ** 0.25
MLA_SCALE = (MLA_NOPE + MLA_ROPE) ** -0.5
DIL_SCALE = DIL_HEAD ** -0.5
LOG2E = 1.4426950408889634
LN2 = 0.6931471805599453

ADAM_LR = 0.001
ADAM_B1 = 0.9
ADAM_B2 = 0.999
ADAM_EPS = 1e-08
ADAM_WD = 0.01
ADAM_STEP = 10

N_DEV = 8
LANES = 128
VMEM_LIMIT = 56 * 1024 * 1024
BLOCK_TOKENS = 512
BLOCK_MLA = 512
BLOCK_DIL = 512
GRAD_W_IN_CUT = 512

C_CQ, C_CKV, C_KR, C_GA, C_QB, C_KB, C_VB, C_GB, C_END = 0, 384, 640, 768, 1280, 1792, 2304, 2816, 3328

NT = (((1,), (1,)), ((), ()))
TN = (((0,), (0,)), ((), ()))


def _pcall(body, **kw):
    return pl.pallas_call(body, **kw)


def _cparams(**kw):
    return pltpu.CompilerParams(vmem_limit_bytes=VMEM_LIMIT, **kw)


def _rope_tables(seq):
    def tabs(dim, period):
        half = dim // 2
        inv = np.float32(ROPE_THETA) ** (-np.arange(0, dim, 2, dtype=np.float32) / np.float32(dim))
        ang = np.arange(seq, dtype=np.float32)[:, None] * inv.astype(np.float32)[None, :]
        cos, sin = np.cos(ang).astype(np.float32), np.sin(ang).astype(np.float32)
        j = np.arange(LANES) % period
        f = j % half
        c = np.where(j < dim, cos[:, f], np.float32(1.0))
        s1 = np.where(j < half, -sin[:, f], np.float32(0.0))
        s2 = np.where((j >= half) & (j < dim), sin[:, f], np.float32(0.0))
        return [c, s1, s2]
    return np.stack(tabs(MLA_ROPE, MLA_ROPE) + tabs(DIL_ROT, DIL_HEAD)).astype(np.float32)


def _rope(t, c, s1, s2, half):
    return t * c + pltpu.roll(t, LANES - half, 1) * s1 + pltpu.roll(t, half, 1) * s2


def _rope_t(d, c, s1, s2, half):
    return d * c + pltpu.roll(d * s1, half, 1) + pltpu.roll(d * s2, LANES - half, 1)


def _rope_wide(fn, t, c, s1, s2, half):
    return jnp.concatenate(
        [fn(t[:, i:i + LANES], c, s1, s2, half) for i in range(0, t.shape[1], LANES)], axis=1)


def _mla_bias_t(blk):
    a = np.arange(blk)
    causal = np.where(a[:, None] <= a[None, :], 0.0, NEG)
    return np.stack([np.zeros((blk, blk)), causal]).astype(np.float32)


def _dil_bias_t(blk, reach):
    a = np.arange(blk)
    out = []
    for off in range(-(-reach // blk) + 1):
        delta = blk * off + a[None, :] - a[:, None]
        mult = np.zeros((blk, blk))
        for window, dil in DIL_CONFIGS:
            mult += (delta >= 0) & (delta % dil == 0) & (delta <= min(window, reach))
        out.append(np.where(mult > 0, np.log2(np.maximum(mult, 1.0)), NEG))
    return np.stack(out).astype(np.float32)


def _dil_far_bias_t(length):
    window, dil = DIL_CONFIGS[-1]
    a = np.arange(length)
    steps_back = a[None, :] - a[:, None]
    seen = (steps_back * dil > DIL_NEAR) & (steps_back * dil <= window)
    return np.where(seen, 0.0, NEG).astype(np.float32)[None]


def _lanes_to_classes(a, dil):
    h, s = a.shape
    return a.reshape(h, s // dil, dil).transpose(0, 2, 1).reshape(h, s)


def _lanes_from_classes(a, dil):
    h, s = a.shape
    return a.reshape(h, dil, s // dil).transpose(0, 2, 1).reshape(h, s)


def _steps(nq, span, by_key, diag_only_bias):
    rows = []
    if by_key:
        for ki in range(nq):
            hi = min(nq - 1, ki + span)
            for qi in range(ki, hi + 1):
                rows.append((qi, ki, int(qi == ki), int(qi == hi)))
    else:
        for qi in range(nq):
            lo = max(0, qi - span)
            for ki in range(lo, qi + 1):
                rows.append((qi, ki, int(ki == lo), int(ki == qi)))
    arr = np.array(rows, dtype=np.int32)
    off = arr[:, 0] - arr[:, 1]
    bias_idx = (off == 0).astype(np.int32) if diag_only_bias else off.astype(np.int32)
    return [jnp.asarray(v) for v in (arr[:, 0], arr[:, 1], bias_idx, arr[:, 2], arr[:, 3])]


def _by_class(val, out_ref, lanes_sc):
    n_cls, per = out_ref.shape[0], out_ref.shape[1]
    for c in range(val.shape[1] // LANES):
        lanes_sc[c] = val[:, LANES * c:LANES * (c + 1)]
        for r in range(n_cls):
            rows = lanes_sc.at[c][pl.ds(r, per, stride=n_cls), :]
            out_ref[r, :, LANES * c:LANES * (c + 1)] = rows.astype(out_ref.dtype)


def _in_sequence(ref, lanes_sc):
    n_cls, per, width = ref.shape
    for c in range(width // LANES):
        for r in range(n_cls):
            lanes_sc.at[c][pl.ds(r, per, stride=n_cls), :] = ref[r, :, LANES * c:LANES * (c + 1)].astype(F32)
    return jnp.concatenate([lanes_sc[c] for c in range(width // LANES)], axis=1)


def _fwd_proj(x, w_in_r, w_uq_r, w_ukv_r, qg, kvg, tabs, bt, n_cls):
    seq = x.shape[0]

    def body(x_ref, win_ref, wuq_ref, wukv_ref, qg_ref, kvg_ref, tab_ref,
             cq_ref, ckv_ref, qn_ref, kvn_ref, qcat_ref, kn_ref, kpe_ref, v_ref,
             ga_ref, gb_ref, qb_ref, kb_ref, vb_ref, knt_ref, kpet_ref, vt_ref, kbt_ref, vbt_ref,
             qbc_ref, kbc_ref, vbc_ref, lanes_sc):
        xb = x_ref[...].astype(BF16)

        def proj(lo, hi):
            return jnp.dot(xb, win_ref[:, lo:hi], preferred_element_type=F32)

        m_tabs = (tab_ref[0], tab_ref[1], tab_ref[2])
        d_tabs = (tab_ref[3], tab_ref[4], tab_ref[5])

        def use_cq(cq):
            cq_ref[...] = cq
            qn = (cq * lax.rsqrt(jnp.mean(cq * cq, axis=1, keepdims=True) + RMS_EPS) * qg_ref[...]).astype(BF16)
            qn_ref[...] = qn
            q = jnp.dot(qn, wuq_ref[...], preferred_element_type=F32)
            qcat_ref[:, :HW] = (q[:, :HW] * (MLA_SCALE * LOG2E)).astype(BF16)
            qcat_ref[:, HW:] = (
                _rope_wide(_rope, q[:, HW:], *m_tabs, MLA_ROPE // 2) * (MLA_SCALE * LOG2E)).astype(BF16)

        def use_ckv(ckv):
            ckv_ref[...] = ckv
            kvn = (ckv * lax.rsqrt(jnp.mean(ckv * ckv, axis=1, keepdims=True) + RMS_EPS) * kvg_ref[...]).astype(BF16)
            kvn_ref[...] = kvn
            kv = jnp.dot(kvn, wukv_ref[...], preferred_element_type=F32)
            kn_ref[...] = kv[:, :HW].astype(BF16)
            v_ref[...] = kv[:, HW:].astype(BF16)
            knt_ref[...] = kv[:, :HW].T.astype(BF16)
            vt_ref[...] = kv[:, HW:].T.astype(BF16)

        def use_kr(kr):
            kpe = _rope(kr, *m_tabs, MLA_ROPE // 2)
            kpe_ref[...] = kpe.astype(BF16)
            kpet_ref[...] = kpe.T[:MLA_ROPE, :].astype(BF16)

        def use_ga(ga):
            ga_ref[...] = ga

        def use_qb(qb):
            qb = _rope_wide(_rope, qb, *d_tabs, DIL_ROT // 2) * (DIL_SCALE * LOG2E)
            qb_ref[...] = qb.astype(BF16)
            _by_class(qb, qbc_ref, lanes_sc)

        def use_kb(kb):
            kb = _rope_wide(_rope, kb, *d_tabs, DIL_ROT // 2)
            kb_ref[...] = kb.astype(BF16)
            kbt_ref[...] = kb.T.astype(BF16)
            _by_class(kb, kbc_ref, lanes_sc)

        def use_vb(vb):
            vb_ref[...] = vb.astype(BF16)
            vbt_ref[...] = vb.T.astype(BF16)
            _by_class(vb, vbc_ref, lanes_sc)

        def use_gb(gb):
            gb_ref[...] = gb

        pieces = [(C_CQ, C_CKV, use_cq), (C_CKV, C_KR, use_ckv), (C_KR, C_GA, use_kr), (C_GA, C_QB, use_ga),
                  (C_QB, C_KB, use_qb), (C_KB, C_VB, use_kb), (C_VB, C_GB, use_vb), (C_GB, C_END, use_gb)]
        ahead = proj(*pieces[0][:2])
        for n, (_, _, use) in enumerate(pieces):
            cur = ahead
            if n + 1 < len(pieces):
                ahead = proj(*pieces[n + 1][:2])
            use(cur)

    def tok(width):
        return pl.BlockSpec((bt, width), lambda i: (i, 0))

    def tok_t(height):
        return pl.BlockSpec((height, bt), lambda i: (0, i))

    def full(a):
        return pl.BlockSpec(a.shape, lambda i: (0,) * a.ndim)

    outs = [(Q_RANK, F32), (KV_RANK, F32), (Q_RANK, BF16), (KV_RANK, BF16), (QW, BF16), (HW, BF16),
            (LANES, BF16), (HW, BF16), (HW, F32), (HW, F32), (HW, BF16), (HW, BF16), (HW, BF16)]
    outs_t = [HW, MLA_ROPE, HW, HW, HW]
    by_class = pl.BlockSpec((n_cls, bt // n_cls, HW), lambda i: (0, i, 0))
    return _pcall(
        body, name="fwd_proj", grid=(seq // bt,),
        in_specs=[tok(D_MODEL), full(w_in_r), full(w_uq_r), full(w_ukv_r), full(qg), full(kvg),
                  pl.BlockSpec((6, bt, LANES), lambda i: (0, i, 0))],
        out_specs=[tok(w) for w, _ in outs] + [tok_t(h) for h in outs_t] + [by_class] * 3,
        out_shape=[jax.ShapeDtypeStruct((seq, w), dt) for w, dt in outs]
        + [jax.ShapeDtypeStruct((h, seq), BF16) for h in outs_t]
        + [jax.ShapeDtypeStruct((n_cls, seq // n_cls, HW), BF16)] * 3,
        scratch_shapes=[pltpu.VMEM((HW // LANES, bt, LANES), F32)],
        compiler_params=_cparams(dimension_semantics=("arbitrary",)),
    )(x, w_in_r, w_uq_r, w_ukv_r, qg, kvg, tabs)


def _head_masks(lane, h):
    e, g = h % 2, h % 4
    me = (lane >= 64 * e) & (lane < 64 * e + 64)
    mr = (lane >= 32 * g) & (lane < 32 * g + 32)
    return me, mr


def _masked(mask, a):
    return jnp.where(mask, a, jnp.zeros_like(a))


def _pair_operands(q_ref, k_ref, kpe_ref, lane, j, ks=slice(None), qs=slice(None)):
    cols = slice(LANES * j, LANES * (j + 1))
    qc = q_ref[qs, cols]
    kj = k_ref[ks, cols]
    kes = []
    for h in (2 * j, 2 * j + 1):
        me, mr = _head_masks(lane, h)
        ke = _masked(me, kj)
        if kpe_ref is not None:
            ke = jnp.concatenate([ke, _masked(mr, kpe_ref[ks, :])], axis=1)
        kes.append(ke)
    if kpe_ref is not None:
        qc = jnp.concatenate([qc, q_ref[qs, HW + LANES * (j // 2):HW + LANES * (j // 2 + 1)]], axis=1)
    return qc, kes


def _tile_variants(bias_t):
    out = {}
    for i, tile in enumerate(np.asarray(bias_t)):
        h = tile.shape[0] // 2
        skip = 1 if (tile[h:, :h] == NEG).all() else 2 if (tile[:h, h:] == NEG).all() else 0
        out[i] = (bool((tile != 0).any()), skip)
    return out


def _tile_parts(blk, skip):
    lo, hi, full = slice(0, blk // 2), slice(blk // 2, blk), slice(0, blk)
    return {0: [(full, full)], 1: [(lo, full), (hi, hi)], 2: [(hi, full), (lo, lo)]}[skip]


class Rider(NamedTuple):
    args: list
    in_specs: list
    out_shape: list
    out_specs: list
    scratch: list
    start: Callable
    finish: Callable
    stages: tuple = ()
    start_step: int = 0
    shared: int = 0


def _ride_along(body, ride, n_prefetch, n_in, n_out, n_scratch, n_steps):
    if ride is None:
        return body

    def wrapped(*refs):
        pre, rest = refs[:n_prefetch], refs[n_prefetch:]
        a = n_in
        b = a + len(ride.args)
        c = b + n_out
        d = c + len(ride.out_shape)
        e = d + n_scratch
        mine = (rest[a:b], rest[c:d], rest[e:])
        t = pl.program_id(0)
        pl.when(t == ride.start_step)(lambda: ride.start(*mine))
        for at, stage in ride.stages:
            pl.when(t == at)(functools.partial(stage, *mine))
        body(*pre, *rest[:a], *rest[b:c], *rest[d:e + ride.shared])
        pl.when(t == n_steps - 1)(lambda: ride.finish(*mine))

    return wrapped


def _attn_fwd(name, q, k, kpe, vt, bias_t, steps, blk, ride=None, v_token_major=False):
    seq = q.shape[0]
    mla = kpe is not None
    n_steps = int(steps[0].shape[0])
    variants = _tile_variants(bias_t)

    def body(qi_r, ki_r, bi_r, fi_r, la_r, *refs):
        if mla:
            q_ref, k_ref, kpe_ref, vt_ref, b_ref, o_ref, lse_ref, m_sc, l_sc, acc_sc, st_sc = refs
        else:
            q_ref, k_ref, vt_ref, b_ref, o_ref, lse_ref, m_sc, l_sc, acc_sc, st_sc = refs
        t = pl.program_id(0)

        @pl.when(fi_r[t] == 1)
        def _():
            m_sc[...] = jnp.full(m_sc.shape, NEG, F32)
            l_sc[...] = jnp.zeros(l_sc.shape, F32)
            acc_sc[...] = jnp.zeros(acc_sc.shape, F32)

        lane = lax.broadcasted_iota(jnp.int32, (1, LANES), 1)
        if v_token_major:
            vt_all = vt_ref[...].astype(F32).T.astype(BF16)
            vt_rows = lambda rows, ks: vt_all[rows, ks]
        else:
            vt_rows = lambda rows, ks: vt_ref[rows, ks]

        def tile_pass(ks, qs, with_bias):
            nk, nq = ks.stop - ks.start, qs.stop - qs.start
            ones = jnp.ones((16, nk), BF16)

            def pair_scores(j):
                qc, kes = _pair_operands(q_ref, k_ref, kpe_ref if mla else None, lane, j, ks, qs)
                st = lax.dot_general(jnp.concatenate(kes, axis=0), qc, NT, preferred_element_type=F32)
                maxes = []
                for e in range(2):
                    se = st[e * nk:(e + 1) * nk]
                    if with_bias:
                        se = se + b_ref[0, ks, qs]
                    st_sc[j % 2, e * nk:(e + 1) * nk, 0:nq] = se
                    maxes.append(jnp.max(se, axis=0, keepdims=True))
                return maxes

            def softmax_pv(h, col_max):
                st = st_sc[(h // 2) % 2, (h % 2) * nk:(h % 2 + 1) * nk, 0:nq]
                hrow = slice(h, h + 1)
                m_prev = m_sc[hrow, qs]
                m_new = jnp.maximum(m_prev, col_max)
                alpha = jnp.exp2(m_prev - m_new)
                pt = jnp.exp2(st - m_new).astype(BF16)
                m_sc[hrow, qs] = m_new
                rows = slice(64 * h, 64 * h + 64)
                res = jnp.dot(jnp.concatenate([vt_rows(rows, ks), ones], axis=0), pt, preferred_element_type=F32)
                acc_sc[rows, qs] = alpha * acc_sc[rows, qs] + res[:64]
                l_sc[hrow, qs] = alpha * l_sc[hrow, qs] + res[64:65]

            maxes = pair_scores(0)
            for j in range(HEADS // 2):
                cur = maxes
                if j + 1 < HEADS // 2:
                    maxes = pair_scores(j + 1)
                softmax_pv(2 * j, cur[0])
                softmax_pv(2 * j + 1, cur[1])

        def step(with_bias, skip):
            for ks, qs in _tile_parts(blk, skip):
                tile_pass(ks, qs, with_bias)

        for idx, (with_bias, skip) in variants.items():
            if len(variants) == 1:
                step(with_bias, skip)
            else:
                pl.when(bi_r[t] == idx)(functools.partial(step, with_bias, skip))

        @pl.when(la_r[t] == 1)
        def _():
            for h in range(HEADS):
                rows = slice(64 * h, 64 * h + 64)
                acc_sc[rows, :] = acc_sc[rows, :] / l_sc[h:h + 1, :]
            o_ref[...] = acc_sc[...].T
            lse_ref[...] = m_sc[...] + jnp.log2(l_sc[...])

    qmap = lambda t, qi, ki, bi, fi, la: (qi[t], 0)
    kmap = lambda t, qi, ki, bi, fi, la: (ki[t], 0)
    in_specs = [pl.BlockSpec((blk, q.shape[1]), qmap), pl.BlockSpec((blk, HW), kmap)]
    args = [q, k]
    if mla:
        in_specs.append(pl.BlockSpec((blk, LANES), kmap))
        args.append(kpe)
    in_specs += [pl.BlockSpec((blk, HW), kmap) if v_token_major else
                 pl.BlockSpec((HW, blk), lambda t, qi, ki, bi, fi, la: (0, ki[t])),
                 pl.BlockSpec((1, blk, blk), lambda t, qi, ki, bi, fi, la: (bi[t], 0, 0))]
    args += [vt, jnp.asarray(bias_t)]
    out_specs = [pl.BlockSpec((blk, HW), qmap), pl.BlockSpec((HEADS, blk), lambda t, qi, ki, bi, fi, la: (0, qi[t]))]
    out_shape = [jax.ShapeDtypeStruct((seq, HW), F32), jax.ShapeDtypeStruct((HEADS, seq), F32)]
    scratch = [pltpu.VMEM((HEADS, blk), F32), pltpu.VMEM((HEADS, blk), F32),
               pltpu.VMEM((HW, blk), F32), pltpu.VMEM((2, 2 * blk, blk), F32)]
    body = _ride_along(body, ride, 5, len(args), len(out_shape), len(scratch), n_steps)
    if ride is not None:
        args, in_specs = args + ride.args, in_specs + ride.in_specs
        out_specs, out_shape, scratch = out_specs + ride.out_specs, out_shape + ride.out_shape, scratch + ride.scratch
    return _pcall(
        body, name=name,
        grid_spec=pltpu.PrefetchScalarGridSpec(
            num_scalar_prefetch=5, grid=(n_steps,), in_specs=in_specs, out_specs=out_specs, scratch_shapes=scratch),
        out_shape=out_shape,
        compiler_params=_cparams(dimension_semantics=("arbitrary",)),
    )(*steps, *args)


def _attn_bwd(name, q, k, kpe, v, kt, kpet, bias_t, do, lse, dstat, steps, blk, ride=None, single_visit=False):
    assert not (single_visit and kpe is not None) and (kt is not None or single_visit)
    seq = q.shape[0]
    mla = kpe is not None
    qw = q.shape[1]
    n_steps = int(steps[0].shape[0])
    dk_dtype = BF16 if mla else F32
    variants = _tile_variants(bias_t)

    def body(qi_r, ki_r, bi_r, fi_r, la_r, *refs):
        if mla:
            (q_ref, k_ref, kpe_ref, v_ref, kt_ref, kpet_ref, b_ref, do_ref, lse_ref, d_ref,
             dq_ref, dk_ref, dkpe_ref, dv_ref, dk_sc, dkpe_sc, dv_sc, st_sc, dpt_sc) = refs
        else:
            q_ref, k_ref, v_ref, *rest = refs
            kt_ref = rest.pop(0) if kt is not None else None
            b_ref, do_ref, lse_ref, d_ref, dq_out_ref, dk_ref, dv_ref, dk_sc, dv_sc, st_sc, dpt_sc, *rest = rest
            dq_ref = rest[0] if single_visit else dq_out_ref
        t = pl.program_id(0)

        @pl.when(jnp.logical_or(t == 0, single_visit))
        def _():
            dq_ref[...] = jnp.zeros(dq_ref.shape, F32)

        @pl.when(fi_r[t] == 1)
        def _():
            dk_sc[...] = jnp.zeros(dk_sc.shape, F32)
            dv_sc[...] = jnp.zeros(dv_sc.shape, F32)
            if mla:
                dkpe_sc[...] = jnp.zeros(dkpe_sc.shape, F32)

        qi = 0 if single_visit else qi_r[t]
        lane = lax.broadcasted_iota(jnp.int32, (1, LANES), 1)
        if kt is None:
            kt_all = k_ref[...].astype(F32).T.astype(BF16)
            kt_rows = lambda rows, ks: kt_all[rows, ks]
        else:
            kt_rows = lambda rows, ks: kt_ref[rows, ks]

        def tile_pass(ks, qs, with_bias):
            nk, nq = ks.stop - ks.start, qs.stop - qs.start

            def pair_matmuls(j):
                cols = slice(LANES * j, LANES * (j + 1))
                qc, kes = _pair_operands(q_ref, k_ref, kpe_ref if mla else None, lane, j, ks, qs)
                st_sc[j % 2, 0:2 * nk, 0:nq] = lax.dot_general(
                    jnp.concatenate(kes, axis=0), qc, NT, preferred_element_type=F32)
                vj = v_ref[ks, cols]
                ves = [_masked(_head_masks(lane, h)[0], vj) for h in (2 * j, 2 * j + 1)]
                dpt_sc[j % 2, 0:2 * nk, 0:nq] = lax.dot_general(
                    jnp.concatenate(ves, axis=0), do_ref[qs, cols], NT, preferred_element_type=F32)

            def pair_grads(j):
                cols = slice(LANES * j, LANES * (j + 1))
                qj, doj = q_ref[qs, cols], do_ref[qs, cols]
                if mla:
                    qr = q_ref[qs, HW + LANES * (j // 2):HW + LANES * (j // 2 + 1)]
                pts, dsts, qms, doms = [], [], [], []
                for e in range(2):
                    h = 2 * j + e
                    me, mr = _head_masks(lane, h)
                    st = st_sc[j % 2, e * nk:(e + 1) * nk, 0:nq]
                    if with_bias:
                        st = st + b_ref[0, ks, qs]
                    pt = jnp.exp2(st - lse_ref[h:h + 1, qs])
                    dst = (pt * (dpt_sc[j % 2, e * nk:(e + 1) * nk, 0:nq] - d_ref[h:h + 1, qs])).astype(BF16)
                    pts.append(pt.astype(BF16))
                    dsts.append(dst)
                    doms.append(_masked(me, doj))
                    qm = _masked(me, qj)
                    if mla:
                        qm = jnp.concatenate([qm, _masked(mr, qr)], axis=1)
                    qms.append(qm)
                    ktl = kt_rows(slice(64 * h, 64 * h + 64), ks)
                    if mla:
                        ktl = jnp.concatenate([ktl, kpet_ref[:, ks]], axis=0)
                    dqc = jnp.dot(ktl, dst, preferred_element_type=F32)
                    dq_ref[qi, 64 * h:64 * h + 64, qs] += dqc[:64]
                    if mla:
                        dq_ref[qi, HW + MLA_ROPE * h:HW + MLA_ROPE * (h + 1), qs] += dqc[64:]
                dv_sc[ks, cols] += jnp.dot(
                    jnp.concatenate(pts, axis=1), jnp.concatenate(doms, axis=0), preferred_element_type=F32)
                dkc = jnp.dot(jnp.concatenate(dsts, axis=1), jnp.concatenate(qms, axis=0), preferred_element_type=F32)
                dk_sc[ks, cols] += dkc[:, :LANES]
                if mla:
                    dkpe_sc[ks, :] += dkc[:, LANES:]

            pair_matmuls(0)
            for j in range(HEADS // 2):
                if j + 1 < HEADS // 2:
                    pair_matmuls(j + 1)
                pair_grads(j)

        def step(with_bias, skip):
            for ks, qs in _tile_parts(blk, skip):
                tile_pass(ks, qs, with_bias)

        for idx, (with_bias, skip) in variants.items():
            if len(variants) == 1:
                step(with_bias, skip)
            else:
                pl.when(bi_r[t] == idx)(functools.partial(step, with_bias, skip))
        if single_visit:
            dq_out_ref[...] = dq_ref[0].T

        @pl.when(la_r[t] == 1)
        def _():
            dk_ref[...] = (dk_sc[...] * LN2).astype(dk_ref.dtype)
            dv_ref[...] = dv_sc[...].astype(dv_ref.dtype)
            if mla:
                dkpe_ref[...] = dkpe_sc[...] * LN2

    qmap = lambda t, qi, ki, bi, fi, la: (qi[t], 0)
    kmap = lambda t, qi, ki, bi, fi, la: (ki[t], 0)
    qmap_t = lambda t, qi, ki, bi, fi, la: (0, qi[t])
    kmap_t = lambda t, qi, ki, bi, fi, la: (0, ki[t])
    in_specs = [pl.BlockSpec((blk, qw), qmap), pl.BlockSpec((blk, HW), kmap)]
    args = [q, k]
    if mla:
        in_specs.append(pl.BlockSpec((blk, LANES), kmap))
        args.append(kpe)
    in_specs.append(pl.BlockSpec((blk, HW), kmap))
    args.append(v)
    if kt is not None:
        in_specs.append(pl.BlockSpec((HW, blk), kmap_t))
        args.append(kt)
    if mla:
        in_specs.append(pl.BlockSpec((MLA_ROPE, blk), kmap_t))
        args.append(kpet)
    in_specs += [pl.BlockSpec((1, blk, blk), lambda t, qi, ki, bi, fi, la: (bi[t], 0, 0)),
                 pl.BlockSpec((blk, HW), qmap), pl.BlockSpec((HEADS, blk), qmap_t), pl.BlockSpec((HEADS, blk), qmap_t)]
    args += [jnp.asarray(bias_t), do, lse, dstat]
    dq_shape = (seq // blk, qw, blk)
    if single_visit:
        out_specs, out_shape = [pl.BlockSpec((blk, qw), qmap)], [jax.ShapeDtypeStruct((seq, qw), F32)]
    else:
        out_specs = [pl.BlockSpec(dq_shape, lambda t, qi, ki, bi, fi, la: (0, 0, 0))]
        out_shape = [jax.ShapeDtypeStruct(dq_shape, F32)]
    out_specs.append(pl.BlockSpec((blk, HW), kmap))
    out_shape.append(jax.ShapeDtypeStruct((seq, HW), dk_dtype))
    scratch = [pltpu.VMEM((blk, HW), F32)]
    if mla:
        out_specs.append(pl.BlockSpec((blk, LANES), kmap))
        out_shape.append(jax.ShapeDtypeStruct((seq, LANES), F32))
        scratch.append(pltpu.VMEM((blk, LANES), F32))
    out_specs.append(pl.BlockSpec((blk, HW), kmap))
    out_shape.append(jax.ShapeDtypeStruct((seq, HW), BF16))
    scratch.append(pltpu.VMEM((blk, HW), F32))
    scratch += [pltpu.VMEM((2, 2 * blk, blk), F32), pltpu.VMEM((2, 2 * blk, blk), F32)]
    if single_visit:
        scratch.append(pltpu.VMEM((1, qw, blk), F32))
    body = _ride_along(body, ride, 5, len(args), len(out_shape), len(scratch), n_steps)
    if ride is not None:
        args, in_specs = args + ride.args, in_specs + ride.in_specs
        out_specs, out_shape, scratch = out_specs + ride.out_specs, out_shape + ride.out_shape, scratch + ride.scratch
    return _pcall(
        body, name=name,
        grid_spec=pltpu.PrefetchScalarGridSpec(
            num_scalar_prefetch=5, grid=(n_steps,), in_specs=in_specs, out_specs=out_specs,
            scratch_shapes=scratch),
        out_shape=out_shape,
        compiler_params=_cparams(dimension_semantics=("arbitrary",)),
    )(*steps, *args)


def _out_ln(oa, ob_near, ob_far, lse_near, lse_far, ga, gb, x, tgt, w_out, ln_g, ln_b, bt):
    seq = x.shape[0]

    def body(oa_ref, obn_ref, obf_ref, lsen_ref, lsef_ref, ga_ref, gb_ref, x_ref, tgt_ref, w_ref, g_ref, b_ref,
             dz_ref, doa_ref, dob_ref, dga_ref, dgb_ref, da_ref, db_ref, lse_ref, gwb_ref, small_ref, dobc_ref,
             gw_ref, lanes_sc):
        i = pl.program_id(0)

        @pl.when(i == 0)
        def _():
            gw_ref[...] = jnp.zeros(gw_ref.shape, F32)
            small_ref[...] = jnp.zeros(small_ref.shape, F32)

        def gate(g):
            sig = 0.5 * jnp.tanh(0.5 * g) + 0.5
            return g * sig, sig * (1.0 + g * (1.0 - sig))

        lse_n, lse_f = lsen_ref[...], lsef_ref[...]
        top = jnp.maximum(lse_n, lse_f)
        e_n, e_f = jnp.exp2(lse_n - top), jnp.exp2(lse_f - top)
        lse_ref[...] = top + jnp.log2(e_n + e_f)
        inv = 1.0 / (e_n + e_f)
        head_row = lax.broadcasted_iota(jnp.int32, (2 * HEADS, HW), 0) % HEADS
        spread = (head_row == lax.broadcasted_iota(jnp.int32, (2 * HEADS, HW), 1) // 64).astype(BF16)

        def per_lane(w):
            hi = w.astype(BF16)
            lo = (w - hi.astype(F32)).astype(BF16)
            return lax.dot_general(jnp.concatenate([hi, lo], axis=0), spread, TN, preferred_element_type=F32)

        o_b_all = per_lane(e_n * inv) * obn_ref[...] + per_lane(e_f * inv) * _in_sequence(obf_ref, lanes_sc)
        gam = g_ref[...]
        halves = [slice(0, bt // 2), slice(bt // 2, bt)]

        def gates_and_projection(rows):
            o_a, o_b = oa_ref[rows, :], o_b_all[rows]
            sa, dsa = gate(ga_ref[rows, :])
            sb, dsb = gate(gb_ref[rows, :])
            mix = jnp.concatenate([o_a * sa, o_b * sb], axis=1).astype(BF16)
            z = ALPHA * x_ref[rows, :] + jnp.dot(mix, w_ref[...], preferred_element_type=F32)
            return o_a, o_b, sa, dsa, sb, dsb, mix, z

        def norm_and_back(rows, mix, z):
            mu = jnp.mean(z, axis=1, keepdims=True)
            zc = z - mu
            rstd = lax.rsqrt(jnp.mean(zc * zc, axis=1, keepdims=True) + LN_EPS)
            xhat = zc * rstd
            diff = xhat * gam + b_ref[...] - tgt_ref[rows, :]
            dy = diff * (1.0 / D_MODEL)
            small_ref[0:1, :] += jnp.sum(dy * xhat, axis=0, keepdims=True)
            small_ref[1:2, :] += jnp.sum(dy, axis=0, keepdims=True)
            small_ref[2:3, :] += jnp.sum(diff * diff, axis=0, keepdims=True)
            dxh = dy * gam
            dz = rstd * (dxh - jnp.mean(dxh, axis=1, keepdims=True)
                         - xhat * jnp.mean(dxh * xhat, axis=1, keepdims=True))
            dz_ref[rows, :] = dz
            dzb = dz.astype(BF16)
            gw_ref[...] += lax.dot_general(mix, dzb, TN, preferred_element_type=F32)
            return lax.dot_general(dzb, w_ref[...], NT, preferred_element_type=F32)

        def gate_back(rows, o_a, o_b, sa, dsa, sb, dsb, dmix):
            doa, dob = dmix[:, :HW] * sa, dmix[:, HW:] * sb
            doa_ref[rows, :] = doa.astype(BF16)
            dob_ref[rows, :] = dob.astype(BF16)
            dga_ref[rows, :] = (dmix[:, :HW] * o_a * dsa).astype(BF16)
            dgb_ref[rows, :] = (dmix[:, HW:] * o_b * dsb).astype(BF16)
            return dob, doa * o_a, dob * o_b

        fronts = [gates_and_projection(rows) for rows in halves]
        dmixes = [norm_and_back(rows, f[6], f[7]) for rows, f in zip(halves, fronts)]
        backs = [gate_back(rows, *f[:6], dmix) for rows, f, dmix in zip(halves, fronts, dmixes)]
        dob, prod_a, prod_b = (jnp.concatenate(parts, axis=0) for parts in zip(*backs))
        _by_class(dob, dobc_ref, lanes_sc)

        @pl.when(i == seq // bt - 1)
        def _():
            gwb_ref[...] = gw_ref[...].astype(BF16)

        head_of = (lax.broadcasted_iota(jnp.int32, (2 * HW, LANES), 0) % HW) // 64
        ind = (head_of == lax.broadcasted_iota(jnp.int32, (2 * HW, LANES), 1)).astype(BF16)

        def head_sums(prod):
            hi = prod.astype(BF16)
            lo = (prod - hi.astype(F32)).astype(BF16)
            sums = jnp.dot(jnp.concatenate([hi, lo], axis=1), ind, preferred_element_type=F32)
            return sums.T[:HEADS, :]

        da_ref[...] = head_sums(prod_a)
        db_ref[...] = head_sums(prod_b)

    def tok(width):
        return pl.BlockSpec((bt, width), lambda i: (i, 0))

    def full(shape):
        return pl.BlockSpec(shape, lambda i: (0,) * len(shape))

    stat = pl.BlockSpec((HEADS, bt), lambda i: (0, i))
    n_cls = ob_far.shape[0]
    by_class = pl.BlockSpec((n_cls, bt // n_cls, HW), lambda i: (0, i, 0))
    return _pcall(
        body, name="out_ln", grid=(seq // bt,),
        in_specs=[tok(HW), tok(HW), by_class, stat, stat, tok(HW), tok(HW), tok(D_MODEL), tok(D_MODEL),
                  full((D_MODEL, D_MODEL)), full((1, D_MODEL)), full((1, D_MODEL))],
        out_specs=[tok(D_MODEL), tok(HW), tok(HW), tok(HW), tok(HW), stat, stat, stat,
                   full((D_MODEL, D_MODEL)), full((8, D_MODEL)), by_class],
        out_shape=[jax.ShapeDtypeStruct((seq, D_MODEL), F32)] + [jax.ShapeDtypeStruct((seq, HW), BF16)] * 4
        + [jax.ShapeDtypeStruct((HEADS, seq), F32)] * 3
        + [jax.ShapeDtypeStruct((D_MODEL, D_MODEL), BF16), jax.ShapeDtypeStruct((8, D_MODEL), F32),
           jax.ShapeDtypeStruct(ob_far.shape, BF16)],
        scratch_shapes=[pltpu.VMEM((D_MODEL, D_MODEL), F32), pltpu.VMEM((HW // LANES, bt, LANES), F32)],
        compiler_params=_cparams(dimension_semantics=("arbitrary",)),
    )(oa, ob_near, ob_far, lse_near, lse_far, ga, gb, x, tgt, w_out, ln_g, ln_b)


def _bwd_mid(dq_m, dkn, dv, dkpe, dqb, dkb, dvb, far, dga, dgb, cq, ckv, qn, kvn, w_uq_r, w_ukv_r, qg, kvg, tabs, bt):
    n_cls = far[0].shape[0]
    seq = cq.shape[0]

    def body(dqm_ref, dkn_ref, dv_ref, dkpe_ref, dqb_ref, dkb_ref, dvb_ref, dqf_ref, dkf_ref, dvf_ref, dga_ref, dgb_ref,
             cq_ref, ckv_ref, qn_ref, kvn_ref, wuq_ref, wukv_ref, qg_ref, kvg_ref, tab_ref,
             dh_ref, guq3_ref, gukv3_ref, small_ref, seq_sc, guq_ref, gukv_ref):
        i = pl.program_id(0)

        @pl.when(i == 0)
        def _():
            guq_ref[...] = jnp.zeros(guq_ref.shape, F32)
            gukv_ref[...] = jnp.zeros(gukv_ref.shape, F32)
            small_ref[...] = jnp.zeros(small_ref.shape, F32)

        m_tabs = (tab_ref[0], tab_ref[1], tab_ref[2])
        d_tabs = (tab_ref[3], tab_ref[4], tab_ref[5])

        def rms_bwd(c, dn, gain):
            r = lax.rsqrt(jnp.mean(c * c, axis=1, keepdims=True) + RMS_EPS)
            u = dn * gain
            dc = r * u - c * (r * r * r) * jnp.mean(u * c, axis=1, keepdims=True)
            return dc, jnp.sum(dn * c * r, axis=0, keepdims=True)

        dqm = dqm_ref[0].T
        dq = jnp.concatenate(
            [dqm[:, :HW], _rope_wide(_rope_t, dqm[:, HW:], *m_tabs, MLA_ROPE // 2)], axis=1) * MLA_SCALE
        dq = dq.astype(BF16)
        dkv = jnp.concatenate([dkn_ref[...], dv_ref[...]], axis=1)
        guq_ref[...] += lax.dot_general(qn_ref[...], dq, TN, preferred_element_type=F32)
        dqn = lax.dot_general(dq, wuq_ref[...], NT, preferred_element_type=F32)
        gukv_ref[...] += lax.dot_general(kvn_ref[...], dkv, TN, preferred_element_type=F32)
        dkvn = lax.dot_general(dkv, wukv_ref[...], NT, preferred_element_type=F32)

        dh_ref[:, C_KR:C_GA] = _rope_t(dkpe_ref[...], *m_tabs, MLA_ROPE // 2).astype(BF16)
        dh_ref[:, C_GA:C_QB] = dga_ref[...]
        in_sequence = functools.partial(_in_sequence, lanes_sc=seq_sc)
        dqb = dqb_ref[0].T + in_sequence(dqf_ref)
        dh_ref[:, C_QB:C_KB] = (_rope_wide(_rope_t, dqb, *d_tabs, DIL_ROT // 2) * DIL_SCALE).astype(BF16)
        dkb = dkb_ref[...] + in_sequence(dkf_ref)
        dh_ref[:, C_KB:C_VB] = _rope_wide(_rope_t, dkb, *d_tabs, DIL_ROT // 2).astype(BF16)
        dh_ref[:, C_VB:C_GB] = (dvb_ref[...].astype(F32) + in_sequence(dvf_ref)).astype(BF16)
        dh_ref[:, C_GB:C_END] = dgb_ref[...]

        dcq, gq = rms_bwd(cq_ref[...], dqn, qg_ref[...])
        small_ref[0:1, :] += gq
        dckv, gkv = rms_bwd(ckv_ref[...], dkvn, kvg_ref[...])
        small_ref[1:2, :KV_RANK] += gkv
        dh_ref[:, C_CQ:C_CKV] = dcq.astype(BF16)
        dh_ref[:, C_CKV:C_KR] = dckv.astype(BF16)

        @pl.when(i == seq // bt - 1)
        def _():
            for h in range(HEADS):
                guq3_ref[h] = jnp.concatenate(
                    [guq_ref[:, MLA_NOPE * h:MLA_NOPE * (h + 1)],
                     guq_ref[:, HW + MLA_ROPE * h:HW + MLA_ROPE * (h + 1)]], axis=1).astype(BF16)
                gukv3_ref[h] = jnp.concatenate(
                    [gukv_ref[:, MLA_NOPE * h:MLA_NOPE * (h + 1)],
                     gukv_ref[:, HW + MLA_V * h:HW + MLA_V * (h + 1)]], axis=1).astype(BF16)

    def tok(width):
        return pl.BlockSpec((bt, width), lambda i: (i, 0))

    def tok_t(a):
        per = a.shape[2] // bt
        return pl.BlockSpec((1, a.shape[1], bt), lambda i: (i // per, 0, i % per))

    def full(shape):
        return pl.BlockSpec(shape, lambda i: (0,) * len(shape))

    by_class = pl.BlockSpec((n_cls, bt // n_cls, HW), lambda i: (0, i, 0))
    uq3 = (HEADS, Q_RANK, MLA_NOPE + MLA_ROPE)
    ukv3 = (HEADS, KV_RANK, MLA_NOPE + MLA_V)
    return _pcall(
        body, name="bwd_mid", grid=(seq // bt,),
        in_specs=[tok_t(dq_m), tok(HW), tok(HW), tok(LANES), tok_t(dqb), tok(HW), tok(HW), by_class, by_class, by_class,
                  tok(HW), tok(HW),
                  tok(Q_RANK), tok(KV_RANK), tok(Q_RANK), tok(KV_RANK),
                  full(w_uq_r.shape), full(w_ukv_r.shape), full((1, Q_RANK)), full((1, KV_RANK)),
                  pl.BlockSpec((6, bt, LANES), lambda i: (0, i, 0))],
        out_specs=[tok(C_END), full(uq3), full(ukv3), full((8, Q_RANK))],
        out_shape=[jax.ShapeDtypeStruct((seq, C_END), BF16), jax.ShapeDtypeStruct(uq3, BF16),
                   jax.ShapeDtypeStruct(ukv3, BF16), jax.ShapeDtypeStruct((8, Q_RANK), F32)],
        scratch_shapes=[pltpu.VMEM((HW // LANES, bt, LANES), F32), pltpu.VMEM(w_uq_r.shape, F32),
                        pltpu.VMEM(w_ukv_r.shape, F32)],
        compiler_params=_cparams(dimension_semantics=("arbitrary",)),
    )(dq_m, dkn, dv, dkpe, dqb, dkb, dvb, *far, dga, dgb, cq, ckv, qn, kvn, w_uq_r, w_ukv_r, qg, kvg, tabs)


def _grad_x(dz, dh, w_in_r, bt, ride=None):
    seq = dz.shape[0]
    n_steps = seq // bt

    def body(dz_ref, dh_ref, w_ref, gx_ref):
        gx_ref[...] = ALPHA * dz_ref[...] + lax.dot_general(
            dh_ref[...], w_ref[...], NT, preferred_element_type=F32)

    args = [dz, dh, w_in_r]
    in_specs = [pl.BlockSpec((bt, D_MODEL), lambda i: (i, 0)), pl.BlockSpec((bt, C_END), lambda i: (i, 0)),
                pl.BlockSpec(w_in_r.shape, lambda i: (0, 0))]
    out_specs = [pl.BlockSpec((bt, D_MODEL), lambda i: (i, 0))]
    out_shape = [jax.ShapeDtypeStruct((seq, D_MODEL), F32)]
    scratch = []
    body = _ride_along(body, ride, 0, len(args), len(out_shape), 0, n_steps)
    if ride is not None:
        args, in_specs = args + ride.args, in_specs + ride.in_specs
        out_specs, out_shape, scratch = out_specs + ride.out_specs, out_shape + ride.out_shape, ride.scratch
    return _pcall(
        body, name="grad_x", grid=(n_steps,),
        in_specs=in_specs, out_specs=out_specs, out_shape=out_shape, scratch_shapes=scratch,
        compiler_params=_cparams(dimension_semantics=("arbitrary",)),
    )(*args)


def _grad_w_in(x, dh, bt, cut, ride):
    seq = x.shape[0]
    n_tok = seq // bt
    shard = IN_WIDTH // N_DEV
    k_lo, k_hi = IN_SPLITS[0] + IN_SPLITS[1], IN_SPLITS[0] + IN_SPLITS[1] + MLA_ROPE

    def body(x_ref, dh_ref, rest_ref, acc, first_ref):
        i = pl.program_id(0)
        for part, (r_lo, r_hi, dst_ref) in enumerate([(0, cut, first_ref), (cut, D_MODEL, rest_ref)]):
            rows = slice(0, r_hi - r_lo)

            @pl.when(i == part * n_tok)
            def _():
                acc[rows, :] = jnp.zeros((r_hi - r_lo, C_END), F32)

            @pl.when(i // n_tok == part)
            def _():
                acc[rows, :] += lax.dot_general(
                    x_ref[:, r_lo:r_hi].astype(BF16), dh_ref[...], TN, preferred_element_type=F32)

            @pl.when(i == (part + 1) * n_tok - 1)
            def _():
                kr = acc[rows, C_KR:C_GA]
                kr = kr + pltpu.roll(kr, 96, 1) + pltpu.roll(kr, 64, 1) + pltpu.roll(kr, 32, 1)
                for d in range(N_DEV):
                    lo, hi = shard * d, shard * (d + 1)
                    pieces = []
                    if lo < k_lo:
                        pieces.append(acc[rows, lo:min(hi, k_lo)])
                    if lo < k_hi and hi > k_lo:
                        pieces.append(kr[:, max(lo, k_lo) - k_lo:min(hi, k_hi) - k_lo])
                    if hi > k_hi:
                        shift = C_GA - k_hi
                        pieces.append(acc[rows, max(lo, k_hi) + shift:hi + shift])
                    blk = pieces[0] if len(pieces) == 1 else jnp.concatenate(pieces, axis=1)
                    dst_ref[d] = blk.astype(BF16)

    args = [x, dh]
    in_specs = [pl.BlockSpec((bt, D_MODEL), lambda i: (i % n_tok, 0)),
                pl.BlockSpec((bt, C_END), lambda i: (i % n_tok, 0))]
    out_specs = [pl.BlockSpec((N_DEV, D_MODEL - cut, shard), lambda i: (0, 0, 0))]
    out_shape = [jax.ShapeDtypeStruct((N_DEV, D_MODEL - cut, shard), BF16)]
    scratch = [pltpu.VMEM((max(cut, D_MODEL - cut), C_END), F32)]
    assert ride.shared == 1
    body = _ride_along(body, ride, 0, len(args), len(out_shape), len(scratch), 2 * n_tok)
    return _pcall(
        body, name="grad_w_in", grid=(2 * n_tok,),
        in_specs=in_specs + ride.in_specs, out_specs=out_specs + ride.out_specs,
        out_shape=out_shape + ride.out_shape, scratch_shapes=scratch + ride.scratch,
        compiler_params=_cparams(dimension_semantics=("arbitrary",)),
    )(*args, *ride.args)


def _local_step(x, tgt, w_in_r, w_uq_r, w_ukv_r, w_out_rider, g_out_rider, reduce_rider, q_norm_g, kv_norm_g,
                ln_g, ln_b, bt=BLOCK_TOKENS, blk_m=BLOCK_MLA, blk_d=BLOCK_DIL):
    seq = x.shape[0]
    tabs = jnp.asarray(_rope_tables(seq))
    qg, kvg = q_norm_g.reshape(1, -1), kv_norm_g.reshape(1, -1)

    far_dil = DIL_CONFIGS[-1][1]
    cls = seq // far_dil
    (cq, ckv, qn, kvn, qcat, kn, kpe, v, ga, gb, qb, kb, vb, knt, kpet, vt, kbt, vbt, qb_c, kb_c, vb_c) = _fwd_proj(
        x, w_in_r, w_uq_r, w_ukv_r, qg, kvg, tabs, bt, far_dil)
    qb_c, kb_c, vb_c = (a.reshape(seq, HW) for a in (qb_c, kb_c, vb_c))

    nq_m, nq_d = seq // blk_m, seq // blk_d
    bias_m = _mla_bias_t(blk_m)
    oa, lse_a, w_out = _attn_fwd(
        "mla_fwd", qcat, kn, kpe, vt, bias_m, _steps(nq_m, nq_m, False, True), blk_m, ride=w_out_rider)

    bias_near = _dil_bias_t(blk_d, DIL_NEAR)
    ob_near, lse_near = _attn_fwd(
        "dil_fwd", qb, kb, None, vbt, bias_near, _steps(nq_d, -(-DIL_NEAR // blk_d), False, False), blk_d)
    each = np.arange(far_dil, dtype=np.int32)
    steps_far = [jnp.asarray(v) for v in (each, each, np.zeros_like(each), np.ones_like(each), np.ones_like(each))]
    bias_far = _dil_far_bias_t(cls)
    ob_far, lse_far = _attn_fwd(
        "dil_far_fwd", qb_c, kb_c, None, vb_c, bias_far, steps_far, cls, v_token_major=True)

    dz, doa, dob, dga, dgb, dst_a, dst_b, lse_b, g_out, small1, dob_c = _out_ln(
        oa, ob_near, ob_far.reshape(far_dil, cls, HW), lse_near, _lanes_from_classes(lse_far, far_dil), ga, gb, x, tgt,
        w_out.reshape(D_MODEL, D_MODEL), ln_g.reshape(1, -1), ln_b.reshape(1, -1), bt)

    dq_m, dkn, dkpe, dv, g_out_recv = _attn_bwd(
        "mla_bwd", qcat, kn, kpe, v, knt, kpet, bias_m, doa, lse_a, dst_a, _steps(nq_m, nq_m, True, True), blk_m,
        ride=g_out_rider(g_out.reshape(N_DEV, D_MODEL // N_DEV, D_MODEL)))
    dqb, dkb_near, dvb_near = _attn_bwd(
        "dil_bwd", qb, kb, None, vb, kbt, None, bias_near, dob, lse_b, dst_b,
        _steps(nq_d, -(-DIL_NEAR // blk_d), True, False), blk_d)
    dqb_far, dkb_far, dvb_far = _attn_bwd(
        "dil_far_bwd", qb_c, kb_c, None, vb_c, None, None, bias_far, dob_c.reshape(seq, HW),
        _lanes_to_classes(lse_b, far_dil), _lanes_to_classes(dst_b, far_dil), steps_far, cls, single_visit=True)
    far = [a.reshape(far_dil, cls, HW) for a in (dqb_far, dkb_far, dvb_far)]

    dh, g_uq, g_ukv, small2 = _bwd_mid(
        dq_m, dkn, dv, dkpe, dqb, dkb_near, dvb_near, far, dga, dgb, cq, ckv, qn, kvn, w_uq_r, w_ukv_r, qg, kvg, tabs, bt)
    bt_w = min(seq, 2 * bt)
    n_tok = seq // bt_w
    first = jax.ShapeDtypeStruct((N_DEV, GRAD_W_IN_CUT, IN_WIDTH // N_DEV), BF16)
    g_in_rest, *reduced_first = _grad_w_in(
        x, dh, bt_w, GRAD_W_IN_CUT,
        reduce_rider([first, g_uq, g_ukv], g_out_recv, (small1, small2), (n_tok, n_tok, 2 * n_tok - 1)))
    grad_x, g_in_rest = _grad_x(dz, dh, w_in_r, bt, ride=reduce_rider([g_in_rest], None, None, (0, 1, seq // bt - 2)))
    return grad_x, [g_in_rest] + reduced_first


MESH_ID = pl.DeviceIdType.MESH
SHARD_SHAPES = ((D_MODEL, IN_WIDTH // N_DEV), (Q_RANK, 768 // N_DEV), (KV_RANK, 1024 // N_DEV), (D_MODEL // N_DEV, D_MODEL))
ADAM_ROWS = (32, 128, 128, 16)


def _me():
    x, y, c = lax.axis_index("x"), lax.axis_index("y"), lax.axis_index("c")
    return x, y, c, 4 * x + 2 * y + c


def _peer(k):
    x, y, c, _ = _me()
    px = 1 - x if (k >> 2) & 1 else x
    py = 1 - y if (k >> 1) & 1 else y
    pc = 1 - c if k & 1 else c
    return (px, py, pc), 4 * px + 2 * py + pc


def _all_gather_weights(shards):
    n = len(shards)
    shard = IN_WIDTH // N_DEV
    k_lo = IN_SPLITS[0] + IN_SPLITS[1]
    k_hi = k_lo + MLA_ROPE

    def body(*refs):
        ins = refs[:n]
        win_ref, wuq_ref, wukv_ref = refs[n:2 * n]
        bufs = refs[2 * n:3 * n]
        send_sems, recv_sems = refs[3 * n:]
        x, y, c, me = _me()
        here, sibling = (x, y, c), (x, y, 1 - c)
        along_x, along_y, across = (1 - x, y), (x, 1 - y), (1 - x, 1 - y)
        for t in range(n):
            bufs[t][me] = ins[t][...].astype(BF16)

        def copy(t, k, chip, pc, to, half=None):
            blk = bufs[t].at[4 * chip[0] + 2 * chip[1] + pc]
            if half is not None:
                rows = SHARD_SHAPES[t][0] // 2
                blk = blk.at[pl.ds(half * rows, rows), :]
            return pltpu.make_async_remote_copy(
                src_ref=blk, dst_ref=blk, send_sem=send_sems.at[t, k], recv_sem=recv_sems.at[t, k],
                device_id=to, device_id_type=MESH_ID)

        sends = []
        for t in range(n):
            sends += [copy(t, 0, (x, y), c, sibling), copy(t, 1, (x, y), c, (*along_x, c)),
                      copy(t, 2, (x, y), c, (*along_y, c))]
        for cp in sends:
            cp.start()
        for t in range(n):
            copy(t, 1, along_x, c, here).wait_recv()
            sends += [copy(t, 3, along_x, c, (*along_y, c), half=0), copy(t, 5, along_x, c, sibling)]
            sends[-2].start()
            sends[-1].start()
        for t in range(n):
            copy(t, 2, along_y, c, here).wait_recv()
            sends += [copy(t, 4, along_y, c, (*along_x, c), half=1), copy(t, 6, along_y, c, sibling)]
            sends[-2].start()
            sends[-1].start()
        for t in range(n):
            copy(t, 3, across, c, here, half=0).wait_recv()
            copy(t, 4, across, c, here, half=1).wait_recv()
            sends.append(copy(t, 7, across, c, sibling))
            sends[-1].start()
        for t in range(n):
            copy(t, 0, (x, y), 1 - c, here).wait_recv()
            for k, chip in ((5, along_x), (6, along_y), (7, across)):
                copy(t, k, chip, 1 - c, here).wait_recv()
        for cp in sends:
            cp.wait_send()

        a_in, a_uq, a_ukv = bufs
        for d in range(N_DEV):
            lo, hi = shard * d, shard * (d + 1)
            if lo < k_lo:
                win_ref[:, lo:min(hi, k_lo)] = a_in[d, :, 0:min(hi, k_lo) - lo]
            if lo < k_hi and hi > k_lo:
                kr = a_in[d, :, k_lo - lo:k_hi - lo]
                for rep in range(4):
                    win_ref[:, C_KR + MLA_ROPE * rep:C_KR + MLA_ROPE * (rep + 1)] = kr
            if hi > k_hi:
                src = max(lo, k_hi)
                win_ref[:, src + C_GA - k_hi:hi + C_GA - k_hi] = a_in[d, :, src - lo:hi - lo]
        for h in range(HEADS):
            wuq_ref[:, MLA_NOPE * h:MLA_NOPE * (h + 1)] = a_uq[h, :, :MLA_NOPE]
            wuq_ref[:, HW + MLA_ROPE * h:HW + MLA_ROPE * (h + 1)] = a_uq[h, :, MLA_NOPE:]
            wukv_ref[:, MLA_NOPE * h:MLA_NOPE * (h + 1)] = a_ukv[h, :, :MLA_NOPE]
            wukv_ref[:, HW + MLA_V * h:HW + MLA_V * (h + 1)] = a_ukv[h, :, MLA_NOPE:]

    vmem = pl.BlockSpec(memory_space=pltpu.VMEM)
    return _pcall(
        body, name="gather_weights",
        in_specs=[vmem] * n, out_specs=[vmem] * n,
        out_shape=[jax.ShapeDtypeStruct((D_MODEL, C_END), BF16), jax.ShapeDtypeStruct((Q_RANK, QW), BF16),
                   jax.ShapeDtypeStruct((KV_RANK, 2 * HW), BF16)],
        scratch_shapes=[pltpu.VMEM((N_DEV,) + s, BF16) for s in SHARD_SHAPES[:n]]
        + [pltpu.SemaphoreType.DMA((n, 8)), pltpu.SemaphoreType.DMA((n, 8))],
        compiler_params=_cparams(),
    )(*shards)


def _gather_w_out_rider(w_out):
    def copies(full_ref, stage, send_sems, recv_sems):
        me = _me()[3]
        out = []
        for k in range(1, N_DEV):
            peer, pidx = _peer(k)
            send = pltpu.make_async_remote_copy(
                src_ref=stage, dst_ref=full_ref.at[me], send_sem=send_sems.at[k - 1], recv_sem=recv_sems.at[k - 1],
                device_id=peer, device_id_type=MESH_ID)
            recv = pltpu.make_async_remote_copy(
                src_ref=stage, dst_ref=full_ref.at[pidx], send_sem=send_sems.at[k - 1], recv_sem=recv_sems.at[k - 1],
                device_id=peer, device_id_type=MESH_ID)
            out.append((send, recv))
        return out

    def start(ins, outs, scr):
        stage, send_sems, recv_sems, own_sem = scr
        stage[...] = ins[0][...].astype(BF16)
        pltpu.make_async_copy(stage, outs[0].at[_me()[3]], own_sem).start()
        for send, _ in copies(outs[0], stage, send_sems, recv_sems):
            send.start()

    def finish(ins, outs, scr):
        stage, send_sems, recv_sems, own_sem = scr
        pltpu.make_async_copy(stage, outs[0].at[_me()[3]], own_sem).wait()
        pairs = copies(outs[0], stage, send_sems, recv_sems)
        for _, recv in pairs:
            recv.wait_recv()
        for send, _ in pairs:
            send.wait_send()

    shape = SHARD_SHAPES[3]
    return Rider(
        args=[w_out], in_specs=[pl.BlockSpec(shape, lambda t, *_: (0, 0))],
        out_shape=[jax.ShapeDtypeStruct((N_DEV,) + shape, BF16)], out_specs=[pl.BlockSpec(memory_space=pl.ANY)],
        scratch=[pltpu.VMEM(shape, BF16), pltpu.SemaphoreType.DMA((N_DEV - 1,)), pltpu.SemaphoreType.DMA((N_DEV - 1,)),
                 pltpu.SemaphoreType.DMA],
        start=start, finish=finish)


def _scatter_g_out_rider(blocks):
    def copies(src_ref, dst_ref, send_sems, recv_sems):
        out = []
        for k in range(1, N_DEV):
            peer, pidx = _peer(k)
            out.append(pltpu.make_async_remote_copy(
                src_ref=src_ref.at[pidx], dst_ref=dst_ref.at[k], send_sem=send_sems.at[k - 1],
                recv_sem=recv_sems.at[k - 1], device_id=peer, device_id_type=MESH_ID))
        return out

    def start(ins, outs, scr):
        send_sems, recv_sems, own_sem = scr
        pltpu.make_async_copy(ins[0].at[_me()[3]], outs[0].at[0], own_sem).start()
        for cp in copies(ins[0], outs[0], send_sems, recv_sems):
            cp.start()

    def finish(ins, outs, scr):
        send_sems, recv_sems, own_sem = scr
        pltpu.make_async_copy(ins[0].at[_me()[3]], outs[0].at[0], own_sem).wait()
        for cp in copies(ins[0], outs[0], send_sems, recv_sems):
            cp.wait()

    hbm = pl.BlockSpec(memory_space=pl.ANY)
    return Rider(
        args=[blocks], in_specs=[hbm], out_shape=[jax.ShapeDtypeStruct(blocks.shape, blocks.dtype)], out_specs=[hbm],
        scratch=[pltpu.SemaphoreType.DMA((N_DEV - 1,)), pltpu.SemaphoreType.DMA((N_DEV - 1,)), pltpu.SemaphoreType.DMA],
        start=start, finish=finish)


def _adamw(w, g, m, v):
    m = ADAM_B1 * m + (1.0 - ADAM_B1) * g
    v = ADAM_B2 * v + (1.0 - ADAM_B2) * jnp.square(g)
    m_hat = m / (1.0 - ADAM_B1 ** ADAM_STEP)
    v_hat = v / (1.0 - ADAM_B2 ** ADAM_STEP)
    delta = -ADAM_LR * (m_hat / (jnp.sqrt(v_hat) + ADAM_EPS) + ADAM_WD * w)
    return delta, m, v


def _chunk_rows(count, cols):
    rows = (16 * 8 * LANES) // (-(-cols // LANES) * LANES)
    while count % rows:
        rows //= 2
    return rows


SMALL_ROWS = (0, 1, 3, 4)
LOSS_ROW = 2


def _reduce_grads_rider(grads3, arrived, small_parts, steps):
    n = len(grads3)
    shapes = [tuple(g.shape[1:]) for g in grads3]
    with_small = arrived is not None
    start_step, first_round, second_round = steps
    n_held = sum(isinstance(g, jax.ShapeDtypeStruct) for g in grads3)
    held, handed = grads3[:n_held], grads3[n_held:]
    assert not any(isinstance(g, jax.ShapeDtypeStruct) for g in handed)

    class Refs:
        def __init__(self, ins, outs, scr):
            self.g3, self.gsum = list(scr[0:n_held]) + list(ins[0:n - n_held]), outs[0:n]
            ins, scr = ins[n - n_held:], scr[n_held:]
            if with_small:
                self.arr, self.sp_wide, self.sp_q = ins
                self.gsum_out, self.ssum = outs[n:n + 2]
            self.own, self.sib, self.part = scr[0:n], scr[n:2 * n], scr[2 * n:3 * n]
            self.in_a, self.out_b, self.in_b = scr[3 * n:4 * n], scr[4 * n:5 * n], scr[5 * n:6 * n]
            self.rsmall = scr[6 * n]
            (self.loc_sems, self.d2d_send, self.d2d_recv, self.a_send, self.a_recv, self.b_send, self.b_recv,
             self.sm_send, self.sm_recv) = scr[6 * n + 1:]
            self.x, self.y, self.c, self.me = _me()
            self.along_x, self.along_y = (1 - self.x, self.y), (self.x, 1 - self.y)
            self.across = (1 - self.x, 1 - self.y)

        def small(self):
            if not with_small:
                return []
            return [pltpu.make_async_remote_copy(
                src_ref=self.rsmall.at[0], dst_ref=self.rsmall.at[k], send_sem=self.sm_send.at[k - 1],
                recv_sem=self.sm_recv.at[k - 1], device_id=_peer(k)[0], device_id_type=MESH_ID)
                for k in range(1, N_DEV)]

        def level1(self):
            local, to_sib = [], []
            for t in range(n):
                for q in range(4):
                    local.append(pltpu.make_async_copy(
                        self.g3[t].at[2 * q + self.c], self.own[t].at[q], self.loc_sems.at[t, q]))
                    to_sib.append(pltpu.make_async_remote_copy(
                        src_ref=self.g3[t].at[2 * q + 1 - self.c], dst_ref=self.sib[t].at[q],
                        send_sem=self.d2d_send.at[t, q], recv_sem=self.d2d_recv.at[t, q],
                        device_id=(self.x, self.y, 1 - self.c), device_id_type=MESH_ID))
            return local, to_sib

        def round_a(self):
            out = []
            for t in range(n):
                half = shapes[t][0] // 2
                for k, (to, chip, h) in enumerate([(self.along_x, self.along_x, 0), (self.along_x, self.across, 0),
                                                   (self.along_y, self.along_y, 1), (self.along_y, self.across, 1)]):
                    out.append(pltpu.make_async_remote_copy(
                        src_ref=self.part[t].at[2 * chip[0] + chip[1], pl.ds(h * half, half), :],
                        dst_ref=self.in_a[t].at[k], send_sem=self.a_send.at[t, k], recv_sem=self.a_recv.at[t, k],
                        device_id=(*to, self.c), device_id_type=MESH_ID))
            return out

        def round_b(self):
            return [pltpu.make_async_remote_copy(
                src_ref=self.out_b[t].at[k], dst_ref=self.in_b[t].at[k], send_sem=self.b_send.at[t, k],
                recv_sem=self.b_recv.at[t, k], device_id=(*to, self.c), device_id_type=MESH_ID)
                for t in range(n) for k, to in enumerate([self.along_y, self.along_x])]

    def chunks(rows, count, fn):
        def step(i, carry):
            fn(pl.multiple_of(i * rows, rows))
            return carry

        lax.fori_loop(0, count // rows, step, 0)

    def start(*refs):
        r = Refs(*refs)
        if with_small:
            r.rsmall[0] = r.sp_wide[...]
            for k, row in enumerate(SMALL_ROWS[2:]):
                r.rsmall[0, row:row + 1, 0:Q_RANK] = r.sp_q[k:k + 1, :]
        local, to_sib = r.level1()
        for cp in r.small() + local + to_sib:
            cp.start()

    def begin_rounds(*refs):
        r = Refs(*refs)
        local, to_sib = r.level1()
        for cp in local:
            cp.wait()
        for cp in to_sib:
            cp.wait_recv()
        my_chip = 2 * r.x + r.y
        for t in range(n):
            rows = _chunk_rows(*shapes[t])

            def pair_sums(at, t=t, rows=rows):
                sl = pl.ds(at, rows)
                for q in range(4):
                    r.part[t][q, sl, :] = (r.own[t][q, sl, :].astype(F32) + r.sib[t][q, sl, :].astype(F32)).astype(BF16)
                r.gsum[t][sl, :] = r.own[t][my_chip, sl, :].astype(F32) + r.sib[t][my_chip, sl, :].astype(F32)

            chunks(rows, shapes[t][0], pair_sums)
        for cp in r.round_a():
            cp.start()
        if not with_small:
            return
        rows_out = _chunk_rows(*SHARD_SHAPES[3])

        def add_arrived(at):
            sl = pl.ds(at, rows_out)
            g = r.arr[0, sl, :].astype(F32)
            for k in range(1, N_DEV):
                g = g + r.arr[k, sl, :].astype(F32)
            r.gsum_out[sl, :] = g

        chunks(rows_out, SHARD_SHAPES[3][0], add_arrived)

    def pass_on(*refs):
        r = Refs(*refs)
        for cp in r.round_a():
            cp.wait_recv()
        q_x, q_y = 2 * r.along_x[0] + r.along_x[1], 2 * r.along_y[0] + r.along_y[1]
        for t in range(n):
            half = shapes[t][0] // 2
            rows = _chunk_rows(half, shapes[t][1])

            def add(at, t=t, rows=rows, half=half):
                lo, hi = pl.ds(at, rows), pl.ds(half + at, rows)
                r.gsum[t][lo, :] = r.gsum[t][lo, :] + r.in_a[t][0, lo, :].astype(F32)
                r.out_b[t][0, lo, :] = (r.part[t][q_y, lo, :].astype(F32) + r.in_a[t][1, lo, :].astype(F32)).astype(BF16)
                r.gsum[t][hi, :] = r.gsum[t][hi, :] + r.in_a[t][2, lo, :].astype(F32)
                r.out_b[t][1, lo, :] = (r.part[t][q_x, hi, :].astype(F32) + r.in_a[t][3, lo, :].astype(F32)).astype(BF16)

            chunks(rows, half, add)
        for cp in r.round_b():
            cp.start()

    def finish(*refs):
        r = Refs(*refs)
        passed = r.round_b()
        for cp in passed:
            cp.wait_recv()
        for t in range(n):
            half = shapes[t][0] // 2
            rows = _chunk_rows(half, shapes[t][1])

            def add(at, t=t, rows=rows, half=half):
                lo, hi = pl.ds(at, rows), pl.ds(half + at, rows)
                r.gsum[t][lo, :] = r.gsum[t][lo, :] + r.in_b[t][0, lo, :].astype(F32)
                r.gsum[t][hi, :] = r.gsum[t][hi, :] + r.in_b[t][1, lo, :].astype(F32)

            chunks(rows, half, add)
        small = r.small()
        for cp in small:
            cp.wait_recv()
        if with_small:
            tot = r.rsmall[r.me]
            for d in range(1, N_DEV):
                tot = tot + r.rsmall[jnp.bitwise_xor(r.me, d)]
            r.ssum[...] = tot
        for cp in small + r.level1()[1] + r.round_a() + passed:
            cp.wait_send()

    hbm = pl.BlockSpec(memory_space=pl.ANY)
    dma = pltpu.SemaphoreType.DMA

    def whole(shape):
        return pl.BlockSpec(shape, lambda i: (0,) * len(shape))

    def halves(slots):
        return [pltpu.VMEM((slots, s[0] // 2, s[1]), BF16) for s in shapes]

    small_args = [arrived, *small_parts] if with_small else []
    out_shapes = shapes + ([SHARD_SHAPES[3], (8, D_MODEL)] if with_small else [])
    return Rider(
        args=list(handed) + small_args,
        in_specs=[hbm] * len(handed) + [whole(a.shape) for a in small_args],
        out_shape=[jax.ShapeDtypeStruct(s, F32) for s in out_shapes], out_specs=[whole(s) for s in out_shapes],
        scratch=[pltpu.VMEM(g.shape, g.dtype) for g in held]
        + [pltpu.VMEM((4,) + s, BF16) for _ in range(3) for s in shapes]
        + halves(4) + halves(2) + halves(2)
        + [pltpu.VMEM((N_DEV, 8, D_MODEL), F32), dma((n, 4)), dma((n, 4)), dma((n, 4)), dma((n, 4)), dma((n, 4)),
           dma((n, 2)), dma((n, 2)), dma((N_DEV - 1,)), dma((N_DEV - 1,))],
        start=start, finish=finish, stages=((first_round, begin_rounds), (second_round, pass_on)),
        start_step=start_step, shared=n_held)


def _adamw_update(grads, small_grad, wmv, small_wmv):
    n_small = len(small_wmv)
    flat_grads = [g for pieces in grads for g in pieces]
    n_g = len(flat_grads)

    def body(*refs):
        g_refs, sg_ref, refs = iter(refs[0:n_g]), refs[n_g], refs[n_g + 1:]
        wmv_refs = [refs[3 * t:3 * t + 3] for t in range(4)]
        swmv_refs = [refs[12 + 3 * t:15 + 3 * t] for t in range(n_small)]
        outs = refs[12 + 3 * n_small:]
        out_refs = [outs[4 * t:4 * t + 4] for t in range(4)]
        sout_refs = [outs[16 + 4 * t:20 + 4 * t] for t in range(n_small)]
        loss_ref = outs[16 + 4 * n_small]
        for t, (w_ref, m_ref, v_ref) in enumerate(swmv_refs):
            g = sg_ref[SMALL_ROWS[t]:SMALL_ROWS[t] + 1, :w_ref.shape[1]]
            delta, m, v = _adamw(w_ref[...], g, m_ref[...], v_ref[...])
            sout_refs[t][0][...], sout_refs[t][1][...], sout_refs[t][2][...], sout_refs[t][3][...] = g, delta, m, v
        loss_ref[...] = (0.5 / D_MODEL) * jnp.sum(sg_ref[LOSS_ROW:LOSS_ROW + 1, :], axis=1, keepdims=True)
        for t in range(4):
            rows = ADAM_ROWS[t]
            w_ref, m_ref, v_ref = wmv_refs[t]
            g_out, d_out, m_out, v_out = out_refs[t]

            def step(i, carry, g_ref, first, rows=rows, w_ref=w_ref, m_ref=m_ref, v_ref=v_ref,
                     g_out=g_out, d_out=d_out, m_out=m_out, v_out=v_out):
                at = pl.multiple_of(i * rows, rows)
                r = pl.ds(pl.multiple_of(first + at, rows), rows)
                g = g_ref[pl.ds(at, rows), :]
                delta, m, v = _adamw(w_ref[r, :], g, m_ref[r, :], v_ref[r, :])
                g_out[r, :], d_out[r, :], m_out[r, :], v_out[r, :] = g, delta, m, v
                return carry

            first = 0
            for piece in grads[t]:
                lax.fori_loop(0, piece.shape[0] // rows, functools.partial(step, g_ref=next(g_refs), first=first), 0)
                first += piece.shape[0]

    vmem = pl.BlockSpec(memory_space=pltpu.VMEM)
    flat_wmv = [a for trio in wmv for a in trio]
    flat_small = [a for trio in small_wmv for a in trio]
    out_shape = ([jax.ShapeDtypeStruct(s, F32) for s in SHARD_SHAPES for _ in range(4)]
                 + [jax.ShapeDtypeStruct(trio[0].shape, F32) for trio in small_wmv for _ in range(4)]
                 + [jax.ShapeDtypeStruct((1, 1), F32)])
    return _pcall(
        body, name="adamw",
        in_specs=[vmem] * (n_g + 1 + len(flat_wmv) + len(flat_small)), out_specs=[vmem] * len(out_shape),
        out_shape=out_shape,
        compiler_params=_cparams(),
    )(*flat_grads, small_grad, *flat_wmv, *flat_small)


def kernel(x, w_in, q_norm_g, kv_norm_g, w_uq, w_ukv, w_out, ln_g, ln_b, loss_target, m_w_in, m_q_norm_g, m_kv_norm_g, m_w_uq, m_w_ukv, m_w_out, m_ln_g, m_ln_b, v_w_in, v_q_norm_g, v_kv_norm_g, v_w_uq, v_w_ukv, v_w_out, v_ln_g, v_ln_b):
    w_in_r, w_uq_r, w_ukv_r = _all_gather_weights([w_in, w_uq, w_ukv])
    grad_x, (g_in_rest, g_in_first, g_uq, g_ukv, g_out, g_small) = _local_step(
        x[0], loss_target[0], w_in_r, w_uq_r, w_ukv_r, _gather_w_out_rider(w_out), _scatter_g_out_rider,
        _reduce_grads_rider, q_norm_g, kv_norm_g, ln_g, ln_b)
    row = lambda a: a.reshape(1, -1)
    small_wmv = [(row(ln_g), row(m_ln_g), row(v_ln_g)), (row(ln_b), row(m_ln_b), row(v_ln_b)),
                 (row(q_norm_g), row(m_q_norm_g), row(v_q_norm_g)), (row(kv_norm_g), row(m_kv_norm_g), row(v_kv_norm_g))]
    wmv = [(w_in, m_w_in, v_w_in), (w_uq, m_w_uq, v_w_uq), (w_ukv, m_w_ukv, v_w_ukv), (w_out, m_w_out, v_w_out)]
    res = _adamw_update([[g_in_first, g_in_rest], [g_uq], [g_ukv], [g_out]], g_small, wmv, small_wmv)
    big = [res[4 * t:4 * t + 4] for t in range(4)]
    small = [[a.reshape(-1) for a in res[16 + 4 * t:20 + 4 * t]] for t in range(4)]
    loss = res[32].reshape(())

    def group(kind):
        return (big[0][kind], small[2][kind], small[3][kind], big[1][kind], big[2][kind], big[3][kind],
                small[0][kind], small[1][kind])

    return (loss, grad_x[None], *group(0), *group(1), *group(2), *group(3))
```

```python
import functools
from typing import Callable, NamedTuple

import numpy as np
import jax
import jax.numpy as jnp
from jax import lax
from jax.experimental import pallas as pl
from jax.experimental.pallas import tpu as pltpu

F32 = jnp.float32
BF16 = jnp.bfloat16

D_MODEL = 1024
ROPE_THETA = 500000.0
NEG = -1e30
RMS_EPS = 1e-6
LN_EPS = 1e-5
HEADS = 8
MLA_NOPE = 64
MLA_ROPE = 32
MLA_V = 64
Q_RANK = 384
KV_RANK = 256
DIL_HEAD = 64
DIL_ROT = 16
DIL_CONFIGS = ((128, 1), (512, 4), (2048, 16))
DIL_NEAR = 512
HW = HEADS * 64
QW = HW + HEADS * MLA_ROPE
IN_SPLITS = (Q_RANK, KV_RANK, MLA_ROPE, HW, HW, HW, HW, HW)
IN_WIDTH = sum(IN_SPLITS)
ALPHA = 2.0 ** 0.25
MLA_SCALE = (MLA_NOPE + MLA_ROPE) ** -0.5
DIL_SCALE = DIL_HEAD ** -0.5
LOG2E = 1.4426950408889634
LN2 = 0.6931471805599453

ADAM_LR = 0.001
ADAM_B1 = 0.9
ADAM_B2 = 0.999
ADAM_EPS = 1e-08
ADAM_WD = 0.01
ADAM_STEP = 10

N_DEV = 8
LANES = 128
VMEM_LIMIT = 56 * 1024 * 1024
BLOCK_TOKENS = 512
BLOCK_MLA = 512
BLOCK_DIL = 512
GRAD_W_IN_CUT = 384

C_CQ, C_CKV, C_KR, C_GA, C_QB, C_KB, C_VB, C_GB, C_END = 0, 384, 640, 768, 1280, 1792, 2304, 2816, 3328

NT = (((1,), (1,)), ((), ()))
TN = (((0,), (0,)), ((), ()))


def _pcall(body, **kw):
    return pl.pallas_call(body, **kw)


def _cparams(**kw):
    return pltpu.CompilerParams(vmem_limit_bytes=VMEM_LIMIT, **kw)


def _rope_tables(seq):
    def tabs(dim, period):
        half = dim // 2
        inv = np.float32(ROPE_THETA) ** (-np.arange(0, dim, 2, dtype=np.float32) / np.float32(dim))
        ang = np.arange(seq, dtype=np.float32)[:, None] * inv.astype(np.float32)[None, :]
        cos, sin = np.cos(ang).astype(np.float32), np.sin(ang).astype(np.float32)
        j = np.arange(LANES) % period
        f = j % half
        c = np.where(j < dim, cos[:, f], np.float32(1.0))
        s1 = np.where(j < half, -sin[:, f], np.float32(0.0))
        s2 = np.where((j >= half) & (j < dim), sin[:, f], np.float32(0.0))
        return [c, s1, s2]
    return np.stack(tabs(MLA_ROPE, MLA_ROPE) + tabs(DIL_ROT, DIL_HEAD)).astype(np.float32)


def _rope(t, c, s1, s2, half):
    return t * c + pltpu.roll(t, LANES - half, 1) * s1 + pltpu.roll(t, half, 1) * s2


def _rope_t(d, c, s1, s2, half):
    return d * c + pltpu.roll(d * s1, half, 1) + pltpu.roll(d * s2, LANES - half, 1)


def _rope_wide(fn, t, c, s1, s2, half):
    return jnp.concatenate(
        [fn(t[:, i:i + LANES], c, s1, s2, half) for i in range(0, t.shape[1], LANES)], axis=1)


def _mla_bias_t(blk):
    a = np.arange(blk)
    causal = np.where(a[:, None] <= a[None, :], 0.0, NEG)
    return np.stack([np.zeros((blk, blk)), causal]).astype(np.float32)


def _dil_bias_t(blk, reach):
    a = np.arange(blk)
    out = []
    for off in range(-(-reach // blk) + 1):
        delta = blk * off + a[None, :] - a[:, None]
        mult = np.zeros((blk, blk))
        for window, dil in DIL_CONFIGS:
            mult += (delta >= 0) & (delta % dil == 0) & (delta <= min(window, reach))
        out.append(np.where(mult > 0, np.log2(np.maximum(mult, 1.0)), NEG))
    return np.stack(out).astype(np.float32)


def _dil_far_bias_t(length):
    window, dil = DIL_CONFIGS[-1]
    a = np.arange(length)
    steps_back = a[None, :] - a[:, None]
    seen = (steps_back * dil > DIL_NEAR) & (steps_back * dil <= window)
    return np.where(seen, 0.0, NEG).astype(np.float32)[None]


def _lanes_to_classes(a, dil):
    h, s = a.shape
    return a.reshape(h, s // dil, dil).transpose(0, 2, 1).reshape(h, s)


def _lanes_from_classes(a, dil):
    h, s = a.shape
    return a.reshape(h, dil, s // dil).transpose(0, 2, 1).reshape(h, s)


def _steps(nq, span, by_key, diag_only_bias):
    rows = []
    if by_key:
        for ki in range(nq):
            hi = min(nq - 1, ki + span)
            for qi in range(ki, hi + 1):
                rows.append((qi, ki, int(qi == ki), int(qi == hi)))
    else:
        for qi in range(nq):
            lo = max(0, qi - span)
            for ki in range(lo, qi + 1):
                rows.append((qi, ki, int(ki == lo), int(ki == qi)))
    arr = np.array(rows, dtype=np.int32)
    off = arr[:, 0] - arr[:, 1]
    bias_idx = (off == 0).astype(np.int32) if diag_only_bias else off.astype(np.int32)
    return [jnp.asarray(v) for v in (arr[:, 0], arr[:, 1], bias_idx, arr[:, 2], arr[:, 3])]


def _by_class(val, out_ref, lanes_sc):
    n_cls, per = out_ref.shape[0], out_ref.shape[1]
    for c in range(val.shape[1] // LANES):
        lanes_sc[c] = val[:, LANES * c:LANES * (c + 1)]
        for r in range(n_cls):
            rows = lanes_sc.at[c][pl.ds(r, per, stride=n_cls), :]
            out_ref[r, :, LANES * c:LANES * (c + 1)] = rows.astype(out_ref.dtype)


def _in_sequence(ref, lanes_sc):
    n_cls, per, width = ref.shape
    for c in range(width // LANES):
        for r in range(n_cls):
            lanes_sc.at[c][pl.ds(r, per, stride=n_cls), :] = ref[r, :, LANES * c:LANES * (c + 1)].astype(F32)
    return jnp.concatenate([lanes_sc[c] for c in range(width // LANES)], axis=1)


def _fwd_proj(x, w_in_r, w_uq_r, w_ukv_r, qg, kvg, tabs, bt, n_cls):
    seq = x.shape[0]

    def body(x_ref, win_ref, wuq_ref, wukv_ref, qg_ref, kvg_ref, tab_ref,
             cq_ref, ckv_ref, qn_ref, kvn_ref, qcat_ref, kn_ref, kpe_ref, v_ref,
             ga_ref, gb_ref, qb_ref, kb_ref, vb_ref, knt_ref, kpet_ref, vt_ref, kbt_ref, vbt_ref,
             qbc_ref, kbc_ref, vbc_ref, lanes_sc):
        xb = x_ref[...].astype(BF16)

        def proj(lo, hi):
            return jnp.dot(xb, win_ref[:, lo:hi], preferred_element_type=F32)

        m_tabs = (tab_ref[0], tab_ref[1], tab_ref[2])
        d_tabs = (tab_ref[3], tab_ref[4], tab_ref[5])

        def use_cq(cq):
            cq_ref[...] = cq
            qn = (cq * lax.rsqrt(jnp.mean(cq * cq, axis=1, keepdims=True) + RMS_EPS) * qg_ref[...]).astype(BF16)
            qn_ref[...] = qn
            q = jnp.dot(qn, wuq_ref[...], preferred_element_type=F32)
            qcat_ref[:, :HW] = (q[:, :HW] * (MLA_SCALE * LOG2E)).astype(BF16)
            qcat_ref[:, HW:] = (
                _rope_wide(_rope, q[:, HW:], *m_tabs, MLA_ROPE // 2) * (MLA_SCALE * LOG2E)).astype(BF16)

        def use_ckv(ckv):
            ckv_ref[...] = ckv
            kvn = (ckv * lax.rsqrt(jnp.mean(ckv * ckv, axis=1, keepdims=True) + RMS_EPS) * kvg_ref[...]).astype(BF16)
            kvn_ref[...] = kvn
            kv = jnp.dot(kvn, wukv_ref[...], preferred_element_type=F32)
            kn_ref[...] = kv[:, :HW].astype(BF16)
            v_ref[...] = kv[:, HW:].astype(BF16)
            knt_ref[...] = kv[:, :HW].T.astype(BF16)
            vt_ref[...] = kv[:, HW:].T.astype(BF16)

        def use_kr(kr):
            kpe = _rope(kr, *m_tabs, MLA_ROPE // 2)
            kpe_ref[...] = kpe.astype(BF16)
            kpet_ref[...] = kpe.T[:MLA_ROPE, :].astype(BF16)

        def use_ga(ga):
            ga_ref[...] = ga

        def use_qb(qb):
            qb = _rope_wide(_rope, qb, *d_tabs, DIL_ROT // 2) * (DIL_SCALE * LOG2E)
            qb_ref[...] = qb.astype(BF16)
            _by_class(qb, qbc_ref, lanes_sc)

        def use_kb(kb):
            kb = _rope_wide(_rope, kb, *d_tabs, DIL_ROT // 2)
            kb_ref[...] = kb.astype(BF16)
            kbt_ref[...] = kb.T.astype(BF16)
            _by_class(kb, kbc_ref, lanes_sc)

        def use_vb(vb):
            vb_ref[...] = vb.astype(BF16)
            vbt_ref[...] = vb.T.astype(BF16)
            _by_class(vb, vbc_ref, lanes_sc)

        def use_gb(gb):
            gb_ref[...] = gb

        pieces = [(C_CQ, C_CKV, use_cq), (C_CKV, C_KR, use_ckv), (C_KR, C_GA, use_kr), (C_GA, C_QB, use_ga),
                  (C_QB, C_KB, use_qb), (C_KB, C_VB, use_kb), (C_VB, C_GB, use_vb), (C_GB, C_END, use_gb)]
        ahead = proj(*pieces[0][:2])
        for n, (_, _, use) in enumerate(pieces):
            cur = ahead
            if n + 1 < len(pieces):
                ahead = proj(*pieces[n + 1][:2])
            use(cur)

    def tok(width):
        return pl.BlockSpec((bt, width), lambda i: (i, 0))

    def tok_t(height):
        return pl.BlockSpec((height, bt), lambda i: (0, i))

    def full(a):
        return pl.BlockSpec(a.shape, lambda i: (0,) * a.ndim)

    outs = [(Q_RANK, F32), (KV_RANK, F32), (Q_RANK, BF16), (KV_RANK, BF16), (QW, BF16), (HW, BF16),
            (LANES, BF16), (HW, BF16), (HW, F32), (HW, F32), (HW, BF16), (HW, BF16), (HW, BF16)]
    outs_t = [HW, MLA_ROPE, HW, HW, HW]
    by_class = pl.BlockSpec((n_cls, bt // n_cls, HW), lambda i: (0, i, 0))
    return _pcall(
        body, name="fwd_proj", grid=(seq // bt,),
        in_specs=[tok(D_MODEL), full(w_in_r), full(w_uq_r), full(w_ukv_r), full(qg), full(kvg),
                  pl.BlockSpec((6, bt, LANES), lambda i: (0, i, 0))],
        out_specs=[tok(w) for w, _ in outs] + [tok_t(h) for h in outs_t] + [by_class] * 3,
        out_shape=[jax.ShapeDtypeStruct((seq, w), dt) for w, dt in outs]
        + [jax.ShapeDtypeStruct((h, seq), BF16) for h in outs_t]
        + [jax.ShapeDtypeStruct((n_cls, seq // n_cls, HW), BF16)] * 3,
        scratch_shapes=[pltpu.VMEM((HW // LANES, bt, LANES), F32)],
        compiler_params=_cparams(dimension_semantics=("arbitrary",)),
    )(x, w_in_r, w_uq_r, w_ukv_r, qg, kvg, tabs)


def _head_masks(lane, h):
    e, g = h % 2, h % 4
    me = (lane >= 64 * e) & (lane < 64 * e + 64)
    mr = (lane >= 32 * g) & (lane < 32 * g + 32)
    return me, mr


def _masked(mask, a):
    return jnp.where(mask, a, jnp.zeros_like(a))


def _pair_operands(q_ref, k_ref, kpe_ref, lane, j, ks=slice(None), qs=slice(None)):
    cols = slice(LANES * j, LANES * (j + 1))
    qc = q_ref[qs, cols]
    kj = k_ref[ks, cols]
    kes = []
    for h in (2 * j, 2 * j + 1):
        me, mr = _head_masks(lane, h)
        ke = _masked(me, kj)
        if kpe_ref is not None:
            ke = jnp.concatenate([ke, _masked(mr, kpe_ref[ks, :])], axis=1)
        kes.append(ke)
    if kpe_ref is not None:
        qc = jnp.concatenate([qc, q_ref[qs, HW + LANES * (j // 2):HW + LANES * (j // 2 + 1)]], axis=1)
    return qc, kes


def _tile_variants(bias_t):
    out = {}
    for i, tile in enumerate(np.asarray(bias_t)):
        h = tile.shape[0] // 2
        skip = 1 if (tile[h:, :h] == NEG).all() else 2 if (tile[:h, h:] == NEG).all() else 0
        out[i] = (bool((tile != 0).any()), skip)
    return out


def _tile_parts(blk, skip):
    lo, hi, full = slice(0, blk // 2), slice(blk // 2, blk), slice(0, blk)
    return {0: [(full, full)], 1: [(lo, full), (hi, hi)], 2: [(hi, full), (lo, lo)]}[skip]


class Rider(NamedTuple):
    args: list
    in_specs: list
    out_shape: list
    out_specs: list
    scratch: list
    start: Callable
    finish: Callable
    stages: tuple = ()
    start_step: int = 0
    shared: int = 0


def _ride_along(body, ride, n_prefetch, n_in, n_out, n_scratch, n_steps):
    if ride is None:
        return body

    def wrapped(*refs):
        pre, rest = refs[:n_prefetch], refs[n_prefetch:]
        a = n_in
        b = a + len(ride.args)
        c = b + n_out
        d = c + len(ride.out_shape)
        e = d + n_scratch
        mine = (rest[a:b], rest[c:d], rest[e:])
        t = pl.program_id(0)
        pl.when(t == ride.start_step)(lambda: ride.start(*mine))
        for at, stage in ride.stages:
            pl.when(t == at)(functools.partial(stage, *mine))
        body(*pre, *rest[:a], *rest[b:c], *rest[d:e + ride.shared])
        pl.when(t == n_steps - 1)(lambda: ride.finish(*mine))

    return wrapped


def _attn_fwd(name, q, k, kpe, vt, bias_t, steps, blk, ride=None, v_token_major=False):
    seq = q.shape[0]
    mla = kpe is not None
    n_steps = int(steps[0].shape[0])
    variants = _tile_variants(bias_t)

    def body(qi_r, ki_r, bi_r, fi_r, la_r, *refs):
        if mla:
            q_ref, k_ref, kpe_ref, vt_ref, b_ref, o_ref, lse_ref, m_sc, l_sc, acc_sc, st_sc = refs
        else:
            q_ref, k_ref, vt_ref, b_ref, o_ref, lse_ref, m_sc, l_sc, acc_sc, st_sc = refs
        t = pl.program_id(0)

        @pl.when(fi_r[t] == 1)
        def _():
            m_sc[...] = jnp.full(m_sc.shape, NEG, F32)
            l_sc[...] = jnp.zeros(l_sc.shape, F32)
            acc_sc[...] = jnp.zeros(acc_sc.shape, F32)

        lane = lax.broadcasted_iota(jnp.int32, (1, LANES), 1)
        if v_token_major:
            vt_all = vt_ref[...].astype(F32).T.astype(BF16)
            vt_rows = lambda rows, ks: vt_all[rows, ks]
        else:
            vt_rows = lambda rows, ks: vt_ref[rows, ks]

        def tile_pass(ks, qs, with_bias):
            nk, nq = ks.stop - ks.start, qs.stop - qs.start
            ones = jnp.ones((16, nk), BF16)

            def pair_scores(j):
                qc, kes = _pair_operands(q_ref, k_ref, kpe_ref if mla else None, lane, j, ks, qs)
                st = lax.dot_general(jnp.concatenate(kes, axis=0), qc, NT, preferred_element_type=F32)
                maxes = []
                for e in range(2):
                    se = st[e * nk:(e + 1) * nk]
                    if with_bias:
                        se = se + b_ref[0, ks, qs]
                    st_sc[j % 2, e * nk:(e + 1) * nk, 0:nq] = se
                    maxes.append(jnp.max(se, axis=0, keepdims=True))
                return maxes

            def softmax_pv(h, col_max):
                st = st_sc[(h // 2) % 2, (h % 2) * nk:(h % 2 + 1) * nk, 0:nq]
                hrow = slice(h, h + 1)
                m_prev = m_sc[hrow, qs]
                m_new = jnp.maximum(m_prev, col_max)
                alpha = jnp.exp2(m_prev - m_new)
                pt = jnp.exp2(st - m_new).astype(BF16)
                m_sc[hrow, qs] = m_new
                rows = slice(64 * h, 64 * h + 64)
                res = jnp.dot(jnp.concatenate([vt_rows(rows, ks), ones], axis=0), pt, preferred_element_type=F32)
                acc_sc[rows, qs] = alpha * acc_sc[rows, qs] + res[:64]
                l_sc[hrow, qs] = alpha * l_sc[hrow, qs] + res[64:65]

            maxes = pair_scores(0)
            for j in range(HEADS // 2):
                cur = maxes
                if j + 1 < HEADS // 2:
                    maxes = pair_scores(j + 1)
                softmax_pv(2 * j, cur[0])
                softmax_pv(2 * j + 1, cur[1])

        def step(with_bias, skip):
            for ks, qs in _tile_parts(blk, skip):
                tile_pass(ks, qs, with_bias)

        for idx, (with_bias, skip) in variants.items():
            if len(variants) == 1:
                step(with_bias, skip)
            else:
                pl.when(bi_r[t] == idx)(functools.partial(step, with_bias, skip))

        @pl.when(la_r[t] == 1)
        def _():
            for h in range(HEADS):
                rows = slice(64 * h, 64 * h + 64)
                acc_sc[rows, :] = acc_sc[rows, :] / l_sc[h:h + 1, :]
            o_ref[...] = acc_sc[...].T
            lse_ref[...] = m_sc[...] + jnp.log2(l_sc[...])

    qmap = lambda t, qi, ki, bi, fi, la: (qi[t], 0)
    kmap = lambda t, qi, ki, bi, fi, la: (ki[t], 0)
    in_specs = [pl.BlockSpec((blk, q.shape[1]), qmap), pl.BlockSpec((blk, HW), kmap)]
    args = [q, k]
    if mla:
        in_specs.append(pl.BlockSpec((blk, LANES), kmap))
        args.append(kpe)
    in_specs += [pl.BlockSpec((blk, HW), kmap) if v_token_major else
                 pl.BlockSpec((HW, blk), lambda t, qi, ki, bi, fi, la: (0, ki[t])),
                 pl.BlockSpec((1, blk, blk), lambda t, qi, ki, bi, fi, la: (bi[t], 0, 0))]
    args += [vt, jnp.asarray(bias_t)]
    out_specs = [pl.BlockSpec((blk, HW), qmap), pl.BlockSpec((HEADS, blk), lambda t, qi, ki, bi, fi, la: (0, qi[t]))]
    out_shape = [jax.ShapeDtypeStruct((seq, HW), F32), jax.ShapeDtypeStruct((HEADS, seq), F32)]
    scratch = [pltpu.VMEM((HEADS, blk), F32), pltpu.VMEM((HEADS, blk), F32),
               pltpu.VMEM((HW, blk), F32), pltpu.VMEM((2, 2 * blk, blk), F32)]
    body = _ride_along(body, ride, 5, len(args), len(out_shape), len(scratch), n_steps)
    if ride is not None:
        args, in_specs = args + ride.args, in_specs + ride.in_specs
        out_specs, out_shape, scratch = out_specs + ride.out_specs, out_shape + ride.out_shape, scratch + ride.scratch
    return _pcall(
        body, name=name,
        grid_spec=pltpu.PrefetchScalarGridSpec(
            num_scalar_prefetch=5, grid=(n_steps,), in_specs=in_specs, out_specs=out_specs, scratch_shapes=scratch),
        out_shape=out_shape,
        compiler_params=_cparams(dimension_semantics=("arbitrary",)),
    )(*steps, *args)


def _attn_bwd(name, q, k, kpe, v, kt, kpet, bias_t, do, lse, dstat, steps, blk, ride=None, single_visit=False):
    assert not (single_visit and kpe is not None) and (kt is not None or single_visit)
    seq = q.shape[0]
    mla = kpe is not None
    qw = q.shape[1]
    n_steps = int(steps[0].shape[0])
    dk_dtype = BF16 if mla else F32
    variants = _tile_variants(bias_t)

    def body(qi_r, ki_r, bi_r, fi_r, la_r, *refs):
        if mla:
            (q_ref, k_ref, kpe_ref, v_ref, kt_ref, kpet_ref, b_ref, do_ref, lse_ref, d_ref,
             dq_ref, dk_ref, dkpe_ref, dv_ref, dk_sc, dkpe_sc, dv_sc, st_sc, dpt_sc) = refs
        else:
            q_ref, k_ref, v_ref, *rest = refs
            kt_ref = rest.pop(0) if kt is not None else None
            b_ref, do_ref, lse_ref, d_ref, dq_out_ref, dk_ref, dv_ref, dk_sc, dv_sc, st_sc, dpt_sc, *rest = rest
            dq_ref = rest[0] if single_visit else dq_out_ref
        t = pl.program_id(0)

        @pl.when(jnp.logical_or(t == 0, single_visit))
        def _():
            dq_ref[...] = jnp.zeros(dq_ref.shape, F32)

        @pl.when(fi_r[t] == 1)
        def _():
            dk_sc[...] = jnp.zeros(dk_sc.shape, F32)
            dv_sc[...] = jnp.zeros(dv_sc.shape, F32)
            if mla:
                dkpe_sc[...] = jnp.zeros(dkpe_sc.shape, F32)

        qi = 0 if single_visit else qi_r[t]
        lane = lax.broadcasted_iota(jnp.int32, (1, LANES), 1)
        if kt is None:
            kt_all = k_ref[...].astype(F32).T.astype(BF16)
            kt_rows = lambda rows, ks: kt_all[rows, ks]
        else:
            kt_rows = lambda rows, ks: kt_ref[rows, ks]

        def tile_pass(ks, qs, with_bias):
            nk, nq = ks.stop - ks.start, qs.stop - qs.start

            def pair_matmuls(j):
                cols = slice(LANES * j, LANES * (j + 1))
                qc, kes = _pair_operands(q_ref, k_ref, kpe_ref if mla else None, lane, j, ks, qs)
                st_sc[j % 2, 0:2 * nk, 0:nq] = lax.dot_general(
                    jnp.concatenate(kes, axis=0), qc, NT, preferred_element_type=F32)
                vj = v_ref[ks, cols]
                ves = [_masked(_head_masks(lane, h)[0], vj) for h in (2 * j, 2 * j + 1)]
                dpt_sc[j % 2, 0:2 * nk, 0:nq] = lax.dot_general(
                    jnp.concatenate(ves, axis=0), do_ref[qs, cols], NT, preferred_element_type=F32)

            def pair_grads(j):
                cols = slice(LANES * j, LANES * (j + 1))
                qj, doj = q_ref[qs, cols], do_ref[qs, cols]
                if mla:
                    qr = q_ref[qs, HW + LANES * (j // 2):HW + LANES * (j // 2 + 1)]
                pts, dsts, qms, doms = [], [], [], []
                for e in range(2):
                    h = 2 * j + e
                    me, mr = _head_masks(lane, h)
                    st = st_sc[j % 2, e * nk:(e + 1) * nk, 0:nq]
                    if with_bias:
                        st = st + b_ref[0, ks, qs]
                    pt = jnp.exp2(st - lse_ref[h:h + 1, qs])
                    dst = (pt * (dpt_sc[j % 2, e * nk:(e + 1) * nk, 0:nq] - d_ref[h:h + 1, qs])).astype(BF16)
                    pts.append(pt.astype(BF16))
                    dsts.append(dst)
                    doms.append(_masked(me, doj))
                    qm = _masked(me, qj)
                    if mla:
                        qm = jnp.concatenate([qm, _masked(mr, qr)], axis=1)
                    qms.append(qm)
                    ktl = kt_rows(slice(64 * h, 64 * h + 64), ks)
                    if mla:
                        ktl = jnp.concatenate([ktl, kpet_ref[:, ks]], axis=0)
                    dqc = jnp.dot(ktl, dst, preferred_element_type=F32)
                    dq_ref[qi, 64 * h:64 * h + 64, qs] += dqc[:64]
                    if mla:
                        dq_ref[qi, HW + MLA_ROPE * h:HW + MLA_ROPE * (h + 1), qs] += dqc[64:]
                dv_sc[ks, cols] += jnp.dot(
                    jnp.concatenate(pts, axis=1), jnp.concatenate(doms, axis=0), preferred_element_type=F32)
                dkc = jnp.dot(jnp.concatenate(dsts, axis=1), jnp.concatenate(qms, axis=0), preferred_element_type=F32)
                dk_sc[ks, cols] += dkc[:, :LANES]
                if mla:
                    dkpe_sc[ks, :] += dkc[:, LANES:]

            pair_matmuls(0)
            for j in range(HEADS // 2):
                if j + 1 < HEADS // 2:
                    pair_matmuls(j + 1)
                pair_grads(j)

        def step(with_bias, skip):
            for ks, qs in _tile_parts(blk, skip):
                tile_pass(ks, qs, with_bias)

        for idx, (with_bias, skip) in variants.items():
            if len(variants) == 1:
                step(with_bias, skip)
            else:
                pl.when(bi_r[t] == idx)(functools.partial(step, with_bias, skip))
        if single_visit:
            dq_out_ref[...] = dq_ref[0].T

        @pl.when(la_r[t] == 1)
        def _():
            dk_ref[...] = (dk_sc[...] * LN2).astype(dk_ref.dtype)
            dv_ref[...] = dv_sc[...].astype(dv_ref.dtype)
            if mla:
                dkpe_ref[...] = dkpe_sc[...] * LN2

    qmap = lambda t, qi, ki, bi, fi, la: (qi[t], 0)
    kmap = lambda t, qi, ki, bi, fi, la: (ki[t], 0)
    qmap_t = lambda t, qi, ki, bi, fi, la: (0, qi[t])
    kmap_t = lambda t, qi, ki, bi, fi, la: (0, ki[t])
    in_specs = [pl.BlockSpec((blk, qw), qmap), pl.BlockSpec((blk, HW), kmap)]
    args = [q, k]
    if mla:
        in_specs.append(pl.BlockSpec((blk, LANES), kmap))
        args.append(kpe)
    in_specs.append(pl.BlockSpec((blk, HW), kmap))
    args.append(v)
    if kt is not None:
        in_specs.append(pl.BlockSpec((HW, blk), kmap_t))
        args.append(kt)
    if mla:
        in_specs.append(pl.BlockSpec((MLA_ROPE, blk), kmap_t))
        args.append(kpet)
    in_specs += [pl.BlockSpec((1, blk, blk), lambda t, qi, ki, bi, fi, la: (bi[t], 0, 0)),
                 pl.BlockSpec((blk, HW), qmap), pl.BlockSpec((HEADS, blk), qmap_t), pl.BlockSpec((HEADS, blk), qmap_t)]
    args += [jnp.asarray(bias_t), do, lse, dstat]
    dq_shape = (seq // blk, qw, blk)
    if single_visit:
        out_specs, out_shape = [pl.BlockSpec((blk, qw), qmap)], [jax.ShapeDtypeStruct((seq, qw), F32)]
    else:
        out_specs = [pl.BlockSpec(dq_shape, lambda t, qi, ki, bi, fi, la: (0, 0, 0))]
        out_shape = [jax.ShapeDtypeStruct(dq_shape, F32)]
    out_specs.append(pl.BlockSpec((blk, HW), kmap))
    out_shape.append(jax.ShapeDtypeStruct((seq, HW), dk_dtype))
    scratch = [pltpu.VMEM((blk, HW), F32)]
    if mla:
        out_specs.append(pl.BlockSpec((blk, LANES), kmap))
        out_shape.append(jax.ShapeDtypeStruct((seq, LANES), F32))
        scratch.append(pltpu.VMEM((blk, LANES), F32))
    out_specs.append(pl.BlockSpec((blk, HW), kmap))
    out_shape.append(jax.ShapeDtypeStruct((seq, HW), BF16))
    scratch.append(pltpu.VMEM((blk, HW), F32))
    scratch += [pltpu.VMEM((2, 2 * blk, blk), F32), pltpu.VMEM((2, 2 * blk, blk), F32)]
    if single_visit:
        scratch.append(pltpu.VMEM((1, qw, blk), F32))
    body = _ride_along(body, ride, 5, len(args), len(out_shape), len(scratch), n_steps)
    if ride is not None:
        args, in_specs = args + ride.args, in_specs + ride.in_specs
        out_specs, out_shape, scratch = out_specs + ride.out_specs, out_shape + ride.out_shape, scratch + ride.scratch
    return _pcall(
        body, name=name,
        grid_spec=pltpu.PrefetchScalarGridSpec(
            num_scalar_prefetch=5, grid=(n_steps,), in_specs=in_specs, out_specs=out_specs,
            scratch_shapes=scratch),
        out_shape=out_shape,
        compiler_params=_cparams(dimension_semantics=("arbitrary",)),
    )(*steps, *args)


def _out_ln(oa, ob_near, ob_far, lse_near, lse_far, ga, gb, x, tgt, w_out, ln_g, ln_b, bt):
    seq = x.shape[0]

    def body(oa_ref, obn_ref, obf_ref, lsen_ref, lsef_ref, ga_ref, gb_ref, x_ref, tgt_ref, w_ref, g_ref, b_ref,
             dz_ref, doa_ref, dob_ref, dga_ref, dgb_ref, da_ref, db_ref, lse_ref, gwb_ref, small_ref, dobc_ref,
             gw_ref, lanes_sc):
        i = pl.program_id(0)

        @pl.when(i == 0)
        def _():
            gw_ref[...] = jnp.zeros(gw_ref.shape, F32)
            small_ref[...] = jnp.zeros(small_ref.shape, F32)

        def gate(g):
            sig = 0.5 * jnp.tanh(0.5 * g) + 0.5
            return g * sig, sig * (1.0 + g * (1.0 - sig))

        lse_n, lse_f = lsen_ref[...], lsef_ref[...]
        top = jnp.maximum(lse_n, lse_f)
        e_n, e_f = jnp.exp2(lse_n - top), jnp.exp2(lse_f - top)
        lse_ref[...] = top + jnp.log2(e_n + e_f)
        inv = 1.0 / (e_n + e_f)
        head_row = lax.broadcasted_iota(jnp.int32, (2 * HEADS, HW), 0) % HEADS
        spread = (head_row == lax.broadcasted_iota(jnp.int32, (2 * HEADS, HW), 1) // 64).astype(BF16)

        def per_lane(w):
            hi = w.astype(BF16)
            lo = (w - hi.astype(F32)).astype(BF16)
            return lax.dot_general(jnp.concatenate([hi, lo], axis=0), spread, TN, preferred_element_type=F32)

        o_b_all = per_lane(e_n * inv) * obn_ref[...] + per_lane(e_f * inv) * _in_sequence(obf_ref, lanes_sc)
        gam = g_ref[...]
        halves = [slice(0, bt // 2), slice(bt // 2, bt)]

        def gates_and_projection(rows):
            o_a, o_b = oa_ref[rows, :], o_b_all[rows]
            sa, dsa = gate(ga_ref[rows, :])
            sb, dsb = gate(gb_ref[rows, :])
            mix = jnp.concatenate([o_a * sa, o_b * sb], axis=1).astype(BF16)
            z = ALPHA * x_ref[rows, :] + jnp.dot(mix, w_ref[...], preferred_element_type=F32)
            return o_a, o_b, sa, dsa, sb, dsb, mix, z

        def norm_and_back(rows, mix, z):
            mu = jnp.mean(z, axis=1, keepdims=True)
            zc = z - mu
            rstd = lax.rsqrt(jnp.mean(zc * zc, axis=1, keepdims=True) + LN_EPS)
            xhat = zc * rstd
            diff = xhat * gam + b_ref[...] - tgt_ref[rows, :]
            dy = diff * (1.0 / D_MODEL)
            small_ref[0:1, :] += jnp.sum(dy * xhat, axis=0, keepdims=True)
            small_ref[1:2, :] += jnp.sum(dy, axis=0, keepdims=True)
            small_ref[2:3, :] += jnp.sum(diff * diff, axis=0, keepdims=True)
            dxh = dy * gam
            dz = rstd * (dxh - jnp.mean(dxh, axis=1, keepdims=True)
                         - xhat * jnp.mean(dxh * xhat, axis=1, keepdims=True))
            dz_ref[rows, :] = dz
            dzb = dz.astype(BF16)
            gw_ref[...] += lax.dot_general(mix, dzb, TN, preferred_element_type=F32)
            return lax.dot_general(dzb, w_ref[...], NT, preferred_element_type=F32)

        def gate_back(rows, o_a, o_b, sa, dsa, sb, dsb, dmix):
            doa, dob = dmix[:, :HW] * sa, dmix[:, HW:] * sb
            doa_ref[rows, :] = doa.astype(BF16)
            dob_ref[rows, :] = dob.astype(BF16)
            dga_ref[rows, :] = (dmix[:, :HW] * o_a * dsa).astype(BF16)
            dgb_ref[rows, :] = (dmix[:, HW:] * o_b * dsb).astype(BF16)
            return dob, doa * o_a, dob * o_b

        fronts = [gates_and_projection(rows) for rows in halves]
        dmixes = [norm_and_back(rows, f[6], f[7]) for rows, f in zip(halves, fronts)]
        backs = [gate_back(rows, *f[:6], dmix) for rows, f, dmix in zip(halves, fronts, dmixes)]
        dob, prod_a, prod_b = (jnp.concatenate(parts, axis=0) for parts in zip(*backs))
        _by_class(dob, dobc_ref, lanes_sc)

        @pl.when(i == seq // bt - 1)
        def _():
            gwb_ref[...] = gw_ref[...].astype(BF16)

        head_of = (lax.broadcasted_iota(jnp.int32, (2 * HW, LANES), 0) % HW) // 64
        ind = (head_of == lax.broadcasted_iota(jnp.int32, (2 * HW, LANES), 1)).astype(BF16)

        def head_sums(prod):
            hi = prod.astype(BF16)
            lo = (prod - hi.astype(F32)).astype(BF16)
            sums = jnp.dot(jnp.concatenate([hi, lo], axis=1), ind, preferred_element_type=F32)
            return sums.T[:HEADS, :]

        da_ref[...] = head_sums(prod_a)
        db_ref[...] = head_sums(prod_b)

    def tok(width):
        return pl.BlockSpec((bt, width), lambda i: (i, 0))

    def full(shape):
        return pl.BlockSpec(shape, lambda i: (0,) * len(shape))

    stat = pl.BlockSpec((HEADS, bt), lambda i: (0, i))
    n_cls = ob_far.shape[0]
    by_class = pl.BlockSpec((n_cls, bt // n_cls, HW), lambda i: (0, i, 0))
    return _pcall(
        body, name="out_ln", grid=(seq // bt,),
        in_specs=[tok(HW), tok(HW), by_class, stat, stat, tok(HW), tok(HW), tok(D_MODEL), tok(D_MODEL),
                  full((D_MODEL, D_MODEL)), full((1, D_MODEL)), full((1, D_MODEL))],
        out_specs=[tok(D_MODEL), tok(HW), tok(HW), tok(HW), tok(HW), stat, stat, stat,
                   full((D_MODEL, D_MODEL)), full((8, D_MODEL)), by_class],
        out_shape=[jax.ShapeDtypeStruct((seq, D_MODEL), F32)] + [jax.ShapeDtypeStruct((seq, HW), BF16)] * 4
        + [jax.ShapeDtypeStruct((HEADS, seq), F32)] * 3
        + [jax.ShapeDtypeStruct((D_MODEL, D_MODEL), BF16), jax.ShapeDtypeStruct((8, D_MODEL), F32),
           jax.ShapeDtypeStruct(ob_far.shape, BF16)],
        scratch_shapes=[pltpu.VMEM((D_MODEL, D_MODEL), F32), pltpu.VMEM((HW // LANES, bt, LANES), F32)],
        compiler_params=_cparams(dimension_semantics=("arbitrary",)),
    )(oa, ob_near, ob_far, lse_near, lse_far, ga, gb, x, tgt, w_out, ln_g, ln_b)


def _bwd_mid(dq_m, dkn, dv, dkpe, dqb, dkb, dvb, far, dga, dgb, cq, ckv, qn, kvn, w_uq_r, w_ukv_r, qg, kvg, tabs, bt):
    n_cls = far[0].shape[0]
    seq = cq.shape[0]

    def body(dqm_ref, dkn_ref, dv_ref, dkpe_ref, dqb_ref, dkb_ref, dvb_ref, dqf_ref, dkf_ref, dvf_ref, dga_ref, dgb_ref,
             cq_ref, ckv_ref, qn_ref, kvn_ref, wuq_ref, wukv_ref, qg_ref, kvg_ref, tab_ref,
             dh_ref, guq3_ref, gukv3_ref, small_ref, seq_sc, guq_ref, gukv_ref):
        i = pl.program_id(0)

        @pl.when(i == 0)
        def _():
            guq_ref[...] = jnp.zeros(guq_ref.shape, F32)
            gukv_ref[...] = jnp.zeros(gukv_ref.shape, F32)
            small_ref[...] = jnp.zeros(small_ref.shape, F32)

        m_tabs = (tab_ref[0], tab_ref[1], tab_ref[2])
        d_tabs = (tab_ref[3], tab_ref[4], tab_ref[5])

        def rms_bwd(c, dn, gain):
            r = lax.rsqrt(jnp.mean(c * c, axis=1, keepdims=True) + RMS_EPS)
            u = dn * gain
            dc = r * u - c * (r * r * r) * jnp.mean(u * c, axis=1, keepdims=True)
            return dc, jnp.sum(dn * c * r, axis=0, keepdims=True)

        dqm = dqm_ref[0].T
        dq = jnp.concatenate(
            [dqm[:, :HW], _rope_wide(_rope_t, dqm[:, HW:], *m_tabs, MLA_ROPE // 2)], axis=1) * MLA_SCALE
        dq = dq.astype(BF16)
        dkv = jnp.concatenate([dkn_ref[...], dv_ref[...]], axis=1)
        guq_ref[...] += lax.dot_general(qn_ref[...], dq, TN, preferred_element_type=F32)
        dqn = lax.dot_general(dq, wuq_ref[...], NT, preferred_element_type=F32)
        gukv_ref[...] += lax.dot_general(kvn_ref[...], dkv, TN, preferred_element_type=F32)
        dkvn = lax.dot_general(dkv, wukv_ref[...], NT, preferred_element_type=F32)

        dh_ref[:, C_KR:C_GA] = _rope_t(dkpe_ref[...], *m_tabs, MLA_ROPE // 2).astype(BF16)
        dh_ref[:, C_GA:C_QB] = dga_ref[...]
        in_sequence = functools.partial(_in_sequence, lanes_sc=seq_sc)
        dqb = dqb_ref[0].T + in_sequence(dqf_ref)
        dh_ref[:, C_QB:C_KB] = (_rope_wide(_rope_t, dqb, *d_tabs, DIL_ROT // 2) * DIL_SCALE).astype(BF16)
        dkb = dkb_ref[...] + in_sequence(dkf_ref)
        dh_ref[:, C_KB:C_VB] = _rope_wide(_rope_t, dkb, *d_tabs, DIL_ROT // 2).astype(BF16)
        dh_ref[:, C_VB:C_GB] = (dvb_ref[...].astype(F32) + in_sequence(dvf_ref)).astype(BF16)
        dh_ref[:, C_GB:C_END] = dgb_ref[...]

        dcq, gq = rms_bwd(cq_ref[...], dqn, qg_ref[...])
        small_ref[0:1, :] += gq
        dckv, gkv = rms_bwd(ckv_ref[...], dkvn, kvg_ref[...])
        small_ref[1:2, :KV_RANK] += gkv
        dh_ref[:, C_CQ:C_CKV] = dcq.astype(BF16)
        dh_ref[:, C_CKV:C_KR] = dckv.astype(BF16)

        @pl.when(i == seq // bt - 1)
        def _():
            for h in range(HEADS):
                guq3_ref[h] = jnp.concatenate(
                    [guq_ref[:, MLA_NOPE * h:MLA_NOPE * (h + 1)],
                     guq_ref[:, HW + MLA_ROPE * h:HW + MLA_ROPE * (h + 1)]], axis=1).astype(BF16)
                gukv3_ref[h] = jnp.concatenate(
                    [gukv_ref[:, MLA_NOPE * h:MLA_NOPE * (h + 1)],
                     gukv_ref[:, HW + MLA_V * h:HW + MLA_V * (h + 1)]], axis=1).astype(BF16)

    def tok(width):
        return pl.BlockSpec((bt, width), lambda i: (i, 0))

    def tok_t(a):
        per = a.shape[2] // bt
        return pl.BlockSpec((1, a.shape[1], bt), lambda i: (i // per, 0, i % per))

    def full(shape):
        return pl.BlockSpec(shape, lambda i: (0,) * len(shape))

    by_class = pl.BlockSpec((n_cls, bt // n_cls, HW), lambda i: (0, i, 0))
    uq3 = (HEADS, Q_RANK, MLA_NOPE + MLA_ROPE)
    ukv3 = (HEADS, KV_RANK, MLA_NOPE + MLA_V)
    return _pcall(
        body, name="bwd_mid", grid=(seq // bt,),
        in_specs=[tok_t(dq_m), tok(HW), tok(HW), tok(LANES), tok_t(dqb), tok(HW), tok(HW), by_class, by_class, by_class,
                  tok(HW), tok(HW),
                  tok(Q_RANK), tok(KV_RANK), tok(Q_RANK), tok(KV_RANK),
                  full(w_uq_r.shape), full(w_ukv_r.shape), full((1, Q_RANK)), full((1, KV_RANK)),
                  pl.BlockSpec((6, bt, LANES), lambda i: (0, i, 0))],
        out_specs=[tok(C_END), full(uq3), full(ukv3), full((8, Q_RANK))],
        out_shape=[jax.ShapeDtypeStruct((seq, C_END), BF16), jax.ShapeDtypeStruct(uq3, BF16),
                   jax.ShapeDtypeStruct(ukv3, BF16), jax.ShapeDtypeStruct((8, Q_RANK), F32)],
        scratch_shapes=[pltpu.VMEM((HW // LANES, bt, LANES), F32), pltpu.VMEM(w_uq_r.shape, F32),
                        pltpu.VMEM(w_ukv_r.shape, F32)],
        compiler_params=_cparams(dimension_semantics=("arbitrary",)),
    )(dq_m, dkn, dv, dkpe, dqb, dkb, dvb, *far, dga, dgb, cq, ckv, qn, kvn, w_uq_r, w_ukv_r, qg, kvg, tabs)


def _grad_x(dz, dh, w_in_r, bt, ride=None):
    seq = dz.shape[0]
    n_steps = seq // bt

    def body(dz_ref, dh_ref, w_ref, gx_ref):
        gx_ref[...] = ALPHA * dz_ref[...] + lax.dot_general(
            dh_ref[...], w_ref[...], NT, preferred_element_type=F32)

    args = [dz, dh, w_in_r]
    in_specs = [pl.BlockSpec((bt, D_MODEL), lambda i: (i, 0)), pl.BlockSpec((bt, C_END), lambda i: (i, 0)),
                pl.BlockSpec(w_in_r.shape, lambda i: (0, 0))]
    out_specs = [pl.BlockSpec((bt, D_MODEL), lambda i: (i, 0))]
    out_shape = [jax.ShapeDtypeStruct((seq, D_MODEL), F32)]
    scratch = []
    body = _ride_along(body, ride, 0, len(args), len(out_shape), 0, n_steps)
    if ride is not None:
        args, in_specs = args + ride.args, in_specs + ride.in_specs
        out_specs, out_shape, scratch = out_specs + ride.out_specs, out_shape + ride.out_shape, ride.scratch
    return _pcall(
        body, name="grad_x", grid=(n_steps,),
        in_specs=in_specs, out_specs=out_specs, out_shape=out_shape, scratch_shapes=scratch,
        compiler_params=_cparams(dimension_semantics=("arbitrary",)),
    )(*args)


def _grad_w_in(x, dh, bt, cut, ride):
    seq = x.shape[0]
    n_tok = seq // bt
    shard = IN_WIDTH // N_DEV
    k_lo, k_hi = IN_SPLITS[0] + IN_SPLITS[1], IN_SPLITS[0] + IN_SPLITS[1] + MLA_ROPE

    def body(x_ref, dh_ref, rest_ref, acc, first_ref):
        i = pl.program_id(0)
        for part, (r_lo, r_hi, dst_ref) in enumerate([(0, cut, first_ref), (cut, D_MODEL, rest_ref)]):
            rows = slice(0, r_hi - r_lo)

            @pl.when(i == part * n_tok)
            def _():
                acc[rows, :] = jnp.zeros((r_hi - r_lo, C_END), F32)

            @pl.when(i // n_tok == part)
            def _():
                acc[rows, :] += lax.dot_general(
                    x_ref[:, r_lo:r_hi].astype(BF16), dh_ref[...], TN, preferred_element_type=F32)

            @pl.when(i == (part + 1) * n_tok - 1)
            def _():
                kr = acc[rows, C_KR:C_GA]
                kr = kr + pltpu.roll(kr, 96, 1) + pltpu.roll(kr, 64, 1) + pltpu.roll(kr, 32, 1)
                for d in range(N_DEV):
                    lo, hi = shard * d, shard * (d + 1)
                    pieces = []
                    if lo < k_lo:
                        pieces.append(acc[rows, lo:min(hi, k_lo)])
                    if lo < k_hi and hi > k_lo:
                        pieces.append(kr[:, max(lo, k_lo) - k_lo:min(hi, k_hi) - k_lo])
                    if hi > k_hi:
                        shift = C_GA - k_hi
                        pieces.append(acc[rows, max(lo, k_hi) + shift:hi + shift])
                    blk = pieces[0] if len(pieces) == 1 else jnp.concatenate(pieces, axis=1)
                    dst_ref[d] = blk.astype(BF16)

    args = [x, dh]
    in_specs = [pl.BlockSpec((bt, D_MODEL), lambda i: (i % n_tok, 0)),
                pl.BlockSpec((bt, C_END), lambda i: (i % n_tok, 0))]
    out_specs = [pl.BlockSpec((N_DEV, D_MODEL - cut, shard), lambda i: (0, 0, 0))]
    out_shape = [jax.ShapeDtypeStruct((N_DEV, D_MODEL - cut, shard), BF16)]
    scratch = [pltpu.VMEM((max(cut, D_MODEL - cut), C_END), F32)]
    assert ride.shared == 1
    body = _ride_along(body, ride, 0, len(args), len(out_shape), len(scratch), 2 * n_tok)
    return _pcall(
        body, name="grad_w_in", grid=(2 * n_tok,),
        in_specs=in_specs + ride.in_specs, out_specs=out_specs + ride.out_specs,
        out_shape=out_shape + ride.out_shape, scratch_shapes=scratch + ride.scratch,
        compiler_params=_cparams(dimension_semantics=("arbitrary",)),
    )(*args, *ride.args)


def _local_step(x, tgt, w_in_r, w_uq_r, w_ukv_r, w_out_rider, g_out_rider, reduce_rider, q_norm_g, kv_norm_g,
                ln_g, ln_b, bt=BLOCK_TOKENS, blk_m=BLOCK_MLA, blk_d=BLOCK_DIL):
    seq = x.shape[0]
    tabs = jnp.asarray(_rope_tables(seq))
    qg, kvg = q_norm_g.reshape(1, -1), kv_norm_g.reshape(1, -1)

    far_dil = DIL_CONFIGS[-1][1]
    cls = seq // far_dil
    (cq, ckv, qn, kvn, qcat, kn, kpe, v, ga, gb, qb, kb, vb, knt, kpet, vt, kbt, vbt, qb_c, kb_c, vb_c) = _fwd_proj(
        x, w_in_r, w_uq_r, w_ukv_r, qg, kvg, tabs, bt, far_dil)
    qb_c, kb_c, vb_c = (a.reshape(seq, HW) for a in (qb_c, kb_c, vb_c))

    nq_m, nq_d = seq // blk_m, seq // blk_d
    bias_m = _mla_bias_t(blk_m)
    oa, lse_a, w_out = _attn_fwd(
        "mla_fwd", qcat, kn, kpe, vt, bias_m, _steps(nq_m, nq_m, False, True), blk_m, ride=w_out_rider)

    bias_near = _dil_bias_t(blk_d, DIL_NEAR)
    ob_near, lse_near = _attn_fwd(
        "dil_fwd", qb, kb, None, vbt, bias_near, _steps(nq_d, -(-DIL_NEAR // blk_d), False, False), blk_d)
    each = np.arange(far_dil, dtype=np.int32)
    steps_far = [jnp.asarray(v) for v in (each, each, np.zeros_like(each), np.ones_like(each), np.ones_like(each))]
    bias_far = _dil_far_bias_t(cls)
    ob_far, lse_far = _attn_fwd(
        "dil_far_fwd", qb_c, kb_c, None, vb_c, bias_far, steps_far, cls, v_token_major=True)

    dz, doa, dob, dga, dgb, dst_a, dst_b, lse_b, g_out, small1, dob_c = _out_ln(
        oa, ob_near, ob_far.reshape(far_dil, cls, HW), lse_near, _lanes_from_classes(lse_far, far_dil), ga, gb, x, tgt,
        w_out.reshape(D_MODEL, D_MODEL), ln_g.reshape(1, -1), ln_b.reshape(1, -1), bt)

    dq_m, dkn, dkpe, dv, g_out_recv = _attn_bwd(
        "mla_bwd", qcat, kn, kpe, v, knt, kpet, bias_m, doa, lse_a, dst_a, _steps(nq_m, nq_m, True, True), blk_m,
        ride=g_out_rider(g_out.reshape(N_DEV, D_MODEL // N_DEV, D_MODEL)))
    dqb, dkb_near, dvb_near = _attn_bwd(
        "dil_bwd", qb, kb, None, vb, kbt, None, bias_near, dob, lse_b, dst_b,
        _steps(nq_d, -(-DIL_NEAR // blk_d), True, False), blk_d)
    dqb_far, dkb_far, dvb_far = _attn_bwd(
        "dil_far_bwd", qb_c, kb_c, None, vb_c, None, None, bias_far, dob_c.reshape(seq, HW),
        _lanes_to_classes(lse_b, far_dil), _lanes_to_classes(dst_b, far_dil), steps_far, cls, single_visit=True)
    far = [a.reshape(far_dil, cls, HW) for a in (dqb_far, dkb_far, dvb_far)]

    dh, g_uq, g_ukv, small2 = _bwd_mid(
        dq_m, dkn, dv, dkpe, dqb, dkb_near, dvb_near, far, dga, dgb, cq, ckv, qn, kvn, w_uq_r, w_ukv_r, qg, kvg, tabs, bt)
    bt_w = min(seq, 2 * bt)
    n_tok = seq // bt_w
    first = jax.ShapeDtypeStruct((N_DEV, GRAD_W_IN_CUT, IN_WIDTH // N_DEV), BF16)
    g_in_rest, *reduced_first = _grad_w_in(
        x, dh, bt_w, GRAD_W_IN_CUT,
        reduce_rider([first, g_uq, g_ukv], g_out_recv, (small1, small2), (n_tok, n_tok, 2 * n_tok - 1)))
    grad_x, g_in_rest = _grad_x(dz, dh, w_in_r, bt, ride=reduce_rider([g_in_rest], None, None, (0, 1, seq // bt - 2)))
    return grad_x, [g_in_rest] + reduced_first


MESH_ID = pl.DeviceIdType.MESH
SHARD_SHAPES = ((D_MODEL, IN_WIDTH // N_DEV), (Q_RANK, 768 // N_DEV), (KV_RANK, 1024 // N_DEV), (D_MODEL // N_DEV, D_MODEL))
ADAM_ROWS = (32, 128, 128, 16)


def _me():
    x, y, c = lax.axis_index("x"), lax.axis_index("y"), lax.axis_index("c")
    return x, y, c, 4 * x + 2 * y + c


def _peer(k):
    x, y, c, _ = _me()
    px = 1 - x if (k >> 2) & 1 else x
    py = 1 - y if (k >> 1) & 1 else y
    pc = 1 - c if k & 1 else c
    return (px, py, pc), 4 * px + 2 * py + pc


def _all_gather_weights(shards):
    n = len(shards)
    shard = IN_WIDTH // N_DEV
    k_lo = IN_SPLITS[0] + IN_SPLITS[1]
    k_hi = k_lo + MLA_ROPE

    def body(*refs):
        ins = refs[:n]
        win_ref, wuq_ref, wukv_ref = refs[n:2 * n]
        bufs = refs[2 * n:3 * n]
        send_sems, recv_sems = refs[3 * n:]
        x, y, c, me = _me()
        here, sibling = (x, y, c), (x, y, 1 - c)
        along_x, along_y, across = (1 - x, y), (x, 1 - y), (1 - x, 1 - y)
        for t in range(n):
            bufs[t][me] = ins[t][...].astype(BF16)

        def copy(t, k, chip, pc, to, half=None):
            blk = bufs[t].at[4 * chip[0] + 2 * chip[1] + pc]
            if half is not None:
                rows = SHARD_SHAPES[t][0] // 2
                blk = blk.at[pl.ds(half * rows, rows), :]
            return pltpu.make_async_remote_copy(
                src_ref=blk, dst_ref=blk, send_sem=send_sems.at[t, k], recv_sem=recv_sems.at[t, k],
                device_id=to, device_id_type=MESH_ID)

        sends = []
        for t in range(n):
            sends += [copy(t, 0, (x, y), c, sibling), copy(t, 1, (x, y), c, (*along_x, c)),
                      copy(t, 2, (x, y), c, (*along_y, c))]
        for cp in sends:
            cp.start()
        for t in range(n):
            copy(t, 1, along_x, c, here).wait_recv()
            sends += [copy(t, 3, along_x, c, (*along_y, c), half=0), copy(t, 5, along_x, c, sibling)]
            sends[-2].start()
            sends[-1].start()
        for t in range(n):
            copy(t, 2, along_y, c, here).wait_recv()
            sends += [copy(t, 4, along_y, c, (*along_x, c), half=1), copy(t, 6, along_y, c, sibling)]
            sends[-2].start()
            sends[-1].start()
        for t in range(n):
            copy(t, 3, across, c, here, half=0).wait_recv()
            copy(t, 4, across, c, here, half=1).wait_recv()
            sends.append(copy(t, 7, across, c, sibling))
            sends[-1].start()
        for t in range(n):
            copy(t, 0, (x, y), 1 - c, here).wait_recv()
            for k, chip in ((5, along_x), (6, along_y), (7, across)):
                copy(t, k, chip, 1 - c, here).wait_recv()
        for cp in sends:
            cp.wait_send()

        a_in, a_uq, a_ukv = bufs
        for d in range(N_DEV):
            lo, hi = shard * d, shard * (d + 1)
            if lo < k_lo:
                win_ref[:, lo:min(hi, k_lo)] = a_in[d, :, 0:min(hi, k_lo) - lo]
            if lo < k_hi and hi > k_lo:
                kr = a_in[d, :, k_lo - lo:k_hi - lo]
                for rep in range(4):
                    win_ref[:, C_KR + MLA_ROPE * rep:C_KR + MLA_ROPE * (rep + 1)] = kr
            if hi > k_hi:
                src = max(lo, k_hi)
                win_ref[:, src + C_GA - k_hi:hi + C_GA - k_hi] = a_in[d, :, src - lo:hi - lo]
        for h in range(HEADS):
            wuq_ref[:, MLA_NOPE * h:MLA_NOPE * (h + 1)] = a_uq[h, :, :MLA_NOPE]
            wuq_ref[:, HW + MLA_ROPE * h:HW + MLA_ROPE * (h + 1)] = a_uq[h, :, MLA_NOPE:]
            wukv_ref[:, MLA_NOPE * h:MLA_NOPE * (h + 1)] = a_ukv[h, :, :MLA_NOPE]
            wukv_ref[:, HW + MLA_V * h:HW + MLA_V * (h + 1)] = a_ukv[h, :, MLA_NOPE:]

    vmem = pl.BlockSpec(memory_space=pltpu.VMEM)
    return _pcall(
        body, name="gather_weights",
        in_specs=[vmem] * n, out_specs=[vmem] * n,
        out_shape=[jax.ShapeDtypeStruct((D_MODEL, C_END), BF16), jax.ShapeDtypeStruct((Q_RANK, QW), BF16),
                   jax.ShapeDtypeStruct((KV_RANK, 2 * HW), BF16)],
        scratch_shapes=[pltpu.VMEM((N_DEV,) + s, BF16) for s in SHARD_SHAPES[:n]]
        + [pltpu.SemaphoreType.DMA((n, 8)), pltpu.SemaphoreType.DMA((n, 8))],
        compiler_params=_cparams(),
    )(*shards)


def _gather_w_out_rider(w_out):
    def copies(full_ref, stage, send_sems, recv_sems):
        me = _me()[3]
        out = []
        for k in range(1, N_DEV):
            peer, pidx = _peer(k)
            send = pltpu.make_async_remote_copy(
                src_ref=stage, dst_ref=full_ref.at[me], send_sem=send_sems.at[k - 1], recv_sem=recv_sems.at[k - 1],
                device_id=peer, device_id_type=MESH_ID)
            recv = pltpu.make_async_remote_copy(
                src_ref=stage, dst_ref=full_ref.at[pidx], send_sem=send_sems.at[k - 1], recv_sem=recv_sems.at[k - 1],
                device_id=peer, device_id_type=MESH_ID)
            out.append((send, recv))
        return out

    def start(ins, outs, scr):
        stage, send_sems, recv_sems, own_sem = scr
        stage[...] = ins[0][...].astype(BF16)
        pltpu.make_async_copy(stage, outs[0].at[_me()[3]], own_sem).start()
        for send, _ in copies(outs[0], stage, send_sems, recv_sems):
            send.start()

    def finish(ins, outs, scr):
        stage, send_sems, recv_sems, own_sem = scr
        pltpu.make_async_copy(stage, outs[0].at[_me()[3]], own_sem).wait()
        pairs = copies(outs[0], stage, send_sems, recv_sems)
        for _, recv in pairs:
            recv.wait_recv()
        for send, _ in pairs:
            send.wait_send()

    shape = SHARD_SHAPES[3]
    return Rider(
        args=[w_out], in_specs=[pl.BlockSpec(shape, lambda t, *_: (0, 0))],
        out_shape=[jax.ShapeDtypeStruct((N_DEV,) + shape, BF16)], out_specs=[pl.BlockSpec(memory_space=pl.ANY)],
        scratch=[pltpu.VMEM(shape, BF16), pltpu.SemaphoreType.DMA((N_DEV - 1,)), pltpu.SemaphoreType.DMA((N_DEV - 1,)),
                 pltpu.SemaphoreType.DMA],
        start=start, finish=finish)


def _scatter_g_out_rider(blocks):
    def copies(src_ref, dst_ref, send_sems, recv_sems):
        out = []
        for k in range(1, N_DEV):
            peer, pidx = _peer(k)
            out.append(pltpu.make_async_remote_copy(
                src_ref=src_ref.at[pidx], dst_ref=dst_ref.at[k], send_sem=send_sems.at[k - 1],
                recv_sem=recv_sems.at[k - 1], device_id=peer, device_id_type=MESH_ID))
        return out

    def start(ins, outs, scr):
        send_sems, recv_sems, own_sem = scr
        pltpu.make_async_copy(ins[0].at[_me()[3]], outs[0].at[0], own_sem).start()
        for cp in copies(ins[0], outs[0], send_sems, recv_sems):
            cp.start()

    def finish(ins, outs, scr):
        send_sems, recv_sems, own_sem = scr
        pltpu.make_async_copy(ins[0].at[_me()[3]], outs[0].at[0], own_sem).wait()
        for cp in copies(ins[0], outs[0], send_sems, recv_sems):
            cp.wait()

    hbm = pl.BlockSpec(memory_space=pl.ANY)
    return Rider(
        args=[blocks], in_specs=[hbm], out_shape=[jax.ShapeDtypeStruct(blocks.shape, blocks.dtype)], out_specs=[hbm],
        scratch=[pltpu.SemaphoreType.DMA((N_DEV - 1,)), pltpu.SemaphoreType.DMA((N_DEV - 1,)), pltpu.SemaphoreType.DMA],
        start=start, finish=finish)


def _adamw(w, g, m, v):
    m = ADAM_B1 * m + (1.0 - ADAM_B1) * g
    v = ADAM_B2 * v + (1.0 - ADAM_B2) * jnp.square(g)
    m_hat = m / (1.0 - ADAM_B1 ** ADAM_STEP)
    v_hat = v / (1.0 - ADAM_B2 ** ADAM_STEP)
    delta = -ADAM_LR * (m_hat / (jnp.sqrt(v_hat) + ADAM_EPS) + ADAM_WD * w)
    return delta, m, v


def _chunk_rows(count, cols):
    rows = (16 * 8 * LANES) // (-(-cols // LANES) * LANES)
    while count % rows:
        rows //= 2
    return rows


SMALL_ROWS = (0, 1, 3, 4)
LOSS_ROW = 2


def _reduce_grads_rider(grads3, arrived, small_parts, steps):
    n = len(grads3)
    shapes = [tuple(g.shape[1:]) for g in grads3]
    with_small = arrived is not None
    start_step, first_round, second_round = steps
    n_held = sum(isinstance(g, jax.ShapeDtypeStruct) for g in grads3)
    held, handed = grads3[:n_held], grads3[n_held:]
    assert not any(isinstance(g, jax.ShapeDtypeStruct) for g in handed)

    class Refs:
        def __init__(self, ins, outs, scr):
            self.g3, self.gsum = list(scr[0:n_held]) + list(ins[0:n - n_held]), outs[0:n]
            ins, scr = ins[n - n_held:], scr[n_held:]
            if with_small:
                self.arr, self.sp_wide, self.sp_q = ins
                self.gsum_out, self.ssum = outs[n:n + 2]
            self.own, self.sib, self.part = scr[0:n], scr[n:2 * n], scr[2 * n:3 * n]
            self.in_a, self.out_b, self.in_b = scr[3 * n:4 * n], scr[4 * n:5 * n], scr[5 * n:6 * n]
            self.rsmall = scr[6 * n]
            (self.loc_sems, self.d2d_send, self.d2d_recv, self.a_send, self.a_recv, self.b_send, self.b_recv,
             self.sm_send, self.sm_recv) = scr[6 * n + 1:]
            self.x, self.y, self.c, self.me = _me()
            self.along_x, self.along_y = (1 - self.x, self.y), (self.x, 1 - self.y)
            self.across = (1 - self.x, 1 - self.y)

        def small(self):
            if not with_small:
                return []
            return [pltpu.make_async_remote_copy(
                src_ref=self.rsmall.at[0], dst_ref=self.rsmall.at[k], send_sem=self.sm_send.at[k - 1],
                recv_sem=self.sm_recv.at[k - 1], device_id=_peer(k)[0], device_id_type=MESH_ID)
                for k in range(1, N_DEV)]

        def level1(self):
            local, to_sib = [], []
            for t in range(n):
                for q in range(4):
                    local.append(pltpu.make_async_copy(
                        self.g3[t].at[2 * q + self.c], self.own[t].at[q], self.loc_sems.at[t, q]))
                    to_sib.append(pltpu.make_async_remote_copy(
                        src_ref=self.g3[t].at[2 * q + 1 - self.c], dst_ref=self.sib[t].at[q],
                        send_sem=self.d2d_send.at[t, q], recv_sem=self.d2d_recv.at[t, q],
                        device_id=(self.x, self.y, 1 - self.c), device_id_type=MESH_ID))
            return local, to_sib

        def round_a(self):
            out = []
            for t in range(n):
                half = shapes[t][0] // 2
                for k, (to, chip, h) in enumerate([(self.along_x, self.along_x, 0), (self.along_x, self.across, 0),
                                                   (self.along_y, self.along_y, 1), (self.along_y, self.across, 1)]):
                    out.append(pltpu.make_async_remote_copy(
                        src_ref=self.part[t].at[2 * chip[0] + chip[1], pl.ds(h * half, half), :],
                        dst_ref=self.in_a[t].at[k], send_sem=self.a_send.at[t, k], recv_sem=self.a_recv.at[t, k],
                        device_id=(*to, self.c), device_id_type=MESH_ID))
            return out

        def round_b(self):
            return [pltpu.make_async_remote_copy(
                src_ref=self.out_b[t].at[k], dst_ref=self.in_b[t].at[k], send_sem=self.b_send.at[t, k],
                recv_sem=self.b_recv.at[t, k], device_id=(*to, self.c), device_id_type=MESH_ID)
                for t in range(n) for k, to in enumerate([self.along_y, self.along_x])]

    def chunks(rows, count, fn):
        def step(i, carry):
            fn(pl.multiple_of(i * rows, rows))
            return carry

        lax.fori_loop(0, count // rows, step, 0)

    def start(*refs):
        r = Refs(*refs)
        if with_small:
            r.rsmall[0] = r.sp_wide[...]
            for k, row in enumerate(SMALL_ROWS[2:]):
                r.rsmall[0, row:row + 1, 0:Q_RANK] = r.sp_q[k:k + 1, :]
        local, to_sib = r.level1()
        for cp in r.small() + local + to_sib:
            cp.start()

    def begin_rounds(*refs):
        r = Refs(*refs)
        local, to_sib = r.level1()
        for cp in local:
            cp.wait()
        for cp in to_sib:
            cp.wait_recv()
        my_chip = 2 * r.x + r.y
        for t in range(n):
            rows = _chunk_rows(*shapes[t])

            def pair_sums(at, t=t, rows=rows):
                sl = pl.ds(at, rows)
                for q in range(4):
                    r.part[t][q, sl, :] = (r.own[t][q, sl, :].astype(F32) + r.sib[t][q, sl, :].astype(F32)).astype(BF16)
                r.gsum[t][sl, :] = r.own[t][my_chip, sl, :].astype(F32) + r.sib[t][my_chip, sl, :].astype(F32)

            chunks(rows, shapes[t][0], pair_sums)
        for cp in r.round_a():
            cp.start()

    def pass_on(*refs):
        r = Refs(*refs)
        for cp in r.round_a():
            cp.wait_recv()
        q_x, q_y = 2 * r.along_x[0] + r.along_x[1], 2 * r.along_y[0] + r.along_y[1]
        for t in range(n):
            half = shapes[t][0] // 2
            rows = _chunk_rows(half, shapes[t][1])

            def add(at, t=t, rows=rows, half=half):
                lo, hi = pl.ds(at, rows), pl.ds(half + at, rows)
                r.gsum[t][lo, :] = r.gsum[t][lo, :] + r.in_a[t][0, lo, :].astype(F32)
                r.out_b[t][0, lo, :] = (r.part[t][q_y, lo, :].astype(F32) + r.in_a[t][1, lo, :].astype(F32)).astype(BF16)
                r.gsum[t][hi, :] = r.gsum[t][hi, :] + r.in_a[t][2, lo, :].astype(F32)
                r.out_b[t][1, lo, :] = (r.part[t][q_x, hi, :].astype(F32) + r.in_a[t][3, lo, :].astype(F32)).astype(BF16)

            chunks(rows, half, add)
        for cp in r.round_b():
            cp.start()

    def finish(*refs):
        r = Refs(*refs)
        if with_small:
            rows_out = _chunk_rows(*SHARD_SHAPES[3])

            def add_arrived(at):
                sl = pl.ds(at, rows_out)
                g = r.arr[0, sl, :].astype(F32)
                for k in range(1, N_DEV):
                    g = g + r.arr[k, sl, :].astype(F32)
                r.gsum_out[sl, :] = g

            chunks(rows_out, SHARD_SHAPES[3][0], add_arrived)
        passed = r.round_b()
        for cp in passed:
            cp.wait_recv()
        for t in range(n):
            half = shapes[t][0] // 2
            rows = _chunk_rows(half, shapes[t][1])

            def add(at, t=t, rows=rows, half=half):
                lo, hi = pl.ds(at, rows), pl.ds(half + at, rows)
                r.gsum[t][lo, :] = r.gsum[t][lo, :] + r.in_b[t][0, lo, :].astype(F32)
                r.gsum[t][hi, :] = r.gsum[t][hi, :] + r.in_b[t][1, lo, :].astype(F32)

            chunks(rows, half, add)
        small = r.small()
        for cp in small:
            cp.wait_recv()
        if with_small:
            tot = r.rsmall[r.me]
            for d in range(1, N_DEV):
                tot = tot + r.rsmall[jnp.bitwise_xor(r.me, d)]
            r.ssum[...] = tot
        for cp in small + r.level1()[1] + r.round_a() + passed:
            cp.wait_send()

    hbm = pl.BlockSpec(memory_space=pl.ANY)
    dma = pltpu.SemaphoreType.DMA

    def whole(shape):
        return pl.BlockSpec(shape, lambda i: (0,) * len(shape))

    def halves(slots):
        return [pltpu.VMEM((slots, s[0] // 2, s[1]), BF16) for s in shapes]

    small_args = [arrived, *small_parts] if with_small else []
    out_shapes = shapes + ([SHARD_SHAPES[3], (8, D_MODEL)] if with_small else [])
    return Rider(
        args=list(handed) + small_args,
        in_specs=[hbm] * len(handed) + [whole(a.shape) for a in small_args],
        out_shape=[jax.ShapeDtypeStruct(s, F32) for s in out_shapes], out_specs=[whole(s) for s in out_shapes],
        scratch=[pltpu.VMEM(g.shape, g.dtype) for g in held]
        + [pltpu.VMEM((4,) + s, BF16) for _ in range(3) for s in shapes]
        + halves(4) + halves(2) + halves(2)
        + [pltpu.VMEM((N_DEV, 8, D_MODEL), F32), dma((n, 4)), dma((n, 4)), dma((n, 4)), dma((n, 4)), dma((n, 4)),
           dma((n, 2)), dma((n, 2)), dma((N_DEV - 1,)), dma((N_DEV - 1,))],
        start=start, finish=finish, stages=((first_round, begin_rounds), (second_round, pass_on)),
        start_step=start_step, shared=n_held)


def _adamw_update(grads, small_grad, wmv, small_wmv):
    n_small = len(small_wmv)
    flat_grads = [g for pieces in grads for g in pieces]
    n_g = len(flat_grads)

    def body(*refs):
        g_refs, sg_ref, refs = iter(refs[0:n_g]), refs[n_g], refs[n_g + 1:]
        wmv_refs = [refs[3 * t:3 * t + 3] for t in range(4)]
        swmv_refs = [refs[12 + 3 * t:15 + 3 * t] for t in range(n_small)]
        outs = refs[12 + 3 * n_small:]
        out_refs = [outs[4 * t:4 * t + 4] for t in range(4)]
        sout_refs = [outs[16 + 4 * t:20 + 4 * t] for t in range(n_small)]
        loss_ref = outs[16 + 4 * n_small]
        for t, (w_ref, m_ref, v_ref) in enumerate(swmv_refs):
            g = sg_ref[SMALL_ROWS[t]:SMALL_ROWS[t] + 1, :w_ref.shape[1]]
            delta, m, v = _adamw(w_ref[...], g, m_ref[...], v_ref[...])
            sout_refs[t][0][...], sout_refs[t][1][...], sout_refs[t][2][...], sout_refs[t][3][...] = g, delta, m, v
        loss_ref[...] = (0.5 / D_MODEL) * jnp.sum(sg_ref[LOSS_ROW:LOSS_ROW + 1, :], axis=1, keepdims=True)
        for t in range(4):
            rows = ADAM_ROWS[t]
            w_ref, m_ref, v_ref = wmv_refs[t]
            g_out, d_out, m_out, v_out = out_refs[t]

            def step(i, carry, g_ref, first, rows=rows, w_ref=w_ref, m_ref=m_ref, v_ref=v_ref,
                     g_out=g_out, d_out=d_out, m_out=m_out, v_out=v_out):
                at = pl.multiple_of(i * rows, rows)
                r = pl.ds(pl.multiple_of(first + at, rows), rows)
                g = g_ref[pl.ds(at, rows), :]
                delta, m, v = _adamw(w_ref[r, :], g, m_ref[r, :], v_ref[r, :])
                g_out[r, :], d_out[r, :], m_out[r, :], v_out[r, :] = g, delta, m, v
                return carry

            first = 0
            for piece in grads[t]:
                lax.fori_loop(0, piece.shape[0] // rows, functools.partial(step, g_ref=next(g_refs), first=first), 0)
                first += piece.shape[0]

    vmem = pl.BlockSpec(memory_space=pltpu.VMEM)
    flat_wmv = [a for trio in wmv for a in trio]
    flat_small = [a for trio in small_wmv for a in trio]
    out_shape = ([jax.ShapeDtypeStruct(s, F32) for s in SHARD_SHAPES for _ in range(4)]
                 + [jax.ShapeDtypeStruct(trio[0].shape, F32) for trio in small_wmv for _ in range(4)]
                 + [jax.ShapeDtypeStruct((1, 1), F32)])
    return _pcall(
        body, name="adamw",
        in_specs=[vmem] * (n_g + 1 + len(flat_wmv) + len(flat_small)), out_specs=[vmem] * len(out_shape),
        out_shape=out_shape,
        compiler_params=_cparams(),
    )(*flat_grads, small_grad, *flat_wmv, *flat_small)


def kernel(x, w_in, q_norm_g, kv_norm_g, w_uq, w_ukv, w_out, ln_g, ln_b, loss_target, m_w_in, m_q_norm_g, m_kv_norm_g, m_w_uq, m_w_ukv, m_w_out, m_ln_g, m_ln_b, v_w_in, v_q_norm_g, v_kv_norm_g, v_w_uq, v_w_ukv, v_w_out, v_ln_g, v_ln_b):
    w_in_r, w_uq_r, w_ukv_r = _all_gather_weights([w_in, w_uq, w_ukv])
    grad_x, (g_in_rest, g_in_first, g_uq, g_ukv, g_out, g_small) = _local_step(
        x[0], loss_target[0], w_in_r, w_uq_r, w_ukv_r, _gather_w_out_rider(w_out), _scatter_g_out_rider,
        _reduce_grads_rider, q_norm_g, kv_norm_g, ln_g, ln_b)
    row = lambda a: a.reshape(1, -1)
    small_wmv = [(row(ln_g), row(m_ln_g), row(v_ln_g)), (row(ln_b), row(m_ln_b), row(v_ln_b)),
                 (row(q_norm_g), row(m_q_norm_g), row(v_q_norm_g)), (row(kv_norm_g), row(m_kv_norm_g), row(v_kv_norm_g))]
    wmv = [(w_in, m_w_in, v_w_in), (w_uq, m_w_uq, v_w_uq), (w_ukv, m_w_ukv, v_w_ukv), (w_out, m_w_out, v_w_out)]
    res = _adamw_update([[g_in_first, g_in_rest], [g_uq], [g_ukv], [g_out]], g_small, wmv, small_wmv)
    big = [res[4 * t:4 * t + 4] for t in range(4)]
    small = [[a.reshape(-1) for a in res[16 + 4 * t:20 + 4 * t]] for t in range(4)]
    loss = res[32].reshape(())

    def group(kind):
        return (big[0][kind], small[2][kind], small[3][kind], big[1][kind], big[2][kind], big[3][kind],
                small[0][kind], small[1][kind])

    return (loss, grad_x[None], *group(0), *group(1), *group(2), *group(3))
```

```python
import functools
from typing import Callable, NamedTuple

import numpy as np
import jax
import jax.numpy as jnp
from jax import lax
from jax.experimental import pallas as pl
from jax.experimental.pallas import tpu as pltpu

F32 = jnp.float32
BF16 = jnp.bfloat16

D_MODEL = 1024
ROPE_THETA = 500000.0
NEG = -1e30
RMS_EPS = 1e-6
LN_EPS = 1e-5
HEADS = 8
MLA_NOPE = 64
MLA_ROPE = 32
MLA_V = 64
Q_RANK = 384
KV_RANK = 256
DIL_HEAD = 64
DIL_ROT = 16
DIL_CONFIGS = ((128, 1), (512, 4), (2048, 16))
DIL_NEAR = 512
HW = HEADS * 64
QW = HW + HEADS * MLA_ROPE
IN_SPLITS = (Q_RANK, KV_RANK, MLA_ROPE, HW, HW, HW, HW, HW)
IN_WIDTH = sum(IN_SPLITS)
ALPHA = 2.0 ** 0.25
MLA_SCALE = (MLA_NOPE + MLA_ROPE) ** -0.5
DIL_SCALE = DIL_HEAD ** -0.5
LOG2E = 1.4426950408889634
LN2 = 0.6931471805599453

ADAM_LR = 0.001
ADAM_B1 = 0.9
ADAM_B2 = 0.999
ADAM_EPS = 1e-08
ADAM_WD = 0.01
ADAM_STEP = 10

N_DEV = 8
LANES = 128
VMEM_LIMIT = 56 * 1024 * 1024
BLOCK_TOKENS = 512
BLOCK_MLA = 512
BLOCK_DIL = 512
GRAD_W_IN_CUT = 384

C_CQ, C_CKV, C_KR, C_GA, C_QB, C_KB, C_VB, C_GB, C_END = 0, 384, 640, 768, 1280, 1792, 2304, 2816, 3328

NT = (((1,), (1,)), ((), ()))
TN = (((0,), (0,)), ((), ()))


def _pcall(body, **kw):
    return pl.pallas_call(body, **kw)


def _cparams(**kw):
    return pltpu.CompilerParams(vmem_limit_bytes=VMEM_LIMIT, **kw)


def _rope_tables(seq):
    def tabs(dim, period):
        half = dim // 2
        inv = np.float32(ROPE_THETA) ** (-np.arange(0, dim, 2, dtype=np.float32) / np.float32(dim))
        ang = np.arange(seq, dtype=np.float32)[:, None] * inv.astype(np.float32)[None, :]
        cos, sin = np.cos(ang).astype(np.float32), np.sin(ang).astype(np.float32)
        j = np.arange(LANES) % period
        f = j % half
        c = np.where(j < dim, cos[:, f], np.float32(1.0))
        s1 = np.where(j < half, -sin[:, f], np.float32(0.0))
        s2 = np.where((j >= half) & (j < dim), sin[:, f], np.float32(0.0))
        return [c, s1, s2]
    return np.stack(tabs(MLA_ROPE, MLA_ROPE) + tabs(DIL_ROT, DIL_HEAD)).astype(np.float32)


def _rope(t, c, s1, s2, half):
    return t * c + pltpu.roll(t, LANES - half, 1) * s1 + pltpu.roll(t, half, 1) * s2


def _rope_t(d, c, s1, s2, half):
    return d * c + pltpu.roll(d * s1, half, 1) + pltpu.roll(d * s2, LANES - half, 1)


def _rope_wide(fn, t, c, s1, s2, half):
    return jnp.concatenate(
        [fn(t[:, i:i + LANES], c, s1, s2, half) for i in range(0, t.shape[1], LANES)], axis=1)


def _mla_bias_t(blk):
    a = np.arange(blk)
    causal = np.where(a[:, None] <= a[None, :], 0.0, NEG)
    return np.stack([np.zeros((blk, blk)), causal]).astype(np.float32)


def _dil_bias_t(blk, reach):
    a = np.arange(blk)
    out = []
    for off in range(-(-reach // blk) + 1):
        delta = blk * off + a[None, :] - a[:, None]
        mult = np.zeros((blk, blk))
        for window, dil in DIL_CONFIGS:
            mult += (delta >= 0) & (delta % dil == 0) & (delta <= min(window, reach))
        out.append(np.where(mult > 0, np.log2(np.maximum(mult, 1.0)), NEG))
    return np.stack(out).astype(np.float32)


def _dil_far_bias_t(length):
    window, dil = DIL_CONFIGS[-1]
    a = np.arange(length)
    steps_back = a[None, :] - a[:, None]
    seen = (steps_back * dil > DIL_NEAR) & (steps_back * dil <= window)
    return np.where(seen, 0.0, NEG).astype(np.float32)[None]


def _lanes_to_classes(a, dil):
    h, s = a.shape
    return a.reshape(h, s // dil, dil).transpose(0, 2, 1).reshape(h, s)


def _lanes_from_classes(a, dil):
    h, s = a.shape
    return a.reshape(h, dil, s // dil).transpose(0, 2, 1).reshape(h, s)


def _steps(nq, span, by_key, diag_only_bias):
    rows = []
    if by_key:
        for ki in range(nq):
            hi = min(nq - 1, ki + span)
            for qi in range(ki, hi + 1):
                rows.append((qi, ki, int(qi == ki), int(qi == hi)))
    else:
        for qi in range(nq):
            lo = max(0, qi - span)
            for ki in range(lo, qi + 1):
                rows.append((qi, ki, int(ki == lo), int(ki == qi)))
    arr = np.array(rows, dtype=np.int32)
    off = arr[:, 0] - arr[:, 1]
    bias_idx = (off == 0).astype(np.int32) if diag_only_bias else off.astype(np.int32)
    return [jnp.asarray(v) for v in (arr[:, 0], arr[:, 1], bias_idx, arr[:, 2], arr[:, 3])]


def _by_class(val, out_ref, lanes_sc):
    n_cls, per = out_ref.shape[0], out_ref.shape[1]
    for c in range(val.shape[1] // LANES):
        lanes_sc[c] = val[:, LANES * c:LANES * (c + 1)]
        for r in range(n_cls):
            rows = lanes_sc.at[c][pl.ds(r, per, stride=n_cls), :]
            out_ref[r, :, LANES * c:LANES * (c + 1)] = rows.astype(out_ref.dtype)


def _in_sequence(ref, lanes_sc):
    n_cls, per, width = ref.shape
    for c in range(width // LANES):
        for r in range(n_cls):
            lanes_sc.at[c][pl.ds(r, per, stride=n_cls), :] = ref[r, :, LANES * c:LANES * (c + 1)].astype(F32)
    return jnp.concatenate([lanes_sc[c] for c in range(width // LANES)], axis=1)


def _fwd_proj(x, w_in_r, w_uq_r, w_ukv_r, qg, kvg, tabs, bt, n_cls):
    seq = x.shape[0]

    def body(x_ref, win_ref, wuq_ref, wukv_ref, qg_ref, kvg_ref, tab_ref,
             cq_ref, ckv_ref, qn_ref, kvn_ref, qcat_ref, kn_ref, kpe_ref, v_ref,
             ga_ref, gb_ref, qb_ref, kb_ref, vb_ref, knt_ref, kpet_ref, vt_ref, kbt_ref, vbt_ref,
             qbc_ref, kbc_ref, vbc_ref, lanes_sc):
        xb = x_ref[...].astype(BF16)

        def proj(lo, hi):
            return jnp.dot(xb, win_ref[:, lo:hi], preferred_element_type=F32)

        m_tabs = (tab_ref[0], tab_ref[1], tab_ref[2])
        d_tabs = (tab_ref[3], tab_ref[4], tab_ref[5])

        def use_cq(cq):
            cq_ref[...] = cq
            qn = (cq * lax.rsqrt(jnp.mean(cq * cq, axis=1, keepdims=True) + RMS_EPS) * qg_ref[...]).astype(BF16)
            qn_ref[...] = qn
            q = jnp.dot(qn, wuq_ref[...], preferred_element_type=F32)
            qcat_ref[:, :HW] = (q[:, :HW] * (MLA_SCALE * LOG2E)).astype(BF16)
            qcat_ref[:, HW:] = (
                _rope_wide(_rope, q[:, HW:], *m_tabs, MLA_ROPE // 2) * (MLA_SCALE * LOG2E)).astype(BF16)

        def use_ckv(ckv):
            ckv_ref[...] = ckv
            kvn = (ckv * lax.rsqrt(jnp.mean(ckv * ckv, axis=1, keepdims=True) + RMS_EPS) * kvg_ref[...]).astype(BF16)
            kvn_ref[...] = kvn
            kv = jnp.dot(kvn, wukv_ref[...], preferred_element_type=F32)
            kn_ref[...] = kv[:, :HW].astype(BF16)
            v_ref[...] = kv[:, HW:].astype(BF16)
            knt_ref[...] = kv[:, :HW].T.astype(BF16)
            vt_ref[...] = kv[:, HW:].T.astype(BF16)

        def use_kr(kr):
            kpe = _rope(kr, *m_tabs, MLA_ROPE // 2)
            kpe_ref[...] = kpe.astype(BF16)
            kpet_ref[...] = kpe.T[:MLA_ROPE, :].astype(BF16)

        def use_ga(ga):
            ga_ref[...] = ga

        def use_qb(qb):
            qb = _rope_wide(_rope, qb, *d_tabs, DIL_ROT // 2) * (DIL_SCALE * LOG2E)
            qb_ref[...] = qb.astype(BF16)
            _by_class(qb, qbc_ref, lanes_sc)

        def use_kb(kb):
            kb = _rope_wide(_rope, kb, *d_tabs, DIL_ROT // 2)
            kb_ref[...] = kb.astype(BF16)
            kbt_ref[...] = kb.T.astype(BF16)
            _by_class(kb, kbc_ref, lanes_sc)

        def use_vb(vb):
            vb_ref[...] = vb.astype(BF16)
            vbt_ref[...] = vb.T.astype(BF16)
            _by_class(vb, vbc_ref, lanes_sc)

        def use_gb(gb):
            gb_ref[...] = gb

        pieces = [(C_CQ, C_CKV, use_cq), (C_CKV, C_KR, use_ckv), (C_KR, C_GA, use_kr), (C_GA, C_QB, use_ga),
                  (C_QB, C_KB, use_qb), (C_KB, C_VB, use_kb), (C_VB, C_GB, use_vb), (C_GB, C_END, use_gb)]
        ahead = proj(*pieces[0][:2])
        for n, (_, _, use) in enumerate(pieces):
            cur = ahead
            if n + 1 < len(pieces):
                ahead = proj(*pieces[n + 1][:2])
            use(cur)

    def tok(width):
        return pl.BlockSpec((bt, width), lambda i: (i, 0))

    def tok_t(height):
        return pl.BlockSpec((height, bt), lambda i: (0, i))

    def full(a):
        return pl.BlockSpec(a.shape, lambda i: (0,) * a.ndim)

    outs = [(Q_RANK, F32), (KV_RANK, F32), (Q_RANK, BF16), (KV_RANK, BF16), (QW, BF16), (HW, BF16),
            (LANES, BF16), (HW, BF16), (HW, F32), (HW, F32), (HW, BF16), (HW, BF16), (HW, BF16)]
    outs_t = [HW, MLA_ROPE, HW, HW, HW]
    by_class = pl.BlockSpec((n_cls, bt // n_cls, HW), lambda i: (0, i, 0))
    return _pcall(
        body, name="fwd_proj", grid=(seq // bt,),
        in_specs=[tok(D_MODEL), full(w_in_r), full(w_uq_r), full(w_ukv_r), full(qg), full(kvg),
                  pl.BlockSpec((6, bt, LANES), lambda i: (0, i, 0))],
        out_specs=[tok(w) for w, _ in outs] + [tok_t(h) for h in outs_t] + [by_class] * 3,
        out_shape=[jax.ShapeDtypeStruct((seq, w), dt) for w, dt in outs]
        + [jax.ShapeDtypeStruct((h, seq), BF16) for h in outs_t]
        + [jax.ShapeDtypeStruct((n_cls, seq // n_cls, HW), BF16)] * 3,
        scratch_shapes=[pltpu.VMEM((HW // LANES, bt, LANES), F32)],
        compiler_params=_cparams(dimension_semantics=("arbitrary",)),
    )(x, w_in_r, w_uq_r, w_ukv_r, qg, kvg, tabs)


def _head_masks(lane, h):
    e, g = h % 2, h % 4
    me = (lane >= 64 * e) & (lane < 64 * e + 64)
    mr = (lane >= 32 * g) & (lane < 32 * g + 32)
    return me, mr


def _masked(mask, a):
    return jnp.where(mask, a, jnp.zeros_like(a))


def _pair_operands(q_ref, k_ref, kpe_ref, lane, j, ks=slice(None), qs=slice(None)):
    cols = slice(LANES * j, LANES * (j + 1))
    qc = q_ref[qs, cols]
    kj = k_ref[ks, cols]
    kes = []
    for h in (2 * j, 2 * j + 1):
        me, mr = _head_masks(lane, h)
        ke = _masked(me, kj)
        if kpe_ref is not None:
            ke = jnp.concatenate([ke, _masked(mr, kpe_ref[ks, :])], axis=1)
        kes.append(ke)
    if kpe_ref is not None:
        qc = jnp.concatenate([qc, q_ref[qs, HW + LANES * (j // 2):HW + LANES * (j // 2 + 1)]], axis=1)
    return qc, kes


def _tile_variants(bias_t):
    out = {}
    for i, tile in enumerate(np.asarray(bias_t)):
        h = tile.shape[0] // 2
        skip = 1 if (tile[h:, :h] == NEG).all() else 2 if (tile[:h, h:] == NEG).all() else 0
        out[i] = (bool((tile != 0).any()), skip)
    return out


def _tile_parts(blk, skip):
    lo, hi, full = slice(0, blk // 2), slice(blk // 2, blk), slice(0, blk)
    return {0: [(full, full)], 1: [(lo, full), (hi, hi)], 2: [(hi, full), (lo, lo)]}[skip]


class Rider(NamedTuple):
    args: list
    in_specs: list
    out_shape: list
    out_specs: list
    scratch: list
    start: Callable
    finish: Callable
    stages: tuple = ()
    start_step: int = 0
    shared: int = 0


def _ride_along(body, ride, n_prefetch, n_in, n_out, n_scratch, n_steps):
    if ride is None:
        return body

    def wrapped(*refs):
        pre, rest = refs[:n_prefetch], refs[n_prefetch:]
        a = n_in
        b = a + len(ride.args)
        c = b + n_out
        d = c + len(ride.out_shape)
        e = d + n_scratch
        mine = (rest[a:b], rest[c:d], rest[e:])
        t = pl.program_id(0)
        pl.when(t == ride.start_step)(lambda: ride.start(*mine))
        for at, stage in ride.stages:
            pl.when(t == at)(functools.partial(stage, *mine))
        body(*pre, *rest[:a], *rest[b:c], *rest[d:e + ride.shared])
        pl.when(t == n_steps - 1)(lambda: ride.finish(*mine))

    return wrapped


def _attn_fwd(name, q, k, kpe, vt, bias_t, steps, blk, ride=None, v_token_major=False):
    seq = q.shape[0]
    mla = kpe is not None
    n_steps = int(steps[0].shape[0])
    variants = _tile_variants(bias_t)

    def body(qi_r, ki_r, bi_r, fi_r, la_r, *refs):
        if mla:
            q_ref, k_ref, kpe_ref, vt_ref, b_ref, o_ref, lse_ref, m_sc, l_sc, acc_sc, st_sc = refs
        else:
            q_ref, k_ref, vt_ref, b_ref, o_ref, lse_ref, m_sc, l_sc, acc_sc, st_sc = refs
        t = pl.program_id(0)

        @pl.when(fi_r[t] == 1)
        def _():
            m_sc[...] = jnp.full(m_sc.shape, NEG, F32)
            l_sc[...] = jnp.zeros(l_sc.shape, F32)
            acc_sc[...] = jnp.zeros(acc_sc.shape, F32)

        lane = lax.broadcasted_iota(jnp.int32, (1, LANES), 1)
        if v_token_major:
            vt_all = vt_ref[...].astype(F32).T.astype(BF16)
            vt_rows = lambda rows, ks: vt_all[rows, ks]
        else:
            vt_rows = lambda rows, ks: vt_ref[rows, ks]

        def tile_pass(ks, qs, with_bias):
            nk, nq = ks.stop - ks.start, qs.stop - qs.start
            ones = jnp.ones((16, nk), BF16)

            def pair_scores(j):
                qc, kes = _pair_operands(q_ref, k_ref, kpe_ref if mla else None, lane, j, ks, qs)
                st = lax.dot_general(jnp.concatenate(kes, axis=0), qc, NT, preferred_element_type=F32)
                maxes = []
                for e in range(2):
                    se = st[e * nk:(e + 1) * nk]
                    if with_bias:
                        se = se + b_ref[0, ks, qs]
                    st_sc[j % 2, e * nk:(e + 1) * nk, 0:nq] = se
                    maxes.append(jnp.max(se, axis=0, keepdims=True))
                return maxes

            def softmax_pv(h, col_max):
                st = st_sc[(h // 2) % 2, (h % 2) * nk:(h % 2 + 1) * nk, 0:nq]
                hrow = slice(h, h + 1)
                m_prev = m_sc[hrow, qs]
                m_new = jnp.maximum(m_prev, col_max)
                alpha = jnp.exp2(m_prev - m_new)
                pt = jnp.exp2(st - m_new).astype(BF16)
                m_sc[hrow, qs] = m_new
                rows = slice(64 * h, 64 * h + 64)
                res = jnp.dot(jnp.concatenate([vt_rows(rows, ks), ones], axis=0), pt, preferred_element_type=F32)
                acc_sc[rows, qs] = alpha * acc_sc[rows, qs] + res[:64]
                l_sc[hrow, qs] = alpha * l_sc[hrow, qs] + res[64:65]

            maxes = pair_scores(0)
            for j in range(HEADS // 2):
                cur = maxes
                if j + 1 < HEADS // 2:
                    maxes = pair_scores(j + 1)
                softmax_pv(2 * j, cur[0])
                softmax_pv(2 * j + 1, cur[1])

        def step(with_bias, skip):
            for ks, qs in _tile_parts(blk, skip):
                tile_pass(ks, qs, with_bias)

        for idx, (with_bias, skip) in variants.items():
            if len(variants) == 1:
                step(with_bias, skip)
            else:
                pl.when(bi_r[t] == idx)(functools.partial(step, with_bias, skip))

        @pl.when(la_r[t] == 1)
        def _():
            for h in range(HEADS):
                rows = slice(64 * h, 64 * h + 64)
                acc_sc[rows, :] = acc_sc[rows, :] / l_sc[h:h + 1, :]
            o_ref[...] = acc_sc[...].T
            lse_ref[...] = m_sc[...] + jnp.log2(l_sc[...])

    qmap = lambda t, qi, ki, bi, fi, la: (qi[t], 0)
    kmap = lambda t, qi, ki, bi, fi, la: (ki[t], 0)
    in_specs = [pl.BlockSpec((blk, q.shape[1]), qmap), pl.BlockSpec((blk, HW), kmap)]
    args = [q, k]
    if mla:
        in_specs.append(pl.BlockSpec((blk, LANES), kmap))
        args.append(kpe)
    in_specs += [pl.BlockSpec((blk, HW), kmap) if v_token_major else
                 pl.BlockSpec((HW, blk), lambda t, qi, ki, bi, fi, la: (0, ki[t])),
                 pl.BlockSpec((1, blk, blk), lambda t, qi, ki, bi, fi, la: (bi[t], 0, 0))]
    args += [vt, jnp.asarray(bias_t)]
    out_specs = [pl.BlockSpec((blk, HW), qmap), pl.BlockSpec((HEADS, blk), lambda t, qi, ki, bi, fi, la: (0, qi[t]))]
    out_shape = [jax.ShapeDtypeStruct((seq, HW), F32), jax.ShapeDtypeStruct((HEADS, seq), F32)]
    scratch = [pltpu.VMEM((HEADS, blk), F32), pltpu.VMEM((HEADS, blk), F32),
               pltpu.VMEM((HW, blk), F32), pltpu.VMEM((2, 2 * blk, blk), F32)]
    body = _ride_along(body, ride, 5, len(args), len(out_shape), len(scratch), n_steps)
    if ride is not None:
        args, in_specs = args + ride.args, in_specs + ride.in_specs
        out_specs, out_shape, scratch = out_specs + ride.out_specs, out_shape + ride.out_shape, scratch + ride.scratch
    return _pcall(
        body, name=name,
        grid_spec=pltpu.PrefetchScalarGridSpec(
            num_scalar_prefetch=5, grid=(n_steps,), in_specs=in_specs, out_specs=out_specs, scratch_shapes=scratch),
        out_shape=out_shape,
        compiler_params=_cparams(dimension_semantics=("arbitrary",)),
    )(*steps, *args)


def _attn_bwd(name, q, k, kpe, v, kt, kpet, bias_t, do, lse, dstat, steps, blk, ride=None, single_visit=False):
    assert not (single_visit and kpe is not None) and (kt is not None or single_visit)
    seq = q.shape[0]
    mla = kpe is not None
    qw = q.shape[1]
    n_steps = int(steps[0].shape[0])
    dk_dtype = BF16 if mla else F32
    variants = _tile_variants(bias_t)

    def body(qi_r, ki_r, bi_r, fi_r, la_r, *refs):
        if mla:
            (q_ref, k_ref, kpe_ref, v_ref, kt_ref, kpet_ref, b_ref, do_ref, lse_ref, d_ref,
             dq_ref, dk_ref, dkpe_ref, dv_ref, dk_sc, dkpe_sc, dv_sc, st_sc, dpt_sc) = refs
        else:
            q_ref, k_ref, v_ref, *rest = refs
            kt_ref = rest.pop(0) if kt is not None else None
            b_ref, do_ref, lse_ref, d_ref, dq_out_ref, dk_ref, dv_ref, dk_sc, dv_sc, st_sc, dpt_sc, *rest = rest
            dq_ref = rest[0] if single_visit else dq_out_ref
        t = pl.program_id(0)

        @pl.when(jnp.logical_or(t == 0, single_visit))
        def _():
            dq_ref[...] = jnp.zeros(dq_ref.shape, F32)

        @pl.when(fi_r[t] == 1)
        def _():
            dk_sc[...] = jnp.zeros(dk_sc.shape, F32)
            dv_sc[...] = jnp.zeros(dv_sc.shape, F32)
            if mla:
                dkpe_sc[...] = jnp.zeros(dkpe_sc.shape, F32)

        qi = 0 if single_visit else qi_r[t]
        lane = lax.broadcasted_iota(jnp.int32, (1, LANES), 1)
        if kt is None:
            kt_all = k_ref[...].astype(F32).T.astype(BF16)
            kt_rows = lambda rows, ks: kt_all[rows, ks]
        else:
            kt_rows = lambda rows, ks: kt_ref[rows, ks]

        def tile_pass(ks, qs, with_bias):
            nk, nq = ks.stop - ks.start, qs.stop - qs.start

            def pair_matmuls(j):
                cols = slice(LANES * j, LANES * (j + 1))
                qc, kes = _pair_operands(q_ref, k_ref, kpe_ref if mla else None, lane, j, ks, qs)
                st_sc[j % 2, 0:2 * nk, 0:nq] = lax.dot_general(
                    jnp.concatenate(kes, axis=0), qc, NT, preferred_element_type=F32)
                vj = v_ref[ks, cols]
                ves = [_masked(_head_masks(lane, h)[0], vj) for h in (2 * j, 2 * j + 1)]
                dpt_sc[j % 2, 0:2 * nk, 0:nq] = lax.dot_general(
                    jnp.concatenate(ves, axis=0), do_ref[qs, cols], NT, preferred_element_type=F32)

            def pair_grads(j):
                cols = slice(LANES * j, LANES * (j + 1))
                qj, doj = q_ref[qs, cols], do_ref[qs, cols]
                if mla:
                    qr = q_ref[qs, HW + LANES * (j // 2):HW + LANES * (j // 2 + 1)]
                pts, dsts, qms, doms = [], [], [], []
                for e in range(2):
                    h = 2 * j + e
                    me, mr = _head_masks(lane, h)
                    st = st_sc[j % 2, e * nk:(e + 1) * nk, 0:nq]
                    if with_bias:
                        st = st + b_ref[0, ks, qs]
                    pt = jnp.exp2(st - lse_ref[h:h + 1, qs])
                    dst = (pt * (dpt_sc[j % 2, e * nk:(e + 1) * nk, 0:nq] - d_ref[h:h + 1, qs])).astype(BF16)
                    pts.append(pt.astype(BF16))
                    dsts.append(dst)
                    doms.append(_masked(me, doj))
                    qm = _masked(me, qj)
                    if mla:
                        qm = jnp.concatenate([qm, _masked(mr, qr)], axis=1)
                    qms.append(qm)
                    ktl = kt_rows(slice(64 * h, 64 * h + 64), ks)
                    if mla:
                        ktl = jnp.concatenate([ktl, kpet_ref[:, ks]], axis=0)
                    dqc = jnp.dot(ktl, dst, preferred_element_type=F32)
                    dq_ref[qi, 64 * h:64 * h + 64, qs] += dqc[:64]
                    if mla:
                        dq_ref[qi, HW + MLA_ROPE * h:HW + MLA_ROPE * (h + 1), qs] += dqc[64:]
                dv_sc[ks, cols] += jnp.dot(
                    jnp.concatenate(pts, axis=1), jnp.concatenate(doms, axis=0), preferred_element_type=F32)
                dkc = jnp.dot(jnp.concatenate(dsts, axis=1), jnp.concatenate(qms, axis=0), preferred_element_type=F32)
                dk_sc[ks, cols] += dkc[:, :LANES]
                if mla:
                    dkpe_sc[ks, :] += dkc[:, LANES:]

            pair_matmuls(0)
            for j in range(HEADS // 2):
                if j + 1 < HEADS // 2:
                    pair_matmuls(j + 1)
                pair_grads(j)

        def step(with_bias, skip):
            for ks, qs in _tile_parts(blk, skip):
                tile_pass(ks, qs, with_bias)

        for idx, (with_bias, skip) in variants.items():
            if len(variants) == 1:
                step(with_bias, skip)
            else:
                pl.when(bi_r[t] == idx)(functools.partial(step, with_bias, skip))
        if single_visit:
            dq_out_ref[...] = dq_ref[0].T

        @pl.when(la_r[t] == 1)
        def _():
            dk_ref[...] = (dk_sc[...] * LN2).astype(dk_ref.dtype)
            dv_ref[...] = dv_sc[...].astype(dv_ref.dtype)
            if mla:
                dkpe_ref[...] = dkpe_sc[...] * LN2

    qmap = lambda t, qi, ki, bi, fi, la: (qi[t], 0)
    kmap = lambda t, qi, ki, bi, fi, la: (ki[t], 0)
    qmap_t = lambda t, qi, ki, bi, fi, la: (0, qi[t])
    kmap_t = lambda t, qi, ki, bi, fi, la: (0, ki[t])
    in_specs = [pl.BlockSpec((blk, qw), qmap), pl.BlockSpec((blk, HW), kmap)]
    args = [q, k]
    if mla:
        in_specs.append(pl.BlockSpec((blk, LANES), kmap))
        args.append(kpe)
    in_specs.append(pl.BlockSpec((blk, HW), kmap))
    args.append(v)
    if kt is not None:
        in_specs.append(pl.BlockSpec((HW, blk), kmap_t))
        args.append(kt)
    if mla:
        in_specs.append(pl.BlockSpec((MLA_ROPE, blk), kmap_t))
        args.append(kpet)
    in_specs += [pl.BlockSpec((1, blk, blk), lambda t, qi, ki, bi, fi, la: (bi[t], 0, 0)),
                 pl.BlockSpec((blk, HW), qmap), pl.BlockSpec((HEADS, blk), qmap_t), pl.BlockSpec((HEADS, blk), qmap_t)]
    args += [jnp.asarray(bias_t), do, lse, dstat]
    dq_shape = (seq // blk, qw, blk)
    if single_visit:
        out_specs, out_shape = [pl.BlockSpec((blk, qw), qmap)], [jax.ShapeDtypeStruct((seq, qw), F32)]
    else:
        out_specs = [pl.BlockSpec(dq_shape, lambda t, qi, ki, bi, fi, la: (0, 0, 0))]
        out_shape = [jax.ShapeDtypeStruct(dq_shape, F32)]
    out_specs.append(pl.BlockSpec((blk, HW), kmap))
    out_shape.append(jax.ShapeDtypeStruct((seq, HW), dk_dtype))
    scratch = [pltpu.VMEM((blk, HW), F32)]
    if mla:
        out_specs.append(pl.BlockSpec((blk, LANES), kmap))
        out_shape.append(jax.ShapeDtypeStruct((seq, LANES), F32))
        scratch.append(pltpu.VMEM((blk, LANES), F32))
    out_specs.append(pl.BlockSpec((blk, HW), kmap))
    out_shape.append(jax.ShapeDtypeStruct((seq, HW), BF16))
    scratch.append(pltpu.VMEM((blk, HW), F32))
    scratch += [pltpu.VMEM((2, 2 * blk, blk), F32), pltpu.VMEM((2, 2 * blk, blk), F32)]
    if single_visit:
        scratch.append(pltpu.VMEM((1, qw, blk), F32))
    body = _ride_along(body, ride, 5, len(args), len(out_shape), len(scratch), n_steps)
    if ride is not None:
        args, in_specs = args + ride.args, in_specs + ride.in_specs
        out_specs, out_shape, scratch = out_specs + ride.out_specs, out_shape + ride.out_shape, scratch + ride.scratch
    return _pcall(
        body, name=name,
        grid_spec=pltpu.PrefetchScalarGridSpec(
            num_scalar_prefetch=5, grid=(n_steps,), in_specs=in_specs, out_specs=out_specs,
            scratch_shapes=scratch),
        out_shape=out_shape,
        compiler_params=_cparams(dimension_semantics=("arbitrary",)),
    )(*steps, *args)


def _out_ln(oa, ob_near, ob_far, lse_near, lse_far, ga, gb, x, tgt, w_out, ln_g, ln_b, bt):
    seq = x.shape[0]

    def body(oa_ref, obn_ref, obf_ref, lsen_ref, lsef_ref, ga_ref, gb_ref, x_ref, tgt_ref, w_ref, g_ref, b_ref,
             dz_ref, doa_ref, dob_ref, dga_ref, dgb_ref, da_ref, db_ref, lse_ref, gwb_ref, small_ref, dobc_ref,
             gw_ref, lanes_sc):
        i = pl.program_id(0)

        @pl.when(i == 0)
        def _():
            gw_ref[...] = jnp.zeros(gw_ref.shape, F32)
            small_ref[...] = jnp.zeros(small_ref.shape, F32)

        def gate(g):
            sig = 0.5 * jnp.tanh(0.5 * g) + 0.5
            return g * sig, sig * (1.0 + g * (1.0 - sig))

        lse_n, lse_f = lsen_ref[...], lsef_ref[...]
        top = jnp.maximum(lse_n, lse_f)
        e_n, e_f = jnp.exp2(lse_n - top), jnp.exp2(lse_f - top)
        lse_ref[...] = top + jnp.log2(e_n + e_f)
        inv = 1.0 / (e_n + e_f)
        head_row = lax.broadcasted_iota(jnp.int32, (2 * HEADS, HW), 0) % HEADS
        spread = (head_row == lax.broadcasted_iota(jnp.int32, (2 * HEADS, HW), 1) // 64).astype(BF16)

        def per_lane(w):
            hi = w.astype(BF16)
            lo = (w - hi.astype(F32)).astype(BF16)
            return lax.dot_general(jnp.concatenate([hi, lo], axis=0), spread, TN, preferred_element_type=F32)

        o_b_all = per_lane(e_n * inv) * obn_ref[...] + per_lane(e_f * inv) * _in_sequence(obf_ref, lanes_sc)
        gam = g_ref[...]
        halves = [slice(0, bt // 2), slice(bt // 2, bt)]

        def gates_and_projection(rows):
            o_a, o_b = oa_ref[rows, :], o_b_all[rows]
            sa, dsa = gate(ga_ref[rows, :])
            sb, dsb = gate(gb_ref[rows, :])
            mix = jnp.concatenate([o_a * sa, o_b * sb], axis=1).astype(BF16)
            z = ALPHA * x_ref[rows, :] + jnp.dot(mix, w_ref[...], preferred_element_type=F32)
            return o_a, o_b, sa, dsa, sb, dsb, mix, z

        def norm_and_back(rows, mix, z):
            mu = jnp.mean(z, axis=1, keepdims=True)
            zc = z - mu
            rstd = lax.rsqrt(jnp.mean(zc * zc, axis=1, keepdims=True) + LN_EPS)
            xhat = zc * rstd
            diff = xhat * gam + b_ref[...] - tgt_ref[rows, :]
            dy = diff * (1.0 / D_MODEL)
            small_ref[0:1, :] += jnp.sum(dy * xhat, axis=0, keepdims=True)
            small_ref[1:2, :] += jnp.sum(dy, axis=0, keepdims=True)
            small_ref[2:3, :] += jnp.sum(diff * diff, axis=0, keepdims=True)
            dxh = dy * gam
            dz = rstd * (dxh - jnp.mean(dxh, axis=1, keepdims=True)
                         - xhat * jnp.mean(dxh * xhat, axis=1, keepdims=True))
            dz_ref[rows, :] = dz
            dzb = dz.astype(BF16)
            gw_ref[...] += lax.dot_general(mix, dzb, TN, preferred_element_type=F32)
            return lax.dot_general(dzb, w_ref[...], NT, preferred_element_type=F32)

        def gate_back(rows, o_a, o_b, sa, dsa, sb, dsb, dmix):
            doa, dob = dmix[:, :HW] * sa, dmix[:, HW:] * sb
            doa_ref[rows, :] = doa.astype(BF16)
            dob_ref[rows, :] = dob.astype(BF16)
            dga_ref[rows, :] = (dmix[:, :HW] * o_a * dsa).astype(BF16)
            dgb_ref[rows, :] = (dmix[:, HW:] * o_b * dsb).astype(BF16)
            return dob, doa * o_a, dob * o_b

        fronts = [gates_and_projection(rows) for rows in halves]
        dmixes = [norm_and_back(rows, f[6], f[7]) for rows, f in zip(halves, fronts)]
        backs = [gate_back(rows, *f[:6], dmix) for rows, f, dmix in zip(halves, fronts, dmixes)]
        dob, prod_a, prod_b = (jnp.concatenate(parts, axis=0) for parts in zip(*backs))
        _by_class(dob, dobc_ref, lanes_sc)

        @pl.when(i == seq // bt - 1)
        def _():
            gwb_ref[...] = gw_ref[...].astype(BF16)

        head_of = (lax.broadcasted_iota(jnp.int32, (2 * HW, LANES), 0) % HW) // 64
        ind = (head_of == lax.broadcasted_iota(jnp.int32, (2 * HW, LANES), 1)).astype(BF16)

        def head_sums(prod):
            hi = prod.astype(BF16)
            lo = (prod - hi.astype(F32)).astype(BF16)
            sums = jnp.dot(jnp.concatenate([hi, lo], axis=1), ind, preferred_element_type=F32)
            return sums.T[:HEADS, :]

        da_ref[...] = head_sums(prod_a)
        db_ref[...] = head_sums(prod_b)

    def tok(width):
        return pl.BlockSpec((bt, width), lambda i: (i, 0))

    def full(shape):
        return pl.BlockSpec(shape, lambda i: (0,) * len(shape))

    stat = pl.BlockSpec((HEADS, bt), lambda i: (0, i))
    n_cls = ob_far.shape[0]
    by_class = pl.BlockSpec((n_cls, bt // n_cls, HW), lambda i: (0, i, 0))
    return _pcall(
        body, name="out_ln", grid=(seq // bt,),
        in_specs=[tok(HW), tok(HW), by_class, stat, stat, tok(HW), tok(HW), tok(D_MODEL), tok(D_MODEL),
                  full((D_MODEL, D_MODEL)), full((1, D_MODEL)), full((1, D_MODEL))],
        out_specs=[tok(D_MODEL), tok(HW), tok(HW), tok(HW), tok(HW), stat, stat, stat,
                   full((D_MODEL, D_MODEL)), full((8, D_MODEL)), by_class],
        out_shape=[jax.ShapeDtypeStruct((seq, D_MODEL), F32)] + [jax.ShapeDtypeStruct((seq, HW), BF16)] * 4
        + [jax.ShapeDtypeStruct((HEADS, seq), F32)] * 3
        + [jax.ShapeDtypeStruct((D_MODEL, D_MODEL), BF16), jax.ShapeDtypeStruct((8, D_MODEL), F32),
           jax.ShapeDtypeStruct(ob_far.shape, BF16)],
        scratch_shapes=[pltpu.VMEM((D_MODEL, D_MODEL), F32), pltpu.VMEM((HW // LANES, bt, LANES), F32)],
        compiler_params=_cparams(dimension_semantics=("arbitrary",)),
    )(oa, ob_near, ob_far, lse_near, lse_far, ga, gb, x, tgt, w_out, ln_g, ln_b)


def _bwd_mid(dq_m, dkn, dv, dkpe, dqb, dkb, dvb, far, dga, dgb, cq, ckv, qn, kvn, w_uq_r, w_ukv_r, qg, kvg, tabs, bt):
    n_cls = far[0].shape[0]
    seq = cq.shape[0]

    def body(dqm_ref, dkn_ref, dv_ref, dkpe_ref, dqb_ref, dkb_ref, dvb_ref, dqf_ref, dkf_ref, dvf_ref, dga_ref, dgb_ref,
             cq_ref, ckv_ref, qn_ref, kvn_ref, wuq_ref, wukv_ref, qg_ref, kvg_ref, tab_ref,
             dh_ref, guq3_ref, gukv3_ref, small_ref, seq_sc, guq_ref, gukv_ref):
        i = pl.program_id(0)

        @pl.when(i == 0)
        def _():
            guq_ref[...] = jnp.zeros(guq_ref.shape, F32)
            gukv_ref[...] = jnp.zeros(gukv_ref.shape, F32)
            small_ref[...] = jnp.zeros(small_ref.shape, F32)

        m_tabs = (tab_ref[0], tab_ref[1], tab_ref[2])
        d_tabs = (tab_ref[3], tab_ref[4], tab_ref[5])

        def rms_bwd(c, dn, gain):
            r = lax.rsqrt(jnp.mean(c * c, axis=1, keepdims=True) + RMS_EPS)
            u = dn * gain
            dc = r * u - c * (r * r * r) * jnp.mean(u * c, axis=1, keepdims=True)
            return dc, jnp.sum(dn * c * r, axis=0, keepdims=True)

        dqm = dqm_ref[0].T
        dq = jnp.concatenate(
            [dqm[:, :HW], _rope_wide(_rope_t, dqm[:, HW:], *m_tabs, MLA_ROPE // 2)], axis=1) * MLA_SCALE
        dq = dq.astype(BF16)
        dkv = jnp.concatenate([dkn_ref[...], dv_ref[...]], axis=1)
        guq_ref[...] += lax.dot_general(qn_ref[...], dq, TN, preferred_element_type=F32)
        dqn = lax.dot_general(dq, wuq_ref[...], NT, preferred_element_type=F32)
        gukv_ref[...] += lax.dot_general(kvn_ref[...], dkv, TN, preferred_element_type=F32)
        dkvn = lax.dot_general(dkv, wukv_ref[...], NT, preferred_element_type=F32)

        dh_ref[:, C_KR:C_GA] = _rope_t(dkpe_ref[...], *m_tabs, MLA_ROPE // 2).astype(BF16)
        dh_ref[:, C_GA:C_QB] = dga_ref[...]
        in_sequence = functools.partial(_in_sequence, lanes_sc=seq_sc)
        dqb = dqb_ref[0].T + in_sequence(dqf_ref)
        dh_ref[:, C_QB:C_KB] = (_rope_wide(_rope_t, dqb, *d_tabs, DIL_ROT // 2) * DIL_SCALE).astype(BF16)
        dkb = dkb_ref[...] + in_sequence(dkf_ref)
        dh_ref[:, C_KB:C_VB] = _rope_wide(_rope_t, dkb, *d_tabs, DIL_ROT // 2).astype(BF16)
        dh_ref[:, C_VB:C_GB] = (dvb_ref[...].astype(F32) + in_sequence(dvf_ref)).astype(BF16)
        dh_ref[:, C_GB:C_END] = dgb_ref[...]

        dcq, gq = rms_bwd(cq_ref[...], dqn, qg_ref[...])
        small_ref[0:1, :] += gq
        dckv, gkv = rms_bwd(ckv_ref[...], dkvn, kvg_ref[...])
        small_ref[1:2, :KV_RANK] += gkv
        dh_ref[:, C_CQ:C_CKV] = dcq.astype(BF16)
        dh_ref[:, C_CKV:C_KR] = dckv.astype(BF16)

        @pl.when(i == seq // bt - 1)
        def _():
            for h in range(HEADS):
                guq3_ref[h] = jnp.concatenate(
                    [guq_ref[:, MLA_NOPE * h:MLA_NOPE * (h + 1)],
                     guq_ref[:, HW + MLA_ROPE * h:HW + MLA_ROPE * (h + 1)]], axis=1).astype(BF16)
                gukv3_ref[h] = jnp.concatenate(
                    [gukv_ref[:, MLA_NOPE * h:MLA_NOPE * (h + 1)],
                     gukv_ref[:, HW + MLA_V * h:HW + MLA_V * (h + 1)]], axis=1).astype(BF16)

    def tok(width):
        return pl.BlockSpec((bt, width), lambda i: (i, 0))

    def tok_t(a):
        per = a.shape[2] // bt
        return pl.BlockSpec((1, a.shape[1], bt), lambda i: (i // per, 0, i % per))

    def full(shape):
        return pl.BlockSpec(shape, lambda i: (0,) * len(shape))

    by_class = pl.BlockSpec((n_cls, bt // n_cls, HW), lambda i: (0, i, 0))
    uq3 = (HEADS, Q_RANK, MLA_NOPE + MLA_ROPE)
    ukv3 = (HEADS, KV_RANK, MLA_NOPE + MLA_V)
    return _pcall(
        body, name="bwd_mid", grid=(seq // bt,),
        in_specs=[tok_t(dq_m), tok(HW), tok(HW), tok(LANES), tok_t(dqb), tok(HW), tok(HW), by_class, by_class, by_class,
                  tok(HW), tok(HW),
                  tok(Q_RANK), tok(KV_RANK), tok(Q_RANK), tok(KV_RANK),
                  full(w_uq_r.shape), full(w_ukv_r.shape), full((1, Q_RANK)), full((1, KV_RANK)),
                  pl.BlockSpec((6, bt, LANES), lambda i: (0, i, 0))],
        out_specs=[tok(C_END), full(uq3), full(ukv3), full((8, Q_RANK))],
        out_shape=[jax.ShapeDtypeStruct((seq, C_END), BF16), jax.ShapeDtypeStruct(uq3, BF16),
                   jax.ShapeDtypeStruct(ukv3, BF16), jax.ShapeDtypeStruct((8, Q_RANK), F32)],
        scratch_shapes=[pltpu.VMEM((HW // LANES, bt, LANES), F32), pltpu.VMEM(w_uq_r.shape, F32),
                        pltpu.VMEM(w_ukv_r.shape, F32)],
        compiler_params=_cparams(dimension_semantics=("arbitrary",)),
    )(dq_m, dkn, dv, dkpe, dqb, dkb, dvb, *far, dga, dgb, cq, ckv, qn, kvn, w_uq_r, w_ukv_r, qg, kvg, tabs)


def _grad_x(dz, dh, w_in_r, bt, ride=None):
    seq = dz.shape[0]
    n_steps = seq // bt

    def body(dz_ref, dh_ref, w_ref, gx_ref):
        gx_ref[...] = ALPHA * dz_ref[...] + lax.dot_general(
            dh_ref[...], w_ref[...], NT, preferred_element_type=F32)

    args = [dz, dh, w_in_r]
    in_specs = [pl.BlockSpec((bt, D_MODEL), lambda i: (i, 0)), pl.BlockSpec((bt, C_END), lambda i: (i, 0)),
                pl.BlockSpec(w_in_r.shape, lambda i: (0, 0))]
    out_specs = [pl.BlockSpec((bt, D_MODEL), lambda i: (i, 0))]
    out_shape = [jax.ShapeDtypeStruct((seq, D_MODEL), F32)]
    scratch = []
    body = _ride_along(body, ride, 0, len(args), len(out_shape), 0, n_steps)
    if ride is not None:
        args, in_specs = args + ride.args, in_specs + ride.in_specs
        out_specs, out_shape, scratch = out_specs + ride.out_specs, out_shape + ride.out_shape, ride.scratch
    return _pcall(
        body, name="grad_x", grid=(n_steps,),
        in_specs=in_specs, out_specs=out_specs, out_shape=out_shape, scratch_shapes=scratch,
        compiler_params=_cparams(dimension_semantics=("arbitrary",)),
    )(*args)


def _grad_w_in(x, dh, bt, cut, ride):
    seq = x.shape[0]
    n_tok = seq // bt
    shard = IN_WIDTH // N_DEV
    k_lo, k_hi = IN_SPLITS[0] + IN_SPLITS[1], IN_SPLITS[0] + IN_SPLITS[1] + MLA_ROPE

    def body(x_ref, dh_ref, rest_ref, acc, first_ref):
        i = pl.program_id(0)
        for part, (r_lo, r_hi, dst_ref) in enumerate([(0, cut, first_ref), (cut, D_MODEL, rest_ref)]):
            rows = slice(0, r_hi - r_lo)

            @pl.when(i == part * n_tok)
            def _():
                acc[rows, :] = jnp.zeros((r_hi - r_lo, C_END), F32)

            @pl.when(i // n_tok == part)
            def _():
                acc[rows, :] += lax.dot_general(
                    x_ref[:, r_lo:r_hi].astype(BF16), dh_ref[...], TN, preferred_element_type=F32)

            @pl.when(i == (part + 1) * n_tok - 1)
            def _():
                kr = acc[rows, C_KR:C_GA]
                kr = kr + pltpu.roll(kr, 96, 1) + pltpu.roll(kr, 64, 1) + pltpu.roll(kr, 32, 1)
                for d in range(N_DEV):
                    lo, hi = shard * d, shard * (d + 1)
                    pieces = []
                    if lo < k_lo:
                        pieces.append(acc[rows, lo:min(hi, k_lo)])
                    if lo < k_hi and hi > k_lo:
                        pieces.append(kr[:, max(lo, k_lo) - k_lo:min(hi, k_hi) - k_lo])
                    if hi > k_hi:
                        shift = C_GA - k_hi
                        pieces.append(acc[rows, max(lo, k_hi) + shift:hi + shift])
                    blk = pieces[0] if len(pieces) == 1 else jnp.concatenate(pieces, axis=1)
                    dst_ref[d] = blk.astype(BF16)

    args = [x, dh]
    in_specs = [pl.BlockSpec((bt, D_MODEL), lambda i: (i % n_tok, 0)),
                pl.BlockSpec((bt, C_END), lambda i: (i % n_tok, 0))]
    out_specs = [pl.BlockSpec((N_DEV, D_MODEL - cut, shard), lambda i: (0, 0, 0))]
    out_shape = [jax.ShapeDtypeStruct((N_DEV, D_MODEL - cut, shard), BF16)]
    scratch = [pltpu.VMEM((max(cut, D_MODEL - cut), C_END), F32)]
    assert ride.shared == 1
    body = _ride_along(body, ride, 0, len(args), len(out_shape), len(scratch), 2 * n_tok)
    return _pcall(
        body, name="grad_w_in", grid=(2 * n_tok,),
        in_specs=in_specs + ride.in_specs, out_specs=out_specs + ride.out_specs,
        out_shape=out_shape + ride.out_shape, scratch_shapes=scratch + ride.scratch,
        compiler_params=_cparams(dimension_semantics=("arbitrary",)),
    )(*args, *ride.args)


def _local_step(x, tgt, w_in_r, w_uq_r, w_ukv_r, w_out_rider, g_out_rider, reduce_rider, q_norm_g, kv_norm_g,
                ln_g, ln_b, bt=BLOCK_TOKENS, blk_m=BLOCK_MLA, blk_d=BLOCK_DIL):
    seq = x.shape[0]
    tabs = jnp.asarray(_rope_tables(seq))
    qg, kvg = q_norm_g.reshape(1, -1), kv_norm_g.reshape(1, -1)

    far_dil = DIL_CONFIGS[-1][1]
    cls = seq // far_dil
    (cq, ckv, qn, kvn, qcat, kn, kpe, v, ga, gb, qb, kb, vb, knt, kpet, vt, kbt, vbt, qb_c, kb_c, vb_c) = _fwd_proj(
        x, w_in_r, w_uq_r, w_ukv_r, qg, kvg, tabs, bt, far_dil)
    qb_c, kb_c, vb_c = (a.reshape(seq, HW) for a in (qb_c, kb_c, vb_c))

    nq_m, nq_d = seq // blk_m, seq // blk_d
    bias_m = _mla_bias_t(blk_m)
    oa, lse_a, w_out = _attn_fwd(
        "mla_fwd", qcat, kn, kpe, vt, bias_m, _steps(nq_m, nq_m, False, True), blk_m, ride=w_out_rider)

    bias_near = _dil_bias_t(blk_d, DIL_NEAR)
    ob_near, lse_near = _attn_fwd(
        "dil_fwd", qb, kb, None, vbt, bias_near, _steps(nq_d, -(-DIL_NEAR // blk_d), False, False), blk_d)
    each = np.arange(far_dil, dtype=np.int32)
    steps_far = [jnp.asarray(v) for v in (each, each, np.zeros_like(each), np.ones_like(each), np.ones_like(each))]
    bias_far = _dil_far_bias_t(cls)
    ob_far, lse_far = _attn_fwd(
        "dil_far_fwd", qb_c, kb_c, None, vb_c, bias_far, steps_far, cls, v_token_major=True)

    dz, doa, dob, dga, dgb, dst_a, dst_b, lse_b, g_out, small1, dob_c = _out_ln(
        oa, ob_near, ob_far.reshape(far_dil, cls, HW), lse_near, _lanes_from_classes(lse_far, far_dil), ga, gb, x, tgt,
        w_out.reshape(D_MODEL, D_MODEL), ln_g.reshape(1, -1), ln_b.reshape(1, -1), bt)

    dq_m, dkn, dkpe, dv, g_out_recv = _attn_bwd(
        "mla_bwd", qcat, kn, kpe, v, knt, kpet, bias_m, doa, lse_a, dst_a, _steps(nq_m, nq_m, True, True), blk_m,
        ride=g_out_rider(g_out.reshape(N_DEV, D_MODEL // N_DEV, D_MODEL)))
    dqb, dkb_near, dvb_near = _attn_bwd(
        "dil_bwd", qb, kb, None, vb, kbt, None, bias_near, dob, lse_b, dst_b,
        _steps(nq_d, -(-DIL_NEAR // blk_d), True, False), blk_d)
    dqb_far, dkb_far, dvb_far = _attn_bwd(
        "dil_far_bwd", qb_c, kb_c, None, vb_c, None, None, bias_far, dob_c.reshape(seq, HW),
        _lanes_to_classes(lse_b, far_dil), _lanes_to_classes(dst_b, far_dil), steps_far, cls, single_visit=True)
    far = [a.reshape(far_dil, cls, HW) for a in (dqb_far, dkb_far, dvb_far)]

    dh, g_uq, g_ukv, small2 = _bwd_mid(
        dq_m, dkn, dv, dkpe, dqb, dkb_near, dvb_near, far, dga, dgb, cq, ckv, qn, kvn, w_uq_r, w_ukv_r, qg, kvg, tabs, bt)
    bt_w = min(seq, 2 * bt)
    n_tok = seq // bt_w
    first = jax.ShapeDtypeStruct((N_DEV, GRAD_W_IN_CUT, IN_WIDTH // N_DEV), BF16)
    g_in_rest, *reduced_first = _grad_w_in(
        x, dh, bt_w, GRAD_W_IN_CUT,
        reduce_rider([first, g_uq, g_ukv], g_out_recv, (small1, small2), (n_tok, n_tok + 1, 2 * n_tok - 1)))
    grad_x, g_in_rest = _grad_x(dz, dh, w_in_r, bt, ride=reduce_rider([g_in_rest], None, None, (0, 1, seq // bt - 2)))
    return grad_x, [g_in_rest] + reduced_first


MESH_ID = pl.DeviceIdType.MESH
SHARD_SHAPES = ((D_MODEL, IN_WIDTH // N_DEV), (Q_RANK, 768 // N_DEV), (KV_RANK, 1024 // N_DEV), (D_MODEL // N_DEV, D_MODEL))
ADAM_ROWS = (32, 128, 128, 16)


def _me():
    x, y, c = lax.axis_index("x"), lax.axis_index("y"), lax.axis_index("c")
    return x, y, c, 4 * x + 2 * y + c


def _peer(k):
    x, y, c, _ = _me()
    px = 1 - x if (k >> 2) & 1 else x
    py = 1 - y if (k >> 1) & 1 else y
    pc = 1 - c if k & 1 else c
    return (px, py, pc), 4 * px + 2 * py + pc


def _all_gather_weights(shards):
    n = len(shards)
    shard = IN_WIDTH // N_DEV
    k_lo = IN_SPLITS[0] + IN_SPLITS[1]
    k_hi = k_lo + MLA_ROPE

    def body(*refs):
        ins = refs[:n]
        win_ref, wuq_ref, wukv_ref = refs[n:2 * n]
        bufs = refs[2 * n:3 * n]
        send_sems, recv_sems = refs[3 * n:]
        x, y, c, me = _me()
        here, sibling = (x, y, c), (x, y, 1 - c)
        along_x, along_y, across = (1 - x, y), (x, 1 - y), (1 - x, 1 - y)
        for t in range(n):
            bufs[t][me] = ins[t][...].astype(BF16)

        def copy(t, k, chip, pc, to, half=None):
            blk = bufs[t].at[4 * chip[0] + 2 * chip[1] + pc]
            if half is not None:
                rows = SHARD_SHAPES[t][0] // 2
                blk = blk.at[pl.ds(half * rows, rows), :]
            return pltpu.make_async_remote_copy(
                src_ref=blk, dst_ref=blk, send_sem=send_sems.at[t, k], recv_sem=recv_sems.at[t, k],
                device_id=to, device_id_type=MESH_ID)

        sends = []
        for t in range(n):
            sends += [copy(t, 0, (x, y), c, sibling), copy(t, 1, (x, y), c, (*along_x, c)),
                      copy(t, 2, (x, y), c, (*along_y, c))]
        for cp in sends:
            cp.start()
        for t in range(n):
            copy(t, 1, along_x, c, here).wait_recv()
            sends += [copy(t, 3, along_x, c, (*along_y, c), half=0), copy(t, 5, along_x, c, sibling)]
            sends[-2].start()
            sends[-1].start()
        for t in range(n):
            copy(t, 2, along_y, c, here).wait_recv()
            sends += [copy(t, 4, along_y, c, (*along_x, c), half=1), copy(t, 6, along_y, c, sibling)]
            sends[-2].start()
            sends[-1].start()
        for t in range(n):
            copy(t, 3, across, c, here, half=0).wait_recv()
            copy(t, 4, across, c, here, half=1).wait_recv()
            sends.append(copy(t, 7, across, c, sibling))
            sends[-1].start()
        for t in range(n):
            copy(t, 0, (x, y), 1 - c, here).wait_recv()
            for k, chip in ((5, along_x), (6, along_y), (7, across)):
                copy(t, k, chip, 1 - c, here).wait_recv()
        for cp in sends:
            cp.wait_send()

        a_in, a_uq, a_ukv = bufs
        for d in range(N_DEV):
            lo, hi = shard * d, shard * (d + 1)
            if lo < k_lo:
                win_ref[:, lo:min(hi, k_lo)] = a_in[d, :, 0:min(hi, k_lo) - lo]
            if lo < k_hi and hi > k_lo:
                kr = a_in[d, :, k_lo - lo:k_hi - lo]
                for rep in range(4):
                    win_ref[:, C_KR + MLA_ROPE * rep:C_KR + MLA_ROPE * (rep + 1)] = kr
            if hi > k_hi:
                src = max(lo, k_hi)
                win_ref[:, src + C_GA - k_hi:hi + C_GA - k_hi] = a_in[d, :, src - lo:hi - lo]
        for h in range(HEADS):
            wuq_ref[:, MLA_NOPE * h:MLA_NOPE * (h + 1)] = a_uq[h, :, :MLA_NOPE]
            wuq_ref[:, HW + MLA_ROPE * h:HW + MLA_ROPE * (h + 1)] = a_uq[h, :, MLA_NOPE:]
            wukv_ref[:, MLA_NOPE * h:MLA_NOPE * (h + 1)] = a_ukv[h, :, :MLA_NOPE]
            wukv_ref[:, HW + MLA_V * h:HW + MLA_V * (h + 1)] = a_ukv[h, :, MLA_NOPE:]

    vmem = pl.BlockSpec(memory_space=pltpu.VMEM)
    return _pcall(
        body, name="gather_weights",
        in_specs=[vmem] * n, out_specs=[vmem] * n,
        out_shape=[jax.ShapeDtypeStruct((D_MODEL, C_END), BF16), jax.ShapeDtypeStruct((Q_RANK, QW), BF16),
                   jax.ShapeDtypeStruct((KV_RANK, 2 * HW), BF16)],
        scratch_shapes=[pltpu.VMEM((N_DEV,) + s, BF16) for s in SHARD_SHAPES[:n]]
        + [pltpu.SemaphoreType.DMA((n, 8)), pltpu.SemaphoreType.DMA((n, 8))],
        compiler_params=_cparams(),
    )(*shards)


def _gather_w_out_rider(w_out):
    def copies(full_ref, stage, send_sems, recv_sems):
        me = _me()[3]
        out = []
        for k in range(1, N_DEV):
            peer, pidx = _peer(k)
            send = pltpu.make_async_remote_copy(
                src_ref=stage, dst_ref=full_ref.at[me], send_sem=send_sems.at[k - 1], recv_sem=recv_sems.at[k - 1],
                device_id=peer, device_id_type=MESH_ID)
            recv = pltpu.make_async_remote_copy(
                src_ref=stage, dst_ref=full_ref.at[pidx], send_sem=send_sems.at[k - 1], recv_sem=recv_sems.at[k - 1],
                device_id=peer, device_id_type=MESH_ID)
            out.append((send, recv))
        return out

    def start(ins, outs, scr):
        stage, send_sems, recv_sems, own_sem = scr
        stage[...] = ins[0][...].astype(BF16)
        pltpu.make_async_copy(stage, outs[0].at[_me()[3]], own_sem).start()
        for send, _ in copies(outs[0], stage, send_sems, recv_sems):
            send.start()

    def finish(ins, outs, scr):
        stage, send_sems, recv_sems, own_sem = scr
        pltpu.make_async_copy(stage, outs[0].at[_me()[3]], own_sem).wait()
        pairs = copies(outs[0], stage, send_sems, recv_sems)
        for _, recv in pairs:
            recv.wait_recv()
        for send, _ in pairs:
            send.wait_send()

    shape = SHARD_SHAPES[3]
    return Rider(
        args=[w_out], in_specs=[pl.BlockSpec(shape, lambda t, *_: (0, 0))],
        out_shape=[jax.ShapeDtypeStruct((N_DEV,) + shape, BF16)], out_specs=[pl.BlockSpec(memory_space=pl.ANY)],
        scratch=[pltpu.VMEM(shape, BF16), pltpu.SemaphoreType.DMA((N_DEV - 1,)), pltpu.SemaphoreType.DMA((N_DEV - 1,)),
                 pltpu.SemaphoreType.DMA],
        start=start, finish=finish)


def _scatter_g_out_rider(blocks):
    def copies(src_ref, dst_ref, send_sems, recv_sems):
        out = []
        for k in range(1, N_DEV):
            peer, pidx = _peer(k)
            out.append(pltpu.make_async_remote_copy(
                src_ref=src_ref.at[pidx], dst_ref=dst_ref.at[k], send_sem=send_sems.at[k - 1],
                recv_sem=recv_sems.at[k - 1], device_id=peer, device_id_type=MESH_ID))
        return out

    def start(ins, outs, scr):
        send_sems, recv_sems, own_sem = scr
        pltpu.make_async_copy(ins[0].at[_me()[3]], outs[0].at[0], own_sem).start()
        for cp in copies(ins[0], outs[0], send_sems, recv_sems):
            cp.start()

    def finish(ins, outs, scr):
        send_sems, recv_sems, own_sem = scr
        pltpu.make_async_copy(ins[0].at[_me()[3]], outs[0].at[0], own_sem).wait()
        for cp in copies(ins[0], outs[0], send_sems, recv_sems):
            cp.wait()

    hbm = pl.BlockSpec(memory_space=pl.ANY)
    return Rider(
        args=[blocks], in_specs=[hbm], out_shape=[jax.ShapeDtypeStruct(blocks.shape, blocks.dtype)], out_specs=[hbm],
        scratch=[pltpu.SemaphoreType.DMA((N_DEV - 1,)), pltpu.SemaphoreType.DMA((N_DEV - 1,)), pltpu.SemaphoreType.DMA],
        start=start, finish=finish)


def _adamw(w, g, m, v):
    m = ADAM_B1 * m + (1.0 - ADAM_B1) * g
    v = ADAM_B2 * v + (1.0 - ADAM_B2) * jnp.square(g)
    m_hat = m / (1.0 - ADAM_B1 ** ADAM_STEP)
    v_hat = v / (1.0 - ADAM_B2 ** ADAM_STEP)
    delta = -ADAM_LR * (m_hat / (jnp.sqrt(v_hat) + ADAM_EPS) + ADAM_WD * w)
    return delta, m, v


def _chunk_rows(count, cols):
    rows = (16 * 8 * LANES) // (-(-cols // LANES) * LANES)
    while count % rows:
        rows //= 2
    return rows


SMALL_ROWS = (0, 1, 3, 4)
LOSS_ROW = 2


def _reduce_grads_rider(grads3, arrived, small_parts, steps):
    n = len(grads3)
    shapes = [tuple(g.shape[1:]) for g in grads3]
    with_small = arrived is not None
    start_step, first_round, second_round = steps
    n_held = sum(isinstance(g, jax.ShapeDtypeStruct) for g in grads3)
    held, handed = grads3[:n_held], grads3[n_held:]
    assert not any(isinstance(g, jax.ShapeDtypeStruct) for g in handed)

    class Refs:
        def __init__(self, ins, outs, scr):
            self.g3, self.gsum = list(scr[0:n_held]) + list(ins[0:n - n_held]), outs[0:n]
            ins, scr = ins[n - n_held:], scr[n_held:]
            if with_small:
                self.arr, self.sp_wide, self.sp_q = ins
                self.gsum_out, self.ssum = outs[n:n + 2]
            self.own, self.sib, self.part = scr[0:n], scr[n:2 * n], scr[2 * n:3 * n]
            self.in_a, self.out_b, self.in_b = scr[3 * n:4 * n], scr[4 * n:5 * n], scr[5 * n:6 * n]
            self.rsmall = scr[6 * n]
            (self.loc_sems, self.d2d_send, self.d2d_recv, self.a_send, self.a_recv, self.b_send, self.b_recv,
             self.sm_send, self.sm_recv) = scr[6 * n + 1:]
            self.x, self.y, self.c, self.me = _me()
            self.along_x, self.along_y = (1 - self.x, self.y), (self.x, 1 - self.y)
            self.across = (1 - self.x, 1 - self.y)

        def small(self):
            if not with_small:
                return []
            return [pltpu.make_async_remote_copy(
                src_ref=self.rsmall.at[0], dst_ref=self.rsmall.at[k], send_sem=self.sm_send.at[k - 1],
                recv_sem=self.sm_recv.at[k - 1], device_id=_peer(k)[0], device_id_type=MESH_ID)
                for k in range(1, N_DEV)]

        def level1(self):
            local, to_sib = [], []
            for t in range(n):
                for q in range(4):
                    local.append(pltpu.make_async_copy(
                        self.g3[t].at[2 * q + self.c], self.own[t].at[q], self.loc_sems.at[t, q]))
                    to_sib.append(pltpu.make_async_remote_copy(
                        src_ref=self.g3[t].at[2 * q + 1 - self.c], dst_ref=self.sib[t].at[q],
                        send_sem=self.d2d_send.at[t, q], recv_sem=self.d2d_recv.at[t, q],
                        device_id=(self.x, self.y, 1 - self.c), device_id_type=MESH_ID))
            return local, to_sib

        def round_a(self):
            out = []
            for t in range(n):
                half = shapes[t][0] // 2
                for k, (to, chip, h) in enumerate([(self.along_x, self.along_x, 0), (self.along_x, self.across, 0),
                                                   (self.along_y, self.along_y, 1), (self.along_y, self.across, 1)]):
                    out.append(pltpu.make_async_remote_copy(
                        src_ref=self.part[t].at[2 * chip[0] + chip[1], pl.ds(h * half, half), :],
                        dst_ref=self.in_a[t].at[k], send_sem=self.a_send.at[t, k], recv_sem=self.a_recv.at[t, k],
                        device_id=(*to, self.c), device_id_type=MESH_ID))
            return out

        def round_b(self):
            return [pltpu.make_async_remote_copy(
                src_ref=self.out_b[t].at[k], dst_ref=self.in_b[t].at[k], send_sem=self.b_send.at[t, k],
                recv_sem=self.b_recv.at[t, k], device_id=(*to, self.c), device_id_type=MESH_ID)
                for t in range(n) for k, to in enumerate([self.along_y, self.along_x])]

    def chunks(rows, count, fn):
        def step(i, carry):
            fn(pl.multiple_of(i * rows, rows))
            return carry

        lax.fori_loop(0, count // rows, step, 0)

    def start(*refs):
        r = Refs(*refs)
        if with_small:
            r.rsmall[0] = r.sp_wide[...]
            for k, row in enumerate(SMALL_ROWS[2:]):
                r.rsmall[0, row:row + 1, 0:Q_RANK] = r.sp_q[k:k + 1, :]
        local, to_sib = r.level1()
        for cp in r.small() + local + to_sib:
            cp.start()

    def begin_rounds(*refs):
        r = Refs(*refs)
        local, to_sib = r.level1()
        for cp in local:
            cp.wait()
        for cp in to_sib:
            cp.wait_recv()
        my_chip = 2 * r.x + r.y
        for t in range(n):
            rows = _chunk_rows(*shapes[t])

            def pair_sums(at, t=t, rows=rows):
                sl = pl.ds(at, rows)
                for q in range(4):
                    r.part[t][q, sl, :] = (r.own[t][q, sl, :].astype(F32) + r.sib[t][q, sl, :].astype(F32)).astype(BF16)
                r.gsum[t][sl, :] = r.own[t][my_chip, sl, :].astype(F32) + r.sib[t][my_chip, sl, :].astype(F32)

            chunks(rows, shapes[t][0], pair_sums)
        for cp in r.round_a():
            cp.start()

    def pass_on(*refs):
        r = Refs(*refs)
        for cp in r.round_a():
            cp.wait_recv()
        q_x, q_y = 2 * r.along_x[0] + r.along_x[1], 2 * r.along_y[0] + r.along_y[1]
        for t in range(n):
            half = shapes[t][0] // 2
            rows = _chunk_rows(half, shapes[t][1])

            def add(at, t=t, rows=rows, half=half):
                lo, hi = pl.ds(at, rows), pl.ds(half + at, rows)
                r.gsum[t][lo, :] = r.gsum[t][lo, :] + r.in_a[t][0, lo, :].astype(F32)
                r.out_b[t][0, lo, :] = (r.part[t][q_y, lo, :].astype(F32) + r.in_a[t][1, lo, :].astype(F32)).astype(BF16)
                r.gsum[t][hi, :] = r.gsum[t][hi, :] + r.in_a[t][2, lo, :].astype(F32)
                r.out_b[t][1, lo, :] = (r.part[t][q_x, hi, :].astype(F32) + r.in_a[t][3, lo, :].astype(F32)).astype(BF16)

            chunks(rows, half, add)
        for cp in r.round_b():
            cp.start()

    def finish(*refs):
        r = Refs(*refs)
        if with_small:
            rows_out = _chunk_rows(*SHARD_SHAPES[3])

            def add_arrived(at):
                sl = pl.ds(at, rows_out)
                g = r.arr[0, sl, :].astype(F32)
                for k in range(1, N_DEV):
                    g = g + r.arr[k, sl, :].astype(F32)
                r.gsum_out[sl, :] = g

            chunks(rows_out, SHARD_SHAPES[3][0], add_arrived)
        passed = r.round_b()
        for cp in passed:
            cp.wait_recv()
        for t in range(n):
            half = shapes[t][0] // 2
            rows = _chunk_rows(half, shapes[t][1])

            def add(at, t=t, rows=rows, half=half):
                lo, hi = pl.ds(at, rows), pl.ds(half + at, rows)
                r.gsum[t][lo, :] = r.gsum[t][lo, :] + r.in_b[t][0, lo, :].astype(F32)
                r.gsum[t][hi, :] = r.gsum[t][hi, :] + r.in_b[t][1, lo, :].astype(F32)

            chunks(rows, half, add)
        small = r.small()
        for cp in small:
            cp.wait_recv()
        if with_small:
            tot = r.rsmall[r.me]
            for d in range(1, N_DEV):
                tot = tot + r.rsmall[jnp.bitwise_xor(r.me, d)]
            r.ssum[...] = tot
        for cp in small + r.level1()[1] + r.round_a() + passed:
            cp.wait_send()

    hbm = pl.BlockSpec(memory_space=pl.ANY)
    dma = pltpu.SemaphoreType.DMA

    def whole(shape):
        return pl.BlockSpec(shape, lambda i: (0,) * len(shape))

    def halves(slots):
        return [pltpu.VMEM((slots, s[0] // 2, s[1]), BF16) for s in shapes]

    small_args = [arrived, *small_parts] if with_small else []
    out_shapes = shapes + ([SHARD_SHAPES[3], (8, D_MODEL)] if with_small else [])
    return Rider(
        args=list(handed) + small_args,
        in_specs=[hbm] * len(handed) + [whole(a.shape) for a in small_args],
        out_shape=[jax.ShapeDtypeStruct(s, F32) for s in out_shapes], out_specs=[whole(s) for s in out_shapes],
        scratch=[pltpu.VMEM(g.shape, g.dtype) for g in held]
        + [pltpu.VMEM((4,) + s, BF16) for _ in range(3) for s in shapes]
        + halves(4) + halves(2) + halves(2)
        + [pltpu.VMEM((N_DEV, 8, D_MODEL), F32), dma((n, 4)), dma((n, 4)), dma((n, 4)), dma((n, 4)), dma((n, 4)),
           dma((n, 2)), dma((n, 2)), dma((N_DEV - 1,)), dma((N_DEV - 1,))],
        start=start, finish=finish, stages=((first_round, begin_rounds), (second_round, pass_on)),
        start_step=start_step, shared=n_held)


def _adamw_update(grads, small_grad, wmv, small_wmv):
    n_small = len(small_wmv)
    flat_grads = [g for pieces in grads for g in pieces]
    n_g = len(flat_grads)

    def body(*refs):
        g_refs, sg_ref, refs = iter(refs[0:n_g]), refs[n_g], refs[n_g + 1:]
        wmv_refs = [refs[3 * t:3 * t + 3] for t in range(4)]
        swmv_refs = [refs[12 + 3 * t:15 + 3 * t] for t in range(n_small)]
        outs = refs[12 + 3 * n_small:]
        out_refs = [outs[4 * t:4 * t + 4] for t in range(4)]
        sout_refs = [outs[16 + 4 * t:20 + 4 * t] for t in range(n_small)]
        loss_ref = outs[16 + 4 * n_small]
        for t, (w_ref, m_ref, v_ref) in enumerate(swmv_refs):
            g = sg_ref[SMALL_ROWS[t]:SMALL_ROWS[t] + 1, :w_ref.shape[1]]
            delta, m, v = _adamw(w_ref[...], g, m_ref[...], v_ref[...])
            sout_refs[t][0][...], sout_refs[t][1][...], sout_refs[t][2][...], sout_refs[t][3][...] = g, delta, m, v
        loss_ref[...] = (0.5 / D_MODEL) * jnp.sum(sg_ref[LOSS_ROW:LOSS_ROW + 1, :], axis=1, keepdims=True)
        for t in range(4):
            rows = ADAM_ROWS[t]
            w_ref, m_ref, v_ref = wmv_refs[t]
            g_out, d_out, m_out, v_out = out_refs[t]

            def step(i, carry, g_ref, first, rows=rows, w_ref=w_ref, m_ref=m_ref, v_ref=v_ref,
                     g_out=g_out, d_out=d_out, m_out=m_out, v_out=v_out):
                at = pl.multiple_of(i * rows, rows)
                r = pl.ds(pl.multiple_of(first + at, rows), rows)
                g = g_ref[pl.ds(at, rows), :]
                delta, m, v = _adamw(w_ref[r, :], g, m_ref[r, :], v_ref[r, :])
                g_out[r, :], d_out[r, :], m_out[r, :], v_out[r, :] = g, delta, m, v
                return carry

            first = 0
            for piece in grads[t]:
                lax.fori_loop(0, piece.shape[0] // rows, functools.partial(step, g_ref=next(g_refs), first=first), 0)
                first += piece.shape[0]

    vmem = pl.BlockSpec(memory_space=pltpu.VMEM)
    flat_wmv = [a for trio in wmv for a in trio]
    flat_small = [a for trio in small_wmv for a in trio]
    out_shape = ([jax.ShapeDtypeStruct(s, F32) for s in SHARD_SHAPES for _ in range(4)]
                 + [jax.ShapeDtypeStruct(trio[0].shape, F32) for trio in small_wmv for _ in range(4)]
                 + [jax.ShapeDtypeStruct((1, 1), F32)])
    return _pcall(
        body, name="adamw",
        in_specs=[vmem] * (n_g + 1 + len(flat_wmv) + len(flat_small)), out_specs=[vmem] * len(out_shape),
        out_shape=out_shape,
        compiler_params=_cparams(),
    )(*flat_grads, small_grad, *flat_wmv, *flat_small)


def kernel(x, w_in, q_norm_g, kv_norm_g, w_uq, w_ukv, w_out, ln_g, ln_b, loss_target, m_w_in, m_q_norm_g, m_kv_norm_g, m_w_uq, m_w_ukv, m_w_out, m_ln_g, m_ln_b, v_w_in, v_q_norm_g, v_kv_norm_g, v_w_uq, v_w_ukv, v_w_out, v_ln_g, v_ln_b):
    w_in_r, w_uq_r, w_ukv_r = _all_gather_weights([w_in, w_uq, w_ukv])
    grad_x, (g_in_rest, g_in_first, g_uq, g_ukv, g_out, g_small) = _local_step(
        x[0], loss_target[0], w_in_r, w_uq_r, w_ukv_r, _gather_w_out_rider(w_out), _scatter_g_out_rider,
        _reduce_grads_rider, q_norm_g, kv_norm_g, ln_g, ln_b)
    row = lambda a: a.reshape(1, -1)
    small_wmv = [(row(ln_g), row(m_ln_g), row(v_ln_g)), (row(ln_b), row(m_ln_b), row(v_ln_b)),
                 (row(q_norm_g), row(m_q_norm_g), row(v_q_norm_g)), (row(kv_norm_g), row(m_kv_norm_g), row(v_kv_norm_g))]
    wmv = [(w_in, m_w_in, v_w_in), (w_uq, m_w_uq, v_w_uq), (w_ukv, m_w_ukv, v_w_ukv), (w_out, m_w_out, v_w_out)]
    res = _adamw_update([[g_in_first, g_in_rest], [g_uq], [g_ukv], [g_out]], g_small, wmv, small_wmv)
    big = [res[4 * t:4 * t + 4] for t in range(4)]
    small = [[a.reshape(-1) for a in res[16 + 4 * t:20 + 4 * t]] for t in range(4)]
    loss = res[32].reshape(())

    def group(kind):
        return (big[0][kind], small[2][kind], small[3][kind], big[1][kind], big[2][kind], big[3][kind],
                small[0][kind], small[1][kind])

    return (loss, grad_x[None], *group(0), *group(1), *group(2), *group(3))
```

```python
import functools
from typing import Callable, NamedTuple

import numpy as np
import jax
import jax.numpy as jnp
from jax import lax
from jax.experimental import pallas as pl
from jax.experimental.pallas import tpu as pltpu

F32 = jnp.float32
BF16 = jnp.bfloat16

D_MODEL = 1024
ROPE_THETA = 500000.0
NEG = -1e30
RMS_EPS = 1e-6
LN_EPS = 1e-5
HEADS = 8
MLA_NOPE = 64
MLA_ROPE = 32
MLA_V = 64
Q_RANK = 384
KV_RANK = 256
DIL_HEAD = 64
DIL_ROT = 16
DIL_CONFIGS = ((128, 1), (512, 4), (2048, 16))
DIL_NEAR = 512
HW = HEADS * 64
QW = HW + HEADS * MLA_ROPE
IN_SPLITS = (Q_RANK, KV_RANK, MLA_ROPE, HW, HW, HW, HW, HW)
IN_WIDTH = sum(IN_SPLITS)
ALPHA = 2.0 ** 0.25
MLA_SCALE = (MLA_NOPE + MLA_ROPE) ** -0.5
DIL_SCALE = DIL_HEAD ** -0.5
LOG2E = 1.4426950408889634
LN2 = 0.6931471805599453

ADAM_LR = 0.001
ADAM_B1 = 0.9
ADAM_B2 = 0.999
ADAM_EPS = 1e-08
ADAM_WD = 0.01
ADAM_STEP = 10

N_DEV = 8
LANES = 128
VMEM_LIMIT = 56 * 1024 * 1024
BLOCK_TOKENS = 512
BLOCK_MLA = 512
BLOCK_DIL = 512
GRAD_W_IN_CUT = 384

C_CQ, C_CKV, C_KR, C_GA, C_QB, C_KB, C_VB, C_GB, C_END = 0, 384, 640, 768, 1280, 1792, 2304, 2816, 3328

NT = (((1,), (1,)), ((), ()))
TN = (((0,), (0,)), ((), ()))


def _pcall(body, **kw):
    return pl.pallas_call(body, **kw)


def _cparams(**kw):
    return pltpu.CompilerParams(vmem_limit_bytes=VMEM_LIMIT, **kw)


def _rope_tables(seq):
    def tabs(dim, period):
        half = dim // 2
        inv = np.float32(ROPE_THETA) ** (-np.arange(0, dim, 2, dtype=np.float32) / np.float32(dim))
        ang = np.arange(seq, dtype=np.float32)[:, None] * inv.astype(np.float32)[None, :]
        cos, sin = np.cos(ang).astype(np.float32), np.sin(ang).astype(np.float32)
        j = np.arange(LANES) % period
        f = j % half
        c = np.where(j < dim, cos[:, f], np.float32(1.0))
        s1 = np.where(j < half, -sin[:, f], np.float32(0.0))
        s2 = np.where((j >= half) & (j < dim), sin[:, f], np.float32(0.0))
        return [c, s1, s2]
    return np.stack(tabs(MLA_ROPE, MLA_ROPE) + tabs(DIL_ROT, DIL_HEAD)).astype(np.float32)


def _rope(t, c, s1, s2, half):
    return t * c + pltpu.roll(t, LANES - half, 1) * s1 + pltpu.roll(t, half, 1) * s2


def _rope_t(d, c, s1, s2, half):
    return d * c + pltpu.roll(d * s1, half, 1) + pltpu.roll(d * s2, LANES - half, 1)


def _rope_wide(fn, t, c, s1, s2, half):
    return jnp.concatenate(
        [fn(t[:, i:i + LANES], c, s1, s2, half) for i in range(0, t.shape[1], LANES)], axis=1)


def _mla_bias_t(blk):
    a = np.arange(blk)
    causal = np.where(a[:, None] <= a[None, :], 0.0, NEG)
    return np.stack([np.zeros((blk, blk)), causal]).astype(np.float32)


def _dil_bias_t(blk, reach):
    a = np.arange(blk)
    out = []
    for off in range(-(-reach // blk) + 1):
        delta = blk * off + a[None, :] - a[:, None]
        mult = np.zeros((blk, blk))
        for window, dil in DIL_CONFIGS:
            mult += (delta >= 0) & (delta % dil == 0) & (delta <= min(window, reach))
        out.append(np.where(mult > 0, np.log2(np.maximum(mult, 1.0)), NEG))
    return np.stack(out).astype(np.float32)


def _dil_far_bias_t(length):
    window, dil = DIL_CONFIGS[-1]
    a = np.arange(length)
    steps_back = a[None, :] - a[:, None]
    seen = (steps_back * dil > DIL_NEAR) & (steps_back * dil <= window)
    return np.where(seen, 0.0, NEG).astype(np.float32)[None]


def _lanes_to_classes(a, dil):
    h, s = a.shape
    return a.reshape(h, s // dil, dil).transpose(0, 2, 1).reshape(h, s)


def _lanes_from_classes(a, dil):
    h, s = a.shape
    return a.reshape(h, dil, s // dil).transpose(0, 2, 1).reshape(h, s)


def _steps(nq, span, by_key, diag_only_bias):
    rows = []
    if by_key:
        for ki in range(nq):
            hi = min(nq - 1, ki + span)
            for qi in range(ki, hi + 1):
                rows.append((qi, ki, int(qi == ki), int(qi == hi)))
    else:
        for qi in range(nq):
            lo = max(0, qi - span)
            for ki in range(lo, qi + 1):
                rows.append((qi, ki, int(ki == lo), int(ki == qi)))
    arr = np.array(rows, dtype=np.int32)
    off = arr[:, 0] - arr[:, 1]
    bias_idx = (off == 0).astype(np.int32) if diag_only_bias else off.astype(np.int32)
    return [jnp.asarray(v) for v in (arr[:, 0], arr[:, 1], bias_idx, arr[:, 2], arr[:, 3])]


def _by_class(val, out_ref, lanes_sc):
    n_cls, per = out_ref.shape[0], out_ref.shape[1]
    for c in range(val.shape[1] // LANES):
        lanes_sc[c] = val[:, LANES * c:LANES * (c + 1)]
        for r in range(n_cls):
            rows = lanes_sc.at[c][pl.ds(r, per, stride=n_cls), :]
            out_ref[r, :, LANES * c:LANES * (c + 1)] = rows.astype(out_ref.dtype)


def _in_sequence(ref, lanes_sc):
    n_cls, per, width = ref.shape
    for c in range(width // LANES):
        for r in range(n_cls):
            lanes_sc.at[c][pl.ds(r, per, stride=n_cls), :] = ref[r, :, LANES * c:LANES * (c + 1)].astype(F32)
    return jnp.concatenate([lanes_sc[c] for c in range(width // LANES)], axis=1)


def _fwd_proj(x, w_in_r, w_uq_r, w_ukv_r, qg, kvg, tabs, bt, n_cls):
    seq = x.shape[0]

    def body(x_ref, win_ref, wuq_ref, wukv_ref, qg_ref, kvg_ref, tab_ref,
             cq_ref, ckv_ref, qn_ref, kvn_ref, qcat_ref, kn_ref, kpe_ref, v_ref,
             ga_ref, gb_ref, qb_ref, kb_ref, vb_ref, knt_ref, kpet_ref, vt_ref, kbt_ref, vbt_ref,
             qbc_ref, kbc_ref, vbc_ref, lanes_sc):
        xb = x_ref[...].astype(BF16)

        def proj(lo, hi):
            return jnp.dot(xb, win_ref[:, lo:hi], preferred_element_type=F32)

        m_tabs = (tab_ref[0], tab_ref[1], tab_ref[2])
        d_tabs = (tab_ref[3], tab_ref[4], tab_ref[5])

        def use_cq(cq):
            cq_ref[...] = cq
            qn = (cq * lax.rsqrt(jnp.mean(cq * cq, axis=1, keepdims=True) + RMS_EPS) * qg_ref[...]).astype(BF16)
            qn_ref[...] = qn
            q = jnp.dot(qn, wuq_ref[...], preferred_element_type=F32)
            qcat_ref[:, :HW] = (q[:, :HW] * (MLA_SCALE * LOG2E)).astype(BF16)
            qcat_ref[:, HW:] = (
                _rope_wide(_rope, q[:, HW:], *m_tabs, MLA_ROPE // 2) * (MLA_SCALE * LOG2E)).astype(BF16)

        def use_ckv(ckv):
            ckv_ref[...] = ckv
            kvn = (ckv * lax.rsqrt(jnp.mean(ckv * ckv, axis=1, keepdims=True) + RMS_EPS) * kvg_ref[...]).astype(BF16)
            kvn_ref[...] = kvn
            kv = jnp.dot(kvn, wukv_ref[...], preferred_element_type=F32)
            kn_ref[...] = kv[:, :HW].astype(BF16)
            v_ref[...] = kv[:, HW:].astype(BF16)
            knt_ref[...] = kv[:, :HW].T.astype(BF16)
            vt_ref[...] = kv[:, HW:].T.astype(BF16)

        def use_kr(kr):
            kpe = _rope(kr, *m_tabs, MLA_ROPE // 2)
            kpe_ref[...] = kpe.astype(BF16)
            kpet_ref[...] = kpe.T[:MLA_ROPE, :].astype(BF16)

        def use_ga(ga):
            ga_ref[...] = ga

        def use_qb(qb):
            qb = _rope_wide(_rope, qb, *d_tabs, DIL_ROT // 2) * (DIL_SCALE * LOG2E)
            qb_ref[...] = qb.astype(BF16)
            _by_class(qb, qbc_ref, lanes_sc)

        def use_kb(kb):
            kb = _rope_wide(_rope, kb, *d_tabs, DIL_ROT // 2)
            kb_ref[...] = kb.astype(BF16)
            kbt_ref[...] = kb.T.astype(BF16)
            _by_class(kb, kbc_ref, lanes_sc)

        def use_vb(vb):
            vb_ref[...] = vb.astype(BF16)
            vbt_ref[...] = vb.T.astype(BF16)
            _by_class(vb, vbc_ref, lanes_sc)

        def use_gb(gb):
            gb_ref[...] = gb

        pieces = [(C_CQ, C_CKV, use_cq), (C_CKV, C_KR, use_ckv), (C_KR, C_GA, use_kr), (C_GA, C_QB, use_ga),
                  (C_QB, C_KB, use_qb), (C_KB, C_VB, use_kb), (C_VB, C_GB, use_vb), (C_GB, C_END, use_gb)]
        ahead = proj(*pieces[0][:2])
        for n, (_, _, use) in enumerate(pieces):
            cur = ahead
            if n + 1 < len(pieces):
                ahead = proj(*pieces[n + 1][:2])
            use(cur)

    def tok(width):
        return pl.BlockSpec((bt, width), lambda i: (i, 0))

    def tok_t(height):
        return pl.BlockSpec((height, bt), lambda i: (0, i))

    def full(a):
        return pl.BlockSpec(a.shape, lambda i: (0,) * a.ndim)

    outs = [(Q_RANK, F32), (KV_RANK, F32), (Q_RANK, BF16), (KV_RANK, BF16), (QW, BF16), (HW, BF16),
            (LANES, BF16), (HW, BF16), (HW, F32), (HW, F32), (HW, BF16), (HW, BF16), (HW, BF16)]
    outs_t = [HW, MLA_ROPE, HW, HW, HW]
    by_class = pl.BlockSpec((n_cls, bt // n_cls, HW), lambda i: (0, i, 0))
    return _pcall(
        body, name="fwd_proj", grid=(seq // bt,),
        in_specs=[tok(D_MODEL), full(w_in_r), full(w_uq_r), full(w_ukv_r), full(qg), full(kvg),
                  pl.BlockSpec((6, bt, LANES), lambda i: (0, i, 0))],
        out_specs=[tok(w) for w, _ in outs] + [tok_t(h) for h in outs_t] + [by_class] * 3,
        out_shape=[jax.ShapeDtypeStruct((seq, w), dt) for w, dt in outs]
        + [jax.ShapeDtypeStruct((h, seq), BF16) for h in outs_t]
        + [jax.ShapeDtypeStruct((n_cls, seq // n_cls, HW), BF16)] * 3,
        scratch_shapes=[pltpu.VMEM((HW // LANES, bt, LANES), F32)],
        compiler_params=_cparams(dimension_semantics=("arbitrary",)),
    )(x, w_in_r, w_uq_r, w_ukv_r, qg, kvg, tabs)


def _head_masks(lane, h):
    e, g = h % 2, h % 4
    me = (lane >= 64 * e) & (lane < 64 * e + 64)
    mr = (lane >= 32 * g) & (lane < 32 * g + 32)
    return me, mr


def _masked(mask, a):
    return jnp.where(mask, a, jnp.zeros_like(a))


def _pair_operands(q_ref, k_ref, kpe_ref, lane, j, ks=slice(None), qs=slice(None)):
    cols = slice(LANES * j, LANES * (j + 1))
    qc = q_ref[qs, cols]
    kj = k_ref[ks, cols]
    kes = []
    for h in (2 * j, 2 * j + 1):
        me, mr = _head_masks(lane, h)
        ke = _masked(me, kj)
        if kpe_ref is not None:
            ke = jnp.concatenate([ke, _masked(mr, kpe_ref[ks, :])], axis=1)
        kes.append(ke)
    if kpe_ref is not None:
        qc = jnp.concatenate([qc, q_ref[qs, HW + LANES * (j // 2):HW + LANES * (j // 2 + 1)]], axis=1)
    return qc, kes


def _tile_variants(bias_t):
    out = {}
    for i, tile in enumerate(np.asarray(bias_t)):
        h = tile.shape[0] // 2
        skip = 1 if (tile[h:, :h] == NEG).all() else 2 if (tile[:h, h:] == NEG).all() else 0
        out[i] = (bool((tile != 0).any()), skip)
    return out


def _tile_parts(blk, skip):
    lo, hi, full = slice(0, blk // 2), slice(blk // 2, blk), slice(0, blk)
    return {0: [(full, full)], 1: [(lo, full), (hi, hi)], 2: [(hi, full), (lo, lo)]}[skip]


class Rider(NamedTuple):
    args: list
    in_specs: list
    out_shape: list
    out_specs: list
    scratch: list
    start: Callable
    finish: Callable
    stages: tuple = ()
    start_step: int = 0
    shared: int = 0


def _ride_along(body, ride, n_prefetch, n_in, n_out, n_scratch, n_steps):
    if ride is None:
        return body

    def wrapped(*refs):
        pre, rest = refs[:n_prefetch], refs[n_prefetch:]
        a = n_in
        b = a + len(ride.args)
        c = b + n_out
        d = c + len(ride.out_shape)
        e = d + n_scratch
        mine = (rest[a:b], rest[c:d], rest[e:])
        t = pl.program_id(0)
        pl.when(t == ride.start_step)(lambda: ride.start(*mine))
        for at, stage in ride.stages:
            pl.when(t == at)(functools.partial(stage, *mine))
        body(*pre, *rest[:a], *rest[b:c], *rest[d:e + ride.shared])
        pl.when(t == n_steps - 1)(lambda: ride.finish(*mine))

    return wrapped


def _attn_fwd(name, q, k, kpe, vt, bias_t, steps, blk, ride=None, v_token_major=False):
    seq = q.shape[0]
    mla = kpe is not None
    n_steps = int(steps[0].shape[0])
    variants = _tile_variants(bias_t)

    def body(qi_r, ki_r, bi_r, fi_r, la_r, *refs):
        if mla:
            q_ref, k_ref, kpe_ref, vt_ref, b_ref, o_ref, lse_ref, m_sc, l_sc, acc_sc, st_sc = refs
        else:
            q_ref, k_ref, vt_ref, b_ref, o_ref, lse_ref, m_sc, l_sc, acc_sc, st_sc = refs
        t = pl.program_id(0)

        @pl.when(fi_r[t] == 1)
        def _():
            m_sc[...] = jnp.full(m_sc.shape, NEG, F32)
            l_sc[...] = jnp.zeros(l_sc.shape, F32)
            acc_sc[...] = jnp.zeros(acc_sc.shape, F32)

        lane = lax.broadcasted_iota(jnp.int32, (1, LANES), 1)
        if v_token_major:
            vt_all = vt_ref[...].astype(F32).T.astype(BF16)
            vt_rows = lambda rows, ks: vt_all[rows, ks]
        else:
            vt_rows = lambda rows, ks: vt_ref[rows, ks]

        def tile_pass(ks, qs, with_bias):
            nk, nq = ks.stop - ks.start, qs.stop - qs.start
            ones = jnp.ones((16, nk), BF16)

            def pair_scores(j):
                qc, kes = _pair_operands(q_ref, k_ref, kpe_ref if mla else None, lane, j, ks, qs)
                st = lax.dot_general(jnp.concatenate(kes, axis=0), qc, NT, preferred_element_type=F32)
                maxes = []
                for e in range(2):
                    se = st[e * nk:(e + 1) * nk]
                    if with_bias:
                        se = se + b_ref[0, ks, qs]
                    st_sc[j % 2, e * nk:(e + 1) * nk, 0:nq] = se
                    maxes.append(jnp.max(se, axis=0, keepdims=True))
                return maxes

            def softmax_pv(h, col_max):
                st = st_sc[(h // 2) % 2, (h % 2) * nk:(h % 2 + 1) * nk, 0:nq]
                hrow = slice(h, h + 1)
                m_prev = m_sc[hrow, qs]
                m_new = jnp.maximum(m_prev, col_max)
                alpha = jnp.exp2(m_prev - m_new)
                pt = jnp.exp2(st - m_new).astype(BF16)
                m_sc[hrow, qs] = m_new
                rows = slice(64 * h, 64 * h + 64)
                res = jnp.dot(jnp.concatenate([vt_rows(rows, ks), ones], axis=0), pt, preferred_element_type=F32)
                acc_sc[rows, qs] = alpha * acc_sc[rows, qs] + res[:64]
                l_sc[hrow, qs] = alpha * l_sc[hrow, qs] + res[64:65]

            maxes = pair_scores(0)
            for j in range(HEADS // 2):
                cur = maxes
                if j + 1 < HEADS // 2:
                    maxes = pair_scores(j + 1)
                softmax_pv(2 * j, cur[0])
                softmax_pv(2 * j + 1, cur[1])

        def step(with_bias, skip):
            for ks, qs in _tile_parts(blk, skip):
                tile_pass(ks, qs, with_bias)

        for idx, (with_bias, skip) in variants.items():
            if len(variants) == 1:
                step(with_bias, skip)
            else:
                pl.when(bi_r[t] == idx)(functools.partial(step, with_bias, skip))

        @pl.when(la_r[t] == 1)
        def _():
            for h in range(HEADS):
                rows = slice(64 * h, 64 * h + 64)
                acc_sc[rows, :] = acc_sc[rows, :] / l_sc[h:h + 1, :]
            o_ref[...] = acc_sc[...].T
            lse_ref[...] = m_sc[...] + jnp.log2(l_sc[...])

    qmap = lambda t, qi, ki, bi, fi, la: (qi[t], 0)
    kmap = lambda t, qi, ki, bi, fi, la: (ki[t], 0)
    in_specs = [pl.BlockSpec((blk, q.shape[1]), qmap), pl.BlockSpec((blk, HW), kmap)]
    args = [q, k]
    if mla:
        in_specs.append(pl.BlockSpec((blk, LANES), kmap))
        args.append(kpe)
    in_specs += [pl.BlockSpec((blk, HW), kmap) if v_token_major else
                 pl.BlockSpec((HW, blk), lambda t, qi, ki, bi, fi, la: (0, ki[t])),
                 pl.BlockSpec((1, blk, blk), lambda t, qi, ki, bi, fi, la: (bi[t], 0, 0))]
    args += [vt, jnp.asarray(bias_t)]
    out_specs = [pl.BlockSpec((blk, HW), qmap), pl.BlockSpec((HEADS, blk), lambda t, qi, ki, bi, fi, la: (0, qi[t]))]
    out_shape = [jax.ShapeDtypeStruct((seq, HW), F32), jax.ShapeDtypeStruct((HEADS, seq), F32)]
    scratch = [pltpu.VMEM((HEADS, blk), F32), pltpu.VMEM((HEADS, blk), F32),
               pltpu.VMEM((HW, blk), F32), pltpu.VMEM((2, 2 * blk, blk), F32)]
    body = _ride_along(body, ride, 5, len(args), len(out_shape), len(scratch), n_steps)
    if ride is not None:
        args, in_specs = args + ride.args, in_specs + ride.in_specs
        out_specs, out_shape, scratch = out_specs + ride.out_specs, out_shape + ride.out_shape, scratch + ride.scratch
    return _pcall(
        body, name=name,
        grid_spec=pltpu.PrefetchScalarGridSpec(
            num_scalar_prefetch=5, grid=(n_steps,), in_specs=in_specs, out_specs=out_specs, scratch_shapes=scratch),
        out_shape=out_shape,
        compiler_params=_cparams(dimension_semantics=("arbitrary",)),
    )(*steps, *args)


def _attn_bwd(name, q, k, kpe, v, kt, kpet, bias_t, do, lse, dstat, steps, blk, ride=None, single_visit=False):
    assert not (single_visit and kpe is not None) and (kt is not None or single_visit)
    seq = q.shape[0]
    mla = kpe is not None
    qw = q.shape[1]
    n_steps = int(steps[0].shape[0])
    dk_dtype = BF16 if mla else F32
    variants = _tile_variants(bias_t)

    def body(qi_r, ki_r, bi_r, fi_r, la_r, *refs):
        if mla:
            (q_ref, k_ref, kpe_ref, v_ref, kt_ref, kpet_ref, b_ref, do_ref, lse_ref, d_ref,
             dq_ref, dk_ref, dkpe_ref, dv_ref, dk_sc, dkpe_sc, dv_sc, st_sc, dpt_sc) = refs
        else:
            q_ref, k_ref, v_ref, *rest = refs
            kt_ref = rest.pop(0) if kt is not None else None
            b_ref, do_ref, lse_ref, d_ref, dq_out_ref, dk_ref, dv_ref, dk_sc, dv_sc, st_sc, dpt_sc, *rest = rest
            dq_ref = rest[0] if single_visit else dq_out_ref
        t = pl.program_id(0)

        @pl.when(jnp.logical_or(t == 0, single_visit))
        def _():
            dq_ref[...] = jnp.zeros(dq_ref.shape, F32)

        @pl.when(fi_r[t] == 1)
        def _():
            dk_sc[...] = jnp.zeros(dk_sc.shape, F32)
            dv_sc[...] = jnp.zeros(dv_sc.shape, F32)
            if mla:
                dkpe_sc[...] = jnp.zeros(dkpe_sc.shape, F32)

        qi = 0 if single_visit else qi_r[t]
        lane = lax.broadcasted_iota(jnp.int32, (1, LANES), 1)
        if kt is None:
            kt_all = k_ref[...].astype(F32).T.astype(BF16)
            kt_rows = lambda rows, ks: kt_all[rows, ks]
        else:
            kt_rows = lambda rows, ks: kt_ref[rows, ks]

        def tile_pass(ks, qs, with_bias):
            nk, nq = ks.stop - ks.start, qs.stop - qs.start

            def pair_matmuls(j):
                cols = slice(LANES * j, LANES * (j + 1))
                qc, kes = _pair_operands(q_ref, k_ref, kpe_ref if mla else None, lane, j, ks, qs)
                st_sc[j % 2, 0:2 * nk, 0:nq] = lax.dot_general(
                    jnp.concatenate(kes, axis=0), qc, NT, preferred_element_type=F32)
                vj = v_ref[ks, cols]
                ves = [_masked(_head_masks(lane, h)[0], vj) for h in (2 * j, 2 * j + 1)]
                dpt_sc[j % 2, 0:2 * nk, 0:nq] = lax.dot_general(
                    jnp.concatenate(ves, axis=0), do_ref[qs, cols], NT, preferred_element_type=F32)

            def pair_grads(j):
                cols = slice(LANES * j, LANES * (j + 1))
                qj, doj = q_ref[qs, cols], do_ref[qs, cols]
                if mla:
                    qr = q_ref[qs, HW + LANES * (j // 2):HW + LANES * (j // 2 + 1)]
                pts, dsts, qms, doms = [], [], [], []
                for e in range(2):
                    h = 2 * j + e
                    me, mr = _head_masks(lane, h)
                    st = st_sc[j % 2, e * nk:(e + 1) * nk, 0:nq]
                    if with_bias:
                        st = st + b_ref[0, ks, qs]
                    pt = jnp.exp2(st - lse_ref[h:h + 1, qs])
                    dst = (pt * (dpt_sc[j % 2, e * nk:(e + 1) * nk, 0:nq] - d_ref[h:h + 1, qs])).astype(BF16)
                    pts.append(pt.astype(BF16))
                    dsts.append(dst)
                    doms.append(_masked(me, doj))
                    qm = _masked(me, qj)
                    if mla:
                        qm = jnp.concatenate([qm, _masked(mr, qr)], axis=1)
                    qms.append(qm)
                    ktl = kt_rows(slice(64 * h, 64 * h + 64), ks)
                    if mla:
                        ktl = jnp.concatenate([ktl, kpet_ref[:, ks]], axis=0)
                    dqc = jnp.dot(ktl, dst, preferred_element_type=F32)
                    dq_ref[qi, 64 * h:64 * h + 64, qs] += dqc[:64]
                    if mla:
                        dq_ref[qi, HW + MLA_ROPE * h:HW + MLA_ROPE * (h + 1), qs] += dqc[64:]
                dv_sc[ks, cols] += jnp.dot(
                    jnp.concatenate(pts, axis=1), jnp.concatenate(doms, axis=0), preferred_element_type=F32)
                dkc = jnp.dot(jnp.concatenate(dsts, axis=1), jnp.concatenate(qms, axis=0), preferred_element_type=F32)
                dk_sc[ks, cols] += dkc[:, :LANES]
                if mla:
                    dkpe_sc[ks, :] += dkc[:, LANES:]

            pair_matmuls(0)
            for j in range(HEADS // 2):
                if j + 1 < HEADS // 2:
                    pair_matmuls(j + 1)
                pair_grads(j)

        def step(with_bias, skip):
            for ks, qs in _tile_parts(blk, skip):
                tile_pass(ks, qs, with_bias)

        for idx, (with_bias, skip) in variants.items():
            if len(variants) == 1:
                step(with_bias, skip)
            else:
                pl.when(bi_r[t] == idx)(functools.partial(step, with_bias, skip))
        if single_visit:
            dq_out_ref[...] = dq_ref[0].T

        @pl.when(la_r[t] == 1)
        def _():
            dk_ref[...] = (dk_sc[...] * LN2).astype(dk_ref.dtype)
            dv_ref[...] = dv_sc[...].astype(dv_ref.dtype)
            if mla:
                dkpe_ref[...] = dkpe_sc[...] * LN2

    qmap = lambda t, qi, ki, bi, fi, la: (qi[t], 0)
    kmap = lambda t, qi, ki, bi, fi, la: (ki[t], 0)
    qmap_t = lambda t, qi, ki, bi, fi, la: (0, qi[t])
    kmap_t = lambda t, qi, ki, bi, fi, la: (0, ki[t])
    in_specs = [pl.BlockSpec((blk, qw), qmap), pl.BlockSpec((blk, HW), kmap)]
    args = [q, k]
    if mla:
        in_specs.append(pl.BlockSpec((blk, LANES), kmap))
        args.append(kpe)
    in_specs.append(pl.BlockSpec((blk, HW), kmap))
    args.append(v)
    if kt is not None:
        in_specs.append(pl.BlockSpec((HW, blk), kmap_t))
        args.append(kt)
    if mla:
        in_specs.append(pl.BlockSpec((MLA_ROPE, blk), kmap_t))
        args.append(kpet)
    in_specs += [pl.BlockSpec((1, blk, blk), lambda t, qi, ki, bi, fi, la: (bi[t], 0, 0)),
                 pl.BlockSpec((blk, HW), qmap), pl.BlockSpec((HEADS, blk), qmap_t), pl.BlockSpec((HEADS, blk), qmap_t)]
    args += [jnp.asarray(bias_t), do, lse, dstat]
    dq_shape = (seq // blk, qw, blk)
    if single_visit:
        out_specs, out_shape = [pl.BlockSpec((blk, qw), qmap)], [jax.ShapeDtypeStruct((seq, qw), F32)]
    else:
        out_specs = [pl.BlockSpec(dq_shape, lambda t, qi, ki, bi, fi, la: (0, 0, 0))]
        out_shape = [jax.ShapeDtypeStruct(dq_shape, F32)]
    out_specs.append(pl.BlockSpec((blk, HW), kmap))
    out_shape.append(jax.ShapeDtypeStruct((seq, HW), dk_dtype))
    scratch = [pltpu.VMEM((blk, HW), F32)]
    if mla:
        out_specs.append(pl.BlockSpec((blk, LANES), kmap))
        out_shape.append(jax.ShapeDtypeStruct((seq, LANES), F32))
        scratch.append(pltpu.VMEM((blk, LANES), F32))
    out_specs.append(pl.BlockSpec((blk, HW), kmap))
    out_shape.append(jax.ShapeDtypeStruct((seq, HW), BF16))
    scratch.append(pltpu.VMEM((blk, HW), F32))
    scratch += [pltpu.VMEM((2, 2 * blk, blk), F32), pltpu.VMEM((2, 2 * blk, blk), F32)]
    if single_visit:
        scratch.append(pltpu.VMEM((1, qw, blk), F32))
    body = _ride_along(body, ride, 5, len(args), len(out_shape), len(scratch), n_steps)
    if ride is not None:
        args, in_specs = args + ride.args, in_specs + ride.in_specs
        out_specs, out_shape, scratch = out_specs + ride.out_specs, out_shape + ride.out_shape, scratch + ride.scratch
    return _pcall(
        body, name=name,
        grid_spec=pltpu.PrefetchScalarGridSpec(
            num_scalar_prefetch=5, grid=(n_steps,), in_specs=in_specs, out_specs=out_specs,
            scratch_shapes=scratch),
        out_shape=out_shape,
        compiler_params=_cparams(dimension_semantics=("arbitrary",)),
    )(*steps, *args)


def _out_ln(oa, ob_near, ob_far, lse_near, lse_far, ga, gb, x, tgt, w_out, ln_g, ln_b, bt):
    seq = x.shape[0]

    def body(oa_ref, obn_ref, obf_ref, lsen_ref, lsef_ref, ga_ref, gb_ref, x_ref, tgt_ref, w_ref, g_ref, b_ref,
             dz_ref, doa_ref, dob_ref, dga_ref, dgb_ref, da_ref, db_ref, lse_ref, gwb_ref, small_ref, dobc_ref,
             gw_ref, lanes_sc):
        i = pl.program_id(0)

        @pl.when(i == 0)
        def _():
            gw_ref[...] = jnp.zeros(gw_ref.shape, F32)
            small_ref[...] = jnp.zeros(small_ref.shape, F32)

        def gate(g):
            sig = 0.5 * jnp.tanh(0.5 * g) + 0.5
            return g * sig, sig * (1.0 + g * (1.0 - sig))

        lse_n, lse_f = lsen_ref[...], lsef_ref[...]
        top = jnp.maximum(lse_n, lse_f)
        e_n, e_f = jnp.exp2(lse_n - top), jnp.exp2(lse_f - top)
        lse_ref[...] = top + jnp.log2(e_n + e_f)
        inv = 1.0 / (e_n + e_f)
        head_row = lax.broadcasted_iota(jnp.int32, (2 * HEADS, HW), 0) % HEADS
        spread = (head_row == lax.broadcasted_iota(jnp.int32, (2 * HEADS, HW), 1) // 64).astype(BF16)

        def per_lane(w):
            hi = w.astype(BF16)
            lo = (w - hi.astype(F32)).astype(BF16)
            return lax.dot_general(jnp.concatenate([hi, lo], axis=0), spread, TN, preferred_element_type=F32)

        o_b_all = per_lane(e_n * inv) * obn_ref[...] + per_lane(e_f * inv) * _in_sequence(obf_ref, lanes_sc)
        gam = g_ref[...]
        halves = [slice(0, bt // 2), slice(bt // 2, bt)]

        def gates_and_projection(rows):
            o_a, o_b = oa_ref[rows, :], o_b_all[rows]
            sa, dsa = gate(ga_ref[rows, :])
            sb, dsb = gate(gb_ref[rows, :])
            mix = jnp.concatenate([o_a * sa, o_b * sb], axis=1).astype(BF16)
            z = ALPHA * x_ref[rows, :] + jnp.dot(mix, w_ref[...], preferred_element_type=F32)
            return o_a, o_b, sa, dsa, sb, dsb, mix, z

        def norm_and_back(rows, mix, z):
            mu = jnp.mean(z, axis=1, keepdims=True)
            zc = z - mu
            rstd = lax.rsqrt(jnp.mean(zc * zc, axis=1, keepdims=True) + LN_EPS)
            xhat = zc * rstd
            diff = xhat * gam + b_ref[...] - tgt_ref[rows, :]
            dy = diff * (1.0 / D_MODEL)
            small_ref[0:1, :] += jnp.sum(dy * xhat, axis=0, keepdims=True)
            small_ref[1:2, :] += jnp.sum(dy, axis=0, keepdims=True)
            small_ref[2:3, :] += jnp.sum(diff * diff, axis=0, keepdims=True)
            dxh = dy * gam
            dz = rstd * (dxh - jnp.mean(dxh, axis=1, keepdims=True)
                         - xhat * jnp.mean(dxh * xhat, axis=1, keepdims=True))
            dz_ref[rows, :] = dz
            dzb = dz.astype(BF16)
            gw_ref[...] += lax.dot_general(mix, dzb, TN, preferred_element_type=F32)
            return lax.dot_general(dzb, w_ref[...], NT, preferred_element_type=F32)

        def gate_back(rows, o_a, o_b, sa, dsa, sb, dsb, dmix):
            doa, dob = dmix[:, :HW] * sa, dmix[:, HW:] * sb
            doa_ref[rows, :] = doa.astype(BF16)
            dob_ref[rows, :] = dob.astype(BF16)
            dga_ref[rows, :] = (dmix[:, :HW] * o_a * dsa).astype(BF16)
            dgb_ref[rows, :] = (dmix[:, HW:] * o_b * dsb).astype(BF16)
            return dob, doa * o_a, dob * o_b

        fronts = [gates_and_projection(rows) for rows in halves]
        dmixes = [norm_and_back(rows, f[6], f[7]) for rows, f in zip(halves, fronts)]
        backs = [gate_back(rows, *f[:6], dmix) for rows, f, dmix in zip(halves, fronts, dmixes)]
        dob, prod_a, prod_b = (jnp.concatenate(parts, axis=0) for parts in zip(*backs))
        _by_class(dob, dobc_ref, lanes_sc)

        @pl.when(i == seq // bt - 1)
        def _():
            gwb_ref[...] = gw_ref[...].astype(BF16)

        head_of = (lax.broadcasted_iota(jnp.int32, (2 * HW, LANES), 0) % HW) // 64
        ind = (head_of == lax.broadcasted_iota(jnp.int32, (2 * HW, LANES), 1)).astype(BF16)

        def head_sums(prod):
            hi = prod.astype(BF16)
            lo = (prod - hi.astype(F32)).astype(BF16)
            sums = jnp.dot(jnp.concatenate([hi, lo], axis=1), ind, preferred_element_type=F32)
            return sums.T[:HEADS, :]

        da_ref[...] = head_sums(prod_a)
        db_ref[...] = head_sums(prod_b)

    def tok(width):
        return pl.BlockSpec((bt, width), lambda i: (i, 0))

    def full(shape):
        return pl.BlockSpec(shape, lambda i: (0,) * len(shape))

    stat = pl.BlockSpec((HEADS, bt), lambda i: (0, i))
    n_cls = ob_far.shape[0]
    by_class = pl.BlockSpec((n_cls, bt // n_cls, HW), lambda i: (0, i, 0))
    return _pcall(
        body, name="out_ln", grid=(seq // bt,),
        in_specs=[tok(HW), tok(HW), by_class, stat, stat, tok(HW), tok(HW), tok(D_MODEL), tok(D_MODEL),
                  full((D_MODEL, D_MODEL)), full((1, D_MODEL)), full((1, D_MODEL))],
        out_specs=[tok(D_MODEL), tok(HW), tok(HW), tok(HW), tok(HW), stat, stat, stat,
                   full((D_MODEL, D_MODEL)), full((8, D_MODEL)), by_class],
        out_shape=[jax.ShapeDtypeStruct((seq, D_MODEL), F32)] + [jax.ShapeDtypeStruct((seq, HW), BF16)] * 4
        + [jax.ShapeDtypeStruct((HEADS, seq), F32)] * 3
        + [jax.ShapeDtypeStruct((D_MODEL, D_MODEL), BF16), jax.ShapeDtypeStruct((8, D_MODEL), F32),
           jax.ShapeDtypeStruct(ob_far.shape, BF16)],
        scratch_shapes=[pltpu.VMEM((D_MODEL, D_MODEL), F32), pltpu.VMEM((HW // LANES, bt, LANES), F32)],
        compiler_params=_cparams(dimension_semantics=("arbitrary",)),
    )(oa, ob_near, ob_far, lse_near, lse_far, ga, gb, x, tgt, w_out, ln_g, ln_b)


def _bwd_mid(dq_m, dkn, dv, dkpe, dqb, dkb, dvb, far, dga, dgb, cq, ckv, qn, kvn, w_uq_r, w_ukv_r, qg, kvg, tabs, bt):
    n_cls = far[0].shape[0]
    seq = cq.shape[0]

    def body(dqm_ref, dkn_ref, dv_ref, dkpe_ref, dqb_ref, dkb_ref, dvb_ref, dqf_ref, dkf_ref, dvf_ref, dga_ref, dgb_ref,
             cq_ref, ckv_ref, qn_ref, kvn_ref, wuq_ref, wukv_ref, qg_ref, kvg_ref, tab_ref,
             dh_ref, guq3_ref, gukv3_ref, small_ref, seq_sc, guq_ref, gukv_ref):
        i = pl.program_id(0)

        @pl.when(i == 0)
        def _():
            guq_ref[...] = jnp.zeros(guq_ref.shape, F32)
            gukv_ref[...] = jnp.zeros(gukv_ref.shape, F32)
            small_ref[...] = jnp.zeros(small_ref.shape, F32)

        m_tabs = (tab_ref[0], tab_ref[1], tab_ref[2])
        d_tabs = (tab_ref[3], tab_ref[4], tab_ref[5])

        def rms_bwd(c, dn, gain):
            r = lax.rsqrt(jnp.mean(c * c, axis=1, keepdims=True) + RMS_EPS)
            u = dn * gain
            dc = r * u - c * (r * r * r) * jnp.mean(u * c, axis=1, keepdims=True)
            return dc, jnp.sum(dn * c * r, axis=0, keepdims=True)

        dqm = dqm_ref[0].T
        dq = jnp.concatenate(
            [dqm[:, :HW], _rope_wide(_rope_t, dqm[:, HW:], *m_tabs, MLA_ROPE // 2)], axis=1) * MLA_SCALE
        dq = dq.astype(BF16)
        dkv = jnp.concatenate([dkn_ref[...], dv_ref[...]], axis=1)
        guq_ref[...] += lax.dot_general(qn_ref[...], dq, TN, preferred_element_type=F32)
        dqn = lax.dot_general(dq, wuq_ref[...], NT, preferred_element_type=F32)
        gukv_ref[...] += lax.dot_general(kvn_ref[...], dkv, TN, preferred_element_type=F32)
        dkvn = lax.dot_general(dkv, wukv_ref[...], NT, preferred_element_type=F32)

        dh_ref[:, C_KR:C_GA] = _rope_t(dkpe_ref[...], *m_tabs, MLA_ROPE // 2).astype(BF16)
        dh_ref[:, C_GA:C_QB] = dga_ref[...]
        in_sequence = functools.partial(_in_sequence, lanes_sc=seq_sc)
        dqb = dqb_ref[0].T + in_sequence(dqf_ref)
        dh_ref[:, C_QB:C_KB] = (_rope_wide(_rope_t, dqb, *d_tabs, DIL_ROT // 2) * DIL_SCALE).astype(BF16)
        dkb = dkb_ref[...] + in_sequence(dkf_ref)
        dh_ref[:, C_KB:C_VB] = _rope_wide(_rope_t, dkb, *d_tabs, DIL_ROT // 2).astype(BF16)
        dh_ref[:, C_VB:C_GB] = (dvb_ref[...].astype(F32) + in_sequence(dvf_ref)).astype(BF16)
        dh_ref[:, C_GB:C_END] = dgb_ref[...]

        dcq, gq = rms_bwd(cq_ref[...], dqn, qg_ref[...])
        small_ref[0:1, :] += gq
        dckv, gkv = rms_bwd(ckv_ref[...], dkvn, kvg_ref[...])
        small_ref[1:2, :KV_RANK] += gkv
        dh_ref[:, C_CQ:C_CKV] = dcq.astype(BF16)
        dh_ref[:, C_CKV:C_KR] = dckv.astype(BF16)

        @pl.when(i == seq // bt - 1)
        def _():
            for h in range(HEADS):
                guq3_ref[h] = jnp.concatenate(
                    [guq_ref[:, MLA_NOPE * h:MLA_NOPE * (h + 1)],
                     guq_ref[:, HW + MLA_ROPE * h:HW + MLA_ROPE * (h + 1)]], axis=1).astype(BF16)
                gukv3_ref[h] = jnp.concatenate(
                    [gukv_ref[:, MLA_NOPE * h:MLA_NOPE * (h + 1)],
                     gukv_ref[:, HW + MLA_V * h:HW + MLA_V * (h + 1)]], axis=1).astype(BF16)

    def tok(width):
        return pl.BlockSpec((bt, width), lambda i: (i, 0))

    def tok_t(a):
        per = a.shape[2] // bt
        return pl.BlockSpec((1, a.shape[1], bt), lambda i: (i // per, 0, i % per))

    def full(shape):
        return pl.BlockSpec(shape, lambda i: (0,) * len(shape))

    by_class = pl.BlockSpec((n_cls, bt // n_cls, HW), lambda i: (0, i, 0))
    uq3 = (HEADS, Q_RANK, MLA_NOPE + MLA_ROPE)
    ukv3 = (HEADS, KV_RANK, MLA_NOPE + MLA_V)
    return _pcall(
        body, name="bwd_mid", grid=(seq // bt,),
        in_specs=[tok_t(dq_m), tok(HW), tok(HW), tok(LANES), tok_t(dqb), tok(HW), tok(HW), by_class, by_class, by_class,
                  tok(HW), tok(HW),
                  tok(Q_RANK), tok(KV_RANK), tok(Q_RANK), tok(KV_RANK),
                  full(w_uq_r.shape), full(w_ukv_r.shape), full((1, Q_RANK)), full((1, KV_RANK)),
                  pl.BlockSpec((6, bt, LANES), lambda i: (0, i, 0))],
        out_specs=[tok(C_END), full(uq3), full(ukv3), full((8, Q_RANK))],
        out_shape=[jax.ShapeDtypeStruct((seq, C_END), BF16), jax.ShapeDtypeStruct(uq3, BF16),
                   jax.ShapeDtypeStruct(ukv3, BF16), jax.ShapeDtypeStruct((8, Q_RANK), F32)],
        scratch_shapes=[pltpu.VMEM((HW // LANES, bt, LANES), F32), pltpu.VMEM(w_uq_r.shape, F32),
                        pltpu.VMEM(w_ukv_r.shape, F32)],
        compiler_params=_cparams(dimension_semantics=("arbitrary",)),
    )(dq_m, dkn, dv, dkpe, dqb, dkb, dvb, *far, dga, dgb, cq, ckv, qn, kvn, w_uq_r, w_ukv_r, qg, kvg, tabs)


def _grad_x(dz, dh, w_in_r, bt, ride=None):
    seq = dz.shape[0]
    n_steps = seq // bt

    def body(dz_ref, dh_ref, w_ref, gx_ref):
        gx_ref[...] = ALPHA * dz_ref[...] + lax.dot_general(
            dh_ref[...], w_ref[...], NT, preferred_element_type=F32)

    args = [dz, dh, w_in_r]
    in_specs = [pl.BlockSpec((bt, D_MODEL), lambda i: (i, 0)), pl.BlockSpec((bt, C_END), lambda i: (i, 0)),
                pl.BlockSpec(w_in_r.shape, lambda i: (0, 0))]
    out_specs = [pl.BlockSpec((bt, D_MODEL), lambda i: (i, 0))]
    out_shape = [jax.ShapeDtypeStruct((seq, D_MODEL), F32)]
    scratch = []
    body = _ride_along(body, ride, 0, len(args), len(out_shape), 0, n_steps)
    if ride is not None:
        args, in_specs = args + ride.args, in_specs + ride.in_specs
        out_specs, out_shape, scratch = out_specs + ride.out_specs, out_shape + ride.out_shape, ride.scratch
    return _pcall(
        body, name="grad_x", grid=(n_steps,),
        in_specs=in_specs, out_specs=out_specs, out_shape=out_shape, scratch_shapes=scratch,
        compiler_params=_cparams(dimension_semantics=("arbitrary",)),
    )(*args)


def _grad_w_in(x, dh, bt, cut, ride):
    seq = x.shape[0]
    n_tok = seq // bt
    shard = IN_WIDTH // N_DEV
    k_lo, k_hi = IN_SPLITS[0] + IN_SPLITS[1], IN_SPLITS[0] + IN_SPLITS[1] + MLA_ROPE

    def body(x_ref, dh_ref, rest_ref, acc, first_ref):
        i = pl.program_id(0)
        for part, (r_lo, r_hi, dst_ref) in enumerate([(0, cut, first_ref), (cut, D_MODEL, rest_ref)]):
            rows = slice(0, r_hi - r_lo)

            @pl.when(i == part * n_tok)
            def _():
                acc[rows, :] = jnp.zeros((r_hi - r_lo, C_END), F32)

            @pl.when(i // n_tok == part)
            def _():
                acc[rows, :] += lax.dot_general(
                    x_ref[:, r_lo:r_hi].astype(BF16), dh_ref[...], TN, preferred_element_type=F32)

            @pl.when(i == (part + 1) * n_tok - 1)
            def _():
                kr = acc[rows, C_KR:C_GA]
                kr = kr + pltpu.roll(kr, 96, 1) + pltpu.roll(kr, 64, 1) + pltpu.roll(kr, 32, 1)
                for d in range(N_DEV):
                    lo, hi = shard * d, shard * (d + 1)
                    pieces = []
                    if lo < k_lo:
                        pieces.append(acc[rows, lo:min(hi, k_lo)])
                    if lo < k_hi and hi > k_lo:
                        pieces.append(kr[:, max(lo, k_lo) - k_lo:min(hi, k_hi) - k_lo])
                    if hi > k_hi:
                        shift = C_GA - k_hi
                        pieces.append(acc[rows, max(lo, k_hi) + shift:hi + shift])
                    blk = pieces[0] if len(pieces) == 1 else jnp.concatenate(pieces, axis=1)
                    dst_ref[d] = blk.astype(BF16)

    args = [x, dh]
    in_specs = [pl.BlockSpec((bt, D_MODEL), lambda i: (i % n_tok, 0)),
                pl.BlockSpec((bt, C_END), lambda i: (i % n_tok, 0))]
    out_specs = [pl.BlockSpec((N_DEV, D_MODEL - cut, shard), lambda i: (0, 0, 0))]
    out_shape = [jax.ShapeDtypeStruct((N_DEV, D_MODEL - cut, shard), BF16)]
    scratch = [pltpu.VMEM((max(cut, D_MODEL - cut), C_END), F32)]
    assert ride.shared == 1
    body = _ride_along(body, ride, 0, len(args), len(out_shape), len(scratch), 2 * n_tok)
    return _pcall(
        body, name="grad_w_in", grid=(2 * n_tok,),
        in_specs=in_specs + ride.in_specs, out_specs=out_specs + ride.out_specs,
        out_shape=out_shape + ride.out_shape, scratch_shapes=scratch + ride.scratch,
        compiler_params=_cparams(dimension_semantics=("arbitrary",)),
    )(*args, *ride.args)


def _local_step(x, tgt, w_in_r, w_uq_r, w_ukv_r, w_out_rider, g_out_rider, reduce_rider, q_norm_g, kv_norm_g,
                ln_g, ln_b, bt=BLOCK_TOKENS, blk_m=BLOCK_MLA, blk_d=BLOCK_DIL):
    seq = x.shape[0]
    tabs = jnp.asarray(_rope_tables(seq))
    qg, kvg = q_norm_g.reshape(1, -1), kv_norm_g.reshape(1, -1)

    far_dil = DIL_CONFIGS[-1][1]
    cls = seq // far_dil
    (cq, ckv, qn, kvn, qcat, kn, kpe, v, ga, gb, qb, kb, vb, knt, kpet, vt, kbt, vbt, qb_c, kb_c, vb_c) = _fwd_proj(
        x, w_in_r, w_uq_r, w_ukv_r, qg, kvg, tabs, bt, far_dil)
    qb_c, kb_c, vb_c = (a.reshape(seq, HW) for a in (qb_c, kb_c, vb_c))

    nq_m, nq_d = seq // blk_m, seq // blk_d
    bias_m = _mla_bias_t(blk_m)
    oa, lse_a, w_out = _attn_fwd(
        "mla_fwd", qcat, kn, kpe, vt, bias_m, _steps(nq_m, nq_m, False, True), blk_m, ride=w_out_rider)

    bias_near = _dil_bias_t(blk_d, DIL_NEAR)
    ob_near, lse_near = _attn_fwd(
        "dil_fwd", qb, kb, None, vbt, bias_near, _steps(nq_d, -(-DIL_NEAR // blk_d), False, False), blk_d)
    each = np.arange(far_dil, dtype=np.int32)
    steps_far = [jnp.asarray(v) for v in (each, each, np.zeros_like(each), np.ones_like(each), np.ones_like(each))]
    bias_far = _dil_far_bias_t(cls)
    ob_far, lse_far = _attn_fwd(
        "dil_far_fwd", qb_c, kb_c, None, vb_c, bias_far, steps_far, cls, v_token_major=True)

    dz, doa, dob, dga, dgb, dst_a, dst_b, lse_b, g_out, small1, dob_c = _out_ln(
        oa, ob_near, ob_far.reshape(far_dil, cls, HW), lse_near, _lanes_from_classes(lse_far, far_dil), ga, gb, x, tgt,
        w_out.reshape(D_MODEL, D_MODEL), ln_g.reshape(1, -1), ln_b.reshape(1, -1), bt)

    dq_m, dkn, dkpe, dv, g_out_recv = _attn_bwd(
        "mla_bwd", qcat, kn, kpe, v, knt, kpet, bias_m, doa, lse_a, dst_a, _steps(nq_m, nq_m, True, True), blk_m,
        ride=g_out_rider(g_out.reshape(N_DEV, D_MODEL // N_DEV, D_MODEL)))
    dqb, dkb_near, dvb_near = _attn_bwd(
        "dil_bwd", qb, kb, None, vb, kbt, None, bias_near, dob, lse_b, dst_b,
        _steps(nq_d, -(-DIL_NEAR // blk_d), True, False), blk_d)
    dqb_far, dkb_far, dvb_far = _attn_bwd(
        "dil_far_bwd", qb_c, kb_c, None, vb_c, None, None, bias_far, dob_c.reshape(seq, HW),
        _lanes_to_classes(lse_b, far_dil), _lanes_to_classes(dst_b, far_dil), steps_far, cls, single_visit=True)
    far = [a.reshape(far_dil, cls, HW) for a in (dqb_far, dkb_far, dvb_far)]

    dh, g_uq, g_ukv, small2 = _bwd_mid(
        dq_m, dkn, dv, dkpe, dqb, dkb_near, dvb_near, far, dga, dgb, cq, ckv, qn, kvn, w_uq_r, w_ukv_r, qg, kvg, tabs, bt)
    bt_w = min(seq, 2 * bt)
    n_tok = seq // bt_w
    first = jax.ShapeDtypeStruct((N_DEV, GRAD_W_IN_CUT, IN_WIDTH // N_DEV), BF16)
    g_in_rest, *reduced_first = _grad_w_in(
        x, dh, bt_w, GRAD_W_IN_CUT,
        reduce_rider([first, g_uq, g_ukv], g_out_recv, (small1, small2), (n_tok, n_tok + 1, 2 * n_tok - 1)))
    grad_x, g_in_rest = _grad_x(dz, dh, w_in_r, bt, ride=reduce_rider([g_in_rest], None, None, (0, 1, seq // bt - 2)))
    return grad_x, [g_in_rest] + reduced_first


MESH_ID = pl.DeviceIdType.MESH
SHARD_SHAPES = ((D_MODEL, IN_WIDTH // N_DEV), (Q_RANK, 768 // N_DEV), (KV_RANK, 1024 // N_DEV), (D_MODEL // N_DEV, D_MODEL))
ADAM_ROWS = (32, 128, 128, 16)


def _me():
    x, y, c = lax.axis_index("x"), lax.axis_index("y"), lax.axis_index("c")
    return x, y, c, 4 * x + 2 * y + c


def _peer(k):
    x, y, c, _ = _me()
    px = 1 - x if (k >> 2) & 1 else x
    py = 1 - y if (k >> 1) & 1 else y
    pc = 1 - c if k & 1 else c
    return (px, py, pc), 4 * px + 2 * py + pc


def _all_gather_weights(shards):
    n = len(shards)
    shard = IN_WIDTH // N_DEV
    k_lo = IN_SPLITS[0] + IN_SPLITS[1]
    k_hi = k_lo + MLA_ROPE

    def body(*refs):
        ins = refs[:n]
        win_ref, wuq_ref, wukv_ref = refs[n:2 * n]
        bufs = refs[2 * n:3 * n]
        send_sems, recv_sems = refs[3 * n:]
        x, y, c, me = _me()
        here, sibling = (x, y, c), (x, y, 1 - c)
        along_x, along_y, across = (1 - x, y), (x, 1 - y), (1 - x, 1 - y)
        for t in range(n):
            bufs[t][me] = ins[t][...].astype(BF16)

        def copy(t, k, chip, pc, to, half=None):
            blk = bufs[t].at[4 * chip[0] + 2 * chip[1] + pc]
            if half is not None:
                rows = SHARD_SHAPES[t][0] // 2
                blk = blk.at[pl.ds(half * rows, rows), :]
            return pltpu.make_async_remote_copy(
                src_ref=blk, dst_ref=blk, send_sem=send_sems.at[t, k], recv_sem=recv_sems.at[t, k],
                device_id=to, device_id_type=MESH_ID)

        sends = []
        for t in range(n):
            sends += [copy(t, 0, (x, y), c, sibling), copy(t, 1, (x, y), c, (*along_x, c)),
                      copy(t, 2, (x, y), c, (*along_y, c))]
        for cp in sends:
            cp.start()
        for t in range(n):
            copy(t, 1, along_x, c, here).wait_recv()
            sends += [copy(t, 3, along_x, c, (*along_y, c), half=0), copy(t, 5, along_x, c, sibling)]
            sends[-2].start()
            sends[-1].start()
        for t in range(n):
            copy(t, 2, along_y, c, here).wait_recv()
            sends += [copy(t, 4, along_y, c, (*along_x, c), half=1), copy(t, 6, along_y, c, sibling)]
            sends[-2].start()
            sends[-1].start()
        for t in range(n):
            copy(t, 3, across, c, here, half=0).wait_recv()
            copy(t, 4, across, c, here, half=1).wait_recv()
            sends.append(copy(t, 7, across, c, sibling))
            sends[-1].start()
        for t in range(n):
            copy(t, 0, (x, y), 1 - c, here).wait_recv()
            for k, chip in ((5, along_x), (6, along_y), (7, across)):
                copy(t, k, chip, 1 - c, here).wait_recv()
        for cp in sends:
            cp.wait_send()

        a_in, a_uq, a_ukv = bufs
        for d in range(N_DEV):
            lo, hi = shard * d, shard * (d + 1)
            if lo < k_lo:
                win_ref[:, lo:min(hi, k_lo)] = a_in[d, :, 0:min(hi, k_lo) - lo]
            if lo < k_hi and hi > k_lo:
                kr = a_in[d, :, k_lo - lo:k_hi - lo]
                for rep in range(4):
                    win_ref[:, C_KR + MLA_ROPE * rep:C_KR + MLA_ROPE * (rep + 1)] = kr
            if hi > k_hi:
                src = max(lo, k_hi)
                win_ref[:, src + C_GA - k_hi:hi + C_GA - k_hi] = a_in[d, :, src - lo:hi - lo]
        for h in range(HEADS):
            wuq_ref[:, MLA_NOPE * h:MLA_NOPE * (h + 1)] = a_uq[h, :, :MLA_NOPE]
            wuq_ref[:, HW + MLA_ROPE * h:HW + MLA_ROPE * (h + 1)] = a_uq[h, :, MLA_NOPE:]
            wukv_ref[:, MLA_NOPE * h:MLA_NOPE * (h + 1)] = a_ukv[h, :, :MLA_NOPE]
            wukv_ref[:, HW + MLA_V * h:HW + MLA_V * (h + 1)] = a_ukv[h, :, MLA_NOPE:]

    vmem = pl.BlockSpec(memory_space=pltpu.VMEM)
    return _pcall(
        body, name="gather_weights",
        in_specs=[vmem] * n, out_specs=[vmem] * n,
        out_shape=[jax.ShapeDtypeStruct((D_MODEL, C_END), BF16), jax.ShapeDtypeStruct((Q_RANK, QW), BF16),
                   jax.ShapeDtypeStruct((KV_RANK, 2 * HW), BF16)],
        scratch_shapes=[pltpu.VMEM((N_DEV,) + s, BF16) for s in SHARD_SHAPES[:n]]
        + [pltpu.SemaphoreType.DMA((n, 8)), pltpu.SemaphoreType.DMA((n, 8))],
        compiler_params=_cparams(),
    )(*shards)


def _gather_w_out_rider(w_out):
    def copies(full_ref, stage, send_sems, recv_sems):
        me = _me()[3]
        out = []
        for k in range(1, N_DEV):
            peer, pidx = _peer(k)
            send = pltpu.make_async_remote_copy(
                src_ref=stage, dst_ref=full_ref.at[me], send_sem=send_sems.at[k - 1], recv_sem=recv_sems.at[k - 1],
                device_id=peer, device_id_type=MESH_ID)
            recv = pltpu.make_async_remote_copy(
                src_ref=stage, dst_ref=full_ref.at[pidx], send_sem=send_sems.at[k - 1], recv_sem=recv_sems.at[k - 1],
                device_id=peer, device_id_type=MESH_ID)
            out.append((send, recv))
        return out

    def start(ins, outs, scr):
        stage, send_sems, recv_sems, own_sem = scr
        stage[...] = ins[0][...].astype(BF16)
        pltpu.make_async_copy(stage, outs[0].at[_me()[3]], own_sem).start()
        for send, _ in copies(outs[0], stage, send_sems, recv_sems):
            send.start()

    def finish(ins, outs, scr):
        stage, send_sems, recv_sems, own_sem = scr
        pltpu.make_async_copy(stage, outs[0].at[_me()[3]], own_sem).wait()
        pairs = copies(outs[0], stage, send_sems, recv_sems)
        for _, recv in pairs:
            recv.wait_recv()
        for send, _ in pairs:
            send.wait_send()

    shape = SHARD_SHAPES[3]
    return Rider(
        args=[w_out], in_specs=[pl.BlockSpec(shape, lambda t, *_: (0, 0))],
        out_shape=[jax.ShapeDtypeStruct((N_DEV,) + shape, BF16)], out_specs=[pl.BlockSpec(memory_space=pl.ANY)],
        scratch=[pltpu.VMEM(shape, BF16), pltpu.SemaphoreType.DMA((N_DEV - 1,)), pltpu.SemaphoreType.DMA((N_DEV - 1,)),
                 pltpu.SemaphoreType.DMA],
        start=start, finish=finish)


def _scatter_g_out_rider(blocks):
    def copies(src_ref, dst_ref, send_sems, recv_sems):
        out = []
        for k in range(1, N_DEV):
            peer, pidx = _peer(k)
            out.append(pltpu.make_async_remote_copy(
                src_ref=src_ref.at[pidx], dst_ref=dst_ref.at[k], send_sem=send_sems.at[k - 1],
                recv_sem=recv_sems.at[k - 1], device_id=peer, device_id_type=MESH_ID))
        return out

    def start(ins, outs, scr):
        send_sems, recv_sems, own_sem = scr
        pltpu.make_async_copy(ins[0].at[_me()[3]], outs[0].at[0], own_sem).start()
        for cp in copies(ins[0], outs[0], send_sems, recv_sems):
            cp.start()

    def finish(ins, outs, scr):
        send_sems, recv_sems, own_sem = scr
        pltpu.make_async_copy(ins[0].at[_me()[3]], outs[0].at[0], own_sem).wait()
        for cp in copies(ins[0], outs[0], send_sems, recv_sems):
            cp.wait()

    hbm = pl.BlockSpec(memory_space=pl.ANY)
    return Rider(
        args=[blocks], in_specs=[hbm], out_shape=[jax.ShapeDtypeStruct(blocks.shape, blocks.dtype)], out_specs=[hbm],
        scratch=[pltpu.SemaphoreType.DMA((N_DEV - 1,)), pltpu.SemaphoreType.DMA((N_DEV - 1,)), pltpu.SemaphoreType.DMA],
        start=start, finish=finish)


def _adamw(w, g, m, v):
    m = ADAM_B1 * m + (1.0 - ADAM_B1) * g
    v = ADAM_B2 * v + (1.0 - ADAM_B2) * jnp.square(g)
    m_hat = m / (1.0 - ADAM_B1 ** ADAM_STEP)
    v_hat = v / (1.0 - ADAM_B2 ** ADAM_STEP)
    delta = -ADAM_LR * (m_hat / (jnp.sqrt(v_hat) + ADAM_EPS) + ADAM_WD * w)
    return delta, m, v


def _chunk_rows(count, cols):
    rows = (16 * 8 * LANES) // (-(-cols // LANES) * LANES)
    while count % rows:
        rows //= 2
    return rows


SMALL_ROWS = (0, 1, 3, 4)
LOSS_ROW = 2


def _reduce_grads_rider(grads3, arrived, small_parts, steps):
    n = len(grads3)
    shapes = [tuple(g.shape[1:]) for g in grads3]
    with_small = arrived is not None
    start_step, first_round, second_round = steps
    n_held = sum(isinstance(g, jax.ShapeDtypeStruct) for g in grads3)
    held, handed = grads3[:n_held], grads3[n_held:]
    assert not any(isinstance(g, jax.ShapeDtypeStruct) for g in handed)

    class Refs:
        def __init__(self, ins, outs, scr):
            self.g3, self.gsum = list(scr[0:n_held]) + list(ins[0:n - n_held]), outs[0:n]
            ins, scr = ins[n - n_held:], scr[n_held:]
            self.own, self.sib, self.part = scr[0:n], scr[n:2 * n], scr[2 * n:3 * n]
            self.in_a, self.out_b, self.in_b = scr[3 * n:4 * n], scr[4 * n:5 * n], scr[5 * n:6 * n]
            (self.loc_sems, self.d2d_send, self.d2d_recv, self.a_send, self.a_recv, self.b_send,
             self.b_recv) = scr[6 * n:6 * n + 7]
            if with_small:
                self.arr, self.sp_wide, self.sp_q = ins
                self.gsum_out, self.ssum = outs[n:n + 2]
                self.rsmall, self.sm_send, self.sm_recv = scr[6 * n + 7:]
            self.x, self.y, self.c, self.me = _me()
            self.along_x, self.along_y = (1 - self.x, self.y), (self.x, 1 - self.y)
            self.across = (1 - self.x, 1 - self.y)

        def small(self):
            if not with_small:
                return []
            return [pltpu.make_async_remote_copy(
                src_ref=self.rsmall.at[0], dst_ref=self.rsmall.at[k], send_sem=self.sm_send.at[k - 1],
                recv_sem=self.sm_recv.at[k - 1], device_id=_peer(k)[0], device_id_type=MESH_ID)
                for k in range(1, N_DEV)]

        def level1(self):
            local, to_sib = [], []
            for t in range(n):
                for q in range(4):
                    local.append(pltpu.make_async_copy(
                        self.g3[t].at[2 * q + self.c], self.own[t].at[q], self.loc_sems.at[t, q]))
                    to_sib.append(pltpu.make_async_remote_copy(
                        src_ref=self.g3[t].at[2 * q + 1 - self.c], dst_ref=self.sib[t].at[q],
                        send_sem=self.d2d_send.at[t, q], recv_sem=self.d2d_recv.at[t, q],
                        device_id=(self.x, self.y, 1 - self.c), device_id_type=MESH_ID))
            return local, to_sib

        def round_a(self):
            out = []
            for t in range(n):
                half = shapes[t][0] // 2
                for k, (to, chip, h) in enumerate([(self.along_x, self.along_x, 0), (self.along_x, self.across, 0),
                                                   (self.along_y, self.along_y, 1), (self.along_y, self.across, 1)]):
                    out.append(pltpu.make_async_remote_copy(
                        src_ref=self.part[t].at[2 * chip[0] + chip[1], pl.ds(h * half, half), :],
                        dst_ref=self.in_a[t].at[k], send_sem=self.a_send.at[t, k], recv_sem=self.a_recv.at[t, k],
                        device_id=(*to, self.c), device_id_type=MESH_ID))
            return out

        def round_b(self):
            return [pltpu.make_async_remote_copy(
                src_ref=self.out_b[t].at[k], dst_ref=self.in_b[t].at[k], send_sem=self.b_send.at[t, k],
                recv_sem=self.b_recv.at[t, k], device_id=(*to, self.c), device_id_type=MESH_ID)
                for t in range(n) for k, to in enumerate([self.along_y, self.along_x])]

    def chunks(rows, count, fn):
        def step(i, carry):
            fn(pl.multiple_of(i * rows, rows))
            return carry

        lax.fori_loop(0, count // rows, step, 0)

    def start(*refs):
        r = Refs(*refs)
        if with_small:
            r.rsmall[0] = r.sp_wide[...]
            for k, row in enumerate(SMALL_ROWS[2:]):
                r.rsmall[0, row:row + 1, 0:Q_RANK] = r.sp_q[k:k + 1, :]
        local, to_sib = r.level1()
        for cp in r.small() + local + to_sib:
            cp.start()

    def begin_rounds(*refs):
        r = Refs(*refs)
        local, to_sib = r.level1()
        for cp in local:
            cp.wait()
        for cp in to_sib:
            cp.wait_recv()
        my_chip = 2 * r.x + r.y
        for t in range(n):
            rows = _chunk_rows(*shapes[t])

            def pair_sums(at, t=t, rows=rows):
                sl = pl.ds(at, rows)
                for q in range(4):
                    r.part[t][q, sl, :] = (r.own[t][q, sl, :].astype(F32) + r.sib[t][q, sl, :].astype(F32)).astype(BF16)
                r.gsum[t][sl, :] = r.own[t][my_chip, sl, :].astype(F32) + r.sib[t][my_chip, sl, :].astype(F32)

            chunks(rows, shapes[t][0], pair_sums)
        for cp in r.round_a():
            cp.start()

    def pass_on(*refs):
        r = Refs(*refs)
        for cp in r.round_a():
            cp.wait_recv()
        q_x, q_y = 2 * r.along_x[0] + r.along_x[1], 2 * r.along_y[0] + r.along_y[1]
        for t in range(n):
            half = shapes[t][0] // 2
            rows = _chunk_rows(half, shapes[t][1])

            def add(at, t=t, rows=rows, half=half):
                lo, hi = pl.ds(at, rows), pl.ds(half + at, rows)
                r.gsum[t][lo, :] = r.gsum[t][lo, :] + r.in_a[t][0, lo, :].astype(F32)
                r.out_b[t][0, lo, :] = (r.part[t][q_y, lo, :].astype(F32) + r.in_a[t][1, lo, :].astype(F32)).astype(BF16)
                r.gsum[t][hi, :] = r.gsum[t][hi, :] + r.in_a[t][2, lo, :].astype(F32)
                r.out_b[t][1, lo, :] = (r.part[t][q_x, hi, :].astype(F32) + r.in_a[t][3, lo, :].astype(F32)).astype(BF16)

            chunks(rows, half, add)
        for cp in r.round_b():
            cp.start()

    def finish(*refs):
        r = Refs(*refs)
        if with_small:
            rows_out = _chunk_rows(*SHARD_SHAPES[3])

            def add_arrived(at):
                sl = pl.ds(at, rows_out)
                g = r.arr[0, sl, :].astype(F32)
                for k in range(1, N_DEV):
                    g = g + r.arr[k, sl, :].astype(F32)
                r.gsum_out[sl, :] = g

            chunks(rows_out, SHARD_SHAPES[3][0], add_arrived)
        passed = r.round_b()
        for cp in passed:
            cp.wait_recv()
        for t in range(n):
            half = shapes[t][0] // 2
            rows = _chunk_rows(half, shapes[t][1])

            def add(at, t=t, rows=rows, half=half):
                lo, hi = pl.ds(at, rows), pl.ds(half + at, rows)
                r.gsum[t][lo, :] = r.gsum[t][lo, :] + r.in_b[t][0, lo, :].astype(F32)
                r.gsum[t][hi, :] = r.gsum[t][hi, :] + r.in_b[t][1, lo, :].astype(F32)

            chunks(rows, half, add)
        small = r.small()
        for cp in small:
            cp.wait_recv()
        if with_small:
            tot = r.rsmall[r.me]
            for d in range(1, N_DEV):
                tot = tot + r.rsmall[jnp.bitwise_xor(r.me, d)]
            r.ssum[...] = tot
        for cp in small + r.level1()[1] + r.round_a() + passed:
            cp.wait_send()

    hbm = pl.BlockSpec(memory_space=pl.ANY)
    dma = pltpu.SemaphoreType.DMA

    def whole(shape):
        return pl.BlockSpec(shape, lambda i: (0,) * len(shape))

    def halves(slots):
        return [pltpu.VMEM((slots, s[0] // 2, s[1]), BF16) for s in shapes]

    small_args = [arrived, *small_parts] if with_small else []
    out_shapes = shapes + ([SHARD_SHAPES[3], (8, D_MODEL)] if with_small else [])
    return Rider(
        args=list(handed) + small_args,
        in_specs=[hbm] * len(handed) + [whole(a.shape) for a in small_args],
        out_shape=[jax.ShapeDtypeStruct(s, F32) for s in out_shapes], out_specs=[whole(s) for s in out_shapes],
        scratch=[pltpu.VMEM(g.shape, g.dtype) for g in held]
        + [pltpu.VMEM((4,) + s, BF16) for _ in range(3) for s in shapes]
        + halves(4) + halves(2) + halves(2)
        + [dma((n, 4)), dma((n, 4)), dma((n, 4)), dma((n, 4)), dma((n, 4)), dma((n, 2)), dma((n, 2))]
        + ([pltpu.VMEM((N_DEV, 8, D_MODEL), F32), dma((N_DEV - 1,)), dma((N_DEV - 1,))] if with_small else []),
        start=start, finish=finish, stages=((first_round, begin_rounds), (second_round, pass_on)),
        start_step=start_step, shared=n_held)


def _adamw_update(grads, small_grad, wmv, small_wmv):
    n_small = len(small_wmv)
    flat_grads = [g for pieces in grads for g in pieces]
    n_g = len(flat_grads)

    def body(*refs):
        g_refs, sg_ref, refs = iter(refs[0:n_g]), refs[n_g], refs[n_g + 1:]
        wmv_refs = [refs[3 * t:3 * t + 3] for t in range(4)]
        swmv_refs = [refs[12 + 3 * t:15 + 3 * t] for t in range(n_small)]
        outs = refs[12 + 3 * n_small:]
        out_refs = [outs[4 * t:4 * t + 4] for t in range(4)]
        sout_refs = [outs[16 + 4 * t:20 + 4 * t] for t in range(n_small)]
        loss_ref = outs[16 + 4 * n_small]
        for t, (w_ref, m_ref, v_ref) in enumerate(swmv_refs):
            g = sg_ref[SMALL_ROWS[t]:SMALL_ROWS[t] + 1, :w_ref.shape[1]]
            delta, m, v = _adamw(w_ref[...], g, m_ref[...], v_ref[...])
            sout_refs[t][0][...], sout_refs[t][1][...], sout_refs[t][2][...], sout_refs[t][3][...] = g, delta, m, v
        loss_ref[...] = (0.5 / D_MODEL) * jnp.sum(sg_ref[LOSS_ROW:LOSS_ROW + 1, :], axis=1, keepdims=True)
        for t in range(4):
            rows = ADAM_ROWS[t]
            w_ref, m_ref, v_ref = wmv_refs[t]
            g_out, d_out, m_out, v_out = out_refs[t]

            def step(i, carry, g_ref, first, rows=rows, w_ref=w_ref, m_ref=m_ref, v_ref=v_ref,
                     g_out=g_out, d_out=d_out, m_out=m_out, v_out=v_out):
                at = pl.multiple_of(i * rows, rows)
                r = pl.ds(pl.multiple_of(first + at, rows), rows)
                g = g_ref[pl.ds(at, rows), :]
                delta, m, v = _adamw(w_ref[r, :], g, m_ref[r, :], v_ref[r, :])
                g_out[r, :], d_out[r, :], m_out[r, :], v_out[r, :] = g, delta, m, v
                return carry

            first = 0
            for piece in grads[t]:
                lax.fori_loop(0, piece.shape[0] // rows, functools.partial(step, g_ref=next(g_refs), first=first), 0)
                first += piece.shape[0]

    vmem = pl.BlockSpec(memory_space=pltpu.VMEM)
    flat_wmv = [a for trio in wmv for a in trio]
    flat_small = [a for trio in small_wmv for a in trio]
    out_shape = ([jax.ShapeDtypeStruct(s, F32) for s in SHARD_SHAPES for _ in range(4)]
                 + [jax.ShapeDtypeStruct(trio[0].shape, F32) for trio in small_wmv for _ in range(4)]
                 + [jax.ShapeDtypeStruct((1, 1), F32)])
    return _pcall(
        body, name="adamw",
        in_specs=[vmem] * (n_g + 1 + len(flat_wmv) + len(flat_small)), out_specs=[vmem] * len(out_shape),
        out_shape=out_shape,
        compiler_params=_cparams(),
    )(*flat_grads, small_grad, *flat_wmv, *flat_small)


def kernel(x, w_in, q_norm_g, kv_norm_g, w_uq, w_ukv, w_out, ln_g, ln_b, loss_target, m_w_in, m_q_norm_g, m_kv_norm_g, m_w_uq, m_w_ukv, m_w_out, m_ln_g, m_ln_b, v_w_in, v_q_norm_g, v_kv_norm_g, v_w_uq, v_w_ukv, v_w_out, v_ln_g, v_ln_b):
    w_in_r, w_uq_r, w_ukv_r = _all_gather_weights([w_in, w_uq, w_ukv])
    grad_x, (g_in_rest, g_in_first, g_uq, g_ukv, g_out, g_small) = _local_step(
        x[0], loss_target[0], w_in_r, w_uq_r, w_ukv_r, _gather_w_out_rider(w_out), _scatter_g_out_rider,
        _reduce_grads_rider, q_norm_g, kv_norm_g, ln_g, ln_b)
    row = lambda a: a.reshape(1, -1)
    small_wmv = [(row(ln_g), row(m_ln_g), row(v_ln_g)), (row(ln_b), row(m_ln_b), row(v_ln_b)),
                 (row(q_norm_g), row(m_q_norm_g), row(v_q_norm_g)), (row(kv_norm_g), row(m_kv_norm_g), row(v_kv_norm_g))]
    wmv = [(w_in, m_w_in, v_w_in), (w_uq, m_w_uq, v_w_uq), (w_ukv, m_w_ukv, v_w_ukv), (w_out, m_w_out, v_w_out)]
    res = _adamw_update([[g_in_first, g_in_rest], [g_uq], [g_ukv], [g_out]], g_small, wmv, small_wmv)
    big = [res[4 * t:4 * t + 4] for t in range(4)]
    small = [[a.reshape(-1) for a in res[16 + 4 * t:20 + 4 * t]] for t in range(4)]
    loss = res[32].reshape(())

    def group(kind):
        return (big[0][kind], small[2][kind], small[3][kind], big[1][kind], big[2][kind], big[3][kind],
                small[0][kind], small[1][kind])

    return (loss, grad_x[None], *group(0), *group(1), *group(2), *group(3))
```
